```python
import jax, jax.numpy as jnp
from jax import lax
import numpy as np

D_MODEL = 2048
BATCH = 8
SEQ = 2048
DEPTH = 1

GM_WIDTH = 2048
CHUNK = 128
GM_GROUPS = 16
GM_GROUP_DIM = GM_WIDTH // GM_GROUPS
MLA_HEADS = 16
Q_LORA = 512
KV_LORA = 256
QK_NOPE = 128
QK_ROPE = 64
V_HEAD = 128
ROPE_THETA = 10000.0
Q_BLOCK = 128
D_FF = 5632
CONV_W = 3
EPS = 1e-6
N_MOD = 6
IN_SIZES = (GM_WIDTH, GM_WIDTH, Q_LORA, KV_LORA, QK_ROPE, D_MODEL, D_MODEL)
IN_COLS = sum(IN_SIZES)
IN_SPLITS = tuple(int(s) for s in np.cumsum(IN_SIZES)[:-1])

kernel_name = "hybrid_gmlp_mla_convffn_block"


def rmsnorm(x, g):
    xf = x.astype(jnp.float32)
    y = xf * lax.rsqrt(jnp.mean(xf * xf, axis=-1, keepdims=True) + EPS)
    return (y * g.astype(jnp.float32)).astype(x.dtype)


def layernorm(x, g, b):
    xf = x.astype(jnp.float32)
    mu = jnp.mean(xf, axis=-1, keepdims=True)
    var = jnp.mean(jnp.square(xf - mu), axis=-1, keepdims=True)
    y = (xf - mu) * lax.rsqrt(var + EPS)
    return (y * g.astype(jnp.float32) + b.astype(jnp.float32)).astype(x.dtype)


def rope_tables(positions, dtype):
    inv = ROPE_THETA ** (-jnp.arange(0, QK_ROPE, 2, dtype=jnp.float32) / QK_ROPE)
    ang = positions.astype(jnp.float32)[..., None] * inv
    return jnp.cos(ang).astype(dtype), jnp.sin(ang).astype(dtype)


def apply_rope(x, cos, sin):
    x1, x2 = jnp.split(x, 2, axis=-1)
    return jnp.concatenate([x1 * cos - x2 * sin, x2 * cos + x1 * sin], axis=-1)


def gmlp_spatial_gating(u, v, ln_g, ln_b, w_s, b_s):
    B, S, _ = v.shape
    v = layernorm(v, ln_g, ln_b)
    v = v.reshape(B, S // CHUNK, CHUNK, GM_GROUPS, GM_GROUP_DIM)
    mask = jnp.tril(jnp.ones((CHUNK, CHUNK), dtype=w_s.dtype))
    mixed = jnp.einsum('bnpgd,gqp->bnqgd', v, w_s * mask) + b_s.T[None, None, :, :, None]
    return u * mixed.reshape(B, S, GM_WIDTH)


def mla_attention(q_lat, kv_lat, k_pe, positions, q_norm_g, w_uq, kv_norm_g, w_ukv):
    B, S, _ = q_lat.shape
    q = (rmsnorm(q_lat, q_norm_g) @ w_uq).reshape(B, S, MLA_HEADS, QK_NOPE + QK_ROPE)
    kv = (rmsnorm(kv_lat, kv_norm_g) @ w_ukv).reshape(B, S, MLA_HEADS, QK_NOPE + V_HEAD)
    q_nope, q_pe = q[..., :QK_NOPE], q[..., QK_NOPE:]
    k_nope, v = kv[..., :QK_NOPE], kv[..., QK_NOPE:]
    cos, sin = rope_tables(positions, q.dtype)
    q_pe = apply_rope(q_pe, cos[:, :, None], sin[:, :, None])
    k_pe = apply_rope(k_pe, cos, sin)
    q = jnp.concatenate([q_nope, q_pe], axis=-1)
    k = jnp.concatenate([k_nope, jnp.broadcast_to(k_pe[:, :, None], (B, S, MLA_HEADS, QK_ROPE))], axis=-1)
    scale = (QK_NOPE + QK_ROPE) ** -0.5
    n_blocks = S // Q_BLOCK
    q_blocks = q.reshape(B, n_blocks, Q_BLOCK, MLA_HEADS, QK_NOPE + QK_ROPE).transpose(1, 0, 2, 3, 4)
    key_pos = jnp.arange(S)

    def attend(args):
        qb, i = args
        s = jnp.einsum('bqhd,bkhd->bhqk', qb, k).astype(jnp.float32) * scale
        q_pos = i * Q_BLOCK + jnp.arange(Q_BLOCK)
        causal = key_pos[None, :] <= q_pos[:, None]
        s = jnp.where(causal[None, None], s, -1e30)
        p = jax.nn.softmax(s, axis=-1).astype(v.dtype)
        return jnp.einsum('bhqk,bkhd->bqhd', p, v)

    o = lax.map(attend, (q_blocks, jnp.arange(n_blocks)))
    return o.transpose(1, 0, 2, 3, 4).reshape(B, S, MLA_HEADS * V_HEAD)


def causal_dwconv(h, w, b):
    S = h.shape[1]
    hp = jnp.pad(h, ((0, 0), (CONV_W - 1, 0), (0, 0)))
    return sum(w[k] * hp[:, k:k + S] for k in range(CONV_W)) + b


def _fwd_setup_inputs(seed: int = 0) -> dict:
    key = jax.random.key(seed)
    ks = jax.random.split(key, 32)
    f32 = jnp.float32
    nrm = lambda k, shape, s: jax.random.normal(k, shape, f32) * s
    gain = lambda k, n: 1.0 + 0.02 * jax.random.normal(k, (n,), f32)
    offset = jax.random.randint(ks[2], (BATCH, 1), 0, 4096, dtype=jnp.int32)
    positions = (jnp.arange(SEQ, dtype=jnp.int32)[None, :] + offset).astype(jnp.int32)
    return {
        "x": nrm(ks[0], (BATCH, SEQ, D_MODEL), 1.0),
        "c": nrm(ks[1], (BATCH, D_MODEL), 1.0),
        "positions": positions,
        "w_ada": nrm(ks[3], (D_MODEL, N_MOD * D_MODEL), 0.5 * D_MODEL ** -0.5),
        "b_ada": nrm(ks[4], (N_MOD * D_MODEL,), 0.01),
        "pre_norm1_g": gain(ks[5], D_MODEL),
        "w_in": nrm(ks[6], (D_MODEL, IN_COLS), D_MODEL ** -0.5),
        "gm_ln_g": gain(ks[7], GM_WIDTH),
        "gm_ln_b": nrm(ks[8], (GM_WIDTH,), 0.01),
        "gm_w_s": nrm(ks[9], (GM_GROUPS, CHUNK, CHUNK), CHUNK ** -0.5),
        "gm_b_s": 1.0 + 0.02 * jax.random.normal(ks[10], (GM_GROUPS, CHUNK), f32),
        "w_branch_a": nrm(ks[11], (GM_WIDTH, D_MODEL), GM_WIDTH ** -0.5),
        "q_norm_g": gain(ks[12], Q_LORA),
        "w_uq": nrm(ks[13], (Q_LORA, MLA_HEADS * (QK_NOPE + QK_ROPE)), Q_LORA ** -0.5),
        "kv_norm_g": gain(ks[14], KV_LORA),
        "w_ukv": nrm(ks[15], (KV_LORA, MLA_HEADS * (QK_NOPE + V_HEAD)), KV_LORA ** -0.5),
        "w_branch_b": nrm(ks[16], (MLA_HEADS * V_HEAD, D_MODEL), (MLA_HEADS * V_HEAD) ** -0.5),
        "w_out": nrm(ks[17], (D_MODEL, D_MODEL), D_MODEL ** -0.5),
        "post_norm1_g": gain(ks[18], D_MODEL),
        "pre_norm2_g": gain(ks[19], D_MODEL),
        "w_up": nrm(ks[20], (D_MODEL, 2 * D_FF), D_MODEL ** -0.5),
        "conv_w": nrm(ks[21], (CONV_W, 2 * D_FF), CONV_W ** -0.5),
        "conv_b": nrm(ks[22], (2 * D_FF,), 0.01),
        "w_down": nrm(ks[23], (D_FF, D_MODEL), D_FF ** -0.5),
        "post_norm2_g": gain(ks[24], D_MODEL),
    }


def _fwd_reference(x, c, positions, w_ada, b_ada, pre_norm1_g, w_in, gm_ln_g, gm_ln_b, gm_w_s, gm_b_s,
              w_branch_a, q_norm_g, w_uq, kv_norm_g, w_ukv, w_branch_b, w_out, post_norm1_g,
              pre_norm2_g, w_up, conv_w, conv_b, w_down, post_norm2_g):
    B = x.shape[0]
    mod = (jax.nn.silu(c) @ w_ada + b_ada).reshape(B, N_MOD, D_MODEL)
    shift1, scale1, gate1 = mod[:, None, 0], mod[:, None, 1], mod[:, None, 2]
    shift2, scale2, gate2 = mod[:, None, 3], mod[:, None, 4], mod[:, None, 5]

    for _ in range(DEPTH):
        h = rmsnorm(x, pre_norm1_g) * (1.0 + scale1) + shift1
        z = h @ w_in
        u, v, q_lat, kv_lat, k_pe, g_a, g_b = jnp.split(z, IN_SPLITS, axis=-1)
        y_a = gmlp_spatial_gating(jax.nn.gelu(u), jax.nn.gelu(v), gm_ln_g, gm_ln_b, gm_w_s, gm_b_s) @ w_branch_a
        y_b = mla_attention(q_lat, kv_lat, k_pe, positions, q_norm_g, w_uq, kv_norm_g, w_ukv) @ w_branch_b
        merged = jax.nn.sigmoid(g_a) * y_a + jax.nn.sigmoid(g_b) * y_b
        x = x + gate1 * rmsnorm(merged @ w_out, post_norm1_g)

        h = rmsnorm(x, pre_norm2_g) * (1.0 + scale2) + shift2
        up = causal_dwconv(h @ w_up, conv_w, conv_b)
        gate_h, val_h = jnp.split(up, 2, axis=-1)
        ffn = (jax.nn.silu(gate_h) * val_h) @ w_down
        x = x + gate2 * rmsnorm(ffn, post_norm2_g)
    return x


import jax as _jax
import jax.numpy as _jnp

TWIN_FORMAT = 'train_step'
FWD_PARAMS = ['x', 'c', 'positions', 'w_ada', 'b_ada', 'pre_norm1_g', 'w_in', 'gm_ln_g', 'gm_ln_b', 'gm_w_s', 'gm_b_s', 'w_branch_a', 'q_norm_g', 'w_uq', 'kv_norm_g', 'w_ukv', 'w_branch_b', 'w_out', 'post_norm1_g', 'pre_norm2_g', 'w_up', 'conv_w', 'conv_b', 'w_down', 'post_norm2_g']
TWIN_WEIGHTS = ['w_ada', 'b_ada', 'pre_norm1_g', 'w_in', 'gm_ln_g', 'gm_ln_b', 'gm_w_s', 'gm_b_s', 'w_branch_a', 'q_norm_g', 'w_uq', 'kv_norm_g', 'w_ukv', 'w_branch_b', 'w_out', 'post_norm1_g', 'pre_norm2_g', 'w_up', 'conv_w', 'conv_b', 'w_down', 'post_norm2_g']
TWIN_DIFF_INPUT = 'x'
TWIN_INPUTS = ['x', 'c', 'positions', 'w_ada', 'b_ada', 'pre_norm1_g', 'w_in', 'gm_ln_g', 'gm_ln_b', 'gm_w_s', 'gm_b_s', 'w_branch_a', 'q_norm_g', 'w_uq', 'kv_norm_g', 'w_ukv', 'w_branch_b', 'w_out', 'post_norm1_g', 'pre_norm2_g', 'w_up', 'conv_w', 'conv_b', 'w_down', 'post_norm2_g', 'loss_target', 'm_w_ada', 'm_b_ada', 'm_pre_norm1_g', 'm_w_in', 'm_gm_ln_g', 'm_gm_ln_b', 'm_gm_w_s', 'm_gm_b_s', 'm_w_branch_a', 'm_q_norm_g', 'm_w_uq', 'm_kv_norm_g', 'm_w_ukv', 'm_w_branch_b', 'm_w_out', 'm_post_norm1_g', 'm_pre_norm2_g', 'm_w_up', 'm_conv_w', 'm_conv_b', 'm_w_down', 'm_post_norm2_g', 'v_w_ada', 'v_b_ada', 'v_pre_norm1_g', 'v_w_in', 'v_gm_ln_g', 'v_gm_ln_b', 'v_gm_w_s', 'v_gm_b_s', 'v_w_branch_a', 'v_q_norm_g', 'v_w_uq', 'v_kv_norm_g', 'v_w_ukv', 'v_w_branch_b', 'v_w_out', 'v_post_norm1_g', 'v_pre_norm2_g', 'v_w_up', 'v_conv_w', 'v_conv_b', 'v_w_down', 'v_post_norm2_g']
TWIN_OUTPUTS = ['loss', 'grad_x', 'grad_w_ada', 'grad_b_ada', 'grad_pre_norm1_g', 'grad_w_in', 'grad_gm_ln_g', 'grad_gm_ln_b', 'grad_gm_w_s', 'grad_gm_b_s', 'grad_w_branch_a', 'grad_q_norm_g', 'grad_w_uq', 'grad_kv_norm_g', 'grad_w_ukv', 'grad_w_branch_b', 'grad_w_out', 'grad_post_norm1_g', 'grad_pre_norm2_g', 'grad_w_up', 'grad_conv_w', 'grad_conv_b', 'grad_w_down', 'grad_post_norm2_g', 'delta_w_ada', 'delta_b_ada', 'delta_pre_norm1_g', 'delta_w_in', 'delta_gm_ln_g', 'delta_gm_ln_b', 'delta_gm_w_s', 'delta_gm_b_s', 'delta_w_branch_a', 'delta_q_norm_g', 'delta_w_uq', 'delta_kv_norm_g', 'delta_w_ukv', 'delta_w_branch_b', 'delta_w_out', 'delta_post_norm1_g', 'delta_pre_norm2_g', 'delta_w_up', 'delta_conv_w', 'delta_conv_b', 'delta_w_down', 'delta_post_norm2_g', 'new_m_w_ada', 'new_m_b_ada', 'new_m_pre_norm1_g', 'new_m_w_in', 'new_m_gm_ln_g', 'new_m_gm_ln_b', 'new_m_gm_w_s', 'new_m_gm_b_s', 'new_m_w_branch_a', 'new_m_q_norm_g', 'new_m_w_uq', 'new_m_kv_norm_g', 'new_m_w_ukv', 'new_m_w_branch_b', 'new_m_w_out', 'new_m_post_norm1_g', 'new_m_pre_norm2_g', 'new_m_w_up', 'new_m_conv_w', 'new_m_conv_b', 'new_m_w_down', 'new_m_post_norm2_g', 'new_v_w_ada', 'new_v_b_ada', 'new_v_pre_norm1_g', 'new_v_w_in', 'new_v_gm_ln_g', 'new_v_gm_ln_b', 'new_v_gm_w_s', 'new_v_gm_b_s', 'new_v_w_branch_a', 'new_v_q_norm_g', 'new_v_w_uq', 'new_v_kv_norm_g', 'new_v_w_ukv', 'new_v_w_branch_b', 'new_v_w_out', 'new_v_post_norm1_g', 'new_v_pre_norm2_g', 'new_v_w_up', 'new_v_conv_w', 'new_v_conv_b', 'new_v_w_down', 'new_v_post_norm2_g']
TWIN_LEAF_KINDS = {'loss': 'loss', 'grad_x': 'grad_x', 'grad_w_ada': 'grad_w', 'grad_b_ada': 'grad_w', 'grad_pre_norm1_g': 'grad_w', 'grad_w_in': 'grad_w', 'grad_gm_ln_g': 'grad_w', 'grad_gm_ln_b': 'grad_w', 'grad_gm_w_s': 'grad_w', 'grad_gm_b_s': 'grad_w', 'grad_w_branch_a': 'grad_w', 'grad_q_norm_g': 'grad_w', 'grad_w_uq': 'grad_w', 'grad_kv_norm_g': 'grad_w', 'grad_w_ukv': 'grad_w', 'grad_w_branch_b': 'grad_w', 'grad_w_out': 'grad_w', 'grad_post_norm1_g': 'grad_w', 'grad_pre_norm2_g': 'grad_w', 'grad_w_up': 'grad_w', 'grad_conv_w': 'grad_w', 'grad_conv_b': 'grad_w', 'grad_w_down': 'grad_w', 'grad_post_norm2_g': 'grad_w', 'delta_w_ada': 'delta_w', 'delta_b_ada': 'delta_w', 'delta_pre_norm1_g': 'delta_w', 'delta_w_in': 'delta_w', 'delta_gm_ln_g': 'delta_w', 'delta_gm_ln_b': 'delta_w', 'delta_gm_w_s': 'delta_w', 'delta_gm_b_s': 'delta_w', 'delta_w_branch_a': 'delta_w', 'delta_q_norm_g': 'delta_w', 'delta_w_uq': 'delta_w', 'delta_kv_norm_g': 'delta_w', 'delta_w_ukv': 'delta_w', 'delta_w_branch_b': 'delta_w', 'delta_w_out': 'delta_w', 'delta_post_norm1_g': 'delta_w', 'delta_pre_norm2_g': 'delta_w', 'delta_w_up': 'delta_w', 'delta_conv_w': 'delta_w', 'delta_conv_b': 'delta_w', 'delta_w_down': 'delta_w', 'delta_post_norm2_g': 'delta_w', 'new_m_w_ada': 'new_m', 'new_m_b_ada': 'new_m', 'new_m_pre_norm1_g': 'new_m', 'new_m_w_in': 'new_m', 'new_m_gm_ln_g': 'new_m', 'new_m_gm_ln_b': 'new_m', 'new_m_gm_w_s': 'new_m', 'new_m_gm_b_s': 'new_m', 'new_m_w_branch_a': 'new_m', 'new_m_q_norm_g': 'new_m', 'new_m_w_uq': 'new_m', 'new_m_kv_norm_g': 'new_m', 'new_m_w_ukv': 'new_m', 'new_m_w_branch_b': 'new_m', 'new_m_w_out': 'new_m', 'new_m_post_norm1_g': 'new_m', 'new_m_pre_norm2_g': 'new_m', 'new_m_w_up': 'new_m', 'new_m_conv_w': 'new_m', 'new_m_conv_b': 'new_m', 'new_m_w_down': 'new_m', 'new_m_post_norm2_g': 'new_m', 'new_v_w_ada': 'new_v', 'new_v_b_ada': 'new_v', 'new_v_pre_norm1_g': 'new_v', 'new_v_w_in': 'new_v', 'new_v_gm_ln_g': 'new_v', 'new_v_gm_ln_b': 'new_v', 'new_v_gm_w_s': 'new_v', 'new_v_gm_b_s': 'new_v', 'new_v_w_branch_a': 'new_v', 'new_v_q_norm_g': 'new_v', 'new_v_w_uq': 'new_v', 'new_v_kv_norm_g': 'new_v', 'new_v_w_ukv': 'new_v', 'new_v_w_branch_b': 'new_v', 'new_v_w_out': 'new_v', 'new_v_post_norm1_g': 'new_v', 'new_v_pre_norm2_g': 'new_v', 'new_v_w_up': 'new_v', 'new_v_conv_w': 'new_v', 'new_v_conv_b': 'new_v', 'new_v_w_down': 'new_v', 'new_v_post_norm2_g': 'new_v'}


def _forward(args):
    return _fwd_reference(*[args[k] for k in FWD_PARAMS])


def _output_shape():
    out = _jax.eval_shape(lambda: _forward(_fwd_setup_inputs(0)))
    return out.shape, out.dtype

N_MICROBATCH = 1
ADAM_LR = 0.001
ADAM_B1 = 0.9
ADAM_B2 = 0.999
ADAM_EPS = 1e-08
ADAM_WD = 0.01
ADAM_STEP = 10
PER_EXAMPLE_BATCH_AXIS = {'x': 0, 'c': 0, 'positions': 0, 'loss_target': 0}
SHARED_INPUTS = []
_WEIGHT_DTYPES = {'w_ada': _jnp.float32, 'b_ada': _jnp.float32, 'pre_norm1_g': _jnp.float32, 'w_in': _jnp.float32, 'gm_ln_g': _jnp.float32, 'gm_ln_b': _jnp.float32, 'gm_w_s': _jnp.float32, 'gm_b_s': _jnp.float32, 'w_branch_a': _jnp.float32, 'q_norm_g': _jnp.float32, 'w_uq': _jnp.float32, 'kv_norm_g': _jnp.float32, 'w_ukv': _jnp.float32, 'w_branch_b': _jnp.float32, 'w_out': _jnp.float32, 'post_norm1_g': _jnp.float32, 'pre_norm2_g': _jnp.float32, 'w_up': _jnp.float32, 'conv_w': _jnp.float32, 'conv_b': _jnp.float32, 'w_down': _jnp.float32, 'post_norm2_g': _jnp.float32}
MOMENT_SCALE = {'w_ada': 4.335864e-01, 'b_ada': 8.212620e-01, 'pre_norm1_g': 3.903824e-02, 'w_in': 3.515112e-02, 'gm_ln_g': 1.619680e-02, 'gm_ln_b': 1.728679e-02, 'gm_w_s': 1.644411e-02, 'gm_b_s': 2.562370e-02, 'w_branch_a': 7.897452e-02, 'q_norm_g': 1.235316e-02, 'w_uq': 4.985832e-03, 'kv_norm_g': 1.567278e-01, 'w_ukv': 4.051448e-02, 'w_branch_b': 5.793994e-02, 'w_out': 9.973279e-02, 'post_norm1_g': 9.499102e-01, 'pre_norm2_g': 4.021851e-02, 'w_up': 1.957156e-02, 'conv_w': 2.013961e-02, 'conv_b': 3.349333e-02, 'w_down': 3.497046e-02, 'post_norm2_g': 8.701822e-01}


def _to_microbatches(a, axis):
    t = _jnp.moveaxis(a, axis, 0)
    t = t.reshape((N_MICROBATCH, t.shape[0] // N_MICROBATCH) + t.shape[1:])
    return _jnp.moveaxis(t, 1, axis + 1)


def setup_inputs(seed: int = 0) -> dict:
    inp = _fwd_setup_inputs(seed)
    key = _jax.random.fold_in(_jax.random.key(seed), 7919)
    shape, _ = _output_shape()
    out = dict(inp)
    out["loss_target"] = _jax.random.normal(_jax.random.fold_in(key, 0), shape, _jnp.float32)
    for i, name in enumerate(TWIN_WEIGHTS):
        w = inp[name].astype(_jnp.float32)
        if MOMENT_SCALE is None:
            s = _jnp.sqrt(_jnp.mean(_jnp.square(w)) + 1e-30)
        else:
            s = MOMENT_SCALE[name]
        km, kv = _jax.random.split(_jax.random.fold_in(key, i + 1))
        out[name] = w
        out["m_" + name] = s * _jax.random.normal(km, w.shape, _jnp.float32)
        out["v_" + name] = (s * s) * _jax.random.uniform(kv, w.shape, _jnp.float32, 0.5, 1.5)
    if N_MICROBATCH > 1:
        for name, axis in PER_EXAMPLE_BATCH_AXIS.items():
            out[name] = _to_microbatches(out[name], axis)
    return {'x': out['x'], 'c': out['c'], 'positions': out['positions'], 'w_ada': out['w_ada'], 'b_ada': out['b_ada'], 'pre_norm1_g': out['pre_norm1_g'], 'w_in': out['w_in'], 'gm_ln_g': out['gm_ln_g'], 'gm_ln_b': out['gm_ln_b'], 'gm_w_s': out['gm_w_s'], 'gm_b_s': out['gm_b_s'], 'w_branch_a': out['w_branch_a'], 'q_norm_g': out['q_norm_g'], 'w_uq': out['w_uq'], 'kv_norm_g': out['kv_norm_g'], 'w_ukv': out['w_ukv'], 'w_branch_b': out['w_branch_b'], 'w_out': out['w_out'], 'post_norm1_g': out['post_norm1_g'], 'pre_norm2_g': out['pre_norm2_g'], 'w_up': out['w_up'], 'conv_w': out['conv_w'], 'conv_b': out['conv_b'], 'w_down': out['w_down'], 'post_norm2_g': out['post_norm2_g'], 'loss_target': out['loss_target'], 'm_w_ada': out['m_w_ada'], 'm_b_ada': out['m_b_ada'], 'm_pre_norm1_g': out['m_pre_norm1_g'], 'm_w_in': out['m_w_in'], 'm_gm_ln_g': out['m_gm_ln_g'], 'm_gm_ln_b': out['m_gm_ln_b'], 'm_gm_w_s': out['m_gm_w_s'], 'm_gm_b_s': out['m_gm_b_s'], 'm_w_branch_a': out['m_w_branch_a'], 'm_q_norm_g': out['m_q_norm_g'], 'm_w_uq': out['m_w_uq'], 'm_kv_norm_g': out['m_kv_norm_g'], 'm_w_ukv': out['m_w_ukv'], 'm_w_branch_b': out['m_w_branch_b'], 'm_w_out': out['m_w_out'], 'm_post_norm1_g': out['m_post_norm1_g'], 'm_pre_norm2_g': out['m_pre_norm2_g'], 'm_w_up': out['m_w_up'], 'm_conv_w': out['m_conv_w'], 'm_conv_b': out['m_conv_b'], 'm_w_down': out['m_w_down'], 'm_post_norm2_g': out['m_post_norm2_g'], 'v_w_ada': out['v_w_ada'], 'v_b_ada': out['v_b_ada'], 'v_pre_norm1_g': out['v_pre_norm1_g'], 'v_w_in': out['v_w_in'], 'v_gm_ln_g': out['v_gm_ln_g'], 'v_gm_ln_b': out['v_gm_ln_b'], 'v_gm_w_s': out['v_gm_w_s'], 'v_gm_b_s': out['v_gm_b_s'], 'v_w_branch_a': out['v_w_branch_a'], 'v_q_norm_g': out['v_q_norm_g'], 'v_w_uq': out['v_w_uq'], 'v_kv_norm_g': out['v_kv_norm_g'], 'v_w_ukv': out['v_w_ukv'], 'v_w_branch_b': out['v_w_branch_b'], 'v_w_out': out['v_w_out'], 'v_post_norm1_g': out['v_post_norm1_g'], 'v_pre_norm2_g': out['v_pre_norm2_g'], 'v_w_up': out['v_w_up'], 'v_conv_w': out['v_conv_w'], 'v_conv_b': out['v_conv_b'], 'v_w_down': out['v_w_down'], 'v_post_norm2_g': out['v_post_norm2_g']}


def _loss(weights, diff, rest, loss_target):
    with _jax.named_scope("forward"):
        args = {**rest, TWIN_DIFF_INPUT: diff, **{k: w.astype(_WEIGHT_DTYPES[k]) for k, w in weights.items()}}
        y = _forward(args)
    with _jax.named_scope("loss_head"):
        err = _jnp.square(y.astype(_jnp.float32) - loss_target)
        return 0.5 * _jnp.sum(_jnp.mean(err, axis=-1)) if err.ndim else 0.5 * err


def _adamw(w, g, m, v):
    m = ADAM_B1 * m + (1.0 - ADAM_B1) * g
    v = ADAM_B2 * v + (1.0 - ADAM_B2) * _jnp.square(g)
    m_hat = m / (1.0 - ADAM_B1 ** ADAM_STEP)
    v_hat = v / (1.0 - ADAM_B2 ** ADAM_STEP)
    delta = -ADAM_LR * (m_hat / (_jnp.sqrt(v_hat) + ADAM_EPS) + ADAM_WD * w)
    return delta, m, v


def reference(x, c, positions, w_ada, b_ada, pre_norm1_g, w_in, gm_ln_g, gm_ln_b, gm_w_s, gm_b_s, w_branch_a, q_norm_g, w_uq, kv_norm_g, w_ukv, w_branch_b, w_out, post_norm1_g, pre_norm2_g, w_up, conv_w, conv_b, w_down, post_norm2_g, loss_target, m_w_ada, m_b_ada, m_pre_norm1_g, m_w_in, m_gm_ln_g, m_gm_ln_b, m_gm_w_s, m_gm_b_s, m_w_branch_a, m_q_norm_g, m_w_uq, m_kv_norm_g, m_w_ukv, m_w_branch_b, m_w_out, m_post_norm1_g, m_pre_norm2_g, m_w_up, m_conv_w, m_conv_b, m_w_down, m_post_norm2_g, v_w_ada, v_b_ada, v_pre_norm1_g, v_w_in, v_gm_ln_g, v_gm_ln_b, v_gm_w_s, v_gm_b_s, v_w_branch_a, v_q_norm_g, v_w_uq, v_kv_norm_g, v_w_ukv, v_w_branch_b, v_w_out, v_post_norm1_g, v_pre_norm2_g, v_w_up, v_conv_w, v_conv_b, v_w_down, v_post_norm2_g):
    given = dict(x=x, c=c, positions=positions, w_ada=w_ada, b_ada=b_ada, pre_norm1_g=pre_norm1_g, w_in=w_in, gm_ln_g=gm_ln_g, gm_ln_b=gm_ln_b, gm_w_s=gm_w_s, gm_b_s=gm_b_s, w_branch_a=w_branch_a, q_norm_g=q_norm_g, w_uq=w_uq, kv_norm_g=kv_norm_g, w_ukv=w_ukv, w_branch_b=w_branch_b, w_out=w_out, post_norm1_g=post_norm1_g, pre_norm2_g=pre_norm2_g, w_up=w_up, conv_w=conv_w, conv_b=conv_b, w_down=w_down, post_norm2_g=post_norm2_g, loss_target=loss_target, m_w_ada=m_w_ada, m_b_ada=m_b_ada, m_pre_norm1_g=m_pre_norm1_g, m_w_in=m_w_in, m_gm_ln_g=m_gm_ln_g, m_gm_ln_b=m_gm_ln_b, m_gm_w_s=m_gm_w_s, m_gm_b_s=m_gm_b_s, m_w_branch_a=m_w_branch_a, m_q_norm_g=m_q_norm_g, m_w_uq=m_w_uq, m_kv_norm_g=m_kv_norm_g, m_w_ukv=m_w_ukv, m_w_branch_b=m_w_branch_b, m_w_out=m_w_out, m_post_norm1_g=m_post_norm1_g, m_pre_norm2_g=m_pre_norm2_g, m_w_up=m_w_up, m_conv_w=m_conv_w, m_conv_b=m_conv_b, m_w_down=m_w_down, m_post_norm2_g=m_post_norm2_g, v_w_ada=v_w_ada, v_b_ada=v_b_ada, v_pre_norm1_g=v_pre_norm1_g, v_w_in=v_w_in, v_gm_ln_g=v_gm_ln_g, v_gm_ln_b=v_gm_ln_b, v_gm_w_s=v_gm_w_s, v_gm_b_s=v_gm_b_s, v_w_branch_a=v_w_branch_a, v_q_norm_g=v_q_norm_g, v_w_uq=v_w_uq, v_kv_norm_g=v_kv_norm_g, v_w_ukv=v_w_ukv, v_w_branch_b=v_w_branch_b, v_w_out=v_w_out, v_post_norm1_g=v_post_norm1_g, v_pre_norm2_g=v_pre_norm2_g, v_w_up=v_w_up, v_conv_w=v_conv_w, v_conv_b=v_conv_b, v_w_down=v_w_down, v_post_norm2_g=v_post_norm2_g)
    weights = {n: given[n] for n in TWIN_WEIGHTS}
    shared = {n: given[n] for n in SHARED_INPUTS}
    per_example = {n: given[n] for n in ['x', 'c', 'positions']}
    grad_fn = _jax.value_and_grad(_loss, argnums=(0, 1))

    def one_microbatch(ex, loss_target):
        ex = dict(ex)
        diff = ex.pop(TWIN_DIFF_INPUT)
        return grad_fn(weights, diff, {**shared, **ex}, loss_target)

    if N_MICROBATCH == 1:
        loss, (grad_w, grad_x) = one_microbatch(per_example, given["loss_target"])
    else:
        def body(carry, xs):
            loss_sum, grad_sum = carry
            l_k, (gw_k, gx_k) = one_microbatch(xs[0], xs[1])
            with _jax.named_scope("update"):
                return (loss_sum + l_k, _jax.tree.map(_jnp.add, grad_sum, gw_k)), gx_k

        init = (_jnp.zeros((), _jnp.float32), _jax.tree.map(_jnp.zeros_like, weights))
        (loss, grad_w), grad_x = _jax.lax.scan(body, init, (per_example, given["loss_target"]))
    with _jax.named_scope("update"):
        delta_w, new_m, new_v = {}, {}, {}
        for n in TWIN_WEIGHTS:
            delta_w[n], new_m[n], new_v[n] = _adamw(weights[n], grad_w[n], given["m_" + n], given["v_" + n])
    return (loss, grad_x, *[grad_w[n] for n in TWIN_WEIGHTS], *[delta_w[n] for n in TWIN_WEIGHTS],
            *[new_m[n] for n in TWIN_WEIGHTS], *[new_v[n] for n in TWIN_WEIGHTS])
```

```python
import functools

import jax
import jax.numpy as jnp
from jax import lax
from jax.experimental import pallas as pl
from jax.experimental.pallas import tpu as pltpu

F32 = jnp.float32
BF16 = jnp.bfloat16

N_DEV = 8
HEADS = 16
QK_NOPE = 128
QK_ROPE = 64
V_HEAD = 128
CHUNK = 128
ROPE_THETA = 10000.0
EPS = 1e-6
N_MOD = 6
ADAM_LR, ADAM_B1, ADAM_B2, ADAM_EPS, ADAM_WD, ADAM_STEP = 0.001, 0.9, 0.999, 1e-08, 0.01, 10

LANES = 128
VMEM_LIMIT_BYTES = 48 * 2 ** 20
ROW_TILE = 256
COL_TILE = 256
ATT_TILE = 256
Z_PAD = 512
ADAMW_TILE_ELEMS = 1 << 18
PACK_ALIGN = 8 * LANES
MESH = pl.DeviceIdType.MESH


def _params(*sem):
    return pltpu.CompilerParams(dimension_semantics=sem if sem else None, vmem_limit_bytes=VMEM_LIMIT_BYTES)


def _tile(dim, target):
    t = (min(dim, target) // LANES) * LANES
    while t >= LANES:
        if dim % t == 0:
            return t
        t -= LANES
    return dim


def _full(shape):
    nd = len(shape)
    return pl.BlockSpec(shape, lambda *_: (0,) * nd)


def _matmul(name, a, b, *, grid, a_spec, b_spec, o_spec, out_shape, contract, acc_shape):
    nk = grid[2]

    def body(a_ref, b_ref, o_ref, acc_ref):
        k = pl.program_id(2)

        @pl.when(k == 0)
        def _():
            acc_ref[...] = jnp.zeros_like(acc_ref)

        acc_ref[...] += lax.dot_general(a_ref[...].astype(BF16), b_ref[...].astype(BF16),
                                        (contract, ((), ())), preferred_element_type=F32)

        @pl.when(k == nk - 1)
        def _():
            o_ref[...] = acc_ref[...].astype(o_ref.dtype)

    return pl.pallas_call(
        body, name=name, grid=grid, in_specs=[a_spec, b_spec], out_specs=o_spec, out_shape=out_shape,
        scratch_shapes=[pltpu.VMEM(acc_shape, F32)],
        compiler_params=_params("parallel", "parallel", "arbitrary"))(a, b)


TM, TN, TK = 1024, 1024, 512


def mm_nn(name, a, b, dtype):
    (m, k), n = a.shape, b.shape[1]
    tm, tn, tk = _tile(m, TM), _tile(n, TN), _tile(k, TK)
    return _matmul(name, a, b, grid=(m // tm, n // tn, k // tk),
                   a_spec=pl.BlockSpec((tm, tk), lambda i, j, kk: (i, kk)),
                   b_spec=pl.BlockSpec((tk, tn), lambda i, j, kk: (kk, j)),
                   o_spec=pl.BlockSpec((tm, tn), lambda i, j, kk: (i, j)),
                   out_shape=jax.ShapeDtypeStruct((m, n), dtype), contract=((1,), (0,)), acc_shape=(tm, tn))


def mm_nn_b3(name, a, b3, dtype):
    (m, k), (nj, _, cs) = a.shape, b3.shape
    tm, tk = _tile(m, TM), _tile(k, TK)
    return _matmul(name, a, b3, grid=(m // tm, nj, k // tk),
                   a_spec=pl.BlockSpec((tm, tk), lambda i, j, kk: (i, kk)),
                   b_spec=pl.BlockSpec((None, tk, cs), lambda i, j, kk: (j, kk, 0)),
                   o_spec=pl.BlockSpec((tm, cs), lambda i, j, kk: (i, j)),
                   out_shape=jax.ShapeDtypeStruct((m, nj * cs), dtype), contract=((1,), (0,)), acc_shape=(tm, cs))


def mm_nt(name, a, b, dtype):
    (m, k), n = a.shape, b.shape[0]
    tm, tn, tk = _tile(m, TM), _tile(n, TN), _tile(k, TK)
    return _matmul(name, a, b, grid=(m // tm, n // tn, k // tk),
                   a_spec=pl.BlockSpec((tm, tk), lambda i, j, kk: (i, kk)),
                   b_spec=pl.BlockSpec((tn, tk), lambda i, j, kk: (j, kk)),
                   o_spec=pl.BlockSpec((tm, tn), lambda i, j, kk: (i, j)),
                   out_shape=jax.ShapeDtypeStruct((m, n), dtype), contract=((1,), (1,)), acc_shape=(tm, tn))


def mm_nt_b3(name, a, b3, dtype):
    m, (nj, n, cs) = a.shape[0], b3.shape
    tm, tn = _tile(m, TM), _tile(n, TN)
    return _matmul(name, a, b3, grid=(m // tm, n // tn, nj),
                   a_spec=pl.BlockSpec((tm, cs), lambda i, j, kk: (i, kk)),
                   b_spec=pl.BlockSpec((None, tn, cs), lambda i, j, kk: (kk, j, 0)),
                   o_spec=pl.BlockSpec((tm, tn), lambda i, j, kk: (i, j)),
                   out_shape=jax.ShapeDtypeStruct((m, n), dtype), contract=((1,), (1,)), acc_shape=(tm, tn))


def mm_tn(name, a, b, dtype):
    (k, m), n = a.shape, b.shape[1]
    tm, tn, tk = _tile(m, TM), _tile(n, TN), _tile(k, TK)
    return _matmul(name, a, b, grid=(m // tm, n // tn, k // tk),
                   a_spec=pl.BlockSpec((tk, tm), lambda i, j, kk: (kk, i)),
                   b_spec=pl.BlockSpec((tk, tn), lambda i, j, kk: (kk, j)),
                   o_spec=pl.BlockSpec((tm, tn), lambda i, j, kk: (i, j)),
                   out_shape=jax.ShapeDtypeStruct((m, n), dtype), contract=((0,), (0,)), acc_shape=(tm, tn))


def mm_tn_o3(name, a, b, nj, dtype):
    (k, m), n = a.shape, b.shape[1]
    cs = n // nj
    tm, tk = _tile(m, TM), _tile(k, TK)
    return _matmul(name, a, b, grid=(m // tm, nj, k // tk),
                   a_spec=pl.BlockSpec((tk, tm), lambda i, j, kk: (kk, i)),
                   b_spec=pl.BlockSpec((tk, cs), lambda i, j, kk: (kk, j)),
                   o_spec=pl.BlockSpec((None, tm, cs), lambda i, j, kk: (j, i, 0)),
                   out_shape=jax.ShapeDtypeStruct((nj, m, cs), dtype), contract=((0,), (0,)), acc_shape=(tm, cs))


_GELU_C = 0.7978845608028654
_GELU_A = 0.044715


def _gelu(x):
    return 0.5 * x * (1.0 + jnp.tanh(_GELU_C * (x + _GELU_A * x * x * x)))


def _gelu_and_grad(x):
    t = jnp.tanh(_GELU_C * (x + _GELU_A * x * x * x))
    y = 0.5 * x * (1.0 + t)
    dy = 0.5 * (1.0 + t) + 0.5 * x * (1.0 - t * t) * (_GELU_C * (1.0 + 3.0 * _GELU_A * x * x))
    return y, dy


def _sigmoid(x):
    return 1.0 / (1.0 + jnp.exp(-x))


def _rms_stats(x):
    inv = lax.rsqrt(jnp.mean(x * x, axis=-1, keepdims=True) + EPS)
    return inv, x * inv


def _rms_bwd(dyhat, yhat, inv):
    return inv * (dyhat - yhat * jnp.mean(dyhat * yhat, axis=-1, keepdims=True))


def _colsum(x):
    return jnp.sum(x, axis=0, keepdims=True)


def _rope(x, cos4, sin4):
    lane = lax.broadcasted_iota(jnp.int32, x.shape, x.ndim - 1)
    first_half = (lane % QK_ROPE) < (QK_ROPE // 2)
    partner = jnp.where(first_half, pltpu.roll(x, LANES - QK_ROPE // 2, x.ndim - 1), pltpu.roll(x, QK_ROPE // 2, x.ndim - 1))
    return x * cos4 + partner * sin4


def norm_mod_fwd(name, x, g, scale, shift):
    s, d = x.shape
    tr = _tile(s, ROW_TILE)

    def body(x_ref, g_ref, sc_ref, sh_ref, o_ref):
        _, xh = _rms_stats(x_ref[...])
        o_ref[...] = (xh * g_ref[...] * (1.0 + sc_ref[...]) + sh_ref[...]).astype(o_ref.dtype)

    row = pl.BlockSpec((tr, d), lambda i: (i, 0))
    vec = pl.BlockSpec((1, d), lambda i: (0, 0))
    return pl.pallas_call(body, name=name, grid=(s // tr,), in_specs=[row, vec, vec, vec], out_specs=row,
                          out_shape=jax.ShapeDtypeStruct((s, d), BF16), compiler_params=_params("parallel"))(x, g, scale, shift)


def rms_fwd_cols(name, z, off, width, g):
    s = z.shape[0]
    tr = _tile(s, ROW_TILE)
    assert off % width == 0

    def body(x_ref, g_ref, o_ref):
        _, xh = _rms_stats(x_ref[...])
        o_ref[...] = (xh * g_ref[...]).astype(o_ref.dtype)

    return pl.pallas_call(body, name=name, grid=(s // tr,),
                          in_specs=[pl.BlockSpec((tr, width), lambda i: (i, off // width)), pl.BlockSpec((1, width), lambda i: (0, 0))],
                          out_specs=pl.BlockSpec((tr, width), lambda i: (i, 0)),
                          out_shape=jax.ShapeDtypeStruct((s, width), BF16), compiler_params=_params("parallel"))(z, g)


def rms_bwd_cols(name, dy, z, off, width, g):
    s = z.shape[0]
    tr = _tile(s, ROW_TILE)

    def body(dy_ref, x_ref, g_ref, dx_ref, gg_ref):
        @pl.when(pl.program_id(0) == 0)
        def _():
            gg_ref[...] = jnp.zeros_like(gg_ref)

        inv, xh = _rms_stats(x_ref[...])
        dy_v = dy_ref[...]
        gg_ref[...] += _colsum(dy_v * xh)
        dx_ref[...] = _rms_bwd(dy_v * g_ref[...], xh, inv).astype(dx_ref.dtype)

    return pl.pallas_call(body, name=name, grid=(s // tr,),
                          in_specs=[pl.BlockSpec((tr, width), lambda i: (i, 0)), pl.BlockSpec((tr, width), lambda i: (i, off // width)),
                                    pl.BlockSpec((1, width), lambda i: (0, 0))],
                          out_specs=[pl.BlockSpec((tr, width), lambda i: (i, 0)), pl.BlockSpec((1, width), lambda i: (0, 0))],
                          out_shape=[jax.ShapeDtypeStruct((s, width), BF16), jax.ShapeDtypeStruct((1, width), F32)],
                          compiler_params=_params("arbitrary"))(dy, z, g)


def post_res_fwd(name, x, y, gate, g):
    s, d = x.shape
    tr = _tile(s, ROW_TILE)

    def body(x_ref, y_ref, gate_ref, g_ref, o_ref):
        _, yh = _rms_stats(y_ref[...])
        o_ref[...] = x_ref[...] + gate_ref[...] * (yh * g_ref[...])

    row = pl.BlockSpec((tr, d), lambda i: (i, 0))
    vec = pl.BlockSpec((1, d), lambda i: (0, 0))
    return pl.pallas_call(body, name=name, grid=(s // tr,), in_specs=[row, row, vec, vec], out_specs=row,
                          out_shape=jax.ShapeDtypeStruct((s, d), F32), compiler_params=_params("parallel"))(x, y, gate, g)


def post2_loss_bwd(x1, ffn, target, gate2, g):
    s, d = x1.shape
    tr = _tile(s, ROW_TILE)

    def body(x_ref, y_ref, t_ref, gate_ref, g_ref, loss_ref, dout_ref, dy_ref, acc_ref):
        @pl.when(pl.program_id(0) == 0)
        def _():
            loss_ref[...] = jnp.zeros_like(loss_ref)
            acc_ref[...] = jnp.zeros_like(acc_ref)

        inv, yh = _rms_stats(y_ref[...])
        r = yh * g_ref[...]
        err = x_ref[...] + gate_ref[...] * r - t_ref[...]
        loss_ref[...] += 0.5 * jnp.sum(jnp.mean(err * err, axis=-1, keepdims=True))
        dout = err / d
        dout_ref[...] = dout
        dr = dout * gate_ref[...]
        acc_ref[0:1, :] += _colsum(dout * r)
        acc_ref[1:2, :] += _colsum(dr * yh)
        dy_ref[...] = _rms_bwd(dr * g_ref[...], yh, inv).astype(dy_ref.dtype)

    row = pl.BlockSpec((tr, d), lambda i: (i, 0))
    vec = pl.BlockSpec((1, d), lambda i: (0, 0))
    return pl.pallas_call(
        body, name="post2_loss_bwd", grid=(s // tr,), in_specs=[row, row, row, vec, vec],
        out_specs=[_full((8, LANES)), row, row, _full((8, d))],
        out_shape=[jax.ShapeDtypeStruct((8, LANES), F32), jax.ShapeDtypeStruct((s, d), F32),
                   jax.ShapeDtypeStruct((s, d), BF16), jax.ShapeDtypeStruct((8, d), F32)],
        compiler_params=_params("arbitrary"))(x1, ffn, target, gate2, g)


def mid_bwd(dh2, dout, x1, y1, pre2_g, scale2, gate1, post1_g):
    s, d = x1.shape
    tr = _tile(s, ROW_TILE)

    def body(dh_ref, dout_ref, x_ref, y_ref, g2_ref, sc_ref, gate_ref, g1_ref, dx_ref, dy_ref, acc_ref):
        @pl.when(pl.program_id(0) == 0)
        def _():
            acc_ref[...] = jnp.zeros_like(acc_ref)

        dh = dh_ref[...]
        inv2, xh = _rms_stats(x_ref[...])
        acc_ref[0:1, :] += _colsum(dh)
        acc_ref[1:2, :] += _colsum(dh * (xh * g2_ref[...]))
        t = dh * (1.0 + sc_ref[...])
        acc_ref[2:3, :] += _colsum(t * xh)
        dx1 = dout_ref[...] + _rms_bwd(t * g2_ref[...], xh, inv2)
        dx_ref[...] = dx1
        inv1, yh = _rms_stats(y_ref[...])
        acc_ref[3:4, :] += _colsum(dx1 * (yh * g1_ref[...]))
        dr = dx1 * gate_ref[...]
        acc_ref[4:5, :] += _colsum(dr * yh)
        dy_ref[...] = _rms_bwd(dr * g1_ref[...], yh, inv1).astype(dy_ref.dtype)

    row = pl.BlockSpec((tr, d), lambda i: (i, 0))
    vec = pl.BlockSpec((1, d), lambda i: (0, 0))
    return pl.pallas_call(
        body, name="mid_bwd", grid=(s // tr,), in_specs=[row, row, row, row, vec, vec, vec, vec],
        out_specs=[row, row, _full((8, d))],
        out_shape=[jax.ShapeDtypeStruct((s, d), F32), jax.ShapeDtypeStruct((s, d), BF16), jax.ShapeDtypeStruct((8, d), F32)],
        compiler_params=_params("arbitrary"))(dh2, dout, x1, y1, pre2_g, scale2, gate1, post1_g)


def pre1_bwd(dh1, dx1, x, pre1_g, scale1):
    s, d = x.shape
    tr = _tile(s, ROW_TILE)

    def body(dh_ref, dx1_ref, x_ref, g_ref, sc_ref, dx_ref, acc_ref):
        @pl.when(pl.program_id(0) == 0)
        def _():
            acc_ref[...] = jnp.zeros_like(acc_ref)

        dh = dh_ref[...]
        inv, xh = _rms_stats(x_ref[...])
        acc_ref[0:1, :] += _colsum(dh)
        acc_ref[1:2, :] += _colsum(dh * (xh * g_ref[...]))
        t = dh * (1.0 + sc_ref[...])
        acc_ref[2:3, :] += _colsum(t * xh)
        dx_ref[...] = dx1_ref[...] + _rms_bwd(t * g_ref[...], xh, inv)

    row = pl.BlockSpec((tr, d), lambda i: (i, 0))
    vec = pl.BlockSpec((1, d), lambda i: (0, 0))
    return pl.pallas_call(
        body, name="pre1_bwd", grid=(s // tr,), in_specs=[row, row, row, vec, vec], out_specs=[row, _full((8, d))],
        out_shape=[jax.ShapeDtypeStruct((s, d), F32), jax.ShapeDtypeStruct((8, d), F32)],
        compiler_params=_params("arbitrary"))(dh1, dx1, x, pre1_g, scale1)


def _ln_stats(v):
    mu = jnp.mean(v, axis=-1, keepdims=True)
    vc = v - mu
    rstd = lax.rsqrt(jnp.mean(vc * vc, axis=-1, keepdims=True) + EPS)
    return rstd, vc * rstd


def gmlp_fwd(z, width, ln_g, ln_b, wm, bs3):
    s = z.shape[0]
    groups = width // CHUNK

    def body(u_ref, v_ref, g_ref, b_ref, wm_ref, bs_ref, a_ref):
        ug = _gelu(u_ref[...])
        _, vh = _ln_stats(_gelu(v_ref[...]))
        vn = (vh * g_ref[...] + b_ref[...]).astype(BF16)
        for g in range(groups):
            cols = slice(g * CHUNK, (g + 1) * CHUNK)
            mixed = jnp.dot(wm_ref[g], vn[:, cols], preferred_element_type=F32) + bs_ref[g]
            a_ref[:, cols] = (ug[:, cols] * mixed).astype(a_ref.dtype)

    vec = pl.BlockSpec((1, width), lambda n: (0, 0))
    return pl.pallas_call(
        body, name="gmlp_fwd", grid=(s // CHUNK,),
        in_specs=[pl.BlockSpec((CHUNK, width), lambda n: (n, 0)), pl.BlockSpec((CHUNK, width), lambda n: (n, 1)), vec, vec,
                  _full(wm.shape), _full(bs3.shape)],
        out_specs=pl.BlockSpec((CHUNK, width), lambda n: (n, 0)),
        out_shape=jax.ShapeDtypeStruct((s, width), BF16), compiler_params=_params("parallel"))(z, z, ln_g, ln_b, wm, bs3)


def gmlp_bwd(z, width, da, ln_g, ln_b, wm, bs3):
    s = z.shape[0]
    groups = width // CHUNK

    def body(u_ref, v_ref, da_ref, g_ref, b_ref, wm_ref, bs_ref, duv_ref, gw_ref, gb_ref, acc_ref, dvn_ref):
        @pl.when(pl.program_id(0) == 0)
        def _():
            gw_ref[...] = jnp.zeros_like(gw_ref)
            gb_ref[...] = jnp.zeros_like(gb_ref)
            acc_ref[...] = jnp.zeros_like(acc_ref)

        ug, dug = _gelu_and_grad(u_ref[...])
        vg, dvg = _gelu_and_grad(v_ref[...])
        rstd, vh = _ln_stats(vg)
        vn = (vh * g_ref[...] + b_ref[...]).astype(BF16)
        da_v = da_ref[...]
        for g in range(groups):
            cols = slice(g * CHUNK, (g + 1) * CHUNK)
            mixed = jnp.dot(wm_ref[g], vn[:, cols], preferred_element_type=F32) + bs_ref[g]
            duv_ref[:, cols] = (da_v[:, cols] * mixed * dug[:, cols]).astype(duv_ref.dtype)
            dm = da_v[:, cols] * ug[:, cols]
            gb_ref[g] += jnp.sum(dm, axis=-1, keepdims=True)
            dmb = dm.astype(BF16)
            gw_ref[g] += lax.dot_general(dmb, vn[:, cols], (((1,), (1,)), ((), ())), preferred_element_type=F32)
            dvn_ref[:, cols] = lax.dot_general(wm_ref[g], dmb, (((0,), (0,)), ((), ())), preferred_element_type=F32)
        dvn = dvn_ref[...]
        acc_ref[0:1, :] += _colsum(dvn * vh)
        acc_ref[1:2, :] += _colsum(dvn)
        dvh = dvn * g_ref[...]
        dv = rstd * (dvh - jnp.mean(dvh, axis=-1, keepdims=True) - vh * jnp.mean(dvh * vh, axis=-1, keepdims=True))
        duv_ref[:, width:] = (dv * dvg).astype(duv_ref.dtype)

        @pl.when(pl.program_id(0) == pl.num_programs(0) - 1)
        def _():
            q = lax.broadcasted_iota(jnp.int32, gw_ref.shape, 1)
            p = lax.broadcasted_iota(jnp.int32, gw_ref.shape, 2)
            gw_ref[...] = jnp.where(p <= q, gw_ref[...], 0.0)

    vec = pl.BlockSpec((1, width), lambda n: (0, 0))
    blk = pl.BlockSpec((CHUNK, width), lambda n: (n, 0))
    return pl.pallas_call(
        body, name="gmlp_bwd", grid=(s // CHUNK,),
        in_specs=[blk, pl.BlockSpec((CHUNK, width), lambda n: (n, 1)), blk, vec, vec, _full(wm.shape), _full(bs3.shape)],
        out_specs=[pl.BlockSpec((CHUNK, 2 * width), lambda n: (n, 0)), _full(wm.shape), _full(bs3.shape), _full((8, width))],
        out_shape=[jax.ShapeDtypeStruct((s, 2 * width), BF16), jax.ShapeDtypeStruct(wm.shape, F32),
                   jax.ShapeDtypeStruct(bs3.shape, F32), jax.ShapeDtypeStruct((8, width), F32)],
        scratch_shapes=[pltpu.VMEM((CHUNK, width), F32)],
        compiler_params=_params("arbitrary"))(z, z, da, ln_g, ln_b, wm, bs3)


def merge_fwd(z, off_a, off_b, ya, yb):
    s, d = ya.shape
    tr, tc = _tile(s, ROW_TILE * 2), _tile(d, COL_TILE)
    assert off_a % tc == 0 and off_b % tc == 0

    def body(ga_ref, gb_ref, ya_ref, yb_ref, o_ref):
        o_ref[...] = (_sigmoid(ga_ref[...]) * ya_ref[...] + _sigmoid(gb_ref[...]) * yb_ref[...]).astype(o_ref.dtype)

    blk = pl.BlockSpec((tr, tc), lambda i, j: (i, j))
    return pl.pallas_call(
        body, name="merge_fwd", grid=(s // tr, d // tc),
        in_specs=[pl.BlockSpec((tr, tc), lambda i, j: (i, off_a // tc + j)), pl.BlockSpec((tr, tc), lambda i, j: (i, off_b // tc + j)), blk, blk],
        out_specs=blk, out_shape=jax.ShapeDtypeStruct((s, d), BF16), compiler_params=_params("parallel", "parallel"))(z, z, ya, yb)


def merge_bwd(z, off_a, off_b, ya, yb, dm):
    s, d = ya.shape
    tr, tc = _tile(s, ROW_TILE * 2), _tile(d, COL_TILE)
    nc = d // tc

    def body(ga_ref, gb_ref, ya_ref, yb_ref, dm_ref, dya_ref, dyb_ref, dga_ref, dgb_ref):
        dm_v = dm_ref[...]
        sa, sb = _sigmoid(ga_ref[...]), _sigmoid(gb_ref[...])
        dya_ref[...] = (dm_v * sa).astype(dya_ref.dtype)
        dyb_ref[...] = (dm_v * sb).astype(dyb_ref.dtype)
        dga_ref[...] = (dm_v * ya_ref[...] * sa * (1.0 - sa)).astype(dga_ref.dtype)
        dgb_ref[...] = (dm_v * yb_ref[...] * sb * (1.0 - sb)).astype(dgb_ref.dtype)

    blk = pl.BlockSpec((tr, tc), lambda i, j: (i, j))
    out = jax.ShapeDtypeStruct((s, d), BF16)
    return pl.pallas_call(
        body, name="merge_bwd", grid=(s // tr, nc),
        in_specs=[pl.BlockSpec((tr, tc), lambda i, j: (i, off_a // tc + j)), pl.BlockSpec((tr, tc), lambda i, j: (i, off_b // tc + j)), blk, blk, blk],
        out_specs=[blk, blk, blk, blk], out_shape=[out, out, out, out],
        compiler_params=_params("parallel", "parallel"))(z, z, ya, yb, dm)


_ATT_SCALE = (QK_NOPE + QK_ROPE) ** -0.5
_NEG = -1e30


def rope_k(z, off, cos4, sin4):
    s = z.shape[0]
    tr = _tile(s, ROW_TILE * 2)
    assert off % LANES == 0

    def body(k_ref, c_ref, s_ref, o_ref):
        k = k_ref[...]
        k = k + pltpu.roll(k, QK_ROPE, 1)
        o_ref[...] = _rope(k, c_ref[...], s_ref[...]).astype(o_ref.dtype)

    row = pl.BlockSpec((tr, LANES), lambda i: (i, 0))
    return pl.pallas_call(body, name="rope_k", grid=(s // tr,),
                          in_specs=[pl.BlockSpec((tr, LANES), lambda i: (i, off // LANES)), row, row], out_specs=row,
                          out_shape=jax.ShapeDtypeStruct((s, LANES), BF16), compiler_params=_params("parallel"))(z, cos4, sin4)


def _head_masks(shape):
    lane = lax.broadcasted_iota(jnp.int32, shape, 1)
    return lane < QK_ROPE, lane >= QK_ROPE


def _scores(qn, qp_h, k, kp, qi, kb, t):
    sc = lax.dot_general(qn, k, (((1,), (1,)), ((), ())), preferred_element_type=F32)
    sc += lax.dot_general(qp_h, kp, (((1,), (1,)), ((), ())), preferred_element_type=F32)
    sc = sc * _ATT_SCALE
    row = lax.broadcasted_iota(jnp.int32, sc.shape, 0) + qi * t
    col = lax.broadcasted_iota(jnp.int32, sc.shape, 1) + kb * t
    return jnp.where(col <= row, sc, _NEG)


def attn_fwd(qn, qp, kv, kpr, cos4, sin4):
    s = qn.shape[0]
    hp = HEADS // 2
    t = _tile(s, ATT_TILE)
    nq = s // t

    def body(qn_ref, qp_ref, kv_ref, kp_ref, c_ref, s_ref, o_ref, qpr_ref, l_ref):
        qi = pl.program_id(1)
        qpr = _rope(qp_ref[...], c_ref[...], s_ref[...]).astype(BF16)
        qpr_ref[...] = qpr
        masks = _head_masks(qpr.shape)
        for hh in range(2):
            q_n = qn_ref[:, hh * QK_NOPE:(hh + 1) * QK_NOPE]
            q_p = jnp.where(masks[hh], qpr, jnp.zeros_like(qpr))
            kc, vc = 2 * hh * QK_NOPE, (2 * hh + 1) * QK_NOPE

            def step(kb, carry):
                m, l, acc = carry
                rows = pl.ds(pl.multiple_of(kb * t, t), t)
                sc = _scores(q_n, q_p, kv_ref[rows, kc:kc + QK_NOPE], kp_ref[rows, :], qi, kb, t)
                m_new = jnp.maximum(m, jnp.max(sc, axis=-1, keepdims=True))
                alpha = jnp.exp(m - m_new)
                p = jnp.exp(sc - m_new)
                l = alpha * l + jnp.sum(p, axis=-1, keepdims=True)
                acc = alpha * acc + jnp.dot(p.astype(BF16), kv_ref[rows, vc:vc + V_HEAD], preferred_element_type=F32)
                return m_new, l, acc

            init = (jnp.full((t, 1), _NEG, F32), jnp.zeros((t, 1), F32), jnp.zeros((t, V_HEAD), F32))
            m, l, acc = lax.fori_loop(0, qi + 1, step, init)
            o_ref[:, hh * V_HEAD:(hh + 1) * V_HEAD] = acc / l
            l_ref[:, hh:hh + 1] = m + jnp.log(l)

    return pl.pallas_call(
        body, name="attn_fwd", grid=(hp, nq),
        in_specs=[pl.BlockSpec((t, 2 * QK_NOPE), lambda h, i: (i, h)), pl.BlockSpec((t, LANES), lambda h, i: (i, h)),
                  pl.BlockSpec((s, 4 * QK_NOPE), lambda h, i: (0, h)), _full((s, LANES)),
                  pl.BlockSpec((t, LANES), lambda h, i: (i, 0)), pl.BlockSpec((t, LANES), lambda h, i: (i, 0))],
        out_specs=[pl.BlockSpec((t, 2 * V_HEAD), lambda h, i: (i, h)), pl.BlockSpec((t, LANES), lambda h, i: (i, h)),
                   pl.BlockSpec((None, t, 2), lambda h, i: (h, i, 0))],
        out_shape=[jax.ShapeDtypeStruct((s, HEADS * V_HEAD), F32), jax.ShapeDtypeStruct((s, HEADS * QK_ROPE), BF16),
                   jax.ShapeDtypeStruct((hp, s, 2), F32)],
        compiler_params=_params("parallel", "parallel"))(qn, qp, kv, kpr, cos4, sin4)


def attn_bwd_q(qn, qpr, kv, kpr, o, do, lse, cos4, sin4):
    s = qn.shape[0]
    hp = HEADS // 2
    t = _tile(s, ATT_TILE)
    nq = s // t

    def body(qn_ref, qpr_ref, kv_ref, kp_ref, o_ref, do_ref, l_ref, c_ref, s_ref, dqn_ref, dqp_ref):
        qi = pl.program_id(1)
        qpr = qpr_ref[...]
        masks = _head_masks(qpr.shape)
        dqp = jnp.zeros(qpr.shape, F32)
        for hh in range(2):
            q_n = qn_ref[:, hh * QK_NOPE:(hh + 1) * QK_NOPE]
            q_p = jnp.where(masks[hh], qpr, jnp.zeros_like(qpr))
            kc, vc = 2 * hh * QK_NOPE, (2 * hh + 1) * QK_NOPE
            do_h = do_ref[:, hh * V_HEAD:(hh + 1) * V_HEAD]
            delta = jnp.sum(do_h * o_ref[:, hh * V_HEAD:(hh + 1) * V_HEAD], axis=-1, keepdims=True)
            do_b = do_h.astype(BF16)
            lse_h = l_ref[:, hh:hh + 1]

            def step(kb, carry):
                dn, dp_ = carry
                rows = pl.ds(pl.multiple_of(kb * t, t), t)
                k = kv_ref[rows, kc:kc + QK_NOPE]
                kp = kp_ref[rows, :]
                p = jnp.exp(_scores(q_n, q_p, k, kp, qi, kb, t) - lse_h)
                dpv = lax.dot_general(do_b, kv_ref[rows, vc:vc + V_HEAD], (((1,), (1,)), ((), ())), preferred_element_type=F32)
                ds = (p * (dpv - delta) * _ATT_SCALE).astype(BF16)
                dn = dn + jnp.dot(ds, k, preferred_element_type=F32)
                dp_ = dp_ + jnp.dot(ds, kp, preferred_element_type=F32)
                return dn, dp_

            dn, dp_h = lax.fori_loop(0, qi + 1, step, (jnp.zeros((t, QK_NOPE), F32), jnp.zeros((t, LANES), F32)))
            dqn_ref[:, hh * QK_NOPE:(hh + 1) * QK_NOPE] = dn.astype(dqn_ref.dtype)
            dqp = dqp + jnp.where(masks[hh], dp_h, jnp.zeros_like(dp_h))
        dqp_ref[...] = _rope(dqp, c_ref[...], -s_ref[...]).astype(dqp_ref.dtype)

    qblk = pl.BlockSpec((t, 2 * QK_NOPE), lambda h, i: (i, h))
    pblk = pl.BlockSpec((t, LANES), lambda h, i: (i, h))
    tab = pl.BlockSpec((t, LANES), lambda h, i: (i, 0))
    return pl.pallas_call(
        body, name="attn_bwd_q", grid=(hp, nq),
        in_specs=[qblk, pblk, pl.BlockSpec((s, 4 * QK_NOPE), lambda h, i: (0, h)), _full((s, LANES)), qblk, qblk,
                  pl.BlockSpec((None, t, 2), lambda h, i: (h, i, 0)), tab, tab],
        out_specs=[qblk, pblk],
        out_shape=[jax.ShapeDtypeStruct((s, HEADS * QK_NOPE), BF16), jax.ShapeDtypeStruct((s, HEADS * QK_ROPE), BF16)],
        compiler_params=_params("parallel", "parallel"))(qn, qpr, kv, kpr, o, do, lse, cos4, sin4)


def attn_bwd_kv(qn, qpr, kv, kpr, o, do, lse):
    s = qn.shape[0]
    hp = HEADS // 2
    t = _tile(s, ATT_TILE)
    nq = s // t

    def body(qn_ref, qpr_ref, kv_ref, kp_ref, o_ref, do_ref, l_ref, dkv_ref, dkp_ref):
        ki = pl.program_id(1)
        rows_k = pl.ds(pl.multiple_of(ki * t, t), t)
        kp = kp_ref[rows_k, :]
        dkp = jnp.zeros((t, LANES), F32)
        for hh in range(2):
            kc, vc = 2 * hh * QK_NOPE, (2 * hh + 1) * QK_NOPE
            k = kv_ref[rows_k, kc:kc + QK_NOPE]
            v = kv_ref[rows_k, vc:vc + V_HEAD]

            def step(qb, carry):
                dk, dv, dkp_h = carry
                rows = pl.ds(pl.multiple_of(qb * t, t), t)
                q_n = qn_ref[rows, hh * QK_NOPE:(hh + 1) * QK_NOPE]
                qpr = qpr_ref[rows, :]
                lane = lax.broadcasted_iota(jnp.int32, qpr.shape, 1)
                sel = (lane < QK_ROPE) if hh == 0 else (lane >= QK_ROPE)
                q_p = jnp.where(sel, qpr, jnp.zeros_like(qpr))
                do_h = do_ref[rows, hh * V_HEAD:(hh + 1) * V_HEAD]
                delta = jnp.sum(do_h * o_ref[rows, hh * V_HEAD:(hh + 1) * V_HEAD], axis=-1, keepdims=True)
                do_b = do_h.astype(BF16)
                p = jnp.exp(_scores(q_n, q_p, k, kp, qb, ki, t) - l_ref[rows, hh:hh + 1])
                dpv = lax.dot_general(do_b, v, (((1,), (1,)), ((), ())), preferred_element_type=F32)
                ds = (p * (dpv - delta) * _ATT_SCALE).astype(BF16)
                dv = dv + lax.dot_general(p.astype(BF16), do_b, (((0,), (0,)), ((), ())), preferred_element_type=F32)
                dk = dk + lax.dot_general(ds, q_n, (((0,), (0,)), ((), ())), preferred_element_type=F32)
                dkp_h = dkp_h + lax.dot_general(ds, q_p, (((0,), (0,)), ((), ())), preferred_element_type=F32)
                return dk, dv, dkp_h

            init = (jnp.zeros((t, QK_NOPE), F32), jnp.zeros((t, V_HEAD), F32), jnp.zeros((t, LANES), F32))
            dk, dv, dkp_h = lax.fori_loop(ki, nq, step, init)
            dkv_ref[:, kc:kc + QK_NOPE] = dk.astype(dkv_ref.dtype)
            dkv_ref[:, vc:vc + V_HEAD] = dv.astype(dkv_ref.dtype)
            dkp = dkp + dkp_h
        dkp_ref[...] = dkp

    return pl.pallas_call(
        body, name="attn_bwd_kv", grid=(hp, nq),
        in_specs=[pl.BlockSpec((s, 2 * QK_NOPE), lambda h, i: (0, h)), pl.BlockSpec((s, LANES), lambda h, i: (0, h)),
                  pl.BlockSpec((s, 4 * QK_NOPE), lambda h, i: (0, h)), _full((s, LANES)),
                  pl.BlockSpec((s, 2 * V_HEAD), lambda h, i: (0, h)), pl.BlockSpec((s, 2 * V_HEAD), lambda h, i: (0, h)),
                  pl.BlockSpec((None, s, 2), lambda h, i: (h, 0, 0))],
        out_specs=[pl.BlockSpec((t, 4 * QK_NOPE), lambda h, i: (i, h)), pl.BlockSpec((None, t, LANES), lambda h, i: (h, i, 0))],
        out_shape=[jax.ShapeDtypeStruct((s, HEADS * 2 * QK_NOPE), BF16), jax.ShapeDtypeStruct((hp, s, LANES), F32)],
        compiler_params=_params("parallel", "parallel"))(qn, qpr, kv, kpr, o, do, lse)


def kpe_bwd(dkp, cos4, sin4, pad_cols):
    hp, s, _ = dkp.shape
    tr = _tile(s, ROW_TILE * 2)

    def body(d_ref, c_ref, s_ref, o_ref):
        tot = d_ref[0]
        for h in range(1, hp):
            tot = tot + d_ref[h]
        tot = tot + pltpu.roll(tot, QK_ROPE, 1)
        lane = lax.broadcasted_iota(jnp.int32, tot.shape, 1)
        dk = jnp.where(lane < QK_ROPE, _rope(tot, c_ref[...], -s_ref[...]), jnp.zeros_like(tot))
        o_ref[...] = jnp.zeros_like(o_ref)
        o_ref[:, 0:LANES] = dk.astype(o_ref.dtype)

    row = pl.BlockSpec((tr, LANES), lambda i: (i, 0))
    return pl.pallas_call(body, name="kpe_bwd", grid=(s // tr,),
                          in_specs=[pl.BlockSpec((hp, tr, LANES), lambda i: (0, i, 0)), row, row],
                          out_specs=pl.BlockSpec((tr, pad_cols), lambda i: (i, 0)),
                          out_shape=jax.ShapeDtypeStruct((s, pad_cols), BF16), compiler_params=_params("parallel"))(dkp, cos4, sin4)


def _shift_down(x, n):
    row = lax.broadcasted_iota(jnp.int32, x.shape, 0)
    return jnp.where(row >= n, pltpu.roll(x, n, 0), jnp.zeros_like(x))


def _shift_up(x, n):
    rows = x.shape[0]
    row = lax.broadcasted_iota(jnp.int32, x.shape, 0)
    return jnp.where(row < rows - n, pltpu.roll(x, rows - n, 0), jnp.zeros_like(x))


def _conv(x, w_ref, b_ref):
    return w_ref[2:3, :] * x + w_ref[1:2, :] * _shift_down(x, 1) + w_ref[0:1, :] * _shift_down(x, 2) + b_ref[...]


def conv_act_fwd(upre, conv_w, conv_b):
    s, f2 = upre.shape
    f = f2 // 2
    tc = _tile(f, COL_TILE)
    nc = f // tc

    def body(ug_ref, uv_ref, wg_ref, wv_ref, bg_ref, bv_ref, o_ref):
        gh = _conv(ug_ref[...], wg_ref, bg_ref)
        vh = _conv(uv_ref[...], wv_ref, bv_ref)
        o_ref[...] = (gh * _sigmoid(gh) * vh).astype(o_ref.dtype)

    def spec(rows, shift):
        return pl.BlockSpec((rows, tc), lambda j: (0, j + shift))

    return pl.pallas_call(
        body, name="conv_act_fwd", grid=(nc,),
        in_specs=[spec(s, 0), spec(s, nc), spec(3, 0), spec(3, nc), spec(1, 0), spec(1, nc)], out_specs=spec(s, 0),
        out_shape=jax.ShapeDtypeStruct((s, f), BF16), compiler_params=_params("parallel"))(upre, upre, conv_w, conv_w, conv_b, conv_b)


def conv_act_bwd(upre, conv_w, conv_b, df):
    s, f2 = upre.shape
    f = f2 // 2
    tc = _tile(f, COL_TILE)
    nc = f // tc

    def half(x, d, w_ref, du_ref, gw_ref, gb_ref):
        gb_ref[...] = _colsum(d)
        gw_ref[2:3, :] = _colsum(d * x)
        gw_ref[1:2, :] = _colsum(d * _shift_down(x, 1))
        gw_ref[0:1, :] = _colsum(d * _shift_down(x, 2))
        du_ref[...] = (w_ref[2:3, :] * d + w_ref[1:2, :] * _shift_up(d, 1) + w_ref[0:1, :] * _shift_up(d, 2)).astype(du_ref.dtype)

    def body(ug_ref, uv_ref, wg_ref, wv_ref, bg_ref, bv_ref, df_ref, dug_ref, duv_ref, gwg_ref, gwv_ref, gbg_ref, gbv_ref):
        xg, xv = ug_ref[...], uv_ref[...]
        gh = _conv(xg, wg_ref, bg_ref)
        vh = _conv(xv, wv_ref, bv_ref)
        sg = _sigmoid(gh)
        df_v = df_ref[...]
        half(xg, df_v * vh * (sg * (1.0 + gh * (1.0 - sg))), wg_ref, dug_ref, gwg_ref, gbg_ref)
        half(xv, df_v * (gh * sg), wv_ref, duv_ref, gwv_ref, gbv_ref)

    def spec(rows, shift):
        return pl.BlockSpec((rows, tc), lambda j: (0, j + shift))

    act = jax.ShapeDtypeStruct((s, f), BF16)
    gw = jax.ShapeDtypeStruct((3, f), F32)
    gb = jax.ShapeDtypeStruct((1, f), F32)
    return pl.pallas_call(
        body, name="conv_act_bwd", grid=(nc,),
        in_specs=[spec(s, 0), spec(s, nc), spec(3, 0), spec(3, nc), spec(1, 0), spec(1, nc), spec(s, 0)],
        out_specs=[spec(s, 0), spec(s, 0), spec(3, 0), spec(3, 0), spec(1, 0), spec(1, 0)],
        out_shape=[act, act, gw, gw, gb, gb],
        compiler_params=_params("parallel"))(upre, upre, conv_w, conv_w, conv_b, conv_b, df)


def adamw(name, w, m, v, parts):
    npart, r, c = parts.shape
    tr = r
    if r % 8 == 0:
        tr = max(8, min(r, ADAMW_TILE_ELEMS // c) // 8 * 8)
        while r % tr:
            tr -= 8
    bc1 = 1.0 - ADAM_B1 ** ADAM_STEP
    bc2 = 1.0 - ADAM_B2 ** ADAM_STEP

    def body(w_ref, m_ref, v_ref, p_ref, g_ref, d_ref, nm_ref, nv_ref):
        g = p_ref[0].astype(F32)
        for k in range(1, npart):
            g = g + p_ref[k].astype(F32)
        m_new = ADAM_B1 * m_ref[...] + (1.0 - ADAM_B1) * g
        v_new = ADAM_B2 * v_ref[...] + (1.0 - ADAM_B2) * (g * g)
        g_ref[...] = g
        nm_ref[...] = m_new
        nv_ref[...] = v_new
        d_ref[...] = -ADAM_LR * ((m_new / bc1) / (jnp.sqrt(v_new / bc2) + ADAM_EPS) + ADAM_WD * w_ref[...])

    blk = pl.BlockSpec((tr, c), lambda i: (i, 0))
    out = jax.ShapeDtypeStruct((r, c), F32)
    return pl.pallas_call(
        body, name=name, grid=(r // tr,), in_specs=[blk, blk, blk, pl.BlockSpec((npart, tr, c), lambda i: (0, i, 0))],
        out_specs=[blk, blk, blk, blk], out_shape=[out, out, out, out], compiler_params=_params("parallel"))(w, m, v, parts)


def _position():
    return lax.axis_index("x"), lax.axis_index("y"), lax.axis_index("c")


def _index(p):
    return 4 * p[0] + 2 * p[1] + p[2]


def _peer(me, r):
    return (me[0] ^ ((r >> 2) & 1), me[1] ^ ((r >> 1) & 1), me[2] ^ (r & 1))


_ANY = pl.BlockSpec(memory_space=pl.ANY)


def all_gather_two_level(shards):
    n = len(shards)

    def body(*refs):
        ins, outs = refs[:n], refs[n:2 * n]
        send_sems, recv_sems, local_sems = refs[2 * n:]
        x, y, c = _position()
        me, sibling = (x, y, c), (x, y, 1 - c)
        chips = [(1 - x, y), (x, 1 - y), (1 - x, 1 - y)]

        def copy(w, k, block, to, src=None):
            slot = outs[w].at[_index(block)]
            return pltpu.make_async_remote_copy(src_ref=slot if src is None else src, dst_ref=slot,
                                                send_sem=send_sems.at[7 * w + k], recv_sem=recv_sems.at[7 * w + k],
                                                device_id=to, device_id_type=MESH)

        mine = [pltpu.make_async_copy(ins[w], outs[w].at[_index(me)], local_sems.at[w]) for w in range(n)]
        for cp in mine:
            cp.start()
        first = []
        for w in range(n):
            first.append(copy(w, 0, me, sibling, src=ins[w]))
            first += [copy(w, 1 + j, me, (*chip, c), src=ins[w]) for j, chip in enumerate(chips)]
        for cp in first:
            cp.start()
        passed = []
        for w in range(n):
            for j, chip in enumerate(chips):
                copy(w, 1 + j, (*chip, c), me).wait_recv()
                cp = copy(w, 4 + j, (*chip, c), sibling)
                cp.start()
                passed.append(cp)
        for w in range(n):
            copy(w, 0, sibling, me).wait_recv()
            for j, chip in enumerate(chips):
                copy(w, 4 + j, (*chip, 1 - c), me).wait_recv()
        for cp in first + passed:
            cp.wait_send()
        for cp in mine:
            cp.wait()

    return pl.pallas_call(
        body, name="all_gather_weights",
        out_shape=[jax.ShapeDtypeStruct((N_DEV,) + a.shape, a.dtype) for a in shards],
        in_specs=[_ANY] * n, out_specs=[_ANY] * n,
        scratch_shapes=[pltpu.SemaphoreType.DMA((7 * n,)), pltpu.SemaphoreType.DMA((7 * n,)), pltpu.SemaphoreType.DMA((n,))],
        )(*shards)


def exchange(name, arrays, scatter):
    n = len(arrays)

    def body(*refs):
        ins, outs = refs[:n], refs[n:2 * n]
        send_sems, recv_sems, local_sems = refs[2 * n:]
        me = _position()
        copies = []
        for w in range(n):
            src = ins[w].at[_index(me)] if scatter else ins[w]
            cp = pltpu.make_async_copy(src, outs[w].at[_index(me)], local_sems.at[w])
            cp.start()
            copies.append(cp)
        remote = []
        for w in range(n):
            for r in range(1, N_DEV):
                peer = _peer(me, r)
                src = ins[w].at[_index(peer)] if scatter else ins[w]
                cp = pltpu.make_async_remote_copy(src_ref=src, dst_ref=outs[w].at[_index(me)],
                                                  send_sem=send_sems.at[7 * w + r - 1], recv_sem=recv_sems.at[7 * w + r - 1],
                                                  device_id=peer, device_id_type=MESH)
                cp.start()
                remote.append(cp)
        for cp in remote:
            cp.wait()
        for cp in copies:
            cp.wait()

    blocks = [a.shape[1:] if scatter else a.shape for a in arrays]
    return pl.pallas_call(
        body, name=name,
        out_shape=[jax.ShapeDtypeStruct((N_DEV,) + b, a.dtype) for a, b in zip(arrays, blocks)],
        in_specs=[_ANY] * n, out_specs=[_ANY] * n,
        scratch_shapes=[pltpu.SemaphoreType.DMA((7 * n,)), pltpu.SemaphoreType.DMA((7 * n,)), pltpu.SemaphoreType.DMA((n,))],
        )(*arrays)


def ada_fwd(c, w_ada, b_ada3):
    d, cs = w_ada.shape

    def body(c_ref, w_ref, b_ref, mod_ref, sc_ref, part_ref, send_sems, recv_sems):
        me = _position()
        my = _index(me)
        cv = c_ref[...]
        sc_ref[my] = cv * _sigmoid(cv)
        gather = []
        for r in range(1, N_DEV):
            cp = pltpu.make_async_remote_copy(src_ref=sc_ref.at[my], dst_ref=sc_ref.at[my], send_sem=send_sems.at[r - 1],
                                              recv_sem=recv_sems.at[r - 1], device_id=_peer(me, r), device_id_type=MESH)
            cp.start()
            gather.append(cp)
        for cp in gather:
            cp.wait()
        sc_all = jnp.concatenate([sc_ref[k] for k in range(N_DEV)], axis=0).astype(BF16)
        part = jnp.dot(sc_all, w_ref[...].astype(BF16), preferred_element_type=F32)
        for k in range(N_DEV):
            part_ref[k] = part[k:k + 1, :]
        scatter = []
        for r in range(1, N_DEV):
            peer = _peer(me, r)
            cp = pltpu.make_async_remote_copy(src_ref=part_ref.at[_index(peer)], dst_ref=mod_ref.at[my], send_sem=send_sems.at[6 + r],
                                              recv_sem=recv_sems.at[6 + r], device_id=peer, device_id_type=MESH)
            cp.start()
            scatter.append(cp)
        mod_ref[my] = part_ref[my]
        for cp in scatter:
            cp.wait()
        mod_ref[...] = mod_ref[...] + b_ref[...]

    vm = pl.BlockSpec(memory_space=pltpu.VMEM)
    return pl.pallas_call(
        body, name="ada_fwd",
        out_shape=[jax.ShapeDtypeStruct((N_DEV, 1, cs), F32), jax.ShapeDtypeStruct((N_DEV, 1, d), F32)],
        in_specs=[vm, vm, vm], out_specs=[vm, vm],
        scratch_shapes=[pltpu.VMEM((N_DEV, 1, cs), F32), pltpu.SemaphoreType.DMA((14,)), pltpu.SemaphoreType.DMA((14,))],
        compiler_params=pltpu.CompilerParams(vmem_limit_bytes=VMEM_LIMIT_BYTES))(c, w_ada, b_ada3)


def ada_bwd_w(sc_all, dmod_cols):
    _, d = sc_all.shape
    cs = dmod_cols.shape[1]
    tr = _tile(d, ROW_TILE)

    def body(sc_ref, dm_ref, o_ref):
        dm = dm_ref[...].astype(BF16)
        o_ref[...] = lax.dot_general(sc_ref[...].astype(BF16), dm, (((0,), (0,)), ((), ())), preferred_element_type=F32)

    return pl.pallas_call(body, name="ada_bwd_w", grid=(d // tr,),
                          in_specs=[pl.BlockSpec((N_DEV, tr), lambda i: (0, i)), _full((N_DEV, cs))],
                          out_specs=pl.BlockSpec((None, tr, cs), lambda i: (0, i, 0)),
                          out_shape=jax.ShapeDtypeStruct((1, d, cs), F32), compiler_params=_params("parallel"))(sc_all, dmod_cols)


def _round_up(n, m):
    return (n + m - 1) // m * m


def kernel(x, c, positions, w_ada, b_ada, pre_norm1_g, w_in, gm_ln_g, gm_ln_b, gm_w_s, gm_b_s, w_branch_a, q_norm_g, w_uq, kv_norm_g, w_ukv, w_branch_b, w_out, post_norm1_g, pre_norm2_g, w_up, conv_w, conv_b, w_down, post_norm2_g, loss_target, m_w_ada, m_b_ada, m_pre_norm1_g, m_w_in, m_gm_ln_g, m_gm_ln_b, m_gm_w_s, m_gm_b_s, m_w_branch_a, m_q_norm_g, m_w_uq, m_kv_norm_g, m_w_ukv, m_w_branch_b, m_w_out, m_post_norm1_g, m_pre_norm2_g, m_w_up, m_conv_w, m_conv_b, m_w_down, m_post_norm2_g, v_w_ada, v_b_ada, v_pre_norm1_g, v_w_in, v_gm_ln_g, v_gm_ln_b, v_gm_w_s, v_gm_b_s, v_w_branch_a, v_q_norm_g, v_w_uq, v_kv_norm_g, v_w_ukv, v_w_branch_b, v_w_out, v_post_norm1_g, v_pre_norm2_g, v_w_up, v_conv_w, v_conv_b, v_w_down, v_post_norm2_g):
    weights = dict(w_ada=w_ada, b_ada=b_ada, pre_norm1_g=pre_norm1_g, w_in=w_in, gm_ln_g=gm_ln_g, gm_ln_b=gm_ln_b, gm_w_s=gm_w_s,
                   gm_b_s=gm_b_s, w_branch_a=w_branch_a, q_norm_g=q_norm_g, w_uq=w_uq, kv_norm_g=kv_norm_g, w_ukv=w_ukv,
                   w_branch_b=w_branch_b, w_out=w_out, post_norm1_g=post_norm1_g, pre_norm2_g=pre_norm2_g, w_up=w_up, conv_w=conv_w,
                   conv_b=conv_b, w_down=w_down, post_norm2_g=post_norm2_g)
    mom1 = dict(w_ada=m_w_ada, b_ada=m_b_ada, pre_norm1_g=m_pre_norm1_g, w_in=m_w_in, gm_ln_g=m_gm_ln_g, gm_ln_b=m_gm_ln_b,
                gm_w_s=m_gm_w_s, gm_b_s=m_gm_b_s, w_branch_a=m_w_branch_a, q_norm_g=m_q_norm_g, w_uq=m_w_uq, kv_norm_g=m_kv_norm_g,
                w_ukv=m_w_ukv, w_branch_b=m_w_branch_b, w_out=m_w_out, post_norm1_g=m_post_norm1_g, pre_norm2_g=m_pre_norm2_g,
                w_up=m_w_up, conv_w=m_conv_w, conv_b=m_conv_b, w_down=m_w_down, post_norm2_g=m_post_norm2_g)
    mom2 = dict(w_ada=v_w_ada, b_ada=v_b_ada, pre_norm1_g=v_pre_norm1_g, w_in=v_w_in, gm_ln_g=v_gm_ln_g, gm_ln_b=v_gm_ln_b,
                gm_w_s=v_gm_w_s, gm_b_s=v_gm_b_s, w_branch_a=v_w_branch_a, q_norm_g=v_q_norm_g, w_uq=v_w_uq, kv_norm_g=v_kv_norm_g,
                w_ukv=v_w_ukv, w_branch_b=v_w_branch_b, w_out=v_w_out, post_norm1_g=v_post_norm1_g, pre_norm2_g=v_pre_norm2_g,
                w_up=v_w_up, conv_w=v_conv_w, conv_b=v_conv_b, w_down=v_w_down, post_norm2_g=v_post_norm2_g)
    order = list(weights)

    s, d = x.shape[1], x.shape[2]
    gmw = gm_ln_g.shape[0]
    groups = gmw // CHUNK
    ql, kvl = q_norm_g.shape[0], kv_norm_g.shape[0]
    f2 = conv_b.shape[0]
    in_cols = w_in.shape[1] * N_DEV
    o_q, o_kv, o_ga, o_gb, o_kpe = 2 * gmw, 2 * gmw + ql, 2 * gmw + ql + kvl, 2 * gmw + ql + kvl + d, 2 * gmw + ql + kvl + 2 * d
    zp = _round_up(o_kpe + LANES, Z_PAD)
    src_kpe = 2 * gmw + ql + kvl
    assert src_kpe + QK_ROPE + 2 * d == in_cols
    my = 4 * lax.axis_index("x") + 2 * lax.axis_index("y") + lax.axis_index("c")

    x2, tgt = x[0], loss_target[0]
    row = lambda a: a.reshape(1, -1)

    mod8, sc_all3 = ada_fwd(c, w_ada, b_ada.reshape(N_DEV, 1, -1))
    mod = mod8.reshape(N_MOD, d)
    shift1, scale1, gate1, shift2, scale2, gate2 = (mod[i:i + 1] for i in range(N_MOD))
    sc_all = sc_all3.reshape(N_DEV, d)

    big = ["w_in", "w_branch_a", "w_uq", "w_ukv", "w_branch_b", "w_out", "w_up", "w_down"]
    g_in, g_a, g_uq, g_ukv, g_b, g_out, g_up, g_down, g_cw = all_gather_two_level([weights[k].astype(BF16) for k in big] + [conv_w])
    w_in_f = g_in.transpose(1, 0, 2).reshape(d, in_cols)
    w_in_p = jnp.concatenate([w_in_f[:, :src_kpe], w_in_f[:, src_kpe + QK_ROPE:], w_in_f[:, src_kpe:src_kpe + QK_ROPE],
                              jnp.zeros((d, zp - in_cols), BF16)], axis=1)
    w_a_f, w_b_f, w_out_f = g_a.reshape(-1, d), g_b.reshape(-1, d), g_out.reshape(-1, d)
    w_down_f = g_down.reshape(-1, d)
    w_uq_f = g_uq.transpose(1, 0, 2).reshape(ql, HEADS, QK_NOPE + QK_ROPE)
    w_uq_n = w_uq_f[:, :, :QK_NOPE].reshape(ql, HEADS * QK_NOPE)
    w_uq_r = w_uq_f[:, :, QK_NOPE:].reshape(ql, HEADS * QK_ROPE)

    inv = ROPE_THETA ** (-jnp.arange(0, QK_ROPE, 2, dtype=F32) / QK_ROPE)
    ang = positions[0].astype(F32)[:, None] * inv
    cos4 = jnp.tile(jnp.cos(ang), (1, 4))
    sin4 = jnp.tile(jnp.concatenate([-jnp.sin(ang), jnp.sin(ang)], axis=1), (1, 2))

    wm = (gm_w_s * jnp.tril(jnp.ones((CHUNK, CHUNK), F32))).astype(BF16)
    bs3 = gm_b_s.reshape(groups, CHUNK, 1)
    ln_g, ln_b = row(gm_ln_g), row(gm_ln_b)

    h1 = norm_mod_fwd("pre1_fwd", x2, row(pre_norm1_g), scale1, shift1)
    z = mm_nn("z_proj", h1, w_in_p, F32)
    a = gmlp_fwd(z, gmw, ln_g, ln_b, wm, bs3)
    y_a = mm_nn("branch_a", a, w_a_f, F32)
    qln = rms_fwd_cols("q_norm", z, o_q, ql, row(q_norm_g))
    kvn = rms_fwd_cols("kv_norm", z, o_kv, kvl, row(kv_norm_g))
    qn = mm_nn("q_nope", qln, w_uq_n, BF16)
    qp = mm_nn("q_rope", qln, w_uq_r, F32)
    kv = mm_nn_b3("kv_up", kvn, g_ukv, BF16)
    kpr = rope_k(z, o_kpe, cos4, sin4)
    o, qpr, lse = attn_fwd(qn, qp, kv, kpr, cos4, sin4)
    y_b = mm_nn("branch_b", o, w_b_f, F32)
    merged = merge_fwd(z, o_ga, o_gb, y_a, y_b)
    y1 = mm_nn("out_proj", merged, w_out_f, F32)
    x1 = post_res_fwd("post1_fwd", x2, y1, gate1, row(post_norm1_g))
    h2 = norm_mod_fwd("pre2_fwd", x1, row(pre_norm2_g), scale2, shift2)
    upre = mm_nn_b3("up_proj", h2, g_up, F32)
    cw = g_cw.transpose(1, 0, 2).reshape(3, f2)
    cb = row(conv_b)
    f = conv_act_fwd(upre, cw, cb)
    ffn = mm_nn("down_proj", f, w_down_f, F32)
    loss_acc, dout, dffn, acc2 = post2_loss_bwd(x1, ffn, tgt, gate2, row(post_norm2_g))

    df = mm_nt("d_f", dffn, w_down_f, F32)
    gw_down = mm_tn("g_w_down", f, dffn, BF16)
    dup_g, dup_v, gcw_g, gcw_v, gcb_g, gcb_v = conv_act_bwd(upre, cw, cb, df)
    dupre = jnp.concatenate([dup_g, dup_v], axis=1)
    dh2 = mm_nt_b3("d_h2", dupre, g_up, F32)
    gw_up3 = mm_tn_o3("g_w_up", h2, dupre, N_DEV, BF16)
    dx1, dy1, acc_mid = mid_bwd(dh2, dout, x1, y1, row(pre_norm2_g), scale2, gate1, row(post_norm1_g))
    dmerged = mm_nt("d_merged", dy1, w_out_f, F32)
    gw_out = mm_tn("g_w_out", merged, dy1, BF16)
    dya, dyb, dga, dgb = merge_bwd(z, o_ga, o_gb, y_a, y_b, dmerged)
    da = mm_nt("d_a", dya, w_a_f, F32)
    gw_a = mm_tn("g_w_a", a, dya, BF16)
    do = mm_nt("d_o", dyb, w_b_f, F32)
    gw_b = mm_tn("g_w_b", o, dyb, BF16)
    duv, g_ws, g_bs3, acc_gm = gmlp_bwd(z, gmw, da, ln_g, ln_b, wm, bs3)
    dqn, dqp = attn_bwd_q(qn, qpr, kv, kpr, o, do, lse, cos4, sin4)
    dkv, dkp = attn_bwd_kv(qn, qpr, kv, kpr, o, do, lse)
    dkpe = kpe_bwd(dkp, cos4, sin4, zp - o_kpe)
    dq_cat = jnp.concatenate([dqn, dqp], axis=1)
    w_uq_cat = jnp.concatenate([w_uq_n, w_uq_r], axis=1)
    gw_uq_cat = mm_tn("g_w_uq", qln, dq_cat, BF16)
    dqln = mm_nt("d_qln", dq_cat, w_uq_cat, F32)
    dq_lat, g_qnorm = rms_bwd_cols("q_norm_bwd", dqln, z, o_q, ql, row(q_norm_g))
    gw_ukv3 = mm_tn_o3("g_w_ukv", kvn, dkv, N_DEV, BF16)
    dkvn = mm_nt_b3("d_kvn", dkv, g_ukv, F32)
    dkv_lat, g_kvnorm = rms_bwd_cols("kv_norm_bwd", dkvn, z, o_kv, kvl, row(kv_norm_g))
    dz = jnp.concatenate([duv, dq_lat, dkv_lat, dga, dgb, dkpe], axis=1)
    dh1 = mm_nt("d_h1", dz, w_in_p, F32)
    gw_in_p = mm_tn("g_w_in", h1, dz, BF16)
    grad_x, acc1 = pre1_bwd(dh1, dx1, x2, row(pre_norm1_g), scale1)

    gw_in_f = jnp.concatenate([gw_in_p[:, :src_kpe], gw_in_p[:, o_kpe:o_kpe + QK_ROPE], gw_in_p[:, src_kpe:o_kpe]], axis=1)
    gw_in3 = gw_in_f.reshape(d, N_DEV, -1).transpose(1, 0, 2)
    gw_uq_f = jnp.concatenate([gw_uq_cat[:, :HEADS * QK_NOPE].reshape(ql, HEADS, QK_NOPE),
                               gw_uq_cat[:, HEADS * QK_NOPE:].reshape(ql, HEADS, QK_ROPE)], axis=2)
    gw_uq3 = gw_uq_f.reshape(ql, N_DEV, -1).transpose(1, 0, 2)
    blocks = lambda g: g.reshape(N_DEV, g.shape[0] // N_DEV, g.shape[1])
    parts = exchange("grad_exchange", [gw_in3, blocks(gw_a), gw_uq3, gw_ukv3, blocks(gw_b), blocks(gw_out), gw_up3, blocks(gw_down)],
                     scatter=True)

    dmod = jnp.concatenate([acc1[0], acc1[1], acc_mid[3], acc_mid[0], acc_mid[1], acc2[0]])
    small = [("pre_norm1_g", acc1[2]), ("gm_ln_g", acc_gm[0]), ("gm_ln_b", acc_gm[1]), ("gm_b_s", g_bs3.reshape(-1)),
             ("q_norm_g", g_qnorm[0]), ("kv_norm_g", g_kvnorm[0]), ("post_norm1_g", acc_mid[4]), ("pre_norm2_g", acc_mid[2]),
             ("conv_b", jnp.concatenate([gcb_g[0], gcb_v[0]])), ("post_norm2_g", acc2[1]), ("gm_w_s", g_ws.reshape(-1)),
             ("b_ada", dmod)]
    n_small = sum(v.shape[0] for _, v in small)
    n_cw = 3 * f2
    n_pack = _round_up(n_small + n_cw, PACK_ALIGN)
    tail = jnp.zeros((n_pack - n_small - n_cw,), F32)
    packed = jnp.concatenate([v for _, v in small] + [jnp.concatenate([gcw_g, gcw_v], axis=1).reshape(-1), tail])
    (gathered,) = exchange("small_gather", [packed.reshape(-1, LANES)], scatter=False)

    def pack(src):
        return jnp.concatenate([src[k].reshape(-1) for k, _ in small] + [jnp.zeros((n_pack - n_small,), F32)]).reshape(-1, LANES)

    sm = [t.reshape(-1) for t in adamw("adamw_small", pack(weights), pack(mom1), pack(mom2), gathered)]
    res = {}
    off = 0
    for k, v in small:
        res[k] = tuple(t[off:off + v.shape[0]].reshape(weights[k].shape) for t in sm)
        off += v.shape[0]

    cs_cw = conv_w.shape[1]
    g_cw_full = sm[0][n_small:n_small + n_cw].reshape(3, f2)
    g_cw_mine = lax.dynamic_slice(g_cw_full, (0, my * cs_cw), (3, cs_cw))
    res["conv_w"] = adamw("adamw_conv_w", conv_w, mom1["conv_w"], mom2["conv_w"], g_cw_mine[None])

    cs_ada = w_ada.shape[1]
    off_b = n_small - N_MOD * d
    dmod_all = gathered.reshape(N_DEV, -1)[:, off_b:off_b + N_MOD * d]
    dmod_cols = lax.dynamic_slice(dmod_all, (0, my * cs_ada), (N_DEV, cs_ada))
    res["w_ada"] = adamw("adamw_w_ada", w_ada, mom1["w_ada"], mom2["w_ada"], ada_bwd_w(sc_all, dmod_cols))

    for k, p in zip(big, parts):
        res[k] = adamw("adamw_" + k, weights[k], mom1[k], mom2[k], p)

    loss = lax.psum(loss_acc[0, 0], ("x", "y", "c"))
    outs = [loss, grad_x[None]]
    for i in range(4):
        outs += [res[k][i] for k in order]
    return tuple(outs)
```

```python
import functools

import jax
import jax.numpy as jnp
from jax import lax
from jax.experimental import pallas as pl
from jax.experimental.pallas import tpu as pltpu

F32 = jnp.float32
BF16 = jnp.bfloat16

N_DEV = 8
HEADS = 16
QK_NOPE = 128
QK_ROPE = 64
V_HEAD = 128
CHUNK = 128
ROPE_THETA = 10000.0
EPS = 1e-6
N_MOD = 6
ADAM_LR, ADAM_B1, ADAM_B2, ADAM_EPS, ADAM_WD, ADAM_STEP = 0.001, 0.9, 0.999, 1e-08, 0.01, 10

LANES = 128
VMEM_LIMIT_BYTES = 48 * 2 ** 20
ROW_TILE = 256
COL_TILE = 256
ATT_TILE = 256
Z_PAD = 512
ADAMW_TILE_ELEMS = 1 << 18
PACK_ALIGN = 8 * LANES
MESH = pl.DeviceIdType.MESH


def _params(*sem):
    return pltpu.CompilerParams(dimension_semantics=sem if sem else None, vmem_limit_bytes=VMEM_LIMIT_BYTES)


def _tile(dim, target):
    t = (min(dim, target) // LANES) * LANES
    while t >= LANES:
        if dim % t == 0:
            return t
        t -= LANES
    return dim


def _full(shape):
    nd = len(shape)
    return pl.BlockSpec(shape, lambda *_: (0,) * nd)


def _matmul(name, a, b, *, grid, a_spec, b_spec, o_spec, out_shape, contract, acc_shape):
    nk = grid[2]

    def body(a_ref, b_ref, o_ref, acc_ref):
        k = pl.program_id(2)

        @pl.when(k == 0)
        def _():
            acc_ref[...] = jnp.zeros_like(acc_ref)

        acc_ref[...] += lax.dot_general(a_ref[...].astype(BF16), b_ref[...].astype(BF16),
                                        (contract, ((), ())), preferred_element_type=F32)

        @pl.when(k == nk - 1)
        def _():
            o_ref[...] = acc_ref[...].astype(o_ref.dtype)

    return pl.pallas_call(
        body, name=name, grid=grid, in_specs=[a_spec, b_spec], out_specs=o_spec, out_shape=out_shape,
        scratch_shapes=[pltpu.VMEM(acc_shape, F32)],
        compiler_params=_params("parallel", "parallel", "arbitrary"))(a, b)


TM, TN, TK = 1024, 1024, 512


def mm_nn(name, a, b, dtype):
    (m, k), n = a.shape, b.shape[1]
    tm, tn, tk = _tile(m, TM), _tile(n, TN), _tile(k, TK)
    return _matmul(name, a, b, grid=(m // tm, n // tn, k // tk),
                   a_spec=pl.BlockSpec((tm, tk), lambda i, j, kk: (i, kk)),
                   b_spec=pl.BlockSpec((tk, tn), lambda i, j, kk: (kk, j)),
                   o_spec=pl.BlockSpec((tm, tn), lambda i, j, kk: (i, j)),
                   out_shape=jax.ShapeDtypeStruct((m, n), dtype), contract=((1,), (0,)), acc_shape=(tm, tn))


def mm_nn_b3(name, a, b3, dtype):
    (m, k), (nj, _, cs) = a.shape, b3.shape
    tm, tk = _tile(m, TM), _tile(k, TK)
    return _matmul(name, a, b3, grid=(m // tm, nj, k // tk),
                   a_spec=pl.BlockSpec((tm, tk), lambda i, j, kk: (i, kk)),
                   b_spec=pl.BlockSpec((None, tk, cs), lambda i, j, kk: (j, kk, 0)),
                   o_spec=pl.BlockSpec((tm, cs), lambda i, j, kk: (i, j)),
                   out_shape=jax.ShapeDtypeStruct((m, nj * cs), dtype), contract=((1,), (0,)), acc_shape=(tm, cs))


def mm_nt(name, a, b, dtype):
    (m, k), n = a.shape, b.shape[0]
    tm, tn, tk = _tile(m, TM), _tile(n, TN), _tile(k, TK)
    return _matmul(name, a, b, grid=(m // tm, n // tn, k // tk),
                   a_spec=pl.BlockSpec((tm, tk), lambda i, j, kk: (i, kk)),
                   b_spec=pl.BlockSpec((tn, tk), lambda i, j, kk: (j, kk)),
                   o_spec=pl.BlockSpec((tm, tn), lambda i, j, kk: (i, j)),
                   out_shape=jax.ShapeDtypeStruct((m, n), dtype), contract=((1,), (1,)), acc_shape=(tm, tn))


def mm_nt_b3(name, a, b3, dtype):
    m, (nj, n, cs) = a.shape[0], b3.shape
    tm, tn = _tile(m, TM), _tile(n, TN)
    return _matmul(name, a, b3, grid=(m // tm, n // tn, nj),
                   a_spec=pl.BlockSpec((tm, cs), lambda i, j, kk: (i, kk)),
                   b_spec=pl.BlockSpec((None, tn, cs), lambda i, j, kk: (kk, j, 0)),
                   o_spec=pl.BlockSpec((tm, tn), lambda i, j, kk: (i, j)),
                   out_shape=jax.ShapeDtypeStruct((m, n), dtype), contract=((1,), (1,)), acc_shape=(tm, tn))


def mm_tn(name, a, b, dtype):
    (k, m), n = a.shape, b.shape[1]
    tm, tn, tk = _tile(m, TM), _tile(n, TN), _tile(k, TK)
    return _matmul(name, a, b, grid=(m // tm, n // tn, k // tk),
                   a_spec=pl.BlockSpec((tk, tm), lambda i, j, kk: (kk, i)),
                   b_spec=pl.BlockSpec((tk, tn), lambda i, j, kk: (kk, j)),
                   o_spec=pl.BlockSpec((tm, tn), lambda i, j, kk: (i, j)),
                   out_shape=jax.ShapeDtypeStruct((m, n), dtype), contract=((0,), (0,)), acc_shape=(tm, tn))


def mm_tn_o3(name, a, b, nj, dtype):
    (k, m), n = a.shape, b.shape[1]
    cs = n // nj
    tm, tk = _tile(m, TM), _tile(k, TK)
    return _matmul(name, a, b, grid=(m // tm, nj, k // tk),
                   a_spec=pl.BlockSpec((tk, tm), lambda i, j, kk: (kk, i)),
                   b_spec=pl.BlockSpec((tk, cs), lambda i, j, kk: (kk, j)),
                   o_spec=pl.BlockSpec((None, tm, cs), lambda i, j, kk: (j, i, 0)),
                   out_shape=jax.ShapeDtypeStruct((nj, m, cs), dtype), contract=((0,), (0,)), acc_shape=(tm, cs))


_GELU_C = 0.7978845608028654
_GELU_A = 0.044715


def _gelu(x):
    return 0.5 * x * (1.0 + jnp.tanh(_GELU_C * (x + _GELU_A * x * x * x)))


def _gelu_and_grad(x):
    t = jnp.tanh(_GELU_C * (x + _GELU_A * x * x * x))
    y = 0.5 * x * (1.0 + t)
    dy = 0.5 * (1.0 + t) + 0.5 * x * (1.0 - t * t) * (_GELU_C * (1.0 + 3.0 * _GELU_A * x * x))
    return y, dy


def _sigmoid(x):
    return 1.0 / (1.0 + jnp.exp(-x))


def _rms_stats(x):
    inv = lax.rsqrt(jnp.mean(x * x, axis=-1, keepdims=True) + EPS)
    return inv, x * inv


def _rms_bwd(dyhat, yhat, inv):
    return inv * (dyhat - yhat * jnp.mean(dyhat * yhat, axis=-1, keepdims=True))


def _colsum(x):
    return jnp.sum(x, axis=0, keepdims=True)


def _rope(x, cos4, sin4):
    lane = lax.broadcasted_iota(jnp.int32, x.shape, x.ndim - 1)
    first_half = (lane % QK_ROPE) < (QK_ROPE // 2)
    partner = jnp.where(first_half, pltpu.roll(x, LANES - QK_ROPE // 2, x.ndim - 1), pltpu.roll(x, QK_ROPE // 2, x.ndim - 1))
    return x * cos4 + partner * sin4


def norm_mod_fwd(name, x, g, scale, shift):
    s, d = x.shape
    tr = _tile(s, ROW_TILE)

    def body(x_ref, g_ref, sc_ref, sh_ref, o_ref):
        _, xh = _rms_stats(x_ref[...])
        o_ref[...] = (xh * g_ref[...] * (1.0 + sc_ref[...]) + sh_ref[...]).astype(o_ref.dtype)

    row = pl.BlockSpec((tr, d), lambda i: (i, 0))
    vec = pl.BlockSpec((1, d), lambda i: (0, 0))
    return pl.pallas_call(body, name=name, grid=(s // tr,), in_specs=[row, vec, vec, vec], out_specs=row,
                          out_shape=jax.ShapeDtypeStruct((s, d), BF16), compiler_params=_params("parallel"))(x, g, scale, shift)


def rms_fwd_cols(name, z, off, width, g):
    s = z.shape[0]
    tr = _tile(s, ROW_TILE)
    assert off % width == 0

    def body(x_ref, g_ref, o_ref):
        _, xh = _rms_stats(x_ref[...])
        o_ref[...] = (xh * g_ref[...]).astype(o_ref.dtype)

    return pl.pallas_call(body, name=name, grid=(s // tr,),
                          in_specs=[pl.BlockSpec((tr, width), lambda i: (i, off // width)), pl.BlockSpec((1, width), lambda i: (0, 0))],
                          out_specs=pl.BlockSpec((tr, width), lambda i: (i, 0)),
                          out_shape=jax.ShapeDtypeStruct((s, width), BF16), compiler_params=_params("parallel"))(z, g)


def rms_bwd_cols(name, dy, z, off, width, g):
    s = z.shape[0]
    tr = _tile(s, ROW_TILE)

    def body(dy_ref, x_ref, g_ref, dx_ref, gg_ref):
        @pl.when(pl.program_id(0) == 0)
        def _():
            gg_ref[...] = jnp.zeros_like(gg_ref)

        inv, xh = _rms_stats(x_ref[...])
        dy_v = dy_ref[...]
        gg_ref[...] += _colsum(dy_v * xh)
        dx_ref[...] = _rms_bwd(dy_v * g_ref[...], xh, inv).astype(dx_ref.dtype)

    return pl.pallas_call(body, name=name, grid=(s // tr,),
                          in_specs=[pl.BlockSpec((tr, width), lambda i: (i, 0)), pl.BlockSpec((tr, width), lambda i: (i, off // width)),
                                    pl.BlockSpec((1, width), lambda i: (0, 0))],
                          out_specs=[pl.BlockSpec((tr, width), lambda i: (i, 0)), pl.BlockSpec((1, width), lambda i: (0, 0))],
                          out_shape=[jax.ShapeDtypeStruct((s, width), BF16), jax.ShapeDtypeStruct((1, width), F32)],
                          compiler_params=_params("arbitrary"))(dy, z, g)


def post_res_fwd(name, x, y, gate, g):
    s, d = x.shape
    tr = _tile(s, ROW_TILE)

    def body(x_ref, y_ref, gate_ref, g_ref, o_ref):
        _, yh = _rms_stats(y_ref[...])
        o_ref[...] = x_ref[...] + gate_ref[...] * (yh * g_ref[...])

    row = pl.BlockSpec((tr, d), lambda i: (i, 0))
    vec = pl.BlockSpec((1, d), lambda i: (0, 0))
    return pl.pallas_call(body, name=name, grid=(s // tr,), in_specs=[row, row, vec, vec], out_specs=row,
                          out_shape=jax.ShapeDtypeStruct((s, d), F32), compiler_params=_params("parallel"))(x, y, gate, g)


def post2_loss_bwd(x1, ffn, target, gate2, g):
    s, d = x1.shape
    tr = _tile(s, ROW_TILE)

    def body(x_ref, y_ref, t_ref, gate_ref, g_ref, loss_ref, dout_ref, dy_ref, acc_ref):
        @pl.when(pl.program_id(0) == 0)
        def _():
            loss_ref[...] = jnp.zeros_like(loss_ref)
            acc_ref[...] = jnp.zeros_like(acc_ref)

        inv, yh = _rms_stats(y_ref[...])
        r = yh * g_ref[...]
        err = x_ref[...] + gate_ref[...] * r - t_ref[...]
        loss_ref[...] += 0.5 * jnp.sum(jnp.mean(err * err, axis=-1, keepdims=True))
        dout = err / d
        dout_ref[...] = dout
        dr = dout * gate_ref[...]
        acc_ref[0:1, :] += _colsum(dout * r)
        acc_ref[1:2, :] += _colsum(dr * yh)
        dy_ref[...] = _rms_bwd(dr * g_ref[...], yh, inv).astype(dy_ref.dtype)

    row = pl.BlockSpec((tr, d), lambda i: (i, 0))
    vec = pl.BlockSpec((1, d), lambda i: (0, 0))
    return pl.pallas_call(
        body, name="post2_loss_bwd", grid=(s // tr,), in_specs=[row, row, row, vec, vec],
        out_specs=[_full((8, LANES)), row, row, _full((8, d))],
        out_shape=[jax.ShapeDtypeStruct((8, LANES), F32), jax.ShapeDtypeStruct((s, d), F32),
                   jax.ShapeDtypeStruct((s, d), BF16), jax.ShapeDtypeStruct((8, d), F32)],
        compiler_params=_params("arbitrary"))(x1, ffn, target, gate2, g)


def mid_bwd(dh2, dout, x1, y1, pre2_g, scale2, gate1, post1_g):
    s, d = x1.shape
    tr = _tile(s, ROW_TILE)

    def body(dh_ref, dout_ref, x_ref, y_ref, g2_ref, sc_ref, gate_ref, g1_ref, dx_ref, dy_ref, acc_ref):
        @pl.when(pl.program_id(0) == 0)
        def _():
            acc_ref[...] = jnp.zeros_like(acc_ref)

        dh = dh_ref[...]
        inv2, xh = _rms_stats(x_ref[...])
        acc_ref[0:1, :] += _colsum(dh)
        acc_ref[1:2, :] += _colsum(dh * (xh * g2_ref[...]))
        t = dh * (1.0 + sc_ref[...])
        acc_ref[2:3, :] += _colsum(t * xh)
        dx1 = dout_ref[...] + _rms_bwd(t * g2_ref[...], xh, inv2)
        dx_ref[...] = dx1
        inv1, yh = _rms_stats(y_ref[...])
        acc_ref[3:4, :] += _colsum(dx1 * (yh * g1_ref[...]))
        dr = dx1 * gate_ref[...]
        acc_ref[4:5, :] += _colsum(dr * yh)
        dy_ref[...] = _rms_bwd(dr * g1_ref[...], yh, inv1).astype(dy_ref.dtype)

    row = pl.BlockSpec((tr, d), lambda i: (i, 0))
    vec = pl.BlockSpec((1, d), lambda i: (0, 0))
    return pl.pallas_call(
        body, name="mid_bwd", grid=(s // tr,), in_specs=[row, row, row, row, vec, vec, vec, vec],
        out_specs=[row, row, _full((8, d))],
        out_shape=[jax.ShapeDtypeStruct((s, d), F32), jax.ShapeDtypeStruct((s, d), BF16), jax.ShapeDtypeStruct((8, d), F32)],
        compiler_params=_params("arbitrary"))(dh2, dout, x1, y1, pre2_g, scale2, gate1, post1_g)


def pre1_bwd(dh1, dx1, x, pre1_g, scale1):
    s, d = x.shape
    tr = _tile(s, ROW_TILE)

    def body(dh_ref, dx1_ref, x_ref, g_ref, sc_ref, dx_ref, acc_ref):
        @pl.when(pl.program_id(0) == 0)
        def _():
            acc_ref[...] = jnp.zeros_like(acc_ref)

        dh = dh_ref[...]
        inv, xh = _rms_stats(x_ref[...])
        acc_ref[0:1, :] += _colsum(dh)
        acc_ref[1:2, :] += _colsum(dh * (xh * g_ref[...]))
        t = dh * (1.0 + sc_ref[...])
        acc_ref[2:3, :] += _colsum(t * xh)
        dx_ref[...] = dx1_ref[...] + _rms_bwd(t * g_ref[...], xh, inv)

    row = pl.BlockSpec((tr, d), lambda i: (i, 0))
    vec = pl.BlockSpec((1, d), lambda i: (0, 0))
    return pl.pallas_call(
        body, name="pre1_bwd", grid=(s // tr,), in_specs=[row, row, row, vec, vec], out_specs=[row, _full((8, d))],
        out_shape=[jax.ShapeDtypeStruct((s, d), F32), jax.ShapeDtypeStruct((8, d), F32)],
        compiler_params=_params("arbitrary"))(dh1, dx1, x, pre1_g, scale1)


def _ln_stats(v):
    mu = jnp.mean(v, axis=-1, keepdims=True)
    vc = v - mu
    rstd = lax.rsqrt(jnp.mean(vc * vc, axis=-1, keepdims=True) + EPS)
    return rstd, vc * rstd


def gmlp_fwd(z, width, ln_g, ln_b, wm, bs3):
    s = z.shape[0]
    groups = width // CHUNK

    def body(u_ref, v_ref, g_ref, b_ref, wm_ref, bs_ref, a_ref):
        ug = _gelu(u_ref[...])
        _, vh = _ln_stats(_gelu(v_ref[...]))
        vn = (vh * g_ref[...] + b_ref[...]).astype(BF16)
        for g in range(groups):
            cols = slice(g * CHUNK, (g + 1) * CHUNK)
            mixed = jnp.dot(wm_ref[g], vn[:, cols], preferred_element_type=F32) + bs_ref[g]
            a_ref[:, cols] = (ug[:, cols] * mixed).astype(a_ref.dtype)

    vec = pl.BlockSpec((1, width), lambda n: (0, 0))
    return pl.pallas_call(
        body, name="gmlp_fwd", grid=(s // CHUNK,),
        in_specs=[pl.BlockSpec((CHUNK, width), lambda n: (n, 0)), pl.BlockSpec((CHUNK, width), lambda n: (n, 1)), vec, vec,
                  _full(wm.shape), _full(bs3.shape)],
        out_specs=pl.BlockSpec((CHUNK, width), lambda n: (n, 0)),
        out_shape=jax.ShapeDtypeStruct((s, width), BF16), compiler_params=_params("parallel"))(z, z, ln_g, ln_b, wm, bs3)


def gmlp_bwd(z, width, da, ln_g, ln_b, wm, bs3):
    s = z.shape[0]
    groups = width // CHUNK

    def body(u_ref, v_ref, da_ref, g_ref, b_ref, wm_ref, bs_ref, duv_ref, gw_ref, gb_ref, acc_ref, dvn_ref):
        @pl.when(pl.program_id(0) == 0)
        def _():
            gw_ref[...] = jnp.zeros_like(gw_ref)
            gb_ref[...] = jnp.zeros_like(gb_ref)
            acc_ref[...] = jnp.zeros_like(acc_ref)

        ug, dug = _gelu_and_grad(u_ref[...])
        vg, dvg = _gelu_and_grad(v_ref[...])
        rstd, vh = _ln_stats(vg)
        vn = (vh * g_ref[...] + b_ref[...]).astype(BF16)
        da_v = da_ref[...]
        for g in range(groups):
            cols = slice(g * CHUNK, (g + 1) * CHUNK)
            mixed = jnp.dot(wm_ref[g], vn[:, cols], preferred_element_type=F32) + bs_ref[g]
            duv_ref[:, cols] = (da_v[:, cols] * mixed * dug[:, cols]).astype(duv_ref.dtype)
            dm = da_v[:, cols] * ug[:, cols]
            gb_ref[g] += jnp.sum(dm, axis=-1, keepdims=True)
            dmb = dm.astype(BF16)
            gw_ref[g] += lax.dot_general(dmb, vn[:, cols], (((1,), (1,)), ((), ())), preferred_element_type=F32)
            dvn_ref[:, cols] = lax.dot_general(wm_ref[g], dmb, (((0,), (0,)), ((), ())), preferred_element_type=F32)
        dvn = dvn_ref[...]
        acc_ref[0:1, :] += _colsum(dvn * vh)
        acc_ref[1:2, :] += _colsum(dvn)
        dvh = dvn * g_ref[...]
        dv = rstd * (dvh - jnp.mean(dvh, axis=-1, keepdims=True) - vh * jnp.mean(dvh * vh, axis=-1, keepdims=True))
        duv_ref[:, width:] = (dv * dvg).astype(duv_ref.dtype)

        @pl.when(pl.program_id(0) == pl.num_programs(0) - 1)
        def _():
            q = lax.broadcasted_iota(jnp.int32, gw_ref.shape, 1)
            p = lax.broadcasted_iota(jnp.int32, gw_ref.shape, 2)
            gw_ref[...] = jnp.where(p <= q, gw_ref[...], 0.0)

    vec = pl.BlockSpec((1, width), lambda n: (0, 0))
    blk = pl.BlockSpec((CHUNK, width), lambda n: (n, 0))
    return pl.pallas_call(
        body, name="gmlp_bwd", grid=(s // CHUNK,),
        in_specs=[blk, pl.BlockSpec((CHUNK, width), lambda n: (n, 1)), blk, vec, vec, _full(wm.shape), _full(bs3.shape)],
        out_specs=[pl.BlockSpec((CHUNK, 2 * width), lambda n: (n, 0)), _full(wm.shape), _full(bs3.shape), _full((8, width))],
        out_shape=[jax.ShapeDtypeStruct((s, 2 * width), BF16), jax.ShapeDtypeStruct(wm.shape, F32),
                   jax.ShapeDtypeStruct(bs3.shape, F32), jax.ShapeDtypeStruct((8, width), F32)],
        scratch_shapes=[pltpu.VMEM((CHUNK, width), F32)],
        compiler_params=_params("arbitrary"))(z, z, da, ln_g, ln_b, wm, bs3)


def merge_fwd(z, off_a, off_b, ya, yb):
    s, d = ya.shape
    tr, tc = _tile(s, ROW_TILE * 2), _tile(d, COL_TILE)
    assert off_a % tc == 0 and off_b % tc == 0

    def body(ga_ref, gb_ref, ya_ref, yb_ref, o_ref):
        o_ref[...] = (_sigmoid(ga_ref[...]) * ya_ref[...] + _sigmoid(gb_ref[...]) * yb_ref[...]).astype(o_ref.dtype)

    blk = pl.BlockSpec((tr, tc), lambda i, j: (i, j))
    return pl.pallas_call(
        body, name="merge_fwd", grid=(s // tr, d // tc),
        in_specs=[pl.BlockSpec((tr, tc), lambda i, j: (i, off_a // tc + j)), pl.BlockSpec((tr, tc), lambda i, j: (i, off_b // tc + j)), blk, blk],
        out_specs=blk, out_shape=jax.ShapeDtypeStruct((s, d), BF16), compiler_params=_params("parallel", "parallel"))(z, z, ya, yb)


def merge_bwd(z, off_a, off_b, ya, yb, dm):
    s, d = ya.shape
    tr, tc = _tile(s, ROW_TILE * 2), _tile(d, COL_TILE)
    nc = d // tc

    def body(ga_ref, gb_ref, ya_ref, yb_ref, dm_ref, dya_ref, dyb_ref, dga_ref, dgb_ref):
        dm_v = dm_ref[...]
        sa, sb = _sigmoid(ga_ref[...]), _sigmoid(gb_ref[...])
        dya_ref[...] = (dm_v * sa).astype(dya_ref.dtype)
        dyb_ref[...] = (dm_v * sb).astype(dyb_ref.dtype)
        dga_ref[...] = (dm_v * ya_ref[...] * sa * (1.0 - sa)).astype(dga_ref.dtype)
        dgb_ref[...] = (dm_v * yb_ref[...] * sb * (1.0 - sb)).astype(dgb_ref.dtype)

    blk = pl.BlockSpec((tr, tc), lambda i, j: (i, j))
    out = jax.ShapeDtypeStruct((s, d), BF16)
    return pl.pallas_call(
        body, name="merge_bwd", grid=(s // tr, nc),
        in_specs=[pl.BlockSpec((tr, tc), lambda i, j: (i, off_a // tc + j)), pl.BlockSpec((tr, tc), lambda i, j: (i, off_b // tc + j)), blk, blk, blk],
        out_specs=[blk, blk, blk, blk], out_shape=[out, out, out, out],
        compiler_params=_params("parallel", "parallel"))(z, z, ya, yb, dm)


_ATT_SCALE = (QK_NOPE + QK_ROPE) ** -0.5
_NEG = -1e30


def rope_k(z, off, cos4, sin4):
    s = z.shape[0]
    tr = _tile(s, ROW_TILE * 2)
    assert off % LANES == 0

    def body(k_ref, c_ref, s_ref, o_ref):
        k = k_ref[...]
        k = k + pltpu.roll(k, QK_ROPE, 1)
        o_ref[...] = _rope(k, c_ref[...], s_ref[...]).astype(o_ref.dtype)

    row = pl.BlockSpec((tr, LANES), lambda i: (i, 0))
    return pl.pallas_call(body, name="rope_k", grid=(s // tr,),
                          in_specs=[pl.BlockSpec((tr, LANES), lambda i: (i, off // LANES)), row, row], out_specs=row,
                          out_shape=jax.ShapeDtypeStruct((s, LANES), BF16), compiler_params=_params("parallel"))(z, cos4, sin4)


def _head_masks(shape):
    lane = lax.broadcasted_iota(jnp.int32, shape, 1)
    return lane < QK_ROPE, lane >= QK_ROPE


def _scores(qn, qp_h, k, kp, qi, kb, t):
    sc = lax.dot_general(qn, k, (((1,), (1,)), ((), ())), preferred_element_type=F32)
    sc += lax.dot_general(qp_h, kp, (((1,), (1,)), ((), ())), preferred_element_type=F32)
    sc = sc * _ATT_SCALE
    row = lax.broadcasted_iota(jnp.int32, sc.shape, 0) + qi * t
    col = lax.broadcasted_iota(jnp.int32, sc.shape, 1) + kb * t
    return jnp.where(col <= row, sc, _NEG)


def attn_fwd(qn, qp, kv, kpr, cos4, sin4):
    s = qn.shape[0]
    hp = HEADS // 2
    t = _tile(s, ATT_TILE)
    nq = s // t

    def body(qn_ref, qp_ref, kv_ref, kp_ref, c_ref, s_ref, o_ref, qpr_ref, l_ref):
        qi = pl.program_id(1)
        qpr = _rope(qp_ref[...], c_ref[...], s_ref[...]).astype(BF16)
        qpr_ref[...] = qpr
        masks = _head_masks(qpr.shape)
        for hh in range(2):
            q_n = qn_ref[:, hh * QK_NOPE:(hh + 1) * QK_NOPE]
            q_p = jnp.where(masks[hh], qpr, jnp.zeros_like(qpr))
            kc, vc = 2 * hh * QK_NOPE, (2 * hh + 1) * QK_NOPE

            def step(kb, carry):
                m, l, acc = carry
                rows = pl.ds(pl.multiple_of(kb * t, t), t)
                sc = _scores(q_n, q_p, kv_ref[rows, kc:kc + QK_NOPE], kp_ref[rows, :], qi, kb, t)
                m_new = jnp.maximum(m, jnp.max(sc, axis=-1, keepdims=True))
                alpha = jnp.exp(m - m_new)
                p = jnp.exp(sc - m_new)
                l = alpha * l + jnp.sum(p, axis=-1, keepdims=True)
                acc = alpha * acc + jnp.dot(p.astype(BF16), kv_ref[rows, vc:vc + V_HEAD], preferred_element_type=F32)
                return m_new, l, acc

            init = (jnp.full((t, 1), _NEG, F32), jnp.zeros((t, 1), F32), jnp.zeros((t, V_HEAD), F32))
            m, l, acc = lax.fori_loop(0, qi + 1, step, init)
            o_ref[:, hh * V_HEAD:(hh + 1) * V_HEAD] = acc / l
            l_ref[:, hh:hh + 1] = m + jnp.log(l)

    return pl.pallas_call(
        body, name="attn_fwd", grid=(hp, nq),
        in_specs=[pl.BlockSpec((t, 2 * QK_NOPE), lambda h, i: (i, h)), pl.BlockSpec((t, LANES), lambda h, i: (i, h)),
                  pl.BlockSpec((s, 4 * QK_NOPE), lambda h, i: (0, h)), _full((s, LANES)),
                  pl.BlockSpec((t, LANES), lambda h, i: (i, 0)), pl.BlockSpec((t, LANES), lambda h, i: (i, 0))],
        out_specs=[pl.BlockSpec((t, 2 * V_HEAD), lambda h, i: (i, h)), pl.BlockSpec((t, LANES), lambda h, i: (i, h)),
                   pl.BlockSpec((None, t, 2), lambda h, i: (h, i, 0))],
        out_shape=[jax.ShapeDtypeStruct((s, HEADS * V_HEAD), F32), jax.ShapeDtypeStruct((s, HEADS * QK_ROPE), BF16),
                   jax.ShapeDtypeStruct((hp, s, 2), F32)],
        compiler_params=_params("parallel", "parallel"))(qn, qp, kv, kpr, cos4, sin4)


def attn_bwd_q(qn, qpr, kv, kpr, o, do, lse, cos4, sin4):
    s = qn.shape[0]
    hp = HEADS // 2
    t = _tile(s, ATT_TILE)
    nq = s // t

    def body(qn_ref, qpr_ref, kv_ref, kp_ref, o_ref, do_ref, l_ref, c_ref, s_ref, dqn_ref, dqp_ref):
        qi = pl.program_id(1)
        qpr = qpr_ref[...]
        masks = _head_masks(qpr.shape)
        dqp = jnp.zeros(qpr.shape, F32)
        for hh in range(2):
            q_n = qn_ref[:, hh * QK_NOPE:(hh + 1) * QK_NOPE]
            q_p = jnp.where(masks[hh], qpr, jnp.zeros_like(qpr))
            kc, vc = 2 * hh * QK_NOPE, (2 * hh + 1) * QK_NOPE
            do_h = do_ref[:, hh * V_HEAD:(hh + 1) * V_HEAD]
            delta = jnp.sum(do_h * o_ref[:, hh * V_HEAD:(hh + 1) * V_HEAD], axis=-1, keepdims=True)
            do_b = do_h.astype(BF16)
            lse_h = l_ref[:, hh:hh + 1]

            def step(kb, carry):
                dn, dp_ = carry
                rows = pl.ds(pl.multiple_of(kb * t, t), t)
                k = kv_ref[rows, kc:kc + QK_NOPE]
                kp = kp_ref[rows, :]
                p = jnp.exp(_scores(q_n, q_p, k, kp, qi, kb, t) - lse_h)
                dpv = lax.dot_general(do_b, kv_ref[rows, vc:vc + V_HEAD], (((1,), (1,)), ((), ())), preferred_element_type=F32)
                ds = (p * (dpv - delta) * _ATT_SCALE).astype(BF16)
                dn = dn + jnp.dot(ds, k, preferred_element_type=F32)
                dp_ = dp_ + jnp.dot(ds, kp, preferred_element_type=F32)
                return dn, dp_

            dn, dp_h = lax.fori_loop(0, qi + 1, step, (jnp.zeros((t, QK_NOPE), F32), jnp.zeros((t, LANES), F32)))
            dqn_ref[:, hh * QK_NOPE:(hh + 1) * QK_NOPE] = dn.astype(dqn_ref.dtype)
            dqp = dqp + jnp.where(masks[hh], dp_h, jnp.zeros_like(dp_h))
        dqp_ref[...] = _rope(dqp, c_ref[...], -s_ref[...]).astype(dqp_ref.dtype)

    qblk = pl.BlockSpec((t, 2 * QK_NOPE), lambda h, i: (i, h))
    pblk = pl.BlockSpec((t, LANES), lambda h, i: (i, h))
    tab = pl.BlockSpec((t, LANES), lambda h, i: (i, 0))
    return pl.pallas_call(
        body, name="attn_bwd_q", grid=(hp, nq),
        in_specs=[qblk, pblk, pl.BlockSpec((s, 4 * QK_NOPE), lambda h, i: (0, h)), _full((s, LANES)), qblk, qblk,
                  pl.BlockSpec((None, t, 2), lambda h, i: (h, i, 0)), tab, tab],
        out_specs=[qblk, pblk],
        out_shape=[jax.ShapeDtypeStruct((s, HEADS * QK_NOPE), BF16), jax.ShapeDtypeStruct((s, HEADS * QK_ROPE), BF16)],
        compiler_params=_params("parallel", "parallel"))(qn, qpr, kv, kpr, o, do, lse, cos4, sin4)


def attn_bwd_kv(qn, qpr, kv, kpr, o, do, lse):
    s = qn.shape[0]
    hp = HEADS // 2
    t = _tile(s, ATT_TILE)
    nq = s // t

    def body(qn_ref, qpr_ref, kv_ref, kp_ref, o_ref, do_ref, l_ref, dkv_ref, dkp_ref):
        ki = pl.program_id(1)
        rows_k = pl.ds(pl.multiple_of(ki * t, t), t)
        kp = kp_ref[rows_k, :]
        dkp = jnp.zeros((t, LANES), F32)
        for hh in range(2):
            kc, vc = 2 * hh * QK_NOPE, (2 * hh + 1) * QK_NOPE
            k = kv_ref[rows_k, kc:kc + QK_NOPE]
            v = kv_ref[rows_k, vc:vc + V_HEAD]

            def step(qb, carry):
                dk, dv, dkp_h = carry
                rows = pl.ds(pl.multiple_of(qb * t, t), t)
                q_n = qn_ref[rows, hh * QK_NOPE:(hh + 1) * QK_NOPE]
                qpr = qpr_ref[rows, :]
                lane = lax.broadcasted_iota(jnp.int32, qpr.shape, 1)
                sel = (lane < QK_ROPE) if hh == 0 else (lane >= QK_ROPE)
                q_p = jnp.where(sel, qpr, jnp.zeros_like(qpr))
                do_h = do_ref[rows, hh * V_HEAD:(hh + 1) * V_HEAD]
                delta = jnp.sum(do_h * o_ref[rows, hh * V_HEAD:(hh + 1) * V_HEAD], axis=-1, keepdims=True)
                do_b = do_h.astype(BF16)
                p = jnp.exp(_scores(q_n, q_p, k, kp, qb, ki, t) - l_ref[rows, hh:hh + 1])
                dpv = lax.dot_general(do_b, v, (((1,), (1,)), ((), ())), preferred_element_type=F32)
                ds = (p * (dpv - delta) * _ATT_SCALE).astype(BF16)
                dv = dv + lax.dot_general(p.astype(BF16), do_b, (((0,), (0,)), ((), ())), preferred_element_type=F32)
                dk = dk + lax.dot_general(ds, q_n, (((0,), (0,)), ((), ())), preferred_element_type=F32)
                dkp_h = dkp_h + lax.dot_general(ds, q_p, (((0,), (0,)), ((), ())), preferred_element_type=F32)
                return dk, dv, dkp_h

            init = (jnp.zeros((t, QK_NOPE), F32), jnp.zeros((t, V_HEAD), F32), jnp.zeros((t, LANES), F32))
            dk, dv, dkp_h = lax.fori_loop(ki, nq, step, init)
            dkv_ref[:, kc:kc + QK_NOPE] = dk.astype(dkv_ref.dtype)
            dkv_ref[:, vc:vc + V_HEAD] = dv.astype(dkv_ref.dtype)
            dkp = dkp + dkp_h
        dkp_ref[...] = dkp

    return pl.pallas_call(
        body, name="attn_bwd_kv", grid=(hp, nq),
        in_specs=[pl.BlockSpec((s, 2 * QK_NOPE), lambda h, i: (0, h)), pl.BlockSpec((s, LANES), lambda h, i: (0, h)),
                  pl.BlockSpec((s, 4 * QK_NOPE), lambda h, i: (0, h)), _full((s, LANES)),
                  pl.BlockSpec((s, 2 * V_HEAD), lambda h, i: (0, h)), pl.BlockSpec((s, 2 * V_HEAD), lambda h, i: (0, h)),
                  pl.BlockSpec((None, s, 2), lambda h, i: (h, 0, 0))],
        out_specs=[pl.BlockSpec((t, 4 * QK_NOPE), lambda h, i: (i, h)), pl.BlockSpec((None, t, LANES), lambda h, i: (h, i, 0))],
        out_shape=[jax.ShapeDtypeStruct((s, HEADS * 2 * QK_NOPE), BF16), jax.ShapeDtypeStruct((hp, s, LANES), F32)],
        compiler_params=_params("parallel", "parallel"))(qn, qpr, kv, kpr, o, do, lse)


def _dot_nt(a, b):
    return lax.dot_general(a, b, (((1,), (1,)), ((), ())), preferred_element_type=F32)


def _dot_tn(a, b):
    return lax.dot_general(a, b, (((0,), (0,)), ((), ())), preferred_element_type=F32)


def _q_cat(q_n, qpr, hh):
    lane = lax.broadcasted_iota(jnp.int32, qpr.shape, 1)
    sel = (lane < QK_ROPE) if hh == 0 else (lane >= QK_ROPE)
    return jnp.concatenate([q_n, jnp.where(sel, qpr, jnp.zeros_like(qpr))], axis=1)


def _causal(sc):
    row = lax.broadcasted_iota(jnp.int32, sc.shape, 0)
    col = lax.broadcasted_iota(jnp.int32, sc.shape, 1)
    return jnp.where(col <= row, sc, _NEG)


def attn_fwd2(qn, qp, kv, kpr, cos4, sin4):
    s = qn.shape[0]
    hp = HEADS // 2
    t = _tile(s, ATT_TILE)
    nq = s // t

    def body(qn_ref, qp_ref, kv_ref, kp_ref, c_ref, s_ref, o_ref, qpr_ref, l_ref, kcat_ref):
        qi = pl.program_id(1)

        @pl.when(qi == 0)
        def _():
            for hh in range(2):
                kcat_ref[hh, :, 0:QK_NOPE] = kv_ref[:, 2 * hh * QK_NOPE:(2 * hh + 1) * QK_NOPE]
                kcat_ref[hh, :, QK_NOPE:] = kp_ref[...]

        qpr = _rope(qp_ref[...], c_ref[...], s_ref[...]).astype(BF16)
        qpr_ref[...] = qpr
        qcat = [_q_cat(qn_ref[:, hh * QK_NOPE:(hh + 1) * QK_NOPE], qpr, hh) for hh in range(2)]

        def block(kb, carry, diagonal):
            rows = pl.ds(pl.multiple_of(kb * t, t), t)
            out = []
            for hh in range(2):
                m, l, acc = carry[hh]
                sc = _dot_nt(qcat[hh], kcat_ref[hh, rows, :]) * _ATT_SCALE
                if diagonal:
                    sc = _causal(sc)
                m_new = jnp.maximum(m, jnp.max(sc, axis=-1, keepdims=True))
                alpha = jnp.exp(m - m_new)
                p = jnp.exp(sc - m_new)
                l = alpha * l + jnp.sum(p, axis=-1, keepdims=True)
                v = kv_ref[rows, (2 * hh + 1) * QK_NOPE:(2 * hh + 2) * QK_NOPE]
                acc = alpha * acc + jnp.dot(p.astype(BF16), v, preferred_element_type=F32)
                out.append((m_new, l, acc))
            return tuple(out)

        one = (jnp.full((t, 1), _NEG, F32), jnp.zeros((t, 1), F32), jnp.zeros((t, V_HEAD), F32))
        carry = lax.fori_loop(0, qi, lambda kb, cr: block(kb, cr, False), (one, one))
        carry = block(qi, carry, True)
        for hh in range(2):
            m, l, acc = carry[hh]
            o_ref[:, hh * V_HEAD:(hh + 1) * V_HEAD] = acc / l
            l_ref[:, hh:hh + 1] = m + jnp.log(l)

    return pl.pallas_call(
        body, name="attn_fwd", grid=(hp, nq),
        in_specs=[pl.BlockSpec((t, 2 * QK_NOPE), lambda h, i: (i, h)), pl.BlockSpec((t, LANES), lambda h, i: (i, h)),
                  pl.BlockSpec((s, 4 * QK_NOPE), lambda h, i: (0, h)), _full((s, LANES)),
                  pl.BlockSpec((t, LANES), lambda h, i: (i, 0)), pl.BlockSpec((t, LANES), lambda h, i: (i, 0))],
        out_specs=[pl.BlockSpec((t, 2 * V_HEAD), lambda h, i: (i, h)), pl.BlockSpec((t, LANES), lambda h, i: (i, h)),
                   pl.BlockSpec((None, t, 2), lambda h, i: (h, i, 0))],
        out_shape=[jax.ShapeDtypeStruct((s, HEADS * V_HEAD), F32), jax.ShapeDtypeStruct((s, HEADS * QK_ROPE), BF16),
                   jax.ShapeDtypeStruct((hp, s, 2), F32)],
        scratch_shapes=[pltpu.VMEM((2, s, 2 * QK_NOPE), BF16)],
        compiler_params=_params("parallel", "arbitrary"))(qn, qp, kv, kpr, cos4, sin4)


def attn_bwd2(qn, qpr, kv, kpr, o, do, lse, cos4, sin4):
    s = qn.shape[0]
    hp = HEADS // 2
    t = _tile(s, ATT_TILE)
    nk = s // t

    def body(qn_ref, qpr_ref, kv_ref, kp_ref, o_ref, do_ref, l_ref, c_ref, s_ref,
             dqn_ref, dqp_ref, dkv_ref, dkp_ref, qcat_ref, dq_ref, delta_ref):
        ki = pl.program_id(1)

        @pl.when(ki == 0)
        def _():
            dq_ref[...] = jnp.zeros_like(dq_ref)
            for hh in range(2):
                qcat_ref[hh] = _q_cat(qn_ref[:, hh * QK_NOPE:(hh + 1) * QK_NOPE], qpr_ref[...], hh)
                cols = slice(hh * V_HEAD, (hh + 1) * V_HEAD)
                delta_ref[hh] = jnp.sum(do_ref[:, cols] * o_ref[:, cols], axis=-1, keepdims=True)

        rows_k = pl.ds(pl.multiple_of(ki * t, t), t)
        kcat = [jnp.concatenate([kv_ref[rows_k, 2 * hh * QK_NOPE:(2 * hh + 1) * QK_NOPE], kp_ref[rows_k, :]], axis=1) for hh in range(2)]
        vs = [kv_ref[rows_k, (2 * hh + 1) * QK_NOPE:(2 * hh + 2) * QK_NOPE] for hh in range(2)]

        def block(qb, carry, diagonal):
            rows = pl.ds(pl.multiple_of(qb * t, t), t)
            out = []
            for hh in range(2):
                dkc, dv = carry[hh]
                q_c = qcat_ref[hh, rows, :]
                do_b = do_ref[rows, hh * V_HEAD:(hh + 1) * V_HEAD].astype(BF16)
                sc = _dot_nt(q_c, kcat[hh]) * _ATT_SCALE
                if diagonal:
                    sc = _causal(sc)
                p = jnp.exp(sc - l_ref[rows, hh:hh + 1])
                dpv = _dot_nt(do_b, vs[hh])
                ds = (p * (dpv - delta_ref[hh, rows, :]) * _ATT_SCALE).astype(BF16)
                dv = dv + _dot_tn(p.astype(BF16), do_b)
                dkc = dkc + _dot_tn(ds, q_c)
                dq_ref[hh, rows, :] += jnp.dot(ds, kcat[hh], preferred_element_type=F32)
                out.append((dkc, dv))
            return tuple(out)

        one = (jnp.zeros((t, 2 * QK_NOPE), F32), jnp.zeros((t, V_HEAD), F32))
        carry = block(ki, (one, one), True)
        carry = lax.fori_loop(ki + 1, nk, lambda qb, cr: block(qb, cr, False), carry)
        dkp = jnp.zeros((t, LANES), F32)
        for hh in range(2):
            dkc, dv = carry[hh]
            dkv_ref[:, 2 * hh * QK_NOPE:(2 * hh + 1) * QK_NOPE] = dkc[:, :QK_NOPE].astype(dkv_ref.dtype)
            dkv_ref[:, (2 * hh + 1) * QK_NOPE:(2 * hh + 2) * QK_NOPE] = dv.astype(dkv_ref.dtype)
            dkp = dkp + dkc[:, QK_NOPE:]
        dkp_ref[...] = dkp

        @pl.when(ki == nk - 1)
        def _():
            lane = lax.broadcasted_iota(jnp.int32, (s, LANES), 1)
            dqp = jnp.where(lane < QK_ROPE, dq_ref[0, :, QK_NOPE:], dq_ref[1, :, QK_NOPE:])
            dqp_ref[...] = _rope(dqp, c_ref[...], -s_ref[...]).astype(dqp_ref.dtype)
            for hh in range(2):
                dqn_ref[:, hh * QK_NOPE:(hh + 1) * QK_NOPE] = dq_ref[hh, :, :QK_NOPE].astype(dqn_ref.dtype)

    qblk = pl.BlockSpec((s, 2 * QK_NOPE), lambda h, i: (0, h))
    pblk = pl.BlockSpec((s, LANES), lambda h, i: (0, h))
    tab = _full((s, LANES))
    return pl.pallas_call(
        body, name="attn_bwd", grid=(hp, nk),
        in_specs=[qblk, pblk, pl.BlockSpec((s, 4 * QK_NOPE), lambda h, i: (0, h)), tab, qblk, qblk,
                  pl.BlockSpec((None, s, 2), lambda h, i: (h, 0, 0)), tab, tab],
        out_specs=[qblk, pblk, pl.BlockSpec((t, 4 * QK_NOPE), lambda h, i: (i, h)), pl.BlockSpec((None, t, LANES), lambda h, i: (h, i, 0))],
        out_shape=[jax.ShapeDtypeStruct((s, HEADS * QK_NOPE), BF16), jax.ShapeDtypeStruct((s, HEADS * QK_ROPE), BF16),
                   jax.ShapeDtypeStruct((s, HEADS * 2 * QK_NOPE), BF16), jax.ShapeDtypeStruct((hp, s, LANES), F32)],
        scratch_shapes=[pltpu.VMEM((2, s, 2 * QK_NOPE), BF16), pltpu.VMEM((2, s, 2 * QK_NOPE), F32), pltpu.VMEM((2, s, 1), F32)],
        compiler_params=_params("parallel", "arbitrary"))(qn, qpr, kv, kpr, o, do, lse, cos4, sin4)


def kpe_bwd(dkp, cos4, sin4, pad_cols):
    hp, s, _ = dkp.shape
    tr = _tile(s, ROW_TILE * 2)

    def body(d_ref, c_ref, s_ref, o_ref):
        tot = d_ref[0]
        for h in range(1, hp):
            tot = tot + d_ref[h]
        tot = tot + pltpu.roll(tot, QK_ROPE, 1)
        lane = lax.broadcasted_iota(jnp.int32, tot.shape, 1)
        dk = jnp.where(lane < QK_ROPE, _rope(tot, c_ref[...], -s_ref[...]), jnp.zeros_like(tot))
        o_ref[...] = jnp.zeros_like(o_ref)
        o_ref[:, 0:LANES] = dk.astype(o_ref.dtype)

    row = pl.BlockSpec((tr, LANES), lambda i: (i, 0))
    return pl.pallas_call(body, name="kpe_bwd", grid=(s // tr,),
                          in_specs=[pl.BlockSpec((hp, tr, LANES), lambda i: (0, i, 0)), row, row],
                          out_specs=pl.BlockSpec((tr, pad_cols), lambda i: (i, 0)),
                          out_shape=jax.ShapeDtypeStruct((s, pad_cols), BF16), compiler_params=_params("parallel"))(dkp, cos4, sin4)


def _shift_down(x, n):
    row = lax.broadcasted_iota(jnp.int32, x.shape, 0)
    return jnp.where(row >= n, pltpu.roll(x, n, 0), jnp.zeros_like(x))


def _shift_up(x, n):
    rows = x.shape[0]
    row = lax.broadcasted_iota(jnp.int32, x.shape, 0)
    return jnp.where(row < rows - n, pltpu.roll(x, rows - n, 0), jnp.zeros_like(x))


def _conv(x, w_ref, b_ref):
    return w_ref[2:3, :] * x + w_ref[1:2, :] * _shift_down(x, 1) + w_ref[0:1, :] * _shift_down(x, 2) + b_ref[...]


def conv_act_fwd(upre, conv_w, conv_b):
    s, f2 = upre.shape
    f = f2 // 2
    tc = _tile(f, COL_TILE)
    nc = f // tc

    def body(ug_ref, uv_ref, wg_ref, wv_ref, bg_ref, bv_ref, o_ref):
        gh = _conv(ug_ref[...], wg_ref, bg_ref)
        vh = _conv(uv_ref[...], wv_ref, bv_ref)
        o_ref[...] = (gh * _sigmoid(gh) * vh).astype(o_ref.dtype)

    def spec(rows, shift):
        return pl.BlockSpec((rows, tc), lambda j: (0, j + shift))

    return pl.pallas_call(
        body, name="conv_act_fwd", grid=(nc,),
        in_specs=[spec(s, 0), spec(s, nc), spec(3, 0), spec(3, nc), spec(1, 0), spec(1, nc)], out_specs=spec(s, 0),
        out_shape=jax.ShapeDtypeStruct((s, f), BF16), compiler_params=_params("parallel"))(upre, upre, conv_w, conv_w, conv_b, conv_b)


def conv_act_bwd(upre, conv_w, conv_b, df):
    s, f2 = upre.shape
    f = f2 // 2
    tc = _tile(f, COL_TILE)
    nc = f // tc

    def half(x, d, w_ref, du_ref, gw_ref, gb_ref):
        gb_ref[...] = _colsum(d)
        gw_ref[2:3, :] = _colsum(d * x)
        gw_ref[1:2, :] = _colsum(d * _shift_down(x, 1))
        gw_ref[0:1, :] = _colsum(d * _shift_down(x, 2))
        du_ref[...] = (w_ref[2:3, :] * d + w_ref[1:2, :] * _shift_up(d, 1) + w_ref[0:1, :] * _shift_up(d, 2)).astype(du_ref.dtype)

    def body(ug_ref, uv_ref, wg_ref, wv_ref, bg_ref, bv_ref, df_ref, dug_ref, duv_ref, gwg_ref, gwv_ref, gbg_ref, gbv_ref):
        xg, xv = ug_ref[...], uv_ref[...]
        gh = _conv(xg, wg_ref, bg_ref)
        vh = _conv(xv, wv_ref, bv_ref)
        sg = _sigmoid(gh)
        df_v = df_ref[...]
        half(xg, df_v * vh * (sg * (1.0 + gh * (1.0 - sg))), wg_ref, dug_ref, gwg_ref, gbg_ref)
        half(xv, df_v * (gh * sg), wv_ref, duv_ref, gwv_ref, gbv_ref)

    def spec(rows, shift):
        return pl.BlockSpec((rows, tc), lambda j: (0, j + shift))

    act = jax.ShapeDtypeStruct((s, f), BF16)
    gw = jax.ShapeDtypeStruct((3, f), F32)
    gb = jax.ShapeDtypeStruct((1, f), F32)
    return pl.pallas_call(
        body, name="conv_act_bwd", grid=(nc,),
        in_specs=[spec(s, 0), spec(s, nc), spec(3, 0), spec(3, nc), spec(1, 0), spec(1, nc), spec(s, 0)],
        out_specs=[spec(s, 0), spec(s, 0), spec(3, 0), spec(3, 0), spec(1, 0), spec(1, 0)],
        out_shape=[act, act, gw, gw, gb, gb],
        compiler_params=_params("parallel"))(upre, upre, conv_w, conv_w, conv_b, conv_b, df)


def adamw(name, w, m, v, parts):
    npart, r, c = parts.shape
    tr = r
    if r % 8 == 0:
        tr = max(8, min(r, ADAMW_TILE_ELEMS // c) // 8 * 8)
        while r % tr:
            tr -= 8
    bc1 = 1.0 - ADAM_B1 ** ADAM_STEP
    bc2 = 1.0 - ADAM_B2 ** ADAM_STEP

    def body(w_ref, m_ref, v_ref, p_ref, g_ref, d_ref, nm_ref, nv_ref):
        g = p_ref[0].astype(F32)
        for k in range(1, npart):
            g = g + p_ref[k].astype(F32)
        m_new = ADAM_B1 * m_ref[...] + (1.0 - ADAM_B1) * g
        v_new = ADAM_B2 * v_ref[...] + (1.0 - ADAM_B2) * (g * g)
        g_ref[...] = g
        nm_ref[...] = m_new
        nv_ref[...] = v_new
        d_ref[...] = -ADAM_LR * ((m_new / bc1) / (jnp.sqrt(v_new / bc2) + ADAM_EPS) + ADAM_WD * w_ref[...])

    blk = pl.BlockSpec((tr, c), lambda i: (i, 0))
    out = jax.ShapeDtypeStruct((r, c), F32)
    return pl.pallas_call(
        body, name=name, grid=(r // tr,), in_specs=[blk, blk, blk, pl.BlockSpec((npart, tr, c), lambda i: (0, i, 0))],
        out_specs=[blk, blk, blk, blk], out_shape=[out, out, out, out], compiler_params=_params("parallel"))(w, m, v, parts)


def _position():
    return lax.axis_index("x"), lax.axis_index("y"), lax.axis_index("c")


def _index(p):
    return 4 * p[0] + 2 * p[1] + p[2]


def _peer(me, r):
    return (me[0] ^ ((r >> 2) & 1), me[1] ^ ((r >> 1) & 1), me[2] ^ (r & 1))


_ANY = pl.BlockSpec(memory_space=pl.ANY)


def all_gather_two_level(shards):
    n = len(shards)

    def body(*refs):
        ins, outs = refs[:n], refs[n:2 * n]
        send_sems, recv_sems, local_sems = refs[2 * n:]
        x, y, c = _position()
        me, sibling = (x, y, c), (x, y, 1 - c)
        chips = [(1 - x, y), (x, 1 - y), (1 - x, 1 - y)]

        def copy(w, k, block, to, src=None):
            slot = outs[w].at[_index(block)]
            return pltpu.make_async_remote_copy(src_ref=slot if src is None else src, dst_ref=slot,
                                                send_sem=send_sems.at[7 * w + k], recv_sem=recv_sems.at[7 * w + k],
                                                device_id=to, device_id_type=MESH)

        mine = [pltpu.make_async_copy(ins[w], outs[w].at[_index(me)], local_sems.at[w]) for w in range(n)]
        for cp in mine:
            cp.start()
        first = []
        for w in range(n):
            first.append(copy(w, 0, me, sibling, src=ins[w]))
            first += [copy(w, 1 + j, me, (*chip, c), src=ins[w]) for j, chip in enumerate(chips)]
        for cp in first:
            cp.start()
        passed = []
        for w in range(n):
            for j, chip in enumerate(chips):
                copy(w, 1 + j, (*chip, c), me).wait_recv()
                cp = copy(w, 4 + j, (*chip, c), sibling)
                cp.start()
                passed.append(cp)
        for w in range(n):
            copy(w, 0, sibling, me).wait_recv()
            for j, chip in enumerate(chips):
                copy(w, 4 + j, (*chip, 1 - c), me).wait_recv()
        for cp in first + passed:
            cp.wait_send()
        for cp in mine:
            cp.wait()

    return pl.pallas_call(
        body, name="all_gather_weights",
        out_shape=[jax.ShapeDtypeStruct((N_DEV,) + a.shape, a.dtype) for a in shards],
        in_specs=[_ANY] * n, out_specs=[_ANY] * n,
        scratch_shapes=[pltpu.SemaphoreType.DMA((7 * n,)), pltpu.SemaphoreType.DMA((7 * n,)), pltpu.SemaphoreType.DMA((n,))],
        )(*shards)


def exchange(name, arrays, scatter):
    n = len(arrays)

    def body(*refs):
        ins, outs = refs[:n], refs[n:2 * n]
        send_sems, recv_sems, local_sems = refs[2 * n:]
        me = _position()
        copies = []
        for w in range(n):
            src = ins[w].at[_index(me)] if scatter else ins[w]
            cp = pltpu.make_async_copy(src, outs[w].at[_index(me)], local_sems.at[w])
            cp.start()
            copies.append(cp)
        remote = []
        for w in range(n):
            for r in range(1, N_DEV):
                peer = _peer(me, r)
                src = ins[w].at[_index(peer)] if scatter else ins[w]
                cp = pltpu.make_async_remote_copy(src_ref=src, dst_ref=outs[w].at[_index(me)],
                                                  send_sem=send_sems.at[7 * w + r - 1], recv_sem=recv_sems.at[7 * w + r - 1],
                                                  device_id=peer, device_id_type=MESH)
                cp.start()
                remote.append(cp)
        for cp in remote:
            cp.wait()
        for cp in copies:
            cp.wait()

    blocks = [a.shape[1:] if scatter else a.shape for a in arrays]
    return pl.pallas_call(
        body, name=name,
        out_shape=[jax.ShapeDtypeStruct((N_DEV,) + b, a.dtype) for a, b in zip(arrays, blocks)],
        in_specs=[_ANY] * n, out_specs=[_ANY] * n,
        scratch_shapes=[pltpu.SemaphoreType.DMA((7 * n,)), pltpu.SemaphoreType.DMA((7 * n,)), pltpu.SemaphoreType.DMA((n,))],
        )(*arrays)


def ada_fwd(c, w_ada, b_ada3):
    d, cs = w_ada.shape

    def body(c_ref, w_ref, b_ref, mod_ref, sc_ref, part_ref, send_sems, recv_sems):
        me = _position()
        my = _index(me)
        cv = c_ref[...]
        sc_ref[my] = cv * _sigmoid(cv)
        gather = []
        for r in range(1, N_DEV):
            cp = pltpu.make_async_remote_copy(src_ref=sc_ref.at[my], dst_ref=sc_ref.at[my], send_sem=send_sems.at[r - 1],
                                              recv_sem=recv_sems.at[r - 1], device_id=_peer(me, r), device_id_type=MESH)
            cp.start()
            gather.append(cp)
        for cp in gather:
            cp.wait()
        sc_all = jnp.concatenate([sc_ref[k] for k in range(N_DEV)], axis=0).astype(BF16)
        part = jnp.dot(sc_all, w_ref[...].astype(BF16), preferred_element_type=F32)
        for k in range(N_DEV):
            part_ref[k] = part[k:k + 1, :]
        scatter = []
        for r in range(1, N_DEV):
            peer = _peer(me, r)
            cp = pltpu.make_async_remote_copy(src_ref=part_ref.at[_index(peer)], dst_ref=mod_ref.at[my], send_sem=send_sems.at[6 + r],
                                              recv_sem=recv_sems.at[6 + r], device_id=peer, device_id_type=MESH)
            cp.start()
            scatter.append(cp)
        mod_ref[my] = part_ref[my]
        for cp in scatter:
            cp.wait()
        mod_ref[...] = mod_ref[...] + b_ref[...]

    vm = pl.BlockSpec(memory_space=pltpu.VMEM)
    return pl.pallas_call(
        body, name="ada_fwd",
        out_shape=[jax.ShapeDtypeStruct((N_DEV, 1, cs), F32), jax.ShapeDtypeStruct((N_DEV, 1, d), F32)],
        in_specs=[vm, vm, vm], out_specs=[vm, vm],
        scratch_shapes=[pltpu.VMEM((N_DEV, 1, cs), F32), pltpu.SemaphoreType.DMA((14,)), pltpu.SemaphoreType.DMA((14,))],
        compiler_params=pltpu.CompilerParams(vmem_limit_bytes=VMEM_LIMIT_BYTES))(c, w_ada, b_ada3)


def ada_bwd_w(sc_all, dmod_cols):
    _, d = sc_all.shape
    cs = dmod_cols.shape[1]
    tr = _tile(d, ROW_TILE)

    def body(sc_ref, dm_ref, o_ref):
        dm = dm_ref[...].astype(BF16)
        o_ref[...] = lax.dot_general(sc_ref[...].astype(BF16), dm, (((0,), (0,)), ((), ())), preferred_element_type=F32)

    return pl.pallas_call(body, name="ada_bwd_w", grid=(d // tr,),
                          in_specs=[pl.BlockSpec((N_DEV, tr), lambda i: (0, i)), _full((N_DEV, cs))],
                          out_specs=pl.BlockSpec((None, tr, cs), lambda i: (0, i, 0)),
                          out_shape=jax.ShapeDtypeStruct((1, d, cs), F32), compiler_params=_params("parallel"))(sc_all, dmod_cols)


def _round_up(n, m):
    return (n + m - 1) // m * m


def kernel(x, c, positions, w_ada, b_ada, pre_norm1_g, w_in, gm_ln_g, gm_ln_b, gm_w_s, gm_b_s, w_branch_a, q_norm_g, w_uq, kv_norm_g, w_ukv, w_branch_b, w_out, post_norm1_g, pre_norm2_g, w_up, conv_w, conv_b, w_down, post_norm2_g, loss_target, m_w_ada, m_b_ada, m_pre_norm1_g, m_w_in, m_gm_ln_g, m_gm_ln_b, m_gm_w_s, m_gm_b_s, m_w_branch_a, m_q_norm_g, m_w_uq, m_kv_norm_g, m_w_ukv, m_w_branch_b, m_w_out, m_post_norm1_g, m_pre_norm2_g, m_w_up, m_conv_w, m_conv_b, m_w_down, m_post_norm2_g, v_w_ada, v_b_ada, v_pre_norm1_g, v_w_in, v_gm_ln_g, v_gm_ln_b, v_gm_w_s, v_gm_b_s, v_w_branch_a, v_q_norm_g, v_w_uq, v_kv_norm_g, v_w_ukv, v_w_branch_b, v_w_out, v_post_norm1_g, v_pre_norm2_g, v_w_up, v_conv_w, v_conv_b, v_w_down, v_post_norm2_g):
    weights = dict(w_ada=w_ada, b_ada=b_ada, pre_norm1_g=pre_norm1_g, w_in=w_in, gm_ln_g=gm_ln_g, gm_ln_b=gm_ln_b, gm_w_s=gm_w_s,
                   gm_b_s=gm_b_s, w_branch_a=w_branch_a, q_norm_g=q_norm_g, w_uq=w_uq, kv_norm_g=kv_norm_g, w_ukv=w_ukv,
                   w_branch_b=w_branch_b, w_out=w_out, post_norm1_g=post_norm1_g, pre_norm2_g=pre_norm2_g, w_up=w_up, conv_w=conv_w,
                   conv_b=conv_b, w_down=w_down, post_norm2_g=post_norm2_g)
    mom1 = dict(w_ada=m_w_ada, b_ada=m_b_ada, pre_norm1_g=m_pre_norm1_g, w_in=m_w_in, gm_ln_g=m_gm_ln_g, gm_ln_b=m_gm_ln_b,
                gm_w_s=m_gm_w_s, gm_b_s=m_gm_b_s, w_branch_a=m_w_branch_a, q_norm_g=m_q_norm_g, w_uq=m_w_uq, kv_norm_g=m_kv_norm_g,
                w_ukv=m_w_ukv, w_branch_b=m_w_branch_b, w_out=m_w_out, post_norm1_g=m_post_norm1_g, pre_norm2_g=m_pre_norm2_g,
                w_up=m_w_up, conv_w=m_conv_w, conv_b=m_conv_b, w_down=m_w_down, post_norm2_g=m_post_norm2_g)
    mom2 = dict(w_ada=v_w_ada, b_ada=v_b_ada, pre_norm1_g=v_pre_norm1_g, w_in=v_w_in, gm_ln_g=v_gm_ln_g, gm_ln_b=v_gm_ln_b,
                gm_w_s=v_gm_w_s, gm_b_s=v_gm_b_s, w_branch_a=v_w_branch_a, q_norm_g=v_q_norm_g, w_uq=v_w_uq, kv_norm_g=v_kv_norm_g,
                w_ukv=v_w_ukv, w_branch_b=v_w_branch_b, w_out=v_w_out, post_norm1_g=v_post_norm1_g, pre_norm2_g=v_pre_norm2_g,
                w_up=v_w_up, conv_w=v_conv_w, conv_b=v_conv_b, w_down=v_w_down, post_norm2_g=v_post_norm2_g)
    order = list(weights)

    s, d = x.shape[1], x.shape[2]
    gmw = gm_ln_g.shape[0]
    groups = gmw // CHUNK
    ql, kvl = q_norm_g.shape[0], kv_norm_g.shape[0]
    f2 = conv_b.shape[0]
    in_cols = w_in.shape[1] * N_DEV
    o_q, o_kv, o_ga, o_gb, o_kpe = 2 * gmw, 2 * gmw + ql, 2 * gmw + ql + kvl, 2 * gmw + ql + kvl + d, 2 * gmw + ql + kvl + 2 * d
    zp = _round_up(o_kpe + LANES, Z_PAD)
    src_kpe = 2 * gmw + ql + kvl
    assert src_kpe + QK_ROPE + 2 * d == in_cols
    my = 4 * lax.axis_index("x") + 2 * lax.axis_index("y") + lax.axis_index("c")

    x2, tgt = x[0], loss_target[0]
    row = lambda a: a.reshape(1, -1)

    mod8, sc_all3 = ada_fwd(c, w_ada, b_ada.reshape(N_DEV, 1, -1))
    mod = mod8.reshape(N_MOD, d)
    shift1, scale1, gate1, shift2, scale2, gate2 = (mod[i:i + 1] for i in range(N_MOD))
    sc_all = sc_all3.reshape(N_DEV, d)

    big = ["w_in", "w_branch_a", "w_uq", "w_ukv", "w_branch_b", "w_out", "w_up", "w_down"]
    g_in, g_a, g_uq, g_ukv, g_b, g_out, g_up, g_down, g_cw = all_gather_two_level([weights[k].astype(BF16) for k in big] + [conv_w])
    w_in_f = g_in.transpose(1, 0, 2).reshape(d, in_cols)
    w_in_p = jnp.concatenate([w_in_f[:, :src_kpe], w_in_f[:, src_kpe + QK_ROPE:], w_in_f[:, src_kpe:src_kpe + QK_ROPE],
                              jnp.zeros((d, zp - in_cols), BF16)], axis=1)
    w_a_f, w_b_f, w_out_f = g_a.reshape(-1, d), g_b.reshape(-1, d), g_out.reshape(-1, d)
    w_down_f = g_down.reshape(-1, d)
    w_uq_f = g_uq.transpose(1, 0, 2).reshape(ql, HEADS, QK_NOPE + QK_ROPE)
    w_uq_n = w_uq_f[:, :, :QK_NOPE].reshape(ql, HEADS * QK_NOPE)
    w_uq_r = w_uq_f[:, :, QK_NOPE:].reshape(ql, HEADS * QK_ROPE)

    inv = ROPE_THETA ** (-jnp.arange(0, QK_ROPE, 2, dtype=F32) / QK_ROPE)
    ang = positions[0].astype(F32)[:, None] * inv
    cos4 = jnp.tile(jnp.cos(ang), (1, 4))
    sin4 = jnp.tile(jnp.concatenate([-jnp.sin(ang), jnp.sin(ang)], axis=1), (1, 2))

    wm = (gm_w_s * jnp.tril(jnp.ones((CHUNK, CHUNK), F32))).astype(BF16)
    bs3 = gm_b_s.reshape(groups, CHUNK, 1)
    ln_g, ln_b = row(gm_ln_g), row(gm_ln_b)

    h1 = norm_mod_fwd("pre1_fwd", x2, row(pre_norm1_g), scale1, shift1)
    z = mm_nn("z_proj", h1, w_in_p, F32)
    a = gmlp_fwd(z, gmw, ln_g, ln_b, wm, bs3)
    y_a = mm_nn("branch_a", a, w_a_f, F32)
    qln = rms_fwd_cols("q_norm", z, o_q, ql, row(q_norm_g))
    kvn = rms_fwd_cols("kv_norm", z, o_kv, kvl, row(kv_norm_g))
    qn = mm_nn("q_nope", qln, w_uq_n, BF16)
    qp = mm_nn("q_rope", qln, w_uq_r, F32)
    kv = mm_nn_b3("kv_up", kvn, g_ukv, BF16)
    kpr = rope_k(z, o_kpe, cos4, sin4)
    o, qpr, lse = attn_fwd2(qn, qp, kv, kpr, cos4, sin4)
    y_b = mm_nn("branch_b", o, w_b_f, F32)
    merged = merge_fwd(z, o_ga, o_gb, y_a, y_b)
    y1 = mm_nn("out_proj", merged, w_out_f, F32)
    x1 = post_res_fwd("post1_fwd", x2, y1, gate1, row(post_norm1_g))
    h2 = norm_mod_fwd("pre2_fwd", x1, row(pre_norm2_g), scale2, shift2)
    upre = mm_nn_b3("up_proj", h2, g_up, F32)
    cw = g_cw.transpose(1, 0, 2).reshape(3, f2)
    cb = row(conv_b)
    f = conv_act_fwd(upre, cw, cb)
    ffn = mm_nn("down_proj", f, w_down_f, F32)
    loss_acc, dout, dffn, acc2 = post2_loss_bwd(x1, ffn, tgt, gate2, row(post_norm2_g))

    df = mm_nt("d_f", dffn, w_down_f, F32)
    gw_down = mm_tn("g_w_down", f, dffn, BF16)
    dup_g, dup_v, gcw_g, gcw_v, gcb_g, gcb_v = conv_act_bwd(upre, cw, cb, df)
    dupre = jnp.concatenate([dup_g, dup_v], axis=1)
    dh2 = mm_nt_b3("d_h2", dupre, g_up, F32)
    gw_up3 = mm_tn_o3("g_w_up", h2, dupre, N_DEV, BF16)
    dx1, dy1, acc_mid = mid_bwd(dh2, dout, x1, y1, row(pre_norm2_g), scale2, gate1, row(post_norm1_g))
    dmerged = mm_nt("d_merged", dy1, w_out_f, F32)
    gw_out = mm_tn("g_w_out", merged, dy1, BF16)
    dya, dyb, dga, dgb = merge_bwd(z, o_ga, o_gb, y_a, y_b, dmerged)
    da = mm_nt("d_a", dya, w_a_f, F32)
    gw_a = mm_tn("g_w_a", a, dya, BF16)
    do = mm_nt("d_o", dyb, w_b_f, F32)
    gw_b = mm_tn("g_w_b", o, dyb, BF16)
    duv, g_ws, g_bs3, acc_gm = gmlp_bwd(z, gmw, da, ln_g, ln_b, wm, bs3)
    dqn, dqp, dkv, dkp = attn_bwd2(qn, qpr, kv, kpr, o, do, lse, cos4, sin4)
    dkpe = kpe_bwd(dkp, cos4, sin4, zp - o_kpe)
    dq_cat = jnp.concatenate([dqn, dqp], axis=1)
    w_uq_cat = jnp.concatenate([w_uq_n, w_uq_r], axis=1)
    gw_uq_cat = mm_tn("g_w_uq", qln, dq_cat, BF16)
    dqln = mm_nt("d_qln", dq_cat, w_uq_cat, F32)
    dq_lat, g_qnorm = rms_bwd_cols("q_norm_bwd", dqln, z, o_q, ql, row(q_norm_g))
    gw_ukv3 = mm_tn_o3("g_w_ukv", kvn, dkv, N_DEV, BF16)
    dkvn = mm_nt_b3("d_kvn", dkv, g_ukv, F32)
    dkv_lat, g_kvnorm = rms_bwd_cols("kv_norm_bwd", dkvn, z, o_kv, kvl, row(kv_norm_g))
    dz = jnp.concatenate([duv, dq_lat, dkv_lat, dga, dgb, dkpe], axis=1)
    dh1 = mm_nt("d_h1", dz, w_in_p, F32)
    gw_in_p = mm_tn("g_w_in", h1, dz, BF16)
    grad_x, acc1 = pre1_bwd(dh1, dx1, x2, row(pre_norm1_g), scale1)

    gw_in_f = jnp.concatenate([gw_in_p[:, :src_kpe], gw_in_p[:, o_kpe:o_kpe + QK_ROPE], gw_in_p[:, src_kpe:o_kpe]], axis=1)
    gw_in3 = gw_in_f.reshape(d, N_DEV, -1).transpose(1, 0, 2)
    gw_uq_f = jnp.concatenate([gw_uq_cat[:, :HEADS * QK_NOPE].reshape(ql, HEADS, QK_NOPE),
                               gw_uq_cat[:, HEADS * QK_NOPE:].reshape(ql, HEADS, QK_ROPE)], axis=2)
    gw_uq3 = gw_uq_f.reshape(ql, N_DEV, -1).transpose(1, 0, 2)
    blocks = lambda g: g.reshape(N_DEV, g.shape[0] // N_DEV, g.shape[1])
    parts = exchange("grad_exchange", [gw_in3, blocks(gw_a), gw_uq3, gw_ukv3, blocks(gw_b), blocks(gw_out), gw_up3, blocks(gw_down)],
                     scatter=True)

    dmod = jnp.concatenate([acc1[0], acc1[1], acc_mid[3], acc_mid[0], acc_mid[1], acc2[0]])
    small = [("pre_norm1_g", acc1[2]), ("gm_ln_g", acc_gm[0]), ("gm_ln_b", acc_gm[1]), ("gm_b_s", g_bs3.reshape(-1)),
             ("q_norm_g", g_qnorm[0]), ("kv_norm_g", g_kvnorm[0]), ("post_norm1_g", acc_mid[4]), ("pre_norm2_g", acc_mid[2]),
             ("conv_b", jnp.concatenate([gcb_g[0], gcb_v[0]])), ("post_norm2_g", acc2[1]), ("gm_w_s", g_ws.reshape(-1)),
             ("b_ada", dmod)]
    n_small = sum(v.shape[0] for _, v in small)
    n_cw = 3 * f2
    n_pack = _round_up(n_small + n_cw, PACK_ALIGN)
    tail = jnp.zeros((n_pack - n_small - n_cw,), F32)
    packed = jnp.concatenate([v for _, v in small] + [jnp.concatenate([gcw_g, gcw_v], axis=1).reshape(-1), tail])
    (gathered,) = exchange("small_gather", [packed.reshape(-1, LANES)], scatter=False)

    def pack(src):
        return jnp.concatenate([src[k].reshape(-1) for k, _ in small] + [jnp.zeros((n_pack - n_small,), F32)]).reshape(-1, LANES)

    sm = [t.reshape(-1) for t in adamw("adamw_small", pack(weights), pack(mom1), pack(mom2), gathered)]
    res = {}
    off = 0
    for k, v in small:
        res[k] = tuple(t[off:off + v.shape[0]].reshape(weights[k].shape) for t in sm)
        off += v.shape[0]

    cs_cw = conv_w.shape[1]
    g_cw_full = sm[0][n_small:n_small + n_cw].reshape(3, f2)
    g_cw_mine = lax.dynamic_slice(g_cw_full, (0, my * cs_cw), (3, cs_cw))
    res["conv_w"] = adamw("adamw_conv_w", conv_w, mom1["conv_w"], mom2["conv_w"], g_cw_mine[None])

    cs_ada = w_ada.shape[1]
    off_b = n_small - N_MOD * d
    dmod_all = gathered.reshape(N_DEV, -1)[:, off_b:off_b + N_MOD * d]
    dmod_cols = lax.dynamic_slice(dmod_all, (0, my * cs_ada), (N_DEV, cs_ada))
    res["w_ada"] = adamw("adamw_w_ada", w_ada, mom1["w_ada"], mom2["w_ada"], ada_bwd_w(sc_all, dmod_cols))

    for k, p in zip(big, parts):
        res[k] = adamw("adamw_" + k, weights[k], mom1[k], mom2[k], p)

    loss = lax.psum(loss_acc[0, 0], ("x", "y", "c"))
    outs = [loss, grad_x[None]]
    for i in range(4):
        outs += [res[k][i] for k in order]
    return tuple(outs)
```

```python
import functools

import jax
import jax.numpy as jnp
from jax import lax
from jax.experimental import pallas as pl
from jax.experimental.pallas import tpu as pltpu

F32 = jnp.float32
BF16 = jnp.bfloat16

N_DEV = 8
HEADS = 16
QK_NOPE = 128
QK_ROPE = 64
V_HEAD = 128
CHUNK = 128
ROPE_THETA = 10000.0
EPS = 1e-6
N_MOD = 6
ADAM_LR, ADAM_B1, ADAM_B2, ADAM_EPS, ADAM_WD, ADAM_STEP = 0.001, 0.9, 0.999, 1e-08, 0.01, 10

LANES = 128
VMEM_LIMIT_BYTES = 48 * 2 ** 20
ROW_TILE = 256
COL_TILE = 256
ATT_TILE = 256
Z_PAD = 512
ADAMW_TILE_ELEMS = 1 << 18
PACK_ALIGN = 8 * LANES
MESH = pl.DeviceIdType.MESH


def _params(*sem):
    return pltpu.CompilerParams(dimension_semantics=sem if sem else None, vmem_limit_bytes=VMEM_LIMIT_BYTES)


def _tile(dim, target):
    t = (min(dim, target) // LANES) * LANES
    while t >= LANES:
        if dim % t == 0:
            return t
        t -= LANES
    return dim


def _full(shape):
    nd = len(shape)
    return pl.BlockSpec(shape, lambda *_: (0,) * nd)


def _matmul(name, a, b, *, grid, a_spec, b_spec, o_spec, out_shape, contract, acc_shape):
    nk = grid[2]

    def body(a_ref, b_ref, o_ref, acc_ref):
        k = pl.program_id(2)

        @pl.when(k == 0)
        def _():
            acc_ref[...] = jnp.zeros_like(acc_ref)

        acc_ref[...] += lax.dot_general(a_ref[...].astype(BF16), b_ref[...].astype(BF16),
                                        (contract, ((), ())), preferred_element_type=F32)

        @pl.when(k == nk - 1)
        def _():
            o_ref[...] = acc_ref[...].astype(o_ref.dtype)

    return pl.pallas_call(
        body, name=name, grid=grid, in_specs=[a_spec, b_spec], out_specs=o_spec, out_shape=out_shape,
        scratch_shapes=[pltpu.VMEM(acc_shape, F32)],
        compiler_params=_params("parallel", "parallel", "arbitrary"))(a, b)


TM, TN, TK = 1024, 1024, 512


def mm_nn(name, a, b, dtype):
    (m, k), n = a.shape, b.shape[1]
    tm, tn, tk = _tile(m, TM), _tile(n, TN), _tile(k, TK)
    return _matmul(name, a, b, grid=(m // tm, n // tn, k // tk),
                   a_spec=pl.BlockSpec((tm, tk), lambda i, j, kk: (i, kk)),
                   b_spec=pl.BlockSpec((tk, tn), lambda i, j, kk: (kk, j)),
                   o_spec=pl.BlockSpec((tm, tn), lambda i, j, kk: (i, j)),
                   out_shape=jax.ShapeDtypeStruct((m, n), dtype), contract=((1,), (0,)), acc_shape=(tm, tn))


def mm_nn_b3(name, a, b3, dtype):
    (m, k), (nj, _, cs) = a.shape, b3.shape
    tm, tk = _tile(m, TM), _tile(k, TK)
    return _matmul(name, a, b3, grid=(m // tm, nj, k // tk),
                   a_spec=pl.BlockSpec((tm, tk), lambda i, j, kk: (i, kk)),
                   b_spec=pl.BlockSpec((None, tk, cs), lambda i, j, kk: (j, kk, 0)),
                   o_spec=pl.BlockSpec((tm, cs), lambda i, j, kk: (i, j)),
                   out_shape=jax.ShapeDtypeStruct((m, nj * cs), dtype), contract=((1,), (0,)), acc_shape=(tm, cs))


def mm_nt(name, a, b, dtype):
    (m, k), n = a.shape, b.shape[0]
    tm, tn, tk = _tile(m, TM), _tile(n, TN), _tile(k, TK)
    return _matmul(name, a, b, grid=(m // tm, n // tn, k // tk),
                   a_spec=pl.BlockSpec((tm, tk), lambda i, j, kk: (i, kk)),
                   b_spec=pl.BlockSpec((tn, tk), lambda i, j, kk: (j, kk)),
                   o_spec=pl.BlockSpec((tm, tn), lambda i, j, kk: (i, j)),
                   out_shape=jax.ShapeDtypeStruct((m, n), dtype), contract=((1,), (1,)), acc_shape=(tm, tn))


def mm_nt_b3(name, a, b3, dtype):
    m, (nj, n, cs) = a.shape[0], b3.shape
    tm, tn = _tile(m, TM), _tile(n, TN)
    return _matmul(name, a, b3, grid=(m // tm, n // tn, nj),
                   a_spec=pl.BlockSpec((tm, cs), lambda i, j, kk: (i, kk)),
                   b_spec=pl.BlockSpec((None, tn, cs), lambda i, j, kk: (kk, j, 0)),
                   o_spec=pl.BlockSpec((tm, tn), lambda i, j, kk: (i, j)),
                   out_shape=jax.ShapeDtypeStruct((m, n), dtype), contract=((1,), (1,)), acc_shape=(tm, tn))


def mm_tn(name, a, b, dtype):
    (k, m), n = a.shape, b.shape[1]
    tm, tn, tk = _tile(m, TM), _tile(n, TN), _tile(k, TK)
    return _matmul(name, a, b, grid=(m // tm, n // tn, k // tk),
                   a_spec=pl.BlockSpec((tk, tm), lambda i, j, kk: (kk, i)),
                   b_spec=pl.BlockSpec((tk, tn), lambda i, j, kk: (kk, j)),
                   o_spec=pl.BlockSpec((tm, tn), lambda i, j, kk: (i, j)),
                   out_shape=jax.ShapeDtypeStruct((m, n), dtype), contract=((0,), (0,)), acc_shape=(tm, tn))


def mm_tn_o3(name, a, b, nj, dtype):
    (k, m), n = a.shape, b.shape[1]
    cs = n // nj
    tm, tk = _tile(m, TM), _tile(k, TK)
    return _matmul(name, a, b, grid=(m // tm, nj, k // tk),
                   a_spec=pl.BlockSpec((tk, tm), lambda i, j, kk: (kk, i)),
                   b_spec=pl.BlockSpec((tk, cs), lambda i, j, kk: (kk, j)),
                   o_spec=pl.BlockSpec((None, tm, cs), lambda i, j, kk: (j, i, 0)),
                   out_shape=jax.ShapeDtypeStruct((nj, m, cs), dtype), contract=((0,), (0,)), acc_shape=(tm, cs))


_GELU_C = 0.7978845608028654
_GELU_A = 0.044715


def _gelu(x):
    return 0.5 * x * (1.0 + jnp.tanh(_GELU_C * (x + _GELU_A * x * x * x)))


def _gelu_and_grad(x):
    t = jnp.tanh(_GELU_C * (x + _GELU_A * x * x * x))
    y = 0.5 * x * (1.0 + t)
    dy = 0.5 * (1.0 + t) + 0.5 * x * (1.0 - t * t) * (_GELU_C * (1.0 + 3.0 * _GELU_A * x * x))
    return y, dy


def _sigmoid(x):
    return 1.0 / (1.0 + jnp.exp(-x))


def _rms_stats(x):
    inv = lax.rsqrt(jnp.mean(x * x, axis=-1, keepdims=True) + EPS)
    return inv, x * inv


def _rms_bwd(dyhat, yhat, inv):
    return inv * (dyhat - yhat * jnp.mean(dyhat * yhat, axis=-1, keepdims=True))


def _colsum(x):
    return jnp.sum(x, axis=0, keepdims=True)


def _rope(x, cos4, sin4):
    lane = lax.broadcasted_iota(jnp.int32, x.shape, x.ndim - 1)
    first_half = (lane % QK_ROPE) < (QK_ROPE // 2)
    partner = jnp.where(first_half, pltpu.roll(x, LANES - QK_ROPE // 2, x.ndim - 1), pltpu.roll(x, QK_ROPE // 2, x.ndim - 1))
    return x * cos4 + partner * sin4


def norm_mod_fwd(name, x, g, scale, shift):
    s, d = x.shape
    tr = _tile(s, ROW_TILE)

    def body(x_ref, g_ref, sc_ref, sh_ref, o_ref):
        _, xh = _rms_stats(x_ref[...])
        o_ref[...] = (xh * g_ref[...] * (1.0 + sc_ref[...]) + sh_ref[...]).astype(o_ref.dtype)

    row = pl.BlockSpec((tr, d), lambda i: (i, 0))
    vec = pl.BlockSpec((1, d), lambda i: (0, 0))
    return pl.pallas_call(body, name=name, grid=(s // tr,), in_specs=[row, vec, vec, vec], out_specs=row,
                          out_shape=jax.ShapeDtypeStruct((s, d), BF16), compiler_params=_params("parallel"))(x, g, scale, shift)


def rms_fwd_cols(name, z, off, width, g):
    s = z.shape[0]
    tr = _tile(s, ROW_TILE)
    assert off % width == 0

    def body(x_ref, g_ref, o_ref):
        _, xh = _rms_stats(x_ref[...])
        o_ref[...] = (xh * g_ref[...]).astype(o_ref.dtype)

    return pl.pallas_call(body, name=name, grid=(s // tr,),
                          in_specs=[pl.BlockSpec((tr, width), lambda i: (i, off // width)), pl.BlockSpec((1, width), lambda i: (0, 0))],
                          out_specs=pl.BlockSpec((tr, width), lambda i: (i, 0)),
                          out_shape=jax.ShapeDtypeStruct((s, width), BF16), compiler_params=_params("parallel"))(z, g)


def rms_bwd_cols(name, dy, z, off, width, g):
    s = z.shape[0]
    tr = _tile(s, ROW_TILE)

    def body(dy_ref, x_ref, g_ref, dx_ref, gg_ref):
        @pl.when(pl.program_id(0) == 0)
        def _():
            gg_ref[...] = jnp.zeros_like(gg_ref)

        inv, xh = _rms_stats(x_ref[...])
        dy_v = dy_ref[...]
        gg_ref[...] += _colsum(dy_v * xh)
        dx_ref[...] = _rms_bwd(dy_v * g_ref[...], xh, inv).astype(dx_ref.dtype)

    return pl.pallas_call(body, name=name, grid=(s // tr,),
                          in_specs=[pl.BlockSpec((tr, width), lambda i: (i, 0)), pl.BlockSpec((tr, width), lambda i: (i, off // width)),
                                    pl.BlockSpec((1, width), lambda i: (0, 0))],
                          out_specs=[pl.BlockSpec((tr, width), lambda i: (i, 0)), pl.BlockSpec((1, width), lambda i: (0, 0))],
                          out_shape=[jax.ShapeDtypeStruct((s, width), BF16), jax.ShapeDtypeStruct((1, width), F32)],
                          compiler_params=_params("arbitrary"))(dy, z, g)


def post_res_fwd(name, x, y, gate, g):
    s, d = x.shape
    tr = _tile(s, ROW_TILE)

    def body(x_ref, y_ref, gate_ref, g_ref, o_ref):
        _, yh = _rms_stats(y_ref[...])
        o_ref[...] = x_ref[...] + gate_ref[...] * (yh * g_ref[...])

    row = pl.BlockSpec((tr, d), lambda i: (i, 0))
    vec = pl.BlockSpec((1, d), lambda i: (0, 0))
    return pl.pallas_call(body, name=name, grid=(s // tr,), in_specs=[row, row, vec, vec], out_specs=row,
                          out_shape=jax.ShapeDtypeStruct((s, d), F32), compiler_params=_params("parallel"))(x, y, gate, g)


def post2_loss_bwd(x1, ffn, target, gate2, g):
    s, d = x1.shape
    tr = _tile(s, ROW_TILE)

    def body(x_ref, y_ref, t_ref, gate_ref, g_ref, loss_ref, dout_ref, dy_ref, acc_ref):
        @pl.when(pl.program_id(0) == 0)
        def _():
            loss_ref[...] = jnp.zeros_like(loss_ref)
            acc_ref[...] = jnp.zeros_like(acc_ref)

        inv, yh = _rms_stats(y_ref[...])
        r = yh * g_ref[...]
        err = x_ref[...] + gate_ref[...] * r - t_ref[...]
        loss_ref[...] += 0.5 * jnp.sum(jnp.mean(err * err, axis=-1, keepdims=True))
        dout = err / d
        dout_ref[...] = dout
        dr = dout * gate_ref[...]
        acc_ref[0:1, :] += _colsum(dout * r)
        acc_ref[1:2, :] += _colsum(dr * yh)
        dy_ref[...] = _rms_bwd(dr * g_ref[...], yh, inv).astype(dy_ref.dtype)

    row = pl.BlockSpec((tr, d), lambda i: (i, 0))
    vec = pl.BlockSpec((1, d), lambda i: (0, 0))
    return pl.pallas_call(
        body, name="post2_loss_bwd", grid=(s // tr,), in_specs=[row, row, row, vec, vec],
        out_specs=[_full((8, LANES)), row, row, _full((8, d))],
        out_shape=[jax.ShapeDtypeStruct((8, LANES), F32), jax.ShapeDtypeStruct((s, d), F32),
                   jax.ShapeDtypeStruct((s, d), BF16), jax.ShapeDtypeStruct((8, d), F32)],
        compiler_params=_params("arbitrary"))(x1, ffn, target, gate2, g)


def mid_bwd(dh2, dout, x1, y1, pre2_g, scale2, gate1, post1_g):
    s, d = x1.shape
    tr = _tile(s, ROW_TILE)

    def body(dh_ref, dout_ref, x_ref, y_ref, g2_ref, sc_ref, gate_ref, g1_ref, dx_ref, dy_ref, acc_ref):
        @pl.when(pl.program_id(0) == 0)
        def _():
            acc_ref[...] = jnp.zeros_like(acc_ref)

        dh = dh_ref[...]
        inv2, xh = _rms_stats(x_ref[...])
        acc_ref[0:1, :] += _colsum(dh)
        acc_ref[1:2, :] += _colsum(dh * (xh * g2_ref[...]))
        t = dh * (1.0 + sc_ref[...])
        acc_ref[2:3, :] += _colsum(t * xh)
        dx1 = dout_ref[...] + _rms_bwd(t * g2_ref[...], xh, inv2)
        dx_ref[...] = dx1
        inv1, yh = _rms_stats(y_ref[...])
        acc_ref[3:4, :] += _colsum(dx1 * (yh * g1_ref[...]))
        dr = dx1 * gate_ref[...]
        acc_ref[4:5, :] += _colsum(dr * yh)
        dy_ref[...] = _rms_bwd(dr * g1_ref[...], yh, inv1).astype(dy_ref.dtype)

    row = pl.BlockSpec((tr, d), lambda i: (i, 0))
    vec = pl.BlockSpec((1, d), lambda i: (0, 0))
    return pl.pallas_call(
        body, name="mid_bwd", grid=(s // tr,), in_specs=[row, row, row, row, vec, vec, vec, vec],
        out_specs=[row, row, _full((8, d))],
        out_shape=[jax.ShapeDtypeStruct((s, d), F32), jax.ShapeDtypeStruct((s, d), BF16), jax.ShapeDtypeStruct((8, d), F32)],
        compiler_params=_params("arbitrary"))(dh2, dout, x1, y1, pre2_g, scale2, gate1, post1_g)


def pre1_bwd(dh1, dx1, x, pre1_g, scale1):
    s, d = x.shape
    tr = _tile(s, ROW_TILE)

    def body(dh_ref, dx1_ref, x_ref, g_ref, sc_ref, dx_ref, acc_ref):
        @pl.when(pl.program_id(0) == 0)
        def _():
            acc_ref[...] = jnp.zeros_like(acc_ref)

        dh = dh_ref[...]
        inv, xh = _rms_stats(x_ref[...])
        acc_ref[0:1, :] += _colsum(dh)
        acc_ref[1:2, :] += _colsum(dh * (xh * g_ref[...]))
        t = dh * (1.0 + sc_ref[...])
        acc_ref[2:3, :] += _colsum(t * xh)
        dx_ref[...] = dx1_ref[...] + _rms_bwd(t * g_ref[...], xh, inv)

    row = pl.BlockSpec((tr, d), lambda i: (i, 0))
    vec = pl.BlockSpec((1, d), lambda i: (0, 0))
    return pl.pallas_call(
        body, name="pre1_bwd", grid=(s // tr,), in_specs=[row, row, row, vec, vec], out_specs=[row, _full((8, d))],
        out_shape=[jax.ShapeDtypeStruct((s, d), F32), jax.ShapeDtypeStruct((8, d), F32)],
        compiler_params=_params("arbitrary"))(dh1, dx1, x, pre1_g, scale1)


def _ln_stats(v):
    mu = jnp.mean(v, axis=-1, keepdims=True)
    vc = v - mu
    rstd = lax.rsqrt(jnp.mean(vc * vc, axis=-1, keepdims=True) + EPS)
    return rstd, vc * rstd


def gmlp_fwd(z, width, ln_g, ln_b, wm, bs3):
    s = z.shape[0]
    groups = width // CHUNK

    def body(u_ref, v_ref, g_ref, b_ref, wm_ref, bs_ref, a_ref):
        ug = _gelu(u_ref[...])
        _, vh = _ln_stats(_gelu(v_ref[...]))
        vn = (vh * g_ref[...] + b_ref[...]).astype(BF16)
        for g in range(groups):
            cols = slice(g * CHUNK, (g + 1) * CHUNK)
            mixed = jnp.dot(wm_ref[g], vn[:, cols], preferred_element_type=F32) + bs_ref[g]
            a_ref[:, cols] = (ug[:, cols] * mixed).astype(a_ref.dtype)

    vec = pl.BlockSpec((1, width), lambda n: (0, 0))
    return pl.pallas_call(
        body, name="gmlp_fwd", grid=(s // CHUNK,),
        in_specs=[pl.BlockSpec((CHUNK, width), lambda n: (n, 0)), pl.BlockSpec((CHUNK, width), lambda n: (n, 1)), vec, vec,
                  _full(wm.shape), _full(bs3.shape)],
        out_specs=pl.BlockSpec((CHUNK, width), lambda n: (n, 0)),
        out_shape=jax.ShapeDtypeStruct((s, width), BF16), compiler_params=_params("parallel"))(z, z, ln_g, ln_b, wm, bs3)


def gmlp_bwd(z, width, da, ln_g, ln_b, wm, bs3):
    s = z.shape[0]
    groups = width // CHUNK

    def body(u_ref, v_ref, da_ref, g_ref, b_ref, wm_ref, bs_ref, duv_ref, gw_ref, gb_ref, acc_ref, dvn_ref):
        @pl.when(pl.program_id(0) == 0)
        def _():
            gw_ref[...] = jnp.zeros_like(gw_ref)
            gb_ref[...] = jnp.zeros_like(gb_ref)
            acc_ref[...] = jnp.zeros_like(acc_ref)

        ug, dug = _gelu_and_grad(u_ref[...])
        vg, dvg = _gelu_and_grad(v_ref[...])
        rstd, vh = _ln_stats(vg)
        vn = (vh * g_ref[...] + b_ref[...]).astype(BF16)
        da_v = da_ref[...]
        for g in range(groups):
            cols = slice(g * CHUNK, (g + 1) * CHUNK)
            mixed = jnp.dot(wm_ref[g], vn[:, cols], preferred_element_type=F32) + bs_ref[g]
            duv_ref[:, cols] = (da_v[:, cols] * mixed * dug[:, cols]).astype(duv_ref.dtype)
            dm = da_v[:, cols] * ug[:, cols]
            gb_ref[g] += jnp.sum(dm, axis=-1, keepdims=True)
            dmb = dm.astype(BF16)
            gw_ref[g] += lax.dot_general(dmb, vn[:, cols], (((1,), (1,)), ((), ())), preferred_element_type=F32)
            dvn_ref[:, cols] = lax.dot_general(wm_ref[g], dmb, (((0,), (0,)), ((), ())), preferred_element_type=F32)
        dvn = dvn_ref[...]
        acc_ref[0:1, :] += _colsum(dvn * vh)
        acc_ref[1:2, :] += _colsum(dvn)
        dvh = dvn * g_ref[...]
        dv = rstd * (dvh - jnp.mean(dvh, axis=-1, keepdims=True) - vh * jnp.mean(dvh * vh, axis=-1, keepdims=True))
        duv_ref[:, width:] = (dv * dvg).astype(duv_ref.dtype)

        @pl.when(pl.program_id(0) == pl.num_programs(0) - 1)
        def _():
            q = lax.broadcasted_iota(jnp.int32, gw_ref.shape, 1)
            p = lax.broadcasted_iota(jnp.int32, gw_ref.shape, 2)
            gw_ref[...] = jnp.where(p <= q, gw_ref[...], 0.0)

    vec = pl.BlockSpec((1, width), lambda n: (0, 0))
    blk = pl.BlockSpec((CHUNK, width), lambda n: (n, 0))
    return pl.pallas_call(
        body, name="gmlp_bwd", grid=(s // CHUNK,),
        in_specs=[blk, pl.BlockSpec((CHUNK, width), lambda n: (n, 1)), blk, vec, vec, _full(wm.shape), _full(bs3.shape)],
        out_specs=[pl.BlockSpec((CHUNK, 2 * width), lambda n: (n, 0)), _full(wm.shape), _full(bs3.shape), _full((8, width))],
        out_shape=[jax.ShapeDtypeStruct((s, 2 * width), BF16), jax.ShapeDtypeStruct(wm.shape, F32),
                   jax.ShapeDtypeStruct(bs3.shape, F32), jax.ShapeDtypeStruct((8, width), F32)],
        scratch_shapes=[pltpu.VMEM((CHUNK, width), F32)],
        compiler_params=_params("arbitrary"))(z, z, da, ln_g, ln_b, wm, bs3)


def merge_fwd(z, off_a, off_b, ya, yb):
    s, d = ya.shape
    tr, tc = _tile(s, ROW_TILE * 2), _tile(d, COL_TILE)
    assert off_a % tc == 0 and off_b % tc == 0

    def body(ga_ref, gb_ref, ya_ref, yb_ref, o_ref):
        o_ref[...] = (_sigmoid(ga_ref[...]) * ya_ref[...] + _sigmoid(gb_ref[...]) * yb_ref[...]).astype(o_ref.dtype)

    blk = pl.BlockSpec((tr, tc), lambda i, j: (i, j))
    return pl.pallas_call(
        body, name="merge_fwd", grid=(s // tr, d // tc),
        in_specs=[pl.BlockSpec((tr, tc), lambda i, j: (i, off_a // tc + j)), pl.BlockSpec((tr, tc), lambda i, j: (i, off_b // tc + j)), blk, blk],
        out_specs=blk, out_shape=jax.ShapeDtypeStruct((s, d), BF16), compiler_params=_params("parallel", "parallel"))(z, z, ya, yb)


def merge_bwd(z, off_a, off_b, ya, yb, dm):
    s, d = ya.shape
    tr, tc = _tile(s, ROW_TILE * 2), _tile(d, COL_TILE)
    nc = d // tc

    def body(ga_ref, gb_ref, ya_ref, yb_ref, dm_ref, dya_ref, dyb_ref, dga_ref, dgb_ref):
        dm_v = dm_ref[...]
        sa, sb = _sigmoid(ga_ref[...]), _sigmoid(gb_ref[...])
        dya_ref[...] = (dm_v * sa).astype(dya_ref.dtype)
        dyb_ref[...] = (dm_v * sb).astype(dyb_ref.dtype)
        dga_ref[...] = (dm_v * ya_ref[...] * sa * (1.0 - sa)).astype(dga_ref.dtype)
        dgb_ref[...] = (dm_v * yb_ref[...] * sb * (1.0 - sb)).astype(dgb_ref.dtype)

    blk = pl.BlockSpec((tr, tc), lambda i, j: (i, j))
    out = jax.ShapeDtypeStruct((s, d), BF16)
    return pl.pallas_call(
        body, name="merge_bwd", grid=(s // tr, nc),
        in_specs=[pl.BlockSpec((tr, tc), lambda i, j: (i, off_a // tc + j)), pl.BlockSpec((tr, tc), lambda i, j: (i, off_b // tc + j)), blk, blk, blk],
        out_specs=[blk, blk, blk, blk], out_shape=[out, out, out, out],
        compiler_params=_params("parallel", "parallel"))(z, z, ya, yb, dm)


_ATT_SCALE = (QK_NOPE + QK_ROPE) ** -0.5
_NEG = -1e30


def rope_k(z, off, cos4, sin4):
    s = z.shape[0]
    tr = _tile(s, ROW_TILE * 2)
    assert off % LANES == 0

    def body(k_ref, c_ref, s_ref, o_ref):
        k = k_ref[...]
        k = k + pltpu.roll(k, QK_ROPE, 1)
        o_ref[...] = _rope(k, c_ref[...], s_ref[...]).astype(o_ref.dtype)

    row = pl.BlockSpec((tr, LANES), lambda i: (i, 0))
    return pl.pallas_call(body, name="rope_k", grid=(s // tr,),
                          in_specs=[pl.BlockSpec((tr, LANES), lambda i: (i, off // LANES)), row, row], out_specs=row,
                          out_shape=jax.ShapeDtypeStruct((s, LANES), BF16), compiler_params=_params("parallel"))(z, cos4, sin4)


def _head_masks(shape):
    lane = lax.broadcasted_iota(jnp.int32, shape, 1)
    return lane < QK_ROPE, lane >= QK_ROPE


def _scores(qn, qp_h, k, kp, qi, kb, t):
    sc = lax.dot_general(qn, k, (((1,), (1,)), ((), ())), preferred_element_type=F32)
    sc += lax.dot_general(qp_h, kp, (((1,), (1,)), ((), ())), preferred_element_type=F32)
    sc = sc * _ATT_SCALE
    row = lax.broadcasted_iota(jnp.int32, sc.shape, 0) + qi * t
    col = lax.broadcasted_iota(jnp.int32, sc.shape, 1) + kb * t
    return jnp.where(col <= row, sc, _NEG)


def attn_fwd(qn, qp, kv, kpr, cos4, sin4):
    s = qn.shape[0]
    hp = HEADS // 2
    t = _tile(s, ATT_TILE)
    nq = s // t

    def body(qn_ref, qp_ref, kv_ref, kp_ref, c_ref, s_ref, o_ref, qpr_ref, l_ref):
        qi = pl.program_id(1)
        qpr = _rope(qp_ref[...], c_ref[...], s_ref[...]).astype(BF16)
        qpr_ref[...] = qpr
        masks = _head_masks(qpr.shape)
        for hh in range(2):
            q_n = qn_ref[:, hh * QK_NOPE:(hh + 1) * QK_NOPE]
            q_p = jnp.where(masks[hh], qpr, jnp.zeros_like(qpr))
            kc, vc = 2 * hh * QK_NOPE, (2 * hh + 1) * QK_NOPE

            def step(kb, carry):
                m, l, acc = carry
                rows = pl.ds(pl.multiple_of(kb * t, t), t)
                sc = _scores(q_n, q_p, kv_ref[rows, kc:kc + QK_NOPE], kp_ref[rows, :], qi, kb, t)
                m_new = jnp.maximum(m, jnp.max(sc, axis=-1, keepdims=True))
                alpha = jnp.exp(m - m_new)
                p = jnp.exp(sc - m_new)
                l = alpha * l + jnp.sum(p, axis=-1, keepdims=True)
                acc = alpha * acc + jnp.dot(p.astype(BF16), kv_ref[rows, vc:vc + V_HEAD], preferred_element_type=F32)
                return m_new, l, acc

            init = (jnp.full((t, 1), _NEG, F32), jnp.zeros((t, 1), F32), jnp.zeros((t, V_HEAD), F32))
            m, l, acc = lax.fori_loop(0, qi + 1, step, init)
            o_ref[:, hh * V_HEAD:(hh + 1) * V_HEAD] = acc / l
            l_ref[:, hh:hh + 1] = m + jnp.log(l)

    return pl.pallas_call(
        body, name="attn_fwd", grid=(hp, nq),
        in_specs=[pl.BlockSpec((t, 2 * QK_NOPE), lambda h, i: (i, h)), pl.BlockSpec((t, LANES), lambda h, i: (i, h)),
                  pl.BlockSpec((s, 4 * QK_NOPE), lambda h, i: (0, h)), _full((s, LANES)),
                  pl.BlockSpec((t, LANES), lambda h, i: (i, 0)), pl.BlockSpec((t, LANES), lambda h, i: (i, 0))],
        out_specs=[pl.BlockSpec((t, 2 * V_HEAD), lambda h, i: (i, h)), pl.BlockSpec((t, LANES), lambda h, i: (i, h)),
                   pl.BlockSpec((None, t, 2), lambda h, i: (h, i, 0))],
        out_shape=[jax.ShapeDtypeStruct((s, HEADS * V_HEAD), F32), jax.ShapeDtypeStruct((s, HEADS * QK_ROPE), BF16),
                   jax.ShapeDtypeStruct((hp, s, 2), F32)],
        compiler_params=_params("parallel", "parallel"))(qn, qp, kv, kpr, cos4, sin4)


def attn_bwd_q(qn, qpr, kv, kpr, o, do, lse, cos4, sin4):
    s = qn.shape[0]
    hp = HEADS // 2
    t = _tile(s, ATT_TILE)
    nq = s // t

    def body(qn_ref, qpr_ref, kv_ref, kp_ref, o_ref, do_ref, l_ref, c_ref, s_ref, dqn_ref, dqp_ref):
        qi = pl.program_id(1)
        qpr = qpr_ref[...]
        masks = _head_masks(qpr.shape)
        dqp = jnp.zeros(qpr.shape, F32)
        for hh in range(2):
            q_n = qn_ref[:, hh * QK_NOPE:(hh + 1) * QK_NOPE]
            q_p = jnp.where(masks[hh], qpr, jnp.zeros_like(qpr))
            kc, vc = 2 * hh * QK_NOPE, (2 * hh + 1) * QK_NOPE
            do_h = do_ref[:, hh * V_HEAD:(hh + 1) * V_HEAD]
            delta = jnp.sum(do_h * o_ref[:, hh * V_HEAD:(hh + 1) * V_HEAD], axis=-1, keepdims=True)
            do_b = do_h.astype(BF16)
            lse_h = l_ref[:, hh:hh + 1]

            def step(kb, carry):
                dn, dp_ = carry
                rows = pl.ds(pl.multiple_of(kb * t, t), t)
                k = kv_ref[rows, kc:kc + QK_NOPE]
                kp = kp_ref[rows, :]
                p = jnp.exp(_scores(q_n, q_p, k, kp, qi, kb, t) - lse_h)
                dpv = lax.dot_general(do_b, kv_ref[rows, vc:vc + V_HEAD], (((1,), (1,)), ((), ())), preferred_element_type=F32)
                ds = (p * (dpv - delta) * _ATT_SCALE).astype(BF16)
                dn = dn + jnp.dot(ds, k, preferred_element_type=F32)
                dp_ = dp_ + jnp.dot(ds, kp, preferred_element_type=F32)
                return dn, dp_

            dn, dp_h = lax.fori_loop(0, qi + 1, step, (jnp.zeros((t, QK_NOPE), F32), jnp.zeros((t, LANES), F32)))
            dqn_ref[:, hh * QK_NOPE:(hh + 1) * QK_NOPE] = dn.astype(dqn_ref.dtype)
            dqp = dqp + jnp.where(masks[hh], dp_h, jnp.zeros_like(dp_h))
        dqp_ref[...] = _rope(dqp, c_ref[...], -s_ref[...]).astype(dqp_ref.dtype)

    qblk = pl.BlockSpec((t, 2 * QK_NOPE), lambda h, i: (i, h))
    pblk = pl.BlockSpec((t, LANES), lambda h, i: (i, h))
    tab = pl.BlockSpec((t, LANES), lambda h, i: (i, 0))
    return pl.pallas_call(
        body, name="attn_bwd_q", grid=(hp, nq),
        in_specs=[qblk, pblk, pl.BlockSpec((s, 4 * QK_NOPE), lambda h, i: (0, h)), _full((s, LANES)), qblk, qblk,
                  pl.BlockSpec((None, t, 2), lambda h, i: (h, i, 0)), tab, tab],
        out_specs=[qblk, pblk],
        out_shape=[jax.ShapeDtypeStruct((s, HEADS * QK_NOPE), BF16), jax.ShapeDtypeStruct((s, HEADS * QK_ROPE), BF16)],
        compiler_params=_params("parallel", "parallel"))(qn, qpr, kv, kpr, o, do, lse, cos4, sin4)


def attn_bwd_kv(qn, qpr, kv, kpr, o, do, lse):
    s = qn.shape[0]
    hp = HEADS // 2
    t = _tile(s, ATT_TILE)
    nq = s // t

    def body(qn_ref, qpr_ref, kv_ref, kp_ref, o_ref, do_ref, l_ref, dkv_ref, dkp_ref):
        ki = pl.program_id(1)
        rows_k = pl.ds(pl.multiple_of(ki * t, t), t)
        kp = kp_ref[rows_k, :]
        dkp = jnp.zeros((t, LANES), F32)
        for hh in range(2):
            kc, vc = 2 * hh * QK_NOPE, (2 * hh + 1) * QK_NOPE
            k = kv_ref[rows_k, kc:kc + QK_NOPE]
            v = kv_ref[rows_k, vc:vc + V_HEAD]

            def step(qb, carry):
                dk, dv, dkp_h = carry
                rows = pl.ds(pl.multiple_of(qb * t, t), t)
                q_n = qn_ref[rows, hh * QK_NOPE:(hh + 1) * QK_NOPE]
                qpr = qpr_ref[rows, :]
                lane = lax.broadcasted_iota(jnp.int32, qpr.shape, 1)
                sel = (lane < QK_ROPE) if hh == 0 else (lane >= QK_ROPE)
                q_p = jnp.where(sel, qpr, jnp.zeros_like(qpr))
                do_h = do_ref[rows, hh * V_HEAD:(hh + 1) * V_HEAD]
                delta = jnp.sum(do_h * o_ref[rows, hh * V_HEAD:(hh + 1) * V_HEAD], axis=-1, keepdims=True)
                do_b = do_h.astype(BF16)
                p = jnp.exp(_scores(q_n, q_p, k, kp, qb, ki, t) - l_ref[rows, hh:hh + 1])
                dpv = lax.dot_general(do_b, v, (((1,), (1,)), ((), ())), preferred_element_type=F32)
                ds = (p * (dpv - delta) * _ATT_SCALE).astype(BF16)
                dv = dv + lax.dot_general(p.astype(BF16), do_b, (((0,), (0,)), ((), ())), preferred_element_type=F32)
                dk = dk + lax.dot_general(ds, q_n, (((0,), (0,)), ((), ())), preferred_element_type=F32)
                dkp_h = dkp_h + lax.dot_general(ds, q_p, (((0,), (0,)), ((), ())), preferred_element_type=F32)
                return dk, dv, dkp_h

            init = (jnp.zeros((t, QK_NOPE), F32), jnp.zeros((t, V_HEAD), F32), jnp.zeros((t, LANES), F32))
            dk, dv, dkp_h = lax.fori_loop(ki, nq, step, init)
            dkv_ref[:, kc:kc + QK_NOPE] = dk.astype(dkv_ref.dtype)
            dkv_ref[:, vc:vc + V_HEAD] = dv.astype(dkv_ref.dtype)
            dkp = dkp + dkp_h
        dkp_ref[...] = dkp

    return pl.pallas_call(
        body, name="attn_bwd_kv", grid=(hp, nq),
        in_specs=[pl.BlockSpec((s, 2 * QK_NOPE), lambda h, i: (0, h)), pl.BlockSpec((s, LANES), lambda h, i: (0, h)),
                  pl.BlockSpec((s, 4 * QK_NOPE), lambda h, i: (0, h)), _full((s, LANES)),
                  pl.BlockSpec((s, 2 * V_HEAD), lambda h, i: (0, h)), pl.BlockSpec((s, 2 * V_HEAD), lambda h, i: (0, h)),
                  pl.BlockSpec((None, s, 2), lambda h, i: (h, 0, 0))],
        out_specs=[pl.BlockSpec((t, 4 * QK_NOPE), lambda h, i: (i, h)), pl.BlockSpec((None, t, LANES), lambda h, i: (h, i, 0))],
        out_shape=[jax.ShapeDtypeStruct((s, HEADS * 2 * QK_NOPE), BF16), jax.ShapeDtypeStruct((hp, s, LANES), F32)],
        compiler_params=_params("parallel", "parallel"))(qn, qpr, kv, kpr, o, do, lse)


def _dot_nt(a, b):
    return lax.dot_general(a, b, (((1,), (1,)), ((), ())), preferred_element_type=F32)


def _dot_tn(a, b):
    return lax.dot_general(a, b, (((0,), (0,)), ((), ())), preferred_element_type=F32)


def _q_cat(q_n, qpr, hh):
    lane = lax.broadcasted_iota(jnp.int32, qpr.shape, 1)
    sel = (lane < QK_ROPE) if hh == 0 else (lane >= QK_ROPE)
    return jnp.concatenate([q_n, jnp.where(sel, qpr, jnp.zeros_like(qpr))], axis=1)


def _causal(sc):
    row = lax.broadcasted_iota(jnp.int32, sc.shape, 0)
    col = lax.broadcasted_iota(jnp.int32, sc.shape, 1)
    return jnp.where(col <= row, sc, _NEG)


def attn_fwd2(qn, qp, kv, kpr, cos4, sin4):
    s = qn.shape[0]
    hp = HEADS // 2
    t = _tile(s, ATT_TILE)
    nq = s // t

    def body(qn_ref, qp_ref, kv_ref, kp_ref, c_ref, s_ref, o_ref, qpr_ref, l_ref, kcat_ref):
        qi = pl.program_id(1)

        @pl.when(qi == 0)
        def _():
            for hh in range(2):
                kcat_ref[hh, :, 0:QK_NOPE] = kv_ref[:, 2 * hh * QK_NOPE:(2 * hh + 1) * QK_NOPE]
                kcat_ref[hh, :, QK_NOPE:] = kp_ref[...]

        qpr = _rope(qp_ref[...], c_ref[...], s_ref[...]).astype(BF16)
        qpr_ref[...] = qpr
        qcat = [_q_cat(qn_ref[:, hh * QK_NOPE:(hh + 1) * QK_NOPE], qpr, hh) for hh in range(2)]

        def block(kb, carry, diagonal):
            rows = pl.ds(pl.multiple_of(kb * t, t), t)
            out = []
            for hh in range(2):
                m, l, acc = carry[hh]
                sc = _dot_nt(qcat[hh], kcat_ref[hh, rows, :]) * _ATT_SCALE
                if diagonal:
                    sc = _causal(sc)
                m_new = jnp.maximum(m, jnp.max(sc, axis=-1, keepdims=True))
                alpha = jnp.exp(m - m_new)
                p = jnp.exp(sc - m_new)
                l = alpha * l + jnp.sum(p, axis=-1, keepdims=True)
                v = kv_ref[rows, (2 * hh + 1) * QK_NOPE:(2 * hh + 2) * QK_NOPE]
                acc = alpha * acc + jnp.dot(p.astype(BF16), v, preferred_element_type=F32)
                out.append((m_new, l, acc))
            return tuple(out)

        one = (jnp.full((t, 1), _NEG, F32), jnp.zeros((t, 1), F32), jnp.zeros((t, V_HEAD), F32))
        carry = lax.fori_loop(0, qi, lambda kb, cr: block(kb, cr, False), (one, one))
        carry = block(qi, carry, True)
        for hh in range(2):
            m, l, acc = carry[hh]
            o_ref[:, hh * V_HEAD:(hh + 1) * V_HEAD] = acc / l
            l_ref[:, hh:hh + 1] = m + jnp.log(l)

    return pl.pallas_call(
        body, name="attn_fwd", grid=(hp, nq),
        in_specs=[pl.BlockSpec((t, 2 * QK_NOPE), lambda h, i: (i, h)), pl.BlockSpec((t, LANES), lambda h, i: (i, h)),
                  pl.BlockSpec((s, 4 * QK_NOPE), lambda h, i: (0, h)), _full((s, LANES)),
                  pl.BlockSpec((t, LANES), lambda h, i: (i, 0)), pl.BlockSpec((t, LANES), lambda h, i: (i, 0))],
        out_specs=[pl.BlockSpec((t, 2 * V_HEAD), lambda h, i: (i, h)), pl.BlockSpec((t, LANES), lambda h, i: (i, h)),
                   pl.BlockSpec((None, t, 2), lambda h, i: (h, i, 0))],
        out_shape=[jax.ShapeDtypeStruct((s, HEADS * V_HEAD), F32), jax.ShapeDtypeStruct((s, HEADS * QK_ROPE), BF16),
                   jax.ShapeDtypeStruct((hp, s, 2), F32)],
        scratch_shapes=[pltpu.VMEM((2, s, 2 * QK_NOPE), BF16)],
        compiler_params=_params("parallel", "arbitrary"))(qn, qp, kv, kpr, cos4, sin4)


def attn_bwd2(qn, qpr, kv, kpr, o, do, lse, cos4, sin4):
    s = qn.shape[0]
    hp = HEADS // 2
    t = _tile(s, ATT_TILE)
    nk = s // t

    def body(qn_ref, qpr_ref, kv_ref, kp_ref, o_ref, do_ref, l_ref, c_ref, s_ref,
             dqn_ref, dqp_ref, dkv_ref, dkp_ref, qcat_ref, dq_ref, delta_ref):
        ki = pl.program_id(1)

        @pl.when(ki == 0)
        def _():
            dq_ref[...] = jnp.zeros_like(dq_ref)
            for hh in range(2):
                qcat_ref[hh] = _q_cat(qn_ref[:, hh * QK_NOPE:(hh + 1) * QK_NOPE], qpr_ref[...], hh)
                cols = slice(hh * V_HEAD, (hh + 1) * V_HEAD)
                delta_ref[hh] = jnp.sum(do_ref[:, cols] * o_ref[:, cols], axis=-1, keepdims=True)

        rows_k = pl.ds(pl.multiple_of(ki * t, t), t)
        kcat = [jnp.concatenate([kv_ref[rows_k, 2 * hh * QK_NOPE:(2 * hh + 1) * QK_NOPE], kp_ref[rows_k, :]], axis=1) for hh in range(2)]
        vs = [kv_ref[rows_k, (2 * hh + 1) * QK_NOPE:(2 * hh + 2) * QK_NOPE] for hh in range(2)]

        def block(qb, carry, diagonal):
            rows = pl.ds(pl.multiple_of(qb * t, t), t)
            out = []
            for hh in range(2):
                dkc, dv = carry[hh]
                q_c = qcat_ref[hh, rows, :]
                do_b = do_ref[rows, hh * V_HEAD:(hh + 1) * V_HEAD].astype(BF16)
                sc = _dot_nt(q_c, kcat[hh]) * _ATT_SCALE
                if diagonal:
                    sc = _causal(sc)
                p = jnp.exp(sc - l_ref[rows, hh:hh + 1])
                dpv = _dot_nt(do_b, vs[hh])
                ds = (p * (dpv - delta_ref[hh, rows, :]) * _ATT_SCALE).astype(BF16)
                dv = dv + _dot_tn(p.astype(BF16), do_b)
                dkc = dkc + _dot_tn(ds, q_c)
                dq_ref[hh, rows, :] += jnp.dot(ds, kcat[hh], preferred_element_type=F32)
                out.append((dkc, dv))
            return tuple(out)

        one = (jnp.zeros((t, 2 * QK_NOPE), F32), jnp.zeros((t, V_HEAD), F32))
        carry = block(ki, (one, one), True)
        carry = lax.fori_loop(ki + 1, nk, lambda qb, cr: block(qb, cr, False), carry)
        dkp = jnp.zeros((t, LANES), F32)
        for hh in range(2):
            dkc, dv = carry[hh]
            dkv_ref[:, 2 * hh * QK_NOPE:(2 * hh + 1) * QK_NOPE] = dkc[:, :QK_NOPE].astype(dkv_ref.dtype)
            dkv_ref[:, (2 * hh + 1) * QK_NOPE:(2 * hh + 2) * QK_NOPE] = dv.astype(dkv_ref.dtype)
            dkp = dkp + dkc[:, QK_NOPE:]
        dkp_ref[...] = dkp

        @pl.when(ki == nk - 1)
        def _():
            lane = lax.broadcasted_iota(jnp.int32, (s, LANES), 1)
            dqp = jnp.where(lane < QK_ROPE, dq_ref[0, :, QK_NOPE:], dq_ref[1, :, QK_NOPE:])
            dqp_ref[...] = _rope(dqp, c_ref[...], -s_ref[...]).astype(dqp_ref.dtype)
            for hh in range(2):
                dqn_ref[:, hh * QK_NOPE:(hh + 1) * QK_NOPE] = dq_ref[hh, :, :QK_NOPE].astype(dqn_ref.dtype)

    qblk = pl.BlockSpec((s, 2 * QK_NOPE), lambda h, i: (0, h))
    pblk = pl.BlockSpec((s, LANES), lambda h, i: (0, h))
    tab = _full((s, LANES))
    return pl.pallas_call(
        body, name="attn_bwd", grid=(hp, nk),
        in_specs=[qblk, pblk, pl.BlockSpec((s, 4 * QK_NOPE), lambda h, i: (0, h)), tab, qblk, qblk,
                  pl.BlockSpec((None, s, 2), lambda h, i: (h, 0, 0)), tab, tab],
        out_specs=[qblk, pblk, pl.BlockSpec((t, 4 * QK_NOPE), lambda h, i: (i, h)), pl.BlockSpec((None, t, LANES), lambda h, i: (h, i, 0))],
        out_shape=[jax.ShapeDtypeStruct((s, HEADS * QK_NOPE), BF16), jax.ShapeDtypeStruct((s, HEADS * QK_ROPE), BF16),
                   jax.ShapeDtypeStruct((s, HEADS * 2 * QK_NOPE), BF16), jax.ShapeDtypeStruct((hp, s, LANES), F32)],
        scratch_shapes=[pltpu.VMEM((2, s, 2 * QK_NOPE), BF16), pltpu.VMEM((2, s, 2 * QK_NOPE), F32), pltpu.VMEM((2, s, 1), F32)],
        compiler_params=_params("parallel", "arbitrary"))(qn, qpr, kv, kpr, o, do, lse, cos4, sin4)


def kpe_bwd(dkp, cos4, sin4, pad_cols):
    hp, s, _ = dkp.shape
    tr = _tile(s, ROW_TILE * 2)

    def body(d_ref, c_ref, s_ref, o_ref):
        tot = d_ref[0]
        for h in range(1, hp):
            tot = tot + d_ref[h]
        tot = tot + pltpu.roll(tot, QK_ROPE, 1)
        lane = lax.broadcasted_iota(jnp.int32, tot.shape, 1)
        dk = jnp.where(lane < QK_ROPE, _rope(tot, c_ref[...], -s_ref[...]), jnp.zeros_like(tot))
        o_ref[...] = jnp.zeros_like(o_ref)
        o_ref[:, 0:LANES] = dk.astype(o_ref.dtype)

    row = pl.BlockSpec((tr, LANES), lambda i: (i, 0))
    return pl.pallas_call(body, name="kpe_bwd", grid=(s // tr,),
                          in_specs=[pl.BlockSpec((hp, tr, LANES), lambda i: (0, i, 0)), row, row],
                          out_specs=pl.BlockSpec((tr, pad_cols), lambda i: (i, 0)),
                          out_shape=jax.ShapeDtypeStruct((s, pad_cols), BF16), compiler_params=_params("parallel"))(dkp, cos4, sin4)


def _shift_down(x, n):
    row = lax.broadcasted_iota(jnp.int32, x.shape, 0)
    return jnp.where(row >= n, pltpu.roll(x, n, 0), jnp.zeros_like(x))


def _shift_up(x, n):
    rows = x.shape[0]
    row = lax.broadcasted_iota(jnp.int32, x.shape, 0)
    return jnp.where(row < rows - n, pltpu.roll(x, rows - n, 0), jnp.zeros_like(x))


def _conv(x, w_ref, b_ref):
    return w_ref[2:3, :] * x + w_ref[1:2, :] * _shift_down(x, 1) + w_ref[0:1, :] * _shift_down(x, 2) + b_ref[...]


def conv_act_fwd(upre, conv_w, conv_b):
    s, f2 = upre.shape
    f = f2 // 2
    tc = _tile(f, COL_TILE)
    nc = f // tc

    def body(ug_ref, uv_ref, wg_ref, wv_ref, bg_ref, bv_ref, o_ref):
        gh = _conv(ug_ref[...], wg_ref, bg_ref)
        vh = _conv(uv_ref[...], wv_ref, bv_ref)
        o_ref[...] = (gh * _sigmoid(gh) * vh).astype(o_ref.dtype)

    def spec(rows, shift):
        return pl.BlockSpec((rows, tc), lambda j: (0, j + shift))

    return pl.pallas_call(
        body, name="conv_act_fwd", grid=(nc,),
        in_specs=[spec(s, 0), spec(s, nc), spec(3, 0), spec(3, nc), spec(1, 0), spec(1, nc)], out_specs=spec(s, 0),
        out_shape=jax.ShapeDtypeStruct((s, f), BF16), compiler_params=_params("parallel"))(upre, upre, conv_w, conv_w, conv_b, conv_b)


def conv_act_bwd(upre, conv_w, conv_b, df):
    s, f2 = upre.shape
    f = f2 // 2
    tc = _tile(f, COL_TILE)
    nc = f // tc

    def half(x, d, w_ref, du_ref, gw_ref, gb_ref):
        gb_ref[...] = _colsum(d)
        gw_ref[2:3, :] = _colsum(d * x)
        gw_ref[1:2, :] = _colsum(d * _shift_down(x, 1))
        gw_ref[0:1, :] = _colsum(d * _shift_down(x, 2))
        du_ref[...] = (w_ref[2:3, :] * d + w_ref[1:2, :] * _shift_up(d, 1) + w_ref[0:1, :] * _shift_up(d, 2)).astype(du_ref.dtype)

    def body(ug_ref, uv_ref, wg_ref, wv_ref, bg_ref, bv_ref, df_ref, dug_ref, duv_ref, gwg_ref, gwv_ref, gbg_ref, gbv_ref):
        xg, xv = ug_ref[...], uv_ref[...]
        gh = _conv(xg, wg_ref, bg_ref)
        vh = _conv(xv, wv_ref, bv_ref)
        sg = _sigmoid(gh)
        df_v = df_ref[...]
        half(xg, df_v * vh * (sg * (1.0 + gh * (1.0 - sg))), wg_ref, dug_ref, gwg_ref, gbg_ref)
        half(xv, df_v * (gh * sg), wv_ref, duv_ref, gwv_ref, gbv_ref)

    def spec(rows, shift):
        return pl.BlockSpec((rows, tc), lambda j: (0, j + shift))

    act = jax.ShapeDtypeStruct((s, f), BF16)
    gw = jax.ShapeDtypeStruct((3, f), F32)
    gb = jax.ShapeDtypeStruct((1, f), F32)
    return pl.pallas_call(
        body, name="conv_act_bwd", grid=(nc,),
        in_specs=[spec(s, 0), spec(s, nc), spec(3, 0), spec(3, nc), spec(1, 0), spec(1, nc), spec(s, 0)],
        out_specs=[spec(s, 0), spec(s, 0), spec(3, 0), spec(3, 0), spec(1, 0), spec(1, 0)],
        out_shape=[act, act, gw, gw, gb, gb],
        compiler_params=_params("parallel"))(upre, upre, conv_w, conv_w, conv_b, conv_b, df)


def adamw(name, w, m, v, parts):
    npart, r, c = parts.shape
    tr = r
    if r % 8 == 0:
        tr = max(8, min(r, ADAMW_TILE_ELEMS // c) // 8 * 8)
        while r % tr:
            tr -= 8
    bc1 = 1.0 - ADAM_B1 ** ADAM_STEP
    bc2 = 1.0 - ADAM_B2 ** ADAM_STEP

    def body(w_ref, m_ref, v_ref, p_ref, g_ref, d_ref, nm_ref, nv_ref):
        g = p_ref[0].astype(F32)
        for k in range(1, npart):
            g = g + p_ref[k].astype(F32)
        m_new = ADAM_B1 * m_ref[...] + (1.0 - ADAM_B1) * g
        v_new = ADAM_B2 * v_ref[...] + (1.0 - ADAM_B2) * (g * g)
        g_ref[...] = g
        nm_ref[...] = m_new
        nv_ref[...] = v_new
        d_ref[...] = -ADAM_LR * ((m_new / bc1) / (jnp.sqrt(v_new / bc2) + ADAM_EPS) + ADAM_WD * w_ref[...])

    blk = pl.BlockSpec((tr, c), lambda i: (i, 0))
    out = jax.ShapeDtypeStruct((r, c), F32)
    return pl.pallas_call(
        body, name=name, grid=(r // tr,), in_specs=[blk, blk, blk, pl.BlockSpec((npart, tr, c), lambda i: (0, i, 0))],
        out_specs=[blk, blk, blk, blk], out_shape=[out, out, out, out], compiler_params=_params("parallel"))(w, m, v, parts)


def _position():
    return lax.axis_index("x"), lax.axis_index("y"), lax.axis_index("c")


def _index(p):
    return 4 * p[0] + 2 * p[1] + p[2]


def _peer(me, r):
    return (me[0] ^ ((r >> 2) & 1), me[1] ^ ((r >> 1) & 1), me[2] ^ (r & 1))


_ANY = pl.BlockSpec(memory_space=pl.ANY)


def all_gather_two_level(shards):
    n = len(shards)

    def body(*refs):
        ins, outs = refs[:n], refs[n:2 * n]
        send_sems, recv_sems, local_sems = refs[2 * n:]
        x, y, c = _position()
        me, sibling = (x, y, c), (x, y, 1 - c)
        chips = [(1 - x, y), (x, 1 - y), (1 - x, 1 - y)]

        def copy(w, k, block, to, src=None):
            slot = outs[w].at[_index(block)]
            return pltpu.make_async_remote_copy(src_ref=slot if src is None else src, dst_ref=slot,
                                                send_sem=send_sems.at[7 * w + k], recv_sem=recv_sems.at[7 * w + k],
                                                device_id=to, device_id_type=MESH)

        mine = [pltpu.make_async_copy(ins[w], outs[w].at[_index(me)], local_sems.at[w]) for w in range(n)]
        for cp in mine:
            cp.start()
        first = []
        for w in range(n):
            first.append(copy(w, 0, me, sibling, src=ins[w]))
            first += [copy(w, 1 + j, me, (*chip, c), src=ins[w]) for j, chip in enumerate(chips)]
        for cp in first:
            cp.start()
        passed = []
        for w in range(n):
            for j, chip in enumerate(chips):
                copy(w, 1 + j, (*chip, c), me).wait_recv()
                cp = copy(w, 4 + j, (*chip, c), sibling)
                cp.start()
                passed.append(cp)
        for w in range(n):
            copy(w, 0, sibling, me).wait_recv()
            for j, chip in enumerate(chips):
                copy(w, 4 + j, (*chip, 1 - c), me).wait_recv()
        for cp in first + passed:
            cp.wait_send()
        for cp in mine:
            cp.wait()

    return pl.pallas_call(
        body, name="all_gather_weights",
        out_shape=[jax.ShapeDtypeStruct((N_DEV,) + a.shape, a.dtype) for a in shards],
        in_specs=[_ANY] * n, out_specs=[_ANY] * n,
        scratch_shapes=[pltpu.SemaphoreType.DMA((7 * n,)), pltpu.SemaphoreType.DMA((7 * n,)), pltpu.SemaphoreType.DMA((n,))],
        )(*shards)


def exchange(name, arrays, scatter):
    n = len(arrays)

    def body(*refs):
        ins, outs = refs[:n], refs[n:2 * n]
        send_sems, recv_sems, local_sems = refs[2 * n:]
        me = _position()
        copies = []
        for w in range(n):
            src = ins[w].at[_index(me)] if scatter else ins[w]
            cp = pltpu.make_async_copy(src, outs[w].at[_index(me)], local_sems.at[w])
            cp.start()
            copies.append(cp)
        remote = []
        for w in range(n):
            for r in range(1, N_DEV):
                peer = _peer(me, r)
                src = ins[w].at[_index(peer)] if scatter else ins[w]
                cp = pltpu.make_async_remote_copy(src_ref=src, dst_ref=outs[w].at[_index(me)],
                                                  send_sem=send_sems.at[7 * w + r - 1], recv_sem=recv_sems.at[7 * w + r - 1],
                                                  device_id=peer, device_id_type=MESH)
                cp.start()
                remote.append(cp)
        for cp in remote:
            cp.wait()
        for cp in copies:
            cp.wait()

    blocks = [a.shape[1:] if scatter else a.shape for a in arrays]
    return pl.pallas_call(
        body, name=name,
        out_shape=[jax.ShapeDtypeStruct((N_DEV,) + b, a.dtype) for a, b in zip(arrays, blocks)],
        in_specs=[_ANY] * n, out_specs=[_ANY] * n,
        scratch_shapes=[pltpu.SemaphoreType.DMA((7 * n,)), pltpu.SemaphoreType.DMA((7 * n,)), pltpu.SemaphoreType.DMA((n,))],
        )(*arrays)


_HBM = pl.BlockSpec(memory_space=pltpu.HBM)
_SEM = pl.BlockSpec(memory_space=pltpu.SEMAPHORE)
_EFFECT = pltpu.SideEffectType.DATAFLOW_SIDE_EFFECTING


def _direct_copies(ins, lands, send_sems, recv_sems, scatter):
    me = _position()
    copies = []
    for w in range(len(ins)):
        for r in range(1, N_DEV):
            peer = _peer(me, r)
            src = ins[w].at[_index(peer)] if scatter else ins[w]
            copies.append(pltpu.make_async_remote_copy(src_ref=src, dst_ref=lands[w].at[_index(me)], send_sem=send_sems.at[7 * w + r - 1],
                                                       recv_sem=recv_sems.at[7 * w + r - 1], device_id=peer, device_id_type=MESH))
    return copies


def exchange_start(name, groups, scatter):
    arrays = [a for g in groups for a in g]
    n = len(arrays)
    blocks = [a.shape[1:] if scatter else a.shape for a in arrays]
    lands = [lax.empty((N_DEV,) + b, a.dtype) for a, b in zip(arrays, blocks)]
    ng = len(groups)

    def body(*refs):
        ins, lnd = refs[:n], refs[n:2 * n]
        sems = refs[2 * n:2 * n + 2 * ng]
        token = refs[2 * n + 2 * ng + 2 * n]
        local_sem = refs[2 * n + 2 * ng + 2 * n + 1]
        me = _position()
        local = []
        for w in range(n):
            src = ins[w].at[_index(me)] if scatter else ins[w]
            cp = pltpu.make_async_copy(src, lnd[w].at[_index(me)], local_sem.at[w])
            cp.start()
            local.append(cp)
        w0 = 0
        for gi, g in enumerate(groups):
            for cp in _direct_copies(ins[w0:w0 + len(g)], lnd[w0:w0 + len(g)], sems[2 * gi], sems[2 * gi + 1], scatter):
                cp.start()
            w0 += len(g)
        for cp in local:
            cp.wait()
        token[...] = jnp.zeros_like(token)

    sem_shapes = []
    for g in groups:
        sem_shapes += [pltpu.SemaphoreType.DMA((7 * len(g),)), pltpu.SemaphoreType.DMA((7 * len(g),))]
    out = pl.pallas_call(
        body, name=name,
        out_shape=tuple(sem_shapes) + tuple(pltpu.HBM(a.shape, a.dtype) for a in arrays) + tuple(pltpu.HBM(l.shape, l.dtype) for l in lands)
        + (jax.ShapeDtypeStruct((8, LANES), F32),),
        in_specs=[_HBM] * (2 * n), out_specs=tuple([_SEM] * (2 * ng) + [_HBM] * (2 * n) + [pl.BlockSpec(memory_space=pltpu.VMEM)]),
        input_output_aliases={i: 2 * ng + i for i in range(2 * n)},
        scratch_shapes=[pltpu.SemaphoreType.DMA((n,))],
        compiler_params=pltpu.CompilerParams(has_side_effects=_EFFECT),
    )(*[pltpu.with_memory_space_constraint(a, pltpu.HBM) for a in arrays], *[pltpu.with_memory_space_constraint(l, pltpu.HBM) for l in lands])
    sems, thru, token = out[:2 * ng], out[2 * ng:2 * ng + 2 * n], out[-1]
    res, w0 = [], 0
    for gi, g in enumerate(groups):
        res.append((sems[2 * gi], sems[2 * gi + 1], list(thru[w0:w0 + len(g)]), list(thru[n + w0:n + w0 + len(g)])))
        w0 += len(g)
    return res, token


def exchange_wait(name, group, after, scatter):
    send_sems, recv_sems, srcs, lands = group
    n = len(srcs)

    def body(*refs):
        ins, lnd = refs[:n], refs[n:2 * n]
        for cp in _direct_copies(ins, lnd, refs[2 * n], refs[2 * n + 1], scatter):
            cp.wait_send()
            cp.wait_recv()

    out = pl.pallas_call(
        body, name=name, out_shape=tuple(pltpu.HBM(a.shape, a.dtype) for a in srcs + lands),
        in_specs=[_HBM] * (2 * n) + [_SEM, _SEM, pl.BlockSpec(memory_space=pl.ANY)], out_specs=tuple([_HBM] * (2 * n)),
        input_output_aliases={i: i for i in range(2 * n)},
        compiler_params=pltpu.CompilerParams(has_side_effects=_EFFECT),
    )(*srcs, *lands, send_sems, recv_sems, after)
    return list(out[n:])


def _after(x, token):
    return lax.optimization_barrier((x, token))[0]


def ada_fwd(c, w_ada, b_ada3):
    d, cs = w_ada.shape

    def body(c_ref, w_ref, b_ref, mod_ref, sc_ref, part_ref, send_sems, recv_sems):
        me = _position()
        my = _index(me)
        cv = c_ref[...]
        sc_ref[my] = cv * _sigmoid(cv)
        gather = []
        for r in range(1, N_DEV):
            cp = pltpu.make_async_remote_copy(src_ref=sc_ref.at[my], dst_ref=sc_ref.at[my], send_sem=send_sems.at[r - 1],
                                              recv_sem=recv_sems.at[r - 1], device_id=_peer(me, r), device_id_type=MESH)
            cp.start()
            gather.append(cp)
        for cp in gather:
            cp.wait()
        sc_all = jnp.concatenate([sc_ref[k] for k in range(N_DEV)], axis=0).astype(BF16)
        part = jnp.dot(sc_all, w_ref[...].astype(BF16), preferred_element_type=F32)
        for k in range(N_DEV):
            part_ref[k] = part[k:k + 1, :]
        scatter = []
        for r in range(1, N_DEV):
            peer = _peer(me, r)
            cp = pltpu.make_async_remote_copy(src_ref=part_ref.at[_index(peer)], dst_ref=mod_ref.at[my], send_sem=send_sems.at[6 + r],
                                              recv_sem=recv_sems.at[6 + r], device_id=peer, device_id_type=MESH)
            cp.start()
            scatter.append(cp)
        mod_ref[my] = part_ref[my]
        for cp in scatter:
            cp.wait()
        mod_ref[...] = mod_ref[...] + b_ref[...]

    vm = pl.BlockSpec(memory_space=pltpu.VMEM)
    return pl.pallas_call(
        body, name="ada_fwd",
        out_shape=[jax.ShapeDtypeStruct((N_DEV, 1, cs), F32), jax.ShapeDtypeStruct((N_DEV, 1, d), F32)],
        in_specs=[vm, vm, vm], out_specs=[vm, vm],
        scratch_shapes=[pltpu.VMEM((N_DEV, 1, cs), F32), pltpu.SemaphoreType.DMA((14,)), pltpu.SemaphoreType.DMA((14,))],
        compiler_params=pltpu.CompilerParams(vmem_limit_bytes=VMEM_LIMIT_BYTES))(c, w_ada, b_ada3)


def ada_bwd_w(sc_all, dmod_cols):
    _, d = sc_all.shape
    cs = dmod_cols.shape[1]
    tr = _tile(d, ROW_TILE)

    def body(sc_ref, dm_ref, o_ref):
        dm = dm_ref[...].astype(BF16)
        o_ref[...] = lax.dot_general(sc_ref[...].astype(BF16), dm, (((0,), (0,)), ((), ())), preferred_element_type=F32)

    return pl.pallas_call(body, name="ada_bwd_w", grid=(d // tr,),
                          in_specs=[pl.BlockSpec((N_DEV, tr), lambda i: (0, i)), _full((N_DEV, cs))],
                          out_specs=pl.BlockSpec((None, tr, cs), lambda i: (0, i, 0)),
                          out_shape=jax.ShapeDtypeStruct((1, d, cs), F32), compiler_params=_params("parallel"))(sc_all, dmod_cols)


def _round_up(n, m):
    return (n + m - 1) // m * m


def kernel(x, c, positions, w_ada, b_ada, pre_norm1_g, w_in, gm_ln_g, gm_ln_b, gm_w_s, gm_b_s, w_branch_a, q_norm_g, w_uq, kv_norm_g, w_ukv, w_branch_b, w_out, post_norm1_g, pre_norm2_g, w_up, conv_w, conv_b, w_down, post_norm2_g, loss_target, m_w_ada, m_b_ada, m_pre_norm1_g, m_w_in, m_gm_ln_g, m_gm_ln_b, m_gm_w_s, m_gm_b_s, m_w_branch_a, m_q_norm_g, m_w_uq, m_kv_norm_g, m_w_ukv, m_w_branch_b, m_w_out, m_post_norm1_g, m_pre_norm2_g, m_w_up, m_conv_w, m_conv_b, m_w_down, m_post_norm2_g, v_w_ada, v_b_ada, v_pre_norm1_g, v_w_in, v_gm_ln_g, v_gm_ln_b, v_gm_w_s, v_gm_b_s, v_w_branch_a, v_q_norm_g, v_w_uq, v_kv_norm_g, v_w_ukv, v_w_branch_b, v_w_out, v_post_norm1_g, v_pre_norm2_g, v_w_up, v_conv_w, v_conv_b, v_w_down, v_post_norm2_g):
    weights = dict(w_ada=w_ada, b_ada=b_ada, pre_norm1_g=pre_norm1_g, w_in=w_in, gm_ln_g=gm_ln_g, gm_ln_b=gm_ln_b, gm_w_s=gm_w_s,
                   gm_b_s=gm_b_s, w_branch_a=w_branch_a, q_norm_g=q_norm_g, w_uq=w_uq, kv_norm_g=kv_norm_g, w_ukv=w_ukv,
                   w_branch_b=w_branch_b, w_out=w_out, post_norm1_g=post_norm1_g, pre_norm2_g=pre_norm2_g, w_up=w_up, conv_w=conv_w,
                   conv_b=conv_b, w_down=w_down, post_norm2_g=post_norm2_g)
    mom1 = dict(w_ada=m_w_ada, b_ada=m_b_ada, pre_norm1_g=m_pre_norm1_g, w_in=m_w_in, gm_ln_g=m_gm_ln_g, gm_ln_b=m_gm_ln_b,
                gm_w_s=m_gm_w_s, gm_b_s=m_gm_b_s, w_branch_a=m_w_branch_a, q_norm_g=m_q_norm_g, w_uq=m_w_uq, kv_norm_g=m_kv_norm_g,
                w_ukv=m_w_ukv, w_branch_b=m_w_branch_b, w_out=m_w_out, post_norm1_g=m_post_norm1_g, pre_norm2_g=m_pre_norm2_g,
                w_up=m_w_up, conv_w=m_conv_w, conv_b=m_conv_b, w_down=m_w_down, post_norm2_g=m_post_norm2_g)
    mom2 = dict(w_ada=v_w_ada, b_ada=v_b_ada, pre_norm1_g=v_pre_norm1_g, w_in=v_w_in, gm_ln_g=v_gm_ln_g, gm_ln_b=v_gm_ln_b,
                gm_w_s=v_gm_w_s, gm_b_s=v_gm_b_s, w_branch_a=v_w_branch_a, q_norm_g=v_q_norm_g, w_uq=v_w_uq, kv_norm_g=v_kv_norm_g,
                w_ukv=v_w_ukv, w_branch_b=v_w_branch_b, w_out=v_w_out, post_norm1_g=v_post_norm1_g, pre_norm2_g=v_pre_norm2_g,
                w_up=v_w_up, conv_w=v_conv_w, conv_b=v_conv_b, w_down=v_w_down, post_norm2_g=v_post_norm2_g)
    order = list(weights)

    s, d = x.shape[1], x.shape[2]
    gmw = gm_ln_g.shape[0]
    groups = gmw // CHUNK
    ql, kvl = q_norm_g.shape[0], kv_norm_g.shape[0]
    f2 = conv_b.shape[0]
    in_cols = w_in.shape[1] * N_DEV
    o_q, o_kv, o_ga, o_gb, o_kpe = 2 * gmw, 2 * gmw + ql, 2 * gmw + ql + kvl, 2 * gmw + ql + kvl + d, 2 * gmw + ql + kvl + 2 * d
    zp = _round_up(o_kpe + LANES, Z_PAD)
    src_kpe = 2 * gmw + ql + kvl
    assert src_kpe + QK_ROPE + 2 * d == in_cols
    my = 4 * lax.axis_index("x") + 2 * lax.axis_index("y") + lax.axis_index("c")

    x2, tgt = x[0], loss_target[0]
    row = lambda a: a.reshape(1, -1)

    big = ["w_in", "w_branch_a", "w_uq", "w_ukv", "w_branch_b", "w_out", "w_up", "w_down"]
    sh = {k: weights[k].astype(BF16) for k in big}
    ag, tok = exchange_start("ag_start", [[sh["w_in"], conv_w], [sh["w_branch_a"], sh["w_uq"], sh["w_ukv"], sh["w_branch_b"], sh["w_out"]],
                                          [sh["w_up"]], [sh["w_down"]]], scatter=False)

    mod8, sc_all3 = ada_fwd(_after(c, tok), w_ada, b_ada.reshape(N_DEV, 1, -1))
    mod = mod8.reshape(N_MOD, d)
    shift1, scale1, gate1, shift2, scale2, gate2 = (mod[i:i + 1] for i in range(N_MOD))
    sc_all = sc_all3.reshape(N_DEV, d)

    g_in, g_cw = exchange_wait("ag_wait_in", ag[0], mod8, False)
    w_in_f = g_in.transpose(1, 0, 2).reshape(d, in_cols)
    w_in_p = jnp.concatenate([w_in_f[:, :src_kpe], w_in_f[:, src_kpe + QK_ROPE:], w_in_f[:, src_kpe:src_kpe + QK_ROPE],
                              jnp.zeros((d, zp - in_cols), BF16)], axis=1)

    inv = ROPE_THETA ** (-jnp.arange(0, QK_ROPE, 2, dtype=F32) / QK_ROPE)
    ang = positions[0].astype(F32)[:, None] * inv
    cos4 = jnp.tile(jnp.cos(ang), (1, 4))
    sin4 = jnp.tile(jnp.concatenate([-jnp.sin(ang), jnp.sin(ang)], axis=1), (1, 2))

    wm = (gm_w_s * jnp.tril(jnp.ones((CHUNK, CHUNK), F32))).astype(BF16)
    bs3 = gm_b_s.reshape(groups, CHUNK, 1)
    ln_g, ln_b = row(gm_ln_g), row(gm_ln_b)

    h1 = norm_mod_fwd("pre1_fwd", x2, row(pre_norm1_g), scale1, shift1)
    z = mm_nn("z_proj", h1, w_in_p, F32)
    a = gmlp_fwd(z, gmw, ln_g, ln_b, wm, bs3)
    g_a, g_uq, g_ukv, g_b, g_out = exchange_wait("ag_wait_mix", ag[1], a, False)
    w_a_f, w_b_f, w_out_f = g_a.reshape(-1, d), g_b.reshape(-1, d), g_out.reshape(-1, d)
    w_uq_f = g_uq.transpose(1, 0, 2).reshape(ql, HEADS, QK_NOPE + QK_ROPE)
    w_uq_n = w_uq_f[:, :, :QK_NOPE].reshape(ql, HEADS * QK_NOPE)
    w_uq_r = w_uq_f[:, :, QK_NOPE:].reshape(ql, HEADS * QK_ROPE)
    y_a = mm_nn("branch_a", a, w_a_f, F32)
    qln = rms_fwd_cols("q_norm", z, o_q, ql, row(q_norm_g))
    kvn = rms_fwd_cols("kv_norm", z, o_kv, kvl, row(kv_norm_g))
    qn = mm_nn("q_nope", qln, w_uq_n, BF16)
    qp = mm_nn("q_rope", qln, w_uq_r, F32)
    kv = mm_nn_b3("kv_up", kvn, g_ukv, BF16)
    kpr = rope_k(z, o_kpe, cos4, sin4)
    o, qpr, lse = attn_fwd2(qn, qp, kv, kpr, cos4, sin4)
    y_b = mm_nn("branch_b", o, w_b_f, F32)
    merged = merge_fwd(z, o_ga, o_gb, y_a, y_b)
    y1 = mm_nn("out_proj", merged, w_out_f, F32)
    x1 = post_res_fwd("post1_fwd", x2, y1, gate1, row(post_norm1_g))
    h2 = norm_mod_fwd("pre2_fwd", x1, row(pre_norm2_g), scale2, shift2)
    (g_up,) = exchange_wait("ag_wait_up", ag[2], h2, False)
    upre = mm_nn_b3("up_proj", h2, g_up, F32)
    cw = g_cw.transpose(1, 0, 2).reshape(3, f2)
    cb = row(conv_b)
    f = conv_act_fwd(upre, cw, cb)
    (g_down,) = exchange_wait("ag_wait_down", ag[3], f, False)
    w_down_f = g_down.reshape(-1, d)
    ffn = mm_nn("down_proj", f, w_down_f, F32)
    loss_acc, dout, dffn, acc2 = post2_loss_bwd(x1, ffn, tgt, gate2, row(post_norm2_g))

    blocks = lambda g: g.reshape(N_DEV, g.shape[0] // N_DEV, g.shape[1])
    rs = {}
    gw_down = mm_tn("g_w_down", f, dffn, BF16)
    (rs["w_down"],), tok = exchange_start("rs_start_down", [[blocks(gw_down)]], scatter=True)
    df = mm_nt("d_f", _after(dffn, tok), w_down_f, F32)
    dup_g, dup_v, gcw_g, gcw_v, gcb_g, gcb_v = conv_act_bwd(upre, cw, cb, df)
    dupre = jnp.concatenate([dup_g, dup_v], axis=1)
    gw_up3 = mm_tn_o3("g_w_up", h2, dupre, N_DEV, BF16)
    (rs["w_up"],), tok = exchange_start("rs_start_up", [[gw_up3]], scatter=True)
    dh2 = mm_nt_b3("d_h2", _after(dupre, tok), g_up, F32)
    dx1, dy1, acc_mid = mid_bwd(dh2, dout, x1, y1, row(pre_norm2_g), scale2, gate1, row(post_norm1_g))
    gw_out = mm_tn("g_w_out", merged, dy1, BF16)
    (rs["w_out"],), tok = exchange_start("rs_start_out", [[blocks(gw_out)]], scatter=True)
    dmerged = mm_nt("d_merged", _after(dy1, tok), w_out_f, F32)
    dya, dyb, dga, dgb = merge_bwd(z, o_ga, o_gb, y_a, y_b, dmerged)
    gw_a = mm_tn("g_w_a", a, dya, BF16)
    gw_b = mm_tn("g_w_b", o, dyb, BF16)
    (rs["ab"],), tok = exchange_start("rs_start_ab", [[blocks(gw_a), blocks(gw_b)]], scatter=True)
    da = mm_nt("d_a", _after(dya, tok), w_a_f, F32)
    do = mm_nt("d_o", dyb, w_b_f, F32)
    duv, g_ws, g_bs3, acc_gm = gmlp_bwd(z, gmw, da, ln_g, ln_b, wm, bs3)
    dqn, dqp, dkv, dkp = attn_bwd2(qn, qpr, kv, kpr, o, do, lse, cos4, sin4)
    dkpe = kpe_bwd(dkp, cos4, sin4, zp - o_kpe)
    dq_cat = jnp.concatenate([dqn, dqp], axis=1)
    w_uq_cat = jnp.concatenate([w_uq_n, w_uq_r], axis=1)
    gw_uq_cat = mm_tn("g_w_uq", qln, dq_cat, BF16)
    gw_uq_f = jnp.concatenate([gw_uq_cat[:, :HEADS * QK_NOPE].reshape(ql, HEADS, QK_NOPE),
                               gw_uq_cat[:, HEADS * QK_NOPE:].reshape(ql, HEADS, QK_ROPE)], axis=2)
    gw_uq3 = gw_uq_f.reshape(ql, N_DEV, -1).transpose(1, 0, 2)
    gw_ukv3 = mm_tn_o3("g_w_ukv", kvn, dkv, N_DEV, BF16)
    (rs["mla"],), tok = exchange_start("rs_start_mla", [[gw_uq3, gw_ukv3]], scatter=True)
    dqln = mm_nt("d_qln", _after(dq_cat, tok), w_uq_cat, F32)
    dq_lat, g_qnorm = rms_bwd_cols("q_norm_bwd", dqln, z, o_q, ql, row(q_norm_g))
    dkvn = mm_nt_b3("d_kvn", dkv, g_ukv, F32)
    dkv_lat, g_kvnorm = rms_bwd_cols("kv_norm_bwd", dkvn, z, o_kv, kvl, row(kv_norm_g))
    dz = jnp.concatenate([duv, dq_lat, dkv_lat, dga, dgb, dkpe], axis=1)
    gw_in_p = mm_tn("g_w_in", h1, dz, BF16)
    gw_in_f = jnp.concatenate([gw_in_p[:, :src_kpe], gw_in_p[:, o_kpe:o_kpe + QK_ROPE], gw_in_p[:, src_kpe:o_kpe]], axis=1)
    gw_in3 = gw_in_f.reshape(d, N_DEV, -1).transpose(1, 0, 2)
    (rs["w_in"],), tok = exchange_start("rs_start_in", [[gw_in3]], scatter=True)
    dh1 = mm_nt("d_h1", _after(dz, tok), w_in_p, F32)
    grad_x, acc1 = pre1_bwd(dh1, dx1, x2, row(pre_norm1_g), scale1)

    dmod = jnp.concatenate([acc1[0], acc1[1], acc_mid[3], acc_mid[0], acc_mid[1], acc2[0]])
    small = [("pre_norm1_g", acc1[2]), ("gm_ln_g", acc_gm[0]), ("gm_ln_b", acc_gm[1]), ("gm_b_s", g_bs3.reshape(-1)),
             ("q_norm_g", g_qnorm[0]), ("kv_norm_g", g_kvnorm[0]), ("post_norm1_g", acc_mid[4]), ("pre_norm2_g", acc_mid[2]),
             ("conv_b", jnp.concatenate([gcb_g[0], gcb_v[0]])), ("post_norm2_g", acc2[1]), ("gm_w_s", g_ws.reshape(-1)),
             ("b_ada", dmod)]
    n_small = sum(v.shape[0] for _, v in small)
    n_cw = 3 * f2
    n_pack = _round_up(n_small + n_cw, PACK_ALIGN)
    tail = jnp.zeros((n_pack - n_small - n_cw,), F32)
    packed = jnp.concatenate([v for _, v in small] + [jnp.concatenate([gcw_g, gcw_v], axis=1).reshape(-1), tail])
    (sg,), tok = exchange_start("small_start", [[packed.reshape(-1, LANES)]], scatter=False)

    res = {}
    last = tok
    for key, names in (("w_down", ["w_down"]), ("w_up", ["w_up"]), ("w_out", ["w_out"]), ("ab", ["w_branch_a", "w_branch_b"]),
                       ("mla", ["w_uq", "w_ukv"])):
        for k, p in zip(names, exchange_wait("rs_wait_" + key, rs[key], last, True)):
            res[k] = adamw("adamw_" + k, weights[k], mom1[k], mom2[k], p)
            last = res[k][0]

    def pack(src):
        return jnp.concatenate([src[k].reshape(-1) for k, _ in small] + [jnp.zeros((n_pack - n_small,), F32)]).reshape(-1, LANES)

    (gathered,) = exchange_wait("small_wait", sg, last, False)
    sm = [t.reshape(-1) for t in adamw("adamw_small", pack(weights), pack(mom1), pack(mom2), gathered)]
    off = 0
    for k, v in small:
        res[k] = tuple(t[off:off + v.shape[0]].reshape(weights[k].shape) for t in sm)
        off += v.shape[0]

    cs_cw = conv_w.shape[1]
    g_cw_full = sm[0][n_small:n_small + n_cw].reshape(3, f2)
    g_cw_mine = lax.dynamic_slice(g_cw_full, (0, my * cs_cw), (3, cs_cw))
    res["conv_w"] = adamw("adamw_conv_w", conv_w, mom1["conv_w"], mom2["conv_w"], g_cw_mine[None])

    cs_ada = w_ada.shape[1]
    off_b = n_small - N_MOD * d
    dmod_all = gathered.reshape(N_DEV, -1)[:, off_b:off_b + N_MOD * d]
    dmod_cols = lax.dynamic_slice(dmod_all, (0, my * cs_ada), (N_DEV, cs_ada))
    res["w_ada"] = adamw("adamw_w_ada", w_ada, mom1["w_ada"], mom2["w_ada"], ada_bwd_w(sc_all, dmod_cols))

    (p_in,) = exchange_wait("rs_wait_in", rs["w_in"], res["w_ada"][0], True)
    res["w_in"] = adamw("adamw_w_in", w_in, mom1["w_in"], mom2["w_in"], p_in)

    loss = lax.psum(loss_acc[0, 0], ("x", "y", "c"))
    outs = [loss, grad_x[None]]
    for i in range(4):
        outs += [res[k][i] for k in order]
    return tuple(outs)
```

```python
import functools

import jax
import jax.numpy as jnp
from jax import lax
from jax.experimental import pallas as pl
from jax.experimental.pallas import tpu as pltpu

F32 = jnp.float32
BF16 = jnp.bfloat16

N_DEV = 8
HEADS = 16
QK_NOPE = 128
QK_ROPE = 64
V_HEAD = 128
CHUNK = 128
ROPE_THETA = 10000.0
EPS = 1e-6
N_MOD = 6
ADAM_LR, ADAM_B1, ADAM_B2, ADAM_EPS, ADAM_WD, ADAM_STEP = 0.001, 0.9, 0.999, 1e-08, 0.01, 10

LANES = 128
VMEM_LIMIT_BYTES = 48 * 2 ** 20
ROW_TILE = 256
COL_TILE = 256
ATT_TILE = 256
Z_PAD = 512
ADAMW_TILE_ELEMS = 1 << 18
PACK_ALIGN = 8 * LANES
MESH = pl.DeviceIdType.MESH


def _params(*sem):
    return pltpu.CompilerParams(dimension_semantics=sem if sem else None, vmem_limit_bytes=VMEM_LIMIT_BYTES)


def _tile(dim, target):
    t = (min(dim, target) // LANES) * LANES
    while t >= LANES:
        if dim % t == 0:
            return t
        t -= LANES
    return dim


def _full(shape):
    nd = len(shape)
    return pl.BlockSpec(shape, lambda *_: (0,) * nd)


class _Tokens:
    def __init__(self):
        self.pending = []

    def push(self, token):
        self.pending.append(token)

    def take(self):
        out, self.pending = self.pending, []
        return out


_TOKENS = _Tokens()


def _matmul(name, a, b, *, grid, a_spec, b_spec, o_spec, out_shape, contract, acc_shape):
    nk = grid[2]
    deps = _TOKENS.take()

    def body(a_ref, b_ref, *rest):
        o_ref, acc_ref = rest[len(deps):]
        k = pl.program_id(2)

        @pl.when(k == 0)
        def _():
            acc_ref[...] = jnp.zeros_like(acc_ref)

        acc_ref[...] += lax.dot_general(a_ref[...].astype(BF16), b_ref[...].astype(BF16),
                                        (contract, ((), ())), preferred_element_type=F32)

        @pl.when(k == nk - 1)
        def _():
            o_ref[...] = acc_ref[...].astype(o_ref.dtype)

    return pl.pallas_call(
        body, name=name, grid=grid, in_specs=[a_spec, b_spec] + [pl.BlockSpec(memory_space=pl.ANY)] * len(deps),
        out_specs=o_spec, out_shape=out_shape, scratch_shapes=[pltpu.VMEM(acc_shape, F32)],
        compiler_params=_params("parallel", "parallel", "arbitrary"))(a, b, *deps)


TM, TN, TK = 1024, 1024, 512


def mm_nn(name, a, b, dtype):
    (m, k), n = a.shape, b.shape[1]
    tm, tn, tk = _tile(m, TM), _tile(n, TN), _tile(k, TK)
    return _matmul(name, a, b, grid=(m // tm, n // tn, k // tk),
                   a_spec=pl.BlockSpec((tm, tk), lambda i, j, kk: (i, kk)),
                   b_spec=pl.BlockSpec((tk, tn), lambda i, j, kk: (kk, j)),
                   o_spec=pl.BlockSpec((tm, tn), lambda i, j, kk: (i, j)),
                   out_shape=jax.ShapeDtypeStruct((m, n), dtype), contract=((1,), (0,)), acc_shape=(tm, tn))


def mm_nn_b3(name, a, b3, dtype):
    (m, k), (nj, _, cs) = a.shape, b3.shape
    tm, tk = _tile(m, TM), _tile(k, TK)
    return _matmul(name, a, b3, grid=(m // tm, nj, k // tk),
                   a_spec=pl.BlockSpec((tm, tk), lambda i, j, kk: (i, kk)),
                   b_spec=pl.BlockSpec((None, tk, cs), lambda i, j, kk: (j, kk, 0)),
                   o_spec=pl.BlockSpec((tm, cs), lambda i, j, kk: (i, j)),
                   out_shape=jax.ShapeDtypeStruct((m, nj * cs), dtype), contract=((1,), (0,)), acc_shape=(tm, cs))


def mm_nt(name, a, b, dtype):
    (m, k), n = a.shape, b.shape[0]
    tm, tn, tk = _tile(m, TM), _tile(n, TN), _tile(k, TK)
    return _matmul(name, a, b, grid=(m // tm, n // tn, k // tk),
                   a_spec=pl.BlockSpec((tm, tk), lambda i, j, kk: (i, kk)),
                   b_spec=pl.BlockSpec((tn, tk), lambda i, j, kk: (j, kk)),
                   o_spec=pl.BlockSpec((tm, tn), lambda i, j, kk: (i, j)),
                   out_shape=jax.ShapeDtypeStruct((m, n), dtype), contract=((1,), (1,)), acc_shape=(tm, tn))


def mm_nt_b3(name, a, b3, dtype):
    m, (nj, n, cs) = a.shape[0], b3.shape
    tm, tn = _tile(m, TM), _tile(n, TN)
    return _matmul(name, a, b3, grid=(m // tm, n // tn, nj),
                   a_spec=pl.BlockSpec((tm, cs), lambda i, j, kk: (i, kk)),
                   b_spec=pl.BlockSpec((None, tn, cs), lambda i, j, kk: (kk, j, 0)),
                   o_spec=pl.BlockSpec((tm, tn), lambda i, j, kk: (i, j)),
                   out_shape=jax.ShapeDtypeStruct((m, n), dtype), contract=((1,), (1,)), acc_shape=(tm, tn))


def mm_tn(name, a, b, dtype):
    (k, m), n = a.shape, b.shape[1]
    tm, tn, tk = _tile(m, TM), _tile(n, TN), _tile(k, TK)
    return _matmul(name, a, b, grid=(m // tm, n // tn, k // tk),
                   a_spec=pl.BlockSpec((tk, tm), lambda i, j, kk: (kk, i)),
                   b_spec=pl.BlockSpec((tk, tn), lambda i, j, kk: (kk, j)),
                   o_spec=pl.BlockSpec((tm, tn), lambda i, j, kk: (i, j)),
                   out_shape=jax.ShapeDtypeStruct((m, n), dtype), contract=((0,), (0,)), acc_shape=(tm, tn))


def mm_tn_o3(name, a, b, nj, dtype):
    (k, m), n = a.shape, b.shape[1]
    cs = n // nj
    tm, tk = _tile(m, TM), _tile(k, TK)
    return _matmul(name, a, b, grid=(m // tm, nj, k // tk),
                   a_spec=pl.BlockSpec((tk, tm), lambda i, j, kk: (kk, i)),
                   b_spec=pl.BlockSpec((tk, cs), lambda i, j, kk: (kk, j)),
                   o_spec=pl.BlockSpec((None, tm, cs), lambda i, j, kk: (j, i, 0)),
                   out_shape=jax.ShapeDtypeStruct((nj, m, cs), dtype), contract=((0,), (0,)), acc_shape=(tm, cs))


_GELU_C = 0.7978845608028654
_GELU_A = 0.044715


def _gelu(x):
    return 0.5 * x * (1.0 + jnp.tanh(_GELU_C * (x + _GELU_A * x * x * x)))


def _gelu_and_grad(x):
    t = jnp.tanh(_GELU_C * (x + _GELU_A * x * x * x))
    y = 0.5 * x * (1.0 + t)
    dy = 0.5 * (1.0 + t) + 0.5 * x * (1.0 - t * t) * (_GELU_C * (1.0 + 3.0 * _GELU_A * x * x))
    return y, dy


def _sigmoid(x):
    return 1.0 / (1.0 + jnp.exp(-x))


def _rms_stats(x):
    inv = lax.rsqrt(jnp.mean(x * x, axis=-1, keepdims=True) + EPS)
    return inv, x * inv


def _rms_bwd(dyhat, yhat, inv):
    return inv * (dyhat - yhat * jnp.mean(dyhat * yhat, axis=-1, keepdims=True))


def _colsum(x):
    return jnp.sum(x, axis=0, keepdims=True)


def _rope(x, cos4, sin4):
    lane = lax.broadcasted_iota(jnp.int32, x.shape, x.ndim - 1)
    first_half = (lane % QK_ROPE) < (QK_ROPE // 2)
    partner = jnp.where(first_half, pltpu.roll(x, LANES - QK_ROPE // 2, x.ndim - 1), pltpu.roll(x, QK_ROPE // 2, x.ndim - 1))
    return x * cos4 + partner * sin4


def norm_mod_fwd(name, x, g, scale, shift):
    s, d = x.shape
    tr = _tile(s, ROW_TILE)

    def body(x_ref, g_ref, sc_ref, sh_ref, o_ref):
        _, xh = _rms_stats(x_ref[...])
        o_ref[...] = (xh * g_ref[...] * (1.0 + sc_ref[...]) + sh_ref[...]).astype(o_ref.dtype)

    row = pl.BlockSpec((tr, d), lambda i: (i, 0))
    vec = pl.BlockSpec((1, d), lambda i: (0, 0))
    return pl.pallas_call(body, name=name, grid=(s // tr,), in_specs=[row, vec, vec, vec], out_specs=row,
                          out_shape=jax.ShapeDtypeStruct((s, d), BF16), compiler_params=_params("parallel"))(x, g, scale, shift)


def rms_fwd_cols(name, z, off, width, g):
    s = z.shape[0]
    tr = _tile(s, ROW_TILE)
    assert off % width == 0

    def body(x_ref, g_ref, o_ref):
        _, xh = _rms_stats(x_ref[...])
        o_ref[...] = (xh * g_ref[...]).astype(o_ref.dtype)

    return pl.pallas_call(body, name=name, grid=(s // tr,),
                          in_specs=[pl.BlockSpec((tr, width), lambda i: (i, off // width)), pl.BlockSpec((1, width), lambda i: (0, 0))],
                          out_specs=pl.BlockSpec((tr, width), lambda i: (i, 0)),
                          out_shape=jax.ShapeDtypeStruct((s, width), BF16), compiler_params=_params("parallel"))(z, g)


def rms_bwd_cols(name, dy, z, off, width, g):
    s = z.shape[0]
    tr = _tile(s, ROW_TILE)

    def body(dy_ref, x_ref, g_ref, dx_ref, gg_ref):
        @pl.when(pl.program_id(0) == 0)
        def _():
            gg_ref[...] = jnp.zeros_like(gg_ref)

        inv, xh = _rms_stats(x_ref[...])
        dy_v = dy_ref[...]
        gg_ref[...] += _colsum(dy_v * xh)
        dx_ref[...] = _rms_bwd(dy_v * g_ref[...], xh, inv).astype(dx_ref.dtype)

    return pl.pallas_call(body, name=name, grid=(s // tr,),
                          in_specs=[pl.BlockSpec((tr, width), lambda i: (i, 0)), pl.BlockSpec((tr, width), lambda i: (i, off // width)),
                                    pl.BlockSpec((1, width), lambda i: (0, 0))],
                          out_specs=[pl.BlockSpec((tr, width), lambda i: (i, 0)), pl.BlockSpec((1, width), lambda i: (0, 0))],
                          out_shape=[jax.ShapeDtypeStruct((s, width), BF16), jax.ShapeDtypeStruct((1, width), F32)],
                          compiler_params=_params("arbitrary"))(dy, z, g)


def post_res_fwd(name, x, y, gate, g):
    s, d = x.shape
    tr = _tile(s, ROW_TILE)

    def body(x_ref, y_ref, gate_ref, g_ref, o_ref):
        _, yh = _rms_stats(y_ref[...])
        o_ref[...] = x_ref[...] + gate_ref[...] * (yh * g_ref[...])

    row = pl.BlockSpec((tr, d), lambda i: (i, 0))
    vec = pl.BlockSpec((1, d), lambda i: (0, 0))
    return pl.pallas_call(body, name=name, grid=(s // tr,), in_specs=[row, row, vec, vec], out_specs=row,
                          out_shape=jax.ShapeDtypeStruct((s, d), F32), compiler_params=_params("parallel"))(x, y, gate, g)


def post2_loss_bwd(x1, ffn, target, gate2, g):
    s, d = x1.shape
    tr = _tile(s, ROW_TILE)

    def body(x_ref, y_ref, t_ref, gate_ref, g_ref, loss_ref, dout_ref, dy_ref, acc_ref):
        @pl.when(pl.program_id(0) == 0)
        def _():
            loss_ref[...] = jnp.zeros_like(loss_ref)
            acc_ref[...] = jnp.zeros_like(acc_ref)

        inv, yh = _rms_stats(y_ref[...])
        r = yh * g_ref[...]
        err = x_ref[...] + gate_ref[...] * r - t_ref[...]
        loss_ref[...] += 0.5 * jnp.sum(jnp.mean(err * err, axis=-1, keepdims=True))
        dout = err / d
        dout_ref[...] = dout
        dr = dout * gate_ref[...]
        acc_ref[0:1, :] += _colsum(dout * r)
        acc_ref[1:2, :] += _colsum(dr * yh)
        dy_ref[...] = _rms_bwd(dr * g_ref[...], yh, inv).astype(dy_ref.dtype)

    row = pl.BlockSpec((tr, d), lambda i: (i, 0))
    vec = pl.BlockSpec((1, d), lambda i: (0, 0))
    return pl.pallas_call(
        body, name="post2_loss_bwd", grid=(s // tr,), in_specs=[row, row, row, vec, vec],
        out_specs=[_full((8, LANES)), row, row, _full((8, d))],
        out_shape=[jax.ShapeDtypeStruct((8, LANES), F32), jax.ShapeDtypeStruct((s, d), F32),
                   jax.ShapeDtypeStruct((s, d), BF16), jax.ShapeDtypeStruct((8, d), F32)],
        compiler_params=_params("arbitrary"))(x1, ffn, target, gate2, g)


def mid_bwd(dh2, dout, x1, y1, pre2_g, scale2, gate1, post1_g):
    s, d = x1.shape
    tr = _tile(s, ROW_TILE)

    def body(dh_ref, dout_ref, x_ref, y_ref, g2_ref, sc_ref, gate_ref, g1_ref, dx_ref, dy_ref, acc_ref):
        @pl.when(pl.program_id(0) == 0)
        def _():
            acc_ref[...] = jnp.zeros_like(acc_ref)

        dh = dh_ref[...]
        inv2, xh = _rms_stats(x_ref[...])
        acc_ref[0:1, :] += _colsum(dh)
        acc_ref[1:2, :] += _colsum(dh * (xh * g2_ref[...]))
        t = dh * (1.0 + sc_ref[...])
        acc_ref[2:3, :] += _colsum(t * xh)
        dx1 = dout_ref[...] + _rms_bwd(t * g2_ref[...], xh, inv2)
        dx_ref[...] = dx1
        inv1, yh = _rms_stats(y_ref[...])
        acc_ref[3:4, :] += _colsum(dx1 * (yh * g1_ref[...]))
        dr = dx1 * gate_ref[...]
        acc_ref[4:5, :] += _colsum(dr * yh)
        dy_ref[...] = _rms_bwd(dr * g1_ref[...], yh, inv1).astype(dy_ref.dtype)

    row = pl.BlockSpec((tr, d), lambda i: (i, 0))
    vec = pl.BlockSpec((1, d), lambda i: (0, 0))
    return pl.pallas_call(
        body, name="mid_bwd", grid=(s // tr,), in_specs=[row, row, row, row, vec, vec, vec, vec],
        out_specs=[row, row, _full((8, d))],
        out_shape=[jax.ShapeDtypeStruct((s, d), F32), jax.ShapeDtypeStruct((s, d), BF16), jax.ShapeDtypeStruct((8, d), F32)],
        compiler_params=_params("arbitrary"))(dh2, dout, x1, y1, pre2_g, scale2, gate1, post1_g)


def pre1_bwd(dh1, dx1, x, pre1_g, scale1):
    s, d = x.shape
    tr = _tile(s, ROW_TILE)

    def body(dh_ref, dx1_ref, x_ref, g_ref, sc_ref, dx_ref, acc_ref):
        @pl.when(pl.program_id(0) == 0)
        def _():
            acc_ref[...] = jnp.zeros_like(acc_ref)

        dh = dh_ref[...]
        inv, xh = _rms_stats(x_ref[...])
        acc_ref[0:1, :] += _colsum(dh)
        acc_ref[1:2, :] += _colsum(dh * (xh * g_ref[...]))
        t = dh * (1.0 + sc_ref[...])
        acc_ref[2:3, :] += _colsum(t * xh)
        dx_ref[...] = dx1_ref[...] + _rms_bwd(t * g_ref[...], xh, inv)

    row = pl.BlockSpec((tr, d), lambda i: (i, 0))
    vec = pl.BlockSpec((1, d), lambda i: (0, 0))
    return pl.pallas_call(
        body, name="pre1_bwd", grid=(s // tr,), in_specs=[row, row, row, vec, vec], out_specs=[row, _full((8, d))],
        out_shape=[jax.ShapeDtypeStruct((s, d), F32), jax.ShapeDtypeStruct((8, d), F32)],
        compiler_params=_params("arbitrary"))(dh1, dx1, x, pre1_g, scale1)


def _ln_stats(v):
    mu = jnp.mean(v, axis=-1, keepdims=True)
    vc = v - mu
    rstd = lax.rsqrt(jnp.mean(vc * vc, axis=-1, keepdims=True) + EPS)
    return rstd, vc * rstd


def gmlp_fwd(z, width, ln_g, ln_b, wm, bs3):
    s = z.shape[0]
    groups = width // CHUNK

    def body(u_ref, v_ref, g_ref, b_ref, wm_ref, bs_ref, a_ref):
        ug = _gelu(u_ref[...])
        _, vh = _ln_stats(_gelu(v_ref[...]))
        vn = (vh * g_ref[...] + b_ref[...]).astype(BF16)
        for g in range(groups):
            cols = slice(g * CHUNK, (g + 1) * CHUNK)
            mixed = jnp.dot(wm_ref[g], vn[:, cols], preferred_element_type=F32) + bs_ref[g]
            a_ref[:, cols] = (ug[:, cols] * mixed).astype(a_ref.dtype)

    vec = pl.BlockSpec((1, width), lambda n: (0, 0))
    return pl.pallas_call(
        body, name="gmlp_fwd", grid=(s // CHUNK,),
        in_specs=[pl.BlockSpec((CHUNK, width), lambda n: (n, 0)), pl.BlockSpec((CHUNK, width), lambda n: (n, 1)), vec, vec,
                  _full(wm.shape), _full(bs3.shape)],
        out_specs=pl.BlockSpec((CHUNK, width), lambda n: (n, 0)),
        out_shape=jax.ShapeDtypeStruct((s, width), BF16), compiler_params=_params("parallel"))(z, z, ln_g, ln_b, wm, bs3)


def gmlp_bwd(z, width, da, ln_g, ln_b, wm, bs3):
    s = z.shape[0]
    groups = width // CHUNK

    def body(u_ref, v_ref, da_ref, g_ref, b_ref, wm_ref, bs_ref, duv_ref, gw_ref, gb_ref, acc_ref, dvn_ref):
        @pl.when(pl.program_id(0) == 0)
        def _():
            gw_ref[...] = jnp.zeros_like(gw_ref)
            gb_ref[...] = jnp.zeros_like(gb_ref)
            acc_ref[...] = jnp.zeros_like(acc_ref)

        ug, dug = _gelu_and_grad(u_ref[...])
        vg, dvg = _gelu_and_grad(v_ref[...])
        rstd, vh = _ln_stats(vg)
        vn = (vh * g_ref[...] + b_ref[...]).astype(BF16)
        da_v = da_ref[...]
        for g in range(groups):
            cols = slice(g * CHUNK, (g + 1) * CHUNK)
            mixed = jnp.dot(wm_ref[g], vn[:, cols], preferred_element_type=F32) + bs_ref[g]
            duv_ref[:, cols] = (da_v[:, cols] * mixed * dug[:, cols]).astype(duv_ref.dtype)
            dm = da_v[:, cols] * ug[:, cols]
            gb_ref[g] += jnp.sum(dm, axis=-1, keepdims=True)
            dmb = dm.astype(BF16)
            gw_ref[g] += lax.dot_general(dmb, vn[:, cols], (((1,), (1,)), ((), ())), preferred_element_type=F32)
            dvn_ref[:, cols] = lax.dot_general(wm_ref[g], dmb, (((0,), (0,)), ((), ())), preferred_element_type=F32)
        dvn = dvn_ref[...]
        acc_ref[0:1, :] += _colsum(dvn * vh)
        acc_ref[1:2, :] += _colsum(dvn)
        dvh = dvn * g_ref[...]
        dv = rstd * (dvh - jnp.mean(dvh, axis=-1, keepdims=True) - vh * jnp.mean(dvh * vh, axis=-1, keepdims=True))
        duv_ref[:, width:] = (dv * dvg).astype(duv_ref.dtype)

        @pl.when(pl.program_id(0) == pl.num_programs(0) - 1)
        def _():
            q = lax.broadcasted_iota(jnp.int32, gw_ref.shape, 1)
            p = lax.broadcasted_iota(jnp.int32, gw_ref.shape, 2)
            gw_ref[...] = jnp.where(p <= q, gw_ref[...], 0.0)

    vec = pl.BlockSpec((1, width), lambda n: (0, 0))
    blk = pl.BlockSpec((CHUNK, width), lambda n: (n, 0))
    return pl.pallas_call(
        body, name="gmlp_bwd", grid=(s // CHUNK,),
        in_specs=[blk, pl.BlockSpec((CHUNK, width), lambda n: (n, 1)), blk, vec, vec, _full(wm.shape), _full(bs3.shape)],
        out_specs=[pl.BlockSpec((CHUNK, 2 * width), lambda n: (n, 0)), _full(wm.shape), _full(bs3.shape), _full((8, width))],
        out_shape=[jax.ShapeDtypeStruct((s, 2 * width), BF16), jax.ShapeDtypeStruct(wm.shape, F32),
                   jax.ShapeDtypeStruct(bs3.shape, F32), jax.ShapeDtypeStruct((8, width), F32)],
        scratch_shapes=[pltpu.VMEM((CHUNK, width), F32)],
        compiler_params=_params("arbitrary"))(z, z, da, ln_g, ln_b, wm, bs3)


def merge_fwd(z, off_a, off_b, ya, yb):
    s, d = ya.shape
    tr, tc = _tile(s, ROW_TILE * 2), _tile(d, COL_TILE)
    assert off_a % tc == 0 and off_b % tc == 0

    def body(ga_ref, gb_ref, ya_ref, yb_ref, o_ref):
        o_ref[...] = (_sigmoid(ga_ref[...]) * ya_ref[...] + _sigmoid(gb_ref[...]) * yb_ref[...]).astype(o_ref.dtype)

    blk = pl.BlockSpec((tr, tc), lambda i, j: (i, j))
    return pl.pallas_call(
        body, name="merge_fwd", grid=(s // tr, d // tc),
        in_specs=[pl.BlockSpec((tr, tc), lambda i, j: (i, off_a // tc + j)), pl.BlockSpec((tr, tc), lambda i, j: (i, off_b // tc + j)), blk, blk],
        out_specs=blk, out_shape=jax.ShapeDtypeStruct((s, d), BF16), compiler_params=_params("parallel", "parallel"))(z, z, ya, yb)


def merge_bwd(z, off_a, off_b, ya, yb, dm):
    s, d = ya.shape
    tr, tc = _tile(s, ROW_TILE * 2), _tile(d, COL_TILE)
    nc = d // tc

    def body(ga_ref, gb_ref, ya_ref, yb_ref, dm_ref, dya_ref, dyb_ref, dga_ref, dgb_ref):
        dm_v = dm_ref[...]
        sa, sb = _sigmoid(ga_ref[...]), _sigmoid(gb_ref[...])
        dya_ref[...] = (dm_v * sa).astype(dya_ref.dtype)
        dyb_ref[...] = (dm_v * sb).astype(dyb_ref.dtype)
        dga_ref[...] = (dm_v * ya_ref[...] * sa * (1.0 - sa)).astype(dga_ref.dtype)
        dgb_ref[...] = (dm_v * yb_ref[...] * sb * (1.0 - sb)).astype(dgb_ref.dtype)

    blk = pl.BlockSpec((tr, tc), lambda i, j: (i, j))
    out = jax.ShapeDtypeStruct((s, d), BF16)
    return pl.pallas_call(
        body, name="merge_bwd", grid=(s // tr, nc),
        in_specs=[pl.BlockSpec((tr, tc), lambda i, j: (i, off_a // tc + j)), pl.BlockSpec((tr, tc), lambda i, j: (i, off_b // tc + j)), blk, blk, blk],
        out_specs=[blk, blk, blk, blk], out_shape=[out, out, out, out],
        compiler_params=_params("parallel", "parallel"))(z, z, ya, yb, dm)


_ATT_SCALE = (QK_NOPE + QK_ROPE) ** -0.5
_NEG = -1e30


def rope_k(z, off, cos4, sin4):
    s = z.shape[0]
    tr = _tile(s, ROW_TILE * 2)
    assert off % LANES == 0

    def body(k_ref, c_ref, s_ref, o_ref):
        k = k_ref[...]
        k = k + pltpu.roll(k, QK_ROPE, 1)
        o_ref[...] = _rope(k, c_ref[...], s_ref[...]).astype(o_ref.dtype)

    row = pl.BlockSpec((tr, LANES), lambda i: (i, 0))
    return pl.pallas_call(body, name="rope_k", grid=(s // tr,),
                          in_specs=[pl.BlockSpec((tr, LANES), lambda i: (i, off // LANES)), row, row], out_specs=row,
                          out_shape=jax.ShapeDtypeStruct((s, LANES), BF16), compiler_params=_params("parallel"))(z, cos4, sin4)


def _head_masks(shape):
    lane = lax.broadcasted_iota(jnp.int32, shape, 1)
    return lane < QK_ROPE, lane >= QK_ROPE


def _scores(qn, qp_h, k, kp, qi, kb, t):
    sc = lax.dot_general(qn, k, (((1,), (1,)), ((), ())), preferred_element_type=F32)
    sc += lax.dot_general(qp_h, kp, (((1,), (1,)), ((), ())), preferred_element_type=F32)
    sc = sc * _ATT_SCALE
    row = lax.broadcasted_iota(jnp.int32, sc.shape, 0) + qi * t
    col = lax.broadcasted_iota(jnp.int32, sc.shape, 1) + kb * t
    return jnp.where(col <= row, sc, _NEG)


def attn_fwd(qn, qp, kv, kpr, cos4, sin4):
    s = qn.shape[0]
    hp = HEADS // 2
    t = _tile(s, ATT_TILE)
    nq = s // t

    def body(qn_ref, qp_ref, kv_ref, kp_ref, c_ref, s_ref, o_ref, qpr_ref, l_ref):
        qi = pl.program_id(1)
        qpr = _rope(qp_ref[...], c_ref[...], s_ref[...]).astype(BF16)
        qpr_ref[...] = qpr
        masks = _head_masks(qpr.shape)
        for hh in range(2):
            q_n = qn_ref[:, hh * QK_NOPE:(hh + 1) * QK_NOPE]
            q_p = jnp.where(masks[hh], qpr, jnp.zeros_like(qpr))
            kc, vc = 2 * hh * QK_NOPE, (2 * hh + 1) * QK_NOPE

            def step(kb, carry):
                m, l, acc = carry
                rows = pl.ds(pl.multiple_of(kb * t, t), t)
                sc = _scores(q_n, q_p, kv_ref[rows, kc:kc + QK_NOPE], kp_ref[rows, :], qi, kb, t)
                m_new = jnp.maximum(m, jnp.max(sc, axis=-1, keepdims=True))
                alpha = jnp.exp(m - m_new)
                p = jnp.exp(sc - m_new)
                l = alpha * l + jnp.sum(p, axis=-1, keepdims=True)
                acc = alpha * acc + jnp.dot(p.astype(BF16), kv_ref[rows, vc:vc + V_HEAD], preferred_element_type=F32)
                return m_new, l, acc

            init = (jnp.full((t, 1), _NEG, F32), jnp.zeros((t, 1), F32), jnp.zeros((t, V_HEAD), F32))
            m, l, acc = lax.fori_loop(0, qi + 1, step, init)
            o_ref[:, hh * V_HEAD:(hh + 1) * V_HEAD] = acc / l
            l_ref[:, hh:hh + 1] = m + jnp.log(l)

    return pl.pallas_call(
        body, name="attn_fwd", grid=(hp, nq),
        in_specs=[pl.BlockSpec((t, 2 * QK_NOPE), lambda h, i: (i, h)), pl.BlockSpec((t, LANES), lambda h, i: (i, h)),
                  pl.BlockSpec((s, 4 * QK_NOPE), lambda h, i: (0, h)), _full((s, LANES)),
                  pl.BlockSpec((t, LANES), lambda h, i: (i, 0)), pl.BlockSpec((t, LANES), lambda h, i: (i, 0))],
        out_specs=[pl.BlockSpec((t, 2 * V_HEAD), lambda h, i: (i, h)), pl.BlockSpec((t, LANES), lambda h, i: (i, h)),
                   pl.BlockSpec((None, t, 2), lambda h, i: (h, i, 0))],
        out_shape=[jax.ShapeDtypeStruct((s, HEADS * V_HEAD), F32), jax.ShapeDtypeStruct((s, HEADS * QK_ROPE), BF16),
                   jax.ShapeDtypeStruct((hp, s, 2), F32)],
        compiler_params=_params("parallel", "parallel"))(qn, qp, kv, kpr, cos4, sin4)


def attn_bwd_q(qn, qpr, kv, kpr, o, do, lse, cos4, sin4):
    s = qn.shape[0]
    hp = HEADS // 2
    t = _tile(s, ATT_TILE)
    nq = s // t

    def body(qn_ref, qpr_ref, kv_ref, kp_ref, o_ref, do_ref, l_ref, c_ref, s_ref, dqn_ref, dqp_ref):
        qi = pl.program_id(1)
        qpr = qpr_ref[...]
        masks = _head_masks(qpr.shape)
        dqp = jnp.zeros(qpr.shape, F32)
        for hh in range(2):
            q_n = qn_ref[:, hh * QK_NOPE:(hh + 1) * QK_NOPE]
            q_p = jnp.where(masks[hh], qpr, jnp.zeros_like(qpr))
            kc, vc = 2 * hh * QK_NOPE, (2 * hh + 1) * QK_NOPE
            do_h = do_ref[:, hh * V_HEAD:(hh + 1) * V_HEAD]
            delta = jnp.sum(do_h * o_ref[:, hh * V_HEAD:(hh + 1) * V_HEAD], axis=-1, keepdims=True)
            do_b = do_h.astype(BF16)
            lse_h = l_ref[:, hh:hh + 1]

            def step(kb, carry):
                dn, dp_ = carry
                rows = pl.ds(pl.multiple_of(kb * t, t), t)
                k = kv_ref[rows, kc:kc + QK_NOPE]
                kp = kp_ref[rows, :]
                p = jnp.exp(_scores(q_n, q_p, k, kp, qi, kb, t) - lse_h)
                dpv = lax.dot_general(do_b, kv_ref[rows, vc:vc + V_HEAD], (((1,), (1,)), ((), ())), preferred_element_type=F32)
                ds = (p * (dpv - delta) * _ATT_SCALE).astype(BF16)
                dn = dn + jnp.dot(ds, k, preferred_element_type=F32)
                dp_ = dp_ + jnp.dot(ds, kp, preferred_element_type=F32)
                return dn, dp_

            dn, dp_h = lax.fori_loop(0, qi + 1, step, (jnp.zeros((t, QK_NOPE), F32), jnp.zeros((t, LANES), F32)))
            dqn_ref[:, hh * QK_NOPE:(hh + 1) * QK_NOPE] = dn.astype(dqn_ref.dtype)
            dqp = dqp + jnp.where(masks[hh], dp_h, jnp.zeros_like(dp_h))
        dqp_ref[...] = _rope(dqp, c_ref[...], -s_ref[...]).astype(dqp_ref.dtype)

    qblk = pl.BlockSpec((t, 2 * QK_NOPE), lambda h, i: (i, h))
    pblk = pl.BlockSpec((t, LANES), lambda h, i: (i, h))
    tab = pl.BlockSpec((t, LANES), lambda h, i: (i, 0))
    return pl.pallas_call(
        body, name="attn_bwd_q", grid=(hp, nq),
        in_specs=[qblk, pblk, pl.BlockSpec((s, 4 * QK_NOPE), lambda h, i: (0, h)), _full((s, LANES)), qblk, qblk,
                  pl.BlockSpec((None, t, 2), lambda h, i: (h, i, 0)), tab, tab],
        out_specs=[qblk, pblk],
        out_shape=[jax.ShapeDtypeStruct((s, HEADS * QK_NOPE), BF16), jax.ShapeDtypeStruct((s, HEADS * QK_ROPE), BF16)],
        compiler_params=_params("parallel", "parallel"))(qn, qpr, kv, kpr, o, do, lse, cos4, sin4)


def attn_bwd_kv(qn, qpr, kv, kpr, o, do, lse):
    s = qn.shape[0]
    hp = HEADS // 2
    t = _tile(s, ATT_TILE)
    nq = s // t

    def body(qn_ref, qpr_ref, kv_ref, kp_ref, o_ref, do_ref, l_ref, dkv_ref, dkp_ref):
        ki = pl.program_id(1)
        rows_k = pl.ds(pl.multiple_of(ki * t, t), t)
        kp = kp_ref[rows_k, :]
        dkp = jnp.zeros((t, LANES), F32)
        for hh in range(2):
            kc, vc = 2 * hh * QK_NOPE, (2 * hh + 1) * QK_NOPE
            k = kv_ref[rows_k, kc:kc + QK_NOPE]
            v = kv_ref[rows_k, vc:vc + V_HEAD]

            def step(qb, carry):
                dk, dv, dkp_h = carry
                rows = pl.ds(pl.multiple_of(qb * t, t), t)
                q_n = qn_ref[rows, hh * QK_NOPE:(hh + 1) * QK_NOPE]
                qpr = qpr_ref[rows, :]
                lane = lax.broadcasted_iota(jnp.int32, qpr.shape, 1)
                sel = (lane < QK_ROPE) if hh == 0 else (lane >= QK_ROPE)
                q_p = jnp.where(sel, qpr, jnp.zeros_like(qpr))
                do_h = do_ref[rows, hh * V_HEAD:(hh + 1) * V_HEAD]
                delta = jnp.sum(do_h * o_ref[rows, hh * V_HEAD:(hh + 1) * V_HEAD], axis=-1, keepdims=True)
                do_b = do_h.astype(BF16)
                p = jnp.exp(_scores(q_n, q_p, k, kp, qb, ki, t) - l_ref[rows, hh:hh + 1])
                dpv = lax.dot_general(do_b, v, (((1,), (1,)), ((), ())), preferred_element_type=F32)
                ds = (p * (dpv - delta) * _ATT_SCALE).astype(BF16)
                dv = dv + lax.dot_general(p.astype(BF16), do_b, (((0,), (0,)), ((), ())), preferred_element_type=F32)
                dk = dk + lax.dot_general(ds, q_n, (((0,), (0,)), ((), ())), preferred_element_type=F32)
                dkp_h = dkp_h + lax.dot_general(ds, q_p, (((0,), (0,)), ((), ())), preferred_element_type=F32)
                return dk, dv, dkp_h

            init = (jnp.zeros((t, QK_NOPE), F32), jnp.zeros((t, V_HEAD), F32), jnp.zeros((t, LANES), F32))
            dk, dv, dkp_h = lax.fori_loop(ki, nq, step, init)
            dkv_ref[:, kc:kc + QK_NOPE] = dk.astype(dkv_ref.dtype)
            dkv_ref[:, vc:vc + V_HEAD] = dv.astype(dkv_ref.dtype)
            dkp = dkp + dkp_h
        dkp_ref[...] = dkp

    return pl.pallas_call(
        body, name="attn_bwd_kv", grid=(hp, nq),
        in_specs=[pl.BlockSpec((s, 2 * QK_NOPE), lambda h, i: (0, h)), pl.BlockSpec((s, LANES), lambda h, i: (0, h)),
                  pl.BlockSpec((s, 4 * QK_NOPE), lambda h, i: (0, h)), _full((s, LANES)),
                  pl.BlockSpec((s, 2 * V_HEAD), lambda h, i: (0, h)), pl.BlockSpec((s, 2 * V_HEAD), lambda h, i: (0, h)),
                  pl.BlockSpec((None, s, 2), lambda h, i: (h, 0, 0))],
        out_specs=[pl.BlockSpec((t, 4 * QK_NOPE), lambda h, i: (i, h)), pl.BlockSpec((None, t, LANES), lambda h, i: (h, i, 0))],
        out_shape=[jax.ShapeDtypeStruct((s, HEADS * 2 * QK_NOPE), BF16), jax.ShapeDtypeStruct((hp, s, LANES), F32)],
        compiler_params=_params("parallel", "parallel"))(qn, qpr, kv, kpr, o, do, lse)


def _dot_nt(a, b):
    return lax.dot_general(a, b, (((1,), (1,)), ((), ())), preferred_element_type=F32)


def _dot_tn(a, b):
    return lax.dot_general(a, b, (((0,), (0,)), ((), ())), preferred_element_type=F32)


def _q_cat(q_n, qpr, hh):
    lane = lax.broadcasted_iota(jnp.int32, qpr.shape, 1)
    sel = (lane < QK_ROPE) if hh == 0 else (lane >= QK_ROPE)
    return jnp.concatenate([q_n, jnp.where(sel, qpr, jnp.zeros_like(qpr))], axis=1)


def _causal(sc):
    row = lax.broadcasted_iota(jnp.int32, sc.shape, 0)
    col = lax.broadcasted_iota(jnp.int32, sc.shape, 1)
    return jnp.where(col <= row, sc, _NEG)


def attn_fwd2(qn, qp, kv, kpr, cos4, sin4):
    s = qn.shape[0]
    hp = HEADS // 2
    t = _tile(s, ATT_TILE)
    nq = s // t

    def body(qn_ref, qp_ref, kv_ref, kp_ref, c_ref, s_ref, o_ref, qpr_ref, l_ref, kcat_ref):
        qi = pl.program_id(1)

        @pl.when(qi == 0)
        def _():
            for hh in range(2):
                kcat_ref[hh, :, 0:QK_NOPE] = kv_ref[:, 2 * hh * QK_NOPE:(2 * hh + 1) * QK_NOPE]
                kcat_ref[hh, :, QK_NOPE:] = kp_ref[...]

        qpr = _rope(qp_ref[...], c_ref[...], s_ref[...]).astype(BF16)
        qpr_ref[...] = qpr
        qcat = [_q_cat(qn_ref[:, hh * QK_NOPE:(hh + 1) * QK_NOPE], qpr, hh) for hh in range(2)]

        def block(kb, carry, diagonal):
            rows = pl.ds(pl.multiple_of(kb * t, t), t)
            out = []
            for hh in range(2):
                m, l, acc = carry[hh]
                sc = _dot_nt(qcat[hh], kcat_ref[hh, rows, :]) * _ATT_SCALE
                if diagonal:
                    sc = _causal(sc)
                m_new = jnp.maximum(m, jnp.max(sc, axis=-1, keepdims=True))
                alpha = jnp.exp(m - m_new)
                p = jnp.exp(sc - m_new)
                l = alpha * l + jnp.sum(p, axis=-1, keepdims=True)
                v = kv_ref[rows, (2 * hh + 1) * QK_NOPE:(2 * hh + 2) * QK_NOPE]
                acc = alpha * acc + jnp.dot(p.astype(BF16), v, preferred_element_type=F32)
                out.append((m_new, l, acc))
            return tuple(out)

        one = (jnp.full((t, 1), _NEG, F32), jnp.zeros((t, 1), F32), jnp.zeros((t, V_HEAD), F32))
        carry = lax.fori_loop(0, qi, lambda kb, cr: block(kb, cr, False), (one, one))
        carry = block(qi, carry, True)
        for hh in range(2):
            m, l, acc = carry[hh]
            o_ref[:, hh * V_HEAD:(hh + 1) * V_HEAD] = acc / l
            l_ref[:, hh:hh + 1] = m + jnp.log(l)

    return pl.pallas_call(
        body, name="attn_fwd", grid=(hp, nq),
        in_specs=[pl.BlockSpec((t, 2 * QK_NOPE), lambda h, i: (i, h)), pl.BlockSpec((t, LANES), lambda h, i: (i, h)),
                  pl.BlockSpec((s, 4 * QK_NOPE), lambda h, i: (0, h)), _full((s, LANES)),
                  pl.BlockSpec((t, LANES), lambda h, i: (i, 0)), pl.BlockSpec((t, LANES), lambda h, i: (i, 0))],
        out_specs=[pl.BlockSpec((t, 2 * V_HEAD), lambda h, i: (i, h)), pl.BlockSpec((t, LANES), lambda h, i: (i, h)),
                   pl.BlockSpec((None, t, 2), lambda h, i: (h, i, 0))],
        out_shape=[jax.ShapeDtypeStruct((s, HEADS * V_HEAD), F32), jax.ShapeDtypeStruct((s, HEADS * QK_ROPE), BF16),
                   jax.ShapeDtypeStruct((hp, s, 2), F32)],
        scratch_shapes=[pltpu.VMEM((2, s, 2 * QK_NOPE), BF16)],
        compiler_params=_params("parallel", "arbitrary"))(qn, qp, kv, kpr, cos4, sin4)


def attn_bwd2(qn, qpr, kv, kpr, o, do, lse, cos4, sin4):
    s = qn.shape[0]
    hp = HEADS // 2
    t = _tile(s, ATT_TILE)
    nk = s // t

    def body(qn_ref, qpr_ref, kv_ref, kp_ref, o_ref, do_ref, l_ref, c_ref, s_ref,
             dqn_ref, dqp_ref, dkv_ref, dkp_ref, qcat_ref, dq_ref, delta_ref):
        ki = pl.program_id(1)

        @pl.when(ki == 0)
        def _():
            dq_ref[...] = jnp.zeros_like(dq_ref)
            for hh in range(2):
                qcat_ref[hh] = _q_cat(qn_ref[:, hh * QK_NOPE:(hh + 1) * QK_NOPE], qpr_ref[...], hh)
                cols = slice(hh * V_HEAD, (hh + 1) * V_HEAD)
                delta_ref[hh] = jnp.sum(do_ref[:, cols] * o_ref[:, cols], axis=-1, keepdims=True)

        rows_k = pl.ds(pl.multiple_of(ki * t, t), t)
        kcat = [jnp.concatenate([kv_ref[rows_k, 2 * hh * QK_NOPE:(2 * hh + 1) * QK_NOPE], kp_ref[rows_k, :]], axis=1) for hh in range(2)]
        vs = [kv_ref[rows_k, (2 * hh + 1) * QK_NOPE:(2 * hh + 2) * QK_NOPE] for hh in range(2)]

        def block(qb, carry, diagonal):
            rows = pl.ds(pl.multiple_of(qb * t, t), t)
            out = []
            for hh in range(2):
                dkc, dv = carry[hh]
                q_c = qcat_ref[hh, rows, :]
                do_b = do_ref[rows, hh * V_HEAD:(hh + 1) * V_HEAD].astype(BF16)
                sc = _dot_nt(q_c, kcat[hh]) * _ATT_SCALE
                if diagonal:
                    sc = _causal(sc)
                p = jnp.exp(sc - l_ref[rows, hh:hh + 1])
                dpv = _dot_nt(do_b, vs[hh])
                ds = (p * (dpv - delta_ref[hh, rows, :]) * _ATT_SCALE).astype(BF16)
                dv = dv + _dot_tn(p.astype(BF16), do_b)
                dkc = dkc + _dot_tn(ds, q_c)
                dq_ref[hh, rows, :] += jnp.dot(ds, kcat[hh], preferred_element_type=F32)
                out.append((dkc, dv))
            return tuple(out)

        one = (jnp.zeros((t, 2 * QK_NOPE), F32), jnp.zeros((t, V_HEAD), F32))
        carry = block(ki, (one, one), True)
        carry = lax.fori_loop(ki + 1, nk, lambda qb, cr: block(qb, cr, False), carry)
        dkp = jnp.zeros((t, LANES), F32)
        for hh in range(2):
            dkc, dv = carry[hh]
            dkv_ref[:, 2 * hh * QK_NOPE:(2 * hh + 1) * QK_NOPE] = dkc[:, :QK_NOPE].astype(dkv_ref.dtype)
            dkv_ref[:, (2 * hh + 1) * QK_NOPE:(2 * hh + 2) * QK_NOPE] = dv.astype(dkv_ref.dtype)
            dkp = dkp + dkc[:, QK_NOPE:]
        dkp_ref[...] = dkp

        @pl.when(ki == nk - 1)
        def _():
            lane = lax.broadcasted_iota(jnp.int32, (s, LANES), 1)
            dqp = jnp.where(lane < QK_ROPE, dq_ref[0, :, QK_NOPE:], dq_ref[1, :, QK_NOPE:])
            dqp_ref[...] = _rope(dqp, c_ref[...], -s_ref[...]).astype(dqp_ref.dtype)
            for hh in range(2):
                dqn_ref[:, hh * QK_NOPE:(hh + 1) * QK_NOPE] = dq_ref[hh, :, :QK_NOPE].astype(dqn_ref.dtype)

    qblk = pl.BlockSpec((s, 2 * QK_NOPE), lambda h, i: (0, h))
    pblk = pl.BlockSpec((s, LANES), lambda h, i: (0, h))
    tab = _full((s, LANES))
    return pl.pallas_call(
        body, name="attn_bwd", grid=(hp, nk),
        in_specs=[qblk, pblk, pl.BlockSpec((s, 4 * QK_NOPE), lambda h, i: (0, h)), tab, qblk, qblk,
                  pl.BlockSpec((None, s, 2), lambda h, i: (h, 0, 0)), tab, tab],
        out_specs=[qblk, pblk, pl.BlockSpec((t, 4 * QK_NOPE), lambda h, i: (i, h)), pl.BlockSpec((None, t, LANES), lambda h, i: (h, i, 0))],
        out_shape=[jax.ShapeDtypeStruct((s, HEADS * QK_NOPE), BF16), jax.ShapeDtypeStruct((s, HEADS * QK_ROPE), BF16),
                   jax.ShapeDtypeStruct((s, HEADS * 2 * QK_NOPE), BF16), jax.ShapeDtypeStruct((hp, s, LANES), F32)],
        scratch_shapes=[pltpu.VMEM((2, s, 2 * QK_NOPE), BF16), pltpu.VMEM((2, s, 2 * QK_NOPE), F32), pltpu.VMEM((2, s, 1), F32)],
        compiler_params=_params("parallel", "arbitrary"))(qn, qpr, kv, kpr, o, do, lse, cos4, sin4)


def kpe_bwd(dkp, cos4, sin4, pad_cols):
    hp, s, _ = dkp.shape
    tr = _tile(s, ROW_TILE * 2)

    def body(d_ref, c_ref, s_ref, o_ref):
        tot = d_ref[0]
        for h in range(1, hp):
            tot = tot + d_ref[h]
        tot = tot + pltpu.roll(tot, QK_ROPE, 1)
        lane = lax.broadcasted_iota(jnp.int32, tot.shape, 1)
        dk = jnp.where(lane < QK_ROPE, _rope(tot, c_ref[...], -s_ref[...]), jnp.zeros_like(tot))
        o_ref[...] = jnp.zeros_like(o_ref)
        o_ref[:, 0:LANES] = dk.astype(o_ref.dtype)

    row = pl.BlockSpec((tr, LANES), lambda i: (i, 0))
    return pl.pallas_call(body, name="kpe_bwd", grid=(s // tr,),
                          in_specs=[pl.BlockSpec((hp, tr, LANES), lambda i: (0, i, 0)), row, row],
                          out_specs=pl.BlockSpec((tr, pad_cols), lambda i: (i, 0)),
                          out_shape=jax.ShapeDtypeStruct((s, pad_cols), BF16), compiler_params=_params("parallel"))(dkp, cos4, sin4)


def _shift_down(x, n):
    row = lax.broadcasted_iota(jnp.int32, x.shape, 0)
    return jnp.where(row >= n, pltpu.roll(x, n, 0), jnp.zeros_like(x))


def _shift_up(x, n):
    rows = x.shape[0]
    row = lax.broadcasted_iota(jnp.int32, x.shape, 0)
    return jnp.where(row < rows - n, pltpu.roll(x, rows - n, 0), jnp.zeros_like(x))


def _conv(x, w_ref, b_ref):
    return w_ref[2:3, :] * x + w_ref[1:2, :] * _shift_down(x, 1) + w_ref[0:1, :] * _shift_down(x, 2) + b_ref[...]


def conv_act_fwd(upre, conv_w, conv_b):
    s, f2 = upre.shape
    f = f2 // 2
    tc = _tile(f, COL_TILE)
    nc = f // tc

    def body(ug_ref, uv_ref, wg_ref, wv_ref, bg_ref, bv_ref, o_ref):
        gh = _conv(ug_ref[...], wg_ref, bg_ref)
        vh = _conv(uv_ref[...], wv_ref, bv_ref)
        o_ref[...] = (gh * _sigmoid(gh) * vh).astype(o_ref.dtype)

    def spec(rows, shift):
        return pl.BlockSpec((rows, tc), lambda j: (0, j + shift))

    return pl.pallas_call(
        body, name="conv_act_fwd", grid=(nc,),
        in_specs=[spec(s, 0), spec(s, nc), spec(3, 0), spec(3, nc), spec(1, 0), spec(1, nc)], out_specs=spec(s, 0),
        out_shape=jax.ShapeDtypeStruct((s, f), BF16), compiler_params=_params("parallel"))(upre, upre, conv_w, conv_w, conv_b, conv_b)


def conv_act_bwd(upre, conv_w, conv_b, df):
    s, f2 = upre.shape
    f = f2 // 2
    tc = _tile(f, COL_TILE)
    nc = f // tc

    def half(x, d, w_ref, du_ref, gw_ref, gb_ref):
        gb_ref[...] = _colsum(d)
        gw_ref[2:3, :] = _colsum(d * x)
        gw_ref[1:2, :] = _colsum(d * _shift_down(x, 1))
        gw_ref[0:1, :] = _colsum(d * _shift_down(x, 2))
        du_ref[...] = (w_ref[2:3, :] * d + w_ref[1:2, :] * _shift_up(d, 1) + w_ref[0:1, :] * _shift_up(d, 2)).astype(du_ref.dtype)

    def body(ug_ref, uv_ref, wg_ref, wv_ref, bg_ref, bv_ref, df_ref, dug_ref, duv_ref, gwg_ref, gwv_ref, gbg_ref, gbv_ref):
        xg, xv = ug_ref[...], uv_ref[...]
        gh = _conv(xg, wg_ref, bg_ref)
        vh = _conv(xv, wv_ref, bv_ref)
        sg = _sigmoid(gh)
        df_v = df_ref[...]
        half(xg, df_v * vh * (sg * (1.0 + gh * (1.0 - sg))), wg_ref, dug_ref, gwg_ref, gbg_ref)
        half(xv, df_v * (gh * sg), wv_ref, duv_ref, gwv_ref, gbv_ref)

    def spec(rows, shift):
        return pl.BlockSpec((rows, tc), lambda j: (0, j + shift))

    act = jax.ShapeDtypeStruct((s, f), BF16)
    gw = jax.ShapeDtypeStruct((3, f), F32)
    gb = jax.ShapeDtypeStruct((1, f), F32)
    return pl.pallas_call(
        body, name="conv_act_bwd", grid=(nc,),
        in_specs=[spec(s, 0), spec(s, nc), spec(3, 0), spec(3, nc), spec(1, 0), spec(1, nc), spec(s, 0)],
        out_specs=[spec(s, 0), spec(s, 0), spec(3, 0), spec(3, 0), spec(1, 0), spec(1, 0)],
        out_shape=[act, act, gw, gw, gb, gb],
        compiler_params=_params("parallel"))(upre, upre, conv_w, conv_w, conv_b, conv_b, df)


def adamw(name, w, m, v, parts, row_off=0):
    npart, c = parts.shape[0], parts.shape[2]
    r = w.shape[0]
    tr = r
    if r % 8 == 0:
        tr = max(8, min(r, ADAMW_TILE_ELEMS // c) // 8 * 8)
        while r % tr:
            tr -= 8
    bc1 = 1.0 - ADAM_B1 ** ADAM_STEP
    bc2 = 1.0 - ADAM_B2 ** ADAM_STEP

    def body(w_ref, m_ref, v_ref, p_ref, g_ref, d_ref, nm_ref, nv_ref):
        g = p_ref[0].astype(F32)
        for k in range(1, npart):
            g = g + p_ref[k].astype(F32)
        m_new = ADAM_B1 * m_ref[...] + (1.0 - ADAM_B1) * g
        v_new = ADAM_B2 * v_ref[...] + (1.0 - ADAM_B2) * (g * g)
        g_ref[...] = g
        nm_ref[...] = m_new
        nv_ref[...] = v_new
        d_ref[...] = -ADAM_LR * ((m_new / bc1) / (jnp.sqrt(v_new / bc2) + ADAM_EPS) + ADAM_WD * w_ref[...])

    assert row_off % tr == 0
    deps = _TOKENS.take()
    blk = pl.BlockSpec((tr, c), lambda i: (i, 0))
    out = jax.ShapeDtypeStruct((r, c), F32)
    return pl.pallas_call(
        lambda *refs: body(*refs[:4], *refs[4 + len(deps):]), name=name, grid=(r // tr,),
        in_specs=[blk, blk, blk, pl.BlockSpec((npart, tr, c), lambda i: (0, row_off // tr + i, 0))] + [pl.BlockSpec(memory_space=pl.ANY)] * len(deps),
        out_specs=[blk, blk, blk, blk], out_shape=[out, out, out, out], compiler_params=_params("parallel"))(w, m, v, parts, *deps)


def _position():
    return lax.axis_index("x"), lax.axis_index("y"), lax.axis_index("c")


def _index(p):
    return 4 * p[0] + 2 * p[1] + p[2]


def _peer(me, r):
    return (me[0] ^ ((r >> 2) & 1), me[1] ^ ((r >> 1) & 1), me[2] ^ (r & 1))


_ANY = pl.BlockSpec(memory_space=pl.ANY)


def all_gather_two_level(shards):
    n = len(shards)

    def body(*refs):
        ins, outs = refs[:n], refs[n:2 * n]
        send_sems, recv_sems, local_sems = refs[2 * n:]
        x, y, c = _position()
        me, sibling = (x, y, c), (x, y, 1 - c)
        chips = [(1 - x, y), (x, 1 - y), (1 - x, 1 - y)]

        def copy(w, k, block, to, src=None):
            slot = outs[w].at[_index(block)]
            return pltpu.make_async_remote_copy(src_ref=slot if src is None else src, dst_ref=slot,
                                                send_sem=send_sems.at[7 * w + k], recv_sem=recv_sems.at[7 * w + k],
                                                device_id=to, device_id_type=MESH)

        mine = [pltpu.make_async_copy(ins[w], outs[w].at[_index(me)], local_sems.at[w]) for w in range(n)]
        for cp in mine:
            cp.start()
        first = []
        for w in range(n):
            first.append(copy(w, 0, me, sibling, src=ins[w]))
            first += [copy(w, 1 + j, me, (*chip, c), src=ins[w]) for j, chip in enumerate(chips)]
        for cp in first:
            cp.start()
        passed = []
        for w in range(n):
            for j, chip in enumerate(chips):
                copy(w, 1 + j, (*chip, c), me).wait_recv()
                cp = copy(w, 4 + j, (*chip, c), sibling)
                cp.start()
                passed.append(cp)
        for w in range(n):
            copy(w, 0, sibling, me).wait_recv()
            for j, chip in enumerate(chips):
                copy(w, 4 + j, (*chip, 1 - c), me).wait_recv()
        for cp in first + passed:
            cp.wait_send()
        for cp in mine:
            cp.wait()

    return pl.pallas_call(
        body, name="all_gather_weights",
        out_shape=[jax.ShapeDtypeStruct((N_DEV,) + a.shape, a.dtype) for a in shards],
        in_specs=[_ANY] * n, out_specs=[_ANY] * n,
        scratch_shapes=[pltpu.SemaphoreType.DMA((7 * n,)), pltpu.SemaphoreType.DMA((7 * n,)), pltpu.SemaphoreType.DMA((n,))],
        )(*shards)


def exchange(name, arrays, scatter):
    n = len(arrays)

    def body(*refs):
        ins, outs = refs[:n], refs[n:2 * n]
        send_sems, recv_sems, local_sems = refs[2 * n:]
        me = _position()
        copies = []
        for w in range(n):
            src = ins[w].at[_index(me)] if scatter else ins[w]
            cp = pltpu.make_async_copy(src, outs[w].at[_index(me)], local_sems.at[w])
            cp.start()
            copies.append(cp)
        remote = []
        for w in range(n):
            for r in range(1, N_DEV):
                peer = _peer(me, r)
                src = ins[w].at[_index(peer)] if scatter else ins[w]
                cp = pltpu.make_async_remote_copy(src_ref=src, dst_ref=outs[w].at[_index(me)],
                                                  send_sem=send_sems.at[7 * w + r - 1], recv_sem=recv_sems.at[7 * w + r - 1],
                                                  device_id=peer, device_id_type=MESH)
                cp.start()
                remote.append(cp)
        for cp in remote:
            cp.wait()
        for cp in copies:
            cp.wait()

    blocks = [a.shape[1:] if scatter else a.shape for a in arrays]
    return pl.pallas_call(
        body, name=name,
        out_shape=[jax.ShapeDtypeStruct((N_DEV,) + b, a.dtype) for a, b in zip(arrays, blocks)],
        in_specs=[_ANY] * n, out_specs=[_ANY] * n,
        scratch_shapes=[pltpu.SemaphoreType.DMA((7 * n,)), pltpu.SemaphoreType.DMA((7 * n,)), pltpu.SemaphoreType.DMA((n,))],
        )(*arrays)


_HBM = pl.BlockSpec(memory_space=pltpu.HBM)
_SEM = pl.BlockSpec(memory_space=pltpu.SEMAPHORE)
_EFFECT = pltpu.SideEffectType.DATAFLOW_SIDE_EFFECTING


def _direct_copies(ins, lands, send_sems, recv_sems, scatter):
    me = _position()
    copies = []
    for w in range(len(ins)):
        for r in range(1, N_DEV):
            peer = _peer(me, r)
            src = ins[w].at[_index(peer)] if scatter else ins[w]
            copies.append(pltpu.make_async_remote_copy(src_ref=src, dst_ref=lands[w].at[_index(me)], send_sem=send_sems.at[7 * w + r - 1],
                                                       recv_sem=recv_sems.at[7 * w + r - 1], device_id=peer, device_id_type=MESH))
    return copies


def exchange_start(name, groups, scatter):
    arrays = [a for g in groups for a in g]
    n = len(arrays)
    blocks = [a.shape[1:] if scatter else a.shape for a in arrays]
    lands = [lax.empty((N_DEV,) + b, a.dtype) for a, b in zip(arrays, blocks)]
    ng = len(groups)

    def body(*refs):
        ins, lnd = refs[:n], refs[n:2 * n]
        sems = refs[2 * n:2 * n + 2 * ng]
        token = refs[2 * n + 2 * ng + 2 * n]
        local_sem = refs[2 * n + 2 * ng + 2 * n + 1]
        me = _position()
        local = []
        for w in range(n):
            src = ins[w].at[_index(me)] if scatter else ins[w]
            cp = pltpu.make_async_copy(src, lnd[w].at[_index(me)], local_sem.at[w])
            cp.start()
            local.append(cp)
        w0 = 0
        for gi, g in enumerate(groups):
            for cp in _direct_copies(ins[w0:w0 + len(g)], lnd[w0:w0 + len(g)], sems[2 * gi], sems[2 * gi + 1], scatter):
                cp.start()
            w0 += len(g)
        for cp in local:
            cp.wait()
        token[...] = jnp.zeros_like(token)

    sem_shapes = []
    for g in groups:
        sem_shapes += [pltpu.SemaphoreType.DMA((7 * len(g),)), pltpu.SemaphoreType.DMA((7 * len(g),))]
    out = pl.pallas_call(
        body, name=name,
        out_shape=tuple(sem_shapes) + tuple(pltpu.HBM(a.shape, a.dtype) for a in arrays) + tuple(pltpu.HBM(l.shape, l.dtype) for l in lands)
        + (jax.ShapeDtypeStruct((8, LANES), F32),),
        in_specs=[_HBM] * (2 * n), out_specs=tuple([_SEM] * (2 * ng) + [_HBM] * (2 * n) + [pl.BlockSpec(memory_space=pltpu.VMEM)]),
        input_output_aliases={i: 2 * ng + i for i in range(2 * n)},
        scratch_shapes=[pltpu.SemaphoreType.DMA((n,))],
        compiler_params=pltpu.CompilerParams(has_side_effects=_EFFECT),
    )(*[pltpu.with_memory_space_constraint(a, pltpu.HBM) for a in arrays], *[pltpu.with_memory_space_constraint(l, pltpu.HBM) for l in lands])
    sems, thru, token = out[:2 * ng], out[2 * ng:2 * ng + 2 * n], out[-1]
    res, w0 = [], 0
    for gi, g in enumerate(groups):
        res.append((sems[2 * gi], sems[2 * gi + 1], list(thru[w0:w0 + len(g)]), list(thru[n + w0:n + w0 + len(g)])))
        w0 += len(g)
    return res, token


def exchange_wait(name, group, after, scatter):
    send_sems, recv_sems, srcs, lands = group
    n = len(srcs)

    def body(*refs):
        ins, lnd = refs[:n], refs[n:2 * n]
        for cp in _direct_copies(ins, lnd, refs[2 * n], refs[2 * n + 1], scatter):
            cp.wait_send()
            cp.wait_recv()

    out = pl.pallas_call(
        body, name=name, out_shape=tuple(pltpu.HBM(a.shape, a.dtype) for a in srcs + lands),
        in_specs=[_HBM] * (2 * n) + [_SEM, _SEM, pl.BlockSpec(memory_space=pl.ANY)], out_specs=tuple([_HBM] * (2 * n)),
        input_output_aliases={i: i for i in range(2 * n)},
        compiler_params=pltpu.CompilerParams(has_side_effects=_EFFECT),
    )(*srcs, *lands, send_sems, recv_sems, after)
    return list(out[n:])


def _after(x, token):
    return lax.optimization_barrier((x, token))[0]


_TOKEN = jax.ShapeDtypeStruct((8, LANES), F32)
_VM = pl.BlockSpec(memory_space=pltpu.VMEM)
_SIDE = pltpu.CompilerParams(has_side_effects=_EFFECT)


def _hbm(a):
    return pltpu.with_memory_space_constraint(a, pltpu.HBM)


def _like(a):
    return pltpu.HBM(a.shape, a.dtype)


def _dma_sems(n):
    return pltpu.SemaphoreType.DMA((n,))


def _other_chips(x, y):
    return [(1 - x, y), (x, 1 - y), (1 - x, 1 - y)]


def _rcopy(src, dst, send_sem, recv_sem, to):
    return pltpu.make_async_remote_copy(src_ref=src, dst_ref=dst, send_sem=send_sem, recv_sem=recv_sem, device_id=to, device_id_type=MESH)


def ag_start(name, shard, after):
    land = lax.empty((N_DEV,) + shard.shape, shard.dtype)

    def body(sh_ref, land_ref, after_ref, send_sems, recv_sems, sh_thru, land_thru, token, local_sem):
        x, y, c = _position()
        slot = land_ref.at[_index((x, y, c))]
        mine = pltpu.make_async_copy(sh_ref, slot, local_sem.at[0])
        mine.start()
        for k, to in enumerate([(x, y, 1 - c)] + [(*chip, c) for chip in _other_chips(x, y)]):
            _rcopy(sh_ref, slot, send_sems.at[k], recv_sems.at[k], to).start()
        mine.wait()
        token[...] = jnp.zeros_like(token)

    send, recv, shard, land, token = pl.pallas_call(
        body, name=name, out_shape=(_dma_sems(4), _dma_sems(4), _like(shard), _like(land), _TOKEN),
        in_specs=[_HBM, _HBM, _ANY], out_specs=(_SEM, _SEM, _HBM, _HBM, _VM), input_output_aliases={0: 2, 1: 3},
        scratch_shapes=[_dma_sems(1)], compiler_params=_SIDE)(_hbm(shard), _hbm(land), after)
    _TOKENS.push(token)
    return send, recv, shard, land


def ag_forward(name, started, after):
    send, recv, shard, land = started

    def body(sh_ref, land_ref, send_sems, recv_sems, after_ref, fsend, frecv, sh_thru, land_thru, token):
        x, y, c = _position()
        for j, chip in enumerate(_other_chips(x, y)):
            slot = land_ref.at[_index((*chip, c))]
            _rcopy(sh_ref, slot, send_sems.at[1 + j], recv_sems.at[1 + j], (*chip, c)).wait_recv()
            _rcopy(slot, slot, fsend.at[j], frecv.at[j], (x, y, 1 - c)).start()
        token[...] = jnp.zeros_like(token)

    fsend, frecv, shard, land, token = pl.pallas_call(
        body, name=name, out_shape=(_dma_sems(3), _dma_sems(3), _like(shard), _like(land), _TOKEN),
        in_specs=[_HBM, _HBM, _SEM, _SEM, _ANY], out_specs=(_SEM, _SEM, _HBM, _HBM, _VM), input_output_aliases={0: 2, 1: 3},
        compiler_params=_SIDE)(shard, land, send, recv, after)
    _TOKENS.push(token)
    return send, recv, fsend, frecv, shard, land


def ag_wait(name, forwarded, after):
    send, recv, fsend, frecv, shard, land = forwarded

    def body(sh_ref, land_ref, send_sems, recv_sems, fsend_r, frecv_r, after_ref, sh_out, land_out):
        x, y, c = _position()
        sibling = (x, y, 1 - c)
        _rcopy(sh_ref, land_ref.at[_index(sibling)], send_sems.at[0], recv_sems.at[0], sibling).wait_recv()
        for j, chip in enumerate(_other_chips(x, y)):
            _rcopy(sh_ref, land_ref.at[_index((*chip, 1 - c))], fsend_r.at[j], frecv_r.at[j], sibling).wait_recv()
        own = land_ref.at[_index((x, y, c))]
        for k in range(4):
            _rcopy(sh_ref, own, send_sems.at[k], recv_sems.at[k], sibling).wait_send()
        for j in range(3):
            _rcopy(sh_ref, own, fsend_r.at[j], frecv_r.at[j], sibling).wait_send()

    return pl.pallas_call(
        body, name=name, out_shape=(_like(shard), _like(land)), in_specs=[_HBM, _HBM, _SEM, _SEM, _SEM, _SEM, _ANY],
        out_specs=(_HBM, _HBM), input_output_aliases={0: 0, 1: 1}, compiler_params=_SIDE)(shard, land, send, recv, fsend, frecv, after)[1]


def rs_d2d_start(name, grads):
    n = len(grads)
    lands = [lax.empty((4,) + g.shape[1:], g.dtype) for g in grads]

    def body(*refs):
        ins, lnd, send_sems, recv_sems, token = refs[:n], refs[n:2 * n], refs[2 * n], refs[2 * n + 1], refs[4 * n + 2]
        x, y, c = _position()
        for w in range(n):
            for i in range(4):
                _rcopy(ins[w].at[2 * i + 1 - c], lnd[w].at[i], send_sems.at[4 * w + i], recv_sems.at[4 * w + i], (x, y, 1 - c)).start()
        token[...] = jnp.zeros_like(token)

    out = pl.pallas_call(
        body, name=name, out_shape=(_dma_sems(4 * n), _dma_sems(4 * n)) + tuple(_like(a) for a in grads + lands) + (_TOKEN,),
        in_specs=[_HBM] * (2 * n), out_specs=(_SEM, _SEM) + (_HBM,) * (2 * n) + (_VM,),
        input_output_aliases={i: 2 + i for i in range(2 * n)}, compiler_params=_SIDE)(*[_hbm(a) for a in grads + lands])
    _TOKENS.push(out[-1])
    return out[0], out[1], list(out[2:2 + n]), list(out[2 + n:2 + 2 * n])


def rs_d2d_wait(name, started, after):
    send, recv, grads, lands = started
    n = len(grads)

    def body(*refs):
        ins, lnd, send_sems, recv_sems = refs[:n], refs[n:2 * n], refs[2 * n], refs[2 * n + 1]
        x, y, c = _position()
        for w in range(n):
            for i in range(4):
                cp = _rcopy(ins[w].at[2 * i + 1 - c], lnd[w].at[i], send_sems.at[4 * w + i], recv_sems.at[4 * w + i], (x, y, 1 - c))
                cp.wait_send()
                cp.wait_recv()

    out = pl.pallas_call(
        body, name=name, out_shape=tuple(_like(a) for a in grads + lands), in_specs=[_HBM] * (2 * n) + [_SEM, _SEM, _ANY],
        out_specs=(_HBM,) * (2 * n), input_output_aliases={i: i for i in range(2 * n)}, compiler_params=_SIDE)(*grads, *lands, send, recv, after)
    return list(out[:n]), list(out[n:])


def pair_sum(name, grad, land, core):
    _, r, c = grad.shape
    tr = r
    if r % 8 == 0:
        tr = max(8, min(r, ADAMW_TILE_ELEMS // c) // 8 * 8)
        while r % tr:
            tr -= 8

    def body(core_ref, a_ref, b_ref, o_ref):
        o_ref[...] = (a_ref[...].astype(F32) + b_ref[...].astype(F32)).astype(o_ref.dtype)

    return pl.pallas_call(
        body, name=name, out_shape=jax.ShapeDtypeStruct((4, r, c), grad.dtype),
        grid_spec=pltpu.PrefetchScalarGridSpec(
            num_scalar_prefetch=1, grid=(4, r // tr),
            in_specs=[pl.BlockSpec((None, None, tr, c), lambda i, j, core_ref: (i, core_ref[0], j, 0)),
                      pl.BlockSpec((None, tr, c), lambda i, j, core_ref: (i, j, 0))],
            out_specs=pl.BlockSpec((None, tr, c), lambda i, j, core_ref: (i, j, 0))),
        compiler_params=_params("parallel", "parallel"))(core, grad.reshape(4, 2, r, c), land)


def rs_ici_start(name, sums):
    n = len(sums)
    lands = [lax.empty(a.shape, a.dtype) for a in sums]

    def body(*refs):
        ins, lnd, send_sems, recv_sems, token, local_sem = refs[:n], refs[n:2 * n], refs[2 * n], refs[2 * n + 1], refs[4 * n + 2], refs[4 * n + 3]
        x, y, c = _position()
        chip = 2 * x + y
        local = [pltpu.make_async_copy(ins[w].at[chip], lnd[w].at[chip], local_sem.at[w]) for w in range(n)]
        for cp in local:
            cp.start()
        for w in range(n):
            for j, other in enumerate(_other_chips(x, y)):
                _rcopy(ins[w].at[2 * other[0] + other[1]], lnd[w].at[chip], send_sems.at[3 * w + j], recv_sems.at[3 * w + j], (*other, c)).start()
        for cp in local:
            cp.wait()
        token[...] = jnp.zeros_like(token)

    out = pl.pallas_call(
        body, name=name, out_shape=(_dma_sems(3 * n), _dma_sems(3 * n)) + tuple(_like(a) for a in sums + lands) + (_TOKEN,),
        in_specs=[_HBM] * (2 * n), out_specs=(_SEM, _SEM) + (_HBM,) * (2 * n) + (_VM,),
        input_output_aliases={i: 2 + i for i in range(2 * n)}, scratch_shapes=[_dma_sems(n)],
        compiler_params=_SIDE)(*[_hbm(a) for a in sums + lands])
    _TOKENS.push(out[-1])
    return out[0], out[1], list(out[2:2 + n]), list(out[2 + n:2 + 2 * n])


def rs_ici_wait(name, started, after):
    send, recv, sums, lands = started
    n = len(sums)

    def body(*refs):
        ins, lnd, send_sems, recv_sems = refs[:n], refs[n:2 * n], refs[2 * n], refs[2 * n + 1]
        x, y, c = _position()
        chip = 2 * x + y
        for w in range(n):
            for j, other in enumerate(_other_chips(x, y)):
                cp = _rcopy(ins[w].at[2 * other[0] + other[1]], lnd[w].at[2 * other[0] + other[1]], send_sems.at[3 * w + j], recv_sems.at[3 * w + j], (*other, c))
                cp.wait_send()
                cp.wait_recv()

    out = pl.pallas_call(
        body, name=name, out_shape=tuple(_like(a) for a in sums + lands), in_specs=[_HBM] * (2 * n) + [_SEM, _SEM, _ANY],
        out_specs=(_HBM,) * (2 * n), input_output_aliases={i: i for i in range(2 * n)}, compiler_params=_SIDE)(*sums, *lands, send, recv, after)
    return list(out[n:])


def ada_fwd(c, w_ada, b_ada3, conv_w):
    d, cs = w_ada.shape

    def body(c_ref, w_ref, b_ref, cw_ref, mod_ref, sc_ref, cwa_ref, part_ref, send_sems, recv_sems):
        me = _position()
        my = _index(me)
        cv = c_ref[...]
        sc_ref[my] = cv * _sigmoid(cv)
        cwa_ref[my] = cw_ref[...]
        gather = []
        for r in range(1, N_DEV):
            for k, ref in enumerate((sc_ref, cwa_ref)):
                cp = pltpu.make_async_remote_copy(src_ref=ref.at[my], dst_ref=ref.at[my], send_sem=send_sems.at[14 * k + r - 1],
                                                  recv_sem=recv_sems.at[14 * k + r - 1], device_id=_peer(me, r), device_id_type=MESH)
                cp.start()
                gather.append(cp)
        for cp in gather:
            cp.wait()
        sc_all = jnp.concatenate([sc_ref[k] for k in range(N_DEV)], axis=0).astype(BF16)
        part = jnp.dot(sc_all, w_ref[...].astype(BF16), preferred_element_type=F32)
        for k in range(N_DEV):
            part_ref[k] = part[k:k + 1, :]
        scatter = []
        for r in range(1, N_DEV):
            peer = _peer(me, r)
            cp = pltpu.make_async_remote_copy(src_ref=part_ref.at[_index(peer)], dst_ref=mod_ref.at[my], send_sem=send_sems.at[6 + r],
                                              recv_sem=recv_sems.at[6 + r], device_id=peer, device_id_type=MESH)
            cp.start()
            scatter.append(cp)
        mod_ref[my] = part_ref[my]
        for cp in scatter:
            cp.wait()
        mod_ref[...] = mod_ref[...] + b_ref[...]

    vm = pl.BlockSpec(memory_space=pltpu.VMEM)
    return pl.pallas_call(
        body, name="ada_fwd",
        out_shape=[jax.ShapeDtypeStruct((N_DEV, 1, cs), F32), jax.ShapeDtypeStruct((N_DEV, 1, d), F32),
                   jax.ShapeDtypeStruct((N_DEV,) + conv_w.shape, F32)],
        in_specs=[vm, vm, vm, vm], out_specs=[vm, vm, vm],
        scratch_shapes=[pltpu.VMEM((N_DEV, 1, cs), F32), pltpu.SemaphoreType.DMA((21,)), pltpu.SemaphoreType.DMA((21,))],
        compiler_params=pltpu.CompilerParams(vmem_limit_bytes=VMEM_LIMIT_BYTES))(c, w_ada, b_ada3, conv_w)


def ada_bwd_w(sc_all, dmod_cols):
    _, d = sc_all.shape
    cs = dmod_cols.shape[1]
    tr = _tile(d, ROW_TILE)

    def body(sc_ref, dm_ref, o_ref):
        dm = dm_ref[...].astype(BF16)
        o_ref[...] = lax.dot_general(sc_ref[...].astype(BF16), dm, (((0,), (0,)), ((), ())), preferred_element_type=F32)

    return pl.pallas_call(body, name="ada_bwd_w", grid=(d // tr,),
                          in_specs=[pl.BlockSpec((N_DEV, tr), lambda i: (0, i)), _full((N_DEV, cs))],
                          out_specs=pl.BlockSpec((None, tr, cs), lambda i: (0, i, 0)),
                          out_shape=jax.ShapeDtypeStruct((1, d, cs), F32), compiler_params=_params("parallel"))(sc_all, dmod_cols)


def _round_up(n, m):
    return (n + m - 1) // m * m


def kernel(x, c, positions, w_ada, b_ada, pre_norm1_g, w_in, gm_ln_g, gm_ln_b, gm_w_s, gm_b_s, w_branch_a, q_norm_g, w_uq, kv_norm_g, w_ukv, w_branch_b, w_out, post_norm1_g, pre_norm2_g, w_up, conv_w, conv_b, w_down, post_norm2_g, loss_target, m_w_ada, m_b_ada, m_pre_norm1_g, m_w_in, m_gm_ln_g, m_gm_ln_b, m_gm_w_s, m_gm_b_s, m_w_branch_a, m_q_norm_g, m_w_uq, m_kv_norm_g, m_w_ukv, m_w_branch_b, m_w_out, m_post_norm1_g, m_pre_norm2_g, m_w_up, m_conv_w, m_conv_b, m_w_down, m_post_norm2_g, v_w_ada, v_b_ada, v_pre_norm1_g, v_w_in, v_gm_ln_g, v_gm_ln_b, v_gm_w_s, v_gm_b_s, v_w_branch_a, v_q_norm_g, v_w_uq, v_kv_norm_g, v_w_ukv, v_w_branch_b, v_w_out, v_post_norm1_g, v_pre_norm2_g, v_w_up, v_conv_w, v_conv_b, v_w_down, v_post_norm2_g):
    weights = dict(w_ada=w_ada, b_ada=b_ada, pre_norm1_g=pre_norm1_g, w_in=w_in, gm_ln_g=gm_ln_g, gm_ln_b=gm_ln_b, gm_w_s=gm_w_s,
                   gm_b_s=gm_b_s, w_branch_a=w_branch_a, q_norm_g=q_norm_g, w_uq=w_uq, kv_norm_g=kv_norm_g, w_ukv=w_ukv,
                   w_branch_b=w_branch_b, w_out=w_out, post_norm1_g=post_norm1_g, pre_norm2_g=pre_norm2_g, w_up=w_up, conv_w=conv_w,
                   conv_b=conv_b, w_down=w_down, post_norm2_g=post_norm2_g)
    mom1 = dict(w_ada=m_w_ada, b_ada=m_b_ada, pre_norm1_g=m_pre_norm1_g, w_in=m_w_in, gm_ln_g=m_gm_ln_g, gm_ln_b=m_gm_ln_b,
                gm_w_s=m_gm_w_s, gm_b_s=m_gm_b_s, w_branch_a=m_w_branch_a, q_norm_g=m_q_norm_g, w_uq=m_w_uq, kv_norm_g=m_kv_norm_g,
                w_ukv=m_w_ukv, w_branch_b=m_w_branch_b, w_out=m_w_out, post_norm1_g=m_post_norm1_g, pre_norm2_g=m_pre_norm2_g,
                w_up=m_w_up, conv_w=m_conv_w, conv_b=m_conv_b, w_down=m_w_down, post_norm2_g=m_post_norm2_g)
    mom2 = dict(w_ada=v_w_ada, b_ada=v_b_ada, pre_norm1_g=v_pre_norm1_g, w_in=v_w_in, gm_ln_g=v_gm_ln_g, gm_ln_b=v_gm_ln_b,
                gm_w_s=v_gm_w_s, gm_b_s=v_gm_b_s, w_branch_a=v_w_branch_a, q_norm_g=v_q_norm_g, w_uq=v_w_uq, kv_norm_g=v_kv_norm_g,
                w_ukv=v_w_ukv, w_branch_b=v_w_branch_b, w_out=v_w_out, post_norm1_g=v_post_norm1_g, pre_norm2_g=v_pre_norm2_g,
                w_up=v_w_up, conv_w=v_conv_w, conv_b=v_conv_b, w_down=v_w_down, post_norm2_g=v_post_norm2_g)
    order = list(weights)
    _TOKENS.take()

    s, d = x.shape[1], x.shape[2]
    gmw = gm_ln_g.shape[0]
    groups = gmw // CHUNK
    ql, kvl = q_norm_g.shape[0], kv_norm_g.shape[0]
    f2 = conv_b.shape[0]
    in_cols = w_in.shape[1] * N_DEV
    o_q, o_kv, o_ga, o_gb, o_kpe = 2 * gmw, 2 * gmw + ql, 2 * gmw + ql + kvl, 2 * gmw + ql + kvl + d, 2 * gmw + ql + kvl + 2 * d
    zp = _round_up(o_kpe + LANES, Z_PAD)
    src_kpe = 2 * gmw + ql + kvl
    assert src_kpe + QK_ROPE + 2 * d == in_cols
    my = 4 * lax.axis_index("x") + 2 * lax.axis_index("y") + lax.axis_index("c")

    x2, tgt = x[0], loss_target[0]
    row = lambda a: a.reshape(1, -1)

    big = ["w_in", "w_branch_a", "w_uq", "w_ukv", "w_branch_b", "w_out", "w_up", "w_down"]
    sh = {k: weights[k].astype(BF16) for k in big}
    mix = ["w_branch_a", "w_uq", "w_ukv", "w_branch_b", "w_out"]
    mix_sizes = [sh[k].size for k in mix]
    mix_packed = jnp.concatenate([sh[k].reshape(-1) for k in mix]).reshape(-1, LANES)
    ag_in = ag_start("ag_start_in", sh["w_in"], c)

    mod8, sc_all3, g_cw = ada_fwd(c, w_ada, b_ada.reshape(N_DEV, 1, -1), conv_w)
    mod = mod8.reshape(N_MOD, d)
    shift1, scale1, gate1, shift2, scale2, gate2 = (mod[i:i + 1] for i in range(N_MOD))
    sc_all = sc_all3.reshape(N_DEV, d)
    h1 = norm_mod_fwd("pre1_fwd", x2, row(pre_norm1_g), scale1, shift1)

    g_in = ag_wait("ag_wait_in", ag_forward("ag_forward_in", ag_in, h1), h1)
    ag_mix = ag_start("ag_start_mix", mix_packed, g_in)
    w_in_f = g_in.transpose(1, 0, 2).reshape(d, in_cols)
    w_in_p = jnp.concatenate([w_in_f[:, :src_kpe], w_in_f[:, src_kpe + QK_ROPE:], w_in_f[:, src_kpe:src_kpe + QK_ROPE],
                              jnp.zeros((d, zp - in_cols), BF16)], axis=1)

    inv = ROPE_THETA ** (-jnp.arange(0, QK_ROPE, 2, dtype=F32) / QK_ROPE)
    ang = positions[0].astype(F32)[:, None] * inv
    cos4 = jnp.tile(jnp.cos(ang), (1, 4))
    sin4 = jnp.tile(jnp.concatenate([-jnp.sin(ang), jnp.sin(ang)], axis=1), (1, 2))

    wm = (gm_w_s * jnp.tril(jnp.ones((CHUNK, CHUNK), F32))).astype(BF16)
    bs3 = gm_b_s.reshape(groups, CHUNK, 1)
    ln_g, ln_b = row(gm_ln_g), row(gm_ln_b)

    z = mm_nn("z_proj", h1, w_in_p, F32)
    a = gmlp_fwd(z, gmw, ln_g, ln_b, wm, bs3)
    g_mix = ag_wait("ag_wait_mix", ag_forward("ag_forward_mix", ag_mix, a), a).reshape(N_DEV, -1)
    ag_up = ag_start("ag_start_up", sh["w_up"], g_mix)
    offs = [sum(mix_sizes[:i]) for i in range(len(mix) + 1)]
    g_a, g_uq, g_ukv, g_b, g_out = (g_mix[:, offs[i]:offs[i + 1]].reshape((N_DEV,) + sh[k].shape) for i, k in enumerate(mix))
    w_a_f, w_b_f, w_out_f = g_a.reshape(-1, d), g_b.reshape(-1, d), g_out.reshape(-1, d)
    w_uq_f = g_uq.transpose(1, 0, 2).reshape(ql, HEADS, QK_NOPE + QK_ROPE)
    w_uq_n = w_uq_f[:, :, :QK_NOPE].reshape(ql, HEADS * QK_NOPE)
    w_uq_r = w_uq_f[:, :, QK_NOPE:].reshape(ql, HEADS * QK_ROPE)
    y_a = mm_nn("branch_a", a, w_a_f, F32)
    qln = rms_fwd_cols("q_norm", z, o_q, ql, row(q_norm_g))
    kvn = rms_fwd_cols("kv_norm", z, o_kv, kvl, row(kv_norm_g))
    qn = mm_nn("q_nope", qln, w_uq_n, BF16)
    qp = mm_nn("q_rope", qln, w_uq_r, F32)
    kv = mm_nn_b3("kv_up", kvn, g_ukv, BF16)
    kpr = rope_k(z, o_kpe, cos4, sin4)
    o, qpr, lse = attn_fwd2(qn, qp, kv, kpr, cos4, sin4)
    y_b = mm_nn("branch_b", o, w_b_f, F32)
    merged = merge_fwd(z, o_ga, o_gb, y_a, y_b)
    y1 = mm_nn("out_proj", merged, w_out_f, F32)
    x1 = post_res_fwd("post1_fwd", x2, y1, gate1, row(post_norm1_g))
    h2 = norm_mod_fwd("pre2_fwd", x1, row(pre_norm2_g), scale2, shift2)
    g_up = ag_wait("ag_wait_up", ag_forward("ag_forward_up", ag_up, h2), h2)
    ag_down = ag_start("ag_start_down", sh["w_down"], g_up)
    upre = mm_nn_b3("up_proj", h2, g_up, F32)
    cw = g_cw.transpose(1, 0, 2).reshape(3, f2)
    cb = row(conv_b)
    f = conv_act_fwd(upre, cw, cb)
    w_down_f = ag_wait("ag_wait_down", ag_forward("ag_forward_down", ag_down, f), f).reshape(-1, d)
    ffn = mm_nn("down_proj", f, w_down_f, F32)
    loss_acc, dout, dffn, acc2 = post2_loss_bwd(x1, ffn, tgt, gate2, row(post_norm2_g))

    blocks = lambda g: g.reshape(N_DEV, g.shape[0] // N_DEV, g.shape[1])
    core = lax.axis_index("c").astype(jnp.int32).reshape(1)
    rs = {}

    def rs_begin(key, grads):
        rs[key] = rs_d2d_start("rs_d2d_start_" + key, grads)

    def rs_middle(key, after):
        grads, lands = rs_d2d_wait("rs_d2d_wait_" + key, rs[key], after)
        sums = [pair_sum("pair_sum_%s_%d" % (key, i), g, l, core) for i, (g, l) in enumerate(zip(grads, lands))]
        rs[key] = rs_ici_start("rs_ici_start_" + key, sums)

    gw_down = mm_tn("g_w_down", f, dffn, BF16)
    rs_begin("down", [blocks(gw_down)])
    df = mm_nt("d_f", dffn, w_down_f, F32)
    rs_middle("down", df)
    dup_g, dup_v, gcw_g, gcw_v, gcb_g, gcb_v = conv_act_bwd(upre, cw, cb, df)
    dupre = jnp.concatenate([dup_g, dup_v], axis=1)
    gw_up3 = mm_tn_o3("g_w_up", h2, dupre, N_DEV, BF16)
    rs_begin("up", [gw_up3])
    dh2 = mm_nt_b3("d_h2", dupre, g_up, F32)
    rs_middle("up", dh2)
    dx1, dy1, acc_mid = mid_bwd(dh2, dout, x1, y1, row(pre_norm2_g), scale2, gate1, row(post_norm1_g))
    gw_out = mm_tn("g_w_out", merged, dy1, BF16)
    dmerged = mm_nt("d_merged", dy1, w_out_f, F32)
    dya, dyb, dga, dgb = merge_bwd(z, o_ga, o_gb, y_a, y_b, dmerged)
    gw_a = mm_tn("g_w_a", a, dya, BF16)
    gw_b = mm_tn("g_w_b", o, dyb, BF16)
    rs_begin("mid", [jnp.concatenate([blocks(gw_out), blocks(gw_a), blocks(gw_b)], axis=1)])
    da = mm_nt("d_a", dya, w_a_f, F32)
    do = mm_nt("d_o", dyb, w_b_f, F32)
    rs_middle("mid", do)
    duv, g_ws, g_bs3, acc_gm = gmlp_bwd(z, gmw, da, ln_g, ln_b, wm, bs3)
    dqn, dqp, dkv, dkp = attn_bwd2(qn, qpr, kv, kpr, o, do, lse, cos4, sin4)
    dkpe = kpe_bwd(dkp, cos4, sin4, zp - o_kpe)
    dq_cat = jnp.concatenate([dqn, dqp], axis=1)
    w_uq_cat = jnp.concatenate([w_uq_n, w_uq_r], axis=1)
    gw_uq_cat = mm_tn("g_w_uq", qln, dq_cat, BF16)
    gw_uq_f = jnp.concatenate([gw_uq_cat[:, :HEADS * QK_NOPE].reshape(ql, HEADS, QK_NOPE),
                               gw_uq_cat[:, HEADS * QK_NOPE:].reshape(ql, HEADS, QK_ROPE)], axis=2)
    gw_uq3 = gw_uq_f.reshape(ql, N_DEV, -1).transpose(1, 0, 2)
    gw_ukv3 = mm_tn_o3("g_w_ukv", kvn, dkv, N_DEV, BF16)
    rs_begin("mla", [gw_uq3, gw_ukv3])
    dqln = mm_nt("d_qln", dq_cat, w_uq_cat, F32)
    dq_lat, g_qnorm = rms_bwd_cols("q_norm_bwd", dqln, z, o_q, ql, row(q_norm_g))
    dkvn = mm_nt_b3("d_kvn", dkv, g_ukv, F32)
    rs_middle("mla", dkvn)
    dkv_lat, g_kvnorm = rms_bwd_cols("kv_norm_bwd", dkvn, z, o_kv, kvl, row(kv_norm_g))
    dz = jnp.concatenate([duv, dq_lat, dkv_lat, dga, dgb, dkpe], axis=1)
    gw_in_p = mm_tn("g_w_in", h1, dz, BF16)
    gw_in_f = jnp.concatenate([gw_in_p[:, :src_kpe], gw_in_p[:, o_kpe:o_kpe + QK_ROPE], gw_in_p[:, src_kpe:o_kpe]], axis=1)
    gw_in3 = gw_in_f.reshape(d, N_DEV, -1).transpose(1, 0, 2)
    rs_begin("in", [gw_in3])
    dh1 = mm_nt("d_h1", dz, w_in_p, F32)
    grad_x, acc1 = pre1_bwd(dh1, dx1, x2, row(pre_norm1_g), scale1)

    dmod = jnp.concatenate([acc1[0], acc1[1], acc_mid[3], acc_mid[0], acc_mid[1], acc2[0]])
    small = [("pre_norm1_g", acc1[2]), ("gm_ln_g", acc_gm[0]), ("gm_ln_b", acc_gm[1]), ("gm_b_s", g_bs3.reshape(-1)),
             ("q_norm_g", g_qnorm[0]), ("kv_norm_g", g_kvnorm[0]), ("post_norm1_g", acc_mid[4]), ("pre_norm2_g", acc_mid[2]),
             ("conv_b", jnp.concatenate([gcb_g[0], gcb_v[0]])), ("post_norm2_g", acc2[1]), ("gm_w_s", g_ws.reshape(-1)),
             ("b_ada", dmod)]
    n_small = sum(v.shape[0] for _, v in small)
    n_cw = 3 * f2
    n_pack = _round_up(n_small + n_cw, PACK_ALIGN)
    tail = jnp.zeros((n_pack - n_small - n_cw,), F32)
    packed = jnp.concatenate([v for _, v in small] + [jnp.concatenate([gcw_g, gcw_v], axis=1).reshape(-1), tail])
    ag_small = ag_start("ag_start_small", packed.reshape(-1, LANES), packed)
    rs_middle("in", packed)

    res = {}
    last = packed
    for key, names in (("down", ["w_down"]), ("up", ["w_up"]), ("mid", ["w_out", "w_branch_a", "w_branch_b"]), ("mla", ["w_uq", "w_ukv"])):
        parts = rs_ici_wait("rs_ici_wait_" + key, rs[key], last)
        for i, k in enumerate(names):
            packed_rows = key == "mid"
            res[k] = adamw("adamw_" + k, weights[k], mom1[k], mom2[k], parts[0 if packed_rows else i],
                           row_off=sum(weights[n].shape[0] for n in names[:i]) if packed_rows else 0)
            last = res[k][0]

    def pack(src):
        return jnp.concatenate([src[k].reshape(-1) for k, _ in small] + [jnp.zeros((n_pack - n_small,), F32)]).reshape(-1, LANES)

    gathered = ag_wait("ag_wait_small", ag_forward("ag_forward_small", ag_small, last), last)
    sm = [t.reshape(-1) for t in adamw("adamw_small", pack(weights), pack(mom1), pack(mom2), gathered)]
    off = 0
    for k, v in small:
        res[k] = tuple(t[off:off + v.shape[0]].reshape(weights[k].shape) for t in sm)
        off += v.shape[0]

    cs_cw = conv_w.shape[1]
    g_cw_full = sm[0][n_small:n_small + n_cw].reshape(3, f2)
    g_cw_mine = lax.dynamic_slice(g_cw_full, (0, my * cs_cw), (3, cs_cw))
    res["conv_w"] = adamw("adamw_conv_w", conv_w, mom1["conv_w"], mom2["conv_w"], g_cw_mine[None])

    cs_ada = w_ada.shape[1]
    off_b = n_small - N_MOD * d
    dmod_all = gathered.reshape(N_DEV, -1)[:, off_b:off_b + N_MOD * d]
    dmod_cols = lax.dynamic_slice(dmod_all, (0, my * cs_ada), (N_DEV, cs_ada))
    res["w_ada"] = adamw("adamw_w_ada", w_ada, mom1["w_ada"], mom2["w_ada"], ada_bwd_w(sc_all, dmod_cols))

    (p_in,) = rs_ici_wait("rs_ici_wait_in", rs["in"], res["w_ada"][0])
    res["w_in"] = adamw("adamw_w_in", w_in, mom1["w_in"], mom2["w_in"], p_in)

    _TOKENS.take()
    loss = lax.psum(loss_acc[0, 0], ("x", "y", "c"))
    outs = [loss, grad_x[None]]
    for i in range(4):
        outs += [res[k][i] for k in order]
    return tuple(outs)
```

```python
import functools

import jax
import jax.numpy as jnp
from jax import lax
from jax.experimental import pallas as pl
from jax.experimental.pallas import tpu as pltpu

F32 = jnp.float32
BF16 = jnp.bfloat16

N_DEV = 8
HEADS = 16
QK_NOPE = 128
QK_ROPE = 64
V_HEAD = 128
CHUNK = 128
ROPE_THETA = 10000.0
EPS = 1e-6
N_MOD = 6
ADAM_LR, ADAM_B1, ADAM_B2, ADAM_EPS, ADAM_WD, ADAM_STEP = 0.001, 0.9, 0.999, 1e-08, 0.01, 10

LANES = 128
VMEM_LIMIT_BYTES = 48 * 2 ** 20
ROW_TILE = 256
COL_TILE = 256
ATT_TILE = 256
Z_PAD = 512
ADAMW_TILE_ELEMS = 1 << 18
PACK_ALIGN = 8 * LANES
MESH = pl.DeviceIdType.MESH


def _params(*sem):
    return pltpu.CompilerParams(dimension_semantics=sem if sem else None, vmem_limit_bytes=VMEM_LIMIT_BYTES)


def _tile(dim, target):
    t = (min(dim, target) // LANES) * LANES
    while t >= LANES:
        if dim % t == 0:
            return t
        t -= LANES
    return dim


def _full(shape):
    nd = len(shape)
    return pl.BlockSpec(shape, lambda *_: (0,) * nd)


class _Tokens:
    def __init__(self):
        self.pending = []

    def push(self, token):
        self.pending.append(token)

    def take(self):
        out, self.pending = self.pending, []
        return out


_TOKENS = _Tokens()


def _matmul(name, a, b, *, grid, a_spec, b_spec, o_spec, out_shape, contract, acc_shape):
    nk = grid[2]
    deps = _TOKENS.take()

    def body(a_ref, b_ref, *rest):
        o_ref, acc_ref = rest[len(deps):]
        k = pl.program_id(2)

        @pl.when(k == 0)
        def _():
            acc_ref[...] = jnp.zeros_like(acc_ref)

        acc_ref[...] += lax.dot_general(a_ref[...].astype(BF16), b_ref[...].astype(BF16),
                                        (contract, ((), ())), preferred_element_type=F32)

        @pl.when(k == nk - 1)
        def _():
            o_ref[...] = acc_ref[...].astype(o_ref.dtype)

    return pl.pallas_call(
        body, name=name, grid=grid, in_specs=[a_spec, b_spec] + [pl.BlockSpec(memory_space=pl.ANY)] * len(deps),
        out_specs=o_spec, out_shape=out_shape, scratch_shapes=[pltpu.VMEM(acc_shape, F32)],
        compiler_params=_params("parallel", "parallel", "arbitrary"))(a, b, *deps)


TM, TN, TK = 1024, 1024, 512


def mm_nn(name, a, b, dtype):
    (m, k), n = a.shape, b.shape[1]
    tm, tn, tk = _tile(m, TM), _tile(n, TN), _tile(k, TK)
    return _matmul(name, a, b, grid=(m // tm, n // tn, k // tk),
                   a_spec=pl.BlockSpec((tm, tk), lambda i, j, kk: (i, kk)),
                   b_spec=pl.BlockSpec((tk, tn), lambda i, j, kk: (kk, j)),
                   o_spec=pl.BlockSpec((tm, tn), lambda i, j, kk: (i, j)),
                   out_shape=jax.ShapeDtypeStruct((m, n), dtype), contract=((1,), (0,)), acc_shape=(tm, tn))


def mm_nn_b3(name, a, b3, dtype):
    (m, k), (nj, _, cs) = a.shape, b3.shape
    tm, tk = _tile(m, TM), _tile(k, TK)
    return _matmul(name, a, b3, grid=(m // tm, nj, k // tk),
                   a_spec=pl.BlockSpec((tm, tk), lambda i, j, kk: (i, kk)),
                   b_spec=pl.BlockSpec((None, tk, cs), lambda i, j, kk: (j, kk, 0)),
                   o_spec=pl.BlockSpec((tm, cs), lambda i, j, kk: (i, j)),
                   out_shape=jax.ShapeDtypeStruct((m, nj * cs), dtype), contract=((1,), (0,)), acc_shape=(tm, cs))


def mm_nt(name, a, b, dtype):
    (m, k), n = a.shape, b.shape[0]
    tm, tn, tk = _tile(m, TM), _tile(n, TN), _tile(k, TK)
    return _matmul(name, a, b, grid=(m // tm, n // tn, k // tk),
                   a_spec=pl.BlockSpec((tm, tk), lambda i, j, kk: (i, kk)),
                   b_spec=pl.BlockSpec((tn, tk), lambda i, j, kk: (j, kk)),
                   o_spec=pl.BlockSpec((tm, tn), lambda i, j, kk: (i, j)),
                   out_shape=jax.ShapeDtypeStruct((m, n), dtype), contract=((1,), (1,)), acc_shape=(tm, tn))


def mm_nt_b3(name, a, b3, dtype):
    m, (nj, n, cs) = a.shape[0], b3.shape
    tm, tn = _tile(m, TM), _tile(n, TN)
    return _matmul(name, a, b3, grid=(m // tm, n // tn, nj),
                   a_spec=pl.BlockSpec((tm, cs), lambda i, j, kk: (i, kk)),
                   b_spec=pl.BlockSpec((None, tn, cs), lambda i, j, kk: (kk, j, 0)),
                   o_spec=pl.BlockSpec((tm, tn), lambda i, j, kk: (i, j)),
                   out_shape=jax.ShapeDtypeStruct((m, n), dtype), contract=((1,), (1,)), acc_shape=(tm, tn))


def mm_tn(name, a, b, dtype):
    (k, m), n = a.shape, b.shape[1]
    tm, tn, tk = _tile(m, TM), _tile(n, TN), _tile(k, TK)
    return _matmul(name, a, b, grid=(m // tm, n // tn, k // tk),
                   a_spec=pl.BlockSpec((tk, tm), lambda i, j, kk: (kk, i)),
                   b_spec=pl.BlockSpec((tk, tn), lambda i, j, kk: (kk, j)),
                   o_spec=pl.BlockSpec((tm, tn), lambda i, j, kk: (i, j)),
                   out_shape=jax.ShapeDtypeStruct((m, n), dtype), contract=((0,), (0,)), acc_shape=(tm, tn))


def mm_tn_o3(name, a, b, nj, dtype):
    (k, m), n = a.shape, b.shape[1]
    cs = n // nj
    tm, tk = _tile(m, TM), _tile(k, TK)
    return _matmul(name, a, b, grid=(m // tm, nj, k // tk),
                   a_spec=pl.BlockSpec((tk, tm), lambda i, j, kk: (kk, i)),
                   b_spec=pl.BlockSpec((tk, cs), lambda i, j, kk: (kk, j)),
                   o_spec=pl.BlockSpec((None, tm, cs), lambda i, j, kk: (j, i, 0)),
                   out_shape=jax.ShapeDtypeStruct((nj, m, cs), dtype), contract=((0,), (0,)), acc_shape=(tm, cs))


_GELU_C = 0.7978845608028654
_GELU_A = 0.044715


def _gelu(x):
    return 0.5 * x * (1.0 + jnp.tanh(_GELU_C * (x + _GELU_A * x * x * x)))


def _gelu_and_grad(x):
    t = jnp.tanh(_GELU_C * (x + _GELU_A * x * x * x))
    y = 0.5 * x * (1.0 + t)
    dy = 0.5 * (1.0 + t) + 0.5 * x * (1.0 - t * t) * (_GELU_C * (1.0 + 3.0 * _GELU_A * x * x))
    return y, dy


def _sigmoid(x):
    return 1.0 / (1.0 + jnp.exp(-x))


def _rms_stats(x):
    inv = lax.rsqrt(jnp.mean(x * x, axis=-1, keepdims=True) + EPS)
    return inv, x * inv


def _rms_bwd(dyhat, yhat, inv):
    return inv * (dyhat - yhat * jnp.mean(dyhat * yhat, axis=-1, keepdims=True))


def _colsum(x):
    return jnp.sum(x, axis=0, keepdims=True)


def _rope(x, cos4, sin4):
    lane = lax.broadcasted_iota(jnp.int32, x.shape, x.ndim - 1)
    first_half = (lane % QK_ROPE) < (QK_ROPE // 2)
    partner = jnp.where(first_half, pltpu.roll(x, LANES - QK_ROPE // 2, x.ndim - 1), pltpu.roll(x, QK_ROPE // 2, x.ndim - 1))
    return x * cos4 + partner * sin4


def norm_mod_fwd(name, x, g, scale, shift):
    s, d = x.shape
    tr = _tile(s, ROW_TILE)

    def body(x_ref, g_ref, sc_ref, sh_ref, o_ref):
        _, xh = _rms_stats(x_ref[...])
        o_ref[...] = (xh * g_ref[...] * (1.0 + sc_ref[...]) + sh_ref[...]).astype(o_ref.dtype)

    row = pl.BlockSpec((tr, d), lambda i: (i, 0))
    vec = pl.BlockSpec((1, d), lambda i: (0, 0))
    return pl.pallas_call(body, name=name, grid=(s // tr,), in_specs=[row, vec, vec, vec], out_specs=row,
                          out_shape=jax.ShapeDtypeStruct((s, d), BF16), compiler_params=_params("parallel"))(x, g, scale, shift)


def rms_fwd_cols(name, z, off, width, g):
    s = z.shape[0]
    tr = _tile(s, ROW_TILE)
    assert off % width == 0

    def body(x_ref, g_ref, o_ref):
        _, xh = _rms_stats(x_ref[...])
        o_ref[...] = (xh * g_ref[...]).astype(o_ref.dtype)

    return pl.pallas_call(body, name=name, grid=(s // tr,),
                          in_specs=[pl.BlockSpec((tr, width), lambda i: (i, off // width)), pl.BlockSpec((1, width), lambda i: (0, 0))],
                          out_specs=pl.BlockSpec((tr, width), lambda i: (i, 0)),
                          out_shape=jax.ShapeDtypeStruct((s, width), BF16), compiler_params=_params("parallel"))(z, g)


def rms_bwd_cols(name, dy, z, off, width, g):
    s = z.shape[0]
    tr = _tile(s, ROW_TILE)

    def body(dy_ref, x_ref, g_ref, dx_ref, gg_ref):
        @pl.when(pl.program_id(0) == 0)
        def _():
            gg_ref[...] = jnp.zeros_like(gg_ref)

        inv, xh = _rms_stats(x_ref[...])
        dy_v = dy_ref[...]
        gg_ref[...] += _colsum(dy_v * xh)
        dx_ref[...] = _rms_bwd(dy_v * g_ref[...], xh, inv).astype(dx_ref.dtype)

    return pl.pallas_call(body, name=name, grid=(s // tr,),
                          in_specs=[pl.BlockSpec((tr, width), lambda i: (i, 0)), pl.BlockSpec((tr, width), lambda i: (i, off // width)),
                                    pl.BlockSpec((1, width), lambda i: (0, 0))],
                          out_specs=[pl.BlockSpec((tr, width), lambda i: (i, 0)), pl.BlockSpec((1, width), lambda i: (0, 0))],
                          out_shape=[jax.ShapeDtypeStruct((s, width), BF16), jax.ShapeDtypeStruct((1, width), F32)],
                          compiler_params=_params("arbitrary"))(dy, z, g)


def post_res_fwd(name, x, y, gate, g):
    s, d = x.shape
    tr = _tile(s, ROW_TILE)

    def body(x_ref, y_ref, gate_ref, g_ref, o_ref):
        _, yh = _rms_stats(y_ref[...])
        o_ref[...] = x_ref[...] + gate_ref[...] * (yh * g_ref[...])

    row = pl.BlockSpec((tr, d), lambda i: (i, 0))
    vec = pl.BlockSpec((1, d), lambda i: (0, 0))
    return pl.pallas_call(body, name=name, grid=(s // tr,), in_specs=[row, row, vec, vec], out_specs=row,
                          out_shape=jax.ShapeDtypeStruct((s, d), F32), compiler_params=_params("parallel"))(x, y, gate, g)


def post2_loss_bwd(x1, ffn, target, gate2, g):
    s, d = x1.shape
    tr = _tile(s, ROW_TILE)

    def body(x_ref, y_ref, t_ref, gate_ref, g_ref, loss_ref, dout_ref, dy_ref, acc_ref):
        @pl.when(pl.program_id(0) == 0)
        def _():
            loss_ref[...] = jnp.zeros_like(loss_ref)
            acc_ref[...] = jnp.zeros_like(acc_ref)

        inv, yh = _rms_stats(y_ref[...])
        r = yh * g_ref[...]
        err = x_ref[...] + gate_ref[...] * r - t_ref[...]
        loss_ref[...] += 0.5 * jnp.sum(jnp.mean(err * err, axis=-1, keepdims=True))
        dout = err / d
        dout_ref[...] = dout
        dr = dout * gate_ref[...]
        acc_ref[0:1, :] += _colsum(dout * r)
        acc_ref[1:2, :] += _colsum(dr * yh)
        dy_ref[...] = _rms_bwd(dr * g_ref[...], yh, inv).astype(dy_ref.dtype)

    row = pl.BlockSpec((tr, d), lambda i: (i, 0))
    vec = pl.BlockSpec((1, d), lambda i: (0, 0))
    return pl.pallas_call(
        body, name="post2_loss_bwd", grid=(s // tr,), in_specs=[row, row, row, vec, vec],
        out_specs=[_full((8, LANES)), row, row, _full((8, d))],
        out_shape=[jax.ShapeDtypeStruct((8, LANES), F32), jax.ShapeDtypeStruct((s, d), F32),
                   jax.ShapeDtypeStruct((s, d), BF16), jax.ShapeDtypeStruct((8, d), F32)],
        compiler_params=_params("arbitrary"))(x1, ffn, target, gate2, g)


def mid_bwd(dh2, dout, x1, y1, pre2_g, scale2, gate1, post1_g):
    s, d = x1.shape
    tr = _tile(s, ROW_TILE)

    def body(dh_ref, dout_ref, x_ref, y_ref, g2_ref, sc_ref, gate_ref, g1_ref, dx_ref, dy_ref, acc_ref):
        @pl.when(pl.program_id(0) == 0)
        def _():
            acc_ref[...] = jnp.zeros_like(acc_ref)

        dh = dh_ref[...]
        inv2, xh = _rms_stats(x_ref[...])
        acc_ref[0:1, :] += _colsum(dh)
        acc_ref[1:2, :] += _colsum(dh * (xh * g2_ref[...]))
        t = dh * (1.0 + sc_ref[...])
        acc_ref[2:3, :] += _colsum(t * xh)
        dx1 = dout_ref[...] + _rms_bwd(t * g2_ref[...], xh, inv2)
        dx_ref[...] = dx1
        inv1, yh = _rms_stats(y_ref[...])
        acc_ref[3:4, :] += _colsum(dx1 * (yh * g1_ref[...]))
        dr = dx1 * gate_ref[...]
        acc_ref[4:5, :] += _colsum(dr * yh)
        dy_ref[...] = _rms_bwd(dr * g1_ref[...], yh, inv1).astype(dy_ref.dtype)

    row = pl.BlockSpec((tr, d), lambda i: (i, 0))
    vec = pl.BlockSpec((1, d), lambda i: (0, 0))
    return pl.pallas_call(
        body, name="mid_bwd", grid=(s // tr,), in_specs=[row, row, row, row, vec, vec, vec, vec],
        out_specs=[row, row, _full((8, d))],
        out_shape=[jax.ShapeDtypeStruct((s, d), F32), jax.ShapeDtypeStruct((s, d), BF16), jax.ShapeDtypeStruct((8, d), F32)],
        compiler_params=_params("arbitrary"))(dh2, dout, x1, y1, pre2_g, scale2, gate1, post1_g)


def pre1_bwd(dh1, dx1, x, pre1_g, scale1):
    s, d = x.shape
    tr = _tile(s, ROW_TILE)

    def body(dh_ref, dx1_ref, x_ref, g_ref, sc_ref, dx_ref, acc_ref):
        @pl.when(pl.program_id(0) == 0)
        def _():
            acc_ref[...] = jnp.zeros_like(acc_ref)

        dh = dh_ref[...]
        inv, xh = _rms_stats(x_ref[...])
        acc_ref[0:1, :] += _colsum(dh)
        acc_ref[1:2, :] += _colsum(dh * (xh * g_ref[...]))
        t = dh * (1.0 + sc_ref[...])
        acc_ref[2:3, :] += _colsum(t * xh)
        dx_ref[...] = dx1_ref[...] + _rms_bwd(t * g_ref[...], xh, inv)

    row = pl.BlockSpec((tr, d), lambda i: (i, 0))
    vec = pl.BlockSpec((1, d), lambda i: (0, 0))
    return pl.pallas_call(
        body, name="pre1_bwd", grid=(s // tr,), in_specs=[row, row, row, vec, vec], out_specs=[row, _full((8, d))],
        out_shape=[jax.ShapeDtypeStruct((s, d), F32), jax.ShapeDtypeStruct((8, d), F32)],
        compiler_params=_params("arbitrary"))(dh1, dx1, x, pre1_g, scale1)


def _ln_stats(v):
    mu = jnp.mean(v, axis=-1, keepdims=True)
    vc = v - mu
    rstd = lax.rsqrt(jnp.mean(vc * vc, axis=-1, keepdims=True) + EPS)
    return rstd, vc * rstd


def gmlp_fwd(z, width, ln_g, ln_b, wm, bs3):
    s = z.shape[0]
    groups = width // CHUNK

    def body(u_ref, v_ref, g_ref, b_ref, wm_ref, bs_ref, a_ref):
        ug = _gelu(u_ref[...])
        _, vh = _ln_stats(_gelu(v_ref[...]))
        vn = (vh * g_ref[...] + b_ref[...]).astype(BF16)
        for g in range(groups):
            cols = slice(g * CHUNK, (g + 1) * CHUNK)
            mixed = jnp.dot(wm_ref[g], vn[:, cols], preferred_element_type=F32) + bs_ref[g]
            a_ref[:, cols] = (ug[:, cols] * mixed).astype(a_ref.dtype)

    vec = pl.BlockSpec((1, width), lambda n: (0, 0))
    return pl.pallas_call(
        body, name="gmlp_fwd", grid=(s // CHUNK,),
        in_specs=[pl.BlockSpec((CHUNK, width), lambda n: (n, 0)), pl.BlockSpec((CHUNK, width), lambda n: (n, 1)), vec, vec,
                  _full(wm.shape), _full(bs3.shape)],
        out_specs=pl.BlockSpec((CHUNK, width), lambda n: (n, 0)),
        out_shape=jax.ShapeDtypeStruct((s, width), BF16), compiler_params=_params("parallel"))(z, z, ln_g, ln_b, wm, bs3)


def gmlp_bwd(z, width, da, ln_g, ln_b, wm, bs3):
    s = z.shape[0]
    groups = width // CHUNK

    def body(u_ref, v_ref, da_ref, g_ref, b_ref, wm_ref, bs_ref, duv_ref, gw_ref, gb_ref, acc_ref, dvn_ref):
        @pl.when(pl.program_id(0) == 0)
        def _():
            gw_ref[...] = jnp.zeros_like(gw_ref)
            gb_ref[...] = jnp.zeros_like(gb_ref)
            acc_ref[...] = jnp.zeros_like(acc_ref)

        ug, dug = _gelu_and_grad(u_ref[...])
        vg, dvg = _gelu_and_grad(v_ref[...])
        rstd, vh = _ln_stats(vg)
        vn = (vh * g_ref[...] + b_ref[...]).astype(BF16)
        da_v = da_ref[...]
        for g in range(groups):
            cols = slice(g * CHUNK, (g + 1) * CHUNK)
            mixed = jnp.dot(wm_ref[g], vn[:, cols], preferred_element_type=F32) + bs_ref[g]
            duv_ref[:, cols] = (da_v[:, cols] * mixed * dug[:, cols]).astype(duv_ref.dtype)
            dm = da_v[:, cols] * ug[:, cols]
            gb_ref[g] += jnp.sum(dm, axis=-1, keepdims=True)
            dmb = dm.astype(BF16)
            gw_ref[g] += lax.dot_general(dmb, vn[:, cols], (((1,), (1,)), ((), ())), preferred_element_type=F32)
            dvn_ref[:, cols] = lax.dot_general(wm_ref[g], dmb, (((0,), (0,)), ((), ())), preferred_element_type=F32)
        dvn = dvn_ref[...]
        acc_ref[0:1, :] += _colsum(dvn * vh)
        acc_ref[1:2, :] += _colsum(dvn)
        dvh = dvn * g_ref[...]
        dv = rstd * (dvh - jnp.mean(dvh, axis=-1, keepdims=True) - vh * jnp.mean(dvh * vh, axis=-1, keepdims=True))
        duv_ref[:, width:] = (dv * dvg).astype(duv_ref.dtype)

        @pl.when(pl.program_id(0) == pl.num_programs(0) - 1)
        def _():
            q = lax.broadcasted_iota(jnp.int32, gw_ref.shape, 1)
            p = lax.broadcasted_iota(jnp.int32, gw_ref.shape, 2)
            gw_ref[...] = jnp.where(p <= q, gw_ref[...], 0.0)

    vec = pl.BlockSpec((1, width), lambda n: (0, 0))
    blk = pl.BlockSpec((CHUNK, width), lambda n: (n, 0))
    return pl.pallas_call(
        body, name="gmlp_bwd", grid=(s // CHUNK,),
        in_specs=[blk, pl.BlockSpec((CHUNK, width), lambda n: (n, 1)), blk, vec, vec, _full(wm.shape), _full(bs3.shape)],
        out_specs=[pl.BlockSpec((CHUNK, 2 * width), lambda n: (n, 0)), _full(wm.shape), _full(bs3.shape), _full((8, width))],
        out_shape=[jax.ShapeDtypeStruct((s, 2 * width), BF16), jax.ShapeDtypeStruct(wm.shape, F32),
                   jax.ShapeDtypeStruct(bs3.shape, F32), jax.ShapeDtypeStruct((8, width), F32)],
        scratch_shapes=[pltpu.VMEM((CHUNK, width), F32)],
        compiler_params=_params("arbitrary"))(z, z, da, ln_g, ln_b, wm, bs3)


def merge_fwd(z, off_a, off_b, ya, yb):
    s, d = ya.shape
    tr, tc = _tile(s, ROW_TILE * 2), _tile(d, COL_TILE)
    assert off_a % tc == 0 and off_b % tc == 0

    def body(ga_ref, gb_ref, ya_ref, yb_ref, o_ref):
        o_ref[...] = (_sigmoid(ga_ref[...]) * ya_ref[...] + _sigmoid(gb_ref[...]) * yb_ref[...]).astype(o_ref.dtype)

    blk = pl.BlockSpec((tr, tc), lambda i, j: (i, j))
    return pl.pallas_call(
        body, name="merge_fwd", grid=(s // tr, d // tc),
        in_specs=[pl.BlockSpec((tr, tc), lambda i, j: (i, off_a // tc + j)), pl.BlockSpec((tr, tc), lambda i, j: (i, off_b // tc + j)), blk, blk],
        out_specs=blk, out_shape=jax.ShapeDtypeStruct((s, d), BF16), compiler_params=_params("parallel", "parallel"))(z, z, ya, yb)


def merge_bwd(z, off_a, off_b, ya, yb, dm):
    s, d = ya.shape
    tr, tc = _tile(s, ROW_TILE * 2), _tile(d, COL_TILE)
    nc = d // tc

    def body(ga_ref, gb_ref, ya_ref, yb_ref, dm_ref, dya_ref, dyb_ref, dga_ref, dgb_ref):
        dm_v = dm_ref[...]
        sa, sb = _sigmoid(ga_ref[...]), _sigmoid(gb_ref[...])
        dya_ref[...] = (dm_v * sa).astype(dya_ref.dtype)
        dyb_ref[...] = (dm_v * sb).astype(dyb_ref.dtype)
        dga_ref[...] = (dm_v * ya_ref[...] * sa * (1.0 - sa)).astype(dga_ref.dtype)
        dgb_ref[...] = (dm_v * yb_ref[...] * sb * (1.0 - sb)).astype(dgb_ref.dtype)

    blk = pl.BlockSpec((tr, tc), lambda i, j: (i, j))
    out = jax.ShapeDtypeStruct((s, d), BF16)
    return pl.pallas_call(
        body, name="merge_bwd", grid=(s // tr, nc),
        in_specs=[pl.BlockSpec((tr, tc), lambda i, j: (i, off_a // tc + j)), pl.BlockSpec((tr, tc), lambda i, j: (i, off_b // tc + j)), blk, blk, blk],
        out_specs=[blk, blk, blk, blk], out_shape=[out, out, out, out],
        compiler_params=_params("parallel", "parallel"))(z, z, ya, yb, dm)


_ATT_SCALE = (QK_NOPE + QK_ROPE) ** -0.5
_NEG = -1e30


def rope_k(z, off, cos4, sin4):
    s = z.shape[0]
    tr = _tile(s, ROW_TILE * 2)
    assert off % LANES == 0

    def body(k_ref, c_ref, s_ref, o_ref):
        k = k_ref[...]
        k = k + pltpu.roll(k, QK_ROPE, 1)
        o_ref[...] = _rope(k, c_ref[...], s_ref[...]).astype(o_ref.dtype)

    row = pl.BlockSpec((tr, LANES), lambda i: (i, 0))
    return pl.pallas_call(body, name="rope_k", grid=(s // tr,),
                          in_specs=[pl.BlockSpec((tr, LANES), lambda i: (i, off // LANES)), row, row], out_specs=row,
                          out_shape=jax.ShapeDtypeStruct((s, LANES), BF16), compiler_params=_params("parallel"))(z, cos4, sin4)


def _head_masks(shape):
    lane = lax.broadcasted_iota(jnp.int32, shape, 1)
    return lane < QK_ROPE, lane >= QK_ROPE


def _scores(qn, qp_h, k, kp, qi, kb, t):
    sc = lax.dot_general(qn, k, (((1,), (1,)), ((), ())), preferred_element_type=F32)
    sc += lax.dot_general(qp_h, kp, (((1,), (1,)), ((), ())), preferred_element_type=F32)
    sc = sc * _ATT_SCALE
    row = lax.broadcasted_iota(jnp.int32, sc.shape, 0) + qi * t
    col = lax.broadcasted_iota(jnp.int32, sc.shape, 1) + kb * t
    return jnp.where(col <= row, sc, _NEG)


def attn_fwd(qn, qp, kv, kpr, cos4, sin4):
    s = qn.shape[0]
    hp = HEADS // 2
    t = _tile(s, ATT_TILE)
    nq = s // t

    def body(qn_ref, qp_ref, kv_ref, kp_ref, c_ref, s_ref, o_ref, qpr_ref, l_ref):
        qi = pl.program_id(1)
        qpr = _rope(qp_ref[...], c_ref[...], s_ref[...]).astype(BF16)
        qpr_ref[...] = qpr
        masks = _head_masks(qpr.shape)
        for hh in range(2):
            q_n = qn_ref[:, hh * QK_NOPE:(hh + 1) * QK_NOPE]
            q_p = jnp.where(masks[hh], qpr, jnp.zeros_like(qpr))
            kc, vc = 2 * hh * QK_NOPE, (2 * hh + 1) * QK_NOPE

            def step(kb, carry):
                m, l, acc = carry
                rows = pl.ds(pl.multiple_of(kb * t, t), t)
                sc = _scores(q_n, q_p, kv_ref[rows, kc:kc + QK_NOPE], kp_ref[rows, :], qi, kb, t)
                m_new = jnp.maximum(m, jnp.max(sc, axis=-1, keepdims=True))
                alpha = jnp.exp(m - m_new)
                p = jnp.exp(sc - m_new)
                l = alpha * l + jnp.sum(p, axis=-1, keepdims=True)
                acc = alpha * acc + jnp.dot(p.astype(BF16), kv_ref[rows, vc:vc + V_HEAD], preferred_element_type=F32)
                return m_new, l, acc

            init = (jnp.full((t, 1), _NEG, F32), jnp.zeros((t, 1), F32), jnp.zeros((t, V_HEAD), F32))
            m, l, acc = lax.fori_loop(0, qi + 1, step, init)
            o_ref[:, hh * V_HEAD:(hh + 1) * V_HEAD] = acc / l
            l_ref[:, hh:hh + 1] = m + jnp.log(l)

    return pl.pallas_call(
        body, name="attn_fwd", grid=(hp, nq),
        in_specs=[pl.BlockSpec((t, 2 * QK_NOPE), lambda h, i: (i, h)), pl.BlockSpec((t, LANES), lambda h, i: (i, h)),
                  pl.BlockSpec((s, 4 * QK_NOPE), lambda h, i: (0, h)), _full((s, LANES)),
                  pl.BlockSpec((t, LANES), lambda h, i: (i, 0)), pl.BlockSpec((t, LANES), lambda h, i: (i, 0))],
        out_specs=[pl.BlockSpec((t, 2 * V_HEAD), lambda h, i: (i, h)), pl.BlockSpec((t, LANES), lambda h, i: (i, h)),
                   pl.BlockSpec((None, t, 2), lambda h, i: (h, i, 0))],
        out_shape=[jax.ShapeDtypeStruct((s, HEADS * V_HEAD), F32), jax.ShapeDtypeStruct((s, HEADS * QK_ROPE), BF16),
                   jax.ShapeDtypeStruct((hp, s, 2), F32)],
        compiler_params=_params("parallel", "parallel"))(qn, qp, kv, kpr, cos4, sin4)


def attn_bwd_q(qn, qpr, kv, kpr, o, do, lse, cos4, sin4):
    s = qn.shape[0]
    hp = HEADS // 2
    t = _tile(s, ATT_TILE)
    nq = s // t

    def body(qn_ref, qpr_ref, kv_ref, kp_ref, o_ref, do_ref, l_ref, c_ref, s_ref, dqn_ref, dqp_ref):
        qi = pl.program_id(1)
        qpr = qpr_ref[...]
        masks = _head_masks(qpr.shape)
        dqp = jnp.zeros(qpr.shape, F32)
        for hh in range(2):
            q_n = qn_ref[:, hh * QK_NOPE:(hh + 1) * QK_NOPE]
            q_p = jnp.where(masks[hh], qpr, jnp.zeros_like(qpr))
            kc, vc = 2 * hh * QK_NOPE, (2 * hh + 1) * QK_NOPE
            do_h = do_ref[:, hh * V_HEAD:(hh + 1) * V_HEAD]
            delta = jnp.sum(do_h * o_ref[:, hh * V_HEAD:(hh + 1) * V_HEAD], axis=-1, keepdims=True)
            do_b = do_h.astype(BF16)
            lse_h = l_ref[:, hh:hh + 1]

            def step(kb, carry):
                dn, dp_ = carry
                rows = pl.ds(pl.multiple_of(kb * t, t), t)
                k = kv_ref[rows, kc:kc + QK_NOPE]
                kp = kp_ref[rows, :]
                p = jnp.exp(_scores(q_n, q_p, k, kp, qi, kb, t) - lse_h)
                dpv = lax.dot_general(do_b, kv_ref[rows, vc:vc + V_HEAD], (((1,), (1,)), ((), ())), preferred_element_type=F32)
                ds = (p * (dpv - delta) * _ATT_SCALE).astype(BF16)
                dn = dn + jnp.dot(ds, k, preferred_element_type=F32)
                dp_ = dp_ + jnp.dot(ds, kp, preferred_element_type=F32)
                return dn, dp_

            dn, dp_h = lax.fori_loop(0, qi + 1, step, (jnp.zeros((t, QK_NOPE), F32), jnp.zeros((t, LANES), F32)))
            dqn_ref[:, hh * QK_NOPE:(hh + 1) * QK_NOPE] = dn.astype(dqn_ref.dtype)
            dqp = dqp + jnp.where(masks[hh], dp_h, jnp.zeros_like(dp_h))
        dqp_ref[...] = _rope(dqp, c_ref[...], -s_ref[...]).astype(dqp_ref.dtype)

    qblk = pl.BlockSpec((t, 2 * QK_NOPE), lambda h, i: (i, h))
    pblk = pl.BlockSpec((t, LANES), lambda h, i: (i, h))
    tab = pl.BlockSpec((t, LANES), lambda h, i: (i, 0))
    return pl.pallas_call(
        body, name="attn_bwd_q", grid=(hp, nq),
        in_specs=[qblk, pblk, pl.BlockSpec((s, 4 * QK_NOPE), lambda h, i: (0, h)), _full((s, LANES)), qblk, qblk,
                  pl.BlockSpec((None, t, 2), lambda h, i: (h, i, 0)), tab, tab],
        out_specs=[qblk, pblk],
        out_shape=[jax.ShapeDtypeStruct((s, HEADS * QK_NOPE), BF16), jax.ShapeDtypeStruct((s, HEADS * QK_ROPE), BF16)],
        compiler_params=_params("parallel", "parallel"))(qn, qpr, kv, kpr, o, do, lse, cos4, sin4)


def attn_bwd_kv(qn, qpr, kv, kpr, o, do, lse):
    s = qn.shape[0]
    hp = HEADS // 2
    t = _tile(s, ATT_TILE)
    nq = s // t

    def body(qn_ref, qpr_ref, kv_ref, kp_ref, o_ref, do_ref, l_ref, dkv_ref, dkp_ref):
        ki = pl.program_id(1)
        rows_k = pl.ds(pl.multiple_of(ki * t, t), t)
        kp = kp_ref[rows_k, :]
        dkp = jnp.zeros((t, LANES), F32)
        for hh in range(2):
            kc, vc = 2 * hh * QK_NOPE, (2 * hh + 1) * QK_NOPE
            k = kv_ref[rows_k, kc:kc + QK_NOPE]
            v = kv_ref[rows_k, vc:vc + V_HEAD]

            def step(qb, carry):
                dk, dv, dkp_h = carry
                rows = pl.ds(pl.multiple_of(qb * t, t), t)
                q_n = qn_ref[rows, hh * QK_NOPE:(hh + 1) * QK_NOPE]
                qpr = qpr_ref[rows, :]
                lane = lax.broadcasted_iota(jnp.int32, qpr.shape, 1)
                sel = (lane < QK_ROPE) if hh == 0 else (lane >= QK_ROPE)
                q_p = jnp.where(sel, qpr, jnp.zeros_like(qpr))
                do_h = do_ref[rows, hh * V_HEAD:(hh + 1) * V_HEAD]
                delta = jnp.sum(do_h * o_ref[rows, hh * V_HEAD:(hh + 1) * V_HEAD], axis=-1, keepdims=True)
                do_b = do_h.astype(BF16)
                p = jnp.exp(_scores(q_n, q_p, k, kp, qb, ki, t) - l_ref[rows, hh:hh + 1])
                dpv = lax.dot_general(do_b, v, (((1,), (1,)), ((), ())), preferred_element_type=F32)
                ds = (p * (dpv - delta) * _ATT_SCALE).astype(BF16)
                dv = dv + lax.dot_general(p.astype(BF16), do_b, (((0,), (0,)), ((), ())), preferred_element_type=F32)
                dk = dk + lax.dot_general(ds, q_n, (((0,), (0,)), ((), ())), preferred_element_type=F32)
                dkp_h = dkp_h + lax.dot_general(ds, q_p, (((0,), (0,)), ((), ())), preferred_element_type=F32)
                return dk, dv, dkp_h

            init = (jnp.zeros((t, QK_NOPE), F32), jnp.zeros((t, V_HEAD), F32), jnp.zeros((t, LANES), F32))
            dk, dv, dkp_h = lax.fori_loop(ki, nq, step, init)
            dkv_ref[:, kc:kc + QK_NOPE] = dk.astype(dkv_ref.dtype)
            dkv_ref[:, vc:vc + V_HEAD] = dv.astype(dkv_ref.dtype)
            dkp = dkp + dkp_h
        dkp_ref[...] = dkp

    return pl.pallas_call(
        body, name="attn_bwd_kv", grid=(hp, nq),
        in_specs=[pl.BlockSpec((s, 2 * QK_NOPE), lambda h, i: (0, h)), pl.BlockSpec((s, LANES), lambda h, i: (0, h)),
                  pl.BlockSpec((s, 4 * QK_NOPE), lambda h, i: (0, h)), _full((s, LANES)),
                  pl.BlockSpec((s, 2 * V_HEAD), lambda h, i: (0, h)), pl.BlockSpec((s, 2 * V_HEAD), lambda h, i: (0, h)),
                  pl.BlockSpec((None, s, 2), lambda h, i: (h, 0, 0))],
        out_specs=[pl.BlockSpec((t, 4 * QK_NOPE), lambda h, i: (i, h)), pl.BlockSpec((None, t, LANES), lambda h, i: (h, i, 0))],
        out_shape=[jax.ShapeDtypeStruct((s, HEADS * 2 * QK_NOPE), BF16), jax.ShapeDtypeStruct((hp, s, LANES), F32)],
        compiler_params=_params("parallel", "parallel"))(qn, qpr, kv, kpr, o, do, lse)


def _dot_nt(a, b):
    return lax.dot_general(a, b, (((1,), (1,)), ((), ())), preferred_element_type=F32)


def _dot_tn(a, b):
    return lax.dot_general(a, b, (((0,), (0,)), ((), ())), preferred_element_type=F32)


def _q_cat(q_n, qpr, hh):
    lane = lax.broadcasted_iota(jnp.int32, qpr.shape, 1)
    sel = (lane < QK_ROPE) if hh == 0 else (lane >= QK_ROPE)
    return jnp.concatenate([q_n, jnp.where(sel, qpr, jnp.zeros_like(qpr))], axis=1)


def _causal(sc):
    row = lax.broadcasted_iota(jnp.int32, sc.shape, 0)
    col = lax.broadcasted_iota(jnp.int32, sc.shape, 1)
    return jnp.where(col <= row, sc, _NEG)


def attn_fwd2(qn, qp, kv, kpr, cos4, sin4):
    s = qn.shape[0]
    hp = HEADS // 2
    t = _tile(s, ATT_TILE)
    nq = s // t

    def body(qn_ref, qp_ref, kv_ref, kp_ref, c_ref, s_ref, o_ref, qpr_ref, l_ref, kcat_ref):
        qi = pl.program_id(1)

        @pl.when(qi == 0)
        def _():
            for hh in range(2):
                kcat_ref[hh, :, 0:QK_NOPE] = kv_ref[:, 2 * hh * QK_NOPE:(2 * hh + 1) * QK_NOPE]
                kcat_ref[hh, :, QK_NOPE:] = kp_ref[...]

        qpr = _rope(qp_ref[...], c_ref[...], s_ref[...]).astype(BF16)
        qpr_ref[...] = qpr
        qcat = [_q_cat(qn_ref[:, hh * QK_NOPE:(hh + 1) * QK_NOPE], qpr, hh) for hh in range(2)]

        def block(kb, carry, diagonal):
            rows = pl.ds(pl.multiple_of(kb * t, t), t)
            out = []
            for hh in range(2):
                m, l, acc = carry[hh]
                sc = _dot_nt(qcat[hh], kcat_ref[hh, rows, :]) * _ATT_SCALE
                if diagonal:
                    sc = _causal(sc)
                m_new = jnp.maximum(m, jnp.max(sc, axis=-1, keepdims=True))
                alpha = jnp.exp(m - m_new)
                p = jnp.exp(sc - m_new)
                l = alpha * l + jnp.sum(p, axis=-1, keepdims=True)
                v = kv_ref[rows, (2 * hh + 1) * QK_NOPE:(2 * hh + 2) * QK_NOPE]
                acc = alpha * acc + jnp.dot(p.astype(BF16), v, preferred_element_type=F32)
                out.append((m_new, l, acc))
            return tuple(out)

        one = (jnp.full((t, 1), _NEG, F32), jnp.zeros((t, 1), F32), jnp.zeros((t, V_HEAD), F32))
        carry = lax.fori_loop(0, qi, lambda kb, cr: block(kb, cr, False), (one, one))
        carry = block(qi, carry, True)
        for hh in range(2):
            m, l, acc = carry[hh]
            o_ref[:, hh * V_HEAD:(hh + 1) * V_HEAD] = acc / l
            l_ref[:, hh:hh + 1] = m + jnp.log(l)

    return pl.pallas_call(
        body, name="attn_fwd", grid=(hp, nq),
        in_specs=[pl.BlockSpec((t, 2 * QK_NOPE), lambda h, i: (i, h)), pl.BlockSpec((t, LANES), lambda h, i: (i, h)),
                  pl.BlockSpec((s, 4 * QK_NOPE), lambda h, i: (0, h)), _full((s, LANES)),
                  pl.BlockSpec((t, LANES), lambda h, i: (i, 0)), pl.BlockSpec((t, LANES), lambda h, i: (i, 0))],
        out_specs=[pl.BlockSpec((t, 2 * V_HEAD), lambda h, i: (i, h)), pl.BlockSpec((t, LANES), lambda h, i: (i, h)),
                   pl.BlockSpec((None, t, 2), lambda h, i: (h, i, 0))],
        out_shape=[jax.ShapeDtypeStruct((s, HEADS * V_HEAD), F32), jax.ShapeDtypeStruct((s, HEADS * QK_ROPE), BF16),
                   jax.ShapeDtypeStruct((hp, s, 2), F32)],
        scratch_shapes=[pltpu.VMEM((2, s, 2 * QK_NOPE), BF16)],
        compiler_params=_params("parallel", "arbitrary"))(qn, qp, kv, kpr, cos4, sin4)


def attn_bwd2(qn, qpr, kv, kpr, o, do, lse, cos4, sin4):
    s = qn.shape[0]
    hp = HEADS // 2
    t = _tile(s, ATT_TILE)
    nk = s // t

    def body(qn_ref, qpr_ref, kv_ref, kp_ref, o_ref, do_ref, l_ref, c_ref, s_ref,
             dqn_ref, dqp_ref, dkv_ref, dkp_ref, qcat_ref, dq_ref, delta_ref):
        ki = pl.program_id(1)

        @pl.when(ki == 0)
        def _():
            dq_ref[...] = jnp.zeros_like(dq_ref)
            for hh in range(2):
                qcat_ref[hh] = _q_cat(qn_ref[:, hh * QK_NOPE:(hh + 1) * QK_NOPE], qpr_ref[...], hh)
                cols = slice(hh * V_HEAD, (hh + 1) * V_HEAD)
                delta_ref[hh] = jnp.sum(do_ref[:, cols] * o_ref[:, cols], axis=-1, keepdims=True)

        rows_k = pl.ds(pl.multiple_of(ki * t, t), t)
        kcat = [jnp.concatenate([kv_ref[rows_k, 2 * hh * QK_NOPE:(2 * hh + 1) * QK_NOPE], kp_ref[rows_k, :]], axis=1) for hh in range(2)]
        vs = [kv_ref[rows_k, (2 * hh + 1) * QK_NOPE:(2 * hh + 2) * QK_NOPE] for hh in range(2)]

        def block(qb, carry, diagonal):
            rows = pl.ds(pl.multiple_of(qb * t, t), t)
            out = []
            for hh in range(2):
                dkc, dv = carry[hh]
                q_c = qcat_ref[hh, rows, :]
                do_b = do_ref[rows, hh * V_HEAD:(hh + 1) * V_HEAD].astype(BF16)
                sc = _dot_nt(q_c, kcat[hh]) * _ATT_SCALE
                if diagonal:
                    sc = _causal(sc)
                p = jnp.exp(sc - l_ref[rows, hh:hh + 1])
                dpv = _dot_nt(do_b, vs[hh])
                ds = (p * (dpv - delta_ref[hh, rows, :]) * _ATT_SCALE).astype(BF16)
                dv = dv + _dot_tn(p.astype(BF16), do_b)
                dkc = dkc + _dot_tn(ds, q_c)
                dq_ref[hh, rows, :] += jnp.dot(ds, kcat[hh], preferred_element_type=F32)
                out.append((dkc, dv))
            return tuple(out)

        one = (jnp.zeros((t, 2 * QK_NOPE), F32), jnp.zeros((t, V_HEAD), F32))
        carry = block(ki, (one, one), True)
        carry = lax.fori_loop(ki + 1, nk, lambda qb, cr: block(qb, cr, False), carry)
        dkp = jnp.zeros((t, LANES), F32)
        for hh in range(2):
            dkc, dv = carry[hh]
            dkv_ref[:, 2 * hh * QK_NOPE:(2 * hh + 1) * QK_NOPE] = dkc[:, :QK_NOPE].astype(dkv_ref.dtype)
            dkv_ref[:, (2 * hh + 1) * QK_NOPE:(2 * hh + 2) * QK_NOPE] = dv.astype(dkv_ref.dtype)
            dkp = dkp + dkc[:, QK_NOPE:]
        dkp_ref[...] = dkp

        @pl.when(ki == nk - 1)
        def _():
            lane = lax.broadcasted_iota(jnp.int32, (s, LANES), 1)
            dqp = jnp.where(lane < QK_ROPE, dq_ref[0, :, QK_NOPE:], dq_ref[1, :, QK_NOPE:])
            dqp_ref[...] = _rope(dqp, c_ref[...], -s_ref[...]).astype(dqp_ref.dtype)
            for hh in range(2):
                dqn_ref[:, hh * QK_NOPE:(hh + 1) * QK_NOPE] = dq_ref[hh, :, :QK_NOPE].astype(dqn_ref.dtype)

    qblk = pl.BlockSpec((s, 2 * QK_NOPE), lambda h, i: (0, h))
    pblk = pl.BlockSpec((s, LANES), lambda h, i: (0, h))
    tab = _full((s, LANES))
    return pl.pallas_call(
        body, name="attn_bwd", grid=(hp, nk),
        in_specs=[qblk, pblk, pl.BlockSpec((s, 4 * QK_NOPE), lambda h, i: (0, h)), tab, qblk, qblk,
                  pl.BlockSpec((None, s, 2), lambda h, i: (h, 0, 0)), tab, tab],
        out_specs=[qblk, pblk, pl.BlockSpec((t, 4 * QK_NOPE), lambda h, i: (i, h)), pl.BlockSpec((None, t, LANES), lambda h, i: (h, i, 0))],
        out_shape=[jax.ShapeDtypeStruct((s, HEADS * QK_NOPE), BF16), jax.ShapeDtypeStruct((s, HEADS * QK_ROPE), BF16),
                   jax.ShapeDtypeStruct((s, HEADS * 2 * QK_NOPE), BF16), jax.ShapeDtypeStruct((hp, s, LANES), F32)],
        scratch_shapes=[pltpu.VMEM((2, s, 2 * QK_NOPE), BF16), pltpu.VMEM((2, s, 2 * QK_NOPE), F32), pltpu.VMEM((2, s, 1), F32)],
        compiler_params=_params("parallel", "arbitrary"))(qn, qpr, kv, kpr, o, do, lse, cos4, sin4)


def kpe_bwd(dkp, cos4, sin4, pad_cols):
    hp, s, _ = dkp.shape
    tr = _tile(s, ROW_TILE * 2)

    def body(d_ref, c_ref, s_ref, o_ref):
        tot = d_ref[0]
        for h in range(1, hp):
            tot = tot + d_ref[h]
        tot = tot + pltpu.roll(tot, QK_ROPE, 1)
        lane = lax.broadcasted_iota(jnp.int32, tot.shape, 1)
        dk = jnp.where(lane < QK_ROPE, _rope(tot, c_ref[...], -s_ref[...]), jnp.zeros_like(tot))
        o_ref[...] = jnp.zeros_like(o_ref)
        o_ref[:, 0:LANES] = dk.astype(o_ref.dtype)

    row = pl.BlockSpec((tr, LANES), lambda i: (i, 0))
    return pl.pallas_call(body, name="kpe_bwd", grid=(s // tr,),
                          in_specs=[pl.BlockSpec((hp, tr, LANES), lambda i: (0, i, 0)), row, row],
                          out_specs=pl.BlockSpec((tr, pad_cols), lambda i: (i, 0)),
                          out_shape=jax.ShapeDtypeStruct((s, pad_cols), BF16), compiler_params=_params("parallel"))(dkp, cos4, sin4)


def _shift_down(x, n):
    row = lax.broadcasted_iota(jnp.int32, x.shape, 0)
    return jnp.where(row >= n, pltpu.roll(x, n, 0), jnp.zeros_like(x))


def _shift_up(x, n):
    rows = x.shape[0]
    row = lax.broadcasted_iota(jnp.int32, x.shape, 0)
    return jnp.where(row < rows - n, pltpu.roll(x, rows - n, 0), jnp.zeros_like(x))


def _conv(x, w_ref, b_ref):
    return w_ref[2:3, :] * x + w_ref[1:2, :] * _shift_down(x, 1) + w_ref[0:1, :] * _shift_down(x, 2) + b_ref[...]


def conv_act_fwd(upre, conv_w, conv_b):
    s, f2 = upre.shape
    f = f2 // 2
    tc = _tile(f, COL_TILE)
    nc = f // tc

    def body(ug_ref, uv_ref, wg_ref, wv_ref, bg_ref, bv_ref, o_ref):
        gh = _conv(ug_ref[...], wg_ref, bg_ref)
        vh = _conv(uv_ref[...], wv_ref, bv_ref)
        o_ref[...] = (gh * _sigmoid(gh) * vh).astype(o_ref.dtype)

    def spec(rows, shift):
        return pl.BlockSpec((rows, tc), lambda j: (0, j + shift))

    return pl.pallas_call(
        body, name="conv_act_fwd", grid=(nc,),
        in_specs=[spec(s, 0), spec(s, nc), spec(3, 0), spec(3, nc), spec(1, 0), spec(1, nc)], out_specs=spec(s, 0),
        out_shape=jax.ShapeDtypeStruct((s, f), BF16), compiler_params=_params("parallel"))(upre, upre, conv_w, conv_w, conv_b, conv_b)


def conv_act_bwd(upre, conv_w, conv_b, df):
    s, f2 = upre.shape
    f = f2 // 2
    tc = _tile(f, COL_TILE)
    nc = f // tc

    def half(x, d, w_ref, du_ref, gw_ref, gb_ref):
        gb_ref[...] = _colsum(d)
        gw_ref[2:3, :] = _colsum(d * x)
        gw_ref[1:2, :] = _colsum(d * _shift_down(x, 1))
        gw_ref[0:1, :] = _colsum(d * _shift_down(x, 2))
        du_ref[...] = (w_ref[2:3, :] * d + w_ref[1:2, :] * _shift_up(d, 1) + w_ref[0:1, :] * _shift_up(d, 2)).astype(du_ref.dtype)

    def body(ug_ref, uv_ref, wg_ref, wv_ref, bg_ref, bv_ref, df_ref, dug_ref, duv_ref, gwg_ref, gwv_ref, gbg_ref, gbv_ref):
        xg, xv = ug_ref[...], uv_ref[...]
        gh = _conv(xg, wg_ref, bg_ref)
        vh = _conv(xv, wv_ref, bv_ref)
        sg = _sigmoid(gh)
        df_v = df_ref[...]
        half(xg, df_v * vh * (sg * (1.0 + gh * (1.0 - sg))), wg_ref, dug_ref, gwg_ref, gbg_ref)
        half(xv, df_v * (gh * sg), wv_ref, duv_ref, gwv_ref, gbv_ref)

    def spec(rows, shift):
        return pl.BlockSpec((rows, tc), lambda j: (0, j + shift))

    act = jax.ShapeDtypeStruct((s, f), BF16)
    gw = jax.ShapeDtypeStruct((3, f), F32)
    gb = jax.ShapeDtypeStruct((1, f), F32)
    return pl.pallas_call(
        body, name="conv_act_bwd", grid=(nc,),
        in_specs=[spec(s, 0), spec(s, nc), spec(3, 0), spec(3, nc), spec(1, 0), spec(1, nc), spec(s, 0)],
        out_specs=[spec(s, 0), spec(s, 0), spec(3, 0), spec(3, 0), spec(1, 0), spec(1, 0)],
        out_shape=[act, act, gw, gw, gb, gb],
        compiler_params=_params("parallel"))(upre, upre, conv_w, conv_w, conv_b, conv_b, df)


def adamw(name, w, m, v, parts, row_off=0):
    npart, c = parts.shape[0], parts.shape[2]
    r = w.shape[0]
    tr = r
    if r % 8 == 0:
        tr = max(8, min(r, ADAMW_TILE_ELEMS // c) // 8 * 8)
        while r % tr:
            tr -= 8
    bc1 = 1.0 - ADAM_B1 ** ADAM_STEP
    bc2 = 1.0 - ADAM_B2 ** ADAM_STEP

    def body(w_ref, m_ref, v_ref, p_ref, g_ref, d_ref, nm_ref, nv_ref):
        g = p_ref[0].astype(F32)
        for k in range(1, npart):
            g = g + p_ref[k].astype(F32)
        m_new = ADAM_B1 * m_ref[...] + (1.0 - ADAM_B1) * g
        v_new = ADAM_B2 * v_ref[...] + (1.0 - ADAM_B2) * (g * g)
        g_ref[...] = g
        nm_ref[...] = m_new
        nv_ref[...] = v_new
        d_ref[...] = -ADAM_LR * ((m_new / bc1) / (jnp.sqrt(v_new / bc2) + ADAM_EPS) + ADAM_WD * w_ref[...])

    assert row_off % tr == 0
    deps = _TOKENS.take()
    blk = pl.BlockSpec((tr, c), lambda i: (i, 0))
    out = jax.ShapeDtypeStruct((r, c), F32)
    return pl.pallas_call(
        lambda *refs: body(*refs[:4], *refs[4 + len(deps):]), name=name, grid=(r // tr,),
        in_specs=[blk, blk, blk, pl.BlockSpec((npart, tr, c), lambda i: (0, row_off // tr + i, 0))] + [pl.BlockSpec(memory_space=pl.ANY)] * len(deps),
        out_specs=[blk, blk, blk, blk], out_shape=[out, out, out, out], compiler_params=_params("parallel"))(w, m, v, parts, *deps)


def _position():
    return lax.axis_index("x"), lax.axis_index("y"), lax.axis_index("c")


def _index(p):
    return 4 * p[0] + 2 * p[1] + p[2]


def _peer(me, r):
    return (me[0] ^ ((r >> 2) & 1), me[1] ^ ((r >> 1) & 1), me[2] ^ (r & 1))


_ANY = pl.BlockSpec(memory_space=pl.ANY)


def all_gather_two_level(shards):
    n = len(shards)

    def body(*refs):
        ins, outs = refs[:n], refs[n:2 * n]
        send_sems, recv_sems, local_sems = refs[2 * n:]
        x, y, c = _position()
        me, sibling = (x, y, c), (x, y, 1 - c)
        chips = [(1 - x, y), (x, 1 - y), (1 - x, 1 - y)]

        def copy(w, k, block, to, src=None):
            slot = outs[w].at[_index(block)]
            return pltpu.make_async_remote_copy(src_ref=slot if src is None else src, dst_ref=slot,
                                                send_sem=send_sems.at[7 * w + k], recv_sem=recv_sems.at[7 * w + k],
                                                device_id=to, device_id_type=MESH)

        mine = [pltpu.make_async_copy(ins[w], outs[w].at[_index(me)], local_sems.at[w]) for w in range(n)]
        for cp in mine:
            cp.start()
        first = []
        for w in range(n):
            first.append(copy(w, 0, me, sibling, src=ins[w]))
            first += [copy(w, 1 + j, me, (*chip, c), src=ins[w]) for j, chip in enumerate(chips)]
        for cp in first:
            cp.start()
        passed = []
        for w in range(n):
            for j, chip in enumerate(chips):
                copy(w, 1 + j, (*chip, c), me).wait_recv()
                cp = copy(w, 4 + j, (*chip, c), sibling)
                cp.start()
                passed.append(cp)
        for w in range(n):
            copy(w, 0, sibling, me).wait_recv()
            for j, chip in enumerate(chips):
                copy(w, 4 + j, (*chip, 1 - c), me).wait_recv()
        for cp in first + passed:
            cp.wait_send()
        for cp in mine:
            cp.wait()

    return pl.pallas_call(
        body, name="all_gather_weights",
        out_shape=[jax.ShapeDtypeStruct((N_DEV,) + a.shape, a.dtype) for a in shards],
        in_specs=[_ANY] * n, out_specs=[_ANY] * n,
        scratch_shapes=[pltpu.SemaphoreType.DMA((7 * n,)), pltpu.SemaphoreType.DMA((7 * n,)), pltpu.SemaphoreType.DMA((n,))],
        )(*shards)


def exchange(name, arrays, scatter):
    n = len(arrays)

    def body(*refs):
        ins, outs = refs[:n], refs[n:2 * n]
        send_sems, recv_sems, local_sems = refs[2 * n:]
        me = _position()
        copies = []
        for w in range(n):
            src = ins[w].at[_index(me)] if scatter else ins[w]
            cp = pltpu.make_async_copy(src, outs[w].at[_index(me)], local_sems.at[w])
            cp.start()
            copies.append(cp)
        remote = []
        for w in range(n):
            for r in range(1, N_DEV):
                peer = _peer(me, r)
                src = ins[w].at[_index(peer)] if scatter else ins[w]
                cp = pltpu.make_async_remote_copy(src_ref=src, dst_ref=outs[w].at[_index(me)],
                                                  send_sem=send_sems.at[7 * w + r - 1], recv_sem=recv_sems.at[7 * w + r - 1],
                                                  device_id=peer, device_id_type=MESH)
                cp.start()
                remote.append(cp)
        for cp in remote:
            cp.wait()
        for cp in copies:
            cp.wait()

    blocks = [a.shape[1:] if scatter else a.shape for a in arrays]
    return pl.pallas_call(
        body, name=name,
        out_shape=[jax.ShapeDtypeStruct((N_DEV,) + b, a.dtype) for a, b in zip(arrays, blocks)],
        in_specs=[_ANY] * n, out_specs=[_ANY] * n,
        scratch_shapes=[pltpu.SemaphoreType.DMA((7 * n,)), pltpu.SemaphoreType.DMA((7 * n,)), pltpu.SemaphoreType.DMA((n,))],
        )(*arrays)


_HBM = pl.BlockSpec(memory_space=pltpu.HBM)
_SEM = pl.BlockSpec(memory_space=pltpu.SEMAPHORE)
_EFFECT = pltpu.SideEffectType.DATAFLOW_SIDE_EFFECTING


def _direct_copies(ins, lands, send_sems, recv_sems, scatter):
    me = _position()
    copies = []
    for w in range(len(ins)):
        for r in range(1, N_DEV):
            peer = _peer(me, r)
            src = ins[w].at[_index(peer)] if scatter else ins[w]
            copies.append(pltpu.make_async_remote_copy(src_ref=src, dst_ref=lands[w].at[_index(me)], send_sem=send_sems.at[7 * w + r - 1],
                                                       recv_sem=recv_sems.at[7 * w + r - 1], device_id=peer, device_id_type=MESH))
    return copies


def exchange_start(name, groups, scatter):
    arrays = [a for g in groups for a in g]
    n = len(arrays)
    blocks = [a.shape[1:] if scatter else a.shape for a in arrays]
    lands = [lax.empty((N_DEV,) + b, a.dtype) for a, b in zip(arrays, blocks)]
    ng = len(groups)

    def body(*refs):
        ins, lnd = refs[:n], refs[n:2 * n]
        sems = refs[2 * n:2 * n + 2 * ng]
        token = refs[2 * n + 2 * ng + 2 * n]
        local_sem = refs[2 * n + 2 * ng + 2 * n + 1]
        me = _position()
        local = []
        for w in range(n):
            src = ins[w].at[_index(me)] if scatter else ins[w]
            cp = pltpu.make_async_copy(src, lnd[w].at[_index(me)], local_sem.at[w])
            cp.start()
            local.append(cp)
        w0 = 0
        for gi, g in enumerate(groups):
            for cp in _direct_copies(ins[w0:w0 + len(g)], lnd[w0:w0 + len(g)], sems[2 * gi], sems[2 * gi + 1], scatter):
                cp.start()
            w0 += len(g)
        for cp in local:
            cp.wait()
        token[...] = jnp.zeros_like(token)

    sem_shapes = []
    for g in groups:
        sem_shapes += [pltpu.SemaphoreType.DMA((7 * len(g),)), pltpu.SemaphoreType.DMA((7 * len(g),))]
    out = pl.pallas_call(
        body, name=name,
        out_shape=tuple(sem_shapes) + tuple(pltpu.HBM(a.shape, a.dtype) for a in arrays) + tuple(pltpu.HBM(l.shape, l.dtype) for l in lands)
        + (jax.ShapeDtypeStruct((8, LANES), F32),),
        in_specs=[_HBM] * (2 * n), out_specs=tuple([_SEM] * (2 * ng) + [_HBM] * (2 * n) + [pl.BlockSpec(memory_space=pltpu.VMEM)]),
        input_output_aliases={i: 2 * ng + i for i in range(2 * n)},
        scratch_shapes=[pltpu.SemaphoreType.DMA((n,))],
        compiler_params=pltpu.CompilerParams(has_side_effects=_EFFECT),
    )(*[pltpu.with_memory_space_constraint(a, pltpu.HBM) for a in arrays], *[pltpu.with_memory_space_constraint(l, pltpu.HBM) for l in lands])
    sems, thru, token = out[:2 * ng], out[2 * ng:2 * ng + 2 * n], out[-1]
    res, w0 = [], 0
    for gi, g in enumerate(groups):
        res.append((sems[2 * gi], sems[2 * gi + 1], list(thru[w0:w0 + len(g)]), list(thru[n + w0:n + w0 + len(g)])))
        w0 += len(g)
    return res, token


def exchange_wait(name, group, after, scatter):
    send_sems, recv_sems, srcs, lands = group
    n = len(srcs)

    def body(*refs):
        ins, lnd = refs[:n], refs[n:2 * n]
        for cp in _direct_copies(ins, lnd, refs[2 * n], refs[2 * n + 1], scatter):
            cp.wait_send()
            cp.wait_recv()

    out = pl.pallas_call(
        body, name=name, out_shape=tuple(pltpu.HBM(a.shape, a.dtype) for a in srcs + lands),
        in_specs=[_HBM] * (2 * n) + [_SEM, _SEM, pl.BlockSpec(memory_space=pl.ANY)], out_specs=tuple([_HBM] * (2 * n)),
        input_output_aliases={i: i for i in range(2 * n)},
        compiler_params=pltpu.CompilerParams(has_side_effects=_EFFECT),
    )(*srcs, *lands, send_sems, recv_sems, after)
    return list(out[n:])


def _after(x, token):
    return lax.optimization_barrier((x, token))[0]


_TOKEN = jax.ShapeDtypeStruct((8, LANES), F32)
_VM = pl.BlockSpec(memory_space=pltpu.VMEM)
_SIDE = pltpu.CompilerParams(has_side_effects=_EFFECT)


def _hbm(a):
    return pltpu.with_memory_space_constraint(a, pltpu.HBM)


def _like(a):
    return pltpu.HBM(a.shape, a.dtype)


def _dma_sems(n):
    return pltpu.SemaphoreType.DMA((n,))


def _other_chips(x, y):
    return [(1 - x, y), (x, 1 - y), (1 - x, 1 - y)]


def _rcopy(src, dst, send_sem, recv_sem, to):
    return pltpu.make_async_remote_copy(src_ref=src, dst_ref=dst, send_sem=send_sem, recv_sem=recv_sem, device_id=to, device_id_type=MESH)


def ag_start(name, shard, after):
    land = lax.empty((N_DEV,) + shard.shape, shard.dtype)

    def body(sh_ref, land_ref, after_ref, send_sems, recv_sems, sh_thru, land_thru, token):
        x, y, c = _position()
        slot = land_ref.at[_index((x, y, c))]
        for k, to in enumerate([(x, y, 1 - c)] + [(*chip, c) for chip in _other_chips(x, y)]):
            _rcopy(sh_ref, slot, send_sems.at[k], recv_sems.at[k], to).start()
        token[...] = jnp.zeros_like(token)

    send, recv, shard, land, token = pl.pallas_call(
        body, name=name, out_shape=(_dma_sems(4), _dma_sems(4), _like(shard), _like(land), _TOKEN),
        in_specs=[_HBM, _HBM, _ANY], out_specs=(_SEM, _SEM, _HBM, _HBM, _VM), input_output_aliases={0: 2, 1: 3},
        compiler_params=_SIDE)(_hbm(shard), _hbm(land), after)
    _TOKENS.push(token)
    return send, recv, shard, land


def ag_forward(name, started, after):
    send, recv, shard, land = started

    def body(sh_ref, land_ref, send_sems, recv_sems, after_ref, fsend, frecv, sh_thru, land_thru, token):
        x, y, c = _position()
        for j, chip in enumerate(_other_chips(x, y)):
            slot = land_ref.at[_index((*chip, c))]
            _rcopy(sh_ref, slot, send_sems.at[1 + j], recv_sems.at[1 + j], (*chip, c)).wait_recv()
            _rcopy(slot, slot, fsend.at[j], frecv.at[j], (x, y, 1 - c)).start()
        token[...] = jnp.zeros_like(token)

    fsend, frecv, shard, land, token = pl.pallas_call(
        body, name=name, out_shape=(_dma_sems(3), _dma_sems(3), _like(shard), _like(land), _TOKEN),
        in_specs=[_HBM, _HBM, _SEM, _SEM, _ANY], out_specs=(_SEM, _SEM, _HBM, _HBM, _VM), input_output_aliases={0: 2, 1: 3},
        compiler_params=_SIDE)(shard, land, send, recv, after)
    _TOKENS.push(token)
    return send, recv, fsend, frecv, shard, land


def ag_wait(name, forwarded, after):
    send, recv, fsend, frecv, shard, land = forwarded

    def body(sh_ref, land_ref, send_sems, recv_sems, fsend_r, frecv_r, after_ref, sh_out, land_out, local_sem):
        x, y, c = _position()
        sibling = (x, y, 1 - c)
        own = land_ref.at[_index((x, y, c))]
        mine = pltpu.make_async_copy(sh_ref, own, local_sem.at[0])
        mine.start()
        _rcopy(sh_ref, land_ref.at[_index(sibling)], send_sems.at[0], recv_sems.at[0], sibling).wait_recv()
        for j, chip in enumerate(_other_chips(x, y)):
            _rcopy(sh_ref, land_ref.at[_index((*chip, 1 - c))], fsend_r.at[j], frecv_r.at[j], sibling).wait_recv()
        for k in range(4):
            _rcopy(sh_ref, own, send_sems.at[k], recv_sems.at[k], sibling).wait_send()
        for j in range(3):
            _rcopy(sh_ref, own, fsend_r.at[j], frecv_r.at[j], sibling).wait_send()
        mine.wait()

    return pl.pallas_call(
        body, name=name, out_shape=(_like(shard), _like(land)), in_specs=[_HBM, _HBM, _SEM, _SEM, _SEM, _SEM, _ANY],
        out_specs=(_HBM, _HBM), input_output_aliases={0: 0, 1: 1}, scratch_shapes=[_dma_sems(1)],
        compiler_params=_SIDE)(shard, land, send, recv, fsend, frecv, after)[1]


def rs_d2d_start(name, grads):
    n = len(grads)
    lands = [lax.empty((4,) + g.shape[1:], g.dtype) for g in grads]

    def body(*refs):
        ins, lnd, send_sems, recv_sems, token = refs[:n], refs[n:2 * n], refs[2 * n], refs[2 * n + 1], refs[4 * n + 2]
        x, y, c = _position()
        for w in range(n):
            for i in range(4):
                _rcopy(ins[w].at[2 * i + 1 - c], lnd[w].at[i], send_sems.at[4 * w + i], recv_sems.at[4 * w + i], (x, y, 1 - c)).start()
        token[...] = jnp.zeros_like(token)

    out = pl.pallas_call(
        body, name=name, out_shape=(_dma_sems(4 * n), _dma_sems(4 * n)) + tuple(_like(a) for a in grads + lands) + (_TOKEN,),
        in_specs=[_HBM] * (2 * n), out_specs=(_SEM, _SEM) + (_HBM,) * (2 * n) + (_VM,),
        input_output_aliases={i: 2 + i for i in range(2 * n)}, compiler_params=_SIDE)(*[_hbm(a) for a in grads + lands])
    _TOKENS.push(out[-1])
    return out[0], out[1], list(out[2:2 + n]), list(out[2 + n:2 + 2 * n])


def rs_d2d_wait(name, started, after):
    send, recv, grads, lands = started
    n = len(grads)

    def body(*refs):
        ins, lnd, send_sems, recv_sems = refs[:n], refs[n:2 * n], refs[2 * n], refs[2 * n + 1]
        x, y, c = _position()
        for w in range(n):
            for i in range(4):
                cp = _rcopy(ins[w].at[2 * i + 1 - c], lnd[w].at[i], send_sems.at[4 * w + i], recv_sems.at[4 * w + i], (x, y, 1 - c))
                cp.wait_send()
                cp.wait_recv()

    out = pl.pallas_call(
        body, name=name, out_shape=tuple(_like(a) for a in grads + lands), in_specs=[_HBM] * (2 * n) + [_SEM, _SEM, _ANY],
        out_specs=(_HBM,) * (2 * n), input_output_aliases={i: i for i in range(2 * n)}, compiler_params=_SIDE)(*grads, *lands, send, recv, after)
    return list(out[:n]), list(out[n:])


def pair_sum(name, grad, land, core):
    _, r, c = grad.shape
    tr = r
    if r % 8 == 0:
        tr = max(8, min(r, ADAMW_TILE_ELEMS // c) // 8 * 8)
        while r % tr:
            tr -= 8

    def body(core_ref, a_ref, b_ref, o_ref):
        o_ref[...] = (a_ref[...].astype(F32) + b_ref[...].astype(F32)).astype(o_ref.dtype)

    return pl.pallas_call(
        body, name=name, out_shape=jax.ShapeDtypeStruct((4, r, c), grad.dtype),
        grid_spec=pltpu.PrefetchScalarGridSpec(
            num_scalar_prefetch=1, grid=(4, r // tr),
            in_specs=[pl.BlockSpec((None, None, tr, c), lambda i, j, core_ref: (i, core_ref[0], j, 0)),
                      pl.BlockSpec((None, tr, c), lambda i, j, core_ref: (i, j, 0))],
            out_specs=pl.BlockSpec((None, tr, c), lambda i, j, core_ref: (i, j, 0))),
        compiler_params=_params("parallel", "parallel"))(core, grad.reshape(4, 2, r, c), land)


def rs_ici_start(name, sums):
    n = len(sums)
    lands = [lax.empty(a.shape, a.dtype) for a in sums]

    def body(*refs):
        ins, lnd, send_sems, recv_sems, token = refs[:n], refs[n:2 * n], refs[2 * n], refs[2 * n + 1], refs[4 * n + 2]
        x, y, c = _position()
        chip = 2 * x + y
        for w in range(n):
            for j, other in enumerate(_other_chips(x, y)):
                _rcopy(ins[w].at[2 * other[0] + other[1]], lnd[w].at[chip], send_sems.at[3 * w + j], recv_sems.at[3 * w + j], (*other, c)).start()
        token[...] = jnp.zeros_like(token)

    out = pl.pallas_call(
        body, name=name, out_shape=(_dma_sems(3 * n), _dma_sems(3 * n)) + tuple(_like(a) for a in sums + lands) + (_TOKEN,),
        in_specs=[_HBM] * (2 * n), out_specs=(_SEM, _SEM) + (_HBM,) * (2 * n) + (_VM,),
        input_output_aliases={i: 2 + i for i in range(2 * n)}, compiler_params=_SIDE)(*[_hbm(a) for a in sums + lands])
    _TOKENS.push(out[-1])
    return out[0], out[1], list(out[2:2 + n]), list(out[2 + n:2 + 2 * n])


def rs_ici_wait(name, started, after):
    send, recv, sums, lands = started
    n = len(sums)

    def body(*refs):
        ins, lnd, send_sems, recv_sems, local_sem = refs[:n], refs[n:2 * n], refs[2 * n], refs[2 * n + 1], refs[4 * n + 3]
        x, y, c = _position()
        chip = 2 * x + y
        local = [pltpu.make_async_copy(ins[w].at[chip], lnd[w].at[chip], local_sem.at[w]) for w in range(n)]
        for cp in local:
            cp.start()
        for w in range(n):
            for j, other in enumerate(_other_chips(x, y)):
                cp = _rcopy(ins[w].at[2 * other[0] + other[1]], lnd[w].at[2 * other[0] + other[1]], send_sems.at[3 * w + j], recv_sems.at[3 * w + j], (*other, c))
                cp.wait_send()
                cp.wait_recv()
        for cp in local:
            cp.wait()

    out = pl.pallas_call(
        body, name=name, out_shape=tuple(_like(a) for a in sums + lands), in_specs=[_HBM] * (2 * n) + [_SEM, _SEM, _ANY],
        out_specs=(_HBM,) * (2 * n), input_output_aliases={i: i for i in range(2 * n)}, scratch_shapes=[_dma_sems(n)],
        compiler_params=_SIDE)(*sums, *lands, send, recv, after)
    return list(out[n:])


def ada_fwd(c, w_ada, b_ada3, conv_w):
    d, cs = w_ada.shape

    def body(c_ref, w_ref, b_ref, cw_ref, mod_ref, sc_ref, cwa_ref, part_ref, send_sems, recv_sems):
        me = _position()
        my = _index(me)
        cv = c_ref[...]
        sc_ref[my] = cv * _sigmoid(cv)
        cwa_ref[my] = cw_ref[...]
        gather = []
        for r in range(1, N_DEV):
            for k, ref in enumerate((sc_ref, cwa_ref)):
                cp = pltpu.make_async_remote_copy(src_ref=ref.at[my], dst_ref=ref.at[my], send_sem=send_sems.at[14 * k + r - 1],
                                                  recv_sem=recv_sems.at[14 * k + r - 1], device_id=_peer(me, r), device_id_type=MESH)
                cp.start()
                gather.append(cp)
        for cp in gather:
            cp.wait()
        sc_all = jnp.concatenate([sc_ref[k] for k in range(N_DEV)], axis=0).astype(BF16)
        part = jnp.dot(sc_all, w_ref[...].astype(BF16), preferred_element_type=F32)
        for k in range(N_DEV):
            part_ref[k] = part[k:k + 1, :]
        scatter = []
        for r in range(1, N_DEV):
            peer = _peer(me, r)
            cp = pltpu.make_async_remote_copy(src_ref=part_ref.at[_index(peer)], dst_ref=mod_ref.at[my], send_sem=send_sems.at[6 + r],
                                              recv_sem=recv_sems.at[6 + r], device_id=peer, device_id_type=MESH)
            cp.start()
            scatter.append(cp)
        mod_ref[my] = part_ref[my]
        for cp in scatter:
            cp.wait()
        mod_ref[...] = mod_ref[...] + b_ref[...]

    vm = pl.BlockSpec(memory_space=pltpu.VMEM)
    return pl.pallas_call(
        body, name="ada_fwd",
        out_shape=[jax.ShapeDtypeStruct((N_DEV, 1, cs), F32), jax.ShapeDtypeStruct((N_DEV, 1, d), F32),
                   jax.ShapeDtypeStruct((N_DEV,) + conv_w.shape, F32)],
        in_specs=[vm, vm, vm, vm], out_specs=[vm, vm, vm],
        scratch_shapes=[pltpu.VMEM((N_DEV, 1, cs), F32), pltpu.SemaphoreType.DMA((21,)), pltpu.SemaphoreType.DMA((21,))],
        compiler_params=pltpu.CompilerParams(vmem_limit_bytes=VMEM_LIMIT_BYTES))(c, w_ada, b_ada3, conv_w)


def ada_bwd_w(sc_all, dmod_cols):
    _, d = sc_all.shape
    cs = dmod_cols.shape[1]
    tr = _tile(d, ROW_TILE)

    def body(sc_ref, dm_ref, o_ref):
        dm = dm_ref[...].astype(BF16)
        o_ref[...] = lax.dot_general(sc_ref[...].astype(BF16), dm, (((0,), (0,)), ((), ())), preferred_element_type=F32)

    return pl.pallas_call(body, name="ada_bwd_w", grid=(d // tr,),
                          in_specs=[pl.BlockSpec((N_DEV, tr), lambda i: (0, i)), _full((N_DEV, cs))],
                          out_specs=pl.BlockSpec((None, tr, cs), lambda i: (0, i, 0)),
                          out_shape=jax.ShapeDtypeStruct((1, d, cs), F32), compiler_params=_params("parallel"))(sc_all, dmod_cols)


def _round_up(n, m):
    return (n + m - 1) // m * m


def kernel(x, c, positions, w_ada, b_ada, pre_norm1_g, w_in, gm_ln_g, gm_ln_b, gm_w_s, gm_b_s, w_branch_a, q_norm_g, w_uq, kv_norm_g, w_ukv, w_branch_b, w_out, post_norm1_g, pre_norm2_g, w_up, conv_w, conv_b, w_down, post_norm2_g, loss_target, m_w_ada, m_b_ada, m_pre_norm1_g, m_w_in, m_gm_ln_g, m_gm_ln_b, m_gm_w_s, m_gm_b_s, m_w_branch_a, m_q_norm_g, m_w_uq, m_kv_norm_g, m_w_ukv, m_w_branch_b, m_w_out, m_post_norm1_g, m_pre_norm2_g, m_w_up, m_conv_w, m_conv_b, m_w_down, m_post_norm2_g, v_w_ada, v_b_ada, v_pre_norm1_g, v_w_in, v_gm_ln_g, v_gm_ln_b, v_gm_w_s, v_gm_b_s, v_w_branch_a, v_q_norm_g, v_w_uq, v_kv_norm_g, v_w_ukv, v_w_branch_b, v_w_out, v_post_norm1_g, v_pre_norm2_g, v_w_up, v_conv_w, v_conv_b, v_w_down, v_post_norm2_g):
    weights = dict(w_ada=w_ada, b_ada=b_ada, pre_norm1_g=pre_norm1_g, w_in=w_in, gm_ln_g=gm_ln_g, gm_ln_b=gm_ln_b, gm_w_s=gm_w_s,
                   gm_b_s=gm_b_s, w_branch_a=w_branch_a, q_norm_g=q_norm_g, w_uq=w_uq, kv_norm_g=kv_norm_g, w_ukv=w_ukv,
                   w_branch_b=w_branch_b, w_out=w_out, post_norm1_g=post_norm1_g, pre_norm2_g=pre_norm2_g, w_up=w_up, conv_w=conv_w,
                   conv_b=conv_b, w_down=w_down, post_norm2_g=post_norm2_g)
    mom1 = dict(w_ada=m_w_ada, b_ada=m_b_ada, pre_norm1_g=m_pre_norm1_g, w_in=m_w_in, gm_ln_g=m_gm_ln_g, gm_ln_b=m_gm_ln_b,
                gm_w_s=m_gm_w_s, gm_b_s=m_gm_b_s, w_branch_a=m_w_branch_a, q_norm_g=m_q_norm_g, w_uq=m_w_uq, kv_norm_g=m_kv_norm_g,
                w_ukv=m_w_ukv, w_branch_b=m_w_branch_b, w_out=m_w_out, post_norm1_g=m_post_norm1_g, pre_norm2_g=m_pre_norm2_g,
                w_up=m_w_up, conv_w=m_conv_w, conv_b=m_conv_b, w_down=m_w_down, post_norm2_g=m_post_norm2_g)
    mom2 = dict(w_ada=v_w_ada, b_ada=v_b_ada, pre_norm1_g=v_pre_norm1_g, w_in=v_w_in, gm_ln_g=v_gm_ln_g, gm_ln_b=v_gm_ln_b,
                gm_w_s=v_gm_w_s, gm_b_s=v_gm_b_s, w_branch_a=v_w_branch_a, q_norm_g=v_q_norm_g, w_uq=v_w_uq, kv_norm_g=v_kv_norm_g,
                w_ukv=v_w_ukv, w_branch_b=v_w_branch_b, w_out=v_w_out, post_norm1_g=v_post_norm1_g, pre_norm2_g=v_pre_norm2_g,
                w_up=v_w_up, conv_w=v_conv_w, conv_b=v_conv_b, w_down=v_w_down, post_norm2_g=v_post_norm2_g)
    order = list(weights)
    _TOKENS.take()

    s, d = x.shape[1], x.shape[2]
    gmw = gm_ln_g.shape[0]
    groups = gmw // CHUNK
    ql, kvl = q_norm_g.shape[0], kv_norm_g.shape[0]
    f2 = conv_b.shape[0]
    in_cols = w_in.shape[1] * N_DEV
    o_q, o_kv, o_ga, o_gb, o_kpe = 2 * gmw, 2 * gmw + ql, 2 * gmw + ql + kvl, 2 * gmw + ql + kvl + d, 2 * gmw + ql + kvl + 2 * d
    zp = _round_up(o_kpe + LANES, Z_PAD)
    src_kpe = 2 * gmw + ql + kvl
    assert src_kpe + QK_ROPE + 2 * d == in_cols
    my = 4 * lax.axis_index("x") + 2 * lax.axis_index("y") + lax.axis_index("c")

    x2, tgt = x[0], loss_target[0]
    row = lambda a: a.reshape(1, -1)

    big = ["w_in", "w_branch_a", "w_uq", "w_ukv", "w_branch_b", "w_out", "w_up", "w_down"]
    sh = {k: weights[k].astype(BF16) for k in big}
    mix = ["w_branch_a", "w_uq", "w_ukv", "w_branch_b", "w_out"]
    mix_sizes = [sh[k].size for k in mix]
    mix_packed = jnp.concatenate([sh[k].reshape(-1) for k in mix]).reshape(-1, LANES)
    ag_in = ag_start("ag_start_in", sh["w_in"], c)

    mod8, sc_all3, g_cw = ada_fwd(c, w_ada, b_ada.reshape(N_DEV, 1, -1), conv_w)
    mod = mod8.reshape(N_MOD, d)
    shift1, scale1, gate1, shift2, scale2, gate2 = (mod[i:i + 1] for i in range(N_MOD))
    sc_all = sc_all3.reshape(N_DEV, d)
    h1 = norm_mod_fwd("pre1_fwd", x2, row(pre_norm1_g), scale1, shift1)

    g_in = ag_wait("ag_wait_in", ag_forward("ag_forward_in", ag_in, h1), h1)
    ag_mix = ag_start("ag_start_mix", mix_packed, g_in)
    w_in_f = g_in.transpose(1, 0, 2).reshape(d, in_cols)
    w_in_p = jnp.concatenate([w_in_f[:, :src_kpe], w_in_f[:, src_kpe + QK_ROPE:], w_in_f[:, src_kpe:src_kpe + QK_ROPE],
                              jnp.zeros((d, zp - in_cols), BF16)], axis=1)

    inv = ROPE_THETA ** (-jnp.arange(0, QK_ROPE, 2, dtype=F32) / QK_ROPE)
    ang = positions[0].astype(F32)[:, None] * inv
    cos4 = jnp.tile(jnp.cos(ang), (1, 4))
    sin4 = jnp.tile(jnp.concatenate([-jnp.sin(ang), jnp.sin(ang)], axis=1), (1, 2))

    wm = (gm_w_s * jnp.tril(jnp.ones((CHUNK, CHUNK), F32))).astype(BF16)
    bs3 = gm_b_s.reshape(groups, CHUNK, 1)
    ln_g, ln_b = row(gm_ln_g), row(gm_ln_b)

    z = mm_nn("z_proj", h1, w_in_p, F32)
    a = gmlp_fwd(z, gmw, ln_g, ln_b, wm, bs3)
    g_mix = ag_wait("ag_wait_mix", ag_forward("ag_forward_mix", ag_mix, a), a).reshape(N_DEV, -1)
    ag_up = ag_start("ag_start_up", sh["w_up"], g_mix)
    offs = [sum(mix_sizes[:i]) for i in range(len(mix) + 1)]
    g_a, g_uq, g_ukv, g_b, g_out = (g_mix[:, offs[i]:offs[i + 1]].reshape((N_DEV,) + sh[k].shape) for i, k in enumerate(mix))
    w_a_f, w_b_f, w_out_f = g_a.reshape(-1, d), g_b.reshape(-1, d), g_out.reshape(-1, d)
    w_uq_f = g_uq.transpose(1, 0, 2).reshape(ql, HEADS, QK_NOPE + QK_ROPE)
    w_uq_n = w_uq_f[:, :, :QK_NOPE].reshape(ql, HEADS * QK_NOPE)
    w_uq_r = w_uq_f[:, :, QK_NOPE:].reshape(ql, HEADS * QK_ROPE)
    y_a = mm_nn("branch_a", a, w_a_f, F32)
    qln = rms_fwd_cols("q_norm", z, o_q, ql, row(q_norm_g))
    kvn = rms_fwd_cols("kv_norm", z, o_kv, kvl, row(kv_norm_g))
    qn = mm_nn("q_nope", qln, w_uq_n, BF16)
    qp = mm_nn("q_rope", qln, w_uq_r, F32)
    kv = mm_nn_b3("kv_up", kvn, g_ukv, BF16)
    kpr = rope_k(z, o_kpe, cos4, sin4)
    o, qpr, lse = attn_fwd2(qn, qp, kv, kpr, cos4, sin4)
    y_b = mm_nn("branch_b", o, w_b_f, F32)
    merged = merge_fwd(z, o_ga, o_gb, y_a, y_b)
    y1 = mm_nn("out_proj", merged, w_out_f, F32)
    x1 = post_res_fwd("post1_fwd", x2, y1, gate1, row(post_norm1_g))
    h2 = norm_mod_fwd("pre2_fwd", x1, row(pre_norm2_g), scale2, shift2)
    g_up = ag_wait("ag_wait_up", ag_forward("ag_forward_up", ag_up, h2), h2)
    ag_down = ag_start("ag_start_down", sh["w_down"], g_up)
    upre = mm_nn_b3("up_proj", h2, g_up, F32)
    cw = g_cw.transpose(1, 0, 2).reshape(3, f2)
    cb = row(conv_b)
    f = conv_act_fwd(upre, cw, cb)
    w_down_f = ag_wait("ag_wait_down", ag_forward("ag_forward_down", ag_down, f), f).reshape(-1, d)
    ffn = mm_nn("down_proj", f, w_down_f, F32)
    loss_acc, dout, dffn, acc2 = post2_loss_bwd(x1, ffn, tgt, gate2, row(post_norm2_g))

    blocks = lambda g: g.reshape(N_DEV, g.shape[0] // N_DEV, g.shape[1])
    core = lax.axis_index("c").astype(jnp.int32).reshape(1)
    rs = {}

    def rs_begin(key, grads):
        rs[key] = rs_d2d_start("rs_d2d_start_" + key, grads)

    def rs_middle(key, after):
        grads, lands = rs_d2d_wait("rs_d2d_wait_" + key, rs[key], after)
        sums = [pair_sum("pair_sum_%s_%d" % (key, i), g, l, core) for i, (g, l) in enumerate(zip(grads, lands))]
        rs[key] = rs_ici_start("rs_ici_start_" + key, sums)

    gw_down = mm_tn("g_w_down", f, dffn, BF16)
    rs_begin("down", [blocks(gw_down)])
    df = mm_nt("d_f", dffn, w_down_f, F32)
    rs_middle("down", df)
    dup_g, dup_v, gcw_g, gcw_v, gcb_g, gcb_v = conv_act_bwd(upre, cw, cb, df)
    dupre = jnp.concatenate([dup_g, dup_v], axis=1)
    gw_up3 = mm_tn_o3("g_w_up", h2, dupre, N_DEV, BF16)
    rs_begin("up", [gw_up3])
    dh2 = mm_nt_b3("d_h2", dupre, g_up, F32)
    rs_middle("up", dh2)
    dx1, dy1, acc_mid = mid_bwd(dh2, dout, x1, y1, row(pre_norm2_g), scale2, gate1, row(post_norm1_g))
    gw_out = mm_tn("g_w_out", merged, dy1, BF16)
    dmerged = mm_nt("d_merged", dy1, w_out_f, F32)
    dya, dyb, dga, dgb = merge_bwd(z, o_ga, o_gb, y_a, y_b, dmerged)
    gw_a = mm_tn("g_w_a", a, dya, BF16)
    gw_b = mm_tn("g_w_b", o, dyb, BF16)
    rs_begin("mid", [jnp.concatenate([blocks(gw_out), blocks(gw_a), blocks(gw_b)], axis=1)])
    da = mm_nt("d_a", dya, w_a_f, F32)
    do = mm_nt("d_o", dyb, w_b_f, F32)
    rs_middle("mid", do)
    duv, g_ws, g_bs3, acc_gm = gmlp_bwd(z, gmw, da, ln_g, ln_b, wm, bs3)
    dqn, dqp, dkv, dkp = attn_bwd2(qn, qpr, kv, kpr, o, do, lse, cos4, sin4)
    dkpe = kpe_bwd(dkp, cos4, sin4, zp - o_kpe)
    dq_cat = jnp.concatenate([dqn, dqp], axis=1)
    w_uq_cat = jnp.concatenate([w_uq_n, w_uq_r], axis=1)
    gw_uq_cat = mm_tn("g_w_uq", qln, dq_cat, BF16)
    gw_uq_f = jnp.concatenate([gw_uq_cat[:, :HEADS * QK_NOPE].reshape(ql, HEADS, QK_NOPE),
                               gw_uq_cat[:, HEADS * QK_NOPE:].reshape(ql, HEADS, QK_ROPE)], axis=2)
    gw_uq3 = gw_uq_f.reshape(ql, N_DEV, -1).transpose(1, 0, 2)
    gw_ukv3 = mm_tn_o3("g_w_ukv", kvn, dkv, N_DEV, BF16)
    rs_begin("mla", [gw_uq3, gw_ukv3])
    dqln = mm_nt("d_qln", dq_cat, w_uq_cat, F32)
    dq_lat, g_qnorm = rms_bwd_cols("q_norm_bwd", dqln, z, o_q, ql, row(q_norm_g))
    dkvn = mm_nt_b3("d_kvn", dkv, g_ukv, F32)
    rs_middle("mla", dkvn)
    dkv_lat, g_kvnorm = rms_bwd_cols("kv_norm_bwd", dkvn, z, o_kv, kvl, row(kv_norm_g))
    dz = jnp.concatenate([duv, dq_lat, dkv_lat, dga, dgb, dkpe], axis=1)
    gw_in_p = mm_tn("g_w_in", h1, dz, BF16)
    gw_in_f = jnp.concatenate([gw_in_p[:, :src_kpe], gw_in_p[:, o_kpe:o_kpe + QK_ROPE], gw_in_p[:, src_kpe:o_kpe]], axis=1)
    gw_in3 = gw_in_f.reshape(d, N_DEV, -1).transpose(1, 0, 2)
    rs_begin("in", [gw_in3])
    dh1 = mm_nt("d_h1", dz, w_in_p, F32)
    grad_x, acc1 = pre1_bwd(dh1, dx1, x2, row(pre_norm1_g), scale1)

    dmod = jnp.concatenate([acc1[0], acc1[1], acc_mid[3], acc_mid[0], acc_mid[1], acc2[0]])
    small = [("pre_norm1_g", acc1[2]), ("gm_ln_g", acc_gm[0]), ("gm_ln_b", acc_gm[1]), ("gm_b_s", g_bs3.reshape(-1)),
             ("q_norm_g", g_qnorm[0]), ("kv_norm_g", g_kvnorm[0]), ("post_norm1_g", acc_mid[4]), ("pre_norm2_g", acc_mid[2]),
             ("conv_b", jnp.concatenate([gcb_g[0], gcb_v[0]])), ("post_norm2_g", acc2[1]), ("gm_w_s", g_ws.reshape(-1)),
             ("b_ada", dmod)]
    n_small = sum(v.shape[0] for _, v in small)
    n_cw = 3 * f2
    n_pack = _round_up(n_small + n_cw, PACK_ALIGN)
    tail = jnp.zeros((n_pack - n_small - n_cw,), F32)
    packed = jnp.concatenate([v for _, v in small] + [jnp.concatenate([gcw_g, gcw_v], axis=1).reshape(-1), tail])
    ag_small = ag_start("ag_start_small", packed.reshape(-1, LANES), packed)
    rs_middle("in", packed)

    res = {}
    last = packed
    for key, names in (("down", ["w_down"]), ("up", ["w_up"]), ("mid", ["w_out", "w_branch_a", "w_branch_b"]), ("mla", ["w_uq", "w_ukv"])):
        parts = rs_ici_wait("rs_ici_wait_" + key, rs[key], last)
        for i, k in enumerate(names):
            packed_rows = key == "mid"
            res[k] = adamw("adamw_" + k, weights[k], mom1[k], mom2[k], parts[0 if packed_rows else i],
                           row_off=sum(weights[n].shape[0] for n in names[:i]) if packed_rows else 0)
            last = res[k][0]

    def pack(src):
        return jnp.concatenate([src[k].reshape(-1) for k, _ in small] + [jnp.zeros((n_pack - n_small,), F32)]).reshape(-1, LANES)

    gathered = ag_wait("ag_wait_small", ag_forward("ag_forward_small", ag_small, last), last)
    sm = [t.reshape(-1) for t in adamw("adamw_small", pack(weights), pack(mom1), pack(mom2), gathered)]
    off = 0
    for k, v in small:
        res[k] = tuple(t[off:off + v.shape[0]].reshape(weights[k].shape) for t in sm)
        off += v.shape[0]

    cs_cw = conv_w.shape[1]
    g_cw_full = sm[0][n_small:n_small + n_cw].reshape(3, f2)
    g_cw_mine = lax.dynamic_slice(g_cw_full, (0, my * cs_cw), (3, cs_cw))
    res["conv_w"] = adamw("adamw_conv_w", conv_w, mom1["conv_w"], mom2["conv_w"], g_cw_mine[None])

    cs_ada = w_ada.shape[1]
    off_b = n_small - N_MOD * d
    dmod_all = gathered.reshape(N_DEV, -1)[:, off_b:off_b + N_MOD * d]
    dmod_cols = lax.dynamic_slice(dmod_all, (0, my * cs_ada), (N_DEV, cs_ada))
    res["w_ada"] = adamw("adamw_w_ada", w_ada, mom1["w_ada"], mom2["w_ada"], ada_bwd_w(sc_all, dmod_cols))

    (p_in,) = rs_ici_wait("rs_ici_wait_in", rs["in"], res["w_ada"][0])
    res["w_in"] = adamw("adamw_w_in", w_in, mom1["w_in"], mom2["w_in"], p_in)

    _TOKENS.take()
    loss = lax.psum(loss_acc[0, 0], ("x", "y", "c"))
    outs = [loss, grad_x[None]]
    for i in range(4):
        outs += [res[k][i] for k in order]
    return tuple(outs)
```

```python
import functools

import jax
import jax.numpy as jnp
from jax import lax
from jax.experimental import pallas as pl
from jax.experimental.pallas import tpu as pltpu

F32 = jnp.float32
BF16 = jnp.bfloat16

N_DEV = 8
HEADS = 16
QK_NOPE = 128
QK_ROPE = 64
V_HEAD = 128
CHUNK = 128
ROPE_THETA = 10000.0
EPS = 1e-6
N_MOD = 6
ADAM_LR, ADAM_B1, ADAM_B2, ADAM_EPS, ADAM_WD, ADAM_STEP = 0.001, 0.9, 0.999, 1e-08, 0.01, 10

LANES = 128
VMEM_LIMIT_BYTES = 48 * 2 ** 20
ROW_TILE = 256
COL_TILE = 256
ATT_TILE = 256
Z_PAD = 512
ADAMW_TILE_ELEMS = 1 << 18
PACK_ALIGN = 8 * LANES
MESH = pl.DeviceIdType.MESH


def _params(*sem):
    return pltpu.CompilerParams(dimension_semantics=sem if sem else None, vmem_limit_bytes=VMEM_LIMIT_BYTES)


def _tile(dim, target):
    t = (min(dim, target) // LANES) * LANES
    while t >= LANES:
        if dim % t == 0:
            return t
        t -= LANES
    return dim


def _full(shape):
    nd = len(shape)
    return pl.BlockSpec(shape, lambda *_: (0,) * nd)


class _Tokens:
    def __init__(self):
        self.pending = []

    def push(self, token):
        self.pending.append(token)

    def take(self):
        out, self.pending = self.pending, []
        return out


_TOKENS = _Tokens()


def _matmul(name, a, b, *, grid, a_spec, b_spec, o_spec, out_shape, contract, acc_shape):
    nk = grid[2]
    deps = _TOKENS.take()

    def body(a_ref, b_ref, *rest):
        o_ref, acc_ref = rest[len(deps):]
        k = pl.program_id(2)

        @pl.when(k == 0)
        def _():
            acc_ref[...] = jnp.zeros_like(acc_ref)

        acc_ref[...] += lax.dot_general(a_ref[...].astype(BF16), b_ref[...].astype(BF16),
                                        (contract, ((), ())), preferred_element_type=F32)

        @pl.when(k == nk - 1)
        def _():
            o_ref[...] = acc_ref[...].astype(o_ref.dtype)

    return pl.pallas_call(
        body, name=name, grid=grid, in_specs=[a_spec, b_spec] + [pl.BlockSpec(memory_space=pl.ANY)] * len(deps),
        out_specs=o_spec, out_shape=out_shape, scratch_shapes=[pltpu.VMEM(acc_shape, F32)],
        compiler_params=_params("parallel", "parallel", "arbitrary"))(a, b, *deps)


TM, TN, TK = 1024, 1024, 512


def mm_nn(name, a, b, dtype):
    (m, k), n = a.shape, b.shape[1]
    tm, tn, tk = _tile(m, TM), _tile(n, TN), _tile(k, TK)
    return _matmul(name, a, b, grid=(m // tm, n // tn, k // tk),
                   a_spec=pl.BlockSpec((tm, tk), lambda i, j, kk: (i, kk)),
                   b_spec=pl.BlockSpec((tk, tn), lambda i, j, kk: (kk, j)),
                   o_spec=pl.BlockSpec((tm, tn), lambda i, j, kk: (i, j)),
                   out_shape=jax.ShapeDtypeStruct((m, n), dtype), contract=((1,), (0,)), acc_shape=(tm, tn))


def mm_nn_b3(name, a, b3, dtype):
    (m, k), (nj, _, cs) = a.shape, b3.shape
    tm, tk = _tile(m, TM), _tile(k, TK)
    return _matmul(name, a, b3, grid=(m // tm, nj, k // tk),
                   a_spec=pl.BlockSpec((tm, tk), lambda i, j, kk: (i, kk)),
                   b_spec=pl.BlockSpec((None, tk, cs), lambda i, j, kk: (j, kk, 0)),
                   o_spec=pl.BlockSpec((tm, cs), lambda i, j, kk: (i, j)),
                   out_shape=jax.ShapeDtypeStruct((m, nj * cs), dtype), contract=((1,), (0,)), acc_shape=(tm, cs))


def mm_nt(name, a, b, dtype):
    (m, k), n = a.shape, b.shape[0]
    tm, tn, tk = _tile(m, TM), _tile(n, TN), _tile(k, TK)
    return _matmul(name, a, b, grid=(m // tm, n // tn, k // tk),
                   a_spec=pl.BlockSpec((tm, tk), lambda i, j, kk: (i, kk)),
                   b_spec=pl.BlockSpec((tn, tk), lambda i, j, kk: (j, kk)),
                   o_spec=pl.BlockSpec((tm, tn), lambda i, j, kk: (i, j)),
                   out_shape=jax.ShapeDtypeStruct((m, n), dtype), contract=((1,), (1,)), acc_shape=(tm, tn))


def mm_nt_b3(name, a, b3, dtype):
    m, (nj, n, cs) = a.shape[0], b3.shape
    tm, tn = _tile(m, TM), _tile(n, TN)
    return _matmul(name, a, b3, grid=(m // tm, n // tn, nj),
                   a_spec=pl.BlockSpec((tm, cs), lambda i, j, kk: (i, kk)),
                   b_spec=pl.BlockSpec((None, tn, cs), lambda i, j, kk: (kk, j, 0)),
                   o_spec=pl.BlockSpec((tm, tn), lambda i, j, kk: (i, j)),
                   out_shape=jax.ShapeDtypeStruct((m, n), dtype), contract=((1,), (1,)), acc_shape=(tm, tn))


def mm_tn(name, a, b, dtype):
    (k, m), n = a.shape, b.shape[1]
    tm, tn, tk = _tile(m, TM), _tile(n, TN), _tile(k, TK)
    return _matmul(name, a, b, grid=(m // tm, n // tn, k // tk),
                   a_spec=pl.BlockSpec((tk, tm), lambda i, j, kk: (kk, i)),
                   b_spec=pl.BlockSpec((tk, tn), lambda i, j, kk: (kk, j)),
                   o_spec=pl.BlockSpec((tm, tn), lambda i, j, kk: (i, j)),
                   out_shape=jax.ShapeDtypeStruct((m, n), dtype), contract=((0,), (0,)), acc_shape=(tm, tn))


def mm_tn_o3(name, a, b, nj, dtype):
    (k, m), n = a.shape, b.shape[1]
    cs = n // nj
    tm, tk = _tile(m, TM), _tile(k, TK)
    return _matmul(name, a, b, grid=(m // tm, nj, k // tk),
                   a_spec=pl.BlockSpec((tk, tm), lambda i, j, kk: (kk, i)),
                   b_spec=pl.BlockSpec((tk, cs), lambda i, j, kk: (kk, j)),
                   o_spec=pl.BlockSpec((None, tm, cs), lambda i, j, kk: (j, i, 0)),
                   out_shape=jax.ShapeDtypeStruct((nj, m, cs), dtype), contract=((0,), (0,)), acc_shape=(tm, cs))


_GELU_C = 0.7978845608028654
_GELU_A = 0.044715


def _gelu(x):
    return 0.5 * x * (1.0 + jnp.tanh(_GELU_C * (x + _GELU_A * x * x * x)))


def _gelu_and_grad(x):
    t = jnp.tanh(_GELU_C * (x + _GELU_A * x * x * x))
    y = 0.5 * x * (1.0 + t)
    dy = 0.5 * (1.0 + t) + 0.5 * x * (1.0 - t * t) * (_GELU_C * (1.0 + 3.0 * _GELU_A * x * x))
    return y, dy


def _sigmoid(x):
    return 1.0 / (1.0 + jnp.exp(-x))


def _rms_stats(x):
    inv = lax.rsqrt(jnp.mean(x * x, axis=-1, keepdims=True) + EPS)
    return inv, x * inv


def _rms_bwd(dyhat, yhat, inv):
    return inv * (dyhat - yhat * jnp.mean(dyhat * yhat, axis=-1, keepdims=True))


def _colsum(x):
    return jnp.sum(x, axis=0, keepdims=True)


def _rope(x, cos4, sin4):
    lane = lax.broadcasted_iota(jnp.int32, x.shape, x.ndim - 1)
    first_half = (lane % QK_ROPE) < (QK_ROPE // 2)
    partner = jnp.where(first_half, pltpu.roll(x, LANES - QK_ROPE // 2, x.ndim - 1), pltpu.roll(x, QK_ROPE // 2, x.ndim - 1))
    return x * cos4 + partner * sin4


def norm_mod_fwd(name, x, g, scale, shift):
    s, d = x.shape
    tr = _tile(s, ROW_TILE)

    def body(x_ref, g_ref, sc_ref, sh_ref, o_ref):
        _, xh = _rms_stats(x_ref[...])
        o_ref[...] = (xh * g_ref[...] * (1.0 + sc_ref[...]) + sh_ref[...]).astype(o_ref.dtype)

    row = pl.BlockSpec((tr, d), lambda i: (i, 0))
    vec = pl.BlockSpec((1, d), lambda i: (0, 0))
    return pl.pallas_call(body, name=name, grid=(s // tr,), in_specs=[row, vec, vec, vec], out_specs=row,
                          out_shape=jax.ShapeDtypeStruct((s, d), BF16), compiler_params=_params("parallel"))(x, g, scale, shift)


def rms_fwd_cols(name, z, off, width, g):
    s = z.shape[0]
    tr = _tile(s, ROW_TILE)
    assert off % width == 0

    def body(x_ref, g_ref, o_ref):
        _, xh = _rms_stats(x_ref[...])
        o_ref[...] = (xh * g_ref[...]).astype(o_ref.dtype)

    return pl.pallas_call(body, name=name, grid=(s // tr,),
                          in_specs=[pl.BlockSpec((tr, width), lambda i: (i, off // width)), pl.BlockSpec((1, width), lambda i: (0, 0))],
                          out_specs=pl.BlockSpec((tr, width), lambda i: (i, 0)),
                          out_shape=jax.ShapeDtypeStruct((s, width), BF16), compiler_params=_params("parallel"))(z, g)


def rms_bwd_cols(name, dy, z, off, width, g):
    s = z.shape[0]
    tr = _tile(s, ROW_TILE)

    def body(dy_ref, x_ref, g_ref, dx_ref, gg_ref):
        @pl.when(pl.program_id(0) == 0)
        def _():
            gg_ref[...] = jnp.zeros_like(gg_ref)

        inv, xh = _rms_stats(x_ref[...])
        dy_v = dy_ref[...]
        gg_ref[...] += _colsum(dy_v * xh)
        dx_ref[...] = _rms_bwd(dy_v * g_ref[...], xh, inv).astype(dx_ref.dtype)

    return pl.pallas_call(body, name=name, grid=(s // tr,),
                          in_specs=[pl.BlockSpec((tr, width), lambda i: (i, 0)), pl.BlockSpec((tr, width), lambda i: (i, off // width)),
                                    pl.BlockSpec((1, width), lambda i: (0, 0))],
                          out_specs=[pl.BlockSpec((tr, width), lambda i: (i, 0)), pl.BlockSpec((1, width), lambda i: (0, 0))],
                          out_shape=[jax.ShapeDtypeStruct((s, width), BF16), jax.ShapeDtypeStruct((1, width), F32)],
                          compiler_params=_params("arbitrary"))(dy, z, g)


def post_res_fwd(name, x, y, gate, g):
    s, d = x.shape
    tr = _tile(s, ROW_TILE)

    def body(x_ref, y_ref, gate_ref, g_ref, o_ref):
        _, yh = _rms_stats(y_ref[...])
        o_ref[...] = x_ref[...] + gate_ref[...] * (yh * g_ref[...])

    row = pl.BlockSpec((tr, d), lambda i: (i, 0))
    vec = pl.BlockSpec((1, d), lambda i: (0, 0))
    return pl.pallas_call(body, name=name, grid=(s // tr,), in_specs=[row, row, vec, vec], out_specs=row,
                          out_shape=jax.ShapeDtypeStruct((s, d), F32), compiler_params=_params("parallel"))(x, y, gate, g)


def post2_loss_bwd(x1, ffn, target, gate2, g):
    s, d = x1.shape
    tr = _tile(s, ROW_TILE)

    def body(x_ref, y_ref, t_ref, gate_ref, g_ref, loss_ref, dout_ref, dy_ref, acc_ref):
        @pl.when(pl.program_id(0) == 0)
        def _():
            loss_ref[...] = jnp.zeros_like(loss_ref)
            acc_ref[...] = jnp.zeros_like(acc_ref)

        inv, yh = _rms_stats(y_ref[...])
        r = yh * g_ref[...]
        err = x_ref[...] + gate_ref[...] * r - t_ref[...]
        loss_ref[...] += 0.5 * jnp.sum(jnp.mean(err * err, axis=-1, keepdims=True))
        dout = err / d
        dout_ref[...] = dout
        dr = dout * gate_ref[...]
        acc_ref[0:1, :] += _colsum(dout * r)
        acc_ref[1:2, :] += _colsum(dr * yh)
        dy_ref[...] = _rms_bwd(dr * g_ref[...], yh, inv).astype(dy_ref.dtype)

    row = pl.BlockSpec((tr, d), lambda i: (i, 0))
    vec = pl.BlockSpec((1, d), lambda i: (0, 0))
    return pl.pallas_call(
        body, name="post2_loss_bwd", grid=(s // tr,), in_specs=[row, row, row, vec, vec],
        out_specs=[_full((8, LANES)), row, row, _full((8, d))],
        out_shape=[jax.ShapeDtypeStruct((8, LANES), F32), jax.ShapeDtypeStruct((s, d), F32),
                   jax.ShapeDtypeStruct((s, d), BF16), jax.ShapeDtypeStruct((8, d), F32)],
        compiler_params=_params("arbitrary"))(x1, ffn, target, gate2, g)


def mid_bwd(dh2, dout, x1, y1, pre2_g, scale2, gate1, post1_g):
    s, d = x1.shape
    tr = _tile(s, ROW_TILE)

    def body(dh_ref, dout_ref, x_ref, y_ref, g2_ref, sc_ref, gate_ref, g1_ref, dx_ref, dy_ref, acc_ref):
        @pl.when(pl.program_id(0) == 0)
        def _():
            acc_ref[...] = jnp.zeros_like(acc_ref)

        dh = dh_ref[...]
        inv2, xh = _rms_stats(x_ref[...])
        acc_ref[0:1, :] += _colsum(dh)
        acc_ref[1:2, :] += _colsum(dh * (xh * g2_ref[...]))
        t = dh * (1.0 + sc_ref[...])
        acc_ref[2:3, :] += _colsum(t * xh)
        dx1 = dout_ref[...] + _rms_bwd(t * g2_ref[...], xh, inv2)
        dx_ref[...] = dx1
        inv1, yh = _rms_stats(y_ref[...])
        acc_ref[3:4, :] += _colsum(dx1 * (yh * g1_ref[...]))
        dr = dx1 * gate_ref[...]
        acc_ref[4:5, :] += _colsum(dr * yh)
        dy_ref[...] = _rms_bwd(dr * g1_ref[...], yh, inv1).astype(dy_ref.dtype)

    row = pl.BlockSpec((tr, d), lambda i: (i, 0))
    vec = pl.BlockSpec((1, d), lambda i: (0, 0))
    return pl.pallas_call(
        body, name="mid_bwd", grid=(s // tr,), in_specs=[row, row, row, row, vec, vec, vec, vec],
        out_specs=[row, row, _full((8, d))],
        out_shape=[jax.ShapeDtypeStruct((s, d), F32), jax.ShapeDtypeStruct((s, d), BF16), jax.ShapeDtypeStruct((8, d), F32)],
        compiler_params=_params("arbitrary"))(dh2, dout, x1, y1, pre2_g, scale2, gate1, post1_g)


def pre1_bwd(dh1, dx1, x, pre1_g, scale1):
    s, d = x.shape
    tr = _tile(s, ROW_TILE)

    def body(dh_ref, dx1_ref, x_ref, g_ref, sc_ref, dx_ref, acc_ref):
        @pl.when(pl.program_id(0) == 0)
        def _():
            acc_ref[...] = jnp.zeros_like(acc_ref)

        dh = dh_ref[...]
        inv, xh = _rms_stats(x_ref[...])
        acc_ref[0:1, :] += _colsum(dh)
        acc_ref[1:2, :] += _colsum(dh * (xh * g_ref[...]))
        t = dh * (1.0 + sc_ref[...])
        acc_ref[2:3, :] += _colsum(t * xh)
        dx_ref[...] = dx1_ref[...] + _rms_bwd(t * g_ref[...], xh, inv)

    row = pl.BlockSpec((tr, d), lambda i: (i, 0))
    vec = pl.BlockSpec((1, d), lambda i: (0, 0))
    return pl.pallas_call(
        body, name="pre1_bwd", grid=(s // tr,), in_specs=[row, row, row, vec, vec], out_specs=[row, _full((8, d))],
        out_shape=[jax.ShapeDtypeStruct((s, d), F32), jax.ShapeDtypeStruct((8, d), F32)],
        compiler_params=_params("arbitrary"))(dh1, dx1, x, pre1_g, scale1)


def _ln_stats(v):
    mu = jnp.mean(v, axis=-1, keepdims=True)
    vc = v - mu
    rstd = lax.rsqrt(jnp.mean(vc * vc, axis=-1, keepdims=True) + EPS)
    return rstd, vc * rstd


def gmlp_fwd(z, width, ln_g, ln_b, wm, bs3):
    s = z.shape[0]
    groups = width // CHUNK

    def body(u_ref, v_ref, g_ref, b_ref, wm_ref, bs_ref, a_ref):
        ug = _gelu(u_ref[...])
        _, vh = _ln_stats(_gelu(v_ref[...]))
        vn = (vh * g_ref[...] + b_ref[...]).astype(BF16)
        for g in range(groups):
            cols = slice(g * CHUNK, (g + 1) * CHUNK)
            mixed = jnp.dot(wm_ref[g], vn[:, cols], preferred_element_type=F32) + bs_ref[g]
            a_ref[:, cols] = (ug[:, cols] * mixed).astype(a_ref.dtype)

    vec = pl.BlockSpec((1, width), lambda n: (0, 0))
    return pl.pallas_call(
        body, name="gmlp_fwd", grid=(s // CHUNK,),
        in_specs=[pl.BlockSpec((CHUNK, width), lambda n: (n, 0)), pl.BlockSpec((CHUNK, width), lambda n: (n, 1)), vec, vec,
                  _full(wm.shape), _full(bs3.shape)],
        out_specs=pl.BlockSpec((CHUNK, width), lambda n: (n, 0)),
        out_shape=jax.ShapeDtypeStruct((s, width), BF16), compiler_params=_params("parallel"))(z, z, ln_g, ln_b, wm, bs3)


def gmlp_bwd(z, width, da, ln_g, ln_b, wm, bs3):
    s = z.shape[0]
    groups = width // CHUNK

    def body(u_ref, v_ref, da_ref, g_ref, b_ref, wm_ref, bs_ref, duv_ref, gw_ref, gb_ref, acc_ref, dvn_ref):
        @pl.when(pl.program_id(0) == 0)
        def _():
            gw_ref[...] = jnp.zeros_like(gw_ref)
            gb_ref[...] = jnp.zeros_like(gb_ref)
            acc_ref[...] = jnp.zeros_like(acc_ref)

        ug, dug = _gelu_and_grad(u_ref[...])
        vg, dvg = _gelu_and_grad(v_ref[...])
        rstd, vh = _ln_stats(vg)
        vn = (vh * g_ref[...] + b_ref[...]).astype(BF16)
        da_v = da_ref[...]
        for g in range(groups):
            cols = slice(g * CHUNK, (g + 1) * CHUNK)
            mixed = jnp.dot(wm_ref[g], vn[:, cols], preferred_element_type=F32) + bs_ref[g]
            duv_ref[:, cols] = (da_v[:, cols] * mixed * dug[:, cols]).astype(duv_ref.dtype)
            dm = da_v[:, cols] * ug[:, cols]
            gb_ref[g] += jnp.sum(dm, axis=-1, keepdims=True)
            dmb = dm.astype(BF16)
            gw_ref[g] += lax.dot_general(dmb, vn[:, cols], (((1,), (1,)), ((), ())), preferred_element_type=F32)
            dvn_ref[:, cols] = lax.dot_general(wm_ref[g], dmb, (((0,), (0,)), ((), ())), preferred_element_type=F32)
        dvn = dvn_ref[...]
        acc_ref[0:1, :] += _colsum(dvn * vh)
        acc_ref[1:2, :] += _colsum(dvn)
        dvh = dvn * g_ref[...]
        dv = rstd * (dvh - jnp.mean(dvh, axis=-1, keepdims=True) - vh * jnp.mean(dvh * vh, axis=-1, keepdims=True))
        duv_ref[:, width:] = (dv * dvg).astype(duv_ref.dtype)

        @pl.when(pl.program_id(0) == pl.num_programs(0) - 1)
        def _():
            q = lax.broadcasted_iota(jnp.int32, gw_ref.shape, 1)
            p = lax.broadcasted_iota(jnp.int32, gw_ref.shape, 2)
            gw_ref[...] = jnp.where(p <= q, gw_ref[...], 0.0)

    vec = pl.BlockSpec((1, width), lambda n: (0, 0))
    blk = pl.BlockSpec((CHUNK, width), lambda n: (n, 0))
    return pl.pallas_call(
        body, name="gmlp_bwd", grid=(s // CHUNK,),
        in_specs=[blk, pl.BlockSpec((CHUNK, width), lambda n: (n, 1)), blk, vec, vec, _full(wm.shape), _full(bs3.shape)],
        out_specs=[pl.BlockSpec((CHUNK, 2 * width), lambda n: (n, 0)), _full(wm.shape), _full(bs3.shape), _full((8, width))],
        out_shape=[jax.ShapeDtypeStruct((s, 2 * width), BF16), jax.ShapeDtypeStruct(wm.shape, F32),
                   jax.ShapeDtypeStruct(bs3.shape, F32), jax.ShapeDtypeStruct((8, width), F32)],
        scratch_shapes=[pltpu.VMEM((CHUNK, width), F32)],
        compiler_params=_params("arbitrary"))(z, z, da, ln_g, ln_b, wm, bs3)


def merge_fwd(z, off_a, off_b, ya, yb):
    s, d = ya.shape
    tr, tc = _tile(s, ROW_TILE * 2), _tile(d, COL_TILE)
    assert off_a % tc == 0 and off_b % tc == 0

    def body(ga_ref, gb_ref, ya_ref, yb_ref, o_ref):
        o_ref[...] = (_sigmoid(ga_ref[...]) * ya_ref[...] + _sigmoid(gb_ref[...]) * yb_ref[...]).astype(o_ref.dtype)

    blk = pl.BlockSpec((tr, tc), lambda i, j: (i, j))
    return pl.pallas_call(
        body, name="merge_fwd", grid=(s // tr, d // tc),
        in_specs=[pl.BlockSpec((tr, tc), lambda i, j: (i, off_a // tc + j)), pl.BlockSpec((tr, tc), lambda i, j: (i, off_b // tc + j)), blk, blk],
        out_specs=blk, out_shape=jax.ShapeDtypeStruct((s, d), BF16), compiler_params=_params("parallel", "parallel"))(z, z, ya, yb)


def merge_bwd(z, off_a, off_b, ya, yb, dm):
    s, d = ya.shape
    tr, tc = _tile(s, ROW_TILE * 2), _tile(d, COL_TILE)
    nc = d // tc

    def body(ga_ref, gb_ref, ya_ref, yb_ref, dm_ref, dya_ref, dyb_ref, dga_ref, dgb_ref):
        dm_v = dm_ref[...]
        sa, sb = _sigmoid(ga_ref[...]), _sigmoid(gb_ref[...])
        dya_ref[...] = (dm_v * sa).astype(dya_ref.dtype)
        dyb_ref[...] = (dm_v * sb).astype(dyb_ref.dtype)
        dga_ref[...] = (dm_v * ya_ref[...] * sa * (1.0 - sa)).astype(dga_ref.dtype)
        dgb_ref[...] = (dm_v * yb_ref[...] * sb * (1.0 - sb)).astype(dgb_ref.dtype)

    blk = pl.BlockSpec((tr, tc), lambda i, j: (i, j))
    out = jax.ShapeDtypeStruct((s, d), BF16)
    return pl.pallas_call(
        body, name="merge_bwd", grid=(s // tr, nc),
        in_specs=[pl.BlockSpec((tr, tc), lambda i, j: (i, off_a // tc + j)), pl.BlockSpec((tr, tc), lambda i, j: (i, off_b // tc + j)), blk, blk, blk],
        out_specs=[blk, blk, blk, blk], out_shape=[out, out, out, out],
        compiler_params=_params("parallel", "parallel"))(z, z, ya, yb, dm)


_ATT_SCALE = (QK_NOPE + QK_ROPE) ** -0.5
_NEG = -1e30


def rope_k(z, off, cos4, sin4):
    s = z.shape[0]
    tr = _tile(s, ROW_TILE * 2)
    assert off % LANES == 0

    def body(k_ref, c_ref, s_ref, o_ref):
        k = k_ref[...]
        k = k + pltpu.roll(k, QK_ROPE, 1)
        o_ref[...] = _rope(k, c_ref[...], s_ref[...]).astype(o_ref.dtype)

    row = pl.BlockSpec((tr, LANES), lambda i: (i, 0))
    return pl.pallas_call(body, name="rope_k", grid=(s // tr,),
                          in_specs=[pl.BlockSpec((tr, LANES), lambda i: (i, off // LANES)), row, row], out_specs=row,
                          out_shape=jax.ShapeDtypeStruct((s, LANES), BF16), compiler_params=_params("parallel"))(z, cos4, sin4)


def _head_masks(shape):
    lane = lax.broadcasted_iota(jnp.int32, shape, 1)
    return lane < QK_ROPE, lane >= QK_ROPE


def _scores(qn, qp_h, k, kp, qi, kb, t):
    sc = lax.dot_general(qn, k, (((1,), (1,)), ((), ())), preferred_element_type=F32)
    sc += lax.dot_general(qp_h, kp, (((1,), (1,)), ((), ())), preferred_element_type=F32)
    sc = sc * _ATT_SCALE
    row = lax.broadcasted_iota(jnp.int32, sc.shape, 0) + qi * t
    col = lax.broadcasted_iota(jnp.int32, sc.shape, 1) + kb * t
    return jnp.where(col <= row, sc, _NEG)


def attn_fwd(qn, qp, kv, kpr, cos4, sin4):
    s = qn.shape[0]
    hp = HEADS // 2
    t = _tile(s, ATT_TILE)
    nq = s // t

    def body(qn_ref, qp_ref, kv_ref, kp_ref, c_ref, s_ref, o_ref, qpr_ref, l_ref):
        qi = pl.program_id(1)
        qpr = _rope(qp_ref[...], c_ref[...], s_ref[...]).astype(BF16)
        qpr_ref[...] = qpr
        masks = _head_masks(qpr.shape)
        for hh in range(2):
            q_n = qn_ref[:, hh * QK_NOPE:(hh + 1) * QK_NOPE]
            q_p = jnp.where(masks[hh], qpr, jnp.zeros_like(qpr))
            kc, vc = 2 * hh * QK_NOPE, (2 * hh + 1) * QK_NOPE

            def step(kb, carry):
                m, l, acc = carry
                rows = pl.ds(pl.multiple_of(kb * t, t), t)
                sc = _scores(q_n, q_p, kv_ref[rows, kc:kc + QK_NOPE], kp_ref[rows, :], qi, kb, t)
                m_new = jnp.maximum(m, jnp.max(sc, axis=-1, keepdims=True))
                alpha = jnp.exp(m - m_new)
                p = jnp.exp(sc - m_new)
                l = alpha * l + jnp.sum(p, axis=-1, keepdims=True)
                acc = alpha * acc + jnp.dot(p.astype(BF16), kv_ref[rows, vc:vc + V_HEAD], preferred_element_type=F32)
                return m_new, l, acc

            init = (jnp.full((t, 1), _NEG, F32), jnp.zeros((t, 1), F32), jnp.zeros((t, V_HEAD), F32))
            m, l, acc = lax.fori_loop(0, qi + 1, step, init)
            o_ref[:, hh * V_HEAD:(hh + 1) * V_HEAD] = acc / l
            l_ref[:, hh:hh + 1] = m + jnp.log(l)

    return pl.pallas_call(
        body, name="attn_fwd", grid=(hp, nq),
        in_specs=[pl.BlockSpec((t, 2 * QK_NOPE), lambda h, i: (i, h)), pl.BlockSpec((t, LANES), lambda h, i: (i, h)),
                  pl.BlockSpec((s, 4 * QK_NOPE), lambda h, i: (0, h)), _full((s, LANES)),
                  pl.BlockSpec((t, LANES), lambda h, i: (i, 0)), pl.BlockSpec((t, LANES), lambda h, i: (i, 0))],
        out_specs=[pl.BlockSpec((t, 2 * V_HEAD), lambda h, i: (i, h)), pl.BlockSpec((t, LANES), lambda h, i: (i, h)),
                   pl.BlockSpec((None, t, 2), lambda h, i: (h, i, 0))],
        out_shape=[jax.ShapeDtypeStruct((s, HEADS * V_HEAD), F32), jax.ShapeDtypeStruct((s, HEADS * QK_ROPE), BF16),
                   jax.ShapeDtypeStruct((hp, s, 2), F32)],
        compiler_params=_params("parallel", "parallel"))(qn, qp, kv, kpr, cos4, sin4)


def attn_bwd_q(qn, qpr, kv, kpr, o, do, lse, cos4, sin4):
    s = qn.shape[0]
    hp = HEADS // 2
    t = _tile(s, ATT_TILE)
    nq = s // t

    def body(qn_ref, qpr_ref, kv_ref, kp_ref, o_ref, do_ref, l_ref, c_ref, s_ref, dqn_ref, dqp_ref):
        qi = pl.program_id(1)
        qpr = qpr_ref[...]
        masks = _head_masks(qpr.shape)
        dqp = jnp.zeros(qpr.shape, F32)
        for hh in range(2):
            q_n = qn_ref[:, hh * QK_NOPE:(hh + 1) * QK_NOPE]
            q_p = jnp.where(masks[hh], qpr, jnp.zeros_like(qpr))
            kc, vc = 2 * hh * QK_NOPE, (2 * hh + 1) * QK_NOPE
            do_h = do_ref[:, hh * V_HEAD:(hh + 1) * V_HEAD]
            delta = jnp.sum(do_h * o_ref[:, hh * V_HEAD:(hh + 1) * V_HEAD], axis=-1, keepdims=True)
            do_b = do_h.astype(BF16)
            lse_h = l_ref[:, hh:hh + 1]

            def step(kb, carry):
                dn, dp_ = carry
                rows = pl.ds(pl.multiple_of(kb * t, t), t)
                k = kv_ref[rows, kc:kc + QK_NOPE]
                kp = kp_ref[rows, :]
                p = jnp.exp(_scores(q_n, q_p, k, kp, qi, kb, t) - lse_h)
                dpv = lax.dot_general(do_b, kv_ref[rows, vc:vc + V_HEAD], (((1,), (1,)), ((), ())), preferred_element_type=F32)
                ds = (p * (dpv - delta) * _ATT_SCALE).astype(BF16)
                dn = dn + jnp.dot(ds, k, preferred_element_type=F32)
                dp_ = dp_ + jnp.dot(ds, kp, preferred_element_type=F32)
                return dn, dp_

            dn, dp_h = lax.fori_loop(0, qi + 1, step, (jnp.zeros((t, QK_NOPE), F32), jnp.zeros((t, LANES), F32)))
            dqn_ref[:, hh * QK_NOPE:(hh + 1) * QK_NOPE] = dn.astype(dqn_ref.dtype)
            dqp = dqp + jnp.where(masks[hh], dp_h, jnp.zeros_like(dp_h))
        dqp_ref[...] = _rope(dqp, c_ref[...], -s_ref[...]).astype(dqp_ref.dtype)

    qblk = pl.BlockSpec((t, 2 * QK_NOPE), lambda h, i: (i, h))
    pblk = pl.BlockSpec((t, LANES), lambda h, i: (i, h))
    tab = pl.BlockSpec((t, LANES), lambda h, i: (i, 0))
    return pl.pallas_call(
        body, name="attn_bwd_q", grid=(hp, nq),
        in_specs=[qblk, pblk, pl.BlockSpec((s, 4 * QK_NOPE), lambda h, i: (0, h)), _full((s, LANES)), qblk, qblk,
                  pl.BlockSpec((None, t, 2), lambda h, i: (h, i, 0)), tab, tab],
        out_specs=[qblk, pblk],
        out_shape=[jax.ShapeDtypeStruct((s, HEADS * QK_NOPE), BF16), jax.ShapeDtypeStruct((s, HEADS * QK_ROPE), BF16)],
        compiler_params=_params("parallel", "parallel"))(qn, qpr, kv, kpr, o, do, lse, cos4, sin4)


def attn_bwd_kv(qn, qpr, kv, kpr, o, do, lse):
    s = qn.shape[0]
    hp = HEADS // 2
    t = _tile(s, ATT_TILE)
    nq = s // t

    def body(qn_ref, qpr_ref, kv_ref, kp_ref, o_ref, do_ref, l_ref, dkv_ref, dkp_ref):
        ki = pl.program_id(1)
        rows_k = pl.ds(pl.multiple_of(ki * t, t), t)
        kp = kp_ref[rows_k, :]
        dkp = jnp.zeros((t, LANES), F32)
        for hh in range(2):
            kc, vc = 2 * hh * QK_NOPE, (2 * hh + 1) * QK_NOPE
            k = kv_ref[rows_k, kc:kc + QK_NOPE]
            v = kv_ref[rows_k, vc:vc + V_HEAD]

            def step(qb, carry):
                dk, dv, dkp_h = carry
                rows = pl.ds(pl.multiple_of(qb * t, t), t)
                q_n = qn_ref[rows, hh * QK_NOPE:(hh + 1) * QK_NOPE]
                qpr = qpr_ref[rows, :]
                lane = lax.broadcasted_iota(jnp.int32, qpr.shape, 1)
                sel = (lane < QK_ROPE) if hh == 0 else (lane >= QK_ROPE)
                q_p = jnp.where(sel, qpr, jnp.zeros_like(qpr))
                do_h = do_ref[rows, hh * V_HEAD:(hh + 1) * V_HEAD]
                delta = jnp.sum(do_h * o_ref[rows, hh * V_HEAD:(hh + 1) * V_HEAD], axis=-1, keepdims=True)
                do_b = do_h.astype(BF16)
                p = jnp.exp(_scores(q_n, q_p, k, kp, qb, ki, t) - l_ref[rows, hh:hh + 1])
                dpv = lax.dot_general(do_b, v, (((1,), (1,)), ((), ())), preferred_element_type=F32)
                ds = (p * (dpv - delta) * _ATT_SCALE).astype(BF16)
                dv = dv + lax.dot_general(p.astype(BF16), do_b, (((0,), (0,)), ((), ())), preferred_element_type=F32)
                dk = dk + lax.dot_general(ds, q_n, (((0,), (0,)), ((), ())), preferred_element_type=F32)
                dkp_h = dkp_h + lax.dot_general(ds, q_p, (((0,), (0,)), ((), ())), preferred_element_type=F32)
                return dk, dv, dkp_h

            init = (jnp.zeros((t, QK_NOPE), F32), jnp.zeros((t, V_HEAD), F32), jnp.zeros((t, LANES), F32))
            dk, dv, dkp_h = lax.fori_loop(ki, nq, step, init)
            dkv_ref[:, kc:kc + QK_NOPE] = dk.astype(dkv_ref.dtype)
            dkv_ref[:, vc:vc + V_HEAD] = dv.astype(dkv_ref.dtype)
            dkp = dkp + dkp_h
        dkp_ref[...] = dkp

    return pl.pallas_call(
        body, name="attn_bwd_kv", grid=(hp, nq),
        in_specs=[pl.BlockSpec((s, 2 * QK_NOPE), lambda h, i: (0, h)), pl.BlockSpec((s, LANES), lambda h, i: (0, h)),
                  pl.BlockSpec((s, 4 * QK_NOPE), lambda h, i: (0, h)), _full((s, LANES)),
                  pl.BlockSpec((s, 2 * V_HEAD), lambda h, i: (0, h)), pl.BlockSpec((s, 2 * V_HEAD), lambda h, i: (0, h)),
                  pl.BlockSpec((None, s, 2), lambda h, i: (h, 0, 0))],
        out_specs=[pl.BlockSpec((t, 4 * QK_NOPE), lambda h, i: (i, h)), pl.BlockSpec((None, t, LANES), lambda h, i: (h, i, 0))],
        out_shape=[jax.ShapeDtypeStruct((s, HEADS * 2 * QK_NOPE), BF16), jax.ShapeDtypeStruct((hp, s, LANES), F32)],
        compiler_params=_params("parallel", "parallel"))(qn, qpr, kv, kpr, o, do, lse)


def _dot_nt(a, b):
    return lax.dot_general(a, b, (((1,), (1,)), ((), ())), preferred_element_type=F32)


def _dot_tn(a, b):
    return lax.dot_general(a, b, (((0,), (0,)), ((), ())), preferred_element_type=F32)


def _q_cat(q_n, qpr, hh):
    lane = lax.broadcasted_iota(jnp.int32, qpr.shape, 1)
    sel = (lane < QK_ROPE) if hh == 0 else (lane >= QK_ROPE)
    return jnp.concatenate([q_n, jnp.where(sel, qpr, jnp.zeros_like(qpr))], axis=1)


def _causal(sc):
    row = lax.broadcasted_iota(jnp.int32, sc.shape, 0)
    col = lax.broadcasted_iota(jnp.int32, sc.shape, 1)
    return jnp.where(col <= row, sc, _NEG)


def attn_fwd2(qn, qp, kv, kpr, cos4, sin4):
    s = qn.shape[0]
    hp = HEADS // 2
    t = _tile(s, ATT_TILE)
    nq = s // t

    def body(qn_ref, qp_ref, kv_ref, kp_ref, c_ref, s_ref, o_ref, qpr_ref, l_ref, kcat_ref):
        qi = pl.program_id(1)

        @pl.when(qi == 0)
        def _():
            for hh in range(2):
                kcat_ref[hh, :, 0:QK_NOPE] = kv_ref[:, 2 * hh * QK_NOPE:(2 * hh + 1) * QK_NOPE]
                kcat_ref[hh, :, QK_NOPE:] = kp_ref[...]

        qpr = _rope(qp_ref[...], c_ref[...], s_ref[...]).astype(BF16)
        qpr_ref[...] = qpr
        qcat = [_q_cat(qn_ref[:, hh * QK_NOPE:(hh + 1) * QK_NOPE], qpr, hh) for hh in range(2)]

        def block(kb, carry, diagonal):
            rows = pl.ds(pl.multiple_of(kb * t, t), t)
            out = []
            for hh in range(2):
                m, l, acc = carry[hh]
                sc = _dot_nt(qcat[hh], kcat_ref[hh, rows, :]) * _ATT_SCALE
                if diagonal:
                    sc = _causal(sc)
                m_new = jnp.maximum(m, jnp.max(sc, axis=-1, keepdims=True))
                alpha = jnp.exp(m - m_new)
                p = jnp.exp(sc - m_new)
                l = alpha * l + jnp.sum(p, axis=-1, keepdims=True)
                v = kv_ref[rows, (2 * hh + 1) * QK_NOPE:(2 * hh + 2) * QK_NOPE]
                acc = alpha * acc + jnp.dot(p.astype(BF16), v, preferred_element_type=F32)
                out.append((m_new, l, acc))
            return tuple(out)

        one = (jnp.full((t, 1), _NEG, F32), jnp.zeros((t, 1), F32), jnp.zeros((t, V_HEAD), F32))
        carry = lax.fori_loop(0, qi, lambda kb, cr: block(kb, cr, False), (one, one))
        carry = block(qi, carry, True)
        for hh in range(2):
            m, l, acc = carry[hh]
            o_ref[:, hh * V_HEAD:(hh + 1) * V_HEAD] = acc / l
            l_ref[:, hh:hh + 1] = m + jnp.log(l)

    return pl.pallas_call(
        body, name="attn_fwd", grid=(hp, nq),
        in_specs=[pl.BlockSpec((t, 2 * QK_NOPE), lambda h, i: (i, h)), pl.BlockSpec((t, LANES), lambda h, i: (i, h)),
                  pl.BlockSpec((s, 4 * QK_NOPE), lambda h, i: (0, h)), _full((s, LANES)),
                  pl.BlockSpec((t, LANES), lambda h, i: (i, 0)), pl.BlockSpec((t, LANES), lambda h, i: (i, 0))],
        out_specs=[pl.BlockSpec((t, 2 * V_HEAD), lambda h, i: (i, h)), pl.BlockSpec((t, LANES), lambda h, i: (i, h)),
                   pl.BlockSpec((None, t, 2), lambda h, i: (h, i, 0))],
        out_shape=[jax.ShapeDtypeStruct((s, HEADS * V_HEAD), F32), jax.ShapeDtypeStruct((s, HEADS * QK_ROPE), BF16),
                   jax.ShapeDtypeStruct((hp, s, 2), F32)],
        scratch_shapes=[pltpu.VMEM((2, s, 2 * QK_NOPE), BF16)],
        compiler_params=_params("parallel", "arbitrary"))(qn, qp, kv, kpr, cos4, sin4)


def attn_bwd2(qn, qpr, kv, kpr, o, do, lse, cos4, sin4):
    s = qn.shape[0]
    hp = HEADS // 2
    t = _tile(s, ATT_TILE)
    nk = s // t

    def body(qn_ref, qpr_ref, kv_ref, kp_ref, o_ref, do_ref, l_ref, c_ref, s_ref,
             dqn_ref, dqp_ref, dkv_ref, dkp_ref, qcat_ref, dq_ref, delta_ref):
        ki = pl.program_id(1)

        @pl.when(ki == 0)
        def _():
            dq_ref[...] = jnp.zeros_like(dq_ref)
            for hh in range(2):
                qcat_ref[hh] = _q_cat(qn_ref[:, hh * QK_NOPE:(hh + 1) * QK_NOPE], qpr_ref[...], hh)
                cols = slice(hh * V_HEAD, (hh + 1) * V_HEAD)
                delta_ref[hh] = jnp.sum(do_ref[:, cols] * o_ref[:, cols], axis=-1, keepdims=True)

        rows_k = pl.ds(pl.multiple_of(ki * t, t), t)
        kcat = [jnp.concatenate([kv_ref[rows_k, 2 * hh * QK_NOPE:(2 * hh + 1) * QK_NOPE], kp_ref[rows_k, :]], axis=1) for hh in range(2)]
        vs = [kv_ref[rows_k, (2 * hh + 1) * QK_NOPE:(2 * hh + 2) * QK_NOPE] for hh in range(2)]

        def block(qb, carry, diagonal):
            rows = pl.ds(pl.multiple_of(qb * t, t), t)
            out = []
            for hh in range(2):
                dkc, dv = carry[hh]
                q_c = qcat_ref[hh, rows, :]
                do_b = do_ref[rows, hh * V_HEAD:(hh + 1) * V_HEAD].astype(BF16)
                sc = _dot_nt(q_c, kcat[hh]) * _ATT_SCALE
                if diagonal:
                    sc = _causal(sc)
                p = jnp.exp(sc - l_ref[rows, hh:hh + 1])
                dpv = _dot_nt(do_b, vs[hh])
                ds = (p * (dpv - delta_ref[hh, rows, :]) * _ATT_SCALE).astype(BF16)
                dv = dv + _dot_tn(p.astype(BF16), do_b)
                dkc = dkc + _dot_tn(ds, q_c)
                dq_ref[hh, rows, :] += jnp.dot(ds, kcat[hh], preferred_element_type=F32)
                out.append((dkc, dv))
            return tuple(out)

        one = (jnp.zeros((t, 2 * QK_NOPE), F32), jnp.zeros((t, V_HEAD), F32))
        carry = block(ki, (one, one), True)
        carry = lax.fori_loop(ki + 1, nk, lambda qb, cr: block(qb, cr, False), carry)
        dkp = jnp.zeros((t, LANES), F32)
        for hh in range(2):
            dkc, dv = carry[hh]
            dkv_ref[:, 2 * hh * QK_NOPE:(2 * hh + 1) * QK_NOPE] = dkc[:, :QK_NOPE].astype(dkv_ref.dtype)
            dkv_ref[:, (2 * hh + 1) * QK_NOPE:(2 * hh + 2) * QK_NOPE] = dv.astype(dkv_ref.dtype)
            dkp = dkp + dkc[:, QK_NOPE:]
        dkp_ref[...] = dkp

        @pl.when(ki == nk - 1)
        def _():
            lane = lax.broadcasted_iota(jnp.int32, (s, LANES), 1)
            dqp = jnp.where(lane < QK_ROPE, dq_ref[0, :, QK_NOPE:], dq_ref[1, :, QK_NOPE:])
            dqp_ref[...] = _rope(dqp, c_ref[...], -s_ref[...]).astype(dqp_ref.dtype)
            for hh in range(2):
                dqn_ref[:, hh * QK_NOPE:(hh + 1) * QK_NOPE] = dq_ref[hh, :, :QK_NOPE].astype(dqn_ref.dtype)

    qblk = pl.BlockSpec((s, 2 * QK_NOPE), lambda h, i: (0, h))
    pblk = pl.BlockSpec((s, LANES), lambda h, i: (0, h))
    tab = _full((s, LANES))
    return pl.pallas_call(
        body, name="attn_bwd", grid=(hp, nk),
        in_specs=[qblk, pblk, pl.BlockSpec((s, 4 * QK_NOPE), lambda h, i: (0, h)), tab, qblk, qblk,
                  pl.BlockSpec((None, s, 2), lambda h, i: (h, 0, 0)), tab, tab],
        out_specs=[qblk, pblk, pl.BlockSpec((t, 4 * QK_NOPE), lambda h, i: (i, h)), pl.BlockSpec((None, t, LANES), lambda h, i: (h, i, 0))],
        out_shape=[jax.ShapeDtypeStruct((s, HEADS * QK_NOPE), BF16), jax.ShapeDtypeStruct((s, HEADS * QK_ROPE), BF16),
                   jax.ShapeDtypeStruct((s, HEADS * 2 * QK_NOPE), BF16), jax.ShapeDtypeStruct((hp, s, LANES), F32)],
        scratch_shapes=[pltpu.VMEM((2, s, 2 * QK_NOPE), BF16), pltpu.VMEM((2, s, 2 * QK_NOPE), F32), pltpu.VMEM((2, s, 1), F32)],
        compiler_params=_params("parallel", "arbitrary"))(qn, qpr, kv, kpr, o, do, lse, cos4, sin4)


def kpe_bwd(dkp, cos4, sin4, pad_cols):
    hp, s, _ = dkp.shape
    tr = _tile(s, ROW_TILE * 2)

    def body(d_ref, c_ref, s_ref, o_ref):
        tot = d_ref[0]
        for h in range(1, hp):
            tot = tot + d_ref[h]
        tot = tot + pltpu.roll(tot, QK_ROPE, 1)
        lane = lax.broadcasted_iota(jnp.int32, tot.shape, 1)
        dk = jnp.where(lane < QK_ROPE, _rope(tot, c_ref[...], -s_ref[...]), jnp.zeros_like(tot))
        o_ref[...] = jnp.zeros_like(o_ref)
        o_ref[:, 0:LANES] = dk.astype(o_ref.dtype)

    row = pl.BlockSpec((tr, LANES), lambda i: (i, 0))
    return pl.pallas_call(body, name="kpe_bwd", grid=(s // tr,),
                          in_specs=[pl.BlockSpec((hp, tr, LANES), lambda i: (0, i, 0)), row, row],
                          out_specs=pl.BlockSpec((tr, pad_cols), lambda i: (i, 0)),
                          out_shape=jax.ShapeDtypeStruct((s, pad_cols), BF16), compiler_params=_params("parallel"))(dkp, cos4, sin4)


def _shift_down(x, n):
    row = lax.broadcasted_iota(jnp.int32, x.shape, 0)
    return jnp.where(row >= n, pltpu.roll(x, n, 0), jnp.zeros_like(x))


def _shift_up(x, n):
    rows = x.shape[0]
    row = lax.broadcasted_iota(jnp.int32, x.shape, 0)
    return jnp.where(row < rows - n, pltpu.roll(x, rows - n, 0), jnp.zeros_like(x))


def _conv(x, w_ref, b_ref):
    return w_ref[2:3, :] * x + w_ref[1:2, :] * _shift_down(x, 1) + w_ref[0:1, :] * _shift_down(x, 2) + b_ref[...]


def conv_act_fwd(upre, conv_w, conv_b):
    s, f2 = upre.shape
    f = f2 // 2
    tc = _tile(f, COL_TILE)
    nc = f // tc

    def body(ug_ref, uv_ref, wg_ref, wv_ref, bg_ref, bv_ref, o_ref):
        gh = _conv(ug_ref[...], wg_ref, bg_ref)
        vh = _conv(uv_ref[...], wv_ref, bv_ref)
        o_ref[...] = (gh * _sigmoid(gh) * vh).astype(o_ref.dtype)

    def spec(rows, shift):
        return pl.BlockSpec((rows, tc), lambda j: (0, j + shift))

    return pl.pallas_call(
        body, name="conv_act_fwd", grid=(nc,),
        in_specs=[spec(s, 0), spec(s, nc), spec(3, 0), spec(3, nc), spec(1, 0), spec(1, nc)], out_specs=spec(s, 0),
        out_shape=jax.ShapeDtypeStruct((s, f), BF16), compiler_params=_params("parallel"))(upre, upre, conv_w, conv_w, conv_b, conv_b)


def conv_act_bwd(upre, conv_w, conv_b, df):
    s, f2 = upre.shape
    f = f2 // 2
    tc = _tile(f, COL_TILE)
    nc = f // tc

    def half(x, d, w_ref, du_ref, gw_ref, gb_ref):
        gb_ref[...] = _colsum(d)
        gw_ref[2:3, :] = _colsum(d * x)
        gw_ref[1:2, :] = _colsum(d * _shift_down(x, 1))
        gw_ref[0:1, :] = _colsum(d * _shift_down(x, 2))
        du_ref[...] = (w_ref[2:3, :] * d + w_ref[1:2, :] * _shift_up(d, 1) + w_ref[0:1, :] * _shift_up(d, 2)).astype(du_ref.dtype)

    def body(ug_ref, uv_ref, wg_ref, wv_ref, bg_ref, bv_ref, df_ref, dug_ref, duv_ref, gwg_ref, gwv_ref, gbg_ref, gbv_ref):
        xg, xv = ug_ref[...], uv_ref[...]
        gh = _conv(xg, wg_ref, bg_ref)
        vh = _conv(xv, wv_ref, bv_ref)
        sg = _sigmoid(gh)
        df_v = df_ref[...]
        half(xg, df_v * vh * (sg * (1.0 + gh * (1.0 - sg))), wg_ref, dug_ref, gwg_ref, gbg_ref)
        half(xv, df_v * (gh * sg), wv_ref, duv_ref, gwv_ref, gbv_ref)

    def spec(rows, shift):
        return pl.BlockSpec((rows, tc), lambda j: (0, j + shift))

    act = jax.ShapeDtypeStruct((s, f), BF16)
    gw = jax.ShapeDtypeStruct((3, f), F32)
    gb = jax.ShapeDtypeStruct((1, f), F32)
    return pl.pallas_call(
        body, name="conv_act_bwd", grid=(nc,),
        in_specs=[spec(s, 0), spec(s, nc), spec(3, 0), spec(3, nc), spec(1, 0), spec(1, nc), spec(s, 0)],
        out_specs=[spec(s, 0), spec(s, 0), spec(3, 0), spec(3, 0), spec(1, 0), spec(1, 0)],
        out_shape=[act, act, gw, gw, gb, gb],
        compiler_params=_params("parallel"))(upre, upre, conv_w, conv_w, conv_b, conv_b, df)


def adamw(name, w, m, v, parts, row_off=0):
    npart, c = parts.shape[0], parts.shape[2]
    r = w.shape[0]
    tr = r
    if r % 8 == 0:
        tr = max(8, min(r, ADAMW_TILE_ELEMS // c) // 8 * 8)
        while r % tr:
            tr -= 8
    bc1 = 1.0 - ADAM_B1 ** ADAM_STEP
    bc2 = 1.0 - ADAM_B2 ** ADAM_STEP

    def body(w_ref, m_ref, v_ref, p_ref, g_ref, d_ref, nm_ref, nv_ref):
        g = p_ref[0].astype(F32)
        for k in range(1, npart):
            g = g + p_ref[k].astype(F32)
        m_new = ADAM_B1 * m_ref[...] + (1.0 - ADAM_B1) * g
        v_new = ADAM_B2 * v_ref[...] + (1.0 - ADAM_B2) * (g * g)
        g_ref[...] = g
        nm_ref[...] = m_new
        nv_ref[...] = v_new
        d_ref[...] = -ADAM_LR * ((m_new / bc1) / (jnp.sqrt(v_new / bc2) + ADAM_EPS) + ADAM_WD * w_ref[...])

    assert row_off % tr == 0
    deps = _TOKENS.take()
    blk = pl.BlockSpec((tr, c), lambda i: (i, 0))
    out = jax.ShapeDtypeStruct((r, c), F32)
    return pl.pallas_call(
        lambda *refs: body(*refs[:4], *refs[4 + len(deps):]), name=name, grid=(r // tr,),
        in_specs=[blk, blk, blk, pl.BlockSpec((npart, tr, c), lambda i: (0, row_off // tr + i, 0))] + [pl.BlockSpec(memory_space=pl.ANY)] * len(deps),
        out_specs=[blk, blk, blk, blk], out_shape=[out, out, out, out], compiler_params=_params("parallel"))(w, m, v, parts, *deps)


def _position():
    return lax.axis_index("x"), lax.axis_index("y"), lax.axis_index("c")


def _index(p):
    return 4 * p[0] + 2 * p[1] + p[2]


def _peer(me, r):
    return (me[0] ^ ((r >> 2) & 1), me[1] ^ ((r >> 1) & 1), me[2] ^ (r & 1))


_ANY = pl.BlockSpec(memory_space=pl.ANY)


def all_gather_two_level(shards):
    n = len(shards)

    def body(*refs):
        ins, outs = refs[:n], refs[n:2 * n]
        send_sems, recv_sems, local_sems = refs[2 * n:]
        x, y, c = _position()
        me, sibling = (x, y, c), (x, y, 1 - c)
        chips = [(1 - x, y), (x, 1 - y), (1 - x, 1 - y)]

        def copy(w, k, block, to, src=None):
            slot = outs[w].at[_index(block)]
            return pltpu.make_async_remote_copy(src_ref=slot if src is None else src, dst_ref=slot,
                                                send_sem=send_sems.at[7 * w + k], recv_sem=recv_sems.at[7 * w + k],
                                                device_id=to, device_id_type=MESH)

        mine = [pltpu.make_async_copy(ins[w], outs[w].at[_index(me)], local_sems.at[w]) for w in range(n)]
        for cp in mine:
            cp.start()
        first = []
        for w in range(n):
            first.append(copy(w, 0, me, sibling, src=ins[w]))
            first += [copy(w, 1 + j, me, (*chip, c), src=ins[w]) for j, chip in enumerate(chips)]
        for cp in first:
            cp.start()
        passed = []
        for w in range(n):
            for j, chip in enumerate(chips):
                copy(w, 1 + j, (*chip, c), me).wait_recv()
                cp = copy(w, 4 + j, (*chip, c), sibling)
                cp.start()
                passed.append(cp)
        for w in range(n):
            copy(w, 0, sibling, me).wait_recv()
            for j, chip in enumerate(chips):
                copy(w, 4 + j, (*chip, 1 - c), me).wait_recv()
        for cp in first + passed:
            cp.wait_send()
        for cp in mine:
            cp.wait()

    return pl.pallas_call(
        body, name="all_gather_weights",
        out_shape=[jax.ShapeDtypeStruct((N_DEV,) + a.shape, a.dtype) for a in shards],
        in_specs=[_ANY] * n, out_specs=[_ANY] * n,
        scratch_shapes=[pltpu.SemaphoreType.DMA((7 * n,)), pltpu.SemaphoreType.DMA((7 * n,)), pltpu.SemaphoreType.DMA((n,))],
        )(*shards)


def exchange(name, arrays, scatter):
    n = len(arrays)

    def body(*refs):
        ins, outs = refs[:n], refs[n:2 * n]
        send_sems, recv_sems, local_sems = refs[2 * n:]
        me = _position()
        copies = []
        for w in range(n):
            src = ins[w].at[_index(me)] if scatter else ins[w]
            cp = pltpu.make_async_copy(src, outs[w].at[_index(me)], local_sems.at[w])
            cp.start()
            copies.append(cp)
        remote = []
        for w in range(n):
            for r in range(1, N_DEV):
                peer = _peer(me, r)
                src = ins[w].at[_index(peer)] if scatter else ins[w]
                cp = pltpu.make_async_remote_copy(src_ref=src, dst_ref=outs[w].at[_index(me)],
                                                  send_sem=send_sems.at[7 * w + r - 1], recv_sem=recv_sems.at[7 * w + r - 1],
                                                  device_id=peer, device_id_type=MESH)
                cp.start()
                remote.append(cp)
        for cp in remote:
            cp.wait()
        for cp in copies:
            cp.wait()

    blocks = [a.shape[1:] if scatter else a.shape for a in arrays]
    return pl.pallas_call(
        body, name=name,
        out_shape=[jax.ShapeDtypeStruct((N_DEV,) + b, a.dtype) for a, b in zip(arrays, blocks)],
        in_specs=[_ANY] * n, out_specs=[_ANY] * n,
        scratch_shapes=[pltpu.SemaphoreType.DMA((7 * n,)), pltpu.SemaphoreType.DMA((7 * n,)), pltpu.SemaphoreType.DMA((n,))],
        )(*arrays)


_HBM = pl.BlockSpec(memory_space=pltpu.HBM)
_SEM = pl.BlockSpec(memory_space=pltpu.SEMAPHORE)
_EFFECT = pltpu.SideEffectType.DATAFLOW_SIDE_EFFECTING


def _direct_copies(ins, lands, send_sems, recv_sems, scatter):
    me = _position()
    copies = []
    for w in range(len(ins)):
        for r in range(1, N_DEV):
            peer = _peer(me, r)
            src = ins[w].at[_index(peer)] if scatter else ins[w]
            copies.append(pltpu.make_async_remote_copy(src_ref=src, dst_ref=lands[w].at[_index(me)], send_sem=send_sems.at[7 * w + r - 1],
                                                       recv_sem=recv_sems.at[7 * w + r - 1], device_id=peer, device_id_type=MESH))
    return copies


def exchange_start(name, groups, scatter):
    arrays = [a for g in groups for a in g]
    n = len(arrays)
    blocks = [a.shape[1:] if scatter else a.shape for a in arrays]
    lands = [lax.empty((N_DEV,) + b, a.dtype) for a, b in zip(arrays, blocks)]
    ng = len(groups)

    def body(*refs):
        ins, lnd = refs[:n], refs[n:2 * n]
        sems = refs[2 * n:2 * n + 2 * ng]
        token = refs[2 * n + 2 * ng + 2 * n]
        local_sem = refs[2 * n + 2 * ng + 2 * n + 1]
        me = _position()
        local = []
        for w in range(n):
            src = ins[w].at[_index(me)] if scatter else ins[w]
            cp = pltpu.make_async_copy(src, lnd[w].at[_index(me)], local_sem.at[w])
            cp.start()
            local.append(cp)
        w0 = 0
        for gi, g in enumerate(groups):
            for cp in _direct_copies(ins[w0:w0 + len(g)], lnd[w0:w0 + len(g)], sems[2 * gi], sems[2 * gi + 1], scatter):
                cp.start()
            w0 += len(g)
        for cp in local:
            cp.wait()
        token[...] = jnp.zeros_like(token)

    sem_shapes = []
    for g in groups:
        sem_shapes += [pltpu.SemaphoreType.DMA((7 * len(g),)), pltpu.SemaphoreType.DMA((7 * len(g),))]
    out = pl.pallas_call(
        body, name=name,
        out_shape=tuple(sem_shapes) + tuple(pltpu.HBM(a.shape, a.dtype) for a in arrays) + tuple(pltpu.HBM(l.shape, l.dtype) for l in lands)
        + (jax.ShapeDtypeStruct((8, LANES), F32),),
        in_specs=[_HBM] * (2 * n), out_specs=tuple([_SEM] * (2 * ng) + [_HBM] * (2 * n) + [pl.BlockSpec(memory_space=pltpu.VMEM)]),
        input_output_aliases={i: 2 * ng + i for i in range(2 * n)},
        scratch_shapes=[pltpu.SemaphoreType.DMA((n,))],
        compiler_params=pltpu.CompilerParams(has_side_effects=_EFFECT),
    )(*[pltpu.with_memory_space_constraint(a, pltpu.HBM) for a in arrays], *[pltpu.with_memory_space_constraint(l, pltpu.HBM) for l in lands])
    sems, thru, token = out[:2 * ng], out[2 * ng:2 * ng + 2 * n], out[-1]
    res, w0 = [], 0
    for gi, g in enumerate(groups):
        res.append((sems[2 * gi], sems[2 * gi + 1], list(thru[w0:w0 + len(g)]), list(thru[n + w0:n + w0 + len(g)])))
        w0 += len(g)
    return res, token


def exchange_wait(name, group, after, scatter):
    send_sems, recv_sems, srcs, lands = group
    n = len(srcs)

    def body(*refs):
        ins, lnd = refs[:n], refs[n:2 * n]
        for cp in _direct_copies(ins, lnd, refs[2 * n], refs[2 * n + 1], scatter):
            cp.wait_send()
            cp.wait_recv()

    out = pl.pallas_call(
        body, name=name, out_shape=tuple(pltpu.HBM(a.shape, a.dtype) for a in srcs + lands),
        in_specs=[_HBM] * (2 * n) + [_SEM, _SEM, pl.BlockSpec(memory_space=pl.ANY)], out_specs=tuple([_HBM] * (2 * n)),
        input_output_aliases={i: i for i in range(2 * n)},
        compiler_params=pltpu.CompilerParams(has_side_effects=_EFFECT),
    )(*srcs, *lands, send_sems, recv_sems, after)
    return list(out[n:])


def _after(x, token):
    return lax.optimization_barrier((x, token))[0]


_TOKEN = jax.ShapeDtypeStruct((8, LANES), F32)
_VM = pl.BlockSpec(memory_space=pltpu.VMEM)
_SIDE = pltpu.CompilerParams(has_side_effects=_EFFECT)


def _hbm(a):
    return pltpu.with_memory_space_constraint(a, pltpu.HBM)


def _like(a):
    return pltpu.HBM(a.shape, a.dtype)


def _dma_sems(n):
    return pltpu.SemaphoreType.DMA((n,))


def _other_chips(x, y):
    return [(1 - x, y), (x, 1 - y), (1 - x, 1 - y)]


COPY_STREAMS = 8


def _row_chunks(src, dst):
    rows = src.shape[0]
    n = COPY_STREAMS
    while n > 1 and rows % (16 * n):
        n //= 2
    r = rows // n
    return [(src.at[pl.ds(i * r, r)], dst.at[pl.ds(i * r, r)]) for i in range(n)]


def _local_copy(src, dst, sem):
    return [pltpu.make_async_copy(s, d, sem) for s, d in _row_chunks(src, dst)]


class _rcopy:
    def __init__(self, src, dst, send_sem, recv_sem, to):
        self.parts = [pltpu.make_async_remote_copy(src_ref=s, dst_ref=d, send_sem=send_sem, recv_sem=recv_sem, device_id=to, device_id_type=MESH)
                      for s, d in _row_chunks(src, dst)]

    def start(self):
        for cp in self.parts:
            cp.start()

    def wait_send(self):
        for cp in self.parts:
            cp.wait_send()

    def wait_recv(self):
        for cp in self.parts:
            cp.wait_recv()


def ag_start(name, shard, after):
    land = lax.empty((N_DEV,) + shard.shape, shard.dtype)

    def body(sh_ref, land_ref, after_ref, send_sems, recv_sems, sh_thru, land_thru, token):
        x, y, c = _position()
        slot = land_ref.at[_index((x, y, c))]
        for k, to in enumerate([(x, y, 1 - c)] + [(*chip, c) for chip in _other_chips(x, y)]):
            _rcopy(sh_ref, slot, send_sems.at[k], recv_sems.at[k], to).start()
        token[...] = jnp.zeros_like(token)

    send, recv, shard, land, token = pl.pallas_call(
        body, name=name, out_shape=(_dma_sems(4), _dma_sems(4), _like(shard), _like(land), _TOKEN),
        in_specs=[_HBM, _HBM, _ANY], out_specs=(_SEM, _SEM, _HBM, _HBM, _VM), input_output_aliases={0: 2, 1: 3},
        compiler_params=_SIDE)(_hbm(shard), _hbm(land), after)
    _TOKENS.push(token)
    return send, recv, shard, land


def ag_forward(name, started, after):
    send, recv, shard, land = started

    def body(sh_ref, land_ref, send_sems, recv_sems, after_ref, fsend, frecv, sh_thru, land_thru, token):
        x, y, c = _position()
        for j, chip in enumerate(_other_chips(x, y)):
            slot = land_ref.at[_index((*chip, c))]
            _rcopy(sh_ref, slot, send_sems.at[1 + j], recv_sems.at[1 + j], (*chip, c)).wait_recv()
            _rcopy(slot, slot, fsend.at[j], frecv.at[j], (x, y, 1 - c)).start()
        token[...] = jnp.zeros_like(token)

    fsend, frecv, shard, land, token = pl.pallas_call(
        body, name=name, out_shape=(_dma_sems(3), _dma_sems(3), _like(shard), _like(land), _TOKEN),
        in_specs=[_HBM, _HBM, _SEM, _SEM, _ANY], out_specs=(_SEM, _SEM, _HBM, _HBM, _VM), input_output_aliases={0: 2, 1: 3},
        compiler_params=_SIDE)(shard, land, send, recv, after)
    _TOKENS.push(token)
    return send, recv, fsend, frecv, shard, land


def ag_wait(name, forwarded, after):
    send, recv, fsend, frecv, shard, land = forwarded

    def body(sh_ref, land_ref, send_sems, recv_sems, fsend_r, frecv_r, after_ref, sh_out, land_out, local_sem):
        x, y, c = _position()
        sibling = (x, y, 1 - c)
        own = land_ref.at[_index((x, y, c))]
        mine = _local_copy(sh_ref, own, local_sem.at[0])
        for cp in mine:
            cp.start()
        _rcopy(sh_ref, land_ref.at[_index(sibling)], send_sems.at[0], recv_sems.at[0], sibling).wait_recv()
        for j, chip in enumerate(_other_chips(x, y)):
            _rcopy(sh_ref, land_ref.at[_index((*chip, 1 - c))], fsend_r.at[j], frecv_r.at[j], sibling).wait_recv()
        for k in range(4):
            _rcopy(sh_ref, own, send_sems.at[k], recv_sems.at[k], sibling).wait_send()
        for j in range(3):
            _rcopy(sh_ref, own, fsend_r.at[j], frecv_r.at[j], sibling).wait_send()
        for cp in mine:
            cp.wait()

    return pl.pallas_call(
        body, name=name, out_shape=(_like(shard), _like(land)), in_specs=[_HBM, _HBM, _SEM, _SEM, _SEM, _SEM, _ANY],
        out_specs=(_HBM, _HBM), input_output_aliases={0: 0, 1: 1}, scratch_shapes=[_dma_sems(1)],
        compiler_params=_SIDE)(shard, land, send, recv, fsend, frecv, after)[1]


def rs_d2d_start(name, grads):
    n = len(grads)
    lands = [lax.empty((4,) + g.shape[1:], g.dtype) for g in grads]

    def body(*refs):
        ins, lnd, send_sems, recv_sems, token = refs[:n], refs[n:2 * n], refs[2 * n], refs[2 * n + 1], refs[4 * n + 2]
        x, y, c = _position()
        for w in range(n):
            for i in range(4):
                _rcopy(ins[w].at[2 * i + 1 - c], lnd[w].at[i], send_sems.at[4 * w + i], recv_sems.at[4 * w + i], (x, y, 1 - c)).start()
        token[...] = jnp.zeros_like(token)

    out = pl.pallas_call(
        body, name=name, out_shape=(_dma_sems(4 * n), _dma_sems(4 * n)) + tuple(_like(a) for a in grads + lands) + (_TOKEN,),
        in_specs=[_HBM] * (2 * n), out_specs=(_SEM, _SEM) + (_HBM,) * (2 * n) + (_VM,),
        input_output_aliases={i: 2 + i for i in range(2 * n)}, compiler_params=_SIDE)(*[_hbm(a) for a in grads + lands])
    _TOKENS.push(out[-1])
    return out[0], out[1], list(out[2:2 + n]), list(out[2 + n:2 + 2 * n])


def rs_d2d_wait(name, started, after):
    send, recv, grads, lands = started
    n = len(grads)

    def body(*refs):
        ins, lnd, send_sems, recv_sems = refs[:n], refs[n:2 * n], refs[2 * n], refs[2 * n + 1]
        x, y, c = _position()
        for w in range(n):
            for i in range(4):
                cp = _rcopy(ins[w].at[2 * i + 1 - c], lnd[w].at[i], send_sems.at[4 * w + i], recv_sems.at[4 * w + i], (x, y, 1 - c))
                cp.wait_send()
                cp.wait_recv()

    out = pl.pallas_call(
        body, name=name, out_shape=tuple(_like(a) for a in grads + lands), in_specs=[_HBM] * (2 * n) + [_SEM, _SEM, _ANY],
        out_specs=(_HBM,) * (2 * n), input_output_aliases={i: i for i in range(2 * n)}, compiler_params=_SIDE)(*grads, *lands, send, recv, after)
    return list(out[:n]), list(out[n:])


def pair_sum(name, grad, land, core):
    _, r, c = grad.shape
    tr = r
    if r % 8 == 0:
        tr = max(8, min(r, ADAMW_TILE_ELEMS // c) // 8 * 8)
        while r % tr:
            tr -= 8

    def body(core_ref, a_ref, b_ref, o_ref):
        o_ref[...] = (a_ref[...].astype(F32) + b_ref[...].astype(F32)).astype(o_ref.dtype)

    return pl.pallas_call(
        body, name=name, out_shape=jax.ShapeDtypeStruct((4, r, c), grad.dtype),
        grid_spec=pltpu.PrefetchScalarGridSpec(
            num_scalar_prefetch=1, grid=(4, r // tr),
            in_specs=[pl.BlockSpec((None, None, tr, c), lambda i, j, core_ref: (i, core_ref[0], j, 0)),
                      pl.BlockSpec((None, tr, c), lambda i, j, core_ref: (i, j, 0))],
            out_specs=pl.BlockSpec((None, tr, c), lambda i, j, core_ref: (i, j, 0))),
        compiler_params=_params("parallel", "parallel"))(core, grad.reshape(4, 2, r, c), land)


def rs_ici_start(name, sums):
    n = len(sums)
    lands = [lax.empty(a.shape, a.dtype) for a in sums]

    def body(*refs):
        ins, lnd, send_sems, recv_sems, token = refs[:n], refs[n:2 * n], refs[2 * n], refs[2 * n + 1], refs[4 * n + 2]
        x, y, c = _position()
        chip = 2 * x + y
        for w in range(n):
            for j, other in enumerate(_other_chips(x, y)):
                _rcopy(ins[w].at[2 * other[0] + other[1]], lnd[w].at[chip], send_sems.at[3 * w + j], recv_sems.at[3 * w + j], (*other, c)).start()
        token[...] = jnp.zeros_like(token)

    out = pl.pallas_call(
        body, name=name, out_shape=(_dma_sems(3 * n), _dma_sems(3 * n)) + tuple(_like(a) for a in sums + lands) + (_TOKEN,),
        in_specs=[_HBM] * (2 * n), out_specs=(_SEM, _SEM) + (_HBM,) * (2 * n) + (_VM,),
        input_output_aliases={i: 2 + i for i in range(2 * n)}, compiler_params=_SIDE)(*[_hbm(a) for a in sums + lands])
    _TOKENS.push(out[-1])
    return out[0], out[1], list(out[2:2 + n]), list(out[2 + n:2 + 2 * n])


def rs_ici_wait(name, started, after):
    send, recv, sums, lands = started
    n = len(sums)

    def body(*refs):
        ins, lnd, send_sems, recv_sems, local_sem = refs[:n], refs[n:2 * n], refs[2 * n], refs[2 * n + 1], refs[4 * n + 3]
        x, y, c = _position()
        chip = 2 * x + y
        local = [cp for w in range(n) for cp in _local_copy(ins[w].at[chip], lnd[w].at[chip], local_sem.at[w])]
        for cp in local:
            cp.start()
        for w in range(n):
            for j, other in enumerate(_other_chips(x, y)):
                cp = _rcopy(ins[w].at[2 * other[0] + other[1]], lnd[w].at[2 * other[0] + other[1]], send_sems.at[3 * w + j], recv_sems.at[3 * w + j], (*other, c))
                cp.wait_send()
                cp.wait_recv()
        for cp in local:
            cp.wait()

    out = pl.pallas_call(
        body, name=name, out_shape=tuple(_like(a) for a in sums + lands), in_specs=[_HBM] * (2 * n) + [_SEM, _SEM, _ANY],
        out_specs=(_HBM,) * (2 * n), input_output_aliases={i: i for i in range(2 * n)}, scratch_shapes=[_dma_sems(n)],
        compiler_params=_SIDE)(*sums, *lands, send, recv, after)
    return list(out[n:])


def ada_fwd(c, w_ada, b_ada3, conv_w):
    d, cs = w_ada.shape

    def body(c_ref, w_ref, b_ref, cw_ref, mod_ref, sc_ref, cwa_ref, part_ref, send_sems, recv_sems):
        me = _position()
        my = _index(me)
        cv = c_ref[...]
        sc_ref[my] = cv * _sigmoid(cv)
        cwa_ref[my] = cw_ref[...]
        gather = []
        for r in range(1, N_DEV):
            for k, ref in enumerate((sc_ref, cwa_ref)):
                cp = pltpu.make_async_remote_copy(src_ref=ref.at[my], dst_ref=ref.at[my], send_sem=send_sems.at[14 * k + r - 1],
                                                  recv_sem=recv_sems.at[14 * k + r - 1], device_id=_peer(me, r), device_id_type=MESH)
                cp.start()
                gather.append(cp)
        for cp in gather:
            cp.wait()
        sc_all = jnp.concatenate([sc_ref[k] for k in range(N_DEV)], axis=0).astype(BF16)
        part = jnp.dot(sc_all, w_ref[...].astype(BF16), preferred_element_type=F32)
        for k in range(N_DEV):
            part_ref[k] = part[k:k + 1, :]
        scatter = []
        for r in range(1, N_DEV):
            peer = _peer(me, r)
            cp = pltpu.make_async_remote_copy(src_ref=part_ref.at[_index(peer)], dst_ref=mod_ref.at[my], send_sem=send_sems.at[6 + r],
                                              recv_sem=recv_sems.at[6 + r], device_id=peer, device_id_type=MESH)
            cp.start()
            scatter.append(cp)
        mod_ref[my] = part_ref[my]
        for cp in scatter:
            cp.wait()
        mod_ref[...] = mod_ref[...] + b_ref[...]

    vm = pl.BlockSpec(memory_space=pltpu.VMEM)
    return pl.pallas_call(
        body, name="ada_fwd",
        out_shape=[jax.ShapeDtypeStruct((N_DEV, 1, cs), F32), jax.ShapeDtypeStruct((N_DEV, 1, d), F32),
                   jax.ShapeDtypeStruct((N_DEV,) + conv_w.shape, F32)],
        in_specs=[vm, vm, vm, vm], out_specs=[vm, vm, vm],
        scratch_shapes=[pltpu.VMEM((N_DEV, 1, cs), F32), pltpu.SemaphoreType.DMA((21,)), pltpu.SemaphoreType.DMA((21,))],
        compiler_params=pltpu.CompilerParams(vmem_limit_bytes=VMEM_LIMIT_BYTES))(c, w_ada, b_ada3, conv_w)


def ada_bwd_w(sc_all, dmod_cols):
    _, d = sc_all.shape
    cs = dmod_cols.shape[1]
    tr = _tile(d, ROW_TILE)

    def body(sc_ref, dm_ref, o_ref):
        dm = dm_ref[...].astype(BF16)
        o_ref[...] = lax.dot_general(sc_ref[...].astype(BF16), dm, (((0,), (0,)), ((), ())), preferred_element_type=F32)

    return pl.pallas_call(body, name="ada_bwd_w", grid=(d // tr,),
                          in_specs=[pl.BlockSpec((N_DEV, tr), lambda i: (0, i)), _full((N_DEV, cs))],
                          out_specs=pl.BlockSpec((None, tr, cs), lambda i: (0, i, 0)),
                          out_shape=jax.ShapeDtypeStruct((1, d, cs), F32), compiler_params=_params("parallel"))(sc_all, dmod_cols)


def _round_up(n, m):
    return (n + m - 1) // m * m


def kernel(x, c, positions, w_ada, b_ada, pre_norm1_g, w_in, gm_ln_g, gm_ln_b, gm_w_s, gm_b_s, w_branch_a, q_norm_g, w_uq, kv_norm_g, w_ukv, w_branch_b, w_out, post_norm1_g, pre_norm2_g, w_up, conv_w, conv_b, w_down, post_norm2_g, loss_target, m_w_ada, m_b_ada, m_pre_norm1_g, m_w_in, m_gm_ln_g, m_gm_ln_b, m_gm_w_s, m_gm_b_s, m_w_branch_a, m_q_norm_g, m_w_uq, m_kv_norm_g, m_w_ukv, m_w_branch_b, m_w_out, m_post_norm1_g, m_pre_norm2_g, m_w_up, m_conv_w, m_conv_b, m_w_down, m_post_norm2_g, v_w_ada, v_b_ada, v_pre_norm1_g, v_w_in, v_gm_ln_g, v_gm_ln_b, v_gm_w_s, v_gm_b_s, v_w_branch_a, v_q_norm_g, v_w_uq, v_kv_norm_g, v_w_ukv, v_w_branch_b, v_w_out, v_post_norm1_g, v_pre_norm2_g, v_w_up, v_conv_w, v_conv_b, v_w_down, v_post_norm2_g):
    weights = dict(w_ada=w_ada, b_ada=b_ada, pre_norm1_g=pre_norm1_g, w_in=w_in, gm_ln_g=gm_ln_g, gm_ln_b=gm_ln_b, gm_w_s=gm_w_s,
                   gm_b_s=gm_b_s, w_branch_a=w_branch_a, q_norm_g=q_norm_g, w_uq=w_uq, kv_norm_g=kv_norm_g, w_ukv=w_ukv,
                   w_branch_b=w_branch_b, w_out=w_out, post_norm1_g=post_norm1_g, pre_norm2_g=pre_norm2_g, w_up=w_up, conv_w=conv_w,
                   conv_b=conv_b, w_down=w_down, post_norm2_g=post_norm2_g)
    mom1 = dict(w_ada=m_w_ada, b_ada=m_b_ada, pre_norm1_g=m_pre_norm1_g, w_in=m_w_in, gm_ln_g=m_gm_ln_g, gm_ln_b=m_gm_ln_b,
                gm_w_s=m_gm_w_s, gm_b_s=m_gm_b_s, w_branch_a=m_w_branch_a, q_norm_g=m_q_norm_g, w_uq=m_w_uq, kv_norm_g=m_kv_norm_g,
                w_ukv=m_w_ukv, w_branch_b=m_w_branch_b, w_out=m_w_out, post_norm1_g=m_post_norm1_g, pre_norm2_g=m_pre_norm2_g,
                w_up=m_w_up, conv_w=m_conv_w, conv_b=m_conv_b, w_down=m_w_down, post_norm2_g=m_post_norm2_g)
    mom2 = dict(w_ada=v_w_ada, b_ada=v_b_ada, pre_norm1_g=v_pre_norm1_g, w_in=v_w_in, gm_ln_g=v_gm_ln_g, gm_ln_b=v_gm_ln_b,
                gm_w_s=v_gm_w_s, gm_b_s=v_gm_b_s, w_branch_a=v_w_branch_a, q_norm_g=v_q_norm_g, w_uq=v_w_uq, kv_norm_g=v_kv_norm_g,
                w_ukv=v_w_ukv, w_branch_b=v_w_branch_b, w_out=v_w_out, post_norm1_g=v_post_norm1_g, pre_norm2_g=v_pre_norm2_g,
                w_up=v_w_up, conv_w=v_conv_w, conv_b=v_conv_b, w_down=v_w_down, post_norm2_g=v_post_norm2_g)
    order = list(weights)
    _TOKENS.take()

    s, d = x.shape[1], x.shape[2]
    gmw = gm_ln_g.shape[0]
    groups = gmw // CHUNK
    ql, kvl = q_norm_g.shape[0], kv_norm_g.shape[0]
    f2 = conv_b.shape[0]
    in_cols = w_in.shape[1] * N_DEV
    o_q, o_kv, o_ga, o_gb, o_kpe = 2 * gmw, 2 * gmw + ql, 2 * gmw + ql + kvl, 2 * gmw + ql + kvl + d, 2 * gmw + ql + kvl + 2 * d
    zp = _round_up(o_kpe + LANES, Z_PAD)
    src_kpe = 2 * gmw + ql + kvl
    assert src_kpe + QK_ROPE + 2 * d == in_cols
    my = 4 * lax.axis_index("x") + 2 * lax.axis_index("y") + lax.axis_index("c")

    x2, tgt = x[0], loss_target[0]
    row = lambda a: a.reshape(1, -1)

    big = ["w_in", "w_branch_a", "w_uq", "w_ukv", "w_branch_b", "w_out", "w_up", "w_down"]
    sh = {k: weights[k].astype(BF16) for k in big}
    mix = ["w_branch_a", "w_uq", "w_ukv", "w_branch_b", "w_out"]
    mix_sizes = [sh[k].size for k in mix]
    mix_packed = jnp.concatenate([sh[k].reshape(-1) for k in mix]).reshape(-1, LANES)
    ag_in = ag_start("ag_start_in", sh["w_in"], c)

    mod8, sc_all3, g_cw = ada_fwd(c, w_ada, b_ada.reshape(N_DEV, 1, -1), conv_w)
    mod = mod8.reshape(N_MOD, d)
    shift1, scale1, gate1, shift2, scale2, gate2 = (mod[i:i + 1] for i in range(N_MOD))
    sc_all = sc_all3.reshape(N_DEV, d)
    h1 = norm_mod_fwd("pre1_fwd", x2, row(pre_norm1_g), scale1, shift1)

    g_in = ag_wait("ag_wait_in", ag_forward("ag_forward_in", ag_in, h1), h1)
    ag_mix = ag_start("ag_start_mix", mix_packed, g_in)
    w_in_f = g_in.transpose(1, 0, 2).reshape(d, in_cols)
    w_in_p = jnp.concatenate([w_in_f[:, :src_kpe], w_in_f[:, src_kpe + QK_ROPE:], w_in_f[:, src_kpe:src_kpe + QK_ROPE],
                              jnp.zeros((d, zp - in_cols), BF16)], axis=1)

    inv = ROPE_THETA ** (-jnp.arange(0, QK_ROPE, 2, dtype=F32) / QK_ROPE)
    ang = positions[0].astype(F32)[:, None] * inv
    cos4 = jnp.tile(jnp.cos(ang), (1, 4))
    sin4 = jnp.tile(jnp.concatenate([-jnp.sin(ang), jnp.sin(ang)], axis=1), (1, 2))

    wm = (gm_w_s * jnp.tril(jnp.ones((CHUNK, CHUNK), F32))).astype(BF16)
    bs3 = gm_b_s.reshape(groups, CHUNK, 1)
    ln_g, ln_b = row(gm_ln_g), row(gm_ln_b)

    z = mm_nn("z_proj", h1, w_in_p, F32)
    a = gmlp_fwd(z, gmw, ln_g, ln_b, wm, bs3)
    g_mix = ag_wait("ag_wait_mix", ag_forward("ag_forward_mix", ag_mix, a), a).reshape(N_DEV, -1)
    ag_up = ag_start("ag_start_up", sh["w_up"], g_mix)
    offs = [sum(mix_sizes[:i]) for i in range(len(mix) + 1)]
    g_a, g_uq, g_ukv, g_b, g_out = (g_mix[:, offs[i]:offs[i + 1]].reshape((N_DEV,) + sh[k].shape) for i, k in enumerate(mix))
    w_a_f, w_b_f, w_out_f = g_a.reshape(-1, d), g_b.reshape(-1, d), g_out.reshape(-1, d)
    w_uq_f = g_uq.transpose(1, 0, 2).reshape(ql, HEADS, QK_NOPE + QK_ROPE)
    w_uq_n = w_uq_f[:, :, :QK_NOPE].reshape(ql, HEADS * QK_NOPE)
    w_uq_r = w_uq_f[:, :, QK_NOPE:].reshape(ql, HEADS * QK_ROPE)
    y_a = mm_nn("branch_a", a, w_a_f, F32)
    qln = rms_fwd_cols("q_norm", z, o_q, ql, row(q_norm_g))
    kvn = rms_fwd_cols("kv_norm", z, o_kv, kvl, row(kv_norm_g))
    qn = mm_nn("q_nope", qln, w_uq_n, BF16)
    qp = mm_nn("q_rope", qln, w_uq_r, F32)
    kv = mm_nn_b3("kv_up", kvn, g_ukv, BF16)
    kpr = rope_k(z, o_kpe, cos4, sin4)
    o, qpr, lse = attn_fwd2(qn, qp, kv, kpr, cos4, sin4)
    y_b = mm_nn("branch_b", o, w_b_f, F32)
    merged = merge_fwd(z, o_ga, o_gb, y_a, y_b)
    y1 = mm_nn("out_proj", merged, w_out_f, F32)
    x1 = post_res_fwd("post1_fwd", x2, y1, gate1, row(post_norm1_g))
    h2 = norm_mod_fwd("pre2_fwd", x1, row(pre_norm2_g), scale2, shift2)
    g_up = ag_wait("ag_wait_up", ag_forward("ag_forward_up", ag_up, h2), h2)
    ag_down = ag_start("ag_start_down", sh["w_down"], g_up)
    upre = mm_nn_b3("up_proj", h2, g_up, F32)
    cw = g_cw.transpose(1, 0, 2).reshape(3, f2)
    cb = row(conv_b)
    f = conv_act_fwd(upre, cw, cb)
    w_down_f = ag_wait("ag_wait_down", ag_forward("ag_forward_down", ag_down, f), f).reshape(-1, d)
    ffn = mm_nn("down_proj", f, w_down_f, F32)
    loss_acc, dout, dffn, acc2 = post2_loss_bwd(x1, ffn, tgt, gate2, row(post_norm2_g))

    blocks = lambda g: g.reshape(N_DEV, g.shape[0] // N_DEV, g.shape[1])
    core = lax.axis_index("c").astype(jnp.int32).reshape(1)
    rs = {}

    def rs_begin(key, grads):
        rs[key] = rs_d2d_start("rs_d2d_start_" + key, grads)

    def rs_middle(key, after):
        grads, lands = rs_d2d_wait("rs_d2d_wait_" + key, rs[key], after)
        sums = [pair_sum("pair_sum_%s_%d" % (key, i), g, l, core) for i, (g, l) in enumerate(zip(grads, lands))]
        rs[key] = rs_ici_start("rs_ici_start_" + key, sums)

    gw_down = mm_tn("g_w_down", f, dffn, BF16)
    rs_begin("down", [blocks(gw_down)])
    df = mm_nt("d_f", dffn, w_down_f, F32)
    rs_middle("down", df)
    dup_g, dup_v, gcw_g, gcw_v, gcb_g, gcb_v = conv_act_bwd(upre, cw, cb, df)
    dupre = jnp.concatenate([dup_g, dup_v], axis=1)
    gw_up3 = mm_tn_o3("g_w_up", h2, dupre, N_DEV, BF16)
    rs_begin("up", [gw_up3])
    dh2 = mm_nt_b3("d_h2", dupre, g_up, F32)
    rs_middle("up", dh2)
    dx1, dy1, acc_mid = mid_bwd(dh2, dout, x1, y1, row(pre_norm2_g), scale2, gate1, row(post_norm1_g))
    gw_out = mm_tn("g_w_out", merged, dy1, BF16)
    dmerged = mm_nt("d_merged", dy1, w_out_f, F32)
    dya, dyb, dga, dgb = merge_bwd(z, o_ga, o_gb, y_a, y_b, dmerged)
    gw_a = mm_tn("g_w_a", a, dya, BF16)
    gw_b = mm_tn("g_w_b", o, dyb, BF16)
    rs_begin("mid", [jnp.concatenate([blocks(gw_out), blocks(gw_a), blocks(gw_b)], axis=1)])
    da = mm_nt("d_a", dya, w_a_f, F32)
    do = mm_nt("d_o", dyb, w_b_f, F32)
    rs_middle("mid", do)
    duv, g_ws, g_bs3, acc_gm = gmlp_bwd(z, gmw, da, ln_g, ln_b, wm, bs3)
    dqn, dqp, dkv, dkp = attn_bwd2(qn, qpr, kv, kpr, o, do, lse, cos4, sin4)
    dkpe = kpe_bwd(dkp, cos4, sin4, zp - o_kpe)
    dq_cat = jnp.concatenate([dqn, dqp], axis=1)
    w_uq_cat = jnp.concatenate([w_uq_n, w_uq_r], axis=1)
    gw_uq_cat = mm_tn("g_w_uq", qln, dq_cat, BF16)
    gw_uq_f = jnp.concatenate([gw_uq_cat[:, :HEADS * QK_NOPE].reshape(ql, HEADS, QK_NOPE),
                               gw_uq_cat[:, HEADS * QK_NOPE:].reshape(ql, HEADS, QK_ROPE)], axis=2)
    gw_uq3 = gw_uq_f.reshape(ql, N_DEV, -1).transpose(1, 0, 2)
    gw_ukv3 = mm_tn_o3("g_w_ukv", kvn, dkv, N_DEV, BF16)
    rs_begin("mla", [gw_uq3, gw_ukv3])
    dqln = mm_nt("d_qln", dq_cat, w_uq_cat, F32)
    dq_lat, g_qnorm = rms_bwd_cols("q_norm_bwd", dqln, z, o_q, ql, row(q_norm_g))
    dkvn = mm_nt_b3("d_kvn", dkv, g_ukv, F32)
    rs_middle("mla", dkvn)
    dkv_lat, g_kvnorm = rms_bwd_cols("kv_norm_bwd", dkvn, z, o_kv, kvl, row(kv_norm_g))
    dz = jnp.concatenate([duv, dq_lat, dkv_lat, dga, dgb, dkpe], axis=1)
    gw_in_p = mm_tn("g_w_in", h1, dz, BF16)
    gw_in_f = jnp.concatenate([gw_in_p[:, :src_kpe], gw_in_p[:, o_kpe:o_kpe + QK_ROPE], gw_in_p[:, src_kpe:o_kpe]], axis=1)
    gw_in3 = gw_in_f.reshape(d, N_DEV, -1).transpose(1, 0, 2)
    rs_begin("in", [gw_in3])
    dh1 = mm_nt("d_h1", dz, w_in_p, F32)
    grad_x, acc1 = pre1_bwd(dh1, dx1, x2, row(pre_norm1_g), scale1)

    dmod = jnp.concatenate([acc1[0], acc1[1], acc_mid[3], acc_mid[0], acc_mid[1], acc2[0]])
    small = [("pre_norm1_g", acc1[2]), ("gm_ln_g", acc_gm[0]), ("gm_ln_b", acc_gm[1]), ("gm_b_s", g_bs3.reshape(-1)),
             ("q_norm_g", g_qnorm[0]), ("kv_norm_g", g_kvnorm[0]), ("post_norm1_g", acc_mid[4]), ("pre_norm2_g", acc_mid[2]),
             ("conv_b", jnp.concatenate([gcb_g[0], gcb_v[0]])), ("post_norm2_g", acc2[1]), ("gm_w_s", g_ws.reshape(-1)),
             ("b_ada", dmod)]
    n_small = sum(v.shape[0] for _, v in small)
    n_cw = 3 * f2
    n_pack = _round_up(n_small + n_cw, PACK_ALIGN)
    tail = jnp.zeros((n_pack - n_small - n_cw,), F32)
    packed = jnp.concatenate([v for _, v in small] + [jnp.concatenate([gcw_g, gcw_v], axis=1).reshape(-1), tail])
    ag_small = ag_start("ag_start_small", packed.reshape(-1, LANES), packed)
    rs_middle("in", packed)

    res = {}
    last = packed
    for key, names in (("down", ["w_down"]), ("up", ["w_up"]), ("mid", ["w_out", "w_branch_a", "w_branch_b"]), ("mla", ["w_uq", "w_ukv"])):
        parts = rs_ici_wait("rs_ici_wait_" + key, rs[key], last)
        for i, k in enumerate(names):
            packed_rows = key == "mid"
            res[k] = adamw("adamw_" + k, weights[k], mom1[k], mom2[k], parts[0 if packed_rows else i],
                           row_off=sum(weights[n].shape[0] for n in names[:i]) if packed_rows else 0)
            last = res[k][0]

    def pack(src):
        return jnp.concatenate([src[k].reshape(-1) for k, _ in small] + [jnp.zeros((n_pack - n_small,), F32)]).reshape(-1, LANES)

    gathered = ag_wait("ag_wait_small", ag_forward("ag_forward_small", ag_small, last), last)
    sm = [t.reshape(-1) for t in adamw("adamw_small", pack(weights), pack(mom1), pack(mom2), gathered)]
    off = 0
    for k, v in small:
        res[k] = tuple(t[off:off + v.shape[0]].reshape(weights[k].shape) for t in sm)
        off += v.shape[0]

    cs_cw = conv_w.shape[1]
    g_cw_full = sm[0][n_small:n_small + n_cw].reshape(3, f2)
    g_cw_mine = lax.dynamic_slice(g_cw_full, (0, my * cs_cw), (3, cs_cw))
    res["conv_w"] = adamw("adamw_conv_w", conv_w, mom1["conv_w"], mom2["conv_w"], g_cw_mine[None])

    cs_ada = w_ada.shape[1]
    off_b = n_small - N_MOD * d
    dmod_all = gathered.reshape(N_DEV, -1)[:, off_b:off_b + N_MOD * d]
    dmod_cols = lax.dynamic_slice(dmod_all, (0, my * cs_ada), (N_DEV, cs_ada))
    res["w_ada"] = adamw("adamw_w_ada", w_ada, mom1["w_ada"], mom2["w_ada"], ada_bwd_w(sc_all, dmod_cols))

    (p_in,) = rs_ici_wait("rs_ici_wait_in", rs["in"], res["w_ada"][0])
    res["w_in"] = adamw("adamw_w_in", w_in, mom1["w_in"], mom2["w_in"], p_in)

    _TOKENS.take()
    loss = lax.psum(loss_acc[0, 0], ("x", "y", "c"))
    outs = [loss, grad_x[None]]
    for i in range(4):
        outs += [res[k][i] for k in order]
    return tuple(outs)
```

```python
import functools

import jax
import jax.numpy as jnp
from jax import lax
from jax.experimental import pallas as pl
from jax.experimental.pallas import tpu as pltpu

F32 = jnp.float32
BF16 = jnp.bfloat16

N_DEV = 8
HEADS = 16
QK_NOPE = 128
QK_ROPE = 64
V_HEAD = 128
CHUNK = 128
ROPE_THETA = 10000.0
EPS = 1e-6
N_MOD = 6
ADAM_LR, ADAM_B1, ADAM_B2, ADAM_EPS, ADAM_WD, ADAM_STEP = 0.001, 0.9, 0.999, 1e-08, 0.01, 10

LANES = 128
VMEM_LIMIT_BYTES = 48 * 2 ** 20
ROW_TILE = 256
COL_TILE = 256
ATT_TILE = 256
Z_PAD = 512
ADAMW_TILE_ELEMS = 1 << 18
PACK_ALIGN = 8 * LANES
MESH = pl.DeviceIdType.MESH


def _params(*sem):
    return pltpu.CompilerParams(dimension_semantics=sem if sem else None, vmem_limit_bytes=VMEM_LIMIT_BYTES)


def _tile(dim, target):
    t = (min(dim, target) // LANES) * LANES
    while t >= LANES:
        if dim % t == 0:
            return t
        t -= LANES
    return dim


def _full(shape):
    nd = len(shape)
    return pl.BlockSpec(shape, lambda *_: (0,) * nd)


class _Tokens:
    KEEP = 2

    def __init__(self):
        self.pending = []

    def push(self, token):
        self.pending = (self.pending + [token])[-self.KEEP:]

    def take(self):
        return list(self.pending)

    def clear(self):
        self.pending = []


_TOKENS = _Tokens()


def _matmul(name, a, b, *, grid, a_spec, b_spec, o_spec, out_shape, contract, acc_shape):
    nk = grid[2]
    deps = _TOKENS.take()

    def body(a_ref, b_ref, *rest):
        o_ref, acc_ref = rest[len(deps):]
        k = pl.program_id(2)

        @pl.when(k == 0)
        def _():
            acc_ref[...] = jnp.zeros_like(acc_ref)

        acc_ref[...] += lax.dot_general(a_ref[...].astype(BF16), b_ref[...].astype(BF16),
                                        (contract, ((), ())), preferred_element_type=F32)

        @pl.when(k == nk - 1)
        def _():
            o_ref[...] = acc_ref[...].astype(o_ref.dtype)

    return pl.pallas_call(
        body, name=name, grid=grid, in_specs=[a_spec, b_spec] + [pl.BlockSpec(memory_space=pl.ANY)] * len(deps),
        out_specs=o_spec, out_shape=out_shape, scratch_shapes=[pltpu.VMEM(acc_shape, F32)],
        compiler_params=_params("parallel", "parallel", "arbitrary"))(a, b, *deps)


TM, TN, TK = 1024, 1024, 512


def mm_nn(name, a, b, dtype):
    (m, k), n = a.shape, b.shape[1]
    tm, tn, tk = _tile(m, TM), _tile(n, TN), _tile(k, TK)
    return _matmul(name, a, b, grid=(m // tm, n // tn, k // tk),
                   a_spec=pl.BlockSpec((tm, tk), lambda i, j, kk: (i, kk)),
                   b_spec=pl.BlockSpec((tk, tn), lambda i, j, kk: (kk, j)),
                   o_spec=pl.BlockSpec((tm, tn), lambda i, j, kk: (i, j)),
                   out_shape=jax.ShapeDtypeStruct((m, n), dtype), contract=((1,), (0,)), acc_shape=(tm, tn))


def mm_nn_b3(name, a, b3, dtype):
    (m, k), (nj, _, cs) = a.shape, b3.shape
    tm, tk = _tile(m, TM), _tile(k, TK)
    return _matmul(name, a, b3, grid=(m // tm, nj, k // tk),
                   a_spec=pl.BlockSpec((tm, tk), lambda i, j, kk: (i, kk)),
                   b_spec=pl.BlockSpec((None, tk, cs), lambda i, j, kk: (j, kk, 0)),
                   o_spec=pl.BlockSpec((tm, cs), lambda i, j, kk: (i, j)),
                   out_shape=jax.ShapeDtypeStruct((m, nj * cs), dtype), contract=((1,), (0,)), acc_shape=(tm, cs))


def mm_nt(name, a, b, dtype):
    (m, k), n = a.shape, b.shape[0]
    tm, tn, tk = _tile(m, TM), _tile(n, TN), _tile(k, TK)
    return _matmul(name, a, b, grid=(m // tm, n // tn, k // tk),
                   a_spec=pl.BlockSpec((tm, tk), lambda i, j, kk: (i, kk)),
                   b_spec=pl.BlockSpec((tn, tk), lambda i, j, kk: (j, kk)),
                   o_spec=pl.BlockSpec((tm, tn), lambda i, j, kk: (i, j)),
                   out_shape=jax.ShapeDtypeStruct((m, n), dtype), contract=((1,), (1,)), acc_shape=(tm, tn))


def mm_nt_b3(name, a, b3, dtype):
    m, (nj, n, cs) = a.shape[0], b3.shape
    tm, tn = _tile(m, TM), _tile(n, TN)
    return _matmul(name, a, b3, grid=(m // tm, n // tn, nj),
                   a_spec=pl.BlockSpec((tm, cs), lambda i, j, kk: (i, kk)),
                   b_spec=pl.BlockSpec((None, tn, cs), lambda i, j, kk: (kk, j, 0)),
                   o_spec=pl.BlockSpec((tm, tn), lambda i, j, kk: (i, j)),
                   out_shape=jax.ShapeDtypeStruct((m, n), dtype), contract=((1,), (1,)), acc_shape=(tm, tn))


def mm_tn(name, a, b, dtype):
    (k, m), n = a.shape, b.shape[1]
    tm, tn, tk = _tile(m, TM), _tile(n, TN), _tile(k, TK)
    return _matmul(name, a, b, grid=(m // tm, n // tn, k // tk),
                   a_spec=pl.BlockSpec((tk, tm), lambda i, j, kk: (kk, i)),
                   b_spec=pl.BlockSpec((tk, tn), lambda i, j, kk: (kk, j)),
                   o_spec=pl.BlockSpec((tm, tn), lambda i, j, kk: (i, j)),
                   out_shape=jax.ShapeDtypeStruct((m, n), dtype), contract=((0,), (0,)), acc_shape=(tm, tn))


def mm_tn_o3(name, a, b, nj, dtype):
    (k, m), n = a.shape, b.shape[1]
    cs = n // nj
    tm, tk = _tile(m, TM), _tile(k, TK)
    return _matmul(name, a, b, grid=(m // tm, nj, k // tk),
                   a_spec=pl.BlockSpec((tk, tm), lambda i, j, kk: (kk, i)),
                   b_spec=pl.BlockSpec((tk, cs), lambda i, j, kk: (kk, j)),
                   o_spec=pl.BlockSpec((None, tm, cs), lambda i, j, kk: (j, i, 0)),
                   out_shape=jax.ShapeDtypeStruct((nj, m, cs), dtype), contract=((0,), (0,)), acc_shape=(tm, cs))


_GELU_C = 0.7978845608028654
_GELU_A = 0.044715


def _gelu(x):
    return 0.5 * x * (1.0 + jnp.tanh(_GELU_C * (x + _GELU_A * x * x * x)))


def _gelu_and_grad(x):
    t = jnp.tanh(_GELU_C * (x + _GELU_A * x * x * x))
    y = 0.5 * x * (1.0 + t)
    dy = 0.5 * (1.0 + t) + 0.5 * x * (1.0 - t * t) * (_GELU_C * (1.0 + 3.0 * _GELU_A * x * x))
    return y, dy


def _sigmoid(x):
    return 1.0 / (1.0 + jnp.exp(-x))


def _rms_stats(x):
    inv = lax.rsqrt(jnp.mean(x * x, axis=-1, keepdims=True) + EPS)
    return inv, x * inv


def _rms_bwd(dyhat, yhat, inv):
    return inv * (dyhat - yhat * jnp.mean(dyhat * yhat, axis=-1, keepdims=True))


def _colsum(x):
    return jnp.sum(x, axis=0, keepdims=True)


def _rope(x, cos4, sin4):
    lane = lax.broadcasted_iota(jnp.int32, x.shape, x.ndim - 1)
    first_half = (lane % QK_ROPE) < (QK_ROPE // 2)
    partner = jnp.where(first_half, pltpu.roll(x, LANES - QK_ROPE // 2, x.ndim - 1), pltpu.roll(x, QK_ROPE // 2, x.ndim - 1))
    return x * cos4 + partner * sin4


def norm_mod_fwd(name, x, g, scale, shift):
    s, d = x.shape
    tr = _tile(s, ROW_TILE)

    def body(x_ref, g_ref, sc_ref, sh_ref, o_ref):
        _, xh = _rms_stats(x_ref[...])
        o_ref[...] = (xh * g_ref[...] * (1.0 + sc_ref[...]) + sh_ref[...]).astype(o_ref.dtype)

    row = pl.BlockSpec((tr, d), lambda i: (i, 0))
    vec = pl.BlockSpec((1, d), lambda i: (0, 0))
    return pl.pallas_call(body, name=name, grid=(s // tr,), in_specs=[row, vec, vec, vec], out_specs=row,
                          out_shape=jax.ShapeDtypeStruct((s, d), BF16), compiler_params=_params("parallel"))(x, g, scale, shift)


def rms_fwd_cols(name, z, off, width, g):
    s = z.shape[0]
    tr = _tile(s, ROW_TILE)
    assert off % width == 0

    def body(x_ref, g_ref, o_ref):
        _, xh = _rms_stats(x_ref[...])
        o_ref[...] = (xh * g_ref[...]).astype(o_ref.dtype)

    return pl.pallas_call(body, name=name, grid=(s // tr,),
                          in_specs=[pl.BlockSpec((tr, width), lambda i: (i, off // width)), pl.BlockSpec((1, width), lambda i: (0, 0))],
                          out_specs=pl.BlockSpec((tr, width), lambda i: (i, 0)),
                          out_shape=jax.ShapeDtypeStruct((s, width), BF16), compiler_params=_params("parallel"))(z, g)


def rms_bwd_cols(name, dy, z, off, width, g):
    s = z.shape[0]
    tr = _tile(s, ROW_TILE)

    def body(dy_ref, x_ref, g_ref, dx_ref, gg_ref):
        @pl.when(pl.program_id(0) == 0)
        def _():
            gg_ref[...] = jnp.zeros_like(gg_ref)

        inv, xh = _rms_stats(x_ref[...])
        dy_v = dy_ref[...]
        gg_ref[...] += _colsum(dy_v * xh)
        dx_ref[...] = _rms_bwd(dy_v * g_ref[...], xh, inv).astype(dx_ref.dtype)

    return pl.pallas_call(body, name=name, grid=(s // tr,),
                          in_specs=[pl.BlockSpec((tr, width), lambda i: (i, 0)), pl.BlockSpec((tr, width), lambda i: (i, off // width)),
                                    pl.BlockSpec((1, width), lambda i: (0, 0))],
                          out_specs=[pl.BlockSpec((tr, width), lambda i: (i, 0)), pl.BlockSpec((1, width), lambda i: (0, 0))],
                          out_shape=[jax.ShapeDtypeStruct((s, width), BF16), jax.ShapeDtypeStruct((1, width), F32)],
                          compiler_params=_params("arbitrary"))(dy, z, g)


def post_res_fwd(name, x, y, gate, g):
    s, d = x.shape
    tr = _tile(s, ROW_TILE)

    def body(x_ref, y_ref, gate_ref, g_ref, o_ref):
        _, yh = _rms_stats(y_ref[...])
        o_ref[...] = x_ref[...] + gate_ref[...] * (yh * g_ref[...])

    row = pl.BlockSpec((tr, d), lambda i: (i, 0))
    vec = pl.BlockSpec((1, d), lambda i: (0, 0))
    return pl.pallas_call(body, name=name, grid=(s // tr,), in_specs=[row, row, vec, vec], out_specs=row,
                          out_shape=jax.ShapeDtypeStruct((s, d), F32), compiler_params=_params("parallel"))(x, y, gate, g)


def post2_loss_bwd(x1, ffn, target, gate2, g):
    s, d = x1.shape
    tr = _tile(s, ROW_TILE)

    def body(x_ref, y_ref, t_ref, gate_ref, g_ref, loss_ref, dout_ref, dy_ref, acc_ref):
        @pl.when(pl.program_id(0) == 0)
        def _():
            loss_ref[...] = jnp.zeros_like(loss_ref)
            acc_ref[...] = jnp.zeros_like(acc_ref)

        inv, yh = _rms_stats(y_ref[...])
        r = yh * g_ref[...]
        err = x_ref[...] + gate_ref[...] * r - t_ref[...]
        loss_ref[...] += 0.5 * jnp.sum(jnp.mean(err * err, axis=-1, keepdims=True))
        dout = err / d
        dout_ref[...] = dout
        dr = dout * gate_ref[...]
        acc_ref[0:1, :] += _colsum(dout * r)
        acc_ref[1:2, :] += _colsum(dr * yh)
        dy_ref[...] = _rms_bwd(dr * g_ref[...], yh, inv).astype(dy_ref.dtype)

    row = pl.BlockSpec((tr, d), lambda i: (i, 0))
    vec = pl.BlockSpec((1, d), lambda i: (0, 0))
    return pl.pallas_call(
        body, name="post2_loss_bwd", grid=(s // tr,), in_specs=[row, row, row, vec, vec],
        out_specs=[_full((8, LANES)), row, row, _full((8, d))],
        out_shape=[jax.ShapeDtypeStruct((8, LANES), F32), jax.ShapeDtypeStruct((s, d), F32),
                   jax.ShapeDtypeStruct((s, d), BF16), jax.ShapeDtypeStruct((8, d), F32)],
        compiler_params=_params("arbitrary"))(x1, ffn, target, gate2, g)


def mid_bwd(dh2, dout, x1, y1, pre2_g, scale2, gate1, post1_g):
    s, d = x1.shape
    tr = _tile(s, ROW_TILE)

    def body(dh_ref, dout_ref, x_ref, y_ref, g2_ref, sc_ref, gate_ref, g1_ref, dx_ref, dy_ref, acc_ref):
        @pl.when(pl.program_id(0) == 0)
        def _():
            acc_ref[...] = jnp.zeros_like(acc_ref)

        dh = dh_ref[...]
        inv2, xh = _rms_stats(x_ref[...])
        acc_ref[0:1, :] += _colsum(dh)
        acc_ref[1:2, :] += _colsum(dh * (xh * g2_ref[...]))
        t = dh * (1.0 + sc_ref[...])
        acc_ref[2:3, :] += _colsum(t * xh)
        dx1 = dout_ref[...] + _rms_bwd(t * g2_ref[...], xh, inv2)
        dx_ref[...] = dx1
        inv1, yh = _rms_stats(y_ref[...])
        acc_ref[3:4, :] += _colsum(dx1 * (yh * g1_ref[...]))
        dr = dx1 * gate_ref[...]
        acc_ref[4:5, :] += _colsum(dr * yh)
        dy_ref[...] = _rms_bwd(dr * g1_ref[...], yh, inv1).astype(dy_ref.dtype)

    row = pl.BlockSpec((tr, d), lambda i: (i, 0))
    vec = pl.BlockSpec((1, d), lambda i: (0, 0))
    return pl.pallas_call(
        body, name="mid_bwd", grid=(s // tr,), in_specs=[row, row, row, row, vec, vec, vec, vec],
        out_specs=[row, row, _full((8, d))],
        out_shape=[jax.ShapeDtypeStruct((s, d), F32), jax.ShapeDtypeStruct((s, d), BF16), jax.ShapeDtypeStruct((8, d), F32)],
        compiler_params=_params("arbitrary"))(dh2, dout, x1, y1, pre2_g, scale2, gate1, post1_g)


def pre1_bwd(dh1, dx1, x, pre1_g, scale1):
    s, d = x.shape
    tr = _tile(s, ROW_TILE)

    def body(dh_ref, dx1_ref, x_ref, g_ref, sc_ref, dx_ref, acc_ref):
        @pl.when(pl.program_id(0) == 0)
        def _():
            acc_ref[...] = jnp.zeros_like(acc_ref)

        dh = dh_ref[...]
        inv, xh = _rms_stats(x_ref[...])
        acc_ref[0:1, :] += _colsum(dh)
        acc_ref[1:2, :] += _colsum(dh * (xh * g_ref[...]))
        t = dh * (1.0 + sc_ref[...])
        acc_ref[2:3, :] += _colsum(t * xh)
        dx_ref[...] = dx1_ref[...] + _rms_bwd(t * g_ref[...], xh, inv)

    row = pl.BlockSpec((tr, d), lambda i: (i, 0))
    vec = pl.BlockSpec((1, d), lambda i: (0, 0))
    return pl.pallas_call(
        body, name="pre1_bwd", grid=(s // tr,), in_specs=[row, row, row, vec, vec], out_specs=[row, _full((8, d))],
        out_shape=[jax.ShapeDtypeStruct((s, d), F32), jax.ShapeDtypeStruct((8, d), F32)],
        compiler_params=_params("arbitrary"))(dh1, dx1, x, pre1_g, scale1)


def _ln_stats(v):
    mu = jnp.mean(v, axis=-1, keepdims=True)
    vc = v - mu
    rstd = lax.rsqrt(jnp.mean(vc * vc, axis=-1, keepdims=True) + EPS)
    return rstd, vc * rstd


def gmlp_fwd(z, width, ln_g, ln_b, wm, bs3):
    s = z.shape[0]
    groups = width // CHUNK

    def body(u_ref, v_ref, g_ref, b_ref, wm_ref, bs_ref, a_ref):
        ug = _gelu(u_ref[...])
        _, vh = _ln_stats(_gelu(v_ref[...]))
        vn = (vh * g_ref[...] + b_ref[...]).astype(BF16)
        for g in range(groups):
            cols = slice(g * CHUNK, (g + 1) * CHUNK)
            mixed = jnp.dot(wm_ref[g], vn[:, cols], preferred_element_type=F32) + bs_ref[g]
            a_ref[:, cols] = (ug[:, cols] * mixed).astype(a_ref.dtype)

    vec = pl.BlockSpec((1, width), lambda n: (0, 0))
    return pl.pallas_call(
        body, name="gmlp_fwd", grid=(s // CHUNK,),
        in_specs=[pl.BlockSpec((CHUNK, width), lambda n: (n, 0)), pl.BlockSpec((CHUNK, width), lambda n: (n, 1)), vec, vec,
                  _full(wm.shape), _full(bs3.shape)],
        out_specs=pl.BlockSpec((CHUNK, width), lambda n: (n, 0)),
        out_shape=jax.ShapeDtypeStruct((s, width), BF16), compiler_params=_params("parallel"))(z, z, ln_g, ln_b, wm, bs3)


def gmlp_bwd(z, width, da, ln_g, ln_b, wm, bs3):
    s = z.shape[0]
    groups = width // CHUNK

    def body(u_ref, v_ref, da_ref, g_ref, b_ref, wm_ref, bs_ref, duv_ref, gw_ref, gb_ref, acc_ref, dvn_ref):
        @pl.when(pl.program_id(0) == 0)
        def _():
            gw_ref[...] = jnp.zeros_like(gw_ref)
            gb_ref[...] = jnp.zeros_like(gb_ref)
            acc_ref[...] = jnp.zeros_like(acc_ref)

        ug, dug = _gelu_and_grad(u_ref[...])
        vg, dvg = _gelu_and_grad(v_ref[...])
        rstd, vh = _ln_stats(vg)
        vn = (vh * g_ref[...] + b_ref[...]).astype(BF16)
        da_v = da_ref[...]
        for g in range(groups):
            cols = slice(g * CHUNK, (g + 1) * CHUNK)
            mixed = jnp.dot(wm_ref[g], vn[:, cols], preferred_element_type=F32) + bs_ref[g]
            duv_ref[:, cols] = (da_v[:, cols] * mixed * dug[:, cols]).astype(duv_ref.dtype)
            dm = da_v[:, cols] * ug[:, cols]
            gb_ref[g] += jnp.sum(dm, axis=-1, keepdims=True)
            dmb = dm.astype(BF16)
            gw_ref[g] += lax.dot_general(dmb, vn[:, cols], (((1,), (1,)), ((), ())), preferred_element_type=F32)
            dvn_ref[:, cols] = lax.dot_general(wm_ref[g], dmb, (((0,), (0,)), ((), ())), preferred_element_type=F32)
        dvn = dvn_ref[...]
        acc_ref[0:1, :] += _colsum(dvn * vh)
        acc_ref[1:2, :] += _colsum(dvn)
        dvh = dvn * g_ref[...]
        dv = rstd * (dvh - jnp.mean(dvh, axis=-1, keepdims=True) - vh * jnp.mean(dvh * vh, axis=-1, keepdims=True))
        duv_ref[:, width:] = (dv * dvg).astype(duv_ref.dtype)

        @pl.when(pl.program_id(0) == pl.num_programs(0) - 1)
        def _():
            q = lax.broadcasted_iota(jnp.int32, gw_ref.shape, 1)
            p = lax.broadcasted_iota(jnp.int32, gw_ref.shape, 2)
            gw_ref[...] = jnp.where(p <= q, gw_ref[...], 0.0)

    vec = pl.BlockSpec((1, width), lambda n: (0, 0))
    blk = pl.BlockSpec((CHUNK, width), lambda n: (n, 0))
    return pl.pallas_call(
        body, name="gmlp_bwd", grid=(s // CHUNK,),
        in_specs=[blk, pl.BlockSpec((CHUNK, width), lambda n: (n, 1)), blk, vec, vec, _full(wm.shape), _full(bs3.shape)],
        out_specs=[pl.BlockSpec((CHUNK, 2 * width), lambda n: (n, 0)), _full(wm.shape), _full(bs3.shape), _full((8, width))],
        out_shape=[jax.ShapeDtypeStruct((s, 2 * width), BF16), jax.ShapeDtypeStruct(wm.shape, F32),
                   jax.ShapeDtypeStruct(bs3.shape, F32), jax.ShapeDtypeStruct((8, width), F32)],
        scratch_shapes=[pltpu.VMEM((CHUNK, width), F32)],
        compiler_params=_params("arbitrary"))(z, z, da, ln_g, ln_b, wm, bs3)


def merge_fwd(z, off_a, off_b, ya, yb):
    s, d = ya.shape
    tr, tc = _tile(s, ROW_TILE * 2), _tile(d, COL_TILE)
    assert off_a % tc == 0 and off_b % tc == 0

    def body(ga_ref, gb_ref, ya_ref, yb_ref, o_ref):
        o_ref[...] = (_sigmoid(ga_ref[...]) * ya_ref[...] + _sigmoid(gb_ref[...]) * yb_ref[...]).astype(o_ref.dtype)

    blk = pl.BlockSpec((tr, tc), lambda i, j: (i, j))
    return pl.pallas_call(
        body, name="merge_fwd", grid=(s // tr, d // tc),
        in_specs=[pl.BlockSpec((tr, tc), lambda i, j: (i, off_a // tc + j)), pl.BlockSpec((tr, tc), lambda i, j: (i, off_b // tc + j)), blk, blk],
        out_specs=blk, out_shape=jax.ShapeDtypeStruct((s, d), BF16), compiler_params=_params("parallel", "parallel"))(z, z, ya, yb)


def merge_bwd(z, off_a, off_b, ya, yb, dm):
    s, d = ya.shape
    tr, tc = _tile(s, ROW_TILE * 2), _tile(d, COL_TILE)
    nc = d // tc

    def body(ga_ref, gb_ref, ya_ref, yb_ref, dm_ref, dya_ref, dyb_ref, dga_ref, dgb_ref):
        dm_v = dm_ref[...]
        sa, sb = _sigmoid(ga_ref[...]), _sigmoid(gb_ref[...])
        dya_ref[...] = (dm_v * sa).astype(dya_ref.dtype)
        dyb_ref[...] = (dm_v * sb).astype(dyb_ref.dtype)
        dga_ref[...] = (dm_v * ya_ref[...] * sa * (1.0 - sa)).astype(dga_ref.dtype)
        dgb_ref[...] = (dm_v * yb_ref[...] * sb * (1.0 - sb)).astype(dgb_ref.dtype)

    blk = pl.BlockSpec((tr, tc), lambda i, j: (i, j))
    out = jax.ShapeDtypeStruct((s, d), BF16)
    return pl.pallas_call(
        body, name="merge_bwd", grid=(s // tr, nc),
        in_specs=[pl.BlockSpec((tr, tc), lambda i, j: (i, off_a // tc + j)), pl.BlockSpec((tr, tc), lambda i, j: (i, off_b // tc + j)), blk, blk, blk],
        out_specs=[blk, blk, blk, blk], out_shape=[out, out, out, out],
        compiler_params=_params("parallel", "parallel"))(z, z, ya, yb, dm)


_ATT_SCALE = (QK_NOPE + QK_ROPE) ** -0.5
_NEG = -1e30


def rope_k(z, off, cos4, sin4):
    s = z.shape[0]
    tr = _tile(s, ROW_TILE * 2)
    assert off % LANES == 0

    def body(k_ref, c_ref, s_ref, o_ref):
        k = k_ref[...]
        k = k + pltpu.roll(k, QK_ROPE, 1)
        o_ref[...] = _rope(k, c_ref[...], s_ref[...]).astype(o_ref.dtype)

    row = pl.BlockSpec((tr, LANES), lambda i: (i, 0))
    return pl.pallas_call(body, name="rope_k", grid=(s // tr,),
                          in_specs=[pl.BlockSpec((tr, LANES), lambda i: (i, off // LANES)), row, row], out_specs=row,
                          out_shape=jax.ShapeDtypeStruct((s, LANES), BF16), compiler_params=_params("parallel"))(z, cos4, sin4)


def _head_masks(shape):
    lane = lax.broadcasted_iota(jnp.int32, shape, 1)
    return lane < QK_ROPE, lane >= QK_ROPE


def _scores(qn, qp_h, k, kp, qi, kb, t):
    sc = lax.dot_general(qn, k, (((1,), (1,)), ((), ())), preferred_element_type=F32)
    sc += lax.dot_general(qp_h, kp, (((1,), (1,)), ((), ())), preferred_element_type=F32)
    sc = sc * _ATT_SCALE
    row = lax.broadcasted_iota(jnp.int32, sc.shape, 0) + qi * t
    col = lax.broadcasted_iota(jnp.int32, sc.shape, 1) + kb * t
    return jnp.where(col <= row, sc, _NEG)


def attn_fwd(qn, qp, kv, kpr, cos4, sin4):
    s = qn.shape[0]
    hp = HEADS // 2
    t = _tile(s, ATT_TILE)
    nq = s // t

    def body(qn_ref, qp_ref, kv_ref, kp_ref, c_ref, s_ref, o_ref, qpr_ref, l_ref):
        qi = pl.program_id(1)
        qpr = _rope(qp_ref[...], c_ref[...], s_ref[...]).astype(BF16)
        qpr_ref[...] = qpr
        masks = _head_masks(qpr.shape)
        for hh in range(2):
            q_n = qn_ref[:, hh * QK_NOPE:(hh + 1) * QK_NOPE]
            q_p = jnp.where(masks[hh], qpr, jnp.zeros_like(qpr))
            kc, vc = 2 * hh * QK_NOPE, (2 * hh + 1) * QK_NOPE

            def step(kb, carry):
                m, l, acc = carry
                rows = pl.ds(pl.multiple_of(kb * t, t), t)
                sc = _scores(q_n, q_p, kv_ref[rows, kc:kc + QK_NOPE], kp_ref[rows, :], qi, kb, t)
                m_new = jnp.maximum(m, jnp.max(sc, axis=-1, keepdims=True))
                alpha = jnp.exp(m - m_new)
                p = jnp.exp(sc - m_new)
                l = alpha * l + jnp.sum(p, axis=-1, keepdims=True)
                acc = alpha * acc + jnp.dot(p.astype(BF16), kv_ref[rows, vc:vc + V_HEAD], preferred_element_type=F32)
                return m_new, l, acc

            init = (jnp.full((t, 1), _NEG, F32), jnp.zeros((t, 1), F32), jnp.zeros((t, V_HEAD), F32))
            m, l, acc = lax.fori_loop(0, qi + 1, step, init)
            o_ref[:, hh * V_HEAD:(hh + 1) * V_HEAD] = acc / l
            l_ref[:, hh:hh + 1] = m + jnp.log(l)

    return pl.pallas_call(
        body, name="attn_fwd", grid=(hp, nq),
        in_specs=[pl.BlockSpec((t, 2 * QK_NOPE), lambda h, i: (i, h)), pl.BlockSpec((t, LANES), lambda h, i: (i, h)),
                  pl.BlockSpec((s, 4 * QK_NOPE), lambda h, i: (0, h)), _full((s, LANES)),
                  pl.BlockSpec((t, LANES), lambda h, i: (i, 0)), pl.BlockSpec((t, LANES), lambda h, i: (i, 0))],
        out_specs=[pl.BlockSpec((t, 2 * V_HEAD), lambda h, i: (i, h)), pl.BlockSpec((t, LANES), lambda h, i: (i, h)),
                   pl.BlockSpec((None, t, 2), lambda h, i: (h, i, 0))],
        out_shape=[jax.ShapeDtypeStruct((s, HEADS * V_HEAD), F32), jax.ShapeDtypeStruct((s, HEADS * QK_ROPE), BF16),
                   jax.ShapeDtypeStruct((hp, s, 2), F32)],
        compiler_params=_params("parallel", "parallel"))(qn, qp, kv, kpr, cos4, sin4)


def attn_bwd_q(qn, qpr, kv, kpr, o, do, lse, cos4, sin4):
    s = qn.shape[0]
    hp = HEADS // 2
    t = _tile(s, ATT_TILE)
    nq = s // t

    def body(qn_ref, qpr_ref, kv_ref, kp_ref, o_ref, do_ref, l_ref, c_ref, s_ref, dqn_ref, dqp_ref):
        qi = pl.program_id(1)
        qpr = qpr_ref[...]
        masks = _head_masks(qpr.shape)
        dqp = jnp.zeros(qpr.shape, F32)
        for hh in range(2):
            q_n = qn_ref[:, hh * QK_NOPE:(hh + 1) * QK_NOPE]
            q_p = jnp.where(masks[hh], qpr, jnp.zeros_like(qpr))
            kc, vc = 2 * hh * QK_NOPE, (2 * hh + 1) * QK_NOPE
            do_h = do_ref[:, hh * V_HEAD:(hh + 1) * V_HEAD]
            delta = jnp.sum(do_h * o_ref[:, hh * V_HEAD:(hh + 1) * V_HEAD], axis=-1, keepdims=True)
            do_b = do_h.astype(BF16)
            lse_h = l_ref[:, hh:hh + 1]

            def step(kb, carry):
                dn, dp_ = carry
                rows = pl.ds(pl.multiple_of(kb * t, t), t)
                k = kv_ref[rows, kc:kc + QK_NOPE]
                kp = kp_ref[rows, :]
                p = jnp.exp(_scores(q_n, q_p, k, kp, qi, kb, t) - lse_h)
                dpv = lax.dot_general(do_b, kv_ref[rows, vc:vc + V_HEAD], (((1,), (1,)), ((), ())), preferred_element_type=F32)
                ds = (p * (dpv - delta) * _ATT_SCALE).astype(BF16)
                dn = dn + jnp.dot(ds, k, preferred_element_type=F32)
                dp_ = dp_ + jnp.dot(ds, kp, preferred_element_type=F32)
                return dn, dp_

            dn, dp_h = lax.fori_loop(0, qi + 1, step, (jnp.zeros((t, QK_NOPE), F32), jnp.zeros((t, LANES), F32)))
            dqn_ref[:, hh * QK_NOPE:(hh + 1) * QK_NOPE] = dn.astype(dqn_ref.dtype)
            dqp = dqp + jnp.where(masks[hh], dp_h, jnp.zeros_like(dp_h))
        dqp_ref[...] = _rope(dqp, c_ref[...], -s_ref[...]).astype(dqp_ref.dtype)

    qblk = pl.BlockSpec((t, 2 * QK_NOPE), lambda h, i: (i, h))
    pblk = pl.BlockSpec((t, LANES), lambda h, i: (i, h))
    tab = pl.BlockSpec((t, LANES), lambda h, i: (i, 0))
    return pl.pallas_call(
        body, name="attn_bwd_q", grid=(hp, nq),
        in_specs=[qblk, pblk, pl.BlockSpec((s, 4 * QK_NOPE), lambda h, i: (0, h)), _full((s, LANES)), qblk, qblk,
                  pl.BlockSpec((None, t, 2), lambda h, i: (h, i, 0)), tab, tab],
        out_specs=[qblk, pblk],
        out_shape=[jax.ShapeDtypeStruct((s, HEADS * QK_NOPE), BF16), jax.ShapeDtypeStruct((s, HEADS * QK_ROPE), BF16)],
        compiler_params=_params("parallel", "parallel"))(qn, qpr, kv, kpr, o, do, lse, cos4, sin4)


def attn_bwd_kv(qn, qpr, kv, kpr, o, do, lse):
    s = qn.shape[0]
    hp = HEADS // 2
    t = _tile(s, ATT_TILE)
    nq = s // t

    def body(qn_ref, qpr_ref, kv_ref, kp_ref, o_ref, do_ref, l_ref, dkv_ref, dkp_ref):
        ki = pl.program_id(1)
        rows_k = pl.ds(pl.multiple_of(ki * t, t), t)
        kp = kp_ref[rows_k, :]
        dkp = jnp.zeros((t, LANES), F32)
        for hh in range(2):
            kc, vc = 2 * hh * QK_NOPE, (2 * hh + 1) * QK_NOPE
            k = kv_ref[rows_k, kc:kc + QK_NOPE]
            v = kv_ref[rows_k, vc:vc + V_HEAD]

            def step(qb, carry):
                dk, dv, dkp_h = carry
                rows = pl.ds(pl.multiple_of(qb * t, t), t)
                q_n = qn_ref[rows, hh * QK_NOPE:(hh + 1) * QK_NOPE]
                qpr = qpr_ref[rows, :]
                lane = lax.broadcasted_iota(jnp.int32, qpr.shape, 1)
                sel = (lane < QK_ROPE) if hh == 0 else (lane >= QK_ROPE)
                q_p = jnp.where(sel, qpr, jnp.zeros_like(qpr))
                do_h = do_ref[rows, hh * V_HEAD:(hh + 1) * V_HEAD]
                delta = jnp.sum(do_h * o_ref[rows, hh * V_HEAD:(hh + 1) * V_HEAD], axis=-1, keepdims=True)
                do_b = do_h.astype(BF16)
                p = jnp.exp(_scores(q_n, q_p, k, kp, qb, ki, t) - l_ref[rows, hh:hh + 1])
                dpv = lax.dot_general(do_b, v, (((1,), (1,)), ((), ())), preferred_element_type=F32)
                ds = (p * (dpv - delta) * _ATT_SCALE).astype(BF16)
                dv = dv + lax.dot_general(p.astype(BF16), do_b, (((0,), (0,)), ((), ())), preferred_element_type=F32)
                dk = dk + lax.dot_general(ds, q_n, (((0,), (0,)), ((), ())), preferred_element_type=F32)
                dkp_h = dkp_h + lax.dot_general(ds, q_p, (((0,), (0,)), ((), ())), preferred_element_type=F32)
                return dk, dv, dkp_h

            init = (jnp.zeros((t, QK_NOPE), F32), jnp.zeros((t, V_HEAD), F32), jnp.zeros((t, LANES), F32))
            dk, dv, dkp_h = lax.fori_loop(ki, nq, step, init)
            dkv_ref[:, kc:kc + QK_NOPE] = dk.astype(dkv_ref.dtype)
            dkv_ref[:, vc:vc + V_HEAD] = dv.astype(dkv_ref.dtype)
            dkp = dkp + dkp_h
        dkp_ref[...] = dkp

    return pl.pallas_call(
        body, name="attn_bwd_kv", grid=(hp, nq),
        in_specs=[pl.BlockSpec((s, 2 * QK_NOPE), lambda h, i: (0, h)), pl.BlockSpec((s, LANES), lambda h, i: (0, h)),
                  pl.BlockSpec((s, 4 * QK_NOPE), lambda h, i: (0, h)), _full((s, LANES)),
                  pl.BlockSpec((s, 2 * V_HEAD), lambda h, i: (0, h)), pl.BlockSpec((s, 2 * V_HEAD), lambda h, i: (0, h)),
                  pl.BlockSpec((None, s, 2), lambda h, i: (h, 0, 0))],
        out_specs=[pl.BlockSpec((t, 4 * QK_NOPE), lambda h, i: (i, h)), pl.BlockSpec((None, t, LANES), lambda h, i: (h, i, 0))],
        out_shape=[jax.ShapeDtypeStruct((s, HEADS * 2 * QK_NOPE), BF16), jax.ShapeDtypeStruct((hp, s, LANES), F32)],
        compiler_params=_params("parallel", "parallel"))(qn, qpr, kv, kpr, o, do, lse)


def _dot_nt(a, b):
    return lax.dot_general(a, b, (((1,), (1,)), ((), ())), preferred_element_type=F32)


def _dot_tn(a, b):
    return lax.dot_general(a, b, (((0,), (0,)), ((), ())), preferred_element_type=F32)


def _q_cat(q_n, qpr, hh):
    lane = lax.broadcasted_iota(jnp.int32, qpr.shape, 1)
    sel = (lane < QK_ROPE) if hh == 0 else (lane >= QK_ROPE)
    return jnp.concatenate([q_n, jnp.where(sel, qpr, jnp.zeros_like(qpr))], axis=1)


def _causal(sc):
    row = lax.broadcasted_iota(jnp.int32, sc.shape, 0)
    col = lax.broadcasted_iota(jnp.int32, sc.shape, 1)
    return jnp.where(col <= row, sc, _NEG)


def attn_fwd2(qn, qp, kv, kpr, cos4, sin4):
    s = qn.shape[0]
    hp = HEADS // 2
    t = _tile(s, ATT_TILE)
    nq = s // t

    def body(qn_ref, qp_ref, kv_ref, kp_ref, c_ref, s_ref, o_ref, qpr_ref, l_ref, kcat_ref):
        qi = pl.program_id(1)

        @pl.when(qi == 0)
        def _():
            for hh in range(2):
                kcat_ref[hh, :, 0:QK_NOPE] = kv_ref[:, 2 * hh * QK_NOPE:(2 * hh + 1) * QK_NOPE]
                kcat_ref[hh, :, QK_NOPE:] = kp_ref[...]

        qpr = _rope(qp_ref[...], c_ref[...], s_ref[...]).astype(BF16)
        qpr_ref[...] = qpr
        qcat = [_q_cat(qn_ref[:, hh * QK_NOPE:(hh + 1) * QK_NOPE], qpr, hh) for hh in range(2)]

        def block(kb, carry, diagonal):
            rows = pl.ds(pl.multiple_of(kb * t, t), t)
            out = []
            for hh in range(2):
                m, l, acc = carry[hh]
                sc = _dot_nt(qcat[hh], kcat_ref[hh, rows, :]) * _ATT_SCALE
                if diagonal:
                    sc = _causal(sc)
                m_new = jnp.maximum(m, jnp.max(sc, axis=-1, keepdims=True))
                alpha = jnp.exp(m - m_new)
                p = jnp.exp(sc - m_new)
                l = alpha * l + jnp.sum(p, axis=-1, keepdims=True)
                v = kv_ref[rows, (2 * hh + 1) * QK_NOPE:(2 * hh + 2) * QK_NOPE]
                acc = alpha * acc + jnp.dot(p.astype(BF16), v, preferred_element_type=F32)
                out.append((m_new, l, acc))
            return tuple(out)

        one = (jnp.full((t, 1), _NEG, F32), jnp.zeros((t, 1), F32), jnp.zeros((t, V_HEAD), F32))
        carry = lax.fori_loop(0, qi, lambda kb, cr: block(kb, cr, False), (one, one))
        carry = block(qi, carry, True)
        for hh in range(2):
            m, l, acc = carry[hh]
            o_ref[:, hh * V_HEAD:(hh + 1) * V_HEAD] = acc / l
            l_ref[:, hh:hh + 1] = m + jnp.log(l)

    return pl.pallas_call(
        body, name="attn_fwd", grid=(hp, nq),
        in_specs=[pl.BlockSpec((t, 2 * QK_NOPE), lambda h, i: (i, h)), pl.BlockSpec((t, LANES), lambda h, i: (i, h)),
                  pl.BlockSpec((s, 4 * QK_NOPE), lambda h, i: (0, h)), _full((s, LANES)),
                  pl.BlockSpec((t, LANES), lambda h, i: (i, 0)), pl.BlockSpec((t, LANES), lambda h, i: (i, 0))],
        out_specs=[pl.BlockSpec((t, 2 * V_HEAD), lambda h, i: (i, h)), pl.BlockSpec((t, LANES), lambda h, i: (i, h)),
                   pl.BlockSpec((None, t, 2), lambda h, i: (h, i, 0))],
        out_shape=[jax.ShapeDtypeStruct((s, HEADS * V_HEAD), F32), jax.ShapeDtypeStruct((s, HEADS * QK_ROPE), BF16),
                   jax.ShapeDtypeStruct((hp, s, 2), F32)],
        scratch_shapes=[pltpu.VMEM((2, s, 2 * QK_NOPE), BF16)],
        compiler_params=_params("parallel", "arbitrary"))(qn, qp, kv, kpr, cos4, sin4)


def attn_bwd2(qn, qpr, kv, kpr, o, do, lse, cos4, sin4):
    s = qn.shape[0]
    hp = HEADS // 2
    t = _tile(s, ATT_TILE)
    nk = s // t

    def body(qn_ref, qpr_ref, kv_ref, kp_ref, o_ref, do_ref, l_ref, c_ref, s_ref,
             dqn_ref, dqp_ref, dkv_ref, dkp_ref, qcat_ref, dq_ref, delta_ref):
        ki = pl.program_id(1)

        @pl.when(ki == 0)
        def _():
            dq_ref[...] = jnp.zeros_like(dq_ref)
            for hh in range(2):
                qcat_ref[hh] = _q_cat(qn_ref[:, hh * QK_NOPE:(hh + 1) * QK_NOPE], qpr_ref[...], hh)
                cols = slice(hh * V_HEAD, (hh + 1) * V_HEAD)
                delta_ref[hh] = jnp.sum(do_ref[:, cols] * o_ref[:, cols], axis=-1, keepdims=True)

        rows_k = pl.ds(pl.multiple_of(ki * t, t), t)
        kcat = [jnp.concatenate([kv_ref[rows_k, 2 * hh * QK_NOPE:(2 * hh + 1) * QK_NOPE], kp_ref[rows_k, :]], axis=1) for hh in range(2)]
        vs = [kv_ref[rows_k, (2 * hh + 1) * QK_NOPE:(2 * hh + 2) * QK_NOPE] for hh in range(2)]

        def block(qb, carry, diagonal):
            rows = pl.ds(pl.multiple_of(qb * t, t), t)
            out = []
            for hh in range(2):
                dkc, dv = carry[hh]
                q_c = qcat_ref[hh, rows, :]
                do_b = do_ref[rows, hh * V_HEAD:(hh + 1) * V_HEAD].astype(BF16)
                sc = _dot_nt(q_c, kcat[hh]) * _ATT_SCALE
                if diagonal:
                    sc = _causal(sc)
                p = jnp.exp(sc - l_ref[rows, hh:hh + 1])
                dpv = _dot_nt(do_b, vs[hh])
                ds = (p * (dpv - delta_ref[hh, rows, :]) * _ATT_SCALE).astype(BF16)
                dv = dv + _dot_tn(p.astype(BF16), do_b)
                dkc = dkc + _dot_tn(ds, q_c)
                dq_ref[hh, rows, :] += jnp.dot(ds, kcat[hh], preferred_element_type=F32)
                out.append((dkc, dv))
            return tuple(out)

        one = (jnp.zeros((t, 2 * QK_NOPE), F32), jnp.zeros((t, V_HEAD), F32))
        carry = block(ki, (one, one), True)
        carry = lax.fori_loop(ki + 1, nk, lambda qb, cr: block(qb, cr, False), carry)
        dkp = jnp.zeros((t, LANES), F32)
        for hh in range(2):
            dkc, dv = carry[hh]
            dkv_ref[:, 2 * hh * QK_NOPE:(2 * hh + 1) * QK_NOPE] = dkc[:, :QK_NOPE].astype(dkv_ref.dtype)
            dkv_ref[:, (2 * hh + 1) * QK_NOPE:(2 * hh + 2) * QK_NOPE] = dv.astype(dkv_ref.dtype)
            dkp = dkp + dkc[:, QK_NOPE:]
        dkp_ref[...] = dkp

        @pl.when(ki == nk - 1)
        def _():
            lane = lax.broadcasted_iota(jnp.int32, (s, LANES), 1)
            dqp = jnp.where(lane < QK_ROPE, dq_ref[0, :, QK_NOPE:], dq_ref[1, :, QK_NOPE:])
            dqp_ref[...] = _rope(dqp, c_ref[...], -s_ref[...]).astype(dqp_ref.dtype)
            for hh in range(2):
                dqn_ref[:, hh * QK_NOPE:(hh + 1) * QK_NOPE] = dq_ref[hh, :, :QK_NOPE].astype(dqn_ref.dtype)

    qblk = pl.BlockSpec((s, 2 * QK_NOPE), lambda h, i: (0, h))
    pblk = pl.BlockSpec((s, LANES), lambda h, i: (0, h))
    tab = _full((s, LANES))
    return pl.pallas_call(
        body, name="attn_bwd", grid=(hp, nk),
        in_specs=[qblk, pblk, pl.BlockSpec((s, 4 * QK_NOPE), lambda h, i: (0, h)), tab, qblk, qblk,
                  pl.BlockSpec((None, s, 2), lambda h, i: (h, 0, 0)), tab, tab],
        out_specs=[qblk, pblk, pl.BlockSpec((t, 4 * QK_NOPE), lambda h, i: (i, h)), pl.BlockSpec((None, t, LANES), lambda h, i: (h, i, 0))],
        out_shape=[jax.ShapeDtypeStruct((s, HEADS * QK_NOPE), BF16), jax.ShapeDtypeStruct((s, HEADS * QK_ROPE), BF16),
                   jax.ShapeDtypeStruct((s, HEADS * 2 * QK_NOPE), BF16), jax.ShapeDtypeStruct((hp, s, LANES), F32)],
        scratch_shapes=[pltpu.VMEM((2, s, 2 * QK_NOPE), BF16), pltpu.VMEM((2, s, 2 * QK_NOPE), F32), pltpu.VMEM((2, s, 1), F32)],
        compiler_params=_params("parallel", "arbitrary"))(qn, qpr, kv, kpr, o, do, lse, cos4, sin4)


def kpe_bwd(dkp, cos4, sin4, pad_cols):
    hp, s, _ = dkp.shape
    tr = _tile(s, ROW_TILE * 2)

    def body(d_ref, c_ref, s_ref, o_ref):
        tot = d_ref[0]
        for h in range(1, hp):
            tot = tot + d_ref[h]
        tot = tot + pltpu.roll(tot, QK_ROPE, 1)
        lane = lax.broadcasted_iota(jnp.int32, tot.shape, 1)
        dk = jnp.where(lane < QK_ROPE, _rope(tot, c_ref[...], -s_ref[...]), jnp.zeros_like(tot))
        o_ref[...] = jnp.zeros_like(o_ref)
        o_ref[:, 0:LANES] = dk.astype(o_ref.dtype)

    row = pl.BlockSpec((tr, LANES), lambda i: (i, 0))
    return pl.pallas_call(body, name="kpe_bwd", grid=(s // tr,),
                          in_specs=[pl.BlockSpec((hp, tr, LANES), lambda i: (0, i, 0)), row, row],
                          out_specs=pl.BlockSpec((tr, pad_cols), lambda i: (i, 0)),
                          out_shape=jax.ShapeDtypeStruct((s, pad_cols), BF16), compiler_params=_params("parallel"))(dkp, cos4, sin4)


def _shift_down(x, n):
    row = lax.broadcasted_iota(jnp.int32, x.shape, 0)
    return jnp.where(row >= n, pltpu.roll(x, n, 0), jnp.zeros_like(x))


def _shift_up(x, n):
    rows = x.shape[0]
    row = lax.broadcasted_iota(jnp.int32, x.shape, 0)
    return jnp.where(row < rows - n, pltpu.roll(x, rows - n, 0), jnp.zeros_like(x))


def _conv(x, w_ref, b_ref):
    return w_ref[2:3, :] * x + w_ref[1:2, :] * _shift_down(x, 1) + w_ref[0:1, :] * _shift_down(x, 2) + b_ref[...]


def conv_act_fwd(upre, conv_w, conv_b):
    s, f2 = upre.shape
    f = f2 // 2
    tc = _tile(f, COL_TILE)
    nc = f // tc

    def body(ug_ref, uv_ref, wg_ref, wv_ref, bg_ref, bv_ref, o_ref):
        gh = _conv(ug_ref[...], wg_ref, bg_ref)
        vh = _conv(uv_ref[...], wv_ref, bv_ref)
        o_ref[...] = (gh * _sigmoid(gh) * vh).astype(o_ref.dtype)

    def spec(rows, shift):
        return pl.BlockSpec((rows, tc), lambda j: (0, j + shift))

    return pl.pallas_call(
        body, name="conv_act_fwd", grid=(nc,),
        in_specs=[spec(s, 0), spec(s, nc), spec(3, 0), spec(3, nc), spec(1, 0), spec(1, nc)], out_specs=spec(s, 0),
        out_shape=jax.ShapeDtypeStruct((s, f), BF16), compiler_params=_params("parallel"))(upre, upre, conv_w, conv_w, conv_b, conv_b)


def conv_act_bwd(upre, conv_w, conv_b, df):
    s, f2 = upre.shape
    f = f2 // 2
    tc = _tile(f, COL_TILE)
    nc = f // tc

    def half(x, d, w_ref, du_ref, gw_ref, gb_ref):
        gb_ref[...] = _colsum(d)
        gw_ref[2:3, :] = _colsum(d * x)
        gw_ref[1:2, :] = _colsum(d * _shift_down(x, 1))
        gw_ref[0:1, :] = _colsum(d * _shift_down(x, 2))
        du_ref[...] = (w_ref[2:3, :] * d + w_ref[1:2, :] * _shift_up(d, 1) + w_ref[0:1, :] * _shift_up(d, 2)).astype(du_ref.dtype)

    def body(ug_ref, uv_ref, wg_ref, wv_ref, bg_ref, bv_ref, df_ref, dug_ref, duv_ref, gwg_ref, gwv_ref, gbg_ref, gbv_ref):
        xg, xv = ug_ref[...], uv_ref[...]
        gh = _conv(xg, wg_ref, bg_ref)
        vh = _conv(xv, wv_ref, bv_ref)
        sg = _sigmoid(gh)
        df_v = df_ref[...]
        half(xg, df_v * vh * (sg * (1.0 + gh * (1.0 - sg))), wg_ref, dug_ref, gwg_ref, gbg_ref)
        half(xv, df_v * (gh * sg), wv_ref, duv_ref, gwv_ref, gbv_ref)

    def spec(rows, shift):
        return pl.BlockSpec((rows, tc), lambda j: (0, j + shift))

    act = jax.ShapeDtypeStruct((s, f), BF16)
    gw = jax.ShapeDtypeStruct((3, f), F32)
    gb = jax.ShapeDtypeStruct((1, f), F32)
    return pl.pallas_call(
        body, name="conv_act_bwd", grid=(nc,),
        in_specs=[spec(s, 0), spec(s, nc), spec(3, 0), spec(3, nc), spec(1, 0), spec(1, nc), spec(s, 0)],
        out_specs=[spec(s, 0), spec(s, 0), spec(3, 0), spec(3, 0), spec(1, 0), spec(1, 0)],
        out_shape=[act, act, gw, gw, gb, gb],
        compiler_params=_params("parallel"))(upre, upre, conv_w, conv_w, conv_b, conv_b, df)


def adamw(name, w, m, v, parts, row_off=0):
    npart, c = parts.shape[0], parts.shape[2]
    r = w.shape[0]
    tr = r
    if r % 8 == 0:
        tr = max(8, min(r, ADAMW_TILE_ELEMS // c) // 8 * 8)
        while r % tr:
            tr -= 8
    bc1 = 1.0 - ADAM_B1 ** ADAM_STEP
    bc2 = 1.0 - ADAM_B2 ** ADAM_STEP

    def body(w_ref, m_ref, v_ref, p_ref, g_ref, d_ref, nm_ref, nv_ref):
        g = p_ref[0].astype(F32)
        for k in range(1, npart):
            g = g + p_ref[k].astype(F32)
        m_new = ADAM_B1 * m_ref[...] + (1.0 - ADAM_B1) * g
        v_new = ADAM_B2 * v_ref[...] + (1.0 - ADAM_B2) * (g * g)
        g_ref[...] = g
        nm_ref[...] = m_new
        nv_ref[...] = v_new
        d_ref[...] = -ADAM_LR * ((m_new / bc1) / (jnp.sqrt(v_new / bc2) + ADAM_EPS) + ADAM_WD * w_ref[...])

    assert row_off % tr == 0
    deps = _TOKENS.take()
    blk = pl.BlockSpec((tr, c), lambda i: (i, 0))
    out = jax.ShapeDtypeStruct((r, c), F32)
    return pl.pallas_call(
        lambda *refs: body(*refs[:4], *refs[4 + len(deps):]), name=name, grid=(r // tr,),
        in_specs=[blk, blk, blk, pl.BlockSpec((npart, tr, c), lambda i: (0, row_off // tr + i, 0))] + [pl.BlockSpec(memory_space=pl.ANY)] * len(deps),
        out_specs=[blk, blk, blk, blk], out_shape=[out, out, out, out], compiler_params=_params("parallel"))(w, m, v, parts, *deps)


def _position():
    return lax.axis_index("x"), lax.axis_index("y"), lax.axis_index("c")


def _index(p):
    return 4 * p[0] + 2 * p[1] + p[2]


def _peer(me, r):
    return (me[0] ^ ((r >> 2) & 1), me[1] ^ ((r >> 1) & 1), me[2] ^ (r & 1))


_ANY = pl.BlockSpec(memory_space=pl.ANY)


def all_gather_two_level(shards):
    n = len(shards)

    def body(*refs):
        ins, outs = refs[:n], refs[n:2 * n]
        send_sems, recv_sems, local_sems = refs[2 * n:]
        x, y, c = _position()
        me, sibling = (x, y, c), (x, y, 1 - c)
        chips = [(1 - x, y), (x, 1 - y), (1 - x, 1 - y)]

        def copy(w, k, block, to, src=None):
            slot = outs[w].at[_index(block)]
            return pltpu.make_async_remote_copy(src_ref=slot if src is None else src, dst_ref=slot,
                                                send_sem=send_sems.at[7 * w + k], recv_sem=recv_sems.at[7 * w + k],
                                                device_id=to, device_id_type=MESH)

        mine = [pltpu.make_async_copy(ins[w], outs[w].at[_index(me)], local_sems.at[w]) for w in range(n)]
        for cp in mine:
            cp.start()
        first = []
        for w in range(n):
            first.append(copy(w, 0, me, sibling, src=ins[w]))
            first += [copy(w, 1 + j, me, (*chip, c), src=ins[w]) for j, chip in enumerate(chips)]
        for cp in first:
            cp.start()
        passed = []
        for w in range(n):
            for j, chip in enumerate(chips):
                copy(w, 1 + j, (*chip, c), me).wait_recv()
                cp = copy(w, 4 + j, (*chip, c), sibling)
                cp.start()
                passed.append(cp)
        for w in range(n):
            copy(w, 0, sibling, me).wait_recv()
            for j, chip in enumerate(chips):
                copy(w, 4 + j, (*chip, 1 - c), me).wait_recv()
        for cp in first + passed:
            cp.wait_send()
        for cp in mine:
            cp.wait()

    return pl.pallas_call(
        body, name="all_gather_weights",
        out_shape=[jax.ShapeDtypeStruct((N_DEV,) + a.shape, a.dtype) for a in shards],
        in_specs=[_ANY] * n, out_specs=[_ANY] * n,
        scratch_shapes=[pltpu.SemaphoreType.DMA((7 * n,)), pltpu.SemaphoreType.DMA((7 * n,)), pltpu.SemaphoreType.DMA((n,))],
        )(*shards)


def exchange(name, arrays, scatter):
    n = len(arrays)

    def body(*refs):
        ins, outs = refs[:n], refs[n:2 * n]
        send_sems, recv_sems, local_sems = refs[2 * n:]
        me = _position()
        copies = []
        for w in range(n):
            src = ins[w].at[_index(me)] if scatter else ins[w]
            cp = pltpu.make_async_copy(src, outs[w].at[_index(me)], local_sems.at[w])
            cp.start()
            copies.append(cp)
        remote = []
        for w in range(n):
            for r in range(1, N_DEV):
                peer = _peer(me, r)
                src = ins[w].at[_index(peer)] if scatter else ins[w]
                cp = pltpu.make_async_remote_copy(src_ref=src, dst_ref=outs[w].at[_index(me)],
                                                  send_sem=send_sems.at[7 * w + r - 1], recv_sem=recv_sems.at[7 * w + r - 1],
                                                  device_id=peer, device_id_type=MESH)
                cp.start()
                remote.append(cp)
        for cp in remote:
            cp.wait()
        for cp in copies:
            cp.wait()

    blocks = [a.shape[1:] if scatter else a.shape for a in arrays]
    return pl.pallas_call(
        body, name=name,
        out_shape=[jax.ShapeDtypeStruct((N_DEV,) + b, a.dtype) for a, b in zip(arrays, blocks)],
        in_specs=[_ANY] * n, out_specs=[_ANY] * n,
        scratch_shapes=[pltpu.SemaphoreType.DMA((7 * n,)), pltpu.SemaphoreType.DMA((7 * n,)), pltpu.SemaphoreType.DMA((n,))],
        )(*arrays)


_HBM = pl.BlockSpec(memory_space=pltpu.HBM)
_SEM = pl.BlockSpec(memory_space=pltpu.SEMAPHORE)
_EFFECT = pltpu.SideEffectType.DATAFLOW_SIDE_EFFECTING


def _direct_copies(ins, lands, send_sems, recv_sems, scatter):
    me = _position()
    copies = []
    for w in range(len(ins)):
        for r in range(1, N_DEV):
            peer = _peer(me, r)
            src = ins[w].at[_index(peer)] if scatter else ins[w]
            copies.append(pltpu.make_async_remote_copy(src_ref=src, dst_ref=lands[w].at[_index(me)], send_sem=send_sems.at[7 * w + r - 1],
                                                       recv_sem=recv_sems.at[7 * w + r - 1], device_id=peer, device_id_type=MESH))
    return copies


def exchange_start(name, groups, scatter):
    arrays = [a for g in groups for a in g]
    n = len(arrays)
    blocks = [a.shape[1:] if scatter else a.shape for a in arrays]
    lands = [lax.empty((N_DEV,) + b, a.dtype) for a, b in zip(arrays, blocks)]
    ng = len(groups)

    def body(*refs):
        ins, lnd = refs[:n], refs[n:2 * n]
        sems = refs[2 * n:2 * n + 2 * ng]
        token = refs[2 * n + 2 * ng + 2 * n]
        local_sem = refs[2 * n + 2 * ng + 2 * n + 1]
        me = _position()
        local = []
        for w in range(n):
            src = ins[w].at[_index(me)] if scatter else ins[w]
            cp = pltpu.make_async_copy(src, lnd[w].at[_index(me)], local_sem.at[w])
            cp.start()
            local.append(cp)
        w0 = 0
        for gi, g in enumerate(groups):
            for cp in _direct_copies(ins[w0:w0 + len(g)], lnd[w0:w0 + len(g)], sems[2 * gi], sems[2 * gi + 1], scatter):
                cp.start()
            w0 += len(g)
        for cp in local:
            cp.wait()
        token[...] = jnp.zeros_like(token)

    sem_shapes = []
    for g in groups:
        sem_shapes += [pltpu.SemaphoreType.DMA((7 * len(g),)), pltpu.SemaphoreType.DMA((7 * len(g),))]
    out = pl.pallas_call(
        body, name=name,
        out_shape=tuple(sem_shapes) + tuple(pltpu.HBM(a.shape, a.dtype) for a in arrays) + tuple(pltpu.HBM(l.shape, l.dtype) for l in lands)
        + (jax.ShapeDtypeStruct((8, LANES), F32),),
        in_specs=[_HBM] * (2 * n), out_specs=tuple([_SEM] * (2 * ng) + [_HBM] * (2 * n) + [pl.BlockSpec(memory_space=pltpu.VMEM)]),
        input_output_aliases={i: 2 * ng + i for i in range(2 * n)},
        scratch_shapes=[pltpu.SemaphoreType.DMA((n,))],
        compiler_params=pltpu.CompilerParams(has_side_effects=_EFFECT),
    )(*[pltpu.with_memory_space_constraint(a, pltpu.HBM) for a in arrays], *[pltpu.with_memory_space_constraint(l, pltpu.HBM) for l in lands])
    sems, thru, token = out[:2 * ng], out[2 * ng:2 * ng + 2 * n], out[-1]
    res, w0 = [], 0
    for gi, g in enumerate(groups):
        res.append((sems[2 * gi], sems[2 * gi + 1], list(thru[w0:w0 + len(g)]), list(thru[n + w0:n + w0 + len(g)])))
        w0 += len(g)
    return res, token


def exchange_wait(name, group, after, scatter):
    send_sems, recv_sems, srcs, lands = group
    n = len(srcs)

    def body(*refs):
        ins, lnd = refs[:n], refs[n:2 * n]
        for cp in _direct_copies(ins, lnd, refs[2 * n], refs[2 * n + 1], scatter):
            cp.wait_send()
            cp.wait_recv()

    out = pl.pallas_call(
        body, name=name, out_shape=tuple(pltpu.HBM(a.shape, a.dtype) for a in srcs + lands),
        in_specs=[_HBM] * (2 * n) + [_SEM, _SEM, pl.BlockSpec(memory_space=pl.ANY)], out_specs=tuple([_HBM] * (2 * n)),
        input_output_aliases={i: i for i in range(2 * n)},
        compiler_params=pltpu.CompilerParams(has_side_effects=_EFFECT),
    )(*srcs, *lands, send_sems, recv_sems, after)
    return list(out[n:])


def _after(x, token):
    return lax.optimization_barrier((x, token))[0]


_TOKEN = jax.ShapeDtypeStruct((8, LANES), F32)
_VM = pl.BlockSpec(memory_space=pltpu.VMEM)
_SIDE = pltpu.CompilerParams(has_side_effects=_EFFECT)


def _hbm(a):
    return pltpu.with_memory_space_constraint(a, pltpu.HBM)


def _like(a):
    return pltpu.HBM(a.shape, a.dtype)


def _dma_sems(n):
    return pltpu.SemaphoreType.DMA((n,))


def _other_chips(x, y):
    return [(1 - x, y), (x, 1 - y), (1 - x, 1 - y)]


COPY_STREAMS = 8


def _row_chunks(src, dst):
    rows = src.shape[0]
    n = COPY_STREAMS
    while n > 1 and rows % (16 * n):
        n //= 2
    r = rows // n
    return [(src.at[pl.ds(i * r, r)], dst.at[pl.ds(i * r, r)]) for i in range(n)]


def _local_copy(src, dst, sem):
    return [pltpu.make_async_copy(s, d, sem) for s, d in _row_chunks(src, dst)]


class _rcopy:
    def __init__(self, src, dst, send_sem, recv_sem, to):
        self.parts = [pltpu.make_async_remote_copy(src_ref=s, dst_ref=d, send_sem=send_sem, recv_sem=recv_sem, device_id=to, device_id_type=MESH)
                      for s, d in _row_chunks(src, dst)]

    def start(self):
        for cp in self.parts:
            cp.start()

    def wait_send(self):
        for cp in self.parts:
            cp.wait_send()

    def wait_recv(self):
        for cp in self.parts:
            cp.wait_recv()


def ag_start(name, shard, after):
    land = lax.empty((N_DEV,) + shard.shape, shard.dtype)

    def body(sh_ref, land_ref, after_ref, send_sems, recv_sems, sh_thru, land_thru, token):
        x, y, c = _position()
        slot = land_ref.at[_index((x, y, c))]
        for k, to in enumerate([(x, y, 1 - c)] + [(*chip, c) for chip in _other_chips(x, y)]):
            _rcopy(sh_ref, slot, send_sems.at[k], recv_sems.at[k], to).start()
        token[...] = jnp.zeros_like(token)

    send, recv, shard, land, token = pl.pallas_call(
        body, name=name, out_shape=(_dma_sems(4), _dma_sems(4), _like(shard), _like(land), _TOKEN),
        in_specs=[_HBM, _HBM, _ANY], out_specs=(_SEM, _SEM, _HBM, _HBM, _VM), input_output_aliases={0: 2, 1: 3},
        compiler_params=_SIDE)(_hbm(shard), _hbm(land), after)
    _TOKENS.push(token)
    return send, recv, shard, land


def ag_forward(name, started, after):
    send, recv, shard, land = started

    def body(sh_ref, land_ref, send_sems, recv_sems, after_ref, fsend, frecv, sh_thru, land_thru, token):
        x, y, c = _position()
        for j, chip in enumerate(_other_chips(x, y)):
            slot = land_ref.at[_index((*chip, c))]
            _rcopy(sh_ref, slot, send_sems.at[1 + j], recv_sems.at[1 + j], (*chip, c)).wait_recv()
            _rcopy(slot, slot, fsend.at[j], frecv.at[j], (x, y, 1 - c)).start()
        token[...] = jnp.zeros_like(token)

    fsend, frecv, shard, land, token = pl.pallas_call(
        body, name=name, out_shape=(_dma_sems(3), _dma_sems(3), _like(shard), _like(land), _TOKEN),
        in_specs=[_HBM, _HBM, _SEM, _SEM, _ANY], out_specs=(_SEM, _SEM, _HBM, _HBM, _VM), input_output_aliases={0: 2, 1: 3},
        compiler_params=_SIDE)(shard, land, send, recv, after)
    _TOKENS.push(token)
    return send, recv, fsend, frecv, shard, land


def ag_wait(name, forwarded, after):
    send, recv, fsend, frecv, shard, land = forwarded

    def body(sh_ref, land_ref, send_sems, recv_sems, fsend_r, frecv_r, after_ref, sh_out, land_out):
        x, y, c = _position()
        sibling = (x, y, 1 - c)
        own = land_ref.at[_index((x, y, c))]
        _rcopy(sh_ref, land_ref.at[_index(sibling)], send_sems.at[0], recv_sems.at[0], sibling).wait_recv()
        for j, chip in enumerate(_other_chips(x, y)):
            _rcopy(sh_ref, land_ref.at[_index((*chip, 1 - c))], fsend_r.at[j], frecv_r.at[j], sibling).wait_recv()
        for k in range(4):
            _rcopy(sh_ref, own, send_sems.at[k], recv_sems.at[k], sibling).wait_send()
        for j in range(3):
            _rcopy(sh_ref, own, fsend_r.at[j], frecv_r.at[j], sibling).wait_send()

    shard, land = pl.pallas_call(
        body, name=name, out_shape=(_like(shard), _like(land)), in_specs=[_HBM, _HBM, _SEM, _SEM, _SEM, _SEM, _ANY],
        out_specs=(_HBM, _HBM), input_output_aliases={0: 0, 1: 1}, compiler_params=_SIDE)(shard, land, send, recv, fsend, frecv, after)
    return lax.dynamic_update_index_in_dim(land, shard, _index(_position()), 0)


def rs_d2d_start(name, grads):
    n = len(grads)
    lands = [lax.empty((4,) + g.shape[1:], g.dtype) for g in grads]

    def body(*refs):
        ins, lnd, send_sems, recv_sems, token = refs[:n], refs[n:2 * n], refs[2 * n], refs[2 * n + 1], refs[4 * n + 2]
        x, y, c = _position()
        for w in range(n):
            for i in range(4):
                _rcopy(ins[w].at[2 * i + 1 - c], lnd[w].at[i], send_sems.at[4 * w + i], recv_sems.at[4 * w + i], (x, y, 1 - c)).start()
        token[...] = jnp.zeros_like(token)

    out = pl.pallas_call(
        body, name=name, out_shape=(_dma_sems(4 * n), _dma_sems(4 * n)) + tuple(_like(a) for a in grads + lands) + (_TOKEN,),
        in_specs=[_HBM] * (2 * n), out_specs=(_SEM, _SEM) + (_HBM,) * (2 * n) + (_VM,),
        input_output_aliases={i: 2 + i for i in range(2 * n)}, compiler_params=_SIDE)(*[_hbm(a) for a in grads + lands])
    _TOKENS.push(out[-1])
    return out[0], out[1], list(out[2:2 + n]), list(out[2 + n:2 + 2 * n])


def rs_d2d_wait(name, started, after):
    send, recv, grads, lands = started
    n = len(grads)

    def body(*refs):
        ins, lnd, send_sems, recv_sems = refs[:n], refs[n:2 * n], refs[2 * n], refs[2 * n + 1]
        x, y, c = _position()
        for w in range(n):
            for i in range(4):
                cp = _rcopy(ins[w].at[2 * i + 1 - c], lnd[w].at[i], send_sems.at[4 * w + i], recv_sems.at[4 * w + i], (x, y, 1 - c))
                cp.wait_send()
                cp.wait_recv()

    out = pl.pallas_call(
        body, name=name, out_shape=tuple(_like(a) for a in grads + lands), in_specs=[_HBM] * (2 * n) + [_SEM, _SEM, _ANY],
        out_specs=(_HBM,) * (2 * n), input_output_aliases={i: i for i in range(2 * n)}, compiler_params=_SIDE)(*grads, *lands, send, recv, after)
    return list(out[:n]), list(out[n:])


def pair_sum(name, grad, land, core):
    _, r, c = grad.shape
    tr = r
    if r % 8 == 0:
        tr = max(8, min(r, ADAMW_TILE_ELEMS // c) // 8 * 8)
        while r % tr:
            tr -= 8

    def body(core_ref, a_ref, b_ref, o_ref):
        o_ref[...] = (a_ref[...].astype(F32) + b_ref[...].astype(F32)).astype(o_ref.dtype)

    return pl.pallas_call(
        body, name=name, out_shape=jax.ShapeDtypeStruct((4, r, c), grad.dtype),
        grid_spec=pltpu.PrefetchScalarGridSpec(
            num_scalar_prefetch=1, grid=(4, r // tr),
            in_specs=[pl.BlockSpec((None, None, tr, c), lambda i, j, core_ref: (i, core_ref[0], j, 0)),
                      pl.BlockSpec((None, tr, c), lambda i, j, core_ref: (i, j, 0))],
            out_specs=pl.BlockSpec((None, tr, c), lambda i, j, core_ref: (i, j, 0))),
        compiler_params=_params("parallel", "parallel"))(core, grad.reshape(4, 2, r, c), land)


def rs_ici_start(name, sums):
    n = len(sums)
    lands = [lax.empty(a.shape, a.dtype) for a in sums]

    def body(*refs):
        ins, lnd, send_sems, recv_sems, token = refs[:n], refs[n:2 * n], refs[2 * n], refs[2 * n + 1], refs[4 * n + 2]
        x, y, c = _position()
        chip = 2 * x + y
        for w in range(n):
            for j, other in enumerate(_other_chips(x, y)):
                _rcopy(ins[w].at[2 * other[0] + other[1]], lnd[w].at[chip], send_sems.at[3 * w + j], recv_sems.at[3 * w + j], (*other, c)).start()
        token[...] = jnp.zeros_like(token)

    out = pl.pallas_call(
        body, name=name, out_shape=(_dma_sems(3 * n), _dma_sems(3 * n)) + tuple(_like(a) for a in sums + lands) + (_TOKEN,),
        in_specs=[_HBM] * (2 * n), out_specs=(_SEM, _SEM) + (_HBM,) * (2 * n) + (_VM,),
        input_output_aliases={i: 2 + i for i in range(2 * n)}, compiler_params=_SIDE)(*[_hbm(a) for a in sums + lands])
    _TOKENS.push(out[-1])
    return out[0], out[1], list(out[2:2 + n]), list(out[2 + n:2 + 2 * n])


def rs_ici_wait(name, started, after):
    send, recv, sums, lands = started
    n = len(sums)

    def body(*refs):
        ins, lnd, send_sems, recv_sems = refs[:n], refs[n:2 * n], refs[2 * n], refs[2 * n + 1]
        x, y, c = _position()
        for w in range(n):
            for j, other in enumerate(_other_chips(x, y)):
                cp = _rcopy(ins[w].at[2 * other[0] + other[1]], lnd[w].at[2 * other[0] + other[1]], send_sems.at[3 * w + j], recv_sems.at[3 * w + j], (*other, c))
                cp.wait_send()
                cp.wait_recv()

    out = pl.pallas_call(
        body, name=name, out_shape=tuple(_like(a) for a in sums + lands), in_specs=[_HBM] * (2 * n) + [_SEM, _SEM, _ANY],
        out_specs=(_HBM,) * (2 * n), input_output_aliases={i: i for i in range(2 * n)}, compiler_params=_SIDE)(*sums, *lands, send, recv, after)
    chip = 2 * lax.axis_index("x") + lax.axis_index("y")
    return [lax.dynamic_update_index_in_dim(land, lax.dynamic_index_in_dim(s, chip, 0, keepdims=False), chip, 0)
            for s, land in zip(out[:n], out[n:])]


def ada_fwd(c, w_ada, b_ada3, conv_w):
    d, cs = w_ada.shape

    def body(c_ref, w_ref, b_ref, cw_ref, mod_ref, sc_ref, cwa_ref, part_ref, send_sems, recv_sems):
        me = _position()
        my = _index(me)
        cv = c_ref[...]
        sc_ref[my] = cv * _sigmoid(cv)
        cwa_ref[my] = cw_ref[...]
        gather = []
        for r in range(1, N_DEV):
            for k, ref in enumerate((sc_ref, cwa_ref)):
                cp = pltpu.make_async_remote_copy(src_ref=ref.at[my], dst_ref=ref.at[my], send_sem=send_sems.at[14 * k + r - 1],
                                                  recv_sem=recv_sems.at[14 * k + r - 1], device_id=_peer(me, r), device_id_type=MESH)
                cp.start()
                gather.append(cp)
        for cp in gather:
            cp.wait()
        sc_all = jnp.concatenate([sc_ref[k] for k in range(N_DEV)], axis=0).astype(BF16)
        part = jnp.dot(sc_all, w_ref[...].astype(BF16), preferred_element_type=F32)
        for k in range(N_DEV):
            part_ref[k] = part[k:k + 1, :]
        scatter = []
        for r in range(1, N_DEV):
            peer = _peer(me, r)
            cp = pltpu.make_async_remote_copy(src_ref=part_ref.at[_index(peer)], dst_ref=mod_ref.at[my], send_sem=send_sems.at[6 + r],
                                              recv_sem=recv_sems.at[6 + r], device_id=peer, device_id_type=MESH)
            cp.start()
            scatter.append(cp)
        mod_ref[my] = part_ref[my]
        for cp in scatter:
            cp.wait()
        mod_ref[...] = mod_ref[...] + b_ref[...]

    vm = pl.BlockSpec(memory_space=pltpu.VMEM)
    return pl.pallas_call(
        body, name="ada_fwd",
        out_shape=[jax.ShapeDtypeStruct((N_DEV, 1, cs), F32), jax.ShapeDtypeStruct((N_DEV, 1, d), F32),
                   jax.ShapeDtypeStruct((N_DEV,) + conv_w.shape, F32)],
        in_specs=[vm, vm, vm, vm], out_specs=[vm, vm, vm],
        scratch_shapes=[pltpu.VMEM((N_DEV, 1, cs), F32), pltpu.SemaphoreType.DMA((21,)), pltpu.SemaphoreType.DMA((21,))],
        compiler_params=pltpu.CompilerParams(vmem_limit_bytes=VMEM_LIMIT_BYTES))(c, w_ada, b_ada3, conv_w)


def ada_bwd_w(sc_all, dmod_cols):
    _, d = sc_all.shape
    cs = dmod_cols.shape[1]
    tr = _tile(d, ROW_TILE)

    def body(sc_ref, dm_ref, o_ref):
        dm = dm_ref[...].astype(BF16)
        o_ref[...] = lax.dot_general(sc_ref[...].astype(BF16), dm, (((0,), (0,)), ((), ())), preferred_element_type=F32)

    return pl.pallas_call(body, name="ada_bwd_w", grid=(d // tr,),
                          in_specs=[pl.BlockSpec((N_DEV, tr), lambda i: (0, i)), _full((N_DEV, cs))],
                          out_specs=pl.BlockSpec((None, tr, cs), lambda i: (0, i, 0)),
                          out_shape=jax.ShapeDtypeStruct((1, d, cs), F32), compiler_params=_params("parallel"))(sc_all, dmod_cols)


def _round_up(n, m):
    return (n + m - 1) // m * m


def kernel(x, c, positions, w_ada, b_ada, pre_norm1_g, w_in, gm_ln_g, gm_ln_b, gm_w_s, gm_b_s, w_branch_a, q_norm_g, w_uq, kv_norm_g, w_ukv, w_branch_b, w_out, post_norm1_g, pre_norm2_g, w_up, conv_w, conv_b, w_down, post_norm2_g, loss_target, m_w_ada, m_b_ada, m_pre_norm1_g, m_w_in, m_gm_ln_g, m_gm_ln_b, m_gm_w_s, m_gm_b_s, m_w_branch_a, m_q_norm_g, m_w_uq, m_kv_norm_g, m_w_ukv, m_w_branch_b, m_w_out, m_post_norm1_g, m_pre_norm2_g, m_w_up, m_conv_w, m_conv_b, m_w_down, m_post_norm2_g, v_w_ada, v_b_ada, v_pre_norm1_g, v_w_in, v_gm_ln_g, v_gm_ln_b, v_gm_w_s, v_gm_b_s, v_w_branch_a, v_q_norm_g, v_w_uq, v_kv_norm_g, v_w_ukv, v_w_branch_b, v_w_out, v_post_norm1_g, v_pre_norm2_g, v_w_up, v_conv_w, v_conv_b, v_w_down, v_post_norm2_g):
    weights = dict(w_ada=w_ada, b_ada=b_ada, pre_norm1_g=pre_norm1_g, w_in=w_in, gm_ln_g=gm_ln_g, gm_ln_b=gm_ln_b, gm_w_s=gm_w_s,
                   gm_b_s=gm_b_s, w_branch_a=w_branch_a, q_norm_g=q_norm_g, w_uq=w_uq, kv_norm_g=kv_norm_g, w_ukv=w_ukv,
                   w_branch_b=w_branch_b, w_out=w_out, post_norm1_g=post_norm1_g, pre_norm2_g=pre_norm2_g, w_up=w_up, conv_w=conv_w,
                   conv_b=conv_b, w_down=w_down, post_norm2_g=post_norm2_g)
    mom1 = dict(w_ada=m_w_ada, b_ada=m_b_ada, pre_norm1_g=m_pre_norm1_g, w_in=m_w_in, gm_ln_g=m_gm_ln_g, gm_ln_b=m_gm_ln_b,
                gm_w_s=m_gm_w_s, gm_b_s=m_gm_b_s, w_branch_a=m_w_branch_a, q_norm_g=m_q_norm_g, w_uq=m_w_uq, kv_norm_g=m_kv_norm_g,
                w_ukv=m_w_ukv, w_branch_b=m_w_branch_b, w_out=m_w_out, post_norm1_g=m_post_norm1_g, pre_norm2_g=m_pre_norm2_g,
                w_up=m_w_up, conv_w=m_conv_w, conv_b=m_conv_b, w_down=m_w_down, post_norm2_g=m_post_norm2_g)
    mom2 = dict(w_ada=v_w_ada, b_ada=v_b_ada, pre_norm1_g=v_pre_norm1_g, w_in=v_w_in, gm_ln_g=v_gm_ln_g, gm_ln_b=v_gm_ln_b,
                gm_w_s=v_gm_w_s, gm_b_s=v_gm_b_s, w_branch_a=v_w_branch_a, q_norm_g=v_q_norm_g, w_uq=v_w_uq, kv_norm_g=v_kv_norm_g,
                w_ukv=v_w_ukv, w_branch_b=v_w_branch_b, w_out=v_w_out, post_norm1_g=v_post_norm1_g, pre_norm2_g=v_pre_norm2_g,
                w_up=v_w_up, conv_w=v_conv_w, conv_b=v_conv_b, w_down=v_w_down, post_norm2_g=v_post_norm2_g)
    order = list(weights)
    _TOKENS.clear()

    s, d = x.shape[1], x.shape[2]
    gmw = gm_ln_g.shape[0]
    groups = gmw // CHUNK
    ql, kvl = q_norm_g.shape[0], kv_norm_g.shape[0]
    f2 = conv_b.shape[0]
    in_cols = w_in.shape[1] * N_DEV
    o_q, o_kv, o_ga, o_gb, o_kpe = 2 * gmw, 2 * gmw + ql, 2 * gmw + ql + kvl, 2 * gmw + ql + kvl + d, 2 * gmw + ql + kvl + 2 * d
    zp = _round_up(o_kpe + LANES, Z_PAD)
    src_kpe = 2 * gmw + ql + kvl
    assert src_kpe + QK_ROPE + 2 * d == in_cols
    my = 4 * lax.axis_index("x") + 2 * lax.axis_index("y") + lax.axis_index("c")

    x2, tgt = x[0], loss_target[0]
    row = lambda a: a.reshape(1, -1)

    big = ["w_in", "w_branch_a", "w_uq", "w_ukv", "w_branch_b", "w_out", "w_up", "w_down"]
    sh = {k: weights[k].astype(BF16) for k in big}
    mix = ["w_branch_a", "w_uq", "w_ukv", "w_branch_b", "w_out"]
    mix_sizes = [sh[k].size for k in mix]
    mix_packed = jnp.concatenate([sh[k].reshape(-1) for k in mix]).reshape(-1, LANES)
    ag_in = ag_start("ag_start_in", sh["w_in"], c)

    mod8, sc_all3, g_cw = ada_fwd(c, w_ada, b_ada.reshape(N_DEV, 1, -1), conv_w)
    mod = mod8.reshape(N_MOD, d)
    shift1, scale1, gate1, shift2, scale2, gate2 = (mod[i:i + 1] for i in range(N_MOD))
    sc_all = sc_all3.reshape(N_DEV, d)
    h1 = norm_mod_fwd("pre1_fwd", x2, row(pre_norm1_g), scale1, shift1)

    g_in = ag_wait("ag_wait_in", ag_forward("ag_forward_in", ag_in, h1), h1)
    ag_mix = ag_start("ag_start_mix", mix_packed, g_in)
    w_in_f = g_in.transpose(1, 0, 2).reshape(d, in_cols)
    w_in_p = jnp.concatenate([w_in_f[:, :src_kpe], w_in_f[:, src_kpe + QK_ROPE:], w_in_f[:, src_kpe:src_kpe + QK_ROPE],
                              jnp.zeros((d, zp - in_cols), BF16)], axis=1)

    inv = ROPE_THETA ** (-jnp.arange(0, QK_ROPE, 2, dtype=F32) / QK_ROPE)
    ang = positions[0].astype(F32)[:, None] * inv
    cos4 = jnp.tile(jnp.cos(ang), (1, 4))
    sin4 = jnp.tile(jnp.concatenate([-jnp.sin(ang), jnp.sin(ang)], axis=1), (1, 2))

    wm = (gm_w_s * jnp.tril(jnp.ones((CHUNK, CHUNK), F32))).astype(BF16)
    bs3 = gm_b_s.reshape(groups, CHUNK, 1)
    ln_g, ln_b = row(gm_ln_g), row(gm_ln_b)

    z = mm_nn("z_proj", h1, w_in_p, F32)
    ag_mix = ag_forward("ag_forward_mix", ag_mix, z)
    a = gmlp_fwd(z, gmw, ln_g, ln_b, wm, bs3)
    g_mix = ag_wait("ag_wait_mix", ag_mix, a).reshape(N_DEV, -1)
    ag_up = ag_start("ag_start_up", sh["w_up"], g_mix)
    offs = [sum(mix_sizes[:i]) for i in range(len(mix) + 1)]
    g_a, g_uq, g_ukv, g_b, g_out = (g_mix[:, offs[i]:offs[i + 1]].reshape((N_DEV,) + sh[k].shape) for i, k in enumerate(mix))
    w_a_f, w_b_f, w_out_f = g_a.reshape(-1, d), g_b.reshape(-1, d), g_out.reshape(-1, d)
    w_uq_f = g_uq.transpose(1, 0, 2).reshape(ql, HEADS, QK_NOPE + QK_ROPE)
    w_uq_n = w_uq_f[:, :, :QK_NOPE].reshape(ql, HEADS * QK_NOPE)
    w_uq_r = w_uq_f[:, :, QK_NOPE:].reshape(ql, HEADS * QK_ROPE)
    y_a = mm_nn("branch_a", a, w_a_f, F32)
    qln = rms_fwd_cols("q_norm", z, o_q, ql, row(q_norm_g))
    kvn = rms_fwd_cols("kv_norm", z, o_kv, kvl, row(kv_norm_g))
    qn = mm_nn("q_nope", qln, w_uq_n, BF16)
    qp = mm_nn("q_rope", qln, w_uq_r, F32)
    kv = mm_nn_b3("kv_up", kvn, g_ukv, BF16)
    kpr = rope_k(z, o_kpe, cos4, sin4)
    o, qpr, lse = attn_fwd2(qn, qp, kv, kpr, cos4, sin4)
    ag_up = ag_forward("ag_forward_up", ag_up, o)
    y_b = mm_nn("branch_b", o, w_b_f, F32)
    merged = merge_fwd(z, o_ga, o_gb, y_a, y_b)
    y1 = mm_nn("out_proj", merged, w_out_f, F32)
    x1 = post_res_fwd("post1_fwd", x2, y1, gate1, row(post_norm1_g))
    h2 = norm_mod_fwd("pre2_fwd", x1, row(pre_norm2_g), scale2, shift2)
    g_up = ag_wait("ag_wait_up", ag_up, h2)
    ag_down = ag_start("ag_start_down", sh["w_down"], g_up)
    upre = mm_nn_b3("up_proj", h2, g_up, F32)
    ag_down = ag_forward("ag_forward_down", ag_down, upre)
    cw = g_cw.transpose(1, 0, 2).reshape(3, f2)
    cb = row(conv_b)
    f = conv_act_fwd(upre, cw, cb)
    w_down_f = ag_wait("ag_wait_down", ag_down, f).reshape(-1, d)
    ffn = mm_nn("down_proj", f, w_down_f, F32)
    loss_acc, dout, dffn, acc2 = post2_loss_bwd(x1, ffn, tgt, gate2, row(post_norm2_g))

    blocks = lambda g: g.reshape(N_DEV, g.shape[0] // N_DEV, g.shape[1])
    core = lax.axis_index("c").astype(jnp.int32).reshape(1)
    rs = {}

    def rs_begin(key, grads):
        rs[key] = rs_d2d_start("rs_d2d_start_" + key, grads)

    def rs_middle(key, after):
        grads, lands = rs_d2d_wait("rs_d2d_wait_" + key, rs[key], after)
        sums = [pair_sum("pair_sum_%s_%d" % (key, i), g, l, core) for i, (g, l) in enumerate(zip(grads, lands))]
        rs[key] = rs_ici_start("rs_ici_start_" + key, sums)

    gw_down = mm_tn("g_w_down", f, dffn, BF16)
    rs_begin("down", [blocks(gw_down)])
    df = mm_nt("d_f", dffn, w_down_f, F32)
    rs_middle("down", df)
    dup_g, dup_v, gcw_g, gcw_v, gcb_g, gcb_v = conv_act_bwd(upre, cw, cb, df)
    dupre = jnp.concatenate([dup_g, dup_v], axis=1)
    gw_up3 = mm_tn_o3("g_w_up", h2, dupre, N_DEV, BF16)
    rs_begin("up", [gw_up3])
    dh2 = mm_nt_b3("d_h2", dupre, g_up, F32)
    rs_middle("up", dh2)
    dx1, dy1, acc_mid = mid_bwd(dh2, dout, x1, y1, row(pre_norm2_g), scale2, gate1, row(post_norm1_g))
    gw_out = mm_tn("g_w_out", merged, dy1, BF16)
    dmerged = mm_nt("d_merged", dy1, w_out_f, F32)
    dya, dyb, dga, dgb = merge_bwd(z, o_ga, o_gb, y_a, y_b, dmerged)
    gw_a = mm_tn("g_w_a", a, dya, BF16)
    gw_b = mm_tn("g_w_b", o, dyb, BF16)
    rs_begin("mid", [jnp.concatenate([blocks(gw_out), blocks(gw_a), blocks(gw_b)], axis=1)])
    da = mm_nt("d_a", dya, w_a_f, F32)
    do = mm_nt("d_o", dyb, w_b_f, F32)
    rs_middle("mid", do)
    duv, g_ws, g_bs3, acc_gm = gmlp_bwd(z, gmw, da, ln_g, ln_b, wm, bs3)
    dqn, dqp, dkv, dkp = attn_bwd2(qn, qpr, kv, kpr, o, do, lse, cos4, sin4)
    dkpe = kpe_bwd(dkp, cos4, sin4, zp - o_kpe)
    dq_cat = jnp.concatenate([dqn, dqp], axis=1)
    w_uq_cat = jnp.concatenate([w_uq_n, w_uq_r], axis=1)
    gw_uq_cat = mm_tn("g_w_uq", qln, dq_cat, BF16)
    gw_uq_f = jnp.concatenate([gw_uq_cat[:, :HEADS * QK_NOPE].reshape(ql, HEADS, QK_NOPE),
                               gw_uq_cat[:, HEADS * QK_NOPE:].reshape(ql, HEADS, QK_ROPE)], axis=2)
    gw_uq3 = gw_uq_f.reshape(ql, N_DEV, -1).transpose(1, 0, 2)
    gw_ukv3 = mm_tn_o3("g_w_ukv", kvn, dkv, N_DEV, BF16)
    rs_begin("mla", [gw_uq3, gw_ukv3])
    dqln = mm_nt("d_qln", dq_cat, w_uq_cat, F32)
    dq_lat, g_qnorm = rms_bwd_cols("q_norm_bwd", dqln, z, o_q, ql, row(q_norm_g))
    dkvn = mm_nt_b3("d_kvn", dkv, g_ukv, F32)
    rs_middle("mla", dkvn)
    dkv_lat, g_kvnorm = rms_bwd_cols("kv_norm_bwd", dkvn, z, o_kv, kvl, row(kv_norm_g))
    dz = jnp.concatenate([duv, dq_lat, dkv_lat, dga, dgb, dkpe], axis=1)
    gw_in_p = mm_tn("g_w_in", h1, dz, BF16)
    gw_in_f = jnp.concatenate([gw_in_p[:, :src_kpe], gw_in_p[:, o_kpe:o_kpe + QK_ROPE], gw_in_p[:, src_kpe:o_kpe]], axis=1)
    gw_in3 = gw_in_f.reshape(d, N_DEV, -1).transpose(1, 0, 2)
    rs_begin("in", [gw_in3])
    dh1 = mm_nt("d_h1", dz, w_in_p, F32)
    grad_x, acc1 = pre1_bwd(dh1, dx1, x2, row(pre_norm1_g), scale1)

    dmod = jnp.concatenate([acc1[0], acc1[1], acc_mid[3], acc_mid[0], acc_mid[1], acc2[0]])
    small = [("pre_norm1_g", acc1[2]), ("gm_ln_g", acc_gm[0]), ("gm_ln_b", acc_gm[1]), ("gm_b_s", g_bs3.reshape(-1)),
             ("q_norm_g", g_qnorm[0]), ("kv_norm_g", g_kvnorm[0]), ("post_norm1_g", acc_mid[4]), ("pre_norm2_g", acc_mid[2]),
             ("conv_b", jnp.concatenate([gcb_g[0], gcb_v[0]])), ("post_norm2_g", acc2[1]), ("gm_w_s", g_ws.reshape(-1)),
             ("b_ada", dmod)]
    n_small = sum(v.shape[0] for _, v in small)
    n_cw = 3 * f2
    n_pack = _round_up(n_small + n_cw, PACK_ALIGN)
    tail = jnp.zeros((n_pack - n_small - n_cw,), F32)
    packed = jnp.concatenate([v for _, v in small] + [jnp.concatenate([gcw_g, gcw_v], axis=1).reshape(-1), tail])
    ag_small = ag_start("ag_start_small", packed.reshape(-1, LANES), packed)
    rs_middle("in", packed)

    res = {}
    last = packed
    for key, names in (("down", ["w_down"]), ("up", ["w_up"]), ("mid", ["w_out", "w_branch_a", "w_branch_b"]), ("mla", ["w_uq", "w_ukv"])):
        parts = rs_ici_wait("rs_ici_wait_" + key, rs[key], last)
        for i, k in enumerate(names):
            packed_rows = key == "mid"
            res[k] = adamw("adamw_" + k, weights[k], mom1[k], mom2[k], parts[0 if packed_rows else i],
                           row_off=sum(weights[n].shape[0] for n in names[:i]) if packed_rows else 0)
            last = res[k][0]

    def pack(src):
        return jnp.concatenate([src[k].reshape(-1) for k, _ in small] + [jnp.zeros((n_pack - n_small,), F32)]).reshape(-1, LANES)

    gathered = ag_wait("ag_wait_small", ag_forward("ag_forward_small", ag_small, last), last)
    sm = [t.reshape(-1) for t in adamw("adamw_small", pack(weights), pack(mom1), pack(mom2), gathered)]
    off = 0
    for k, v in small:
        res[k] = tuple(t[off:off + v.shape[0]].reshape(weights[k].shape) for t in sm)
        off += v.shape[0]

    cs_cw = conv_w.shape[1]
    g_cw_full = sm[0][n_small:n_small + n_cw].reshape(3, f2)
    g_cw_mine = lax.dynamic_slice(g_cw_full, (0, my * cs_cw), (3, cs_cw))
    res["conv_w"] = adamw("adamw_conv_w", conv_w, mom1["conv_w"], mom2["conv_w"], g_cw_mine[None])

    cs_ada = w_ada.shape[1]
    off_b = n_small - N_MOD * d
    dmod_all = gathered.reshape(N_DEV, -1)[:, off_b:off_b + N_MOD * d]
    dmod_cols = lax.dynamic_slice(dmod_all, (0, my * cs_ada), (N_DEV, cs_ada))
    res["w_ada"] = adamw("adamw_w_ada", w_ada, mom1["w_ada"], mom2["w_ada"], ada_bwd_w(sc_all, dmod_cols))

    (p_in,) = rs_ici_wait("rs_ici_wait_in", rs["in"], res["w_ada"][0])
    res["w_in"] = adamw("adamw_w_in", w_in, mom1["w_in"], mom2["w_in"], p_in)

    _TOKENS.clear()
    loss = lax.psum(loss_acc[0, 0], ("x", "y", "c"))
    outs = [loss, grad_x[None]]
    for i in range(4):
        outs += [res[k][i] for k in order]
    return tuple(outs)
```

```python
import functools

import jax
import jax.numpy as jnp
from jax import lax
from jax.experimental import pallas as pl
from jax.experimental.pallas import tpu as pltpu

F32 = jnp.float32
BF16 = jnp.bfloat16

N_DEV = 8
HEADS = 16
QK_NOPE = 128
QK_ROPE = 64
V_HEAD = 128
CHUNK = 128
ROPE_THETA = 10000.0
EPS = 1e-6
N_MOD = 6
ADAM_LR, ADAM_B1, ADAM_B2, ADAM_EPS, ADAM_WD, ADAM_STEP = 0.001, 0.9, 0.999, 1e-08, 0.01, 10

LANES = 128
VMEM_LIMIT_BYTES = 48 * 2 ** 20
ROW_TILE = 256
COL_TILE = 256
ATT_TILE = 256
Z_PAD = 512
ADAMW_TILE_ELEMS = 1 << 18
PACK_ALIGN = 8 * LANES
MESH = pl.DeviceIdType.MESH


def _params(*sem):
    return pltpu.CompilerParams(dimension_semantics=sem if sem else None, vmem_limit_bytes=VMEM_LIMIT_BYTES)


def _tile(dim, target):
    t = (min(dim, target) // LANES) * LANES
    while t >= LANES:
        if dim % t == 0:
            return t
        t -= LANES
    return dim


def _full(shape):
    nd = len(shape)
    return pl.BlockSpec(shape, lambda *_: (0,) * nd)


class _Tokens:
    KEEP = 2

    def __init__(self):
        self.pending = []

    def push(self, token):
        self.pending = (self.pending + [token])[-self.KEEP:]

    def take(self):
        return list(self.pending)

    def clear(self):
        self.pending = []


_TOKENS = _Tokens()


def _matmul(name, a, b, *, grid, a_spec, b_spec, o_spec, out_shape, contract, acc_shape):
    nk = grid[2]
    deps = _TOKENS.take()

    def product(a_ref, b_ref):
        return lax.dot_general(a_ref[...].astype(BF16), b_ref[...].astype(BF16), (contract, ((), ())), preferred_element_type=F32)

    def body_one_step(a_ref, b_ref, *rest):
        o_ref = rest[len(deps)]
        o_ref[...] = product(a_ref, b_ref).astype(o_ref.dtype)

    def body(a_ref, b_ref, *rest):
        o_ref, acc_ref = rest[len(deps):]
        k = pl.program_id(2)

        @pl.when(k == 0)
        def _():
            acc_ref[...] = jnp.zeros_like(acc_ref)

        acc_ref[...] += product(a_ref, b_ref)

        @pl.when(k == nk - 1)
        def _():
            o_ref[...] = acc_ref[...].astype(o_ref.dtype)

    return pl.pallas_call(
        body_one_step if nk == 1 else body, name=name, grid=grid,
        in_specs=[a_spec, b_spec] + [pl.BlockSpec(memory_space=pl.ANY)] * len(deps),
        out_specs=o_spec, out_shape=out_shape, scratch_shapes=[] if nk == 1 else [pltpu.VMEM(acc_shape, F32)],
        compiler_params=_params("parallel", "parallel", "arbitrary"))(a, b, *deps)


TM, TN, TK = 1024, 1024, 2304


def _tk(a, b):
    return TK if a.dtype == BF16 and b.dtype == BF16 else TK // 2


def mm_nn(name, a, b, dtype):
    (m, k), n = a.shape, b.shape[1]
    tm, tn, tk = _tile(m, TM), _tile(n, TN), _tile(k, _tk(a, b))
    return _matmul(name, a, b, grid=(m // tm, n // tn, k // tk),
                   a_spec=pl.BlockSpec((tm, tk), lambda i, j, kk: (i, kk)),
                   b_spec=pl.BlockSpec((tk, tn), lambda i, j, kk: (kk, j)),
                   o_spec=pl.BlockSpec((tm, tn), lambda i, j, kk: (i, j)),
                   out_shape=jax.ShapeDtypeStruct((m, n), dtype), contract=((1,), (0,)), acc_shape=(tm, tn))


def mm_nn_b3(name, a, b3, dtype):
    (m, k), (nj, _, cs) = a.shape, b3.shape
    tm, tk = _tile(m, TM), _tile(k, _tk(a, b3))
    return _matmul(name, a, b3, grid=(m // tm, nj, k // tk),
                   a_spec=pl.BlockSpec((tm, tk), lambda i, j, kk: (i, kk)),
                   b_spec=pl.BlockSpec((None, tk, cs), lambda i, j, kk: (j, kk, 0)),
                   o_spec=pl.BlockSpec((tm, cs), lambda i, j, kk: (i, j)),
                   out_shape=jax.ShapeDtypeStruct((m, nj * cs), dtype), contract=((1,), (0,)), acc_shape=(tm, cs))


def mm_nt(name, a, b, dtype):
    (m, k), n = a.shape, b.shape[0]
    tm, tn, tk = _tile(m, TM), _tile(n, TN), _tile(k, _tk(a, b))
    return _matmul(name, a, b, grid=(m // tm, n // tn, k // tk),
                   a_spec=pl.BlockSpec((tm, tk), lambda i, j, kk: (i, kk)),
                   b_spec=pl.BlockSpec((tn, tk), lambda i, j, kk: (j, kk)),
                   o_spec=pl.BlockSpec((tm, tn), lambda i, j, kk: (i, j)),
                   out_shape=jax.ShapeDtypeStruct((m, n), dtype), contract=((1,), (1,)), acc_shape=(tm, tn))


def mm_nt_b3(name, a, b3, dtype):
    m, (nj, n, cs) = a.shape[0], b3.shape
    tm, tn = _tile(m, TM), _tile(n, TN)
    return _matmul(name, a, b3, grid=(m // tm, n // tn, nj),
                   a_spec=pl.BlockSpec((tm, cs), lambda i, j, kk: (i, kk)),
                   b_spec=pl.BlockSpec((None, tn, cs), lambda i, j, kk: (kk, j, 0)),
                   o_spec=pl.BlockSpec((tm, tn), lambda i, j, kk: (i, j)),
                   out_shape=jax.ShapeDtypeStruct((m, n), dtype), contract=((1,), (1,)), acc_shape=(tm, tn))


def mm_tn(name, a, b, dtype):
    (k, m), n = a.shape, b.shape[1]
    tm, tn, tk = _tile(m, TM), _tile(n, TN), _tile(k, _tk(a, b))
    return _matmul(name, a, b, grid=(m // tm, n // tn, k // tk),
                   a_spec=pl.BlockSpec((tk, tm), lambda i, j, kk: (kk, i)),
                   b_spec=pl.BlockSpec((tk, tn), lambda i, j, kk: (kk, j)),
                   o_spec=pl.BlockSpec((tm, tn), lambda i, j, kk: (i, j)),
                   out_shape=jax.ShapeDtypeStruct((m, n), dtype), contract=((0,), (0,)), acc_shape=(tm, tn))


def mm_tn_o3(name, a, b, nj, dtype):
    (k, m), n = a.shape, b.shape[1]
    cs = n // nj
    tm, tk = _tile(m, TM), _tile(k, _tk(a, b))
    return _matmul(name, a, b, grid=(m // tm, nj, k // tk),
                   a_spec=pl.BlockSpec((tk, tm), lambda i, j, kk: (kk, i)),
                   b_spec=pl.BlockSpec((tk, cs), lambda i, j, kk: (kk, j)),
                   o_spec=pl.BlockSpec((None, tm, cs), lambda i, j, kk: (j, i, 0)),
                   out_shape=jax.ShapeDtypeStruct((nj, m, cs), dtype), contract=((0,), (0,)), acc_shape=(tm, cs))


_GELU_C = 0.7978845608028654
_GELU_A = 0.044715


def _gelu(x):
    return 0.5 * x * (1.0 + jnp.tanh(_GELU_C * (x + _GELU_A * x * x * x)))


def _gelu_and_grad(x):
    t = jnp.tanh(_GELU_C * (x + _GELU_A * x * x * x))
    y = 0.5 * x * (1.0 + t)
    dy = 0.5 * (1.0 + t) + 0.5 * x * (1.0 - t * t) * (_GELU_C * (1.0 + 3.0 * _GELU_A * x * x))
    return y, dy


def _sigmoid(x):
    return 1.0 / (1.0 + jnp.exp(-x))


def _rms_stats(x):
    inv = lax.rsqrt(jnp.mean(x * x, axis=-1, keepdims=True) + EPS)
    return inv, x * inv


def _rms_bwd(dyhat, yhat, inv):
    return inv * (dyhat - yhat * jnp.mean(dyhat * yhat, axis=-1, keepdims=True))


def _colsum(x):
    return jnp.sum(x, axis=0, keepdims=True)


def _rope(x, cos4, sin4):
    lane = lax.broadcasted_iota(jnp.int32, x.shape, x.ndim - 1)
    first_half = (lane % QK_ROPE) < (QK_ROPE // 2)
    partner = jnp.where(first_half, pltpu.roll(x, LANES - QK_ROPE // 2, x.ndim - 1), pltpu.roll(x, QK_ROPE // 2, x.ndim - 1))
    return x * cos4 + partner * sin4


def norm_mod_fwd(name, x, g, scale, shift):
    s, d = x.shape
    tr = _tile(s, ROW_TILE)

    def body(x_ref, g_ref, sc_ref, sh_ref, o_ref):
        _, xh = _rms_stats(x_ref[...])
        o_ref[...] = (xh * g_ref[...] * (1.0 + sc_ref[...]) + sh_ref[...]).astype(o_ref.dtype)

    row = pl.BlockSpec((tr, d), lambda i: (i, 0))
    vec = pl.BlockSpec((1, d), lambda i: (0, 0))
    return pl.pallas_call(body, name=name, grid=(s // tr,), in_specs=[row, vec, vec, vec], out_specs=row,
                          out_shape=jax.ShapeDtypeStruct((s, d), BF16), compiler_params=_params("parallel"))(x, g, scale, shift)


def rms_fwd_cols(name, z, off, width, g):
    s = z.shape[0]
    tr = _tile(s, ROW_TILE)
    assert off % width == 0

    def body(x_ref, g_ref, o_ref):
        _, xh = _rms_stats(x_ref[...])
        o_ref[...] = (xh * g_ref[...]).astype(o_ref.dtype)

    return pl.pallas_call(body, name=name, grid=(s // tr,),
                          in_specs=[pl.BlockSpec((tr, width), lambda i: (i, off // width)), pl.BlockSpec((1, width), lambda i: (0, 0))],
                          out_specs=pl.BlockSpec((tr, width), lambda i: (i, 0)),
                          out_shape=jax.ShapeDtypeStruct((s, width), BF16), compiler_params=_params("parallel"))(z, g)


def rms_bwd_cols(name, dy, z, off, width, g):
    s = z.shape[0]
    tr = _tile(s, ROW_TILE)

    def body(dy_ref, x_ref, g_ref, dx_ref, gg_ref):
        @pl.when(pl.program_id(0) == 0)
        def _():
            gg_ref[...] = jnp.zeros_like(gg_ref)

        inv, xh = _rms_stats(x_ref[...])
        dy_v = dy_ref[...]
        gg_ref[...] += _colsum(dy_v * xh)
        dx_ref[...] = _rms_bwd(dy_v * g_ref[...], xh, inv).astype(dx_ref.dtype)

    return pl.pallas_call(body, name=name, grid=(s // tr,),
                          in_specs=[pl.BlockSpec((tr, width), lambda i: (i, 0)), pl.BlockSpec((tr, width), lambda i: (i, off // width)),
                                    pl.BlockSpec((1, width), lambda i: (0, 0))],
                          out_specs=[pl.BlockSpec((tr, width), lambda i: (i, 0)), pl.BlockSpec((1, width), lambda i: (0, 0))],
                          out_shape=[jax.ShapeDtypeStruct((s, width), BF16), jax.ShapeDtypeStruct((1, width), F32)],
                          compiler_params=_params("arbitrary"))(dy, z, g)


def post_res_fwd(name, x, y, gate, g):
    s, d = x.shape
    tr = _tile(s, ROW_TILE)

    def body(x_ref, y_ref, gate_ref, g_ref, o_ref):
        _, yh = _rms_stats(y_ref[...])
        o_ref[...] = x_ref[...] + gate_ref[...] * (yh * g_ref[...])

    row = pl.BlockSpec((tr, d), lambda i: (i, 0))
    vec = pl.BlockSpec((1, d), lambda i: (0, 0))
    return pl.pallas_call(body, name=name, grid=(s // tr,), in_specs=[row, row, vec, vec], out_specs=row,
                          out_shape=jax.ShapeDtypeStruct((s, d), F32), compiler_params=_params("parallel"))(x, y, gate, g)


def post2_loss_bwd(x1, ffn, target, gate2, g):
    s, d = x1.shape
    tr = _tile(s, ROW_TILE)

    def body(x_ref, y_ref, t_ref, gate_ref, g_ref, loss_ref, dout_ref, dy_ref, acc_ref):
        @pl.when(pl.program_id(0) == 0)
        def _():
            loss_ref[...] = jnp.zeros_like(loss_ref)
            acc_ref[...] = jnp.zeros_like(acc_ref)

        inv, yh = _rms_stats(y_ref[...])
        r = yh * g_ref[...]
        err = x_ref[...] + gate_ref[...] * r - t_ref[...]
        loss_ref[...] += 0.5 * jnp.sum(jnp.mean(err * err, axis=-1, keepdims=True))
        dout = err / d
        dout_ref[...] = dout
        dr = dout * gate_ref[...]
        acc_ref[0:1, :] += _colsum(dout * r)
        acc_ref[1:2, :] += _colsum(dr * yh)
        dy_ref[...] = _rms_bwd(dr * g_ref[...], yh, inv).astype(dy_ref.dtype)

    row = pl.BlockSpec((tr, d), lambda i: (i, 0))
    vec = pl.BlockSpec((1, d), lambda i: (0, 0))
    return pl.pallas_call(
        body, name="post2_loss_bwd", grid=(s // tr,), in_specs=[row, row, row, vec, vec],
        out_specs=[_full((8, LANES)), row, row, _full((8, d))],
        out_shape=[jax.ShapeDtypeStruct((8, LANES), F32), jax.ShapeDtypeStruct((s, d), F32),
                   jax.ShapeDtypeStruct((s, d), BF16), jax.ShapeDtypeStruct((8, d), F32)],
        compiler_params=_params("arbitrary"))(x1, ffn, target, gate2, g)


def mid_bwd(dh2, dout, x1, y1, pre2_g, scale2, gate1, post1_g):
    s, d = x1.shape
    tr = _tile(s, ROW_TILE)

    def body(dh_ref, dout_ref, x_ref, y_ref, g2_ref, sc_ref, gate_ref, g1_ref, dx_ref, dy_ref, acc_ref):
        @pl.when(pl.program_id(0) == 0)
        def _():
            acc_ref[...] = jnp.zeros_like(acc_ref)

        dh = dh_ref[...]
        inv2, xh = _rms_stats(x_ref[...])
        acc_ref[0:1, :] += _colsum(dh)
        acc_ref[1:2, :] += _colsum(dh * (xh * g2_ref[...]))
        t = dh * (1.0 + sc_ref[...])
        acc_ref[2:3, :] += _colsum(t * xh)
        dx1 = dout_ref[...] + _rms_bwd(t * g2_ref[...], xh, inv2)
        dx_ref[...] = dx1
        inv1, yh = _rms_stats(y_ref[...])
        acc_ref[3:4, :] += _colsum(dx1 * (yh * g1_ref[...]))
        dr = dx1 * gate_ref[...]
        acc_ref[4:5, :] += _colsum(dr * yh)
        dy_ref[...] = _rms_bwd(dr * g1_ref[...], yh, inv1).astype(dy_ref.dtype)

    row = pl.BlockSpec((tr, d), lambda i: (i, 0))
    vec = pl.BlockSpec((1, d), lambda i: (0, 0))
    return pl.pallas_call(
        body, name="mid_bwd", grid=(s // tr,), in_specs=[row, row, row, row, vec, vec, vec, vec],
        out_specs=[row, row, _full((8, d))],
        out_shape=[jax.ShapeDtypeStruct((s, d), F32), jax.ShapeDtypeStruct((s, d), BF16), jax.ShapeDtypeStruct((8, d), F32)],
        compiler_params=_params("arbitrary"))(dh2, dout, x1, y1, pre2_g, scale2, gate1, post1_g)


def pre1_bwd(dh1, dx1, x, pre1_g, scale1):
    s, d = x.shape
    tr = _tile(s, ROW_TILE)

    def body(dh_ref, dx1_ref, x_ref, g_ref, sc_ref, dx_ref, acc_ref):
        @pl.when(pl.program_id(0) == 0)
        def _():
            acc_ref[...] = jnp.zeros_like(acc_ref)

        dh = dh_ref[...]
        inv, xh = _rms_stats(x_ref[...])
        acc_ref[0:1, :] += _colsum(dh)
        acc_ref[1:2, :] += _colsum(dh * (xh * g_ref[...]))
        t = dh * (1.0 + sc_ref[...])
        acc_ref[2:3, :] += _colsum(t * xh)
        dx_ref[...] = dx1_ref[...] + _rms_bwd(t * g_ref[...], xh, inv)

    row = pl.BlockSpec((tr, d), lambda i: (i, 0))
    vec = pl.BlockSpec((1, d), lambda i: (0, 0))
    return pl.pallas_call(
        body, name="pre1_bwd", grid=(s // tr,), in_specs=[row, row, row, vec, vec], out_specs=[row, _full((8, d))],
        out_shape=[jax.ShapeDtypeStruct((s, d), F32), jax.ShapeDtypeStruct((8, d), F32)],
        compiler_params=_params("arbitrary"))(dh1, dx1, x, pre1_g, scale1)


def _ln_stats(v):
    mu = jnp.mean(v, axis=-1, keepdims=True)
    vc = v - mu
    rstd = lax.rsqrt(jnp.mean(vc * vc, axis=-1, keepdims=True) + EPS)
    return rstd, vc * rstd


def gmlp_fwd(z, width, ln_g, ln_b, wm, bs3):
    s = z.shape[0]
    groups = width // CHUNK

    def body(u_ref, v_ref, g_ref, b_ref, wm_ref, bs_ref, a_ref):
        ug = _gelu(u_ref[...])
        _, vh = _ln_stats(_gelu(v_ref[...]))
        vn = (vh * g_ref[...] + b_ref[...]).astype(BF16)
        for g in range(groups):
            cols = slice(g * CHUNK, (g + 1) * CHUNK)
            mixed = jnp.dot(wm_ref[g], vn[:, cols], preferred_element_type=F32) + bs_ref[g]
            a_ref[:, cols] = (ug[:, cols] * mixed).astype(a_ref.dtype)

    vec = pl.BlockSpec((1, width), lambda n: (0, 0))
    return pl.pallas_call(
        body, name="gmlp_fwd", grid=(s // CHUNK,),
        in_specs=[pl.BlockSpec((CHUNK, width), lambda n: (n, 0)), pl.BlockSpec((CHUNK, width), lambda n: (n, 1)), vec, vec,
                  _full(wm.shape), _full(bs3.shape)],
        out_specs=pl.BlockSpec((CHUNK, width), lambda n: (n, 0)),
        out_shape=jax.ShapeDtypeStruct((s, width), BF16), compiler_params=_params("parallel"))(z, z, ln_g, ln_b, wm, bs3)


def gmlp_bwd(z, width, da, ln_g, ln_b, wm, bs3):
    s = z.shape[0]
    groups = width // CHUNK

    def body(u_ref, v_ref, da_ref, g_ref, b_ref, wm_ref, bs_ref, duv_ref, gw_ref, gb_ref, acc_ref, dvn_ref):
        @pl.when(pl.program_id(0) == 0)
        def _():
            gw_ref[...] = jnp.zeros_like(gw_ref)
            gb_ref[...] = jnp.zeros_like(gb_ref)
            acc_ref[...] = jnp.zeros_like(acc_ref)

        ug, dug = _gelu_and_grad(u_ref[...])
        vg, dvg = _gelu_and_grad(v_ref[...])
        rstd, vh = _ln_stats(vg)
        vn = (vh * g_ref[...] + b_ref[...]).astype(BF16)
        da_v = da_ref[...]
        for g in range(groups):
            cols = slice(g * CHUNK, (g + 1) * CHUNK)
            mixed = jnp.dot(wm_ref[g], vn[:, cols], preferred_element_type=F32) + bs_ref[g]
            duv_ref[:, cols] = (da_v[:, cols] * mixed * dug[:, cols]).astype(duv_ref.dtype)
            dm = da_v[:, cols] * ug[:, cols]
            gb_ref[g] += jnp.sum(dm, axis=-1, keepdims=True)
            dmb = dm.astype(BF16)
            gw_ref[g] += lax.dot_general(dmb, vn[:, cols], (((1,), (1,)), ((), ())), preferred_element_type=F32)
            dvn_ref[:, cols] = lax.dot_general(wm_ref[g], dmb, (((0,), (0,)), ((), ())), preferred_element_type=F32)
        dvn = dvn_ref[...]
        acc_ref[0:1, :] += _colsum(dvn * vh)
        acc_ref[1:2, :] += _colsum(dvn)
        dvh = dvn * g_ref[...]
        dv = rstd * (dvh - jnp.mean(dvh, axis=-1, keepdims=True) - vh * jnp.mean(dvh * vh, axis=-1, keepdims=True))
        duv_ref[:, width:] = (dv * dvg).astype(duv_ref.dtype)

        @pl.when(pl.program_id(0) == pl.num_programs(0) - 1)
        def _():
            q = lax.broadcasted_iota(jnp.int32, gw_ref.shape, 1)
            p = lax.broadcasted_iota(jnp.int32, gw_ref.shape, 2)
            gw_ref[...] = jnp.where(p <= q, gw_ref[...], 0.0)

    vec = pl.BlockSpec((1, width), lambda n: (0, 0))
    blk = pl.BlockSpec((CHUNK, width), lambda n: (n, 0))
    return pl.pallas_call(
        body, name="gmlp_bwd", grid=(s // CHUNK,),
        in_specs=[blk, pl.BlockSpec((CHUNK, width), lambda n: (n, 1)), blk, vec, vec, _full(wm.shape), _full(bs3.shape)],
        out_specs=[pl.BlockSpec((CHUNK, 2 * width), lambda n: (n, 0)), _full(wm.shape), _full(bs3.shape), _full((8, width))],
        out_shape=[jax.ShapeDtypeStruct((s, 2 * width), BF16), jax.ShapeDtypeStruct(wm.shape, F32),
                   jax.ShapeDtypeStruct(bs3.shape, F32), jax.ShapeDtypeStruct((8, width), F32)],
        scratch_shapes=[pltpu.VMEM((CHUNK, width), F32)],
        compiler_params=_params("arbitrary"))(z, z, da, ln_g, ln_b, wm, bs3)


def merge_fwd(z, off_a, off_b, ya, yb):
    s, d = ya.shape
    tr, tc = _tile(s, ROW_TILE * 2), _tile(d, COL_TILE)
    assert off_a % tc == 0 and off_b % tc == 0

    def body(ga_ref, gb_ref, ya_ref, yb_ref, o_ref):
        o_ref[...] = (_sigmoid(ga_ref[...]) * ya_ref[...] + _sigmoid(gb_ref[...]) * yb_ref[...]).astype(o_ref.dtype)

    blk = pl.BlockSpec((tr, tc), lambda i, j: (i, j))
    return pl.pallas_call(
        body, name="merge_fwd", grid=(s // tr, d // tc),
        in_specs=[pl.BlockSpec((tr, tc), lambda i, j: (i, off_a // tc + j)), pl.BlockSpec((tr, tc), lambda i, j: (i, off_b // tc + j)), blk, blk],
        out_specs=blk, out_shape=jax.ShapeDtypeStruct((s, d), BF16), compiler_params=_params("parallel", "parallel"))(z, z, ya, yb)


def merge_bwd(z, off_a, off_b, ya, yb, dm):
    s, d = ya.shape
    tr, tc = _tile(s, ROW_TILE * 2), _tile(d, COL_TILE)
    nc = d // tc

    def body(ga_ref, gb_ref, ya_ref, yb_ref, dm_ref, dya_ref, dyb_ref, dga_ref, dgb_ref):
        dm_v = dm_ref[...]
        sa, sb = _sigmoid(ga_ref[...]), _sigmoid(gb_ref[...])
        dya_ref[...] = (dm_v * sa).astype(dya_ref.dtype)
        dyb_ref[...] = (dm_v * sb).astype(dyb_ref.dtype)
        dga_ref[...] = (dm_v * ya_ref[...] * sa * (1.0 - sa)).astype(dga_ref.dtype)
        dgb_ref[...] = (dm_v * yb_ref[...] * sb * (1.0 - sb)).astype(dgb_ref.dtype)

    blk = pl.BlockSpec((tr, tc), lambda i, j: (i, j))
    out = jax.ShapeDtypeStruct((s, d), BF16)
    return pl.pallas_call(
        body, name="merge_bwd", grid=(s // tr, nc),
        in_specs=[pl.BlockSpec((tr, tc), lambda i, j: (i, off_a // tc + j)), pl.BlockSpec((tr, tc), lambda i, j: (i, off_b // tc + j)), blk, blk, blk],
        out_specs=[blk, blk, blk, blk], out_shape=[out, out, out, out],
        compiler_params=_params("parallel", "parallel"))(z, z, ya, yb, dm)


_ATT_SCALE = (QK_NOPE + QK_ROPE) ** -0.5
_NEG = -1e30


def rope_k(z, off, cos4, sin4):
    s = z.shape[0]
    tr = _tile(s, ROW_TILE * 2)
    assert off % LANES == 0

    def body(k_ref, c_ref, s_ref, o_ref):
        k = k_ref[...]
        k = k + pltpu.roll(k, QK_ROPE, 1)
        o_ref[...] = _rope(k, c_ref[...], s_ref[...]).astype(o_ref.dtype)

    row = pl.BlockSpec((tr, LANES), lambda i: (i, 0))
    return pl.pallas_call(body, name="rope_k", grid=(s // tr,),
                          in_specs=[pl.BlockSpec((tr, LANES), lambda i: (i, off // LANES)), row, row], out_specs=row,
                          out_shape=jax.ShapeDtypeStruct((s, LANES), BF16), compiler_params=_params("parallel"))(z, cos4, sin4)


def _head_masks(shape):
    lane = lax.broadcasted_iota(jnp.int32, shape, 1)
    return lane < QK_ROPE, lane >= QK_ROPE


def _scores(qn, qp_h, k, kp, qi, kb, t):
    sc = lax.dot_general(qn, k, (((1,), (1,)), ((), ())), preferred_element_type=F32)
    sc += lax.dot_general(qp_h, kp, (((1,), (1,)), ((), ())), preferred_element_type=F32)
    sc = sc * _ATT_SCALE
    row = lax.broadcasted_iota(jnp.int32, sc.shape, 0) + qi * t
    col = lax.broadcasted_iota(jnp.int32, sc.shape, 1) + kb * t
    return jnp.where(col <= row, sc, _NEG)


def attn_fwd(qn, qp, kv, kpr, cos4, sin4):
    s = qn.shape[0]
    hp = HEADS // 2
    t = _tile(s, ATT_TILE)
    nq = s // t

    def body(qn_ref, qp_ref, kv_ref, kp_ref, c_ref, s_ref, o_ref, qpr_ref, l_ref):
        qi = pl.program_id(1)
        qpr = _rope(qp_ref[...], c_ref[...], s_ref[...]).astype(BF16)
        qpr_ref[...] = qpr
        masks = _head_masks(qpr.shape)
        for hh in range(2):
            q_n = qn_ref[:, hh * QK_NOPE:(hh + 1) * QK_NOPE]
            q_p = jnp.where(masks[hh], qpr, jnp.zeros_like(qpr))
            kc, vc = 2 * hh * QK_NOPE, (2 * hh + 1) * QK_NOPE

            def step(kb, carry):
                m, l, acc = carry
                rows = pl.ds(pl.multiple_of(kb * t, t), t)
                sc = _scores(q_n, q_p, kv_ref[rows, kc:kc + QK_NOPE], kp_ref[rows, :], qi, kb, t)
                m_new = jnp.maximum(m, jnp.max(sc, axis=-1, keepdims=True))
                alpha = jnp.exp(m - m_new)
                p = jnp.exp(sc - m_new)
                l = alpha * l + jnp.sum(p, axis=-1, keepdims=True)
                acc = alpha * acc + jnp.dot(p.astype(BF16), kv_ref[rows, vc:vc + V_HEAD], preferred_element_type=F32)
                return m_new, l, acc

            init = (jnp.full((t, 1), _NEG, F32), jnp.zeros((t, 1), F32), jnp.zeros((t, V_HEAD), F32))
            m, l, acc = lax.fori_loop(0, qi + 1, step, init)
            o_ref[:, hh * V_HEAD:(hh + 1) * V_HEAD] = acc / l
            l_ref[:, hh:hh + 1] = m + jnp.log(l)

    return pl.pallas_call(
        body, name="attn_fwd", grid=(hp, nq),
        in_specs=[pl.BlockSpec((t, 2 * QK_NOPE), lambda h, i: (i, h)), pl.BlockSpec((t, LANES), lambda h, i: (i, h)),
                  pl.BlockSpec((s, 4 * QK_NOPE), lambda h, i: (0, h)), _full((s, LANES)),
                  pl.BlockSpec((t, LANES), lambda h, i: (i, 0)), pl.BlockSpec((t, LANES), lambda h, i: (i, 0))],
        out_specs=[pl.BlockSpec((t, 2 * V_HEAD), lambda h, i: (i, h)), pl.BlockSpec((t, LANES), lambda h, i: (i, h)),
                   pl.BlockSpec((None, t, 2), lambda h, i: (h, i, 0))],
        out_shape=[jax.ShapeDtypeStruct((s, HEADS * V_HEAD), F32), jax.ShapeDtypeStruct((s, HEADS * QK_ROPE), BF16),
                   jax.ShapeDtypeStruct((hp, s, 2), F32)],
        compiler_params=_params("parallel", "parallel"))(qn, qp, kv, kpr, cos4, sin4)


def attn_bwd_q(qn, qpr, kv, kpr, o, do, lse, cos4, sin4):
    s = qn.shape[0]
    hp = HEADS // 2
    t = _tile(s, ATT_TILE)
    nq = s // t

    def body(qn_ref, qpr_ref, kv_ref, kp_ref, o_ref, do_ref, l_ref, c_ref, s_ref, dqn_ref, dqp_ref):
        qi = pl.program_id(1)
        qpr = qpr_ref[...]
        masks = _head_masks(qpr.shape)
        dqp = jnp.zeros(qpr.shape, F32)
        for hh in range(2):
            q_n = qn_ref[:, hh * QK_NOPE:(hh + 1) * QK_NOPE]
            q_p = jnp.where(masks[hh], qpr, jnp.zeros_like(qpr))
            kc, vc = 2 * hh * QK_NOPE, (2 * hh + 1) * QK_NOPE
            do_h = do_ref[:, hh * V_HEAD:(hh + 1) * V_HEAD]
            delta = jnp.sum(do_h * o_ref[:, hh * V_HEAD:(hh + 1) * V_HEAD], axis=-1, keepdims=True)
            do_b = do_h.astype(BF16)
            lse_h = l_ref[:, hh:hh + 1]

            def step(kb, carry):
                dn, dp_ = carry
                rows = pl.ds(pl.multiple_of(kb * t, t), t)
                k = kv_ref[rows, kc:kc + QK_NOPE]
                kp = kp_ref[rows, :]
                p = jnp.exp(_scores(q_n, q_p, k, kp, qi, kb, t) - lse_h)
                dpv = lax.dot_general(do_b, kv_ref[rows, vc:vc + V_HEAD], (((1,), (1,)), ((), ())), preferred_element_type=F32)
                ds = (p * (dpv - delta) * _ATT_SCALE).astype(BF16)
                dn = dn + jnp.dot(ds, k, preferred_element_type=F32)
                dp_ = dp_ + jnp.dot(ds, kp, preferred_element_type=F32)
                return dn, dp_

            dn, dp_h = lax.fori_loop(0, qi + 1, step, (jnp.zeros((t, QK_NOPE), F32), jnp.zeros((t, LANES), F32)))
            dqn_ref[:, hh * QK_NOPE:(hh + 1) * QK_NOPE] = dn.astype(dqn_ref.dtype)
            dqp = dqp + jnp.where(masks[hh], dp_h, jnp.zeros_like(dp_h))
        dqp_ref[...] = _rope(dqp, c_ref[...], -s_ref[...]).astype(dqp_ref.dtype)

    qblk = pl.BlockSpec((t, 2 * QK_NOPE), lambda h, i: (i, h))
    pblk = pl.BlockSpec((t, LANES), lambda h, i: (i, h))
    tab = pl.BlockSpec((t, LANES), lambda h, i: (i, 0))
    return pl.pallas_call(
        body, name="attn_bwd_q", grid=(hp, nq),
        in_specs=[qblk, pblk, pl.BlockSpec((s, 4 * QK_NOPE), lambda h, i: (0, h)), _full((s, LANES)), qblk, qblk,
                  pl.BlockSpec((None, t, 2), lambda h, i: (h, i, 0)), tab, tab],
        out_specs=[qblk, pblk],
        out_shape=[jax.ShapeDtypeStruct((s, HEADS * QK_NOPE), BF16), jax.ShapeDtypeStruct((s, HEADS * QK_ROPE), BF16)],
        compiler_params=_params("parallel", "parallel"))(qn, qpr, kv, kpr, o, do, lse, cos4, sin4)


def attn_bwd_kv(qn, qpr, kv, kpr, o, do, lse):
    s = qn.shape[0]
    hp = HEADS // 2
    t = _tile(s, ATT_TILE)
    nq = s // t

    def body(qn_ref, qpr_ref, kv_ref, kp_ref, o_ref, do_ref, l_ref, dkv_ref, dkp_ref):
        ki = pl.program_id(1)
        rows_k = pl.ds(pl.multiple_of(ki * t, t), t)
        kp = kp_ref[rows_k, :]
        dkp = jnp.zeros((t, LANES), F32)
        for hh in range(2):
            kc, vc = 2 * hh * QK_NOPE, (2 * hh + 1) * QK_NOPE
            k = kv_ref[rows_k, kc:kc + QK_NOPE]
            v = kv_ref[rows_k, vc:vc + V_HEAD]

            def step(qb, carry):
                dk, dv, dkp_h = carry
                rows = pl.ds(pl.multiple_of(qb * t, t), t)
                q_n = qn_ref[rows, hh * QK_NOPE:(hh + 1) * QK_NOPE]
                qpr = qpr_ref[rows, :]
                lane = lax.broadcasted_iota(jnp.int32, qpr.shape, 1)
                sel = (lane < QK_ROPE) if hh == 0 else (lane >= QK_ROPE)
                q_p = jnp.where(sel, qpr, jnp.zeros_like(qpr))
                do_h = do_ref[rows, hh * V_HEAD:(hh + 1) * V_HEAD]
                delta = jnp.sum(do_h * o_ref[rows, hh * V_HEAD:(hh + 1) * V_HEAD], axis=-1, keepdims=True)
                do_b = do_h.astype(BF16)
                p = jnp.exp(_scores(q_n, q_p, k, kp, qb, ki, t) - l_ref[rows, hh:hh + 1])
                dpv = lax.dot_general(do_b, v, (((1,), (1,)), ((), ())), preferred_element_type=F32)
                ds = (p * (dpv - delta) * _ATT_SCALE).astype(BF16)
                dv = dv + lax.dot_general(p.astype(BF16), do_b, (((0,), (0,)), ((), ())), preferred_element_type=F32)
                dk = dk + lax.dot_general(ds, q_n, (((0,), (0,)), ((), ())), preferred_element_type=F32)
                dkp_h = dkp_h + lax.dot_general(ds, q_p, (((0,), (0,)), ((), ())), preferred_element_type=F32)
                return dk, dv, dkp_h

            init = (jnp.zeros((t, QK_NOPE), F32), jnp.zeros((t, V_HEAD), F32), jnp.zeros((t, LANES), F32))
            dk, dv, dkp_h = lax.fori_loop(ki, nq, step, init)
            dkv_ref[:, kc:kc + QK_NOPE] = dk.astype(dkv_ref.dtype)
            dkv_ref[:, vc:vc + V_HEAD] = dv.astype(dkv_ref.dtype)
            dkp = dkp + dkp_h
        dkp_ref[...] = dkp

    return pl.pallas_call(
        body, name="attn_bwd_kv", grid=(hp, nq),
        in_specs=[pl.BlockSpec((s, 2 * QK_NOPE), lambda h, i: (0, h)), pl.BlockSpec((s, LANES), lambda h, i: (0, h)),
                  pl.BlockSpec((s, 4 * QK_NOPE), lambda h, i: (0, h)), _full((s, LANES)),
                  pl.BlockSpec((s, 2 * V_HEAD), lambda h, i: (0, h)), pl.BlockSpec((s, 2 * V_HEAD), lambda h, i: (0, h)),
                  pl.BlockSpec((None, s, 2), lambda h, i: (h, 0, 0))],
        out_specs=[pl.BlockSpec((t, 4 * QK_NOPE), lambda h, i: (i, h)), pl.BlockSpec((None, t, LANES), lambda h, i: (h, i, 0))],
        out_shape=[jax.ShapeDtypeStruct((s, HEADS * 2 * QK_NOPE), BF16), jax.ShapeDtypeStruct((hp, s, LANES), F32)],
        compiler_params=_params("parallel", "parallel"))(qn, qpr, kv, kpr, o, do, lse)


def _dot_nt(a, b):
    return lax.dot_general(a, b, (((1,), (1,)), ((), ())), preferred_element_type=F32)


def _dot_tn(a, b):
    return lax.dot_general(a, b, (((0,), (0,)), ((), ())), preferred_element_type=F32)


def _q_cat(q_n, qpr, hh):
    lane = lax.broadcasted_iota(jnp.int32, qpr.shape, 1)
    sel = (lane < QK_ROPE) if hh == 0 else (lane >= QK_ROPE)
    return jnp.concatenate([q_n, jnp.where(sel, qpr, jnp.zeros_like(qpr))], axis=1)


def _causal(sc):
    row = lax.broadcasted_iota(jnp.int32, sc.shape, 0)
    col = lax.broadcasted_iota(jnp.int32, sc.shape, 1)
    return jnp.where(col <= row, sc, _NEG)


def attn_fwd2(qn, qp, kv, kpr, cos4, sin4):
    s = qn.shape[0]
    hp = HEADS // 2
    t = _tile(s, ATT_TILE)
    nq = s // t

    def body(qn_ref, qp_ref, kv_ref, kp_ref, c_ref, s_ref, o_ref, qpr_ref, l_ref, kcat_ref):
        qi = pl.program_id(1)

        @pl.when(qi == 0)
        def _():
            for hh in range(2):
                kcat_ref[hh, :, 0:QK_NOPE] = kv_ref[:, 2 * hh * QK_NOPE:(2 * hh + 1) * QK_NOPE]
                kcat_ref[hh, :, QK_NOPE:] = kp_ref[...]

        qpr = _rope(qp_ref[...], c_ref[...], s_ref[...]).astype(BF16)
        qpr_ref[...] = qpr
        qcat = [_q_cat(qn_ref[:, hh * QK_NOPE:(hh + 1) * QK_NOPE], qpr, hh) for hh in range(2)]

        def block(kb, carry, diagonal):
            rows = pl.ds(pl.multiple_of(kb * t, t), t)
            out = []
            for hh in range(2):
                m, l, acc = carry[hh]
                sc = _dot_nt(qcat[hh], kcat_ref[hh, rows, :]) * _ATT_SCALE
                if diagonal:
                    sc = _causal(sc)
                m_new = jnp.maximum(m, jnp.max(sc, axis=-1, keepdims=True))
                alpha = jnp.exp(m - m_new)
                p = jnp.exp(sc - m_new)
                l = alpha * l + jnp.sum(p, axis=-1, keepdims=True)
                v = kv_ref[rows, (2 * hh + 1) * QK_NOPE:(2 * hh + 2) * QK_NOPE]
                acc = alpha * acc + jnp.dot(p.astype(BF16), v, preferred_element_type=F32)
                out.append((m_new, l, acc))
            return tuple(out)

        one = (jnp.full((t, 1), _NEG, F32), jnp.zeros((t, 1), F32), jnp.zeros((t, V_HEAD), F32))
        carry = lax.fori_loop(0, qi, lambda kb, cr: block(kb, cr, False), (one, one))
        carry = block(qi, carry, True)
        for hh in range(2):
            m, l, acc = carry[hh]
            o_ref[:, hh * V_HEAD:(hh + 1) * V_HEAD] = acc / l
            l_ref[:, hh:hh + 1] = m + jnp.log(l)

    return pl.pallas_call(
        body, name="attn_fwd", grid=(hp, nq),
        in_specs=[pl.BlockSpec((t, 2 * QK_NOPE), lambda h, i: (i, h)), pl.BlockSpec((t, LANES), lambda h, i: (i, h)),
                  pl.BlockSpec((s, 4 * QK_NOPE), lambda h, i: (0, h)), _full((s, LANES)),
                  pl.BlockSpec((t, LANES), lambda h, i: (i, 0)), pl.BlockSpec((t, LANES), lambda h, i: (i, 0))],
        out_specs=[pl.BlockSpec((t, 2 * V_HEAD), lambda h, i: (i, h)), pl.BlockSpec((t, LANES), lambda h, i: (i, h)),
                   pl.BlockSpec((None, t, 2), lambda h, i: (h, i, 0))],
        out_shape=[jax.ShapeDtypeStruct((s, HEADS * V_HEAD), F32), jax.ShapeDtypeStruct((s, HEADS * QK_ROPE), BF16),
                   jax.ShapeDtypeStruct((hp, s, 2), F32)],
        scratch_shapes=[pltpu.VMEM((2, s, 2 * QK_NOPE), BF16)],
        compiler_params=_params("parallel", "arbitrary"))(qn, qp, kv, kpr, cos4, sin4)


def attn_bwd2(qn, qpr, kv, kpr, o, do, lse, cos4, sin4):
    s = qn.shape[0]
    hp = HEADS // 2
    t = _tile(s, ATT_TILE)
    nk = s // t

    def body(qn_ref, qpr_ref, kv_ref, kp_ref, o_ref, do_ref, l_ref, c_ref, s_ref,
             dqn_ref, dqp_ref, dkv_ref, dkp_ref, qcat_ref, dq_ref, delta_ref):
        ki = pl.program_id(1)

        @pl.when(ki == 0)
        def _():
            dq_ref[...] = jnp.zeros_like(dq_ref)
            for hh in range(2):
                qcat_ref[hh] = _q_cat(qn_ref[:, hh * QK_NOPE:(hh + 1) * QK_NOPE], qpr_ref[...], hh)
                cols = slice(hh * V_HEAD, (hh + 1) * V_HEAD)
                delta_ref[hh] = jnp.sum(do_ref[:, cols] * o_ref[:, cols], axis=-1, keepdims=True)

        rows_k = pl.ds(pl.multiple_of(ki * t, t), t)
        kcat = [jnp.concatenate([kv_ref[rows_k, 2 * hh * QK_NOPE:(2 * hh + 1) * QK_NOPE], kp_ref[rows_k, :]], axis=1) for hh in range(2)]
        vs = [kv_ref[rows_k, (2 * hh + 1) * QK_NOPE:(2 * hh + 2) * QK_NOPE] for hh in range(2)]

        def block(qb, carry, diagonal):
            rows = pl.ds(pl.multiple_of(qb * t, t), t)
            out = []
            for hh in range(2):
                dkc, dv = carry[hh]
                q_c = qcat_ref[hh, rows, :]
                do_b = do_ref[rows, hh * V_HEAD:(hh + 1) * V_HEAD].astype(BF16)
                sc = _dot_nt(q_c, kcat[hh]) * _ATT_SCALE
                if diagonal:
                    sc = _causal(sc)
                p = jnp.exp(sc - l_ref[rows, hh:hh + 1])
                dpv = _dot_nt(do_b, vs[hh])
                ds = (p * (dpv - delta_ref[hh, rows, :]) * _ATT_SCALE).astype(BF16)
                dv = dv + _dot_tn(p.astype(BF16), do_b)
                dkc = dkc + _dot_tn(ds, q_c)
                dq_ref[hh, rows, :] += jnp.dot(ds, kcat[hh], preferred_element_type=F32)
                out.append((dkc, dv))
            return tuple(out)

        one = (jnp.zeros((t, 2 * QK_NOPE), F32), jnp.zeros((t, V_HEAD), F32))
        carry = block(ki, (one, one), True)
        carry = lax.fori_loop(ki + 1, nk, lambda qb, cr: block(qb, cr, False), carry)
        dkp = jnp.zeros((t, LANES), F32)
        for hh in range(2):
            dkc, dv = carry[hh]
            dkv_ref[:, 2 * hh * QK_NOPE:(2 * hh + 1) * QK_NOPE] = dkc[:, :QK_NOPE].astype(dkv_ref.dtype)
            dkv_ref[:, (2 * hh + 1) * QK_NOPE:(2 * hh + 2) * QK_NOPE] = dv.astype(dkv_ref.dtype)
            dkp = dkp + dkc[:, QK_NOPE:]
        dkp_ref[...] = dkp

        @pl.when(ki == nk - 1)
        def _():
            lane = lax.broadcasted_iota(jnp.int32, (s, LANES), 1)
            dqp = jnp.where(lane < QK_ROPE, dq_ref[0, :, QK_NOPE:], dq_ref[1, :, QK_NOPE:])
            dqp_ref[...] = _rope(dqp, c_ref[...], -s_ref[...]).astype(dqp_ref.dtype)
            for hh in range(2):
                dqn_ref[:, hh * QK_NOPE:(hh + 1) * QK_NOPE] = dq_ref[hh, :, :QK_NOPE].astype(dqn_ref.dtype)

    qblk = pl.BlockSpec((s, 2 * QK_NOPE), lambda h, i: (0, h))
    pblk = pl.BlockSpec((s, LANES), lambda h, i: (0, h))
    tab = _full((s, LANES))
    return pl.pallas_call(
        body, name="attn_bwd", grid=(hp, nk),
        in_specs=[qblk, pblk, pl.BlockSpec((s, 4 * QK_NOPE), lambda h, i: (0, h)), tab, qblk, qblk,
                  pl.BlockSpec((None, s, 2), lambda h, i: (h, 0, 0)), tab, tab],
        out_specs=[qblk, pblk, pl.BlockSpec((t, 4 * QK_NOPE), lambda h, i: (i, h)), pl.BlockSpec((None, t, LANES), lambda h, i: (h, i, 0))],
        out_shape=[jax.ShapeDtypeStruct((s, HEADS * QK_NOPE), BF16), jax.ShapeDtypeStruct((s, HEADS * QK_ROPE), BF16),
                   jax.ShapeDtypeStruct((s, HEADS * 2 * QK_NOPE), BF16), jax.ShapeDtypeStruct((hp, s, LANES), F32)],
        scratch_shapes=[pltpu.VMEM((2, s, 2 * QK_NOPE), BF16), pltpu.VMEM((2, s, 2 * QK_NOPE), F32), pltpu.VMEM((2, s, 1), F32)],
        compiler_params=_params("parallel", "arbitrary"))(qn, qpr, kv, kpr, o, do, lse, cos4, sin4)


def kpe_bwd(dkp, cos4, sin4, pad_cols):
    hp, s, _ = dkp.shape
    tr = _tile(s, ROW_TILE * 2)

    def body(d_ref, c_ref, s_ref, o_ref):
        tot = d_ref[0]
        for h in range(1, hp):
            tot = tot + d_ref[h]
        tot = tot + pltpu.roll(tot, QK_ROPE, 1)
        lane = lax.broadcasted_iota(jnp.int32, tot.shape, 1)
        dk = jnp.where(lane < QK_ROPE, _rope(tot, c_ref[...], -s_ref[...]), jnp.zeros_like(tot))
        o_ref[...] = jnp.zeros_like(o_ref)
        o_ref[:, 0:LANES] = dk.astype(o_ref.dtype)

    row = pl.BlockSpec((tr, LANES), lambda i: (i, 0))
    return pl.pallas_call(body, name="kpe_bwd", grid=(s // tr,),
                          in_specs=[pl.BlockSpec((hp, tr, LANES), lambda i: (0, i, 0)), row, row],
                          out_specs=pl.BlockSpec((tr, pad_cols), lambda i: (i, 0)),
                          out_shape=jax.ShapeDtypeStruct((s, pad_cols), BF16), compiler_params=_params("parallel"))(dkp, cos4, sin4)


def _shift_down(x, n):
    row = lax.broadcasted_iota(jnp.int32, x.shape, 0)
    return jnp.where(row >= n, pltpu.roll(x, n, 0), jnp.zeros_like(x))


def _shift_up(x, n):
    rows = x.shape[0]
    row = lax.broadcasted_iota(jnp.int32, x.shape, 0)
    return jnp.where(row < rows - n, pltpu.roll(x, rows - n, 0), jnp.zeros_like(x))


def _conv(x, w_ref, b_ref):
    return w_ref[2:3, :] * x + w_ref[1:2, :] * _shift_down(x, 1) + w_ref[0:1, :] * _shift_down(x, 2) + b_ref[...]


def conv_act_fwd(upre, conv_w, conv_b):
    s, f2 = upre.shape
    f = f2 // 2
    tc = _tile(f, COL_TILE)
    nc = f // tc

    def body(ug_ref, uv_ref, wg_ref, wv_ref, bg_ref, bv_ref, o_ref):
        gh = _conv(ug_ref[...], wg_ref, bg_ref)
        vh = _conv(uv_ref[...], wv_ref, bv_ref)
        o_ref[...] = (gh * _sigmoid(gh) * vh).astype(o_ref.dtype)

    def spec(rows, shift):
        return pl.BlockSpec((rows, tc), lambda j: (0, j + shift))

    return pl.pallas_call(
        body, name="conv_act_fwd", grid=(nc,),
        in_specs=[spec(s, 0), spec(s, nc), spec(3, 0), spec(3, nc), spec(1, 0), spec(1, nc)], out_specs=spec(s, 0),
        out_shape=jax.ShapeDtypeStruct((s, f), BF16), compiler_params=_params("parallel"))(upre, upre, conv_w, conv_w, conv_b, conv_b)


def conv_act_bwd(upre, conv_w, conv_b, df):
    s, f2 = upre.shape
    f = f2 // 2
    tc = _tile(f, COL_TILE)
    nc = f // tc

    def half(x, d, w_ref, du_ref, gw_ref, gb_ref):
        gb_ref[...] = _colsum(d)
        gw_ref[2:3, :] = _colsum(d * x)
        gw_ref[1:2, :] = _colsum(d * _shift_down(x, 1))
        gw_ref[0:1, :] = _colsum(d * _shift_down(x, 2))
        du_ref[...] = (w_ref[2:3, :] * d + w_ref[1:2, :] * _shift_up(d, 1) + w_ref[0:1, :] * _shift_up(d, 2)).astype(du_ref.dtype)

    def body(ug_ref, uv_ref, wg_ref, wv_ref, bg_ref, bv_ref, df_ref, dug_ref, duv_ref, gwg_ref, gwv_ref, gbg_ref, gbv_ref):
        xg, xv = ug_ref[...], uv_ref[...]
        gh = _conv(xg, wg_ref, bg_ref)
        vh = _conv(xv, wv_ref, bv_ref)
        sg = _sigmoid(gh)
        df_v = df_ref[...]
        half(xg, df_v * vh * (sg * (1.0 + gh * (1.0 - sg))), wg_ref, dug_ref, gwg_ref, gbg_ref)
        half(xv, df_v * (gh * sg), wv_ref, duv_ref, gwv_ref, gbv_ref)

    def spec(rows, shift):
        return pl.BlockSpec((rows, tc), lambda j: (0, j + shift))

    act = jax.ShapeDtypeStruct((s, f), BF16)
    gw = jax.ShapeDtypeStruct((3, f), F32)
    gb = jax.ShapeDtypeStruct((1, f), F32)
    return pl.pallas_call(
        body, name="conv_act_bwd", grid=(nc,),
        in_specs=[spec(s, 0), spec(s, nc), spec(3, 0), spec(3, nc), spec(1, 0), spec(1, nc), spec(s, 0)],
        out_specs=[spec(s, 0), spec(s, 0), spec(3, 0), spec(3, 0), spec(1, 0), spec(1, 0)],
        out_shape=[act, act, gw, gw, gb, gb],
        compiler_params=_params("parallel"))(upre, upre, conv_w, conv_w, conv_b, conv_b, df)


def adamw(name, w, m, v, parts, row_off=0):
    npart, c = parts.shape[0], parts.shape[2]
    r = w.shape[0]
    tr = r
    if r % 8 == 0:
        tr = max(8, min(r, ADAMW_TILE_ELEMS // c) // 8 * 8)
        while r % tr:
            tr -= 8
    bc1 = 1.0 - ADAM_B1 ** ADAM_STEP
    bc2 = 1.0 - ADAM_B2 ** ADAM_STEP

    def body(w_ref, m_ref, v_ref, p_ref, g_ref, d_ref, nm_ref, nv_ref):
        g = p_ref[0].astype(F32)
        for k in range(1, npart):
            g = g + p_ref[k].astype(F32)
        m_new = ADAM_B1 * m_ref[...] + (1.0 - ADAM_B1) * g
        v_new = ADAM_B2 * v_ref[...] + (1.0 - ADAM_B2) * (g * g)
        g_ref[...] = g
        nm_ref[...] = m_new
        nv_ref[...] = v_new
        d_ref[...] = -ADAM_LR * ((m_new / bc1) / (jnp.sqrt(v_new / bc2) + ADAM_EPS) + ADAM_WD * w_ref[...])

    assert row_off % tr == 0
    deps = _TOKENS.take()
    blk = pl.BlockSpec((tr, c), lambda i: (i, 0))
    out = jax.ShapeDtypeStruct((r, c), F32)
    return pl.pallas_call(
        lambda *refs: body(*refs[:4], *refs[4 + len(deps):]), name=name, grid=(r // tr,),
        in_specs=[blk, blk, blk, pl.BlockSpec((npart, tr, c), lambda i: (0, row_off // tr + i, 0))] + [pl.BlockSpec(memory_space=pl.ANY)] * len(deps),
        out_specs=[blk, blk, blk, blk], out_shape=[out, out, out, out], compiler_params=_params("parallel"))(w, m, v, parts, *deps)


def _position():
    return lax.axis_index("x"), lax.axis_index("y"), lax.axis_index("c")


def _index(p):
    return 4 * p[0] + 2 * p[1] + p[2]


def _peer(me, r):
    return (me[0] ^ ((r >> 2) & 1), me[1] ^ ((r >> 1) & 1), me[2] ^ (r & 1))


_ANY = pl.BlockSpec(memory_space=pl.ANY)


def all_gather_two_level(shards):
    n = len(shards)

    def body(*refs):
        ins, outs = refs[:n], refs[n:2 * n]
        send_sems, recv_sems, local_sems = refs[2 * n:]
        x, y, c = _position()
        me, sibling = (x, y, c), (x, y, 1 - c)
        chips = [(1 - x, y), (x, 1 - y), (1 - x, 1 - y)]

        def copy(w, k, block, to, src=None):
            slot = outs[w].at[_index(block)]
            return pltpu.make_async_remote_copy(src_ref=slot if src is None else src, dst_ref=slot,
                                                send_sem=send_sems.at[7 * w + k], recv_sem=recv_sems.at[7 * w + k],
                                                device_id=to, device_id_type=MESH)

        mine = [pltpu.make_async_copy(ins[w], outs[w].at[_index(me)], local_sems.at[w]) for w in range(n)]
        for cp in mine:
            cp.start()
        first = []
        for w in range(n):
            first.append(copy(w, 0, me, sibling, src=ins[w]))
            first += [copy(w, 1 + j, me, (*chip, c), src=ins[w]) for j, chip in enumerate(chips)]
        for cp in first:
            cp.start()
        passed = []
        for w in range(n):
            for j, chip in enumerate(chips):
                copy(w, 1 + j, (*chip, c), me).wait_recv()
                cp = copy(w, 4 + j, (*chip, c), sibling)
                cp.start()
                passed.append(cp)
        for w in range(n):
            copy(w, 0, sibling, me).wait_recv()
            for j, chip in enumerate(chips):
                copy(w, 4 + j, (*chip, 1 - c), me).wait_recv()
        for cp in first + passed:
            cp.wait_send()
        for cp in mine:
            cp.wait()

    return pl.pallas_call(
        body, name="all_gather_weights",
        out_shape=[jax.ShapeDtypeStruct((N_DEV,) + a.shape, a.dtype) for a in shards],
        in_specs=[_ANY] * n, out_specs=[_ANY] * n,
        scratch_shapes=[pltpu.SemaphoreType.DMA((7 * n,)), pltpu.SemaphoreType.DMA((7 * n,)), pltpu.SemaphoreType.DMA((n,))],
        )(*shards)


def exchange(name, arrays, scatter):
    n = len(arrays)

    def body(*refs):
        ins, outs = refs[:n], refs[n:2 * n]
        send_sems, recv_sems, local_sems = refs[2 * n:]
        me = _position()
        copies = []
        for w in range(n):
            src = ins[w].at[_index(me)] if scatter else ins[w]
            cp = pltpu.make_async_copy(src, outs[w].at[_index(me)], local_sems.at[w])
            cp.start()
            copies.append(cp)
        remote = []
        for w in range(n):
            for r in range(1, N_DEV):
                peer = _peer(me, r)
                src = ins[w].at[_index(peer)] if scatter else ins[w]
                cp = pltpu.make_async_remote_copy(src_ref=src, dst_ref=outs[w].at[_index(me)],
                                                  send_sem=send_sems.at[7 * w + r - 1], recv_sem=recv_sems.at[7 * w + r - 1],
                                                  device_id=peer, device_id_type=MESH)
                cp.start()
                remote.append(cp)
        for cp in remote:
            cp.wait()
        for cp in copies:
            cp.wait()

    blocks = [a.shape[1:] if scatter else a.shape for a in arrays]
    return pl.pallas_call(
        body, name=name,
        out_shape=[jax.ShapeDtypeStruct((N_DEV,) + b, a.dtype) for a, b in zip(arrays, blocks)],
        in_specs=[_ANY] * n, out_specs=[_ANY] * n,
        scratch_shapes=[pltpu.SemaphoreType.DMA((7 * n,)), pltpu.SemaphoreType.DMA((7 * n,)), pltpu.SemaphoreType.DMA((n,))],
        )(*arrays)


_HBM = pl.BlockSpec(memory_space=pltpu.HBM)
_SEM = pl.BlockSpec(memory_space=pltpu.SEMAPHORE)
_EFFECT = pltpu.SideEffectType.DATAFLOW_SIDE_EFFECTING


def _direct_copies(ins, lands, send_sems, recv_sems, scatter):
    me = _position()
    copies = []
    for w in range(len(ins)):
        for r in range(1, N_DEV):
            peer = _peer(me, r)
            src = ins[w].at[_index(peer)] if scatter else ins[w]
            copies.append(pltpu.make_async_remote_copy(src_ref=src, dst_ref=lands[w].at[_index(me)], send_sem=send_sems.at[7 * w + r - 1],
                                                       recv_sem=recv_sems.at[7 * w + r - 1], device_id=peer, device_id_type=MESH))
    return copies


def exchange_start(name, groups, scatter):
    arrays = [a for g in groups for a in g]
    n = len(arrays)
    blocks = [a.shape[1:] if scatter else a.shape for a in arrays]
    lands = [lax.empty((N_DEV,) + b, a.dtype) for a, b in zip(arrays, blocks)]
    ng = len(groups)

    def body(*refs):
        ins, lnd = refs[:n], refs[n:2 * n]
        sems = refs[2 * n:2 * n + 2 * ng]
        token = refs[2 * n + 2 * ng + 2 * n]
        local_sem = refs[2 * n + 2 * ng + 2 * n + 1]
        me = _position()
        local = []
        for w in range(n):
            src = ins[w].at[_index(me)] if scatter else ins[w]
            cp = pltpu.make_async_copy(src, lnd[w].at[_index(me)], local_sem.at[w])
            cp.start()
            local.append(cp)
        w0 = 0
        for gi, g in enumerate(groups):
            for cp in _direct_copies(ins[w0:w0 + len(g)], lnd[w0:w0 + len(g)], sems[2 * gi], sems[2 * gi + 1], scatter):
                cp.start()
            w0 += len(g)
        for cp in local:
            cp.wait()
        token[...] = jnp.zeros_like(token)

    sem_shapes = []
    for g in groups:
        sem_shapes += [pltpu.SemaphoreType.DMA((7 * len(g),)), pltpu.SemaphoreType.DMA((7 * len(g),))]
    out = pl.pallas_call(
        body, name=name,
        out_shape=tuple(sem_shapes) + tuple(pltpu.HBM(a.shape, a.dtype) for a in arrays) + tuple(pltpu.HBM(l.shape, l.dtype) for l in lands)
        + (jax.ShapeDtypeStruct((8, LANES), F32),),
        in_specs=[_HBM] * (2 * n), out_specs=tuple([_SEM] * (2 * ng) + [_HBM] * (2 * n) + [pl.BlockSpec(memory_space=pltpu.VMEM)]),
        input_output_aliases={i: 2 * ng + i for i in range(2 * n)},
        scratch_shapes=[pltpu.SemaphoreType.DMA((n,))],
        compiler_params=pltpu.CompilerParams(has_side_effects=_EFFECT),
    )(*[pltpu.with_memory_space_constraint(a, pltpu.HBM) for a in arrays], *[pltpu.with_memory_space_constraint(l, pltpu.HBM) for l in lands])
    sems, thru, token = out[:2 * ng], out[2 * ng:2 * ng + 2 * n], out[-1]
    res, w0 = [], 0
    for gi, g in enumerate(groups):
        res.append((sems[2 * gi], sems[2 * gi + 1], list(thru[w0:w0 + len(g)]), list(thru[n + w0:n + w0 + len(g)])))
        w0 += len(g)
    return res, token


def exchange_wait(name, group, after, scatter):
    send_sems, recv_sems, srcs, lands = group
    n = len(srcs)

    def body(*refs):
        ins, lnd = refs[:n], refs[n:2 * n]
        for cp in _direct_copies(ins, lnd, refs[2 * n], refs[2 * n + 1], scatter):
            cp.wait_send()
            cp.wait_recv()

    out = pl.pallas_call(
        body, name=name, out_shape=tuple(pltpu.HBM(a.shape, a.dtype) for a in srcs + lands),
        in_specs=[_HBM] * (2 * n) + [_SEM, _SEM, pl.BlockSpec(memory_space=pl.ANY)], out_specs=tuple([_HBM] * (2 * n)),
        input_output_aliases={i: i for i in range(2 * n)},
        compiler_params=pltpu.CompilerParams(has_side_effects=_EFFECT),
    )(*srcs, *lands, send_sems, recv_sems, after)
    return list(out[n:])


def _after(x, token):
    return lax.optimization_barrier((x, token))[0]


_TOKEN = jax.ShapeDtypeStruct((8, LANES), F32)
_VM = pl.BlockSpec(memory_space=pltpu.VMEM)
_SIDE = pltpu.CompilerParams(has_side_effects=_EFFECT)


def _hbm(a):
    return pltpu.with_memory_space_constraint(a, pltpu.HBM)


def _like(a):
    return pltpu.HBM(a.shape, a.dtype)


def _dma_sems(n):
    return pltpu.SemaphoreType.DMA((n,))


def _other_chips(x, y):
    return [(1 - x, y), (x, 1 - y), (1 - x, 1 - y)]


COPY_STREAMS = 8


def _row_chunks(src, dst):
    rows = src.shape[0]
    n = COPY_STREAMS
    while n > 1 and rows % (16 * n):
        n //= 2
    r = rows // n
    return [(src.at[pl.ds(i * r, r)], dst.at[pl.ds(i * r, r)]) for i in range(n)]


def _local_copy(src, dst, sem):
    return [pltpu.make_async_copy(s, d, sem) for s, d in _row_chunks(src, dst)]


class _rcopy:
    def __init__(self, src, dst, send_sem, recv_sem, to):
        self.parts = [pltpu.make_async_remote_copy(src_ref=s, dst_ref=d, send_sem=send_sem, recv_sem=recv_sem, device_id=to, device_id_type=MESH)
                      for s, d in _row_chunks(src, dst)]

    def start(self):
        for cp in self.parts:
            cp.start()

    def wait_send(self):
        for cp in self.parts:
            cp.wait_send()

    def wait_recv(self):
        for cp in self.parts:
            cp.wait_recv()


def ag_start(name, shards, after):
    n = len(shards)
    lands = [lax.empty((N_DEV,) + a.shape, a.dtype) for a in shards]

    def body(*refs):
        ins, lnd, send_sems, recv_sems, token = refs[:n], refs[n:2 * n], refs[2 * n + 1], refs[2 * n + 2], refs[4 * n + 3]
        x, y, c = _position()
        for w in range(n):
            slot = lnd[w].at[_index((x, y, c))]
            for k, to in enumerate([(x, y, 1 - c)] + [(*chip, c) for chip in _other_chips(x, y)]):
                _rcopy(ins[w], slot, send_sems.at[4 * w + k], recv_sems.at[4 * w + k], to).start()
        token[...] = jnp.zeros_like(token)

    out = pl.pallas_call(
        body, name=name, out_shape=(_dma_sems(4 * n), _dma_sems(4 * n)) + tuple(_like(a) for a in shards + lands) + (_TOKEN,),
        in_specs=[_HBM] * (2 * n) + [_ANY], out_specs=(_SEM, _SEM) + (_HBM,) * (2 * n) + (_VM,),
        input_output_aliases={i: 2 + i for i in range(2 * n)}, compiler_params=_SIDE)(*[_hbm(a) for a in shards + lands], after)
    _TOKENS.push(out[-1])
    return out[0], out[1], list(out[2:2 + n]), list(out[2 + n:2 + 2 * n])


def ag_forward(name, started, after):
    send, recv, shards, lands = started
    n = len(shards)

    def body(*refs):
        ins, lnd, send_sems, recv_sems = refs[:n], refs[n:2 * n], refs[2 * n], refs[2 * n + 1]
        fsend, frecv, token = refs[2 * n + 3], refs[2 * n + 4], refs[4 * n + 5]
        x, y, c = _position()
        for w in range(n):
            for j, chip in enumerate(_other_chips(x, y)):
                slot = lnd[w].at[_index((*chip, c))]
                _rcopy(ins[w], slot, send_sems.at[4 * w + 1 + j], recv_sems.at[4 * w + 1 + j], (*chip, c)).wait_recv()
                _rcopy(slot, slot, fsend.at[3 * w + j], frecv.at[3 * w + j], (x, y, 1 - c)).start()
        token[...] = jnp.zeros_like(token)

    out = pl.pallas_call(
        body, name=name, out_shape=(_dma_sems(3 * n), _dma_sems(3 * n)) + tuple(_like(a) for a in shards + lands) + (_TOKEN,),
        in_specs=[_HBM] * (2 * n) + [_SEM, _SEM, _ANY], out_specs=(_SEM, _SEM) + (_HBM,) * (2 * n) + (_VM,),
        input_output_aliases={i: 2 + i for i in range(2 * n)}, compiler_params=_SIDE)(*shards, *lands, send, recv, after)
    _TOKENS.push(out[-1])
    return send, recv, out[0], out[1], list(out[2:2 + n]), list(out[2 + n:2 + 2 * n])


def ag_wait(name, forwarded, after):
    send, recv, fsend, frecv, shards, lands = forwarded
    n = len(shards)

    def body(*refs):
        ins, lnd, send_sems, recv_sems, fsend_r, frecv_r = refs[:n], refs[n:2 * n], refs[2 * n], refs[2 * n + 1], refs[2 * n + 2], refs[2 * n + 3]
        x, y, c = _position()
        sibling = (x, y, 1 - c)
        for w in range(n):
            own = lnd[w].at[_index((x, y, c))]
            _rcopy(ins[w], lnd[w].at[_index(sibling)], send_sems.at[4 * w], recv_sems.at[4 * w], sibling).wait_recv()
            for j, chip in enumerate(_other_chips(x, y)):
                _rcopy(ins[w], lnd[w].at[_index((*chip, 1 - c))], fsend_r.at[3 * w + j], frecv_r.at[3 * w + j], sibling).wait_recv()
            for k in range(4):
                _rcopy(ins[w], own, send_sems.at[4 * w + k], recv_sems.at[4 * w + k], sibling).wait_send()
            for j in range(3):
                _rcopy(ins[w], own, fsend_r.at[3 * w + j], frecv_r.at[3 * w + j], sibling).wait_send()

    out = pl.pallas_call(
        body, name=name, out_shape=tuple(_like(a) for a in shards + lands), in_specs=[_HBM] * (2 * n) + [_SEM] * 4 + [_ANY],
        out_specs=(_HBM,) * (2 * n), input_output_aliases={i: i for i in range(2 * n)},
        compiler_params=_SIDE)(*shards, *lands, send, recv, fsend, frecv, after)
    return [lax.dynamic_update_index_in_dim(land, shard, _index(_position()), 0) for shard, land in zip(out[:n], out[n:])]


def rs_d2d_start(name, grads):
    n = len(grads)
    lands = [lax.empty((4,) + g.shape[1:], g.dtype) for g in grads]

    def body(*refs):
        ins, lnd, send_sems, recv_sems, token = refs[:n], refs[n:2 * n], refs[2 * n], refs[2 * n + 1], refs[4 * n + 2]
        x, y, c = _position()
        for w in range(n):
            for i in range(4):
                _rcopy(ins[w].at[2 * i + 1 - c], lnd[w].at[i], send_sems.at[4 * w + i], recv_sems.at[4 * w + i], (x, y, 1 - c)).start()
        token[...] = jnp.zeros_like(token)

    out = pl.pallas_call(
        body, name=name, out_shape=(_dma_sems(4 * n), _dma_sems(4 * n)) + tuple(_like(a) for a in grads + lands) + (_TOKEN,),
        in_specs=[_HBM] * (2 * n), out_specs=(_SEM, _SEM) + (_HBM,) * (2 * n) + (_VM,),
        input_output_aliases={i: 2 + i for i in range(2 * n)}, compiler_params=_SIDE)(*[_hbm(a) for a in grads + lands])
    _TOKENS.push(out[-1])
    return out[0], out[1], list(out[2:2 + n]), list(out[2 + n:2 + 2 * n])


def rs_d2d_wait(name, started, after):
    send, recv, grads, lands = started
    n = len(grads)

    def body(*refs):
        ins, lnd, send_sems, recv_sems = refs[:n], refs[n:2 * n], refs[2 * n], refs[2 * n + 1]
        x, y, c = _position()
        for w in range(n):
            for i in range(4):
                cp = _rcopy(ins[w].at[2 * i + 1 - c], lnd[w].at[i], send_sems.at[4 * w + i], recv_sems.at[4 * w + i], (x, y, 1 - c))
                cp.wait_send()
                cp.wait_recv()

    out = pl.pallas_call(
        body, name=name, out_shape=tuple(_like(a) for a in grads + lands), in_specs=[_HBM] * (2 * n) + [_SEM, _SEM, _ANY],
        out_specs=(_HBM,) * (2 * n), input_output_aliases={i: i for i in range(2 * n)}, compiler_params=_SIDE)(*grads, *lands, send, recv, after)
    return list(out[:n]), list(out[n:])


def pair_sum(name, grad, land, core):
    _, r, c = grad.shape
    tr = r
    if r % 8 == 0:
        tr = max(8, min(r, ADAMW_TILE_ELEMS // c) // 8 * 8)
        while r % tr:
            tr -= 8

    def body(core_ref, a_ref, b_ref, o_ref):
        o_ref[...] = (a_ref[...].astype(F32) + b_ref[...].astype(F32)).astype(o_ref.dtype)

    return pl.pallas_call(
        body, name=name, out_shape=jax.ShapeDtypeStruct((4, r, c), grad.dtype),
        grid_spec=pltpu.PrefetchScalarGridSpec(
            num_scalar_prefetch=1, grid=(4, r // tr),
            in_specs=[pl.BlockSpec((None, None, tr, c), lambda i, j, core_ref: (i, core_ref[0], j, 0)),
                      pl.BlockSpec((None, tr, c), lambda i, j, core_ref: (i, j, 0))],
            out_specs=pl.BlockSpec((None, tr, c), lambda i, j, core_ref: (i, j, 0))),
        compiler_params=_params("parallel", "parallel"))(core, grad.reshape(4, 2, r, c), land)


def rs_ici_start(name, sums):
    n = len(sums)
    lands = [lax.empty(a.shape, a.dtype) for a in sums]

    def body(*refs):
        ins, lnd, send_sems, recv_sems, token = refs[:n], refs[n:2 * n], refs[2 * n], refs[2 * n + 1], refs[4 * n + 2]
        x, y, c = _position()
        chip = 2 * x + y
        for w in range(n):
            for j, other in enumerate(_other_chips(x, y)):
                _rcopy(ins[w].at[2 * other[0] + other[1]], lnd[w].at[chip], send_sems.at[3 * w + j], recv_sems.at[3 * w + j], (*other, c)).start()
        token[...] = jnp.zeros_like(token)

    out = pl.pallas_call(
        body, name=name, out_shape=(_dma_sems(3 * n), _dma_sems(3 * n)) + tuple(_like(a) for a in sums + lands) + (_TOKEN,),
        in_specs=[_HBM] * (2 * n), out_specs=(_SEM, _SEM) + (_HBM,) * (2 * n) + (_VM,),
        input_output_aliases={i: 2 + i for i in range(2 * n)}, compiler_params=_SIDE)(*[_hbm(a) for a in sums + lands])
    _TOKENS.push(out[-1])
    return out[0], out[1], list(out[2:2 + n]), list(out[2 + n:2 + 2 * n])


def rs_ici_wait(name, started, after):
    send, recv, sums, lands = started
    n = len(sums)

    def body(*refs):
        ins, lnd, send_sems, recv_sems = refs[:n], refs[n:2 * n], refs[2 * n], refs[2 * n + 1]
        x, y, c = _position()
        for w in range(n):
            for j, other in enumerate(_other_chips(x, y)):
                cp = _rcopy(ins[w].at[2 * other[0] + other[1]], lnd[w].at[2 * other[0] + other[1]], send_sems.at[3 * w + j], recv_sems.at[3 * w + j], (*other, c))
                cp.wait_send()
                cp.wait_recv()

    out = pl.pallas_call(
        body, name=name, out_shape=tuple(_like(a) for a in sums + lands), in_specs=[_HBM] * (2 * n) + [_SEM, _SEM, _ANY],
        out_specs=(_HBM,) * (2 * n), input_output_aliases={i: i for i in range(2 * n)}, compiler_params=_SIDE)(*sums, *lands, send, recv, after)
    chip = 2 * lax.axis_index("x") + lax.axis_index("y")
    return [lax.dynamic_update_index_in_dim(land, lax.dynamic_index_in_dim(s, chip, 0, keepdims=False), chip, 0)
            for s, land in zip(out[:n], out[n:])]


def ada_fwd(c, w_ada, b_ada3, conv_w):
    d, cs = w_ada.shape

    def body(c_ref, w_ref, b_ref, cw_ref, mod_ref, sc_ref, cwa_ref, part_ref, send_sems, recv_sems):
        me = _position()
        my = _index(me)
        cv = c_ref[...]
        sc_ref[my] = cv * _sigmoid(cv)
        cwa_ref[my] = cw_ref[...]
        gather = []
        for r in range(1, N_DEV):
            for k, ref in enumerate((sc_ref, cwa_ref)):
                cp = pltpu.make_async_remote_copy(src_ref=ref.at[my], dst_ref=ref.at[my], send_sem=send_sems.at[14 * k + r - 1],
                                                  recv_sem=recv_sems.at[14 * k + r - 1], device_id=_peer(me, r), device_id_type=MESH)
                cp.start()
                gather.append(cp)
        for cp in gather:
            cp.wait()
        sc_all = jnp.concatenate([sc_ref[k] for k in range(N_DEV)], axis=0).astype(BF16)
        part = jnp.dot(sc_all, w_ref[...].astype(BF16), preferred_element_type=F32)
        for k in range(N_DEV):
            part_ref[k] = part[k:k + 1, :]
        scatter = []
        for r in range(1, N_DEV):
            peer = _peer(me, r)
            cp = pltpu.make_async_remote_copy(src_ref=part_ref.at[_index(peer)], dst_ref=mod_ref.at[my], send_sem=send_sems.at[6 + r],
                                              recv_sem=recv_sems.at[6 + r], device_id=peer, device_id_type=MESH)
            cp.start()
            scatter.append(cp)
        mod_ref[my] = part_ref[my]
        for cp in scatter:
            cp.wait()
        mod_ref[...] = mod_ref[...] + b_ref[...]

    vm = pl.BlockSpec(memory_space=pltpu.VMEM)
    return pl.pallas_call(
        body, name="ada_fwd",
        out_shape=[jax.ShapeDtypeStruct((N_DEV, 1, cs), F32), jax.ShapeDtypeStruct((N_DEV, 1, d), F32),
                   jax.ShapeDtypeStruct((N_DEV,) + conv_w.shape, F32)],
        in_specs=[vm, vm, vm, vm], out_specs=[vm, vm, vm],
        scratch_shapes=[pltpu.VMEM((N_DEV, 1, cs), F32), pltpu.SemaphoreType.DMA((21,)), pltpu.SemaphoreType.DMA((21,))],
        compiler_params=pltpu.CompilerParams(vmem_limit_bytes=VMEM_LIMIT_BYTES))(c, w_ada, b_ada3, conv_w)


def ada_bwd_w(sc_all, dmod_cols):
    _, d = sc_all.shape
    cs = dmod_cols.shape[1]
    tr = _tile(d, ROW_TILE)

    def body(sc_ref, dm_ref, o_ref):
        dm = dm_ref[...].astype(BF16)
        o_ref[...] = lax.dot_general(sc_ref[...].astype(BF16), dm, (((0,), (0,)), ((), ())), preferred_element_type=F32)

    return pl.pallas_call(body, name="ada_bwd_w", grid=(d // tr,),
                          in_specs=[pl.BlockSpec((N_DEV, tr), lambda i: (0, i)), _full((N_DEV, cs))],
                          out_specs=pl.BlockSpec((None, tr, cs), lambda i: (0, i, 0)),
                          out_shape=jax.ShapeDtypeStruct((1, d, cs), F32), compiler_params=_params("parallel"))(sc_all, dmod_cols)


def _round_up(n, m):
    return (n + m - 1) // m * m


def kernel(x, c, positions, w_ada, b_ada, pre_norm1_g, w_in, gm_ln_g, gm_ln_b, gm_w_s, gm_b_s, w_branch_a, q_norm_g, w_uq, kv_norm_g, w_ukv, w_branch_b, w_out, post_norm1_g, pre_norm2_g, w_up, conv_w, conv_b, w_down, post_norm2_g, loss_target, m_w_ada, m_b_ada, m_pre_norm1_g, m_w_in, m_gm_ln_g, m_gm_ln_b, m_gm_w_s, m_gm_b_s, m_w_branch_a, m_q_norm_g, m_w_uq, m_kv_norm_g, m_w_ukv, m_w_branch_b, m_w_out, m_post_norm1_g, m_pre_norm2_g, m_w_up, m_conv_w, m_conv_b, m_w_down, m_post_norm2_g, v_w_ada, v_b_ada, v_pre_norm1_g, v_w_in, v_gm_ln_g, v_gm_ln_b, v_gm_w_s, v_gm_b_s, v_w_branch_a, v_q_norm_g, v_w_uq, v_kv_norm_g, v_w_ukv, v_w_branch_b, v_w_out, v_post_norm1_g, v_pre_norm2_g, v_w_up, v_conv_w, v_conv_b, v_w_down, v_post_norm2_g):
    weights = dict(w_ada=w_ada, b_ada=b_ada, pre_norm1_g=pre_norm1_g, w_in=w_in, gm_ln_g=gm_ln_g, gm_ln_b=gm_ln_b, gm_w_s=gm_w_s,
                   gm_b_s=gm_b_s, w_branch_a=w_branch_a, q_norm_g=q_norm_g, w_uq=w_uq, kv_norm_g=kv_norm_g, w_ukv=w_ukv,
                   w_branch_b=w_branch_b, w_out=w_out, post_norm1_g=post_norm1_g, pre_norm2_g=pre_norm2_g, w_up=w_up, conv_w=conv_w,
                   conv_b=conv_b, w_down=w_down, post_norm2_g=post_norm2_g)
    mom1 = dict(w_ada=m_w_ada, b_ada=m_b_ada, pre_norm1_g=m_pre_norm1_g, w_in=m_w_in, gm_ln_g=m_gm_ln_g, gm_ln_b=m_gm_ln_b,
                gm_w_s=m_gm_w_s, gm_b_s=m_gm_b_s, w_branch_a=m_w_branch_a, q_norm_g=m_q_norm_g, w_uq=m_w_uq, kv_norm_g=m_kv_norm_g,
                w_ukv=m_w_ukv, w_branch_b=m_w_branch_b, w_out=m_w_out, post_norm1_g=m_post_norm1_g, pre_norm2_g=m_pre_norm2_g,
                w_up=m_w_up, conv_w=m_conv_w, conv_b=m_conv_b, w_down=m_w_down, post_norm2_g=m_post_norm2_g)
    mom2 = dict(w_ada=v_w_ada, b_ada=v_b_ada, pre_norm1_g=v_pre_norm1_g, w_in=v_w_in, gm_ln_g=v_gm_ln_g, gm_ln_b=v_gm_ln_b,
                gm_w_s=v_gm_w_s, gm_b_s=v_gm_b_s, w_branch_a=v_w_branch_a, q_norm_g=v_q_norm_g, w_uq=v_w_uq, kv_norm_g=v_kv_norm_g,
                w_ukv=v_w_ukv, w_branch_b=v_w_branch_b, w_out=v_w_out, post_norm1_g=v_post_norm1_g, pre_norm2_g=v_pre_norm2_g,
                w_up=v_w_up, conv_w=v_conv_w, conv_b=v_conv_b, w_down=v_w_down, post_norm2_g=v_post_norm2_g)
    order = list(weights)
    _TOKENS.clear()

    s, d = x.shape[1], x.shape[2]
    gmw = gm_ln_g.shape[0]
    groups = gmw // CHUNK
    ql, kvl = q_norm_g.shape[0], kv_norm_g.shape[0]
    f2 = conv_b.shape[0]
    in_cols = w_in.shape[1] * N_DEV
    o_q, o_kv, o_ga, o_gb, o_kpe = 2 * gmw, 2 * gmw + ql, 2 * gmw + ql + kvl, 2 * gmw + ql + kvl + d, 2 * gmw + ql + kvl + 2 * d
    zp = _round_up(o_kpe + LANES, Z_PAD)
    src_kpe = 2 * gmw + ql + kvl
    assert src_kpe + QK_ROPE + 2 * d == in_cols
    my = 4 * lax.axis_index("x") + 2 * lax.axis_index("y") + lax.axis_index("c")

    x2, tgt = x[0], loss_target[0]
    row = lambda a: a.reshape(1, -1)

    big = ["w_in", "w_branch_a", "w_uq", "w_ukv", "w_branch_b", "w_out", "w_up", "w_down"]
    sh = {k: weights[k].astype(BF16) for k in big}
    mix = ["w_branch_a", "w_uq", "w_ukv", "w_branch_b", "w_out"]
    ag_in = ag_start("ag_start_in", [sh["w_in"]], c)

    mod8, sc_all3, g_cw = ada_fwd(c, w_ada, b_ada.reshape(N_DEV, 1, -1), conv_w)
    mod = mod8.reshape(N_MOD, d)
    shift1, scale1, gate1, shift2, scale2, gate2 = (mod[i:i + 1] for i in range(N_MOD))
    sc_all = sc_all3.reshape(N_DEV, d)
    h1 = norm_mod_fwd("pre1_fwd", x2, row(pre_norm1_g), scale1, shift1)

    (g_in,) = ag_wait("ag_wait_in", ag_forward("ag_forward_in", ag_in, h1), h1)
    ag_mix = ag_start("ag_start_mix", [sh[k] for k in mix], g_in)
    w_in_f = g_in.transpose(1, 0, 2).reshape(d, in_cols)
    w_in_p = jnp.concatenate([w_in_f[:, :src_kpe], w_in_f[:, src_kpe + QK_ROPE:], w_in_f[:, src_kpe:src_kpe + QK_ROPE],
                              jnp.zeros((d, zp - in_cols), BF16)], axis=1)

    inv = ROPE_THETA ** (-jnp.arange(0, QK_ROPE, 2, dtype=F32) / QK_ROPE)
    ang = positions[0].astype(F32)[:, None] * inv
    cos4 = jnp.tile(jnp.cos(ang), (1, 4))
    sin4 = jnp.tile(jnp.concatenate([-jnp.sin(ang), jnp.sin(ang)], axis=1), (1, 2))

    wm = (gm_w_s * jnp.tril(jnp.ones((CHUNK, CHUNK), F32))).astype(BF16)
    bs3 = gm_b_s.reshape(groups, CHUNK, 1)
    ln_g, ln_b = row(gm_ln_g), row(gm_ln_b)

    z = mm_nn("z_proj", h1, w_in_p, F32)
    ag_mix = ag_forward("ag_forward_mix", ag_mix, z)
    a = gmlp_fwd(z, gmw, ln_g, ln_b, wm, bs3)
    g_a, g_uq, g_ukv, g_b, g_out = ag_wait("ag_wait_mix", ag_mix, a)
    ag_up = ag_start("ag_start_up", [sh["w_up"]], g_out)
    w_a_f, w_b_f, w_out_f = g_a.reshape(-1, d), g_b.reshape(-1, d), g_out.reshape(-1, d)
    w_uq_f = g_uq.transpose(1, 0, 2).reshape(ql, HEADS, QK_NOPE + QK_ROPE)
    w_uq_n = w_uq_f[:, :, :QK_NOPE].reshape(ql, HEADS * QK_NOPE)
    w_uq_r = w_uq_f[:, :, QK_NOPE:].reshape(ql, HEADS * QK_ROPE)
    y_a = mm_nn("branch_a", a, w_a_f, F32)
    qln = rms_fwd_cols("q_norm", z, o_q, ql, row(q_norm_g))
    kvn = rms_fwd_cols("kv_norm", z, o_kv, kvl, row(kv_norm_g))
    qn = mm_nn("q_nope", qln, w_uq_n, BF16)
    qp = mm_nn("q_rope", qln, w_uq_r, F32)
    kv = mm_nn_b3("kv_up", kvn, g_ukv, BF16)
    kpr = rope_k(z, o_kpe, cos4, sin4)
    o, qpr, lse = attn_fwd2(qn, qp, kv, kpr, cos4, sin4)
    ag_up = ag_forward("ag_forward_up", ag_up, o)
    y_b = mm_nn("branch_b", o, w_b_f, F32)
    merged = merge_fwd(z, o_ga, o_gb, y_a, y_b)
    y1 = mm_nn("out_proj", merged, w_out_f, F32)
    x1 = post_res_fwd("post1_fwd", x2, y1, gate1, row(post_norm1_g))
    h2 = norm_mod_fwd("pre2_fwd", x1, row(pre_norm2_g), scale2, shift2)
    (g_up,) = ag_wait("ag_wait_up", ag_up, h2)
    ag_down = ag_start("ag_start_down", [sh["w_down"]], g_up)
    upre = mm_nn_b3("up_proj", h2, g_up, F32)
    ag_down = ag_forward("ag_forward_down", ag_down, upre)
    cw = g_cw.transpose(1, 0, 2).reshape(3, f2)
    cb = row(conv_b)
    f = conv_act_fwd(upre, cw, cb)
    w_down_f = ag_wait("ag_wait_down", ag_down, f)[0].reshape(-1, d)
    ffn = mm_nn("down_proj", f, w_down_f, F32)
    loss_acc, dout, dffn, acc2 = post2_loss_bwd(x1, ffn, tgt, gate2, row(post_norm2_g))

    blocks = lambda g: g.reshape(N_DEV, g.shape[0] // N_DEV, g.shape[1])
    core = lax.axis_index("c").astype(jnp.int32).reshape(1)
    rs = {}

    def rs_begin(key, grads):
        rs[key] = rs_d2d_start("rs_d2d_start_" + key, grads)

    def rs_middle(key, after):
        grads, lands = rs_d2d_wait("rs_d2d_wait_" + key, rs[key], after)
        sums = [pair_sum("pair_sum_%s_%d" % (key, i), g, l, core) for i, (g, l) in enumerate(zip(grads, lands))]
        rs[key] = rs_ici_start("rs_ici_start_" + key, sums)

    gw_down = mm_tn("g_w_down", f, dffn, BF16)
    rs_begin("down", [blocks(gw_down)])
    df = mm_nt("d_f", dffn, w_down_f, F32)
    rs_middle("down", df)
    dup_g, dup_v, gcw_g, gcw_v, gcb_g, gcb_v = conv_act_bwd(upre, cw, cb, df)
    dupre = jnp.concatenate([dup_g, dup_v], axis=1)
    gw_up3 = mm_tn_o3("g_w_up", h2, dupre, N_DEV, BF16)
    rs_begin("up", [gw_up3])
    dh2 = mm_nt_b3("d_h2", dupre, g_up, F32)
    rs_middle("up", dh2)
    dx1, dy1, acc_mid = mid_bwd(dh2, dout, x1, y1, row(pre_norm2_g), scale2, gate1, row(post_norm1_g))
    gw_out = mm_tn("g_w_out", merged, dy1, BF16)
    dmerged = mm_nt("d_merged", dy1, w_out_f, F32)
    dya, dyb, dga, dgb = merge_bwd(z, o_ga, o_gb, y_a, y_b, dmerged)
    gw_a = mm_tn("g_w_a", a, dya, BF16)
    gw_b = mm_tn("g_w_b", o, dyb, BF16)
    rs_begin("mid", [blocks(gw_out), blocks(gw_a), blocks(gw_b)])
    da = mm_nt("d_a", dya, w_a_f, F32)
    do = mm_nt("d_o", dyb, w_b_f, F32)
    rs_middle("mid", do)
    duv, g_ws, g_bs3, acc_gm = gmlp_bwd(z, gmw, da, ln_g, ln_b, wm, bs3)
    dqn, dqp, dkv, dkp = attn_bwd2(qn, qpr, kv, kpr, o, do, lse, cos4, sin4)
    dkpe = kpe_bwd(dkp, cos4, sin4, zp - o_kpe)
    dq_cat = jnp.concatenate([dqn, dqp], axis=1)
    w_uq_cat = jnp.concatenate([w_uq_n, w_uq_r], axis=1)
    gw_uq_cat = mm_tn("g_w_uq", qln, dq_cat, BF16)
    gw_uq_f = jnp.concatenate([gw_uq_cat[:, :HEADS * QK_NOPE].reshape(ql, HEADS, QK_NOPE),
                               gw_uq_cat[:, HEADS * QK_NOPE:].reshape(ql, HEADS, QK_ROPE)], axis=2)
    gw_uq3 = gw_uq_f.reshape(ql, N_DEV, -1).transpose(1, 0, 2)
    gw_ukv3 = mm_tn_o3("g_w_ukv", kvn, dkv, N_DEV, BF16)
    rs_begin("mla", [gw_uq3, gw_ukv3])
    dqln = mm_nt("d_qln", dq_cat, w_uq_cat, F32)
    dq_lat, g_qnorm = rms_bwd_cols("q_norm_bwd", dqln, z, o_q, ql, row(q_norm_g))
    dkvn = mm_nt_b3("d_kvn", dkv, g_ukv, F32)
    rs_middle("mla", dkvn)
    dkv_lat, g_kvnorm = rms_bwd_cols("kv_norm_bwd", dkvn, z, o_kv, kvl, row(kv_norm_g))
    dz = jnp.concatenate([duv, dq_lat, dkv_lat, dga, dgb, dkpe], axis=1)
    gw_in_p = mm_tn("g_w_in", h1, dz, BF16)
    gw_in_f = jnp.concatenate([gw_in_p[:, :src_kpe], gw_in_p[:, o_kpe:o_kpe + QK_ROPE], gw_in_p[:, src_kpe:o_kpe]], axis=1)
    gw_in3 = gw_in_f.reshape(d, N_DEV, -1).transpose(1, 0, 2)
    rs_begin("in", [gw_in3])
    dh1 = mm_nt("d_h1", dz, w_in_p, F32)
    grad_x, acc1 = pre1_bwd(dh1, dx1, x2, row(pre_norm1_g), scale1)

    dmod = jnp.concatenate([acc1[0], acc1[1], acc_mid[3], acc_mid[0], acc_mid[1], acc2[0]])
    small = [("pre_norm1_g", acc1[2]), ("gm_ln_g", acc_gm[0]), ("gm_ln_b", acc_gm[1]), ("gm_b_s", g_bs3.reshape(-1)),
             ("q_norm_g", g_qnorm[0]), ("kv_norm_g", g_kvnorm[0]), ("post_norm1_g", acc_mid[4]), ("pre_norm2_g", acc_mid[2]),
             ("conv_b", jnp.concatenate([gcb_g[0], gcb_v[0]])), ("post_norm2_g", acc2[1]), ("gm_w_s", g_ws.reshape(-1)),
             ("b_ada", dmod)]
    n_small = sum(v.shape[0] for _, v in small)
    n_cw = 3 * f2
    n_pack = _round_up(n_small + n_cw, PACK_ALIGN)
    tail = jnp.zeros((n_pack - n_small - n_cw,), F32)
    packed = jnp.concatenate([v for _, v in small] + [jnp.concatenate([gcw_g, gcw_v], axis=1).reshape(-1), tail])
    ag_small = ag_start("ag_start_small", [packed.reshape(-1, LANES)], packed)
    rs_middle("in", packed)

    res = {}
    last = packed
    for key, names in (("down", ["w_down"]), ("up", ["w_up"]), ("mid", ["w_out", "w_branch_a", "w_branch_b"]), ("mla", ["w_uq", "w_ukv"])):
        parts = rs_ici_wait("rs_ici_wait_" + key, rs[key], last)
        for k, p in zip(names, parts):
            res[k] = adamw("adamw_" + k, weights[k], mom1[k], mom2[k], p)
            last = res[k][0]

    def pack(src):
        return jnp.concatenate([src[k].reshape(-1) for k, _ in small] + [jnp.zeros((n_pack - n_small,), F32)]).reshape(-1, LANES)

    (gathered,) = ag_wait("ag_wait_small", ag_forward("ag_forward_small", ag_small, last), last)
    sm = [t.reshape(-1) for t in adamw("adamw_small", pack(weights), pack(mom1), pack(mom2), gathered)]
    off = 0
    for k, v in small:
        res[k] = tuple(t[off:off + v.shape[0]].reshape(weights[k].shape) for t in sm)
        off += v.shape[0]

    cs_cw = conv_w.shape[1]
    g_cw_full = sm[0][n_small:n_small + n_cw].reshape(3, f2)
    g_cw_mine = lax.dynamic_slice(g_cw_full, (0, my * cs_cw), (3, cs_cw))
    res["conv_w"] = adamw("adamw_conv_w", conv_w, mom1["conv_w"], mom2["conv_w"], g_cw_mine[None])

    cs_ada = w_ada.shape[1]
    off_b = n_small - N_MOD * d
    dmod_all = gathered.reshape(N_DEV, -1)[:, off_b:off_b + N_MOD * d]
    dmod_cols = lax.dynamic_slice(dmod_all, (0, my * cs_ada), (N_DEV, cs_ada))
    res["w_ada"] = adamw("adamw_w_ada", w_ada, mom1["w_ada"], mom2["w_ada"], ada_bwd_w(sc_all, dmod_cols))

    (p_in,) = rs_ici_wait("rs_ici_wait_in", rs["in"], res["w_ada"][0])
    res["w_in"] = adamw("adamw_w_in", w_in, mom1["w_in"], mom2["w_in"], p_in)

    _TOKENS.clear()
    loss = lax.psum(loss_acc[0, 0], ("x", "y", "c"))
    outs = [loss, grad_x[None]]
    for i in range(4):
        outs += [res[k][i] for k in order]
    return tuple(outs)
```

```python
import functools

import jax
import jax.numpy as jnp
from jax import lax
from jax.experimental import pallas as pl
from jax.experimental.pallas import tpu as pltpu

F32 = jnp.float32
BF16 = jnp.bfloat16

N_DEV = 8
HEADS = 16
QK_NOPE = 128
QK_ROPE = 64
V_HEAD = 128
CHUNK = 128
ROPE_THETA = 10000.0
EPS = 1e-6
N_MOD = 6
ADAM_LR, ADAM_B1, ADAM_B2, ADAM_EPS, ADAM_WD, ADAM_STEP = 0.001, 0.9, 0.999, 1e-08, 0.01, 10

LANES = 128
VMEM_LIMIT_BYTES = 48 * 2 ** 20
ROW_TILE = 256
COL_TILE = 256
ATT_TILE = 256
Z_PAD = 512
ADAMW_TILE_ELEMS = 1 << 18
PACK_ALIGN = 8 * LANES
MESH = pl.DeviceIdType.MESH


def _params(*sem):
    return pltpu.CompilerParams(dimension_semantics=sem if sem else None, vmem_limit_bytes=VMEM_LIMIT_BYTES)


def _tile(dim, target):
    t = (min(dim, target) // LANES) * LANES
    while t >= LANES:
        if dim % t == 0:
            return t
        t -= LANES
    return dim


def _full(shape):
    nd = len(shape)
    return pl.BlockSpec(shape, lambda *_: (0,) * nd)


class _Tokens:
    KEEP = 2

    def __init__(self):
        self.pending = []

    def push(self, token):
        self.pending = (self.pending + [token])[-self.KEEP:]

    def take(self):
        return list(self.pending)

    def clear(self):
        self.pending = []


_TOKENS = _Tokens()


def _matmul(name, a, b, *, grid, a_spec, b_spec, o_spec, out_shape, contract, acc_shape):
    nk = grid[2]
    deps = _TOKENS.take()

    def product(a_ref, b_ref):
        return lax.dot_general(a_ref[...].astype(BF16), b_ref[...].astype(BF16), (contract, ((), ())), preferred_element_type=F32)

    def body_one_step(a_ref, b_ref, *rest):
        o_ref = rest[len(deps)]
        o_ref[...] = product(a_ref, b_ref).astype(o_ref.dtype)

    def body(a_ref, b_ref, *rest):
        o_ref, acc_ref = rest[len(deps):]
        k = pl.program_id(2)

        @pl.when(k == 0)
        def _():
            acc_ref[...] = jnp.zeros_like(acc_ref)

        acc_ref[...] += product(a_ref, b_ref)

        @pl.when(k == nk - 1)
        def _():
            o_ref[...] = acc_ref[...].astype(o_ref.dtype)

    return pl.pallas_call(
        body_one_step if nk == 1 else body, name=name, grid=grid,
        in_specs=[a_spec, b_spec] + [pl.BlockSpec(memory_space=pl.ANY)] * len(deps),
        out_specs=o_spec, out_shape=out_shape, scratch_shapes=[] if nk == 1 else [pltpu.VMEM(acc_shape, F32)],
        compiler_params=_params("parallel", "parallel", "arbitrary"))(a, b, *deps)


TM, TN, TK = 1024, 1024, 2304


def _tk(a, b):
    return TK if a.dtype == BF16 and b.dtype == BF16 else TK // 2


def mm_nn(name, a, b, dtype):
    (m, k), n = a.shape, b.shape[1]
    tm, tn, tk = _tile(m, TM), _tile(n, TN), _tile(k, _tk(a, b))
    return _matmul(name, a, b, grid=(m // tm, n // tn, k // tk),
                   a_spec=pl.BlockSpec((tm, tk), lambda i, j, kk: (i, kk)),
                   b_spec=pl.BlockSpec((tk, tn), lambda i, j, kk: (kk, j)),
                   o_spec=pl.BlockSpec((tm, tn), lambda i, j, kk: (i, j)),
                   out_shape=jax.ShapeDtypeStruct((m, n), dtype), contract=((1,), (0,)), acc_shape=(tm, tn))


def mm_nn_b3(name, a, b3, dtype):
    (m, k), (nj, _, cs) = a.shape, b3.shape
    tm, tk = _tile(m, TM), _tile(k, _tk(a, b3))
    return _matmul(name, a, b3, grid=(m // tm, nj, k // tk),
                   a_spec=pl.BlockSpec((tm, tk), lambda i, j, kk: (i, kk)),
                   b_spec=pl.BlockSpec((None, tk, cs), lambda i, j, kk: (j, kk, 0)),
                   o_spec=pl.BlockSpec((tm, cs), lambda i, j, kk: (i, j)),
                   out_shape=jax.ShapeDtypeStruct((m, nj * cs), dtype), contract=((1,), (0,)), acc_shape=(tm, cs))


def mm_nt(name, a, b, dtype):
    (m, k), n = a.shape, b.shape[0]
    tm, tn, tk = _tile(m, TM), _tile(n, TN), _tile(k, _tk(a, b))
    return _matmul(name, a, b, grid=(m // tm, n // tn, k // tk),
                   a_spec=pl.BlockSpec((tm, tk), lambda i, j, kk: (i, kk)),
                   b_spec=pl.BlockSpec((tn, tk), lambda i, j, kk: (j, kk)),
                   o_spec=pl.BlockSpec((tm, tn), lambda i, j, kk: (i, j)),
                   out_shape=jax.ShapeDtypeStruct((m, n), dtype), contract=((1,), (1,)), acc_shape=(tm, tn))


def mm_nt_b3(name, a, b3, dtype):
    m, (nj, n, cs) = a.shape[0], b3.shape
    tm, tn = _tile(m, TM), _tile(n, TN)
    return _matmul(name, a, b3, grid=(m // tm, n // tn, nj),
                   a_spec=pl.BlockSpec((tm, cs), lambda i, j, kk: (i, kk)),
                   b_spec=pl.BlockSpec((None, tn, cs), lambda i, j, kk: (kk, j, 0)),
                   o_spec=pl.BlockSpec((tm, tn), lambda i, j, kk: (i, j)),
                   out_shape=jax.ShapeDtypeStruct((m, n), dtype), contract=((1,), (1,)), acc_shape=(tm, tn))


def mm_nt_h3(name, a3, b3, dtype):
    (_, m, _), (nj, n, cs) = a3.shape, b3.shape
    tm, tn, hj = _tile(m, TM), _tile(n, TN), nj // 2
    return _matmul(name, a3, b3, grid=(m // tm, n // tn, nj),
                   a_spec=pl.BlockSpec((None, tm, cs), lambda i, j, kk: (kk // hj, i, kk % hj)),
                   b_spec=pl.BlockSpec((None, tn, cs), lambda i, j, kk: (kk, j, 0)),
                   o_spec=pl.BlockSpec((tm, tn), lambda i, j, kk: (i, j)),
                   out_shape=jax.ShapeDtypeStruct((m, n), dtype), contract=((1,), (1,)), acc_shape=(tm, tn))


def mm_tn_h3(name, a, b3, nj, dtype):
    (k, m), half = a.shape, b3.shape[2]
    hj = nj // 2
    cs = half // hj
    tm, tk = _tile(m, TM), _tile(k, _tk(a, b3))
    return _matmul(name, a, b3, grid=(m // tm, nj, k // tk),
                   a_spec=pl.BlockSpec((tk, tm), lambda i, j, kk: (kk, i)),
                   b_spec=pl.BlockSpec((None, tk, cs), lambda i, j, kk: (j // hj, kk, j % hj)),
                   o_spec=pl.BlockSpec((None, tm, cs), lambda i, j, kk: (j, i, 0)),
                   out_shape=jax.ShapeDtypeStruct((nj, m, cs), dtype), contract=((0,), (0,)), acc_shape=(tm, cs))


def mm_tn(name, a, b, dtype):
    (k, m), n = a.shape, b.shape[1]
    tm, tn, tk = _tile(m, TM), _tile(n, TN), _tile(k, _tk(a, b))
    return _matmul(name, a, b, grid=(m // tm, n // tn, k // tk),
                   a_spec=pl.BlockSpec((tk, tm), lambda i, j, kk: (kk, i)),
                   b_spec=pl.BlockSpec((tk, tn), lambda i, j, kk: (kk, j)),
                   o_spec=pl.BlockSpec((tm, tn), lambda i, j, kk: (i, j)),
                   out_shape=jax.ShapeDtypeStruct((m, n), dtype), contract=((0,), (0,)), acc_shape=(tm, tn))


def mm_tn_o3(name, a, b, nj, dtype):
    (k, m), n = a.shape, b.shape[1]
    cs = n // nj
    tm, tk = _tile(m, TM), _tile(k, _tk(a, b))
    return _matmul(name, a, b, grid=(m // tm, nj, k // tk),
                   a_spec=pl.BlockSpec((tk, tm), lambda i, j, kk: (kk, i)),
                   b_spec=pl.BlockSpec((tk, cs), lambda i, j, kk: (kk, j)),
                   o_spec=pl.BlockSpec((None, tm, cs), lambda i, j, kk: (j, i, 0)),
                   out_shape=jax.ShapeDtypeStruct((nj, m, cs), dtype), contract=((0,), (0,)), acc_shape=(tm, cs))


_GELU_C = 0.7978845608028654
_GELU_A = 0.044715


def _gelu(x):
    return 0.5 * x * (1.0 + jnp.tanh(_GELU_C * (x + _GELU_A * x * x * x)))


def _gelu_and_grad(x):
    t = jnp.tanh(_GELU_C * (x + _GELU_A * x * x * x))
    y = 0.5 * x * (1.0 + t)
    dy = 0.5 * (1.0 + t) + 0.5 * x * (1.0 - t * t) * (_GELU_C * (1.0 + 3.0 * _GELU_A * x * x))
    return y, dy


def _sigmoid(x):
    return 1.0 / (1.0 + jnp.exp(-x))


def _rms_stats(x):
    inv = lax.rsqrt(jnp.mean(x * x, axis=-1, keepdims=True) + EPS)
    return inv, x * inv


def _rms_bwd(dyhat, yhat, inv):
    return inv * (dyhat - yhat * jnp.mean(dyhat * yhat, axis=-1, keepdims=True))


def _colsum(x):
    return jnp.sum(x, axis=0, keepdims=True)


def _rope(x, cos4, sin4):
    lane = lax.broadcasted_iota(jnp.int32, x.shape, x.ndim - 1)
    first_half = (lane % QK_ROPE) < (QK_ROPE // 2)
    partner = jnp.where(first_half, pltpu.roll(x, LANES - QK_ROPE // 2, x.ndim - 1), pltpu.roll(x, QK_ROPE // 2, x.ndim - 1))
    return x * cos4 + partner * sin4


def norm_mod_fwd(name, x, g, scale, shift):
    s, d = x.shape
    tr = _tile(s, ROW_TILE)

    def body(x_ref, g_ref, sc_ref, sh_ref, o_ref):
        _, xh = _rms_stats(x_ref[...])
        o_ref[...] = (xh * g_ref[...] * (1.0 + sc_ref[...]) + sh_ref[...]).astype(o_ref.dtype)

    row = pl.BlockSpec((tr, d), lambda i: (i, 0))
    vec = pl.BlockSpec((1, d), lambda i: (0, 0))
    return pl.pallas_call(body, name=name, grid=(s // tr,), in_specs=[row, vec, vec, vec], out_specs=row,
                          out_shape=jax.ShapeDtypeStruct((s, d), BF16), compiler_params=_params("parallel"))(x, g, scale, shift)


def rms_fwd_cols(name, z, off, width, g):
    s = z.shape[0]
    tr = _tile(s, ROW_TILE)
    assert off % width == 0

    def body(x_ref, g_ref, o_ref):
        _, xh = _rms_stats(x_ref[...])
        o_ref[...] = (xh * g_ref[...]).astype(o_ref.dtype)

    return pl.pallas_call(body, name=name, grid=(s // tr,),
                          in_specs=[pl.BlockSpec((tr, width), lambda i: (i, off // width)), pl.BlockSpec((1, width), lambda i: (0, 0))],
                          out_specs=pl.BlockSpec((tr, width), lambda i: (i, 0)),
                          out_shape=jax.ShapeDtypeStruct((s, width), BF16), compiler_params=_params("parallel"))(z, g)


def rms_bwd_cols(name, dy, z, off, width, g):
    s = z.shape[0]
    tr = _tile(s, ROW_TILE)

    def body(dy_ref, x_ref, g_ref, dx_ref, gg_ref):
        @pl.when(pl.program_id(0) == 0)
        def _():
            gg_ref[...] = jnp.zeros_like(gg_ref)

        inv, xh = _rms_stats(x_ref[...])
        dy_v = dy_ref[...]
        gg_ref[...] += _colsum(dy_v * xh)
        dx_ref[...] = _rms_bwd(dy_v * g_ref[...], xh, inv).astype(dx_ref.dtype)

    return pl.pallas_call(body, name=name, grid=(s // tr,),
                          in_specs=[pl.BlockSpec((tr, width), lambda i: (i, 0)), pl.BlockSpec((tr, width), lambda i: (i, off // width)),
                                    pl.BlockSpec((1, width), lambda i: (0, 0))],
                          out_specs=[pl.BlockSpec((tr, width), lambda i: (i, 0)), pl.BlockSpec((1, width), lambda i: (0, 0))],
                          out_shape=[jax.ShapeDtypeStruct((s, width), BF16), jax.ShapeDtypeStruct((1, width), F32)],
                          compiler_params=_params("arbitrary"))(dy, z, g)


def post_res_fwd(name, x, y, gate, g):
    s, d = x.shape
    tr = _tile(s, ROW_TILE)

    def body(x_ref, y_ref, gate_ref, g_ref, o_ref):
        _, yh = _rms_stats(y_ref[...])
        o_ref[...] = x_ref[...] + gate_ref[...] * (yh * g_ref[...])

    row = pl.BlockSpec((tr, d), lambda i: (i, 0))
    vec = pl.BlockSpec((1, d), lambda i: (0, 0))
    return pl.pallas_call(body, name=name, grid=(s // tr,), in_specs=[row, row, vec, vec], out_specs=row,
                          out_shape=jax.ShapeDtypeStruct((s, d), F32), compiler_params=_params("parallel"))(x, y, gate, g)


def post2_loss_bwd(x1, ffn, target, gate2, g):
    s, d = x1.shape
    tr = _tile(s, ROW_TILE)

    def body(x_ref, y_ref, t_ref, gate_ref, g_ref, loss_ref, dout_ref, dy_ref, acc_ref):
        @pl.when(pl.program_id(0) == 0)
        def _():
            loss_ref[...] = jnp.zeros_like(loss_ref)
            acc_ref[...] = jnp.zeros_like(acc_ref)

        inv, yh = _rms_stats(y_ref[...])
        r = yh * g_ref[...]
        err = x_ref[...] + gate_ref[...] * r - t_ref[...]
        loss_ref[...] += 0.5 * jnp.sum(jnp.mean(err * err, axis=-1, keepdims=True))
        dout = err / d
        dout_ref[...] = dout
        dr = dout * gate_ref[...]
        acc_ref[0:1, :] += _colsum(dout * r)
        acc_ref[1:2, :] += _colsum(dr * yh)
        dy_ref[...] = _rms_bwd(dr * g_ref[...], yh, inv).astype(dy_ref.dtype)

    row = pl.BlockSpec((tr, d), lambda i: (i, 0))
    vec = pl.BlockSpec((1, d), lambda i: (0, 0))
    return pl.pallas_call(
        body, name="post2_loss_bwd", grid=(s // tr,), in_specs=[row, row, row, vec, vec],
        out_specs=[_full((8, LANES)), row, row, _full((8, d))],
        out_shape=[jax.ShapeDtypeStruct((8, LANES), F32), jax.ShapeDtypeStruct((s, d), F32),
                   jax.ShapeDtypeStruct((s, d), BF16), jax.ShapeDtypeStruct((8, d), F32)],
        compiler_params=_params("arbitrary"))(x1, ffn, target, gate2, g)


def mid_bwd(dh2, dout, x1, y1, pre2_g, scale2, gate1, post1_g):
    s, d = x1.shape
    tr = _tile(s, ROW_TILE)

    def body(dh_ref, dout_ref, x_ref, y_ref, g2_ref, sc_ref, gate_ref, g1_ref, dx_ref, dy_ref, acc_ref):
        @pl.when(pl.program_id(0) == 0)
        def _():
            acc_ref[...] = jnp.zeros_like(acc_ref)

        dh = dh_ref[...]
        inv2, xh = _rms_stats(x_ref[...])
        acc_ref[0:1, :] += _colsum(dh)
        acc_ref[1:2, :] += _colsum(dh * (xh * g2_ref[...]))
        t = dh * (1.0 + sc_ref[...])
        acc_ref[2:3, :] += _colsum(t * xh)
        dx1 = dout_ref[...] + _rms_bwd(t * g2_ref[...], xh, inv2)
        dx_ref[...] = dx1
        inv1, yh = _rms_stats(y_ref[...])
        acc_ref[3:4, :] += _colsum(dx1 * (yh * g1_ref[...]))
        dr = dx1 * gate_ref[...]
        acc_ref[4:5, :] += _colsum(dr * yh)
        dy_ref[...] = _rms_bwd(dr * g1_ref[...], yh, inv1).astype(dy_ref.dtype)

    row = pl.BlockSpec((tr, d), lambda i: (i, 0))
    vec = pl.BlockSpec((1, d), lambda i: (0, 0))
    return pl.pallas_call(
        body, name="mid_bwd", grid=(s // tr,), in_specs=[row, row, row, row, vec, vec, vec, vec],
        out_specs=[row, row, _full((8, d))],
        out_shape=[jax.ShapeDtypeStruct((s, d), F32), jax.ShapeDtypeStruct((s, d), BF16), jax.ShapeDtypeStruct((8, d), F32)],
        compiler_params=_params("arbitrary"))(dh2, dout, x1, y1, pre2_g, scale2, gate1, post1_g)


def pre1_bwd(dh1, dx1, x, pre1_g, scale1):
    s, d = x.shape
    tr = _tile(s, ROW_TILE)

    def body(dh_ref, dx1_ref, x_ref, g_ref, sc_ref, dx_ref, acc_ref):
        @pl.when(pl.program_id(0) == 0)
        def _():
            acc_ref[...] = jnp.zeros_like(acc_ref)

        dh = dh_ref[...]
        inv, xh = _rms_stats(x_ref[...])
        acc_ref[0:1, :] += _colsum(dh)
        acc_ref[1:2, :] += _colsum(dh * (xh * g_ref[...]))
        t = dh * (1.0 + sc_ref[...])
        acc_ref[2:3, :] += _colsum(t * xh)
        dx_ref[...] = dx1_ref[...] + _rms_bwd(t * g_ref[...], xh, inv)

    row = pl.BlockSpec((tr, d), lambda i: (i, 0))
    vec = pl.BlockSpec((1, d), lambda i: (0, 0))
    return pl.pallas_call(
        body, name="pre1_bwd", grid=(s // tr,), in_specs=[row, row, row, vec, vec], out_specs=[row, _full((8, d))],
        out_shape=[jax.ShapeDtypeStruct((s, d), F32), jax.ShapeDtypeStruct((8, d), F32)],
        compiler_params=_params("arbitrary"))(dh1, dx1, x, pre1_g, scale1)


def _ln_stats(v):
    mu = jnp.mean(v, axis=-1, keepdims=True)
    vc = v - mu
    rstd = lax.rsqrt(jnp.mean(vc * vc, axis=-1, keepdims=True) + EPS)
    return rstd, vc * rstd


def gmlp_fwd(z, width, ln_g, ln_b, wm, bs3):
    s = z.shape[0]
    groups = width // CHUNK

    def body(u_ref, v_ref, g_ref, b_ref, wm_ref, bs_ref, a_ref):
        ug = _gelu(u_ref[...])
        _, vh = _ln_stats(_gelu(v_ref[...]))
        vn = (vh * g_ref[...] + b_ref[...]).astype(BF16)
        for g in range(groups):
            cols = slice(g * CHUNK, (g + 1) * CHUNK)
            mixed = jnp.dot(wm_ref[g], vn[:, cols], preferred_element_type=F32) + bs_ref[g]
            a_ref[:, cols] = (ug[:, cols] * mixed).astype(a_ref.dtype)

    vec = pl.BlockSpec((1, width), lambda n: (0, 0))
    return pl.pallas_call(
        body, name="gmlp_fwd", grid=(s // CHUNK,),
        in_specs=[pl.BlockSpec((CHUNK, width), lambda n: (n, 0)), pl.BlockSpec((CHUNK, width), lambda n: (n, 1)), vec, vec,
                  _full(wm.shape), _full(bs3.shape)],
        out_specs=pl.BlockSpec((CHUNK, width), lambda n: (n, 0)),
        out_shape=jax.ShapeDtypeStruct((s, width), BF16), compiler_params=_params("parallel"))(z, z, ln_g, ln_b, wm, bs3)


def gmlp_bwd(z, width, da, ln_g, ln_b, wm, bs3):
    s = z.shape[0]
    groups = width // CHUNK

    def body(u_ref, v_ref, da_ref, g_ref, b_ref, wm_ref, bs_ref, duv_ref, gw_ref, gb_ref, acc_ref, dvn_ref):
        @pl.when(pl.program_id(0) == 0)
        def _():
            gw_ref[...] = jnp.zeros_like(gw_ref)
            gb_ref[...] = jnp.zeros_like(gb_ref)
            acc_ref[...] = jnp.zeros_like(acc_ref)

        ug, dug = _gelu_and_grad(u_ref[...])
        vg, dvg = _gelu_and_grad(v_ref[...])
        rstd, vh = _ln_stats(vg)
        vn = (vh * g_ref[...] + b_ref[...]).astype(BF16)
        da_v = da_ref[...]
        for g in range(groups):
            cols = slice(g * CHUNK, (g + 1) * CHUNK)
            mixed = jnp.dot(wm_ref[g], vn[:, cols], preferred_element_type=F32) + bs_ref[g]
            duv_ref[:, cols] = (da_v[:, cols] * mixed * dug[:, cols]).astype(duv_ref.dtype)
            dm = da_v[:, cols] * ug[:, cols]
            gb_ref[g] += jnp.sum(dm, axis=-1, keepdims=True)
            dmb = dm.astype(BF16)
            gw_ref[g] += lax.dot_general(dmb, vn[:, cols], (((1,), (1,)), ((), ())), preferred_element_type=F32)
            dvn_ref[:, cols] = lax.dot_general(wm_ref[g], dmb, (((0,), (0,)), ((), ())), preferred_element_type=F32)
        dvn = dvn_ref[...]
        acc_ref[0:1, :] += _colsum(dvn * vh)
        acc_ref[1:2, :] += _colsum(dvn)
        dvh = dvn * g_ref[...]
        dv = rstd * (dvh - jnp.mean(dvh, axis=-1, keepdims=True) - vh * jnp.mean(dvh * vh, axis=-1, keepdims=True))
        duv_ref[:, width:] = (dv * dvg).astype(duv_ref.dtype)

        @pl.when(pl.program_id(0) == pl.num_programs(0) - 1)
        def _():
            q = lax.broadcasted_iota(jnp.int32, gw_ref.shape, 1)
            p = lax.broadcasted_iota(jnp.int32, gw_ref.shape, 2)
            gw_ref[...] = jnp.where(p <= q, gw_ref[...], 0.0)

    vec = pl.BlockSpec((1, width), lambda n: (0, 0))
    blk = pl.BlockSpec((CHUNK, width), lambda n: (n, 0))
    return pl.pallas_call(
        body, name="gmlp_bwd", grid=(s // CHUNK,),
        in_specs=[blk, pl.BlockSpec((CHUNK, width), lambda n: (n, 1)), blk, vec, vec, _full(wm.shape), _full(bs3.shape)],
        out_specs=[pl.BlockSpec((CHUNK, 2 * width), lambda n: (n, 0)), _full(wm.shape), _full(bs3.shape), _full((8, width))],
        out_shape=[jax.ShapeDtypeStruct((s, 2 * width), BF16), jax.ShapeDtypeStruct(wm.shape, F32),
                   jax.ShapeDtypeStruct(bs3.shape, F32), jax.ShapeDtypeStruct((8, width), F32)],
        scratch_shapes=[pltpu.VMEM((CHUNK, width), F32)],
        compiler_params=_params("arbitrary"))(z, z, da, ln_g, ln_b, wm, bs3)


def merge_fwd(z, off_a, off_b, ya, yb):
    s, d = ya.shape
    tr, tc = _tile(s, ROW_TILE * 2), _tile(d, COL_TILE)
    assert off_a % tc == 0 and off_b % tc == 0

    def body(ga_ref, gb_ref, ya_ref, yb_ref, o_ref):
        o_ref[...] = (_sigmoid(ga_ref[...]) * ya_ref[...] + _sigmoid(gb_ref[...]) * yb_ref[...]).astype(o_ref.dtype)

    blk = pl.BlockSpec((tr, tc), lambda i, j: (i, j))
    return pl.pallas_call(
        body, name="merge_fwd", grid=(s // tr, d // tc),
        in_specs=[pl.BlockSpec((tr, tc), lambda i, j: (i, off_a // tc + j)), pl.BlockSpec((tr, tc), lambda i, j: (i, off_b // tc + j)), blk, blk],
        out_specs=blk, out_shape=jax.ShapeDtypeStruct((s, d), BF16), compiler_params=_params("parallel", "parallel"))(z, z, ya, yb)


def merge_bwd(z, off_a, off_b, ya, yb, dm):
    s, d = ya.shape
    tr, tc = _tile(s, ROW_TILE * 2), _tile(d, COL_TILE)
    nc = d // tc

    def body(ga_ref, gb_ref, ya_ref, yb_ref, dm_ref, dya_ref, dyb_ref, dga_ref, dgb_ref):
        dm_v = dm_ref[...]
        sa, sb = _sigmoid(ga_ref[...]), _sigmoid(gb_ref[...])
        dya_ref[...] = (dm_v * sa).astype(dya_ref.dtype)
        dyb_ref[...] = (dm_v * sb).astype(dyb_ref.dtype)
        dga_ref[...] = (dm_v * ya_ref[...] * sa * (1.0 - sa)).astype(dga_ref.dtype)
        dgb_ref[...] = (dm_v * yb_ref[...] * sb * (1.0 - sb)).astype(dgb_ref.dtype)

    blk = pl.BlockSpec((tr, tc), lambda i, j: (i, j))
    out = jax.ShapeDtypeStruct((s, d), BF16)
    return pl.pallas_call(
        body, name="merge_bwd", grid=(s // tr, nc),
        in_specs=[pl.BlockSpec((tr, tc), lambda i, j: (i, off_a // tc + j)), pl.BlockSpec((tr, tc), lambda i, j: (i, off_b // tc + j)), blk, blk, blk],
        out_specs=[blk, blk, blk, blk], out_shape=[out, out, out, out],
        compiler_params=_params("parallel", "parallel"))(z, z, ya, yb, dm)


_ATT_SCALE = (QK_NOPE + QK_ROPE) ** -0.5
_NEG = -1e30


def rope_k(z, off, cos4, sin4):
    s = z.shape[0]
    tr = _tile(s, ROW_TILE * 2)
    assert off % LANES == 0

    def body(k_ref, c_ref, s_ref, o_ref):
        k = k_ref[...]
        k = k + pltpu.roll(k, QK_ROPE, 1)
        o_ref[...] = _rope(k, c_ref[...], s_ref[...]).astype(o_ref.dtype)

    row = pl.BlockSpec((tr, LANES), lambda i: (i, 0))
    return pl.pallas_call(body, name="rope_k", grid=(s // tr,),
                          in_specs=[pl.BlockSpec((tr, LANES), lambda i: (i, off // LANES)), row, row], out_specs=row,
                          out_shape=jax.ShapeDtypeStruct((s, LANES), BF16), compiler_params=_params("parallel"))(z, cos4, sin4)


def _head_masks(shape):
    lane = lax.broadcasted_iota(jnp.int32, shape, 1)
    return lane < QK_ROPE, lane >= QK_ROPE


def _scores(qn, qp_h, k, kp, qi, kb, t):
    sc = lax.dot_general(qn, k, (((1,), (1,)), ((), ())), preferred_element_type=F32)
    sc += lax.dot_general(qp_h, kp, (((1,), (1,)), ((), ())), preferred_element_type=F32)
    sc = sc * _ATT_SCALE
    row = lax.broadcasted_iota(jnp.int32, sc.shape, 0) + qi * t
    col = lax.broadcasted_iota(jnp.int32, sc.shape, 1) + kb * t
    return jnp.where(col <= row, sc, _NEG)


def attn_fwd(qn, qp, kv, kpr, cos4, sin4):
    s = qn.shape[0]
    hp = HEADS // 2
    t = _tile(s, ATT_TILE)
    nq = s // t

    def body(qn_ref, qp_ref, kv_ref, kp_ref, c_ref, s_ref, o_ref, qpr_ref, l_ref):
        qi = pl.program_id(1)
        qpr = _rope(qp_ref[...], c_ref[...], s_ref[...]).astype(BF16)
        qpr_ref[...] = qpr
        masks = _head_masks(qpr.shape)
        for hh in range(2):
            q_n = qn_ref[:, hh * QK_NOPE:(hh + 1) * QK_NOPE]
            q_p = jnp.where(masks[hh], qpr, jnp.zeros_like(qpr))
            kc, vc = 2 * hh * QK_NOPE, (2 * hh + 1) * QK_NOPE

            def step(kb, carry):
                m, l, acc = carry
                rows = pl.ds(pl.multiple_of(kb * t, t), t)
                sc = _scores(q_n, q_p, kv_ref[rows, kc:kc + QK_NOPE], kp_ref[rows, :], qi, kb, t)
                m_new = jnp.maximum(m, jnp.max(sc, axis=-1, keepdims=True))
                alpha = jnp.exp(m - m_new)
                p = jnp.exp(sc - m_new)
                l = alpha * l + jnp.sum(p, axis=-1, keepdims=True)
                acc = alpha * acc + jnp.dot(p.astype(BF16), kv_ref[rows, vc:vc + V_HEAD], preferred_element_type=F32)
                return m_new, l, acc

            init = (jnp.full((t, 1), _NEG, F32), jnp.zeros((t, 1), F32), jnp.zeros((t, V_HEAD), F32))
            m, l, acc = lax.fori_loop(0, qi + 1, step, init)
            o_ref[:, hh * V_HEAD:(hh + 1) * V_HEAD] = acc / l
            l_ref[:, hh:hh + 1] = m + jnp.log(l)

    return pl.pallas_call(
        body, name="attn_fwd", grid=(hp, nq),
        in_specs=[pl.BlockSpec((t, 2 * QK_NOPE), lambda h, i: (i, h)), pl.BlockSpec((t, LANES), lambda h, i: (i, h)),
                  pl.BlockSpec((s, 4 * QK_NOPE), lambda h, i: (0, h)), _full((s, LANES)),
                  pl.BlockSpec((t, LANES), lambda h, i: (i, 0)), pl.BlockSpec((t, LANES), lambda h, i: (i, 0))],
        out_specs=[pl.BlockSpec((t, 2 * V_HEAD), lambda h, i: (i, h)), pl.BlockSpec((t, LANES), lambda h, i: (i, h)),
                   pl.BlockSpec((None, t, 2), lambda h, i: (h, i, 0))],
        out_shape=[jax.ShapeDtypeStruct((s, HEADS * V_HEAD), F32), jax.ShapeDtypeStruct((s, HEADS * QK_ROPE), BF16),
                   jax.ShapeDtypeStruct((hp, s, 2), F32)],
        compiler_params=_params("parallel", "parallel"))(qn, qp, kv, kpr, cos4, sin4)


def attn_bwd_q(qn, qpr, kv, kpr, o, do, lse, cos4, sin4):
    s = qn.shape[0]
    hp = HEADS // 2
    t = _tile(s, ATT_TILE)
    nq = s // t

    def body(qn_ref, qpr_ref, kv_ref, kp_ref, o_ref, do_ref, l_ref, c_ref, s_ref, dqn_ref, dqp_ref):
        qi = pl.program_id(1)
        qpr = qpr_ref[...]
        masks = _head_masks(qpr.shape)
        dqp = jnp.zeros(qpr.shape, F32)
        for hh in range(2):
            q_n = qn_ref[:, hh * QK_NOPE:(hh + 1) * QK_NOPE]
            q_p = jnp.where(masks[hh], qpr, jnp.zeros_like(qpr))
            kc, vc = 2 * hh * QK_NOPE, (2 * hh + 1) * QK_NOPE
            do_h = do_ref[:, hh * V_HEAD:(hh + 1) * V_HEAD]
            delta = jnp.sum(do_h * o_ref[:, hh * V_HEAD:(hh + 1) * V_HEAD], axis=-1, keepdims=True)
            do_b = do_h.astype(BF16)
            lse_h = l_ref[:, hh:hh + 1]

            def step(kb, carry):
                dn, dp_ = carry
                rows = pl.ds(pl.multiple_of(kb * t, t), t)
                k = kv_ref[rows, kc:kc + QK_NOPE]
                kp = kp_ref[rows, :]
                p = jnp.exp(_scores(q_n, q_p, k, kp, qi, kb, t) - lse_h)
                dpv = lax.dot_general(do_b, kv_ref[rows, vc:vc + V_HEAD], (((1,), (1,)), ((), ())), preferred_element_type=F32)
                ds = (p * (dpv - delta) * _ATT_SCALE).astype(BF16)
                dn = dn + jnp.dot(ds, k, preferred_element_type=F32)
                dp_ = dp_ + jnp.dot(ds, kp, preferred_element_type=F32)
                return dn, dp_

            dn, dp_h = lax.fori_loop(0, qi + 1, step, (jnp.zeros((t, QK_NOPE), F32), jnp.zeros((t, LANES), F32)))
            dqn_ref[:, hh * QK_NOPE:(hh + 1) * QK_NOPE] = dn.astype(dqn_ref.dtype)
            dqp = dqp + jnp.where(masks[hh], dp_h, jnp.zeros_like(dp_h))
        dqp_ref[...] = _rope(dqp, c_ref[...], -s_ref[...]).astype(dqp_ref.dtype)

    qblk = pl.BlockSpec((t, 2 * QK_NOPE), lambda h, i: (i, h))
    pblk = pl.BlockSpec((t, LANES), lambda h, i: (i, h))
    tab = pl.BlockSpec((t, LANES), lambda h, i: (i, 0))
    return pl.pallas_call(
        body, name="attn_bwd_q", grid=(hp, nq),
        in_specs=[qblk, pblk, pl.BlockSpec((s, 4 * QK_NOPE), lambda h, i: (0, h)), _full((s, LANES)), qblk, qblk,
                  pl.BlockSpec((None, t, 2), lambda h, i: (h, i, 0)), tab, tab],
        out_specs=[qblk, pblk],
        out_shape=[jax.ShapeDtypeStruct((s, HEADS * QK_NOPE), BF16), jax.ShapeDtypeStruct((s, HEADS * QK_ROPE), BF16)],
        compiler_params=_params("parallel", "parallel"))(qn, qpr, kv, kpr, o, do, lse, cos4, sin4)


def attn_bwd_kv(qn, qpr, kv, kpr, o, do, lse):
    s = qn.shape[0]
    hp = HEADS // 2
    t = _tile(s, ATT_TILE)
    nq = s // t

    def body(qn_ref, qpr_ref, kv_ref, kp_ref, o_ref, do_ref, l_ref, dkv_ref, dkp_ref):
        ki = pl.program_id(1)
        rows_k = pl.ds(pl.multiple_of(ki * t, t), t)
        kp = kp_ref[rows_k, :]
        dkp = jnp.zeros((t, LANES), F32)
        for hh in range(2):
            kc, vc = 2 * hh * QK_NOPE, (2 * hh + 1) * QK_NOPE
            k = kv_ref[rows_k, kc:kc + QK_NOPE]
            v = kv_ref[rows_k, vc:vc + V_HEAD]

            def step(qb, carry):
                dk, dv, dkp_h = carry
                rows = pl.ds(pl.multiple_of(qb * t, t), t)
                q_n = qn_ref[rows, hh * QK_NOPE:(hh + 1) * QK_NOPE]
                qpr = qpr_ref[rows, :]
                lane = lax.broadcasted_iota(jnp.int32, qpr.shape, 1)
                sel = (lane < QK_ROPE) if hh == 0 else (lane >= QK_ROPE)
                q_p = jnp.where(sel, qpr, jnp.zeros_like(qpr))
                do_h = do_ref[rows, hh * V_HEAD:(hh + 1) * V_HEAD]
                delta = jnp.sum(do_h * o_ref[rows, hh * V_HEAD:(hh + 1) * V_HEAD], axis=-1, keepdims=True)
                do_b = do_h.astype(BF16)
                p = jnp.exp(_scores(q_n, q_p, k, kp, qb, ki, t) - l_ref[rows, hh:hh + 1])
                dpv = lax.dot_general(do_b, v, (((1,), (1,)), ((), ())), preferred_element_type=F32)
                ds = (p * (dpv - delta) * _ATT_SCALE).astype(BF16)
                dv = dv + lax.dot_general(p.astype(BF16), do_b, (((0,), (0,)), ((), ())), preferred_element_type=F32)
                dk = dk + lax.dot_general(ds, q_n, (((0,), (0,)), ((), ())), preferred_element_type=F32)
                dkp_h = dkp_h + lax.dot_general(ds, q_p, (((0,), (0,)), ((), ())), preferred_element_type=F32)
                return dk, dv, dkp_h

            init = (jnp.zeros((t, QK_NOPE), F32), jnp.zeros((t, V_HEAD), F32), jnp.zeros((t, LANES), F32))
            dk, dv, dkp_h = lax.fori_loop(ki, nq, step, init)
            dkv_ref[:, kc:kc + QK_NOPE] = dk.astype(dkv_ref.dtype)
            dkv_ref[:, vc:vc + V_HEAD] = dv.astype(dkv_ref.dtype)
            dkp = dkp + dkp_h
        dkp_ref[...] = dkp

    return pl.pallas_call(
        body, name="attn_bwd_kv", grid=(hp, nq),
        in_specs=[pl.BlockSpec((s, 2 * QK_NOPE), lambda h, i: (0, h)), pl.BlockSpec((s, LANES), lambda h, i: (0, h)),
                  pl.BlockSpec((s, 4 * QK_NOPE), lambda h, i: (0, h)), _full((s, LANES)),
                  pl.BlockSpec((s, 2 * V_HEAD), lambda h, i: (0, h)), pl.BlockSpec((s, 2 * V_HEAD), lambda h, i: (0, h)),
                  pl.BlockSpec((None, s, 2), lambda h, i: (h, 0, 0))],
        out_specs=[pl.BlockSpec((t, 4 * QK_NOPE), lambda h, i: (i, h)), pl.BlockSpec((None, t, LANES), lambda h, i: (h, i, 0))],
        out_shape=[jax.ShapeDtypeStruct((s, HEADS * 2 * QK_NOPE), BF16), jax.ShapeDtypeStruct((hp, s, LANES), F32)],
        compiler_params=_params("parallel", "parallel"))(qn, qpr, kv, kpr, o, do, lse)


def _dot_nt(a, b):
    return lax.dot_general(a, b, (((1,), (1,)), ((), ())), preferred_element_type=F32)


def _dot_tn(a, b):
    return lax.dot_general(a, b, (((0,), (0,)), ((), ())), preferred_element_type=F32)


def _q_cat(q_n, qpr, hh):
    lane = lax.broadcasted_iota(jnp.int32, qpr.shape, 1)
    sel = (lane < QK_ROPE) if hh == 0 else (lane >= QK_ROPE)
    return jnp.concatenate([q_n, jnp.where(sel, qpr, jnp.zeros_like(qpr))], axis=1)


def _causal(sc):
    row = lax.broadcasted_iota(jnp.int32, sc.shape, 0)
    col = lax.broadcasted_iota(jnp.int32, sc.shape, 1)
    return jnp.where(col <= row, sc, _NEG)


def attn_fwd2(qn, qp, kv, kpr, cos4, sin4):
    s = qn.shape[0]
    hp = HEADS // 2
    t = _tile(s, ATT_TILE)
    nq = s // t

    def body(qn_ref, qp_ref, kv_ref, kp_ref, c_ref, s_ref, o_ref, qpr_ref, l_ref, kcat_ref):
        qi = pl.program_id(1)

        @pl.when(qi == 0)
        def _():
            for hh in range(2):
                kcat_ref[hh, :, 0:QK_NOPE] = kv_ref[:, 2 * hh * QK_NOPE:(2 * hh + 1) * QK_NOPE]
                kcat_ref[hh, :, QK_NOPE:] = kp_ref[...]

        qpr = _rope(qp_ref[...], c_ref[...], s_ref[...]).astype(BF16)
        qpr_ref[...] = qpr
        qcat = [_q_cat(qn_ref[:, hh * QK_NOPE:(hh + 1) * QK_NOPE], qpr, hh) for hh in range(2)]

        def block(kb, carry, diagonal):
            rows = pl.ds(pl.multiple_of(kb * t, t), t)
            out = []
            for hh in range(2):
                m, l, acc = carry[hh]
                sc = _dot_nt(qcat[hh], kcat_ref[hh, rows, :]) * _ATT_SCALE
                if diagonal:
                    sc = _causal(sc)
                m_new = jnp.maximum(m, jnp.max(sc, axis=-1, keepdims=True))
                alpha = jnp.exp(m - m_new)
                p = jnp.exp(sc - m_new)
                l = alpha * l + jnp.sum(p, axis=-1, keepdims=True)
                v = kv_ref[rows, (2 * hh + 1) * QK_NOPE:(2 * hh + 2) * QK_NOPE]
                acc = alpha * acc + jnp.dot(p.astype(BF16), v, preferred_element_type=F32)
                out.append((m_new, l, acc))
            return tuple(out)

        one = (jnp.full((t, 1), _NEG, F32), jnp.zeros((t, 1), F32), jnp.zeros((t, V_HEAD), F32))
        carry = lax.fori_loop(0, qi, lambda kb, cr: block(kb, cr, False), (one, one))
        carry = block(qi, carry, True)
        for hh in range(2):
            m, l, acc = carry[hh]
            o_ref[:, hh * V_HEAD:(hh + 1) * V_HEAD] = acc / l
            l_ref[:, hh:hh + 1] = m + jnp.log(l)

    return pl.pallas_call(
        body, name="attn_fwd", grid=(hp, nq),
        in_specs=[pl.BlockSpec((t, 2 * QK_NOPE), lambda h, i: (i, h)), pl.BlockSpec((t, LANES), lambda h, i: (i, h)),
                  pl.BlockSpec((s, 4 * QK_NOPE), lambda h, i: (0, h)), _full((s, LANES)),
                  pl.BlockSpec((t, LANES), lambda h, i: (i, 0)), pl.BlockSpec((t, LANES), lambda h, i: (i, 0))],
        out_specs=[pl.BlockSpec((t, 2 * V_HEAD), lambda h, i: (i, h)), pl.BlockSpec((t, LANES), lambda h, i: (i, h)),
                   pl.BlockSpec((None, t, 2), lambda h, i: (h, i, 0))],
        out_shape=[jax.ShapeDtypeStruct((s, HEADS * V_HEAD), F32), jax.ShapeDtypeStruct((s, HEADS * QK_ROPE), BF16),
                   jax.ShapeDtypeStruct((hp, s, 2), F32)],
        scratch_shapes=[pltpu.VMEM((2, s, 2 * QK_NOPE), BF16)],
        compiler_params=_params("parallel", "arbitrary"))(qn, qp, kv, kpr, cos4, sin4)


def attn_bwd2(qn, qpr, kv, kpr, o, do, lse, cos4, sin4):
    s = qn.shape[0]
    hp = HEADS // 2
    t = _tile(s, ATT_TILE)
    nk = s // t

    def body(qn_ref, qpr_ref, kv_ref, kp_ref, o_ref, do_ref, l_ref, c_ref, s_ref,
             dqn_ref, dqp_ref, dkv_ref, dkp_ref, qcat_ref, dq_ref, delta_ref):
        ki = pl.program_id(1)

        @pl.when(ki == 0)
        def _():
            dq_ref[...] = jnp.zeros_like(dq_ref)
            for hh in range(2):
                qcat_ref[hh] = _q_cat(qn_ref[:, hh * QK_NOPE:(hh + 1) * QK_NOPE], qpr_ref[...], hh)
                cols = slice(hh * V_HEAD, (hh + 1) * V_HEAD)
                delta_ref[hh] = jnp.sum(do_ref[:, cols] * o_ref[:, cols], axis=-1, keepdims=True)

        rows_k = pl.ds(pl.multiple_of(ki * t, t), t)
        kcat = [jnp.concatenate([kv_ref[rows_k, 2 * hh * QK_NOPE:(2 * hh + 1) * QK_NOPE], kp_ref[rows_k, :]], axis=1) for hh in range(2)]
        vs = [kv_ref[rows_k, (2 * hh + 1) * QK_NOPE:(2 * hh + 2) * QK_NOPE] for hh in range(2)]

        def block(qb, carry, diagonal):
            rows = pl.ds(pl.multiple_of(qb * t, t), t)
            out = []
            for hh in range(2):
                dkc, dv = carry[hh]
                q_c = qcat_ref[hh, rows, :]
                do_b = do_ref[rows, hh * V_HEAD:(hh + 1) * V_HEAD].astype(BF16)
                sc = _dot_nt(q_c, kcat[hh]) * _ATT_SCALE
                if diagonal:
                    sc = _causal(sc)
                p = jnp.exp(sc - l_ref[rows, hh:hh + 1])
                dpv = _dot_nt(do_b, vs[hh])
                ds = (p * (dpv - delta_ref[hh, rows, :]) * _ATT_SCALE).astype(BF16)
                dv = dv + _dot_tn(p.astype(BF16), do_b)
                dkc = dkc + _dot_tn(ds, q_c)
                dq_ref[hh, rows, :] += jnp.dot(ds, kcat[hh], preferred_element_type=F32)
                out.append((dkc, dv))
            return tuple(out)

        one = (jnp.zeros((t, 2 * QK_NOPE), F32), jnp.zeros((t, V_HEAD), F32))
        carry = block(ki, (one, one), True)
        carry = lax.fori_loop(ki + 1, nk, lambda qb, cr: block(qb, cr, False), carry)
        dkp = jnp.zeros((t, LANES), F32)
        for hh in range(2):
            dkc, dv = carry[hh]
            dkv_ref[:, 2 * hh * QK_NOPE:(2 * hh + 1) * QK_NOPE] = dkc[:, :QK_NOPE].astype(dkv_ref.dtype)
            dkv_ref[:, (2 * hh + 1) * QK_NOPE:(2 * hh + 2) * QK_NOPE] = dv.astype(dkv_ref.dtype)
            dkp = dkp + dkc[:, QK_NOPE:]
        dkp_ref[...] = dkp

        @pl.when(ki == nk - 1)
        def _():
            lane = lax.broadcasted_iota(jnp.int32, (s, LANES), 1)
            dqp = jnp.where(lane < QK_ROPE, dq_ref[0, :, QK_NOPE:], dq_ref[1, :, QK_NOPE:])
            dqp_ref[...] = _rope(dqp, c_ref[...], -s_ref[...]).astype(dqp_ref.dtype)
            for hh in range(2):
                dqn_ref[:, hh * QK_NOPE:(hh + 1) * QK_NOPE] = dq_ref[hh, :, :QK_NOPE].astype(dqn_ref.dtype)

    qblk = pl.BlockSpec((s, 2 * QK_NOPE), lambda h, i: (0, h))
    pblk = pl.BlockSpec((s, LANES), lambda h, i: (0, h))
    tab = _full((s, LANES))
    return pl.pallas_call(
        body, name="attn_bwd", grid=(hp, nk),
        in_specs=[qblk, pblk, pl.BlockSpec((s, 4 * QK_NOPE), lambda h, i: (0, h)), tab, qblk, qblk,
                  pl.BlockSpec((None, s, 2), lambda h, i: (h, 0, 0)), tab, tab],
        out_specs=[qblk, pblk, pl.BlockSpec((t, 4 * QK_NOPE), lambda h, i: (i, h)), pl.BlockSpec((None, t, LANES), lambda h, i: (h, i, 0))],
        out_shape=[jax.ShapeDtypeStruct((s, HEADS * QK_NOPE), BF16), jax.ShapeDtypeStruct((s, HEADS * QK_ROPE), BF16),
                   jax.ShapeDtypeStruct((s, HEADS * 2 * QK_NOPE), BF16), jax.ShapeDtypeStruct((hp, s, LANES), F32)],
        scratch_shapes=[pltpu.VMEM((2, s, 2 * QK_NOPE), BF16), pltpu.VMEM((2, s, 2 * QK_NOPE), F32), pltpu.VMEM((2, s, 1), F32)],
        compiler_params=_params("parallel", "arbitrary"))(qn, qpr, kv, kpr, o, do, lse, cos4, sin4)


def kpe_bwd(dkp, cos4, sin4, pad_cols):
    hp, s, _ = dkp.shape
    tr = _tile(s, ROW_TILE * 2)

    def body(d_ref, c_ref, s_ref, o_ref):
        tot = d_ref[0]
        for h in range(1, hp):
            tot = tot + d_ref[h]
        tot = tot + pltpu.roll(tot, QK_ROPE, 1)
        lane = lax.broadcasted_iota(jnp.int32, tot.shape, 1)
        dk = jnp.where(lane < QK_ROPE, _rope(tot, c_ref[...], -s_ref[...]), jnp.zeros_like(tot))
        o_ref[...] = jnp.zeros_like(o_ref)
        o_ref[:, 0:LANES] = dk.astype(o_ref.dtype)

    row = pl.BlockSpec((tr, LANES), lambda i: (i, 0))
    return pl.pallas_call(body, name="kpe_bwd", grid=(s // tr,),
                          in_specs=[pl.BlockSpec((hp, tr, LANES), lambda i: (0, i, 0)), row, row],
                          out_specs=pl.BlockSpec((tr, pad_cols), lambda i: (i, 0)),
                          out_shape=jax.ShapeDtypeStruct((s, pad_cols), BF16), compiler_params=_params("parallel"))(dkp, cos4, sin4)


def _shift_down(x, n):
    row = lax.broadcasted_iota(jnp.int32, x.shape, 0)
    return jnp.where(row >= n, pltpu.roll(x, n, 0), jnp.zeros_like(x))


def _shift_up(x, n):
    rows = x.shape[0]
    row = lax.broadcasted_iota(jnp.int32, x.shape, 0)
    return jnp.where(row < rows - n, pltpu.roll(x, rows - n, 0), jnp.zeros_like(x))


def _conv(x, w_ref, b_ref):
    return w_ref[2:3, :] * x + w_ref[1:2, :] * _shift_down(x, 1) + w_ref[0:1, :] * _shift_down(x, 2) + b_ref[...]


def conv_act_fwd(upre, conv_w, conv_b):
    s, f2 = upre.shape
    f = f2 // 2
    tc = _tile(f, COL_TILE)
    nc = f // tc

    def body(ug_ref, uv_ref, wg_ref, wv_ref, bg_ref, bv_ref, o_ref):
        gh = _conv(ug_ref[...], wg_ref, bg_ref)
        vh = _conv(uv_ref[...], wv_ref, bv_ref)
        o_ref[...] = (gh * _sigmoid(gh) * vh).astype(o_ref.dtype)

    def spec(rows, shift):
        return pl.BlockSpec((rows, tc), lambda j: (0, j + shift))

    return pl.pallas_call(
        body, name="conv_act_fwd", grid=(nc,),
        in_specs=[spec(s, 0), spec(s, nc), spec(3, 0), spec(3, nc), spec(1, 0), spec(1, nc)], out_specs=spec(s, 0),
        out_shape=jax.ShapeDtypeStruct((s, f), BF16), compiler_params=_params("parallel"))(upre, upre, conv_w, conv_w, conv_b, conv_b)


def conv_act_bwd(upre, conv_w, conv_b, df):
    s, f2 = upre.shape
    f = f2 // 2
    tc = _tile(f, COL_TILE)
    nc = f // tc

    def half(x, d, w_ref, du_ref, which, gw_ref, gb_ref):
        gb_ref[...] = _colsum(d)
        gw_ref[2:3, :] = _colsum(d * x)
        gw_ref[1:2, :] = _colsum(d * _shift_down(x, 1))
        gw_ref[0:1, :] = _colsum(d * _shift_down(x, 2))
        du_ref[which] = (w_ref[2:3, :] * d + w_ref[1:2, :] * _shift_up(d, 1) + w_ref[0:1, :] * _shift_up(d, 2)).astype(du_ref.dtype)

    def body(ug_ref, uv_ref, wg_ref, wv_ref, bg_ref, bv_ref, df_ref, du_ref, gwg_ref, gwv_ref, gbg_ref, gbv_ref):
        xg, xv = ug_ref[...], uv_ref[...]
        gh = _conv(xg, wg_ref, bg_ref)
        vh = _conv(xv, wv_ref, bv_ref)
        sg = _sigmoid(gh)
        df_v = df_ref[...]
        half(xg, df_v * vh * (sg * (1.0 + gh * (1.0 - sg))), wg_ref, du_ref, 0, gwg_ref, gbg_ref)
        half(xv, df_v * (gh * sg), wv_ref, du_ref, 1, gwv_ref, gbv_ref)

    def spec(rows, shift):
        return pl.BlockSpec((rows, tc), lambda j: (0, j + shift))

    gw = jax.ShapeDtypeStruct((3, f), F32)
    gb = jax.ShapeDtypeStruct((1, f), F32)
    return pl.pallas_call(
        body, name="conv_act_bwd", grid=(nc,),
        in_specs=[spec(s, 0), spec(s, nc), spec(3, 0), spec(3, nc), spec(1, 0), spec(1, nc), spec(s, 0)],
        out_specs=[pl.BlockSpec((2, s, tc), lambda j: (0, 0, j)), spec(3, 0), spec(3, 0), spec(1, 0), spec(1, 0)],
        out_shape=[jax.ShapeDtypeStruct((2, s, f), BF16), gw, gw, gb, gb],
        compiler_params=_params("parallel"))(upre, upre, conv_w, conv_w, conv_b, conv_b, df)


def adamw(name, w, m, v, parts, row_off=0):
    npart, c = parts.shape[0], parts.shape[2]
    r = w.shape[0]
    tr = r
    if r % 8 == 0:
        tr = max(8, min(r, ADAMW_TILE_ELEMS // c) // 8 * 8)
        while r % tr:
            tr -= 8
    bc1 = 1.0 - ADAM_B1 ** ADAM_STEP
    bc2 = 1.0 - ADAM_B2 ** ADAM_STEP

    def body(w_ref, m_ref, v_ref, p_ref, g_ref, d_ref, nm_ref, nv_ref):
        g = p_ref[0].astype(F32)
        for k in range(1, npart):
            g = g + p_ref[k].astype(F32)
        m_new = ADAM_B1 * m_ref[...] + (1.0 - ADAM_B1) * g
        v_new = ADAM_B2 * v_ref[...] + (1.0 - ADAM_B2) * (g * g)
        g_ref[...] = g
        nm_ref[...] = m_new
        nv_ref[...] = v_new
        d_ref[...] = -ADAM_LR * ((m_new / bc1) / (jnp.sqrt(v_new / bc2) + ADAM_EPS) + ADAM_WD * w_ref[...])

    assert row_off % tr == 0
    deps = _TOKENS.take()
    blk = pl.BlockSpec((tr, c), lambda i: (i, 0))
    out = jax.ShapeDtypeStruct((r, c), F32)
    return pl.pallas_call(
        lambda *refs: body(*refs[:4], *refs[4 + len(deps):]), name=name, grid=(r // tr,),
        in_specs=[blk, blk, blk, pl.BlockSpec((npart, tr, c), lambda i: (0, row_off // tr + i, 0))] + [pl.BlockSpec(memory_space=pl.ANY)] * len(deps),
        out_specs=[blk, blk, blk, blk], out_shape=[out, out, out, out], compiler_params=_params("parallel"))(w, m, v, parts, *deps)


def _position():
    return lax.axis_index("x"), lax.axis_index("y"), lax.axis_index("c")


def _index(p):
    return 4 * p[0] + 2 * p[1] + p[2]


def _peer(me, r):
    return (me[0] ^ ((r >> 2) & 1), me[1] ^ ((r >> 1) & 1), me[2] ^ (r & 1))


_ANY = pl.BlockSpec(memory_space=pl.ANY)


def all_gather_two_level(shards):
    n = len(shards)

    def body(*refs):
        ins, outs = refs[:n], refs[n:2 * n]
        send_sems, recv_sems, local_sems = refs[2 * n:]
        x, y, c = _position()
        me, sibling = (x, y, c), (x, y, 1 - c)
        chips = [(1 - x, y), (x, 1 - y), (1 - x, 1 - y)]

        def copy(w, k, block, to, src=None):
            slot = outs[w].at[_index(block)]
            return pltpu.make_async_remote_copy(src_ref=slot if src is None else src, dst_ref=slot,
                                                send_sem=send_sems.at[7 * w + k], recv_sem=recv_sems.at[7 * w + k],
                                                device_id=to, device_id_type=MESH)

        mine = [pltpu.make_async_copy(ins[w], outs[w].at[_index(me)], local_sems.at[w]) for w in range(n)]
        for cp in mine:
            cp.start()
        first = []
        for w in range(n):
            first.append(copy(w, 0, me, sibling, src=ins[w]))
            first += [copy(w, 1 + j, me, (*chip, c), src=ins[w]) for j, chip in enumerate(chips)]
        for cp in first:
            cp.start()
        passed = []
        for w in range(n):
            for j, chip in enumerate(chips):
                copy(w, 1 + j, (*chip, c), me).wait_recv()
                cp = copy(w, 4 + j, (*chip, c), sibling)
                cp.start()
                passed.append(cp)
        for w in range(n):
            copy(w, 0, sibling, me).wait_recv()
            for j, chip in enumerate(chips):
                copy(w, 4 + j, (*chip, 1 - c), me).wait_recv()
        for cp in first + passed:
            cp.wait_send()
        for cp in mine:
            cp.wait()

    return pl.pallas_call(
        body, name="all_gather_weights",
        out_shape=[jax.ShapeDtypeStruct((N_DEV,) + a.shape, a.dtype) for a in shards],
        in_specs=[_ANY] * n, out_specs=[_ANY] * n,
        scratch_shapes=[pltpu.SemaphoreType.DMA((7 * n,)), pltpu.SemaphoreType.DMA((7 * n,)), pltpu.SemaphoreType.DMA((n,))],
        )(*shards)


def exchange(name, arrays, scatter):
    n = len(arrays)

    def body(*refs):
        ins, outs = refs[:n], refs[n:2 * n]
        send_sems, recv_sems, local_sems = refs[2 * n:]
        me = _position()
        copies = []
        for w in range(n):
            src = ins[w].at[_index(me)] if scatter else ins[w]
            cp = pltpu.make_async_copy(src, outs[w].at[_index(me)], local_sems.at[w])
            cp.start()
            copies.append(cp)
        remote = []
        for w in range(n):
            for r in range(1, N_DEV):
                peer = _peer(me, r)
                src = ins[w].at[_index(peer)] if scatter else ins[w]
                cp = pltpu.make_async_remote_copy(src_ref=src, dst_ref=outs[w].at[_index(me)],
                                                  send_sem=send_sems.at[7 * w + r - 1], recv_sem=recv_sems.at[7 * w + r - 1],
                                                  device_id=peer, device_id_type=MESH)
                cp.start()
                remote.append(cp)
        for cp in remote:
            cp.wait()
        for cp in copies:
            cp.wait()

    blocks = [a.shape[1:] if scatter else a.shape for a in arrays]
    return pl.pallas_call(
        body, name=name,
        out_shape=[jax.ShapeDtypeStruct((N_DEV,) + b, a.dtype) for a, b in zip(arrays, blocks)],
        in_specs=[_ANY] * n, out_specs=[_ANY] * n,
        scratch_shapes=[pltpu.SemaphoreType.DMA((7 * n,)), pltpu.SemaphoreType.DMA((7 * n,)), pltpu.SemaphoreType.DMA((n,))],
        )(*arrays)


_HBM = pl.BlockSpec(memory_space=pltpu.HBM)
_SEM = pl.BlockSpec(memory_space=pltpu.SEMAPHORE)
_EFFECT = pltpu.SideEffectType.DATAFLOW_SIDE_EFFECTING


def _direct_copies(ins, lands, send_sems, recv_sems, scatter):
    me = _position()
    copies = []
    for w in range(len(ins)):
        for r in range(1, N_DEV):
            peer = _peer(me, r)
            src = ins[w].at[_index(peer)] if scatter else ins[w]
            copies.append(pltpu.make_async_remote_copy(src_ref=src, dst_ref=lands[w].at[_index(me)], send_sem=send_sems.at[7 * w + r - 1],
                                                       recv_sem=recv_sems.at[7 * w + r - 1], device_id=peer, device_id_type=MESH))
    return copies


def exchange_start(name, groups, scatter):
    arrays = [a for g in groups for a in g]
    n = len(arrays)
    blocks = [a.shape[1:] if scatter else a.shape for a in arrays]
    lands = [lax.empty((N_DEV,) + b, a.dtype) for a, b in zip(arrays, blocks)]
    ng = len(groups)

    def body(*refs):
        ins, lnd = refs[:n], refs[n:2 * n]
        sems = refs[2 * n:2 * n + 2 * ng]
        token = refs[2 * n + 2 * ng + 2 * n]
        local_sem = refs[2 * n + 2 * ng + 2 * n + 1]
        me = _position()
        local = []
        for w in range(n):
            src = ins[w].at[_index(me)] if scatter else ins[w]
            cp = pltpu.make_async_copy(src, lnd[w].at[_index(me)], local_sem.at[w])
            cp.start()
            local.append(cp)
        w0 = 0
        for gi, g in enumerate(groups):
            for cp in _direct_copies(ins[w0:w0 + len(g)], lnd[w0:w0 + len(g)], sems[2 * gi], sems[2 * gi + 1], scatter):
                cp.start()
            w0 += len(g)
        for cp in local:
            cp.wait()
        token[...] = jnp.zeros_like(token)

    sem_shapes = []
    for g in groups:
        sem_shapes += [pltpu.SemaphoreType.DMA((7 * len(g),)), pltpu.SemaphoreType.DMA((7 * len(g),))]
    out = pl.pallas_call(
        body, name=name,
        out_shape=tuple(sem_shapes) + tuple(pltpu.HBM(a.shape, a.dtype) for a in arrays) + tuple(pltpu.HBM(l.shape, l.dtype) for l in lands)
        + (jax.ShapeDtypeStruct((8, LANES), F32),),
        in_specs=[_HBM] * (2 * n), out_specs=tuple([_SEM] * (2 * ng) + [_HBM] * (2 * n) + [pl.BlockSpec(memory_space=pltpu.VMEM)]),
        input_output_aliases={i: 2 * ng + i for i in range(2 * n)},
        scratch_shapes=[pltpu.SemaphoreType.DMA((n,))],
        compiler_params=pltpu.CompilerParams(has_side_effects=_EFFECT),
    )(*[pltpu.with_memory_space_constraint(a, pltpu.HBM) for a in arrays], *[pltpu.with_memory_space_constraint(l, pltpu.HBM) for l in lands])
    sems, thru, token = out[:2 * ng], out[2 * ng:2 * ng + 2 * n], out[-1]
    res, w0 = [], 0
    for gi, g in enumerate(groups):
        res.append((sems[2 * gi], sems[2 * gi + 1], list(thru[w0:w0 + len(g)]), list(thru[n + w0:n + w0 + len(g)])))
        w0 += len(g)
    return res, token


def exchange_wait(name, group, after, scatter):
    send_sems, recv_sems, srcs, lands = group
    n = len(srcs)

    def body(*refs):
        ins, lnd = refs[:n], refs[n:2 * n]
        for cp in _direct_copies(ins, lnd, refs[2 * n], refs[2 * n + 1], scatter):
            cp.wait_send()
            cp.wait_recv()

    out = pl.pallas_call(
        body, name=name, out_shape=tuple(pltpu.HBM(a.shape, a.dtype) for a in srcs + lands),
        in_specs=[_HBM] * (2 * n) + [_SEM, _SEM, pl.BlockSpec(memory_space=pl.ANY)], out_specs=tuple([_HBM] * (2 * n)),
        input_output_aliases={i: i for i in range(2 * n)},
        compiler_params=pltpu.CompilerParams(has_side_effects=_EFFECT),
    )(*srcs, *lands, send_sems, recv_sems, after)
    return list(out[n:])


def _after(x, token):
    return lax.optimization_barrier((x, token))[0]


_TOKEN = jax.ShapeDtypeStruct((8, LANES), F32)
_VM = pl.BlockSpec(memory_space=pltpu.VMEM)
_SIDE = pltpu.CompilerParams(has_side_effects=_EFFECT)


def _hbm(a):
    return pltpu.with_memory_space_constraint(a, pltpu.HBM)


def _like(a):
    return pltpu.HBM(a.shape, a.dtype)


def _dma_sems(n):
    return pltpu.SemaphoreType.DMA((n,))


def _other_chips(x, y):
    return [(1 - x, y), (x, 1 - y), (1 - x, 1 - y)]


COPY_STREAMS = 8


def _row_chunks(src, dst):
    rows = src.shape[0]
    n = COPY_STREAMS
    while n > 1 and rows % (16 * n):
        n //= 2
    r = rows // n
    return [(src.at[pl.ds(i * r, r)], dst.at[pl.ds(i * r, r)]) for i in range(n)]


def _local_copy(src, dst, sem):
    return [pltpu.make_async_copy(s, d, sem) for s, d in _row_chunks(src, dst)]


class _rcopy:
    def __init__(self, src, dst, send_sem, recv_sem, to):
        self.parts = [pltpu.make_async_remote_copy(src_ref=s, dst_ref=d, send_sem=send_sem, recv_sem=recv_sem, device_id=to, device_id_type=MESH)
                      for s, d in _row_chunks(src, dst)]

    def start(self):
        for cp in self.parts:
            cp.start()

    def wait_send(self):
        for cp in self.parts:
            cp.wait_send()

    def wait_recv(self):
        for cp in self.parts:
            cp.wait_recv()


def ag_start(name, shards, after):
    n = len(shards)
    lands = [lax.empty((N_DEV,) + a.shape, a.dtype) for a in shards]

    def body(*refs):
        ins, lnd, send_sems, recv_sems, token = refs[:n], refs[n:2 * n], refs[2 * n + 1], refs[2 * n + 2], refs[4 * n + 3]
        x, y, c = _position()
        for w in range(n):
            slot = lnd[w].at[_index((x, y, c))]
            for k, to in enumerate([(x, y, 1 - c)] + [(*chip, c) for chip in _other_chips(x, y)]):
                _rcopy(ins[w], slot, send_sems.at[4 * w + k], recv_sems.at[4 * w + k], to).start()
        token[...] = jnp.zeros_like(token)

    out = pl.pallas_call(
        body, name=name, out_shape=(_dma_sems(4 * n), _dma_sems(4 * n)) + tuple(_like(a) for a in shards + lands) + (_TOKEN,),
        in_specs=[_HBM] * (2 * n) + [_ANY], out_specs=(_SEM, _SEM) + (_HBM,) * (2 * n) + (_VM,),
        input_output_aliases={i: 2 + i for i in range(2 * n)}, compiler_params=_SIDE)(*[_hbm(a) for a in shards + lands], after)
    _TOKENS.push(out[-1])
    return out[0], out[1], list(out[2:2 + n]), list(out[2 + n:2 + 2 * n])


def ag_forward(name, started, after):
    send, recv, shards, lands = started
    n = len(shards)
    afters = list(after) if isinstance(after, (list, tuple)) else [after]
    na = len(afters)

    def body(*refs):
        ins, lnd, send_sems, recv_sems = refs[:n], refs[n:2 * n], refs[2 * n], refs[2 * n + 1]
        fsend, frecv, token = refs[2 * n + 2 + na], refs[2 * n + 3 + na], refs[4 * n + 4 + na]
        x, y, c = _position()
        for w in range(n):
            for j, chip in enumerate(_other_chips(x, y)):
                slot = lnd[w].at[_index((*chip, c))]
                _rcopy(ins[w], slot, send_sems.at[4 * w + 1 + j], recv_sems.at[4 * w + 1 + j], (*chip, c)).wait_recv()
                _rcopy(slot, slot, fsend.at[3 * w + j], frecv.at[3 * w + j], (x, y, 1 - c)).start()
        token[...] = jnp.zeros_like(token)

    out = pl.pallas_call(
        body, name=name, out_shape=(_dma_sems(3 * n), _dma_sems(3 * n)) + tuple(_like(a) for a in shards + lands) + (_TOKEN,),
        in_specs=[_HBM] * (2 * n) + [_SEM, _SEM] + [_ANY] * na, out_specs=(_SEM, _SEM) + (_HBM,) * (2 * n) + (_VM,),
        input_output_aliases={i: 2 + i for i in range(2 * n)}, compiler_params=_SIDE)(*shards, *lands, send, recv, *afters)
    _TOKENS.push(out[-1])
    return send, recv, out[0], out[1], list(out[2:2 + n]), list(out[2 + n:2 + 2 * n])


def ag_wait(name, forwarded, after):
    send, recv, fsend, frecv, shards, lands = forwarded
    n = len(shards)

    def body(*refs):
        ins, lnd, send_sems, recv_sems, fsend_r, frecv_r = refs[:n], refs[n:2 * n], refs[2 * n], refs[2 * n + 1], refs[2 * n + 2], refs[2 * n + 3]
        x, y, c = _position()
        sibling = (x, y, 1 - c)
        for w in range(n):
            own = lnd[w].at[_index((x, y, c))]
            _rcopy(ins[w], lnd[w].at[_index(sibling)], send_sems.at[4 * w], recv_sems.at[4 * w], sibling).wait_recv()
            for j, chip in enumerate(_other_chips(x, y)):
                _rcopy(ins[w], lnd[w].at[_index((*chip, 1 - c))], fsend_r.at[3 * w + j], frecv_r.at[3 * w + j], sibling).wait_recv()
            for k in range(4):
                _rcopy(ins[w], own, send_sems.at[4 * w + k], recv_sems.at[4 * w + k], sibling).wait_send()
            for j in range(3):
                _rcopy(ins[w], own, fsend_r.at[3 * w + j], frecv_r.at[3 * w + j], sibling).wait_send()

    out = pl.pallas_call(
        body, name=name, out_shape=tuple(_like(a) for a in shards + lands), in_specs=[_HBM] * (2 * n) + [_SEM] * 4 + [_ANY],
        out_specs=(_HBM,) * (2 * n), input_output_aliases={i: i for i in range(2 * n)},
        compiler_params=_SIDE)(*shards, *lands, send, recv, fsend, frecv, after)
    return [lax.dynamic_update_index_in_dim(land, shard, _index(_position()), 0) for shard, land in zip(out[:n], out[n:])]


def rs_d2d_start(name, grads):
    n = len(grads)
    lands = [lax.empty((4,) + g.shape[1:], g.dtype) for g in grads]

    def body(*refs):
        ins, lnd, send_sems, recv_sems, token = refs[:n], refs[n:2 * n], refs[2 * n], refs[2 * n + 1], refs[4 * n + 2]
        x, y, c = _position()
        for w in range(n):
            for i in range(4):
                _rcopy(ins[w].at[2 * i + 1 - c], lnd[w].at[i], send_sems.at[4 * w + i], recv_sems.at[4 * w + i], (x, y, 1 - c)).start()
        token[...] = jnp.zeros_like(token)

    out = pl.pallas_call(
        body, name=name, out_shape=(_dma_sems(4 * n), _dma_sems(4 * n)) + tuple(_like(a) for a in grads + lands) + (_TOKEN,),
        in_specs=[_HBM] * (2 * n), out_specs=(_SEM, _SEM) + (_HBM,) * (2 * n) + (_VM,),
        input_output_aliases={i: 2 + i for i in range(2 * n)}, compiler_params=_SIDE)(*[_hbm(a) for a in grads + lands])
    _TOKENS.push(out[-1])
    return out[0], out[1], list(out[2:2 + n]), list(out[2 + n:2 + 2 * n])


def rs_d2d_wait(name, started, after):
    send, recv, grads, lands = started
    n = len(grads)

    def body(*refs):
        ins, lnd, send_sems, recv_sems = refs[:n], refs[n:2 * n], refs[2 * n], refs[2 * n + 1]
        x, y, c = _position()
        for w in range(n):
            for i in range(4):
                cp = _rcopy(ins[w].at[2 * i + 1 - c], lnd[w].at[i], send_sems.at[4 * w + i], recv_sems.at[4 * w + i], (x, y, 1 - c))
                cp.wait_send()
                cp.wait_recv()

    out = pl.pallas_call(
        body, name=name, out_shape=tuple(_like(a) for a in grads + lands), in_specs=[_HBM] * (2 * n) + [_SEM, _SEM, _ANY],
        out_specs=(_HBM,) * (2 * n), input_output_aliases={i: i for i in range(2 * n)}, compiler_params=_SIDE)(*grads, *lands, send, recv, after)
    return list(out[:n]), list(out[n:])


def pair_sum(name, grad, land, core):
    _, r, c = grad.shape
    tr = r
    if r % 8 == 0:
        tr = max(8, min(r, 4 * ADAMW_TILE_ELEMS // c) // 8 * 8)
        while r % tr:
            tr -= 8

    def body(core_ref, a_ref, b_ref, o_ref):
        o_ref[...] = (a_ref[...].astype(F32) + b_ref[...].astype(F32)).astype(o_ref.dtype)

    return pl.pallas_call(
        body, name=name, out_shape=jax.ShapeDtypeStruct((4, r, c), grad.dtype),
        grid_spec=pltpu.PrefetchScalarGridSpec(
            num_scalar_prefetch=1, grid=(4, r // tr),
            in_specs=[pl.BlockSpec((None, None, tr, c), lambda i, j, core_ref: (i, core_ref[0], j, 0)),
                      pl.BlockSpec((None, tr, c), lambda i, j, core_ref: (i, j, 0))],
            out_specs=pl.BlockSpec((None, tr, c), lambda i, j, core_ref: (i, j, 0))),
        compiler_params=_params("parallel", "parallel"))(core, grad.reshape(4, 2, r, c), land)


def rs_ici_start(name, sums):
    n = len(sums)
    lands = [lax.empty(a.shape, a.dtype) for a in sums]

    def body(*refs):
        ins, lnd, send_sems, recv_sems, token = refs[:n], refs[n:2 * n], refs[2 * n], refs[2 * n + 1], refs[4 * n + 2]
        x, y, c = _position()
        chip = 2 * x + y
        for w in range(n):
            for j, other in enumerate(_other_chips(x, y)):
                _rcopy(ins[w].at[2 * other[0] + other[1]], lnd[w].at[chip], send_sems.at[3 * w + j], recv_sems.at[3 * w + j], (*other, c)).start()
        token[...] = jnp.zeros_like(token)

    out = pl.pallas_call(
        body, name=name, out_shape=(_dma_sems(3 * n), _dma_sems(3 * n)) + tuple(_like(a) for a in sums + lands) + (_TOKEN,),
        in_specs=[_HBM] * (2 * n), out_specs=(_SEM, _SEM) + (_HBM,) * (2 * n) + (_VM,),
        input_output_aliases={i: 2 + i for i in range(2 * n)}, compiler_params=_SIDE)(*[_hbm(a) for a in sums + lands])
    _TOKENS.push(out[-1])
    return out[0], out[1], list(out[2:2 + n]), list(out[2 + n:2 + 2 * n])


def rs_ici_wait(name, started, after):
    send, recv, sums, lands = started
    n = len(sums)

    def body(*refs):
        ins, lnd, send_sems, recv_sems = refs[:n], refs[n:2 * n], refs[2 * n], refs[2 * n + 1]
        x, y, c = _position()
        for w in range(n):
            for j, other in enumerate(_other_chips(x, y)):
                cp = _rcopy(ins[w].at[2 * other[0] + other[1]], lnd[w].at[2 * other[0] + other[1]], send_sems.at[3 * w + j], recv_sems.at[3 * w + j], (*other, c))
                cp.wait_send()
                cp.wait_recv()

    out = pl.pallas_call(
        body, name=name, out_shape=tuple(_like(a) for a in sums + lands), in_specs=[_HBM] * (2 * n) + [_SEM, _SEM, _ANY],
        out_specs=(_HBM,) * (2 * n), input_output_aliases={i: i for i in range(2 * n)}, compiler_params=_SIDE)(*sums, *lands, send, recv, after)
    chip = 2 * lax.axis_index("x") + lax.axis_index("y")
    return [lax.dynamic_update_index_in_dim(land, lax.dynamic_index_in_dim(s, chip, 0, keepdims=False), chip, 0)
            for s, land in zip(out[:n], out[n:])]


def ada_fwd(c, w_ada, b_ada3, conv_w):
    d, cs = w_ada.shape

    def body(c_ref, w_ref, b_ref, cw_ref, mod_ref, sc_ref, cwa_ref, part_ref, send_sems, recv_sems):
        me = _position()
        my = _index(me)
        cv = c_ref[...]
        sc_ref[my] = cv * _sigmoid(cv)
        cwa_ref[my] = cw_ref[...]
        gather = []
        for r in range(1, N_DEV):
            for k, ref in enumerate((sc_ref, cwa_ref)):
                cp = pltpu.make_async_remote_copy(src_ref=ref.at[my], dst_ref=ref.at[my], send_sem=send_sems.at[14 * k + r - 1],
                                                  recv_sem=recv_sems.at[14 * k + r - 1], device_id=_peer(me, r), device_id_type=MESH)
                cp.start()
                gather.append(cp)
        for cp in gather:
            cp.wait()
        sc_all = jnp.concatenate([sc_ref[k] for k in range(N_DEV)], axis=0).astype(BF16)
        part = jnp.dot(sc_all, w_ref[...].astype(BF16), preferred_element_type=F32)
        for k in range(N_DEV):
            part_ref[k] = part[k:k + 1, :]
        scatter = []
        for r in range(1, N_DEV):
            peer = _peer(me, r)
            cp = pltpu.make_async_remote_copy(src_ref=part_ref.at[_index(peer)], dst_ref=mod_ref.at[my], send_sem=send_sems.at[6 + r],
                                              recv_sem=recv_sems.at[6 + r], device_id=peer, device_id_type=MESH)
            cp.start()
            scatter.append(cp)
        mod_ref[my] = part_ref[my]
        for cp in scatter:
            cp.wait()
        mod_ref[...] = mod_ref[...] + b_ref[...]

    vm = pl.BlockSpec(memory_space=pltpu.VMEM)
    return pl.pallas_call(
        body, name="ada_fwd",
        out_shape=[jax.ShapeDtypeStruct((N_DEV, 1, cs), F32), jax.ShapeDtypeStruct((N_DEV, 1, d), F32),
                   jax.ShapeDtypeStruct((N_DEV,) + conv_w.shape, F32)],
        in_specs=[vm, vm, vm, vm], out_specs=[vm, vm, vm],
        scratch_shapes=[pltpu.VMEM((N_DEV, 1, cs), F32), pltpu.SemaphoreType.DMA((21,)), pltpu.SemaphoreType.DMA((21,))],
        compiler_params=pltpu.CompilerParams(vmem_limit_bytes=VMEM_LIMIT_BYTES))(c, w_ada, b_ada3, conv_w)


def ada_bwd_w(sc_all, dmod_cols):
    _, d = sc_all.shape
    cs = dmod_cols.shape[1]
    tr = _tile(d, ROW_TILE)

    def body(sc_ref, dm_ref, o_ref):
        dm = dm_ref[...].astype(BF16)
        o_ref[...] = lax.dot_general(sc_ref[...].astype(BF16), dm, (((0,), (0,)), ((), ())), preferred_element_type=F32)

    return pl.pallas_call(body, name="ada_bwd_w", grid=(d // tr,),
                          in_specs=[pl.BlockSpec((N_DEV, tr), lambda i: (0, i)), _full((N_DEV, cs))],
                          out_specs=pl.BlockSpec((None, tr, cs), lambda i: (0, i, 0)),
                          out_shape=jax.ShapeDtypeStruct((1, d, cs), F32), compiler_params=_params("parallel"))(sc_all, dmod_cols)


def _round_up(n, m):
    return (n + m - 1) // m * m


def kernel(x, c, positions, w_ada, b_ada, pre_norm1_g, w_in, gm_ln_g, gm_ln_b, gm_w_s, gm_b_s, w_branch_a, q_norm_g, w_uq, kv_norm_g, w_ukv, w_branch_b, w_out, post_norm1_g, pre_norm2_g, w_up, conv_w, conv_b, w_down, post_norm2_g, loss_target, m_w_ada, m_b_ada, m_pre_norm1_g, m_w_in, m_gm_ln_g, m_gm_ln_b, m_gm_w_s, m_gm_b_s, m_w_branch_a, m_q_norm_g, m_w_uq, m_kv_norm_g, m_w_ukv, m_w_branch_b, m_w_out, m_post_norm1_g, m_pre_norm2_g, m_w_up, m_conv_w, m_conv_b, m_w_down, m_post_norm2_g, v_w_ada, v_b_ada, v_pre_norm1_g, v_w_in, v_gm_ln_g, v_gm_ln_b, v_gm_w_s, v_gm_b_s, v_w_branch_a, v_q_norm_g, v_w_uq, v_kv_norm_g, v_w_ukv, v_w_branch_b, v_w_out, v_post_norm1_g, v_pre_norm2_g, v_w_up, v_conv_w, v_conv_b, v_w_down, v_post_norm2_g):
    weights = dict(w_ada=w_ada, b_ada=b_ada, pre_norm1_g=pre_norm1_g, w_in=w_in, gm_ln_g=gm_ln_g, gm_ln_b=gm_ln_b, gm_w_s=gm_w_s,
                   gm_b_s=gm_b_s, w_branch_a=w_branch_a, q_norm_g=q_norm_g, w_uq=w_uq, kv_norm_g=kv_norm_g, w_ukv=w_ukv,
                   w_branch_b=w_branch_b, w_out=w_out, post_norm1_g=post_norm1_g, pre_norm2_g=pre_norm2_g, w_up=w_up, conv_w=conv_w,
                   conv_b=conv_b, w_down=w_down, post_norm2_g=post_norm2_g)
    mom1 = dict(w_ada=m_w_ada, b_ada=m_b_ada, pre_norm1_g=m_pre_norm1_g, w_in=m_w_in, gm_ln_g=m_gm_ln_g, gm_ln_b=m_gm_ln_b,
                gm_w_s=m_gm_w_s, gm_b_s=m_gm_b_s, w_branch_a=m_w_branch_a, q_norm_g=m_q_norm_g, w_uq=m_w_uq, kv_norm_g=m_kv_norm_g,
                w_ukv=m_w_ukv, w_branch_b=m_w_branch_b, w_out=m_w_out, post_norm1_g=m_post_norm1_g, pre_norm2_g=m_pre_norm2_g,
                w_up=m_w_up, conv_w=m_conv_w, conv_b=m_conv_b, w_down=m_w_down, post_norm2_g=m_post_norm2_g)
    mom2 = dict(w_ada=v_w_ada, b_ada=v_b_ada, pre_norm1_g=v_pre_norm1_g, w_in=v_w_in, gm_ln_g=v_gm_ln_g, gm_ln_b=v_gm_ln_b,
                gm_w_s=v_gm_w_s, gm_b_s=v_gm_b_s, w_branch_a=v_w_branch_a, q_norm_g=v_q_norm_g, w_uq=v_w_uq, kv_norm_g=v_kv_norm_g,
                w_ukv=v_w_ukv, w_branch_b=v_w_branch_b, w_out=v_w_out, post_norm1_g=v_post_norm1_g, pre_norm2_g=v_pre_norm2_g,
                w_up=v_w_up, conv_w=v_conv_w, conv_b=v_conv_b, w_down=v_w_down, post_norm2_g=v_post_norm2_g)
    order = list(weights)
    _TOKENS.clear()

    s, d = x.shape[1], x.shape[2]
    gmw = gm_ln_g.shape[0]
    groups = gmw // CHUNK
    ql, kvl = q_norm_g.shape[0], kv_norm_g.shape[0]
    f2 = conv_b.shape[0]
    in_cols = w_in.shape[1] * N_DEV
    o_q, o_kv, o_ga, o_gb, o_kpe = 2 * gmw, 2 * gmw + ql, 2 * gmw + ql + kvl, 2 * gmw + ql + kvl + d, 2 * gmw + ql + kvl + 2 * d
    zp = _round_up(o_kpe + LANES, Z_PAD)
    src_kpe = 2 * gmw + ql + kvl
    assert src_kpe + QK_ROPE + 2 * d == in_cols
    my = 4 * lax.axis_index("x") + 2 * lax.axis_index("y") + lax.axis_index("c")

    x2, tgt = x[0], loss_target[0]
    row = lambda a: a.reshape(1, -1)

    big = ["w_in", "w_branch_a", "w_uq", "w_ukv", "w_branch_b", "w_out", "w_up", "w_down"]
    sh = {k: weights[k].astype(BF16) for k in big}
    mix = ["w_branch_a", "w_uq", "w_ukv", "w_branch_b", "w_out"]
    ag_in = ag_start("ag_start_in", [sh["w_in"]], c)

    mod8, sc_all3, g_cw = ada_fwd(c, w_ada, b_ada.reshape(N_DEV, 1, -1), conv_w)
    mod = mod8.reshape(N_MOD, d)
    shift1, scale1, gate1, shift2, scale2, gate2 = (mod[i:i + 1] for i in range(N_MOD))
    sc_all = sc_all3.reshape(N_DEV, d)
    h1 = norm_mod_fwd("pre1_fwd", x2, row(pre_norm1_g), scale1, shift1)

    inv = ROPE_THETA ** (-jnp.arange(0, QK_ROPE, 2, dtype=F32) / QK_ROPE)
    ang = positions[0].astype(F32)[:, None] * inv
    cos4 = jnp.tile(jnp.cos(ang), (1, 4))
    sin4 = jnp.tile(jnp.concatenate([-jnp.sin(ang), jnp.sin(ang)], axis=1), (1, 2))

    wm = (gm_w_s * jnp.tril(jnp.ones((CHUNK, CHUNK), F32))).astype(BF16)
    bs3 = gm_b_s.reshape(groups, CHUNK, 1)
    ln_g, ln_b = row(gm_ln_g), row(gm_ln_b)

    early = [h1, cos4, sin4, wm] + [sh[k] for k in big[1:]]
    (g_in,) = ag_wait("ag_wait_in", ag_forward("ag_forward_in", ag_in, early), h1)
    ag_mix = ag_start("ag_start_mix", [sh[k] for k in mix], g_in)
    w_in_f = g_in.transpose(1, 0, 2).reshape(d, in_cols)
    w_in_p = jnp.concatenate([w_in_f[:, :src_kpe], w_in_f[:, src_kpe + QK_ROPE:], w_in_f[:, src_kpe:src_kpe + QK_ROPE],
                              jnp.zeros((d, zp - in_cols), BF16)], axis=1)

    z = mm_nn("z_proj", h1, w_in_p, F32)
    ag_mix = ag_forward("ag_forward_mix", ag_mix, z)
    a = gmlp_fwd(z, gmw, ln_g, ln_b, wm, bs3)
    g_a, g_uq, g_ukv, g_b, g_out = ag_wait("ag_wait_mix", ag_mix, a)
    ag_up = ag_start("ag_start_up", [sh["w_up"]], g_out)
    w_a_f, w_b_f, w_out_f = g_a.reshape(-1, d), g_b.reshape(-1, d), g_out.reshape(-1, d)
    w_uq_f = g_uq.transpose(1, 0, 2).reshape(ql, HEADS, QK_NOPE + QK_ROPE)
    w_uq_n = w_uq_f[:, :, :QK_NOPE].reshape(ql, HEADS * QK_NOPE)
    w_uq_r = w_uq_f[:, :, QK_NOPE:].reshape(ql, HEADS * QK_ROPE)
    y_a = mm_nn("branch_a", a, w_a_f, F32)
    qln = rms_fwd_cols("q_norm", z, o_q, ql, row(q_norm_g))
    kvn = rms_fwd_cols("kv_norm", z, o_kv, kvl, row(kv_norm_g))
    qn = mm_nn("q_nope", qln, w_uq_n, BF16)
    qp = mm_nn("q_rope", qln, w_uq_r, F32)
    kv = mm_nn_b3("kv_up", kvn, g_ukv, BF16)
    kpr = rope_k(z, o_kpe, cos4, sin4)
    o, qpr, lse = attn_fwd2(qn, qp, kv, kpr, cos4, sin4)
    ag_up = ag_forward("ag_forward_up", ag_up, o)
    y_b = mm_nn("branch_b", o, w_b_f, F32)
    merged = merge_fwd(z, o_ga, o_gb, y_a, y_b)
    y1 = mm_nn("out_proj", merged, w_out_f, F32)
    x1 = post_res_fwd("post1_fwd", x2, y1, gate1, row(post_norm1_g))
    h2 = norm_mod_fwd("pre2_fwd", x1, row(pre_norm2_g), scale2, shift2)
    (g_up,) = ag_wait("ag_wait_up", ag_up, h2)
    ag_down = ag_start("ag_start_down", [sh["w_down"]], g_up)
    upre = mm_nn_b3("up_proj", h2, g_up, F32)
    ag_down = ag_forward("ag_forward_down", ag_down, upre)
    cw = g_cw.transpose(1, 0, 2).reshape(3, f2)
    cb = row(conv_b)
    f = conv_act_fwd(upre, cw, cb)
    w_down_f = ag_wait("ag_wait_down", ag_down, f)[0].reshape(-1, d)
    ffn = mm_nn("down_proj", f, w_down_f, F32)
    loss_acc, dout, dffn, acc2 = post2_loss_bwd(x1, ffn, tgt, gate2, row(post_norm2_g))

    blocks = lambda g: g.reshape(N_DEV, g.shape[0] // N_DEV, g.shape[1])
    core = lax.axis_index("c").astype(jnp.int32).reshape(1)
    rs = {}

    def rs_begin(key, grads):
        rs[key] = rs_d2d_start("rs_d2d_start_" + key, grads)

    def rs_middle(key, after):
        grads, lands = rs_d2d_wait("rs_d2d_wait_" + key, rs[key], after)
        sums = [pair_sum("pair_sum_%s_%d" % (key, i), g, l, core) for i, (g, l) in enumerate(zip(grads, lands))]
        rs[key] = rs_ici_start("rs_ici_start_" + key, sums)

    gw_down = mm_tn("g_w_down", f, dffn, BF16)
    rs_begin("down", [blocks(gw_down)])
    df = mm_nt("d_f", dffn, w_down_f, F32)
    rs_middle("down", df)
    dupre, gcw_g, gcw_v, gcb_g, gcb_v = conv_act_bwd(upre, cw, cb, df)
    gw_up3 = mm_tn_h3("g_w_up", h2, dupre, N_DEV, BF16)
    rs_begin("up", [gw_up3])
    dh2 = mm_nt_h3("d_h2", dupre, g_up, F32)
    rs_middle("up", dh2)
    dx1, dy1, acc_mid = mid_bwd(dh2, dout, x1, y1, row(pre_norm2_g), scale2, gate1, row(post_norm1_g))
    gw_out = mm_tn("g_w_out", merged, dy1, BF16)
    dmerged = mm_nt("d_merged", dy1, w_out_f, F32)
    dya, dyb, dga, dgb = merge_bwd(z, o_ga, o_gb, y_a, y_b, dmerged)
    gw_a = mm_tn("g_w_a", a, dya, BF16)
    gw_b = mm_tn("g_w_b", o, dyb, BF16)
    rs_begin("mid", [blocks(gw_out), blocks(gw_a), blocks(gw_b)])
    da = mm_nt("d_a", dya, w_a_f, F32)
    do = mm_nt("d_o", dyb, w_b_f, F32)
    rs_middle("mid", do)
    duv, g_ws, g_bs3, acc_gm = gmlp_bwd(z, gmw, da, ln_g, ln_b, wm, bs3)
    dqn, dqp, dkv, dkp = attn_bwd2(qn, qpr, kv, kpr, o, do, lse, cos4, sin4)
    dkpe = kpe_bwd(dkp, cos4, sin4, zp - o_kpe)
    dq_cat = jnp.concatenate([dqn, dqp], axis=1)
    w_uq_cat = jnp.concatenate([w_uq_n, w_uq_r], axis=1)
    gw_uq_cat = mm_tn("g_w_uq", qln, dq_cat, BF16)
    gw_uq_f = jnp.concatenate([gw_uq_cat[:, :HEADS * QK_NOPE].reshape(ql, HEADS, QK_NOPE),
                               gw_uq_cat[:, HEADS * QK_NOPE:].reshape(ql, HEADS, QK_ROPE)], axis=2)
    gw_uq3 = gw_uq_f.reshape(ql, N_DEV, -1).transpose(1, 0, 2)
    gw_ukv3 = mm_tn_o3("g_w_ukv", kvn, dkv, N_DEV, BF16)
    rs_begin("mla", [gw_uq3, gw_ukv3])
    dqln = mm_nt("d_qln", dq_cat, w_uq_cat, F32)
    dq_lat, g_qnorm = rms_bwd_cols("q_norm_bwd", dqln, z, o_q, ql, row(q_norm_g))
    dkvn = mm_nt_b3("d_kvn", dkv, g_ukv, F32)
    rs_middle("mla", dkvn)
    dkv_lat, g_kvnorm = rms_bwd_cols("kv_norm_bwd", dkvn, z, o_kv, kvl, row(kv_norm_g))
    dz = jnp.concatenate([duv, dq_lat, dkv_lat, dga, dgb, dkpe], axis=1)
    gw_in_p = mm_tn("g_w_in", h1, dz, BF16)
    gw_in_f = jnp.concatenate([gw_in_p[:, :src_kpe], gw_in_p[:, o_kpe:o_kpe + QK_ROPE], gw_in_p[:, src_kpe:o_kpe]], axis=1)
    gw_in3 = gw_in_f.reshape(d, N_DEV, -1).transpose(1, 0, 2)
    rs_begin("in", [gw_in3])
    dh1 = mm_nt("d_h1", dz, w_in_p, F32)
    grad_x, acc1 = pre1_bwd(dh1, dx1, x2, row(pre_norm1_g), scale1)

    dmod = jnp.concatenate([acc1[0], acc1[1], acc_mid[3], acc_mid[0], acc_mid[1], acc2[0]])
    small = [("pre_norm1_g", acc1[2]), ("gm_ln_g", acc_gm[0]), ("gm_ln_b", acc_gm[1]), ("gm_b_s", g_bs3.reshape(-1)),
             ("q_norm_g", g_qnorm[0]), ("kv_norm_g", g_kvnorm[0]), ("post_norm1_g", acc_mid[4]), ("pre_norm2_g", acc_mid[2]),
             ("conv_b", jnp.concatenate([gcb_g[0], gcb_v[0]])), ("post_norm2_g", acc2[1]), ("gm_w_s", g_ws.reshape(-1)),
             ("b_ada", dmod)]
    n_small = sum(v.shape[0] for _, v in small)
    n_cw = 3 * f2
    n_pack = _round_up(n_small + n_cw, PACK_ALIGN)
    tail = jnp.zeros((n_pack - n_small - n_cw,), F32)
    packed = jnp.concatenate([v for _, v in small] + [jnp.concatenate([gcw_g, gcw_v], axis=1).reshape(-1), tail])
    ag_small = ag_start("ag_start_small", [packed.reshape(-1, LANES)], packed)
    rs_middle("in", packed)

    res = {}
    last = packed
    for key, names in (("down", ["w_down"]), ("up", ["w_up"]), ("mid", ["w_out", "w_branch_a", "w_branch_b"]), ("mla", ["w_uq", "w_ukv"])):
        parts = rs_ici_wait("rs_ici_wait_" + key, rs[key], last)
        for k, p in zip(names, parts):
            res[k] = adamw("adamw_" + k, weights[k], mom1[k], mom2[k], p)
            last = res[k][0]

    def pack(src):
        return jnp.concatenate([src[k].reshape(-1) for k, _ in small] + [jnp.zeros((n_pack - n_small,), F32)]).reshape(-1, LANES)

    (gathered,) = ag_wait("ag_wait_small", ag_forward("ag_forward_small", ag_small, last), last)
    sm = [t.reshape(-1) for t in adamw("adamw_small", pack(weights), pack(mom1), pack(mom2), gathered)]
    off = 0
    for k, v in small:
        res[k] = tuple(t[off:off + v.shape[0]].reshape(weights[k].shape) for t in sm)
        off += v.shape[0]

    cs_cw = conv_w.shape[1]
    g_cw_full = sm[0][n_small:n_small + n_cw].reshape(3, f2)
    g_cw_mine = lax.dynamic_slice(g_cw_full, (0, my * cs_cw), (3, cs_cw))
    res["conv_w"] = adamw("adamw_conv_w", conv_w, mom1["conv_w"], mom2["conv_w"], g_cw_mine[None])

    cs_ada = w_ada.shape[1]
    off_b = n_small - N_MOD * d
    dmod_all = gathered.reshape(N_DEV, -1)[:, off_b:off_b + N_MOD * d]
    dmod_cols = lax.dynamic_slice(dmod_all, (0, my * cs_ada), (N_DEV, cs_ada))
    res["w_ada"] = adamw("adamw_w_ada", w_ada, mom1["w_ada"], mom2["w_ada"], ada_bwd_w(sc_all, dmod_cols))

    (p_in,) = rs_ici_wait("rs_ici_wait_in", rs["in"], res["w_ada"][0])
    res["w_in"] = tuple(t.T for t in adamw("adamw_w_in", w_in.T, mom1["w_in"].T, mom2["w_in"].T, p_in.transpose(0, 2, 1)))

    _TOKENS.clear()
    loss = lax.psum(loss_acc[0, 0], ("x", "y", "c"))
    outs = [loss, grad_x[None]]
    for i in range(4):
        outs += [res[k][i] for k in order]
    return tuple(outs)
```

```python
import functools

import jax
import jax.numpy as jnp
from jax import lax
from jax.experimental import pallas as pl
from jax.experimental.pallas import tpu as pltpu

F32 = jnp.float32
BF16 = jnp.bfloat16

N_DEV = 8
HEADS = 16
QK_NOPE = 128
QK_ROPE = 64
V_HEAD = 128
CHUNK = 128
ROPE_THETA = 10000.0
EPS = 1e-6
N_MOD = 6
ADAM_LR, ADAM_B1, ADAM_B2, ADAM_EPS, ADAM_WD, ADAM_STEP = 0.001, 0.9, 0.999, 1e-08, 0.01, 10

LANES = 128
VMEM_LIMIT_BYTES = 48 * 2 ** 20
ROW_TILE = 256
COL_TILE = 256
ATT_TILE = 256
Z_PAD = 512
ADAMW_TILE_ELEMS = 1 << 18
PACK_ALIGN = 8 * LANES
MESH = pl.DeviceIdType.MESH


def _params(*sem):
    return pltpu.CompilerParams(dimension_semantics=sem if sem else None, vmem_limit_bytes=VMEM_LIMIT_BYTES)


def _tile(dim, target):
    t = (min(dim, target) // LANES) * LANES
    while t >= LANES:
        if dim % t == 0:
            return t
        t -= LANES
    return dim


def _full(shape):
    nd = len(shape)
    return pl.BlockSpec(shape, lambda *_: (0,) * nd)


class _Tokens:
    KEEP = 2

    def __init__(self):
        self.pending = []

    def push(self, token):
        self.pending = (self.pending + [token])[-self.KEEP:]

    def take(self):
        return list(self.pending)

    def clear(self):
        self.pending = []


_TOKENS = _Tokens()


def _matmul(name, a, b, *, grid, a_spec, b_spec, o_spec, out_shape, contract, acc_shape):
    nk = grid[2]
    deps = _TOKENS.take()

    def product(a_ref, b_ref):
        return lax.dot_general(a_ref[...].astype(BF16), b_ref[...].astype(BF16), (contract, ((), ())), preferred_element_type=F32)

    def body_one_step(a_ref, b_ref, *rest):
        o_ref = rest[len(deps)]
        o_ref[...] = product(a_ref, b_ref).astype(o_ref.dtype)

    def body(a_ref, b_ref, *rest):
        o_ref, acc_ref = rest[len(deps):]
        k = pl.program_id(2)

        @pl.when(k == 0)
        def _():
            acc_ref[...] = jnp.zeros_like(acc_ref)

        acc_ref[...] += product(a_ref, b_ref)

        @pl.when(k == nk - 1)
        def _():
            o_ref[...] = acc_ref[...].astype(o_ref.dtype)

    return pl.pallas_call(
        body_one_step if nk == 1 else body, name=name, grid=grid,
        in_specs=[a_spec, b_spec] + [pl.BlockSpec(memory_space=pl.ANY)] * len(deps),
        out_specs=o_spec, out_shape=out_shape, scratch_shapes=[] if nk == 1 else [pltpu.VMEM(acc_shape, F32)],
        compiler_params=_params("parallel", "parallel", "arbitrary"))(a, b, *deps)


TM, TN, TK = 1024, 1024, 2304


def _tk(a, b):
    return TK if a.dtype == BF16 and b.dtype == BF16 else TK // 2


def mm_nn(name, a, b, dtype):
    (m, k), n = a.shape, b.shape[1]
    tm, tn, tk = _tile(m, TM), _tile(n, TN), _tile(k, _tk(a, b))
    return _matmul(name, a, b, grid=(m // tm, n // tn, k // tk),
                   a_spec=pl.BlockSpec((tm, tk), lambda i, j, kk: (i, kk)),
                   b_spec=pl.BlockSpec((tk, tn), lambda i, j, kk: (kk, j)),
                   o_spec=pl.BlockSpec((tm, tn), lambda i, j, kk: (i, j)),
                   out_shape=jax.ShapeDtypeStruct((m, n), dtype), contract=((1,), (0,)), acc_shape=(tm, tn))


def mm_nn_b3(name, a, b3, dtype):
    (m, k), (nj, _, cs) = a.shape, b3.shape
    tm, tk = _tile(m, TM), _tile(k, _tk(a, b3))
    return _matmul(name, a, b3, grid=(m // tm, nj, k // tk),
                   a_spec=pl.BlockSpec((tm, tk), lambda i, j, kk: (i, kk)),
                   b_spec=pl.BlockSpec((None, tk, cs), lambda i, j, kk: (j, kk, 0)),
                   o_spec=pl.BlockSpec((tm, cs), lambda i, j, kk: (i, j)),
                   out_shape=jax.ShapeDtypeStruct((m, nj * cs), dtype), contract=((1,), (0,)), acc_shape=(tm, cs))


def mm_nt(name, a, b, dtype):
    (m, k), n = a.shape, b.shape[0]
    tm, tn, tk = _tile(m, TM), _tile(n, TN), _tile(k, _tk(a, b))
    return _matmul(name, a, b, grid=(m // tm, n // tn, k // tk),
                   a_spec=pl.BlockSpec((tm, tk), lambda i, j, kk: (i, kk)),
                   b_spec=pl.BlockSpec((tn, tk), lambda i, j, kk: (j, kk)),
                   o_spec=pl.BlockSpec((tm, tn), lambda i, j, kk: (i, j)),
                   out_shape=jax.ShapeDtypeStruct((m, n), dtype), contract=((1,), (1,)), acc_shape=(tm, tn))


def mm_nt_b3(name, a, b3, dtype):
    m, (nj, n, cs) = a.shape[0], b3.shape
    tm, tn = _tile(m, TM), _tile(n, TN)
    return _matmul(name, a, b3, grid=(m // tm, n // tn, nj),
                   a_spec=pl.BlockSpec((tm, cs), lambda i, j, kk: (i, kk)),
                   b_spec=pl.BlockSpec((None, tn, cs), lambda i, j, kk: (kk, j, 0)),
                   o_spec=pl.BlockSpec((tm, tn), lambda i, j, kk: (i, j)),
                   out_shape=jax.ShapeDtypeStruct((m, n), dtype), contract=((1,), (1,)), acc_shape=(tm, tn))


def mm_nt_h3(name, a3, b3, dtype):
    (_, m, _), (nj, n, cs) = a3.shape, b3.shape
    tm, tn, hj = _tile(m, TM), _tile(n, TN), nj // 2
    return _matmul(name, a3, b3, grid=(m // tm, n // tn, nj),
                   a_spec=pl.BlockSpec((None, tm, cs), lambda i, j, kk: (kk // hj, i, kk % hj)),
                   b_spec=pl.BlockSpec((None, tn, cs), lambda i, j, kk: (kk, j, 0)),
                   o_spec=pl.BlockSpec((tm, tn), lambda i, j, kk: (i, j)),
                   out_shape=jax.ShapeDtypeStruct((m, n), dtype), contract=((1,), (1,)), acc_shape=(tm, tn))


def mm_tn_h3(name, a, b3, nj, dtype):
    (k, m), half = a.shape, b3.shape[2]
    hj = nj // 2
    cs = half // hj
    tm, tk = _tile(m, TM), _tile(k, _tk(a, b3))
    return _matmul(name, a, b3, grid=(m // tm, nj, k // tk),
                   a_spec=pl.BlockSpec((tk, tm), lambda i, j, kk: (kk, i)),
                   b_spec=pl.BlockSpec((None, tk, cs), lambda i, j, kk: (j // hj, kk, j % hj)),
                   o_spec=pl.BlockSpec((None, tm, cs), lambda i, j, kk: (j, i, 0)),
                   out_shape=jax.ShapeDtypeStruct((nj, m, cs), dtype), contract=((0,), (0,)), acc_shape=(tm, cs))


def mm_tn(name, a, b, dtype):
    (k, m), n = a.shape, b.shape[1]
    tm, tn, tk = _tile(m, TM), _tile(n, TN), _tile(k, _tk(a, b))
    return _matmul(name, a, b, grid=(m // tm, n // tn, k // tk),
                   a_spec=pl.BlockSpec((tk, tm), lambda i, j, kk: (kk, i)),
                   b_spec=pl.BlockSpec((tk, tn), lambda i, j, kk: (kk, j)),
                   o_spec=pl.BlockSpec((tm, tn), lambda i, j, kk: (i, j)),
                   out_shape=jax.ShapeDtypeStruct((m, n), dtype), contract=((0,), (0,)), acc_shape=(tm, tn))


def mm_tn_o3(name, a, b, nj, dtype):
    (k, m), n = a.shape, b.shape[1]
    cs = n // nj
    tm, tk = _tile(m, TM), _tile(k, _tk(a, b))
    return _matmul(name, a, b, grid=(m // tm, nj, k // tk),
                   a_spec=pl.BlockSpec((tk, tm), lambda i, j, kk: (kk, i)),
                   b_spec=pl.BlockSpec((tk, cs), lambda i, j, kk: (kk, j)),
                   o_spec=pl.BlockSpec((None, tm, cs), lambda i, j, kk: (j, i, 0)),
                   out_shape=jax.ShapeDtypeStruct((nj, m, cs), dtype), contract=((0,), (0,)), acc_shape=(tm, cs))


_GELU_C = 0.7978845608028654
_GELU_A = 0.044715


def _gelu(x):
    return 0.5 * x * (1.0 + jnp.tanh(_GELU_C * (x + _GELU_A * x * x * x)))


def _gelu_and_grad(x):
    t = jnp.tanh(_GELU_C * (x + _GELU_A * x * x * x))
    y = 0.5 * x * (1.0 + t)
    dy = 0.5 * (1.0 + t) + 0.5 * x * (1.0 - t * t) * (_GELU_C * (1.0 + 3.0 * _GELU_A * x * x))
    return y, dy


def _sigmoid(x):
    return 1.0 / (1.0 + jnp.exp(-x))


def _rms_stats(x):
    inv = lax.rsqrt(jnp.mean(x * x, axis=-1, keepdims=True) + EPS)
    return inv, x * inv


def _rms_bwd(dyhat, yhat, inv):
    return inv * (dyhat - yhat * jnp.mean(dyhat * yhat, axis=-1, keepdims=True))


def _colsum(x):
    return jnp.sum(x, axis=0, keepdims=True)


def _rope(x, cos4, sin4):
    lane = lax.broadcasted_iota(jnp.int32, x.shape, x.ndim - 1)
    first_half = (lane % QK_ROPE) < (QK_ROPE // 2)
    partner = jnp.where(first_half, pltpu.roll(x, LANES - QK_ROPE // 2, x.ndim - 1), pltpu.roll(x, QK_ROPE // 2, x.ndim - 1))
    return x * cos4 + partner * sin4


def norm_mod_fwd(name, x, g, scale, shift):
    s, d = x.shape
    tr = _tile(s, ROW_TILE)

    def body(x_ref, g_ref, sc_ref, sh_ref, o_ref):
        _, xh = _rms_stats(x_ref[...])
        o_ref[...] = (xh * g_ref[...] * (1.0 + sc_ref[...]) + sh_ref[...]).astype(o_ref.dtype)

    row = pl.BlockSpec((tr, d), lambda i: (i, 0))
    vec = pl.BlockSpec((1, d), lambda i: (0, 0))
    return pl.pallas_call(body, name=name, grid=(s // tr,), in_specs=[row, vec, vec, vec], out_specs=row,
                          out_shape=jax.ShapeDtypeStruct((s, d), BF16), compiler_params=_params("parallel"))(x, g, scale, shift)


def rms_fwd_cols(name, z, off, width, g):
    s = z.shape[0]
    tr = _tile(s, ROW_TILE)
    assert off % width == 0

    def body(x_ref, g_ref, o_ref):
        _, xh = _rms_stats(x_ref[...])
        o_ref[...] = (xh * g_ref[...]).astype(o_ref.dtype)

    return pl.pallas_call(body, name=name, grid=(s // tr,),
                          in_specs=[pl.BlockSpec((tr, width), lambda i: (i, off // width)), pl.BlockSpec((1, width), lambda i: (0, 0))],
                          out_specs=pl.BlockSpec((tr, width), lambda i: (i, 0)),
                          out_shape=jax.ShapeDtypeStruct((s, width), BF16), compiler_params=_params("parallel"))(z, g)


def rms_bwd_cols(name, dy, z, off, width, g):
    s = z.shape[0]
    tr = _tile(s, ROW_TILE)

    def body(dy_ref, x_ref, g_ref, dx_ref, gg_ref):
        @pl.when(pl.program_id(0) == 0)
        def _():
            gg_ref[...] = jnp.zeros_like(gg_ref)

        inv, xh = _rms_stats(x_ref[...])
        dy_v = dy_ref[...]
        gg_ref[...] += _colsum(dy_v * xh)
        dx_ref[...] = _rms_bwd(dy_v * g_ref[...], xh, inv).astype(dx_ref.dtype)

    return pl.pallas_call(body, name=name, grid=(s // tr,),
                          in_specs=[pl.BlockSpec((tr, width), lambda i: (i, 0)), pl.BlockSpec((tr, width), lambda i: (i, off // width)),
                                    pl.BlockSpec((1, width), lambda i: (0, 0))],
                          out_specs=[pl.BlockSpec((tr, width), lambda i: (i, 0)), pl.BlockSpec((1, width), lambda i: (0, 0))],
                          out_shape=[jax.ShapeDtypeStruct((s, width), BF16), jax.ShapeDtypeStruct((1, width), F32)],
                          compiler_params=_params("arbitrary"))(dy, z, g)


def post_res_fwd(name, x, y, gate, g):
    s, d = x.shape
    tr = _tile(s, ROW_TILE)

    def body(x_ref, y_ref, gate_ref, g_ref, o_ref):
        _, yh = _rms_stats(y_ref[...])
        o_ref[...] = x_ref[...] + gate_ref[...] * (yh * g_ref[...])

    row = pl.BlockSpec((tr, d), lambda i: (i, 0))
    vec = pl.BlockSpec((1, d), lambda i: (0, 0))
    return pl.pallas_call(body, name=name, grid=(s // tr,), in_specs=[row, row, vec, vec], out_specs=row,
                          out_shape=jax.ShapeDtypeStruct((s, d), F32), compiler_params=_params("parallel"))(x, y, gate, g)


def post2_loss_bwd(x1, ffn, target, gate2, g):
    s, d = x1.shape
    tr = _tile(s, ROW_TILE)

    def body(x_ref, y_ref, t_ref, gate_ref, g_ref, loss_ref, dout_ref, dy_ref, acc_ref):
        @pl.when(pl.program_id(0) == 0)
        def _():
            loss_ref[...] = jnp.zeros_like(loss_ref)
            acc_ref[...] = jnp.zeros_like(acc_ref)

        inv, yh = _rms_stats(y_ref[...])
        r = yh * g_ref[...]
        err = x_ref[...] + gate_ref[...] * r - t_ref[...]
        loss_ref[...] += 0.5 * jnp.sum(jnp.mean(err * err, axis=-1, keepdims=True))
        dout = err / d
        dout_ref[...] = dout
        dr = dout * gate_ref[...]
        acc_ref[0:1, :] += _colsum(dout * r)
        acc_ref[1:2, :] += _colsum(dr * yh)
        dy_ref[...] = _rms_bwd(dr * g_ref[...], yh, inv).astype(dy_ref.dtype)

    row = pl.BlockSpec((tr, d), lambda i: (i, 0))
    vec = pl.BlockSpec((1, d), lambda i: (0, 0))
    return pl.pallas_call(
        body, name="post2_loss_bwd", grid=(s // tr,), in_specs=[row, row, row, vec, vec],
        out_specs=[_full((8, LANES)), row, row, _full((8, d))],
        out_shape=[jax.ShapeDtypeStruct((8, LANES), F32), jax.ShapeDtypeStruct((s, d), F32),
                   jax.ShapeDtypeStruct((s, d), BF16), jax.ShapeDtypeStruct((8, d), F32)],
        compiler_params=_params("arbitrary"))(x1, ffn, target, gate2, g)


def mid_bwd(dh2, dout, x1, y1, pre2_g, scale2, gate1, post1_g):
    s, d = x1.shape
    tr = _tile(s, ROW_TILE)

    def body(dh_ref, dout_ref, x_ref, y_ref, g2_ref, sc_ref, gate_ref, g1_ref, dx_ref, dy_ref, acc_ref):
        @pl.when(pl.program_id(0) == 0)
        def _():
            acc_ref[...] = jnp.zeros_like(acc_ref)

        dh = dh_ref[...]
        inv2, xh = _rms_stats(x_ref[...])
        acc_ref[0:1, :] += _colsum(dh)
        acc_ref[1:2, :] += _colsum(dh * (xh * g2_ref[...]))
        t = dh * (1.0 + sc_ref[...])
        acc_ref[2:3, :] += _colsum(t * xh)
        dx1 = dout_ref[...] + _rms_bwd(t * g2_ref[...], xh, inv2)
        dx_ref[...] = dx1
        inv1, yh = _rms_stats(y_ref[...])
        acc_ref[3:4, :] += _colsum(dx1 * (yh * g1_ref[...]))
        dr = dx1 * gate_ref[...]
        acc_ref[4:5, :] += _colsum(dr * yh)
        dy_ref[...] = _rms_bwd(dr * g1_ref[...], yh, inv1).astype(dy_ref.dtype)

    row = pl.BlockSpec((tr, d), lambda i: (i, 0))
    vec = pl.BlockSpec((1, d), lambda i: (0, 0))
    return pl.pallas_call(
        body, name="mid_bwd", grid=(s // tr,), in_specs=[row, row, row, row, vec, vec, vec, vec],
        out_specs=[row, row, _full((8, d))],
        out_shape=[jax.ShapeDtypeStruct((s, d), F32), jax.ShapeDtypeStruct((s, d), BF16), jax.ShapeDtypeStruct((8, d), F32)],
        compiler_params=_params("arbitrary"))(dh2, dout, x1, y1, pre2_g, scale2, gate1, post1_g)


def pre1_bwd(dh1, dx1, x, pre1_g, scale1):
    s, d = x.shape
    tr = _tile(s, ROW_TILE)

    def body(dh_ref, dx1_ref, x_ref, g_ref, sc_ref, dx_ref, acc_ref):
        @pl.when(pl.program_id(0) == 0)
        def _():
            acc_ref[...] = jnp.zeros_like(acc_ref)

        dh = dh_ref[...]
        inv, xh = _rms_stats(x_ref[...])
        acc_ref[0:1, :] += _colsum(dh)
        acc_ref[1:2, :] += _colsum(dh * (xh * g_ref[...]))
        t = dh * (1.0 + sc_ref[...])
        acc_ref[2:3, :] += _colsum(t * xh)
        dx_ref[...] = dx1_ref[...] + _rms_bwd(t * g_ref[...], xh, inv)

    row = pl.BlockSpec((tr, d), lambda i: (i, 0))
    vec = pl.BlockSpec((1, d), lambda i: (0, 0))
    return pl.pallas_call(
        body, name="pre1_bwd", grid=(s // tr,), in_specs=[row, row, row, vec, vec], out_specs=[row, _full((8, d))],
        out_shape=[jax.ShapeDtypeStruct((s, d), F32), jax.ShapeDtypeStruct((8, d), F32)],
        compiler_params=_params("arbitrary"))(dh1, dx1, x, pre1_g, scale1)


def _ln_stats(v):
    mu = jnp.mean(v, axis=-1, keepdims=True)
    vc = v - mu
    rstd = lax.rsqrt(jnp.mean(vc * vc, axis=-1, keepdims=True) + EPS)
    return rstd, vc * rstd


def gmlp_fwd(z, width, ln_g, ln_b, wm, bs3):
    s = z.shape[0]
    groups = width // CHUNK

    def body(u_ref, v_ref, g_ref, b_ref, wm_ref, bs_ref, a_ref):
        ug = _gelu(u_ref[...])
        _, vh = _ln_stats(_gelu(v_ref[...]))
        vn = (vh * g_ref[...] + b_ref[...]).astype(BF16)
        for g in range(groups):
            cols = slice(g * CHUNK, (g + 1) * CHUNK)
            mixed = jnp.dot(wm_ref[g], vn[:, cols], preferred_element_type=F32) + bs_ref[g]
            a_ref[:, cols] = (ug[:, cols] * mixed).astype(a_ref.dtype)

    vec = pl.BlockSpec((1, width), lambda n: (0, 0))
    return pl.pallas_call(
        body, name="gmlp_fwd", grid=(s // CHUNK,),
        in_specs=[pl.BlockSpec((CHUNK, width), lambda n: (n, 0)), pl.BlockSpec((CHUNK, width), lambda n: (n, 1)), vec, vec,
                  _full(wm.shape), _full(bs3.shape)],
        out_specs=pl.BlockSpec((CHUNK, width), lambda n: (n, 0)),
        out_shape=jax.ShapeDtypeStruct((s, width), BF16), compiler_params=_params("parallel"))(z, z, ln_g, ln_b, wm, bs3)


def gmlp_bwd(z, width, da, ln_g, ln_b, wm, bs3):
    s = z.shape[0]
    groups = width // CHUNK

    def body(u_ref, v_ref, da_ref, g_ref, b_ref, wm_ref, bs_ref, duv_ref, gw_ref, gb_ref, acc_ref, dvn_ref):
        @pl.when(pl.program_id(0) == 0)
        def _():
            gw_ref[...] = jnp.zeros_like(gw_ref)
            gb_ref[...] = jnp.zeros_like(gb_ref)
            acc_ref[...] = jnp.zeros_like(acc_ref)

        ug, dug = _gelu_and_grad(u_ref[...])
        vg, dvg = _gelu_and_grad(v_ref[...])
        rstd, vh = _ln_stats(vg)
        vn = (vh * g_ref[...] + b_ref[...]).astype(BF16)
        da_v = da_ref[...]
        for g in range(groups):
            cols = slice(g * CHUNK, (g + 1) * CHUNK)
            mixed = jnp.dot(wm_ref[g], vn[:, cols], preferred_element_type=F32) + bs_ref[g]
            duv_ref[:, cols] = (da_v[:, cols] * mixed * dug[:, cols]).astype(duv_ref.dtype)
            dm = da_v[:, cols] * ug[:, cols]
            gb_ref[g] += jnp.sum(dm, axis=-1, keepdims=True)
            dmb = dm.astype(BF16)
            gw_ref[g] += lax.dot_general(dmb, vn[:, cols], (((1,), (1,)), ((), ())), preferred_element_type=F32)
            dvn_ref[:, cols] = lax.dot_general(wm_ref[g], dmb, (((0,), (0,)), ((), ())), preferred_element_type=F32)
        dvn = dvn_ref[...]
        acc_ref[0:1, :] += _colsum(dvn * vh)
        acc_ref[1:2, :] += _colsum(dvn)
        dvh = dvn * g_ref[...]
        dv = rstd * (dvh - jnp.mean(dvh, axis=-1, keepdims=True) - vh * jnp.mean(dvh * vh, axis=-1, keepdims=True))
        duv_ref[:, width:] = (dv * dvg).astype(duv_ref.dtype)

        @pl.when(pl.program_id(0) == pl.num_programs(0) - 1)
        def _():
            q = lax.broadcasted_iota(jnp.int32, gw_ref.shape, 1)
            p = lax.broadcasted_iota(jnp.int32, gw_ref.shape, 2)
            gw_ref[...] = jnp.where(p <= q, gw_ref[...], 0.0)

    vec = pl.BlockSpec((1, width), lambda n: (0, 0))
    blk = pl.BlockSpec((CHUNK, width), lambda n: (n, 0))
    return pl.pallas_call(
        body, name="gmlp_bwd", grid=(s // CHUNK,),
        in_specs=[blk, pl.BlockSpec((CHUNK, width), lambda n: (n, 1)), blk, vec, vec, _full(wm.shape), _full(bs3.shape)],
        out_specs=[pl.BlockSpec((CHUNK, 2 * width), lambda n: (n, 0)), _full(wm.shape), _full(bs3.shape), _full((8, width))],
        out_shape=[jax.ShapeDtypeStruct((s, 2 * width), BF16), jax.ShapeDtypeStruct(wm.shape, F32),
                   jax.ShapeDtypeStruct(bs3.shape, F32), jax.ShapeDtypeStruct((8, width), F32)],
        scratch_shapes=[pltpu.VMEM((CHUNK, width), F32)],
        compiler_params=_params("arbitrary"))(z, z, da, ln_g, ln_b, wm, bs3)


def merge_fwd(z, off_a, off_b, ya, yb):
    s, d = ya.shape
    tr, tc = _tile(s, ROW_TILE * 2), _tile(d, COL_TILE)
    assert off_a % tc == 0 and off_b % tc == 0

    def body(ga_ref, gb_ref, ya_ref, yb_ref, o_ref):
        o_ref[...] = (_sigmoid(ga_ref[...]) * ya_ref[...] + _sigmoid(gb_ref[...]) * yb_ref[...]).astype(o_ref.dtype)

    blk = pl.BlockSpec((tr, tc), lambda i, j: (i, j))
    return pl.pallas_call(
        body, name="merge_fwd", grid=(s // tr, d // tc),
        in_specs=[pl.BlockSpec((tr, tc), lambda i, j: (i, off_a // tc + j)), pl.BlockSpec((tr, tc), lambda i, j: (i, off_b // tc + j)), blk, blk],
        out_specs=blk, out_shape=jax.ShapeDtypeStruct((s, d), BF16), compiler_params=_params("parallel", "parallel"))(z, z, ya, yb)


def merge_bwd(z, off_a, off_b, ya, yb, dm):
    s, d = ya.shape
    tr, tc = _tile(s, ROW_TILE * 2), _tile(d, COL_TILE)
    nc = d // tc

    def body(ga_ref, gb_ref, ya_ref, yb_ref, dm_ref, dya_ref, dyb_ref, dga_ref, dgb_ref):
        dm_v = dm_ref[...]
        sa, sb = _sigmoid(ga_ref[...]), _sigmoid(gb_ref[...])
        dya_ref[...] = (dm_v * sa).astype(dya_ref.dtype)
        dyb_ref[...] = (dm_v * sb).astype(dyb_ref.dtype)
        dga_ref[...] = (dm_v * ya_ref[...] * sa * (1.0 - sa)).astype(dga_ref.dtype)
        dgb_ref[...] = (dm_v * yb_ref[...] * sb * (1.0 - sb)).astype(dgb_ref.dtype)

    blk = pl.BlockSpec((tr, tc), lambda i, j: (i, j))
    out = jax.ShapeDtypeStruct((s, d), BF16)
    return pl.pallas_call(
        body, name="merge_bwd", grid=(s // tr, nc),
        in_specs=[pl.BlockSpec((tr, tc), lambda i, j: (i, off_a // tc + j)), pl.BlockSpec((tr, tc), lambda i, j: (i, off_b // tc + j)), blk, blk, blk],
        out_specs=[blk, blk, blk, blk], out_shape=[out, out, out, out],
        compiler_params=_params("parallel", "parallel"))(z, z, ya, yb, dm)


_ATT_SCALE = (QK_NOPE + QK_ROPE) ** -0.5
_NEG = -1e30


def rope_k(z, off, cos4, sin4):
    s = z.shape[0]
    tr = _tile(s, ROW_TILE * 2)
    assert off % LANES == 0

    def body(k_ref, c_ref, s_ref, o_ref):
        k = k_ref[...]
        k = k + pltpu.roll(k, QK_ROPE, 1)
        o_ref[...] = _rope(k, c_ref[...], s_ref[...]).astype(o_ref.dtype)

    row = pl.BlockSpec((tr, LANES), lambda i: (i, 0))
    return pl.pallas_call(body, name="rope_k", grid=(s // tr,),
                          in_specs=[pl.BlockSpec((tr, LANES), lambda i: (i, off // LANES)), row, row], out_specs=row,
                          out_shape=jax.ShapeDtypeStruct((s, LANES), BF16), compiler_params=_params("parallel"))(z, cos4, sin4)


def _head_masks(shape):
    lane = lax.broadcasted_iota(jnp.int32, shape, 1)
    return lane < QK_ROPE, lane >= QK_ROPE


def _scores(qn, qp_h, k, kp, qi, kb, t):
    sc = lax.dot_general(qn, k, (((1,), (1,)), ((), ())), preferred_element_type=F32)
    sc += lax.dot_general(qp_h, kp, (((1,), (1,)), ((), ())), preferred_element_type=F32)
    sc = sc * _ATT_SCALE
    row = lax.broadcasted_iota(jnp.int32, sc.shape, 0) + qi * t
    col = lax.broadcasted_iota(jnp.int32, sc.shape, 1) + kb * t
    return jnp.where(col <= row, sc, _NEG)


def attn_fwd(qn, qp, kv, kpr, cos4, sin4):
    s = qn.shape[0]
    hp = HEADS // 2
    t = _tile(s, ATT_TILE)
    nq = s // t

    def body(qn_ref, qp_ref, kv_ref, kp_ref, c_ref, s_ref, o_ref, qpr_ref, l_ref):
        qi = pl.program_id(1)
        qpr = _rope(qp_ref[...], c_ref[...], s_ref[...]).astype(BF16)
        qpr_ref[...] = qpr
        masks = _head_masks(qpr.shape)
        for hh in range(2):
            q_n = qn_ref[:, hh * QK_NOPE:(hh + 1) * QK_NOPE]
            q_p = jnp.where(masks[hh], qpr, jnp.zeros_like(qpr))
            kc, vc = 2 * hh * QK_NOPE, (2 * hh + 1) * QK_NOPE

            def step(kb, carry):
                m, l, acc = carry
                rows = pl.ds(pl.multiple_of(kb * t, t), t)
                sc = _scores(q_n, q_p, kv_ref[rows, kc:kc + QK_NOPE], kp_ref[rows, :], qi, kb, t)
                m_new = jnp.maximum(m, jnp.max(sc, axis=-1, keepdims=True))
                alpha = jnp.exp(m - m_new)
                p = jnp.exp(sc - m_new)
                l = alpha * l + jnp.sum(p, axis=-1, keepdims=True)
                acc = alpha * acc + jnp.dot(p.astype(BF16), kv_ref[rows, vc:vc + V_HEAD], preferred_element_type=F32)
                return m_new, l, acc

            init = (jnp.full((t, 1), _NEG, F32), jnp.zeros((t, 1), F32), jnp.zeros((t, V_HEAD), F32))
            m, l, acc = lax.fori_loop(0, qi + 1, step, init)
            o_ref[:, hh * V_HEAD:(hh + 1) * V_HEAD] = acc / l
            l_ref[:, hh:hh + 1] = m + jnp.log(l)

    return pl.pallas_call(
        body, name="attn_fwd", grid=(hp, nq),
        in_specs=[pl.BlockSpec((t, 2 * QK_NOPE), lambda h, i: (i, h)), pl.BlockSpec((t, LANES), lambda h, i: (i, h)),
                  pl.BlockSpec((s, 4 * QK_NOPE), lambda h, i: (0, h)), _full((s, LANES)),
                  pl.BlockSpec((t, LANES), lambda h, i: (i, 0)), pl.BlockSpec((t, LANES), lambda h, i: (i, 0))],
        out_specs=[pl.BlockSpec((t, 2 * V_HEAD), lambda h, i: (i, h)), pl.BlockSpec((t, LANES), lambda h, i: (i, h)),
                   pl.BlockSpec((None, t, 2), lambda h, i: (h, i, 0))],
        out_shape=[jax.ShapeDtypeStruct((s, HEADS * V_HEAD), F32), jax.ShapeDtypeStruct((s, HEADS * QK_ROPE), BF16),
                   jax.ShapeDtypeStruct((hp, s, 2), F32)],
        compiler_params=_params("parallel", "parallel"))(qn, qp, kv, kpr, cos4, sin4)


def attn_bwd_q(qn, qpr, kv, kpr, o, do, lse, cos4, sin4):
    s = qn.shape[0]
    hp = HEADS // 2
    t = _tile(s, ATT_TILE)
    nq = s // t

    def body(qn_ref, qpr_ref, kv_ref, kp_ref, o_ref, do_ref, l_ref, c_ref, s_ref, dqn_ref, dqp_ref):
        qi = pl.program_id(1)
        qpr = qpr_ref[...]
        masks = _head_masks(qpr.shape)
        dqp = jnp.zeros(qpr.shape, F32)
        for hh in range(2):
            q_n = qn_ref[:, hh * QK_NOPE:(hh + 1) * QK_NOPE]
            q_p = jnp.where(masks[hh], qpr, jnp.zeros_like(qpr))
            kc, vc = 2 * hh * QK_NOPE, (2 * hh + 1) * QK_NOPE
            do_h = do_ref[:, hh * V_HEAD:(hh + 1) * V_HEAD]
            delta = jnp.sum(do_h * o_ref[:, hh * V_HEAD:(hh + 1) * V_HEAD], axis=-1, keepdims=True)
            do_b = do_h.astype(BF16)
            lse_h = l_ref[:, hh:hh + 1]

            def step(kb, carry):
                dn, dp_ = carry
                rows = pl.ds(pl.multiple_of(kb * t, t), t)
                k = kv_ref[rows, kc:kc + QK_NOPE]
                kp = kp_ref[rows, :]
                p = jnp.exp(_scores(q_n, q_p, k, kp, qi, kb, t) - lse_h)
                dpv = lax.dot_general(do_b, kv_ref[rows, vc:vc + V_HEAD], (((1,), (1,)), ((), ())), preferred_element_type=F32)
                ds = (p * (dpv - delta) * _ATT_SCALE).astype(BF16)
                dn = dn + jnp.dot(ds, k, preferred_element_type=F32)
                dp_ = dp_ + jnp.dot(ds, kp, preferred_element_type=F32)
                return dn, dp_

            dn, dp_h = lax.fori_loop(0, qi + 1, step, (jnp.zeros((t, QK_NOPE), F32), jnp.zeros((t, LANES), F32)))
            dqn_ref[:, hh * QK_NOPE:(hh + 1) * QK_NOPE] = dn.astype(dqn_ref.dtype)
            dqp = dqp + jnp.where(masks[hh], dp_h, jnp.zeros_like(dp_h))
        dqp_ref[...] = _rope(dqp, c_ref[...], -s_ref[...]).astype(dqp_ref.dtype)

    qblk = pl.BlockSpec((t, 2 * QK_NOPE), lambda h, i: (i, h))
    pblk = pl.BlockSpec((t, LANES), lambda h, i: (i, h))
    tab = pl.BlockSpec((t, LANES), lambda h, i: (i, 0))
    return pl.pallas_call(
        body, name="attn_bwd_q", grid=(hp, nq),
        in_specs=[qblk, pblk, pl.BlockSpec((s, 4 * QK_NOPE), lambda h, i: (0, h)), _full((s, LANES)), qblk, qblk,
                  pl.BlockSpec((None, t, 2), lambda h, i: (h, i, 0)), tab, tab],
        out_specs=[qblk, pblk],
        out_shape=[jax.ShapeDtypeStruct((s, HEADS * QK_NOPE), BF16), jax.ShapeDtypeStruct((s, HEADS * QK_ROPE), BF16)],
        compiler_params=_params("parallel", "parallel"))(qn, qpr, kv, kpr, o, do, lse, cos4, sin4)


def attn_bwd_kv(qn, qpr, kv, kpr, o, do, lse):
    s = qn.shape[0]
    hp = HEADS // 2
    t = _tile(s, ATT_TILE)
    nq = s // t

    def body(qn_ref, qpr_ref, kv_ref, kp_ref, o_ref, do_ref, l_ref, dkv_ref, dkp_ref):
        ki = pl.program_id(1)
        rows_k = pl.ds(pl.multiple_of(ki * t, t), t)
        kp = kp_ref[rows_k, :]
        dkp = jnp.zeros((t, LANES), F32)
        for hh in range(2):
            kc, vc = 2 * hh * QK_NOPE, (2 * hh + 1) * QK_NOPE
            k = kv_ref[rows_k, kc:kc + QK_NOPE]
            v = kv_ref[rows_k, vc:vc + V_HEAD]

            def step(qb, carry):
                dk, dv, dkp_h = carry
                rows = pl.ds(pl.multiple_of(qb * t, t), t)
                q_n = qn_ref[rows, hh * QK_NOPE:(hh + 1) * QK_NOPE]
                qpr = qpr_ref[rows, :]
                lane = lax.broadcasted_iota(jnp.int32, qpr.shape, 1)
                sel = (lane < QK_ROPE) if hh == 0 else (lane >= QK_ROPE)
                q_p = jnp.where(sel, qpr, jnp.zeros_like(qpr))
                do_h = do_ref[rows, hh * V_HEAD:(hh + 1) * V_HEAD]
                delta = jnp.sum(do_h * o_ref[rows, hh * V_HEAD:(hh + 1) * V_HEAD], axis=-1, keepdims=True)
                do_b = do_h.astype(BF16)
                p = jnp.exp(_scores(q_n, q_p, k, kp, qb, ki, t) - l_ref[rows, hh:hh + 1])
                dpv = lax.dot_general(do_b, v, (((1,), (1,)), ((), ())), preferred_element_type=F32)
                ds = (p * (dpv - delta) * _ATT_SCALE).astype(BF16)
                dv = dv + lax.dot_general(p.astype(BF16), do_b, (((0,), (0,)), ((), ())), preferred_element_type=F32)
                dk = dk + lax.dot_general(ds, q_n, (((0,), (0,)), ((), ())), preferred_element_type=F32)
                dkp_h = dkp_h + lax.dot_general(ds, q_p, (((0,), (0,)), ((), ())), preferred_element_type=F32)
                return dk, dv, dkp_h

            init = (jnp.zeros((t, QK_NOPE), F32), jnp.zeros((t, V_HEAD), F32), jnp.zeros((t, LANES), F32))
            dk, dv, dkp_h = lax.fori_loop(ki, nq, step, init)
            dkv_ref[:, kc:kc + QK_NOPE] = dk.astype(dkv_ref.dtype)
            dkv_ref[:, vc:vc + V_HEAD] = dv.astype(dkv_ref.dtype)
            dkp = dkp + dkp_h
        dkp_ref[...] = dkp

    return pl.pallas_call(
        body, name="attn_bwd_kv", grid=(hp, nq),
        in_specs=[pl.BlockSpec((s, 2 * QK_NOPE), lambda h, i: (0, h)), pl.BlockSpec((s, LANES), lambda h, i: (0, h)),
                  pl.BlockSpec((s, 4 * QK_NOPE), lambda h, i: (0, h)), _full((s, LANES)),
                  pl.BlockSpec((s, 2 * V_HEAD), lambda h, i: (0, h)), pl.BlockSpec((s, 2 * V_HEAD), lambda h, i: (0, h)),
                  pl.BlockSpec((None, s, 2), lambda h, i: (h, 0, 0))],
        out_specs=[pl.BlockSpec((t, 4 * QK_NOPE), lambda h, i: (i, h)), pl.BlockSpec((None, t, LANES), lambda h, i: (h, i, 0))],
        out_shape=[jax.ShapeDtypeStruct((s, HEADS * 2 * QK_NOPE), BF16), jax.ShapeDtypeStruct((hp, s, LANES), F32)],
        compiler_params=_params("parallel", "parallel"))(qn, qpr, kv, kpr, o, do, lse)


def _dot_nt(a, b):
    return lax.dot_general(a, b, (((1,), (1,)), ((), ())), preferred_element_type=F32)


def _dot_tn(a, b):
    return lax.dot_general(a, b, (((0,), (0,)), ((), ())), preferred_element_type=F32)


def _q_cat(q_n, qpr, hh):
    lane = lax.broadcasted_iota(jnp.int32, qpr.shape, 1)
    sel = (lane < QK_ROPE) if hh == 0 else (lane >= QK_ROPE)
    return jnp.concatenate([q_n, jnp.where(sel, qpr, jnp.zeros_like(qpr))], axis=1)


def _causal(sc):
    row = lax.broadcasted_iota(jnp.int32, sc.shape, 0)
    col = lax.broadcasted_iota(jnp.int32, sc.shape, 1)
    return jnp.where(col <= row, sc, _NEG)


def attn_fwd2(qn, qp, kv, kpr, cos4, sin4):
    s = qn.shape[0]
    hp = HEADS // 2
    t = _tile(s, ATT_TILE)
    nq = s // t

    def body(qn_ref, qp_ref, kv_ref, kp_ref, c_ref, s_ref, o_ref, qpr_ref, l_ref, kcat_ref):
        qi = pl.program_id(1)

        @pl.when(qi == 0)
        def _():
            for hh in range(2):
                kcat_ref[hh, :, 0:QK_NOPE] = kv_ref[:, 2 * hh * QK_NOPE:(2 * hh + 1) * QK_NOPE]
                kcat_ref[hh, :, QK_NOPE:] = kp_ref[...]

        qpr = _rope(qp_ref[...], c_ref[...], s_ref[...]).astype(BF16)
        qpr_ref[...] = qpr
        qcat = [_q_cat(qn_ref[:, hh * QK_NOPE:(hh + 1) * QK_NOPE], qpr, hh) for hh in range(2)]

        def block(kb, carry, diagonal):
            rows = pl.ds(pl.multiple_of(kb * t, t), t)
            out = []
            for hh in range(2):
                m, l, acc = carry[hh]
                sc = _dot_nt(qcat[hh], kcat_ref[hh, rows, :]) * _ATT_SCALE
                if diagonal:
                    sc = _causal(sc)
                m_new = jnp.maximum(m, jnp.max(sc, axis=-1, keepdims=True))
                alpha = jnp.exp(m - m_new)
                p = jnp.exp(sc - m_new)
                l = alpha * l + jnp.sum(p, axis=-1, keepdims=True)
                v = kv_ref[rows, (2 * hh + 1) * QK_NOPE:(2 * hh + 2) * QK_NOPE]
                acc = alpha * acc + jnp.dot(p.astype(BF16), v, preferred_element_type=F32)
                out.append((m_new, l, acc))
            return tuple(out)

        one = (jnp.full((t, 1), _NEG, F32), jnp.zeros((t, 1), F32), jnp.zeros((t, V_HEAD), F32))
        carry = lax.fori_loop(0, qi, lambda kb, cr: block(kb, cr, False), (one, one))
        carry = block(qi, carry, True)
        for hh in range(2):
            m, l, acc = carry[hh]
            o_ref[:, hh * V_HEAD:(hh + 1) * V_HEAD] = acc / l
            l_ref[:, hh:hh + 1] = m + jnp.log(l)

    return pl.pallas_call(
        body, name="attn_fwd", grid=(hp, nq),
        in_specs=[pl.BlockSpec((t, 2 * QK_NOPE), lambda h, i: (i, h)), pl.BlockSpec((t, LANES), lambda h, i: (i, h)),
                  pl.BlockSpec((s, 4 * QK_NOPE), lambda h, i: (0, h)), _full((s, LANES)),
                  pl.BlockSpec((t, LANES), lambda h, i: (i, 0)), pl.BlockSpec((t, LANES), lambda h, i: (i, 0))],
        out_specs=[pl.BlockSpec((t, 2 * V_HEAD), lambda h, i: (i, h)), pl.BlockSpec((t, LANES), lambda h, i: (i, h)),
                   pl.BlockSpec((None, t, 2), lambda h, i: (h, i, 0))],
        out_shape=[jax.ShapeDtypeStruct((s, HEADS * V_HEAD), F32), jax.ShapeDtypeStruct((s, HEADS * QK_ROPE), BF16),
                   jax.ShapeDtypeStruct((hp, s, 2), F32)],
        scratch_shapes=[pltpu.VMEM((2, s, 2 * QK_NOPE), BF16)],
        compiler_params=_params("parallel", "arbitrary"))(qn, qp, kv, kpr, cos4, sin4)


def attn_bwd2(qn, qpr, kv, kpr, o, do, lse, cos4, sin4):
    s = qn.shape[0]
    hp = HEADS // 2
    t = _tile(s, ATT_TILE)
    nk = s // t

    def body(qn_ref, qpr_ref, kv_ref, kp_ref, o_ref, do_ref, l_ref, c_ref, s_ref,
             dqn_ref, dqp_ref, dkv_ref, dkp_ref, qcat_ref, dq_ref, delta_ref):
        ki = pl.program_id(1)

        @pl.when(ki == 0)
        def _():
            dq_ref[...] = jnp.zeros_like(dq_ref)
            for hh in range(2):
                qcat_ref[hh] = _q_cat(qn_ref[:, hh * QK_NOPE:(hh + 1) * QK_NOPE], qpr_ref[...], hh)
                cols = slice(hh * V_HEAD, (hh + 1) * V_HEAD)
                delta_ref[hh] = jnp.sum(do_ref[:, cols] * o_ref[:, cols], axis=-1, keepdims=True)

        rows_k = pl.ds(pl.multiple_of(ki * t, t), t)
        kcat = [jnp.concatenate([kv_ref[rows_k, 2 * hh * QK_NOPE:(2 * hh + 1) * QK_NOPE], kp_ref[rows_k, :]], axis=1) for hh in range(2)]
        vs = [kv_ref[rows_k, (2 * hh + 1) * QK_NOPE:(2 * hh + 2) * QK_NOPE] for hh in range(2)]

        def block(qb, carry, diagonal):
            rows = pl.ds(pl.multiple_of(qb * t, t), t)
            out = []
            for hh in range(2):
                dkc, dv = carry[hh]
                q_c = qcat_ref[hh, rows, :]
                do_b = do_ref[rows, hh * V_HEAD:(hh + 1) * V_HEAD].astype(BF16)
                sc = _dot_nt(q_c, kcat[hh]) * _ATT_SCALE
                if diagonal:
                    sc = _causal(sc)
                p = jnp.exp(sc - l_ref[rows, hh:hh + 1])
                dpv = _dot_nt(do_b, vs[hh])
                ds = (p * (dpv - delta_ref[hh, rows, :]) * _ATT_SCALE).astype(BF16)
                dv = dv + _dot_tn(p.astype(BF16), do_b)
                dkc = dkc + _dot_tn(ds, q_c)
                dq_ref[hh, rows, :] += jnp.dot(ds, kcat[hh], preferred_element_type=F32)
                out.append((dkc, dv))
            return tuple(out)

        one = (jnp.zeros((t, 2 * QK_NOPE), F32), jnp.zeros((t, V_HEAD), F32))
        carry = block(ki, (one, one), True)
        carry = lax.fori_loop(ki + 1, nk, lambda qb, cr: block(qb, cr, False), carry)
        dkp = jnp.zeros((t, LANES), F32)
        for hh in range(2):
            dkc, dv = carry[hh]
            dkv_ref[:, 2 * hh * QK_NOPE:(2 * hh + 1) * QK_NOPE] = dkc[:, :QK_NOPE].astype(dkv_ref.dtype)
            dkv_ref[:, (2 * hh + 1) * QK_NOPE:(2 * hh + 2) * QK_NOPE] = dv.astype(dkv_ref.dtype)
            dkp = dkp + dkc[:, QK_NOPE:]
        dkp_ref[...] = dkp

        @pl.when(ki == nk - 1)
        def _():
            lane = lax.broadcasted_iota(jnp.int32, (s, LANES), 1)
            dqp = jnp.where(lane < QK_ROPE, dq_ref[0, :, QK_NOPE:], dq_ref[1, :, QK_NOPE:])
            dqp_ref[...] = _rope(dqp, c_ref[...], -s_ref[...]).astype(dqp_ref.dtype)
            for hh in range(2):
                dqn_ref[:, hh * QK_NOPE:(hh + 1) * QK_NOPE] = dq_ref[hh, :, :QK_NOPE].astype(dqn_ref.dtype)

    qblk = pl.BlockSpec((s, 2 * QK_NOPE), lambda h, i: (0, h))
    pblk = pl.BlockSpec((s, LANES), lambda h, i: (0, h))
    tab = _full((s, LANES))
    return pl.pallas_call(
        body, name="attn_bwd", grid=(hp, nk),
        in_specs=[qblk, pblk, pl.BlockSpec((s, 4 * QK_NOPE), lambda h, i: (0, h)), tab, qblk, qblk,
                  pl.BlockSpec((None, s, 2), lambda h, i: (h, 0, 0)), tab, tab],
        out_specs=[qblk, pblk, pl.BlockSpec((t, 4 * QK_NOPE), lambda h, i: (i, h)), pl.BlockSpec((None, t, LANES), lambda h, i: (h, i, 0))],
        out_shape=[jax.ShapeDtypeStruct((s, HEADS * QK_NOPE), BF16), jax.ShapeDtypeStruct((s, HEADS * QK_ROPE), BF16),
                   jax.ShapeDtypeStruct((s, HEADS * 2 * QK_NOPE), BF16), jax.ShapeDtypeStruct((hp, s, LANES), F32)],
        scratch_shapes=[pltpu.VMEM((2, s, 2 * QK_NOPE), BF16), pltpu.VMEM((2, s, 2 * QK_NOPE), F32), pltpu.VMEM((2, s, 1), F32)],
        compiler_params=_params("parallel", "arbitrary"))(qn, qpr, kv, kpr, o, do, lse, cos4, sin4)


def kpe_bwd(dkp, cos4, sin4, pad_cols):
    hp, s, _ = dkp.shape
    tr = _tile(s, ROW_TILE * 2)

    def body(d_ref, c_ref, s_ref, o_ref):
        tot = d_ref[0]
        for h in range(1, hp):
            tot = tot + d_ref[h]
        tot = tot + pltpu.roll(tot, QK_ROPE, 1)
        lane = lax.broadcasted_iota(jnp.int32, tot.shape, 1)
        dk = jnp.where(lane < QK_ROPE, _rope(tot, c_ref[...], -s_ref[...]), jnp.zeros_like(tot))
        o_ref[...] = jnp.zeros_like(o_ref)
        o_ref[:, 0:LANES] = dk.astype(o_ref.dtype)

    row = pl.BlockSpec((tr, LANES), lambda i: (i, 0))
    return pl.pallas_call(body, name="kpe_bwd", grid=(s // tr,),
                          in_specs=[pl.BlockSpec((hp, tr, LANES), lambda i: (0, i, 0)), row, row],
                          out_specs=pl.BlockSpec((tr, pad_cols), lambda i: (i, 0)),
                          out_shape=jax.ShapeDtypeStruct((s, pad_cols), BF16), compiler_params=_params("parallel"))(dkp, cos4, sin4)


def _shift_down(x, n):
    row = lax.broadcasted_iota(jnp.int32, x.shape, 0)
    return jnp.where(row >= n, pltpu.roll(x, n, 0), jnp.zeros_like(x))


def _shift_up(x, n):
    rows = x.shape[0]
    row = lax.broadcasted_iota(jnp.int32, x.shape, 0)
    return jnp.where(row < rows - n, pltpu.roll(x, rows - n, 0), jnp.zeros_like(x))


def _conv(x, w_ref, b_ref):
    return w_ref[2:3, :] * x + w_ref[1:2, :] * _shift_down(x, 1) + w_ref[0:1, :] * _shift_down(x, 2) + b_ref[...]


def conv_act_fwd(upre, conv_w, conv_b):
    s, f2 = upre.shape
    f = f2 // 2
    tc = _tile(f, COL_TILE)
    nc = f // tc

    def body(ug_ref, uv_ref, wg_ref, wv_ref, bg_ref, bv_ref, o_ref):
        gh = _conv(ug_ref[...], wg_ref, bg_ref)
        vh = _conv(uv_ref[...], wv_ref, bv_ref)
        o_ref[...] = (gh * _sigmoid(gh) * vh).astype(o_ref.dtype)

    def spec(rows, shift):
        return pl.BlockSpec((rows, tc), lambda j: (0, j + shift))

    return pl.pallas_call(
        body, name="conv_act_fwd", grid=(nc,),
        in_specs=[spec(s, 0), spec(s, nc), spec(3, 0), spec(3, nc), spec(1, 0), spec(1, nc)], out_specs=spec(s, 0),
        out_shape=jax.ShapeDtypeStruct((s, f), BF16), compiler_params=_params("parallel"))(upre, upre, conv_w, conv_w, conv_b, conv_b)


def conv_act_bwd(upre, conv_w, conv_b, df):
    s, f2 = upre.shape
    f = f2 // 2
    tc = _tile(f, COL_TILE)
    nc = f // tc

    def half(x, d, w_ref, du_ref, which, gw_ref, gb_ref):
        gb_ref[...] = _colsum(d)
        gw_ref[2:3, :] = _colsum(d * x)
        gw_ref[1:2, :] = _colsum(d * _shift_down(x, 1))
        gw_ref[0:1, :] = _colsum(d * _shift_down(x, 2))
        du_ref[which] = (w_ref[2:3, :] * d + w_ref[1:2, :] * _shift_up(d, 1) + w_ref[0:1, :] * _shift_up(d, 2)).astype(du_ref.dtype)

    def body(ug_ref, uv_ref, wg_ref, wv_ref, bg_ref, bv_ref, df_ref, du_ref, gwg_ref, gwv_ref, gbg_ref, gbv_ref):
        xg, xv = ug_ref[...], uv_ref[...]
        gh = _conv(xg, wg_ref, bg_ref)
        vh = _conv(xv, wv_ref, bv_ref)
        sg = _sigmoid(gh)
        df_v = df_ref[...]
        half(xg, df_v * vh * (sg * (1.0 + gh * (1.0 - sg))), wg_ref, du_ref, 0, gwg_ref, gbg_ref)
        half(xv, df_v * (gh * sg), wv_ref, du_ref, 1, gwv_ref, gbv_ref)

    def spec(rows, shift):
        return pl.BlockSpec((rows, tc), lambda j: (0, j + shift))

    gw = jax.ShapeDtypeStruct((3, f), F32)
    gb = jax.ShapeDtypeStruct((1, f), F32)
    return pl.pallas_call(
        body, name="conv_act_bwd", grid=(nc,),
        in_specs=[spec(s, 0), spec(s, nc), spec(3, 0), spec(3, nc), spec(1, 0), spec(1, nc), spec(s, 0)],
        out_specs=[pl.BlockSpec((2, s, tc), lambda j: (0, 0, j)), spec(3, 0), spec(3, 0), spec(1, 0), spec(1, 0)],
        out_shape=[jax.ShapeDtypeStruct((2, s, f), BF16), gw, gw, gb, gb],
        compiler_params=_params("parallel"))(upre, upre, conv_w, conv_w, conv_b, conv_b, df)


def _elementwise_tile(r, c, limit):
    if r % 8:
        return r, c
    best = (8, c if c % LANES else LANES)
    for k in (1, 2, 4, 8, 16):
        if k > 1 and c % (LANES * k):
            continue
        tc = c // k
        tr = max(8, min(r, limit // tc) // 8 * 8)
        while r % tr:
            tr -= 8
        if tr * tc <= max(limit, 8 * tc) and tr * tc > best[0] * best[1]:
            best = (tr, tc)
    return best


def adamw(name, w, m, v, parts):
    npart, r, c = parts.shape
    tr, tc = _elementwise_tile(r, c, ADAMW_TILE_ELEMS)
    bc1 = 1.0 - ADAM_B1 ** ADAM_STEP
    bc2 = 1.0 - ADAM_B2 ** ADAM_STEP

    def body(w_ref, m_ref, v_ref, p_ref, g_ref, d_ref, nm_ref, nv_ref):
        g = p_ref[0].astype(F32)
        for k in range(1, npart):
            g = g + p_ref[k].astype(F32)
        m_new = ADAM_B1 * m_ref[...] + (1.0 - ADAM_B1) * g
        v_new = ADAM_B2 * v_ref[...] + (1.0 - ADAM_B2) * (g * g)
        g_ref[...] = g
        nm_ref[...] = m_new
        nv_ref[...] = v_new
        d_ref[...] = -ADAM_LR * ((m_new / bc1) / (jnp.sqrt(v_new / bc2) + ADAM_EPS) + ADAM_WD * w_ref[...])

    deps = _TOKENS.take()
    blk = pl.BlockSpec((tr, tc), lambda i, j: (i, j))
    out = jax.ShapeDtypeStruct((r, c), F32)
    return pl.pallas_call(
        lambda *refs: body(*refs[:4], *refs[4 + len(deps):]), name=name, grid=(r // tr, c // tc),
        in_specs=[blk, blk, blk, pl.BlockSpec((npart, tr, tc), lambda i, j: (0, i, j))] + [pl.BlockSpec(memory_space=pl.ANY)] * len(deps),
        out_specs=[blk, blk, blk, blk], out_shape=[out, out, out, out],
        compiler_params=_params("parallel", "parallel"))(w, m, v, parts, *deps)


def _position():
    return lax.axis_index("x"), lax.axis_index("y"), lax.axis_index("c")


def _index(p):
    return 4 * p[0] + 2 * p[1] + p[2]


def _peer(me, r):
    return (me[0] ^ ((r >> 2) & 1), me[1] ^ ((r >> 1) & 1), me[2] ^ (r & 1))


_ANY = pl.BlockSpec(memory_space=pl.ANY)


def all_gather_two_level(shards):
    n = len(shards)

    def body(*refs):
        ins, outs = refs[:n], refs[n:2 * n]
        send_sems, recv_sems, local_sems = refs[2 * n:]
        x, y, c = _position()
        me, sibling = (x, y, c), (x, y, 1 - c)
        chips = [(1 - x, y), (x, 1 - y), (1 - x, 1 - y)]

        def copy(w, k, block, to, src=None):
            slot = outs[w].at[_index(block)]
            return pltpu.make_async_remote_copy(src_ref=slot if src is None else src, dst_ref=slot,
                                                send_sem=send_sems.at[7 * w + k], recv_sem=recv_sems.at[7 * w + k],
                                                device_id=to, device_id_type=MESH)

        mine = [pltpu.make_async_copy(ins[w], outs[w].at[_index(me)], local_sems.at[w]) for w in range(n)]
        for cp in mine:
            cp.start()
        first = []
        for w in range(n):
            first.append(copy(w, 0, me, sibling, src=ins[w]))
            first += [copy(w, 1 + j, me, (*chip, c), src=ins[w]) for j, chip in enumerate(chips)]
        for cp in first:
            cp.start()
        passed = []
        for w in range(n):
            for j, chip in enumerate(chips):
                copy(w, 1 + j, (*chip, c), me).wait_recv()
                cp = copy(w, 4 + j, (*chip, c), sibling)
                cp.start()
                passed.append(cp)
        for w in range(n):
            copy(w, 0, sibling, me).wait_recv()
            for j, chip in enumerate(chips):
                copy(w, 4 + j, (*chip, 1 - c), me).wait_recv()
        for cp in first + passed:
            cp.wait_send()
        for cp in mine:
            cp.wait()

    return pl.pallas_call(
        body, name="all_gather_weights",
        out_shape=[jax.ShapeDtypeStruct((N_DEV,) + a.shape, a.dtype) for a in shards],
        in_specs=[_ANY] * n, out_specs=[_ANY] * n,
        scratch_shapes=[pltpu.SemaphoreType.DMA((7 * n,)), pltpu.SemaphoreType.DMA((7 * n,)), pltpu.SemaphoreType.DMA((n,))],
        )(*shards)


def exchange(name, arrays, scatter):
    n = len(arrays)

    def body(*refs):
        ins, outs = refs[:n], refs[n:2 * n]
        send_sems, recv_sems, local_sems = refs[2 * n:]
        me = _position()
        copies = []
        for w in range(n):
            src = ins[w].at[_index(me)] if scatter else ins[w]
            cp = pltpu.make_async_copy(src, outs[w].at[_index(me)], local_sems.at[w])
            cp.start()
            copies.append(cp)
        remote = []
        for w in range(n):
            for r in range(1, N_DEV):
                peer = _peer(me, r)
                src = ins[w].at[_index(peer)] if scatter else ins[w]
                cp = pltpu.make_async_remote_copy(src_ref=src, dst_ref=outs[w].at[_index(me)],
                                                  send_sem=send_sems.at[7 * w + r - 1], recv_sem=recv_sems.at[7 * w + r - 1],
                                                  device_id=peer, device_id_type=MESH)
                cp.start()
                remote.append(cp)
        for cp in remote:
            cp.wait()
        for cp in copies:
            cp.wait()

    blocks = [a.shape[1:] if scatter else a.shape for a in arrays]
    return pl.pallas_call(
        body, name=name,
        out_shape=[jax.ShapeDtypeStruct((N_DEV,) + b, a.dtype) for a, b in zip(arrays, blocks)],
        in_specs=[_ANY] * n, out_specs=[_ANY] * n,
        scratch_shapes=[pltpu.SemaphoreType.DMA((7 * n,)), pltpu.SemaphoreType.DMA((7 * n,)), pltpu.SemaphoreType.DMA((n,))],
        )(*arrays)


_HBM = pl.BlockSpec(memory_space=pltpu.HBM)
_SEM = pl.BlockSpec(memory_space=pltpu.SEMAPHORE)
_EFFECT = pltpu.SideEffectType.DATAFLOW_SIDE_EFFECTING


def _direct_copies(ins, lands, send_sems, recv_sems, scatter):
    me = _position()
    copies = []
    for w in range(len(ins)):
        for r in range(1, N_DEV):
            peer = _peer(me, r)
            src = ins[w].at[_index(peer)] if scatter else ins[w]
            copies.append(pltpu.make_async_remote_copy(src_ref=src, dst_ref=lands[w].at[_index(me)], send_sem=send_sems.at[7 * w + r - 1],
                                                       recv_sem=recv_sems.at[7 * w + r - 1], device_id=peer, device_id_type=MESH))
    return copies


def exchange_start(name, groups, scatter):
    arrays = [a for g in groups for a in g]
    n = len(arrays)
    blocks = [a.shape[1:] if scatter else a.shape for a in arrays]
    lands = [lax.empty((N_DEV,) + b, a.dtype) for a, b in zip(arrays, blocks)]
    ng = len(groups)

    def body(*refs):
        ins, lnd = refs[:n], refs[n:2 * n]
        sems = refs[2 * n:2 * n + 2 * ng]
        token = refs[2 * n + 2 * ng + 2 * n]
        local_sem = refs[2 * n + 2 * ng + 2 * n + 1]
        me = _position()
        local = []
        for w in range(n):
            src = ins[w].at[_index(me)] if scatter else ins[w]
            cp = pltpu.make_async_copy(src, lnd[w].at[_index(me)], local_sem.at[w])
            cp.start()
            local.append(cp)
        w0 = 0
        for gi, g in enumerate(groups):
            for cp in _direct_copies(ins[w0:w0 + len(g)], lnd[w0:w0 + len(g)], sems[2 * gi], sems[2 * gi + 1], scatter):
                cp.start()
            w0 += len(g)
        for cp in local:
            cp.wait()
        token[...] = jnp.zeros_like(token)

    sem_shapes = []
    for g in groups:
        sem_shapes += [pltpu.SemaphoreType.DMA((7 * len(g),)), pltpu.SemaphoreType.DMA((7 * len(g),))]
    out = pl.pallas_call(
        body, name=name,
        out_shape=tuple(sem_shapes) + tuple(pltpu.HBM(a.shape, a.dtype) for a in arrays) + tuple(pltpu.HBM(l.shape, l.dtype) for l in lands)
        + (jax.ShapeDtypeStruct((8, LANES), F32),),
        in_specs=[_HBM] * (2 * n), out_specs=tuple([_SEM] * (2 * ng) + [_HBM] * (2 * n) + [pl.BlockSpec(memory_space=pltpu.VMEM)]),
        input_output_aliases={i: 2 * ng + i for i in range(2 * n)},
        scratch_shapes=[pltpu.SemaphoreType.DMA((n,))],
        compiler_params=pltpu.CompilerParams(has_side_effects=_EFFECT),
    )(*[pltpu.with_memory_space_constraint(a, pltpu.HBM) for a in arrays], *[pltpu.with_memory_space_constraint(l, pltpu.HBM) for l in lands])
    sems, thru, token = out[:2 * ng], out[2 * ng:2 * ng + 2 * n], out[-1]
    res, w0 = [], 0
    for gi, g in enumerate(groups):
        res.append((sems[2 * gi], sems[2 * gi + 1], list(thru[w0:w0 + len(g)]), list(thru[n + w0:n + w0 + len(g)])))
        w0 += len(g)
    return res, token


def exchange_wait(name, group, after, scatter):
    send_sems, recv_sems, srcs, lands = group
    n = len(srcs)

    def body(*refs):
        ins, lnd = refs[:n], refs[n:2 * n]
        for cp in _direct_copies(ins, lnd, refs[2 * n], refs[2 * n + 1], scatter):
            cp.wait_send()
            cp.wait_recv()

    out = pl.pallas_call(
        body, name=name, out_shape=tuple(pltpu.HBM(a.shape, a.dtype) for a in srcs + lands),
        in_specs=[_HBM] * (2 * n) + [_SEM, _SEM, pl.BlockSpec(memory_space=pl.ANY)], out_specs=tuple([_HBM] * (2 * n)),
        input_output_aliases={i: i for i in range(2 * n)},
        compiler_params=pltpu.CompilerParams(has_side_effects=_EFFECT),
    )(*srcs, *lands, send_sems, recv_sems, after)
    return list(out[n:])


def _after(x, token):
    return lax.optimization_barrier((x, token))[0]


_TOKEN = jax.ShapeDtypeStruct((8, LANES), F32)
_VM = pl.BlockSpec(memory_space=pltpu.VMEM)
_SIDE = pltpu.CompilerParams(has_side_effects=_EFFECT)


def _hbm(a):
    return pltpu.with_memory_space_constraint(a, pltpu.HBM)


def _like(a):
    return pltpu.HBM(a.shape, a.dtype)


def _dma_sems(n):
    return pltpu.SemaphoreType.DMA((n,))


def _other_chips(x, y):
    return [(1 - x, y), (x, 1 - y), (1 - x, 1 - y)]


COPY_STREAMS = 8


def _row_chunks(src, dst):
    rows = src.shape[0]
    n = COPY_STREAMS
    while n > 1 and rows % (16 * n):
        n //= 2
    r = rows // n
    return [(src.at[pl.ds(i * r, r)], dst.at[pl.ds(i * r, r)]) for i in range(n)]


def _local_copy(src, dst, sem):
    return [pltpu.make_async_copy(s, d, sem) for s, d in _row_chunks(src, dst)]


class _rcopy:
    def __init__(self, src, dst, send_sem, recv_sem, to):
        self.parts = [pltpu.make_async_remote_copy(src_ref=s, dst_ref=d, send_sem=send_sem, recv_sem=recv_sem, device_id=to, device_id_type=MESH)
                      for s, d in _row_chunks(src, dst)]

    def start(self):
        for cp in self.parts:
            cp.start()

    def wait_send(self):
        for cp in self.parts:
            cp.wait_send()

    def wait_recv(self):
        for cp in self.parts:
            cp.wait_recv()


def ag_start(name, shards, after):
    n = len(shards)
    lands = [lax.empty((N_DEV,) + a.shape, a.dtype) for a in shards]

    def body(*refs):
        ins, lnd, send_sems, recv_sems, token = refs[:n], refs[n:2 * n], refs[2 * n + 1], refs[2 * n + 2], refs[4 * n + 3]
        x, y, c = _position()
        for w in range(n):
            slot = lnd[w].at[_index((x, y, c))]
            for k, to in enumerate([(x, y, 1 - c)] + [(*chip, c) for chip in _other_chips(x, y)]):
                _rcopy(ins[w], slot, send_sems.at[4 * w + k], recv_sems.at[4 * w + k], to).start()
        token[...] = jnp.zeros_like(token)

    out = pl.pallas_call(
        body, name=name, out_shape=(_dma_sems(4 * n), _dma_sems(4 * n)) + tuple(_like(a) for a in shards + lands) + (_TOKEN,),
        in_specs=[_HBM] * (2 * n) + [_ANY], out_specs=(_SEM, _SEM) + (_HBM,) * (2 * n) + (_VM,),
        input_output_aliases={i: 2 + i for i in range(2 * n)}, compiler_params=_SIDE)(*[_hbm(a) for a in shards + lands], after)
    _TOKENS.push(out[-1])
    return out[0], out[1], list(out[2:2 + n]), list(out[2 + n:2 + 2 * n])


def ag_forward(name, started, after):
    send, recv, shards, lands = started
    n = len(shards)
    afters = list(after) if isinstance(after, (list, tuple)) else [after]
    na = len(afters)

    def body(*refs):
        ins, lnd, send_sems, recv_sems = refs[:n], refs[n:2 * n], refs[2 * n], refs[2 * n + 1]
        fsend, frecv, token = refs[2 * n + 2 + na], refs[2 * n + 3 + na], refs[4 * n + 4 + na]
        x, y, c = _position()
        for w in range(n):
            for j, chip in enumerate(_other_chips(x, y)):
                slot = lnd[w].at[_index((*chip, c))]
                _rcopy(ins[w], slot, send_sems.at[4 * w + 1 + j], recv_sems.at[4 * w + 1 + j], (*chip, c)).wait_recv()
                _rcopy(slot, slot, fsend.at[3 * w + j], frecv.at[3 * w + j], (x, y, 1 - c)).start()
        token[...] = jnp.zeros_like(token)

    out = pl.pallas_call(
        body, name=name, out_shape=(_dma_sems(3 * n), _dma_sems(3 * n)) + tuple(_like(a) for a in shards + lands) + (_TOKEN,),
        in_specs=[_HBM] * (2 * n) + [_SEM, _SEM] + [_ANY] * na, out_specs=(_SEM, _SEM) + (_HBM,) * (2 * n) + (_VM,),
        input_output_aliases={i: 2 + i for i in range(2 * n)}, compiler_params=_SIDE)(*shards, *lands, send, recv, *afters)
    _TOKENS.push(out[-1])
    return send, recv, out[0], out[1], list(out[2:2 + n]), list(out[2 + n:2 + 2 * n])


def ag_wait(name, forwarded, after):
    send, recv, fsend, frecv, shards, lands = forwarded
    n = len(shards)

    def body(*refs):
        ins, lnd, send_sems, recv_sems, fsend_r, frecv_r = refs[:n], refs[n:2 * n], refs[2 * n], refs[2 * n + 1], refs[2 * n + 2], refs[2 * n + 3]
        x, y, c = _position()
        sibling = (x, y, 1 - c)
        for w in range(n):
            own = lnd[w].at[_index((x, y, c))]
            _rcopy(ins[w], lnd[w].at[_index(sibling)], send_sems.at[4 * w], recv_sems.at[4 * w], sibling).wait_recv()
            for j, chip in enumerate(_other_chips(x, y)):
                _rcopy(ins[w], lnd[w].at[_index((*chip, 1 - c))], fsend_r.at[3 * w + j], frecv_r.at[3 * w + j], sibling).wait_recv()
            for k in range(4):
                _rcopy(ins[w], own, send_sems.at[4 * w + k], recv_sems.at[4 * w + k], sibling).wait_send()
            for j in range(3):
                _rcopy(ins[w], own, fsend_r.at[3 * w + j], frecv_r.at[3 * w + j], sibling).wait_send()

    out = pl.pallas_call(
        body, name=name, out_shape=tuple(_like(a) for a in shards + lands), in_specs=[_HBM] * (2 * n) + [_SEM] * 4 + [_ANY],
        out_specs=(_HBM,) * (2 * n), input_output_aliases={i: i for i in range(2 * n)},
        compiler_params=_SIDE)(*shards, *lands, send, recv, fsend, frecv, after)
    return [lax.dynamic_update_index_in_dim(land, shard, _index(_position()), 0) for shard, land in zip(out[:n], out[n:])]


def rs_d2d_start(name, grads):
    n = len(grads)
    lands = [lax.empty((4,) + g.shape[1:], g.dtype) for g in grads]

    def body(*refs):
        ins, lnd, send_sems, recv_sems, token = refs[:n], refs[n:2 * n], refs[2 * n], refs[2 * n + 1], refs[4 * n + 2]
        x, y, c = _position()
        for w in range(n):
            for i in range(4):
                _rcopy(ins[w].at[2 * i + 1 - c], lnd[w].at[i], send_sems.at[4 * w + i], recv_sems.at[4 * w + i], (x, y, 1 - c)).start()
        token[...] = jnp.zeros_like(token)

    out = pl.pallas_call(
        body, name=name, out_shape=(_dma_sems(4 * n), _dma_sems(4 * n)) + tuple(_like(a) for a in grads + lands) + (_TOKEN,),
        in_specs=[_HBM] * (2 * n), out_specs=(_SEM, _SEM) + (_HBM,) * (2 * n) + (_VM,),
        input_output_aliases={i: 2 + i for i in range(2 * n)}, compiler_params=_SIDE)(*[_hbm(a) for a in grads + lands])
    _TOKENS.push(out[-1])
    return out[0], out[1], list(out[2:2 + n]), list(out[2 + n:2 + 2 * n])


def rs_d2d_wait(name, started, after):
    send, recv, grads, lands = started
    n = len(grads)

    def body(*refs):
        ins, lnd, send_sems, recv_sems = refs[:n], refs[n:2 * n], refs[2 * n], refs[2 * n + 1]
        x, y, c = _position()
        for w in range(n):
            for i in range(4):
                cp = _rcopy(ins[w].at[2 * i + 1 - c], lnd[w].at[i], send_sems.at[4 * w + i], recv_sems.at[4 * w + i], (x, y, 1 - c))
                cp.wait_send()
                cp.wait_recv()

    out = pl.pallas_call(
        body, name=name, out_shape=tuple(_like(a) for a in grads + lands), in_specs=[_HBM] * (2 * n) + [_SEM, _SEM, _ANY],
        out_specs=(_HBM,) * (2 * n), input_output_aliases={i: i for i in range(2 * n)}, compiler_params=_SIDE)(*grads, *lands, send, recv, after)
    return list(out[:n]), list(out[n:])


def pair_sum(name, grad, land, core):
    _, r, c = grad.shape
    tr = r
    if r % 8 == 0:
        tr = max(8, min(r, 4 * ADAMW_TILE_ELEMS // c) // 8 * 8)
        while r % tr:
            tr -= 8

    def body(core_ref, a_ref, b_ref, o_ref):
        o_ref[...] = (a_ref[...].astype(F32) + b_ref[...].astype(F32)).astype(o_ref.dtype)

    return pl.pallas_call(
        body, name=name, out_shape=jax.ShapeDtypeStruct((4, r, c), grad.dtype),
        grid_spec=pltpu.PrefetchScalarGridSpec(
            num_scalar_prefetch=1, grid=(4, r // tr),
            in_specs=[pl.BlockSpec((None, None, tr, c), lambda i, j, core_ref: (i, core_ref[0], j, 0)),
                      pl.BlockSpec((None, tr, c), lambda i, j, core_ref: (i, j, 0))],
            out_specs=pl.BlockSpec((None, tr, c), lambda i, j, core_ref: (i, j, 0))),
        compiler_params=_params("parallel", "parallel"))(core, grad.reshape(4, 2, r, c), land)


def rs_ici_start(name, sums):
    n = len(sums)
    lands = [lax.empty(a.shape, a.dtype) for a in sums]

    def body(*refs):
        ins, lnd, send_sems, recv_sems, token = refs[:n], refs[n:2 * n], refs[2 * n], refs[2 * n + 1], refs[4 * n + 2]
        x, y, c = _position()
        chip = 2 * x + y
        for w in range(n):
            for j, other in enumerate(_other_chips(x, y)):
                _rcopy(ins[w].at[2 * other[0] + other[1]], lnd[w].at[chip], send_sems.at[3 * w + j], recv_sems.at[3 * w + j], (*other, c)).start()
        token[...] = jnp.zeros_like(token)

    out = pl.pallas_call(
        body, name=name, out_shape=(_dma_sems(3 * n), _dma_sems(3 * n)) + tuple(_like(a) for a in sums + lands) + (_TOKEN,),
        in_specs=[_HBM] * (2 * n), out_specs=(_SEM, _SEM) + (_HBM,) * (2 * n) + (_VM,),
        input_output_aliases={i: 2 + i for i in range(2 * n)}, compiler_params=_SIDE)(*[_hbm(a) for a in sums + lands])
    _TOKENS.push(out[-1])
    return out[0], out[1], list(out[2:2 + n]), list(out[2 + n:2 + 2 * n])


def rs_ici_wait(name, started, after):
    send, recv, sums, lands = started
    n = len(sums)

    def body(*refs):
        ins, lnd, send_sems, recv_sems = refs[:n], refs[n:2 * n], refs[2 * n], refs[2 * n + 1]
        x, y, c = _position()
        for w in range(n):
            for j, other in enumerate(_other_chips(x, y)):
                cp = _rcopy(ins[w].at[2 * other[0] + other[1]], lnd[w].at[2 * other[0] + other[1]], send_sems.at[3 * w + j], recv_sems.at[3 * w + j], (*other, c))
                cp.wait_send()
                cp.wait_recv()

    out = pl.pallas_call(
        body, name=name, out_shape=tuple(_like(a) for a in sums + lands), in_specs=[_HBM] * (2 * n) + [_SEM, _SEM, _ANY],
        out_specs=(_HBM,) * (2 * n), input_output_aliases={i: i for i in range(2 * n)}, compiler_params=_SIDE)(*sums, *lands, send, recv, after)
    chip = 2 * lax.axis_index("x") + lax.axis_index("y")
    return [lax.dynamic_update_index_in_dim(land, lax.dynamic_index_in_dim(s, chip, 0, keepdims=False), chip, 0)
            for s, land in zip(out[:n], out[n:])]


def ada_fwd(c, w_ada, b_ada3, conv_w):
    d, cs = w_ada.shape

    def body(c_ref, w_ref, b_ref, cw_ref, mod_ref, sc_ref, cwa_ref, part_ref, send_sems, recv_sems):
        me = _position()
        my = _index(me)
        cv = c_ref[...]
        sc_ref[my] = cv * _sigmoid(cv)
        cwa_ref[my] = cw_ref[...]
        gather = []
        for r in range(1, N_DEV):
            for k, ref in enumerate((sc_ref, cwa_ref)):
                cp = pltpu.make_async_remote_copy(src_ref=ref.at[my], dst_ref=ref.at[my], send_sem=send_sems.at[14 * k + r - 1],
                                                  recv_sem=recv_sems.at[14 * k + r - 1], device_id=_peer(me, r), device_id_type=MESH)
                cp.start()
                gather.append(cp)
        for cp in gather:
            cp.wait()
        sc_all = jnp.concatenate([sc_ref[k] for k in range(N_DEV)], axis=0).astype(BF16)
        part = jnp.dot(sc_all, w_ref[...].astype(BF16), preferred_element_type=F32)
        for k in range(N_DEV):
            part_ref[k] = part[k:k + 1, :]
        scatter = []
        for r in range(1, N_DEV):
            peer = _peer(me, r)
            cp = pltpu.make_async_remote_copy(src_ref=part_ref.at[_index(peer)], dst_ref=mod_ref.at[my], send_sem=send_sems.at[6 + r],
                                              recv_sem=recv_sems.at[6 + r], device_id=peer, device_id_type=MESH)
            cp.start()
            scatter.append(cp)
        mod_ref[my] = part_ref[my]
        for cp in scatter:
            cp.wait()
        mod_ref[...] = mod_ref[...] + b_ref[...]

    vm = pl.BlockSpec(memory_space=pltpu.VMEM)
    return pl.pallas_call(
        body, name="ada_fwd",
        out_shape=[jax.ShapeDtypeStruct((N_DEV, 1, cs), F32), jax.ShapeDtypeStruct((N_DEV, 1, d), F32),
                   jax.ShapeDtypeStruct((N_DEV,) + conv_w.shape, F32)],
        in_specs=[vm, vm, vm, vm], out_specs=[vm, vm, vm],
        scratch_shapes=[pltpu.VMEM((N_DEV, 1, cs), F32), pltpu.SemaphoreType.DMA((21,)), pltpu.SemaphoreType.DMA((21,))],
        compiler_params=pltpu.CompilerParams(vmem_limit_bytes=VMEM_LIMIT_BYTES))(c, w_ada, b_ada3, conv_w)


def ada_bwd_w(sc_all, dmod_cols):
    _, d = sc_all.shape
    cs = dmod_cols.shape[1]
    tr = _tile(d, ROW_TILE)

    def body(sc_ref, dm_ref, o_ref):
        dm = dm_ref[...].astype(BF16)
        o_ref[...] = lax.dot_general(sc_ref[...].astype(BF16), dm, (((0,), (0,)), ((), ())), preferred_element_type=F32)

    return pl.pallas_call(body, name="ada_bwd_w", grid=(d // tr,),
                          in_specs=[pl.BlockSpec((N_DEV, tr), lambda i: (0, i)), _full((N_DEV, cs))],
                          out_specs=pl.BlockSpec((None, tr, cs), lambda i: (0, i, 0)),
                          out_shape=jax.ShapeDtypeStruct((1, d, cs), F32), compiler_params=_params("parallel"))(sc_all, dmod_cols)


def _round_up(n, m):
    return (n + m - 1) // m * m


def kernel(x, c, positions, w_ada, b_ada, pre_norm1_g, w_in, gm_ln_g, gm_ln_b, gm_w_s, gm_b_s, w_branch_a, q_norm_g, w_uq, kv_norm_g, w_ukv, w_branch_b, w_out, post_norm1_g, pre_norm2_g, w_up, conv_w, conv_b, w_down, post_norm2_g, loss_target, m_w_ada, m_b_ada, m_pre_norm1_g, m_w_in, m_gm_ln_g, m_gm_ln_b, m_gm_w_s, m_gm_b_s, m_w_branch_a, m_q_norm_g, m_w_uq, m_kv_norm_g, m_w_ukv, m_w_branch_b, m_w_out, m_post_norm1_g, m_pre_norm2_g, m_w_up, m_conv_w, m_conv_b, m_w_down, m_post_norm2_g, v_w_ada, v_b_ada, v_pre_norm1_g, v_w_in, v_gm_ln_g, v_gm_ln_b, v_gm_w_s, v_gm_b_s, v_w_branch_a, v_q_norm_g, v_w_uq, v_kv_norm_g, v_w_ukv, v_w_branch_b, v_w_out, v_post_norm1_g, v_pre_norm2_g, v_w_up, v_conv_w, v_conv_b, v_w_down, v_post_norm2_g):
    weights = dict(w_ada=w_ada, b_ada=b_ada, pre_norm1_g=pre_norm1_g, w_in=w_in, gm_ln_g=gm_ln_g, gm_ln_b=gm_ln_b, gm_w_s=gm_w_s,
                   gm_b_s=gm_b_s, w_branch_a=w_branch_a, q_norm_g=q_norm_g, w_uq=w_uq, kv_norm_g=kv_norm_g, w_ukv=w_ukv,
                   w_branch_b=w_branch_b, w_out=w_out, post_norm1_g=post_norm1_g, pre_norm2_g=pre_norm2_g, w_up=w_up, conv_w=conv_w,
                   conv_b=conv_b, w_down=w_down, post_norm2_g=post_norm2_g)
    mom1 = dict(w_ada=m_w_ada, b_ada=m_b_ada, pre_norm1_g=m_pre_norm1_g, w_in=m_w_in, gm_ln_g=m_gm_ln_g, gm_ln_b=m_gm_ln_b,
                gm_w_s=m_gm_w_s, gm_b_s=m_gm_b_s, w_branch_a=m_w_branch_a, q_norm_g=m_q_norm_g, w_uq=m_w_uq, kv_norm_g=m_kv_norm_g,
                w_ukv=m_w_ukv, w_branch_b=m_w_branch_b, w_out=m_w_out, post_norm1_g=m_post_norm1_g, pre_norm2_g=m_pre_norm2_g,
                w_up=m_w_up, conv_w=m_conv_w, conv_b=m_conv_b, w_down=m_w_down, post_norm2_g=m_post_norm2_g)
    mom2 = dict(w_ada=v_w_ada, b_ada=v_b_ada, pre_norm1_g=v_pre_norm1_g, w_in=v_w_in, gm_ln_g=v_gm_ln_g, gm_ln_b=v_gm_ln_b,
                gm_w_s=v_gm_w_s, gm_b_s=v_gm_b_s, w_branch_a=v_w_branch_a, q_norm_g=v_q_norm_g, w_uq=v_w_uq, kv_norm_g=v_kv_norm_g,
                w_ukv=v_w_ukv, w_branch_b=v_w_branch_b, w_out=v_w_out, post_norm1_g=v_post_norm1_g, pre_norm2_g=v_pre_norm2_g,
                w_up=v_w_up, conv_w=v_conv_w, conv_b=v_conv_b, w_down=v_w_down, post_norm2_g=v_post_norm2_g)
    order = list(weights)
    _TOKENS.clear()

    s, d = x.shape[1], x.shape[2]
    gmw = gm_ln_g.shape[0]
    groups = gmw // CHUNK
    ql, kvl = q_norm_g.shape[0], kv_norm_g.shape[0]
    f2 = conv_b.shape[0]
    in_cols = w_in.shape[1] * N_DEV
    o_q, o_kv, o_ga, o_gb, o_kpe = 2 * gmw, 2 * gmw + ql, 2 * gmw + ql + kvl, 2 * gmw + ql + kvl + d, 2 * gmw + ql + kvl + 2 * d
    zp = _round_up(o_kpe + LANES, Z_PAD)
    src_kpe = 2 * gmw + ql + kvl
    assert src_kpe + QK_ROPE + 2 * d == in_cols
    my = 4 * lax.axis_index("x") + 2 * lax.axis_index("y") + lax.axis_index("c")

    x2, tgt = x[0], loss_target[0]
    row = lambda a: a.reshape(1, -1)

    big = ["w_in", "w_branch_a", "w_uq", "w_ukv", "w_branch_b", "w_out", "w_up", "w_down"]
    sh = {k: weights[k].astype(BF16) for k in big[1:]}
    mix = ["w_branch_a", "w_uq", "w_ukv", "w_branch_b", "w_out"]
    ag_in = ag_start("ag_start_in", [w_in.T.astype(BF16)], c)

    mod8, sc_all3, g_cw = ada_fwd(c, w_ada, b_ada.reshape(N_DEV, 1, -1), conv_w)
    mod = mod8.reshape(N_MOD, d)
    shift1, scale1, gate1, shift2, scale2, gate2 = (mod[i:i + 1] for i in range(N_MOD))
    sc_all = sc_all3.reshape(N_DEV, d)
    h1 = norm_mod_fwd("pre1_fwd", x2, row(pre_norm1_g), scale1, shift1)

    inv = ROPE_THETA ** (-jnp.arange(0, QK_ROPE, 2, dtype=F32) / QK_ROPE)
    ang = positions[0].astype(F32)[:, None] * inv
    cos4 = jnp.tile(jnp.cos(ang), (1, 4))
    sin4 = jnp.tile(jnp.concatenate([-jnp.sin(ang), jnp.sin(ang)], axis=1), (1, 2))

    wm = (gm_w_s * jnp.tril(jnp.ones((CHUNK, CHUNK), F32))).astype(BF16)
    bs3 = gm_b_s.reshape(groups, CHUNK, 1)
    ln_g, ln_b = row(gm_ln_g), row(gm_ln_b)

    early = [h1, cos4, sin4, wm] + [sh[k] for k in big[1:]]
    (g_in,) = ag_wait("ag_wait_in", ag_forward("ag_forward_in", ag_in, early), h1)
    ag_mix = ag_start("ag_start_mix", [sh[k] for k in mix], g_in)
    w_in_f = g_in.reshape(in_cols, d)
    w_in_p = jnp.concatenate([w_in_f[:src_kpe], w_in_f[src_kpe + QK_ROPE:], w_in_f[src_kpe:src_kpe + QK_ROPE],
                              jnp.zeros((zp - in_cols, d), BF16)], axis=0)

    z = mm_nt("z_proj", h1, w_in_p, F32)
    ag_mix = ag_forward("ag_forward_mix", ag_mix, z)
    a = gmlp_fwd(z, gmw, ln_g, ln_b, wm, bs3)
    g_a, g_uq, g_ukv, g_b, g_out = ag_wait("ag_wait_mix", ag_mix, a)
    ag_up = ag_start("ag_start_up", [sh["w_up"]], g_out)
    w_a_f, w_b_f, w_out_f = g_a.reshape(-1, d), g_b.reshape(-1, d), g_out.reshape(-1, d)
    w_uq_f = g_uq.transpose(1, 0, 2).reshape(ql, HEADS, QK_NOPE + QK_ROPE)
    w_uq_n = w_uq_f[:, :, :QK_NOPE].reshape(ql, HEADS * QK_NOPE)
    w_uq_r = w_uq_f[:, :, QK_NOPE:].reshape(ql, HEADS * QK_ROPE)
    y_a = mm_nn("branch_a", a, w_a_f, F32)
    qln = rms_fwd_cols("q_norm", z, o_q, ql, row(q_norm_g))
    kvn = rms_fwd_cols("kv_norm", z, o_kv, kvl, row(kv_norm_g))
    qn = mm_nn("q_nope", qln, w_uq_n, BF16)
    qp = mm_nn("q_rope", qln, w_uq_r, F32)
    kv = mm_nn_b3("kv_up", kvn, g_ukv, BF16)
    kpr = rope_k(z, o_kpe, cos4, sin4)
    o, qpr, lse = attn_fwd2(qn, qp, kv, kpr, cos4, sin4)
    ag_up = ag_forward("ag_forward_up", ag_up, o)
    y_b = mm_nn("branch_b", o, w_b_f, F32)
    merged = merge_fwd(z, o_ga, o_gb, y_a, y_b)
    y1 = mm_nn("out_proj", merged, w_out_f, F32)
    x1 = post_res_fwd("post1_fwd", x2, y1, gate1, row(post_norm1_g))
    h2 = norm_mod_fwd("pre2_fwd", x1, row(pre_norm2_g), scale2, shift2)
    (g_up,) = ag_wait("ag_wait_up", ag_up, h2)
    ag_down = ag_start("ag_start_down", [sh["w_down"]], g_up)
    upre = mm_nn_b3("up_proj", h2, g_up, F32)
    ag_down = ag_forward("ag_forward_down", ag_down, upre)
    cw = g_cw.transpose(1, 0, 2).reshape(3, f2)
    cb = row(conv_b)
    f = conv_act_fwd(upre, cw, cb)
    w_down_f = ag_wait("ag_wait_down", ag_down, f)[0].reshape(-1, d)
    ffn = mm_nn("down_proj", f, w_down_f, F32)
    loss_acc, dout, dffn, acc2 = post2_loss_bwd(x1, ffn, tgt, gate2, row(post_norm2_g))

    blocks = lambda g: g.reshape(N_DEV, g.shape[0] // N_DEV, g.shape[1])
    core = lax.axis_index("c").astype(jnp.int32).reshape(1)
    rs = {}

    def rs_begin(key, grads):
        rs[key] = rs_d2d_start("rs_d2d_start_" + key, grads)

    def rs_middle(key, after):
        grads, lands = rs_d2d_wait("rs_d2d_wait_" + key, rs[key], after)
        sums = [pair_sum("pair_sum_%s_%d" % (key, i), g, l, core) for i, (g, l) in enumerate(zip(grads, lands))]
        rs[key] = rs_ici_start("rs_ici_start_" + key, sums)

    gw_down = mm_tn("g_w_down", f, dffn, BF16)
    rs_begin("down", [blocks(gw_down)])
    df = mm_nt("d_f", dffn, w_down_f, F32)
    rs_middle("down", df)
    dupre, gcw_g, gcw_v, gcb_g, gcb_v = conv_act_bwd(upre, cw, cb, df)
    gw_up3 = mm_tn_h3("g_w_up", h2, dupre, N_DEV, BF16)
    rs_begin("up", [gw_up3])
    dh2 = mm_nt_h3("d_h2", dupre, g_up, F32)
    rs_middle("up", dh2)
    dx1, dy1, acc_mid = mid_bwd(dh2, dout, x1, y1, row(pre_norm2_g), scale2, gate1, row(post_norm1_g))
    gw_out = mm_tn("g_w_out", merged, dy1, BF16)
    dmerged = mm_nt("d_merged", dy1, w_out_f, F32)
    dya, dyb, dga, dgb = merge_bwd(z, o_ga, o_gb, y_a, y_b, dmerged)
    gw_a = mm_tn("g_w_a", a, dya, BF16)
    gw_b = mm_tn("g_w_b", o, dyb, BF16)
    rs_begin("mid", [blocks(gw_out), blocks(gw_a), blocks(gw_b)])
    da = mm_nt("d_a", dya, w_a_f, F32)
    do = mm_nt("d_o", dyb, w_b_f, F32)
    rs_middle("mid", do)
    duv, g_ws, g_bs3, acc_gm = gmlp_bwd(z, gmw, da, ln_g, ln_b, wm, bs3)
    dqn, dqp, dkv, dkp = attn_bwd2(qn, qpr, kv, kpr, o, do, lse, cos4, sin4)
    dkpe = kpe_bwd(dkp, cos4, sin4, zp - o_kpe)
    dq_cat = jnp.concatenate([dqn, dqp], axis=1)
    w_uq_cat = jnp.concatenate([w_uq_n, w_uq_r], axis=1)
    gw_uq_cat = mm_tn("g_w_uq", qln, dq_cat, BF16)
    gw_uq_f = jnp.concatenate([gw_uq_cat[:, :HEADS * QK_NOPE].reshape(ql, HEADS, QK_NOPE),
                               gw_uq_cat[:, HEADS * QK_NOPE:].reshape(ql, HEADS, QK_ROPE)], axis=2)
    gw_uq3 = gw_uq_f.reshape(ql, N_DEV, -1).transpose(1, 0, 2)
    gw_ukv3 = mm_tn_o3("g_w_ukv", kvn, dkv, N_DEV, BF16)
    rs_begin("mla", [gw_uq3, gw_ukv3])
    dqln = mm_nt("d_qln", dq_cat, w_uq_cat, F32)
    dq_lat, g_qnorm = rms_bwd_cols("q_norm_bwd", dqln, z, o_q, ql, row(q_norm_g))
    dkvn = mm_nt_b3("d_kvn", dkv, g_ukv, F32)
    rs_middle("mla", dkvn)
    dkv_lat, g_kvnorm = rms_bwd_cols("kv_norm_bwd", dkvn, z, o_kv, kvl, row(kv_norm_g))
    dz = jnp.concatenate([duv, dq_lat, dkv_lat, dga, dgb, dkpe], axis=1)
    gw_in_p = mm_tn("g_w_in", dz, h1, BF16)
    gw_in_f = jnp.concatenate([gw_in_p[:src_kpe], gw_in_p[o_kpe:o_kpe + QK_ROPE], gw_in_p[src_kpe:o_kpe]], axis=0)
    rs_begin("in", [gw_in_f.reshape(N_DEV, -1, d)])
    dh1 = mm_nn("d_h1", dz, w_in_p, F32)
    grad_x, acc1 = pre1_bwd(dh1, dx1, x2, row(pre_norm1_g), scale1)

    dmod = jnp.concatenate([acc1[0], acc1[1], acc_mid[3], acc_mid[0], acc_mid[1], acc2[0]])
    small = [("pre_norm1_g", acc1[2]), ("gm_ln_g", acc_gm[0]), ("gm_ln_b", acc_gm[1]), ("gm_b_s", g_bs3.reshape(-1)),
             ("q_norm_g", g_qnorm[0]), ("kv_norm_g", g_kvnorm[0]), ("post_norm1_g", acc_mid[4]), ("pre_norm2_g", acc_mid[2]),
             ("conv_b", jnp.concatenate([gcb_g[0], gcb_v[0]])), ("post_norm2_g", acc2[1]), ("gm_w_s", g_ws.reshape(-1)),
             ("b_ada", dmod)]
    n_small = sum(v.shape[0] for _, v in small)
    n_cw = 3 * f2
    n_pack = _round_up(n_small + n_cw, PACK_ALIGN)
    tail = jnp.zeros((n_pack - n_small - n_cw,), F32)
    packed = jnp.concatenate([v for _, v in small] + [jnp.concatenate([gcw_g, gcw_v], axis=1).reshape(-1), tail])
    ag_small = ag_start("ag_start_small", [packed.reshape(-1, LANES)], packed)
    rs_middle("in", packed)

    res = {}
    last = packed
    for key, names in (("down", ["w_down"]), ("up", ["w_up"]), ("mid", ["w_out", "w_branch_a", "w_branch_b"]), ("mla", ["w_uq", "w_ukv"])):
        parts = rs_ici_wait("rs_ici_wait_" + key, rs[key], last)
        for k, p in zip(names, parts):
            res[k] = adamw("adamw_" + k, weights[k], mom1[k], mom2[k], p)
            last = res[k][0]

    def pack(src):
        return jnp.concatenate([src[k].reshape(-1) for k, _ in small] + [jnp.zeros((n_pack - n_small,), F32)]).reshape(-1, LANES)

    (gathered,) = ag_wait("ag_wait_small", ag_forward("ag_forward_small", ag_small, last), last)
    sm = [t.reshape(-1) for t in adamw("adamw_small", pack(weights), pack(mom1), pack(mom2), gathered)]
    off = 0
    for k, v in small:
        res[k] = tuple(t[off:off + v.shape[0]].reshape(weights[k].shape) for t in sm)
        off += v.shape[0]

    cs_cw = conv_w.shape[1]
    g_cw_full = sm[0][n_small:n_small + n_cw].reshape(3, f2)
    g_cw_mine = lax.dynamic_slice(g_cw_full, (0, my * cs_cw), (3, cs_cw))
    res["conv_w"] = adamw("adamw_conv_w", conv_w, mom1["conv_w"], mom2["conv_w"], g_cw_mine[None])

    cs_ada = w_ada.shape[1]
    off_b = n_small - N_MOD * d
    dmod_all = gathered.reshape(N_DEV, -1)[:, off_b:off_b + N_MOD * d]
    dmod_cols = lax.dynamic_slice(dmod_all, (0, my * cs_ada), (N_DEV, cs_ada))
    res["w_ada"] = adamw("adamw_w_ada", w_ada, mom1["w_ada"], mom2["w_ada"], ada_bwd_w(sc_all, dmod_cols))

    (p_in,) = rs_ici_wait("rs_ici_wait_in", rs["in"], res["w_ada"][0])
    res["w_in"] = tuple(t.T for t in adamw("adamw_w_in", w_in.T, mom1["w_in"].T, mom2["w_in"].T, p_in))

    _TOKENS.clear()
    loss = lax.psum(loss_acc[0, 0], ("x", "y", "c"))
    outs = [loss, grad_x[None]]
    for i in range(4):
        outs += [res[k][i] for k in order]
    return tuple(outs)
```

```python
import functools

import jax
import jax.numpy as jnp
from jax import lax
from jax.experimental import pallas as pl
from jax.experimental.pallas import tpu as pltpu

F32 = jnp.float32
BF16 = jnp.bfloat16

N_DEV = 8
HEADS = 16
QK_NOPE = 128
QK_ROPE = 64
V_HEAD = 128
CHUNK = 128
ROPE_THETA = 10000.0
EPS = 1e-6
N_MOD = 6
ADAM_LR, ADAM_B1, ADAM_B2, ADAM_EPS, ADAM_WD, ADAM_STEP = 0.001, 0.9, 0.999, 1e-08, 0.01, 10

LANES = 128
VMEM_LIMIT_BYTES = 48 * 2 ** 20
ROW_TILE = 256
COL_TILE = 256
ATT_TILE = 256
Z_PAD = 512
ADAMW_TILE_ELEMS = 1 << 18
PACK_ALIGN = 8 * LANES
MESH = pl.DeviceIdType.MESH


def _params(*sem):
    return pltpu.CompilerParams(dimension_semantics=sem if sem else None, vmem_limit_bytes=VMEM_LIMIT_BYTES)


def _tile(dim, target):
    t = (min(dim, target) // LANES) * LANES
    while t >= LANES:
        if dim % t == 0:
            return t
        t -= LANES
    return dim


def _full(shape):
    nd = len(shape)
    return pl.BlockSpec(shape, lambda *_: (0,) * nd)


class _Tokens:
    KEEP = 2

    def __init__(self):
        self.pending = []

    def push(self, token):
        self.pending = (self.pending + [token])[-self.KEEP:]

    def take(self):
        return list(self.pending)

    def clear(self):
        self.pending = []


_TOKENS = _Tokens()


def _matmul(name, a, b, *, grid, a_spec, b_spec, o_spec, out_shape, contract, acc_shape, split=1):
    nk = grid[2]
    deps = _TOKENS.take()

    def product(a_ref, b_ref):
        if len(b_ref.shape) == 2:
            return lax.dot_general(a_ref[...].astype(BF16), b_ref[...].astype(BF16), (contract, ((), ())), preferred_element_type=F32)
        cs = b_ref.shape[2]
        return sum(lax.dot_general(a_ref[:, s * cs:(s + 1) * cs].astype(BF16), b_ref[s].astype(BF16), (contract, ((), ())),
                                   preferred_element_type=F32) for s in range(split))

    def body_one_step(a_ref, b_ref, *rest):
        o_ref = rest[len(deps)]
        o_ref[...] = product(a_ref, b_ref).astype(o_ref.dtype)

    def body(a_ref, b_ref, *rest):
        o_ref, acc_ref = rest[len(deps):]
        k = pl.program_id(2)

        @pl.when(k == 0)
        def _():
            acc_ref[...] = jnp.zeros_like(acc_ref)

        acc_ref[...] += product(a_ref, b_ref)

        @pl.when(k == nk - 1)
        def _():
            o_ref[...] = acc_ref[...].astype(o_ref.dtype)

    return pl.pallas_call(
        body_one_step if nk == 1 else body, name=name, grid=grid,
        in_specs=[a_spec, b_spec] + [pl.BlockSpec(memory_space=pl.ANY)] * len(deps),
        out_specs=o_spec, out_shape=out_shape, scratch_shapes=[] if nk == 1 else [pltpu.VMEM(acc_shape, F32)],
        compiler_params=_params("parallel", "parallel", "arbitrary"))(a, b, *deps)


TM, TN, TK = 1024, 1024, 2304


def _tk(a, b):
    return TK if a.dtype == BF16 and b.dtype == BF16 else TK // 2


def mm_nn(name, a, b, dtype):
    (m, k), n = a.shape, b.shape[1]
    tm, tn, tk = _tile(m, TM), _tile(n, TN), _tile(k, _tk(a, b))
    return _matmul(name, a, b, grid=(m // tm, n // tn, k // tk),
                   a_spec=pl.BlockSpec((tm, tk), lambda i, j, kk: (i, kk)),
                   b_spec=pl.BlockSpec((tk, tn), lambda i, j, kk: (kk, j)),
                   o_spec=pl.BlockSpec((tm, tn), lambda i, j, kk: (i, j)),
                   out_shape=jax.ShapeDtypeStruct((m, n), dtype), contract=((1,), (0,)), acc_shape=(tm, tn))


def mm_nn_b3(name, a, b3, dtype):
    (m, k), (nj, _, cs) = a.shape, b3.shape
    tm, tk = _tile(m, TM), _tile(k, _tk(a, b3))
    return _matmul(name, a, b3, grid=(m // tm, nj, k // tk),
                   a_spec=pl.BlockSpec((tm, tk), lambda i, j, kk: (i, kk)),
                   b_spec=pl.BlockSpec((None, tk, cs), lambda i, j, kk: (j, kk, 0)),
                   o_spec=pl.BlockSpec((tm, cs), lambda i, j, kk: (i, j)),
                   out_shape=jax.ShapeDtypeStruct((m, nj * cs), dtype), contract=((1,), (0,)), acc_shape=(tm, cs))


def mm_nt(name, a, b, dtype):
    (m, k), n = a.shape, b.shape[0]
    tm, tn, tk = _tile(m, TM), _tile(n, TN), _tile(k, _tk(a, b))
    return _matmul(name, a, b, grid=(m // tm, n // tn, k // tk),
                   a_spec=pl.BlockSpec((tm, tk), lambda i, j, kk: (i, kk)),
                   b_spec=pl.BlockSpec((tn, tk), lambda i, j, kk: (j, kk)),
                   o_spec=pl.BlockSpec((tm, tn), lambda i, j, kk: (i, j)),
                   out_shape=jax.ShapeDtypeStruct((m, n), dtype), contract=((1,), (1,)), acc_shape=(tm, tn))


def mm_nt_b3(name, a, b3, dtype):
    m, (nj, n, cs) = a.shape[0], b3.shape
    tm, tn = _tile(m, TM), _tile(n, TN)
    return _matmul(name, a, b3, grid=(m // tm, n // tn, nj),
                   a_spec=pl.BlockSpec((tm, cs), lambda i, j, kk: (i, kk)),
                   b_spec=pl.BlockSpec((None, tn, cs), lambda i, j, kk: (kk, j, 0)),
                   o_spec=pl.BlockSpec((tm, tn), lambda i, j, kk: (i, j)),
                   out_shape=jax.ShapeDtypeStruct((m, n), dtype), contract=((1,), (1,)), acc_shape=(tm, tn))


def mm_nt_h3(name, a3, b3, dtype):
    (_, m, _), (nj, n, cs) = a3.shape, b3.shape
    tm, tn, hj = _tile(m, TM), _tile(n, TN), nj // 2
    pair = 2 if hj % 2 == 0 else 1
    return _matmul(name, a3, b3.reshape(nj // pair, pair, n, cs), grid=(m // tm, n // tn, nj // pair),
                   a_spec=pl.BlockSpec((None, tm, pair * cs), lambda i, j, kk: (kk // (hj // pair), i, kk % (hj // pair))),
                   b_spec=pl.BlockSpec((None, pair, tn, cs), lambda i, j, kk: (kk, 0, j, 0)),
                   o_spec=pl.BlockSpec((tm, tn), lambda i, j, kk: (i, j)),
                   out_shape=jax.ShapeDtypeStruct((m, n), dtype), contract=((1,), (1,)), acc_shape=(tm, tn), split=pair)


def mm_tn_h3(name, a, b3, nj, dtype):
    (k, m), half = a.shape, b3.shape[2]
    hj = nj // 2
    cs = half // hj
    tm, tk = _tile(m, TM), _tile(k, _tk(a, b3))
    return _matmul(name, a, b3, grid=(m // tm, nj, k // tk),
                   a_spec=pl.BlockSpec((tk, tm), lambda i, j, kk: (kk, i)),
                   b_spec=pl.BlockSpec((None, tk, cs), lambda i, j, kk: (j // hj, kk, j % hj)),
                   o_spec=pl.BlockSpec((None, tm, cs), lambda i, j, kk: (j, i, 0)),
                   out_shape=jax.ShapeDtypeStruct((nj, m, cs), dtype), contract=((0,), (0,)), acc_shape=(tm, cs))


def mm_tn(name, a, b, dtype):
    (k, m), n = a.shape, b.shape[1]
    tm, tn, tk = _tile(m, TM), _tile(n, TN), _tile(k, _tk(a, b))
    return _matmul(name, a, b, grid=(m // tm, n // tn, k // tk),
                   a_spec=pl.BlockSpec((tk, tm), lambda i, j, kk: (kk, i)),
                   b_spec=pl.BlockSpec((tk, tn), lambda i, j, kk: (kk, j)),
                   o_spec=pl.BlockSpec((tm, tn), lambda i, j, kk: (i, j)),
                   out_shape=jax.ShapeDtypeStruct((m, n), dtype), contract=((0,), (0,)), acc_shape=(tm, tn))


def mm_tn_o3(name, a, b, nj, dtype):
    (k, m), n = a.shape, b.shape[1]
    cs = n // nj
    tm, tk = _tile(m, TM), _tile(k, _tk(a, b))
    return _matmul(name, a, b, grid=(m // tm, nj, k // tk),
                   a_spec=pl.BlockSpec((tk, tm), lambda i, j, kk: (kk, i)),
                   b_spec=pl.BlockSpec((tk, cs), lambda i, j, kk: (kk, j)),
                   o_spec=pl.BlockSpec((None, tm, cs), lambda i, j, kk: (j, i, 0)),
                   out_shape=jax.ShapeDtypeStruct((nj, m, cs), dtype), contract=((0,), (0,)), acc_shape=(tm, cs))


_GELU_C = 0.7978845608028654
_GELU_A = 0.044715


def _gelu(x):
    return 0.5 * x * (1.0 + jnp.tanh(_GELU_C * (x + _GELU_A * x * x * x)))


def _gelu_and_grad(x):
    t = jnp.tanh(_GELU_C * (x + _GELU_A * x * x * x))
    y = 0.5 * x * (1.0 + t)
    dy = 0.5 * (1.0 + t) + 0.5 * x * (1.0 - t * t) * (_GELU_C * (1.0 + 3.0 * _GELU_A * x * x))
    return y, dy


def _sigmoid(x):
    return 1.0 / (1.0 + jnp.exp(-x))


def _rms_stats(x):
    inv = lax.rsqrt(jnp.mean(x * x, axis=-1, keepdims=True) + EPS)
    return inv, x * inv


def _rms_bwd(dyhat, yhat, inv):
    return inv * (dyhat - yhat * jnp.mean(dyhat * yhat, axis=-1, keepdims=True))


def _colsum(x):
    return jnp.sum(x, axis=0, keepdims=True)


def _rope(x, cos4, sin4):
    lane = lax.broadcasted_iota(jnp.int32, x.shape, x.ndim - 1)
    first_half = (lane % QK_ROPE) < (QK_ROPE // 2)
    partner = jnp.where(first_half, pltpu.roll(x, LANES - QK_ROPE // 2, x.ndim - 1), pltpu.roll(x, QK_ROPE // 2, x.ndim - 1))
    return x * cos4 + partner * sin4


def norm_mod_fwd(name, x, g, scale, shift):
    s, d = x.shape
    tr = _tile(s, ROW_TILE)

    def body(x_ref, g_ref, sc_ref, sh_ref, o_ref):
        _, xh = _rms_stats(x_ref[...])
        o_ref[...] = (xh * g_ref[...] * (1.0 + sc_ref[...]) + sh_ref[...]).astype(o_ref.dtype)

    row = pl.BlockSpec((tr, d), lambda i: (i, 0))
    vec = pl.BlockSpec((1, d), lambda i: (0, 0))
    return pl.pallas_call(body, name=name, grid=(s // tr,), in_specs=[row, vec, vec, vec], out_specs=row,
                          out_shape=jax.ShapeDtypeStruct((s, d), BF16), compiler_params=_params("parallel"))(x, g, scale, shift)


def rms_fwd_cols(name, z, off, width, g):
    s = z.shape[0]
    tr = _tile(s, ROW_TILE)
    assert off % width == 0

    def body(x_ref, g_ref, o_ref):
        _, xh = _rms_stats(x_ref[...])
        o_ref[...] = (xh * g_ref[...]).astype(o_ref.dtype)

    return pl.pallas_call(body, name=name, grid=(s // tr,),
                          in_specs=[pl.BlockSpec((tr, width), lambda i: (i, off // width)), pl.BlockSpec((1, width), lambda i: (0, 0))],
                          out_specs=pl.BlockSpec((tr, width), lambda i: (i, 0)),
                          out_shape=jax.ShapeDtypeStruct((s, width), BF16), compiler_params=_params("parallel"))(z, g)


def rms_bwd_cols(name, dy, z, off, width, g):
    s = z.shape[0]
    tr = _tile(s, ROW_TILE)

    def body(dy_ref, x_ref, g_ref, dx_ref, gg_ref):
        @pl.when(pl.program_id(0) == 0)
        def _():
            gg_ref[...] = jnp.zeros_like(gg_ref)

        inv, xh = _rms_stats(x_ref[...])
        dy_v = dy_ref[...]
        gg_ref[...] += _colsum(dy_v * xh)
        dx_ref[...] = _rms_bwd(dy_v * g_ref[...], xh, inv).astype(dx_ref.dtype)

    return pl.pallas_call(body, name=name, grid=(s // tr,),
                          in_specs=[pl.BlockSpec((tr, width), lambda i: (i, 0)), pl.BlockSpec((tr, width), lambda i: (i, off // width)),
                                    pl.BlockSpec((1, width), lambda i: (0, 0))],
                          out_specs=[pl.BlockSpec((tr, width), lambda i: (i, 0)), pl.BlockSpec((1, width), lambda i: (0, 0))],
                          out_shape=[jax.ShapeDtypeStruct((s, width), BF16), jax.ShapeDtypeStruct((1, width), F32)],
                          compiler_params=_params("arbitrary"))(dy, z, g)


def post_res_fwd(name, x, y, gate, g):
    s, d = x.shape
    tr = _tile(s, ROW_TILE)

    def body(x_ref, y_ref, gate_ref, g_ref, o_ref):
        _, yh = _rms_stats(y_ref[...])
        o_ref[...] = x_ref[...] + gate_ref[...] * (yh * g_ref[...])

    row = pl.BlockSpec((tr, d), lambda i: (i, 0))
    vec = pl.BlockSpec((1, d), lambda i: (0, 0))
    return pl.pallas_call(body, name=name, grid=(s // tr,), in_specs=[row, row, vec, vec], out_specs=row,
                          out_shape=jax.ShapeDtypeStruct((s, d), F32), compiler_params=_params("parallel"))(x, y, gate, g)


def post2_loss_bwd(x1, ffn, target, gate2, g):
    s, d = x1.shape
    tr = _tile(s, ROW_TILE)

    def body(x_ref, y_ref, t_ref, gate_ref, g_ref, loss_ref, dout_ref, dy_ref, acc_ref):
        @pl.when(pl.program_id(0) == 0)
        def _():
            loss_ref[...] = jnp.zeros_like(loss_ref)
            acc_ref[...] = jnp.zeros_like(acc_ref)

        inv, yh = _rms_stats(y_ref[...])
        r = yh * g_ref[...]
        err = x_ref[...] + gate_ref[...] * r - t_ref[...]
        loss_ref[...] += 0.5 * jnp.sum(jnp.mean(err * err, axis=-1, keepdims=True))
        dout = err / d
        dout_ref[...] = dout
        dr = dout * gate_ref[...]
        acc_ref[0:1, :] += _colsum(dout * r)
        acc_ref[1:2, :] += _colsum(dr * yh)
        dy_ref[...] = _rms_bwd(dr * g_ref[...], yh, inv).astype(dy_ref.dtype)

    row = pl.BlockSpec((tr, d), lambda i: (i, 0))
    vec = pl.BlockSpec((1, d), lambda i: (0, 0))
    return pl.pallas_call(
        body, name="post2_loss_bwd", grid=(s // tr,), in_specs=[row, row, row, vec, vec],
        out_specs=[_full((8, LANES)), row, row, _full((8, d))],
        out_shape=[jax.ShapeDtypeStruct((8, LANES), F32), jax.ShapeDtypeStruct((s, d), F32),
                   jax.ShapeDtypeStruct((s, d), BF16), jax.ShapeDtypeStruct((8, d), F32)],
        compiler_params=_params("arbitrary"))(x1, ffn, target, gate2, g)


def mid_bwd(dh2, dout, x1, y1, pre2_g, scale2, gate1, post1_g):
    s, d = x1.shape
    tr = _tile(s, ROW_TILE)

    def body(dh_ref, dout_ref, x_ref, y_ref, g2_ref, sc_ref, gate_ref, g1_ref, dx_ref, dy_ref, acc_ref):
        @pl.when(pl.program_id(0) == 0)
        def _():
            acc_ref[...] = jnp.zeros_like(acc_ref)

        dh = dh_ref[...]
        inv2, xh = _rms_stats(x_ref[...])
        acc_ref[0:1, :] += _colsum(dh)
        acc_ref[1:2, :] += _colsum(dh * (xh * g2_ref[...]))
        t = dh * (1.0 + sc_ref[...])
        acc_ref[2:3, :] += _colsum(t * xh)
        dx1 = dout_ref[...] + _rms_bwd(t * g2_ref[...], xh, inv2)
        dx_ref[...] = dx1
        inv1, yh = _rms_stats(y_ref[...])
        acc_ref[3:4, :] += _colsum(dx1 * (yh * g1_ref[...]))
        dr = dx1 * gate_ref[...]
        acc_ref[4:5, :] += _colsum(dr * yh)
        dy_ref[...] = _rms_bwd(dr * g1_ref[...], yh, inv1).astype(dy_ref.dtype)

    row = pl.BlockSpec((tr, d), lambda i: (i, 0))
    vec = pl.BlockSpec((1, d), lambda i: (0, 0))
    return pl.pallas_call(
        body, name="mid_bwd", grid=(s // tr,), in_specs=[row, row, row, row, vec, vec, vec, vec],
        out_specs=[row, row, _full((8, d))],
        out_shape=[jax.ShapeDtypeStruct((s, d), F32), jax.ShapeDtypeStruct((s, d), BF16), jax.ShapeDtypeStruct((8, d), F32)],
        compiler_params=_params("arbitrary"))(dh2, dout, x1, y1, pre2_g, scale2, gate1, post1_g)


def pre1_bwd(dh1, dx1, x, pre1_g, scale1):
    s, d = x.shape
    tr = _tile(s, ROW_TILE)

    def body(dh_ref, dx1_ref, x_ref, g_ref, sc_ref, dx_ref, acc_ref):
        @pl.when(pl.program_id(0) == 0)
        def _():
            acc_ref[...] = jnp.zeros_like(acc_ref)

        dh = dh_ref[...]
        inv, xh = _rms_stats(x_ref[...])
        acc_ref[0:1, :] += _colsum(dh)
        acc_ref[1:2, :] += _colsum(dh * (xh * g_ref[...]))
        t = dh * (1.0 + sc_ref[...])
        acc_ref[2:3, :] += _colsum(t * xh)
        dx_ref[...] = dx1_ref[...] + _rms_bwd(t * g_ref[...], xh, inv)

    row = pl.BlockSpec((tr, d), lambda i: (i, 0))
    vec = pl.BlockSpec((1, d), lambda i: (0, 0))
    return pl.pallas_call(
        body, name="pre1_bwd", grid=(s // tr,), in_specs=[row, row, row, vec, vec], out_specs=[row, _full((8, d))],
        out_shape=[jax.ShapeDtypeStruct((s, d), F32), jax.ShapeDtypeStruct((8, d), F32)],
        compiler_params=_params("arbitrary"))(dh1, dx1, x, pre1_g, scale1)


def _ln_stats(v):
    mu = jnp.mean(v, axis=-1, keepdims=True)
    vc = v - mu
    rstd = lax.rsqrt(jnp.mean(vc * vc, axis=-1, keepdims=True) + EPS)
    return rstd, vc * rstd


def gmlp_fwd(z, width, ln_g, ln_b, wm, bs3):
    s = z.shape[0]
    groups = width // CHUNK

    def body(u_ref, v_ref, g_ref, b_ref, wm_ref, bs_ref, a_ref):
        ug = _gelu(u_ref[...])
        _, vh = _ln_stats(_gelu(v_ref[...]))
        vn = (vh * g_ref[...] + b_ref[...]).astype(BF16)
        for g in range(groups):
            cols = slice(g * CHUNK, (g + 1) * CHUNK)
            mixed = jnp.dot(wm_ref[g], vn[:, cols], preferred_element_type=F32) + bs_ref[g]
            a_ref[:, cols] = (ug[:, cols] * mixed).astype(a_ref.dtype)

    vec = pl.BlockSpec((1, width), lambda n: (0, 0))
    return pl.pallas_call(
        body, name="gmlp_fwd", grid=(s // CHUNK,),
        in_specs=[pl.BlockSpec((CHUNK, width), lambda n: (n, 0)), pl.BlockSpec((CHUNK, width), lambda n: (n, 1)), vec, vec,
                  _full(wm.shape), _full(bs3.shape)],
        out_specs=pl.BlockSpec((CHUNK, width), lambda n: (n, 0)),
        out_shape=jax.ShapeDtypeStruct((s, width), BF16), compiler_params=_params("parallel"))(z, z, ln_g, ln_b, wm, bs3)


def gmlp_bwd(z, width, da, ln_g, ln_b, wm, bs3):
    s = z.shape[0]
    groups = width // CHUNK

    def body(u_ref, v_ref, da_ref, g_ref, b_ref, wm_ref, bs_ref, duv_ref, gw_ref, gb_ref, acc_ref, dvn_ref):
        @pl.when(pl.program_id(0) == 0)
        def _():
            gw_ref[...] = jnp.zeros_like(gw_ref)
            gb_ref[...] = jnp.zeros_like(gb_ref)
            acc_ref[...] = jnp.zeros_like(acc_ref)

        ug, dug = _gelu_and_grad(u_ref[...])
        vg, dvg = _gelu_and_grad(v_ref[...])
        rstd, vh = _ln_stats(vg)
        vn = (vh * g_ref[...] + b_ref[...]).astype(BF16)
        da_v = da_ref[...]
        for g in range(groups):
            cols = slice(g * CHUNK, (g + 1) * CHUNK)
            mixed = jnp.dot(wm_ref[g], vn[:, cols], preferred_element_type=F32) + bs_ref[g]
            duv_ref[:, cols] = (da_v[:, cols] * mixed * dug[:, cols]).astype(duv_ref.dtype)
            dm = da_v[:, cols] * ug[:, cols]
            gb_ref[g] += jnp.sum(dm, axis=-1, keepdims=True)
            dmb = dm.astype(BF16)
            gw_ref[g] += lax.dot_general(dmb, vn[:, cols], (((1,), (1,)), ((), ())), preferred_element_type=F32)
            dvn_ref[:, cols] = lax.dot_general(wm_ref[g], dmb, (((0,), (0,)), ((), ())), preferred_element_type=F32)
        dvn = dvn_ref[...]
        acc_ref[0:1, :] += _colsum(dvn * vh)
        acc_ref[1:2, :] += _colsum(dvn)
        dvh = dvn * g_ref[...]
        dv = rstd * (dvh - jnp.mean(dvh, axis=-1, keepdims=True) - vh * jnp.mean(dvh * vh, axis=-1, keepdims=True))
        duv_ref[:, width:] = (dv * dvg).astype(duv_ref.dtype)

        @pl.when(pl.program_id(0) == pl.num_programs(0) - 1)
        def _():
            q = lax.broadcasted_iota(jnp.int32, gw_ref.shape, 1)
            p = lax.broadcasted_iota(jnp.int32, gw_ref.shape, 2)
            gw_ref[...] = jnp.where(p <= q, gw_ref[...], 0.0)

    vec = pl.BlockSpec((1, width), lambda n: (0, 0))
    blk = pl.BlockSpec((CHUNK, width), lambda n: (n, 0))
    return pl.pallas_call(
        body, name="gmlp_bwd", grid=(s // CHUNK,),
        in_specs=[blk, pl.BlockSpec((CHUNK, width), lambda n: (n, 1)), blk, vec, vec, _full(wm.shape), _full(bs3.shape)],
        out_specs=[pl.BlockSpec((CHUNK, 2 * width), lambda n: (n, 0)), _full(wm.shape), _full(bs3.shape), _full((8, width))],
        out_shape=[jax.ShapeDtypeStruct((s, 2 * width), BF16), jax.ShapeDtypeStruct(wm.shape, F32),
                   jax.ShapeDtypeStruct(bs3.shape, F32), jax.ShapeDtypeStruct((8, width), F32)],
        scratch_shapes=[pltpu.VMEM((CHUNK, width), F32)],
        compiler_params=_params("arbitrary"))(z, z, da, ln_g, ln_b, wm, bs3)


def merge_fwd(z, off_a, off_b, ya, yb):
    s, d = ya.shape
    tr, tc = _tile(s, ROW_TILE * 2), _tile(d, COL_TILE)
    assert off_a % tc == 0 and off_b % tc == 0

    def body(ga_ref, gb_ref, ya_ref, yb_ref, o_ref):
        o_ref[...] = (_sigmoid(ga_ref[...]) * ya_ref[...] + _sigmoid(gb_ref[...]) * yb_ref[...]).astype(o_ref.dtype)

    blk = pl.BlockSpec((tr, tc), lambda i, j: (i, j))
    return pl.pallas_call(
        body, name="merge_fwd", grid=(s // tr, d // tc),
        in_specs=[pl.BlockSpec((tr, tc), lambda i, j: (i, off_a // tc + j)), pl.BlockSpec((tr, tc), lambda i, j: (i, off_b // tc + j)), blk, blk],
        out_specs=blk, out_shape=jax.ShapeDtypeStruct((s, d), BF16), compiler_params=_params("parallel", "parallel"))(z, z, ya, yb)


def merge_bwd(z, off_a, off_b, ya, yb, dm):
    s, d = ya.shape
    tr, tc = _tile(s, ROW_TILE * 2), _tile(d, COL_TILE)
    nc = d // tc

    def body(ga_ref, gb_ref, ya_ref, yb_ref, dm_ref, dya_ref, dyb_ref, dga_ref, dgb_ref):
        dm_v = dm_ref[...]
        sa, sb = _sigmoid(ga_ref[...]), _sigmoid(gb_ref[...])
        dya_ref[...] = (dm_v * sa).astype(dya_ref.dtype)
        dyb_ref[...] = (dm_v * sb).astype(dyb_ref.dtype)
        dga_ref[...] = (dm_v * ya_ref[...] * sa * (1.0 - sa)).astype(dga_ref.dtype)
        dgb_ref[...] = (dm_v * yb_ref[...] * sb * (1.0 - sb)).astype(dgb_ref.dtype)

    blk = pl.BlockSpec((tr, tc), lambda i, j: (i, j))
    out = jax.ShapeDtypeStruct((s, d), BF16)
    return pl.pallas_call(
        body, name="merge_bwd", grid=(s // tr, nc),
        in_specs=[pl.BlockSpec((tr, tc), lambda i, j: (i, off_a // tc + j)), pl.BlockSpec((tr, tc), lambda i, j: (i, off_b // tc + j)), blk, blk, blk],
        out_specs=[blk, blk, blk, blk], out_shape=[out, out, out, out],
        compiler_params=_params("parallel", "parallel"))(z, z, ya, yb, dm)


_ATT_SCALE = (QK_NOPE + QK_ROPE) ** -0.5
_NEG = -1e30


def rope_k(z, off, cos4, sin4):
    s = z.shape[0]
    tr = _tile(s, ROW_TILE * 2)
    assert off % LANES == 0

    def body(k_ref, c_ref, s_ref, o_ref):
        k = k_ref[...]
        k = k + pltpu.roll(k, QK_ROPE, 1)
        o_ref[...] = _rope(k, c_ref[...], s_ref[...]).astype(o_ref.dtype)

    row = pl.BlockSpec((tr, LANES), lambda i: (i, 0))
    return pl.pallas_call(body, name="rope_k", grid=(s // tr,),
                          in_specs=[pl.BlockSpec((tr, LANES), lambda i: (i, off // LANES)), row, row], out_specs=row,
                          out_shape=jax.ShapeDtypeStruct((s, LANES), BF16), compiler_params=_params("parallel"))(z, cos4, sin4)


def _head_masks(shape):
    lane = lax.broadcasted_iota(jnp.int32, shape, 1)
    return lane < QK_ROPE, lane >= QK_ROPE


def _scores(qn, qp_h, k, kp, qi, kb, t):
    sc = lax.dot_general(qn, k, (((1,), (1,)), ((), ())), preferred_element_type=F32)
    sc += lax.dot_general(qp_h, kp, (((1,), (1,)), ((), ())), preferred_element_type=F32)
    sc = sc * _ATT_SCALE
    row = lax.broadcasted_iota(jnp.int32, sc.shape, 0) + qi * t
    col = lax.broadcasted_iota(jnp.int32, sc.shape, 1) + kb * t
    return jnp.where(col <= row, sc, _NEG)


def attn_fwd(qn, qp, kv, kpr, cos4, sin4):
    s = qn.shape[0]
    hp = HEADS // 2
    t = _tile(s, ATT_TILE)
    nq = s // t

    def body(qn_ref, qp_ref, kv_ref, kp_ref, c_ref, s_ref, o_ref, qpr_ref, l_ref):
        qi = pl.program_id(1)
        qpr = _rope(qp_ref[...], c_ref[...], s_ref[...]).astype(BF16)
        qpr_ref[...] = qpr
        masks = _head_masks(qpr.shape)
        for hh in range(2):
            q_n = qn_ref[:, hh * QK_NOPE:(hh + 1) * QK_NOPE]
            q_p = jnp.where(masks[hh], qpr, jnp.zeros_like(qpr))
            kc, vc = 2 * hh * QK_NOPE, (2 * hh + 1) * QK_NOPE

            def step(kb, carry):
                m, l, acc = carry
                rows = pl.ds(pl.multiple_of(kb * t, t), t)
                sc = _scores(q_n, q_p, kv_ref[rows, kc:kc + QK_NOPE], kp_ref[rows, :], qi, kb, t)
                m_new = jnp.maximum(m, jnp.max(sc, axis=-1, keepdims=True))
                alpha = jnp.exp(m - m_new)
                p = jnp.exp(sc - m_new)
                l = alpha * l + jnp.sum(p, axis=-1, keepdims=True)
                acc = alpha * acc + jnp.dot(p.astype(BF16), kv_ref[rows, vc:vc + V_HEAD], preferred_element_type=F32)
                return m_new, l, acc

            init = (jnp.full((t, 1), _NEG, F32), jnp.zeros((t, 1), F32), jnp.zeros((t, V_HEAD), F32))
            m, l, acc = lax.fori_loop(0, qi + 1, step, init)
            o_ref[:, hh * V_HEAD:(hh + 1) * V_HEAD] = acc / l
            l_ref[:, hh:hh + 1] = m + jnp.log(l)

    return pl.pallas_call(
        body, name="attn_fwd", grid=(hp, nq),
        in_specs=[pl.BlockSpec((t, 2 * QK_NOPE), lambda h, i: (i, h)), pl.BlockSpec((t, LANES), lambda h, i: (i, h)),
                  pl.BlockSpec((s, 4 * QK_NOPE), lambda h, i: (0, h)), _full((s, LANES)),
                  pl.BlockSpec((t, LANES), lambda h, i: (i, 0)), pl.BlockSpec((t, LANES), lambda h, i: (i, 0))],
        out_specs=[pl.BlockSpec((t, 2 * V_HEAD), lambda h, i: (i, h)), pl.BlockSpec((t, LANES), lambda h, i: (i, h)),
                   pl.BlockSpec((None, t, 2), lambda h, i: (h, i, 0))],
        out_shape=[jax.ShapeDtypeStruct((s, HEADS * V_HEAD), F32), jax.ShapeDtypeStruct((s, HEADS * QK_ROPE), BF16),
                   jax.ShapeDtypeStruct((hp, s, 2), F32)],
        compiler_params=_params("parallel", "parallel"))(qn, qp, kv, kpr, cos4, sin4)


def attn_bwd_q(qn, qpr, kv, kpr, o, do, lse, cos4, sin4):
    s = qn.shape[0]
    hp = HEADS // 2
    t = _tile(s, ATT_TILE)
    nq = s // t

    def body(qn_ref, qpr_ref, kv_ref, kp_ref, o_ref, do_ref, l_ref, c_ref, s_ref, dqn_ref, dqp_ref):
        qi = pl.program_id(1)
        qpr = qpr_ref[...]
        masks = _head_masks(qpr.shape)
        dqp = jnp.zeros(qpr.shape, F32)
        for hh in range(2):
            q_n = qn_ref[:, hh * QK_NOPE:(hh + 1) * QK_NOPE]
            q_p = jnp.where(masks[hh], qpr, jnp.zeros_like(qpr))
            kc, vc = 2 * hh * QK_NOPE, (2 * hh + 1) * QK_NOPE
            do_h = do_ref[:, hh * V_HEAD:(hh + 1) * V_HEAD]
            delta = jnp.sum(do_h * o_ref[:, hh * V_HEAD:(hh + 1) * V_HEAD], axis=-1, keepdims=True)
            do_b = do_h.astype(BF16)
            lse_h = l_ref[:, hh:hh + 1]

            def step(kb, carry):
                dn, dp_ = carry
                rows = pl.ds(pl.multiple_of(kb * t, t), t)
                k = kv_ref[rows, kc:kc + QK_NOPE]
                kp = kp_ref[rows, :]
                p = jnp.exp(_scores(q_n, q_p, k, kp, qi, kb, t) - lse_h)
                dpv = lax.dot_general(do_b, kv_ref[rows, vc:vc + V_HEAD], (((1,), (1,)), ((), ())), preferred_element_type=F32)
                ds = (p * (dpv - delta) * _ATT_SCALE).astype(BF16)
                dn = dn + jnp.dot(ds, k, preferred_element_type=F32)
                dp_ = dp_ + jnp.dot(ds, kp, preferred_element_type=F32)
                return dn, dp_

            dn, dp_h = lax.fori_loop(0, qi + 1, step, (jnp.zeros((t, QK_NOPE), F32), jnp.zeros((t, LANES), F32)))
            dqn_ref[:, hh * QK_NOPE:(hh + 1) * QK_NOPE] = dn.astype(dqn_ref.dtype)
            dqp = dqp + jnp.where(masks[hh], dp_h, jnp.zeros_like(dp_h))
        dqp_ref[...] = _rope(dqp, c_ref[...], -s_ref[...]).astype(dqp_ref.dtype)

    qblk = pl.BlockSpec((t, 2 * QK_NOPE), lambda h, i: (i, h))
    pblk = pl.BlockSpec((t, LANES), lambda h, i: (i, h))
    tab = pl.BlockSpec((t, LANES), lambda h, i: (i, 0))
    return pl.pallas_call(
        body, name="attn_bwd_q", grid=(hp, nq),
        in_specs=[qblk, pblk, pl.BlockSpec((s, 4 * QK_NOPE), lambda h, i: (0, h)), _full((s, LANES)), qblk, qblk,
                  pl.BlockSpec((None, t, 2), lambda h, i: (h, i, 0)), tab, tab],
        out_specs=[qblk, pblk],
        out_shape=[jax.ShapeDtypeStruct((s, HEADS * QK_NOPE), BF16), jax.ShapeDtypeStruct((s, HEADS * QK_ROPE), BF16)],
        compiler_params=_params("parallel", "parallel"))(qn, qpr, kv, kpr, o, do, lse, cos4, sin4)


def attn_bwd_kv(qn, qpr, kv, kpr, o, do, lse):
    s = qn.shape[0]
    hp = HEADS // 2
    t = _tile(s, ATT_TILE)
    nq = s // t

    def body(qn_ref, qpr_ref, kv_ref, kp_ref, o_ref, do_ref, l_ref, dkv_ref, dkp_ref):
        ki = pl.program_id(1)
        rows_k = pl.ds(pl.multiple_of(ki * t, t), t)
        kp = kp_ref[rows_k, :]
        dkp = jnp.zeros((t, LANES), F32)
        for hh in range(2):
            kc, vc = 2 * hh * QK_NOPE, (2 * hh + 1) * QK_NOPE
            k = kv_ref[rows_k, kc:kc + QK_NOPE]
            v = kv_ref[rows_k, vc:vc + V_HEAD]

            def step(qb, carry):
                dk, dv, dkp_h = carry
                rows = pl.ds(pl.multiple_of(qb * t, t), t)
                q_n = qn_ref[rows, hh * QK_NOPE:(hh + 1) * QK_NOPE]
                qpr = qpr_ref[rows, :]
                lane = lax.broadcasted_iota(jnp.int32, qpr.shape, 1)
                sel = (lane < QK_ROPE) if hh == 0 else (lane >= QK_ROPE)
                q_p = jnp.where(sel, qpr, jnp.zeros_like(qpr))
                do_h = do_ref[rows, hh * V_HEAD:(hh + 1) * V_HEAD]
                delta = jnp.sum(do_h * o_ref[rows, hh * V_HEAD:(hh + 1) * V_HEAD], axis=-1, keepdims=True)
                do_b = do_h.astype(BF16)
                p = jnp.exp(_scores(q_n, q_p, k, kp, qb, ki, t) - l_ref[rows, hh:hh + 1])
                dpv = lax.dot_general(do_b, v, (((1,), (1,)), ((), ())), preferred_element_type=F32)
                ds = (p * (dpv - delta) * _ATT_SCALE).astype(BF16)
                dv = dv + lax.dot_general(p.astype(BF16), do_b, (((0,), (0,)), ((), ())), preferred_element_type=F32)
                dk = dk + lax.dot_general(ds, q_n, (((0,), (0,)), ((), ())), preferred_element_type=F32)
                dkp_h = dkp_h + lax.dot_general(ds, q_p, (((0,), (0,)), ((), ())), preferred_element_type=F32)
                return dk, dv, dkp_h

            init = (jnp.zeros((t, QK_NOPE), F32), jnp.zeros((t, V_HEAD), F32), jnp.zeros((t, LANES), F32))
            dk, dv, dkp_h = lax.fori_loop(ki, nq, step, init)
            dkv_ref[:, kc:kc + QK_NOPE] = dk.astype(dkv_ref.dtype)
            dkv_ref[:, vc:vc + V_HEAD] = dv.astype(dkv_ref.dtype)
            dkp = dkp + dkp_h
        dkp_ref[...] = dkp

    return pl.pallas_call(
        body, name="attn_bwd_kv", grid=(hp, nq),
        in_specs=[pl.BlockSpec((s, 2 * QK_NOPE), lambda h, i: (0, h)), pl.BlockSpec((s, LANES), lambda h, i: (0, h)),
                  pl.BlockSpec((s, 4 * QK_NOPE), lambda h, i: (0, h)), _full((s, LANES)),
                  pl.BlockSpec((s, 2 * V_HEAD), lambda h, i: (0, h)), pl.BlockSpec((s, 2 * V_HEAD), lambda h, i: (0, h)),
                  pl.BlockSpec((None, s, 2), lambda h, i: (h, 0, 0))],
        out_specs=[pl.BlockSpec((t, 4 * QK_NOPE), lambda h, i: (i, h)), pl.BlockSpec((None, t, LANES), lambda h, i: (h, i, 0))],
        out_shape=[jax.ShapeDtypeStruct((s, HEADS * 2 * QK_NOPE), BF16), jax.ShapeDtypeStruct((hp, s, LANES), F32)],
        compiler_params=_params("parallel", "parallel"))(qn, qpr, kv, kpr, o, do, lse)


def _dot_nt(a, b):
    return lax.dot_general(a, b, (((1,), (1,)), ((), ())), preferred_element_type=F32)


def _dot_tn(a, b):
    return lax.dot_general(a, b, (((0,), (0,)), ((), ())), preferred_element_type=F32)


def _q_cat(q_n, qpr, hh):
    lane = lax.broadcasted_iota(jnp.int32, qpr.shape, 1)
    sel = (lane < QK_ROPE) if hh == 0 else (lane >= QK_ROPE)
    return jnp.concatenate([q_n, jnp.where(sel, qpr, jnp.zeros_like(qpr))], axis=1)


def _causal(sc):
    row = lax.broadcasted_iota(jnp.int32, sc.shape, 0)
    col = lax.broadcasted_iota(jnp.int32, sc.shape, 1)
    return jnp.where(col <= row, sc, _NEG)


def attn_fwd2(qn, qp, kv, kpr, cos4, sin4):
    s = qn.shape[0]
    hp = HEADS // 2
    t = _tile(s, ATT_TILE)
    nq = s // t

    def body(qn_ref, qp_ref, kv_ref, kp_ref, c_ref, s_ref, o_ref, qpr_ref, l_ref, kcat_ref):
        qi = pl.program_id(1)

        @pl.when(qi == 0)
        def _():
            for hh in range(2):
                kcat_ref[hh, :, 0:QK_NOPE] = kv_ref[:, 2 * hh * QK_NOPE:(2 * hh + 1) * QK_NOPE]
                kcat_ref[hh, :, QK_NOPE:] = kp_ref[...]

        qpr = _rope(qp_ref[...], c_ref[...], s_ref[...]).astype(BF16)
        qpr_ref[...] = qpr
        qcat = [_q_cat(qn_ref[:, hh * QK_NOPE:(hh + 1) * QK_NOPE], qpr, hh) for hh in range(2)]

        def block(kb, carry, diagonal):
            rows = pl.ds(pl.multiple_of(kb * t, t), t)
            out = []
            for hh in range(2):
                m, l, acc = carry[hh]
                sc = _dot_nt(qcat[hh], kcat_ref[hh, rows, :]) * _ATT_SCALE
                if diagonal:
                    sc = _causal(sc)
                m_new = jnp.maximum(m, jnp.max(sc, axis=-1, keepdims=True))
                alpha = jnp.exp(m - m_new)
                p = jnp.exp(sc - m_new)
                l = alpha * l + jnp.sum(p, axis=-1, keepdims=True)
                v = kv_ref[rows, (2 * hh + 1) * QK_NOPE:(2 * hh + 2) * QK_NOPE]
                acc = alpha * acc + jnp.dot(p.astype(BF16), v, preferred_element_type=F32)
                out.append((m_new, l, acc))
            return tuple(out)

        one = (jnp.full((t, 1), _NEG, F32), jnp.zeros((t, 1), F32), jnp.zeros((t, V_HEAD), F32))
        carry = lax.fori_loop(0, qi, lambda kb, cr: block(kb, cr, False), (one, one))
        carry = block(qi, carry, True)
        for hh in range(2):
            m, l, acc = carry[hh]
            o_ref[:, hh * V_HEAD:(hh + 1) * V_HEAD] = acc / l
            l_ref[:, hh:hh + 1] = m + jnp.log(l)

    return pl.pallas_call(
        body, name="attn_fwd", grid=(hp, nq),
        in_specs=[pl.BlockSpec((t, 2 * QK_NOPE), lambda h, i: (i, h)), pl.BlockSpec((t, LANES), lambda h, i: (i, h)),
                  pl.BlockSpec((s, 4 * QK_NOPE), lambda h, i: (0, h)), _full((s, LANES)),
                  pl.BlockSpec((t, LANES), lambda h, i: (i, 0)), pl.BlockSpec((t, LANES), lambda h, i: (i, 0))],
        out_specs=[pl.BlockSpec((t, 2 * V_HEAD), lambda h, i: (i, h)), pl.BlockSpec((t, LANES), lambda h, i: (i, h)),
                   pl.BlockSpec((None, t, 2), lambda h, i: (h, i, 0))],
        out_shape=[jax.ShapeDtypeStruct((s, HEADS * V_HEAD), F32), jax.ShapeDtypeStruct((s, HEADS * QK_ROPE), BF16),
                   jax.ShapeDtypeStruct((hp, s, 2), F32)],
        scratch_shapes=[pltpu.VMEM((2, s, 2 * QK_NOPE), BF16)],
        compiler_params=_params("parallel", "arbitrary"))(qn, qp, kv, kpr, cos4, sin4)


def attn_bwd2(qn, qpr, kv, kpr, o, do, lse, cos4, sin4):
    s = qn.shape[0]
    hp = HEADS // 2
    t = _tile(s, ATT_TILE)
    nk = s // t

    def body(qn_ref, qpr_ref, kv_ref, kp_ref, o_ref, do_ref, l_ref, c_ref, s_ref,
             dqn_ref, dqp_ref, dkv_ref, dkp_ref, qcat_ref, dq_ref, delta_ref):
        ki = pl.program_id(1)

        @pl.when(ki == 0)
        def _():
            dq_ref[...] = jnp.zeros_like(dq_ref)
            for hh in range(2):
                qcat_ref[hh] = _q_cat(qn_ref[:, hh * QK_NOPE:(hh + 1) * QK_NOPE], qpr_ref[...], hh)
                cols = slice(hh * V_HEAD, (hh + 1) * V_HEAD)
                delta_ref[hh] = jnp.sum(do_ref[:, cols] * o_ref[:, cols], axis=-1, keepdims=True)

        rows_k = pl.ds(pl.multiple_of(ki * t, t), t)
        kcat = [jnp.concatenate([kv_ref[rows_k, 2 * hh * QK_NOPE:(2 * hh + 1) * QK_NOPE], kp_ref[rows_k, :]], axis=1) for hh in range(2)]
        vs = [kv_ref[rows_k, (2 * hh + 1) * QK_NOPE:(2 * hh + 2) * QK_NOPE] for hh in range(2)]

        def block(qb, carry, diagonal):
            rows = pl.ds(pl.multiple_of(qb * t, t), t)
            out = []
            for hh in range(2):
                dkc, dv = carry[hh]
                q_c = qcat_ref[hh, rows, :]
                do_b = do_ref[rows, hh * V_HEAD:(hh + 1) * V_HEAD].astype(BF16)
                sc = _dot_nt(q_c, kcat[hh]) * _ATT_SCALE
                if diagonal:
                    sc = _causal(sc)
                p = jnp.exp(sc - l_ref[rows, hh:hh + 1])
                dpv = _dot_nt(do_b, vs[hh])
                ds = (p * (dpv - delta_ref[hh, rows, :]) * _ATT_SCALE).astype(BF16)
                dv = dv + _dot_tn(p.astype(BF16), do_b)
                dkc = dkc + _dot_tn(ds, q_c)
                dq_ref[hh, rows, :] += jnp.dot(ds, kcat[hh], preferred_element_type=F32)
                out.append((dkc, dv))
            return tuple(out)

        one = (jnp.zeros((t, 2 * QK_NOPE), F32), jnp.zeros((t, V_HEAD), F32))
        carry = block(ki, (one, one), True)
        carry = lax.fori_loop(ki + 1, nk, lambda qb, cr: block(qb, cr, False), carry)
        dkp = jnp.zeros((t, LANES), F32)
        for hh in range(2):
            dkc, dv = carry[hh]
            dkv_ref[:, 2 * hh * QK_NOPE:(2 * hh + 1) * QK_NOPE] = dkc[:, :QK_NOPE].astype(dkv_ref.dtype)
            dkv_ref[:, (2 * hh + 1) * QK_NOPE:(2 * hh + 2) * QK_NOPE] = dv.astype(dkv_ref.dtype)
            dkp = dkp + dkc[:, QK_NOPE:]
        dkp_ref[...] = dkp

        @pl.when(ki == nk - 1)
        def _():
            lane = lax.broadcasted_iota(jnp.int32, (s, LANES), 1)
            dqp = jnp.where(lane < QK_ROPE, dq_ref[0, :, QK_NOPE:], dq_ref[1, :, QK_NOPE:])
            dqp_ref[...] = _rope(dqp, c_ref[...], -s_ref[...]).astype(dqp_ref.dtype)
            for hh in range(2):
                dqn_ref[:, hh * QK_NOPE:(hh + 1) * QK_NOPE] = dq_ref[hh, :, :QK_NOPE].astype(dqn_ref.dtype)

    qblk = pl.BlockSpec((s, 2 * QK_NOPE), lambda h, i: (0, h))
    pblk = pl.BlockSpec((s, LANES), lambda h, i: (0, h))
    tab = _full((s, LANES))
    return pl.pallas_call(
        body, name="attn_bwd", grid=(hp, nk),
        in_specs=[qblk, pblk, pl.BlockSpec((s, 4 * QK_NOPE), lambda h, i: (0, h)), tab, qblk, qblk,
                  pl.BlockSpec((None, s, 2), lambda h, i: (h, 0, 0)), tab, tab],
        out_specs=[qblk, pblk, pl.BlockSpec((t, 4 * QK_NOPE), lambda h, i: (i, h)), pl.BlockSpec((None, t, LANES), lambda h, i: (h, i, 0))],
        out_shape=[jax.ShapeDtypeStruct((s, HEADS * QK_NOPE), BF16), jax.ShapeDtypeStruct((s, HEADS * QK_ROPE), BF16),
                   jax.ShapeDtypeStruct((s, HEADS * 2 * QK_NOPE), BF16), jax.ShapeDtypeStruct((hp, s, LANES), F32)],
        scratch_shapes=[pltpu.VMEM((2, s, 2 * QK_NOPE), BF16), pltpu.VMEM((2, s, 2 * QK_NOPE), F32), pltpu.VMEM((2, s, 1), F32)],
        compiler_params=_params("parallel", "arbitrary"))(qn, qpr, kv, kpr, o, do, lse, cos4, sin4)


def kpe_bwd(dkp, cos4, sin4, pad_cols):
    hp, s, _ = dkp.shape
    tr = _tile(s, ROW_TILE * 2)

    def body(d_ref, c_ref, s_ref, o_ref):
        tot = d_ref[0]
        for h in range(1, hp):
            tot = tot + d_ref[h]
        tot = tot + pltpu.roll(tot, QK_ROPE, 1)
        lane = lax.broadcasted_iota(jnp.int32, tot.shape, 1)
        dk = jnp.where(lane < QK_ROPE, _rope(tot, c_ref[...], -s_ref[...]), jnp.zeros_like(tot))
        o_ref[...] = jnp.zeros_like(o_ref)
        o_ref[:, 0:LANES] = dk.astype(o_ref.dtype)

    row = pl.BlockSpec((tr, LANES), lambda i: (i, 0))
    return pl.pallas_call(body, name="kpe_bwd", grid=(s // tr,),
                          in_specs=[pl.BlockSpec((hp, tr, LANES), lambda i: (0, i, 0)), row, row],
                          out_specs=pl.BlockSpec((tr, pad_cols), lambda i: (i, 0)),
                          out_shape=jax.ShapeDtypeStruct((s, pad_cols), BF16), compiler_params=_params("parallel"))(dkp, cos4, sin4)


def _shift_down(x, n):
    row = lax.broadcasted_iota(jnp.int32, x.shape, 0)
    return jnp.where(row >= n, pltpu.roll(x, n, 0), jnp.zeros_like(x))


def _shift_up(x, n):
    rows = x.shape[0]
    row = lax.broadcasted_iota(jnp.int32, x.shape, 0)
    return jnp.where(row < rows - n, pltpu.roll(x, rows - n, 0), jnp.zeros_like(x))


def _conv(x, w_ref, b_ref):
    return w_ref[2:3, :] * x + w_ref[1:2, :] * _shift_down(x, 1) + w_ref[0:1, :] * _shift_down(x, 2) + b_ref[...]


def conv_act_fwd(upre, conv_w, conv_b):
    s, f2 = upre.shape
    f = f2 // 2
    tc = _tile(f, COL_TILE)
    nc = f // tc

    def body(ug_ref, uv_ref, wg_ref, wv_ref, bg_ref, bv_ref, o_ref):
        gh = _conv(ug_ref[...], wg_ref, bg_ref)
        vh = _conv(uv_ref[...], wv_ref, bv_ref)
        o_ref[...] = (gh * _sigmoid(gh) * vh).astype(o_ref.dtype)

    def spec(rows, shift):
        return pl.BlockSpec((rows, tc), lambda j: (0, j + shift))

    return pl.pallas_call(
        body, name="conv_act_fwd", grid=(nc,),
        in_specs=[spec(s, 0), spec(s, nc), spec(3, 0), spec(3, nc), spec(1, 0), spec(1, nc)], out_specs=spec(s, 0),
        out_shape=jax.ShapeDtypeStruct((s, f), BF16), compiler_params=_params("parallel"))(upre, upre, conv_w, conv_w, conv_b, conv_b)


def conv_act_bwd(upre, conv_w, conv_b, df):
    s, f2 = upre.shape
    f = f2 // 2
    tc = _tile(f, COL_TILE)
    nc = f // tc

    def half(x, d, w_ref, du_ref, which, gw_ref, gb_ref):
        d1, d2 = _shift_up(d, 1), _shift_up(d, 2)
        gb_ref[...] = _colsum(d)
        gw_ref[2:3, :] = _colsum(d * x)
        gw_ref[1:2, :] = _colsum(d1 * x)
        gw_ref[0:1, :] = _colsum(d2 * x)
        du_ref[which] = (w_ref[2:3, :] * d + w_ref[1:2, :] * d1 + w_ref[0:1, :] * d2).astype(du_ref.dtype)

    def body(ug_ref, uv_ref, wg_ref, wv_ref, bg_ref, bv_ref, df_ref, du_ref, gwg_ref, gwv_ref, gbg_ref, gbv_ref):
        xg, xv = ug_ref[...], uv_ref[...]
        gh = _conv(xg, wg_ref, bg_ref)
        vh = _conv(xv, wv_ref, bv_ref)
        sg = _sigmoid(gh)
        df_v = df_ref[...]
        half(xg, df_v * vh * (sg * (1.0 + gh * (1.0 - sg))), wg_ref, du_ref, 0, gwg_ref, gbg_ref)
        half(xv, df_v * (gh * sg), wv_ref, du_ref, 1, gwv_ref, gbv_ref)

    def spec(rows, shift):
        return pl.BlockSpec((rows, tc), lambda j: (0, j + shift))

    gw = jax.ShapeDtypeStruct((3, f), F32)
    gb = jax.ShapeDtypeStruct((1, f), F32)
    return pl.pallas_call(
        body, name="conv_act_bwd", grid=(nc,),
        in_specs=[spec(s, 0), spec(s, nc), spec(3, 0), spec(3, nc), spec(1, 0), spec(1, nc), spec(s, 0)],
        out_specs=[pl.BlockSpec((2, s, tc), lambda j: (0, 0, j)), spec(3, 0), spec(3, 0), spec(1, 0), spec(1, 0)],
        out_shape=[jax.ShapeDtypeStruct((2, s, f), BF16), gw, gw, gb, gb],
        compiler_params=_params("parallel"))(upre, upre, conv_w, conv_w, conv_b, conv_b, df)


def _elementwise_tile(r, c, limit):
    if r % 8:
        return r, c
    best = (8, c if c % LANES else LANES)
    for k in (1, 2, 4, 8, 16):
        if k > 1 and c % (LANES * k):
            continue
        tc = c // k
        tr = max(8, min(r, limit // tc) // 8 * 8)
        while r % tr:
            tr -= 8
        if tr * tc <= max(limit, 8 * tc) and tr * tc > best[0] * best[1]:
            best = (tr, tc)
    return best


def adamw(name, w, m, v, parts):
    npart, r, c = parts.shape
    tr, tc = _elementwise_tile(r, c, ADAMW_TILE_ELEMS)
    bc1 = 1.0 - ADAM_B1 ** ADAM_STEP
    bc2 = 1.0 - ADAM_B2 ** ADAM_STEP

    def body(w_ref, m_ref, v_ref, p_ref, g_ref, d_ref, nm_ref, nv_ref):
        g = p_ref[0].astype(F32)
        for k in range(1, npart):
            g = g + p_ref[k].astype(F32)
        m_new = ADAM_B1 * m_ref[...] + (1.0 - ADAM_B1) * g
        v_new = ADAM_B2 * v_ref[...] + (1.0 - ADAM_B2) * (g * g)
        g_ref[...] = g
        nm_ref[...] = m_new
        nv_ref[...] = v_new
        d_ref[...] = -ADAM_LR * ((m_new / bc1) / (jnp.sqrt(v_new / bc2) + ADAM_EPS) + ADAM_WD * w_ref[...])

    deps = _TOKENS.take()
    blk = pl.BlockSpec((tr, tc), lambda i, j: (i, j))
    out = jax.ShapeDtypeStruct((r, c), F32)
    return pl.pallas_call(
        lambda *refs: body(*refs[:4], *refs[4 + len(deps):]), name=name, grid=(r // tr, c // tc),
        in_specs=[blk, blk, blk, pl.BlockSpec((npart, tr, tc), lambda i, j: (0, i, j))] + [pl.BlockSpec(memory_space=pl.ANY)] * len(deps),
        out_specs=[blk, blk, blk, blk], out_shape=[out, out, out, out],
        compiler_params=_params("parallel", "parallel"))(w, m, v, parts, *deps)


def _position():
    return lax.axis_index("x"), lax.axis_index("y"), lax.axis_index("c")


def _index(p):
    return 4 * p[0] + 2 * p[1] + p[2]


def _peer(me, r):
    return (me[0] ^ ((r >> 2) & 1), me[1] ^ ((r >> 1) & 1), me[2] ^ (r & 1))


_ANY = pl.BlockSpec(memory_space=pl.ANY)


def all_gather_two_level(shards):
    n = len(shards)

    def body(*refs):
        ins, outs = refs[:n], refs[n:2 * n]
        send_sems, recv_sems, local_sems = refs[2 * n:]
        x, y, c = _position()
        me, sibling = (x, y, c), (x, y, 1 - c)
        chips = [(1 - x, y), (x, 1 - y), (1 - x, 1 - y)]

        def copy(w, k, block, to, src=None):
            slot = outs[w].at[_index(block)]
            return pltpu.make_async_remote_copy(src_ref=slot if src is None else src, dst_ref=slot,
                                                send_sem=send_sems.at[7 * w + k], recv_sem=recv_sems.at[7 * w + k],
                                                device_id=to, device_id_type=MESH)

        mine = [pltpu.make_async_copy(ins[w], outs[w].at[_index(me)], local_sems.at[w]) for w in range(n)]
        for cp in mine:
            cp.start()
        first = []
        for w in range(n):
            first.append(copy(w, 0, me, sibling, src=ins[w]))
            first += [copy(w, 1 + j, me, (*chip, c), src=ins[w]) for j, chip in enumerate(chips)]
        for cp in first:
            cp.start()
        passed = []
        for w in range(n):
            for j, chip in enumerate(chips):
                copy(w, 1 + j, (*chip, c), me).wait_recv()
                cp = copy(w, 4 + j, (*chip, c), sibling)
                cp.start()
                passed.append(cp)
        for w in range(n):
            copy(w, 0, sibling, me).wait_recv()
            for j, chip in enumerate(chips):
                copy(w, 4 + j, (*chip, 1 - c), me).wait_recv()
        for cp in first + passed:
            cp.wait_send()
        for cp in mine:
            cp.wait()

    return pl.pallas_call(
        body, name="all_gather_weights",
        out_shape=[jax.ShapeDtypeStruct((N_DEV,) + a.shape, a.dtype) for a in shards],
        in_specs=[_ANY] * n, out_specs=[_ANY] * n,
        scratch_shapes=[pltpu.SemaphoreType.DMA((7 * n,)), pltpu.SemaphoreType.DMA((7 * n,)), pltpu.SemaphoreType.DMA((n,))],
        )(*shards)


def exchange(name, arrays, scatter):
    n = len(arrays)

    def body(*refs):
        ins, outs = refs[:n], refs[n:2 * n]
        send_sems, recv_sems, local_sems = refs[2 * n:]
        me = _position()
        copies = []
        for w in range(n):
            src = ins[w].at[_index(me)] if scatter else ins[w]
            cp = pltpu.make_async_copy(src, outs[w].at[_index(me)], local_sems.at[w])
            cp.start()
            copies.append(cp)
        remote = []
        for w in range(n):
            for r in range(1, N_DEV):
                peer = _peer(me, r)
                src = ins[w].at[_index(peer)] if scatter else ins[w]
                cp = pltpu.make_async_remote_copy(src_ref=src, dst_ref=outs[w].at[_index(me)],
                                                  send_sem=send_sems.at[7 * w + r - 1], recv_sem=recv_sems.at[7 * w + r - 1],
                                                  device_id=peer, device_id_type=MESH)
                cp.start()
                remote.append(cp)
        for cp in remote:
            cp.wait()
        for cp in copies:
            cp.wait()

    blocks = [a.shape[1:] if scatter else a.shape for a in arrays]
    return pl.pallas_call(
        body, name=name,
        out_shape=[jax.ShapeDtypeStruct((N_DEV,) + b, a.dtype) for a, b in zip(arrays, blocks)],
        in_specs=[_ANY] * n, out_specs=[_ANY] * n,
        scratch_shapes=[pltpu.SemaphoreType.DMA((7 * n,)), pltpu.SemaphoreType.DMA((7 * n,)), pltpu.SemaphoreType.DMA((n,))],
        )(*arrays)


_HBM = pl.BlockSpec(memory_space=pltpu.HBM)
_SEM = pl.BlockSpec(memory_space=pltpu.SEMAPHORE)
_EFFECT = pltpu.SideEffectType.DATAFLOW_SIDE_EFFECTING


def _direct_copies(ins, lands, send_sems, recv_sems, scatter):
    me = _position()
    copies = []
    for w in range(len(ins)):
        for r in range(1, N_DEV):
            peer = _peer(me, r)
            src = ins[w].at[_index(peer)] if scatter else ins[w]
            copies.append(pltpu.make_async_remote_copy(src_ref=src, dst_ref=lands[w].at[_index(me)], send_sem=send_sems.at[7 * w + r - 1],
                                                       recv_sem=recv_sems.at[7 * w + r - 1], device_id=peer, device_id_type=MESH))
    return copies


def exchange_start(name, groups, scatter):
    arrays = [a for g in groups for a in g]
    n = len(arrays)
    blocks = [a.shape[1:] if scatter else a.shape for a in arrays]
    lands = [lax.empty((N_DEV,) + b, a.dtype) for a, b in zip(arrays, blocks)]
    ng = len(groups)

    def body(*refs):
        ins, lnd = refs[:n], refs[n:2 * n]
        sems = refs[2 * n:2 * n + 2 * ng]
        token = refs[2 * n + 2 * ng + 2 * n]
        local_sem = refs[2 * n + 2 * ng + 2 * n + 1]
        me = _position()
        local = []
        for w in range(n):
            src = ins[w].at[_index(me)] if scatter else ins[w]
            cp = pltpu.make_async_copy(src, lnd[w].at[_index(me)], local_sem.at[w])
            cp.start()
            local.append(cp)
        w0 = 0
        for gi, g in enumerate(groups):
            for cp in _direct_copies(ins[w0:w0 + len(g)], lnd[w0:w0 + len(g)], sems[2 * gi], sems[2 * gi + 1], scatter):
                cp.start()
            w0 += len(g)
        for cp in local:
            cp.wait()
        token[...] = jnp.zeros_like(token)

    sem_shapes = []
    for g in groups:
        sem_shapes += [pltpu.SemaphoreType.DMA((7 * len(g),)), pltpu.SemaphoreType.DMA((7 * len(g),))]
    out = pl.pallas_call(
        body, name=name,
        out_shape=tuple(sem_shapes) + tuple(pltpu.HBM(a.shape, a.dtype) for a in arrays) + tuple(pltpu.HBM(l.shape, l.dtype) for l in lands)
        + (jax.ShapeDtypeStruct((8, LANES), F32),),
        in_specs=[_HBM] * (2 * n), out_specs=tuple([_SEM] * (2 * ng) + [_HBM] * (2 * n) + [pl.BlockSpec(memory_space=pltpu.VMEM)]),
        input_output_aliases={i: 2 * ng + i for i in range(2 * n)},
        scratch_shapes=[pltpu.SemaphoreType.DMA((n,))],
        compiler_params=pltpu.CompilerParams(has_side_effects=_EFFECT),
    )(*[pltpu.with_memory_space_constraint(a, pltpu.HBM) for a in arrays], *[pltpu.with_memory_space_constraint(l, pltpu.HBM) for l in lands])
    sems, thru, token = out[:2 * ng], out[2 * ng:2 * ng + 2 * n], out[-1]
    res, w0 = [], 0
    for gi, g in enumerate(groups):
        res.append((sems[2 * gi], sems[2 * gi + 1], list(thru[w0:w0 + len(g)]), list(thru[n + w0:n + w0 + len(g)])))
        w0 += len(g)
    return res, token


def exchange_wait(name, group, after, scatter):
    send_sems, recv_sems, srcs, lands = group
    n = len(srcs)

    def body(*refs):
        ins, lnd = refs[:n], refs[n:2 * n]
        for cp in _direct_copies(ins, lnd, refs[2 * n], refs[2 * n + 1], scatter):
            cp.wait_send()
            cp.wait_recv()

    out = pl.pallas_call(
        body, name=name, out_shape=tuple(pltpu.HBM(a.shape, a.dtype) for a in srcs + lands),
        in_specs=[_HBM] * (2 * n) + [_SEM, _SEM, pl.BlockSpec(memory_space=pl.ANY)], out_specs=tuple([_HBM] * (2 * n)),
        input_output_aliases={i: i for i in range(2 * n)},
        compiler_params=pltpu.CompilerParams(has_side_effects=_EFFECT),
    )(*srcs, *lands, send_sems, recv_sems, after)
    return list(out[n:])


def _after(x, token):
    return lax.optimization_barrier((x, token))[0]


_TOKEN = jax.ShapeDtypeStruct((8, LANES), F32)
_VM = pl.BlockSpec(memory_space=pltpu.VMEM)
_SIDE = pltpu.CompilerParams(has_side_effects=_EFFECT)


def _hbm(a):
    return pltpu.with_memory_space_constraint(a, pltpu.HBM)


def _like(a):
    return pltpu.HBM(a.shape, a.dtype)


def _dma_sems(n):
    return pltpu.SemaphoreType.DMA((n,))


def _other_chips(x, y):
    return [(1 - x, y), (x, 1 - y), (1 - x, 1 - y)]


COPY_STREAMS = 8


def _row_chunks(src, dst):
    rows = src.shape[0]
    n = COPY_STREAMS
    while n > 1 and rows % (16 * n):
        n //= 2
    r = rows // n
    return [(src.at[pl.ds(i * r, r)], dst.at[pl.ds(i * r, r)]) for i in range(n)]


def _local_copy(src, dst, sem):
    return [pltpu.make_async_copy(s, d, sem) for s, d in _row_chunks(src, dst)]


class _rcopy:
    def __init__(self, src, dst, send_sem, recv_sem, to):
        self.parts = [pltpu.make_async_remote_copy(src_ref=s, dst_ref=d, send_sem=send_sem, recv_sem=recv_sem, device_id=to, device_id_type=MESH)
                      for s, d in _row_chunks(src, dst)]

    def start(self):
        for cp in self.parts:
            cp.start()

    def wait_send(self):
        for cp in self.parts:
            cp.wait_send()

    def wait_recv(self):
        for cp in self.parts:
            cp.wait_recv()


def ag_start(name, shards, after):
    n = len(shards)
    lands = [lax.empty((N_DEV,) + a.shape, a.dtype) for a in shards]

    def body(*refs):
        ins, lnd, send_sems, recv_sems, token = refs[:n], refs[n:2 * n], refs[2 * n + 1], refs[2 * n + 2], refs[4 * n + 3]
        x, y, c = _position()
        for w in range(n):
            slot = lnd[w].at[_index((x, y, c))]
            for k, to in enumerate([(x, y, 1 - c)] + [(*chip, c) for chip in _other_chips(x, y)]):
                _rcopy(ins[w], slot, send_sems.at[4 * w + k], recv_sems.at[4 * w + k], to).start()
        token[...] = jnp.zeros_like(token)

    out = pl.pallas_call(
        body, name=name, out_shape=(_dma_sems(4 * n), _dma_sems(4 * n)) + tuple(_like(a) for a in shards + lands) + (_TOKEN,),
        in_specs=[_HBM] * (2 * n) + [_ANY], out_specs=(_SEM, _SEM) + (_HBM,) * (2 * n) + (_VM,),
        input_output_aliases={i: 2 + i for i in range(2 * n)}, compiler_params=_SIDE)(*[_hbm(a) for a in shards + lands], after)
    _TOKENS.push(out[-1])
    return out[0], out[1], list(out[2:2 + n]), list(out[2 + n:2 + 2 * n])


def ag_forward(name, started, after):
    send, recv, shards, lands = started
    n = len(shards)
    afters = list(after) if isinstance(after, (list, tuple)) else [after]
    na = len(afters)

    def body(*refs):
        ins, lnd, send_sems, recv_sems = refs[:n], refs[n:2 * n], refs[2 * n], refs[2 * n + 1]
        fsend, frecv, token = refs[2 * n + 2 + na], refs[2 * n + 3 + na], refs[4 * n + 4 + na]
        x, y, c = _position()
        for w in range(n):
            for j, chip in enumerate(_other_chips(x, y)):
                slot = lnd[w].at[_index((*chip, c))]
                _rcopy(ins[w], slot, send_sems.at[4 * w + 1 + j], recv_sems.at[4 * w + 1 + j], (*chip, c)).wait_recv()
                _rcopy(slot, slot, fsend.at[3 * w + j], frecv.at[3 * w + j], (x, y, 1 - c)).start()
        token[...] = jnp.zeros_like(token)

    out = pl.pallas_call(
        body, name=name, out_shape=(_dma_sems(3 * n), _dma_sems(3 * n)) + tuple(_like(a) for a in shards + lands) + (_TOKEN,),
        in_specs=[_HBM] * (2 * n) + [_SEM, _SEM] + [_ANY] * na, out_specs=(_SEM, _SEM) + (_HBM,) * (2 * n) + (_VM,),
        input_output_aliases={i: 2 + i for i in range(2 * n)}, compiler_params=_SIDE)(*shards, *lands, send, recv, *afters)
    _TOKENS.push(out[-1])
    return send, recv, out[0], out[1], list(out[2:2 + n]), list(out[2 + n:2 + 2 * n])


def ag_wait(name, forwarded, after):
    send, recv, fsend, frecv, shards, lands = forwarded
    n = len(shards)

    def body(*refs):
        ins, lnd, send_sems, recv_sems, fsend_r, frecv_r = refs[:n], refs[n:2 * n], refs[2 * n], refs[2 * n + 1], refs[2 * n + 2], refs[2 * n + 3]
        x, y, c = _position()
        sibling = (x, y, 1 - c)
        for w in range(n):
            own = lnd[w].at[_index((x, y, c))]
            _rcopy(ins[w], lnd[w].at[_index(sibling)], send_sems.at[4 * w], recv_sems.at[4 * w], sibling).wait_recv()
            for j, chip in enumerate(_other_chips(x, y)):
                _rcopy(ins[w], lnd[w].at[_index((*chip, 1 - c))], fsend_r.at[3 * w + j], frecv_r.at[3 * w + j], sibling).wait_recv()
            for k in range(4):
                _rcopy(ins[w], own, send_sems.at[4 * w + k], recv_sems.at[4 * w + k], sibling).wait_send()
            for j in range(3):
                _rcopy(ins[w], own, fsend_r.at[3 * w + j], frecv_r.at[3 * w + j], sibling).wait_send()

    out = pl.pallas_call(
        body, name=name, out_shape=tuple(_like(a) for a in shards + lands), in_specs=[_HBM] * (2 * n) + [_SEM] * 4 + [_ANY],
        out_specs=(_HBM,) * (2 * n), input_output_aliases={i: i for i in range(2 * n)},
        compiler_params=_SIDE)(*shards, *lands, send, recv, fsend, frecv, after)
    return [lax.dynamic_update_index_in_dim(land, shard, _index(_position()), 0) for shard, land in zip(out[:n], out[n:])]


def rs_d2d_start(name, grads):
    n = len(grads)
    lands = [lax.empty((4,) + g.shape[1:], g.dtype) for g in grads]

    def body(*refs):
        ins, lnd, send_sems, recv_sems, token = refs[:n], refs[n:2 * n], refs[2 * n], refs[2 * n + 1], refs[4 * n + 2]
        x, y, c = _position()
        for w in range(n):
            for i in range(4):
                _rcopy(ins[w].at[2 * i + 1 - c], lnd[w].at[i], send_sems.at[4 * w + i], recv_sems.at[4 * w + i], (x, y, 1 - c)).start()
        token[...] = jnp.zeros_like(token)

    out = pl.pallas_call(
        body, name=name, out_shape=(_dma_sems(4 * n), _dma_sems(4 * n)) + tuple(_like(a) for a in grads + lands) + (_TOKEN,),
        in_specs=[_HBM] * (2 * n), out_specs=(_SEM, _SEM) + (_HBM,) * (2 * n) + (_VM,),
        input_output_aliases={i: 2 + i for i in range(2 * n)}, compiler_params=_SIDE)(*[_hbm(a) for a in grads + lands])
    _TOKENS.push(out[-1])
    return out[0], out[1], list(out[2:2 + n]), list(out[2 + n:2 + 2 * n])


def rs_d2d_wait(name, started, after):
    send, recv, grads, lands = started
    n = len(grads)

    def body(*refs):
        ins, lnd, send_sems, recv_sems = refs[:n], refs[n:2 * n], refs[2 * n], refs[2 * n + 1]
        x, y, c = _position()
        for w in range(n):
            for i in range(4):
                cp = _rcopy(ins[w].at[2 * i + 1 - c], lnd[w].at[i], send_sems.at[4 * w + i], recv_sems.at[4 * w + i], (x, y, 1 - c))
                cp.wait_send()
                cp.wait_recv()

    out = pl.pallas_call(
        body, name=name, out_shape=tuple(_like(a) for a in grads + lands), in_specs=[_HBM] * (2 * n) + [_SEM, _SEM, _ANY],
        out_specs=(_HBM,) * (2 * n), input_output_aliases={i: i for i in range(2 * n)}, compiler_params=_SIDE)(*grads, *lands, send, recv, after)
    return list(out[:n]), list(out[n:])


def pair_sum(name, grad, land, core):
    _, r, c = grad.shape
    tr = r
    if r % 8 == 0:
        tr = max(8, min(r, 4 * ADAMW_TILE_ELEMS // c) // 8 * 8)
        while r % tr:
            tr -= 8

    def body(core_ref, a_ref, b_ref, o_ref):
        o_ref[...] = (a_ref[...].astype(F32) + b_ref[...].astype(F32)).astype(o_ref.dtype)

    return pl.pallas_call(
        body, name=name, out_shape=jax.ShapeDtypeStruct((4, r, c), grad.dtype),
        grid_spec=pltpu.PrefetchScalarGridSpec(
            num_scalar_prefetch=1, grid=(4, r // tr),
            in_specs=[pl.BlockSpec((None, None, tr, c), lambda i, j, core_ref: (i, core_ref[0], j, 0)),
                      pl.BlockSpec((None, tr, c), lambda i, j, core_ref: (i, j, 0))],
            out_specs=pl.BlockSpec((None, tr, c), lambda i, j, core_ref: (i, j, 0))),
        compiler_params=_params("parallel", "parallel"))(core, grad.reshape(4, 2, r, c), land)


def rs_ici_start(name, sums):
    n = len(sums)
    lands = [lax.empty(a.shape, a.dtype) for a in sums]

    def body(*refs):
        ins, lnd, send_sems, recv_sems, token = refs[:n], refs[n:2 * n], refs[2 * n], refs[2 * n + 1], refs[4 * n + 2]
        x, y, c = _position()
        chip = 2 * x + y
        for w in range(n):
            for j, other in enumerate(_other_chips(x, y)):
                _rcopy(ins[w].at[2 * other[0] + other[1]], lnd[w].at[chip], send_sems.at[3 * w + j], recv_sems.at[3 * w + j], (*other, c)).start()
        token[...] = jnp.zeros_like(token)

    out = pl.pallas_call(
        body, name=name, out_shape=(_dma_sems(3 * n), _dma_sems(3 * n)) + tuple(_like(a) for a in sums + lands) + (_TOKEN,),
        in_specs=[_HBM] * (2 * n), out_specs=(_SEM, _SEM) + (_HBM,) * (2 * n) + (_VM,),
        input_output_aliases={i: 2 + i for i in range(2 * n)}, compiler_params=_SIDE)(*[_hbm(a) for a in sums + lands])
    _TOKENS.push(out[-1])
    return out[0], out[1], list(out[2:2 + n]), list(out[2 + n:2 + 2 * n])


def rs_ici_wait(name, started, after):
    send, recv, sums, lands = started
    n = len(sums)

    def body(*refs):
        ins, lnd, send_sems, recv_sems = refs[:n], refs[n:2 * n], refs[2 * n], refs[2 * n + 1]
        x, y, c = _position()
        for w in range(n):
            for j, other in enumerate(_other_chips(x, y)):
                cp = _rcopy(ins[w].at[2 * other[0] + other[1]], lnd[w].at[2 * other[0] + other[1]], send_sems.at[3 * w + j], recv_sems.at[3 * w + j], (*other, c))
                cp.wait_send()
                cp.wait_recv()

    out = pl.pallas_call(
        body, name=name, out_shape=tuple(_like(a) for a in sums + lands), in_specs=[_HBM] * (2 * n) + [_SEM, _SEM, _ANY],
        out_specs=(_HBM,) * (2 * n), input_output_aliases={i: i for i in range(2 * n)}, compiler_params=_SIDE)(*sums, *lands, send, recv, after)
    chip = 2 * lax.axis_index("x") + lax.axis_index("y")
    return [lax.dynamic_update_index_in_dim(land, lax.dynamic_index_in_dim(s, chip, 0, keepdims=False), chip, 0)
            for s, land in zip(out[:n], out[n:])]


def ada_fwd(c, w_ada, b_ada3, conv_w):
    d, cs = w_ada.shape

    def body(c_ref, w_ref, b_ref, cw_ref, mod_ref, sc_ref, cwa_ref, part_ref, send_sems, recv_sems):
        me = _position()
        my = _index(me)
        cv = c_ref[...]
        sc_ref[my] = cv * _sigmoid(cv)
        cwa_ref[my] = cw_ref[...]
        gather = []
        for r in range(1, N_DEV):
            for k, ref in enumerate((sc_ref, cwa_ref)):
                cp = pltpu.make_async_remote_copy(src_ref=ref.at[my], dst_ref=ref.at[my], send_sem=send_sems.at[14 * k + r - 1],
                                                  recv_sem=recv_sems.at[14 * k + r - 1], device_id=_peer(me, r), device_id_type=MESH)
                cp.start()
                gather.append(cp)
        for cp in gather:
            cp.wait()
        sc_all = jnp.concatenate([sc_ref[k] for k in range(N_DEV)], axis=0).astype(BF16)
        part = jnp.dot(sc_all, w_ref[...].astype(BF16), preferred_element_type=F32)
        for k in range(N_DEV):
            part_ref[k] = part[k:k + 1, :]
        scatter = []
        for r in range(1, N_DEV):
            peer = _peer(me, r)
            cp = pltpu.make_async_remote_copy(src_ref=part_ref.at[_index(peer)], dst_ref=mod_ref.at[my], send_sem=send_sems.at[6 + r],
                                              recv_sem=recv_sems.at[6 + r], device_id=peer, device_id_type=MESH)
            cp.start()
            scatter.append(cp)
        mod_ref[my] = part_ref[my]
        for cp in scatter:
            cp.wait()
        mod_ref[...] = mod_ref[...] + b_ref[...]

    vm = pl.BlockSpec(memory_space=pltpu.VMEM)
    return pl.pallas_call(
        body, name="ada_fwd",
        out_shape=[jax.ShapeDtypeStruct((N_DEV, 1, cs), F32), jax.ShapeDtypeStruct((N_DEV, 1, d), F32),
                   jax.ShapeDtypeStruct((N_DEV,) + conv_w.shape, F32)],
        in_specs=[vm, vm, vm, vm], out_specs=[vm, vm, vm],
        scratch_shapes=[pltpu.VMEM((N_DEV, 1, cs), F32), pltpu.SemaphoreType.DMA((21,)), pltpu.SemaphoreType.DMA((21,))],
        compiler_params=pltpu.CompilerParams(vmem_limit_bytes=VMEM_LIMIT_BYTES))(c, w_ada, b_ada3, conv_w)


def ada_bwd_w(sc_all, dmod_cols):
    _, d = sc_all.shape
    cs = dmod_cols.shape[1]
    tr = _tile(d, ROW_TILE)

    def body(sc_ref, dm_ref, o_ref):
        dm = dm_ref[...].astype(BF16)
        o_ref[...] = lax.dot_general(sc_ref[...].astype(BF16), dm, (((0,), (0,)), ((), ())), preferred_element_type=F32)

    return pl.pallas_call(body, name="ada_bwd_w", grid=(d // tr,),
                          in_specs=[pl.BlockSpec((N_DEV, tr), lambda i: (0, i)), _full((N_DEV, cs))],
                          out_specs=pl.BlockSpec((None, tr, cs), lambda i: (0, i, 0)),
                          out_shape=jax.ShapeDtypeStruct((1, d, cs), F32), compiler_params=_params("parallel"))(sc_all, dmod_cols)


def _round_up(n, m):
    return (n + m - 1) // m * m


def kernel(x, c, positions, w_ada, b_ada, pre_norm1_g, w_in, gm_ln_g, gm_ln_b, gm_w_s, gm_b_s, w_branch_a, q_norm_g, w_uq, kv_norm_g, w_ukv, w_branch_b, w_out, post_norm1_g, pre_norm2_g, w_up, conv_w, conv_b, w_down, post_norm2_g, loss_target, m_w_ada, m_b_ada, m_pre_norm1_g, m_w_in, m_gm_ln_g, m_gm_ln_b, m_gm_w_s, m_gm_b_s, m_w_branch_a, m_q_norm_g, m_w_uq, m_kv_norm_g, m_w_ukv, m_w_branch_b, m_w_out, m_post_norm1_g, m_pre_norm2_g, m_w_up, m_conv_w, m_conv_b, m_w_down, m_post_norm2_g, v_w_ada, v_b_ada, v_pre_norm1_g, v_w_in, v_gm_ln_g, v_gm_ln_b, v_gm_w_s, v_gm_b_s, v_w_branch_a, v_q_norm_g, v_w_uq, v_kv_norm_g, v_w_ukv, v_w_branch_b, v_w_out, v_post_norm1_g, v_pre_norm2_g, v_w_up, v_conv_w, v_conv_b, v_w_down, v_post_norm2_g):
    weights = dict(w_ada=w_ada, b_ada=b_ada, pre_norm1_g=pre_norm1_g, w_in=w_in, gm_ln_g=gm_ln_g, gm_ln_b=gm_ln_b, gm_w_s=gm_w_s,
                   gm_b_s=gm_b_s, w_branch_a=w_branch_a, q_norm_g=q_norm_g, w_uq=w_uq, kv_norm_g=kv_norm_g, w_ukv=w_ukv,
                   w_branch_b=w_branch_b, w_out=w_out, post_norm1_g=post_norm1_g, pre_norm2_g=pre_norm2_g, w_up=w_up, conv_w=conv_w,
                   conv_b=conv_b, w_down=w_down, post_norm2_g=post_norm2_g)
    mom1 = dict(w_ada=m_w_ada, b_ada=m_b_ada, pre_norm1_g=m_pre_norm1_g, w_in=m_w_in, gm_ln_g=m_gm_ln_g, gm_ln_b=m_gm_ln_b,
                gm_w_s=m_gm_w_s, gm_b_s=m_gm_b_s, w_branch_a=m_w_branch_a, q_norm_g=m_q_norm_g, w_uq=m_w_uq, kv_norm_g=m_kv_norm_g,
                w_ukv=m_w_ukv, w_branch_b=m_w_branch_b, w_out=m_w_out, post_norm1_g=m_post_norm1_g, pre_norm2_g=m_pre_norm2_g,
                w_up=m_w_up, conv_w=m_conv_w, conv_b=m_conv_b, w_down=m_w_down, post_norm2_g=m_post_norm2_g)
    mom2 = dict(w_ada=v_w_ada, b_ada=v_b_ada, pre_norm1_g=v_pre_norm1_g, w_in=v_w_in, gm_ln_g=v_gm_ln_g, gm_ln_b=v_gm_ln_b,
                gm_w_s=v_gm_w_s, gm_b_s=v_gm_b_s, w_branch_a=v_w_branch_a, q_norm_g=v_q_norm_g, w_uq=v_w_uq, kv_norm_g=v_kv_norm_g,
                w_ukv=v_w_ukv, w_branch_b=v_w_branch_b, w_out=v_w_out, post_norm1_g=v_post_norm1_g, pre_norm2_g=v_pre_norm2_g,
                w_up=v_w_up, conv_w=v_conv_w, conv_b=v_conv_b, w_down=v_w_down, post_norm2_g=v_post_norm2_g)
    order = list(weights)
    _TOKENS.clear()

    s, d = x.shape[1], x.shape[2]
    gmw = gm_ln_g.shape[0]
    groups = gmw // CHUNK
    ql, kvl = q_norm_g.shape[0], kv_norm_g.shape[0]
    f2 = conv_b.shape[0]
    in_cols = w_in.shape[1] * N_DEV
    o_q, o_kv, o_ga, o_gb, o_kpe = 2 * gmw, 2 * gmw + ql, 2 * gmw + ql + kvl, 2 * gmw + ql + kvl + d, 2 * gmw + ql + kvl + 2 * d
    zp = _round_up(o_kpe + LANES, Z_PAD)
    src_kpe = 2 * gmw + ql + kvl
    assert src_kpe + QK_ROPE + 2 * d == in_cols
    my = 4 * lax.axis_index("x") + 2 * lax.axis_index("y") + lax.axis_index("c")

    x2, tgt = x[0], loss_target[0]
    row = lambda a: a.reshape(1, -1)

    big = ["w_in", "w_branch_a", "w_uq", "w_ukv", "w_branch_b", "w_out", "w_up", "w_down"]
    sh = {k: weights[k].astype(BF16) for k in big[1:]}
    mix = ["w_branch_a", "w_uq", "w_ukv", "w_branch_b", "w_out"]
    ag_in = ag_start("ag_start_in", [w_in.T.astype(BF16)], c)

    mod8, sc_all3, g_cw = ada_fwd(c, w_ada, b_ada.reshape(N_DEV, 1, -1), conv_w)
    mod = mod8.reshape(N_MOD, d)
    shift1, scale1, gate1, shift2, scale2, gate2 = (mod[i:i + 1] for i in range(N_MOD))
    sc_all = sc_all3.reshape(N_DEV, d)
    h1 = norm_mod_fwd("pre1_fwd", x2, row(pre_norm1_g), scale1, shift1)

    inv = ROPE_THETA ** (-jnp.arange(0, QK_ROPE, 2, dtype=F32) / QK_ROPE)
    ang = positions[0].astype(F32)[:, None] * inv
    cos4 = jnp.tile(jnp.cos(ang), (1, 4))
    sin4 = jnp.tile(jnp.concatenate([-jnp.sin(ang), jnp.sin(ang)], axis=1), (1, 2))

    wm = (gm_w_s * jnp.tril(jnp.ones((CHUNK, CHUNK), F32))).astype(BF16)
    bs3 = gm_b_s.reshape(groups, CHUNK, 1)
    ln_g, ln_b = row(gm_ln_g), row(gm_ln_b)

    early = [h1, cos4, sin4, wm] + [sh[k] for k in big[1:]]
    ag_in = ag_forward("ag_forward_in", ag_in, early)
    ag_mix = ag_start("ag_start_mix", [sh[k] for k in mix], _TOKENS.pending[-1])
    (g_in,) = ag_wait("ag_wait_in", ag_in, h1)
    w_in_f = g_in.reshape(in_cols, d)
    w_in_p = jnp.concatenate([w_in_f[:src_kpe], w_in_f[src_kpe + QK_ROPE:], w_in_f[src_kpe:src_kpe + QK_ROPE],
                              jnp.zeros((zp - in_cols, d), BF16)], axis=0)

    z = mm_nt("z_proj", h1, w_in_p, F32)
    ag_mix = ag_forward("ag_forward_mix", ag_mix, z)
    ag_up = ag_start("ag_start_up", [sh["w_up"]], _TOKENS.pending[-1])
    a = gmlp_fwd(z, gmw, ln_g, ln_b, wm, bs3)
    g_a, g_uq, g_ukv, g_b, g_out = ag_wait("ag_wait_mix", ag_mix, a)
    w_a_f, w_b_f, w_out_f = g_a.reshape(-1, d), g_b.reshape(-1, d), g_out.reshape(-1, d)
    w_uq_f = g_uq.transpose(1, 0, 2).reshape(ql, HEADS, QK_NOPE + QK_ROPE)
    w_uq_n = w_uq_f[:, :, :QK_NOPE].reshape(ql, HEADS * QK_NOPE)
    w_uq_r = w_uq_f[:, :, QK_NOPE:].reshape(ql, HEADS * QK_ROPE)
    y_a = mm_nn("branch_a", a, w_a_f, F32)
    qln = rms_fwd_cols("q_norm", z, o_q, ql, row(q_norm_g))
    kvn = rms_fwd_cols("kv_norm", z, o_kv, kvl, row(kv_norm_g))
    qn = mm_nn("q_nope", qln, w_uq_n, BF16)
    qp = mm_nn("q_rope", qln, w_uq_r, F32)
    kv = mm_nn_b3("kv_up", kvn, g_ukv, BF16)
    kpr = rope_k(z, o_kpe, cos4, sin4)
    o, qpr, lse = attn_fwd2(qn, qp, kv, kpr, cos4, sin4)
    ag_up = ag_forward("ag_forward_up", ag_up, o)
    ag_down = ag_start("ag_start_down", [sh["w_down"]], _TOKENS.pending[-1])
    y_b = mm_nn("branch_b", o, w_b_f, F32)
    merged = merge_fwd(z, o_ga, o_gb, y_a, y_b)
    y1 = mm_nn("out_proj", merged, w_out_f, F32)
    x1 = post_res_fwd("post1_fwd", x2, y1, gate1, row(post_norm1_g))
    h2 = norm_mod_fwd("pre2_fwd", x1, row(pre_norm2_g), scale2, shift2)
    (g_up,) = ag_wait("ag_wait_up", ag_up, h2)
    upre = mm_nn_b3("up_proj", h2, g_up, F32)
    ag_down = ag_forward("ag_forward_down", ag_down, upre)
    cw = g_cw.transpose(1, 0, 2).reshape(3, f2)
    cb = row(conv_b)
    f = conv_act_fwd(upre, cw, cb)
    w_down_f = ag_wait("ag_wait_down", ag_down, f)[0].reshape(-1, d)
    ffn = mm_nn("down_proj", f, w_down_f, F32)
    loss_acc, dout, dffn, acc2 = post2_loss_bwd(x1, ffn, tgt, gate2, row(post_norm2_g))
    loss = lax.psum(loss_acc[0, 0], ("x", "y", "c"))
    _TOKENS.push(jnp.broadcast_to(loss, (8, LANES)))

    blocks = lambda g: g.reshape(N_DEV, g.shape[0] // N_DEV, g.shape[1])
    core = lax.axis_index("c").astype(jnp.int32).reshape(1)
    rs = {}

    def rs_begin(key, grads):
        rs[key] = rs_d2d_start("rs_d2d_start_" + key, grads)

    def rs_middle(key, after):
        grads, lands = rs_d2d_wait("rs_d2d_wait_" + key, rs[key], after)
        sums = [pair_sum("pair_sum_%s_%d" % (key, i), g, l, core) for i, (g, l) in enumerate(zip(grads, lands))]
        rs[key] = rs_ici_start("rs_ici_start_" + key, sums)

    gw_down = mm_tn("g_w_down", f, dffn, BF16)
    rs_begin("down", [blocks(gw_down)])
    df = mm_nt("d_f", dffn, w_down_f, F32)
    rs_middle("down", df)
    dupre, gcw_g, gcw_v, gcb_g, gcb_v = conv_act_bwd(upre, cw, cb, df)
    gw_up3 = mm_tn_h3("g_w_up", h2, dupre, N_DEV, BF16)
    rs_begin("up", [gw_up3])
    dh2 = mm_nt_h3("d_h2", dupre, g_up, F32)
    rs_middle("up", dh2)
    dx1, dy1, acc_mid = mid_bwd(dh2, dout, x1, y1, row(pre_norm2_g), scale2, gate1, row(post_norm1_g))
    gw_out = mm_tn("g_w_out", merged, dy1, BF16)
    dmerged = mm_nt("d_merged", dy1, w_out_f, F32)
    dya, dyb, dga, dgb = merge_bwd(z, o_ga, o_gb, y_a, y_b, dmerged)
    gw_a = mm_tn("g_w_a", a, dya, BF16)
    gw_b = mm_tn("g_w_b", o, dyb, BF16)
    rs_begin("mid", [blocks(gw_out), blocks(gw_a), blocks(gw_b)])
    da = mm_nt("d_a", dya, w_a_f, F32)
    do = mm_nt("d_o", dyb, w_b_f, F32)
    rs_middle("mid", do)
    duv, g_ws, g_bs3, acc_gm = gmlp_bwd(z, gmw, da, ln_g, ln_b, wm, bs3)
    dqn, dqp, dkv, dkp = attn_bwd2(qn, qpr, kv, kpr, o, do, lse, cos4, sin4)
    dkpe = kpe_bwd(dkp, cos4, sin4, zp - o_kpe)
    dq_cat = jnp.concatenate([dqn, dqp], axis=1)
    w_uq_cat = jnp.concatenate([w_uq_n, w_uq_r], axis=1)
    gw_uq_cat = mm_tn("g_w_uq", qln, dq_cat, BF16)
    gw_uq_f = jnp.concatenate([gw_uq_cat[:, :HEADS * QK_NOPE].reshape(ql, HEADS, QK_NOPE),
                               gw_uq_cat[:, HEADS * QK_NOPE:].reshape(ql, HEADS, QK_ROPE)], axis=2)
    gw_uq3 = gw_uq_f.reshape(ql, N_DEV, -1).transpose(1, 0, 2)
    gw_ukv3 = mm_tn_o3("g_w_ukv", kvn, dkv, N_DEV, BF16)
    rs_begin("mla", [gw_uq3, gw_ukv3])
    dqln = mm_nt("d_qln", dq_cat, w_uq_cat, F32)
    dq_lat, g_qnorm = rms_bwd_cols("q_norm_bwd", dqln, z, o_q, ql, row(q_norm_g))
    dkvn = mm_nt_b3("d_kvn", dkv, g_ukv, F32)
    rs_middle("mla", dkvn)
    dkv_lat, g_kvnorm = rms_bwd_cols("kv_norm_bwd", dkvn, z, o_kv, kvl, row(kv_norm_g))
    dz = jnp.concatenate([duv, dq_lat, dkv_lat, dga, dgb, dkpe], axis=1)
    gw_in_p = mm_tn("g_w_in", dz, h1, BF16)
    gw_in_f = jnp.concatenate([gw_in_p[:src_kpe], gw_in_p[o_kpe:o_kpe + QK_ROPE], gw_in_p[src_kpe:o_kpe]], axis=0)
    rs_begin("in", [gw_in_f.reshape(N_DEV, -1, d)])
    dh1 = mm_nn("d_h1", dz, w_in_p, F32)
    grad_x, acc1 = pre1_bwd(dh1, dx1, x2, row(pre_norm1_g), scale1)

    dmod = jnp.concatenate([acc1[0], acc1[1], acc_mid[3], acc_mid[0], acc_mid[1], acc2[0]])
    small = [("pre_norm1_g", acc1[2]), ("gm_ln_g", acc_gm[0]), ("gm_ln_b", acc_gm[1]), ("gm_b_s", g_bs3.reshape(-1)),
             ("q_norm_g", g_qnorm[0]), ("kv_norm_g", g_kvnorm[0]), ("post_norm1_g", acc_mid[4]), ("pre_norm2_g", acc_mid[2]),
             ("conv_b", jnp.concatenate([gcb_g[0], gcb_v[0]])), ("post_norm2_g", acc2[1]), ("gm_w_s", g_ws.reshape(-1)),
             ("b_ada", dmod)]
    n_small = sum(v.shape[0] for _, v in small)
    n_cw = 3 * f2
    n_pack = _round_up(n_small + n_cw, PACK_ALIGN)
    tail = jnp.zeros((n_pack - n_small - n_cw,), F32)
    packed = jnp.concatenate([v for _, v in small] + [jnp.concatenate([gcw_g, gcw_v], axis=1).reshape(-1), tail])
    ag_small = ag_start("ag_start_small", [packed.reshape(-1, LANES)], packed)
    rs_middle("in", packed)

    res = {}
    last = packed
    for key, names in (("down", ["w_down"]), ("up", ["w_up"]), ("mid", ["w_out", "w_branch_a", "w_branch_b"]), ("mla", ["w_uq", "w_ukv"])):
        parts = rs_ici_wait("rs_ici_wait_" + key, rs[key], last)
        for k, p in zip(names, parts):
            res[k] = adamw("adamw_" + k, weights[k], mom1[k], mom2[k], p)
            last = res[k][0]

    def pack(src):
        return jnp.concatenate([src[k].reshape(-1) for k, _ in small] + [jnp.zeros((n_pack - n_small,), F32)]).reshape(-1, LANES)

    (gathered,) = ag_wait("ag_wait_small", ag_forward("ag_forward_small", ag_small, last), last)
    sm = [t.reshape(-1) for t in adamw("adamw_small", pack(weights), pack(mom1), pack(mom2), gathered)]
    off = 0
    for k, v in small:
        res[k] = tuple(t[off:off + v.shape[0]].reshape(weights[k].shape) for t in sm)
        off += v.shape[0]

    cs_cw = conv_w.shape[1]
    g_cw_full = sm[0][n_small:n_small + n_cw].reshape(3, f2)
    g_cw_mine = lax.dynamic_slice(g_cw_full, (0, my * cs_cw), (3, cs_cw))
    res["conv_w"] = adamw("adamw_conv_w", conv_w, mom1["conv_w"], mom2["conv_w"], g_cw_mine[None])

    cs_ada = w_ada.shape[1]
    off_b = n_small - N_MOD * d
    dmod_all = gathered.reshape(N_DEV, -1)[:, off_b:off_b + N_MOD * d]
    dmod_cols = lax.dynamic_slice(dmod_all, (0, my * cs_ada), (N_DEV, cs_ada))
    res["w_ada"] = adamw("adamw_w_ada", w_ada, mom1["w_ada"], mom2["w_ada"], ada_bwd_w(sc_all, dmod_cols))

    (p_in,) = rs_ici_wait("rs_ici_wait_in", rs["in"], res["w_ada"][0])
    res["w_in"] = tuple(t.T for t in adamw("adamw_w_in", w_in.T, mom1["w_in"].T, mom2["w_in"].T, p_in))

    _TOKENS.clear()
    outs = [loss, grad_x[None]]
    for i in range(4):
        outs += [res[k][i] for k in order]
    return tuple(outs)
```

```python
import functools

import jax
import jax.numpy as jnp
from jax import lax
from jax.experimental import pallas as pl
from jax.experimental.pallas import tpu as pltpu

F32 = jnp.float32
BF16 = jnp.bfloat16

N_DEV = 8
HEADS = 16
QK_NOPE = 128
QK_ROPE = 64
V_HEAD = 128
CHUNK = 128
ROPE_THETA = 10000.0
EPS = 1e-6
N_MOD = 6
ADAM_LR, ADAM_B1, ADAM_B2, ADAM_EPS, ADAM_WD, ADAM_STEP = 0.001, 0.9, 0.999, 1e-08, 0.01, 10

LANES = 128
VMEM_LIMIT_BYTES = 48 * 2 ** 20
ROW_TILE = 256
COL_TILE = 256
ATT_TILE = 256
Z_PAD = 512
ADAMW_TILE_ELEMS = 1 << 18
PACK_ALIGN = 8 * LANES
MESH = pl.DeviceIdType.MESH


def _params(*sem):
    return pltpu.CompilerParams(dimension_semantics=sem if sem else None, vmem_limit_bytes=VMEM_LIMIT_BYTES)


def _tile(dim, target):
    t = (min(dim, target) // LANES) * LANES
    while t >= LANES:
        if dim % t == 0:
            return t
        t -= LANES
    return dim


def _full(shape):
    nd = len(shape)
    return pl.BlockSpec(shape, lambda *_: (0,) * nd)


class _Tokens:
    KEEP = 2

    def __init__(self):
        self.pending = []

    def push(self, token):
        self.pending = (self.pending + [token])[-self.KEEP:]

    def take(self):
        return list(self.pending)

    def clear(self):
        self.pending = []


_TOKENS = _Tokens()


def _matmul(name, a, b, *, grid, a_spec, b_spec, o_spec, out_shape, contract, acc_shape, split=1):
    nk = grid[2]
    deps = _TOKENS.take()

    def product(a_ref, b_ref):
        if len(b_ref.shape) == 2:
            return lax.dot_general(a_ref[...].astype(BF16), b_ref[...].astype(BF16), (contract, ((), ())), preferred_element_type=F32)
        cs = b_ref.shape[2]
        return sum(lax.dot_general(a_ref[:, s * cs:(s + 1) * cs].astype(BF16), b_ref[s].astype(BF16), (contract, ((), ())),
                                   preferred_element_type=F32) for s in range(split))

    def body_one_step(a_ref, b_ref, *rest):
        o_ref = rest[len(deps)]
        o_ref[...] = product(a_ref, b_ref).astype(o_ref.dtype)

    def body(a_ref, b_ref, *rest):
        o_ref, acc_ref = rest[len(deps):]
        k = pl.program_id(2)

        @pl.when(k == 0)
        def _():
            acc_ref[...] = jnp.zeros_like(acc_ref)

        acc_ref[...] += product(a_ref, b_ref)

        @pl.when(k == nk - 1)
        def _():
            o_ref[...] = acc_ref[...].astype(o_ref.dtype)

    return pl.pallas_call(
        body_one_step if nk == 1 else body, name=name, grid=grid,
        in_specs=[a_spec, b_spec] + [pl.BlockSpec(memory_space=pl.ANY)] * len(deps),
        out_specs=o_spec, out_shape=out_shape, scratch_shapes=[] if nk == 1 else [pltpu.VMEM(acc_shape, F32)],
        compiler_params=_params("parallel", "parallel", "arbitrary"))(a, b, *deps)


TM, TN, TK = 1024, 1024, 2304


def _tk(a, b):
    return TK if a.dtype == BF16 and b.dtype == BF16 else TK // 2


def mm_nn(name, a, b, dtype):
    (m, k), n = a.shape, b.shape[1]
    tm, tn, tk = _tile(m, TM), _tile(n, TN), _tile(k, _tk(a, b))
    return _matmul(name, a, b, grid=(m // tm, n // tn, k // tk),
                   a_spec=pl.BlockSpec((tm, tk), lambda i, j, kk: (i, kk)),
                   b_spec=pl.BlockSpec((tk, tn), lambda i, j, kk: (kk, j)),
                   o_spec=pl.BlockSpec((tm, tn), lambda i, j, kk: (i, j)),
                   out_shape=jax.ShapeDtypeStruct((m, n), dtype), contract=((1,), (0,)), acc_shape=(tm, tn))


def mm_nn_b3(name, a, b3, dtype):
    (m, k), (nj, _, cs) = a.shape, b3.shape
    tm, tk = _tile(m, TM), _tile(k, _tk(a, b3))
    return _matmul(name, a, b3, grid=(m // tm, nj, k // tk),
                   a_spec=pl.BlockSpec((tm, tk), lambda i, j, kk: (i, kk)),
                   b_spec=pl.BlockSpec((None, tk, cs), lambda i, j, kk: (j, kk, 0)),
                   o_spec=pl.BlockSpec((tm, cs), lambda i, j, kk: (i, j)),
                   out_shape=jax.ShapeDtypeStruct((m, nj * cs), dtype), contract=((1,), (0,)), acc_shape=(tm, cs))


def mm_nt(name, a, b, dtype):
    (m, k), n = a.shape, b.shape[0]
    tm, tn, tk = _tile(m, TM), _tile(n, TN), _tile(k, _tk(a, b))
    return _matmul(name, a, b, grid=(m // tm, n // tn, k // tk),
                   a_spec=pl.BlockSpec((tm, tk), lambda i, j, kk: (i, kk)),
                   b_spec=pl.BlockSpec((tn, tk), lambda i, j, kk: (j, kk)),
                   o_spec=pl.BlockSpec((tm, tn), lambda i, j, kk: (i, j)),
                   out_shape=jax.ShapeDtypeStruct((m, n), dtype), contract=((1,), (1,)), acc_shape=(tm, tn))


def mm_nt_b3(name, a, b3, dtype):
    m, (nj, n, cs) = a.shape[0], b3.shape
    tm, tn = _tile(m, TM), _tile(n, TN)
    return _matmul(name, a, b3, grid=(m // tm, n // tn, nj),
                   a_spec=pl.BlockSpec((tm, cs), lambda i, j, kk: (i, kk)),
                   b_spec=pl.BlockSpec((None, tn, cs), lambda i, j, kk: (kk, j, 0)),
                   o_spec=pl.BlockSpec((tm, tn), lambda i, j, kk: (i, j)),
                   out_shape=jax.ShapeDtypeStruct((m, n), dtype), contract=((1,), (1,)), acc_shape=(tm, tn))


def mm_nt_h3(name, a3, b3, dtype):
    (_, m, _), (nj, n, cs) = a3.shape, b3.shape
    tm, tn, hj = _tile(m, TM), _tile(n, TN), nj // 2
    pair = 2 if hj % 2 == 0 else 1
    return _matmul(name, a3, b3.reshape(nj // pair, pair, n, cs), grid=(m // tm, n // tn, nj // pair),
                   a_spec=pl.BlockSpec((None, tm, pair * cs), lambda i, j, kk: (kk // (hj // pair), i, kk % (hj // pair))),
                   b_spec=pl.BlockSpec((None, pair, tn, cs), lambda i, j, kk: (kk, 0, j, 0)),
                   o_spec=pl.BlockSpec((tm, tn), lambda i, j, kk: (i, j)),
                   out_shape=jax.ShapeDtypeStruct((m, n), dtype), contract=((1,), (1,)), acc_shape=(tm, tn), split=pair)


def mm_tn_h3(name, a, b3, nj, dtype):
    (k, m), half = a.shape, b3.shape[2]
    hj = nj // 2
    cs = half // hj
    tm, tk = _tile(m, TM), _tile(k, _tk(a, b3))
    return _matmul(name, a, b3, grid=(m // tm, nj, k // tk),
                   a_spec=pl.BlockSpec((tk, tm), lambda i, j, kk: (kk, i)),
                   b_spec=pl.BlockSpec((None, tk, cs), lambda i, j, kk: (j // hj, kk, j % hj)),
                   o_spec=pl.BlockSpec((None, tm, cs), lambda i, j, kk: (j, i, 0)),
                   out_shape=jax.ShapeDtypeStruct((nj, m, cs), dtype), contract=((0,), (0,)), acc_shape=(tm, cs))


def mm_tn(name, a, b, dtype):
    (k, m), n = a.shape, b.shape[1]
    tm, tn, tk = _tile(m, TM), _tile(n, TN), _tile(k, _tk(a, b))
    return _matmul(name, a, b, grid=(m // tm, n // tn, k // tk),
                   a_spec=pl.BlockSpec((tk, tm), lambda i, j, kk: (kk, i)),
                   b_spec=pl.BlockSpec((tk, tn), lambda i, j, kk: (kk, j)),
                   o_spec=pl.BlockSpec((tm, tn), lambda i, j, kk: (i, j)),
                   out_shape=jax.ShapeDtypeStruct((m, n), dtype), contract=((0,), (0,)), acc_shape=(tm, tn))


def mm_tn_o3(name, a, b, nj, dtype):
    (k, m), n = a.shape, b.shape[1]
    cs = n // nj
    tm, tk = _tile(m, TM), _tile(k, _tk(a, b))
    return _matmul(name, a, b, grid=(m // tm, nj, k // tk),
                   a_spec=pl.BlockSpec((tk, tm), lambda i, j, kk: (kk, i)),
                   b_spec=pl.BlockSpec((tk, cs), lambda i, j, kk: (kk, j)),
                   o_spec=pl.BlockSpec((None, tm, cs), lambda i, j, kk: (j, i, 0)),
                   out_shape=jax.ShapeDtypeStruct((nj, m, cs), dtype), contract=((0,), (0,)), acc_shape=(tm, cs))


_GELU_C = 0.7978845608028654
_GELU_A = 0.044715


def _gelu(x):
    return 0.5 * x * (1.0 + jnp.tanh(_GELU_C * (x + _GELU_A * x * x * x)))


def _gelu_and_grad(x):
    t = jnp.tanh(_GELU_C * (x + _GELU_A * x * x * x))
    y = 0.5 * x * (1.0 + t)
    dy = 0.5 * (1.0 + t) + 0.5 * x * (1.0 - t * t) * (_GELU_C * (1.0 + 3.0 * _GELU_A * x * x))
    return y, dy


def _sigmoid(x):
    return 1.0 / (1.0 + jnp.exp(-x))


def _rms_stats(x):
    inv = lax.rsqrt(jnp.mean(x * x, axis=-1, keepdims=True) + EPS)
    return inv, x * inv


def _rms_bwd(dyhat, yhat, inv):
    return inv * (dyhat - yhat * jnp.mean(dyhat * yhat, axis=-1, keepdims=True))


def _colsum(x):
    return jnp.sum(x, axis=0, keepdims=True)


def _rope(x, cos4, sin4):
    lane = lax.broadcasted_iota(jnp.int32, x.shape, x.ndim - 1)
    first_half = (lane % QK_ROPE) < (QK_ROPE // 2)
    partner = jnp.where(first_half, pltpu.roll(x, LANES - QK_ROPE // 2, x.ndim - 1), pltpu.roll(x, QK_ROPE // 2, x.ndim - 1))
    return x * cos4 + partner * sin4


def norm_mod_fwd(name, x, g, scale, shift):
    s, d = x.shape
    tr = _tile(s, ROW_TILE)

    def body(x_ref, g_ref, sc_ref, sh_ref, o_ref):
        _, xh = _rms_stats(x_ref[...])
        o_ref[...] = (xh * g_ref[...] * (1.0 + sc_ref[...]) + sh_ref[...]).astype(o_ref.dtype)

    row = pl.BlockSpec((tr, d), lambda i: (i, 0))
    vec = pl.BlockSpec((1, d), lambda i: (0, 0))
    return pl.pallas_call(body, name=name, grid=(s // tr,), in_specs=[row, vec, vec, vec], out_specs=row,
                          out_shape=jax.ShapeDtypeStruct((s, d), BF16), compiler_params=_params("parallel"))(x, g, scale, shift)


def rms_fwd_cols(name, z, off, width, g):
    s = z.shape[0]
    tr = _tile(s, ROW_TILE)
    assert off % width == 0

    def body(x_ref, g_ref, o_ref):
        _, xh = _rms_stats(x_ref[...])
        o_ref[...] = (xh * g_ref[...]).astype(o_ref.dtype)

    return pl.pallas_call(body, name=name, grid=(s // tr,),
                          in_specs=[pl.BlockSpec((tr, width), lambda i: (i, off // width)), pl.BlockSpec((1, width), lambda i: (0, 0))],
                          out_specs=pl.BlockSpec((tr, width), lambda i: (i, 0)),
                          out_shape=jax.ShapeDtypeStruct((s, width), BF16), compiler_params=_params("parallel"))(z, g)


def rms_bwd_cols(name, dy, z, off, width, g):
    s = z.shape[0]
    tr = _tile(s, ROW_TILE)

    def body(dy_ref, x_ref, g_ref, dx_ref, gg_ref):
        @pl.when(pl.program_id(0) == 0)
        def _():
            gg_ref[...] = jnp.zeros_like(gg_ref)

        inv, xh = _rms_stats(x_ref[...])
        dy_v = dy_ref[...]
        gg_ref[...] += _colsum(dy_v * xh)
        dx_ref[...] = _rms_bwd(dy_v * g_ref[...], xh, inv).astype(dx_ref.dtype)

    return pl.pallas_call(body, name=name, grid=(s // tr,),
                          in_specs=[pl.BlockSpec((tr, width), lambda i: (i, 0)), pl.BlockSpec((tr, width), lambda i: (i, off // width)),
                                    pl.BlockSpec((1, width), lambda i: (0, 0))],
                          out_specs=[pl.BlockSpec((tr, width), lambda i: (i, 0)), pl.BlockSpec((1, width), lambda i: (0, 0))],
                          out_shape=[jax.ShapeDtypeStruct((s, width), BF16), jax.ShapeDtypeStruct((1, width), F32)],
                          compiler_params=_params("arbitrary"))(dy, z, g)


def post_res_fwd(name, x, y, gate, g):
    s, d = x.shape
    tr = _tile(s, ROW_TILE)

    def body(x_ref, y_ref, gate_ref, g_ref, o_ref):
        _, yh = _rms_stats(y_ref[...])
        o_ref[...] = x_ref[...] + gate_ref[...] * (yh * g_ref[...])

    row = pl.BlockSpec((tr, d), lambda i: (i, 0))
    vec = pl.BlockSpec((1, d), lambda i: (0, 0))
    return pl.pallas_call(body, name=name, grid=(s // tr,), in_specs=[row, row, vec, vec], out_specs=row,
                          out_shape=jax.ShapeDtypeStruct((s, d), F32), compiler_params=_params("parallel"))(x, y, gate, g)


def post2_loss_bwd(x1, ffn, target, gate2, g):
    s, d = x1.shape
    tr = _tile(s, ROW_TILE)

    def body(x_ref, y_ref, t_ref, gate_ref, g_ref, loss_ref, dout_ref, dy_ref, acc_ref):
        @pl.when(pl.program_id(0) == 0)
        def _():
            loss_ref[...] = jnp.zeros_like(loss_ref)
            acc_ref[...] = jnp.zeros_like(acc_ref)

        inv, yh = _rms_stats(y_ref[...])
        r = yh * g_ref[...]
        err = x_ref[...] + gate_ref[...] * r - t_ref[...]
        loss_ref[...] += 0.5 * jnp.sum(jnp.mean(err * err, axis=-1, keepdims=True))
        dout = err / d
        dout_ref[...] = dout
        dr = dout * gate_ref[...]
        acc_ref[0:1, :] += _colsum(dout * r)
        acc_ref[1:2, :] += _colsum(dr * yh)
        dy_ref[...] = _rms_bwd(dr * g_ref[...], yh, inv).astype(dy_ref.dtype)

    row = pl.BlockSpec((tr, d), lambda i: (i, 0))
    vec = pl.BlockSpec((1, d), lambda i: (0, 0))
    return pl.pallas_call(
        body, name="post2_loss_bwd", grid=(s // tr,), in_specs=[row, row, row, vec, vec],
        out_specs=[_full((8, LANES)), row, row, _full((8, d))],
        out_shape=[jax.ShapeDtypeStruct((8, LANES), F32), jax.ShapeDtypeStruct((s, d), F32),
                   jax.ShapeDtypeStruct((s, d), BF16), jax.ShapeDtypeStruct((8, d), F32)],
        compiler_params=_params("arbitrary"))(x1, ffn, target, gate2, g)


def mid_bwd(dh2, dout, x1, y1, pre2_g, scale2, gate1, post1_g):
    s, d = x1.shape
    tr = _tile(s, ROW_TILE)

    def body(dh_ref, dout_ref, x_ref, y_ref, g2_ref, sc_ref, gate_ref, g1_ref, dx_ref, dy_ref, acc_ref):
        @pl.when(pl.program_id(0) == 0)
        def _():
            acc_ref[...] = jnp.zeros_like(acc_ref)

        dh = dh_ref[...]
        inv2, xh = _rms_stats(x_ref[...])
        acc_ref[0:1, :] += _colsum(dh)
        acc_ref[1:2, :] += _colsum(dh * (xh * g2_ref[...]))
        t = dh * (1.0 + sc_ref[...])
        acc_ref[2:3, :] += _colsum(t * xh)
        dx1 = dout_ref[...] + _rms_bwd(t * g2_ref[...], xh, inv2)
        dx_ref[...] = dx1
        inv1, yh = _rms_stats(y_ref[...])
        acc_ref[3:4, :] += _colsum(dx1 * (yh * g1_ref[...]))
        dr = dx1 * gate_ref[...]
        acc_ref[4:5, :] += _colsum(dr * yh)
        dy_ref[...] = _rms_bwd(dr * g1_ref[...], yh, inv1).astype(dy_ref.dtype)

    row = pl.BlockSpec((tr, d), lambda i: (i, 0))
    vec = pl.BlockSpec((1, d), lambda i: (0, 0))
    return pl.pallas_call(
        body, name="mid_bwd", grid=(s // tr,), in_specs=[row, row, row, row, vec, vec, vec, vec],
        out_specs=[row, row, _full((8, d))],
        out_shape=[jax.ShapeDtypeStruct((s, d), F32), jax.ShapeDtypeStruct((s, d), BF16), jax.ShapeDtypeStruct((8, d), F32)],
        compiler_params=_params("arbitrary"))(dh2, dout, x1, y1, pre2_g, scale2, gate1, post1_g)


def pre1_bwd(dh1, dx1, x, pre1_g, scale1):
    s, d = x.shape
    tr = _tile(s, ROW_TILE)

    def body(dh_ref, dx1_ref, x_ref, g_ref, sc_ref, dx_ref, acc_ref):
        @pl.when(pl.program_id(0) == 0)
        def _():
            acc_ref[...] = jnp.zeros_like(acc_ref)

        dh = dh_ref[...]
        inv, xh = _rms_stats(x_ref[...])
        acc_ref[0:1, :] += _colsum(dh)
        acc_ref[1:2, :] += _colsum(dh * (xh * g_ref[...]))
        t = dh * (1.0 + sc_ref[...])
        acc_ref[2:3, :] += _colsum(t * xh)
        dx_ref[...] = dx1_ref[...] + _rms_bwd(t * g_ref[...], xh, inv)

    row = pl.BlockSpec((tr, d), lambda i: (i, 0))
    vec = pl.BlockSpec((1, d), lambda i: (0, 0))
    return pl.pallas_call(
        body, name="pre1_bwd", grid=(s // tr,), in_specs=[row, row, row, vec, vec], out_specs=[row, _full((8, d))],
        out_shape=[jax.ShapeDtypeStruct((s, d), F32), jax.ShapeDtypeStruct((8, d), F32)],
        compiler_params=_params("arbitrary"))(dh1, dx1, x, pre1_g, scale1)


def _ln_stats(v):
    mu = jnp.mean(v, axis=-1, keepdims=True)
    vc = v - mu
    rstd = lax.rsqrt(jnp.mean(vc * vc, axis=-1, keepdims=True) + EPS)
    return rstd, vc * rstd


def gmlp_fwd(z, width, ln_g, ln_b, wm, bs3):
    s = z.shape[0]
    groups = width // CHUNK

    def body(u_ref, v_ref, g_ref, b_ref, wm_ref, bs_ref, a_ref):
        ug = _gelu(u_ref[...])
        _, vh = _ln_stats(_gelu(v_ref[...]))
        vn = (vh * g_ref[...] + b_ref[...]).astype(BF16)
        for g in range(groups):
            cols = slice(g * CHUNK, (g + 1) * CHUNK)
            mixed = jnp.dot(wm_ref[g], vn[:, cols], preferred_element_type=F32) + bs_ref[g]
            a_ref[:, cols] = (ug[:, cols] * mixed).astype(a_ref.dtype)

    vec = pl.BlockSpec((1, width), lambda n: (0, 0))
    return pl.pallas_call(
        body, name="gmlp_fwd", grid=(s // CHUNK,),
        in_specs=[pl.BlockSpec((CHUNK, width), lambda n: (n, 0)), pl.BlockSpec((CHUNK, width), lambda n: (n, 1)), vec, vec,
                  _full(wm.shape), _full(bs3.shape)],
        out_specs=pl.BlockSpec((CHUNK, width), lambda n: (n, 0)),
        out_shape=jax.ShapeDtypeStruct((s, width), BF16), compiler_params=_params("parallel"))(z, z, ln_g, ln_b, wm, bs3)


def gmlp_bwd(z, width, da, ln_g, ln_b, wm, bs3):
    s = z.shape[0]
    groups = width // CHUNK

    def body(u_ref, v_ref, da_ref, g_ref, b_ref, wm_ref, bs_ref, duv_ref, gw_ref, gb_ref, acc_ref, dvn_ref):
        @pl.when(pl.program_id(0) == 0)
        def _():
            gw_ref[...] = jnp.zeros_like(gw_ref)
            gb_ref[...] = jnp.zeros_like(gb_ref)
            acc_ref[...] = jnp.zeros_like(acc_ref)

        ug, dug = _gelu_and_grad(u_ref[...])
        vg, dvg = _gelu_and_grad(v_ref[...])
        rstd, vh = _ln_stats(vg)
        vn = (vh * g_ref[...] + b_ref[...]).astype(BF16)
        da_v = da_ref[...]
        for g in range(groups):
            cols = slice(g * CHUNK, (g + 1) * CHUNK)
            mixed = jnp.dot(wm_ref[g], vn[:, cols], preferred_element_type=F32) + bs_ref[g]
            duv_ref[:, cols] = (da_v[:, cols] * mixed * dug[:, cols]).astype(duv_ref.dtype)
            dm = da_v[:, cols] * ug[:, cols]
            gb_ref[g] += jnp.sum(dm, axis=-1, keepdims=True)
            dmb = dm.astype(BF16)
            gw_ref[g] += lax.dot_general(dmb, vn[:, cols], (((1,), (1,)), ((), ())), preferred_element_type=F32)
            dvn_ref[:, cols] = lax.dot_general(wm_ref[g], dmb, (((0,), (0,)), ((), ())), preferred_element_type=F32)
        dvn = dvn_ref[...]
        acc_ref[0:1, :] += _colsum(dvn * vh)
        acc_ref[1:2, :] += _colsum(dvn)
        dvh = dvn * g_ref[...]
        dv = rstd * (dvh - jnp.mean(dvh, axis=-1, keepdims=True) - vh * jnp.mean(dvh * vh, axis=-1, keepdims=True))
        duv_ref[:, width:] = (dv * dvg).astype(duv_ref.dtype)

        @pl.when(pl.program_id(0) == pl.num_programs(0) - 1)
        def _():
            q = lax.broadcasted_iota(jnp.int32, gw_ref.shape, 1)
            p = lax.broadcasted_iota(jnp.int32, gw_ref.shape, 2)
            gw_ref[...] = jnp.where(p <= q, gw_ref[...], 0.0)

    vec = pl.BlockSpec((1, width), lambda n: (0, 0))
    blk = pl.BlockSpec((CHUNK, width), lambda n: (n, 0))
    return pl.pallas_call(
        body, name="gmlp_bwd", grid=(s // CHUNK,),
        in_specs=[blk, pl.BlockSpec((CHUNK, width), lambda n: (n, 1)), blk, vec, vec, _full(wm.shape), _full(bs3.shape)],
        out_specs=[pl.BlockSpec((CHUNK, 2 * width), lambda n: (n, 0)), _full(wm.shape), _full(bs3.shape), _full((8, width))],
        out_shape=[jax.ShapeDtypeStruct((s, 2 * width), BF16), jax.ShapeDtypeStruct(wm.shape, F32),
                   jax.ShapeDtypeStruct(bs3.shape, F32), jax.ShapeDtypeStruct((8, width), F32)],
        scratch_shapes=[pltpu.VMEM((CHUNK, width), F32)],
        compiler_params=_params("arbitrary"))(z, z, da, ln_g, ln_b, wm, bs3)


def merge_fwd(z, off_a, off_b, ya, yb):
    s, d = ya.shape
    tr, tc = _tile(s, ROW_TILE * 2), _tile(d, COL_TILE)
    assert off_a % tc == 0 and off_b % tc == 0

    def body(ga_ref, gb_ref, ya_ref, yb_ref, o_ref):
        o_ref[...] = (_sigmoid(ga_ref[...]) * ya_ref[...] + _sigmoid(gb_ref[...]) * yb_ref[...]).astype(o_ref.dtype)

    blk = pl.BlockSpec((tr, tc), lambda i, j: (i, j))
    return pl.pallas_call(
        body, name="merge_fwd", grid=(s // tr, d // tc),
        in_specs=[pl.BlockSpec((tr, tc), lambda i, j: (i, off_a // tc + j)), pl.BlockSpec((tr, tc), lambda i, j: (i, off_b // tc + j)), blk, blk],
        out_specs=blk, out_shape=jax.ShapeDtypeStruct((s, d), BF16), compiler_params=_params("parallel", "parallel"))(z, z, ya, yb)


def merge_bwd(z, off_a, off_b, ya, yb, dm):
    s, d = ya.shape
    tr, tc = _tile(s, ROW_TILE * 2), _tile(d, COL_TILE)
    nc = d // tc

    def body(ga_ref, gb_ref, ya_ref, yb_ref, dm_ref, dya_ref, dyb_ref, dga_ref, dgb_ref):
        dm_v = dm_ref[...]
        sa, sb = _sigmoid(ga_ref[...]), _sigmoid(gb_ref[...])
        dya_ref[...] = (dm_v * sa).astype(dya_ref.dtype)
        dyb_ref[...] = (dm_v * sb).astype(dyb_ref.dtype)
        dga_ref[...] = (dm_v * ya_ref[...] * sa * (1.0 - sa)).astype(dga_ref.dtype)
        dgb_ref[...] = (dm_v * yb_ref[...] * sb * (1.0 - sb)).astype(dgb_ref.dtype)

    blk = pl.BlockSpec((tr, tc), lambda i, j: (i, j))
    out = jax.ShapeDtypeStruct((s, d), BF16)
    return pl.pallas_call(
        body, name="merge_bwd", grid=(s // tr, nc),
        in_specs=[pl.BlockSpec((tr, tc), lambda i, j: (i, off_a // tc + j)), pl.BlockSpec((tr, tc), lambda i, j: (i, off_b // tc + j)), blk, blk, blk],
        out_specs=[blk, blk, blk, blk], out_shape=[out, out, out, out],
        compiler_params=_params("parallel", "parallel"))(z, z, ya, yb, dm)


_ATT_SCALE = (QK_NOPE + QK_ROPE) ** -0.5
_NEG = -1e30


def rope_k(z, off, cos4, sin4):
    s = z.shape[0]
    tr = _tile(s, ROW_TILE * 2)
    assert off % LANES == 0

    def body(k_ref, c_ref, s_ref, o_ref):
        k = k_ref[...]
        k = k + pltpu.roll(k, QK_ROPE, 1)
        o_ref[...] = _rope(k, c_ref[...], s_ref[...]).astype(o_ref.dtype)

    row = pl.BlockSpec((tr, LANES), lambda i: (i, 0))
    return pl.pallas_call(body, name="rope_k", grid=(s // tr,),
                          in_specs=[pl.BlockSpec((tr, LANES), lambda i: (i, off // LANES)), row, row], out_specs=row,
                          out_shape=jax.ShapeDtypeStruct((s, LANES), BF16), compiler_params=_params("parallel"))(z, cos4, sin4)


def _head_masks(shape):
    lane = lax.broadcasted_iota(jnp.int32, shape, 1)
    return lane < QK_ROPE, lane >= QK_ROPE


def _scores(qn, qp_h, k, kp, qi, kb, t):
    sc = lax.dot_general(qn, k, (((1,), (1,)), ((), ())), preferred_element_type=F32)
    sc += lax.dot_general(qp_h, kp, (((1,), (1,)), ((), ())), preferred_element_type=F32)
    sc = sc * _ATT_SCALE
    row = lax.broadcasted_iota(jnp.int32, sc.shape, 0) + qi * t
    col = lax.broadcasted_iota(jnp.int32, sc.shape, 1) + kb * t
    return jnp.where(col <= row, sc, _NEG)


def attn_fwd(qn, qp, kv, kpr, cos4, sin4):
    s = qn.shape[0]
    hp = HEADS // 2
    t = _tile(s, ATT_TILE)
    nq = s // t

    def body(qn_ref, qp_ref, kv_ref, kp_ref, c_ref, s_ref, o_ref, qpr_ref, l_ref):
        qi = pl.program_id(1)
        qpr = _rope(qp_ref[...], c_ref[...], s_ref[...]).astype(BF16)
        qpr_ref[...] = qpr
        masks = _head_masks(qpr.shape)
        for hh in range(2):
            q_n = qn_ref[:, hh * QK_NOPE:(hh + 1) * QK_NOPE]
            q_p = jnp.where(masks[hh], qpr, jnp.zeros_like(qpr))
            kc, vc = 2 * hh * QK_NOPE, (2 * hh + 1) * QK_NOPE

            def step(kb, carry):
                m, l, acc = carry
                rows = pl.ds(pl.multiple_of(kb * t, t), t)
                sc = _scores(q_n, q_p, kv_ref[rows, kc:kc + QK_NOPE], kp_ref[rows, :], qi, kb, t)
                m_new = jnp.maximum(m, jnp.max(sc, axis=-1, keepdims=True))
                alpha = jnp.exp(m - m_new)
                p = jnp.exp(sc - m_new)
                l = alpha * l + jnp.sum(p, axis=-1, keepdims=True)
                acc = alpha * acc + jnp.dot(p.astype(BF16), kv_ref[rows, vc:vc + V_HEAD], preferred_element_type=F32)
                return m_new, l, acc

            init = (jnp.full((t, 1), _NEG, F32), jnp.zeros((t, 1), F32), jnp.zeros((t, V_HEAD), F32))
            m, l, acc = lax.fori_loop(0, qi + 1, step, init)
            o_ref[:, hh * V_HEAD:(hh + 1) * V_HEAD] = acc / l
            l_ref[:, hh:hh + 1] = m + jnp.log(l)

    return pl.pallas_call(
        body, name="attn_fwd", grid=(hp, nq),
        in_specs=[pl.BlockSpec((t, 2 * QK_NOPE), lambda h, i: (i, h)), pl.BlockSpec((t, LANES), lambda h, i: (i, h)),
                  pl.BlockSpec((s, 4 * QK_NOPE), lambda h, i: (0, h)), _full((s, LANES)),
                  pl.BlockSpec((t, LANES), lambda h, i: (i, 0)), pl.BlockSpec((t, LANES), lambda h, i: (i, 0))],
        out_specs=[pl.BlockSpec((t, 2 * V_HEAD), lambda h, i: (i, h)), pl.BlockSpec((t, LANES), lambda h, i: (i, h)),
                   pl.BlockSpec((None, t, 2), lambda h, i: (h, i, 0))],
        out_shape=[jax.ShapeDtypeStruct((s, HEADS * V_HEAD), F32), jax.ShapeDtypeStruct((s, HEADS * QK_ROPE), BF16),
                   jax.ShapeDtypeStruct((hp, s, 2), F32)],
        compiler_params=_params("parallel", "parallel"))(qn, qp, kv, kpr, cos4, sin4)


def attn_bwd_q(qn, qpr, kv, kpr, o, do, lse, cos4, sin4):
    s = qn.shape[0]
    hp = HEADS // 2
    t = _tile(s, ATT_TILE)
    nq = s // t

    def body(qn_ref, qpr_ref, kv_ref, kp_ref, o_ref, do_ref, l_ref, c_ref, s_ref, dqn_ref, dqp_ref):
        qi = pl.program_id(1)
        qpr = qpr_ref[...]
        masks = _head_masks(qpr.shape)
        dqp = jnp.zeros(qpr.shape, F32)
        for hh in range(2):
            q_n = qn_ref[:, hh * QK_NOPE:(hh + 1) * QK_NOPE]
            q_p = jnp.where(masks[hh], qpr, jnp.zeros_like(qpr))
            kc, vc = 2 * hh * QK_NOPE, (2 * hh + 1) * QK_NOPE
            do_h = do_ref[:, hh * V_HEAD:(hh + 1) * V_HEAD]
            delta = jnp.sum(do_h * o_ref[:, hh * V_HEAD:(hh + 1) * V_HEAD], axis=-1, keepdims=True)
            do_b = do_h.astype(BF16)
            lse_h = l_ref[:, hh:hh + 1]

            def step(kb, carry):
                dn, dp_ = carry
                rows = pl.ds(pl.multiple_of(kb * t, t), t)
                k = kv_ref[rows, kc:kc + QK_NOPE]
                kp = kp_ref[rows, :]
                p = jnp.exp(_scores(q_n, q_p, k, kp, qi, kb, t) - lse_h)
                dpv = lax.dot_general(do_b, kv_ref[rows, vc:vc + V_HEAD], (((1,), (1,)), ((), ())), preferred_element_type=F32)
                ds = (p * (dpv - delta) * _ATT_SCALE).astype(BF16)
                dn = dn + jnp.dot(ds, k, preferred_element_type=F32)
                dp_ = dp_ + jnp.dot(ds, kp, preferred_element_type=F32)
                return dn, dp_

            dn, dp_h = lax.fori_loop(0, qi + 1, step, (jnp.zeros((t, QK_NOPE), F32), jnp.zeros((t, LANES), F32)))
            dqn_ref[:, hh * QK_NOPE:(hh + 1) * QK_NOPE] = dn.astype(dqn_ref.dtype)
            dqp = dqp + jnp.where(masks[hh], dp_h, jnp.zeros_like(dp_h))
        dqp_ref[...] = _rope(dqp, c_ref[...], -s_ref[...]).astype(dqp_ref.dtype)

    qblk = pl.BlockSpec((t, 2 * QK_NOPE), lambda h, i: (i, h))
    pblk = pl.BlockSpec((t, LANES), lambda h, i: (i, h))
    tab = pl.BlockSpec((t, LANES), lambda h, i: (i, 0))
    return pl.pallas_call(
        body, name="attn_bwd_q", grid=(hp, nq),
        in_specs=[qblk, pblk, pl.BlockSpec((s, 4 * QK_NOPE), lambda h, i: (0, h)), _full((s, LANES)), qblk, qblk,
                  pl.BlockSpec((None, t, 2), lambda h, i: (h, i, 0)), tab, tab],
        out_specs=[qblk, pblk],
        out_shape=[jax.ShapeDtypeStruct((s, HEADS * QK_NOPE), BF16), jax.ShapeDtypeStruct((s, HEADS * QK_ROPE), BF16)],
        compiler_params=_params("parallel", "parallel"))(qn, qpr, kv, kpr, o, do, lse, cos4, sin4)


def attn_bwd_kv(qn, qpr, kv, kpr, o, do, lse):
    s = qn.shape[0]
    hp = HEADS // 2
    t = _tile(s, ATT_TILE)
    nq = s // t

    def body(qn_ref, qpr_ref, kv_ref, kp_ref, o_ref, do_ref, l_ref, dkv_ref, dkp_ref):
        ki = pl.program_id(1)
        rows_k = pl.ds(pl.multiple_of(ki * t, t), t)
        kp = kp_ref[rows_k, :]
        dkp = jnp.zeros((t, LANES), F32)
        for hh in range(2):
            kc, vc = 2 * hh * QK_NOPE, (2 * hh + 1) * QK_NOPE
            k = kv_ref[rows_k, kc:kc + QK_NOPE]
            v = kv_ref[rows_k, vc:vc + V_HEAD]

            def step(qb, carry):
                dk, dv, dkp_h = carry
                rows = pl.ds(pl.multiple_of(qb * t, t), t)
                q_n = qn_ref[rows, hh * QK_NOPE:(hh + 1) * QK_NOPE]
                qpr = qpr_ref[rows, :]
                lane = lax.broadcasted_iota(jnp.int32, qpr.shape, 1)
                sel = (lane < QK_ROPE) if hh == 0 else (lane >= QK_ROPE)
                q_p = jnp.where(sel, qpr, jnp.zeros_like(qpr))
                do_h = do_ref[rows, hh * V_HEAD:(hh + 1) * V_HEAD]
                delta = jnp.sum(do_h * o_ref[rows, hh * V_HEAD:(hh + 1) * V_HEAD], axis=-1, keepdims=True)
                do_b = do_h.astype(BF16)
                p = jnp.exp(_scores(q_n, q_p, k, kp, qb, ki, t) - l_ref[rows, hh:hh + 1])
                dpv = lax.dot_general(do_b, v, (((1,), (1,)), ((), ())), preferred_element_type=F32)
                ds = (p * (dpv - delta) * _ATT_SCALE).astype(BF16)
                dv = dv + lax.dot_general(p.astype(BF16), do_b, (((0,), (0,)), ((), ())), preferred_element_type=F32)
                dk = dk + lax.dot_general(ds, q_n, (((0,), (0,)), ((), ())), preferred_element_type=F32)
                dkp_h = dkp_h + lax.dot_general(ds, q_p, (((0,), (0,)), ((), ())), preferred_element_type=F32)
                return dk, dv, dkp_h

            init = (jnp.zeros((t, QK_NOPE), F32), jnp.zeros((t, V_HEAD), F32), jnp.zeros((t, LANES), F32))
            dk, dv, dkp_h = lax.fori_loop(ki, nq, step, init)
            dkv_ref[:, kc:kc + QK_NOPE] = dk.astype(dkv_ref.dtype)
            dkv_ref[:, vc:vc + V_HEAD] = dv.astype(dkv_ref.dtype)
            dkp = dkp + dkp_h
        dkp_ref[...] = dkp

    return pl.pallas_call(
        body, name="attn_bwd_kv", grid=(hp, nq),
        in_specs=[pl.BlockSpec((s, 2 * QK_NOPE), lambda h, i: (0, h)), pl.BlockSpec((s, LANES), lambda h, i: (0, h)),
                  pl.BlockSpec((s, 4 * QK_NOPE), lambda h, i: (0, h)), _full((s, LANES)),
                  pl.BlockSpec((s, 2 * V_HEAD), lambda h, i: (0, h)), pl.BlockSpec((s, 2 * V_HEAD), lambda h, i: (0, h)),
                  pl.BlockSpec((None, s, 2), lambda h, i: (h, 0, 0))],
        out_specs=[pl.BlockSpec((t, 4 * QK_NOPE), lambda h, i: (i, h)), pl.BlockSpec((None, t, LANES), lambda h, i: (h, i, 0))],
        out_shape=[jax.ShapeDtypeStruct((s, HEADS * 2 * QK_NOPE), BF16), jax.ShapeDtypeStruct((hp, s, LANES), F32)],
        compiler_params=_params("parallel", "parallel"))(qn, qpr, kv, kpr, o, do, lse)


def _dot_nt(a, b):
    return lax.dot_general(a, b, (((1,), (1,)), ((), ())), preferred_element_type=F32)


def _dot_tn(a, b):
    return lax.dot_general(a, b, (((0,), (0,)), ((), ())), preferred_element_type=F32)


def _q_cat(q_n, qpr, hh):
    lane = lax.broadcasted_iota(jnp.int32, qpr.shape, 1)
    sel = (lane < QK_ROPE) if hh == 0 else (lane >= QK_ROPE)
    return jnp.concatenate([q_n, jnp.where(sel, qpr, jnp.zeros_like(qpr))], axis=1)


def _causal(sc):
    row = lax.broadcasted_iota(jnp.int32, sc.shape, 0)
    col = lax.broadcasted_iota(jnp.int32, sc.shape, 1)
    return jnp.where(col <= row, sc, _NEG)


def attn_fwd2(qn, qp, kv, kpr, cos4, sin4):
    s = qn.shape[0]
    hp = HEADS // 2
    t = _tile(s, ATT_TILE)
    nq = s // t

    def body(qn_ref, qp_ref, kv_ref, kp_ref, c_ref, s_ref, o_ref, qpr_ref, l_ref, kcat_ref):
        qi = pl.program_id(1)

        @pl.when(qi == 0)
        def _():
            for hh in range(2):
                kcat_ref[hh, :, 0:QK_NOPE] = kv_ref[:, 2 * hh * QK_NOPE:(2 * hh + 1) * QK_NOPE]
                kcat_ref[hh, :, QK_NOPE:] = kp_ref[...]

        qpr = _rope(qp_ref[...], c_ref[...], s_ref[...]).astype(BF16)
        qpr_ref[...] = qpr
        qcat = [_q_cat(qn_ref[:, hh * QK_NOPE:(hh + 1) * QK_NOPE], qpr, hh) for hh in range(2)]

        def block(kb, carry, diagonal):
            rows = pl.ds(pl.multiple_of(kb * t, t), t)
            out = []
            for hh in range(2):
                m, l, acc = carry[hh]
                sc = _dot_nt(qcat[hh], kcat_ref[hh, rows, :]) * _ATT_SCALE
                if diagonal:
                    sc = _causal(sc)
                m_new = jnp.maximum(m, jnp.max(sc, axis=-1, keepdims=True))
                alpha = jnp.exp(m - m_new)
                p = jnp.exp(sc - m_new)
                l = alpha * l + jnp.sum(p, axis=-1, keepdims=True)
                v = kv_ref[rows, (2 * hh + 1) * QK_NOPE:(2 * hh + 2) * QK_NOPE]
                acc = alpha * acc + jnp.dot(p.astype(BF16), v, preferred_element_type=F32)
                out.append((m_new, l, acc))
            return tuple(out)

        one = (jnp.full((t, 1), _NEG, F32), jnp.zeros((t, 1), F32), jnp.zeros((t, V_HEAD), F32))
        carry = lax.fori_loop(0, qi, lambda kb, cr: block(kb, cr, False), (one, one))
        carry = block(qi, carry, True)
        for hh in range(2):
            m, l, acc = carry[hh]
            o_ref[:, hh * V_HEAD:(hh + 1) * V_HEAD] = acc / l
            l_ref[:, hh:hh + 1] = m + jnp.log(l)

    return pl.pallas_call(
        body, name="attn_fwd", grid=(hp, nq),
        in_specs=[pl.BlockSpec((t, 2 * QK_NOPE), lambda h, i: (i, h)), pl.BlockSpec((t, LANES), lambda h, i: (i, h)),
                  pl.BlockSpec((s, 4 * QK_NOPE), lambda h, i: (0, h)), _full((s, LANES)),
                  pl.BlockSpec((t, LANES), lambda h, i: (i, 0)), pl.BlockSpec((t, LANES), lambda h, i: (i, 0))],
        out_specs=[pl.BlockSpec((t, 2 * V_HEAD), lambda h, i: (i, h)), pl.BlockSpec((t, LANES), lambda h, i: (i, h)),
                   pl.BlockSpec((None, t, 2), lambda h, i: (h, i, 0))],
        out_shape=[jax.ShapeDtypeStruct((s, HEADS * V_HEAD), F32), jax.ShapeDtypeStruct((s, HEADS * QK_ROPE), BF16),
                   jax.ShapeDtypeStruct((hp, s, 2), F32)],
        scratch_shapes=[pltpu.VMEM((2, s, 2 * QK_NOPE), BF16)],
        compiler_params=_params("parallel", "arbitrary"))(qn, qp, kv, kpr, cos4, sin4)


def attn_bwd2(qn, qpr, kv, kpr, o, do, lse, cos4, sin4):
    s = qn.shape[0]
    hp = HEADS // 2
    t = _tile(s, ATT_TILE)
    nk = s // t

    def body(qn_ref, qpr_ref, kv_ref, kp_ref, o_ref, do_ref, l_ref, c_ref, s_ref,
             dqn_ref, dqp_ref, dkv_ref, dkp_ref, qcat_ref, dq_ref, delta_ref):
        ki = pl.program_id(1)

        @pl.when(ki == 0)
        def _():
            dq_ref[...] = jnp.zeros_like(dq_ref)
            for hh in range(2):
                qcat_ref[hh] = _q_cat(qn_ref[:, hh * QK_NOPE:(hh + 1) * QK_NOPE], qpr_ref[...], hh)
                cols = slice(hh * V_HEAD, (hh + 1) * V_HEAD)
                delta_ref[hh] = jnp.sum(do_ref[:, cols] * o_ref[:, cols], axis=-1, keepdims=True)

        rows_k = pl.ds(pl.multiple_of(ki * t, t), t)
        kcat = [jnp.concatenate([kv_ref[rows_k, 2 * hh * QK_NOPE:(2 * hh + 1) * QK_NOPE], kp_ref[rows_k, :]], axis=1) for hh in range(2)]
        vs = [kv_ref[rows_k, (2 * hh + 1) * QK_NOPE:(2 * hh + 2) * QK_NOPE] for hh in range(2)]

        def block(qb, carry, diagonal):
            rows = pl.ds(pl.multiple_of(qb * t, t), t)
            out = []
            for hh in range(2):
                dkc, dv = carry[hh]
                q_c = qcat_ref[hh, rows, :]
                do_b = do_ref[rows, hh * V_HEAD:(hh + 1) * V_HEAD].astype(BF16)
                sc = _dot_nt(q_c, kcat[hh]) * _ATT_SCALE
                if diagonal:
                    sc = _causal(sc)
                p = jnp.exp(sc - l_ref[rows, hh:hh + 1])
                dpv = _dot_nt(do_b, vs[hh])
                ds = (p * (dpv - delta_ref[hh, rows, :]) * _ATT_SCALE).astype(BF16)
                dv = dv + _dot_tn(p.astype(BF16), do_b)
                dkc = dkc + _dot_tn(ds, q_c)
                dq_ref[hh, rows, :] += jnp.dot(ds, kcat[hh], preferred_element_type=F32)
                out.append((dkc, dv))
            return tuple(out)

        one = (jnp.zeros((t, 2 * QK_NOPE), F32), jnp.zeros((t, V_HEAD), F32))
        carry = block(ki, (one, one), True)
        carry = lax.fori_loop(ki + 1, nk, lambda qb, cr: block(qb, cr, False), carry)
        dkp = jnp.zeros((t, LANES), F32)
        for hh in range(2):
            dkc, dv = carry[hh]
            dkv_ref[:, 2 * hh * QK_NOPE:(2 * hh + 1) * QK_NOPE] = dkc[:, :QK_NOPE].astype(dkv_ref.dtype)
            dkv_ref[:, (2 * hh + 1) * QK_NOPE:(2 * hh + 2) * QK_NOPE] = dv.astype(dkv_ref.dtype)
            dkp = dkp + dkc[:, QK_NOPE:]
        dkp_ref[...] = dkp

        @pl.when(ki == nk - 1)
        def _():
            lane = lax.broadcasted_iota(jnp.int32, (s, LANES), 1)
            dqp = jnp.where(lane < QK_ROPE, dq_ref[0, :, QK_NOPE:], dq_ref[1, :, QK_NOPE:])
            dqp_ref[...] = _rope(dqp, c_ref[...], -s_ref[...]).astype(dqp_ref.dtype)
            for hh in range(2):
                dqn_ref[:, hh * QK_NOPE:(hh + 1) * QK_NOPE] = dq_ref[hh, :, :QK_NOPE].astype(dqn_ref.dtype)

    qblk = pl.BlockSpec((s, 2 * QK_NOPE), lambda h, i: (0, h))
    pblk = pl.BlockSpec((s, LANES), lambda h, i: (0, h))
    tab = _full((s, LANES))
    return pl.pallas_call(
        body, name="attn_bwd", grid=(hp, nk),
        in_specs=[qblk, pblk, pl.BlockSpec((s, 4 * QK_NOPE), lambda h, i: (0, h)), tab, qblk, qblk,
                  pl.BlockSpec((None, s, 2), lambda h, i: (h, 0, 0)), tab, tab],
        out_specs=[qblk, pblk, pl.BlockSpec((t, 4 * QK_NOPE), lambda h, i: (i, h)), pl.BlockSpec((None, t, LANES), lambda h, i: (h, i, 0))],
        out_shape=[jax.ShapeDtypeStruct((s, HEADS * QK_NOPE), BF16), jax.ShapeDtypeStruct((s, HEADS * QK_ROPE), BF16),
                   jax.ShapeDtypeStruct((s, HEADS * 2 * QK_NOPE), BF16), jax.ShapeDtypeStruct((hp, s, LANES), F32)],
        scratch_shapes=[pltpu.VMEM((2, s, 2 * QK_NOPE), BF16), pltpu.VMEM((2, s, 2 * QK_NOPE), F32), pltpu.VMEM((2, s, 1), F32)],
        compiler_params=_params("parallel", "arbitrary"))(qn, qpr, kv, kpr, o, do, lse, cos4, sin4)


def kpe_bwd(dkp, cos4, sin4, pad_cols):
    hp, s, _ = dkp.shape
    tr = _tile(s, ROW_TILE * 2)

    def body(d_ref, c_ref, s_ref, o_ref):
        tot = d_ref[0]
        for h in range(1, hp):
            tot = tot + d_ref[h]
        tot = tot + pltpu.roll(tot, QK_ROPE, 1)
        lane = lax.broadcasted_iota(jnp.int32, tot.shape, 1)
        dk = jnp.where(lane < QK_ROPE, _rope(tot, c_ref[...], -s_ref[...]), jnp.zeros_like(tot))
        o_ref[...] = jnp.zeros_like(o_ref)
        o_ref[:, 0:LANES] = dk.astype(o_ref.dtype)

    row = pl.BlockSpec((tr, LANES), lambda i: (i, 0))
    return pl.pallas_call(body, name="kpe_bwd", grid=(s // tr,),
                          in_specs=[pl.BlockSpec((hp, tr, LANES), lambda i: (0, i, 0)), row, row],
                          out_specs=pl.BlockSpec((tr, pad_cols), lambda i: (i, 0)),
                          out_shape=jax.ShapeDtypeStruct((s, pad_cols), BF16), compiler_params=_params("parallel"))(dkp, cos4, sin4)


def _shift_down(x, n):
    row = lax.broadcasted_iota(jnp.int32, x.shape, 0)
    return jnp.where(row >= n, pltpu.roll(x, n, 0), jnp.zeros_like(x))


def _shift_up(x, n):
    rows = x.shape[0]
    row = lax.broadcasted_iota(jnp.int32, x.shape, 0)
    return jnp.where(row < rows - n, pltpu.roll(x, rows - n, 0), jnp.zeros_like(x))


def _conv(x, w_ref, b_ref):
    return w_ref[2:3, :] * x + w_ref[1:2, :] * _shift_down(x, 1) + w_ref[0:1, :] * _shift_down(x, 2) + b_ref[...]


def conv_act_fwd(upre, conv_w, conv_b):
    s, f2 = upre.shape
    f = f2 // 2
    tc = _tile(f, COL_TILE)
    nc = f // tc

    def body(ug_ref, uv_ref, wg_ref, wv_ref, bg_ref, bv_ref, o_ref):
        gh = _conv(ug_ref[...], wg_ref, bg_ref)
        vh = _conv(uv_ref[...], wv_ref, bv_ref)
        o_ref[...] = (gh * _sigmoid(gh) * vh).astype(o_ref.dtype)

    def spec(rows, shift):
        return pl.BlockSpec((rows, tc), lambda j: (0, j + shift))

    return pl.pallas_call(
        body, name="conv_act_fwd", grid=(nc,),
        in_specs=[spec(s, 0), spec(s, nc), spec(3, 0), spec(3, nc), spec(1, 0), spec(1, nc)], out_specs=spec(s, 0),
        out_shape=jax.ShapeDtypeStruct((s, f), BF16), compiler_params=_params("parallel"))(upre, upre, conv_w, conv_w, conv_b, conv_b)


def conv_act_bwd(upre, conv_w, conv_b, df):
    s, f2 = upre.shape
    f = f2 // 2
    tc = _tile(f, COL_TILE)
    nc = f // tc

    def half(x, d, w_ref, du_ref, which, gw_ref, gb_ref):
        d1, d2 = _shift_up(d, 1), _shift_up(d, 2)
        gb_ref[...] = _colsum(d)
        gw_ref[2:3, :] = _colsum(d * x)
        gw_ref[1:2, :] = _colsum(d1 * x)
        gw_ref[0:1, :] = _colsum(d2 * x)
        du_ref[which] = (w_ref[2:3, :] * d + w_ref[1:2, :] * d1 + w_ref[0:1, :] * d2).astype(du_ref.dtype)

    def body(ug_ref, uv_ref, wg_ref, wv_ref, bg_ref, bv_ref, df_ref, du_ref, gwg_ref, gwv_ref, gbg_ref, gbv_ref):
        xg, xv = ug_ref[...], uv_ref[...]
        gh = _conv(xg, wg_ref, bg_ref)
        vh = _conv(xv, wv_ref, bv_ref)
        sg = _sigmoid(gh)
        df_v = df_ref[...]
        half(xg, df_v * vh * (sg * (1.0 + gh * (1.0 - sg))), wg_ref, du_ref, 0, gwg_ref, gbg_ref)
        half(xv, df_v * (gh * sg), wv_ref, du_ref, 1, gwv_ref, gbv_ref)

    def spec(rows, shift):
        return pl.BlockSpec((rows, tc), lambda j: (0, j + shift))

    gw = jax.ShapeDtypeStruct((3, f), F32)
    gb = jax.ShapeDtypeStruct((1, f), F32)
    return pl.pallas_call(
        body, name="conv_act_bwd", grid=(nc,),
        in_specs=[spec(s, 0), spec(s, nc), spec(3, 0), spec(3, nc), spec(1, 0), spec(1, nc), spec(s, 0)],
        out_specs=[pl.BlockSpec((2, s, tc), lambda j: (0, 0, j)), spec(3, 0), spec(3, 0), spec(1, 0), spec(1, 0)],
        out_shape=[jax.ShapeDtypeStruct((2, s, f), BF16), gw, gw, gb, gb],
        compiler_params=_params("parallel"))(upre, upre, conv_w, conv_w, conv_b, conv_b, df)


def _elementwise_tile(r, c, limit):
    if r % 8:
        return r, c
    best = (8, c if c % LANES else LANES)
    for k in (1, 2, 4, 8, 16):
        if k > 1 and c % (LANES * k):
            continue
        tc = c // k
        tr = max(8, min(r, limit // tc) // 8 * 8)
        while r % tr:
            tr -= 8
        if tr * tc <= max(limit, 8 * tc) and tr * tc > best[0] * best[1]:
            best = (tr, tc)
    return best


def adamw(name, w, m, v, parts):
    npart, r, c = parts.shape
    tr, tc = _elementwise_tile(r, c, ADAMW_TILE_ELEMS)
    bc1 = 1.0 - ADAM_B1 ** ADAM_STEP
    bc2 = 1.0 - ADAM_B2 ** ADAM_STEP

    def body(w_ref, m_ref, v_ref, p_ref, g_ref, d_ref, nm_ref, nv_ref):
        g = p_ref[0].astype(F32)
        for k in range(1, npart):
            g = g + p_ref[k].astype(F32)
        m_new = ADAM_B1 * m_ref[...] + (1.0 - ADAM_B1) * g
        v_new = ADAM_B2 * v_ref[...] + (1.0 - ADAM_B2) * (g * g)
        g_ref[...] = g
        nm_ref[...] = m_new
        nv_ref[...] = v_new
        d_ref[...] = -ADAM_LR * ((m_new / bc1) / (jnp.sqrt(v_new / bc2) + ADAM_EPS) + ADAM_WD * w_ref[...])

    deps = _TOKENS.take()
    blk = pl.BlockSpec((tr, tc), lambda i, j: (i, j))
    out = jax.ShapeDtypeStruct((r, c), F32)
    return pl.pallas_call(
        lambda *refs: body(*refs[:4], *refs[4 + len(deps):]), name=name, grid=(r // tr, c // tc),
        in_specs=[blk, blk, blk, pl.BlockSpec((npart, tr, tc), lambda i, j: (0, i, j))] + [pl.BlockSpec(memory_space=pl.ANY)] * len(deps),
        out_specs=[blk, blk, blk, blk], out_shape=[out, out, out, out],
        compiler_params=_params("parallel", "parallel"))(w, m, v, parts, *deps)


def _position():
    return lax.axis_index("x"), lax.axis_index("y"), lax.axis_index("c")


def _index(p):
    return 4 * p[0] + 2 * p[1] + p[2]


def _peer(me, r):
    return (me[0] ^ ((r >> 2) & 1), me[1] ^ ((r >> 1) & 1), me[2] ^ (r & 1))


_ANY = pl.BlockSpec(memory_space=pl.ANY)


def all_gather_two_level(shards):
    n = len(shards)

    def body(*refs):
        ins, outs = refs[:n], refs[n:2 * n]
        send_sems, recv_sems, local_sems = refs[2 * n:]
        x, y, c = _position()
        me, sibling = (x, y, c), (x, y, 1 - c)
        chips = [(1 - x, y), (x, 1 - y), (1 - x, 1 - y)]

        def copy(w, k, block, to, src=None):
            slot = outs[w].at[_index(block)]
            return pltpu.make_async_remote_copy(src_ref=slot if src is None else src, dst_ref=slot,
                                                send_sem=send_sems.at[7 * w + k], recv_sem=recv_sems.at[7 * w + k],
                                                device_id=to, device_id_type=MESH)

        mine = [pltpu.make_async_copy(ins[w], outs[w].at[_index(me)], local_sems.at[w]) for w in range(n)]
        for cp in mine:
            cp.start()
        first = []
        for w in range(n):
            first.append(copy(w, 0, me, sibling, src=ins[w]))
            first += [copy(w, 1 + j, me, (*chip, c), src=ins[w]) for j, chip in enumerate(chips)]
        for cp in first:
            cp.start()
        passed = []
        for w in range(n):
            for j, chip in enumerate(chips):
                copy(w, 1 + j, (*chip, c), me).wait_recv()
                cp = copy(w, 4 + j, (*chip, c), sibling)
                cp.start()
                passed.append(cp)
        for w in range(n):
            copy(w, 0, sibling, me).wait_recv()
            for j, chip in enumerate(chips):
                copy(w, 4 + j, (*chip, 1 - c), me).wait_recv()
        for cp in first + passed:
            cp.wait_send()
        for cp in mine:
            cp.wait()

    return pl.pallas_call(
        body, name="all_gather_weights",
        out_shape=[jax.ShapeDtypeStruct((N_DEV,) + a.shape, a.dtype) for a in shards],
        in_specs=[_ANY] * n, out_specs=[_ANY] * n,
        scratch_shapes=[pltpu.SemaphoreType.DMA((7 * n,)), pltpu.SemaphoreType.DMA((7 * n,)), pltpu.SemaphoreType.DMA((n,))],
        )(*shards)


def exchange(name, arrays, scatter):
    n = len(arrays)

    def body(*refs):
        ins, outs = refs[:n], refs[n:2 * n]
        send_sems, recv_sems, local_sems = refs[2 * n:]
        me = _position()
        copies = []
        for w in range(n):
            src = ins[w].at[_index(me)] if scatter else ins[w]
            cp = pltpu.make_async_copy(src, outs[w].at[_index(me)], local_sems.at[w])
            cp.start()
            copies.append(cp)
        remote = []
        for w in range(n):
            for r in range(1, N_DEV):
                peer = _peer(me, r)
                src = ins[w].at[_index(peer)] if scatter else ins[w]
                cp = pltpu.make_async_remote_copy(src_ref=src, dst_ref=outs[w].at[_index(me)],
                                                  send_sem=send_sems.at[7 * w + r - 1], recv_sem=recv_sems.at[7 * w + r - 1],
                                                  device_id=peer, device_id_type=MESH)
                cp.start()
                remote.append(cp)
        for cp in remote:
            cp.wait()
        for cp in copies:
            cp.wait()

    blocks = [a.shape[1:] if scatter else a.shape for a in arrays]
    return pl.pallas_call(
        body, name=name,
        out_shape=[jax.ShapeDtypeStruct((N_DEV,) + b, a.dtype) for a, b in zip(arrays, blocks)],
        in_specs=[_ANY] * n, out_specs=[_ANY] * n,
        scratch_shapes=[pltpu.SemaphoreType.DMA((7 * n,)), pltpu.SemaphoreType.DMA((7 * n,)), pltpu.SemaphoreType.DMA((n,))],
        )(*arrays)


_HBM = pl.BlockSpec(memory_space=pltpu.HBM)
_SEM = pl.BlockSpec(memory_space=pltpu.SEMAPHORE)
_EFFECT = pltpu.SideEffectType.DATAFLOW_SIDE_EFFECTING


def _direct_copies(ins, lands, send_sems, recv_sems, scatter):
    me = _position()
    copies = []
    for w in range(len(ins)):
        for r in range(1, N_DEV):
            peer = _peer(me, r)
            src = ins[w].at[_index(peer)] if scatter else ins[w]
            copies.append(pltpu.make_async_remote_copy(src_ref=src, dst_ref=lands[w].at[_index(me)], send_sem=send_sems.at[7 * w + r - 1],
                                                       recv_sem=recv_sems.at[7 * w + r - 1], device_id=peer, device_id_type=MESH))
    return copies


def exchange_start(name, groups, scatter):
    arrays = [a for g in groups for a in g]
    n = len(arrays)
    blocks = [a.shape[1:] if scatter else a.shape for a in arrays]
    lands = [lax.empty((N_DEV,) + b, a.dtype) for a, b in zip(arrays, blocks)]
    ng = len(groups)

    def body(*refs):
        ins, lnd = refs[:n], refs[n:2 * n]
        sems = refs[2 * n:2 * n + 2 * ng]
        token = refs[2 * n + 2 * ng + 2 * n]
        local_sem = refs[2 * n + 2 * ng + 2 * n + 1]
        me = _position()
        local = []
        for w in range(n):
            src = ins[w].at[_index(me)] if scatter else ins[w]
            cp = pltpu.make_async_copy(src, lnd[w].at[_index(me)], local_sem.at[w])
            cp.start()
            local.append(cp)
        w0 = 0
        for gi, g in enumerate(groups):
            for cp in _direct_copies(ins[w0:w0 + len(g)], lnd[w0:w0 + len(g)], sems[2 * gi], sems[2 * gi + 1], scatter):
                cp.start()
            w0 += len(g)
        for cp in local:
            cp.wait()
        token[...] = jnp.zeros_like(token)

    sem_shapes = []
    for g in groups:
        sem_shapes += [pltpu.SemaphoreType.DMA((7 * len(g),)), pltpu.SemaphoreType.DMA((7 * len(g),))]
    out = pl.pallas_call(
        body, name=name,
        out_shape=tuple(sem_shapes) + tuple(pltpu.HBM(a.shape, a.dtype) for a in arrays) + tuple(pltpu.HBM(l.shape, l.dtype) for l in lands)
        + (jax.ShapeDtypeStruct((8, LANES), F32),),
        in_specs=[_HBM] * (2 * n), out_specs=tuple([_SEM] * (2 * ng) + [_HBM] * (2 * n) + [pl.BlockSpec(memory_space=pltpu.VMEM)]),
        input_output_aliases={i: 2 * ng + i for i in range(2 * n)},
        scratch_shapes=[pltpu.SemaphoreType.DMA((n,))],
        compiler_params=pltpu.CompilerParams(has_side_effects=_EFFECT),
    )(*[pltpu.with_memory_space_constraint(a, pltpu.HBM) for a in arrays], *[pltpu.with_memory_space_constraint(l, pltpu.HBM) for l in lands])
    sems, thru, token = out[:2 * ng], out[2 * ng:2 * ng + 2 * n], out[-1]
    res, w0 = [], 0
    for gi, g in enumerate(groups):
        res.append((sems[2 * gi], sems[2 * gi + 1], list(thru[w0:w0 + len(g)]), list(thru[n + w0:n + w0 + len(g)])))
        w0 += len(g)
    return res, token


def exchange_wait(name, group, after, scatter):
    send_sems, recv_sems, srcs, lands = group
    n = len(srcs)

    def body(*refs):
        ins, lnd = refs[:n], refs[n:2 * n]
        for cp in _direct_copies(ins, lnd, refs[2 * n], refs[2 * n + 1], scatter):
            cp.wait_send()
            cp.wait_recv()

    out = pl.pallas_call(
        body, name=name, out_shape=tuple(pltpu.HBM(a.shape, a.dtype) for a in srcs + lands),
        in_specs=[_HBM] * (2 * n) + [_SEM, _SEM, pl.BlockSpec(memory_space=pl.ANY)], out_specs=tuple([_HBM] * (2 * n)),
        input_output_aliases={i: i for i in range(2 * n)},
        compiler_params=pltpu.CompilerParams(has_side_effects=_EFFECT),
    )(*srcs, *lands, send_sems, recv_sems, after)
    return list(out[n:])


def _after(x, token):
    return lax.optimization_barrier((x, token))[0]


_TOKEN = jax.ShapeDtypeStruct((8, LANES), F32)
_VM = pl.BlockSpec(memory_space=pltpu.VMEM)
_SIDE = pltpu.CompilerParams(has_side_effects=_EFFECT)


def _hbm(a):
    return pltpu.with_memory_space_constraint(a, pltpu.HBM)


def _like(a):
    return pltpu.HBM(a.shape, a.dtype)


def _dma_sems(n):
    return pltpu.SemaphoreType.DMA((n,))


def _other_chips(x, y):
    return [(1 - x, y), (x, 1 - y), (1 - x, 1 - y)]


COPY_STREAMS = 8


def _row_chunks(src, dst):
    rows = src.shape[0]
    n = COPY_STREAMS
    while n > 1 and rows % (16 * n):
        n //= 2
    r = rows // n
    return [(src.at[pl.ds(i * r, r)], dst.at[pl.ds(i * r, r)]) for i in range(n)]


def _local_copy(src, dst, sem):
    return [pltpu.make_async_copy(s, d, sem) for s, d in _row_chunks(src, dst)]


class _rcopy:
    def __init__(self, src, dst, send_sem, recv_sem, to):
        self.parts = [pltpu.make_async_remote_copy(src_ref=s, dst_ref=d, send_sem=send_sem, recv_sem=recv_sem, device_id=to, device_id_type=MESH)
                      for s, d in _row_chunks(src, dst)]

    def start(self):
        for cp in self.parts:
            cp.start()

    def wait_send(self):
        for cp in self.parts:
            cp.wait_send()

    def wait_recv(self):
        for cp in self.parts:
            cp.wait_recv()


def _afters(after):
    return list(after) if isinstance(after, (list, tuple)) else [after]


def ag_start(name, shards, after):
    n = len(shards)
    lands = [lax.empty((N_DEV,) + a.shape, a.dtype) for a in shards]
    afters = _afters(after)
    na = len(afters)

    def body(*refs):
        ins, lnd, send_sems, recv_sems, token = refs[:n], refs[n:2 * n], refs[2 * n + na], refs[2 * n + na + 1], refs[4 * n + na + 2]
        x, y, c = _position()
        for w in range(n):
            slot = lnd[w].at[_index((x, y, c))]
            for k, to in enumerate([(x, y, 1 - c)] + [(*chip, c) for chip in _other_chips(x, y)]):
                _rcopy(ins[w], slot, send_sems.at[4 * w + k], recv_sems.at[4 * w + k], to).start()
        token[...] = jnp.zeros_like(token)

    out = pl.pallas_call(
        body, name=name, out_shape=(_dma_sems(4 * n), _dma_sems(4 * n)) + tuple(_like(a) for a in shards + lands) + (_TOKEN,),
        in_specs=[_HBM] * (2 * n) + [_ANY] * na, out_specs=(_SEM, _SEM) + (_HBM,) * (2 * n) + (_VM,),
        input_output_aliases={i: 2 + i for i in range(2 * n)}, compiler_params=_SIDE)(*[_hbm(a) for a in shards + lands], *afters)
    _TOKENS.push(out[-1])
    return out[0], out[1], list(out[2:2 + n]), list(out[2 + n:2 + 2 * n])


def ag_forward(name, started, after):
    send, recv, shards, lands = started
    n = len(shards)
    afters = list(after) if isinstance(after, (list, tuple)) else [after]
    na = len(afters)

    def body(*refs):
        ins, lnd, send_sems, recv_sems = refs[:n], refs[n:2 * n], refs[2 * n], refs[2 * n + 1]
        fsend, frecv, token = refs[2 * n + 2 + na], refs[2 * n + 3 + na], refs[4 * n + 4 + na]
        x, y, c = _position()
        for w in range(n):
            for j, chip in enumerate(_other_chips(x, y)):
                slot = lnd[w].at[_index((*chip, c))]
                _rcopy(ins[w], slot, send_sems.at[4 * w + 1 + j], recv_sems.at[4 * w + 1 + j], (*chip, c)).wait_recv()
                _rcopy(slot, slot, fsend.at[3 * w + j], frecv.at[3 * w + j], (x, y, 1 - c)).start()
        token[...] = jnp.zeros_like(token)

    out = pl.pallas_call(
        body, name=name, out_shape=(_dma_sems(3 * n), _dma_sems(3 * n)) + tuple(_like(a) for a in shards + lands) + (_TOKEN,),
        in_specs=[_HBM] * (2 * n) + [_SEM, _SEM] + [_ANY] * na, out_specs=(_SEM, _SEM) + (_HBM,) * (2 * n) + (_VM,),
        input_output_aliases={i: 2 + i for i in range(2 * n)}, compiler_params=_SIDE)(*shards, *lands, send, recv, *afters)
    _TOKENS.push(out[-1])
    return send, recv, out[0], out[1], list(out[2:2 + n]), list(out[2 + n:2 + 2 * n])


def ag_wait(name, forwarded, after):
    send, recv, fsend, frecv, shards, lands = forwarded
    n = len(shards)

    def body(*refs):
        ins, lnd, send_sems, recv_sems, fsend_r, frecv_r = refs[:n], refs[n:2 * n], refs[2 * n], refs[2 * n + 1], refs[2 * n + 2], refs[2 * n + 3]
        x, y, c = _position()
        sibling = (x, y, 1 - c)
        for w in range(n):
            own = lnd[w].at[_index((x, y, c))]
            _rcopy(ins[w], lnd[w].at[_index(sibling)], send_sems.at[4 * w], recv_sems.at[4 * w], sibling).wait_recv()
            for j, chip in enumerate(_other_chips(x, y)):
                _rcopy(ins[w], lnd[w].at[_index((*chip, 1 - c))], fsend_r.at[3 * w + j], frecv_r.at[3 * w + j], sibling).wait_recv()
            for k in range(4):
                _rcopy(ins[w], own, send_sems.at[4 * w + k], recv_sems.at[4 * w + k], sibling).wait_send()
            for j in range(3):
                _rcopy(ins[w], own, fsend_r.at[3 * w + j], frecv_r.at[3 * w + j], sibling).wait_send()

    out = pl.pallas_call(
        body, name=name, out_shape=tuple(_like(a) for a in shards + lands),
        in_specs=[_HBM] * (2 * n) + [_SEM] * 4 + [_ANY] * len(_afters(after)),
        out_specs=(_HBM,) * (2 * n), input_output_aliases={i: i for i in range(2 * n)},
        compiler_params=_SIDE)(*shards, *lands, send, recv, fsend, frecv, *_afters(after))
    return [lax.dynamic_update_index_in_dim(land, shard, _index(_position()), 0) for shard, land in zip(out[:n], out[n:])]


def rs_d2d_start(name, grads):
    n = len(grads)
    lands = [lax.empty((4,) + g.shape[1:], g.dtype) for g in grads]

    def body(*refs):
        ins, lnd, send_sems, recv_sems, token = refs[:n], refs[n:2 * n], refs[2 * n], refs[2 * n + 1], refs[4 * n + 2]
        x, y, c = _position()
        for w in range(n):
            for i in range(4):
                _rcopy(ins[w].at[2 * i + 1 - c], lnd[w].at[i], send_sems.at[4 * w + i], recv_sems.at[4 * w + i], (x, y, 1 - c)).start()
        token[...] = jnp.zeros_like(token)

    out = pl.pallas_call(
        body, name=name, out_shape=(_dma_sems(4 * n), _dma_sems(4 * n)) + tuple(_like(a) for a in grads + lands) + (_TOKEN,),
        in_specs=[_HBM] * (2 * n), out_specs=(_SEM, _SEM) + (_HBM,) * (2 * n) + (_VM,),
        input_output_aliases={i: 2 + i for i in range(2 * n)}, compiler_params=_SIDE)(*[_hbm(a) for a in grads + lands])
    _TOKENS.push(out[-1])
    return out[0], out[1], list(out[2:2 + n]), list(out[2 + n:2 + 2 * n])


def rs_d2d_wait(name, started, after):
    send, recv, grads, lands = started
    n = len(grads)

    def body(*refs):
        ins, lnd, send_sems, recv_sems = refs[:n], refs[n:2 * n], refs[2 * n], refs[2 * n + 1]
        x, y, c = _position()
        for w in range(n):
            for i in range(4):
                cp = _rcopy(ins[w].at[2 * i + 1 - c], lnd[w].at[i], send_sems.at[4 * w + i], recv_sems.at[4 * w + i], (x, y, 1 - c))
                cp.wait_send()
                cp.wait_recv()

    out = pl.pallas_call(
        body, name=name, out_shape=tuple(_like(a) for a in grads + lands),
        in_specs=[_HBM] * (2 * n) + [_SEM, _SEM] + [_ANY] * len(_afters(after)),
        out_specs=(_HBM,) * (2 * n), input_output_aliases={i: i for i in range(2 * n)},
        compiler_params=_SIDE)(*grads, *lands, send, recv, *_afters(after))
    return list(out[:n]), list(out[n:])


def pair_sum(name, grad, land, core):
    _, r, c = grad.shape
    tr = r
    if r % 8 == 0:
        tr = max(8, min(r, 4 * ADAMW_TILE_ELEMS // c) // 8 * 8)
        while r % tr:
            tr -= 8

    def body(core_ref, a_ref, b_ref, o_ref):
        o_ref[...] = (a_ref[...].astype(F32) + b_ref[...].astype(F32)).astype(o_ref.dtype)

    return pl.pallas_call(
        body, name=name, out_shape=jax.ShapeDtypeStruct((4, r, c), grad.dtype),
        grid_spec=pltpu.PrefetchScalarGridSpec(
            num_scalar_prefetch=1, grid=(4, r // tr),
            in_specs=[pl.BlockSpec((None, None, tr, c), lambda i, j, core_ref: (i, core_ref[0], j, 0)),
                      pl.BlockSpec((None, tr, c), lambda i, j, core_ref: (i, j, 0))],
            out_specs=pl.BlockSpec((None, tr, c), lambda i, j, core_ref: (i, j, 0))),
        compiler_params=_params("parallel", "parallel"))(core, grad.reshape(4, 2, r, c), land)


def rs_ici_start(name, sums):
    n = len(sums)
    lands = [lax.empty(a.shape, a.dtype) for a in sums]

    def body(*refs):
        ins, lnd, send_sems, recv_sems, token = refs[:n], refs[n:2 * n], refs[2 * n], refs[2 * n + 1], refs[4 * n + 2]
        x, y, c = _position()
        chip = 2 * x + y
        for w in range(n):
            for j, other in enumerate(_other_chips(x, y)):
                _rcopy(ins[w].at[2 * other[0] + other[1]], lnd[w].at[chip], send_sems.at[3 * w + j], recv_sems.at[3 * w + j], (*other, c)).start()
        token[...] = jnp.zeros_like(token)

    out = pl.pallas_call(
        body, name=name, out_shape=(_dma_sems(3 * n), _dma_sems(3 * n)) + tuple(_like(a) for a in sums + lands) + (_TOKEN,),
        in_specs=[_HBM] * (2 * n), out_specs=(_SEM, _SEM) + (_HBM,) * (2 * n) + (_VM,),
        input_output_aliases={i: 2 + i for i in range(2 * n)}, compiler_params=_SIDE)(*[_hbm(a) for a in sums + lands])
    _TOKENS.push(out[-1])
    return out[0], out[1], list(out[2:2 + n]), list(out[2 + n:2 + 2 * n])


def rs_ici_wait(name, started, after):
    send, recv, sums, lands = started
    n = len(sums)

    def body(*refs):
        ins, lnd, send_sems, recv_sems = refs[:n], refs[n:2 * n], refs[2 * n], refs[2 * n + 1]
        x, y, c = _position()
        for w in range(n):
            for j, other in enumerate(_other_chips(x, y)):
                cp = _rcopy(ins[w].at[2 * other[0] + other[1]], lnd[w].at[2 * other[0] + other[1]], send_sems.at[3 * w + j], recv_sems.at[3 * w + j], (*other, c))
                cp.wait_send()
                cp.wait_recv()

    out = pl.pallas_call(
        body, name=name, out_shape=tuple(_like(a) for a in sums + lands), in_specs=[_HBM] * (2 * n) + [_SEM, _SEM, _ANY],
        out_specs=(_HBM,) * (2 * n), input_output_aliases={i: i for i in range(2 * n)}, compiler_params=_SIDE)(*sums, *lands, send, recv, after)
    chip = 2 * lax.axis_index("x") + lax.axis_index("y")
    return [lax.dynamic_update_index_in_dim(land, lax.dynamic_index_in_dim(s, chip, 0, keepdims=False), chip, 0)
            for s, land in zip(out[:n], out[n:])]


def ada_fwd(c, w_ada, b_ada3, conv_w):
    d, cs = w_ada.shape

    def body(c_ref, w_ref, b_ref, cw_ref, mod_ref, sc_ref, cwa_ref, part_ref, send_sems, recv_sems):
        me = _position()
        my = _index(me)
        cv = c_ref[...]
        sc_ref[my] = cv * _sigmoid(cv)
        cwa_ref[my] = cw_ref[...]
        gather = []
        for r in range(1, N_DEV):
            for k, ref in enumerate((sc_ref, cwa_ref)):
                cp = pltpu.make_async_remote_copy(src_ref=ref.at[my], dst_ref=ref.at[my], send_sem=send_sems.at[14 * k + r - 1],
                                                  recv_sem=recv_sems.at[14 * k + r - 1], device_id=_peer(me, r), device_id_type=MESH)
                cp.start()
                gather.append(cp)
        for cp in gather:
            cp.wait()
        sc_all = jnp.concatenate([sc_ref[k] for k in range(N_DEV)], axis=0).astype(BF16)
        part = jnp.dot(sc_all, w_ref[...].astype(BF16), preferred_element_type=F32)
        for k in range(N_DEV):
            part_ref[k] = part[k:k + 1, :]
        scatter = []
        for r in range(1, N_DEV):
            peer = _peer(me, r)
            cp = pltpu.make_async_remote_copy(src_ref=part_ref.at[_index(peer)], dst_ref=mod_ref.at[my], send_sem=send_sems.at[6 + r],
                                              recv_sem=recv_sems.at[6 + r], device_id=peer, device_id_type=MESH)
            cp.start()
            scatter.append(cp)
        mod_ref[my] = part_ref[my]
        for cp in scatter:
            cp.wait()
        mod_ref[...] = mod_ref[...] + b_ref[...]

    vm = pl.BlockSpec(memory_space=pltpu.VMEM)
    return pl.pallas_call(
        body, name="ada_fwd",
        out_shape=[jax.ShapeDtypeStruct((N_DEV, 1, cs), F32), jax.ShapeDtypeStruct((N_DEV, 1, d), F32),
                   jax.ShapeDtypeStruct((N_DEV,) + conv_w.shape, F32)],
        in_specs=[vm, vm, vm, vm], out_specs=[vm, vm, vm],
        scratch_shapes=[pltpu.VMEM((N_DEV, 1, cs), F32), pltpu.SemaphoreType.DMA((21,)), pltpu.SemaphoreType.DMA((21,))],
        compiler_params=pltpu.CompilerParams(vmem_limit_bytes=VMEM_LIMIT_BYTES))(c, w_ada, b_ada3, conv_w)


def ada_bwd_w(sc_all, dmod_cols):
    _, d = sc_all.shape
    cs = dmod_cols.shape[1]
    tr = _tile(d, ROW_TILE)

    def body(sc_ref, dm_ref, o_ref):
        dm = dm_ref[...].astype(BF16)
        o_ref[...] = lax.dot_general(sc_ref[...].astype(BF16), dm, (((0,), (0,)), ((), ())), preferred_element_type=F32)

    return pl.pallas_call(body, name="ada_bwd_w", grid=(d // tr,),
                          in_specs=[pl.BlockSpec((N_DEV, tr), lambda i: (0, i)), _full((N_DEV, cs))],
                          out_specs=pl.BlockSpec((None, tr, cs), lambda i: (0, i, 0)),
                          out_shape=jax.ShapeDtypeStruct((1, d, cs), F32), compiler_params=_params("parallel"))(sc_all, dmod_cols)


def _round_up(n, m):
    return (n + m - 1) // m * m


def kernel(x, c, positions, w_ada, b_ada, pre_norm1_g, w_in, gm_ln_g, gm_ln_b, gm_w_s, gm_b_s, w_branch_a, q_norm_g, w_uq, kv_norm_g, w_ukv, w_branch_b, w_out, post_norm1_g, pre_norm2_g, w_up, conv_w, conv_b, w_down, post_norm2_g, loss_target, m_w_ada, m_b_ada, m_pre_norm1_g, m_w_in, m_gm_ln_g, m_gm_ln_b, m_gm_w_s, m_gm_b_s, m_w_branch_a, m_q_norm_g, m_w_uq, m_kv_norm_g, m_w_ukv, m_w_branch_b, m_w_out, m_post_norm1_g, m_pre_norm2_g, m_w_up, m_conv_w, m_conv_b, m_w_down, m_post_norm2_g, v_w_ada, v_b_ada, v_pre_norm1_g, v_w_in, v_gm_ln_g, v_gm_ln_b, v_gm_w_s, v_gm_b_s, v_w_branch_a, v_q_norm_g, v_w_uq, v_kv_norm_g, v_w_ukv, v_w_branch_b, v_w_out, v_post_norm1_g, v_pre_norm2_g, v_w_up, v_conv_w, v_conv_b, v_w_down, v_post_norm2_g):
    weights = dict(w_ada=w_ada, b_ada=b_ada, pre_norm1_g=pre_norm1_g, w_in=w_in, gm_ln_g=gm_ln_g, gm_ln_b=gm_ln_b, gm_w_s=gm_w_s,
                   gm_b_s=gm_b_s, w_branch_a=w_branch_a, q_norm_g=q_norm_g, w_uq=w_uq, kv_norm_g=kv_norm_g, w_ukv=w_ukv,
                   w_branch_b=w_branch_b, w_out=w_out, post_norm1_g=post_norm1_g, pre_norm2_g=pre_norm2_g, w_up=w_up, conv_w=conv_w,
                   conv_b=conv_b, w_down=w_down, post_norm2_g=post_norm2_g)
    mom1 = dict(w_ada=m_w_ada, b_ada=m_b_ada, pre_norm1_g=m_pre_norm1_g, w_in=m_w_in, gm_ln_g=m_gm_ln_g, gm_ln_b=m_gm_ln_b,
                gm_w_s=m_gm_w_s, gm_b_s=m_gm_b_s, w_branch_a=m_w_branch_a, q_norm_g=m_q_norm_g, w_uq=m_w_uq, kv_norm_g=m_kv_norm_g,
                w_ukv=m_w_ukv, w_branch_b=m_w_branch_b, w_out=m_w_out, post_norm1_g=m_post_norm1_g, pre_norm2_g=m_pre_norm2_g,
                w_up=m_w_up, conv_w=m_conv_w, conv_b=m_conv_b, w_down=m_w_down, post_norm2_g=m_post_norm2_g)
    mom2 = dict(w_ada=v_w_ada, b_ada=v_b_ada, pre_norm1_g=v_pre_norm1_g, w_in=v_w_in, gm_ln_g=v_gm_ln_g, gm_ln_b=v_gm_ln_b,
                gm_w_s=v_gm_w_s, gm_b_s=v_gm_b_s, w_branch_a=v_w_branch_a, q_norm_g=v_q_norm_g, w_uq=v_w_uq, kv_norm_g=v_kv_norm_g,
                w_ukv=v_w_ukv, w_branch_b=v_w_branch_b, w_out=v_w_out, post_norm1_g=v_post_norm1_g, pre_norm2_g=v_pre_norm2_g,
                w_up=v_w_up, conv_w=v_conv_w, conv_b=v_conv_b, w_down=v_w_down, post_norm2_g=v_post_norm2_g)
    order = list(weights)
    _TOKENS.clear()

    s, d = x.shape[1], x.shape[2]
    gmw = gm_ln_g.shape[0]
    groups = gmw // CHUNK
    ql, kvl = q_norm_g.shape[0], kv_norm_g.shape[0]
    f2 = conv_b.shape[0]
    in_cols = w_in.shape[1] * N_DEV
    o_q, o_kv, o_ga, o_gb, o_kpe = 2 * gmw, 2 * gmw + ql, 2 * gmw + ql + kvl, 2 * gmw + ql + kvl + d, 2 * gmw + ql + kvl + 2 * d
    zp = _round_up(o_kpe + LANES, Z_PAD)
    src_kpe = 2 * gmw + ql + kvl
    assert src_kpe + QK_ROPE + 2 * d == in_cols
    my = 4 * lax.axis_index("x") + 2 * lax.axis_index("y") + lax.axis_index("c")

    x2, tgt = x[0], loss_target[0]
    row = lambda a: a.reshape(1, -1)

    big = ["w_in", "w_branch_a", "w_uq", "w_ukv", "w_branch_b", "w_out", "w_up", "w_down"]
    sh = {k: weights[k].astype(BF16) for k in big[1:]}
    mix = ["w_branch_a", "w_uq", "w_ukv", "w_branch_b", "w_out"]
    ag_in = ag_start("ag_start_in", [w_in.T.astype(BF16)], c)

    mod8, sc_all3, g_cw = ada_fwd(c, w_ada, b_ada.reshape(N_DEV, 1, -1), conv_w)
    mod = mod8.reshape(N_MOD, d)
    shift1, scale1, gate1, shift2, scale2, gate2 = (mod[i:i + 1] for i in range(N_MOD))
    sc_all = sc_all3.reshape(N_DEV, d)
    h1 = norm_mod_fwd("pre1_fwd", x2, row(pre_norm1_g), scale1, shift1)

    inv = ROPE_THETA ** (-jnp.arange(0, QK_ROPE, 2, dtype=F32) / QK_ROPE)
    ang = positions[0].astype(F32)[:, None] * inv
    cos4 = jnp.tile(jnp.cos(ang), (1, 4))
    sin4 = jnp.tile(jnp.concatenate([-jnp.sin(ang), jnp.sin(ang)], axis=1), (1, 2))

    wm = (gm_w_s * jnp.tril(jnp.ones((CHUNK, CHUNK), F32))).astype(BF16)
    bs3 = gm_b_s.reshape(groups, CHUNK, 1)
    ln_g, ln_b = row(gm_ln_g), row(gm_ln_b)

    early = [h1, cos4, sin4, wm] + [sh[k] for k in big[1:]]
    ag_in = ag_forward("ag_forward_in", ag_in, early)
    ag_mix = ag_start("ag_start_mix", [sh[k] for k in mix], _TOKENS.pending[-1])
    (g_in,) = ag_wait("ag_wait_in", ag_in, [h1, _TOKENS.pending[-1]])
    w_in_f = g_in.reshape(in_cols, d)
    w_in_p = jnp.concatenate([w_in_f[:src_kpe], w_in_f[src_kpe + QK_ROPE:], w_in_f[src_kpe:src_kpe + QK_ROPE],
                              jnp.zeros((zp - in_cols, d), BF16)], axis=0)

    z = mm_nt("z_proj", h1, w_in_p, F32)
    ag_mix = ag_forward("ag_forward_mix", ag_mix, z)
    ag_up = ag_start("ag_start_up", [sh["w_up"]], _TOKENS.pending[-1])
    a = gmlp_fwd(z, gmw, ln_g, ln_b, wm, bs3)
    g_a, g_uq, g_ukv, g_b, g_out = ag_wait("ag_wait_mix", ag_mix, [a, _TOKENS.pending[-1]])
    w_a_f, w_b_f, w_out_f = g_a.reshape(-1, d), g_b.reshape(-1, d), g_out.reshape(-1, d)
    w_uq_f = g_uq.transpose(1, 0, 2).reshape(ql, HEADS, QK_NOPE + QK_ROPE)
    w_uq_n = w_uq_f[:, :, :QK_NOPE].reshape(ql, HEADS * QK_NOPE)
    w_uq_r = w_uq_f[:, :, QK_NOPE:].reshape(ql, HEADS * QK_ROPE)
    y_a = mm_nn("branch_a", a, w_a_f, F32)
    qln = rms_fwd_cols("q_norm", z, o_q, ql, row(q_norm_g))
    kvn = rms_fwd_cols("kv_norm", z, o_kv, kvl, row(kv_norm_g))
    qn = mm_nn("q_nope", qln, w_uq_n, BF16)
    qp = mm_nn("q_rope", qln, w_uq_r, F32)
    kv = mm_nn_b3("kv_up", kvn, g_ukv, BF16)
    kpr = rope_k(z, o_kpe, cos4, sin4)
    o, qpr, lse = attn_fwd2(qn, qp, kv, kpr, cos4, sin4)
    ag_up = ag_forward("ag_forward_up", ag_up, o)
    ag_down = ag_start("ag_start_down", [sh["w_down"]], _TOKENS.pending[-1])
    y_b = mm_nn("branch_b", o, w_b_f, F32)
    merged = merge_fwd(z, o_ga, o_gb, y_a, y_b)
    y1 = mm_nn("out_proj", merged, w_out_f, F32)
    x1 = post_res_fwd("post1_fwd", x2, y1, gate1, row(post_norm1_g))
    h2 = norm_mod_fwd("pre2_fwd", x1, row(pre_norm2_g), scale2, shift2)
    (g_up,) = ag_wait("ag_wait_up", ag_up, h2)
    upre = mm_nn_b3("up_proj", h2, g_up, F32)
    ag_down = ag_forward("ag_forward_down", ag_down, upre)
    cw = g_cw.transpose(1, 0, 2).reshape(3, f2)
    cb = row(conv_b)
    f = conv_act_fwd(upre, cw, cb)
    w_down_f = ag_wait("ag_wait_down", ag_down, f)[0].reshape(-1, d)
    ffn = mm_nn("down_proj", f, w_down_f, F32)
    loss_acc, dout, dffn, acc2 = post2_loss_bwd(x1, ffn, tgt, gate2, row(post_norm2_g))
    loss = lax.psum(loss_acc[0, 0], ("x", "y", "c"))
    _TOKENS.push(jnp.broadcast_to(loss, (8, LANES)))

    blocks = lambda g: g.reshape(N_DEV, g.shape[0] // N_DEV, g.shape[1])
    core = lax.axis_index("c").astype(jnp.int32).reshape(1)
    rs = {}

    def rs_begin(key, grads):
        rs[key] = rs_d2d_start("rs_d2d_start_" + key, grads)

    def rs_middle(key, after):
        grads, lands = rs_d2d_wait("rs_d2d_wait_" + key, rs[key], after)
        sums = [pair_sum("pair_sum_%s_%d" % (key, i), g, l, core) for i, (g, l) in enumerate(zip(grads, lands))]
        rs[key] = rs_ici_start("rs_ici_start_" + key, sums)

    gw_down = mm_tn("g_w_down", f, dffn, BF16)
    rs_begin("down", [blocks(gw_down)])
    df = mm_nt("d_f", dffn, w_down_f, F32)
    rs_middle("down", df)
    dupre, gcw_g, gcw_v, gcb_g, gcb_v = conv_act_bwd(upre, cw, cb, df)
    gw_up3 = mm_tn_h3("g_w_up", h2, dupre, N_DEV, BF16)
    rs_begin("up", [gw_up3])
    dh2 = mm_nt_h3("d_h2", dupre, g_up, F32)
    rs_middle("up", dh2)
    dx1, dy1, acc_mid = mid_bwd(dh2, dout, x1, y1, row(pre_norm2_g), scale2, gate1, row(post_norm1_g))
    gw_out = mm_tn("g_w_out", merged, dy1, BF16)
    dmerged = mm_nt("d_merged", dy1, w_out_f, F32)
    dya, dyb, dga, dgb = merge_bwd(z, o_ga, o_gb, y_a, y_b, dmerged)
    gw_a = mm_tn("g_w_a", a, dya, BF16)
    gw_b = mm_tn("g_w_b", o, dyb, BF16)
    rs_begin("mid", [blocks(gw_out), blocks(gw_a), blocks(gw_b)])
    da = mm_nt("d_a", dya, w_a_f, F32)
    do = mm_nt("d_o", dyb, w_b_f, F32)
    rs_middle("mid", do)
    duv, g_ws, g_bs3, acc_gm = gmlp_bwd(z, gmw, da, ln_g, ln_b, wm, bs3)
    dqn, dqp, dkv, dkp = attn_bwd2(qn, qpr, kv, kpr, o, do, lse, cos4, sin4)
    dkpe = kpe_bwd(dkp, cos4, sin4, zp - o_kpe)
    dq_cat = jnp.concatenate([dqn, dqp], axis=1)
    w_uq_cat = jnp.concatenate([w_uq_n, w_uq_r], axis=1)
    gw_uq_cat = mm_tn("g_w_uq", qln, dq_cat, BF16)
    gw_uq_f = jnp.concatenate([gw_uq_cat[:, :HEADS * QK_NOPE].reshape(ql, HEADS, QK_NOPE),
                               gw_uq_cat[:, HEADS * QK_NOPE:].reshape(ql, HEADS, QK_ROPE)], axis=2)
    gw_uq3 = gw_uq_f.reshape(ql, N_DEV, -1).transpose(1, 0, 2)
    gw_ukv3 = mm_tn_o3("g_w_ukv", kvn, dkv, N_DEV, BF16)
    rs_begin("mla", [gw_uq3, gw_ukv3])
    dqln = mm_nt("d_qln", dq_cat, w_uq_cat, F32)
    dq_lat, g_qnorm = rms_bwd_cols("q_norm_bwd", dqln, z, o_q, ql, row(q_norm_g))
    dkvn = mm_nt_b3("d_kvn", dkv, g_ukv, F32)
    rs_middle("mla", dkvn)
    dkv_lat, g_kvnorm = rms_bwd_cols("kv_norm_bwd", dkvn, z, o_kv, kvl, row(kv_norm_g))
    dz = jnp.concatenate([duv, dq_lat, dkv_lat, dga, dgb, dkpe], axis=1)
    gw_in_p = mm_tn("g_w_in", dz, h1, BF16)
    gw_in_f = jnp.concatenate([gw_in_p[:src_kpe], gw_in_p[o_kpe:o_kpe + QK_ROPE], gw_in_p[src_kpe:o_kpe]], axis=0)
    rs_begin("in", [gw_in_f.reshape(N_DEV, -1, d)])
    dh1 = mm_nn("d_h1", dz, w_in_p, F32)
    grad_x, acc1 = pre1_bwd(dh1, dx1, x2, row(pre_norm1_g), scale1)

    dmod = jnp.concatenate([acc1[0], acc1[1], acc_mid[3], acc_mid[0], acc_mid[1], acc2[0]])
    small = [("pre_norm1_g", acc1[2]), ("gm_ln_g", acc_gm[0]), ("gm_ln_b", acc_gm[1]), ("gm_b_s", g_bs3.reshape(-1)),
             ("q_norm_g", g_qnorm[0]), ("kv_norm_g", g_kvnorm[0]), ("post_norm1_g", acc_mid[4]), ("pre_norm2_g", acc_mid[2]),
             ("conv_b", jnp.concatenate([gcb_g[0], gcb_v[0]])), ("post_norm2_g", acc2[1]), ("gm_w_s", g_ws.reshape(-1)),
             ("b_ada", dmod)]
    n_small = sum(v.shape[0] for _, v in small)
    n_cw = 3 * f2
    n_pack = _round_up(n_small + n_cw, PACK_ALIGN)
    tail = jnp.zeros((n_pack - n_small - n_cw,), F32)
    packed = jnp.concatenate([v for _, v in small] + [jnp.concatenate([gcw_g, gcw_v], axis=1).reshape(-1), tail])
    ag_small = ag_start("ag_start_small", [packed.reshape(-1, LANES)], packed)
    rs_middle("in", [packed, _TOKENS.pending[-1]])

    res = {}
    last = packed
    for key, names in (("down", ["w_down"]), ("up", ["w_up"]), ("mid", ["w_out", "w_branch_a", "w_branch_b"]), ("mla", ["w_uq", "w_ukv"])):
        parts = rs_ici_wait("rs_ici_wait_" + key, rs[key], last)
        for k, p in zip(names, parts):
            res[k] = adamw("adamw_" + k, weights[k], mom1[k], mom2[k], p)
            last = res[k][0]

    def pack(src):
        return jnp.concatenate([src[k].reshape(-1) for k, _ in small] + [jnp.zeros((n_pack - n_small,), F32)]).reshape(-1, LANES)

    (gathered,) = ag_wait("ag_wait_small", ag_forward("ag_forward_small", ag_small, last), last)
    sm = [t.reshape(-1) for t in adamw("adamw_small", pack(weights), pack(mom1), pack(mom2), gathered)]
    off = 0
    for k, v in small:
        res[k] = tuple(t[off:off + v.shape[0]].reshape(weights[k].shape) for t in sm)
        off += v.shape[0]

    cs_cw = conv_w.shape[1]
    g_cw_full = sm[0][n_small:n_small + n_cw].reshape(3, f2)
    g_cw_mine = lax.dynamic_slice(g_cw_full, (0, my * cs_cw), (3, cs_cw))
    res["conv_w"] = adamw("adamw_conv_w", conv_w, mom1["conv_w"], mom2["conv_w"], g_cw_mine[None])

    cs_ada = w_ada.shape[1]
    off_b = n_small - N_MOD * d
    dmod_all = gathered.reshape(N_DEV, -1)[:, off_b:off_b + N_MOD * d]
    dmod_cols = lax.dynamic_slice(dmod_all, (0, my * cs_ada), (N_DEV, cs_ada))
    res["w_ada"] = adamw("adamw_w_ada", w_ada, mom1["w_ada"], mom2["w_ada"], ada_bwd_w(sc_all, dmod_cols))

    (p_in,) = rs_ici_wait("rs_ici_wait_in", rs["in"], res["w_ada"][0])
    res["w_in"] = tuple(t.T for t in adamw("adamw_w_in", w_in.T, mom1["w_in"].T, mom2["w_in"].T, p_in))

    _TOKENS.clear()
    outs = [loss, grad_x[None]]
    for i in range(4):
        outs += [res[k][i] for k in order]
    return tuple(outs)
```

```python
import functools

import jax
import jax.numpy as jnp
from jax import lax
from jax.experimental import pallas as pl
from jax.experimental.pallas import tpu as pltpu

F32 = jnp.float32
BF16 = jnp.bfloat16

N_DEV = 8
HEADS = 16
QK_NOPE = 128
QK_ROPE = 64
V_HEAD = 128
CHUNK = 128
ROPE_THETA = 10000.0
EPS = 1e-6
N_MOD = 6
ADAM_LR, ADAM_B1, ADAM_B2, ADAM_EPS, ADAM_WD, ADAM_STEP = 0.001, 0.9, 0.999, 1e-08, 0.01, 10

LANES = 128
VMEM_LIMIT_BYTES = 48 * 2 ** 20
ROW_TILE = 256
COL_TILE = 256
ATT_TILE = 256
Z_PAD = 512
ADAMW_TILE_ELEMS = 1 << 18
PACK_ALIGN = 8 * LANES
MESH = pl.DeviceIdType.MESH


def _params(*sem):
    return pltpu.CompilerParams(dimension_semantics=sem if sem else None, vmem_limit_bytes=VMEM_LIMIT_BYTES)


def _tile(dim, target):
    t = (min(dim, target) // LANES) * LANES
    while t >= LANES:
        if dim % t == 0:
            return t
        t -= LANES
    return dim


def _full(shape):
    nd = len(shape)
    return pl.BlockSpec(shape, lambda *_: (0,) * nd)


class _Tokens:
    KEEP = 2

    def __init__(self):
        self.pending = []

    def push(self, token):
        self.pending = (self.pending + [token])[-self.KEEP:]

    def take(self):
        return list(self.pending)

    def clear(self):
        self.pending = []


_TOKENS = _Tokens()


def _matmul(name, a, b, *, grid, a_spec, b_spec, o_spec, out_shape, contract, acc_shape, split=1):
    nk = grid[2]
    deps = _TOKENS.take()

    def product(a_ref, b_ref):
        if len(b_ref.shape) == 2:
            return lax.dot_general(a_ref[...].astype(BF16), b_ref[...].astype(BF16), (contract, ((), ())), preferred_element_type=F32)
        cs = b_ref.shape[2]
        return sum(lax.dot_general(a_ref[:, s * cs:(s + 1) * cs].astype(BF16), b_ref[s].astype(BF16), (contract, ((), ())),
                                   preferred_element_type=F32) for s in range(split))

    def body_one_step(a_ref, b_ref, *rest):
        o_ref = rest[len(deps)]
        o_ref[...] = product(a_ref, b_ref).astype(o_ref.dtype)

    def body(a_ref, b_ref, *rest):
        o_ref, acc_ref = rest[len(deps):]
        k = pl.program_id(2)

        @pl.when(k == 0)
        def _():
            acc_ref[...] = jnp.zeros_like(acc_ref)

        acc_ref[...] += product(a_ref, b_ref)

        @pl.when(k == nk - 1)
        def _():
            o_ref[...] = acc_ref[...].astype(o_ref.dtype)

    return pl.pallas_call(
        body_one_step if nk == 1 else body, name=name, grid=grid,
        in_specs=[a_spec, b_spec] + [pl.BlockSpec(memory_space=pl.ANY)] * len(deps),
        out_specs=o_spec, out_shape=out_shape, scratch_shapes=[] if nk == 1 else [pltpu.VMEM(acc_shape, F32)],
        compiler_params=_params("parallel", "parallel", "arbitrary"))(a, b, *deps)


TM, TN, TK = 1024, 1024, 2304


def _tk(a, b):
    return TK if a.dtype == BF16 and b.dtype == BF16 else TK // 2


def mm_nn(name, a, b, dtype):
    (m, k), n = a.shape, b.shape[1]
    tm, tn, tk = _tile(m, TM), _tile(n, TN), _tile(k, _tk(a, b))
    return _matmul(name, a, b, grid=(m // tm, n // tn, k // tk),
                   a_spec=pl.BlockSpec((tm, tk), lambda i, j, kk: (i, kk)),
                   b_spec=pl.BlockSpec((tk, tn), lambda i, j, kk: (kk, j)),
                   o_spec=pl.BlockSpec((tm, tn), lambda i, j, kk: (i, j)),
                   out_shape=jax.ShapeDtypeStruct((m, n), dtype), contract=((1,), (0,)), acc_shape=(tm, tn))


def mm_nn_b3(name, a, b3, dtype):
    (m, k), (nj, _, cs) = a.shape, b3.shape
    tm, tk = _tile(m, TM), _tile(k, _tk(a, b3))
    return _matmul(name, a, b3, grid=(m // tm, nj, k // tk),
                   a_spec=pl.BlockSpec((tm, tk), lambda i, j, kk: (i, kk)),
                   b_spec=pl.BlockSpec((None, tk, cs), lambda i, j, kk: (j, kk, 0)),
                   o_spec=pl.BlockSpec((tm, cs), lambda i, j, kk: (i, j)),
                   out_shape=jax.ShapeDtypeStruct((m, nj * cs), dtype), contract=((1,), (0,)), acc_shape=(tm, cs))


def mm_nt(name, a, b, dtype):
    (m, k), n = a.shape, b.shape[0]
    tm, tn, tk = _tile(m, TM), _tile(n, TN), _tile(k, _tk(a, b))
    return _matmul(name, a, b, grid=(m // tm, n // tn, k // tk),
                   a_spec=pl.BlockSpec((tm, tk), lambda i, j, kk: (i, kk)),
                   b_spec=pl.BlockSpec((tn, tk), lambda i, j, kk: (j, kk)),
                   o_spec=pl.BlockSpec((tm, tn), lambda i, j, kk: (i, j)),
                   out_shape=jax.ShapeDtypeStruct((m, n), dtype), contract=((1,), (1,)), acc_shape=(tm, tn))


def mm_nt_b3(name, a, b3, dtype):
    m, (nj, n, cs) = a.shape[0], b3.shape
    tm, tn = _tile(m, TM), _tile(n, TN)
    return _matmul(name, a, b3, grid=(m // tm, n // tn, nj),
                   a_spec=pl.BlockSpec((tm, cs), lambda i, j, kk: (i, kk)),
                   b_spec=pl.BlockSpec((None, tn, cs), lambda i, j, kk: (kk, j, 0)),
                   o_spec=pl.BlockSpec((tm, tn), lambda i, j, kk: (i, j)),
                   out_shape=jax.ShapeDtypeStruct((m, n), dtype), contract=((1,), (1,)), acc_shape=(tm, tn))


def mm_nt_h3(name, a3, b3, dtype):
    (_, m, _), (nj, n, cs) = a3.shape, b3.shape
    tm, tn, hj = _tile(m, TM), _tile(n, TN), nj // 2
    pair = 2 if hj % 2 == 0 else 1
    return _matmul(name, a3, b3.reshape(nj // pair, pair, n, cs), grid=(m // tm, n // tn, nj // pair),
                   a_spec=pl.BlockSpec((None, tm, pair * cs), lambda i, j, kk: (kk // (hj // pair), i, kk % (hj // pair))),
                   b_spec=pl.BlockSpec((None, pair, tn, cs), lambda i, j, kk: (kk, 0, j, 0)),
                   o_spec=pl.BlockSpec((tm, tn), lambda i, j, kk: (i, j)),
                   out_shape=jax.ShapeDtypeStruct((m, n), dtype), contract=((1,), (1,)), acc_shape=(tm, tn), split=pair)


def mm_tn_h3(name, a, b3, nj, dtype):
    (k, m), half = a.shape, b3.shape[2]
    hj = nj // 2
    cs = half // hj
    tm, tk = _tile(m, TM), _tile(k, _tk(a, b3))
    return _matmul(name, a, b3, grid=(m // tm, nj, k // tk),
                   a_spec=pl.BlockSpec((tk, tm), lambda i, j, kk: (kk, i)),
                   b_spec=pl.BlockSpec((None, tk, cs), lambda i, j, kk: (j // hj, kk, j % hj)),
                   o_spec=pl.BlockSpec((None, tm, cs), lambda i, j, kk: (j, i, 0)),
                   out_shape=jax.ShapeDtypeStruct((nj, m, cs), dtype), contract=((0,), (0,)), acc_shape=(tm, cs))


def mm_tn(name, a, b, dtype):
    (k, m), n = a.shape, b.shape[1]
    tm, tn, tk = _tile(m, TM), _tile(n, TN), _tile(k, _tk(a, b))
    return _matmul(name, a, b, grid=(m // tm, n // tn, k // tk),
                   a_spec=pl.BlockSpec((tk, tm), lambda i, j, kk: (kk, i)),
                   b_spec=pl.BlockSpec((tk, tn), lambda i, j, kk: (kk, j)),
                   o_spec=pl.BlockSpec((tm, tn), lambda i, j, kk: (i, j)),
                   out_shape=jax.ShapeDtypeStruct((m, n), dtype), contract=((0,), (0,)), acc_shape=(tm, tn))


def mm_tn_o3(name, a, b, nj, dtype):
    (k, m), n = a.shape, b.shape[1]
    cs = n // nj
    tm, tk = _tile(m, TM), _tile(k, _tk(a, b))
    return _matmul(name, a, b, grid=(m // tm, nj, k // tk),
                   a_spec=pl.BlockSpec((tk, tm), lambda i, j, kk: (kk, i)),
                   b_spec=pl.BlockSpec((tk, cs), lambda i, j, kk: (kk, j)),
                   o_spec=pl.BlockSpec((None, tm, cs), lambda i, j, kk: (j, i, 0)),
                   out_shape=jax.ShapeDtypeStruct((nj, m, cs), dtype), contract=((0,), (0,)), acc_shape=(tm, cs))


_GELU_C = 0.7978845608028654
_GELU_A = 0.044715


def _gelu(x):
    return 0.5 * x * (1.0 + jnp.tanh(_GELU_C * (x + _GELU_A * x * x * x)))


def _gelu_and_grad(x):
    t = jnp.tanh(_GELU_C * (x + _GELU_A * x * x * x))
    y = 0.5 * x * (1.0 + t)
    dy = 0.5 * (1.0 + t) + 0.5 * x * (1.0 - t * t) * (_GELU_C * (1.0 + 3.0 * _GELU_A * x * x))
    return y, dy


def _sigmoid(x):
    return 1.0 / (1.0 + jnp.exp(-x))


def _rms_stats(x):
    inv = lax.rsqrt(jnp.mean(x * x, axis=-1, keepdims=True) + EPS)
    return inv, x * inv


def _rms_bwd(dyhat, yhat, inv):
    return inv * (dyhat - yhat * jnp.mean(dyhat * yhat, axis=-1, keepdims=True))


def _colsum(x):
    return jnp.sum(x, axis=0, keepdims=True)


def _rope(x, cos4, sin4):
    lane = lax.broadcasted_iota(jnp.int32, x.shape, x.ndim - 1)
    first_half = (lane % QK_ROPE) < (QK_ROPE // 2)
    partner = jnp.where(first_half, pltpu.roll(x, LANES - QK_ROPE // 2, x.ndim - 1), pltpu.roll(x, QK_ROPE // 2, x.ndim - 1))
    return x * cos4 + partner * sin4


def norm_mod_fwd(name, x, g, scale, shift):
    s, d = x.shape
    tr = _tile(s, ROW_TILE)

    def body(x_ref, g_ref, sc_ref, sh_ref, o_ref):
        _, xh = _rms_stats(x_ref[...])
        o_ref[...] = (xh * g_ref[...] * (1.0 + sc_ref[...]) + sh_ref[...]).astype(o_ref.dtype)

    row = pl.BlockSpec((tr, d), lambda i: (i, 0))
    vec = pl.BlockSpec((1, d), lambda i: (0, 0))
    return pl.pallas_call(body, name=name, grid=(s // tr,), in_specs=[row, vec, vec, vec], out_specs=row,
                          out_shape=jax.ShapeDtypeStruct((s, d), BF16), compiler_params=_params("parallel"))(x, g, scale, shift)


def rms_fwd_cols(name, z, off, width, g):
    s = z.shape[0]
    tr = _tile(s, ROW_TILE)
    assert off % width == 0

    def body(x_ref, g_ref, o_ref):
        _, xh = _rms_stats(x_ref[...])
        o_ref[...] = (xh * g_ref[...]).astype(o_ref.dtype)

    return pl.pallas_call(body, name=name, grid=(s // tr,),
                          in_specs=[pl.BlockSpec((tr, width), lambda i: (i, off // width)), pl.BlockSpec((1, width), lambda i: (0, 0))],
                          out_specs=pl.BlockSpec((tr, width), lambda i: (i, 0)),
                          out_shape=jax.ShapeDtypeStruct((s, width), BF16), compiler_params=_params("parallel"))(z, g)


def rms_bwd_cols(name, dy, z, off, width, g):
    s = z.shape[0]
    tr = _tile(s, ROW_TILE)

    def body(dy_ref, x_ref, g_ref, dx_ref, gg_ref):
        @pl.when(pl.program_id(0) == 0)
        def _():
            gg_ref[...] = jnp.zeros_like(gg_ref)

        inv, xh = _rms_stats(x_ref[...])
        dy_v = dy_ref[...]
        gg_ref[...] += _colsum(dy_v * xh)
        dx_ref[...] = _rms_bwd(dy_v * g_ref[...], xh, inv).astype(dx_ref.dtype)

    return pl.pallas_call(body, name=name, grid=(s // tr,),
                          in_specs=[pl.BlockSpec((tr, width), lambda i: (i, 0)), pl.BlockSpec((tr, width), lambda i: (i, off // width)),
                                    pl.BlockSpec((1, width), lambda i: (0, 0))],
                          out_specs=[pl.BlockSpec((tr, width), lambda i: (i, 0)), pl.BlockSpec((1, width), lambda i: (0, 0))],
                          out_shape=[jax.ShapeDtypeStruct((s, width), BF16), jax.ShapeDtypeStruct((1, width), F32)],
                          compiler_params=_params("arbitrary"))(dy, z, g)


def post_res_fwd(name, x, y, gate, g):
    s, d = x.shape
    tr = _tile(s, ROW_TILE)

    def body(x_ref, y_ref, gate_ref, g_ref, o_ref):
        _, yh = _rms_stats(y_ref[...])
        o_ref[...] = x_ref[...] + gate_ref[...] * (yh * g_ref[...])

    row = pl.BlockSpec((tr, d), lambda i: (i, 0))
    vec = pl.BlockSpec((1, d), lambda i: (0, 0))
    return pl.pallas_call(body, name=name, grid=(s // tr,), in_specs=[row, row, vec, vec], out_specs=row,
                          out_shape=jax.ShapeDtypeStruct((s, d), F32), compiler_params=_params("parallel"))(x, y, gate, g)


def post2_loss_bwd(x1, ffn, target, gate2, g):
    s, d = x1.shape
    tr = _tile(s, ROW_TILE)

    def body(x_ref, y_ref, t_ref, gate_ref, g_ref, loss_ref, dout_ref, dy_ref, acc_ref):
        @pl.when(pl.program_id(0) == 0)
        def _():
            loss_ref[...] = jnp.zeros_like(loss_ref)
            acc_ref[...] = jnp.zeros_like(acc_ref)

        inv, yh = _rms_stats(y_ref[...])
        r = yh * g_ref[...]
        err = x_ref[...] + gate_ref[...] * r - t_ref[...]
        loss_ref[...] += 0.5 * jnp.sum(jnp.mean(err * err, axis=-1, keepdims=True))
        dout = err / d
        dout_ref[...] = dout
        dr = dout * gate_ref[...]
        acc_ref[0:1, :] += _colsum(dout * r)
        acc_ref[1:2, :] += _colsum(dr * yh)
        dy_ref[...] = _rms_bwd(dr * g_ref[...], yh, inv).astype(dy_ref.dtype)

    row = pl.BlockSpec((tr, d), lambda i: (i, 0))
    vec = pl.BlockSpec((1, d), lambda i: (0, 0))
    return pl.pallas_call(
        body, name="post2_loss_bwd", grid=(s // tr,), in_specs=[row, row, row, vec, vec],
        out_specs=[_full((8, LANES)), row, row, _full((8, d))],
        out_shape=[jax.ShapeDtypeStruct((8, LANES), F32), jax.ShapeDtypeStruct((s, d), F32),
                   jax.ShapeDtypeStruct((s, d), BF16), jax.ShapeDtypeStruct((8, d), F32)],
        compiler_params=_params("arbitrary"))(x1, ffn, target, gate2, g)


def mid_bwd(dh2, dout, x1, y1, pre2_g, scale2, gate1, post1_g):
    s, d = x1.shape
    tr = _tile(s, ROW_TILE)

    def body(dh_ref, dout_ref, x_ref, y_ref, g2_ref, sc_ref, gate_ref, g1_ref, dx_ref, dy_ref, acc_ref):
        @pl.when(pl.program_id(0) == 0)
        def _():
            acc_ref[...] = jnp.zeros_like(acc_ref)

        dh = dh_ref[...]
        inv2, xh = _rms_stats(x_ref[...])
        acc_ref[0:1, :] += _colsum(dh)
        acc_ref[1:2, :] += _colsum(dh * (xh * g2_ref[...]))
        t = dh * (1.0 + sc_ref[...])
        acc_ref[2:3, :] += _colsum(t * xh)
        dx1 = dout_ref[...] + _rms_bwd(t * g2_ref[...], xh, inv2)
        dx_ref[...] = dx1
        inv1, yh = _rms_stats(y_ref[...])
        acc_ref[3:4, :] += _colsum(dx1 * (yh * g1_ref[...]))
        dr = dx1 * gate_ref[...]
        acc_ref[4:5, :] += _colsum(dr * yh)
        dy_ref[...] = _rms_bwd(dr * g1_ref[...], yh, inv1).astype(dy_ref.dtype)

    row = pl.BlockSpec((tr, d), lambda i: (i, 0))
    vec = pl.BlockSpec((1, d), lambda i: (0, 0))
    return pl.pallas_call(
        body, name="mid_bwd", grid=(s // tr,), in_specs=[row, row, row, row, vec, vec, vec, vec],
        out_specs=[row, row, _full((8, d))],
        out_shape=[jax.ShapeDtypeStruct((s, d), F32), jax.ShapeDtypeStruct((s, d), BF16), jax.ShapeDtypeStruct((8, d), F32)],
        compiler_params=_params("arbitrary"))(dh2, dout, x1, y1, pre2_g, scale2, gate1, post1_g)


def pre1_bwd(dh1, dx1, x, pre1_g, scale1):
    s, d = x.shape
    tr = _tile(s, ROW_TILE)

    def body(dh_ref, dx1_ref, x_ref, g_ref, sc_ref, dx_ref, acc_ref):
        @pl.when(pl.program_id(0) == 0)
        def _():
            acc_ref[...] = jnp.zeros_like(acc_ref)

        dh = dh_ref[...]
        inv, xh = _rms_stats(x_ref[...])
        acc_ref[0:1, :] += _colsum(dh)
        acc_ref[1:2, :] += _colsum(dh * (xh * g_ref[...]))
        t = dh * (1.0 + sc_ref[...])
        acc_ref[2:3, :] += _colsum(t * xh)
        dx_ref[...] = dx1_ref[...] + _rms_bwd(t * g_ref[...], xh, inv)

    row = pl.BlockSpec((tr, d), lambda i: (i, 0))
    vec = pl.BlockSpec((1, d), lambda i: (0, 0))
    return pl.pallas_call(
        body, name="pre1_bwd", grid=(s // tr,), in_specs=[row, row, row, vec, vec], out_specs=[row, _full((8, d))],
        out_shape=[jax.ShapeDtypeStruct((s, d), F32), jax.ShapeDtypeStruct((8, d), F32)],
        compiler_params=_params("arbitrary"))(dh1, dx1, x, pre1_g, scale1)


def _ln_stats(v):
    mu = jnp.mean(v, axis=-1, keepdims=True)
    vc = v - mu
    rstd = lax.rsqrt(jnp.mean(vc * vc, axis=-1, keepdims=True) + EPS)
    return rstd, vc * rstd


def gmlp_fwd(z, width, ln_g, ln_b, wm, bs3):
    s = z.shape[0]
    groups = width // CHUNK

    def body(u_ref, v_ref, g_ref, b_ref, wm_ref, bs_ref, a_ref):
        ug = _gelu(u_ref[...])
        _, vh = _ln_stats(_gelu(v_ref[...]))
        vn = (vh * g_ref[...] + b_ref[...]).astype(BF16)
        for g in range(groups):
            cols = slice(g * CHUNK, (g + 1) * CHUNK)
            mixed = jnp.dot(wm_ref[g], vn[:, cols], preferred_element_type=F32) + bs_ref[g]
            a_ref[:, cols] = (ug[:, cols] * mixed).astype(a_ref.dtype)

    vec = pl.BlockSpec((1, width), lambda n: (0, 0))
    return pl.pallas_call(
        body, name="gmlp_fwd", grid=(s // CHUNK,),
        in_specs=[pl.BlockSpec((CHUNK, width), lambda n: (n, 0)), pl.BlockSpec((CHUNK, width), lambda n: (n, 1)), vec, vec,
                  _full(wm.shape), _full(bs3.shape)],
        out_specs=pl.BlockSpec((CHUNK, width), lambda n: (n, 0)),
        out_shape=jax.ShapeDtypeStruct((s, width), BF16), compiler_params=_params("parallel"))(z, z, ln_g, ln_b, wm, bs3)


def gmlp_bwd(z, width, da, ln_g, ln_b, wm, bs3):
    s = z.shape[0]
    groups = width // CHUNK

    def body(u_ref, v_ref, da_ref, g_ref, b_ref, wm_ref, bs_ref, duv_ref, gw_ref, gb_ref, acc_ref, dvn_ref):
        @pl.when(pl.program_id(0) == 0)
        def _():
            gw_ref[...] = jnp.zeros_like(gw_ref)
            gb_ref[...] = jnp.zeros_like(gb_ref)
            acc_ref[...] = jnp.zeros_like(acc_ref)

        ug, dug = _gelu_and_grad(u_ref[...])
        vg, dvg = _gelu_and_grad(v_ref[...])
        rstd, vh = _ln_stats(vg)
        vn = (vh * g_ref[...] + b_ref[...]).astype(BF16)
        da_v = da_ref[...]
        for g in range(groups):
            cols = slice(g * CHUNK, (g + 1) * CHUNK)
            mixed = jnp.dot(wm_ref[g], vn[:, cols], preferred_element_type=F32) + bs_ref[g]
            duv_ref[:, cols] = (da_v[:, cols] * mixed * dug[:, cols]).astype(duv_ref.dtype)
            dm = da_v[:, cols] * ug[:, cols]
            gb_ref[g] += jnp.sum(dm, axis=-1, keepdims=True)
            dmb = dm.astype(BF16)
            gw_ref[g] += lax.dot_general(dmb, vn[:, cols], (((1,), (1,)), ((), ())), preferred_element_type=F32)
            dvn_ref[:, cols] = lax.dot_general(wm_ref[g], dmb, (((0,), (0,)), ((), ())), preferred_element_type=F32)
        dvn = dvn_ref[...]
        acc_ref[0:1, :] += _colsum(dvn * vh)
        acc_ref[1:2, :] += _colsum(dvn)
        dvh = dvn * g_ref[...]
        dv = rstd * (dvh - jnp.mean(dvh, axis=-1, keepdims=True) - vh * jnp.mean(dvh * vh, axis=-1, keepdims=True))
        duv_ref[:, width:] = (dv * dvg).astype(duv_ref.dtype)

        @pl.when(pl.program_id(0) == pl.num_programs(0) - 1)
        def _():
            q = lax.broadcasted_iota(jnp.int32, gw_ref.shape, 1)
            p = lax.broadcasted_iota(jnp.int32, gw_ref.shape, 2)
            gw_ref[...] = jnp.where(p <= q, gw_ref[...], 0.0)

    vec = pl.BlockSpec((1, width), lambda n: (0, 0))
    blk = pl.BlockSpec((CHUNK, width), lambda n: (n, 0))
    return pl.pallas_call(
        body, name="gmlp_bwd", grid=(s // CHUNK,),
        in_specs=[blk, pl.BlockSpec((CHUNK, width), lambda n: (n, 1)), blk, vec, vec, _full(wm.shape), _full(bs3.shape)],
        out_specs=[pl.BlockSpec((CHUNK, 2 * width), lambda n: (n, 0)), _full(wm.shape), _full(bs3.shape), _full((8, width))],
        out_shape=[jax.ShapeDtypeStruct((s, 2 * width), BF16), jax.ShapeDtypeStruct(wm.shape, F32),
                   jax.ShapeDtypeStruct(bs3.shape, F32), jax.ShapeDtypeStruct((8, width), F32)],
        scratch_shapes=[pltpu.VMEM((CHUNK, width), F32)],
        compiler_params=_params("arbitrary"))(z, z, da, ln_g, ln_b, wm, bs3)


def merge_fwd(z, off_a, off_b, ya, yb):
    s, d = ya.shape
    tr, tc = _tile(s, ROW_TILE * 2), _tile(d, COL_TILE)
    assert off_a % tc == 0 and off_b % tc == 0

    def body(ga_ref, gb_ref, ya_ref, yb_ref, o_ref):
        o_ref[...] = (_sigmoid(ga_ref[...]) * ya_ref[...] + _sigmoid(gb_ref[...]) * yb_ref[...]).astype(o_ref.dtype)

    blk = pl.BlockSpec((tr, tc), lambda i, j: (i, j))
    return pl.pallas_call(
        body, name="merge_fwd", grid=(s // tr, d // tc),
        in_specs=[pl.BlockSpec((tr, tc), lambda i, j: (i, off_a // tc + j)), pl.BlockSpec((tr, tc), lambda i, j: (i, off_b // tc + j)), blk, blk],
        out_specs=blk, out_shape=jax.ShapeDtypeStruct((s, d), BF16), compiler_params=_params("parallel", "parallel"))(z, z, ya, yb)


def merge_bwd(z, off_a, off_b, ya, yb, dm):
    s, d = ya.shape
    tr, tc = _tile(s, ROW_TILE * 2), _tile(d, COL_TILE)
    nc = d // tc

    def body(ga_ref, gb_ref, ya_ref, yb_ref, dm_ref, dya_ref, dyb_ref, dga_ref, dgb_ref):
        dm_v = dm_ref[...]
        sa, sb = _sigmoid(ga_ref[...]), _sigmoid(gb_ref[...])
        dya_ref[...] = (dm_v * sa).astype(dya_ref.dtype)
        dyb_ref[...] = (dm_v * sb).astype(dyb_ref.dtype)
        dga_ref[...] = (dm_v * ya_ref[...] * sa * (1.0 - sa)).astype(dga_ref.dtype)
        dgb_ref[...] = (dm_v * yb_ref[...] * sb * (1.0 - sb)).astype(dgb_ref.dtype)

    blk = pl.BlockSpec((tr, tc), lambda i, j: (i, j))
    out = jax.ShapeDtypeStruct((s, d), BF16)
    return pl.pallas_call(
        body, name="merge_bwd", grid=(s // tr, nc),
        in_specs=[pl.BlockSpec((tr, tc), lambda i, j: (i, off_a // tc + j)), pl.BlockSpec((tr, tc), lambda i, j: (i, off_b // tc + j)), blk, blk, blk],
        out_specs=[blk, blk, blk, blk], out_shape=[out, out, out, out],
        compiler_params=_params("parallel", "parallel"))(z, z, ya, yb, dm)


_ATT_SCALE = (QK_NOPE + QK_ROPE) ** -0.5
_NEG = -1e30


def rope_k(z, off, cos4, sin4):
    s = z.shape[0]
    tr = _tile(s, ROW_TILE * 2)
    assert off % LANES == 0

    def body(k_ref, c_ref, s_ref, o_ref):
        k = k_ref[...]
        k = k + pltpu.roll(k, QK_ROPE, 1)
        o_ref[...] = _rope(k, c_ref[...], s_ref[...]).astype(o_ref.dtype)

    row = pl.BlockSpec((tr, LANES), lambda i: (i, 0))
    return pl.pallas_call(body, name="rope_k", grid=(s // tr,),
                          in_specs=[pl.BlockSpec((tr, LANES), lambda i: (i, off // LANES)), row, row], out_specs=row,
                          out_shape=jax.ShapeDtypeStruct((s, LANES), BF16), compiler_params=_params("parallel"))(z, cos4, sin4)


def _head_masks(shape):
    lane = lax.broadcasted_iota(jnp.int32, shape, 1)
    return lane < QK_ROPE, lane >= QK_ROPE


def _scores(qn, qp_h, k, kp, qi, kb, t):
    sc = lax.dot_general(qn, k, (((1,), (1,)), ((), ())), preferred_element_type=F32)
    sc += lax.dot_general(qp_h, kp, (((1,), (1,)), ((), ())), preferred_element_type=F32)
    sc = sc * _ATT_SCALE
    row = lax.broadcasted_iota(jnp.int32, sc.shape, 0) + qi * t
    col = lax.broadcasted_iota(jnp.int32, sc.shape, 1) + kb * t
    return jnp.where(col <= row, sc, _NEG)


def attn_fwd(qn, qp, kv, kpr, cos4, sin4):
    s = qn.shape[0]
    hp = HEADS // 2
    t = _tile(s, ATT_TILE)
    nq = s // t

    def body(qn_ref, qp_ref, kv_ref, kp_ref, c_ref, s_ref, o_ref, qpr_ref, l_ref):
        qi = pl.program_id(1)
        qpr = _rope(qp_ref[...], c_ref[...], s_ref[...]).astype(BF16)
        qpr_ref[...] = qpr
        masks = _head_masks(qpr.shape)
        for hh in range(2):
            q_n = qn_ref[:, hh * QK_NOPE:(hh + 1) * QK_NOPE]
            q_p = jnp.where(masks[hh], qpr, jnp.zeros_like(qpr))
            kc, vc = 2 * hh * QK_NOPE, (2 * hh + 1) * QK_NOPE

            def step(kb, carry):
                m, l, acc = carry
                rows = pl.ds(pl.multiple_of(kb * t, t), t)
                sc = _scores(q_n, q_p, kv_ref[rows, kc:kc + QK_NOPE], kp_ref[rows, :], qi, kb, t)
                m_new = jnp.maximum(m, jnp.max(sc, axis=-1, keepdims=True))
                alpha = jnp.exp(m - m_new)
                p = jnp.exp(sc - m_new)
                l = alpha * l + jnp.sum(p, axis=-1, keepdims=True)
                acc = alpha * acc + jnp.dot(p.astype(BF16), kv_ref[rows, vc:vc + V_HEAD], preferred_element_type=F32)
                return m_new, l, acc

            init = (jnp.full((t, 1), _NEG, F32), jnp.zeros((t, 1), F32), jnp.zeros((t, V_HEAD), F32))
            m, l, acc = lax.fori_loop(0, qi + 1, step, init)
            o_ref[:, hh * V_HEAD:(hh + 1) * V_HEAD] = acc / l
            l_ref[:, hh:hh + 1] = m + jnp.log(l)

    return pl.pallas_call(
        body, name="attn_fwd", grid=(hp, nq),
        in_specs=[pl.BlockSpec((t, 2 * QK_NOPE), lambda h, i: (i, h)), pl.BlockSpec((t, LANES), lambda h, i: (i, h)),
                  pl.BlockSpec((s, 4 * QK_NOPE), lambda h, i: (0, h)), _full((s, LANES)),
                  pl.BlockSpec((t, LANES), lambda h, i: (i, 0)), pl.BlockSpec((t, LANES), lambda h, i: (i, 0))],
        out_specs=[pl.BlockSpec((t, 2 * V_HEAD), lambda h, i: (i, h)), pl.BlockSpec((t, LANES), lambda h, i: (i, h)),
                   pl.BlockSpec((None, t, 2), lambda h, i: (h, i, 0))],
        out_shape=[jax.ShapeDtypeStruct((s, HEADS * V_HEAD), F32), jax.ShapeDtypeStruct((s, HEADS * QK_ROPE), BF16),
                   jax.ShapeDtypeStruct((hp, s, 2), F32)],
        compiler_params=_params("parallel", "parallel"))(qn, qp, kv, kpr, cos4, sin4)


def attn_bwd_q(qn, qpr, kv, kpr, o, do, lse, cos4, sin4):
    s = qn.shape[0]
    hp = HEADS // 2
    t = _tile(s, ATT_TILE)
    nq = s // t

    def body(qn_ref, qpr_ref, kv_ref, kp_ref, o_ref, do_ref, l_ref, c_ref, s_ref, dqn_ref, dqp_ref):
        qi = pl.program_id(1)
        qpr = qpr_ref[...]
        masks = _head_masks(qpr.shape)
        dqp = jnp.zeros(qpr.shape, F32)
        for hh in range(2):
            q_n = qn_ref[:, hh * QK_NOPE:(hh + 1) * QK_NOPE]
            q_p = jnp.where(masks[hh], qpr, jnp.zeros_like(qpr))
            kc, vc = 2 * hh * QK_NOPE, (2 * hh + 1) * QK_NOPE
            do_h = do_ref[:, hh * V_HEAD:(hh + 1) * V_HEAD]
            delta = jnp.sum(do_h * o_ref[:, hh * V_HEAD:(hh + 1) * V_HEAD], axis=-1, keepdims=True)
            do_b = do_h.astype(BF16)
            lse_h = l_ref[:, hh:hh + 1]

            def step(kb, carry):
                dn, dp_ = carry
                rows = pl.ds(pl.multiple_of(kb * t, t), t)
                k = kv_ref[rows, kc:kc + QK_NOPE]
                kp = kp_ref[rows, :]
                p = jnp.exp(_scores(q_n, q_p, k, kp, qi, kb, t) - lse_h)
                dpv = lax.dot_general(do_b, kv_ref[rows, vc:vc + V_HEAD], (((1,), (1,)), ((), ())), preferred_element_type=F32)
                ds = (p * (dpv - delta) * _ATT_SCALE).astype(BF16)
                dn = dn + jnp.dot(ds, k, preferred_element_type=F32)
                dp_ = dp_ + jnp.dot(ds, kp, preferred_element_type=F32)
                return dn, dp_

            dn, dp_h = lax.fori_loop(0, qi + 1, step, (jnp.zeros((t, QK_NOPE), F32), jnp.zeros((t, LANES), F32)))
            dqn_ref[:, hh * QK_NOPE:(hh + 1) * QK_NOPE] = dn.astype(dqn_ref.dtype)
            dqp = dqp + jnp.where(masks[hh], dp_h, jnp.zeros_like(dp_h))
        dqp_ref[...] = _rope(dqp, c_ref[...], -s_ref[...]).astype(dqp_ref.dtype)

    qblk = pl.BlockSpec((t, 2 * QK_NOPE), lambda h, i: (i, h))
    pblk = pl.BlockSpec((t, LANES), lambda h, i: (i, h))
    tab = pl.BlockSpec((t, LANES), lambda h, i: (i, 0))
    return pl.pallas_call(
        body, name="attn_bwd_q", grid=(hp, nq),
        in_specs=[qblk, pblk, pl.BlockSpec((s, 4 * QK_NOPE), lambda h, i: (0, h)), _full((s, LANES)), qblk, qblk,
                  pl.BlockSpec((None, t, 2), lambda h, i: (h, i, 0)), tab, tab],
        out_specs=[qblk, pblk],
        out_shape=[jax.ShapeDtypeStruct((s, HEADS * QK_NOPE), BF16), jax.ShapeDtypeStruct((s, HEADS * QK_ROPE), BF16)],
        compiler_params=_params("parallel", "parallel"))(qn, qpr, kv, kpr, o, do, lse, cos4, sin4)


def attn_bwd_kv(qn, qpr, kv, kpr, o, do, lse):
    s = qn.shape[0]
    hp = HEADS // 2
    t = _tile(s, ATT_TILE)
    nq = s // t

    def body(qn_ref, qpr_ref, kv_ref, kp_ref, o_ref, do_ref, l_ref, dkv_ref, dkp_ref):
        ki = pl.program_id(1)
        rows_k = pl.ds(pl.multiple_of(ki * t, t), t)
        kp = kp_ref[rows_k, :]
        dkp = jnp.zeros((t, LANES), F32)
        for hh in range(2):
            kc, vc = 2 * hh * QK_NOPE, (2 * hh + 1) * QK_NOPE
            k = kv_ref[rows_k, kc:kc + QK_NOPE]
            v = kv_ref[rows_k, vc:vc + V_HEAD]

            def step(qb, carry):
                dk, dv, dkp_h = carry
                rows = pl.ds(pl.multiple_of(qb * t, t), t)
                q_n = qn_ref[rows, hh * QK_NOPE:(hh + 1) * QK_NOPE]
                qpr = qpr_ref[rows, :]
                lane = lax.broadcasted_iota(jnp.int32, qpr.shape, 1)
                sel = (lane < QK_ROPE) if hh == 0 else (lane >= QK_ROPE)
                q_p = jnp.where(sel, qpr, jnp.zeros_like(qpr))
                do_h = do_ref[rows, hh * V_HEAD:(hh + 1) * V_HEAD]
                delta = jnp.sum(do_h * o_ref[rows, hh * V_HEAD:(hh + 1) * V_HEAD], axis=-1, keepdims=True)
                do_b = do_h.astype(BF16)
                p = jnp.exp(_scores(q_n, q_p, k, kp, qb, ki, t) - l_ref[rows, hh:hh + 1])
                dpv = lax.dot_general(do_b, v, (((1,), (1,)), ((), ())), preferred_element_type=F32)
                ds = (p * (dpv - delta) * _ATT_SCALE).astype(BF16)
                dv = dv + lax.dot_general(p.astype(BF16), do_b, (((0,), (0,)), ((), ())), preferred_element_type=F32)
                dk = dk + lax.dot_general(ds, q_n, (((0,), (0,)), ((), ())), preferred_element_type=F32)
                dkp_h = dkp_h + lax.dot_general(ds, q_p, (((0,), (0,)), ((), ())), preferred_element_type=F32)
                return dk, dv, dkp_h

            init = (jnp.zeros((t, QK_NOPE), F32), jnp.zeros((t, V_HEAD), F32), jnp.zeros((t, LANES), F32))
            dk, dv, dkp_h = lax.fori_loop(ki, nq, step, init)
            dkv_ref[:, kc:kc + QK_NOPE] = dk.astype(dkv_ref.dtype)
            dkv_ref[:, vc:vc + V_HEAD] = dv.astype(dkv_ref.dtype)
            dkp = dkp + dkp_h
        dkp_ref[...] = dkp

    return pl.pallas_call(
        body, name="attn_bwd_kv", grid=(hp, nq),
        in_specs=[pl.BlockSpec((s, 2 * QK_NOPE), lambda h, i: (0, h)), pl.BlockSpec((s, LANES), lambda h, i: (0, h)),
                  pl.BlockSpec((s, 4 * QK_NOPE), lambda h, i: (0, h)), _full((s, LANES)),
                  pl.BlockSpec((s, 2 * V_HEAD), lambda h, i: (0, h)), pl.BlockSpec((s, 2 * V_HEAD), lambda h, i: (0, h)),
                  pl.BlockSpec((None, s, 2), lambda h, i: (h, 0, 0))],
        out_specs=[pl.BlockSpec((t, 4 * QK_NOPE), lambda h, i: (i, h)), pl.BlockSpec((None, t, LANES), lambda h, i: (h, i, 0))],
        out_shape=[jax.ShapeDtypeStruct((s, HEADS * 2 * QK_NOPE), BF16), jax.ShapeDtypeStruct((hp, s, LANES), F32)],
        compiler_params=_params("parallel", "parallel"))(qn, qpr, kv, kpr, o, do, lse)


def _dot_nt(a, b):
    return lax.dot_general(a, b, (((1,), (1,)), ((), ())), preferred_element_type=F32)


def _dot_tn(a, b):
    return lax.dot_general(a, b, (((0,), (0,)), ((), ())), preferred_element_type=F32)


def _q_cat(q_n, qpr, hh):
    lane = lax.broadcasted_iota(jnp.int32, qpr.shape, 1)
    sel = (lane < QK_ROPE) if hh == 0 else (lane >= QK_ROPE)
    return jnp.concatenate([q_n, jnp.where(sel, qpr, jnp.zeros_like(qpr))], axis=1)


def _causal(sc):
    row = lax.broadcasted_iota(jnp.int32, sc.shape, 0)
    col = lax.broadcasted_iota(jnp.int32, sc.shape, 1)
    return jnp.where(col <= row, sc, _NEG)


def attn_fwd2(qn, qp, kv, kpr, cos4, sin4):
    s = qn.shape[0]
    hp = HEADS // 2
    t = _tile(s, ATT_TILE)
    nq = s // t

    def body(qn_ref, qp_ref, kv_ref, kp_ref, c_ref, s_ref, o_ref, qpr_ref, l_ref, kcat_ref):
        qi = pl.program_id(1)

        @pl.when(qi == 0)
        def _():
            for hh in range(2):
                kcat_ref[hh, :, 0:QK_NOPE] = kv_ref[:, 2 * hh * QK_NOPE:(2 * hh + 1) * QK_NOPE]
                kcat_ref[hh, :, QK_NOPE:] = kp_ref[...]

        qpr = _rope(qp_ref[...], c_ref[...], s_ref[...]).astype(BF16)
        qpr_ref[...] = qpr
        qcat = [_q_cat(qn_ref[:, hh * QK_NOPE:(hh + 1) * QK_NOPE], qpr, hh) for hh in range(2)]

        def block(kb, carry, diagonal):
            rows = pl.ds(pl.multiple_of(kb * t, t), t)
            out = []
            for hh in range(2):
                m, l, acc = carry[hh]
                sc = _dot_nt(qcat[hh], kcat_ref[hh, rows, :]) * _ATT_SCALE
                if diagonal:
                    sc = _causal(sc)
                m_new = jnp.maximum(m, jnp.max(sc, axis=-1, keepdims=True))
                alpha = jnp.exp(m - m_new)
                p = jnp.exp(sc - m_new)
                l = alpha * l + jnp.sum(p, axis=-1, keepdims=True)
                v = kv_ref[rows, (2 * hh + 1) * QK_NOPE:(2 * hh + 2) * QK_NOPE]
                acc = alpha * acc + jnp.dot(p.astype(BF16), v, preferred_element_type=F32)
                out.append((m_new, l, acc))
            return tuple(out)

        one = (jnp.full((t, 1), _NEG, F32), jnp.zeros((t, 1), F32), jnp.zeros((t, V_HEAD), F32))
        carry = lax.fori_loop(0, qi, lambda kb, cr: block(kb, cr, False), (one, one))
        carry = block(qi, carry, True)
        for hh in range(2):
            m, l, acc = carry[hh]
            o_ref[:, hh * V_HEAD:(hh + 1) * V_HEAD] = acc / l
            l_ref[:, hh:hh + 1] = m + jnp.log(l)

    return pl.pallas_call(
        body, name="attn_fwd", grid=(hp, nq),
        in_specs=[pl.BlockSpec((t, 2 * QK_NOPE), lambda h, i: (i, h)), pl.BlockSpec((t, LANES), lambda h, i: (i, h)),
                  pl.BlockSpec((s, 4 * QK_NOPE), lambda h, i: (0, h)), _full((s, LANES)),
                  pl.BlockSpec((t, LANES), lambda h, i: (i, 0)), pl.BlockSpec((t, LANES), lambda h, i: (i, 0))],
        out_specs=[pl.BlockSpec((t, 2 * V_HEAD), lambda h, i: (i, h)), pl.BlockSpec((t, LANES), lambda h, i: (i, h)),
                   pl.BlockSpec((None, t, 2), lambda h, i: (h, i, 0))],
        out_shape=[jax.ShapeDtypeStruct((s, HEADS * V_HEAD), F32), jax.ShapeDtypeStruct((s, HEADS * QK_ROPE), BF16),
                   jax.ShapeDtypeStruct((hp, s, 2), F32)],
        scratch_shapes=[pltpu.VMEM((2, s, 2 * QK_NOPE), BF16)],
        compiler_params=_params("parallel", "arbitrary"))(qn, qp, kv, kpr, cos4, sin4)


def attn_bwd2(qn, qpr, kv, kpr, o, do, lse, cos4, sin4):
    s = qn.shape[0]
    hp = HEADS // 2
    t = _tile(s, ATT_TILE)
    nk = s // t

    def body(qn_ref, qpr_ref, kv_ref, kp_ref, o_ref, do_ref, l_ref, c_ref, s_ref,
             dqn_ref, dqp_ref, dkv_ref, dkp_ref, qcat_ref, dq_ref, delta_ref):
        ki = pl.program_id(1)

        @pl.when(ki == 0)
        def _():
            dq_ref[...] = jnp.zeros_like(dq_ref)
            for hh in range(2):
                qcat_ref[hh] = _q_cat(qn_ref[:, hh * QK_NOPE:(hh + 1) * QK_NOPE], qpr_ref[...], hh)
                cols = slice(hh * V_HEAD, (hh + 1) * V_HEAD)
                delta_ref[hh] = jnp.sum(do_ref[:, cols] * o_ref[:, cols], axis=-1, keepdims=True)

        rows_k = pl.ds(pl.multiple_of(ki * t, t), t)
        kcat = [jnp.concatenate([kv_ref[rows_k, 2 * hh * QK_NOPE:(2 * hh + 1) * QK_NOPE], kp_ref[rows_k, :]], axis=1) for hh in range(2)]
        vs = [kv_ref[rows_k, (2 * hh + 1) * QK_NOPE:(2 * hh + 2) * QK_NOPE] for hh in range(2)]

        def block(qb, carry, diagonal):
            rows = pl.ds(pl.multiple_of(qb * t, t), t)
            out = []
            for hh in range(2):
                dkc, dv = carry[hh]
                q_c = qcat_ref[hh, rows, :]
                do_b = do_ref[rows, hh * V_HEAD:(hh + 1) * V_HEAD].astype(BF16)
                sc = _dot_nt(q_c, kcat[hh]) * _ATT_SCALE
                if diagonal:
                    sc = _causal(sc)
                p = jnp.exp(sc - l_ref[rows, hh:hh + 1])
                dpv = _dot_nt(do_b, vs[hh])
                ds = (p * (dpv - delta_ref[hh, rows, :]) * _ATT_SCALE).astype(BF16)
                dv = dv + _dot_tn(p.astype(BF16), do_b)
                dkc = dkc + _dot_tn(ds, q_c)
                dq_ref[hh, rows, :] += jnp.dot(ds, kcat[hh], preferred_element_type=F32)
                out.append((dkc, dv))
            return tuple(out)

        one = (jnp.zeros((t, 2 * QK_NOPE), F32), jnp.zeros((t, V_HEAD), F32))
        carry = block(ki, (one, one), True)
        carry = lax.fori_loop(ki + 1, nk, lambda qb, cr: block(qb, cr, False), carry)
        dkp = jnp.zeros((t, LANES), F32)
        for hh in range(2):
            dkc, dv = carry[hh]
            dkv_ref[:, 2 * hh * QK_NOPE:(2 * hh + 1) * QK_NOPE] = dkc[:, :QK_NOPE].astype(dkv_ref.dtype)
            dkv_ref[:, (2 * hh + 1) * QK_NOPE:(2 * hh + 2) * QK_NOPE] = dv.astype(dkv_ref.dtype)
            dkp = dkp + dkc[:, QK_NOPE:]
        dkp_ref[...] = dkp

        @pl.when(ki == nk - 1)
        def _():
            lane = lax.broadcasted_iota(jnp.int32, (s, LANES), 1)
            dqp = jnp.where(lane < QK_ROPE, dq_ref[0, :, QK_NOPE:], dq_ref[1, :, QK_NOPE:])
            dqp_ref[...] = _rope(dqp, c_ref[...], -s_ref[...]).astype(dqp_ref.dtype)
            for hh in range(2):
                dqn_ref[:, hh * QK_NOPE:(hh + 1) * QK_NOPE] = dq_ref[hh, :, :QK_NOPE].astype(dqn_ref.dtype)

    qblk = pl.BlockSpec((s, 2 * QK_NOPE), lambda h, i: (0, h))
    pblk = pl.BlockSpec((s, LANES), lambda h, i: (0, h))
    tab = _full((s, LANES))
    return pl.pallas_call(
        body, name="attn_bwd", grid=(hp, nk),
        in_specs=[qblk, pblk, pl.BlockSpec((s, 4 * QK_NOPE), lambda h, i: (0, h)), tab, qblk, qblk,
                  pl.BlockSpec((None, s, 2), lambda h, i: (h, 0, 0)), tab, tab],
        out_specs=[qblk, pblk, pl.BlockSpec((t, 4 * QK_NOPE), lambda h, i: (i, h)), pl.BlockSpec((None, t, LANES), lambda h, i: (h, i, 0))],
        out_shape=[jax.ShapeDtypeStruct((s, HEADS * QK_NOPE), BF16), jax.ShapeDtypeStruct((s, HEADS * QK_ROPE), BF16),
                   jax.ShapeDtypeStruct((s, HEADS * 2 * QK_NOPE), BF16), jax.ShapeDtypeStruct((hp, s, LANES), F32)],
        scratch_shapes=[pltpu.VMEM((2, s, 2 * QK_NOPE), BF16), pltpu.VMEM((2, s, 2 * QK_NOPE), F32), pltpu.VMEM((2, s, 1), F32)],
        compiler_params=_params("parallel", "arbitrary"))(qn, qpr, kv, kpr, o, do, lse, cos4, sin4)


def _causal_at(sc, row0, col0):
    row = lax.broadcasted_iota(jnp.int32, sc.shape, 0) + row0
    col = lax.broadcasted_iota(jnp.int32, sc.shape, 1) + col0
    return jnp.where(col <= row, sc, _NEG)


def attn_fwd3(qn, qp, kv, kpr, cos4, sin4):
    s = qn.shape[0]
    hp = HEADS // 2
    t = _tile(s, ATT_TILE)
    tk = 2 * t
    nq = s // t
    assert s % tk == 0

    def body(qn_ref, qp_ref, kv_ref, kp_ref, c_ref, s_ref, o_ref, qpr_ref, l_ref, kcat_ref):
        qi = pl.program_id(1)

        @pl.when(qi == 0)
        def _():
            for hh in range(2):
                kcat_ref[hh, :, 0:QK_NOPE] = kv_ref[:, 2 * hh * QK_NOPE:(2 * hh + 1) * QK_NOPE]
                kcat_ref[hh, :, QK_NOPE:] = kp_ref[...]

        qpr = _rope(qp_ref[...], c_ref[...], s_ref[...]).astype(BF16)
        qpr_ref[...] = qpr
        qcat = [_q_cat(qn_ref[:, hh * QK_NOPE:(hh + 1) * QK_NOPE], qpr, hh) for hh in range(2)]

        def block(kb, carry, diagonal):
            start = pl.multiple_of(kb * tk, tk)
            rows = pl.ds(start, tk)
            out = []
            for hh in range(2):
                m, l, acc = carry[hh]
                sc = _dot_nt(qcat[hh], kcat_ref[hh, rows, :]) * _ATT_SCALE
                if diagonal:
                    sc = _causal_at(sc, qi * t, start)
                m_new = jnp.maximum(m, jnp.max(sc, axis=-1, keepdims=True))
                alpha = jnp.exp(m - m_new)
                p = jnp.exp(sc - m_new)
                l = alpha * l + jnp.sum(p, axis=-1, keepdims=True)
                v = kv_ref[rows, (2 * hh + 1) * QK_NOPE:(2 * hh + 2) * QK_NOPE]
                acc = alpha * acc + jnp.dot(p.astype(BF16), v, preferred_element_type=F32)
                out.append((m_new, l, acc))
            return tuple(out)

        one = (jnp.full((t, 1), _NEG, F32), jnp.zeros((t, 1), F32), jnp.zeros((t, V_HEAD), F32))
        carry = lax.fori_loop(0, qi // 2, lambda kb, cr: block(kb, cr, False), (one, one))
        carry = block(qi // 2, carry, True)
        for hh in range(2):
            m, l, acc = carry[hh]
            o_ref[:, hh * V_HEAD:(hh + 1) * V_HEAD] = acc / l
            l_ref[:, hh:hh + 1] = m + jnp.log(l)

    return pl.pallas_call(
        body, name="attn_fwd", grid=(hp, nq),
        in_specs=[pl.BlockSpec((t, 2 * QK_NOPE), lambda h, i: (i, h)), pl.BlockSpec((t, LANES), lambda h, i: (i, h)),
                  pl.BlockSpec((s, 4 * QK_NOPE), lambda h, i: (0, h)), _full((s, LANES)),
                  pl.BlockSpec((t, LANES), lambda h, i: (i, 0)), pl.BlockSpec((t, LANES), lambda h, i: (i, 0))],
        out_specs=[pl.BlockSpec((t, 2 * V_HEAD), lambda h, i: (i, h)), pl.BlockSpec((t, LANES), lambda h, i: (i, h)),
                   pl.BlockSpec((None, t, 2), lambda h, i: (h, i, 0))],
        out_shape=[jax.ShapeDtypeStruct((s, HEADS * V_HEAD), F32), jax.ShapeDtypeStruct((s, HEADS * QK_ROPE), BF16),
                   jax.ShapeDtypeStruct((hp, s, 2), F32)],
        scratch_shapes=[pltpu.VMEM((2, s, 2 * QK_NOPE), BF16)],
        compiler_params=_params("parallel", "arbitrary"))(qn, qp, kv, kpr, cos4, sin4)


def attn_bwd3(qn, qpr, kv, kpr, o, do, lse, cos4, sin4):
    s = qn.shape[0]
    hp = HEADS // 2
    t = _tile(s, ATT_TILE)
    tq = 2 * t
    nk = s // t
    nq2 = s // tq
    assert s % tq == 0

    def body(qn_ref, qpr_ref, kv_ref, kp_ref, o_ref, do_ref, l_ref, c_ref, s_ref,
             dqn_ref, dqp_ref, dkv_ref, dkp_ref, qcat_ref, dq_ref, delta_ref):
        ki = pl.program_id(1)

        @pl.when(ki == 0)
        def _():
            dq_ref[...] = jnp.zeros_like(dq_ref)
            for hh in range(2):
                qcat_ref[hh] = _q_cat(qn_ref[:, hh * QK_NOPE:(hh + 1) * QK_NOPE], qpr_ref[...], hh)
                cols = slice(hh * V_HEAD, (hh + 1) * V_HEAD)
                delta_ref[hh] = jnp.sum(do_ref[:, cols] * o_ref[:, cols], axis=-1, keepdims=True)

        rows_k = pl.ds(pl.multiple_of(ki * t, t), t)
        kcat = [jnp.concatenate([kv_ref[rows_k, 2 * hh * QK_NOPE:(2 * hh + 1) * QK_NOPE], kp_ref[rows_k, :]], axis=1) for hh in range(2)]
        vs = [kv_ref[rows_k, (2 * hh + 1) * QK_NOPE:(2 * hh + 2) * QK_NOPE] for hh in range(2)]

        def block(qb, carry, diagonal):
            start = pl.multiple_of(qb * tq, tq)
            rows = pl.ds(start, tq)
            out = []
            for hh in range(2):
                dkc, dv = carry[hh]
                q_c = qcat_ref[hh, rows, :]
                do_b = do_ref[rows, hh * V_HEAD:(hh + 1) * V_HEAD].astype(BF16)
                sc = _dot_nt(q_c, kcat[hh]) * _ATT_SCALE
                if diagonal:
                    sc = _causal_at(sc, start, ki * t)
                p = jnp.exp(sc - l_ref[rows, hh:hh + 1])
                dpv = _dot_nt(do_b, vs[hh])
                ds = (p * (dpv - delta_ref[hh, rows, :]) * _ATT_SCALE).astype(BF16)
                dv = dv + _dot_tn(p.astype(BF16), do_b)
                dkc = dkc + _dot_tn(ds, q_c)
                dq_ref[hh, rows, :] += jnp.dot(ds, kcat[hh], preferred_element_type=F32)
                out.append((dkc, dv))
            return tuple(out)

        one = (jnp.zeros((t, 2 * QK_NOPE), F32), jnp.zeros((t, V_HEAD), F32))
        carry = block(ki // 2, (one, one), True)
        carry = lax.fori_loop(ki // 2 + 1, nq2, lambda qb, cr: block(qb, cr, False), carry)
        dkp = jnp.zeros((t, LANES), F32)
        for hh in range(2):
            dkc, dv = carry[hh]
            dkv_ref[:, 2 * hh * QK_NOPE:(2 * hh + 1) * QK_NOPE] = dkc[:, :QK_NOPE].astype(dkv_ref.dtype)
            dkv_ref[:, (2 * hh + 1) * QK_NOPE:(2 * hh + 2) * QK_NOPE] = dv.astype(dkv_ref.dtype)
            dkp = dkp + dkc[:, QK_NOPE:]
        dkp_ref[...] = dkp

        @pl.when(ki == nk - 1)
        def _():
            lane = lax.broadcasted_iota(jnp.int32, (s, LANES), 1)
            dqp = jnp.where(lane < QK_ROPE, dq_ref[0, :, QK_NOPE:], dq_ref[1, :, QK_NOPE:])
            dqp_ref[...] = _rope(dqp, c_ref[...], -s_ref[...]).astype(dqp_ref.dtype)
            for hh in range(2):
                dqn_ref[:, hh * QK_NOPE:(hh + 1) * QK_NOPE] = dq_ref[hh, :, :QK_NOPE].astype(dqn_ref.dtype)

    qblk = pl.BlockSpec((s, 2 * QK_NOPE), lambda h, i: (0, h))
    pblk = pl.BlockSpec((s, LANES), lambda h, i: (0, h))
    tab = _full((s, LANES))
    return pl.pallas_call(
        body, name="attn_bwd", grid=(hp, nk),
        in_specs=[qblk, pblk, pl.BlockSpec((s, 4 * QK_NOPE), lambda h, i: (0, h)), tab, qblk, qblk,
                  pl.BlockSpec((None, s, 2), lambda h, i: (h, 0, 0)), tab, tab],
        out_specs=[qblk, pblk, pl.BlockSpec((t, 4 * QK_NOPE), lambda h, i: (i, h)), pl.BlockSpec((None, t, LANES), lambda h, i: (h, i, 0))],
        out_shape=[jax.ShapeDtypeStruct((s, HEADS * QK_NOPE), BF16), jax.ShapeDtypeStruct((s, HEADS * QK_ROPE), BF16),
                   jax.ShapeDtypeStruct((s, HEADS * 2 * QK_NOPE), BF16), jax.ShapeDtypeStruct((hp, s, LANES), F32)],
        scratch_shapes=[pltpu.VMEM((2, s, 2 * QK_NOPE), BF16), pltpu.VMEM((2, s, 2 * QK_NOPE), F32), pltpu.VMEM((2, s, 1), F32)],
        compiler_params=_params("parallel", "arbitrary"))(qn, qpr, kv, kpr, o, do, lse, cos4, sin4)


def kpe_bwd(dkp, cos4, sin4, pad_cols):
    hp, s, _ = dkp.shape
    tr = _tile(s, ROW_TILE * 2)

    def body(d_ref, c_ref, s_ref, o_ref):
        tot = d_ref[0]
        for h in range(1, hp):
            tot = tot + d_ref[h]
        tot = tot + pltpu.roll(tot, QK_ROPE, 1)
        lane = lax.broadcasted_iota(jnp.int32, tot.shape, 1)
        dk = jnp.where(lane < QK_ROPE, _rope(tot, c_ref[...], -s_ref[...]), jnp.zeros_like(tot))
        o_ref[...] = jnp.zeros_like(o_ref)
        o_ref[:, 0:LANES] = dk.astype(o_ref.dtype)

    row = pl.BlockSpec((tr, LANES), lambda i: (i, 0))
    return pl.pallas_call(body, name="kpe_bwd", grid=(s // tr,),
                          in_specs=[pl.BlockSpec((hp, tr, LANES), lambda i: (0, i, 0)), row, row],
                          out_specs=pl.BlockSpec((tr, pad_cols), lambda i: (i, 0)),
                          out_shape=jax.ShapeDtypeStruct((s, pad_cols), BF16), compiler_params=_params("parallel"))(dkp, cos4, sin4)


def _shift_down(x, n):
    row = lax.broadcasted_iota(jnp.int32, x.shape, 0)
    return jnp.where(row >= n, pltpu.roll(x, n, 0), jnp.zeros_like(x))


def _shift_up(x, n):
    rows = x.shape[0]
    row = lax.broadcasted_iota(jnp.int32, x.shape, 0)
    return jnp.where(row < rows - n, pltpu.roll(x, rows - n, 0), jnp.zeros_like(x))


def _conv(x, w_ref, b_ref):
    return w_ref[2:3, :] * x + w_ref[1:2, :] * _shift_down(x, 1) + w_ref[0:1, :] * _shift_down(x, 2) + b_ref[...]


def conv_act_fwd(upre, conv_w, conv_b):
    s, f2 = upre.shape
    f = f2 // 2
    tc = _tile(f, COL_TILE)
    nc = f // tc

    def body(ug_ref, uv_ref, wg_ref, wv_ref, bg_ref, bv_ref, o_ref):
        gh = _conv(ug_ref[...], wg_ref, bg_ref)
        vh = _conv(uv_ref[...], wv_ref, bv_ref)
        o_ref[...] = (gh * _sigmoid(gh) * vh).astype(o_ref.dtype)

    def spec(rows, shift):
        return pl.BlockSpec((rows, tc), lambda j: (0, j + shift))

    return pl.pallas_call(
        body, name="conv_act_fwd", grid=(nc,),
        in_specs=[spec(s, 0), spec(s, nc), spec(3, 0), spec(3, nc), spec(1, 0), spec(1, nc)], out_specs=spec(s, 0),
        out_shape=jax.ShapeDtypeStruct((s, f), BF16), compiler_params=_params("parallel"))(upre, upre, conv_w, conv_w, conv_b, conv_b)


def conv_act_bwd(upre, conv_w, conv_b, df):
    s, f2 = upre.shape
    f = f2 // 2
    tc = _tile(f, COL_TILE)
    nc = f // tc

    def half(x, d, w_ref, du_ref, which, gw_ref, gb_ref):
        d1, d2 = _shift_up(d, 1), _shift_up(d, 2)
        gb_ref[...] = _colsum(d)
        gw_ref[2:3, :] = _colsum(d * x)
        gw_ref[1:2, :] = _colsum(d1 * x)
        gw_ref[0:1, :] = _colsum(d2 * x)
        du_ref[which] = (w_ref[2:3, :] * d + w_ref[1:2, :] * d1 + w_ref[0:1, :] * d2).astype(du_ref.dtype)

    def body(ug_ref, uv_ref, wg_ref, wv_ref, bg_ref, bv_ref, df_ref, du_ref, gwg_ref, gwv_ref, gbg_ref, gbv_ref):
        xg, xv = ug_ref[...], uv_ref[...]
        gh = _conv(xg, wg_ref, bg_ref)
        vh = _conv(xv, wv_ref, bv_ref)
        sg = _sigmoid(gh)
        df_v = df_ref[...]
        half(xg, df_v * vh * (sg * (1.0 + gh * (1.0 - sg))), wg_ref, du_ref, 0, gwg_ref, gbg_ref)
        half(xv, df_v * (gh * sg), wv_ref, du_ref, 1, gwv_ref, gbv_ref)

    def spec(rows, shift):
        return pl.BlockSpec((rows, tc), lambda j: (0, j + shift))

    gw = jax.ShapeDtypeStruct((3, f), F32)
    gb = jax.ShapeDtypeStruct((1, f), F32)
    return pl.pallas_call(
        body, name="conv_act_bwd", grid=(nc,),
        in_specs=[spec(s, 0), spec(s, nc), spec(3, 0), spec(3, nc), spec(1, 0), spec(1, nc), spec(s, 0)],
        out_specs=[pl.BlockSpec((2, s, tc), lambda j: (0, 0, j)), spec(3, 0), spec(3, 0), spec(1, 0), spec(1, 0)],
        out_shape=[jax.ShapeDtypeStruct((2, s, f), BF16), gw, gw, gb, gb],
        compiler_params=_params("parallel"))(upre, upre, conv_w, conv_w, conv_b, conv_b, df)


def _elementwise_tile(r, c, limit):
    if r % 8:
        return r, c
    best = (8, c if c % LANES else LANES)
    for k in (1, 2, 4, 8, 16):
        if k > 1 and c % (LANES * k):
            continue
        tc = c // k
        tr = max(8, min(r, limit // tc) // 8 * 8)
        while r % tr:
            tr -= 8
        if tr * tc <= max(limit, 8 * tc) and tr * tc > best[0] * best[1]:
            best = (tr, tc)
    return best


def adamw(name, w, m, v, parts):
    npart, r, c = parts.shape
    tr, tc = _elementwise_tile(r, c, ADAMW_TILE_ELEMS)
    bc1 = 1.0 - ADAM_B1 ** ADAM_STEP
    bc2 = 1.0 - ADAM_B2 ** ADAM_STEP

    def body(w_ref, m_ref, v_ref, p_ref, g_ref, d_ref, nm_ref, nv_ref):
        g = p_ref[0].astype(F32)
        for k in range(1, npart):
            g = g + p_ref[k].astype(F32)
        m_new = ADAM_B1 * m_ref[...] + (1.0 - ADAM_B1) * g
        v_new = ADAM_B2 * v_ref[...] + (1.0 - ADAM_B2) * (g * g)
        g_ref[...] = g
        nm_ref[...] = m_new
        nv_ref[...] = v_new
        d_ref[...] = -ADAM_LR * ((m_new / bc1) / (jnp.sqrt(v_new / bc2) + ADAM_EPS) + ADAM_WD * w_ref[...])

    deps = _TOKENS.take()
    blk = pl.BlockSpec((tr, tc), lambda i, j: (i, j))
    out = jax.ShapeDtypeStruct((r, c), F32)
    return pl.pallas_call(
        lambda *refs: body(*refs[:4], *refs[4 + len(deps):]), name=name, grid=(r // tr, c // tc),
        in_specs=[blk, blk, blk, pl.BlockSpec((npart, tr, tc), lambda i, j: (0, i, j))] + [pl.BlockSpec(memory_space=pl.ANY)] * len(deps),
        out_specs=[blk, blk, blk, blk], out_shape=[out, out, out, out],
        compiler_params=_params("parallel", "parallel"))(w, m, v, parts, *deps)


def _position():
    return lax.axis_index("x"), lax.axis_index("y"), lax.axis_index("c")


def _index(p):
    return 4 * p[0] + 2 * p[1] + p[2]


def _peer(me, r):
    return (me[0] ^ ((r >> 2) & 1), me[1] ^ ((r >> 1) & 1), me[2] ^ (r & 1))


_ANY = pl.BlockSpec(memory_space=pl.ANY)


def all_gather_two_level(shards):
    n = len(shards)

    def body(*refs):
        ins, outs = refs[:n], refs[n:2 * n]
        send_sems, recv_sems, local_sems = refs[2 * n:]
        x, y, c = _position()
        me, sibling = (x, y, c), (x, y, 1 - c)
        chips = [(1 - x, y), (x, 1 - y), (1 - x, 1 - y)]

        def copy(w, k, block, to, src=None):
            slot = outs[w].at[_index(block)]
            return pltpu.make_async_remote_copy(src_ref=slot if src is None else src, dst_ref=slot,
                                                send_sem=send_sems.at[7 * w + k], recv_sem=recv_sems.at[7 * w + k],
                                                device_id=to, device_id_type=MESH)

        mine = [pltpu.make_async_copy(ins[w], outs[w].at[_index(me)], local_sems.at[w]) for w in range(n)]
        for cp in mine:
            cp.start()
        first = []
        for w in range(n):
            first.append(copy(w, 0, me, sibling, src=ins[w]))
            first += [copy(w, 1 + j, me, (*chip, c), src=ins[w]) for j, chip in enumerate(chips)]
        for cp in first:
            cp.start()
        passed = []
        for w in range(n):
            for j, chip in enumerate(chips):
                copy(w, 1 + j, (*chip, c), me).wait_recv()
                cp = copy(w, 4 + j, (*chip, c), sibling)
                cp.start()
                passed.append(cp)
        for w in range(n):
            copy(w, 0, sibling, me).wait_recv()
            for j, chip in enumerate(chips):
                copy(w, 4 + j, (*chip, 1 - c), me).wait_recv()
        for cp in first + passed:
            cp.wait_send()
        for cp in mine:
            cp.wait()

    return pl.pallas_call(
        body, name="all_gather_weights",
        out_shape=[jax.ShapeDtypeStruct((N_DEV,) + a.shape, a.dtype) for a in shards],
        in_specs=[_ANY] * n, out_specs=[_ANY] * n,
        scratch_shapes=[pltpu.SemaphoreType.DMA((7 * n,)), pltpu.SemaphoreType.DMA((7 * n,)), pltpu.SemaphoreType.DMA((n,))],
        )(*shards)


def exchange(name, arrays, scatter):
    n = len(arrays)

    def body(*refs):
        ins, outs = refs[:n], refs[n:2 * n]
        send_sems, recv_sems, local_sems = refs[2 * n:]
        me = _position()
        copies = []
        for w in range(n):
            src = ins[w].at[_index(me)] if scatter else ins[w]
            cp = pltpu.make_async_copy(src, outs[w].at[_index(me)], local_sems.at[w])
            cp.start()
            copies.append(cp)
        remote = []
        for w in range(n):
            for r in range(1, N_DEV):
                peer = _peer(me, r)
                src = ins[w].at[_index(peer)] if scatter else ins[w]
                cp = pltpu.make_async_remote_copy(src_ref=src, dst_ref=outs[w].at[_index(me)],
                                                  send_sem=send_sems.at[7 * w + r - 1], recv_sem=recv_sems.at[7 * w + r - 1],
                                                  device_id=peer, device_id_type=MESH)
                cp.start()
                remote.append(cp)
        for cp in remote:
            cp.wait()
        for cp in copies:
            cp.wait()

    blocks = [a.shape[1:] if scatter else a.shape for a in arrays]
    return pl.pallas_call(
        body, name=name,
        out_shape=[jax.ShapeDtypeStruct((N_DEV,) + b, a.dtype) for a, b in zip(arrays, blocks)],
        in_specs=[_ANY] * n, out_specs=[_ANY] * n,
        scratch_shapes=[pltpu.SemaphoreType.DMA((7 * n,)), pltpu.SemaphoreType.DMA((7 * n,)), pltpu.SemaphoreType.DMA((n,))],
        )(*arrays)


_HBM = pl.BlockSpec(memory_space=pltpu.HBM)
_SEM = pl.BlockSpec(memory_space=pltpu.SEMAPHORE)
_EFFECT = pltpu.SideEffectType.DATAFLOW_SIDE_EFFECTING


def _direct_copies(ins, lands, send_sems, recv_sems, scatter):
    me = _position()
    copies = []
    for w in range(len(ins)):
        for r in range(1, N_DEV):
            peer = _peer(me, r)
            src = ins[w].at[_index(peer)] if scatter else ins[w]
            copies.append(pltpu.make_async_remote_copy(src_ref=src, dst_ref=lands[w].at[_index(me)], send_sem=send_sems.at[7 * w + r - 1],
                                                       recv_sem=recv_sems.at[7 * w + r - 1], device_id=peer, device_id_type=MESH))
    return copies


def exchange_start(name, groups, scatter):
    arrays = [a for g in groups for a in g]
    n = len(arrays)
    blocks = [a.shape[1:] if scatter else a.shape for a in arrays]
    lands = [lax.empty((N_DEV,) + b, a.dtype) for a, b in zip(arrays, blocks)]
    ng = len(groups)

    def body(*refs):
        ins, lnd = refs[:n], refs[n:2 * n]
        sems = refs[2 * n:2 * n + 2 * ng]
        token = refs[2 * n + 2 * ng + 2 * n]
        local_sem = refs[2 * n + 2 * ng + 2 * n + 1]
        me = _position()
        local = []
        for w in range(n):
            src = ins[w].at[_index(me)] if scatter else ins[w]
            cp = pltpu.make_async_copy(src, lnd[w].at[_index(me)], local_sem.at[w])
            cp.start()
            local.append(cp)
        w0 = 0
        for gi, g in enumerate(groups):
            for cp in _direct_copies(ins[w0:w0 + len(g)], lnd[w0:w0 + len(g)], sems[2 * gi], sems[2 * gi + 1], scatter):
                cp.start()
            w0 += len(g)
        for cp in local:
            cp.wait()
        token[...] = jnp.zeros_like(token)

    sem_shapes = []
    for g in groups:
        sem_shapes += [pltpu.SemaphoreType.DMA((7 * len(g),)), pltpu.SemaphoreType.DMA((7 * len(g),))]
    out = pl.pallas_call(
        body, name=name,
        out_shape=tuple(sem_shapes) + tuple(pltpu.HBM(a.shape, a.dtype) for a in arrays) + tuple(pltpu.HBM(l.shape, l.dtype) for l in lands)
        + (jax.ShapeDtypeStruct((8, LANES), F32),),
        in_specs=[_HBM] * (2 * n), out_specs=tuple([_SEM] * (2 * ng) + [_HBM] * (2 * n) + [pl.BlockSpec(memory_space=pltpu.VMEM)]),
        input_output_aliases={i: 2 * ng + i for i in range(2 * n)},
        scratch_shapes=[pltpu.SemaphoreType.DMA((n,))],
        compiler_params=pltpu.CompilerParams(has_side_effects=_EFFECT),
    )(*[pltpu.with_memory_space_constraint(a, pltpu.HBM) for a in arrays], *[pltpu.with_memory_space_constraint(l, pltpu.HBM) for l in lands])
    sems, thru, token = out[:2 * ng], out[2 * ng:2 * ng + 2 * n], out[-1]
    res, w0 = [], 0
    for gi, g in enumerate(groups):
        res.append((sems[2 * gi], sems[2 * gi + 1], list(thru[w0:w0 + len(g)]), list(thru[n + w0:n + w0 + len(g)])))
        w0 += len(g)
    return res, token


def exchange_wait(name, group, after, scatter):
    send_sems, recv_sems, srcs, lands = group
    n = len(srcs)

    def body(*refs):
        ins, lnd = refs[:n], refs[n:2 * n]
        for cp in _direct_copies(ins, lnd, refs[2 * n], refs[2 * n + 1], scatter):
            cp.wait_send()
            cp.wait_recv()

    out = pl.pallas_call(
        body, name=name, out_shape=tuple(pltpu.HBM(a.shape, a.dtype) for a in srcs + lands),
        in_specs=[_HBM] * (2 * n) + [_SEM, _SEM, pl.BlockSpec(memory_space=pl.ANY)], out_specs=tuple([_HBM] * (2 * n)),
        input_output_aliases={i: i for i in range(2 * n)},
        compiler_params=pltpu.CompilerParams(has_side_effects=_EFFECT),
    )(*srcs, *lands, send_sems, recv_sems, after)
    return list(out[n:])


def _after(x, token):
    return lax.optimization_barrier((x, token))[0]


_TOKEN = jax.ShapeDtypeStruct((8, LANES), F32)
_VM = pl.BlockSpec(memory_space=pltpu.VMEM)
_SIDE = pltpu.CompilerParams(has_side_effects=_EFFECT)


def _hbm(a):
    return pltpu.with_memory_space_constraint(a, pltpu.HBM)


def _like(a):
    return pltpu.HBM(a.shape, a.dtype)


def _dma_sems(n):
    return pltpu.SemaphoreType.DMA((n,))


def _other_chips(x, y):
    return [(1 - x, y), (x, 1 - y), (1 - x, 1 - y)]


COPY_STREAMS = 8


def _row_chunks(src, dst):
    rows = src.shape[0]
    n = COPY_STREAMS
    while n > 1 and rows % (16 * n):
        n //= 2
    r = rows // n
    return [(src.at[pl.ds(i * r, r)], dst.at[pl.ds(i * r, r)]) for i in range(n)]


def _local_copy(src, dst, sem):
    return [pltpu.make_async_copy(s, d, sem) for s, d in _row_chunks(src, dst)]


class _rcopy:
    def __init__(self, src, dst, send_sem, recv_sem, to):
        self.parts = [pltpu.make_async_remote_copy(src_ref=s, dst_ref=d, send_sem=send_sem, recv_sem=recv_sem, device_id=to, device_id_type=MESH)
                      for s, d in _row_chunks(src, dst)]

    def start(self):
        for cp in self.parts:
            cp.start()

    def wait_send(self):
        for cp in self.parts:
            cp.wait_send()

    def wait_recv(self):
        for cp in self.parts:
            cp.wait_recv()


def _afters(after):
    return list(after) if isinstance(after, (list, tuple)) else [after]


def ag_start(name, shards, after):
    n = len(shards)
    lands = [lax.empty((N_DEV,) + a.shape, a.dtype) for a in shards]
    afters = _afters(after)
    na = len(afters)

    def body(*refs):
        ins, lnd, send_sems, recv_sems, token = refs[:n], refs[n:2 * n], refs[2 * n + na], refs[2 * n + na + 1], refs[4 * n + na + 2]
        x, y, c = _position()
        for w in range(n):
            slot = lnd[w].at[_index((x, y, c))]
            for k, to in enumerate([(x, y, 1 - c)] + [(*chip, c) for chip in _other_chips(x, y)]):
                _rcopy(ins[w], slot, send_sems.at[4 * w + k], recv_sems.at[4 * w + k], to).start()
        token[...] = jnp.zeros_like(token)

    out = pl.pallas_call(
        body, name=name, out_shape=(_dma_sems(4 * n), _dma_sems(4 * n)) + tuple(_like(a) for a in shards + lands) + (_TOKEN,),
        in_specs=[_HBM] * (2 * n) + [_ANY] * na, out_specs=(_SEM, _SEM) + (_HBM,) * (2 * n) + (_VM,),
        input_output_aliases={i: 2 + i for i in range(2 * n)}, compiler_params=_SIDE)(*[_hbm(a) for a in shards + lands], *afters)
    _TOKENS.push(out[-1])
    return out[0], out[1], list(out[2:2 + n]), list(out[2 + n:2 + 2 * n])


def ag_forward(name, started, after):
    send, recv, shards, lands = started
    n = len(shards)
    afters = list(after) if isinstance(after, (list, tuple)) else [after]
    na = len(afters)

    def body(*refs):
        ins, lnd, send_sems, recv_sems = refs[:n], refs[n:2 * n], refs[2 * n], refs[2 * n + 1]
        fsend, frecv, token = refs[2 * n + 2 + na], refs[2 * n + 3 + na], refs[4 * n + 4 + na]
        x, y, c = _position()
        for w in range(n):
            for j, chip in enumerate(_other_chips(x, y)):
                slot = lnd[w].at[_index((*chip, c))]
                _rcopy(ins[w], slot, send_sems.at[4 * w + 1 + j], recv_sems.at[4 * w + 1 + j], (*chip, c)).wait_recv()
                _rcopy(slot, slot, fsend.at[3 * w + j], frecv.at[3 * w + j], (x, y, 1 - c)).start()
        token[...] = jnp.zeros_like(token)

    out = pl.pallas_call(
        body, name=name, out_shape=(_dma_sems(3 * n), _dma_sems(3 * n)) + tuple(_like(a) for a in shards + lands) + (_TOKEN,),
        in_specs=[_HBM] * (2 * n) + [_SEM, _SEM] + [_ANY] * na, out_specs=(_SEM, _SEM) + (_HBM,) * (2 * n) + (_VM,),
        input_output_aliases={i: 2 + i for i in range(2 * n)}, compiler_params=_SIDE)(*shards, *lands, send, recv, *afters)
    _TOKENS.push(out[-1])
    return send, recv, out[0], out[1], list(out[2:2 + n]), list(out[2 + n:2 + 2 * n])


def ag_wait(name, forwarded, after):
    send, recv, fsend, frecv, shards, lands = forwarded
    n = len(shards)

    def body(*refs):
        ins, lnd, send_sems, recv_sems, fsend_r, frecv_r = refs[:n], refs[n:2 * n], refs[2 * n], refs[2 * n + 1], refs[2 * n + 2], refs[2 * n + 3]
        x, y, c = _position()
        sibling = (x, y, 1 - c)
        for w in range(n):
            own = lnd[w].at[_index((x, y, c))]
            _rcopy(ins[w], lnd[w].at[_index(sibling)], send_sems.at[4 * w], recv_sems.at[4 * w], sibling).wait_recv()
            for j, chip in enumerate(_other_chips(x, y)):
                _rcopy(ins[w], lnd[w].at[_index((*chip, 1 - c))], fsend_r.at[3 * w + j], frecv_r.at[3 * w + j], sibling).wait_recv()
            for k in range(4):
                _rcopy(ins[w], own, send_sems.at[4 * w + k], recv_sems.at[4 * w + k], sibling).wait_send()
            for j in range(3):
                _rcopy(ins[w], own, fsend_r.at[3 * w + j], frecv_r.at[3 * w + j], sibling).wait_send()

    out = pl.pallas_call(
        body, name=name, out_shape=tuple(_like(a) for a in shards + lands),
        in_specs=[_HBM] * (2 * n) + [_SEM] * 4 + [_ANY] * len(_afters(after)),
        out_specs=(_HBM,) * (2 * n), input_output_aliases={i: i for i in range(2 * n)},
        compiler_params=_SIDE)(*shards, *lands, send, recv, fsend, frecv, *_afters(after))
    return [lax.dynamic_update_index_in_dim(land, shard, _index(_position()), 0) for shard, land in zip(out[:n], out[n:])]


def rs_d2d_start(name, grads):
    n = len(grads)
    lands = [lax.empty((4,) + g.shape[1:], g.dtype) for g in grads]

    def body(*refs):
        ins, lnd, send_sems, recv_sems, token = refs[:n], refs[n:2 * n], refs[2 * n], refs[2 * n + 1], refs[4 * n + 2]
        x, y, c = _position()
        for w in range(n):
            for i in range(4):
                _rcopy(ins[w].at[2 * i + 1 - c], lnd[w].at[i], send_sems.at[4 * w + i], recv_sems.at[4 * w + i], (x, y, 1 - c)).start()
        token[...] = jnp.zeros_like(token)

    out = pl.pallas_call(
        body, name=name, out_shape=(_dma_sems(4 * n), _dma_sems(4 * n)) + tuple(_like(a) for a in grads + lands) + (_TOKEN,),
        in_specs=[_HBM] * (2 * n), out_specs=(_SEM, _SEM) + (_HBM,) * (2 * n) + (_VM,),
        input_output_aliases={i: 2 + i for i in range(2 * n)}, compiler_params=_SIDE)(*[_hbm(a) for a in grads + lands])
    _TOKENS.push(out[-1])
    return out[0], out[1], list(out[2:2 + n]), list(out[2 + n:2 + 2 * n])


def rs_d2d_wait(name, started, after):
    send, recv, grads, lands = started
    n = len(grads)

    def body(*refs):
        ins, lnd, send_sems, recv_sems = refs[:n], refs[n:2 * n], refs[2 * n], refs[2 * n + 1]
        x, y, c = _position()
        for w in range(n):
            for i in range(4):
                cp = _rcopy(ins[w].at[2 * i + 1 - c], lnd[w].at[i], send_sems.at[4 * w + i], recv_sems.at[4 * w + i], (x, y, 1 - c))
                cp.wait_send()
                cp.wait_recv()

    out = pl.pallas_call(
        body, name=name, out_shape=tuple(_like(a) for a in grads + lands),
        in_specs=[_HBM] * (2 * n) + [_SEM, _SEM] + [_ANY] * len(_afters(after)),
        out_specs=(_HBM,) * (2 * n), input_output_aliases={i: i for i in range(2 * n)},
        compiler_params=_SIDE)(*grads, *lands, send, recv, *_afters(after))
    return list(out[:n]), list(out[n:])


def pair_sum(name, grad, land, core):
    _, r, c = grad.shape
    tr = r
    if r % 8 == 0:
        tr = max(8, min(r, 4 * ADAMW_TILE_ELEMS // c) // 8 * 8)
        while r % tr:
            tr -= 8

    def body(core_ref, a_ref, b_ref, o_ref):
        o_ref[...] = (a_ref[...].astype(F32) + b_ref[...].astype(F32)).astype(o_ref.dtype)

    return pl.pallas_call(
        body, name=name, out_shape=jax.ShapeDtypeStruct((4, r, c), grad.dtype),
        grid_spec=pltpu.PrefetchScalarGridSpec(
            num_scalar_prefetch=1, grid=(4, r // tr),
            in_specs=[pl.BlockSpec((None, None, tr, c), lambda i, j, core_ref: (i, core_ref[0], j, 0)),
                      pl.BlockSpec((None, tr, c), lambda i, j, core_ref: (i, j, 0))],
            out_specs=pl.BlockSpec((None, tr, c), lambda i, j, core_ref: (i, j, 0))),
        compiler_params=_params("parallel", "parallel"))(core, grad.reshape(4, 2, r, c), land)


def rs_ici_start(name, sums):
    n = len(sums)
    lands = [lax.empty(a.shape, a.dtype) for a in sums]

    def body(*refs):
        ins, lnd, send_sems, recv_sems, token = refs[:n], refs[n:2 * n], refs[2 * n], refs[2 * n + 1], refs[4 * n + 2]
        x, y, c = _position()
        chip = 2 * x + y
        for w in range(n):
            for j, other in enumerate(_other_chips(x, y)):
                _rcopy(ins[w].at[2 * other[0] + other[1]], lnd[w].at[chip], send_sems.at[3 * w + j], recv_sems.at[3 * w + j], (*other, c)).start()
        token[...] = jnp.zeros_like(token)

    out = pl.pallas_call(
        body, name=name, out_shape=(_dma_sems(3 * n), _dma_sems(3 * n)) + tuple(_like(a) for a in sums + lands) + (_TOKEN,),
        in_specs=[_HBM] * (2 * n), out_specs=(_SEM, _SEM) + (_HBM,) * (2 * n) + (_VM,),
        input_output_aliases={i: 2 + i for i in range(2 * n)}, compiler_params=_SIDE)(*[_hbm(a) for a in sums + lands])
    _TOKENS.push(out[-1])
    return out[0], out[1], list(out[2:2 + n]), list(out[2 + n:2 + 2 * n])


def rs_ici_wait(name, started, after):
    send, recv, sums, lands = started
    n = len(sums)

    def body(*refs):
        ins, lnd, send_sems, recv_sems = refs[:n], refs[n:2 * n], refs[2 * n], refs[2 * n + 1]
        x, y, c = _position()
        for w in range(n):
            for j, other in enumerate(_other_chips(x, y)):
                cp = _rcopy(ins[w].at[2 * other[0] + other[1]], lnd[w].at[2 * other[0] + other[1]], send_sems.at[3 * w + j], recv_sems.at[3 * w + j], (*other, c))
                cp.wait_send()
                cp.wait_recv()

    out = pl.pallas_call(
        body, name=name, out_shape=tuple(_like(a) for a in sums + lands), in_specs=[_HBM] * (2 * n) + [_SEM, _SEM, _ANY],
        out_specs=(_HBM,) * (2 * n), input_output_aliases={i: i for i in range(2 * n)}, compiler_params=_SIDE)(*sums, *lands, send, recv, after)
    chip = 2 * lax.axis_index("x") + lax.axis_index("y")
    return [lax.dynamic_update_index_in_dim(land, lax.dynamic_index_in_dim(s, chip, 0, keepdims=False), chip, 0)
            for s, land in zip(out[:n], out[n:])]


def ada_fwd(c, w_ada, b_ada3, conv_w):
    d, cs = w_ada.shape

    def body(c_ref, w_ref, b_ref, cw_ref, mod_ref, sc_ref, cwa_ref, part_ref, send_sems, recv_sems):
        me = _position()
        my = _index(me)
        cv = c_ref[...]
        sc_ref[my] = cv * _sigmoid(cv)
        cwa_ref[my] = cw_ref[...]
        gather = []
        for r in range(1, N_DEV):
            for k, ref in enumerate((sc_ref, cwa_ref)):
                cp = pltpu.make_async_remote_copy(src_ref=ref.at[my], dst_ref=ref.at[my], send_sem=send_sems.at[14 * k + r - 1],
                                                  recv_sem=recv_sems.at[14 * k + r - 1], device_id=_peer(me, r), device_id_type=MESH)
                cp.start()
                gather.append(cp)
        for cp in gather:
            cp.wait()
        sc_all = jnp.concatenate([sc_ref[k] for k in range(N_DEV)], axis=0).astype(BF16)
        part = jnp.dot(sc_all, w_ref[...].astype(BF16), preferred_element_type=F32)
        for k in range(N_DEV):
            part_ref[k] = part[k:k + 1, :]
        scatter = []
        for r in range(1, N_DEV):
            peer = _peer(me, r)
            cp = pltpu.make_async_remote_copy(src_ref=part_ref.at[_index(peer)], dst_ref=mod_ref.at[my], send_sem=send_sems.at[6 + r],
                                              recv_sem=recv_sems.at[6 + r], device_id=peer, device_id_type=MESH)
            cp.start()
            scatter.append(cp)
        mod_ref[my] = part_ref[my]
        for cp in scatter:
            cp.wait()
        mod_ref[...] = mod_ref[...] + b_ref[...]

    vm = pl.BlockSpec(memory_space=pltpu.VMEM)
    return pl.pallas_call(
        body, name="ada_fwd",
        out_shape=[jax.ShapeDtypeStruct((N_DEV, 1, cs), F32), jax.ShapeDtypeStruct((N_DEV, 1, d), F32),
                   jax.ShapeDtypeStruct((N_DEV,) + conv_w.shape, F32)],
        in_specs=[vm, vm, vm, vm], out_specs=[vm, vm, vm],
        scratch_shapes=[pltpu.VMEM((N_DEV, 1, cs), F32), pltpu.SemaphoreType.DMA((21,)), pltpu.SemaphoreType.DMA((21,))],
        compiler_params=pltpu.CompilerParams(vmem_limit_bytes=VMEM_LIMIT_BYTES))(c, w_ada, b_ada3, conv_w)


def ada_bwd_w(sc_all, dmod_cols):
    _, d = sc_all.shape
    cs = dmod_cols.shape[1]
    tr = _tile(d, ROW_TILE)

    def body(sc_ref, dm_ref, o_ref):
        dm = dm_ref[...].astype(BF16)
        o_ref[...] = lax.dot_general(sc_ref[...].astype(BF16), dm, (((0,), (0,)), ((), ())), preferred_element_type=F32)

    return pl.pallas_call(body, name="ada_bwd_w", grid=(d // tr,),
                          in_specs=[pl.BlockSpec((N_DEV, tr), lambda i: (0, i)), _full((N_DEV, cs))],
                          out_specs=pl.BlockSpec((None, tr, cs), lambda i: (0, i, 0)),
                          out_shape=jax.ShapeDtypeStruct((1, d, cs), F32), compiler_params=_params("parallel"))(sc_all, dmod_cols)


def _round_up(n, m):
    return (n + m - 1) // m * m


def kernel(x, c, positions, w_ada, b_ada, pre_norm1_g, w_in, gm_ln_g, gm_ln_b, gm_w_s, gm_b_s, w_branch_a, q_norm_g, w_uq, kv_norm_g, w_ukv, w_branch_b, w_out, post_norm1_g, pre_norm2_g, w_up, conv_w, conv_b, w_down, post_norm2_g, loss_target, m_w_ada, m_b_ada, m_pre_norm1_g, m_w_in, m_gm_ln_g, m_gm_ln_b, m_gm_w_s, m_gm_b_s, m_w_branch_a, m_q_norm_g, m_w_uq, m_kv_norm_g, m_w_ukv, m_w_branch_b, m_w_out, m_post_norm1_g, m_pre_norm2_g, m_w_up, m_conv_w, m_conv_b, m_w_down, m_post_norm2_g, v_w_ada, v_b_ada, v_pre_norm1_g, v_w_in, v_gm_ln_g, v_gm_ln_b, v_gm_w_s, v_gm_b_s, v_w_branch_a, v_q_norm_g, v_w_uq, v_kv_norm_g, v_w_ukv, v_w_branch_b, v_w_out, v_post_norm1_g, v_pre_norm2_g, v_w_up, v_conv_w, v_conv_b, v_w_down, v_post_norm2_g):
    weights = dict(w_ada=w_ada, b_ada=b_ada, pre_norm1_g=pre_norm1_g, w_in=w_in, gm_ln_g=gm_ln_g, gm_ln_b=gm_ln_b, gm_w_s=gm_w_s,
                   gm_b_s=gm_b_s, w_branch_a=w_branch_a, q_norm_g=q_norm_g, w_uq=w_uq, kv_norm_g=kv_norm_g, w_ukv=w_ukv,
                   w_branch_b=w_branch_b, w_out=w_out, post_norm1_g=post_norm1_g, pre_norm2_g=pre_norm2_g, w_up=w_up, conv_w=conv_w,
                   conv_b=conv_b, w_down=w_down, post_norm2_g=post_norm2_g)
    mom1 = dict(w_ada=m_w_ada, b_ada=m_b_ada, pre_norm1_g=m_pre_norm1_g, w_in=m_w_in, gm_ln_g=m_gm_ln_g, gm_ln_b=m_gm_ln_b,
                gm_w_s=m_gm_w_s, gm_b_s=m_gm_b_s, w_branch_a=m_w_branch_a, q_norm_g=m_q_norm_g, w_uq=m_w_uq, kv_norm_g=m_kv_norm_g,
                w_ukv=m_w_ukv, w_branch_b=m_w_branch_b, w_out=m_w_out, post_norm1_g=m_post_norm1_g, pre_norm2_g=m_pre_norm2_g,
                w_up=m_w_up, conv_w=m_conv_w, conv_b=m_conv_b, w_down=m_w_down, post_norm2_g=m_post_norm2_g)
    mom2 = dict(w_ada=v_w_ada, b_ada=v_b_ada, pre_norm1_g=v_pre_norm1_g, w_in=v_w_in, gm_ln_g=v_gm_ln_g, gm_ln_b=v_gm_ln_b,
                gm_w_s=v_gm_w_s, gm_b_s=v_gm_b_s, w_branch_a=v_w_branch_a, q_norm_g=v_q_norm_g, w_uq=v_w_uq, kv_norm_g=v_kv_norm_g,
                w_ukv=v_w_ukv, w_branch_b=v_w_branch_b, w_out=v_w_out, post_norm1_g=v_post_norm1_g, pre_norm2_g=v_pre_norm2_g,
                w_up=v_w_up, conv_w=v_conv_w, conv_b=v_conv_b, w_down=v_w_down, post_norm2_g=v_post_norm2_g)
    order = list(weights)
    _TOKENS.clear()

    s, d = x.shape[1], x.shape[2]
    gmw = gm_ln_g.shape[0]
    groups = gmw // CHUNK
    ql, kvl = q_norm_g.shape[0], kv_norm_g.shape[0]
    f2 = conv_b.shape[0]
    in_cols = w_in.shape[1] * N_DEV
    o_q, o_kv, o_ga, o_gb, o_kpe = 2 * gmw, 2 * gmw + ql, 2 * gmw + ql + kvl, 2 * gmw + ql + kvl + d, 2 * gmw + ql + kvl + 2 * d
    zp = _round_up(o_kpe + LANES, Z_PAD)
    src_kpe = 2 * gmw + ql + kvl
    assert src_kpe + QK_ROPE + 2 * d == in_cols
    my = 4 * lax.axis_index("x") + 2 * lax.axis_index("y") + lax.axis_index("c")

    x2, tgt = x[0], loss_target[0]
    row = lambda a: a.reshape(1, -1)

    big = ["w_in", "w_branch_a", "w_uq", "w_ukv", "w_branch_b", "w_out", "w_up", "w_down"]
    sh = {k: weights[k].astype(BF16) for k in big[1:]}
    mix = ["w_branch_a", "w_uq", "w_ukv", "w_branch_b", "w_out"]
    ag_in = ag_start("ag_start_in", [w_in.T.astype(BF16)], c)

    mod8, sc_all3, g_cw = ada_fwd(c, w_ada, b_ada.reshape(N_DEV, 1, -1), conv_w)
    mod = mod8.reshape(N_MOD, d)
    shift1, scale1, gate1, shift2, scale2, gate2 = (mod[i:i + 1] for i in range(N_MOD))
    sc_all = sc_all3.reshape(N_DEV, d)
    h1 = norm_mod_fwd("pre1_fwd", x2, row(pre_norm1_g), scale1, shift1)

    inv = ROPE_THETA ** (-jnp.arange(0, QK_ROPE, 2, dtype=F32) / QK_ROPE)
    ang = positions[0].astype(F32)[:, None] * inv
    cos4 = jnp.tile(jnp.cos(ang), (1, 4))
    sin4 = jnp.tile(jnp.concatenate([-jnp.sin(ang), jnp.sin(ang)], axis=1), (1, 2))

    wm = (gm_w_s * jnp.tril(jnp.ones((CHUNK, CHUNK), F32))).astype(BF16)
    bs3 = gm_b_s.reshape(groups, CHUNK, 1)
    ln_g, ln_b = row(gm_ln_g), row(gm_ln_b)

    early = [h1, cos4, sin4, wm] + [sh[k] for k in big[1:]]
    ag_in = ag_forward("ag_forward_in", ag_in, early)
    ag_mix = ag_start("ag_start_mix", [sh[k] for k in mix], _TOKENS.pending[-1])
    (g_in,) = ag_wait("ag_wait_in", ag_in, [h1, _TOKENS.pending[-1]])
    w_in_f = g_in.reshape(in_cols, d)
    w_in_p = jnp.concatenate([w_in_f[:src_kpe], w_in_f[src_kpe + QK_ROPE:], w_in_f[src_kpe:src_kpe + QK_ROPE],
                              jnp.zeros((zp - in_cols, d), BF16)], axis=0)

    z = mm_nt("z_proj", h1, w_in_p, F32)
    ag_mix = ag_forward("ag_forward_mix", ag_mix, z)
    ag_up = ag_start("ag_start_up", [sh["w_up"]], _TOKENS.pending[-1])
    a = gmlp_fwd(z, gmw, ln_g, ln_b, wm, bs3)
    g_a, g_uq, g_ukv, g_b, g_out = ag_wait("ag_wait_mix", ag_mix, [a, _TOKENS.pending[-1]])
    w_a_f, w_b_f, w_out_f = g_a.reshape(-1, d), g_b.reshape(-1, d), g_out.reshape(-1, d)
    w_uq_f = g_uq.transpose(1, 0, 2).reshape(ql, HEADS, QK_NOPE + QK_ROPE)
    w_uq_n = w_uq_f[:, :, :QK_NOPE].reshape(ql, HEADS * QK_NOPE)
    w_uq_r = w_uq_f[:, :, QK_NOPE:].reshape(ql, HEADS * QK_ROPE)
    y_a = mm_nn("branch_a", a, w_a_f, F32)
    qln = rms_fwd_cols("q_norm", z, o_q, ql, row(q_norm_g))
    kvn = rms_fwd_cols("kv_norm", z, o_kv, kvl, row(kv_norm_g))
    qn = mm_nn("q_nope", qln, w_uq_n, BF16)
    qp = mm_nn("q_rope", qln, w_uq_r, F32)
    kv = mm_nn_b3("kv_up", kvn, g_ukv, BF16)
    kpr = rope_k(z, o_kpe, cos4, sin4)
    o, qpr, lse = attn_fwd3(qn, qp, kv, kpr, cos4, sin4)
    ag_up = ag_forward("ag_forward_up", ag_up, o)
    ag_down = ag_start("ag_start_down", [sh["w_down"]], _TOKENS.pending[-1])
    y_b = mm_nn("branch_b", o, w_b_f, F32)
    merged = merge_fwd(z, o_ga, o_gb, y_a, y_b)
    y1 = mm_nn("out_proj", merged, w_out_f, F32)
    x1 = post_res_fwd("post1_fwd", x2, y1, gate1, row(post_norm1_g))
    h2 = norm_mod_fwd("pre2_fwd", x1, row(pre_norm2_g), scale2, shift2)
    (g_up,) = ag_wait("ag_wait_up", ag_up, h2)
    upre = mm_nn_b3("up_proj", h2, g_up, F32)
    ag_down = ag_forward("ag_forward_down", ag_down, upre)
    cw = g_cw.transpose(1, 0, 2).reshape(3, f2)
    cb = row(conv_b)
    f = conv_act_fwd(upre, cw, cb)
    w_down_f = ag_wait("ag_wait_down", ag_down, f)[0].reshape(-1, d)
    ffn = mm_nn("down_proj", f, w_down_f, F32)
    loss_acc, dout, dffn, acc2 = post2_loss_bwd(x1, ffn, tgt, gate2, row(post_norm2_g))
    loss = lax.psum(loss_acc[0, 0], ("x", "y", "c"))
    _TOKENS.push(jnp.broadcast_to(loss, (8, LANES)))

    blocks = lambda g: g.reshape(N_DEV, g.shape[0] // N_DEV, g.shape[1])
    core = lax.axis_index("c").astype(jnp.int32).reshape(1)
    rs = {}

    def rs_begin(key, grads):
        rs[key] = rs_d2d_start("rs_d2d_start_" + key, grads)

    def rs_middle(key, after):
        grads, lands = rs_d2d_wait("rs_d2d_wait_" + key, rs[key], after)
        sums = [pair_sum("pair_sum_%s_%d" % (key, i), g, l, core) for i, (g, l) in enumerate(zip(grads, lands))]
        rs[key] = rs_ici_start("rs_ici_start_" + key, sums)

    gw_down = mm_tn("g_w_down", f, dffn, BF16)
    rs_begin("down", [blocks(gw_down)])
    df = mm_nt("d_f", dffn, w_down_f, F32)
    rs_middle("down", df)
    dupre, gcw_g, gcw_v, gcb_g, gcb_v = conv_act_bwd(upre, cw, cb, df)
    gw_up3 = mm_tn_h3("g_w_up", h2, dupre, N_DEV, BF16)
    rs_begin("up", [gw_up3])
    dh2 = mm_nt_h3("d_h2", dupre, g_up, F32)
    rs_middle("up", dh2)
    dx1, dy1, acc_mid = mid_bwd(dh2, dout, x1, y1, row(pre_norm2_g), scale2, gate1, row(post_norm1_g))
    gw_out = mm_tn("g_w_out", merged, dy1, BF16)
    dmerged = mm_nt("d_merged", dy1, w_out_f, F32)
    dya, dyb, dga, dgb = merge_bwd(z, o_ga, o_gb, y_a, y_b, dmerged)
    gw_a = mm_tn("g_w_a", a, dya, BF16)
    gw_b = mm_tn("g_w_b", o, dyb, BF16)
    rs_begin("mid", [blocks(gw_out), blocks(gw_a), blocks(gw_b)])
    da = mm_nt("d_a", dya, w_a_f, F32)
    do = mm_nt("d_o", dyb, w_b_f, F32)
    rs_middle("mid", do)
    duv, g_ws, g_bs3, acc_gm = gmlp_bwd(z, gmw, da, ln_g, ln_b, wm, bs3)
    dqn, dqp, dkv, dkp = attn_bwd3(qn, qpr, kv, kpr, o, do, lse, cos4, sin4)
    dkpe = kpe_bwd(dkp, cos4, sin4, zp - o_kpe)
    dq_cat = jnp.concatenate([dqn, dqp], axis=1)
    w_uq_cat = jnp.concatenate([w_uq_n, w_uq_r], axis=1)
    gw_uq_cat = mm_tn("g_w_uq", qln, dq_cat, BF16)
    gw_uq_f = jnp.concatenate([gw_uq_cat[:, :HEADS * QK_NOPE].reshape(ql, HEADS, QK_NOPE),
                               gw_uq_cat[:, HEADS * QK_NOPE:].reshape(ql, HEADS, QK_ROPE)], axis=2)
    gw_uq3 = gw_uq_f.reshape(ql, N_DEV, -1).transpose(1, 0, 2)
    gw_ukv3 = mm_tn_o3("g_w_ukv", kvn, dkv, N_DEV, BF16)
    rs_begin("mla", [gw_uq3, gw_ukv3])
    dqln = mm_nt("d_qln", dq_cat, w_uq_cat, F32)
    dq_lat, g_qnorm = rms_bwd_cols("q_norm_bwd", dqln, z, o_q, ql, row(q_norm_g))
    dkvn = mm_nt_b3("d_kvn", dkv, g_ukv, F32)
    rs_middle("mla", dkvn)
    dkv_lat, g_kvnorm = rms_bwd_cols("kv_norm_bwd", dkvn, z, o_kv, kvl, row(kv_norm_g))
    dz = jnp.concatenate([duv, dq_lat, dkv_lat, dga, dgb, dkpe], axis=1)
    gw_in_p = mm_tn("g_w_in", dz, h1, BF16)
    gw_in_f = jnp.concatenate([gw_in_p[:src_kpe], gw_in_p[o_kpe:o_kpe + QK_ROPE], gw_in_p[src_kpe:o_kpe]], axis=0)
    rs_begin("in", [gw_in_f.reshape(N_DEV, -1, d)])
    dh1 = mm_nn("d_h1", dz, w_in_p, F32)
    grad_x, acc1 = pre1_bwd(dh1, dx1, x2, row(pre_norm1_g), scale1)

    dmod = jnp.concatenate([acc1[0], acc1[1], acc_mid[3], acc_mid[0], acc_mid[1], acc2[0]])
    small = [("pre_norm1_g", acc1[2]), ("gm_ln_g", acc_gm[0]), ("gm_ln_b", acc_gm[1]), ("gm_b_s", g_bs3.reshape(-1)),
             ("q_norm_g", g_qnorm[0]), ("kv_norm_g", g_kvnorm[0]), ("post_norm1_g", acc_mid[4]), ("pre_norm2_g", acc_mid[2]),
             ("conv_b", jnp.concatenate([gcb_g[0], gcb_v[0]])), ("post_norm2_g", acc2[1]), ("gm_w_s", g_ws.reshape(-1)),
             ("b_ada", dmod)]
    n_small = sum(v.shape[0] for _, v in small)
    n_cw = 3 * f2
    n_pack = _round_up(n_small + n_cw, PACK_ALIGN)
    tail = jnp.zeros((n_pack - n_small - n_cw,), F32)
    packed = jnp.concatenate([v for _, v in small] + [jnp.concatenate([gcw_g, gcw_v], axis=1).reshape(-1), tail])
    ag_small = ag_start("ag_start_small", [packed.reshape(-1, LANES)], packed)
    rs_middle("in", [packed, _TOKENS.pending[-1]])

    res = {}
    last = packed
    for key, names in (("down", ["w_down"]), ("up", ["w_up"]), ("mid", ["w_out", "w_branch_a", "w_branch_b"]), ("mla", ["w_uq", "w_ukv"])):
        parts = rs_ici_wait("rs_ici_wait_" + key, rs[key], last)
        for k, p in zip(names, parts):
            res[k] = adamw("adamw_" + k, weights[k], mom1[k], mom2[k], p)
            last = res[k][0]

    def pack(src):
        return jnp.concatenate([src[k].reshape(-1) for k, _ in small] + [jnp.zeros((n_pack - n_small,), F32)]).reshape(-1, LANES)

    (gathered,) = ag_wait("ag_wait_small", ag_forward("ag_forward_small", ag_small, last), last)
    sm = [t.reshape(-1) for t in adamw("adamw_small", pack(weights), pack(mom1), pack(mom2), gathered)]
    off = 0
    for k, v in small:
        res[k] = tuple(t[off:off + v.shape[0]].reshape(weights[k].shape) for t in sm)
        off += v.shape[0]

    cs_cw = conv_w.shape[1]
    g_cw_full = sm[0][n_small:n_small + n_cw].reshape(3, f2)
    g_cw_mine = lax.dynamic_slice(g_cw_full, (0, my * cs_cw), (3, cs_cw))
    res["conv_w"] = adamw("adamw_conv_w", conv_w, mom1["conv_w"], mom2["conv_w"], g_cw_mine[None])

    cs_ada = w_ada.shape[1]
    off_b = n_small - N_MOD * d
    dmod_all = gathered.reshape(N_DEV, -1)[:, off_b:off_b + N_MOD * d]
    dmod_cols = lax.dynamic_slice(dmod_all, (0, my * cs_ada), (N_DEV, cs_ada))
    res["w_ada"] = adamw("adamw_w_ada", w_ada, mom1["w_ada"], mom2["w_ada"], ada_bwd_w(sc_all, dmod_cols))

    (p_in,) = rs_ici_wait("rs_ici_wait_in", rs["in"], res["w_ada"][0])
    res["w_in"] = tuple(t.T for t in adamw("adamw_w_in", w_in.T, mom1["w_in"].T, mom2["w_in"].T, p_in))

    _TOKENS.clear()
    outs = [loss, grad_x[None]]
    for i in range(4):
        outs += [res[k][i] for k in order]
    return tuple(outs)
```

```python
import functools

import jax
import jax.numpy as jnp
from jax import lax
from jax.experimental import pallas as pl
from jax.experimental.pallas import tpu as pltpu

F32 = jnp.float32
BF16 = jnp.bfloat16
ACT = BF16

N_DEV = 8
HEADS = 16
QK_NOPE = 128
QK_ROPE = 64
V_HEAD = 128
CHUNK = 128
ROPE_THETA = 10000.0
EPS = 1e-6
N_MOD = 6
ADAM_LR, ADAM_B1, ADAM_B2, ADAM_EPS, ADAM_WD, ADAM_STEP = 0.001, 0.9, 0.999, 1e-08, 0.01, 10

LANES = 128
VMEM_LIMIT_BYTES = 48 * 2 ** 20
ROW_TILE = 256
COL_TILE = 256
ATT_TILE = 256
Z_PAD = 512
ADAMW_TILE_ELEMS = 1 << 18
PACK_ALIGN = 8 * LANES
MESH = pl.DeviceIdType.MESH


def _params(*sem):
    return pltpu.CompilerParams(dimension_semantics=sem if sem else None, vmem_limit_bytes=VMEM_LIMIT_BYTES)


def _tile(dim, target):
    t = (min(dim, target) // LANES) * LANES
    while t >= LANES:
        if dim % t == 0:
            return t
        t -= LANES
    return dim


def _full(shape):
    nd = len(shape)
    return pl.BlockSpec(shape, lambda *_: (0,) * nd)


class _Tokens:
    KEEP = 2

    def __init__(self):
        self.pending = []

    def push(self, token):
        self.pending = (self.pending + [token])[-self.KEEP:]

    def take(self):
        return list(self.pending)

    def clear(self):
        self.pending = []


_TOKENS = _Tokens()


def _matmul(name, a, b, *, grid, a_spec, b_spec, o_spec, out_shape, contract, acc_shape, split=1):
    nk = grid[2]
    deps = _TOKENS.take()

    def product(a_ref, b_ref):
        if len(b_ref.shape) == 2:
            return lax.dot_general(a_ref[...].astype(BF16), b_ref[...].astype(BF16), (contract, ((), ())), preferred_element_type=F32)
        cs = b_ref.shape[2]
        return sum(lax.dot_general(a_ref[:, s * cs:(s + 1) * cs].astype(BF16), b_ref[s].astype(BF16), (contract, ((), ())),
                                   preferred_element_type=F32) for s in range(split))

    def body_one_step(a_ref, b_ref, *rest):
        o_ref = rest[len(deps)]
        o_ref[...] = product(a_ref, b_ref).astype(o_ref.dtype)

    def body(a_ref, b_ref, *rest):
        o_ref, acc_ref = rest[len(deps):]
        k = pl.program_id(2)

        @pl.when(k == 0)
        def _():
            acc_ref[...] = jnp.zeros_like(acc_ref)

        acc_ref[...] += product(a_ref, b_ref)

        @pl.when(k == nk - 1)
        def _():
            o_ref[...] = acc_ref[...].astype(o_ref.dtype)

    return pl.pallas_call(
        body_one_step if nk == 1 else body, name=name, grid=grid,
        in_specs=[a_spec, b_spec] + [pl.BlockSpec(memory_space=pl.ANY)] * len(deps),
        out_specs=o_spec, out_shape=out_shape, scratch_shapes=[] if nk == 1 else [pltpu.VMEM(acc_shape, F32)],
        compiler_params=_params("parallel", "parallel", "arbitrary"))(a, b, *deps)


TM, TN, TK = 1024, 1024, 2304


def _tk(a, b):
    return TK if a.dtype == BF16 and b.dtype == BF16 else TK // 2


def mm_nn(name, a, b, dtype):
    (m, k), n = a.shape, b.shape[1]
    tm, tn, tk = _tile(m, TM), _tile(n, TN), _tile(k, _tk(a, b))
    return _matmul(name, a, b, grid=(m // tm, n // tn, k // tk),
                   a_spec=pl.BlockSpec((tm, tk), lambda i, j, kk: (i, kk)),
                   b_spec=pl.BlockSpec((tk, tn), lambda i, j, kk: (kk, j)),
                   o_spec=pl.BlockSpec((tm, tn), lambda i, j, kk: (i, j)),
                   out_shape=jax.ShapeDtypeStruct((m, n), dtype), contract=((1,), (0,)), acc_shape=(tm, tn))


def mm_nn_b3(name, a, b3, dtype):
    (m, k), (nj, _, cs) = a.shape, b3.shape
    tm, tk = _tile(m, TM), _tile(k, _tk(a, b3))
    return _matmul(name, a, b3, grid=(m // tm, nj, k // tk),
                   a_spec=pl.BlockSpec((tm, tk), lambda i, j, kk: (i, kk)),
                   b_spec=pl.BlockSpec((None, tk, cs), lambda i, j, kk: (j, kk, 0)),
                   o_spec=pl.BlockSpec((tm, cs), lambda i, j, kk: (i, j)),
                   out_shape=jax.ShapeDtypeStruct((m, nj * cs), dtype), contract=((1,), (0,)), acc_shape=(tm, cs))


def mm_nt(name, a, b, dtype):
    (m, k), n = a.shape, b.shape[0]
    tm, tn, tk = _tile(m, TM), _tile(n, TN), _tile(k, _tk(a, b))
    return _matmul(name, a, b, grid=(m // tm, n // tn, k // tk),
                   a_spec=pl.BlockSpec((tm, tk), lambda i, j, kk: (i, kk)),
                   b_spec=pl.BlockSpec((tn, tk), lambda i, j, kk: (j, kk)),
                   o_spec=pl.BlockSpec((tm, tn), lambda i, j, kk: (i, j)),
                   out_shape=jax.ShapeDtypeStruct((m, n), dtype), contract=((1,), (1,)), acc_shape=(tm, tn))


def mm_nt_b3(name, a, b3, dtype):
    m, (nj, n, cs) = a.shape[0], b3.shape
    tm, tn = _tile(m, TM), _tile(n, TN)
    return _matmul(name, a, b3, grid=(m // tm, n // tn, nj),
                   a_spec=pl.BlockSpec((tm, cs), lambda i, j, kk: (i, kk)),
                   b_spec=pl.BlockSpec((None, tn, cs), lambda i, j, kk: (kk, j, 0)),
                   o_spec=pl.BlockSpec((tm, tn), lambda i, j, kk: (i, j)),
                   out_shape=jax.ShapeDtypeStruct((m, n), dtype), contract=((1,), (1,)), acc_shape=(tm, tn))


def mm_nt_h3(name, a3, b3, dtype):
    (_, m, _), (nj, n, cs) = a3.shape, b3.shape
    tm, tn, hj = _tile(m, TM), _tile(n, TN), nj // 2
    pair = 2 if hj % 2 == 0 else 1
    return _matmul(name, a3, b3.reshape(nj // pair, pair, n, cs), grid=(m // tm, n // tn, nj // pair),
                   a_spec=pl.BlockSpec((None, tm, pair * cs), lambda i, j, kk: (kk // (hj // pair), i, kk % (hj // pair))),
                   b_spec=pl.BlockSpec((None, pair, tn, cs), lambda i, j, kk: (kk, 0, j, 0)),
                   o_spec=pl.BlockSpec((tm, tn), lambda i, j, kk: (i, j)),
                   out_shape=jax.ShapeDtypeStruct((m, n), dtype), contract=((1,), (1,)), acc_shape=(tm, tn), split=pair)


def mm_tn_h3(name, a, b3, nj, dtype):
    (k, m), half = a.shape, b3.shape[2]
    hj = nj // 2
    cs = half // hj
    tm, tk = _tile(m, TM), _tile(k, _tk(a, b3))
    return _matmul(name, a, b3, grid=(m // tm, nj, k // tk),
                   a_spec=pl.BlockSpec((tk, tm), lambda i, j, kk: (kk, i)),
                   b_spec=pl.BlockSpec((None, tk, cs), lambda i, j, kk: (j // hj, kk, j % hj)),
                   o_spec=pl.BlockSpec((None, tm, cs), lambda i, j, kk: (j, i, 0)),
                   out_shape=jax.ShapeDtypeStruct((nj, m, cs), dtype), contract=((0,), (0,)), acc_shape=(tm, cs))


def mm_tn(name, a, b, dtype):
    (k, m), n = a.shape, b.shape[1]
    tm, tn, tk = _tile(m, TM), _tile(n, TN), _tile(k, _tk(a, b))
    return _matmul(name, a, b, grid=(m // tm, n // tn, k // tk),
                   a_spec=pl.BlockSpec((tk, tm), lambda i, j, kk: (kk, i)),
                   b_spec=pl.BlockSpec((tk, tn), lambda i, j, kk: (kk, j)),
                   o_spec=pl.BlockSpec((tm, tn), lambda i, j, kk: (i, j)),
                   out_shape=jax.ShapeDtypeStruct((m, n), dtype), contract=((0,), (0,)), acc_shape=(tm, tn))


def mm_tn_o3(name, a, b, nj, dtype):
    (k, m), n = a.shape, b.shape[1]
    cs = n // nj
    tm, tk = _tile(m, TM), _tile(k, _tk(a, b))
    return _matmul(name, a, b, grid=(m // tm, nj, k // tk),
                   a_spec=pl.BlockSpec((tk, tm), lambda i, j, kk: (kk, i)),
                   b_spec=pl.BlockSpec((tk, cs), lambda i, j, kk: (kk, j)),
                   o_spec=pl.BlockSpec((None, tm, cs), lambda i, j, kk: (j, i, 0)),
                   out_shape=jax.ShapeDtypeStruct((nj, m, cs), dtype), contract=((0,), (0,)), acc_shape=(tm, cs))


_GELU_C = 0.7978845608028654
_GELU_A = 0.044715


def _f32(ref):
    return ref[...].astype(F32)


def _gelu(x):
    x = x.astype(F32)
    return 0.5 * x * (1.0 + jnp.tanh(_GELU_C * (x + _GELU_A * x * x * x)))


def _gelu_and_grad(x):
    x = x.astype(F32)
    t = jnp.tanh(_GELU_C * (x + _GELU_A * x * x * x))
    y = 0.5 * x * (1.0 + t)
    dy = 0.5 * (1.0 + t) + 0.5 * x * (1.0 - t * t) * (_GELU_C * (1.0 + 3.0 * _GELU_A * x * x))
    return y, dy


def _sigmoid(x):
    return 1.0 / (1.0 + jnp.exp(-x.astype(F32)))


def _rms_stats(x):
    x = x.astype(F32)
    inv = lax.rsqrt(jnp.mean(x * x, axis=-1, keepdims=True) + EPS)
    return inv, x * inv


def _rms_bwd(dyhat, yhat, inv):
    return inv * (dyhat - yhat * jnp.mean(dyhat * yhat, axis=-1, keepdims=True))


def _colsum(x):
    return jnp.sum(x, axis=0, keepdims=True)


def _rope(x, cos4, sin4):
    lane = lax.broadcasted_iota(jnp.int32, x.shape, x.ndim - 1)
    first_half = (lane % QK_ROPE) < (QK_ROPE // 2)
    partner = jnp.where(first_half, pltpu.roll(x, LANES - QK_ROPE // 2, x.ndim - 1), pltpu.roll(x, QK_ROPE // 2, x.ndim - 1))
    return x * cos4 + partner * sin4


def norm_mod_fwd(name, x, g, scale, shift):
    s, d = x.shape
    tr = _tile(s, ROW_TILE)

    def body(x_ref, g_ref, sc_ref, sh_ref, o_ref):
        _, xh = _rms_stats(x_ref[...])
        o_ref[...] = (xh * g_ref[...] * (1.0 + sc_ref[...]) + sh_ref[...]).astype(o_ref.dtype)

    row = pl.BlockSpec((tr, d), lambda i: (i, 0))
    vec = pl.BlockSpec((1, d), lambda i: (0, 0))
    return pl.pallas_call(body, name=name, grid=(s // tr,), in_specs=[row, vec, vec, vec], out_specs=row,
                          out_shape=jax.ShapeDtypeStruct((s, d), BF16), compiler_params=_params("parallel"))(x, g, scale, shift)


def rms_fwd_cols(name, z, off, width, g):
    s = z.shape[0]
    tr = _tile(s, ROW_TILE)
    assert off % width == 0

    def body(x_ref, g_ref, o_ref):
        _, xh = _rms_stats(x_ref[...])
        o_ref[...] = (xh * g_ref[...]).astype(o_ref.dtype)

    return pl.pallas_call(body, name=name, grid=(s // tr,),
                          in_specs=[pl.BlockSpec((tr, width), lambda i: (i, off // width)), pl.BlockSpec((1, width), lambda i: (0, 0))],
                          out_specs=pl.BlockSpec((tr, width), lambda i: (i, 0)),
                          out_shape=jax.ShapeDtypeStruct((s, width), BF16), compiler_params=_params("parallel"))(z, g)


def rms_bwd_cols(name, dy, z, off, width, g):
    s = z.shape[0]
    tr = _tile(s, ROW_TILE)

    def body(dy_ref, x_ref, g_ref, dx_ref, gg_ref):
        @pl.when(pl.program_id(0) == 0)
        def _():
            gg_ref[...] = jnp.zeros_like(gg_ref)

        inv, xh = _rms_stats(x_ref[...])
        dy_v = _f32(dy_ref)
        gg_ref[...] += _colsum(dy_v * xh)
        dx_ref[...] = _rms_bwd(dy_v * g_ref[...], xh, inv).astype(dx_ref.dtype)

    return pl.pallas_call(body, name=name, grid=(s // tr,),
                          in_specs=[pl.BlockSpec((tr, width), lambda i: (i, 0)), pl.BlockSpec((tr, width), lambda i: (i, off // width)),
                                    pl.BlockSpec((1, width), lambda i: (0, 0))],
                          out_specs=[pl.BlockSpec((tr, width), lambda i: (i, 0)), pl.BlockSpec((1, width), lambda i: (0, 0))],
                          out_shape=[jax.ShapeDtypeStruct((s, width), BF16), jax.ShapeDtypeStruct((1, width), F32)],
                          compiler_params=_params("arbitrary"))(dy, z, g)


def post_res_fwd(name, x, y, gate, g):
    s, d = x.shape
    tr = _tile(s, ROW_TILE)

    def body(x_ref, y_ref, gate_ref, g_ref, o_ref):
        _, yh = _rms_stats(y_ref[...])
        o_ref[...] = x_ref[...] + gate_ref[...] * (yh * g_ref[...])

    row = pl.BlockSpec((tr, d), lambda i: (i, 0))
    vec = pl.BlockSpec((1, d), lambda i: (0, 0))
    return pl.pallas_call(body, name=name, grid=(s // tr,), in_specs=[row, row, vec, vec], out_specs=row,
                          out_shape=jax.ShapeDtypeStruct((s, d), F32), compiler_params=_params("parallel"))(x, y, gate, g)


def post2_loss_bwd(x1, ffn, target, gate2, g):
    s, d = x1.shape
    tr = _tile(s, ROW_TILE)

    def body(x_ref, y_ref, t_ref, gate_ref, g_ref, loss_ref, dout_ref, dy_ref, acc_ref):
        @pl.when(pl.program_id(0) == 0)
        def _():
            loss_ref[...] = jnp.zeros_like(loss_ref)
            acc_ref[...] = jnp.zeros_like(acc_ref)

        inv, yh = _rms_stats(y_ref[...])
        r = yh * g_ref[...]
        err = x_ref[...] + gate_ref[...] * r - t_ref[...]
        loss_ref[...] += 0.5 * jnp.sum(jnp.mean(err * err, axis=-1, keepdims=True))
        dout = err / d
        dout_ref[...] = dout
        dr = dout * gate_ref[...]
        acc_ref[0:1, :] += _colsum(dout * r)
        acc_ref[1:2, :] += _colsum(dr * yh)
        dy_ref[...] = _rms_bwd(dr * g_ref[...], yh, inv).astype(dy_ref.dtype)

    row = pl.BlockSpec((tr, d), lambda i: (i, 0))
    vec = pl.BlockSpec((1, d), lambda i: (0, 0))
    return pl.pallas_call(
        body, name="post2_loss_bwd", grid=(s // tr,), in_specs=[row, row, row, vec, vec],
        out_specs=[_full((8, LANES)), row, row, _full((8, d))],
        out_shape=[jax.ShapeDtypeStruct((8, LANES), F32), jax.ShapeDtypeStruct((s, d), F32),
                   jax.ShapeDtypeStruct((s, d), BF16), jax.ShapeDtypeStruct((8, d), F32)],
        compiler_params=_params("arbitrary"))(x1, ffn, target, gate2, g)


def mid_bwd(dh2, dout, x1, y1, pre2_g, scale2, gate1, post1_g):
    s, d = x1.shape
    tr = _tile(s, ROW_TILE)

    def body(dh_ref, dout_ref, x_ref, y_ref, g2_ref, sc_ref, gate_ref, g1_ref, dx_ref, dy_ref, acc_ref):
        @pl.when(pl.program_id(0) == 0)
        def _():
            acc_ref[...] = jnp.zeros_like(acc_ref)

        dh = _f32(dh_ref)
        inv2, xh = _rms_stats(x_ref[...])
        acc_ref[0:1, :] += _colsum(dh)
        acc_ref[1:2, :] += _colsum(dh * (xh * g2_ref[...]))
        t = dh * (1.0 + sc_ref[...])
        acc_ref[2:3, :] += _colsum(t * xh)
        dx1 = dout_ref[...] + _rms_bwd(t * g2_ref[...], xh, inv2)
        dx_ref[...] = dx1
        inv1, yh = _rms_stats(y_ref[...])
        acc_ref[3:4, :] += _colsum(dx1 * (yh * g1_ref[...]))
        dr = dx1 * gate_ref[...]
        acc_ref[4:5, :] += _colsum(dr * yh)
        dy_ref[...] = _rms_bwd(dr * g1_ref[...], yh, inv1).astype(dy_ref.dtype)

    row = pl.BlockSpec((tr, d), lambda i: (i, 0))
    vec = pl.BlockSpec((1, d), lambda i: (0, 0))
    return pl.pallas_call(
        body, name="mid_bwd", grid=(s // tr,), in_specs=[row, row, row, row, vec, vec, vec, vec],
        out_specs=[row, row, _full((8, d))],
        out_shape=[jax.ShapeDtypeStruct((s, d), F32), jax.ShapeDtypeStruct((s, d), BF16), jax.ShapeDtypeStruct((8, d), F32)],
        compiler_params=_params("arbitrary"))(dh2, dout, x1, y1, pre2_g, scale2, gate1, post1_g)


def pre1_bwd(dh1, dx1, x, pre1_g, scale1):
    s, d = x.shape
    tr = _tile(s, ROW_TILE)

    def body(dh_ref, dx1_ref, x_ref, g_ref, sc_ref, dx_ref, acc_ref):
        @pl.when(pl.program_id(0) == 0)
        def _():
            acc_ref[...] = jnp.zeros_like(acc_ref)

        dh = _f32(dh_ref)
        inv, xh = _rms_stats(x_ref[...])
        acc_ref[0:1, :] += _colsum(dh)
        acc_ref[1:2, :] += _colsum(dh * (xh * g_ref[...]))
        t = dh * (1.0 + sc_ref[...])
        acc_ref[2:3, :] += _colsum(t * xh)
        dx_ref[...] = dx1_ref[...] + _rms_bwd(t * g_ref[...], xh, inv)

    row = pl.BlockSpec((tr, d), lambda i: (i, 0))
    vec = pl.BlockSpec((1, d), lambda i: (0, 0))
    return pl.pallas_call(
        body, name="pre1_bwd", grid=(s // tr,), in_specs=[row, row, row, vec, vec], out_specs=[row, _full((8, d))],
        out_shape=[jax.ShapeDtypeStruct((s, d), F32), jax.ShapeDtypeStruct((8, d), F32)],
        compiler_params=_params("arbitrary"))(dh1, dx1, x, pre1_g, scale1)


def _ln_stats(v):
    mu = jnp.mean(v, axis=-1, keepdims=True)
    vc = v - mu
    rstd = lax.rsqrt(jnp.mean(vc * vc, axis=-1, keepdims=True) + EPS)
    return rstd, vc * rstd


def gmlp_fwd(z, width, ln_g, ln_b, wm, bs3):
    s = z.shape[0]
    groups = width // CHUNK

    def body(u_ref, v_ref, g_ref, b_ref, wm_ref, bs_ref, a_ref):
        ug = _gelu(u_ref[...])
        _, vh = _ln_stats(_gelu(v_ref[...]))
        vn = (vh * g_ref[...] + b_ref[...]).astype(BF16)
        for g in range(groups):
            cols = slice(g * CHUNK, (g + 1) * CHUNK)
            mixed = jnp.dot(wm_ref[g], vn[:, cols], preferred_element_type=F32) + bs_ref[g]
            a_ref[:, cols] = (ug[:, cols] * mixed).astype(a_ref.dtype)

    vec = pl.BlockSpec((1, width), lambda n: (0, 0))
    return pl.pallas_call(
        body, name="gmlp_fwd", grid=(s // CHUNK,),
        in_specs=[pl.BlockSpec((CHUNK, width), lambda n: (n, 0)), pl.BlockSpec((CHUNK, width), lambda n: (n, 1)), vec, vec,
                  _full(wm.shape), _full(bs3.shape)],
        out_specs=pl.BlockSpec((CHUNK, width), lambda n: (n, 0)),
        out_shape=jax.ShapeDtypeStruct((s, width), BF16), compiler_params=_params("parallel"))(z, z, ln_g, ln_b, wm, bs3)


def gmlp_bwd(z, width, da, ln_g, ln_b, wm, bs3):
    s = z.shape[0]
    groups = width // CHUNK

    def body(u_ref, v_ref, da_ref, g_ref, b_ref, wm_ref, bs_ref, duv_ref, gw_ref, gb_ref, acc_ref, dvn_ref):
        @pl.when(pl.program_id(0) == 0)
        def _():
            gw_ref[...] = jnp.zeros_like(gw_ref)
            gb_ref[...] = jnp.zeros_like(gb_ref)
            acc_ref[...] = jnp.zeros_like(acc_ref)

        ug, dug = _gelu_and_grad(u_ref[...])
        vg, dvg = _gelu_and_grad(v_ref[...])
        rstd, vh = _ln_stats(vg)
        vn = (vh * g_ref[...] + b_ref[...]).astype(BF16)
        da_v = _f32(da_ref)
        for g in range(groups):
            cols = slice(g * CHUNK, (g + 1) * CHUNK)
            mixed = jnp.dot(wm_ref[g], vn[:, cols], preferred_element_type=F32) + bs_ref[g]
            duv_ref[:, cols] = (da_v[:, cols] * mixed * dug[:, cols]).astype(duv_ref.dtype)
            dm = da_v[:, cols] * ug[:, cols]
            gb_ref[g] += jnp.sum(dm, axis=-1, keepdims=True)
            dmb = dm.astype(BF16)
            gw_ref[g] += lax.dot_general(dmb, vn[:, cols], (((1,), (1,)), ((), ())), preferred_element_type=F32)
            dvn_ref[:, cols] = lax.dot_general(wm_ref[g], dmb, (((0,), (0,)), ((), ())), preferred_element_type=F32)
        dvn = dvn_ref[...]
        acc_ref[0:1, :] += _colsum(dvn * vh)
        acc_ref[1:2, :] += _colsum(dvn)
        dvh = dvn * g_ref[...]
        dv = rstd * (dvh - jnp.mean(dvh, axis=-1, keepdims=True) - vh * jnp.mean(dvh * vh, axis=-1, keepdims=True))
        duv_ref[:, width:] = (dv * dvg).astype(duv_ref.dtype)

        @pl.when(pl.program_id(0) == pl.num_programs(0) - 1)
        def _():
            q = lax.broadcasted_iota(jnp.int32, gw_ref.shape, 1)
            p = lax.broadcasted_iota(jnp.int32, gw_ref.shape, 2)
            gw_ref[...] = jnp.where(p <= q, gw_ref[...], 0.0)

    vec = pl.BlockSpec((1, width), lambda n: (0, 0))
    blk = pl.BlockSpec((CHUNK, width), lambda n: (n, 0))
    return pl.pallas_call(
        body, name="gmlp_bwd", grid=(s // CHUNK,),
        in_specs=[blk, pl.BlockSpec((CHUNK, width), lambda n: (n, 1)), blk, vec, vec, _full(wm.shape), _full(bs3.shape)],
        out_specs=[pl.BlockSpec((CHUNK, 2 * width), lambda n: (n, 0)), _full(wm.shape), _full(bs3.shape), _full((8, width))],
        out_shape=[jax.ShapeDtypeStruct((s, 2 * width), BF16), jax.ShapeDtypeStruct(wm.shape, F32),
                   jax.ShapeDtypeStruct(bs3.shape, F32), jax.ShapeDtypeStruct((8, width), F32)],
        scratch_shapes=[pltpu.VMEM((CHUNK, width), F32)],
        compiler_params=_params("arbitrary"))(z, z, da, ln_g, ln_b, wm, bs3)


def merge_fwd(z, off_a, off_b, ya, yb):
    s, d = ya.shape
    tr, tc = _tile(s, ROW_TILE * 2), _tile(d, COL_TILE)
    assert off_a % tc == 0 and off_b % tc == 0

    def body(ga_ref, gb_ref, ya_ref, yb_ref, o_ref):
        o_ref[...] = (_sigmoid(ga_ref[...]) * _f32(ya_ref) + _sigmoid(gb_ref[...]) * _f32(yb_ref)).astype(o_ref.dtype)

    blk = pl.BlockSpec((tr, tc), lambda i, j: (i, j))
    return pl.pallas_call(
        body, name="merge_fwd", grid=(s // tr, d // tc),
        in_specs=[pl.BlockSpec((tr, tc), lambda i, j: (i, off_a // tc + j)), pl.BlockSpec((tr, tc), lambda i, j: (i, off_b // tc + j)), blk, blk],
        out_specs=blk, out_shape=jax.ShapeDtypeStruct((s, d), BF16), compiler_params=_params("parallel", "parallel"))(z, z, ya, yb)


def merge_bwd(z, off_a, off_b, ya, yb, dm):
    s, d = ya.shape
    tr, tc = _tile(s, ROW_TILE * 2), _tile(d, COL_TILE)
    nc = d // tc

    def body(ga_ref, gb_ref, ya_ref, yb_ref, dm_ref, dya_ref, dyb_ref, dga_ref, dgb_ref):
        dm_v = _f32(dm_ref)
        sa, sb = _sigmoid(ga_ref[...]), _sigmoid(gb_ref[...])
        dya_ref[...] = (dm_v * sa).astype(dya_ref.dtype)
        dyb_ref[...] = (dm_v * sb).astype(dyb_ref.dtype)
        dga_ref[...] = (dm_v * _f32(ya_ref) * sa * (1.0 - sa)).astype(dga_ref.dtype)
        dgb_ref[...] = (dm_v * _f32(yb_ref) * sb * (1.0 - sb)).astype(dgb_ref.dtype)

    blk = pl.BlockSpec((tr, tc), lambda i, j: (i, j))
    out = jax.ShapeDtypeStruct((s, d), BF16)
    return pl.pallas_call(
        body, name="merge_bwd", grid=(s // tr, nc),
        in_specs=[pl.BlockSpec((tr, tc), lambda i, j: (i, off_a // tc + j)), pl.BlockSpec((tr, tc), lambda i, j: (i, off_b // tc + j)), blk, blk, blk],
        out_specs=[blk, blk, blk, blk], out_shape=[out, out, out, out],
        compiler_params=_params("parallel", "parallel"))(z, z, ya, yb, dm)


_ATT_SCALE = (QK_NOPE + QK_ROPE) ** -0.5
_NEG = -1e30


def rope_k(z, off, cos4, sin4):
    s = z.shape[0]
    tr = _tile(s, ROW_TILE * 2)
    assert off % LANES == 0

    def body(k_ref, c_ref, s_ref, o_ref):
        k = _f32(k_ref)
        k = k + pltpu.roll(k, QK_ROPE, 1)
        o_ref[...] = _rope(k, c_ref[...], s_ref[...]).astype(o_ref.dtype)

    row = pl.BlockSpec((tr, LANES), lambda i: (i, 0))
    return pl.pallas_call(body, name="rope_k", grid=(s // tr,),
                          in_specs=[pl.BlockSpec((tr, LANES), lambda i: (i, off // LANES)), row, row], out_specs=row,
                          out_shape=jax.ShapeDtypeStruct((s, LANES), BF16), compiler_params=_params("parallel"))(z, cos4, sin4)


def _head_masks(shape):
    lane = lax.broadcasted_iota(jnp.int32, shape, 1)
    return lane < QK_ROPE, lane >= QK_ROPE


def _scores(qn, qp_h, k, kp, qi, kb, t):
    sc = lax.dot_general(qn, k, (((1,), (1,)), ((), ())), preferred_element_type=F32)
    sc += lax.dot_general(qp_h, kp, (((1,), (1,)), ((), ())), preferred_element_type=F32)
    sc = sc * _ATT_SCALE
    row = lax.broadcasted_iota(jnp.int32, sc.shape, 0) + qi * t
    col = lax.broadcasted_iota(jnp.int32, sc.shape, 1) + kb * t
    return jnp.where(col <= row, sc, _NEG)


def attn_fwd(qn, qp, kv, kpr, cos4, sin4):
    s = qn.shape[0]
    hp = HEADS // 2
    t = _tile(s, ATT_TILE)
    nq = s // t

    def body(qn_ref, qp_ref, kv_ref, kp_ref, c_ref, s_ref, o_ref, qpr_ref, l_ref):
        qi = pl.program_id(1)
        qpr = _rope(qp_ref[...], c_ref[...], s_ref[...]).astype(BF16)
        qpr_ref[...] = qpr
        masks = _head_masks(qpr.shape)
        for hh in range(2):
            q_n = qn_ref[:, hh * QK_NOPE:(hh + 1) * QK_NOPE]
            q_p = jnp.where(masks[hh], qpr, jnp.zeros_like(qpr))
            kc, vc = 2 * hh * QK_NOPE, (2 * hh + 1) * QK_NOPE

            def step(kb, carry):
                m, l, acc = carry
                rows = pl.ds(pl.multiple_of(kb * t, t), t)
                sc = _scores(q_n, q_p, kv_ref[rows, kc:kc + QK_NOPE], kp_ref[rows, :], qi, kb, t)
                m_new = jnp.maximum(m, jnp.max(sc, axis=-1, keepdims=True))
                alpha = jnp.exp(m - m_new)
                p = jnp.exp(sc - m_new)
                l = alpha * l + jnp.sum(p, axis=-1, keepdims=True)
                acc = alpha * acc + jnp.dot(p.astype(BF16), kv_ref[rows, vc:vc + V_HEAD], preferred_element_type=F32)
                return m_new, l, acc

            init = (jnp.full((t, 1), _NEG, F32), jnp.zeros((t, 1), F32), jnp.zeros((t, V_HEAD), F32))
            m, l, acc = lax.fori_loop(0, qi + 1, step, init)
            o_ref[:, hh * V_HEAD:(hh + 1) * V_HEAD] = acc / l
            l_ref[:, hh:hh + 1] = m + jnp.log(l)

    return pl.pallas_call(
        body, name="attn_fwd", grid=(hp, nq),
        in_specs=[pl.BlockSpec((t, 2 * QK_NOPE), lambda h, i: (i, h)), pl.BlockSpec((t, LANES), lambda h, i: (i, h)),
                  pl.BlockSpec((s, 4 * QK_NOPE), lambda h, i: (0, h)), _full((s, LANES)),
                  pl.BlockSpec((t, LANES), lambda h, i: (i, 0)), pl.BlockSpec((t, LANES), lambda h, i: (i, 0))],
        out_specs=[pl.BlockSpec((t, 2 * V_HEAD), lambda h, i: (i, h)), pl.BlockSpec((t, LANES), lambda h, i: (i, h)),
                   pl.BlockSpec((None, t, 2), lambda h, i: (h, i, 0))],
        out_shape=[jax.ShapeDtypeStruct((s, HEADS * V_HEAD), F32), jax.ShapeDtypeStruct((s, HEADS * QK_ROPE), BF16),
                   jax.ShapeDtypeStruct((hp, s, 2), F32)],
        compiler_params=_params("parallel", "parallel"))(qn, qp, kv, kpr, cos4, sin4)


def attn_bwd_q(qn, qpr, kv, kpr, o, do, lse, cos4, sin4):
    s = qn.shape[0]
    hp = HEADS // 2
    t = _tile(s, ATT_TILE)
    nq = s // t

    def body(qn_ref, qpr_ref, kv_ref, kp_ref, o_ref, do_ref, l_ref, c_ref, s_ref, dqn_ref, dqp_ref):
        qi = pl.program_id(1)
        qpr = qpr_ref[...]
        masks = _head_masks(qpr.shape)
        dqp = jnp.zeros(qpr.shape, F32)
        for hh in range(2):
            q_n = qn_ref[:, hh * QK_NOPE:(hh + 1) * QK_NOPE]
            q_p = jnp.where(masks[hh], qpr, jnp.zeros_like(qpr))
            kc, vc = 2 * hh * QK_NOPE, (2 * hh + 1) * QK_NOPE
            do_h = do_ref[:, hh * V_HEAD:(hh + 1) * V_HEAD]
            delta = jnp.sum(do_h * o_ref[:, hh * V_HEAD:(hh + 1) * V_HEAD], axis=-1, keepdims=True)
            do_b = do_h.astype(BF16)
            lse_h = l_ref[:, hh:hh + 1]

            def step(kb, carry):
                dn, dp_ = carry
                rows = pl.ds(pl.multiple_of(kb * t, t), t)
                k = kv_ref[rows, kc:kc + QK_NOPE]
                kp = kp_ref[rows, :]
                p = jnp.exp(_scores(q_n, q_p, k, kp, qi, kb, t) - lse_h)
                dpv = lax.dot_general(do_b, kv_ref[rows, vc:vc + V_HEAD], (((1,), (1,)), ((), ())), preferred_element_type=F32)
                ds = (p * (dpv - delta) * _ATT_SCALE).astype(BF16)
                dn = dn + jnp.dot(ds, k, preferred_element_type=F32)
                dp_ = dp_ + jnp.dot(ds, kp, preferred_element_type=F32)
                return dn, dp_

            dn, dp_h = lax.fori_loop(0, qi + 1, step, (jnp.zeros((t, QK_NOPE), F32), jnp.zeros((t, LANES), F32)))
            dqn_ref[:, hh * QK_NOPE:(hh + 1) * QK_NOPE] = dn.astype(dqn_ref.dtype)
            dqp = dqp + jnp.where(masks[hh], dp_h, jnp.zeros_like(dp_h))
        dqp_ref[...] = _rope(dqp, c_ref[...], -s_ref[...]).astype(dqp_ref.dtype)

    qblk = pl.BlockSpec((t, 2 * QK_NOPE), lambda h, i: (i, h))
    pblk = pl.BlockSpec((t, LANES), lambda h, i: (i, h))
    tab = pl.BlockSpec((t, LANES), lambda h, i: (i, 0))
    return pl.pallas_call(
        body, name="attn_bwd_q", grid=(hp, nq),
        in_specs=[qblk, pblk, pl.BlockSpec((s, 4 * QK_NOPE), lambda h, i: (0, h)), _full((s, LANES)), qblk, qblk,
                  pl.BlockSpec((None, t, 2), lambda h, i: (h, i, 0)), tab, tab],
        out_specs=[qblk, pblk],
        out_shape=[jax.ShapeDtypeStruct((s, HEADS * QK_NOPE), BF16), jax.ShapeDtypeStruct((s, HEADS * QK_ROPE), BF16)],
        compiler_params=_params("parallel", "parallel"))(qn, qpr, kv, kpr, o, do, lse, cos4, sin4)


def attn_bwd_kv(qn, qpr, kv, kpr, o, do, lse):
    s = qn.shape[0]
    hp = HEADS // 2
    t = _tile(s, ATT_TILE)
    nq = s // t

    def body(qn_ref, qpr_ref, kv_ref, kp_ref, o_ref, do_ref, l_ref, dkv_ref, dkp_ref):
        ki = pl.program_id(1)
        rows_k = pl.ds(pl.multiple_of(ki * t, t), t)
        kp = kp_ref[rows_k, :]
        dkp = jnp.zeros((t, LANES), F32)
        for hh in range(2):
            kc, vc = 2 * hh * QK_NOPE, (2 * hh + 1) * QK_NOPE
            k = kv_ref[rows_k, kc:kc + QK_NOPE]
            v = kv_ref[rows_k, vc:vc + V_HEAD]

            def step(qb, carry):
                dk, dv, dkp_h = carry
                rows = pl.ds(pl.multiple_of(qb * t, t), t)
                q_n = qn_ref[rows, hh * QK_NOPE:(hh + 1) * QK_NOPE]
                qpr = qpr_ref[rows, :]
                lane = lax.broadcasted_iota(jnp.int32, qpr.shape, 1)
                sel = (lane < QK_ROPE) if hh == 0 else (lane >= QK_ROPE)
                q_p = jnp.where(sel, qpr, jnp.zeros_like(qpr))
                do_h = do_ref[rows, hh * V_HEAD:(hh + 1) * V_HEAD]
                delta = jnp.sum(do_h * o_ref[rows, hh * V_HEAD:(hh + 1) * V_HEAD], axis=-1, keepdims=True)
                do_b = do_h.astype(BF16)
                p = jnp.exp(_scores(q_n, q_p, k, kp, qb, ki, t) - l_ref[rows, hh:hh + 1])
                dpv = lax.dot_general(do_b, v, (((1,), (1,)), ((), ())), preferred_element_type=F32)
                ds = (p * (dpv - delta) * _ATT_SCALE).astype(BF16)
                dv = dv + lax.dot_general(p.astype(BF16), do_b, (((0,), (0,)), ((), ())), preferred_element_type=F32)
                dk = dk + lax.dot_general(ds, q_n, (((0,), (0,)), ((), ())), preferred_element_type=F32)
                dkp_h = dkp_h + lax.dot_general(ds, q_p, (((0,), (0,)), ((), ())), preferred_element_type=F32)
                return dk, dv, dkp_h

            init = (jnp.zeros((t, QK_NOPE), F32), jnp.zeros((t, V_HEAD), F32), jnp.zeros((t, LANES), F32))
            dk, dv, dkp_h = lax.fori_loop(ki, nq, step, init)
            dkv_ref[:, kc:kc + QK_NOPE] = dk.astype(dkv_ref.dtype)
            dkv_ref[:, vc:vc + V_HEAD] = dv.astype(dkv_ref.dtype)
            dkp = dkp + dkp_h
        dkp_ref[...] = dkp

    return pl.pallas_call(
        body, name="attn_bwd_kv", grid=(hp, nq),
        in_specs=[pl.BlockSpec((s, 2 * QK_NOPE), lambda h, i: (0, h)), pl.BlockSpec((s, LANES), lambda h, i: (0, h)),
                  pl.BlockSpec((s, 4 * QK_NOPE), lambda h, i: (0, h)), _full((s, LANES)),
                  pl.BlockSpec((s, 2 * V_HEAD), lambda h, i: (0, h)), pl.BlockSpec((s, 2 * V_HEAD), lambda h, i: (0, h)),
                  pl.BlockSpec((None, s, 2), lambda h, i: (h, 0, 0))],
        out_specs=[pl.BlockSpec((t, 4 * QK_NOPE), lambda h, i: (i, h)), pl.BlockSpec((None, t, LANES), lambda h, i: (h, i, 0))],
        out_shape=[jax.ShapeDtypeStruct((s, HEADS * 2 * QK_NOPE), BF16), jax.ShapeDtypeStruct((hp, s, LANES), F32)],
        compiler_params=_params("parallel", "parallel"))(qn, qpr, kv, kpr, o, do, lse)


def _dot_nt(a, b):
    return lax.dot_general(a, b, (((1,), (1,)), ((), ())), preferred_element_type=F32)


def _dot_tn(a, b):
    return lax.dot_general(a, b, (((0,), (0,)), ((), ())), preferred_element_type=F32)


def _q_cat(q_n, qpr, hh):
    lane = lax.broadcasted_iota(jnp.int32, qpr.shape, 1)
    sel = (lane < QK_ROPE) if hh == 0 else (lane >= QK_ROPE)
    return jnp.concatenate([q_n, jnp.where(sel, qpr, jnp.zeros_like(qpr))], axis=1)


def _causal(sc):
    row = lax.broadcasted_iota(jnp.int32, sc.shape, 0)
    col = lax.broadcasted_iota(jnp.int32, sc.shape, 1)
    return jnp.where(col <= row, sc, _NEG)


def attn_fwd2(qn, qp, kv, kpr, cos4, sin4):
    s = qn.shape[0]
    hp = HEADS // 2
    t = _tile(s, ATT_TILE)
    nq = s // t

    def body(qn_ref, qp_ref, kv_ref, kp_ref, c_ref, s_ref, o_ref, qpr_ref, l_ref, kcat_ref):
        qi = pl.program_id(1)

        @pl.when(qi == 0)
        def _():
            for hh in range(2):
                kcat_ref[hh, :, 0:QK_NOPE] = kv_ref[:, 2 * hh * QK_NOPE:(2 * hh + 1) * QK_NOPE]
                kcat_ref[hh, :, QK_NOPE:] = kp_ref[...]

        qpr = _rope(qp_ref[...], c_ref[...], s_ref[...]).astype(BF16)
        qpr_ref[...] = qpr
        qcat = [_q_cat(qn_ref[:, hh * QK_NOPE:(hh + 1) * QK_NOPE], qpr, hh) for hh in range(2)]

        def block(kb, carry, diagonal):
            rows = pl.ds(pl.multiple_of(kb * t, t), t)
            out = []
            for hh in range(2):
                m, l, acc = carry[hh]
                sc = _dot_nt(qcat[hh], kcat_ref[hh, rows, :]) * _ATT_SCALE
                if diagonal:
                    sc = _causal(sc)
                m_new = jnp.maximum(m, jnp.max(sc, axis=-1, keepdims=True))
                alpha = jnp.exp(m - m_new)
                p = jnp.exp(sc - m_new)
                l = alpha * l + jnp.sum(p, axis=-1, keepdims=True)
                v = kv_ref[rows, (2 * hh + 1) * QK_NOPE:(2 * hh + 2) * QK_NOPE]
                acc = alpha * acc + jnp.dot(p.astype(BF16), v, preferred_element_type=F32)
                out.append((m_new, l, acc))
            return tuple(out)

        one = (jnp.full((t, 1), _NEG, F32), jnp.zeros((t, 1), F32), jnp.zeros((t, V_HEAD), F32))
        carry = lax.fori_loop(0, qi, lambda kb, cr: block(kb, cr, False), (one, one))
        carry = block(qi, carry, True)
        for hh in range(2):
            m, l, acc = carry[hh]
            o_ref[:, hh * V_HEAD:(hh + 1) * V_HEAD] = acc / l
            l_ref[:, hh:hh + 1] = m + jnp.log(l)

    return pl.pallas_call(
        body, name="attn_fwd", grid=(hp, nq),
        in_specs=[pl.BlockSpec((t, 2 * QK_NOPE), lambda h, i: (i, h)), pl.BlockSpec((t, LANES), lambda h, i: (i, h)),
                  pl.BlockSpec((s, 4 * QK_NOPE), lambda h, i: (0, h)), _full((s, LANES)),
                  pl.BlockSpec((t, LANES), lambda h, i: (i, 0)), pl.BlockSpec((t, LANES), lambda h, i: (i, 0))],
        out_specs=[pl.BlockSpec((t, 2 * V_HEAD), lambda h, i: (i, h)), pl.BlockSpec((t, LANES), lambda h, i: (i, h)),
                   pl.BlockSpec((None, t, 2), lambda h, i: (h, i, 0))],
        out_shape=[jax.ShapeDtypeStruct((s, HEADS * V_HEAD), F32), jax.ShapeDtypeStruct((s, HEADS * QK_ROPE), BF16),
                   jax.ShapeDtypeStruct((hp, s, 2), F32)],
        scratch_shapes=[pltpu.VMEM((2, s, 2 * QK_NOPE), BF16)],
        compiler_params=_params("parallel", "arbitrary"))(qn, qp, kv, kpr, cos4, sin4)


def attn_bwd2(qn, qpr, kv, kpr, o, do, lse, cos4, sin4):
    s = qn.shape[0]
    hp = HEADS // 2
    t = _tile(s, ATT_TILE)
    nk = s // t

    def body(qn_ref, qpr_ref, kv_ref, kp_ref, o_ref, do_ref, l_ref, c_ref, s_ref,
             dqn_ref, dqp_ref, dkv_ref, dkp_ref, qcat_ref, dq_ref, delta_ref):
        ki = pl.program_id(1)

        @pl.when(ki == 0)
        def _():
            dq_ref[...] = jnp.zeros_like(dq_ref)
            for hh in range(2):
                qcat_ref[hh] = _q_cat(qn_ref[:, hh * QK_NOPE:(hh + 1) * QK_NOPE], qpr_ref[...], hh)
                cols = slice(hh * V_HEAD, (hh + 1) * V_HEAD)
                delta_ref[hh] = jnp.sum(do_ref[:, cols] * o_ref[:, cols], axis=-1, keepdims=True)

        rows_k = pl.ds(pl.multiple_of(ki * t, t), t)
        kcat = [jnp.concatenate([kv_ref[rows_k, 2 * hh * QK_NOPE:(2 * hh + 1) * QK_NOPE], kp_ref[rows_k, :]], axis=1) for hh in range(2)]
        vs = [kv_ref[rows_k, (2 * hh + 1) * QK_NOPE:(2 * hh + 2) * QK_NOPE] for hh in range(2)]

        def block(qb, carry, diagonal):
            rows = pl.ds(pl.multiple_of(qb * t, t), t)
            out = []
            for hh in range(2):
                dkc, dv = carry[hh]
                q_c = qcat_ref[hh, rows, :]
                do_b = do_ref[rows, hh * V_HEAD:(hh + 1) * V_HEAD].astype(BF16)
                sc = _dot_nt(q_c, kcat[hh]) * _ATT_SCALE
                if diagonal:
                    sc = _causal(sc)
                p = jnp.exp(sc - l_ref[rows, hh:hh + 1])
                dpv = _dot_nt(do_b, vs[hh])
                ds = (p * (dpv - delta_ref[hh, rows, :]) * _ATT_SCALE).astype(BF16)
                dv = dv + _dot_tn(p.astype(BF16), do_b)
                dkc = dkc + _dot_tn(ds, q_c)
                dq_ref[hh, rows, :] += jnp.dot(ds, kcat[hh], preferred_element_type=F32)
                out.append((dkc, dv))
            return tuple(out)

        one = (jnp.zeros((t, 2 * QK_NOPE), F32), jnp.zeros((t, V_HEAD), F32))
        carry = block(ki, (one, one), True)
        carry = lax.fori_loop(ki + 1, nk, lambda qb, cr: block(qb, cr, False), carry)
        dkp = jnp.zeros((t, LANES), F32)
        for hh in range(2):
            dkc, dv = carry[hh]
            dkv_ref[:, 2 * hh * QK_NOPE:(2 * hh + 1) * QK_NOPE] = dkc[:, :QK_NOPE].astype(dkv_ref.dtype)
            dkv_ref[:, (2 * hh + 1) * QK_NOPE:(2 * hh + 2) * QK_NOPE] = dv.astype(dkv_ref.dtype)
            dkp = dkp + dkc[:, QK_NOPE:]
        dkp_ref[...] = dkp

        @pl.when(ki == nk - 1)
        def _():
            lane = lax.broadcasted_iota(jnp.int32, (s, LANES), 1)
            dqp = jnp.where(lane < QK_ROPE, dq_ref[0, :, QK_NOPE:], dq_ref[1, :, QK_NOPE:])
            dqp_ref[...] = _rope(dqp, c_ref[...], -s_ref[...]).astype(dqp_ref.dtype)
            for hh in range(2):
                dqn_ref[:, hh * QK_NOPE:(hh + 1) * QK_NOPE] = dq_ref[hh, :, :QK_NOPE].astype(dqn_ref.dtype)

    qblk = pl.BlockSpec((s, 2 * QK_NOPE), lambda h, i: (0, h))
    pblk = pl.BlockSpec((s, LANES), lambda h, i: (0, h))
    tab = _full((s, LANES))
    return pl.pallas_call(
        body, name="attn_bwd", grid=(hp, nk),
        in_specs=[qblk, pblk, pl.BlockSpec((s, 4 * QK_NOPE), lambda h, i: (0, h)), tab, qblk, qblk,
                  pl.BlockSpec((None, s, 2), lambda h, i: (h, 0, 0)), tab, tab],
        out_specs=[qblk, pblk, pl.BlockSpec((t, 4 * QK_NOPE), lambda h, i: (i, h)), pl.BlockSpec((None, t, LANES), lambda h, i: (h, i, 0))],
        out_shape=[jax.ShapeDtypeStruct((s, HEADS * QK_NOPE), BF16), jax.ShapeDtypeStruct((s, HEADS * QK_ROPE), BF16),
                   jax.ShapeDtypeStruct((s, HEADS * 2 * QK_NOPE), BF16), jax.ShapeDtypeStruct((hp, s, LANES), F32)],
        scratch_shapes=[pltpu.VMEM((2, s, 2 * QK_NOPE), BF16), pltpu.VMEM((2, s, 2 * QK_NOPE), F32), pltpu.VMEM((2, s, 1), F32)],
        compiler_params=_params("parallel", "arbitrary"))(qn, qpr, kv, kpr, o, do, lse, cos4, sin4)


def _causal_at(sc, row0, col0):
    row = lax.broadcasted_iota(jnp.int32, sc.shape, 0) + row0
    col = lax.broadcasted_iota(jnp.int32, sc.shape, 1) + col0
    return jnp.where(col <= row, sc, _NEG)


def attn_fwd3(qn, qp, kv, kpr, cos4, sin4):
    s = qn.shape[0]
    hp = HEADS // 2
    t = _tile(s, ATT_TILE)
    tk = 2 * t
    nq = s // t
    assert s % tk == 0

    def body(qn_ref, qp_ref, kv_ref, kp_ref, c_ref, s_ref, o_ref, qpr_ref, l_ref, kcat_ref):
        qi = pl.program_id(1)

        @pl.when(qi == 0)
        def _():
            for hh in range(2):
                kcat_ref[hh, :, 0:QK_NOPE] = kv_ref[:, 2 * hh * QK_NOPE:(2 * hh + 1) * QK_NOPE]
                kcat_ref[hh, :, QK_NOPE:] = kp_ref[...]

        qpr = _rope(qp_ref[...], c_ref[...], s_ref[...]).astype(BF16)
        qpr_ref[...] = qpr
        qcat = [_q_cat(qn_ref[:, hh * QK_NOPE:(hh + 1) * QK_NOPE], qpr, hh) for hh in range(2)]

        def block(kb, carry, diagonal):
            start = pl.multiple_of(kb * tk, tk)
            rows = pl.ds(start, tk)
            out = []
            for hh in range(2):
                m, l, acc = carry[hh]
                sc = _dot_nt(qcat[hh], kcat_ref[hh, rows, :]) * _ATT_SCALE
                if diagonal:
                    sc = _causal_at(sc, qi * t, start)
                m_new = jnp.maximum(m, jnp.max(sc, axis=-1, keepdims=True))
                alpha = jnp.exp(m - m_new)
                p = jnp.exp(sc - m_new)
                l = alpha * l + jnp.sum(p, axis=-1, keepdims=True)
                v = kv_ref[rows, (2 * hh + 1) * QK_NOPE:(2 * hh + 2) * QK_NOPE]
                acc = alpha * acc + jnp.dot(p.astype(BF16), v, preferred_element_type=F32)
                out.append((m_new, l, acc))
            return tuple(out)

        one = (jnp.full((t, 1), _NEG, F32), jnp.zeros((t, 1), F32), jnp.zeros((t, V_HEAD), F32))
        carry = lax.fori_loop(0, qi // 2, lambda kb, cr: block(kb, cr, False), (one, one))
        carry = block(qi // 2, carry, True)
        for hh in range(2):
            m, l, acc = carry[hh]
            o_ref[:, hh * V_HEAD:(hh + 1) * V_HEAD] = acc / l
            l_ref[:, hh:hh + 1] = m + jnp.log(l)

    return pl.pallas_call(
        body, name="attn_fwd", grid=(hp, nq),
        in_specs=[pl.BlockSpec((t, 2 * QK_NOPE), lambda h, i: (i, h)), pl.BlockSpec((t, LANES), lambda h, i: (i, h)),
                  pl.BlockSpec((s, 4 * QK_NOPE), lambda h, i: (0, h)), _full((s, LANES)),
                  pl.BlockSpec((t, LANES), lambda h, i: (i, 0)), pl.BlockSpec((t, LANES), lambda h, i: (i, 0))],
        out_specs=[pl.BlockSpec((t, 2 * V_HEAD), lambda h, i: (i, h)), pl.BlockSpec((t, LANES), lambda h, i: (i, h)),
                   pl.BlockSpec((None, t, 2), lambda h, i: (h, i, 0))],
        out_shape=[jax.ShapeDtypeStruct((s, HEADS * V_HEAD), F32), jax.ShapeDtypeStruct((s, HEADS * QK_ROPE), BF16),
                   jax.ShapeDtypeStruct((hp, s, 2), F32)],
        scratch_shapes=[pltpu.VMEM((2, s, 2 * QK_NOPE), BF16)],
        compiler_params=_params("parallel", "arbitrary"))(qn, qp, kv, kpr, cos4, sin4)


def attn_bwd3(qn, qpr, kv, kpr, o, do, lse, cos4, sin4):
    s = qn.shape[0]
    hp = HEADS // 2
    t = _tile(s, ATT_TILE)
    tq = 2 * t
    nk = s // t
    nq2 = s // tq
    assert s % tq == 0

    def body(qn_ref, qpr_ref, kv_ref, kp_ref, o_ref, do_ref, l_ref, c_ref, s_ref,
             dqn_ref, dqp_ref, dkv_ref, dkp_ref, qcat_ref, dq_ref, delta_ref):
        ki = pl.program_id(1)

        @pl.when(ki == 0)
        def _():
            dq_ref[...] = jnp.zeros_like(dq_ref)
            for hh in range(2):
                qcat_ref[hh] = _q_cat(qn_ref[:, hh * QK_NOPE:(hh + 1) * QK_NOPE], qpr_ref[...], hh)
                cols = slice(hh * V_HEAD, (hh + 1) * V_HEAD)
                delta_ref[hh] = jnp.sum(do_ref[:, cols] * o_ref[:, cols], axis=-1, keepdims=True)

        rows_k = pl.ds(pl.multiple_of(ki * t, t), t)
        kcat = [jnp.concatenate([kv_ref[rows_k, 2 * hh * QK_NOPE:(2 * hh + 1) * QK_NOPE], kp_ref[rows_k, :]], axis=1) for hh in range(2)]
        vs = [kv_ref[rows_k, (2 * hh + 1) * QK_NOPE:(2 * hh + 2) * QK_NOPE] for hh in range(2)]

        def block(qb, carry, diagonal):
            start = pl.multiple_of(qb * tq, tq)
            rows = pl.ds(start, tq)
            out = []
            for hh in range(2):
                dkc, dv = carry[hh]
                q_c = qcat_ref[hh, rows, :]
                do_b = do_ref[rows, hh * V_HEAD:(hh + 1) * V_HEAD].astype(BF16)
                sc = _dot_nt(q_c, kcat[hh]) * _ATT_SCALE
                if diagonal:
                    sc = _causal_at(sc, start, ki * t)
                p = jnp.exp(sc - l_ref[rows, hh:hh + 1])
                dpv = _dot_nt(do_b, vs[hh])
                ds = (p * (dpv - delta_ref[hh, rows, :]) * _ATT_SCALE).astype(BF16)
                dv = dv + _dot_tn(p.astype(BF16), do_b)
                dkc = dkc + _dot_tn(ds, q_c)
                dq_ref[hh, rows, :] += jnp.dot(ds, kcat[hh], preferred_element_type=F32)
                out.append((dkc, dv))
            return tuple(out)

        one = (jnp.zeros((t, 2 * QK_NOPE), F32), jnp.zeros((t, V_HEAD), F32))
        carry = block(ki // 2, (one, one), True)
        carry = lax.fori_loop(ki // 2 + 1, nq2, lambda qb, cr: block(qb, cr, False), carry)
        dkp = jnp.zeros((t, LANES), F32)
        for hh in range(2):
            dkc, dv = carry[hh]
            dkv_ref[:, 2 * hh * QK_NOPE:(2 * hh + 1) * QK_NOPE] = dkc[:, :QK_NOPE].astype(dkv_ref.dtype)
            dkv_ref[:, (2 * hh + 1) * QK_NOPE:(2 * hh + 2) * QK_NOPE] = dv.astype(dkv_ref.dtype)
            dkp = dkp + dkc[:, QK_NOPE:]
        dkp_ref[...] = dkp

        @pl.when(ki == nk - 1)
        def _():
            lane = lax.broadcasted_iota(jnp.int32, (s, LANES), 1)
            dqp = jnp.where(lane < QK_ROPE, dq_ref[0, :, QK_NOPE:], dq_ref[1, :, QK_NOPE:])
            dqp_ref[...] = _rope(dqp, c_ref[...], -s_ref[...]).astype(dqp_ref.dtype)
            for hh in range(2):
                dqn_ref[:, hh * QK_NOPE:(hh + 1) * QK_NOPE] = dq_ref[hh, :, :QK_NOPE].astype(dqn_ref.dtype)

    qblk = pl.BlockSpec((s, 2 * QK_NOPE), lambda h, i: (0, h))
    pblk = pl.BlockSpec((s, LANES), lambda h, i: (0, h))
    tab = _full((s, LANES))
    return pl.pallas_call(
        body, name="attn_bwd", grid=(hp, nk),
        in_specs=[qblk, pblk, pl.BlockSpec((s, 4 * QK_NOPE), lambda h, i: (0, h)), tab, qblk, qblk,
                  pl.BlockSpec((None, s, 2), lambda h, i: (h, 0, 0)), tab, tab],
        out_specs=[qblk, pblk, pl.BlockSpec((t, 4 * QK_NOPE), lambda h, i: (i, h)), pl.BlockSpec((None, t, LANES), lambda h, i: (h, i, 0))],
        out_shape=[jax.ShapeDtypeStruct((s, HEADS * QK_NOPE), BF16), jax.ShapeDtypeStruct((s, HEADS * QK_ROPE), BF16),
                   jax.ShapeDtypeStruct((s, HEADS * 2 * QK_NOPE), BF16), jax.ShapeDtypeStruct((hp, s, LANES), F32)],
        scratch_shapes=[pltpu.VMEM((2, s, 2 * QK_NOPE), BF16), pltpu.VMEM((2, s, 2 * QK_NOPE), F32), pltpu.VMEM((2, s, 1), F32)],
        compiler_params=_params("parallel", "arbitrary"))(qn, qpr, kv, kpr, o, do, lse, cos4, sin4)


def kpe_bwd(dkp, cos4, sin4, pad_cols):
    hp, s, _ = dkp.shape
    tr = _tile(s, ROW_TILE * 2)

    def body(d_ref, c_ref, s_ref, o_ref):
        tot = d_ref[0]
        for h in range(1, hp):
            tot = tot + d_ref[h]
        tot = tot + pltpu.roll(tot, QK_ROPE, 1)
        lane = lax.broadcasted_iota(jnp.int32, tot.shape, 1)
        dk = jnp.where(lane < QK_ROPE, _rope(tot, c_ref[...], -s_ref[...]), jnp.zeros_like(tot))
        o_ref[...] = jnp.zeros_like(o_ref)
        o_ref[:, 0:LANES] = dk.astype(o_ref.dtype)

    row = pl.BlockSpec((tr, LANES), lambda i: (i, 0))
    return pl.pallas_call(body, name="kpe_bwd", grid=(s // tr,),
                          in_specs=[pl.BlockSpec((hp, tr, LANES), lambda i: (0, i, 0)), row, row],
                          out_specs=pl.BlockSpec((tr, pad_cols), lambda i: (i, 0)),
                          out_shape=jax.ShapeDtypeStruct((s, pad_cols), BF16), compiler_params=_params("parallel"))(dkp, cos4, sin4)


def _shift_down(x, n):
    row = lax.broadcasted_iota(jnp.int32, x.shape, 0)
    return jnp.where(row >= n, pltpu.roll(x, n, 0), jnp.zeros_like(x))


def _shift_up(x, n):
    rows = x.shape[0]
    row = lax.broadcasted_iota(jnp.int32, x.shape, 0)
    return jnp.where(row < rows - n, pltpu.roll(x, rows - n, 0), jnp.zeros_like(x))


def _conv(x, w_ref, b_ref):
    return w_ref[2:3, :] * x + w_ref[1:2, :] * _shift_down(x, 1) + w_ref[0:1, :] * _shift_down(x, 2) + b_ref[...]


def conv_act_fwd(upre, conv_w, conv_b):
    s, f2 = upre.shape
    f = f2 // 2
    tc = _tile(f, COL_TILE)
    nc = f // tc

    def body(ug_ref, uv_ref, wg_ref, wv_ref, bg_ref, bv_ref, o_ref):
        gh = _conv(_f32(ug_ref), wg_ref, bg_ref)
        vh = _conv(_f32(uv_ref), wv_ref, bv_ref)
        o_ref[...] = (gh * _sigmoid(gh) * vh).astype(o_ref.dtype)

    def spec(rows, shift):
        return pl.BlockSpec((rows, tc), lambda j: (0, j + shift))

    return pl.pallas_call(
        body, name="conv_act_fwd", grid=(nc,),
        in_specs=[spec(s, 0), spec(s, nc), spec(3, 0), spec(3, nc), spec(1, 0), spec(1, nc)], out_specs=spec(s, 0),
        out_shape=jax.ShapeDtypeStruct((s, f), BF16), compiler_params=_params("parallel"))(upre, upre, conv_w, conv_w, conv_b, conv_b)


def conv_act_bwd(upre, conv_w, conv_b, df):
    s, f2 = upre.shape
    f = f2 // 2
    tc = _tile(f, COL_TILE)
    nc = f // tc

    def half(x, d, w_ref, du_ref, which, gw_ref, gb_ref):
        d1, d2 = _shift_up(d, 1), _shift_up(d, 2)
        gb_ref[...] = _colsum(d)
        gw_ref[2:3, :] = _colsum(d * x)
        gw_ref[1:2, :] = _colsum(d1 * x)
        gw_ref[0:1, :] = _colsum(d2 * x)
        du_ref[which] = (w_ref[2:3, :] * d + w_ref[1:2, :] * d1 + w_ref[0:1, :] * d2).astype(du_ref.dtype)

    def body(ug_ref, uv_ref, wg_ref, wv_ref, bg_ref, bv_ref, df_ref, du_ref, gwg_ref, gwv_ref, gbg_ref, gbv_ref):
        xg, xv = _f32(ug_ref), _f32(uv_ref)
        gh = _conv(xg, wg_ref, bg_ref)
        vh = _conv(xv, wv_ref, bv_ref)
        sg = _sigmoid(gh)
        df_v = _f32(df_ref)
        half(xg, df_v * vh * (sg * (1.0 + gh * (1.0 - sg))), wg_ref, du_ref, 0, gwg_ref, gbg_ref)
        half(xv, df_v * (gh * sg), wv_ref, du_ref, 1, gwv_ref, gbv_ref)

    def spec(rows, shift):
        return pl.BlockSpec((rows, tc), lambda j: (0, j + shift))

    gw = jax.ShapeDtypeStruct((3, f), F32)
    gb = jax.ShapeDtypeStruct((1, f), F32)
    return pl.pallas_call(
        body, name="conv_act_bwd", grid=(nc,),
        in_specs=[spec(s, 0), spec(s, nc), spec(3, 0), spec(3, nc), spec(1, 0), spec(1, nc), spec(s, 0)],
        out_specs=[pl.BlockSpec((2, s, tc), lambda j: (0, 0, j)), spec(3, 0), spec(3, 0), spec(1, 0), spec(1, 0)],
        out_shape=[jax.ShapeDtypeStruct((2, s, f), BF16), gw, gw, gb, gb],
        compiler_params=_params("parallel"))(upre, upre, conv_w, conv_w, conv_b, conv_b, df)


def _elementwise_tile(r, c, limit):
    if r % 8:
        return r, c
    best = (8, c if c % LANES else LANES)
    for k in (1, 2, 4, 8, 16):
        if k > 1 and c % (LANES * k):
            continue
        tc = c // k
        tr = max(8, min(r, limit // tc) // 8 * 8)
        while r % tr:
            tr -= 8
        if tr * tc <= max(limit, 8 * tc) and tr * tc > best[0] * best[1]:
            best = (tr, tc)
    return best


def adamw(name, w, m, v, parts):
    npart, r, c = parts.shape
    tr, tc = _elementwise_tile(r, c, ADAMW_TILE_ELEMS)
    bc1 = 1.0 - ADAM_B1 ** ADAM_STEP
    bc2 = 1.0 - ADAM_B2 ** ADAM_STEP

    def body(w_ref, m_ref, v_ref, p_ref, g_ref, d_ref, nm_ref, nv_ref):
        g = p_ref[0].astype(F32)
        for k in range(1, npart):
            g = g + p_ref[k].astype(F32)
        m_new = ADAM_B1 * m_ref[...] + (1.0 - ADAM_B1) * g
        v_new = ADAM_B2 * v_ref[...] + (1.0 - ADAM_B2) * (g * g)
        g_ref[...] = g
        nm_ref[...] = m_new
        nv_ref[...] = v_new
        d_ref[...] = -ADAM_LR * ((m_new / bc1) / (jnp.sqrt(v_new / bc2) + ADAM_EPS) + ADAM_WD * w_ref[...])

    deps = _TOKENS.take()
    blk = pl.BlockSpec((tr, tc), lambda i, j: (i, j))
    out = jax.ShapeDtypeStruct((r, c), F32)
    return pl.pallas_call(
        lambda *refs: body(*refs[:4], *refs[4 + len(deps):]), name=name, grid=(r // tr, c // tc),
        in_specs=[blk, blk, blk, pl.BlockSpec((npart, tr, tc), lambda i, j: (0, i, j))] + [pl.BlockSpec(memory_space=pl.ANY)] * len(deps),
        out_specs=[blk, blk, blk, blk], out_shape=[out, out, out, out],
        compiler_params=_params("parallel", "parallel"))(w, m, v, parts, *deps)


def _position():
    return lax.axis_index("x"), lax.axis_index("y"), lax.axis_index("c")


def _index(p):
    return 4 * p[0] + 2 * p[1] + p[2]


def _peer(me, r):
    return (me[0] ^ ((r >> 2) & 1), me[1] ^ ((r >> 1) & 1), me[2] ^ (r & 1))


_ANY = pl.BlockSpec(memory_space=pl.ANY)


def all_gather_two_level(shards):
    n = len(shards)

    def body(*refs):
        ins, outs = refs[:n], refs[n:2 * n]
        send_sems, recv_sems, local_sems = refs[2 * n:]
        x, y, c = _position()
        me, sibling = (x, y, c), (x, y, 1 - c)
        chips = [(1 - x, y), (x, 1 - y), (1 - x, 1 - y)]

        def copy(w, k, block, to, src=None):
            slot = outs[w].at[_index(block)]
            return pltpu.make_async_remote_copy(src_ref=slot if src is None else src, dst_ref=slot,
                                                send_sem=send_sems.at[7 * w + k], recv_sem=recv_sems.at[7 * w + k],
                                                device_id=to, device_id_type=MESH)

        mine = [pltpu.make_async_copy(ins[w], outs[w].at[_index(me)], local_sems.at[w]) for w in range(n)]
        for cp in mine:
            cp.start()
        first = []
        for w in range(n):
            first.append(copy(w, 0, me, sibling, src=ins[w]))
            first += [copy(w, 1 + j, me, (*chip, c), src=ins[w]) for j, chip in enumerate(chips)]
        for cp in first:
            cp.start()
        passed = []
        for w in range(n):
            for j, chip in enumerate(chips):
                copy(w, 1 + j, (*chip, c), me).wait_recv()
                cp = copy(w, 4 + j, (*chip, c), sibling)
                cp.start()
                passed.append(cp)
        for w in range(n):
            copy(w, 0, sibling, me).wait_recv()
            for j, chip in enumerate(chips):
                copy(w, 4 + j, (*chip, 1 - c), me).wait_recv()
        for cp in first + passed:
            cp.wait_send()
        for cp in mine:
            cp.wait()

    return pl.pallas_call(
        body, name="all_gather_weights",
        out_shape=[jax.ShapeDtypeStruct((N_DEV,) + a.shape, a.dtype) for a in shards],
        in_specs=[_ANY] * n, out_specs=[_ANY] * n,
        scratch_shapes=[pltpu.SemaphoreType.DMA((7 * n,)), pltpu.SemaphoreType.DMA((7 * n,)), pltpu.SemaphoreType.DMA((n,))],
        )(*shards)


def exchange(name, arrays, scatter):
    n = len(arrays)

    def body(*refs):
        ins, outs = refs[:n], refs[n:2 * n]
        send_sems, recv_sems, local_sems = refs[2 * n:]
        me = _position()
        copies = []
        for w in range(n):
            src = ins[w].at[_index(me)] if scatter else ins[w]
            cp = pltpu.make_async_copy(src, outs[w].at[_index(me)], local_sems.at[w])
            cp.start()
            copies.append(cp)
        remote = []
        for w in range(n):
            for r in range(1, N_DEV):
                peer = _peer(me, r)
                src = ins[w].at[_index(peer)] if scatter else ins[w]
                cp = pltpu.make_async_remote_copy(src_ref=src, dst_ref=outs[w].at[_index(me)],
                                                  send_sem=send_sems.at[7 * w + r - 1], recv_sem=recv_sems.at[7 * w + r - 1],
                                                  device_id=peer, device_id_type=MESH)
                cp.start()
                remote.append(cp)
        for cp in remote:
            cp.wait()
        for cp in copies:
            cp.wait()

    blocks = [a.shape[1:] if scatter else a.shape for a in arrays]
    return pl.pallas_call(
        body, name=name,
        out_shape=[jax.ShapeDtypeStruct((N_DEV,) + b, a.dtype) for a, b in zip(arrays, blocks)],
        in_specs=[_ANY] * n, out_specs=[_ANY] * n,
        scratch_shapes=[pltpu.SemaphoreType.DMA((7 * n,)), pltpu.SemaphoreType.DMA((7 * n,)), pltpu.SemaphoreType.DMA((n,))],
        )(*arrays)


_HBM = pl.BlockSpec(memory_space=pltpu.HBM)
_SEM = pl.BlockSpec(memory_space=pltpu.SEMAPHORE)
_EFFECT = pltpu.SideEffectType.DATAFLOW_SIDE_EFFECTING


def _direct_copies(ins, lands, send_sems, recv_sems, scatter):
    me = _position()
    copies = []
    for w in range(len(ins)):
        for r in range(1, N_DEV):
            peer = _peer(me, r)
            src = ins[w].at[_index(peer)] if scatter else ins[w]
            copies.append(pltpu.make_async_remote_copy(src_ref=src, dst_ref=lands[w].at[_index(me)], send_sem=send_sems.at[7 * w + r - 1],
                                                       recv_sem=recv_sems.at[7 * w + r - 1], device_id=peer, device_id_type=MESH))
    return copies


def exchange_start(name, groups, scatter):
    arrays = [a for g in groups for a in g]
    n = len(arrays)
    blocks = [a.shape[1:] if scatter else a.shape for a in arrays]
    lands = [lax.empty((N_DEV,) + b, a.dtype) for a, b in zip(arrays, blocks)]
    ng = len(groups)

    def body(*refs):
        ins, lnd = refs[:n], refs[n:2 * n]
        sems = refs[2 * n:2 * n + 2 * ng]
        token = refs[2 * n + 2 * ng + 2 * n]
        local_sem = refs[2 * n + 2 * ng + 2 * n + 1]
        me = _position()
        local = []
        for w in range(n):
            src = ins[w].at[_index(me)] if scatter else ins[w]
            cp = pltpu.make_async_copy(src, lnd[w].at[_index(me)], local_sem.at[w])
            cp.start()
            local.append(cp)
        w0 = 0
        for gi, g in enumerate(groups):
            for cp in _direct_copies(ins[w0:w0 + len(g)], lnd[w0:w0 + len(g)], sems[2 * gi], sems[2 * gi + 1], scatter):
                cp.start()
            w0 += len(g)
        for cp in local:
            cp.wait()
        token[...] = jnp.zeros_like(token)

    sem_shapes = []
    for g in groups:
        sem_shapes += [pltpu.SemaphoreType.DMA((7 * len(g),)), pltpu.SemaphoreType.DMA((7 * len(g),))]
    out = pl.pallas_call(
        body, name=name,
        out_shape=tuple(sem_shapes) + tuple(pltpu.HBM(a.shape, a.dtype) for a in arrays) + tuple(pltpu.HBM(l.shape, l.dtype) for l in lands)
        + (jax.ShapeDtypeStruct((8, LANES), F32),),
        in_specs=[_HBM] * (2 * n), out_specs=tuple([_SEM] * (2 * ng) + [_HBM] * (2 * n) + [pl.BlockSpec(memory_space=pltpu.VMEM)]),
        input_output_aliases={i: 2 * ng + i for i in range(2 * n)},
        scratch_shapes=[pltpu.SemaphoreType.DMA((n,))],
        compiler_params=pltpu.CompilerParams(has_side_effects=_EFFECT),
    )(*[pltpu.with_memory_space_constraint(a, pltpu.HBM) for a in arrays], *[pltpu.with_memory_space_constraint(l, pltpu.HBM) for l in lands])
    sems, thru, token = out[:2 * ng], out[2 * ng:2 * ng + 2 * n], out[-1]
    res, w0 = [], 0
    for gi, g in enumerate(groups):
        res.append((sems[2 * gi], sems[2 * gi + 1], list(thru[w0:w0 + len(g)]), list(thru[n + w0:n + w0 + len(g)])))
        w0 += len(g)
    return res, token


def exchange_wait(name, group, after, scatter):
    send_sems, recv_sems, srcs, lands = group
    n = len(srcs)

    def body(*refs):
        ins, lnd = refs[:n], refs[n:2 * n]
        for cp in _direct_copies(ins, lnd, refs[2 * n], refs[2 * n + 1], scatter):
            cp.wait_send()
            cp.wait_recv()

    out = pl.pallas_call(
        body, name=name, out_shape=tuple(pltpu.HBM(a.shape, a.dtype) for a in srcs + lands),
        in_specs=[_HBM] * (2 * n) + [_SEM, _SEM, pl.BlockSpec(memory_space=pl.ANY)], out_specs=tuple([_HBM] * (2 * n)),
        input_output_aliases={i: i for i in range(2 * n)},
        compiler_params=pltpu.CompilerParams(has_side_effects=_EFFECT),
    )(*srcs, *lands, send_sems, recv_sems, after)
    return list(out[n:])


def _after(x, token):
    return lax.optimization_barrier((x, token))[0]


_TOKEN = jax.ShapeDtypeStruct((8, LANES), F32)
_VM = pl.BlockSpec(memory_space=pltpu.VMEM)
_SIDE = pltpu.CompilerParams(has_side_effects=_EFFECT)


def _hbm(a):
    return pltpu.with_memory_space_constraint(a, pltpu.HBM)


def _like(a):
    return pltpu.HBM(a.shape, a.dtype)


def _dma_sems(n):
    return pltpu.SemaphoreType.DMA((n,))


def _other_chips(x, y):
    return [(1 - x, y), (x, 1 - y), (1 - x, 1 - y)]


COPY_STREAMS = 8


def _row_chunks(src, dst):
    rows = src.shape[0]
    n = COPY_STREAMS
    while n > 1 and rows % (16 * n):
        n //= 2
    r = rows // n
    return [(src.at[pl.ds(i * r, r)], dst.at[pl.ds(i * r, r)]) for i in range(n)]


def _local_copy(src, dst, sem):
    return [pltpu.make_async_copy(s, d, sem) for s, d in _row_chunks(src, dst)]


class _rcopy:
    def __init__(self, src, dst, send_sem, recv_sem, to):
        self.parts = [pltpu.make_async_remote_copy(src_ref=s, dst_ref=d, send_sem=send_sem, recv_sem=recv_sem, device_id=to, device_id_type=MESH)
                      for s, d in _row_chunks(src, dst)]

    def start(self):
        for cp in self.parts:
            cp.start()

    def wait_send(self):
        for cp in self.parts:
            cp.wait_send()

    def wait_recv(self):
        for cp in self.parts:
            cp.wait_recv()


def _afters(after):
    return list(after) if isinstance(after, (list, tuple)) else [after]


def ag_start(name, shards, after):
    n = len(shards)
    lands = [lax.empty((N_DEV,) + a.shape, a.dtype) for a in shards]
    afters = _afters(after)
    na = len(afters)

    def body(*refs):
        ins, lnd, send_sems, recv_sems, token = refs[:n], refs[n:2 * n], refs[2 * n + na], refs[2 * n + na + 1], refs[4 * n + na + 2]
        x, y, c = _position()
        for w in range(n):
            slot = lnd[w].at[_index((x, y, c))]
            for k, to in enumerate([(x, y, 1 - c)] + [(*chip, c) for chip in _other_chips(x, y)]):
                _rcopy(ins[w], slot, send_sems.at[4 * w + k], recv_sems.at[4 * w + k], to).start()
        token[...] = jnp.zeros_like(token)

    out = pl.pallas_call(
        body, name=name, out_shape=(_dma_sems(4 * n), _dma_sems(4 * n)) + tuple(_like(a) for a in shards + lands) + (_TOKEN,),
        in_specs=[_HBM] * (2 * n) + [_ANY] * na, out_specs=(_SEM, _SEM) + (_HBM,) * (2 * n) + (_VM,),
        input_output_aliases={i: 2 + i for i in range(2 * n)}, compiler_params=_SIDE)(*[_hbm(a) for a in shards + lands], *afters)
    _TOKENS.push(out[-1])
    return out[0], out[1], list(out[2:2 + n]), list(out[2 + n:2 + 2 * n])


def ag_forward(name, started, after):
    send, recv, shards, lands = started
    n = len(shards)
    afters = list(after) if isinstance(after, (list, tuple)) else [after]
    na = len(afters)

    def body(*refs):
        ins, lnd, send_sems, recv_sems = refs[:n], refs[n:2 * n], refs[2 * n], refs[2 * n + 1]
        fsend, frecv, token = refs[2 * n + 2 + na], refs[2 * n + 3 + na], refs[4 * n + 4 + na]
        x, y, c = _position()
        for w in range(n):
            for j, chip in enumerate(_other_chips(x, y)):
                slot = lnd[w].at[_index((*chip, c))]
                _rcopy(ins[w], slot, send_sems.at[4 * w + 1 + j], recv_sems.at[4 * w + 1 + j], (*chip, c)).wait_recv()
                _rcopy(slot, slot, fsend.at[3 * w + j], frecv.at[3 * w + j], (x, y, 1 - c)).start()
        token[...] = jnp.zeros_like(token)

    out = pl.pallas_call(
        body, name=name, out_shape=(_dma_sems(3 * n), _dma_sems(3 * n)) + tuple(_like(a) for a in shards + lands) + (_TOKEN,),
        in_specs=[_HBM] * (2 * n) + [_SEM, _SEM] + [_ANY] * na, out_specs=(_SEM, _SEM) + (_HBM,) * (2 * n) + (_VM,),
        input_output_aliases={i: 2 + i for i in range(2 * n)}, compiler_params=_SIDE)(*shards, *lands, send, recv, *afters)
    _TOKENS.push(out[-1])
    return send, recv, out[0], out[1], list(out[2:2 + n]), list(out[2 + n:2 + 2 * n])


def ag_wait(name, forwarded, after):
    send, recv, fsend, frecv, shards, lands = forwarded
    n = len(shards)

    def body(*refs):
        ins, lnd, send_sems, recv_sems, fsend_r, frecv_r = refs[:n], refs[n:2 * n], refs[2 * n], refs[2 * n + 1], refs[2 * n + 2], refs[2 * n + 3]
        x, y, c = _position()
        sibling = (x, y, 1 - c)
        for w in range(n):
            own = lnd[w].at[_index((x, y, c))]
            _rcopy(ins[w], lnd[w].at[_index(sibling)], send_sems.at[4 * w], recv_sems.at[4 * w], sibling).wait_recv()
            for j, chip in enumerate(_other_chips(x, y)):
                _rcopy(ins[w], lnd[w].at[_index((*chip, 1 - c))], fsend_r.at[3 * w + j], frecv_r.at[3 * w + j], sibling).wait_recv()
            for k in range(4):
                _rcopy(ins[w], own, send_sems.at[4 * w + k], recv_sems.at[4 * w + k], sibling).wait_send()
            for j in range(3):
                _rcopy(ins[w], own, fsend_r.at[3 * w + j], frecv_r.at[3 * w + j], sibling).wait_send()

    out = pl.pallas_call(
        body, name=name, out_shape=tuple(_like(a) for a in shards + lands),
        in_specs=[_HBM] * (2 * n) + [_SEM] * 4 + [_ANY] * len(_afters(after)),
        out_specs=(_HBM,) * (2 * n), input_output_aliases={i: i for i in range(2 * n)},
        compiler_params=_SIDE)(*shards, *lands, send, recv, fsend, frecv, *_afters(after))
    return [lax.dynamic_update_index_in_dim(land, shard, _index(_position()), 0) for shard, land in zip(out[:n], out[n:])]


def rs_d2d_start(name, grads):
    n = len(grads)
    lands = [lax.empty((4,) + g.shape[1:], g.dtype) for g in grads]

    def body(*refs):
        ins, lnd, send_sems, recv_sems, token = refs[:n], refs[n:2 * n], refs[2 * n], refs[2 * n + 1], refs[4 * n + 2]
        x, y, c = _position()
        for w in range(n):
            for i in range(4):
                _rcopy(ins[w].at[2 * i + 1 - c], lnd[w].at[i], send_sems.at[4 * w + i], recv_sems.at[4 * w + i], (x, y, 1 - c)).start()
        token[...] = jnp.zeros_like(token)

    out = pl.pallas_call(
        body, name=name, out_shape=(_dma_sems(4 * n), _dma_sems(4 * n)) + tuple(_like(a) for a in grads + lands) + (_TOKEN,),
        in_specs=[_HBM] * (2 * n), out_specs=(_SEM, _SEM) + (_HBM,) * (2 * n) + (_VM,),
        input_output_aliases={i: 2 + i for i in range(2 * n)}, compiler_params=_SIDE)(*[_hbm(a) for a in grads + lands])
    _TOKENS.push(out[-1])
    return out[0], out[1], list(out[2:2 + n]), list(out[2 + n:2 + 2 * n])


def rs_d2d_wait(name, started, after):
    send, recv, grads, lands = started
    n = len(grads)

    def body(*refs):
        ins, lnd, send_sems, recv_sems = refs[:n], refs[n:2 * n], refs[2 * n], refs[2 * n + 1]
        x, y, c = _position()
        for w in range(n):
            for i in range(4):
                cp = _rcopy(ins[w].at[2 * i + 1 - c], lnd[w].at[i], send_sems.at[4 * w + i], recv_sems.at[4 * w + i], (x, y, 1 - c))
                cp.wait_send()
                cp.wait_recv()

    out = pl.pallas_call(
        body, name=name, out_shape=tuple(_like(a) for a in grads + lands),
        in_specs=[_HBM] * (2 * n) + [_SEM, _SEM] + [_ANY] * len(_afters(after)),
        out_specs=(_HBM,) * (2 * n), input_output_aliases={i: i for i in range(2 * n)},
        compiler_params=_SIDE)(*grads, *lands, send, recv, *_afters(after))
    return list(out[:n]), list(out[n:])


def pair_sum(name, grad, land, core):
    _, r, c = grad.shape
    tr = r
    if r % 8 == 0:
        tr = max(8, min(r, 4 * ADAMW_TILE_ELEMS // c) // 8 * 8)
        while r % tr:
            tr -= 8

    def body(core_ref, a_ref, b_ref, o_ref):
        o_ref[...] = (a_ref[...].astype(F32) + b_ref[...].astype(F32)).astype(o_ref.dtype)

    return pl.pallas_call(
        body, name=name, out_shape=jax.ShapeDtypeStruct((4, r, c), grad.dtype),
        grid_spec=pltpu.PrefetchScalarGridSpec(
            num_scalar_prefetch=1, grid=(4, r // tr),
            in_specs=[pl.BlockSpec((None, None, tr, c), lambda i, j, core_ref: (i, core_ref[0], j, 0)),
                      pl.BlockSpec((None, tr, c), lambda i, j, core_ref: (i, j, 0))],
            out_specs=pl.BlockSpec((None, tr, c), lambda i, j, core_ref: (i, j, 0))),
        compiler_params=_params("parallel", "parallel"))(core, grad.reshape(4, 2, r, c), land)


def rs_ici_start(name, sums):
    n = len(sums)
    lands = [lax.empty(a.shape, a.dtype) for a in sums]

    def body(*refs):
        ins, lnd, send_sems, recv_sems, token = refs[:n], refs[n:2 * n], refs[2 * n], refs[2 * n + 1], refs[4 * n + 2]
        x, y, c = _position()
        chip = 2 * x + y
        for w in range(n):
            for j, other in enumerate(_other_chips(x, y)):
                _rcopy(ins[w].at[2 * other[0] + other[1]], lnd[w].at[chip], send_sems.at[3 * w + j], recv_sems.at[3 * w + j], (*other, c)).start()
        token[...] = jnp.zeros_like(token)

    out = pl.pallas_call(
        body, name=name, out_shape=(_dma_sems(3 * n), _dma_sems(3 * n)) + tuple(_like(a) for a in sums + lands) + (_TOKEN,),
        in_specs=[_HBM] * (2 * n), out_specs=(_SEM, _SEM) + (_HBM,) * (2 * n) + (_VM,),
        input_output_aliases={i: 2 + i for i in range(2 * n)}, compiler_params=_SIDE)(*[_hbm(a) for a in sums + lands])
    _TOKENS.push(out[-1])
    return out[0], out[1], list(out[2:2 + n]), list(out[2 + n:2 + 2 * n])


def rs_ici_wait(name, started, after):
    send, recv, sums, lands = started
    n = len(sums)

    def body(*refs):
        ins, lnd, send_sems, recv_sems = refs[:n], refs[n:2 * n], refs[2 * n], refs[2 * n + 1]
        x, y, c = _position()
        for w in range(n):
            for j, other in enumerate(_other_chips(x, y)):
                cp = _rcopy(ins[w].at[2 * other[0] + other[1]], lnd[w].at[2 * other[0] + other[1]], send_sems.at[3 * w + j], recv_sems.at[3 * w + j], (*other, c))
                cp.wait_send()
                cp.wait_recv()

    out = pl.pallas_call(
        body, name=name, out_shape=tuple(_like(a) for a in sums + lands), in_specs=[_HBM] * (2 * n) + [_SEM, _SEM, _ANY],
        out_specs=(_HBM,) * (2 * n), input_output_aliases={i: i for i in range(2 * n)}, compiler_params=_SIDE)(*sums, *lands, send, recv, after)
    chip = 2 * lax.axis_index("x") + lax.axis_index("y")
    return [lax.dynamic_update_index_in_dim(land, lax.dynamic_index_in_dim(s, chip, 0, keepdims=False), chip, 0)
            for s, land in zip(out[:n], out[n:])]


def ada_fwd(c, w_ada, b_ada3, conv_w):
    d, cs = w_ada.shape

    def body(c_ref, w_ref, b_ref, cw_ref, mod_ref, sc_ref, cwa_ref, part_ref, send_sems, recv_sems):
        me = _position()
        my = _index(me)
        cv = c_ref[...]
        sc_ref[my] = cv * _sigmoid(cv)
        cwa_ref[my] = cw_ref[...]
        gather = []
        for r in range(1, N_DEV):
            for k, ref in enumerate((sc_ref, cwa_ref)):
                cp = pltpu.make_async_remote_copy(src_ref=ref.at[my], dst_ref=ref.at[my], send_sem=send_sems.at[14 * k + r - 1],
                                                  recv_sem=recv_sems.at[14 * k + r - 1], device_id=_peer(me, r), device_id_type=MESH)
                cp.start()
                gather.append(cp)
        for cp in gather:
            cp.wait()
        sc_all = jnp.concatenate([sc_ref[k] for k in range(N_DEV)], axis=0).astype(BF16)
        part = jnp.dot(sc_all, w_ref[...].astype(BF16), preferred_element_type=F32)
        for k in range(N_DEV):
            part_ref[k] = part[k:k + 1, :]
        scatter = []
        for r in range(1, N_DEV):
            peer = _peer(me, r)
            cp = pltpu.make_async_remote_copy(src_ref=part_ref.at[_index(peer)], dst_ref=mod_ref.at[my], send_sem=send_sems.at[6 + r],
                                              recv_sem=recv_sems.at[6 + r], device_id=peer, device_id_type=MESH)
            cp.start()
            scatter.append(cp)
        mod_ref[my] = part_ref[my]
        for cp in scatter:
            cp.wait()
        mod_ref[...] = mod_ref[...] + b_ref[...]

    vm = pl.BlockSpec(memory_space=pltpu.VMEM)
    return pl.pallas_call(
        body, name="ada_fwd",
        out_shape=[jax.ShapeDtypeStruct((N_DEV, 1, cs), F32), jax.ShapeDtypeStruct((N_DEV, 1, d), F32),
                   jax.ShapeDtypeStruct((N_DEV,) + conv_w.shape, F32)],
        in_specs=[vm, vm, vm, vm], out_specs=[vm, vm, vm],
        scratch_shapes=[pltpu.VMEM((N_DEV, 1, cs), F32), pltpu.SemaphoreType.DMA((21,)), pltpu.SemaphoreType.DMA((21,))],
        compiler_params=pltpu.CompilerParams(vmem_limit_bytes=VMEM_LIMIT_BYTES))(c, w_ada, b_ada3, conv_w)


def ada_bwd_w(sc_all, dmod_cols):
    _, d = sc_all.shape
    cs = dmod_cols.shape[1]
    tr = _tile(d, ROW_TILE)

    def body(sc_ref, dm_ref, o_ref):
        dm = dm_ref[...].astype(BF16)
        o_ref[...] = lax.dot_general(sc_ref[...].astype(BF16), dm, (((0,), (0,)), ((), ())), preferred_element_type=F32)

    return pl.pallas_call(body, name="ada_bwd_w", grid=(d // tr,),
                          in_specs=[pl.BlockSpec((N_DEV, tr), lambda i: (0, i)), _full((N_DEV, cs))],
                          out_specs=pl.BlockSpec((None, tr, cs), lambda i: (0, i, 0)),
                          out_shape=jax.ShapeDtypeStruct((1, d, cs), F32), compiler_params=_params("parallel"))(sc_all, dmod_cols)


def _round_up(n, m):
    return (n + m - 1) // m * m


def kernel(x, c, positions, w_ada, b_ada, pre_norm1_g, w_in, gm_ln_g, gm_ln_b, gm_w_s, gm_b_s, w_branch_a, q_norm_g, w_uq, kv_norm_g, w_ukv, w_branch_b, w_out, post_norm1_g, pre_norm2_g, w_up, conv_w, conv_b, w_down, post_norm2_g, loss_target, m_w_ada, m_b_ada, m_pre_norm1_g, m_w_in, m_gm_ln_g, m_gm_ln_b, m_gm_w_s, m_gm_b_s, m_w_branch_a, m_q_norm_g, m_w_uq, m_kv_norm_g, m_w_ukv, m_w_branch_b, m_w_out, m_post_norm1_g, m_pre_norm2_g, m_w_up, m_conv_w, m_conv_b, m_w_down, m_post_norm2_g, v_w_ada, v_b_ada, v_pre_norm1_g, v_w_in, v_gm_ln_g, v_gm_ln_b, v_gm_w_s, v_gm_b_s, v_w_branch_a, v_q_norm_g, v_w_uq, v_kv_norm_g, v_w_ukv, v_w_branch_b, v_w_out, v_post_norm1_g, v_pre_norm2_g, v_w_up, v_conv_w, v_conv_b, v_w_down, v_post_norm2_g):
    weights = dict(w_ada=w_ada, b_ada=b_ada, pre_norm1_g=pre_norm1_g, w_in=w_in, gm_ln_g=gm_ln_g, gm_ln_b=gm_ln_b, gm_w_s=gm_w_s,
                   gm_b_s=gm_b_s, w_branch_a=w_branch_a, q_norm_g=q_norm_g, w_uq=w_uq, kv_norm_g=kv_norm_g, w_ukv=w_ukv,
                   w_branch_b=w_branch_b, w_out=w_out, post_norm1_g=post_norm1_g, pre_norm2_g=pre_norm2_g, w_up=w_up, conv_w=conv_w,
                   conv_b=conv_b, w_down=w_down, post_norm2_g=post_norm2_g)
    mom1 = dict(w_ada=m_w_ada, b_ada=m_b_ada, pre_norm1_g=m_pre_norm1_g, w_in=m_w_in, gm_ln_g=m_gm_ln_g, gm_ln_b=m_gm_ln_b,
                gm_w_s=m_gm_w_s, gm_b_s=m_gm_b_s, w_branch_a=m_w_branch_a, q_norm_g=m_q_norm_g, w_uq=m_w_uq, kv_norm_g=m_kv_norm_g,
                w_ukv=m_w_ukv, w_branch_b=m_w_branch_b, w_out=m_w_out, post_norm1_g=m_post_norm1_g, pre_norm2_g=m_pre_norm2_g,
                w_up=m_w_up, conv_w=m_conv_w, conv_b=m_conv_b, w_down=m_w_down, post_norm2_g=m_post_norm2_g)
    mom2 = dict(w_ada=v_w_ada, b_ada=v_b_ada, pre_norm1_g=v_pre_norm1_g, w_in=v_w_in, gm_ln_g=v_gm_ln_g, gm_ln_b=v_gm_ln_b,
                gm_w_s=v_gm_w_s, gm_b_s=v_gm_b_s, w_branch_a=v_w_branch_a, q_norm_g=v_q_norm_g, w_uq=v_w_uq, kv_norm_g=v_kv_norm_g,
                w_ukv=v_w_ukv, w_branch_b=v_w_branch_b, w_out=v_w_out, post_norm1_g=v_post_norm1_g, pre_norm2_g=v_pre_norm2_g,
                w_up=v_w_up, conv_w=v_conv_w, conv_b=v_conv_b, w_down=v_w_down, post_norm2_g=v_post_norm2_g)
    order = list(weights)
    _TOKENS.clear()

    s, d = x.shape[1], x.shape[2]
    gmw = gm_ln_g.shape[0]
    groups = gmw // CHUNK
    ql, kvl = q_norm_g.shape[0], kv_norm_g.shape[0]
    f2 = conv_b.shape[0]
    in_cols = w_in.shape[1] * N_DEV
    o_q, o_kv, o_ga, o_gb, o_kpe = 2 * gmw, 2 * gmw + ql, 2 * gmw + ql + kvl, 2 * gmw + ql + kvl + d, 2 * gmw + ql + kvl + 2 * d
    zp = _round_up(o_kpe + LANES, Z_PAD)
    src_kpe = 2 * gmw + ql + kvl
    assert src_kpe + QK_ROPE + 2 * d == in_cols
    my = 4 * lax.axis_index("x") + 2 * lax.axis_index("y") + lax.axis_index("c")

    x2, tgt = x[0], loss_target[0]
    row = lambda a: a.reshape(1, -1)

    big = ["w_in", "w_branch_a", "w_uq", "w_ukv", "w_branch_b", "w_out", "w_up", "w_down"]
    sh = {k: weights[k].astype(BF16) for k in big[1:]}
    mix = ["w_branch_a", "w_uq", "w_ukv", "w_branch_b", "w_out"]
    ag_in = ag_start("ag_start_in", [w_in.T.astype(BF16)], c)

    mod8, sc_all3, g_cw = ada_fwd(c, w_ada, b_ada.reshape(N_DEV, 1, -1), conv_w)
    mod = mod8.reshape(N_MOD, d)
    shift1, scale1, gate1, shift2, scale2, gate2 = (mod[i:i + 1] for i in range(N_MOD))
    sc_all = sc_all3.reshape(N_DEV, d)
    h1 = norm_mod_fwd("pre1_fwd", x2, row(pre_norm1_g), scale1, shift1)

    inv = ROPE_THETA ** (-jnp.arange(0, QK_ROPE, 2, dtype=F32) / QK_ROPE)
    ang = positions[0].astype(F32)[:, None] * inv
    cos4 = jnp.tile(jnp.cos(ang), (1, 4))
    sin4 = jnp.tile(jnp.concatenate([-jnp.sin(ang), jnp.sin(ang)], axis=1), (1, 2))

    wm = (gm_w_s * jnp.tril(jnp.ones((CHUNK, CHUNK), F32))).astype(BF16)
    bs3 = gm_b_s.reshape(groups, CHUNK, 1)
    ln_g, ln_b = row(gm_ln_g), row(gm_ln_b)

    small_names = ["pre_norm1_g", "gm_ln_g", "gm_ln_b", "gm_b_s", "q_norm_g", "kv_norm_g", "post_norm1_g", "pre_norm2_g", "conv_b",
                   "post_norm2_g", "gm_w_s", "b_ada"]
    n_small_early = sum(weights[k].size for k in small_names)
    n_pack_early = _round_up(n_small_early + 3 * f2, PACK_ALIGN)

    def pack(src):
        return jnp.concatenate([src[k].reshape(-1) for k in small_names] + [jnp.zeros((n_pack_early - n_small_early,), F32)]).reshape(-1, LANES)

    packed_state = [pack(weights), pack(mom1), pack(mom2)]

    early = [h1, cos4, sin4, wm] + [sh[k] for k in big[1:]] + packed_state
    ag_in = ag_forward("ag_forward_in", ag_in, early)
    ag_mix = ag_start("ag_start_mix", [sh[k] for k in mix], _TOKENS.pending[-1])
    (g_in,) = ag_wait("ag_wait_in", ag_in, [h1, _TOKENS.pending[-1]])
    w_in_f = g_in.reshape(in_cols, d)
    w_in_p = jnp.concatenate([w_in_f[:src_kpe], w_in_f[src_kpe + QK_ROPE:], w_in_f[src_kpe:src_kpe + QK_ROPE],
                              jnp.zeros((zp - in_cols, d), BF16)], axis=0)

    z = mm_nt("z_proj", h1, w_in_p, ACT)
    ag_mix = ag_forward("ag_forward_mix", ag_mix, z)
    ag_up = ag_start("ag_start_up", [sh["w_up"]], _TOKENS.pending[-1])
    a = gmlp_fwd(z, gmw, ln_g, ln_b, wm, bs3)
    g_a, g_uq, g_ukv, g_b, g_out = ag_wait("ag_wait_mix", ag_mix, [a, _TOKENS.pending[-1]])
    w_a_f, w_b_f, w_out_f = g_a.reshape(-1, d), g_b.reshape(-1, d), g_out.reshape(-1, d)
    w_uq_f = g_uq.transpose(1, 0, 2).reshape(ql, HEADS, QK_NOPE + QK_ROPE)
    w_uq_n = w_uq_f[:, :, :QK_NOPE].reshape(ql, HEADS * QK_NOPE)
    w_uq_r = w_uq_f[:, :, QK_NOPE:].reshape(ql, HEADS * QK_ROPE)
    y_a = mm_nn("branch_a", a, w_a_f, ACT)
    qln = rms_fwd_cols("q_norm", z, o_q, ql, row(q_norm_g))
    kvn = rms_fwd_cols("kv_norm", z, o_kv, kvl, row(kv_norm_g))
    qn = mm_nn("q_nope", qln, w_uq_n, BF16)
    qp = mm_nn("q_rope", qln, w_uq_r, F32)
    kv = mm_nn_b3("kv_up", kvn, g_ukv, BF16)
    kpr = rope_k(z, o_kpe, cos4, sin4)
    o, qpr, lse = attn_fwd3(qn, qp, kv, kpr, cos4, sin4)
    ag_up = ag_forward("ag_forward_up", ag_up, o)
    ag_down = ag_start("ag_start_down", [sh["w_down"]], _TOKENS.pending[-1])
    y_b = mm_nn("branch_b", o, w_b_f, ACT)
    merged = merge_fwd(z, o_ga, o_gb, y_a, y_b)
    y1 = mm_nn("out_proj", merged, w_out_f, ACT)
    x1 = post_res_fwd("post1_fwd", x2, y1, gate1, row(post_norm1_g))
    h2 = norm_mod_fwd("pre2_fwd", x1, row(pre_norm2_g), scale2, shift2)
    (g_up,) = ag_wait("ag_wait_up", ag_up, h2)
    upre = mm_nn_b3("up_proj", h2, g_up, ACT)
    ag_down = ag_forward("ag_forward_down", ag_down, upre)
    cw = g_cw.transpose(1, 0, 2).reshape(3, f2)
    cb = row(conv_b)
    f = conv_act_fwd(upre, cw, cb)
    w_down_f = ag_wait("ag_wait_down", ag_down, f)[0].reshape(-1, d)
    ffn = mm_nn("down_proj", f, w_down_f, ACT)
    loss_acc, dout, dffn, acc2 = post2_loss_bwd(x1, ffn, tgt, gate2, row(post_norm2_g))
    loss = lax.psum(loss_acc[0, 0], ("x", "y", "c"))
    _TOKENS.push(jnp.broadcast_to(loss, (8, LANES)))

    blocks = lambda g: g.reshape(N_DEV, g.shape[0] // N_DEV, g.shape[1])
    core = lax.axis_index("c").astype(jnp.int32).reshape(1)
    rs = {}

    def rs_begin(key, grads):
        rs[key] = rs_d2d_start("rs_d2d_start_" + key, grads)

    def rs_middle(key, after):
        grads, lands = rs_d2d_wait("rs_d2d_wait_" + key, rs[key], after)
        sums = [pair_sum("pair_sum_%s_%d" % (key, i), g, l, core) for i, (g, l) in enumerate(zip(grads, lands))]
        rs[key] = rs_ici_start("rs_ici_start_" + key, sums)

    gw_down = mm_tn("g_w_down", f, dffn, BF16)
    rs_begin("down", [blocks(gw_down)])
    df = mm_nt("d_f", dffn, w_down_f, ACT)
    rs_middle("down", df)
    dupre, gcw_g, gcw_v, gcb_g, gcb_v = conv_act_bwd(upre, cw, cb, df)
    gw_up3 = mm_tn_h3("g_w_up", h2, dupre, N_DEV, BF16)
    rs_begin("up", [gw_up3])
    dh2 = mm_nt_h3("d_h2", dupre, g_up, ACT)
    rs_middle("up", dh2)
    dx1, dy1, acc_mid = mid_bwd(dh2, dout, x1, y1, row(pre_norm2_g), scale2, gate1, row(post_norm1_g))
    gw_out = mm_tn("g_w_out", merged, dy1, BF16)
    dmerged = mm_nt("d_merged", dy1, w_out_f, ACT)
    dya, dyb, dga, dgb = merge_bwd(z, o_ga, o_gb, y_a, y_b, dmerged)
    gw_a = mm_tn("g_w_a", a, dya, BF16)
    gw_b = mm_tn("g_w_b", o, dyb, BF16)
    rs_begin("mid", [blocks(gw_out), blocks(gw_a), blocks(gw_b)])
    da = mm_nt("d_a", dya, w_a_f, ACT)
    do = mm_nt("d_o", dyb, w_b_f, ACT)
    rs_middle("mid", do)
    duv, g_ws, g_bs3, acc_gm = gmlp_bwd(z, gmw, da, ln_g, ln_b, wm, bs3)
    dqn, dqp, dkv, dkp = attn_bwd3(qn, qpr, kv, kpr, o, do, lse, cos4, sin4)
    dkpe = kpe_bwd(dkp, cos4, sin4, zp - o_kpe)
    dq_cat = jnp.concatenate([dqn, dqp], axis=1)
    w_uq_cat = jnp.concatenate([w_uq_n, w_uq_r], axis=1)
    gw_uq_cat = mm_tn("g_w_uq", qln, dq_cat, BF16)
    gw_uq_f = jnp.concatenate([gw_uq_cat[:, :HEADS * QK_NOPE].reshape(ql, HEADS, QK_NOPE),
                               gw_uq_cat[:, HEADS * QK_NOPE:].reshape(ql, HEADS, QK_ROPE)], axis=2)
    gw_uq3 = gw_uq_f.reshape(ql, N_DEV, -1).transpose(1, 0, 2)
    gw_ukv3 = mm_tn_o3("g_w_ukv", kvn, dkv, N_DEV, BF16)
    rs_begin("mla", [gw_uq3, gw_ukv3])
    dqln = mm_nt("d_qln", dq_cat, w_uq_cat, ACT)
    dq_lat, g_qnorm = rms_bwd_cols("q_norm_bwd", dqln, z, o_q, ql, row(q_norm_g))
    dkvn = mm_nt_b3("d_kvn", dkv, g_ukv, ACT)
    rs_middle("mla", dkvn)
    dkv_lat, g_kvnorm = rms_bwd_cols("kv_norm_bwd", dkvn, z, o_kv, kvl, row(kv_norm_g))
    dz = jnp.concatenate([duv, dq_lat, dkv_lat, dga, dgb, dkpe], axis=1)
    gw_in_p = mm_tn("g_w_in", dz, h1, BF16)
    gw_in_f = jnp.concatenate([gw_in_p[:src_kpe], gw_in_p[o_kpe:o_kpe + QK_ROPE], gw_in_p[src_kpe:o_kpe]], axis=0)
    rs_begin("in", [gw_in_f.reshape(N_DEV, -1, d)])
    dh1 = mm_nn("d_h1", dz, w_in_p, ACT)
    grad_x, acc1 = pre1_bwd(dh1, dx1, x2, row(pre_norm1_g), scale1)

    dmod = jnp.concatenate([acc1[0], acc1[1], acc_mid[3], acc_mid[0], acc_mid[1], acc2[0]])
    small = [("pre_norm1_g", acc1[2]), ("gm_ln_g", acc_gm[0]), ("gm_ln_b", acc_gm[1]), ("gm_b_s", g_bs3.reshape(-1)),
             ("q_norm_g", g_qnorm[0]), ("kv_norm_g", g_kvnorm[0]), ("post_norm1_g", acc_mid[4]), ("pre_norm2_g", acc_mid[2]),
             ("conv_b", jnp.concatenate([gcb_g[0], gcb_v[0]])), ("post_norm2_g", acc2[1]), ("gm_w_s", g_ws.reshape(-1)),
             ("b_ada", dmod)]
    n_small = sum(v.shape[0] for _, v in small)
    n_cw = 3 * f2
    n_pack = _round_up(n_small + n_cw, PACK_ALIGN)
    tail = jnp.zeros((n_pack - n_small - n_cw,), F32)
    packed = jnp.concatenate([v for _, v in small] + [jnp.concatenate([gcw_g, gcw_v], axis=1).reshape(-1), tail])
    ag_small = ag_start("ag_start_small", [packed.reshape(-1, LANES)], packed)
    rs_middle("in", [packed, _TOKENS.pending[-1]])

    res = {}
    last = packed
    for key, names in (("down", ["w_down"]), ("up", ["w_up"]), ("mid", ["w_out", "w_branch_a", "w_branch_b"]), ("mla", ["w_uq", "w_ukv"])):
        parts = rs_ici_wait("rs_ici_wait_" + key, rs[key], last)
        for k, p in zip(names, parts):
            res[k] = adamw("adamw_" + k, weights[k], mom1[k], mom2[k], p)
            last = res[k][0]

    assert [k for k, _ in small] == small_names and n_small == n_small_early
    (gathered,) = ag_wait("ag_wait_small", ag_forward("ag_forward_small", ag_small, last), last)
    sm = [t.reshape(-1) for t in adamw("adamw_small", *packed_state, gathered)]
    off = 0
    for k, v in small:
        res[k] = tuple(t[off:off + v.shape[0]].reshape(weights[k].shape) for t in sm)
        off += v.shape[0]

    cs_cw = conv_w.shape[1]
    g_cw_full = sm[0][n_small:n_small + n_cw].reshape(3, f2)
    g_cw_mine = lax.dynamic_slice(g_cw_full, (0, my * cs_cw), (3, cs_cw))
    res["conv_w"] = adamw("adamw_conv_w", conv_w, mom1["conv_w"], mom2["conv_w"], g_cw_mine[None])

    cs_ada = w_ada.shape[1]
    off_b = n_small - N_MOD * d
    dmod_all = gathered.reshape(N_DEV, -1)[:, off_b:off_b + N_MOD * d]
    dmod_cols = lax.dynamic_slice(dmod_all, (0, my * cs_ada), (N_DEV, cs_ada))
    res["w_ada"] = adamw("adamw_w_ada", w_ada, mom1["w_ada"], mom2["w_ada"], ada_bwd_w(sc_all, dmod_cols))

    (p_in,) = rs_ici_wait("rs_ici_wait_in", rs["in"], res["w_ada"][0])
    res["w_in"] = tuple(t.T for t in adamw("adamw_w_in", w_in.T, mom1["w_in"].T, mom2["w_in"].T, p_in))

    _TOKENS.clear()
    outs = [loss, grad_x[None]]
    for i in range(4):
        outs += [res[k][i] for k in order]
    return tuple(outs)
```

```python
import functools

import jax
import jax.numpy as jnp
from jax import lax
from jax.experimental import pallas as pl
from jax.experimental.pallas import tpu as pltpu

F32 = jnp.float32
BF16 = jnp.bfloat16
ACT = BF16

N_DEV = 8
HEADS = 16
QK_NOPE = 128
QK_ROPE = 64
V_HEAD = 128
CHUNK = 128
ROPE_THETA = 10000.0
EPS = 1e-6
N_MOD = 6
ADAM_LR, ADAM_B1, ADAM_B2, ADAM_EPS, ADAM_WD, ADAM_STEP = 0.001, 0.9, 0.999, 1e-08, 0.01, 10

LANES = 128
VMEM_LIMIT_BYTES = 48 * 2 ** 20
ROW_TILE = 256
COL_TILE = 256
ATT_TILE = 512
Z_PAD = 512
ADAMW_TILE_ELEMS = 1 << 18
PACK_ALIGN = 8 * LANES
MESH = pl.DeviceIdType.MESH


def _params(*sem):
    return pltpu.CompilerParams(dimension_semantics=sem if sem else None, vmem_limit_bytes=VMEM_LIMIT_BYTES)


def _tile(dim, target):
    t = (min(dim, target) // LANES) * LANES
    while t >= LANES:
        if dim % t == 0:
            return t
        t -= LANES
    return dim


def _full(shape):
    nd = len(shape)
    return pl.BlockSpec(shape, lambda *_: (0,) * nd)


class _Tokens:
    KEEP = 2

    def __init__(self):
        self.pending = []

    def push(self, token):
        self.pending = (self.pending + [token])[-self.KEEP:]

    def take(self):
        return list(self.pending)

    def clear(self):
        self.pending = []


_TOKENS = _Tokens()


def _matmul(name, a, b, *, grid, a_spec, b_spec, o_spec, out_shape, contract, acc_shape, split=1):
    nk = grid[2]
    deps = _TOKENS.take()

    def product(a_ref, b_ref):
        if len(b_ref.shape) == 2:
            return lax.dot_general(a_ref[...].astype(BF16), b_ref[...].astype(BF16), (contract, ((), ())), preferred_element_type=F32)
        cs = b_ref.shape[2]
        return sum(lax.dot_general(a_ref[:, s * cs:(s + 1) * cs].astype(BF16), b_ref[s].astype(BF16), (contract, ((), ())),
                                   preferred_element_type=F32) for s in range(split))

    def body_one_step(a_ref, b_ref, *rest):
        o_ref = rest[len(deps)]
        o_ref[...] = product(a_ref, b_ref).astype(o_ref.dtype)

    def body(a_ref, b_ref, *rest):
        o_ref, acc_ref = rest[len(deps):]
        k = pl.program_id(2)

        @pl.when(k == 0)
        def _():
            acc_ref[...] = jnp.zeros_like(acc_ref)

        acc_ref[...] += product(a_ref, b_ref)

        @pl.when(k == nk - 1)
        def _():
            o_ref[...] = acc_ref[...].astype(o_ref.dtype)

    return pl.pallas_call(
        body_one_step if nk == 1 else body, name=name, grid=grid,
        in_specs=[a_spec, b_spec] + [pl.BlockSpec(memory_space=pl.ANY)] * len(deps),
        out_specs=o_spec, out_shape=out_shape, scratch_shapes=[] if nk == 1 else [pltpu.VMEM(acc_shape, F32)],
        compiler_params=_params("parallel", "parallel", "arbitrary"))(a, b, *deps)


TM, TN, TK = 1408, 1408, 2816


def _tk(a, b):
    return TK if a.dtype == BF16 and b.dtype == BF16 else TK // 2


def mm_nn(name, a, b, dtype):
    (m, k), n = a.shape, b.shape[1]
    tm, tn, tk = _tile(m, TM), _tile(n, TN), _tile(k, _tk(a, b))
    return _matmul(name, a, b, grid=(m // tm, n // tn, k // tk),
                   a_spec=pl.BlockSpec((tm, tk), lambda i, j, kk: (i, kk)),
                   b_spec=pl.BlockSpec((tk, tn), lambda i, j, kk: (kk, j)),
                   o_spec=pl.BlockSpec((tm, tn), lambda i, j, kk: (i, j)),
                   out_shape=jax.ShapeDtypeStruct((m, n), dtype), contract=((1,), (0,)), acc_shape=(tm, tn))


def mm_nn_b3(name, a, b3, dtype):
    (m, k), (nj, _, cs) = a.shape, b3.shape
    tm, tk = _tile(m, TM), _tile(k, _tk(a, b3))
    return _matmul(name, a, b3, grid=(m // tm, nj, k // tk),
                   a_spec=pl.BlockSpec((tm, tk), lambda i, j, kk: (i, kk)),
                   b_spec=pl.BlockSpec((None, tk, cs), lambda i, j, kk: (j, kk, 0)),
                   o_spec=pl.BlockSpec((tm, cs), lambda i, j, kk: (i, j)),
                   out_shape=jax.ShapeDtypeStruct((m, nj * cs), dtype), contract=((1,), (0,)), acc_shape=(tm, cs))


def mm_nt(name, a, b, dtype):
    (m, k), n = a.shape, b.shape[0]
    tm, tn, tk = _tile(m, TM), _tile(n, TN), _tile(k, _tk(a, b))
    return _matmul(name, a, b, grid=(m // tm, n // tn, k // tk),
                   a_spec=pl.BlockSpec((tm, tk), lambda i, j, kk: (i, kk)),
                   b_spec=pl.BlockSpec((tn, tk), lambda i, j, kk: (j, kk)),
                   o_spec=pl.BlockSpec((tm, tn), lambda i, j, kk: (i, j)),
                   out_shape=jax.ShapeDtypeStruct((m, n), dtype), contract=((1,), (1,)), acc_shape=(tm, tn))


def mm_nt_b3(name, a, b3, dtype):
    m, (nj, n, cs) = a.shape[0], b3.shape
    tm, tn = _tile(m, TM), _tile(n, TN)
    return _matmul(name, a, b3, grid=(m // tm, n // tn, nj),
                   a_spec=pl.BlockSpec((tm, cs), lambda i, j, kk: (i, kk)),
                   b_spec=pl.BlockSpec((None, tn, cs), lambda i, j, kk: (kk, j, 0)),
                   o_spec=pl.BlockSpec((tm, tn), lambda i, j, kk: (i, j)),
                   out_shape=jax.ShapeDtypeStruct((m, n), dtype), contract=((1,), (1,)), acc_shape=(tm, tn))


def mm_nt_h3(name, a3, b3, dtype):
    (_, m, _), (nj, n, cs) = a3.shape, b3.shape
    tm, tn, hj = _tile(m, TM), _tile(n, TN), nj // 2
    pair = 2 if hj % 2 == 0 else 1
    return _matmul(name, a3, b3.reshape(nj // pair, pair, n, cs), grid=(m // tm, n // tn, nj // pair),
                   a_spec=pl.BlockSpec((None, tm, pair * cs), lambda i, j, kk: (kk // (hj // pair), i, kk % (hj // pair))),
                   b_spec=pl.BlockSpec((None, pair, tn, cs), lambda i, j, kk: (kk, 0, j, 0)),
                   o_spec=pl.BlockSpec((tm, tn), lambda i, j, kk: (i, j)),
                   out_shape=jax.ShapeDtypeStruct((m, n), dtype), contract=((1,), (1,)), acc_shape=(tm, tn), split=pair)


def mm_tn_h3(name, a, b3, nj, dtype):
    (k, m), half = a.shape, b3.shape[2]
    hj = nj // 2
    cs = half // hj
    tm, tk = _tile(m, TM), _tile(k, _tk(a, b3))
    return _matmul(name, a, b3, grid=(m // tm, nj, k // tk),
                   a_spec=pl.BlockSpec((tk, tm), lambda i, j, kk: (kk, i)),
                   b_spec=pl.BlockSpec((None, tk, cs), lambda i, j, kk: (j // hj, kk, j % hj)),
                   o_spec=pl.BlockSpec((None, tm, cs), lambda i, j, kk: (j, i, 0)),
                   out_shape=jax.ShapeDtypeStruct((nj, m, cs), dtype), contract=((0,), (0,)), acc_shape=(tm, cs))


def mm_tn(name, a, b, dtype):
    (k, m), n = a.shape, b.shape[1]
    tm, tn, tk = _tile(m, TM), _tile(n, TN), _tile(k, _tk(a, b))
    return _matmul(name, a, b, grid=(m // tm, n // tn, k // tk),
                   a_spec=pl.BlockSpec((tk, tm), lambda i, j, kk: (kk, i)),
                   b_spec=pl.BlockSpec((tk, tn), lambda i, j, kk: (kk, j)),
                   o_spec=pl.BlockSpec((tm, tn), lambda i, j, kk: (i, j)),
                   out_shape=jax.ShapeDtypeStruct((m, n), dtype), contract=((0,), (0,)), acc_shape=(tm, tn))


def mm_tn_o3(name, a, b, nj, dtype):
    (k, m), n = a.shape, b.shape[1]
    cs = n // nj
    tm, tk = _tile(m, TM), _tile(k, _tk(a, b))
    return _matmul(name, a, b, grid=(m // tm, nj, k // tk),
                   a_spec=pl.BlockSpec((tk, tm), lambda i, j, kk: (kk, i)),
                   b_spec=pl.BlockSpec((tk, cs), lambda i, j, kk: (kk, j)),
                   o_spec=pl.BlockSpec((None, tm, cs), lambda i, j, kk: (j, i, 0)),
                   out_shape=jax.ShapeDtypeStruct((nj, m, cs), dtype), contract=((0,), (0,)), acc_shape=(tm, cs))


_GELU_C = 0.7978845608028654
_GELU_A = 0.044715


def _f32(ref):
    return ref[...].astype(F32)


def _gelu(x):
    x = x.astype(F32)
    return 0.5 * x * (1.0 + jnp.tanh(_GELU_C * (x + _GELU_A * x * x * x)))


def _gelu_and_grad(x):
    x = x.astype(F32)
    t = jnp.tanh(_GELU_C * (x + _GELU_A * x * x * x))
    y = 0.5 * x * (1.0 + t)
    dy = 0.5 * (1.0 + t) + 0.5 * x * (1.0 - t * t) * (_GELU_C * (1.0 + 3.0 * _GELU_A * x * x))
    return y, dy


def _sigmoid(x):
    return 1.0 / (1.0 + jnp.exp(-x.astype(F32)))


def _rms_stats(x):
    x = x.astype(F32)
    inv = lax.rsqrt(jnp.mean(x * x, axis=-1, keepdims=True) + EPS)
    return inv, x * inv


def _rms_bwd(dyhat, yhat, inv):
    return inv * (dyhat - yhat * jnp.mean(dyhat * yhat, axis=-1, keepdims=True))


def _colsum(x):
    return jnp.sum(x, axis=0, keepdims=True)


def _rope(x, cos4, sin4):
    lane = lax.broadcasted_iota(jnp.int32, x.shape, x.ndim - 1)
    first_half = (lane % QK_ROPE) < (QK_ROPE // 2)
    partner = jnp.where(first_half, pltpu.roll(x, LANES - QK_ROPE // 2, x.ndim - 1), pltpu.roll(x, QK_ROPE // 2, x.ndim - 1))
    return x * cos4 + partner * sin4


def norm_mod_fwd(name, x, g, scale, shift):
    s, d = x.shape
    tr = _tile(s, ROW_TILE)

    def body(x_ref, g_ref, sc_ref, sh_ref, o_ref):
        _, xh = _rms_stats(x_ref[...])
        o_ref[...] = (xh * g_ref[...] * (1.0 + sc_ref[...]) + sh_ref[...]).astype(o_ref.dtype)

    row = pl.BlockSpec((tr, d), lambda i: (i, 0))
    vec = pl.BlockSpec((1, d), lambda i: (0, 0))
    return pl.pallas_call(body, name=name, grid=(s // tr,), in_specs=[row, vec, vec, vec], out_specs=row,
                          out_shape=jax.ShapeDtypeStruct((s, d), BF16), compiler_params=_params("parallel"))(x, g, scale, shift)


def rms_fwd_cols(name, z, off, width, g):
    s = z.shape[0]
    tr = _tile(s, ROW_TILE)
    assert off % width == 0

    def body(x_ref, g_ref, o_ref):
        _, xh = _rms_stats(x_ref[...])
        o_ref[...] = (xh * g_ref[...]).astype(o_ref.dtype)

    return pl.pallas_call(body, name=name, grid=(s // tr,),
                          in_specs=[pl.BlockSpec((tr, width), lambda i: (i, off // width)), pl.BlockSpec((1, width), lambda i: (0, 0))],
                          out_specs=pl.BlockSpec((tr, width), lambda i: (i, 0)),
                          out_shape=jax.ShapeDtypeStruct((s, width), BF16), compiler_params=_params("parallel"))(z, g)


def rms_bwd_cols(name, dy, z, off, width, g):
    s = z.shape[0]
    tr = _tile(s, ROW_TILE)

    def body(dy_ref, x_ref, g_ref, dx_ref, gg_ref):
        @pl.when(pl.program_id(0) == 0)
        def _():
            gg_ref[...] = jnp.zeros_like(gg_ref)

        inv, xh = _rms_stats(x_ref[...])
        dy_v = _f32(dy_ref)
        gg_ref[...] += _colsum(dy_v * xh)
        dx_ref[...] = _rms_bwd(dy_v * g_ref[...], xh, inv).astype(dx_ref.dtype)

    return pl.pallas_call(body, name=name, grid=(s // tr,),
                          in_specs=[pl.BlockSpec((tr, width), lambda i: (i, 0)), pl.BlockSpec((tr, width), lambda i: (i, off // width)),
                                    pl.BlockSpec((1, width), lambda i: (0, 0))],
                          out_specs=[pl.BlockSpec((tr, width), lambda i: (i, 0)), pl.BlockSpec((1, width), lambda i: (0, 0))],
                          out_shape=[jax.ShapeDtypeStruct((s, width), BF16), jax.ShapeDtypeStruct((1, width), F32)],
                          compiler_params=_params("arbitrary"))(dy, z, g)


def post_res_fwd(name, x, y, gate, g):
    s, d = x.shape
    tr = _tile(s, ROW_TILE)

    def body(x_ref, y_ref, gate_ref, g_ref, o_ref):
        _, yh = _rms_stats(y_ref[...])
        o_ref[...] = x_ref[...] + gate_ref[...] * (yh * g_ref[...])

    row = pl.BlockSpec((tr, d), lambda i: (i, 0))
    vec = pl.BlockSpec((1, d), lambda i: (0, 0))
    return pl.pallas_call(body, name=name, grid=(s // tr,), in_specs=[row, row, vec, vec], out_specs=row,
                          out_shape=jax.ShapeDtypeStruct((s, d), F32), compiler_params=_params("parallel"))(x, y, gate, g)


def post2_loss_bwd(x1, ffn, target, gate2, g):
    s, d = x1.shape
    tr = _tile(s, ROW_TILE)

    def body(x_ref, y_ref, t_ref, gate_ref, g_ref, loss_ref, dout_ref, dy_ref, acc_ref):
        @pl.when(pl.program_id(0) == 0)
        def _():
            loss_ref[...] = jnp.zeros_like(loss_ref)
            acc_ref[...] = jnp.zeros_like(acc_ref)

        inv, yh = _rms_stats(y_ref[...])
        r = yh * g_ref[...]
        err = x_ref[...] + gate_ref[...] * r - t_ref[...]
        loss_ref[...] += 0.5 * jnp.sum(jnp.mean(err * err, axis=-1, keepdims=True))
        dout = err / d
        dout_ref[...] = dout
        dr = dout * gate_ref[...]
        acc_ref[0:1, :] += _colsum(dout * r)
        acc_ref[1:2, :] += _colsum(dr * yh)
        dy_ref[...] = _rms_bwd(dr * g_ref[...], yh, inv).astype(dy_ref.dtype)

    row = pl.BlockSpec((tr, d), lambda i: (i, 0))
    vec = pl.BlockSpec((1, d), lambda i: (0, 0))
    return pl.pallas_call(
        body, name="post2_loss_bwd", grid=(s // tr,), in_specs=[row, row, row, vec, vec],
        out_specs=[_full((8, LANES)), row, row, _full((8, d))],
        out_shape=[jax.ShapeDtypeStruct((8, LANES), F32), jax.ShapeDtypeStruct((s, d), F32),
                   jax.ShapeDtypeStruct((s, d), BF16), jax.ShapeDtypeStruct((8, d), F32)],
        compiler_params=_params("arbitrary"))(x1, ffn, target, gate2, g)


def mid_bwd(dh2, dout, x1, y1, pre2_g, scale2, gate1, post1_g):
    s, d = x1.shape
    tr = _tile(s, ROW_TILE)

    def body(dh_ref, dout_ref, x_ref, y_ref, g2_ref, sc_ref, gate_ref, g1_ref, dx_ref, dy_ref, acc_ref):
        @pl.when(pl.program_id(0) == 0)
        def _():
            acc_ref[...] = jnp.zeros_like(acc_ref)

        dh = _f32(dh_ref)
        inv2, xh = _rms_stats(x_ref[...])
        acc_ref[0:1, :] += _colsum(dh)
        acc_ref[1:2, :] += _colsum(dh * (xh * g2_ref[...]))
        t = dh * (1.0 + sc_ref[...])
        acc_ref[2:3, :] += _colsum(t * xh)
        dx1 = dout_ref[...] + _rms_bwd(t * g2_ref[...], xh, inv2)
        dx_ref[...] = dx1
        inv1, yh = _rms_stats(y_ref[...])
        acc_ref[3:4, :] += _colsum(dx1 * (yh * g1_ref[...]))
        dr = dx1 * gate_ref[...]
        acc_ref[4:5, :] += _colsum(dr * yh)
        dy_ref[...] = _rms_bwd(dr * g1_ref[...], yh, inv1).astype(dy_ref.dtype)

    row = pl.BlockSpec((tr, d), lambda i: (i, 0))
    vec = pl.BlockSpec((1, d), lambda i: (0, 0))
    return pl.pallas_call(
        body, name="mid_bwd", grid=(s // tr,), in_specs=[row, row, row, row, vec, vec, vec, vec],
        out_specs=[row, row, _full((8, d))],
        out_shape=[jax.ShapeDtypeStruct((s, d), F32), jax.ShapeDtypeStruct((s, d), BF16), jax.ShapeDtypeStruct((8, d), F32)],
        compiler_params=_params("arbitrary"))(dh2, dout, x1, y1, pre2_g, scale2, gate1, post1_g)


def pre1_bwd(dh1, dx1, x, pre1_g, scale1):
    s, d = x.shape
    tr = _tile(s, ROW_TILE)

    def body(dh_ref, dx1_ref, x_ref, g_ref, sc_ref, dx_ref, acc_ref):
        @pl.when(pl.program_id(0) == 0)
        def _():
            acc_ref[...] = jnp.zeros_like(acc_ref)

        dh = _f32(dh_ref)
        inv, xh = _rms_stats(x_ref[...])
        acc_ref[0:1, :] += _colsum(dh)
        acc_ref[1:2, :] += _colsum(dh * (xh * g_ref[...]))
        t = dh * (1.0 + sc_ref[...])
        acc_ref[2:3, :] += _colsum(t * xh)
        dx_ref[...] = dx1_ref[...] + _rms_bwd(t * g_ref[...], xh, inv)

    row = pl.BlockSpec((tr, d), lambda i: (i, 0))
    vec = pl.BlockSpec((1, d), lambda i: (0, 0))
    return pl.pallas_call(
        body, name="pre1_bwd", grid=(s // tr,), in_specs=[row, row, row, vec, vec], out_specs=[row, _full((8, d))],
        out_shape=[jax.ShapeDtypeStruct((s, d), F32), jax.ShapeDtypeStruct((8, d), F32)],
        compiler_params=_params("arbitrary"))(dh1, dx1, x, pre1_g, scale1)


def _ln_stats(v):
    mu = jnp.mean(v, axis=-1, keepdims=True)
    vc = v - mu
    rstd = lax.rsqrt(jnp.mean(vc * vc, axis=-1, keepdims=True) + EPS)
    return rstd, vc * rstd


def gmlp_fwd(z, width, ln_g, ln_b, wm, bs3):
    s = z.shape[0]
    groups = width // CHUNK

    def body(u_ref, v_ref, g_ref, b_ref, wm_ref, bs_ref, a_ref):
        ug = _gelu(u_ref[...])
        _, vh = _ln_stats(_gelu(v_ref[...]))
        vn = (vh * g_ref[...] + b_ref[...]).astype(BF16)
        for g in range(groups):
            cols = slice(g * CHUNK, (g + 1) * CHUNK)
            mixed = jnp.dot(wm_ref[g], vn[:, cols], preferred_element_type=F32) + bs_ref[g]
            a_ref[:, cols] = (ug[:, cols] * mixed).astype(a_ref.dtype)

    vec = pl.BlockSpec((1, width), lambda n: (0, 0))
    return pl.pallas_call(
        body, name="gmlp_fwd", grid=(s // CHUNK,),
        in_specs=[pl.BlockSpec((CHUNK, width), lambda n: (n, 0)), pl.BlockSpec((CHUNK, width), lambda n: (n, 1)), vec, vec,
                  _full(wm.shape), _full(bs3.shape)],
        out_specs=pl.BlockSpec((CHUNK, width), lambda n: (n, 0)),
        out_shape=jax.ShapeDtypeStruct((s, width), BF16), compiler_params=_params("parallel"))(z, z, ln_g, ln_b, wm, bs3)


def gmlp_bwd(z, width, da, ln_g, ln_b, wm, bs3):
    s = z.shape[0]
    groups = width // CHUNK

    def body(u_ref, v_ref, da_ref, g_ref, b_ref, wm_ref, bs_ref, duv_ref, gw_ref, gb_ref, acc_ref, dvn_ref):
        @pl.when(pl.program_id(0) == 0)
        def _():
            gw_ref[...] = jnp.zeros_like(gw_ref)
            gb_ref[...] = jnp.zeros_like(gb_ref)
            acc_ref[...] = jnp.zeros_like(acc_ref)

        ug, dug = _gelu_and_grad(u_ref[...])
        vg, dvg = _gelu_and_grad(v_ref[...])
        rstd, vh = _ln_stats(vg)
        vn = (vh * g_ref[...] + b_ref[...]).astype(BF16)
        da_v = _f32(da_ref)
        for g in range(groups):
            cols = slice(g * CHUNK, (g + 1) * CHUNK)
            mixed = jnp.dot(wm_ref[g], vn[:, cols], preferred_element_type=F32) + bs_ref[g]
            duv_ref[:, cols] = (da_v[:, cols] * mixed * dug[:, cols]).astype(duv_ref.dtype)
            dm = da_v[:, cols] * ug[:, cols]
            gb_ref[g] += jnp.sum(dm, axis=-1, keepdims=True)
            dmb = dm.astype(BF16)
            gw_ref[g] += lax.dot_general(dmb, vn[:, cols], (((1,), (1,)), ((), ())), preferred_element_type=F32)
            dvn_ref[:, cols] = lax.dot_general(wm_ref[g], dmb, (((0,), (0,)), ((), ())), preferred_element_type=F32)
        dvn = dvn_ref[...]
        acc_ref[0:1, :] += _colsum(dvn * vh)
        acc_ref[1:2, :] += _colsum(dvn)
        dvh = dvn * g_ref[...]
        dv = rstd * (dvh - jnp.mean(dvh, axis=-1, keepdims=True) - vh * jnp.mean(dvh * vh, axis=-1, keepdims=True))
        duv_ref[:, width:] = (dv * dvg).astype(duv_ref.dtype)

        @pl.when(pl.program_id(0) == pl.num_programs(0) - 1)
        def _():
            q = lax.broadcasted_iota(jnp.int32, gw_ref.shape, 1)
            p = lax.broadcasted_iota(jnp.int32, gw_ref.shape, 2)
            gw_ref[...] = jnp.where(p <= q, gw_ref[...], 0.0)

    vec = pl.BlockSpec((1, width), lambda n: (0, 0))
    blk = pl.BlockSpec((CHUNK, width), lambda n: (n, 0))
    return pl.pallas_call(
        body, name="gmlp_bwd", grid=(s // CHUNK,),
        in_specs=[blk, pl.BlockSpec((CHUNK, width), lambda n: (n, 1)), blk, vec, vec, _full(wm.shape), _full(bs3.shape)],
        out_specs=[pl.BlockSpec((CHUNK, 2 * width), lambda n: (n, 0)), _full(wm.shape), _full(bs3.shape), _full((8, width))],
        out_shape=[jax.ShapeDtypeStruct((s, 2 * width), BF16), jax.ShapeDtypeStruct(wm.shape, F32),
                   jax.ShapeDtypeStruct(bs3.shape, F32), jax.ShapeDtypeStruct((8, width), F32)],
        scratch_shapes=[pltpu.VMEM((CHUNK, width), F32)],
        compiler_params=_params("arbitrary"))(z, z, da, ln_g, ln_b, wm, bs3)


def merge_fwd(z, off_a, off_b, ya, yb):
    s, d = ya.shape
    tr, tc = _tile(s, ROW_TILE * 2), _tile(d, COL_TILE)
    assert off_a % tc == 0 and off_b % tc == 0

    def body(ga_ref, gb_ref, ya_ref, yb_ref, o_ref):
        o_ref[...] = (_sigmoid(ga_ref[...]) * _f32(ya_ref) + _sigmoid(gb_ref[...]) * _f32(yb_ref)).astype(o_ref.dtype)

    blk = pl.BlockSpec((tr, tc), lambda i, j: (i, j))
    return pl.pallas_call(
        body, name="merge_fwd", grid=(s // tr, d // tc),
        in_specs=[pl.BlockSpec((tr, tc), lambda i, j: (i, off_a // tc + j)), pl.BlockSpec((tr, tc), lambda i, j: (i, off_b // tc + j)), blk, blk],
        out_specs=blk, out_shape=jax.ShapeDtypeStruct((s, d), BF16), compiler_params=_params("parallel", "parallel"))(z, z, ya, yb)


def merge_bwd(z, off_a, off_b, ya, yb, dm):
    s, d = ya.shape
    tr, tc = _tile(s, ROW_TILE * 2), _tile(d, COL_TILE)
    nc = d // tc

    def body(ga_ref, gb_ref, ya_ref, yb_ref, dm_ref, dya_ref, dyb_ref, dga_ref, dgb_ref):
        dm_v = _f32(dm_ref)
        sa, sb = _sigmoid(ga_ref[...]), _sigmoid(gb_ref[...])
        dya_ref[...] = (dm_v * sa).astype(dya_ref.dtype)
        dyb_ref[...] = (dm_v * sb).astype(dyb_ref.dtype)
        dga_ref[...] = (dm_v * _f32(ya_ref) * sa * (1.0 - sa)).astype(dga_ref.dtype)
        dgb_ref[...] = (dm_v * _f32(yb_ref) * sb * (1.0 - sb)).astype(dgb_ref.dtype)

    blk = pl.BlockSpec((tr, tc), lambda i, j: (i, j))
    out = jax.ShapeDtypeStruct((s, d), BF16)
    return pl.pallas_call(
        body, name="merge_bwd", grid=(s // tr, nc),
        in_specs=[pl.BlockSpec((tr, tc), lambda i, j: (i, off_a // tc + j)), pl.BlockSpec((tr, tc), lambda i, j: (i, off_b // tc + j)), blk, blk, blk],
        out_specs=[blk, blk, blk, blk], out_shape=[out, out, out, out],
        compiler_params=_params("parallel", "parallel"))(z, z, ya, yb, dm)


_ATT_SCALE = (QK_NOPE + QK_ROPE) ** -0.5
_NEG = -1e30


def rope_k(z, off, cos4, sin4):
    s = z.shape[0]
    tr = _tile(s, ROW_TILE * 2)
    assert off % LANES == 0

    def body(k_ref, c_ref, s_ref, o_ref):
        k = _f32(k_ref)
        k = k + pltpu.roll(k, QK_ROPE, 1)
        o_ref[...] = _rope(k, c_ref[...], s_ref[...]).astype(o_ref.dtype)

    row = pl.BlockSpec((tr, LANES), lambda i: (i, 0))
    return pl.pallas_call(body, name="rope_k", grid=(s // tr,),
                          in_specs=[pl.BlockSpec((tr, LANES), lambda i: (i, off // LANES)), row, row], out_specs=row,
                          out_shape=jax.ShapeDtypeStruct((s, LANES), BF16), compiler_params=_params("parallel"))(z, cos4, sin4)


def _head_masks(shape):
    lane = lax.broadcasted_iota(jnp.int32, shape, 1)
    return lane < QK_ROPE, lane >= QK_ROPE


def _scores(qn, qp_h, k, kp, qi, kb, t):
    sc = lax.dot_general(qn, k, (((1,), (1,)), ((), ())), preferred_element_type=F32)
    sc += lax.dot_general(qp_h, kp, (((1,), (1,)), ((), ())), preferred_element_type=F32)
    sc = sc * _ATT_SCALE
    row = lax.broadcasted_iota(jnp.int32, sc.shape, 0) + qi * t
    col = lax.broadcasted_iota(jnp.int32, sc.shape, 1) + kb * t
    return jnp.where(col <= row, sc, _NEG)


def attn_fwd(qn, qp, kv, kpr, cos4, sin4):
    s = qn.shape[0]
    hp = HEADS // 2
    t = _tile(s, ATT_TILE)
    nq = s // t

    def body(qn_ref, qp_ref, kv_ref, kp_ref, c_ref, s_ref, o_ref, qpr_ref, l_ref):
        qi = pl.program_id(1)
        qpr = _rope(qp_ref[...], c_ref[...], s_ref[...]).astype(BF16)
        qpr_ref[...] = qpr
        masks = _head_masks(qpr.shape)
        for hh in range(2):
            q_n = qn_ref[:, hh * QK_NOPE:(hh + 1) * QK_NOPE]
            q_p = jnp.where(masks[hh], qpr, jnp.zeros_like(qpr))
            kc, vc = 2 * hh * QK_NOPE, (2 * hh + 1) * QK_NOPE

            def step(kb, carry):
                m, l, acc = carry
                rows = pl.ds(pl.multiple_of(kb * t, t), t)
                sc = _scores(q_n, q_p, kv_ref[rows, kc:kc + QK_NOPE], kp_ref[rows, :], qi, kb, t)
                m_new = jnp.maximum(m, jnp.max(sc, axis=-1, keepdims=True))
                alpha = jnp.exp(m - m_new)
                p = jnp.exp(sc - m_new)
                l = alpha * l + jnp.sum(p, axis=-1, keepdims=True)
                acc = alpha * acc + jnp.dot(p.astype(BF16), kv_ref[rows, vc:vc + V_HEAD], preferred_element_type=F32)
                return m_new, l, acc

            init = (jnp.full((t, 1), _NEG, F32), jnp.zeros((t, 1), F32), jnp.zeros((t, V_HEAD), F32))
            m, l, acc = lax.fori_loop(0, qi + 1, step, init)
            o_ref[:, hh * V_HEAD:(hh + 1) * V_HEAD] = acc / l
            l_ref[:, hh:hh + 1] = m + jnp.log(l)

    return pl.pallas_call(
        body, name="attn_fwd", grid=(hp, nq),
        in_specs=[pl.BlockSpec((t, 2 * QK_NOPE), lambda h, i: (i, h)), pl.BlockSpec((t, LANES), lambda h, i: (i, h)),
                  pl.BlockSpec((s, 4 * QK_NOPE), lambda h, i: (0, h)), _full((s, LANES)),
                  pl.BlockSpec((t, LANES), lambda h, i: (i, 0)), pl.BlockSpec((t, LANES), lambda h, i: (i, 0))],
        out_specs=[pl.BlockSpec((t, 2 * V_HEAD), lambda h, i: (i, h)), pl.BlockSpec((t, LANES), lambda h, i: (i, h)),
                   pl.BlockSpec((None, t, 2), lambda h, i: (h, i, 0))],
        out_shape=[jax.ShapeDtypeStruct((s, HEADS * V_HEAD), F32), jax.ShapeDtypeStruct((s, HEADS * QK_ROPE), BF16),
                   jax.ShapeDtypeStruct((hp, s, 2), F32)],
        compiler_params=_params("parallel", "parallel"))(qn, qp, kv, kpr, cos4, sin4)


def attn_bwd_q(qn, qpr, kv, kpr, o, do, lse, cos4, sin4):
    s = qn.shape[0]
    hp = HEADS // 2
    t = _tile(s, ATT_TILE)
    nq = s // t

    def body(qn_ref, qpr_ref, kv_ref, kp_ref, o_ref, do_ref, l_ref, c_ref, s_ref, dqn_ref, dqp_ref):
        qi = pl.program_id(1)
        qpr = qpr_ref[...]
        masks = _head_masks(qpr.shape)
        dqp = jnp.zeros(qpr.shape, F32)
        for hh in range(2):
            q_n = qn_ref[:, hh * QK_NOPE:(hh + 1) * QK_NOPE]
            q_p = jnp.where(masks[hh], qpr, jnp.zeros_like(qpr))
            kc, vc = 2 * hh * QK_NOPE, (2 * hh + 1) * QK_NOPE
            do_h = do_ref[:, hh * V_HEAD:(hh + 1) * V_HEAD]
            delta = jnp.sum(do_h * o_ref[:, hh * V_HEAD:(hh + 1) * V_HEAD], axis=-1, keepdims=True)
            do_b = do_h.astype(BF16)
            lse_h = l_ref[:, hh:hh + 1]

            def step(kb, carry):
                dn, dp_ = carry
                rows = pl.ds(pl.multiple_of(kb * t, t), t)
                k = kv_ref[rows, kc:kc + QK_NOPE]
                kp = kp_ref[rows, :]
                p = jnp.exp(_scores(q_n, q_p, k, kp, qi, kb, t) - lse_h)
                dpv = lax.dot_general(do_b, kv_ref[rows, vc:vc + V_HEAD], (((1,), (1,)), ((), ())), preferred_element_type=F32)
                ds = (p * (dpv - delta) * _ATT_SCALE).astype(BF16)
                dn = dn + jnp.dot(ds, k, preferred_element_type=F32)
                dp_ = dp_ + jnp.dot(ds, kp, preferred_element_type=F32)
                return dn, dp_

            dn, dp_h = lax.fori_loop(0, qi + 1, step, (jnp.zeros((t, QK_NOPE), F32), jnp.zeros((t, LANES), F32)))
            dqn_ref[:, hh * QK_NOPE:(hh + 1) * QK_NOPE] = dn.astype(dqn_ref.dtype)
            dqp = dqp + jnp.where(masks[hh], dp_h, jnp.zeros_like(dp_h))
        dqp_ref[...] = _rope(dqp, c_ref[...], -s_ref[...]).astype(dqp_ref.dtype)

    qblk = pl.BlockSpec((t, 2 * QK_NOPE), lambda h, i: (i, h))
    pblk = pl.BlockSpec((t, LANES), lambda h, i: (i, h))
    tab = pl.BlockSpec((t, LANES), lambda h, i: (i, 0))
    return pl.pallas_call(
        body, name="attn_bwd_q", grid=(hp, nq),
        in_specs=[qblk, pblk, pl.BlockSpec((s, 4 * QK_NOPE), lambda h, i: (0, h)), _full((s, LANES)), qblk, qblk,
                  pl.BlockSpec((None, t, 2), lambda h, i: (h, i, 0)), tab, tab],
        out_specs=[qblk, pblk],
        out_shape=[jax.ShapeDtypeStruct((s, HEADS * QK_NOPE), BF16), jax.ShapeDtypeStruct((s, HEADS * QK_ROPE), BF16)],
        compiler_params=_params("parallel", "parallel"))(qn, qpr, kv, kpr, o, do, lse, cos4, sin4)


def attn_bwd_kv(qn, qpr, kv, kpr, o, do, lse):
    s = qn.shape[0]
    hp = HEADS // 2
    t = _tile(s, ATT_TILE)
    nq = s // t

    def body(qn_ref, qpr_ref, kv_ref, kp_ref, o_ref, do_ref, l_ref, dkv_ref, dkp_ref):
        ki = pl.program_id(1)
        rows_k = pl.ds(pl.multiple_of(ki * t, t), t)
        kp = kp_ref[rows_k, :]
        dkp = jnp.zeros((t, LANES), F32)
        for hh in range(2):
            kc, vc = 2 * hh * QK_NOPE, (2 * hh + 1) * QK_NOPE
            k = kv_ref[rows_k, kc:kc + QK_NOPE]
            v = kv_ref[rows_k, vc:vc + V_HEAD]

            def step(qb, carry):
                dk, dv, dkp_h = carry
                rows = pl.ds(pl.multiple_of(qb * t, t), t)
                q_n = qn_ref[rows, hh * QK_NOPE:(hh + 1) * QK_NOPE]
                qpr = qpr_ref[rows, :]
                lane = lax.broadcasted_iota(jnp.int32, qpr.shape, 1)
                sel = (lane < QK_ROPE) if hh == 0 else (lane >= QK_ROPE)
                q_p = jnp.where(sel, qpr, jnp.zeros_like(qpr))
                do_h = do_ref[rows, hh * V_HEAD:(hh + 1) * V_HEAD]
                delta = jnp.sum(do_h * o_ref[rows, hh * V_HEAD:(hh + 1) * V_HEAD], axis=-1, keepdims=True)
                do_b = do_h.astype(BF16)
                p = jnp.exp(_scores(q_n, q_p, k, kp, qb, ki, t) - l_ref[rows, hh:hh + 1])
                dpv = lax.dot_general(do_b, v, (((1,), (1,)), ((), ())), preferred_element_type=F32)
                ds = (p * (dpv - delta) * _ATT_SCALE).astype(BF16)
                dv = dv + lax.dot_general(p.astype(BF16), do_b, (((0,), (0,)), ((), ())), preferred_element_type=F32)
                dk = dk + lax.dot_general(ds, q_n, (((0,), (0,)), ((), ())), preferred_element_type=F32)
                dkp_h = dkp_h + lax.dot_general(ds, q_p, (((0,), (0,)), ((), ())), preferred_element_type=F32)
                return dk, dv, dkp_h

            init = (jnp.zeros((t, QK_NOPE), F32), jnp.zeros((t, V_HEAD), F32), jnp.zeros((t, LANES), F32))
            dk, dv, dkp_h = lax.fori_loop(ki, nq, step, init)
            dkv_ref[:, kc:kc + QK_NOPE] = dk.astype(dkv_ref.dtype)
            dkv_ref[:, vc:vc + V_HEAD] = dv.astype(dkv_ref.dtype)
            dkp = dkp + dkp_h
        dkp_ref[...] = dkp

    return pl.pallas_call(
        body, name="attn_bwd_kv", grid=(hp, nq),
        in_specs=[pl.BlockSpec((s, 2 * QK_NOPE), lambda h, i: (0, h)), pl.BlockSpec((s, LANES), lambda h, i: (0, h)),
                  pl.BlockSpec((s, 4 * QK_NOPE), lambda h, i: (0, h)), _full((s, LANES)),
                  pl.BlockSpec((s, 2 * V_HEAD), lambda h, i: (0, h)), pl.BlockSpec((s, 2 * V_HEAD), lambda h, i: (0, h)),
                  pl.BlockSpec((None, s, 2), lambda h, i: (h, 0, 0))],
        out_specs=[pl.BlockSpec((t, 4 * QK_NOPE), lambda h, i: (i, h)), pl.BlockSpec((None, t, LANES), lambda h, i: (h, i, 0))],
        out_shape=[jax.ShapeDtypeStruct((s, HEADS * 2 * QK_NOPE), BF16), jax.ShapeDtypeStruct((hp, s, LANES), F32)],
        compiler_params=_params("parallel", "parallel"))(qn, qpr, kv, kpr, o, do, lse)


def _dot_nt(a, b):
    return lax.dot_general(a, b, (((1,), (1,)), ((), ())), preferred_element_type=F32)


def _dot_tn(a, b):
    return lax.dot_general(a, b, (((0,), (0,)), ((), ())), preferred_element_type=F32)


def _q_cat(q_n, qpr, hh):
    lane = lax.broadcasted_iota(jnp.int32, qpr.shape, 1)
    sel = (lane < QK_ROPE) if hh == 0 else (lane >= QK_ROPE)
    return jnp.concatenate([q_n, jnp.where(sel, qpr, jnp.zeros_like(qpr))], axis=1)


def _causal(sc):
    row = lax.broadcasted_iota(jnp.int32, sc.shape, 0)
    col = lax.broadcasted_iota(jnp.int32, sc.shape, 1)
    return jnp.where(col <= row, sc, _NEG)


def attn_fwd2(qn, qp, kv, kpr, cos4, sin4):
    s = qn.shape[0]
    hp = HEADS // 2
    t = _tile(s, ATT_TILE)
    nq = s // t

    def body(qn_ref, qp_ref, kv_ref, kp_ref, c_ref, s_ref, o_ref, qpr_ref, l_ref, kcat_ref):
        qi = pl.program_id(1)

        @pl.when(qi == 0)
        def _():
            for hh in range(2):
                kcat_ref[hh, :, 0:QK_NOPE] = kv_ref[:, 2 * hh * QK_NOPE:(2 * hh + 1) * QK_NOPE]
                kcat_ref[hh, :, QK_NOPE:] = kp_ref[...]

        qpr = _rope(qp_ref[...], c_ref[...], s_ref[...]).astype(BF16)
        qpr_ref[...] = qpr
        qcat = [_q_cat(qn_ref[:, hh * QK_NOPE:(hh + 1) * QK_NOPE], qpr, hh) for hh in range(2)]

        def block(kb, carry, diagonal):
            rows = pl.ds(pl.multiple_of(kb * t, t), t)
            out = []
            for hh in range(2):
                m, l, acc = carry[hh]
                sc = _dot_nt(qcat[hh], kcat_ref[hh, rows, :]) * _ATT_SCALE
                if diagonal:
                    sc = _causal(sc)
                m_new = jnp.maximum(m, jnp.max(sc, axis=-1, keepdims=True))
                alpha = jnp.exp(m - m_new)
                p = jnp.exp(sc - m_new)
                l = alpha * l + jnp.sum(p, axis=-1, keepdims=True)
                v = kv_ref[rows, (2 * hh + 1) * QK_NOPE:(2 * hh + 2) * QK_NOPE]
                acc = alpha * acc + jnp.dot(p.astype(BF16), v, preferred_element_type=F32)
                out.append((m_new, l, acc))
            return tuple(out)

        one = (jnp.full((t, 1), _NEG, F32), jnp.zeros((t, 1), F32), jnp.zeros((t, V_HEAD), F32))
        carry = lax.fori_loop(0, qi, lambda kb, cr: block(kb, cr, False), (one, one))
        carry = block(qi, carry, True)
        for hh in range(2):
            m, l, acc = carry[hh]
            o_ref[:, hh * V_HEAD:(hh + 1) * V_HEAD] = acc / l
            l_ref[:, hh:hh + 1] = m + jnp.log(l)

    return pl.pallas_call(
        body, name="attn_fwd", grid=(hp, nq),
        in_specs=[pl.BlockSpec((t, 2 * QK_NOPE), lambda h, i: (i, h)), pl.BlockSpec((t, LANES), lambda h, i: (i, h)),
                  pl.BlockSpec((s, 4 * QK_NOPE), lambda h, i: (0, h)), _full((s, LANES)),
                  pl.BlockSpec((t, LANES), lambda h, i: (i, 0)), pl.BlockSpec((t, LANES), lambda h, i: (i, 0))],
        out_specs=[pl.BlockSpec((t, 2 * V_HEAD), lambda h, i: (i, h)), pl.BlockSpec((t, LANES), lambda h, i: (i, h)),
                   pl.BlockSpec((None, t, 2), lambda h, i: (h, i, 0))],
        out_shape=[jax.ShapeDtypeStruct((s, HEADS * V_HEAD), F32), jax.ShapeDtypeStruct((s, HEADS * QK_ROPE), BF16),
                   jax.ShapeDtypeStruct((hp, s, 2), F32)],
        scratch_shapes=[pltpu.VMEM((2, s, 2 * QK_NOPE), BF16)],
        compiler_params=_params("parallel", "arbitrary"))(qn, qp, kv, kpr, cos4, sin4)


def attn_bwd2(qn, qpr, kv, kpr, o, do, lse, cos4, sin4):
    s = qn.shape[0]
    hp = HEADS // 2
    t = _tile(s, ATT_TILE)
    nk = s // t

    def body(qn_ref, qpr_ref, kv_ref, kp_ref, o_ref, do_ref, l_ref, c_ref, s_ref,
             dqn_ref, dqp_ref, dkv_ref, dkp_ref, qcat_ref, dq_ref, delta_ref):
        ki = pl.program_id(1)

        @pl.when(ki == 0)
        def _():
            dq_ref[...] = jnp.zeros_like(dq_ref)
            for hh in range(2):
                qcat_ref[hh] = _q_cat(qn_ref[:, hh * QK_NOPE:(hh + 1) * QK_NOPE], qpr_ref[...], hh)
                cols = slice(hh * V_HEAD, (hh + 1) * V_HEAD)
                delta_ref[hh] = jnp.sum(do_ref[:, cols] * o_ref[:, cols], axis=-1, keepdims=True)

        rows_k = pl.ds(pl.multiple_of(ki * t, t), t)
        kcat = [jnp.concatenate([kv_ref[rows_k, 2 * hh * QK_NOPE:(2 * hh + 1) * QK_NOPE], kp_ref[rows_k, :]], axis=1) for hh in range(2)]
        vs = [kv_ref[rows_k, (2 * hh + 1) * QK_NOPE:(2 * hh + 2) * QK_NOPE] for hh in range(2)]

        def block(qb, carry, diagonal):
            rows = pl.ds(pl.multiple_of(qb * t, t), t)
            out = []
            for hh in range(2):
                dkc, dv = carry[hh]
                q_c = qcat_ref[hh, rows, :]
                do_b = do_ref[rows, hh * V_HEAD:(hh + 1) * V_HEAD].astype(BF16)
                sc = _dot_nt(q_c, kcat[hh]) * _ATT_SCALE
                if diagonal:
                    sc = _causal(sc)
                p = jnp.exp(sc - l_ref[rows, hh:hh + 1])
                dpv = _dot_nt(do_b, vs[hh])
                ds = (p * (dpv - delta_ref[hh, rows, :]) * _ATT_SCALE).astype(BF16)
                dv = dv + _dot_tn(p.astype(BF16), do_b)
                dkc = dkc + _dot_tn(ds, q_c)
                dq_ref[hh, rows, :] += jnp.dot(ds, kcat[hh], preferred_element_type=F32)
                out.append((dkc, dv))
            return tuple(out)

        one = (jnp.zeros((t, 2 * QK_NOPE), F32), jnp.zeros((t, V_HEAD), F32))
        carry = block(ki, (one, one), True)
        carry = lax.fori_loop(ki + 1, nk, lambda qb, cr: block(qb, cr, False), carry)
        dkp = jnp.zeros((t, LANES), F32)
        for hh in range(2):
            dkc, dv = carry[hh]
            dkv_ref[:, 2 * hh * QK_NOPE:(2 * hh + 1) * QK_NOPE] = dkc[:, :QK_NOPE].astype(dkv_ref.dtype)
            dkv_ref[:, (2 * hh + 1) * QK_NOPE:(2 * hh + 2) * QK_NOPE] = dv.astype(dkv_ref.dtype)
            dkp = dkp + dkc[:, QK_NOPE:]
        dkp_ref[...] = dkp

        @pl.when(ki == nk - 1)
        def _():
            lane = lax.broadcasted_iota(jnp.int32, (s, LANES), 1)
            dqp = jnp.where(lane < QK_ROPE, dq_ref[0, :, QK_NOPE:], dq_ref[1, :, QK_NOPE:])
            dqp_ref[...] = _rope(dqp, c_ref[...], -s_ref[...]).astype(dqp_ref.dtype)
            for hh in range(2):
                dqn_ref[:, hh * QK_NOPE:(hh + 1) * QK_NOPE] = dq_ref[hh, :, :QK_NOPE].astype(dqn_ref.dtype)

    qblk = pl.BlockSpec((s, 2 * QK_NOPE), lambda h, i: (0, h))
    pblk = pl.BlockSpec((s, LANES), lambda h, i: (0, h))
    tab = _full((s, LANES))
    return pl.pallas_call(
        body, name="attn_bwd", grid=(hp, nk),
        in_specs=[qblk, pblk, pl.BlockSpec((s, 4 * QK_NOPE), lambda h, i: (0, h)), tab, qblk, qblk,
                  pl.BlockSpec((None, s, 2), lambda h, i: (h, 0, 0)), tab, tab],
        out_specs=[qblk, pblk, pl.BlockSpec((t, 4 * QK_NOPE), lambda h, i: (i, h)), pl.BlockSpec((None, t, LANES), lambda h, i: (h, i, 0))],
        out_shape=[jax.ShapeDtypeStruct((s, HEADS * QK_NOPE), BF16), jax.ShapeDtypeStruct((s, HEADS * QK_ROPE), BF16),
                   jax.ShapeDtypeStruct((s, HEADS * 2 * QK_NOPE), BF16), jax.ShapeDtypeStruct((hp, s, LANES), F32)],
        scratch_shapes=[pltpu.VMEM((2, s, 2 * QK_NOPE), BF16), pltpu.VMEM((2, s, 2 * QK_NOPE), F32), pltpu.VMEM((2, s, 1), F32)],
        compiler_params=_params("parallel", "arbitrary"))(qn, qpr, kv, kpr, o, do, lse, cos4, sin4)


def _causal_at(sc, row0, col0):
    row = lax.broadcasted_iota(jnp.int32, sc.shape, 0) + row0
    col = lax.broadcasted_iota(jnp.int32, sc.shape, 1) + col0
    return jnp.where(col <= row, sc, _NEG)


def attn_fwd3(qn, qp, kv, kpr, cos4, sin4):
    s = qn.shape[0]
    hp = HEADS // 2
    t = _tile(s, ATT_TILE)
    tk = 2 * t
    nq = s // t
    assert s % tk == 0

    def body(qn_ref, qp_ref, kv_ref, kp_ref, c_ref, s_ref, o_ref, qpr_ref, l_ref, kcat_ref):
        qi = pl.program_id(1)

        @pl.when(qi == 0)
        def _():
            for hh in range(2):
                kcat_ref[hh, :, 0:QK_NOPE] = kv_ref[:, 2 * hh * QK_NOPE:(2 * hh + 1) * QK_NOPE]
                kcat_ref[hh, :, QK_NOPE:] = kp_ref[...]

        qpr = _rope(qp_ref[...], c_ref[...], s_ref[...]).astype(BF16)
        qpr_ref[...] = qpr
        qcat = [_q_cat(qn_ref[:, hh * QK_NOPE:(hh + 1) * QK_NOPE], qpr, hh) for hh in range(2)]

        def block(kb, carry, diagonal):
            start = pl.multiple_of(kb * tk, tk)
            rows = pl.ds(start, tk)
            out = []
            for hh in range(2):
                m, l, acc = carry[hh]
                sc = _dot_nt(qcat[hh], kcat_ref[hh, rows, :]) * _ATT_SCALE
                if diagonal:
                    sc = _causal_at(sc, qi * t, start)
                m_new = jnp.maximum(m, jnp.max(sc, axis=-1, keepdims=True))
                alpha = jnp.exp(m - m_new)
                p = jnp.exp(sc - m_new)
                l = alpha * l + jnp.sum(p, axis=-1, keepdims=True)
                v = kv_ref[rows, (2 * hh + 1) * QK_NOPE:(2 * hh + 2) * QK_NOPE]
                acc = alpha * acc + jnp.dot(p.astype(BF16), v, preferred_element_type=F32)
                out.append((m_new, l, acc))
            return tuple(out)

        one = (jnp.full((t, 1), _NEG, F32), jnp.zeros((t, 1), F32), jnp.zeros((t, V_HEAD), F32))
        carry = lax.fori_loop(0, qi // 2, lambda kb, cr: block(kb, cr, False), (one, one))
        carry = block(qi // 2, carry, True)
        for hh in range(2):
            m, l, acc = carry[hh]
            o_ref[:, hh * V_HEAD:(hh + 1) * V_HEAD] = acc / l
            l_ref[:, hh:hh + 1] = m + jnp.log(l)

    return pl.pallas_call(
        body, name="attn_fwd", grid=(hp, nq),
        in_specs=[pl.BlockSpec((t, 2 * QK_NOPE), lambda h, i: (i, h)), pl.BlockSpec((t, LANES), lambda h, i: (i, h)),
                  pl.BlockSpec((s, 4 * QK_NOPE), lambda h, i: (0, h)), _full((s, LANES)),
                  pl.BlockSpec((t, LANES), lambda h, i: (i, 0)), pl.BlockSpec((t, LANES), lambda h, i: (i, 0))],
        out_specs=[pl.BlockSpec((t, 2 * V_HEAD), lambda h, i: (i, h)), pl.BlockSpec((t, LANES), lambda h, i: (i, h)),
                   pl.BlockSpec((None, t, 2), lambda h, i: (h, i, 0))],
        out_shape=[jax.ShapeDtypeStruct((s, HEADS * V_HEAD), F32), jax.ShapeDtypeStruct((s, HEADS * QK_ROPE), BF16),
                   jax.ShapeDtypeStruct((hp, s, 2), F32)],
        scratch_shapes=[pltpu.VMEM((2, s, 2 * QK_NOPE), BF16)],
        compiler_params=_params("parallel", "arbitrary"))(qn, qp, kv, kpr, cos4, sin4)


def attn_bwd3(qn, qpr, kv, kpr, o, do, lse, cos4, sin4):
    s = qn.shape[0]
    hp = HEADS // 2
    t = _tile(s, ATT_TILE)
    tq = 2 * t
    nk = s // t
    nq2 = s // tq
    assert s % tq == 0

    def body(qn_ref, qpr_ref, kv_ref, kp_ref, o_ref, do_ref, l_ref, c_ref, s_ref,
             dqn_ref, dqp_ref, dkv_ref, dkp_ref, qcat_ref, dq_ref, delta_ref):
        ki = pl.program_id(1)

        @pl.when(ki == 0)
        def _():
            dq_ref[...] = jnp.zeros_like(dq_ref)
            for hh in range(2):
                qcat_ref[hh] = _q_cat(qn_ref[:, hh * QK_NOPE:(hh + 1) * QK_NOPE], qpr_ref[...], hh)
                cols = slice(hh * V_HEAD, (hh + 1) * V_HEAD)
                delta_ref[hh] = jnp.sum(do_ref[:, cols] * o_ref[:, cols], axis=-1, keepdims=True)

        rows_k = pl.ds(pl.multiple_of(ki * t, t), t)
        kcat = [jnp.concatenate([kv_ref[rows_k, 2 * hh * QK_NOPE:(2 * hh + 1) * QK_NOPE], kp_ref[rows_k, :]], axis=1) for hh in range(2)]
        vs = [kv_ref[rows_k, (2 * hh + 1) * QK_NOPE:(2 * hh + 2) * QK_NOPE] for hh in range(2)]

        def block(qb, carry, diagonal):
            start = pl.multiple_of(qb * tq, tq)
            rows = pl.ds(start, tq)
            out = []
            for hh in range(2):
                dkc, dv = carry[hh]
                q_c = qcat_ref[hh, rows, :]
                do_b = do_ref[rows, hh * V_HEAD:(hh + 1) * V_HEAD].astype(BF16)
                sc = _dot_nt(q_c, kcat[hh]) * _ATT_SCALE
                if diagonal:
                    sc = _causal_at(sc, start, ki * t)
                p = jnp.exp(sc - l_ref[rows, hh:hh + 1])
                dpv = _dot_nt(do_b, vs[hh])
                ds = (p * (dpv - delta_ref[hh, rows, :]) * _ATT_SCALE).astype(BF16)
                dv = dv + _dot_tn(p.astype(BF16), do_b)
                dkc = dkc + _dot_tn(ds, q_c)
                dq_ref[hh, rows, :] += jnp.dot(ds, kcat[hh], preferred_element_type=F32)
                out.append((dkc, dv))
            return tuple(out)

        one = (jnp.zeros((t, 2 * QK_NOPE), F32), jnp.zeros((t, V_HEAD), F32))
        carry = block(ki // 2, (one, one), True)
        carry = lax.fori_loop(ki // 2 + 1, nq2, lambda qb, cr: block(qb, cr, False), carry)
        dkp = jnp.zeros((t, LANES), F32)
        for hh in range(2):
            dkc, dv = carry[hh]
            dkv_ref[:, 2 * hh * QK_NOPE:(2 * hh + 1) * QK_NOPE] = dkc[:, :QK_NOPE].astype(dkv_ref.dtype)
            dkv_ref[:, (2 * hh + 1) * QK_NOPE:(2 * hh + 2) * QK_NOPE] = dv.astype(dkv_ref.dtype)
            dkp = dkp + dkc[:, QK_NOPE:]
        dkp_ref[...] = dkp

        @pl.when(ki == nk - 1)
        def _():
            lane = lax.broadcasted_iota(jnp.int32, (s, LANES), 1)
            dqp = jnp.where(lane < QK_ROPE, dq_ref[0, :, QK_NOPE:], dq_ref[1, :, QK_NOPE:])
            dqp_ref[...] = _rope(dqp, c_ref[...], -s_ref[...]).astype(dqp_ref.dtype)
            for hh in range(2):
                dqn_ref[:, hh * QK_NOPE:(hh + 1) * QK_NOPE] = dq_ref[hh, :, :QK_NOPE].astype(dqn_ref.dtype)

    qblk = pl.BlockSpec((s, 2 * QK_NOPE), lambda h, i: (0, h))
    pblk = pl.BlockSpec((s, LANES), lambda h, i: (0, h))
    tab = _full((s, LANES))
    return pl.pallas_call(
        body, name="attn_bwd", grid=(hp, nk),
        in_specs=[qblk, pblk, pl.BlockSpec((s, 4 * QK_NOPE), lambda h, i: (0, h)), tab, qblk, qblk,
                  pl.BlockSpec((None, s, 2), lambda h, i: (h, 0, 0)), tab, tab],
        out_specs=[qblk, pblk, pl.BlockSpec((t, 4 * QK_NOPE), lambda h, i: (i, h)), pl.BlockSpec((None, t, LANES), lambda h, i: (h, i, 0))],
        out_shape=[jax.ShapeDtypeStruct((s, HEADS * QK_NOPE), BF16), jax.ShapeDtypeStruct((s, HEADS * QK_ROPE), BF16),
                   jax.ShapeDtypeStruct((s, HEADS * 2 * QK_NOPE), BF16), jax.ShapeDtypeStruct((hp, s, LANES), F32)],
        scratch_shapes=[pltpu.VMEM((2, s, 2 * QK_NOPE), BF16), pltpu.VMEM((2, s, 2 * QK_NOPE), F32), pltpu.VMEM((2, s, 1), F32)],
        compiler_params=_params("parallel", "arbitrary"))(qn, qpr, kv, kpr, o, do, lse, cos4, sin4)


def kpe_bwd(dkp, cos4, sin4, pad_cols):
    hp, s, _ = dkp.shape
    tr = _tile(s, ROW_TILE * 2)

    def body(d_ref, c_ref, s_ref, o_ref):
        tot = d_ref[0]
        for h in range(1, hp):
            tot = tot + d_ref[h]
        tot = tot + pltpu.roll(tot, QK_ROPE, 1)
        lane = lax.broadcasted_iota(jnp.int32, tot.shape, 1)
        dk = jnp.where(lane < QK_ROPE, _rope(tot, c_ref[...], -s_ref[...]), jnp.zeros_like(tot))
        o_ref[...] = jnp.zeros_like(o_ref)
        o_ref[:, 0:LANES] = dk.astype(o_ref.dtype)

    row = pl.BlockSpec((tr, LANES), lambda i: (i, 0))
    return pl.pallas_call(body, name="kpe_bwd", grid=(s // tr,),
                          in_specs=[pl.BlockSpec((hp, tr, LANES), lambda i: (0, i, 0)), row, row],
                          out_specs=pl.BlockSpec((tr, pad_cols), lambda i: (i, 0)),
                          out_shape=jax.ShapeDtypeStruct((s, pad_cols), BF16), compiler_params=_params("parallel"))(dkp, cos4, sin4)


def _shift_down(x, n):
    row = lax.broadcasted_iota(jnp.int32, x.shape, 0)
    return jnp.where(row >= n, pltpu.roll(x, n, 0), jnp.zeros_like(x))


def _shift_up(x, n):
    rows = x.shape[0]
    row = lax.broadcasted_iota(jnp.int32, x.shape, 0)
    return jnp.where(row < rows - n, pltpu.roll(x, rows - n, 0), jnp.zeros_like(x))


def _conv(x, w_ref, b_ref):
    return w_ref[2:3, :] * x + w_ref[1:2, :] * _shift_down(x, 1) + w_ref[0:1, :] * _shift_down(x, 2) + b_ref[...]


def conv_act_fwd(upre, conv_w, conv_b):
    s, f2 = upre.shape
    f = f2 // 2
    tc = _tile(f, COL_TILE)
    nc = f // tc

    def body(ug_ref, uv_ref, wg_ref, wv_ref, bg_ref, bv_ref, o_ref):
        gh = _conv(_f32(ug_ref), wg_ref, bg_ref)
        vh = _conv(_f32(uv_ref), wv_ref, bv_ref)
        o_ref[...] = (gh * _sigmoid(gh) * vh).astype(o_ref.dtype)

    def spec(rows, shift):
        return pl.BlockSpec((rows, tc), lambda j: (0, j + shift))

    return pl.pallas_call(
        body, name="conv_act_fwd", grid=(nc,),
        in_specs=[spec(s, 0), spec(s, nc), spec(3, 0), spec(3, nc), spec(1, 0), spec(1, nc)], out_specs=spec(s, 0),
        out_shape=jax.ShapeDtypeStruct((s, f), BF16), compiler_params=_params("parallel"))(upre, upre, conv_w, conv_w, conv_b, conv_b)


def conv_act_bwd(upre, conv_w, conv_b, df):
    s, f2 = upre.shape
    f = f2 // 2
    tc = _tile(f, COL_TILE)
    nc = f // tc

    def half(x, d, w_ref, du_ref, which, gw_ref, gb_ref):
        d1, d2 = _shift_up(d, 1), _shift_up(d, 2)
        gb_ref[...] = _colsum(d)
        gw_ref[2:3, :] = _colsum(d * x)
        gw_ref[1:2, :] = _colsum(d1 * x)
        gw_ref[0:1, :] = _colsum(d2 * x)
        du_ref[which] = (w_ref[2:3, :] * d + w_ref[1:2, :] * d1 + w_ref[0:1, :] * d2).astype(du_ref.dtype)

    def body(ug_ref, uv_ref, wg_ref, wv_ref, bg_ref, bv_ref, df_ref, du_ref, gwg_ref, gwv_ref, gbg_ref, gbv_ref):
        xg, xv = _f32(ug_ref), _f32(uv_ref)
        gh = _conv(xg, wg_ref, bg_ref)
        vh = _conv(xv, wv_ref, bv_ref)
        sg = _sigmoid(gh)
        df_v = _f32(df_ref)
        half(xg, df_v * vh * (sg * (1.0 + gh * (1.0 - sg))), wg_ref, du_ref, 0, gwg_ref, gbg_ref)
        half(xv, df_v * (gh * sg), wv_ref, du_ref, 1, gwv_ref, gbv_ref)

    def spec(rows, shift):
        return pl.BlockSpec((rows, tc), lambda j: (0, j + shift))

    gw = jax.ShapeDtypeStruct((3, f), F32)
    gb = jax.ShapeDtypeStruct((1, f), F32)
    return pl.pallas_call(
        body, name="conv_act_bwd", grid=(nc,),
        in_specs=[spec(s, 0), spec(s, nc), spec(3, 0), spec(3, nc), spec(1, 0), spec(1, nc), spec(s, 0)],
        out_specs=[pl.BlockSpec((2, s, tc), lambda j: (0, 0, j)), spec(3, 0), spec(3, 0), spec(1, 0), spec(1, 0)],
        out_shape=[jax.ShapeDtypeStruct((2, s, f), BF16), gw, gw, gb, gb],
        compiler_params=_params("parallel"))(upre, upre, conv_w, conv_w, conv_b, conv_b, df)


def _elementwise_tile(r, c, limit):
    if r % 8:
        return r, c
    best = (8, c if c % LANES else LANES)
    for k in (1, 2, 4, 8, 16):
        if k > 1 and c % (LANES * k):
            continue
        tc = c // k
        tr = max(8, min(r, limit // tc) // 8 * 8)
        while r % tr:
            tr -= 8
        if tr * tc <= max(limit, 8 * tc) and tr * tc > best[0] * best[1]:
            best = (tr, tc)
    return best


def adamw(name, w, m, v, parts):
    npart, r, c = parts.shape
    tr, tc = _elementwise_tile(r, c, ADAMW_TILE_ELEMS)
    bc1 = 1.0 - ADAM_B1 ** ADAM_STEP
    bc2 = 1.0 - ADAM_B2 ** ADAM_STEP

    def body(w_ref, m_ref, v_ref, p_ref, g_ref, d_ref, nm_ref, nv_ref):
        g = p_ref[0].astype(F32)
        for k in range(1, npart):
            g = g + p_ref[k].astype(F32)
        m_new = ADAM_B1 * m_ref[...] + (1.0 - ADAM_B1) * g
        v_new = ADAM_B2 * v_ref[...] + (1.0 - ADAM_B2) * (g * g)
        g_ref[...] = g
        nm_ref[...] = m_new
        nv_ref[...] = v_new
        d_ref[...] = -ADAM_LR * ((m_new / bc1) / (jnp.sqrt(v_new / bc2) + ADAM_EPS) + ADAM_WD * w_ref[...])

    deps = _TOKENS.take()
    blk = pl.BlockSpec((tr, tc), lambda i, j: (i, j))
    out = jax.ShapeDtypeStruct((r, c), F32)
    return pl.pallas_call(
        lambda *refs: body(*refs[:4], *refs[4 + len(deps):]), name=name, grid=(r // tr, c // tc),
        in_specs=[blk, blk, blk, pl.BlockSpec((npart, tr, tc), lambda i, j: (0, i, j))] + [pl.BlockSpec(memory_space=pl.ANY)] * len(deps),
        out_specs=[blk, blk, blk, blk], out_shape=[out, out, out, out],
        compiler_params=_params("parallel", "parallel"))(w, m, v, parts, *deps)


def _position():
    return lax.axis_index("x"), lax.axis_index("y"), lax.axis_index("c")


def _index(p):
    return 4 * p[0] + 2 * p[1] + p[2]


def _peer(me, r):
    return (me[0] ^ ((r >> 2) & 1), me[1] ^ ((r >> 1) & 1), me[2] ^ (r & 1))


_ANY = pl.BlockSpec(memory_space=pl.ANY)


def all_gather_two_level(shards):
    n = len(shards)

    def body(*refs):
        ins, outs = refs[:n], refs[n:2 * n]
        send_sems, recv_sems, local_sems = refs[2 * n:]
        x, y, c = _position()
        me, sibling = (x, y, c), (x, y, 1 - c)
        chips = [(1 - x, y), (x, 1 - y), (1 - x, 1 - y)]

        def copy(w, k, block, to, src=None):
            slot = outs[w].at[_index(block)]
            return pltpu.make_async_remote_copy(src_ref=slot if src is None else src, dst_ref=slot,
                                                send_sem=send_sems.at[7 * w + k], recv_sem=recv_sems.at[7 * w + k],
                                                device_id=to, device_id_type=MESH)

        mine = [pltpu.make_async_copy(ins[w], outs[w].at[_index(me)], local_sems.at[w]) for w in range(n)]
        for cp in mine:
            cp.start()
        first = []
        for w in range(n):
            first.append(copy(w, 0, me, sibling, src=ins[w]))
            first += [copy(w, 1 + j, me, (*chip, c), src=ins[w]) for j, chip in enumerate(chips)]
        for cp in first:
            cp.start()
        passed = []
        for w in range(n):
            for j, chip in enumerate(chips):
                copy(w, 1 + j, (*chip, c), me).wait_recv()
                cp = copy(w, 4 + j, (*chip, c), sibling)
                cp.start()
                passed.append(cp)
        for w in range(n):
            copy(w, 0, sibling, me).wait_recv()
            for j, chip in enumerate(chips):
                copy(w, 4 + j, (*chip, 1 - c), me).wait_recv()
        for cp in first + passed:
            cp.wait_send()
        for cp in mine:
            cp.wait()

    return pl.pallas_call(
        body, name="all_gather_weights",
        out_shape=[jax.ShapeDtypeStruct((N_DEV,) + a.shape, a.dtype) for a in shards],
        in_specs=[_ANY] * n, out_specs=[_ANY] * n,
        scratch_shapes=[pltpu.SemaphoreType.DMA((7 * n,)), pltpu.SemaphoreType.DMA((7 * n,)), pltpu.SemaphoreType.DMA((n,))],
        )(*shards)


def exchange(name, arrays, scatter):
    n = len(arrays)

    def body(*refs):
        ins, outs = refs[:n], refs[n:2 * n]
        send_sems, recv_sems, local_sems = refs[2 * n:]
        me = _position()
        copies = []
        for w in range(n):
            src = ins[w].at[_index(me)] if scatter else ins[w]
            cp = pltpu.make_async_copy(src, outs[w].at[_index(me)], local_sems.at[w])
            cp.start()
            copies.append(cp)
        remote = []
        for w in range(n):
            for r in range(1, N_DEV):
                peer = _peer(me, r)
                src = ins[w].at[_index(peer)] if scatter else ins[w]
                cp = pltpu.make_async_remote_copy(src_ref=src, dst_ref=outs[w].at[_index(me)],
                                                  send_sem=send_sems.at[7 * w + r - 1], recv_sem=recv_sems.at[7 * w + r - 1],
                                                  device_id=peer, device_id_type=MESH)
                cp.start()
                remote.append(cp)
        for cp in remote:
            cp.wait()
        for cp in copies:
            cp.wait()

    blocks = [a.shape[1:] if scatter else a.shape for a in arrays]
    return pl.pallas_call(
        body, name=name,
        out_shape=[jax.ShapeDtypeStruct((N_DEV,) + b, a.dtype) for a, b in zip(arrays, blocks)],
        in_specs=[_ANY] * n, out_specs=[_ANY] * n,
        scratch_shapes=[pltpu.SemaphoreType.DMA((7 * n,)), pltpu.SemaphoreType.DMA((7 * n,)), pltpu.SemaphoreType.DMA((n,))],
        )(*arrays)


_HBM = pl.BlockSpec(memory_space=pltpu.HBM)
_SEM = pl.BlockSpec(memory_space=pltpu.SEMAPHORE)
_EFFECT = pltpu.SideEffectType.DATAFLOW_SIDE_EFFECTING


def _direct_copies(ins, lands, send_sems, recv_sems, scatter):
    me = _position()
    copies = []
    for w in range(len(ins)):
        for r in range(1, N_DEV):
            peer = _peer(me, r)
            src = ins[w].at[_index(peer)] if scatter else ins[w]
            copies.append(pltpu.make_async_remote_copy(src_ref=src, dst_ref=lands[w].at[_index(me)], send_sem=send_sems.at[7 * w + r - 1],
                                                       recv_sem=recv_sems.at[7 * w + r - 1], device_id=peer, device_id_type=MESH))
    return copies


def exchange_start(name, groups, scatter):
    arrays = [a for g in groups for a in g]
    n = len(arrays)
    blocks = [a.shape[1:] if scatter else a.shape for a in arrays]
    lands = [lax.empty((N_DEV,) + b, a.dtype) for a, b in zip(arrays, blocks)]
    ng = len(groups)

    def body(*refs):
        ins, lnd = refs[:n], refs[n:2 * n]
        sems = refs[2 * n:2 * n + 2 * ng]
        token = refs[2 * n + 2 * ng + 2 * n]
        local_sem = refs[2 * n + 2 * ng + 2 * n + 1]
        me = _position()
        local = []
        for w in range(n):
            src = ins[w].at[_index(me)] if scatter else ins[w]
            cp = pltpu.make_async_copy(src, lnd[w].at[_index(me)], local_sem.at[w])
            cp.start()
            local.append(cp)
        w0 = 0
        for gi, g in enumerate(groups):
            for cp in _direct_copies(ins[w0:w0 + len(g)], lnd[w0:w0 + len(g)], sems[2 * gi], sems[2 * gi + 1], scatter):
                cp.start()
            w0 += len(g)
        for cp in local:
            cp.wait()
        token[...] = jnp.zeros_like(token)

    sem_shapes = []
    for g in groups:
        sem_shapes += [pltpu.SemaphoreType.DMA((7 * len(g),)), pltpu.SemaphoreType.DMA((7 * len(g),))]
    out = pl.pallas_call(
        body, name=name,
        out_shape=tuple(sem_shapes) + tuple(pltpu.HBM(a.shape, a.dtype) for a in arrays) + tuple(pltpu.HBM(l.shape, l.dtype) for l in lands)
        + (jax.ShapeDtypeStruct((8, LANES), F32),),
        in_specs=[_HBM] * (2 * n), out_specs=tuple([_SEM] * (2 * ng) + [_HBM] * (2 * n) + [pl.BlockSpec(memory_space=pltpu.VMEM)]),
        input_output_aliases={i: 2 * ng + i for i in range(2 * n)},
        scratch_shapes=[pltpu.SemaphoreType.DMA((n,))],
        compiler_params=pltpu.CompilerParams(has_side_effects=_EFFECT),
    )(*[pltpu.with_memory_space_constraint(a, pltpu.HBM) for a in arrays], *[pltpu.with_memory_space_constraint(l, pltpu.HBM) for l in lands])
    sems, thru, token = out[:2 * ng], out[2 * ng:2 * ng + 2 * n], out[-1]
    res, w0 = [], 0
    for gi, g in enumerate(groups):
        res.append((sems[2 * gi], sems[2 * gi + 1], list(thru[w0:w0 + len(g)]), list(thru[n + w0:n + w0 + len(g)])))
        w0 += len(g)
    return res, token


def exchange_wait(name, group, after, scatter):
    send_sems, recv_sems, srcs, lands = group
    n = len(srcs)

    def body(*refs):
        ins, lnd = refs[:n], refs[n:2 * n]
        for cp in _direct_copies(ins, lnd, refs[2 * n], refs[2 * n + 1], scatter):
            cp.wait_send()
            cp.wait_recv()

    out = pl.pallas_call(
        body, name=name, out_shape=tuple(pltpu.HBM(a.shape, a.dtype) for a in srcs + lands),
        in_specs=[_HBM] * (2 * n) + [_SEM, _SEM, pl.BlockSpec(memory_space=pl.ANY)], out_specs=tuple([_HBM] * (2 * n)),
        input_output_aliases={i: i for i in range(2 * n)},
        compiler_params=pltpu.CompilerParams(has_side_effects=_EFFECT),
    )(*srcs, *lands, send_sems, recv_sems, after)
    return list(out[n:])


def _after(x, token):
    return lax.optimization_barrier((x, token))[0]


_TOKEN = jax.ShapeDtypeStruct((8, LANES), F32)
_VM = pl.BlockSpec(memory_space=pltpu.VMEM)
_SIDE = pltpu.CompilerParams(has_side_effects=_EFFECT)


def _hbm(a):
    return pltpu.with_memory_space_constraint(a, pltpu.HBM)


def _like(a):
    return pltpu.HBM(a.shape, a.dtype)


def _dma_sems(n):
    return pltpu.SemaphoreType.DMA((n,))


def _other_chips(x, y):
    return [(1 - x, y), (x, 1 - y), (1 - x, 1 - y)]


COPY_STREAMS = 8


def _row_chunks(src, dst):
    rows = src.shape[0]
    n = COPY_STREAMS
    while n > 1 and rows % (16 * n):
        n //= 2
    r = rows // n
    return [(src.at[pl.ds(i * r, r)], dst.at[pl.ds(i * r, r)]) for i in range(n)]


def _local_copy(src, dst, sem):
    return [pltpu.make_async_copy(s, d, sem) for s, d in _row_chunks(src, dst)]


class _rcopy:
    def __init__(self, src, dst, send_sem, recv_sem, to):
        self.parts = [pltpu.make_async_remote_copy(src_ref=s, dst_ref=d, send_sem=send_sem, recv_sem=recv_sem, device_id=to, device_id_type=MESH)
                      for s, d in _row_chunks(src, dst)]

    def start(self):
        for cp in self.parts:
            cp.start()

    def wait_send(self):
        for cp in self.parts:
            cp.wait_send()

    def wait_recv(self):
        for cp in self.parts:
            cp.wait_recv()


def _afters(after):
    return list(after) if isinstance(after, (list, tuple)) else [after]


def ag_start(name, shards, after):
    n = len(shards)
    lands = [lax.empty((N_DEV,) + a.shape, a.dtype) for a in shards]
    afters = _afters(after)
    na = len(afters)

    def body(*refs):
        ins, lnd, send_sems, recv_sems, token = refs[:n], refs[n:2 * n], refs[2 * n + na], refs[2 * n + na + 1], refs[4 * n + na + 2]
        x, y, c = _position()
        for w in range(n):
            slot = lnd[w].at[_index((x, y, c))]
            for k, to in enumerate([(x, y, 1 - c)] + [(*chip, c) for chip in _other_chips(x, y)]):
                _rcopy(ins[w], slot, send_sems.at[4 * w + k], recv_sems.at[4 * w + k], to).start()
        token[...] = jnp.zeros_like(token)

    out = pl.pallas_call(
        body, name=name, out_shape=(_dma_sems(4 * n), _dma_sems(4 * n)) + tuple(_like(a) for a in shards + lands) + (_TOKEN,),
        in_specs=[_HBM] * (2 * n) + [_ANY] * na, out_specs=(_SEM, _SEM) + (_HBM,) * (2 * n) + (_VM,),
        input_output_aliases={i: 2 + i for i in range(2 * n)}, compiler_params=_SIDE)(*[_hbm(a) for a in shards + lands], *afters)
    _TOKENS.push(out[-1])
    return out[0], out[1], list(out[2:2 + n]), list(out[2 + n:2 + 2 * n])


def ag_forward(name, started, after):
    send, recv, shards, lands = started
    n = len(shards)
    afters = list(after) if isinstance(after, (list, tuple)) else [after]
    na = len(afters)

    def body(*refs):
        ins, lnd, send_sems, recv_sems = refs[:n], refs[n:2 * n], refs[2 * n], refs[2 * n + 1]
        fsend, frecv, token = refs[2 * n + 2 + na], refs[2 * n + 3 + na], refs[4 * n + 4 + na]
        x, y, c = _position()
        for w in range(n):
            for j, chip in enumerate(_other_chips(x, y)):
                slot = lnd[w].at[_index((*chip, c))]
                _rcopy(ins[w], slot, send_sems.at[4 * w + 1 + j], recv_sems.at[4 * w + 1 + j], (*chip, c)).wait_recv()
                _rcopy(slot, slot, fsend.at[3 * w + j], frecv.at[3 * w + j], (x, y, 1 - c)).start()
        token[...] = jnp.zeros_like(token)

    out = pl.pallas_call(
        body, name=name, out_shape=(_dma_sems(3 * n), _dma_sems(3 * n)) + tuple(_like(a) for a in shards + lands) + (_TOKEN,),
        in_specs=[_HBM] * (2 * n) + [_SEM, _SEM] + [_ANY] * na, out_specs=(_SEM, _SEM) + (_HBM,) * (2 * n) + (_VM,),
        input_output_aliases={i: 2 + i for i in range(2 * n)}, compiler_params=_SIDE)(*shards, *lands, send, recv, *afters)
    _TOKENS.push(out[-1])
    return send, recv, out[0], out[1], list(out[2:2 + n]), list(out[2 + n:2 + 2 * n])


def ag_wait(name, forwarded, after):
    send, recv, fsend, frecv, shards, lands = forwarded
    n = len(shards)

    def body(*refs):
        ins, lnd, send_sems, recv_sems, fsend_r, frecv_r = refs[:n], refs[n:2 * n], refs[2 * n], refs[2 * n + 1], refs[2 * n + 2], refs[2 * n + 3]
        x, y, c = _position()
        sibling = (x, y, 1 - c)
        for w in range(n):
            own = lnd[w].at[_index((x, y, c))]
            _rcopy(ins[w], lnd[w].at[_index(sibling)], send_sems.at[4 * w], recv_sems.at[4 * w], sibling).wait_recv()
            for j, chip in enumerate(_other_chips(x, y)):
                _rcopy(ins[w], lnd[w].at[_index((*chip, 1 - c))], fsend_r.at[3 * w + j], frecv_r.at[3 * w + j], sibling).wait_recv()
            for k in range(4):
                _rcopy(ins[w], own, send_sems.at[4 * w + k], recv_sems.at[4 * w + k], sibling).wait_send()
            for j in range(3):
                _rcopy(ins[w], own, fsend_r.at[3 * w + j], frecv_r.at[3 * w + j], sibling).wait_send()

    out = pl.pallas_call(
        body, name=name, out_shape=tuple(_like(a) for a in shards + lands),
        in_specs=[_HBM] * (2 * n) + [_SEM] * 4 + [_ANY] * len(_afters(after)),
        out_specs=(_HBM,) * (2 * n), input_output_aliases={i: i for i in range(2 * n)},
        compiler_params=_SIDE)(*shards, *lands, send, recv, fsend, frecv, *_afters(after))
    return [lax.dynamic_update_index_in_dim(land, shard, _index(_position()), 0) for shard, land in zip(out[:n], out[n:])]


def rs_d2d_start(name, grads):
    n = len(grads)
    lands = [lax.empty((4,) + g.shape[1:], g.dtype) for g in grads]

    def body(*refs):
        ins, lnd, send_sems, recv_sems, token = refs[:n], refs[n:2 * n], refs[2 * n], refs[2 * n + 1], refs[4 * n + 2]
        x, y, c = _position()
        for w in range(n):
            for i in range(4):
                _rcopy(ins[w].at[2 * i + 1 - c], lnd[w].at[i], send_sems.at[4 * w + i], recv_sems.at[4 * w + i], (x, y, 1 - c)).start()
        token[...] = jnp.zeros_like(token)

    out = pl.pallas_call(
        body, name=name, out_shape=(_dma_sems(4 * n), _dma_sems(4 * n)) + tuple(_like(a) for a in grads + lands) + (_TOKEN,),
        in_specs=[_HBM] * (2 * n), out_specs=(_SEM, _SEM) + (_HBM,) * (2 * n) + (_VM,),
        input_output_aliases={i: 2 + i for i in range(2 * n)}, compiler_params=_SIDE)(*[_hbm(a) for a in grads + lands])
    _TOKENS.push(out[-1])
    return out[0], out[1], list(out[2:2 + n]), list(out[2 + n:2 + 2 * n])


def rs_d2d_wait(name, started, after):
    send, recv, grads, lands = started
    n = len(grads)

    def body(*refs):
        ins, lnd, send_sems, recv_sems = refs[:n], refs[n:2 * n], refs[2 * n], refs[2 * n + 1]
        x, y, c = _position()
        for w in range(n):
            for i in range(4):
                cp = _rcopy(ins[w].at[2 * i + 1 - c], lnd[w].at[i], send_sems.at[4 * w + i], recv_sems.at[4 * w + i], (x, y, 1 - c))
                cp.wait_send()
                cp.wait_recv()

    out = pl.pallas_call(
        body, name=name, out_shape=tuple(_like(a) for a in grads + lands),
        in_specs=[_HBM] * (2 * n) + [_SEM, _SEM] + [_ANY] * len(_afters(after)),
        out_specs=(_HBM,) * (2 * n), input_output_aliases={i: i for i in range(2 * n)},
        compiler_params=_SIDE)(*grads, *lands, send, recv, *_afters(after))
    return list(out[:n]), list(out[n:])


def pair_sum(name, grad, land, core):
    _, r, c = grad.shape
    tr = r
    if r % 8 == 0:
        tr = max(8, min(r, 4 * ADAMW_TILE_ELEMS // c) // 8 * 8)
        while r % tr:
            tr -= 8

    def body(core_ref, a_ref, b_ref, o_ref):
        o_ref[...] = (a_ref[...].astype(F32) + b_ref[...].astype(F32)).astype(o_ref.dtype)

    return pl.pallas_call(
        body, name=name, out_shape=jax.ShapeDtypeStruct((4, r, c), grad.dtype),
        grid_spec=pltpu.PrefetchScalarGridSpec(
            num_scalar_prefetch=1, grid=(4, r // tr),
            in_specs=[pl.BlockSpec((None, None, tr, c), lambda i, j, core_ref: (i, core_ref[0], j, 0)),
                      pl.BlockSpec((None, tr, c), lambda i, j, core_ref: (i, j, 0))],
            out_specs=pl.BlockSpec((None, tr, c), lambda i, j, core_ref: (i, j, 0))),
        compiler_params=_params("parallel", "parallel"))(core, grad.reshape(4, 2, r, c), land)


def rs_ici_start(name, sums):
    n = len(sums)
    lands = [lax.empty(a.shape, a.dtype) for a in sums]

    def body(*refs):
        ins, lnd, send_sems, recv_sems, token = refs[:n], refs[n:2 * n], refs[2 * n], refs[2 * n + 1], refs[4 * n + 2]
        x, y, c = _position()
        chip = 2 * x + y
        for w in range(n):
            for j, other in enumerate(_other_chips(x, y)):
                _rcopy(ins[w].at[2 * other[0] + other[1]], lnd[w].at[chip], send_sems.at[3 * w + j], recv_sems.at[3 * w + j], (*other, c)).start()
        token[...] = jnp.zeros_like(token)

    out = pl.pallas_call(
        body, name=name, out_shape=(_dma_sems(3 * n), _dma_sems(3 * n)) + tuple(_like(a) for a in sums + lands) + (_TOKEN,),
        in_specs=[_HBM] * (2 * n), out_specs=(_SEM, _SEM) + (_HBM,) * (2 * n) + (_VM,),
        input_output_aliases={i: 2 + i for i in range(2 * n)}, compiler_params=_SIDE)(*[_hbm(a) for a in sums + lands])
    _TOKENS.push(out[-1])
    return out[0], out[1], list(out[2:2 + n]), list(out[2 + n:2 + 2 * n])


def rs_ici_wait(name, started, after):
    send, recv, sums, lands = started
    n = len(sums)

    def body(*refs):
        ins, lnd, send_sems, recv_sems = refs[:n], refs[n:2 * n], refs[2 * n], refs[2 * n + 1]
        x, y, c = _position()
        for w in range(n):
            for j, other in enumerate(_other_chips(x, y)):
                cp = _rcopy(ins[w].at[2 * other[0] + other[1]], lnd[w].at[2 * other[0] + other[1]], send_sems.at[3 * w + j], recv_sems.at[3 * w + j], (*other, c))
                cp.wait_send()
                cp.wait_recv()

    out = pl.pallas_call(
        body, name=name, out_shape=tuple(_like(a) for a in sums + lands), in_specs=[_HBM] * (2 * n) + [_SEM, _SEM, _ANY],
        out_specs=(_HBM,) * (2 * n), input_output_aliases={i: i for i in range(2 * n)}, compiler_params=_SIDE)(*sums, *lands, send, recv, after)
    chip = 2 * lax.axis_index("x") + lax.axis_index("y")
    return [lax.dynamic_update_index_in_dim(land, lax.dynamic_index_in_dim(s, chip, 0, keepdims=False), chip, 0)
            for s, land in zip(out[:n], out[n:])]


def ada_fwd(c, w_ada, b_ada3, conv_w):
    d, cs = w_ada.shape

    def body(c_ref, w_ref, b_ref, cw_ref, mod_ref, sc_ref, cwa_ref, part_ref, send_sems, recv_sems):
        me = _position()
        my = _index(me)
        cv = c_ref[...]
        sc_ref[my] = cv * _sigmoid(cv)
        cwa_ref[my] = cw_ref[...]
        gather = []
        for r in range(1, N_DEV):
            for k, ref in enumerate((sc_ref, cwa_ref)):
                cp = pltpu.make_async_remote_copy(src_ref=ref.at[my], dst_ref=ref.at[my], send_sem=send_sems.at[14 * k + r - 1],
                                                  recv_sem=recv_sems.at[14 * k + r - 1], device_id=_peer(me, r), device_id_type=MESH)
                cp.start()
                gather.append(cp)
        for cp in gather:
            cp.wait()
        sc_all = jnp.concatenate([sc_ref[k] for k in range(N_DEV)], axis=0).astype(BF16)
        part = jnp.dot(sc_all, w_ref[...].astype(BF16), preferred_element_type=F32)
        for k in range(N_DEV):
            part_ref[k] = part[k:k + 1, :]
        scatter = []
        for r in range(1, N_DEV):
            peer = _peer(me, r)
            cp = pltpu.make_async_remote_copy(src_ref=part_ref.at[_index(peer)], dst_ref=mod_ref.at[my], send_sem=send_sems.at[6 + r],
                                              recv_sem=recv_sems.at[6 + r], device_id=peer, device_id_type=MESH)
            cp.start()
            scatter.append(cp)
        mod_ref[my] = part_ref[my]
        for cp in scatter:
            cp.wait()
        mod_ref[...] = mod_ref[...] + b_ref[...]

    vm = pl.BlockSpec(memory_space=pltpu.VMEM)
    return pl.pallas_call(
        body, name="ada_fwd",
        out_shape=[jax.ShapeDtypeStruct((N_DEV, 1, cs), F32), jax.ShapeDtypeStruct((N_DEV, 1, d), F32),
                   jax.ShapeDtypeStruct((N_DEV,) + conv_w.shape, F32)],
        in_specs=[vm, vm, vm, vm], out_specs=[vm, vm, vm],
        scratch_shapes=[pltpu.VMEM((N_DEV, 1, cs), F32), pltpu.SemaphoreType.DMA((21,)), pltpu.SemaphoreType.DMA((21,))],
        compiler_params=pltpu.CompilerParams(vmem_limit_bytes=VMEM_LIMIT_BYTES))(c, w_ada, b_ada3, conv_w)


def ada_bwd_w(sc_all, dmod_cols):
    _, d = sc_all.shape
    cs = dmod_cols.shape[1]
    tr = _tile(d, ROW_TILE)

    def body(sc_ref, dm_ref, o_ref):
        dm = dm_ref[...].astype(BF16)
        o_ref[...] = lax.dot_general(sc_ref[...].astype(BF16), dm, (((0,), (0,)), ((), ())), preferred_element_type=F32)

    return pl.pallas_call(body, name="ada_bwd_w", grid=(d // tr,),
                          in_specs=[pl.BlockSpec((N_DEV, tr), lambda i: (0, i)), _full((N_DEV, cs))],
                          out_specs=pl.BlockSpec((None, tr, cs), lambda i: (0, i, 0)),
                          out_shape=jax.ShapeDtypeStruct((1, d, cs), F32), compiler_params=_params("parallel"))(sc_all, dmod_cols)


def _round_up(n, m):
    return (n + m - 1) // m * m


def kernel(x, c, positions, w_ada, b_ada, pre_norm1_g, w_in, gm_ln_g, gm_ln_b, gm_w_s, gm_b_s, w_branch_a, q_norm_g, w_uq, kv_norm_g, w_ukv, w_branch_b, w_out, post_norm1_g, pre_norm2_g, w_up, conv_w, conv_b, w_down, post_norm2_g, loss_target, m_w_ada, m_b_ada, m_pre_norm1_g, m_w_in, m_gm_ln_g, m_gm_ln_b, m_gm_w_s, m_gm_b_s, m_w_branch_a, m_q_norm_g, m_w_uq, m_kv_norm_g, m_w_ukv, m_w_branch_b, m_w_out, m_post_norm1_g, m_pre_norm2_g, m_w_up, m_conv_w, m_conv_b, m_w_down, m_post_norm2_g, v_w_ada, v_b_ada, v_pre_norm1_g, v_w_in, v_gm_ln_g, v_gm_ln_b, v_gm_w_s, v_gm_b_s, v_w_branch_a, v_q_norm_g, v_w_uq, v_kv_norm_g, v_w_ukv, v_w_branch_b, v_w_out, v_post_norm1_g, v_pre_norm2_g, v_w_up, v_conv_w, v_conv_b, v_w_down, v_post_norm2_g):
    weights = dict(w_ada=w_ada, b_ada=b_ada, pre_norm1_g=pre_norm1_g, w_in=w_in, gm_ln_g=gm_ln_g, gm_ln_b=gm_ln_b, gm_w_s=gm_w_s,
                   gm_b_s=gm_b_s, w_branch_a=w_branch_a, q_norm_g=q_norm_g, w_uq=w_uq, kv_norm_g=kv_norm_g, w_ukv=w_ukv,
                   w_branch_b=w_branch_b, w_out=w_out, post_norm1_g=post_norm1_g, pre_norm2_g=pre_norm2_g, w_up=w_up, conv_w=conv_w,
                   conv_b=conv_b, w_down=w_down, post_norm2_g=post_norm2_g)
    mom1 = dict(w_ada=m_w_ada, b_ada=m_b_ada, pre_norm1_g=m_pre_norm1_g, w_in=m_w_in, gm_ln_g=m_gm_ln_g, gm_ln_b=m_gm_ln_b,
                gm_w_s=m_gm_w_s, gm_b_s=m_gm_b_s, w_branch_a=m_w_branch_a, q_norm_g=m_q_norm_g, w_uq=m_w_uq, kv_norm_g=m_kv_norm_g,
                w_ukv=m_w_ukv, w_branch_b=m_w_branch_b, w_out=m_w_out, post_norm1_g=m_post_norm1_g, pre_norm2_g=m_pre_norm2_g,
                w_up=m_w_up, conv_w=m_conv_w, conv_b=m_conv_b, w_down=m_w_down, post_norm2_g=m_post_norm2_g)
    mom2 = dict(w_ada=v_w_ada, b_ada=v_b_ada, pre_norm1_g=v_pre_norm1_g, w_in=v_w_in, gm_ln_g=v_gm_ln_g, gm_ln_b=v_gm_ln_b,
                gm_w_s=v_gm_w_s, gm_b_s=v_gm_b_s, w_branch_a=v_w_branch_a, q_norm_g=v_q_norm_g, w_uq=v_w_uq, kv_norm_g=v_kv_norm_g,
                w_ukv=v_w_ukv, w_branch_b=v_w_branch_b, w_out=v_w_out, post_norm1_g=v_post_norm1_g, pre_norm2_g=v_pre_norm2_g,
                w_up=v_w_up, conv_w=v_conv_w, conv_b=v_conv_b, w_down=v_w_down, post_norm2_g=v_post_norm2_g)
    order = list(weights)
    _TOKENS.clear()

    s, d = x.shape[1], x.shape[2]
    gmw = gm_ln_g.shape[0]
    groups = gmw // CHUNK
    ql, kvl = q_norm_g.shape[0], kv_norm_g.shape[0]
    f2 = conv_b.shape[0]
    in_cols = w_in.shape[1] * N_DEV
    o_q, o_kv, o_ga, o_gb, o_kpe = 2 * gmw, 2 * gmw + ql, 2 * gmw + ql + kvl, 2 * gmw + ql + kvl + d, 2 * gmw + ql + kvl + 2 * d
    zp = _round_up(o_kpe + LANES, Z_PAD)
    src_kpe = 2 * gmw + ql + kvl
    assert src_kpe + QK_ROPE + 2 * d == in_cols
    my = 4 * lax.axis_index("x") + 2 * lax.axis_index("y") + lax.axis_index("c")

    x2, tgt = x[0], loss_target[0]
    row = lambda a: a.reshape(1, -1)

    big = ["w_in", "w_branch_a", "w_uq", "w_ukv", "w_branch_b", "w_out", "w_up", "w_down"]
    sh = {k: weights[k].astype(BF16) for k in big[1:]}
    mix = ["w_branch_a", "w_uq", "w_ukv", "w_branch_b", "w_out"]
    ag_in = ag_start("ag_start_in", [w_in.T.astype(BF16)], c)

    mod8, sc_all3, g_cw = ada_fwd(c, w_ada, b_ada.reshape(N_DEV, 1, -1), conv_w)
    mod = mod8.reshape(N_MOD, d)
    shift1, scale1, gate1, shift2, scale2, gate2 = (mod[i:i + 1] for i in range(N_MOD))
    sc_all = sc_all3.reshape(N_DEV, d)
    h1 = norm_mod_fwd("pre1_fwd", x2, row(pre_norm1_g), scale1, shift1)

    inv = ROPE_THETA ** (-jnp.arange(0, QK_ROPE, 2, dtype=F32) / QK_ROPE)
    ang = positions[0].astype(F32)[:, None] * inv
    cos4 = jnp.tile(jnp.cos(ang), (1, 4))
    sin4 = jnp.tile(jnp.concatenate([-jnp.sin(ang), jnp.sin(ang)], axis=1), (1, 2))

    wm = (gm_w_s * jnp.tril(jnp.ones((CHUNK, CHUNK), F32))).astype(BF16)
    bs3 = gm_b_s.reshape(groups, CHUNK, 1)
    ln_g, ln_b = row(gm_ln_g), row(gm_ln_b)

    small_names = ["pre_norm1_g", "gm_ln_g", "gm_ln_b", "gm_b_s", "q_norm_g", "kv_norm_g", "post_norm1_g", "pre_norm2_g", "conv_b",
                   "post_norm2_g", "gm_w_s", "b_ada"]
    n_small_early = sum(weights[k].size for k in small_names)
    n_pack_early = _round_up(n_small_early + 3 * f2, PACK_ALIGN)

    def pack(src):
        return jnp.concatenate([src[k].reshape(-1) for k in small_names] + [jnp.zeros((n_pack_early - n_small_early,), F32)]).reshape(-1, LANES)

    packed_state = [pack(weights), pack(mom1), pack(mom2)]

    early = [h1, cos4, sin4, wm] + [sh[k] for k in big[1:]] + packed_state
    ag_in = ag_forward("ag_forward_in", ag_in, early)
    ag_mix = ag_start("ag_start_mix", [sh[k] for k in mix], _TOKENS.pending[-1])
    (g_in,) = ag_wait("ag_wait_in", ag_in, [h1, _TOKENS.pending[-1]])
    w_in_f = g_in.reshape(in_cols, d)
    w_in_p = jnp.concatenate([w_in_f[:src_kpe], w_in_f[src_kpe + QK_ROPE:], w_in_f[src_kpe:src_kpe + QK_ROPE],
                              jnp.zeros((zp - in_cols, d), BF16)], axis=0)

    z = mm_nt("z_proj", h1, w_in_p, ACT)
    ag_mix = ag_forward("ag_forward_mix", ag_mix, z)
    ag_up = ag_start("ag_start_up", [sh["w_up"]], _TOKENS.pending[-1])
    a = gmlp_fwd(z, gmw, ln_g, ln_b, wm, bs3)
    g_a, g_uq, g_ukv, g_b, g_out = ag_wait("ag_wait_mix", ag_mix, [a, _TOKENS.pending[-1]])
    w_a_f, w_b_f, w_out_f = g_a.reshape(-1, d), g_b.reshape(-1, d), g_out.reshape(-1, d)
    w_uq_f = g_uq.transpose(1, 0, 2).reshape(ql, HEADS, QK_NOPE + QK_ROPE)
    w_uq_n = w_uq_f[:, :, :QK_NOPE].reshape(ql, HEADS * QK_NOPE)
    w_uq_r = w_uq_f[:, :, QK_NOPE:].reshape(ql, HEADS * QK_ROPE)
    y_a = mm_nn("branch_a", a, w_a_f, ACT)
    qln = rms_fwd_cols("q_norm", z, o_q, ql, row(q_norm_g))
    kvn = rms_fwd_cols("kv_norm", z, o_kv, kvl, row(kv_norm_g))
    qn = mm_nn("q_nope", qln, w_uq_n, BF16)
    qp = mm_nn("q_rope", qln, w_uq_r, F32)
    kv = mm_nn_b3("kv_up", kvn, g_ukv, BF16)
    kpr = rope_k(z, o_kpe, cos4, sin4)
    o, qpr, lse = attn_fwd2(qn, qp, kv, kpr, cos4, sin4)
    ag_up = ag_forward("ag_forward_up", ag_up, o)
    ag_down = ag_start("ag_start_down", [sh["w_down"]], _TOKENS.pending[-1])
    y_b = mm_nn("branch_b", o, w_b_f, ACT)
    merged = merge_fwd(z, o_ga, o_gb, y_a, y_b)
    y1 = mm_nn("out_proj", merged, w_out_f, ACT)
    x1 = post_res_fwd("post1_fwd", x2, y1, gate1, row(post_norm1_g))
    h2 = norm_mod_fwd("pre2_fwd", x1, row(pre_norm2_g), scale2, shift2)
    (g_up,) = ag_wait("ag_wait_up", ag_up, h2)
    upre = mm_nn_b3("up_proj", h2, g_up, ACT)
    ag_down = ag_forward("ag_forward_down", ag_down, upre)
    cw = g_cw.transpose(1, 0, 2).reshape(3, f2)
    cb = row(conv_b)
    f = conv_act_fwd(upre, cw, cb)
    w_down_f = ag_wait("ag_wait_down", ag_down, f)[0].reshape(-1, d)
    ffn = mm_nn("down_proj", f, w_down_f, ACT)
    loss_acc, dout, dffn, acc2 = post2_loss_bwd(x1, ffn, tgt, gate2, row(post_norm2_g))
    loss = lax.psum(loss_acc[0, 0], ("x", "y", "c"))
    _TOKENS.push(jnp.broadcast_to(loss, (8, LANES)))

    blocks = lambda g: g.reshape(N_DEV, g.shape[0] // N_DEV, g.shape[1])
    core = lax.axis_index("c").astype(jnp.int32).reshape(1)
    rs = {}

    def rs_begin(key, grads):
        rs[key] = rs_d2d_start("rs_d2d_start_" + key, grads)

    def rs_middle(key, after):
        grads, lands = rs_d2d_wait("rs_d2d_wait_" + key, rs[key], after)
        sums = [pair_sum("pair_sum_%s_%d" % (key, i), g, l, core) for i, (g, l) in enumerate(zip(grads, lands))]
        rs[key] = rs_ici_start("rs_ici_start_" + key, sums)

    gw_down = mm_tn("g_w_down", f, dffn, BF16)
    rs_begin("down", [blocks(gw_down)])
    df = mm_nt("d_f", dffn, w_down_f, ACT)
    rs_middle("down", df)
    dupre, gcw_g, gcw_v, gcb_g, gcb_v = conv_act_bwd(upre, cw, cb, df)
    gw_up3 = mm_tn_h3("g_w_up", h2, dupre, N_DEV, BF16)
    rs_begin("up", [gw_up3])
    dh2 = mm_nt_h3("d_h2", dupre, g_up, ACT)
    rs_middle("up", dh2)
    dx1, dy1, acc_mid = mid_bwd(dh2, dout, x1, y1, row(pre_norm2_g), scale2, gate1, row(post_norm1_g))
    gw_out = mm_tn("g_w_out", merged, dy1, BF16)
    dmerged = mm_nt("d_merged", dy1, w_out_f, ACT)
    dya, dyb, dga, dgb = merge_bwd(z, o_ga, o_gb, y_a, y_b, dmerged)
    gw_a = mm_tn("g_w_a", a, dya, BF16)
    gw_b = mm_tn("g_w_b", o, dyb, BF16)
    rs_begin("mid", [blocks(gw_out), blocks(gw_a), blocks(gw_b)])
    da = mm_nt("d_a", dya, w_a_f, ACT)
    do = mm_nt("d_o", dyb, w_b_f, ACT)
    rs_middle("mid", do)
    duv, g_ws, g_bs3, acc_gm = gmlp_bwd(z, gmw, da, ln_g, ln_b, wm, bs3)
    dqn, dqp, dkv, dkp = attn_bwd2(qn, qpr, kv, kpr, o, do, lse, cos4, sin4)
    dkpe = kpe_bwd(dkp, cos4, sin4, zp - o_kpe)
    dq_cat = jnp.concatenate([dqn, dqp], axis=1)
    w_uq_cat = jnp.concatenate([w_uq_n, w_uq_r], axis=1)
    gw_uq_cat = mm_tn("g_w_uq", qln, dq_cat, BF16)
    gw_uq_f = jnp.concatenate([gw_uq_cat[:, :HEADS * QK_NOPE].reshape(ql, HEADS, QK_NOPE),
                               gw_uq_cat[:, HEADS * QK_NOPE:].reshape(ql, HEADS, QK_ROPE)], axis=2)
    gw_uq3 = gw_uq_f.reshape(ql, N_DEV, -1).transpose(1, 0, 2)
    gw_ukv3 = mm_tn_o3("g_w_ukv", kvn, dkv, N_DEV, BF16)
    rs_begin("mla", [gw_uq3, gw_ukv3])
    dqln = mm_nt("d_qln", dq_cat, w_uq_cat, ACT)
    dq_lat, g_qnorm = rms_bwd_cols("q_norm_bwd", dqln, z, o_q, ql, row(q_norm_g))
    dkvn = mm_nt_b3("d_kvn", dkv, g_ukv, ACT)
    rs_middle("mla", dkvn)
    dkv_lat, g_kvnorm = rms_bwd_cols("kv_norm_bwd", dkvn, z, o_kv, kvl, row(kv_norm_g))
    dz = jnp.concatenate([duv, dq_lat, dkv_lat, dga, dgb, dkpe], axis=1)
    gw_in_p = mm_tn("g_w_in", dz, h1, BF16)
    gw_in_f = jnp.concatenate([gw_in_p[:src_kpe], gw_in_p[o_kpe:o_kpe + QK_ROPE], gw_in_p[src_kpe:o_kpe]], axis=0)
    rs_begin("in", [gw_in_f.reshape(N_DEV, -1, d)])
    dh1 = mm_nn("d_h1", dz, w_in_p, ACT)
    grad_x, acc1 = pre1_bwd(dh1, dx1, x2, row(pre_norm1_g), scale1)

    dmod = jnp.concatenate([acc1[0], acc1[1], acc_mid[3], acc_mid[0], acc_mid[1], acc2[0]])
    small = [("pre_norm1_g", acc1[2]), ("gm_ln_g", acc_gm[0]), ("gm_ln_b", acc_gm[1]), ("gm_b_s", g_bs3.reshape(-1)),
             ("q_norm_g", g_qnorm[0]), ("kv_norm_g", g_kvnorm[0]), ("post_norm1_g", acc_mid[4]), ("pre_norm2_g", acc_mid[2]),
             ("conv_b", jnp.concatenate([gcb_g[0], gcb_v[0]])), ("post_norm2_g", acc2[1]), ("gm_w_s", g_ws.reshape(-1)),
             ("b_ada", dmod)]
    n_small = sum(v.shape[0] for _, v in small)
    n_cw = 3 * f2
    n_pack = _round_up(n_small + n_cw, PACK_ALIGN)
    tail = jnp.zeros((n_pack - n_small - n_cw,), F32)
    packed = jnp.concatenate([v for _, v in small] + [jnp.concatenate([gcw_g, gcw_v], axis=1).reshape(-1), tail])
    ag_small = ag_start("ag_start_small", [packed.reshape(-1, LANES)], packed)
    rs_middle("in", [packed, _TOKENS.pending[-1]])

    res = {}
    last = packed
    for key, names in (("down", ["w_down"]), ("up", ["w_up"]), ("mid", ["w_out", "w_branch_a", "w_branch_b"]), ("mla", ["w_uq", "w_ukv"])):
        parts = rs_ici_wait("rs_ici_wait_" + key, rs[key], last)
        for k, p in zip(names, parts):
            res[k] = adamw("adamw_" + k, weights[k], mom1[k], mom2[k], p)
            last = res[k][0]

    assert [k for k, _ in small] == small_names and n_small == n_small_early
    (gathered,) = ag_wait("ag_wait_small", ag_forward("ag_forward_small", ag_small, last), last)
    sm = [t.reshape(-1) for t in adamw("adamw_small", *packed_state, gathered)]
    off = 0
    for k, v in small:
        res[k] = tuple(t[off:off + v.shape[0]].reshape(weights[k].shape) for t in sm)
        off += v.shape[0]

    cs_cw = conv_w.shape[1]
    g_cw_full = sm[0][n_small:n_small + n_cw].reshape(3, f2)
    g_cw_mine = lax.dynamic_slice(g_cw_full, (0, my * cs_cw), (3, cs_cw))
    res["conv_w"] = adamw("adamw_conv_w", conv_w, mom1["conv_w"], mom2["conv_w"], g_cw_mine[None])

    cs_ada = w_ada.shape[1]
    off_b = n_small - N_MOD * d
    dmod_all = gathered.reshape(N_DEV, -1)[:, off_b:off_b + N_MOD * d]
    dmod_cols = lax.dynamic_slice(dmod_all, (0, my * cs_ada), (N_DEV, cs_ada))
    res["w_ada"] = adamw("adamw_w_ada", w_ada, mom1["w_ada"], mom2["w_ada"], ada_bwd_w(sc_all, dmod_cols))

    (p_in,) = rs_ici_wait("rs_ici_wait_in", rs["in"], res["w_ada"][0])
    res["w_in"] = tuple(t.T for t in adamw("adamw_w_in", w_in.T, mom1["w_in"].T, mom2["w_in"].T, p_in))

    _TOKENS.clear()
    outs = [loss, grad_x[None]]
    for i in range(4):
        outs += [res[k][i] for k in order]
    return tuple(outs)
```

```python
import functools

import jax
import jax.numpy as jnp
from jax import lax
from jax.experimental import pallas as pl
from jax.experimental.pallas import tpu as pltpu

F32 = jnp.float32
BF16 = jnp.bfloat16
ACT = BF16

N_DEV = 8
HEADS = 16
QK_NOPE = 128
QK_ROPE = 64
V_HEAD = 128
CHUNK = 128
ROPE_THETA = 10000.0
EPS = 1e-6
N_MOD = 6
ADAM_LR, ADAM_B1, ADAM_B2, ADAM_EPS, ADAM_WD, ADAM_STEP = 0.001, 0.9, 0.999, 1e-08, 0.01, 10

LANES = 128
VMEM_LIMIT_BYTES = 48 * 2 ** 20
ROW_TILE = 256
COL_TILE = 256
ATT_TILE = 512
Z_PAD = 512
ADAMW_TILE_ELEMS = 1 << 18
PACK_ALIGN = 8 * LANES
MESH = pl.DeviceIdType.MESH


def _params(*sem):
    return pltpu.CompilerParams(dimension_semantics=sem if sem else None, vmem_limit_bytes=VMEM_LIMIT_BYTES)


def _tile(dim, target):
    t = (min(dim, target) // LANES) * LANES
    while t >= LANES:
        if dim % t == 0:
            return t
        t -= LANES
    return dim


def _full(shape):
    nd = len(shape)
    return pl.BlockSpec(shape, lambda *_: (0,) * nd)


class _Tokens:
    KEEP = 2

    def __init__(self):
        self.pending = []

    def push(self, token):
        self.pending = (self.pending + [token])[-self.KEEP:]

    def take(self):
        return list(self.pending)

    def clear(self):
        self.pending = []


_TOKENS = _Tokens()


def _matmul(name, a, b, *, grid, a_spec, b_spec, o_spec, out_shape, contract, acc_shape, split=1):
    nk = grid[2]
    deps = _TOKENS.take()

    def product(a_ref, b_ref):
        if len(b_ref.shape) == 2:
            return lax.dot_general(a_ref[...].astype(BF16), b_ref[...].astype(BF16), (contract, ((), ())), preferred_element_type=F32)
        cs = b_ref.shape[2]
        return sum(lax.dot_general(a_ref[:, s * cs:(s + 1) * cs].astype(BF16), b_ref[s].astype(BF16), (contract, ((), ())),
                                   preferred_element_type=F32) for s in range(split))

    def body_one_step(a_ref, b_ref, *rest):
        o_ref = rest[len(deps)]
        o_ref[...] = product(a_ref, b_ref).astype(o_ref.dtype)

    def body(a_ref, b_ref, *rest):
        o_ref, acc_ref = rest[len(deps):]
        k = pl.program_id(2)

        @pl.when(k == 0)
        def _():
            acc_ref[...] = jnp.zeros_like(acc_ref)

        acc_ref[...] += product(a_ref, b_ref)

        @pl.when(k == nk - 1)
        def _():
            o_ref[...] = acc_ref[...].astype(o_ref.dtype)

    return pl.pallas_call(
        body_one_step if nk == 1 else body, name=name, grid=grid,
        in_specs=[a_spec, b_spec] + [pl.BlockSpec(memory_space=pl.ANY)] * len(deps),
        out_specs=o_spec, out_shape=out_shape, scratch_shapes=[] if nk == 1 else [pltpu.VMEM(acc_shape, F32)],
        compiler_params=_params("parallel", "parallel", "arbitrary"))(a, b, *deps)


T_OUT, T_OUT_WIDE, TK = 1024, 1408, 2816


def _out_tile(dim):
    return T_OUT_WIDE if dim % T_OUT_WIDE == 0 else _tile(dim, T_OUT)


def _tk(a, b):
    return TK if a.dtype == BF16 and b.dtype == BF16 else TK // 2


def mm_nn(name, a, b, dtype):
    (m, k), n = a.shape, b.shape[1]
    tm, tn, tk = _out_tile(m), _out_tile(n), _tile(k, _tk(a, b))
    return _matmul(name, a, b, grid=(m // tm, n // tn, k // tk),
                   a_spec=pl.BlockSpec((tm, tk), lambda i, j, kk: (i, kk)),
                   b_spec=pl.BlockSpec((tk, tn), lambda i, j, kk: (kk, j)),
                   o_spec=pl.BlockSpec((tm, tn), lambda i, j, kk: (i, j)),
                   out_shape=jax.ShapeDtypeStruct((m, n), dtype), contract=((1,), (0,)), acc_shape=(tm, tn))


def mm_nn_b3(name, a, b3, dtype):
    (m, k), (nj, _, cs) = a.shape, b3.shape
    tm, tk = _out_tile(m), _tile(k, _tk(a, b3))
    return _matmul(name, a, b3, grid=(m // tm, nj, k // tk),
                   a_spec=pl.BlockSpec((tm, tk), lambda i, j, kk: (i, kk)),
                   b_spec=pl.BlockSpec((None, tk, cs), lambda i, j, kk: (j, kk, 0)),
                   o_spec=pl.BlockSpec((tm, cs), lambda i, j, kk: (i, j)),
                   out_shape=jax.ShapeDtypeStruct((m, nj * cs), dtype), contract=((1,), (0,)), acc_shape=(tm, cs))


def mm_nt(name, a, b, dtype):
    (m, k), n = a.shape, b.shape[0]
    tm, tn, tk = _out_tile(m), _out_tile(n), _tile(k, _tk(a, b))
    return _matmul(name, a, b, grid=(m // tm, n // tn, k // tk),
                   a_spec=pl.BlockSpec((tm, tk), lambda i, j, kk: (i, kk)),
                   b_spec=pl.BlockSpec((tn, tk), lambda i, j, kk: (j, kk)),
                   o_spec=pl.BlockSpec((tm, tn), lambda i, j, kk: (i, j)),
                   out_shape=jax.ShapeDtypeStruct((m, n), dtype), contract=((1,), (1,)), acc_shape=(tm, tn))


def mm_nt_b3(name, a, b3, dtype):
    m, (nj, n, cs) = a.shape[0], b3.shape
    tm, tn = _out_tile(m), _out_tile(n)
    return _matmul(name, a, b3, grid=(m // tm, n // tn, nj),
                   a_spec=pl.BlockSpec((tm, cs), lambda i, j, kk: (i, kk)),
                   b_spec=pl.BlockSpec((None, tn, cs), lambda i, j, kk: (kk, j, 0)),
                   o_spec=pl.BlockSpec((tm, tn), lambda i, j, kk: (i, j)),
                   out_shape=jax.ShapeDtypeStruct((m, n), dtype), contract=((1,), (1,)), acc_shape=(tm, tn))


def mm_nt_h3(name, a3, b3, dtype):
    (_, m, _), (nj, n, cs) = a3.shape, b3.shape
    tm, tn, hj = _out_tile(m), _out_tile(n), nj // 2
    pair = 2 if hj % 2 == 0 else 1
    return _matmul(name, a3, b3.reshape(nj // pair, pair, n, cs), grid=(m // tm, n // tn, nj // pair),
                   a_spec=pl.BlockSpec((None, tm, pair * cs), lambda i, j, kk: (kk // (hj // pair), i, kk % (hj // pair))),
                   b_spec=pl.BlockSpec((None, pair, tn, cs), lambda i, j, kk: (kk, 0, j, 0)),
                   o_spec=pl.BlockSpec((tm, tn), lambda i, j, kk: (i, j)),
                   out_shape=jax.ShapeDtypeStruct((m, n), dtype), contract=((1,), (1,)), acc_shape=(tm, tn), split=pair)


def mm_tn_h3(name, a, b3, nj, dtype):
    (k, m), half = a.shape, b3.shape[2]
    hj = nj // 2
    cs = half // hj
    tm, tk = _out_tile(m), _tile(k, _tk(a, b3))
    return _matmul(name, a, b3, grid=(m // tm, nj, k // tk),
                   a_spec=pl.BlockSpec((tk, tm), lambda i, j, kk: (kk, i)),
                   b_spec=pl.BlockSpec((None, tk, cs), lambda i, j, kk: (j // hj, kk, j % hj)),
                   o_spec=pl.BlockSpec((None, tm, cs), lambda i, j, kk: (j, i, 0)),
                   out_shape=jax.ShapeDtypeStruct((nj, m, cs), dtype), contract=((0,), (0,)), acc_shape=(tm, cs))


def mm_tn(name, a, b, dtype):
    (k, m), n = a.shape, b.shape[1]
    tm, tn, tk = _out_tile(m), _out_tile(n), _tile(k, _tk(a, b))
    return _matmul(name, a, b, grid=(m // tm, n // tn, k // tk),
                   a_spec=pl.BlockSpec((tk, tm), lambda i, j, kk: (kk, i)),
                   b_spec=pl.BlockSpec((tk, tn), lambda i, j, kk: (kk, j)),
                   o_spec=pl.BlockSpec((tm, tn), lambda i, j, kk: (i, j)),
                   out_shape=jax.ShapeDtypeStruct((m, n), dtype), contract=((0,), (0,)), acc_shape=(tm, tn))


def mm_tn_o3(name, a, b, nj, dtype):
    (k, m), n = a.shape, b.shape[1]
    cs = n // nj
    tm, tk = _out_tile(m), _tile(k, _tk(a, b))
    return _matmul(name, a, b, grid=(m // tm, nj, k // tk),
                   a_spec=pl.BlockSpec((tk, tm), lambda i, j, kk: (kk, i)),
                   b_spec=pl.BlockSpec((tk, cs), lambda i, j, kk: (kk, j)),
                   o_spec=pl.BlockSpec((None, tm, cs), lambda i, j, kk: (j, i, 0)),
                   out_shape=jax.ShapeDtypeStruct((nj, m, cs), dtype), contract=((0,), (0,)), acc_shape=(tm, cs))


_GELU_C = 0.7978845608028654
_GELU_A = 0.044715


def _f32(ref):
    return ref[...].astype(F32)


def _gelu(x):
    x = x.astype(F32)
    return 0.5 * x * (1.0 + jnp.tanh(_GELU_C * (x + _GELU_A * x * x * x)))


def _gelu_and_grad(x):
    x = x.astype(F32)
    t = jnp.tanh(_GELU_C * (x + _GELU_A * x * x * x))
    y = 0.5 * x * (1.0 + t)
    dy = 0.5 * (1.0 + t) + 0.5 * x * (1.0 - t * t) * (_GELU_C * (1.0 + 3.0 * _GELU_A * x * x))
    return y, dy


def _sigmoid(x):
    return 1.0 / (1.0 + jnp.exp(-x.astype(F32)))


def _rms_stats(x):
    x = x.astype(F32)
    inv = lax.rsqrt(jnp.mean(x * x, axis=-1, keepdims=True) + EPS)
    return inv, x * inv


def _rms_bwd(dyhat, yhat, inv):
    return inv * (dyhat - yhat * jnp.mean(dyhat * yhat, axis=-1, keepdims=True))


def _colsum(x):
    return jnp.sum(x, axis=0, keepdims=True)


def _rope(x, cos4, sin4):
    lane = lax.broadcasted_iota(jnp.int32, x.shape, x.ndim - 1)
    first_half = (lane % QK_ROPE) < (QK_ROPE // 2)
    partner = jnp.where(first_half, pltpu.roll(x, LANES - QK_ROPE // 2, x.ndim - 1), pltpu.roll(x, QK_ROPE // 2, x.ndim - 1))
    return x * cos4 + partner * sin4


def norm_mod_fwd(name, x, g, scale, shift):
    s, d = x.shape
    tr = _tile(s, ROW_TILE)

    def body(x_ref, g_ref, sc_ref, sh_ref, o_ref):
        _, xh = _rms_stats(x_ref[...])
        o_ref[...] = (xh * g_ref[...] * (1.0 + sc_ref[...]) + sh_ref[...]).astype(o_ref.dtype)

    row = pl.BlockSpec((tr, d), lambda i: (i, 0))
    vec = pl.BlockSpec((1, d), lambda i: (0, 0))
    return pl.pallas_call(body, name=name, grid=(s // tr,), in_specs=[row, vec, vec, vec], out_specs=row,
                          out_shape=jax.ShapeDtypeStruct((s, d), BF16), compiler_params=_params("parallel"))(x, g, scale, shift)


def rms_fwd_cols(name, z, off, width, g):
    s = z.shape[0]
    tr = _tile(s, ROW_TILE)
    assert off % width == 0

    def body(x_ref, g_ref, o_ref):
        _, xh = _rms_stats(x_ref[...])
        o_ref[...] = (xh * g_ref[...]).astype(o_ref.dtype)

    return pl.pallas_call(body, name=name, grid=(s // tr,),
                          in_specs=[pl.BlockSpec((tr, width), lambda i: (i, off // width)), pl.BlockSpec((1, width), lambda i: (0, 0))],
                          out_specs=pl.BlockSpec((tr, width), lambda i: (i, 0)),
                          out_shape=jax.ShapeDtypeStruct((s, width), BF16), compiler_params=_params("parallel"))(z, g)


def rms_bwd_cols(name, dy, z, off, width, g):
    s = z.shape[0]
    tr = _tile(s, ROW_TILE)

    def body(dy_ref, x_ref, g_ref, dx_ref, gg_ref):
        @pl.when(pl.program_id(0) == 0)
        def _():
            gg_ref[...] = jnp.zeros_like(gg_ref)

        inv, xh = _rms_stats(x_ref[...])
        dy_v = _f32(dy_ref)
        gg_ref[...] += _colsum(dy_v * xh)
        dx_ref[...] = _rms_bwd(dy_v * g_ref[...], xh, inv).astype(dx_ref.dtype)

    return pl.pallas_call(body, name=name, grid=(s // tr,),
                          in_specs=[pl.BlockSpec((tr, width), lambda i: (i, 0)), pl.BlockSpec((tr, width), lambda i: (i, off // width)),
                                    pl.BlockSpec((1, width), lambda i: (0, 0))],
                          out_specs=[pl.BlockSpec((tr, width), lambda i: (i, 0)), pl.BlockSpec((1, width), lambda i: (0, 0))],
                          out_shape=[jax.ShapeDtypeStruct((s, width), BF16), jax.ShapeDtypeStruct((1, width), F32)],
                          compiler_params=_params("arbitrary"))(dy, z, g)


def post_res_fwd(name, x, y, gate, g):
    s, d = x.shape
    tr = _tile(s, ROW_TILE)

    def body(x_ref, y_ref, gate_ref, g_ref, o_ref):
        _, yh = _rms_stats(y_ref[...])
        o_ref[...] = x_ref[...] + gate_ref[...] * (yh * g_ref[...])

    row = pl.BlockSpec((tr, d), lambda i: (i, 0))
    vec = pl.BlockSpec((1, d), lambda i: (0, 0))
    return pl.pallas_call(body, name=name, grid=(s // tr,), in_specs=[row, row, vec, vec], out_specs=row,
                          out_shape=jax.ShapeDtypeStruct((s, d), F32), compiler_params=_params("parallel"))(x, y, gate, g)


def post2_loss_bwd(x1, ffn, target, gate2, g):
    s, d = x1.shape
    tr = _tile(s, ROW_TILE)

    def body(x_ref, y_ref, t_ref, gate_ref, g_ref, loss_ref, dout_ref, dy_ref, acc_ref):
        @pl.when(pl.program_id(0) == 0)
        def _():
            loss_ref[...] = jnp.zeros_like(loss_ref)
            acc_ref[...] = jnp.zeros_like(acc_ref)

        inv, yh = _rms_stats(y_ref[...])
        r = yh * g_ref[...]
        err = x_ref[...] + gate_ref[...] * r - t_ref[...]
        loss_ref[...] += 0.5 * jnp.sum(jnp.mean(err * err, axis=-1, keepdims=True))
        dout = err / d
        dout_ref[...] = dout
        dr = dout * gate_ref[...]
        acc_ref[0:1, :] += _colsum(dout * r)
        acc_ref[1:2, :] += _colsum(dr * yh)
        dy_ref[...] = _rms_bwd(dr * g_ref[...], yh, inv).astype(dy_ref.dtype)

    row = pl.BlockSpec((tr, d), lambda i: (i, 0))
    vec = pl.BlockSpec((1, d), lambda i: (0, 0))
    return pl.pallas_call(
        body, name="post2_loss_bwd", grid=(s // tr,), in_specs=[row, row, row, vec, vec],
        out_specs=[_full((8, LANES)), row, row, _full((8, d))],
        out_shape=[jax.ShapeDtypeStruct((8, LANES), F32), jax.ShapeDtypeStruct((s, d), F32),
                   jax.ShapeDtypeStruct((s, d), BF16), jax.ShapeDtypeStruct((8, d), F32)],
        compiler_params=_params("arbitrary"))(x1, ffn, target, gate2, g)


def mid_bwd(dh2, dout, x1, y1, pre2_g, scale2, gate1, post1_g):
    s, d = x1.shape
    tr = _tile(s, ROW_TILE)

    def body(dh_ref, dout_ref, x_ref, y_ref, g2_ref, sc_ref, gate_ref, g1_ref, dx_ref, dy_ref, acc_ref):
        @pl.when(pl.program_id(0) == 0)
        def _():
            acc_ref[...] = jnp.zeros_like(acc_ref)

        dh = _f32(dh_ref)
        inv2, xh = _rms_stats(x_ref[...])
        acc_ref[0:1, :] += _colsum(dh)
        acc_ref[1:2, :] += _colsum(dh * (xh * g2_ref[...]))
        t = dh * (1.0 + sc_ref[...])
        acc_ref[2:3, :] += _colsum(t * xh)
        dx1 = dout_ref[...] + _rms_bwd(t * g2_ref[...], xh, inv2)
        dx_ref[...] = dx1
        inv1, yh = _rms_stats(y_ref[...])
        acc_ref[3:4, :] += _colsum(dx1 * (yh * g1_ref[...]))
        dr = dx1 * gate_ref[...]
        acc_ref[4:5, :] += _colsum(dr * yh)
        dy_ref[...] = _rms_bwd(dr * g1_ref[...], yh, inv1).astype(dy_ref.dtype)

    row = pl.BlockSpec((tr, d), lambda i: (i, 0))
    vec = pl.BlockSpec((1, d), lambda i: (0, 0))
    return pl.pallas_call(
        body, name="mid_bwd", grid=(s // tr,), in_specs=[row, row, row, row, vec, vec, vec, vec],
        out_specs=[row, row, _full((8, d))],
        out_shape=[jax.ShapeDtypeStruct((s, d), F32), jax.ShapeDtypeStruct((s, d), BF16), jax.ShapeDtypeStruct((8, d), F32)],
        compiler_params=_params("arbitrary"))(dh2, dout, x1, y1, pre2_g, scale2, gate1, post1_g)


def pre1_bwd(dh1, dx1, x, pre1_g, scale1):
    s, d = x.shape
    tr = _tile(s, ROW_TILE)

    def body(dh_ref, dx1_ref, x_ref, g_ref, sc_ref, dx_ref, acc_ref):
        @pl.when(pl.program_id(0) == 0)
        def _():
            acc_ref[...] = jnp.zeros_like(acc_ref)

        dh = _f32(dh_ref)
        inv, xh = _rms_stats(x_ref[...])
        acc_ref[0:1, :] += _colsum(dh)
        acc_ref[1:2, :] += _colsum(dh * (xh * g_ref[...]))
        t = dh * (1.0 + sc_ref[...])
        acc_ref[2:3, :] += _colsum(t * xh)
        dx_ref[...] = dx1_ref[...] + _rms_bwd(t * g_ref[...], xh, inv)

    row = pl.BlockSpec((tr, d), lambda i: (i, 0))
    vec = pl.BlockSpec((1, d), lambda i: (0, 0))
    return pl.pallas_call(
        body, name="pre1_bwd", grid=(s // tr,), in_specs=[row, row, row, vec, vec], out_specs=[row, _full((8, d))],
        out_shape=[jax.ShapeDtypeStruct((s, d), F32), jax.ShapeDtypeStruct((8, d), F32)],
        compiler_params=_params("arbitrary"))(dh1, dx1, x, pre1_g, scale1)


def _ln_stats(v):
    mu = jnp.mean(v, axis=-1, keepdims=True)
    vc = v - mu
    rstd = lax.rsqrt(jnp.mean(vc * vc, axis=-1, keepdims=True) + EPS)
    return rstd, vc * rstd


def gmlp_fwd(z, width, ln_g, ln_b, wm, bs3):
    s = z.shape[0]
    groups = width // CHUNK

    def body(u_ref, v_ref, g_ref, b_ref, wm_ref, bs_ref, a_ref):
        ug = _gelu(u_ref[...])
        _, vh = _ln_stats(_gelu(v_ref[...]))
        vn = (vh * g_ref[...] + b_ref[...]).astype(BF16)
        for g in range(groups):
            cols = slice(g * CHUNK, (g + 1) * CHUNK)
            mixed = jnp.dot(wm_ref[g], vn[:, cols], preferred_element_type=F32) + bs_ref[g]
            a_ref[:, cols] = (ug[:, cols] * mixed).astype(a_ref.dtype)

    vec = pl.BlockSpec((1, width), lambda n: (0, 0))
    return pl.pallas_call(
        body, name="gmlp_fwd", grid=(s // CHUNK,),
        in_specs=[pl.BlockSpec((CHUNK, width), lambda n: (n, 0)), pl.BlockSpec((CHUNK, width), lambda n: (n, 1)), vec, vec,
                  _full(wm.shape), _full(bs3.shape)],
        out_specs=pl.BlockSpec((CHUNK, width), lambda n: (n, 0)),
        out_shape=jax.ShapeDtypeStruct((s, width), BF16), compiler_params=_params("parallel"))(z, z, ln_g, ln_b, wm, bs3)


def gmlp_bwd(z, width, da, ln_g, ln_b, wm, bs3):
    s = z.shape[0]
    groups = width // CHUNK

    def body(u_ref, v_ref, da_ref, g_ref, b_ref, wm_ref, bs_ref, duv_ref, gw_ref, gb_ref, acc_ref, dvn_ref):
        @pl.when(pl.program_id(0) == 0)
        def _():
            gw_ref[...] = jnp.zeros_like(gw_ref)
            gb_ref[...] = jnp.zeros_like(gb_ref)
            acc_ref[...] = jnp.zeros_like(acc_ref)

        ug, dug = _gelu_and_grad(u_ref[...])
        vg, dvg = _gelu_and_grad(v_ref[...])
        rstd, vh = _ln_stats(vg)
        vn = (vh * g_ref[...] + b_ref[...]).astype(BF16)
        da_v = _f32(da_ref)
        for g in range(groups):
            cols = slice(g * CHUNK, (g + 1) * CHUNK)
            mixed = jnp.dot(wm_ref[g], vn[:, cols], preferred_element_type=F32) + bs_ref[g]
            duv_ref[:, cols] = (da_v[:, cols] * mixed * dug[:, cols]).astype(duv_ref.dtype)
            dm = da_v[:, cols] * ug[:, cols]
            gb_ref[g] += jnp.sum(dm, axis=-1, keepdims=True)
            dmb = dm.astype(BF16)
            gw_ref[g] += lax.dot_general(dmb, vn[:, cols], (((1,), (1,)), ((), ())), preferred_element_type=F32)
            dvn_ref[:, cols] = lax.dot_general(wm_ref[g], dmb, (((0,), (0,)), ((), ())), preferred_element_type=F32)
        dvn = dvn_ref[...]
        acc_ref[0:1, :] += _colsum(dvn * vh)
        acc_ref[1:2, :] += _colsum(dvn)
        dvh = dvn * g_ref[...]
        dv = rstd * (dvh - jnp.mean(dvh, axis=-1, keepdims=True) - vh * jnp.mean(dvh * vh, axis=-1, keepdims=True))
        duv_ref[:, width:] = (dv * dvg).astype(duv_ref.dtype)

        @pl.when(pl.program_id(0) == pl.num_programs(0) - 1)
        def _():
            q = lax.broadcasted_iota(jnp.int32, gw_ref.shape, 1)
            p = lax.broadcasted_iota(jnp.int32, gw_ref.shape, 2)
            gw_ref[...] = jnp.where(p <= q, gw_ref[...], 0.0)

    vec = pl.BlockSpec((1, width), lambda n: (0, 0))
    blk = pl.BlockSpec((CHUNK, width), lambda n: (n, 0))
    return pl.pallas_call(
        body, name="gmlp_bwd", grid=(s // CHUNK,),
        in_specs=[blk, pl.BlockSpec((CHUNK, width), lambda n: (n, 1)), blk, vec, vec, _full(wm.shape), _full(bs3.shape)],
        out_specs=[pl.BlockSpec((CHUNK, 2 * width), lambda n: (n, 0)), _full(wm.shape), _full(bs3.shape), _full((8, width))],
        out_shape=[jax.ShapeDtypeStruct((s, 2 * width), BF16), jax.ShapeDtypeStruct(wm.shape, F32),
                   jax.ShapeDtypeStruct(bs3.shape, F32), jax.ShapeDtypeStruct((8, width), F32)],
        scratch_shapes=[pltpu.VMEM((CHUNK, width), F32)],
        compiler_params=_params("arbitrary"))(z, z, da, ln_g, ln_b, wm, bs3)


def merge_fwd(z, off_a, off_b, ya, yb):
    s, d = ya.shape
    tr, tc = _tile(s, ROW_TILE * 2), _tile(d, COL_TILE)
    assert off_a % tc == 0 and off_b % tc == 0

    def body(ga_ref, gb_ref, ya_ref, yb_ref, o_ref):
        o_ref[...] = (_sigmoid(ga_ref[...]) * _f32(ya_ref) + _sigmoid(gb_ref[...]) * _f32(yb_ref)).astype(o_ref.dtype)

    blk = pl.BlockSpec((tr, tc), lambda i, j: (i, j))
    return pl.pallas_call(
        body, name="merge_fwd", grid=(s // tr, d // tc),
        in_specs=[pl.BlockSpec((tr, tc), lambda i, j: (i, off_a // tc + j)), pl.BlockSpec((tr, tc), lambda i, j: (i, off_b // tc + j)), blk, blk],
        out_specs=blk, out_shape=jax.ShapeDtypeStruct((s, d), BF16), compiler_params=_params("parallel", "parallel"))(z, z, ya, yb)


def merge_bwd(z, off_a, off_b, ya, yb, dm):
    s, d = ya.shape
    tr, tc = _tile(s, ROW_TILE * 2), _tile(d, COL_TILE)
    nc = d // tc

    def body(ga_ref, gb_ref, ya_ref, yb_ref, dm_ref, dya_ref, dyb_ref, dga_ref, dgb_ref):
        dm_v = _f32(dm_ref)
        sa, sb = _sigmoid(ga_ref[...]), _sigmoid(gb_ref[...])
        dya_ref[...] = (dm_v * sa).astype(dya_ref.dtype)
        dyb_ref[...] = (dm_v * sb).astype(dyb_ref.dtype)
        dga_ref[...] = (dm_v * _f32(ya_ref) * sa * (1.0 - sa)).astype(dga_ref.dtype)
        dgb_ref[...] = (dm_v * _f32(yb_ref) * sb * (1.0 - sb)).astype(dgb_ref.dtype)

    blk = pl.BlockSpec((tr, tc), lambda i, j: (i, j))
    out = jax.ShapeDtypeStruct((s, d), BF16)
    return pl.pallas_call(
        body, name="merge_bwd", grid=(s // tr, nc),
        in_specs=[pl.BlockSpec((tr, tc), lambda i, j: (i, off_a // tc + j)), pl.BlockSpec((tr, tc), lambda i, j: (i, off_b // tc + j)), blk, blk, blk],
        out_specs=[blk, blk, blk, blk], out_shape=[out, out, out, out],
        compiler_params=_params("parallel", "parallel"))(z, z, ya, yb, dm)


_ATT_SCALE = (QK_NOPE + QK_ROPE) ** -0.5
_NEG = -1e30


def rope_k(z, off, cos4, sin4):
    s = z.shape[0]
    tr = _tile(s, ROW_TILE * 2)
    assert off % LANES == 0

    def body(k_ref, c_ref, s_ref, o_ref):
        k = _f32(k_ref)
        k = k + pltpu.roll(k, QK_ROPE, 1)
        o_ref[...] = _rope(k, c_ref[...], s_ref[...]).astype(o_ref.dtype)

    row = pl.BlockSpec((tr, LANES), lambda i: (i, 0))
    return pl.pallas_call(body, name="rope_k", grid=(s // tr,),
                          in_specs=[pl.BlockSpec((tr, LANES), lambda i: (i, off // LANES)), row, row], out_specs=row,
                          out_shape=jax.ShapeDtypeStruct((s, LANES), BF16), compiler_params=_params("parallel"))(z, cos4, sin4)


def _dot_nt(a, b):
    return lax.dot_general(a, b, (((1,), (1,)), ((), ())), preferred_element_type=F32)


def _dot_tn(a, b):
    return lax.dot_general(a, b, (((0,), (0,)), ((), ())), preferred_element_type=F32)


def _q_cat(q_n, qpr, hh):
    lane = lax.broadcasted_iota(jnp.int32, qpr.shape, 1)
    sel = (lane < QK_ROPE) if hh == 0 else (lane >= QK_ROPE)
    return jnp.concatenate([q_n, jnp.where(sel, qpr, jnp.zeros_like(qpr))], axis=1)


def _causal(sc):
    row = lax.broadcasted_iota(jnp.int32, sc.shape, 0)
    col = lax.broadcasted_iota(jnp.int32, sc.shape, 1)
    return jnp.where(col <= row, sc, _NEG)


def attn_fwd2(qn, qp, kv, kpr, cos4, sin4):
    s = qn.shape[0]
    hp = HEADS // 2
    t = _tile(s, ATT_TILE)
    nq = s // t

    def body(qn_ref, qp_ref, kv_ref, kp_ref, c_ref, s_ref, o_ref, qpr_ref, l_ref, kcat_ref):
        qi = pl.program_id(1)

        @pl.when(qi == 0)
        def _():
            for hh in range(2):
                kcat_ref[hh, :, 0:QK_NOPE] = kv_ref[:, 2 * hh * QK_NOPE:(2 * hh + 1) * QK_NOPE]
                kcat_ref[hh, :, QK_NOPE:] = kp_ref[...]

        qpr = _rope(qp_ref[...], c_ref[...], s_ref[...]).astype(BF16)
        qpr_ref[...] = qpr
        qcat = [_q_cat(qn_ref[:, hh * QK_NOPE:(hh + 1) * QK_NOPE], qpr, hh) for hh in range(2)]

        def block(kb, carry, diagonal):
            rows = pl.ds(pl.multiple_of(kb * t, t), t)
            out = []
            for hh in range(2):
                m, l, acc = carry[hh]
                sc = _dot_nt(qcat[hh], kcat_ref[hh, rows, :]) * _ATT_SCALE
                if diagonal:
                    sc = _causal(sc)
                m_new = jnp.maximum(m, jnp.max(sc, axis=-1, keepdims=True))
                alpha = jnp.exp(m - m_new)
                p = jnp.exp(sc - m_new)
                l = alpha * l + jnp.sum(p, axis=-1, keepdims=True)
                v = kv_ref[rows, (2 * hh + 1) * QK_NOPE:(2 * hh + 2) * QK_NOPE]
                acc = alpha * acc + jnp.dot(p.astype(BF16), v, preferred_element_type=F32)
                out.append((m_new, l, acc))
            return tuple(out)

        one = (jnp.full((t, 1), _NEG, F32), jnp.zeros((t, 1), F32), jnp.zeros((t, V_HEAD), F32))
        carry = lax.fori_loop(0, qi, lambda kb, cr: block(kb, cr, False), (one, one))
        carry = block(qi, carry, True)
        for hh in range(2):
            m, l, acc = carry[hh]
            o_ref[:, hh * V_HEAD:(hh + 1) * V_HEAD] = acc / l
            l_ref[:, hh:hh + 1] = m + jnp.log(l)

    return pl.pallas_call(
        body, name="attn_fwd", grid=(hp, nq),
        in_specs=[pl.BlockSpec((t, 2 * QK_NOPE), lambda h, i: (i, h)), pl.BlockSpec((t, LANES), lambda h, i: (i, h)),
                  pl.BlockSpec((s, 4 * QK_NOPE), lambda h, i: (0, h)), _full((s, LANES)),
                  pl.BlockSpec((t, LANES), lambda h, i: (i, 0)), pl.BlockSpec((t, LANES), lambda h, i: (i, 0))],
        out_specs=[pl.BlockSpec((t, 2 * V_HEAD), lambda h, i: (i, h)), pl.BlockSpec((t, LANES), lambda h, i: (i, h)),
                   pl.BlockSpec((None, t, 2), lambda h, i: (h, i, 0))],
        out_shape=[jax.ShapeDtypeStruct((s, HEADS * V_HEAD), F32), jax.ShapeDtypeStruct((s, HEADS * QK_ROPE), BF16),
                   jax.ShapeDtypeStruct((hp, s, 2), F32)],
        scratch_shapes=[pltpu.VMEM((2, s, 2 * QK_NOPE), BF16)],
        compiler_params=_params("parallel", "arbitrary"))(qn, qp, kv, kpr, cos4, sin4)


def attn_bwd2(qn, qpr, kv, kpr, o, do, lse, cos4, sin4):
    s = qn.shape[0]
    hp = HEADS // 2
    t = _tile(s, ATT_TILE)
    nk = s // t

    def body(qn_ref, qpr_ref, kv_ref, kp_ref, o_ref, do_ref, l_ref, c_ref, s_ref,
             dqn_ref, dqp_ref, dkv_ref, dkp_ref, qcat_ref, dq_ref, delta_ref):
        ki = pl.program_id(1)

        @pl.when(ki == 0)
        def _():
            dq_ref[...] = jnp.zeros_like(dq_ref)
            for hh in range(2):
                qcat_ref[hh] = _q_cat(qn_ref[:, hh * QK_NOPE:(hh + 1) * QK_NOPE], qpr_ref[...], hh)
                cols = slice(hh * V_HEAD, (hh + 1) * V_HEAD)
                delta_ref[hh] = jnp.sum(do_ref[:, cols] * o_ref[:, cols], axis=-1, keepdims=True)

        rows_k = pl.ds(pl.multiple_of(ki * t, t), t)
        kcat = [jnp.concatenate([kv_ref[rows_k, 2 * hh * QK_NOPE:(2 * hh + 1) * QK_NOPE], kp_ref[rows_k, :]], axis=1) for hh in range(2)]
        vs = [kv_ref[rows_k, (2 * hh + 1) * QK_NOPE:(2 * hh + 2) * QK_NOPE] for hh in range(2)]

        def block(qb, carry, diagonal):
            rows = pl.ds(pl.multiple_of(qb * t, t), t)
            out = []
            for hh in range(2):
                dkc, dv = carry[hh]
                q_c = qcat_ref[hh, rows, :]
                do_b = do_ref[rows, hh * V_HEAD:(hh + 1) * V_HEAD].astype(BF16)
                sc = _dot_nt(q_c, kcat[hh]) * _ATT_SCALE
                if diagonal:
                    sc = _causal(sc)
                p = jnp.exp(sc - l_ref[rows, hh:hh + 1])
                dpv = _dot_nt(do_b, vs[hh])
                ds = (p * (dpv - delta_ref[hh, rows, :]) * _ATT_SCALE).astype(BF16)
                dv = dv + _dot_tn(p.astype(BF16), do_b)
                dkc = dkc + _dot_tn(ds, q_c)
                dq_ref[hh, rows, :] += jnp.dot(ds, kcat[hh], preferred_element_type=F32)
                out.append((dkc, dv))
            return tuple(out)

        one = (jnp.zeros((t, 2 * QK_NOPE), F32), jnp.zeros((t, V_HEAD), F32))
        carry = block(ki, (one, one), True)
        carry = lax.fori_loop(ki + 1, nk, lambda qb, cr: block(qb, cr, False), carry)
        dkp = jnp.zeros((t, LANES), F32)
        for hh in range(2):
            dkc, dv = carry[hh]
            dkv_ref[:, 2 * hh * QK_NOPE:(2 * hh + 1) * QK_NOPE] = dkc[:, :QK_NOPE].astype(dkv_ref.dtype)
            dkv_ref[:, (2 * hh + 1) * QK_NOPE:(2 * hh + 2) * QK_NOPE] = dv.astype(dkv_ref.dtype)
            dkp = dkp + dkc[:, QK_NOPE:]
        dkp_ref[...] = dkp

        @pl.when(ki == nk - 1)
        def _():
            lane = lax.broadcasted_iota(jnp.int32, (s, LANES), 1)
            dqp = jnp.where(lane < QK_ROPE, dq_ref[0, :, QK_NOPE:], dq_ref[1, :, QK_NOPE:])
            dqp_ref[...] = _rope(dqp, c_ref[...], -s_ref[...]).astype(dqp_ref.dtype)
            for hh in range(2):
                dqn_ref[:, hh * QK_NOPE:(hh + 1) * QK_NOPE] = dq_ref[hh, :, :QK_NOPE].astype(dqn_ref.dtype)

    qblk = pl.BlockSpec((s, 2 * QK_NOPE), lambda h, i: (0, h))
    pblk = pl.BlockSpec((s, LANES), lambda h, i: (0, h))
    tab = _full((s, LANES))
    return pl.pallas_call(
        body, name="attn_bwd", grid=(hp, nk),
        in_specs=[qblk, pblk, pl.BlockSpec((s, 4 * QK_NOPE), lambda h, i: (0, h)), tab, qblk, qblk,
                  pl.BlockSpec((None, s, 2), lambda h, i: (h, 0, 0)), tab, tab],
        out_specs=[qblk, pblk, pl.BlockSpec((t, 4 * QK_NOPE), lambda h, i: (i, h)), pl.BlockSpec((None, t, LANES), lambda h, i: (h, i, 0))],
        out_shape=[jax.ShapeDtypeStruct((s, HEADS * QK_NOPE), BF16), jax.ShapeDtypeStruct((s, HEADS * QK_ROPE), BF16),
                   jax.ShapeDtypeStruct((s, HEADS * 2 * QK_NOPE), BF16), jax.ShapeDtypeStruct((hp, s, LANES), F32)],
        scratch_shapes=[pltpu.VMEM((2, s, 2 * QK_NOPE), BF16), pltpu.VMEM((2, s, 2 * QK_NOPE), F32), pltpu.VMEM((2, s, 1), F32)],
        compiler_params=_params("parallel", "arbitrary"))(qn, qpr, kv, kpr, o, do, lse, cos4, sin4)


def _causal_at(sc, row0, col0):
    row = lax.broadcasted_iota(jnp.int32, sc.shape, 0) + row0
    col = lax.broadcasted_iota(jnp.int32, sc.shape, 1) + col0
    return jnp.where(col <= row, sc, _NEG)


def attn_fwd3(qn, qp, kv, kpr, cos4, sin4):
    s = qn.shape[0]
    hp = HEADS // 2
    t = _tile(s, ATT_TILE)
    tk = 2 * t
    nq = s // t
    assert s % tk == 0

    def body(qn_ref, qp_ref, kv_ref, kp_ref, c_ref, s_ref, o_ref, qpr_ref, l_ref, kcat_ref):
        qi = pl.program_id(1)

        @pl.when(qi == 0)
        def _():
            for hh in range(2):
                kcat_ref[hh, :, 0:QK_NOPE] = kv_ref[:, 2 * hh * QK_NOPE:(2 * hh + 1) * QK_NOPE]
                kcat_ref[hh, :, QK_NOPE:] = kp_ref[...]

        qpr = _rope(qp_ref[...], c_ref[...], s_ref[...]).astype(BF16)
        qpr_ref[...] = qpr
        qcat = [_q_cat(qn_ref[:, hh * QK_NOPE:(hh + 1) * QK_NOPE], qpr, hh) for hh in range(2)]

        def block(kb, carry, diagonal):
            start = pl.multiple_of(kb * tk, tk)
            rows = pl.ds(start, tk)
            out = []
            for hh in range(2):
                m, l, acc = carry[hh]
                sc = _dot_nt(qcat[hh], kcat_ref[hh, rows, :]) * _ATT_SCALE
                if diagonal:
                    sc = _causal_at(sc, qi * t, start)
                m_new = jnp.maximum(m, jnp.max(sc, axis=-1, keepdims=True))
                alpha = jnp.exp(m - m_new)
                p = jnp.exp(sc - m_new)
                l = alpha * l + jnp.sum(p, axis=-1, keepdims=True)
                v = kv_ref[rows, (2 * hh + 1) * QK_NOPE:(2 * hh + 2) * QK_NOPE]
                acc = alpha * acc + jnp.dot(p.astype(BF16), v, preferred_element_type=F32)
                out.append((m_new, l, acc))
            return tuple(out)

        one = (jnp.full((t, 1), _NEG, F32), jnp.zeros((t, 1), F32), jnp.zeros((t, V_HEAD), F32))
        carry = lax.fori_loop(0, qi // 2, lambda kb, cr: block(kb, cr, False), (one, one))
        carry = block(qi // 2, carry, True)
        for hh in range(2):
            m, l, acc = carry[hh]
            o_ref[:, hh * V_HEAD:(hh + 1) * V_HEAD] = acc / l
            l_ref[:, hh:hh + 1] = m + jnp.log(l)

    return pl.pallas_call(
        body, name="attn_fwd", grid=(hp, nq),
        in_specs=[pl.BlockSpec((t, 2 * QK_NOPE), lambda h, i: (i, h)), pl.BlockSpec((t, LANES), lambda h, i: (i, h)),
                  pl.BlockSpec((s, 4 * QK_NOPE), lambda h, i: (0, h)), _full((s, LANES)),
                  pl.BlockSpec((t, LANES), lambda h, i: (i, 0)), pl.BlockSpec((t, LANES), lambda h, i: (i, 0))],
        out_specs=[pl.BlockSpec((t, 2 * V_HEAD), lambda h, i: (i, h)), pl.BlockSpec((t, LANES), lambda h, i: (i, h)),
                   pl.BlockSpec((None, t, 2), lambda h, i: (h, i, 0))],
        out_shape=[jax.ShapeDtypeStruct((s, HEADS * V_HEAD), F32), jax.ShapeDtypeStruct((s, HEADS * QK_ROPE), BF16),
                   jax.ShapeDtypeStruct((hp, s, 2), F32)],
        scratch_shapes=[pltpu.VMEM((2, s, 2 * QK_NOPE), BF16)],
        compiler_params=_params("parallel", "arbitrary"))(qn, qp, kv, kpr, cos4, sin4)


def attn_bwd3(qn, qpr, kv, kpr, o, do, lse, cos4, sin4):
    s = qn.shape[0]
    hp = HEADS // 2
    t = _tile(s, ATT_TILE)
    tq = 2 * t
    nk = s // t
    nq2 = s // tq
    assert s % tq == 0

    def body(qn_ref, qpr_ref, kv_ref, kp_ref, o_ref, do_ref, l_ref, c_ref, s_ref,
             dqn_ref, dqp_ref, dkv_ref, dkp_ref, qcat_ref, dq_ref, delta_ref):
        ki = pl.program_id(1)

        @pl.when(ki == 0)
        def _():
            dq_ref[...] = jnp.zeros_like(dq_ref)
            for hh in range(2):
                qcat_ref[hh] = _q_cat(qn_ref[:, hh * QK_NOPE:(hh + 1) * QK_NOPE], qpr_ref[...], hh)
                cols = slice(hh * V_HEAD, (hh + 1) * V_HEAD)
                delta_ref[hh] = jnp.sum(do_ref[:, cols] * o_ref[:, cols], axis=-1, keepdims=True)

        rows_k = pl.ds(pl.multiple_of(ki * t, t), t)
        kcat = [jnp.concatenate([kv_ref[rows_k, 2 * hh * QK_NOPE:(2 * hh + 1) * QK_NOPE], kp_ref[rows_k, :]], axis=1) for hh in range(2)]
        vs = [kv_ref[rows_k, (2 * hh + 1) * QK_NOPE:(2 * hh + 2) * QK_NOPE] for hh in range(2)]

        def block(qb, carry, diagonal):
            start = pl.multiple_of(qb * tq, tq)
            rows = pl.ds(start, tq)
            out = []
            for hh in range(2):
                dkc, dv = carry[hh]
                q_c = qcat_ref[hh, rows, :]
                do_b = do_ref[rows, hh * V_HEAD:(hh + 1) * V_HEAD].astype(BF16)
                sc = _dot_nt(q_c, kcat[hh]) * _ATT_SCALE
                if diagonal:
                    sc = _causal_at(sc, start, ki * t)
                p = jnp.exp(sc - l_ref[rows, hh:hh + 1])
                dpv = _dot_nt(do_b, vs[hh])
                ds = (p * (dpv - delta_ref[hh, rows, :]) * _ATT_SCALE).astype(BF16)
                dv = dv + _dot_tn(p.astype(BF16), do_b)
                dkc = dkc + _dot_tn(ds, q_c)
                dq_ref[hh, rows, :] += jnp.dot(ds, kcat[hh], preferred_element_type=F32)
                out.append((dkc, dv))
            return tuple(out)

        one = (jnp.zeros((t, 2 * QK_NOPE), F32), jnp.zeros((t, V_HEAD), F32))
        carry = block(ki // 2, (one, one), True)
        carry = lax.fori_loop(ki // 2 + 1, nq2, lambda qb, cr: block(qb, cr, False), carry)
        dkp = jnp.zeros((t, LANES), F32)
        for hh in range(2):
            dkc, dv = carry[hh]
            dkv_ref[:, 2 * hh * QK_NOPE:(2 * hh + 1) * QK_NOPE] = dkc[:, :QK_NOPE].astype(dkv_ref.dtype)
            dkv_ref[:, (2 * hh + 1) * QK_NOPE:(2 * hh + 2) * QK_NOPE] = dv.astype(dkv_ref.dtype)
            dkp = dkp + dkc[:, QK_NOPE:]
        dkp_ref[...] = dkp

        @pl.when(ki == nk - 1)
        def _():
            lane = lax.broadcasted_iota(jnp.int32, (s, LANES), 1)
            dqp = jnp.where(lane < QK_ROPE, dq_ref[0, :, QK_NOPE:], dq_ref[1, :, QK_NOPE:])
            dqp_ref[...] = _rope(dqp, c_ref[...], -s_ref[...]).astype(dqp_ref.dtype)
            for hh in range(2):
                dqn_ref[:, hh * QK_NOPE:(hh + 1) * QK_NOPE] = dq_ref[hh, :, :QK_NOPE].astype(dqn_ref.dtype)

    qblk = pl.BlockSpec((s, 2 * QK_NOPE), lambda h, i: (0, h))
    pblk = pl.BlockSpec((s, LANES), lambda h, i: (0, h))
    tab = _full((s, LANES))
    return pl.pallas_call(
        body, name="attn_bwd", grid=(hp, nk),
        in_specs=[qblk, pblk, pl.BlockSpec((s, 4 * QK_NOPE), lambda h, i: (0, h)), tab, qblk, qblk,
                  pl.BlockSpec((None, s, 2), lambda h, i: (h, 0, 0)), tab, tab],
        out_specs=[qblk, pblk, pl.BlockSpec((t, 4 * QK_NOPE), lambda h, i: (i, h)), pl.BlockSpec((None, t, LANES), lambda h, i: (h, i, 0))],
        out_shape=[jax.ShapeDtypeStruct((s, HEADS * QK_NOPE), BF16), jax.ShapeDtypeStruct((s, HEADS * QK_ROPE), BF16),
                   jax.ShapeDtypeStruct((s, HEADS * 2 * QK_NOPE), BF16), jax.ShapeDtypeStruct((hp, s, LANES), F32)],
        scratch_shapes=[pltpu.VMEM((2, s, 2 * QK_NOPE), BF16), pltpu.VMEM((2, s, 2 * QK_NOPE), F32), pltpu.VMEM((2, s, 1), F32)],
        compiler_params=_params("parallel", "arbitrary"))(qn, qpr, kv, kpr, o, do, lse, cos4, sin4)


def kpe_bwd(dkp, cos4, sin4, pad_cols):
    hp, s, _ = dkp.shape
    tr = _tile(s, ROW_TILE * 2)

    def body(d_ref, c_ref, s_ref, o_ref):
        tot = d_ref[0]
        for h in range(1, hp):
            tot = tot + d_ref[h]
        tot = tot + pltpu.roll(tot, QK_ROPE, 1)
        lane = lax.broadcasted_iota(jnp.int32, tot.shape, 1)
        dk = jnp.where(lane < QK_ROPE, _rope(tot, c_ref[...], -s_ref[...]), jnp.zeros_like(tot))
        o_ref[...] = jnp.zeros_like(o_ref)
        o_ref[:, 0:LANES] = dk.astype(o_ref.dtype)

    row = pl.BlockSpec((tr, LANES), lambda i: (i, 0))
    return pl.pallas_call(body, name="kpe_bwd", grid=(s // tr,),
                          in_specs=[pl.BlockSpec((hp, tr, LANES), lambda i: (0, i, 0)), row, row],
                          out_specs=pl.BlockSpec((tr, pad_cols), lambda i: (i, 0)),
                          out_shape=jax.ShapeDtypeStruct((s, pad_cols), BF16), compiler_params=_params("parallel"))(dkp, cos4, sin4)


def _shift_down(x, n):
    row = lax.broadcasted_iota(jnp.int32, x.shape, 0)
    return jnp.where(row >= n, pltpu.roll(x, n, 0), jnp.zeros_like(x))


def _shift_up(x, n):
    rows = x.shape[0]
    row = lax.broadcasted_iota(jnp.int32, x.shape, 0)
    return jnp.where(row < rows - n, pltpu.roll(x, rows - n, 0), jnp.zeros_like(x))


def _conv(x, w_ref, b_ref):
    return w_ref[2:3, :] * x + w_ref[1:2, :] * _shift_down(x, 1) + w_ref[0:1, :] * _shift_down(x, 2) + b_ref[...]


def conv_act_fwd(upre, conv_w, conv_b):
    s, f2 = upre.shape
    f = f2 // 2
    tc = _tile(f, COL_TILE)
    nc = f // tc

    def body(ug_ref, uv_ref, wg_ref, wv_ref, bg_ref, bv_ref, o_ref, gv_ref):
        gh = _conv(_f32(ug_ref), wg_ref, bg_ref)
        vh = _conv(_f32(uv_ref), wv_ref, bv_ref)
        o_ref[...] = (gh * _sigmoid(gh) * vh).astype(o_ref.dtype)
        gv_ref[0] = gh.astype(gv_ref.dtype)
        gv_ref[1] = vh.astype(gv_ref.dtype)

    def spec(rows, shift):
        return pl.BlockSpec((rows, tc), lambda j: (0, j + shift))

    return pl.pallas_call(
        body, name="conv_act_fwd", grid=(nc,),
        in_specs=[spec(s, 0), spec(s, nc), spec(3, 0), spec(3, nc), spec(1, 0), spec(1, nc)],
        out_specs=[spec(s, 0), pl.BlockSpec((2, s, tc), lambda j: (0, 0, j))],
        out_shape=[jax.ShapeDtypeStruct((s, f), BF16), jax.ShapeDtypeStruct((2, s, f), ACT)],
        compiler_params=_params("parallel"))(upre, upre, conv_w, conv_w, conv_b, conv_b)


def conv_act_bwd(upre, conv_w, gv, df):
    s, f2 = upre.shape
    f = f2 // 2
    tc = _tile(f, COL_TILE)
    nc = f // tc

    def half(x, d, w_ref, du_ref, which, gw_ref, gb_ref):
        d1, d2 = _shift_up(d, 1), _shift_up(d, 2)
        gb_ref[...] = _colsum(d)
        gw_ref[2:3, :] = _colsum(d * x)
        gw_ref[1:2, :] = _colsum(d1 * x)
        gw_ref[0:1, :] = _colsum(d2 * x)
        du_ref[which] = (w_ref[2:3, :] * d + w_ref[1:2, :] * d1 + w_ref[0:1, :] * d2).astype(du_ref.dtype)

    def body(ug_ref, uv_ref, wg_ref, wv_ref, gv_ref, df_ref, du_ref, gwg_ref, gwv_ref, gbg_ref, gbv_ref):
        xg, xv = _f32(ug_ref), _f32(uv_ref)
        gh, vh = gv_ref[0].astype(F32), gv_ref[1].astype(F32)
        sg = _sigmoid(gh)
        df_v = _f32(df_ref)
        half(xg, df_v * vh * (sg * (1.0 + gh * (1.0 - sg))), wg_ref, du_ref, 0, gwg_ref, gbg_ref)
        half(xv, df_v * (gh * sg), wv_ref, du_ref, 1, gwv_ref, gbv_ref)

    def spec(rows, shift):
        return pl.BlockSpec((rows, tc), lambda j: (0, j + shift))

    gw = jax.ShapeDtypeStruct((3, f), F32)
    gb = jax.ShapeDtypeStruct((1, f), F32)
    return pl.pallas_call(
        body, name="conv_act_bwd", grid=(nc,),
        in_specs=[spec(s, 0), spec(s, nc), spec(3, 0), spec(3, nc), pl.BlockSpec((2, s, tc), lambda j: (0, 0, j)), spec(s, 0)],
        out_specs=[pl.BlockSpec((2, s, tc), lambda j: (0, 0, j)), spec(3, 0), spec(3, 0), spec(1, 0), spec(1, 0)],
        out_shape=[jax.ShapeDtypeStruct((2, s, f), BF16), gw, gw, gb, gb],
        compiler_params=_params("parallel"))(upre, upre, conv_w, conv_w, gv, df)


def _elementwise_tile(r, c, limit):
    if r % 8:
        return r, c
    best = (8, c if c % LANES else LANES)
    for k in (1, 2, 4, 8, 16):
        if k > 1 and c % (LANES * k):
            continue
        tc = c // k
        tr = max(8, min(r, limit // tc) // 8 * 8)
        while r % tr:
            tr -= 8
        if tr * tc <= max(limit, 8 * tc) and tr * tc > best[0] * best[1]:
            best = (tr, tc)
    return best


def adamw(name, w, m, v, parts):
    npart, r, c = parts.shape
    tr, tc = _elementwise_tile(r, c, ADAMW_TILE_ELEMS)
    bc1 = 1.0 - ADAM_B1 ** ADAM_STEP
    bc2 = 1.0 - ADAM_B2 ** ADAM_STEP

    def body(w_ref, m_ref, v_ref, p_ref, g_ref, d_ref, nm_ref, nv_ref):
        g = p_ref[0].astype(F32)
        for k in range(1, npart):
            g = g + p_ref[k].astype(F32)
        m_new = ADAM_B1 * m_ref[...] + (1.0 - ADAM_B1) * g
        v_new = ADAM_B2 * v_ref[...] + (1.0 - ADAM_B2) * (g * g)
        g_ref[...] = g
        nm_ref[...] = m_new
        nv_ref[...] = v_new
        d_ref[...] = -ADAM_LR * ((m_new / bc1) / (jnp.sqrt(v_new / bc2) + ADAM_EPS) + ADAM_WD * w_ref[...])

    deps = _TOKENS.take()
    blk = pl.BlockSpec((tr, tc), lambda i, j: (i, j))
    out = jax.ShapeDtypeStruct((r, c), F32)
    return pl.pallas_call(
        lambda *refs: body(*refs[:4], *refs[4 + len(deps):]), name=name, grid=(r // tr, c // tc),
        in_specs=[blk, blk, blk, pl.BlockSpec((npart, tr, tc), lambda i, j: (0, i, j))] + [pl.BlockSpec(memory_space=pl.ANY)] * len(deps),
        out_specs=[blk, blk, blk, blk], out_shape=[out, out, out, out],
        compiler_params=_params("parallel", "parallel"))(w, m, v, parts, *deps)


def _position():
    return lax.axis_index("x"), lax.axis_index("y"), lax.axis_index("c")


def _index(p):
    return 4 * p[0] + 2 * p[1] + p[2]


def _peer(me, r):
    return (me[0] ^ ((r >> 2) & 1), me[1] ^ ((r >> 1) & 1), me[2] ^ (r & 1))


_ANY = pl.BlockSpec(memory_space=pl.ANY)


def all_gather_two_level(shards):
    n = len(shards)

    def body(*refs):
        ins, outs = refs[:n], refs[n:2 * n]
        send_sems, recv_sems, local_sems = refs[2 * n:]
        x, y, c = _position()
        me, sibling = (x, y, c), (x, y, 1 - c)
        chips = [(1 - x, y), (x, 1 - y), (1 - x, 1 - y)]

        def copy(w, k, block, to, src=None):
            slot = outs[w].at[_index(block)]
            return pltpu.make_async_remote_copy(src_ref=slot if src is None else src, dst_ref=slot,
                                                send_sem=send_sems.at[7 * w + k], recv_sem=recv_sems.at[7 * w + k],
                                                device_id=to, device_id_type=MESH)

        mine = [pltpu.make_async_copy(ins[w], outs[w].at[_index(me)], local_sems.at[w]) for w in range(n)]
        for cp in mine:
            cp.start()
        first = []
        for w in range(n):
            first.append(copy(w, 0, me, sibling, src=ins[w]))
            first += [copy(w, 1 + j, me, (*chip, c), src=ins[w]) for j, chip in enumerate(chips)]
        for cp in first:
            cp.start()
        passed = []
        for w in range(n):
            for j, chip in enumerate(chips):
                copy(w, 1 + j, (*chip, c), me).wait_recv()
                cp = copy(w, 4 + j, (*chip, c), sibling)
                cp.start()
                passed.append(cp)
        for w in range(n):
            copy(w, 0, sibling, me).wait_recv()
            for j, chip in enumerate(chips):
                copy(w, 4 + j, (*chip, 1 - c), me).wait_recv()
        for cp in first + passed:
            cp.wait_send()
        for cp in mine:
            cp.wait()

    return pl.pallas_call(
        body, name="all_gather_weights",
        out_shape=[jax.ShapeDtypeStruct((N_DEV,) + a.shape, a.dtype) for a in shards],
        in_specs=[_ANY] * n, out_specs=[_ANY] * n,
        scratch_shapes=[pltpu.SemaphoreType.DMA((7 * n,)), pltpu.SemaphoreType.DMA((7 * n,)), pltpu.SemaphoreType.DMA((n,))],
        )(*shards)


def exchange(name, arrays, scatter):
    n = len(arrays)

    def body(*refs):
        ins, outs = refs[:n], refs[n:2 * n]
        send_sems, recv_sems, local_sems = refs[2 * n:]
        me = _position()
        copies = []
        for w in range(n):
            src = ins[w].at[_index(me)] if scatter else ins[w]
            cp = pltpu.make_async_copy(src, outs[w].at[_index(me)], local_sems.at[w])
            cp.start()
            copies.append(cp)
        remote = []
        for w in range(n):
            for r in range(1, N_DEV):
                peer = _peer(me, r)
                src = ins[w].at[_index(peer)] if scatter else ins[w]
                cp = pltpu.make_async_remote_copy(src_ref=src, dst_ref=outs[w].at[_index(me)],
                                                  send_sem=send_sems.at[7 * w + r - 1], recv_sem=recv_sems.at[7 * w + r - 1],
                                                  device_id=peer, device_id_type=MESH)
                cp.start()
                remote.append(cp)
        for cp in remote:
            cp.wait()
        for cp in copies:
            cp.wait()

    blocks = [a.shape[1:] if scatter else a.shape for a in arrays]
    return pl.pallas_call(
        body, name=name,
        out_shape=[jax.ShapeDtypeStruct((N_DEV,) + b, a.dtype) for a, b in zip(arrays, blocks)],
        in_specs=[_ANY] * n, out_specs=[_ANY] * n,
        scratch_shapes=[pltpu.SemaphoreType.DMA((7 * n,)), pltpu.SemaphoreType.DMA((7 * n,)), pltpu.SemaphoreType.DMA((n,))],
        )(*arrays)


_HBM = pl.BlockSpec(memory_space=pltpu.HBM)
_SEM = pl.BlockSpec(memory_space=pltpu.SEMAPHORE)
_EFFECT = pltpu.SideEffectType.DATAFLOW_SIDE_EFFECTING


def _direct_copies(ins, lands, send_sems, recv_sems, scatter):
    me = _position()
    copies = []
    for w in range(len(ins)):
        for r in range(1, N_DEV):
            peer = _peer(me, r)
            src = ins[w].at[_index(peer)] if scatter else ins[w]
            copies.append(pltpu.make_async_remote_copy(src_ref=src, dst_ref=lands[w].at[_index(me)], send_sem=send_sems.at[7 * w + r - 1],
                                                       recv_sem=recv_sems.at[7 * w + r - 1], device_id=peer, device_id_type=MESH))
    return copies


def exchange_start(name, groups, scatter):
    arrays = [a for g in groups for a in g]
    n = len(arrays)
    blocks = [a.shape[1:] if scatter else a.shape for a in arrays]
    lands = [lax.empty((N_DEV,) + b, a.dtype) for a, b in zip(arrays, blocks)]
    ng = len(groups)

    def body(*refs):
        ins, lnd = refs[:n], refs[n:2 * n]
        sems = refs[2 * n:2 * n + 2 * ng]
        token = refs[2 * n + 2 * ng + 2 * n]
        local_sem = refs[2 * n + 2 * ng + 2 * n + 1]
        me = _position()
        local = []
        for w in range(n):
            src = ins[w].at[_index(me)] if scatter else ins[w]
            cp = pltpu.make_async_copy(src, lnd[w].at[_index(me)], local_sem.at[w])
            cp.start()
            local.append(cp)
        w0 = 0
        for gi, g in enumerate(groups):
            for cp in _direct_copies(ins[w0:w0 + len(g)], lnd[w0:w0 + len(g)], sems[2 * gi], sems[2 * gi + 1], scatter):
                cp.start()
            w0 += len(g)
        for cp in local:
            cp.wait()
        token[...] = jnp.zeros_like(token)

    sem_shapes = []
    for g in groups:
        sem_shapes += [pltpu.SemaphoreType.DMA((7 * len(g),)), pltpu.SemaphoreType.DMA((7 * len(g),))]
    out = pl.pallas_call(
        body, name=name,
        out_shape=tuple(sem_shapes) + tuple(pltpu.HBM(a.shape, a.dtype) for a in arrays) + tuple(pltpu.HBM(l.shape, l.dtype) for l in lands)
        + (jax.ShapeDtypeStruct((8, LANES), F32),),
        in_specs=[_HBM] * (2 * n), out_specs=tuple([_SEM] * (2 * ng) + [_HBM] * (2 * n) + [pl.BlockSpec(memory_space=pltpu.VMEM)]),
        input_output_aliases={i: 2 * ng + i for i in range(2 * n)},
        scratch_shapes=[pltpu.SemaphoreType.DMA((n,))],
        compiler_params=pltpu.CompilerParams(has_side_effects=_EFFECT),
    )(*[pltpu.with_memory_space_constraint(a, pltpu.HBM) for a in arrays], *[pltpu.with_memory_space_constraint(l, pltpu.HBM) for l in lands])
    sems, thru, token = out[:2 * ng], out[2 * ng:2 * ng + 2 * n], out[-1]
    res, w0 = [], 0
    for gi, g in enumerate(groups):
        res.append((sems[2 * gi], sems[2 * gi + 1], list(thru[w0:w0 + len(g)]), list(thru[n + w0:n + w0 + len(g)])))
        w0 += len(g)
    return res, token


def exchange_wait(name, group, after, scatter):
    send_sems, recv_sems, srcs, lands = group
    n = len(srcs)

    def body(*refs):
        ins, lnd = refs[:n], refs[n:2 * n]
        for cp in _direct_copies(ins, lnd, refs[2 * n], refs[2 * n + 1], scatter):
            cp.wait_send()
            cp.wait_recv()

    out = pl.pallas_call(
        body, name=name, out_shape=tuple(pltpu.HBM(a.shape, a.dtype) for a in srcs + lands),
        in_specs=[_HBM] * (2 * n) + [_SEM, _SEM, pl.BlockSpec(memory_space=pl.ANY)], out_specs=tuple([_HBM] * (2 * n)),
        input_output_aliases={i: i for i in range(2 * n)},
        compiler_params=pltpu.CompilerParams(has_side_effects=_EFFECT),
    )(*srcs, *lands, send_sems, recv_sems, after)
    return list(out[n:])


def _after(x, token):
    return lax.optimization_barrier((x, token))[0]


_TOKEN = jax.ShapeDtypeStruct((8, LANES), F32)
_VM = pl.BlockSpec(memory_space=pltpu.VMEM)
_SIDE = pltpu.CompilerParams(has_side_effects=_EFFECT)


def _hbm(a):
    return pltpu.with_memory_space_constraint(a, pltpu.HBM)


def _like(a):
    return pltpu.HBM(a.shape, a.dtype)


def _dma_sems(n):
    return pltpu.SemaphoreType.DMA((n,))


def _other_chips(x, y):
    return [(1 - x, y), (x, 1 - y), (1 - x, 1 - y)]


COPY_STREAMS = 8


def _row_chunks(src, dst):
    rows = src.shape[0]
    n = COPY_STREAMS
    while n > 1 and rows % (16 * n):
        n //= 2
    r = rows // n
    return [(src.at[pl.ds(i * r, r)], dst.at[pl.ds(i * r, r)]) for i in range(n)]


def _local_copy(src, dst, sem):
    return [pltpu.make_async_copy(s, d, sem) for s, d in _row_chunks(src, dst)]


class _rcopy:
    def __init__(self, src, dst, send_sem, recv_sem, to):
        self.parts = [pltpu.make_async_remote_copy(src_ref=s, dst_ref=d, send_sem=send_sem, recv_sem=recv_sem, device_id=to, device_id_type=MESH)
                      for s, d in _row_chunks(src, dst)]

    def start(self):
        for cp in self.parts:
            cp.start()

    def wait_send(self):
        for cp in self.parts:
            cp.wait_send()

    def wait_recv(self):
        for cp in self.parts:
            cp.wait_recv()


def _afters(after):
    return list(after) if isinstance(after, (list, tuple)) else [after]


def ag_start(name, shards, after):
    n = len(shards)
    lands = [lax.empty((N_DEV,) + a.shape, a.dtype) for a in shards]
    afters = _afters(after)
    na = len(afters)

    def body(*refs):
        ins, lnd, send_sems, recv_sems, token = refs[:n], refs[n:2 * n], refs[2 * n + na], refs[2 * n + na + 1], refs[4 * n + na + 2]
        x, y, c = _position()
        for w in range(n):
            slot = lnd[w].at[_index((x, y, c))]
            for k, to in enumerate([(x, y, 1 - c)] + [(*chip, c) for chip in _other_chips(x, y)]):
                _rcopy(ins[w], slot, send_sems.at[4 * w + k], recv_sems.at[4 * w + k], to).start()
        token[...] = jnp.zeros_like(token)

    out = pl.pallas_call(
        body, name=name, out_shape=(_dma_sems(4 * n), _dma_sems(4 * n)) + tuple(_like(a) for a in shards + lands) + (_TOKEN,),
        in_specs=[_HBM] * (2 * n) + [_ANY] * na, out_specs=(_SEM, _SEM) + (_HBM,) * (2 * n) + (_VM,),
        input_output_aliases={i: 2 + i for i in range(2 * n)}, compiler_params=_SIDE)(*[_hbm(a) for a in shards + lands], *afters)
    _TOKENS.push(out[-1])
    return out[0], out[1], list(out[2:2 + n]), list(out[2 + n:2 + 2 * n])


def ag_forward(name, started, after):
    send, recv, shards, lands = started
    n = len(shards)
    afters = list(after) if isinstance(after, (list, tuple)) else [after]
    na = len(afters)

    def body(*refs):
        ins, lnd, send_sems, recv_sems = refs[:n], refs[n:2 * n], refs[2 * n], refs[2 * n + 1]
        fsend, frecv, token = refs[2 * n + 2 + na], refs[2 * n + 3 + na], refs[4 * n + 4 + na]
        x, y, c = _position()
        for w in range(n):
            for j, chip in enumerate(_other_chips(x, y)):
                slot = lnd[w].at[_index((*chip, c))]
                _rcopy(ins[w], slot, send_sems.at[4 * w + 1 + j], recv_sems.at[4 * w + 1 + j], (*chip, c)).wait_recv()
                _rcopy(slot, slot, fsend.at[3 * w + j], frecv.at[3 * w + j], (x, y, 1 - c)).start()
        token[...] = jnp.zeros_like(token)

    out = pl.pallas_call(
        body, name=name, out_shape=(_dma_sems(3 * n), _dma_sems(3 * n)) + tuple(_like(a) for a in shards + lands) + (_TOKEN,),
        in_specs=[_HBM] * (2 * n) + [_SEM, _SEM] + [_ANY] * na, out_specs=(_SEM, _SEM) + (_HBM,) * (2 * n) + (_VM,),
        input_output_aliases={i: 2 + i for i in range(2 * n)}, compiler_params=_SIDE)(*shards, *lands, send, recv, *afters)
    _TOKENS.push(out[-1])
    return send, recv, out[0], out[1], list(out[2:2 + n]), list(out[2 + n:2 + 2 * n])


def ag_wait(name, forwarded, after):
    send, recv, fsend, frecv, shards, lands = forwarded
    n = len(shards)

    def body(*refs):
        ins, lnd, send_sems, recv_sems, fsend_r, frecv_r = refs[:n], refs[n:2 * n], refs[2 * n], refs[2 * n + 1], refs[2 * n + 2], refs[2 * n + 3]
        x, y, c = _position()
        sibling = (x, y, 1 - c)
        for w in range(n):
            own = lnd[w].at[_index((x, y, c))]
            _rcopy(ins[w], lnd[w].at[_index(sibling)], send_sems.at[4 * w], recv_sems.at[4 * w], sibling).wait_recv()
            for j, chip in enumerate(_other_chips(x, y)):
                _rcopy(ins[w], lnd[w].at[_index((*chip, 1 - c))], fsend_r.at[3 * w + j], frecv_r.at[3 * w + j], sibling).wait_recv()
            for k in range(4):
                _rcopy(ins[w], own, send_sems.at[4 * w + k], recv_sems.at[4 * w + k], sibling).wait_send()
            for j in range(3):
                _rcopy(ins[w], own, fsend_r.at[3 * w + j], frecv_r.at[3 * w + j], sibling).wait_send()

    out = pl.pallas_call(
        body, name=name, out_shape=tuple(_like(a) for a in shards + lands),
        in_specs=[_HBM] * (2 * n) + [_SEM] * 4 + [_ANY] * len(_afters(after)),
        out_specs=(_HBM,) * (2 * n), input_output_aliases={i: i for i in range(2 * n)},
        compiler_params=_SIDE)(*shards, *lands, send, recv, fsend, frecv, *_afters(after))
    return [lax.dynamic_update_index_in_dim(land, shard, _index(_position()), 0) for shard, land in zip(out[:n], out[n:])]


def rs_d2d_start(name, grads):
    n = len(grads)
    lands = [lax.empty((4,) + g.shape[1:], g.dtype) for g in grads]

    def body(*refs):
        ins, lnd, send_sems, recv_sems, token = refs[:n], refs[n:2 * n], refs[2 * n], refs[2 * n + 1], refs[4 * n + 2]
        x, y, c = _position()
        for w in range(n):
            for i in range(4):
                _rcopy(ins[w].at[2 * i + 1 - c], lnd[w].at[i], send_sems.at[4 * w + i], recv_sems.at[4 * w + i], (x, y, 1 - c)).start()
        token[...] = jnp.zeros_like(token)

    out = pl.pallas_call(
        body, name=name, out_shape=(_dma_sems(4 * n), _dma_sems(4 * n)) + tuple(_like(a) for a in grads + lands) + (_TOKEN,),
        in_specs=[_HBM] * (2 * n), out_specs=(_SEM, _SEM) + (_HBM,) * (2 * n) + (_VM,),
        input_output_aliases={i: 2 + i for i in range(2 * n)}, compiler_params=_SIDE)(*[_hbm(a) for a in grads + lands])
    _TOKENS.push(out[-1])
    return out[0], out[1], list(out[2:2 + n]), list(out[2 + n:2 + 2 * n])


def rs_d2d_wait(name, started, after):
    send, recv, grads, lands = started
    n = len(grads)

    def body(*refs):
        ins, lnd, send_sems, recv_sems = refs[:n], refs[n:2 * n], refs[2 * n], refs[2 * n + 1]
        x, y, c = _position()
        for w in range(n):
            for i in range(4):
                cp = _rcopy(ins[w].at[2 * i + 1 - c], lnd[w].at[i], send_sems.at[4 * w + i], recv_sems.at[4 * w + i], (x, y, 1 - c))
                cp.wait_send()
                cp.wait_recv()

    out = pl.pallas_call(
        body, name=name, out_shape=tuple(_like(a) for a in grads + lands),
        in_specs=[_HBM] * (2 * n) + [_SEM, _SEM] + [_ANY] * len(_afters(after)),
        out_specs=(_HBM,) * (2 * n), input_output_aliases={i: i for i in range(2 * n)},
        compiler_params=_SIDE)(*grads, *lands, send, recv, *_afters(after))
    return list(out[:n]), list(out[n:])


def pair_sum(name, grad, land, core):
    _, r, c = grad.shape
    tr = r
    if r % 8 == 0:
        tr = max(8, min(r, 4 * ADAMW_TILE_ELEMS // c) // 8 * 8)
        while r % tr:
            tr -= 8

    def body(core_ref, a_ref, b_ref, o_ref):
        o_ref[...] = (a_ref[...].astype(F32) + b_ref[...].astype(F32)).astype(o_ref.dtype)

    return pl.pallas_call(
        body, name=name, out_shape=jax.ShapeDtypeStruct((4, r, c), grad.dtype),
        grid_spec=pltpu.PrefetchScalarGridSpec(
            num_scalar_prefetch=1, grid=(4, r // tr),
            in_specs=[pl.BlockSpec((None, None, tr, c), lambda i, j, core_ref: (i, core_ref[0], j, 0)),
                      pl.BlockSpec((None, tr, c), lambda i, j, core_ref: (i, j, 0))],
            out_specs=pl.BlockSpec((None, tr, c), lambda i, j, core_ref: (i, j, 0))),
        compiler_params=_params("parallel", "parallel"))(core, grad.reshape(4, 2, r, c), land)


def rs_ici_start(name, sums):
    n = len(sums)
    lands = [lax.empty(a.shape, a.dtype) for a in sums]

    def body(*refs):
        ins, lnd, send_sems, recv_sems, token = refs[:n], refs[n:2 * n], refs[2 * n], refs[2 * n + 1], refs[4 * n + 2]
        x, y, c = _position()
        chip = 2 * x + y
        for w in range(n):
            for j, other in enumerate(_other_chips(x, y)):
                _rcopy(ins[w].at[2 * other[0] + other[1]], lnd[w].at[chip], send_sems.at[3 * w + j], recv_sems.at[3 * w + j], (*other, c)).start()
        token[...] = jnp.zeros_like(token)

    out = pl.pallas_call(
        body, name=name, out_shape=(_dma_sems(3 * n), _dma_sems(3 * n)) + tuple(_like(a) for a in sums + lands) + (_TOKEN,),
        in_specs=[_HBM] * (2 * n), out_specs=(_SEM, _SEM) + (_HBM,) * (2 * n) + (_VM,),
        input_output_aliases={i: 2 + i for i in range(2 * n)}, compiler_params=_SIDE)(*[_hbm(a) for a in sums + lands])
    _TOKENS.push(out[-1])
    return out[0], out[1], list(out[2:2 + n]), list(out[2 + n:2 + 2 * n])


def rs_ici_wait(name, started, after):
    send, recv, sums, lands = started
    n = len(sums)

    def body(*refs):
        ins, lnd, send_sems, recv_sems = refs[:n], refs[n:2 * n], refs[2 * n], refs[2 * n + 1]
        x, y, c = _position()
        for w in range(n):
            for j, other in enumerate(_other_chips(x, y)):
                cp = _rcopy(ins[w].at[2 * other[0] + other[1]], lnd[w].at[2 * other[0] + other[1]], send_sems.at[3 * w + j], recv_sems.at[3 * w + j], (*other, c))
                cp.wait_send()
                cp.wait_recv()

    out = pl.pallas_call(
        body, name=name, out_shape=tuple(_like(a) for a in sums + lands), in_specs=[_HBM] * (2 * n) + [_SEM, _SEM, _ANY],
        out_specs=(_HBM,) * (2 * n), input_output_aliases={i: i for i in range(2 * n)}, compiler_params=_SIDE)(*sums, *lands, send, recv, after)
    chip = 2 * lax.axis_index("x") + lax.axis_index("y")
    return [lax.dynamic_update_index_in_dim(land, lax.dynamic_index_in_dim(s, chip, 0, keepdims=False), chip, 0)
            for s, land in zip(out[:n], out[n:])]


def ada_fwd(c, w_ada, b_ada3, conv_w):
    d, cs = w_ada.shape

    def body(c_ref, w_ref, b_ref, cw_ref, mod_ref, sc_ref, cwa_ref, part_ref, send_sems, recv_sems):
        me = _position()
        my = _index(me)
        cv = c_ref[...]
        sc_ref[my] = cv * _sigmoid(cv)
        cwa_ref[my] = cw_ref[...]
        gather = []
        for r in range(1, N_DEV):
            for k, ref in enumerate((sc_ref, cwa_ref)):
                cp = pltpu.make_async_remote_copy(src_ref=ref.at[my], dst_ref=ref.at[my], send_sem=send_sems.at[14 * k + r - 1],
                                                  recv_sem=recv_sems.at[14 * k + r - 1], device_id=_peer(me, r), device_id_type=MESH)
                cp.start()
                gather.append(cp)
        for cp in gather:
            cp.wait()
        sc_all = jnp.concatenate([sc_ref[k] for k in range(N_DEV)], axis=0).astype(BF16)
        part = jnp.dot(sc_all, w_ref[...].astype(BF16), preferred_element_type=F32)
        for k in range(N_DEV):
            part_ref[k] = part[k:k + 1, :]
        scatter = []
        for r in range(1, N_DEV):
            peer = _peer(me, r)
            cp = pltpu.make_async_remote_copy(src_ref=part_ref.at[_index(peer)], dst_ref=mod_ref.at[my], send_sem=send_sems.at[6 + r],
                                              recv_sem=recv_sems.at[6 + r], device_id=peer, device_id_type=MESH)
            cp.start()
            scatter.append(cp)
        mod_ref[my] = part_ref[my]
        for cp in scatter:
            cp.wait()
        mod_ref[...] = mod_ref[...] + b_ref[...]

    vm = pl.BlockSpec(memory_space=pltpu.VMEM)
    return pl.pallas_call(
        body, name="ada_fwd",
        out_shape=[jax.ShapeDtypeStruct((N_DEV, 1, cs), F32), jax.ShapeDtypeStruct((N_DEV, 1, d), F32),
                   jax.ShapeDtypeStruct((N_DEV,) + conv_w.shape, F32)],
        in_specs=[vm, vm, vm, vm], out_specs=[vm, vm, vm],
        scratch_shapes=[pltpu.VMEM((N_DEV, 1, cs), F32), pltpu.SemaphoreType.DMA((21,)), pltpu.SemaphoreType.DMA((21,))],
        compiler_params=pltpu.CompilerParams(vmem_limit_bytes=VMEM_LIMIT_BYTES))(c, w_ada, b_ada3, conv_w)


def ada_bwd_w(sc_all, dmod_cols):
    _, d = sc_all.shape
    cs = dmod_cols.shape[1]
    tr = _tile(d, ROW_TILE)

    def body(sc_ref, dm_ref, o_ref):
        dm = dm_ref[...].astype(BF16)
        o_ref[...] = lax.dot_general(sc_ref[...].astype(BF16), dm, (((0,), (0,)), ((), ())), preferred_element_type=F32)

    return pl.pallas_call(body, name="ada_bwd_w", grid=(d // tr,),
                          in_specs=[pl.BlockSpec((N_DEV, tr), lambda i: (0, i)), _full((N_DEV, cs))],
                          out_specs=pl.BlockSpec((None, tr, cs), lambda i: (0, i, 0)),
                          out_shape=jax.ShapeDtypeStruct((1, d, cs), F32), compiler_params=_params("parallel"))(sc_all, dmod_cols)


def _round_up(n, m):
    return (n + m - 1) // m * m


def kernel(x, c, positions, w_ada, b_ada, pre_norm1_g, w_in, gm_ln_g, gm_ln_b, gm_w_s, gm_b_s, w_branch_a, q_norm_g, w_uq, kv_norm_g, w_ukv, w_branch_b, w_out, post_norm1_g, pre_norm2_g, w_up, conv_w, conv_b, w_down, post_norm2_g, loss_target, m_w_ada, m_b_ada, m_pre_norm1_g, m_w_in, m_gm_ln_g, m_gm_ln_b, m_gm_w_s, m_gm_b_s, m_w_branch_a, m_q_norm_g, m_w_uq, m_kv_norm_g, m_w_ukv, m_w_branch_b, m_w_out, m_post_norm1_g, m_pre_norm2_g, m_w_up, m_conv_w, m_conv_b, m_w_down, m_post_norm2_g, v_w_ada, v_b_ada, v_pre_norm1_g, v_w_in, v_gm_ln_g, v_gm_ln_b, v_gm_w_s, v_gm_b_s, v_w_branch_a, v_q_norm_g, v_w_uq, v_kv_norm_g, v_w_ukv, v_w_branch_b, v_w_out, v_post_norm1_g, v_pre_norm2_g, v_w_up, v_conv_w, v_conv_b, v_w_down, v_post_norm2_g):
    weights = dict(w_ada=w_ada, b_ada=b_ada, pre_norm1_g=pre_norm1_g, w_in=w_in, gm_ln_g=gm_ln_g, gm_ln_b=gm_ln_b, gm_w_s=gm_w_s,
                   gm_b_s=gm_b_s, w_branch_a=w_branch_a, q_norm_g=q_norm_g, w_uq=w_uq, kv_norm_g=kv_norm_g, w_ukv=w_ukv,
                   w_branch_b=w_branch_b, w_out=w_out, post_norm1_g=post_norm1_g, pre_norm2_g=pre_norm2_g, w_up=w_up, conv_w=conv_w,
                   conv_b=conv_b, w_down=w_down, post_norm2_g=post_norm2_g)
    mom1 = dict(w_ada=m_w_ada, b_ada=m_b_ada, pre_norm1_g=m_pre_norm1_g, w_in=m_w_in, gm_ln_g=m_gm_ln_g, gm_ln_b=m_gm_ln_b,
                gm_w_s=m_gm_w_s, gm_b_s=m_gm_b_s, w_branch_a=m_w_branch_a, q_norm_g=m_q_norm_g, w_uq=m_w_uq, kv_norm_g=m_kv_norm_g,
                w_ukv=m_w_ukv, w_branch_b=m_w_branch_b, w_out=m_w_out, post_norm1_g=m_post_norm1_g, pre_norm2_g=m_pre_norm2_g,
                w_up=m_w_up, conv_w=m_conv_w, conv_b=m_conv_b, w_down=m_w_down, post_norm2_g=m_post_norm2_g)
    mom2 = dict(w_ada=v_w_ada, b_ada=v_b_ada, pre_norm1_g=v_pre_norm1_g, w_in=v_w_in, gm_ln_g=v_gm_ln_g, gm_ln_b=v_gm_ln_b,
                gm_w_s=v_gm_w_s, gm_b_s=v_gm_b_s, w_branch_a=v_w_branch_a, q_norm_g=v_q_norm_g, w_uq=v_w_uq, kv_norm_g=v_kv_norm_g,
                w_ukv=v_w_ukv, w_branch_b=v_w_branch_b, w_out=v_w_out, post_norm1_g=v_post_norm1_g, pre_norm2_g=v_pre_norm2_g,
                w_up=v_w_up, conv_w=v_conv_w, conv_b=v_conv_b, w_down=v_w_down, post_norm2_g=v_post_norm2_g)
    order = list(weights)
    _TOKENS.clear()

    s, d = x.shape[1], x.shape[2]
    gmw = gm_ln_g.shape[0]
    groups = gmw // CHUNK
    ql, kvl = q_norm_g.shape[0], kv_norm_g.shape[0]
    f2 = conv_b.shape[0]
    in_cols = w_in.shape[1] * N_DEV
    o_q, o_kv, o_ga, o_gb, o_kpe = 2 * gmw, 2 * gmw + ql, 2 * gmw + ql + kvl, 2 * gmw + ql + kvl + d, 2 * gmw + ql + kvl + 2 * d
    zp = _round_up(o_kpe + LANES, Z_PAD)
    src_kpe = 2 * gmw + ql + kvl
    assert src_kpe + QK_ROPE + 2 * d == in_cols
    my = 4 * lax.axis_index("x") + 2 * lax.axis_index("y") + lax.axis_index("c")

    x2, tgt = x[0], loss_target[0]
    row = lambda a: a.reshape(1, -1)

    big = ["w_in", "w_branch_a", "w_uq", "w_ukv", "w_branch_b", "w_out", "w_up", "w_down"]
    sh = {k: weights[k].astype(BF16) for k in big[1:]}
    mix = ["w_branch_a", "w_uq", "w_ukv", "w_branch_b", "w_out"]
    ag_in = ag_start("ag_start_in", [w_in.T.astype(BF16)], c)

    mod8, sc_all3, g_cw = ada_fwd(c, w_ada, b_ada.reshape(N_DEV, 1, -1), conv_w)
    mod = mod8.reshape(N_MOD, d)
    shift1, scale1, gate1, shift2, scale2, gate2 = (mod[i:i + 1] for i in range(N_MOD))
    sc_all = sc_all3.reshape(N_DEV, d)
    h1 = norm_mod_fwd("pre1_fwd", x2, row(pre_norm1_g), scale1, shift1)

    inv = ROPE_THETA ** (-jnp.arange(0, QK_ROPE, 2, dtype=F32) / QK_ROPE)
    ang = positions[0].astype(F32)[:, None] * inv
    cos4 = jnp.tile(jnp.cos(ang), (1, 4))
    sin4 = jnp.tile(jnp.concatenate([-jnp.sin(ang), jnp.sin(ang)], axis=1), (1, 2))

    wm = (gm_w_s * jnp.tril(jnp.ones((CHUNK, CHUNK), F32))).astype(BF16)
    bs3 = gm_b_s.reshape(groups, CHUNK, 1)
    ln_g, ln_b = row(gm_ln_g), row(gm_ln_b)

    small_names = ["pre_norm1_g", "gm_ln_g", "gm_ln_b", "gm_b_s", "q_norm_g", "kv_norm_g", "post_norm1_g", "pre_norm2_g", "conv_b",
                   "post_norm2_g", "gm_w_s", "b_ada"]
    n_small_early = sum(weights[k].size for k in small_names)
    n_pack_early = _round_up(n_small_early + 3 * f2, PACK_ALIGN)

    def pack(src):
        return jnp.concatenate([src[k].reshape(-1) for k in small_names] + [jnp.zeros((n_pack_early - n_small_early,), F32)]).reshape(-1, LANES)

    packed_state = [pack(weights), pack(mom1), pack(mom2)]

    early = [h1, cos4, sin4, wm] + [sh[k] for k in big[1:]] + packed_state
    ag_in = ag_forward("ag_forward_in", ag_in, early)
    ag_mix = ag_start("ag_start_mix", [sh[k] for k in mix], _TOKENS.pending[-1])
    (g_in,) = ag_wait("ag_wait_in", ag_in, [h1, _TOKENS.pending[-1]])
    w_in_f = g_in.reshape(in_cols, d)
    w_in_p = jnp.concatenate([w_in_f[:src_kpe], w_in_f[src_kpe + QK_ROPE:], w_in_f[src_kpe:src_kpe + QK_ROPE],
                              jnp.zeros((zp - in_cols, d), BF16)], axis=0)

    z = mm_nt("z_proj", h1, w_in_p, ACT)
    ag_mix = ag_forward("ag_forward_mix", ag_mix, z)
    ag_up = ag_start("ag_start_up", [sh["w_up"]], _TOKENS.pending[-1])
    a = gmlp_fwd(z, gmw, ln_g, ln_b, wm, bs3)
    g_a, g_uq, g_ukv, g_b, g_out = ag_wait("ag_wait_mix", ag_mix, [a, _TOKENS.pending[-1]])
    w_a_f, w_b_f, w_out_f = g_a.reshape(-1, d), g_b.reshape(-1, d), g_out.reshape(-1, d)
    w_uq_f = g_uq.transpose(1, 0, 2).reshape(ql, HEADS, QK_NOPE + QK_ROPE)
    w_uq_n = w_uq_f[:, :, :QK_NOPE].reshape(ql, HEADS * QK_NOPE)
    w_uq_r = w_uq_f[:, :, QK_NOPE:].reshape(ql, HEADS * QK_ROPE)
    y_a = mm_nn("branch_a", a, w_a_f, ACT)
    qln = rms_fwd_cols("q_norm", z, o_q, ql, row(q_norm_g))
    kvn = rms_fwd_cols("kv_norm", z, o_kv, kvl, row(kv_norm_g))
    qn = mm_nn("q_nope", qln, w_uq_n, BF16)
    qp = mm_nn("q_rope", qln, w_uq_r, F32)
    kv = mm_nn_b3("kv_up", kvn, g_ukv, BF16)
    kpr = rope_k(z, o_kpe, cos4, sin4)
    o, qpr, lse = attn_fwd2(qn, qp, kv, kpr, cos4, sin4)
    ag_up = ag_forward("ag_forward_up", ag_up, o)
    ag_down = ag_start("ag_start_down", [sh["w_down"]], _TOKENS.pending[-1])
    y_b = mm_nn("branch_b", o, w_b_f, ACT)
    merged = merge_fwd(z, o_ga, o_gb, y_a, y_b)
    y1 = mm_nn("out_proj", merged, w_out_f, ACT)
    x1 = post_res_fwd("post1_fwd", x2, y1, gate1, row(post_norm1_g))
    h2 = norm_mod_fwd("pre2_fwd", x1, row(pre_norm2_g), scale2, shift2)
    (g_up,) = ag_wait("ag_wait_up", ag_up, h2)
    upre = mm_nn_b3("up_proj", h2, g_up, ACT)
    ag_down = ag_forward("ag_forward_down", ag_down, upre)
    cw = g_cw.transpose(1, 0, 2).reshape(3, f2)
    cb = row(conv_b)
    f, gv = conv_act_fwd(upre, cw, cb)
    w_down_f = ag_wait("ag_wait_down", ag_down, f)[0].reshape(-1, d)
    ffn = mm_nn("down_proj", f, w_down_f, ACT)
    loss_acc, dout, dffn, acc2 = post2_loss_bwd(x1, ffn, tgt, gate2, row(post_norm2_g))
    loss = lax.psum(loss_acc[0, 0], ("x", "y", "c"))
    _TOKENS.push(jnp.broadcast_to(loss, (8, LANES)))

    blocks = lambda g: g.reshape(N_DEV, g.shape[0] // N_DEV, g.shape[1])
    core = lax.axis_index("c").astype(jnp.int32).reshape(1)
    rs = {}

    def rs_begin(key, grads):
        rs[key] = rs_d2d_start("rs_d2d_start_" + key, grads)

    def rs_middle(key, after):
        grads, lands = rs_d2d_wait("rs_d2d_wait_" + key, rs[key], after)
        sums = [pair_sum("pair_sum_%s_%d" % (key, i), g, l, core) for i, (g, l) in enumerate(zip(grads, lands))]
        rs[key] = rs_ici_start("rs_ici_start_" + key, sums)

    gw_down = mm_tn("g_w_down", f, dffn, BF16)
    rs_begin("down", [blocks(gw_down)])
    df = mm_nt("d_f", dffn, w_down_f, ACT)
    rs_middle("down", df)
    dupre, gcw_g, gcw_v, gcb_g, gcb_v = conv_act_bwd(upre, cw, gv, df)
    gw_up3 = mm_tn_h3("g_w_up", h2, dupre, N_DEV, BF16)
    rs_begin("up", [gw_up3])
    dh2 = mm_nt_h3("d_h2", dupre, g_up, ACT)
    rs_middle("up", dh2)
    dx1, dy1, acc_mid = mid_bwd(dh2, dout, x1, y1, row(pre_norm2_g), scale2, gate1, row(post_norm1_g))
    gw_out = mm_tn("g_w_out", merged, dy1, BF16)
    dmerged = mm_nt("d_merged", dy1, w_out_f, ACT)
    dya, dyb, dga, dgb = merge_bwd(z, o_ga, o_gb, y_a, y_b, dmerged)
    gw_a = mm_tn("g_w_a", a, dya, BF16)
    gw_b = mm_tn("g_w_b", o, dyb, BF16)
    rs_begin("mid", [blocks(gw_out), blocks(gw_a), blocks(gw_b)])
    da = mm_nt("d_a", dya, w_a_f, ACT)
    do = mm_nt("d_o", dyb, w_b_f, ACT)
    rs_middle("mid", do)
    duv, g_ws, g_bs3, acc_gm = gmlp_bwd(z, gmw, da, ln_g, ln_b, wm, bs3)
    dqn, dqp, dkv, dkp = attn_bwd2(qn, qpr, kv, kpr, o, do, lse, cos4, sin4)
    dkpe = kpe_bwd(dkp, cos4, sin4, zp - o_kpe)
    dq_cat = jnp.concatenate([dqn, dqp], axis=1)
    w_uq_cat = jnp.concatenate([w_uq_n, w_uq_r], axis=1)
    gw_uq_cat = mm_tn("g_w_uq", qln, dq_cat, BF16)
    gw_uq_f = jnp.concatenate([gw_uq_cat[:, :HEADS * QK_NOPE].reshape(ql, HEADS, QK_NOPE),
                               gw_uq_cat[:, HEADS * QK_NOPE:].reshape(ql, HEADS, QK_ROPE)], axis=2)
    gw_uq3 = gw_uq_f.reshape(ql, N_DEV, -1).transpose(1, 0, 2)
    gw_ukv3 = mm_tn_o3("g_w_ukv", kvn, dkv, N_DEV, BF16)
    rs_begin("mla", [gw_uq3, gw_ukv3])
    dqln = mm_nt("d_qln", dq_cat, w_uq_cat, ACT)
    dq_lat, g_qnorm = rms_bwd_cols("q_norm_bwd", dqln, z, o_q, ql, row(q_norm_g))
    dkvn = mm_nt_b3("d_kvn", dkv, g_ukv, ACT)
    rs_middle("mla", dkvn)
    dkv_lat, g_kvnorm = rms_bwd_cols("kv_norm_bwd", dkvn, z, o_kv, kvl, row(kv_norm_g))
    dz = jnp.concatenate([duv, dq_lat, dkv_lat, dga, dgb, dkpe], axis=1)
    gw_in_p = mm_tn("g_w_in", dz, h1, BF16)
    gw_in_f = jnp.concatenate([gw_in_p[:src_kpe], gw_in_p[o_kpe:o_kpe + QK_ROPE], gw_in_p[src_kpe:o_kpe]], axis=0)
    rs_begin("in", [gw_in_f.reshape(N_DEV, -1, d)])
    dh1 = mm_nn("d_h1", dz, w_in_p, ACT)
    grad_x, acc1 = pre1_bwd(dh1, dx1, x2, row(pre_norm1_g), scale1)

    dmod = jnp.concatenate([acc1[0], acc1[1], acc_mid[3], acc_mid[0], acc_mid[1], acc2[0]])
    small = [("pre_norm1_g", acc1[2]), ("gm_ln_g", acc_gm[0]), ("gm_ln_b", acc_gm[1]), ("gm_b_s", g_bs3.reshape(-1)),
             ("q_norm_g", g_qnorm[0]), ("kv_norm_g", g_kvnorm[0]), ("post_norm1_g", acc_mid[4]), ("pre_norm2_g", acc_mid[2]),
             ("conv_b", jnp.concatenate([gcb_g[0], gcb_v[0]])), ("post_norm2_g", acc2[1]), ("gm_w_s", g_ws.reshape(-1)),
             ("b_ada", dmod)]
    n_small = sum(v.shape[0] for _, v in small)
    n_cw = 3 * f2
    n_pack = _round_up(n_small + n_cw, PACK_ALIGN)
    tail = jnp.zeros((n_pack - n_small - n_cw,), F32)
    packed = jnp.concatenate([v for _, v in small] + [jnp.concatenate([gcw_g, gcw_v], axis=1).reshape(-1), tail])
    ag_small = ag_start("ag_start_small", [packed.reshape(-1, LANES)], packed)
    rs_middle("in", [packed, _TOKENS.pending[-1]])

    res = {}
    last = packed
    for key, names in (("down", ["w_down"]), ("up", ["w_up"]), ("mid", ["w_out", "w_branch_a", "w_branch_b"]), ("mla", ["w_uq", "w_ukv"])):
        parts = rs_ici_wait("rs_ici_wait_" + key, rs[key], last)
        for k, p in zip(names, parts):
            res[k] = adamw("adamw_" + k, weights[k], mom1[k], mom2[k], p)
            last = res[k][0]

    assert [k for k, _ in small] == small_names and n_small == n_small_early
    (gathered,) = ag_wait("ag_wait_small", ag_forward("ag_forward_small", ag_small, last), last)
    sm = [t.reshape(-1) for t in adamw("adamw_small", *packed_state, gathered)]
    off = 0
    for k, v in small:
        res[k] = tuple(t[off:off + v.shape[0]].reshape(weights[k].shape) for t in sm)
        off += v.shape[0]

    cs_cw = conv_w.shape[1]
    g_cw_full = sm[0][n_small:n_small + n_cw].reshape(3, f2)
    g_cw_mine = lax.dynamic_slice(g_cw_full, (0, my * cs_cw), (3, cs_cw))
    res["conv_w"] = adamw("adamw_conv_w", conv_w, mom1["conv_w"], mom2["conv_w"], g_cw_mine[None])

    cs_ada = w_ada.shape[1]
    off_b = n_small - N_MOD * d
    dmod_all = gathered.reshape(N_DEV, -1)[:, off_b:off_b + N_MOD * d]
    dmod_cols = lax.dynamic_slice(dmod_all, (0, my * cs_ada), (N_DEV, cs_ada))
    res["w_ada"] = adamw("adamw_w_ada", w_ada, mom1["w_ada"], mom2["w_ada"], ada_bwd_w(sc_all, dmod_cols))

    (p_in,) = rs_ici_wait("rs_ici_wait_in", rs["in"], res["w_ada"][0])
    res["w_in"] = tuple(t.T for t in adamw("adamw_w_in", w_in.T, mom1["w_in"].T, mom2["w_in"].T, p_in))

    _TOKENS.clear()
    outs = [loss, grad_x[None]]
    for i in range(4):
        outs += [res[k][i] for k in order]
    return tuple(outs)
```

```python
import jax
import jax.numpy as jnp
from jax import lax
from jax.experimental import pallas as pl
from jax.experimental.pallas import tpu as pltpu

F32 = jnp.float32
BF16 = jnp.bfloat16
ACT = BF16

N_DEV = 8
HEADS = 16
QK_NOPE = 128
QK_ROPE = 64
V_HEAD = 128
CHUNK = 128
ROPE_THETA = 10000.0
EPS = 1e-6
N_MOD = 6
ADAM_LR, ADAM_B1, ADAM_B2, ADAM_EPS, ADAM_WD, ADAM_STEP = 0.001, 0.9, 0.999, 1e-08, 0.01, 10

LANES = 128
VMEM_LIMIT_BYTES = 48 * 2 ** 20
ROW_TILE = 256
COL_TILE = 256
ATT_TILE = 512
Z_PAD = 512
ADAMW_TILE_ELEMS = 1 << 18
PACK_ALIGN = 8 * LANES
MESH = pl.DeviceIdType.MESH


def _params(*sem):
    return pltpu.CompilerParams(dimension_semantics=sem if sem else None, vmem_limit_bytes=VMEM_LIMIT_BYTES)


def _tile(dim, target):
    t = (min(dim, target) // LANES) * LANES
    while t >= LANES:
        if dim % t == 0:
            return t
        t -= LANES
    return dim


def _full(shape):
    nd = len(shape)
    return pl.BlockSpec(shape, lambda *_: (0,) * nd)


class _Tokens:
    KEEP = 2

    def __init__(self):
        self.pending = []

    def push(self, token):
        self.pending = (self.pending + [token])[-self.KEEP:]

    def take(self):
        return list(self.pending)

    def clear(self):
        self.pending = []


_TOKENS = _Tokens()


def _matmul(name, a, b, *, grid, a_spec, b_spec, o_spec, out_shape, contract, acc_shape, split=1):
    nk = grid[2]
    deps = _TOKENS.take()

    def product(a_ref, b_ref):
        if len(b_ref.shape) == 2:
            return lax.dot_general(a_ref[...].astype(BF16), b_ref[...].astype(BF16), (contract, ((), ())), preferred_element_type=F32)
        cs = b_ref.shape[2]
        return sum(lax.dot_general(a_ref[:, s * cs:(s + 1) * cs].astype(BF16), b_ref[s].astype(BF16), (contract, ((), ())),
                                   preferred_element_type=F32) for s in range(split))

    def body_one_step(a_ref, b_ref, *rest):
        o_ref = rest[len(deps)]
        o_ref[...] = product(a_ref, b_ref).astype(o_ref.dtype)

    def body(a_ref, b_ref, *rest):
        o_ref, acc_ref = rest[len(deps):]
        k = pl.program_id(2)

        @pl.when(k == 0)
        def _():
            acc_ref[...] = jnp.zeros_like(acc_ref)

        acc_ref[...] += product(a_ref, b_ref)

        @pl.when(k == nk - 1)
        def _():
            o_ref[...] = acc_ref[...].astype(o_ref.dtype)

    return pl.pallas_call(
        body_one_step if nk == 1 else body, name=name, grid=grid,
        in_specs=[a_spec, b_spec] + [pl.BlockSpec(memory_space=pl.ANY)] * len(deps),
        out_specs=o_spec, out_shape=out_shape, scratch_shapes=[] if nk == 1 else [pltpu.VMEM(acc_shape, F32)],
        compiler_params=_params("parallel", "parallel", "arbitrary"))(a, b, *deps)


T_OUT, T_OUT_WIDE, TK = 1024, 1408, 2816


def _out_tile(dim):
    return T_OUT_WIDE if dim % T_OUT_WIDE == 0 else _tile(dim, T_OUT)


def _tk(a, b):
    return TK if a.dtype == BF16 and b.dtype == BF16 else TK // 2


def mm_nn(name, a, b, dtype):
    (m, k), n = a.shape, b.shape[1]
    tm, tn, tk = _out_tile(m), _out_tile(n), _tile(k, _tk(a, b))
    return _matmul(name, a, b, grid=(m // tm, n // tn, k // tk),
                   a_spec=pl.BlockSpec((tm, tk), lambda i, j, kk: (i, kk)),
                   b_spec=pl.BlockSpec((tk, tn), lambda i, j, kk: (kk, j)),
                   o_spec=pl.BlockSpec((tm, tn), lambda i, j, kk: (i, j)),
                   out_shape=jax.ShapeDtypeStruct((m, n), dtype), contract=((1,), (0,)), acc_shape=(tm, tn))


def mm_nn_b3(name, a, b3, dtype):
    (m, k), (nj, _, cs) = a.shape, b3.shape
    tm, tk = _out_tile(m), _tile(k, _tk(a, b3))
    return _matmul(name, a, b3, grid=(m // tm, nj, k // tk),
                   a_spec=pl.BlockSpec((tm, tk), lambda i, j, kk: (i, kk)),
                   b_spec=pl.BlockSpec((None, tk, cs), lambda i, j, kk: (j, kk, 0)),
                   o_spec=pl.BlockSpec((tm, cs), lambda i, j, kk: (i, j)),
                   out_shape=jax.ShapeDtypeStruct((m, nj * cs), dtype), contract=((1,), (0,)), acc_shape=(tm, cs))


def mm_nt(name, a, b, dtype):
    (m, k), n = a.shape, b.shape[0]
    tm, tn, tk = _out_tile(m), _out_tile(n), _tile(k, _tk(a, b))
    return _matmul(name, a, b, grid=(m // tm, n // tn, k // tk),
                   a_spec=pl.BlockSpec((tm, tk), lambda i, j, kk: (i, kk)),
                   b_spec=pl.BlockSpec((tn, tk), lambda i, j, kk: (j, kk)),
                   o_spec=pl.BlockSpec((tm, tn), lambda i, j, kk: (i, j)),
                   out_shape=jax.ShapeDtypeStruct((m, n), dtype), contract=((1,), (1,)), acc_shape=(tm, tn))


def mm_nt_b3(name, a, b3, dtype):
    m, (nj, n, cs) = a.shape[0], b3.shape
    tm, tn = _out_tile(m), _out_tile(n)
    return _matmul(name, a, b3, grid=(m // tm, n // tn, nj),
                   a_spec=pl.BlockSpec((tm, cs), lambda i, j, kk: (i, kk)),
                   b_spec=pl.BlockSpec((None, tn, cs), lambda i, j, kk: (kk, j, 0)),
                   o_spec=pl.BlockSpec((tm, tn), lambda i, j, kk: (i, j)),
                   out_shape=jax.ShapeDtypeStruct((m, n), dtype), contract=((1,), (1,)), acc_shape=(tm, tn))


def mm_nt_h3(name, a3, b3, dtype):
    (_, m, _), (nj, n, cs) = a3.shape, b3.shape
    tm, tn, hj = _out_tile(m), _out_tile(n), nj // 2
    pair = 2 if hj % 2 == 0 else 1
    return _matmul(name, a3, b3.reshape(nj // pair, pair, n, cs), grid=(m // tm, n // tn, nj // pair),
                   a_spec=pl.BlockSpec((None, tm, pair * cs), lambda i, j, kk: (kk // (hj // pair), i, kk % (hj // pair))),
                   b_spec=pl.BlockSpec((None, pair, tn, cs), lambda i, j, kk: (kk, 0, j, 0)),
                   o_spec=pl.BlockSpec((tm, tn), lambda i, j, kk: (i, j)),
                   out_shape=jax.ShapeDtypeStruct((m, n), dtype), contract=((1,), (1,)), acc_shape=(tm, tn), split=pair)


def mm_tn_h3(name, a, b3, nj, dtype):
    (k, m), half = a.shape, b3.shape[2]
    hj = nj // 2
    cs = half // hj
    tm, tk = _out_tile(m), _tile(k, _tk(a, b3))
    return _matmul(name, a, b3, grid=(m // tm, nj, k // tk),
                   a_spec=pl.BlockSpec((tk, tm), lambda i, j, kk: (kk, i)),
                   b_spec=pl.BlockSpec((None, tk, cs), lambda i, j, kk: (j // hj, kk, j % hj)),
                   o_spec=pl.BlockSpec((None, tm, cs), lambda i, j, kk: (j, i, 0)),
                   out_shape=jax.ShapeDtypeStruct((nj, m, cs), dtype), contract=((0,), (0,)), acc_shape=(tm, cs))


def mm_tn(name, a, b, dtype):
    (k, m), n = a.shape, b.shape[1]
    tm, tn, tk = _out_tile(m), _out_tile(n), _tile(k, _tk(a, b))
    return _matmul(name, a, b, grid=(m // tm, n // tn, k // tk),
                   a_spec=pl.BlockSpec((tk, tm), lambda i, j, kk: (kk, i)),
                   b_spec=pl.BlockSpec((tk, tn), lambda i, j, kk: (kk, j)),
                   o_spec=pl.BlockSpec((tm, tn), lambda i, j, kk: (i, j)),
                   out_shape=jax.ShapeDtypeStruct((m, n), dtype), contract=((0,), (0,)), acc_shape=(tm, tn))


def mm_tn_o3(name, a, b, nj, dtype):
    (k, m), n = a.shape, b.shape[1]
    cs = n // nj
    tm, tk = _out_tile(m), _tile(k, _tk(a, b))
    return _matmul(name, a, b, grid=(m // tm, nj, k // tk),
                   a_spec=pl.BlockSpec((tk, tm), lambda i, j, kk: (kk, i)),
                   b_spec=pl.BlockSpec((tk, cs), lambda i, j, kk: (kk, j)),
                   o_spec=pl.BlockSpec((None, tm, cs), lambda i, j, kk: (j, i, 0)),
                   out_shape=jax.ShapeDtypeStruct((nj, m, cs), dtype), contract=((0,), (0,)), acc_shape=(tm, cs))


_GELU_C = 0.7978845608028654
_GELU_A = 0.044715


def _f32(ref):
    return ref[...].astype(F32)


def _gelu(x):
    x = x.astype(F32)
    return 0.5 * x * (1.0 + jnp.tanh(_GELU_C * (x + _GELU_A * x * x * x)))


def _gelu_and_grad(x):
    x = x.astype(F32)
    t = jnp.tanh(_GELU_C * (x + _GELU_A * x * x * x))
    y = 0.5 * x * (1.0 + t)
    dy = 0.5 * (1.0 + t) + 0.5 * x * (1.0 - t * t) * (_GELU_C * (1.0 + 3.0 * _GELU_A * x * x))
    return y, dy


def _sigmoid(x):
    return 1.0 / (1.0 + jnp.exp(-x.astype(F32)))


def _rms_stats(x):
    x = x.astype(F32)
    inv = lax.rsqrt(jnp.mean(x * x, axis=-1, keepdims=True) + EPS)
    return inv, x * inv


def _rms_bwd(dyhat, yhat, inv):
    return inv * (dyhat - yhat * jnp.mean(dyhat * yhat, axis=-1, keepdims=True))


def _colsum(x):
    return jnp.sum(x, axis=0, keepdims=True)


def _rope(x, cos4, sin4):
    lane = lax.broadcasted_iota(jnp.int32, x.shape, x.ndim - 1)
    first_half = (lane % QK_ROPE) < (QK_ROPE // 2)
    partner = jnp.where(first_half, pltpu.roll(x, LANES - QK_ROPE // 2, x.ndim - 1), pltpu.roll(x, QK_ROPE // 2, x.ndim - 1))
    return x * cos4 + partner * sin4


def norm_mod_fwd(name, x, g, scale, shift):
    s, d = x.shape
    tr = _tile(s, ROW_TILE)

    def body(x_ref, g_ref, sc_ref, sh_ref, o_ref):
        _, xh = _rms_stats(x_ref[...])
        o_ref[...] = (xh * g_ref[...] * (1.0 + sc_ref[...]) + sh_ref[...]).astype(o_ref.dtype)

    row = pl.BlockSpec((tr, d), lambda i: (i, 0))
    vec = pl.BlockSpec((1, d), lambda i: (0, 0))
    return pl.pallas_call(body, name=name, grid=(s // tr,), in_specs=[row, vec, vec, vec], out_specs=row,
                          out_shape=jax.ShapeDtypeStruct((s, d), BF16), compiler_params=_params("parallel"))(x, g, scale, shift)


def rms_fwd_cols(name, z, off, width, g):
    s = z.shape[0]
    tr = _tile(s, ROW_TILE)
    assert off % width == 0

    def body(x_ref, g_ref, o_ref):
        _, xh = _rms_stats(x_ref[...])
        o_ref[...] = (xh * g_ref[...]).astype(o_ref.dtype)

    return pl.pallas_call(body, name=name, grid=(s // tr,),
                          in_specs=[pl.BlockSpec((tr, width), lambda i: (i, off // width)), pl.BlockSpec((1, width), lambda i: (0, 0))],
                          out_specs=pl.BlockSpec((tr, width), lambda i: (i, 0)),
                          out_shape=jax.ShapeDtypeStruct((s, width), BF16), compiler_params=_params("parallel"))(z, g)


def rms_bwd_cols(name, dy, z, off, width, g):
    s = z.shape[0]
    tr = _tile(s, ROW_TILE)

    def body(dy_ref, x_ref, g_ref, dx_ref, gg_ref):
        @pl.when(pl.program_id(0) == 0)
        def _():
            gg_ref[...] = jnp.zeros_like(gg_ref)

        inv, xh = _rms_stats(x_ref[...])
        dy_v = _f32(dy_ref)
        gg_ref[...] += _colsum(dy_v * xh)
        dx_ref[...] = _rms_bwd(dy_v * g_ref[...], xh, inv).astype(dx_ref.dtype)

    return pl.pallas_call(body, name=name, grid=(s // tr,),
                          in_specs=[pl.BlockSpec((tr, width), lambda i: (i, 0)), pl.BlockSpec((tr, width), lambda i: (i, off // width)),
                                    pl.BlockSpec((1, width), lambda i: (0, 0))],
                          out_specs=[pl.BlockSpec((tr, width), lambda i: (i, 0)), pl.BlockSpec((1, width), lambda i: (0, 0))],
                          out_shape=[jax.ShapeDtypeStruct((s, width), BF16), jax.ShapeDtypeStruct((1, width), F32)],
                          compiler_params=_params("arbitrary"))(dy, z, g)


def post_res_fwd(name, x, y, gate, g):
    s, d = x.shape
    tr = _tile(s, ROW_TILE)

    def body(x_ref, y_ref, gate_ref, g_ref, o_ref):
        _, yh = _rms_stats(y_ref[...])
        o_ref[...] = x_ref[...] + gate_ref[...] * (yh * g_ref[...])

    row = pl.BlockSpec((tr, d), lambda i: (i, 0))
    vec = pl.BlockSpec((1, d), lambda i: (0, 0))
    return pl.pallas_call(body, name=name, grid=(s // tr,), in_specs=[row, row, vec, vec], out_specs=row,
                          out_shape=jax.ShapeDtypeStruct((s, d), F32), compiler_params=_params("parallel"))(x, y, gate, g)


def post2_loss_bwd(x1, ffn, target, gate2, g):
    s, d = x1.shape
    tr = _tile(s, ROW_TILE)

    def body(x_ref, y_ref, t_ref, gate_ref, g_ref, loss_ref, dout_ref, dy_ref, acc_ref):
        @pl.when(pl.program_id(0) == 0)
        def _():
            loss_ref[...] = jnp.zeros_like(loss_ref)
            acc_ref[...] = jnp.zeros_like(acc_ref)

        inv, yh = _rms_stats(y_ref[...])
        r = yh * g_ref[...]
        err = x_ref[...] + gate_ref[...] * r - t_ref[...]
        loss_ref[...] += 0.5 * jnp.sum(jnp.mean(err * err, axis=-1, keepdims=True))
        dout = err / d
        dout_ref[...] = dout
        dr = dout * gate_ref[...]
        acc_ref[0:1, :] += _colsum(dout * r)
        acc_ref[1:2, :] += _colsum(dr * yh)
        dy_ref[...] = _rms_bwd(dr * g_ref[...], yh, inv).astype(dy_ref.dtype)

    row = pl.BlockSpec((tr, d), lambda i: (i, 0))
    vec = pl.BlockSpec((1, d), lambda i: (0, 0))
    return pl.pallas_call(
        body, name="post2_loss_bwd", grid=(s // tr,), in_specs=[row, row, row, vec, vec],
        out_specs=[_full((8, LANES)), row, row, _full((8, d))],
        out_shape=[jax.ShapeDtypeStruct((8, LANES), F32), jax.ShapeDtypeStruct((s, d), F32),
                   jax.ShapeDtypeStruct((s, d), BF16), jax.ShapeDtypeStruct((8, d), F32)],
        compiler_params=_params("arbitrary"))(x1, ffn, target, gate2, g)


def mid_bwd(dh2, dout, x1, y1, pre2_g, scale2, gate1, post1_g):
    s, d = x1.shape
    tr = _tile(s, ROW_TILE)

    def body(dh_ref, dout_ref, x_ref, y_ref, g2_ref, sc_ref, gate_ref, g1_ref, dx_ref, dy_ref, acc_ref):
        @pl.when(pl.program_id(0) == 0)
        def _():
            acc_ref[...] = jnp.zeros_like(acc_ref)

        dh = _f32(dh_ref)
        inv2, xh = _rms_stats(x_ref[...])
        acc_ref[0:1, :] += _colsum(dh)
        acc_ref[1:2, :] += _colsum(dh * (xh * g2_ref[...]))
        t = dh * (1.0 + sc_ref[...])
        acc_ref[2:3, :] += _colsum(t * xh)
        dx1 = dout_ref[...] + _rms_bwd(t * g2_ref[...], xh, inv2)
        dx_ref[...] = dx1
        inv1, yh = _rms_stats(y_ref[...])
        acc_ref[3:4, :] += _colsum(dx1 * (yh * g1_ref[...]))
        dr = dx1 * gate_ref[...]
        acc_ref[4:5, :] += _colsum(dr * yh)
        dy_ref[...] = _rms_bwd(dr * g1_ref[...], yh, inv1).astype(dy_ref.dtype)

    row = pl.BlockSpec((tr, d), lambda i: (i, 0))
    vec = pl.BlockSpec((1, d), lambda i: (0, 0))
    return pl.pallas_call(
        body, name="mid_bwd", grid=(s // tr,), in_specs=[row, row, row, row, vec, vec, vec, vec],
        out_specs=[row, row, _full((8, d))],
        out_shape=[jax.ShapeDtypeStruct((s, d), F32), jax.ShapeDtypeStruct((s, d), BF16), jax.ShapeDtypeStruct((8, d), F32)],
        compiler_params=_params("arbitrary"))(dh2, dout, x1, y1, pre2_g, scale2, gate1, post1_g)


def pre1_bwd(dh1, dx1, x, pre1_g, scale1):
    s, d = x.shape
    tr = _tile(s, ROW_TILE)

    def body(dh_ref, dx1_ref, x_ref, g_ref, sc_ref, dx_ref, acc_ref):
        @pl.when(pl.program_id(0) == 0)
        def _():
            acc_ref[...] = jnp.zeros_like(acc_ref)

        dh = _f32(dh_ref)
        inv, xh = _rms_stats(x_ref[...])
        acc_ref[0:1, :] += _colsum(dh)
        acc_ref[1:2, :] += _colsum(dh * (xh * g_ref[...]))
        t = dh * (1.0 + sc_ref[...])
        acc_ref[2:3, :] += _colsum(t * xh)
        dx_ref[...] = dx1_ref[...] + _rms_bwd(t * g_ref[...], xh, inv)

    row = pl.BlockSpec((tr, d), lambda i: (i, 0))
    vec = pl.BlockSpec((1, d), lambda i: (0, 0))
    return pl.pallas_call(
        body, name="pre1_bwd", grid=(s // tr,), in_specs=[row, row, row, vec, vec], out_specs=[row, _full((8, d))],
        out_shape=[jax.ShapeDtypeStruct((s, d), F32), jax.ShapeDtypeStruct((8, d), F32)],
        compiler_params=_params("arbitrary"))(dh1, dx1, x, pre1_g, scale1)


def _ln_stats(v):
    mu = jnp.mean(v, axis=-1, keepdims=True)
    vc = v - mu
    rstd = lax.rsqrt(jnp.mean(vc * vc, axis=-1, keepdims=True) + EPS)
    return rstd, vc * rstd


def gmlp_fwd(z, width, ln_g, ln_b, wm, bs3):
    s = z.shape[0]
    groups = width // CHUNK

    def body(u_ref, v_ref, g_ref, b_ref, wm_ref, bs_ref, a_ref):
        ug = _gelu(u_ref[...])
        _, vh = _ln_stats(_gelu(v_ref[...]))
        vn = (vh * g_ref[...] + b_ref[...]).astype(BF16)
        for g in range(groups):
            cols = slice(g * CHUNK, (g + 1) * CHUNK)
            mixed = jnp.dot(wm_ref[g], vn[:, cols], preferred_element_type=F32) + bs_ref[g]
            a_ref[:, cols] = (ug[:, cols] * mixed).astype(a_ref.dtype)

    vec = pl.BlockSpec((1, width), lambda n: (0, 0))
    return pl.pallas_call(
        body, name="gmlp_fwd", grid=(s // CHUNK,),
        in_specs=[pl.BlockSpec((CHUNK, width), lambda n: (n, 0)), pl.BlockSpec((CHUNK, width), lambda n: (n, 1)), vec, vec,
                  _full(wm.shape), _full(bs3.shape)],
        out_specs=pl.BlockSpec((CHUNK, width), lambda n: (n, 0)),
        out_shape=jax.ShapeDtypeStruct((s, width), BF16), compiler_params=_params("parallel"))(z, z, ln_g, ln_b, wm, bs3)


def gmlp_bwd(z, width, da, ln_g, ln_b, wm, bs3):
    s = z.shape[0]
    groups = width // CHUNK

    def body(u_ref, v_ref, da_ref, g_ref, b_ref, wm_ref, bs_ref, duv_ref, gw_ref, gb_ref, acc_ref, dvn_ref):
        @pl.when(pl.program_id(0) == 0)
        def _():
            gw_ref[...] = jnp.zeros_like(gw_ref)
            gb_ref[...] = jnp.zeros_like(gb_ref)
            acc_ref[...] = jnp.zeros_like(acc_ref)

        ug, dug = _gelu_and_grad(u_ref[...])
        vg, dvg = _gelu_and_grad(v_ref[...])
        rstd, vh = _ln_stats(vg)
        vn = (vh * g_ref[...] + b_ref[...]).astype(BF16)
        da_v = _f32(da_ref)
        for g in range(groups):
            cols = slice(g * CHUNK, (g + 1) * CHUNK)
            mixed = jnp.dot(wm_ref[g], vn[:, cols], preferred_element_type=F32) + bs_ref[g]
            duv_ref[:, cols] = (da_v[:, cols] * mixed * dug[:, cols]).astype(duv_ref.dtype)
            dm = da_v[:, cols] * ug[:, cols]
            gb_ref[g] += jnp.sum(dm, axis=-1, keepdims=True)
            dmb = dm.astype(BF16)
            gw_ref[g] += lax.dot_general(dmb, vn[:, cols], (((1,), (1,)), ((), ())), preferred_element_type=F32)
            dvn_ref[:, cols] = lax.dot_general(wm_ref[g], dmb, (((0,), (0,)), ((), ())), preferred_element_type=F32)
        dvn = dvn_ref[...]
        acc_ref[0:1, :] += _colsum(dvn * vh)
        acc_ref[1:2, :] += _colsum(dvn)
        dvh = dvn * g_ref[...]
        dv = rstd * (dvh - jnp.mean(dvh, axis=-1, keepdims=True) - vh * jnp.mean(dvh * vh, axis=-1, keepdims=True))
        duv_ref[:, width:] = (dv * dvg).astype(duv_ref.dtype)

        @pl.when(pl.program_id(0) == pl.num_programs(0) - 1)
        def _():
            q = lax.broadcasted_iota(jnp.int32, gw_ref.shape, 1)
            p = lax.broadcasted_iota(jnp.int32, gw_ref.shape, 2)
            gw_ref[...] = jnp.where(p <= q, gw_ref[...], 0.0)

    vec = pl.BlockSpec((1, width), lambda n: (0, 0))
    blk = pl.BlockSpec((CHUNK, width), lambda n: (n, 0))
    return pl.pallas_call(
        body, name="gmlp_bwd", grid=(s // CHUNK,),
        in_specs=[blk, pl.BlockSpec((CHUNK, width), lambda n: (n, 1)), blk, vec, vec, _full(wm.shape), _full(bs3.shape)],
        out_specs=[pl.BlockSpec((CHUNK, 2 * width), lambda n: (n, 0)), _full(wm.shape), _full(bs3.shape), _full((8, width))],
        out_shape=[jax.ShapeDtypeStruct((s, 2 * width), BF16), jax.ShapeDtypeStruct(wm.shape, F32),
                   jax.ShapeDtypeStruct(bs3.shape, F32), jax.ShapeDtypeStruct((8, width), F32)],
        scratch_shapes=[pltpu.VMEM((CHUNK, width), F32)],
        compiler_params=_params("arbitrary"))(z, z, da, ln_g, ln_b, wm, bs3)


def merge_fwd(z, off_a, off_b, ya, yb):
    s, d = ya.shape
    tr, tc = _tile(s, ROW_TILE * 2), _tile(d, COL_TILE)
    assert off_a % tc == 0 and off_b % tc == 0

    def body(ga_ref, gb_ref, ya_ref, yb_ref, o_ref):
        o_ref[...] = (_sigmoid(ga_ref[...]) * _f32(ya_ref) + _sigmoid(gb_ref[...]) * _f32(yb_ref)).astype(o_ref.dtype)

    blk = pl.BlockSpec((tr, tc), lambda i, j: (i, j))
    return pl.pallas_call(
        body, name="merge_fwd", grid=(s // tr, d // tc),
        in_specs=[pl.BlockSpec((tr, tc), lambda i, j: (i, off_a // tc + j)), pl.BlockSpec((tr, tc), lambda i, j: (i, off_b // tc + j)), blk, blk],
        out_specs=blk, out_shape=jax.ShapeDtypeStruct((s, d), BF16), compiler_params=_params("parallel", "parallel"))(z, z, ya, yb)


def merge_bwd(z, off_a, off_b, ya, yb, dm):
    s, d = ya.shape
    tr, tc = _tile(s, ROW_TILE * 2), _tile(d, COL_TILE)
    nc = d // tc

    def body(ga_ref, gb_ref, ya_ref, yb_ref, dm_ref, dya_ref, dyb_ref, dga_ref, dgb_ref):
        dm_v = _f32(dm_ref)
        sa, sb = _sigmoid(ga_ref[...]), _sigmoid(gb_ref[...])
        dya_ref[...] = (dm_v * sa).astype(dya_ref.dtype)
        dyb_ref[...] = (dm_v * sb).astype(dyb_ref.dtype)
        dga_ref[...] = (dm_v * _f32(ya_ref) * sa * (1.0 - sa)).astype(dga_ref.dtype)
        dgb_ref[...] = (dm_v * _f32(yb_ref) * sb * (1.0 - sb)).astype(dgb_ref.dtype)

    blk = pl.BlockSpec((tr, tc), lambda i, j: (i, j))
    out = jax.ShapeDtypeStruct((s, d), BF16)
    return pl.pallas_call(
        body, name="merge_bwd", grid=(s // tr, nc),
        in_specs=[pl.BlockSpec((tr, tc), lambda i, j: (i, off_a // tc + j)), pl.BlockSpec((tr, tc), lambda i, j: (i, off_b // tc + j)), blk, blk, blk],
        out_specs=[blk, blk, blk, blk], out_shape=[out, out, out, out],
        compiler_params=_params("parallel", "parallel"))(z, z, ya, yb, dm)


_ATT_SCALE = (QK_NOPE + QK_ROPE) ** -0.5
_NEG = -1e30


def rope_k(z, off, cos4, sin4):
    s = z.shape[0]
    tr = _tile(s, ROW_TILE * 2)
    assert off % LANES == 0

    def body(k_ref, c_ref, s_ref, o_ref):
        k = _f32(k_ref)
        k = k + pltpu.roll(k, QK_ROPE, 1)
        o_ref[...] = _rope(k, c_ref[...], s_ref[...]).astype(o_ref.dtype)

    row = pl.BlockSpec((tr, LANES), lambda i: (i, 0))
    return pl.pallas_call(body, name="rope_k", grid=(s // tr,),
                          in_specs=[pl.BlockSpec((tr, LANES), lambda i: (i, off // LANES)), row, row], out_specs=row,
                          out_shape=jax.ShapeDtypeStruct((s, LANES), BF16), compiler_params=_params("parallel"))(z, cos4, sin4)


def _dot_nt(a, b):
    return lax.dot_general(a, b, (((1,), (1,)), ((), ())), preferred_element_type=F32)


def _dot_tn(a, b):
    return lax.dot_general(a, b, (((0,), (0,)), ((), ())), preferred_element_type=F32)


def _q_cat(q_n, qpr, hh):
    lane = lax.broadcasted_iota(jnp.int32, qpr.shape, 1)
    sel = (lane < QK_ROPE) if hh == 0 else (lane >= QK_ROPE)
    return jnp.concatenate([q_n, jnp.where(sel, qpr, jnp.zeros_like(qpr))], axis=1)


def _causal(sc):
    row = lax.broadcasted_iota(jnp.int32, sc.shape, 0)
    col = lax.broadcasted_iota(jnp.int32, sc.shape, 1)
    return jnp.where(col <= row, sc, _NEG)


def attn_fwd(qn, qp, kv, kpr, cos4, sin4):
    s = qn.shape[0]
    hp = HEADS // 2
    t = _tile(s, ATT_TILE)
    nq = s // t

    def body(qn_ref, qp_ref, kv_ref, kp_ref, c_ref, s_ref, o_ref, qpr_ref, l_ref, kcat_ref):
        qi = pl.program_id(1)

        @pl.when(qi == 0)
        def _():
            for hh in range(2):
                kcat_ref[hh, :, 0:QK_NOPE] = kv_ref[:, 2 * hh * QK_NOPE:(2 * hh + 1) * QK_NOPE]
                kcat_ref[hh, :, QK_NOPE:] = kp_ref[...]

        qpr = _rope(qp_ref[...], c_ref[...], s_ref[...]).astype(BF16)
        qpr_ref[...] = qpr
        qcat = [_q_cat(qn_ref[:, hh * QK_NOPE:(hh + 1) * QK_NOPE], qpr, hh) for hh in range(2)]

        def block(kb, carry, diagonal):
            rows = pl.ds(pl.multiple_of(kb * t, t), t)
            out = []
            for hh in range(2):
                m, l, acc = carry[hh]
                sc = _dot_nt(qcat[hh], kcat_ref[hh, rows, :]) * _ATT_SCALE
                if diagonal:
                    sc = _causal(sc)
                m_new = jnp.maximum(m, jnp.max(sc, axis=-1, keepdims=True))
                alpha = jnp.exp(m - m_new)
                p = jnp.exp(sc - m_new)
                l = alpha * l + jnp.sum(p, axis=-1, keepdims=True)
                v = kv_ref[rows, (2 * hh + 1) * QK_NOPE:(2 * hh + 2) * QK_NOPE]
                acc = alpha * acc + jnp.dot(p.astype(BF16), v, preferred_element_type=F32)
                out.append((m_new, l, acc))
            return tuple(out)

        one = (jnp.full((t, 1), _NEG, F32), jnp.zeros((t, 1), F32), jnp.zeros((t, V_HEAD), F32))
        carry = lax.fori_loop(0, qi, lambda kb, cr: block(kb, cr, False), (one, one))
        carry = block(qi, carry, True)
        for hh in range(2):
            m, l, acc = carry[hh]
            o_ref[:, hh * V_HEAD:(hh + 1) * V_HEAD] = (acc / l).astype(o_ref.dtype)
            l_ref[:, hh:hh + 1] = m + jnp.log(l)

    return pl.pallas_call(
        body, name="attn_fwd", grid=(hp, nq),
        in_specs=[pl.BlockSpec((t, 2 * QK_NOPE), lambda h, i: (i, h)), pl.BlockSpec((t, LANES), lambda h, i: (i, h)),
                  pl.BlockSpec((s, 4 * QK_NOPE), lambda h, i: (0, h)), _full((s, LANES)),
                  pl.BlockSpec((t, LANES), lambda h, i: (i, 0)), pl.BlockSpec((t, LANES), lambda h, i: (i, 0))],
        out_specs=[pl.BlockSpec((t, 2 * V_HEAD), lambda h, i: (i, h)), pl.BlockSpec((t, LANES), lambda h, i: (i, h)),
                   pl.BlockSpec((None, t, 2), lambda h, i: (h, i, 0))],
        out_shape=[jax.ShapeDtypeStruct((s, HEADS * V_HEAD), ACT), jax.ShapeDtypeStruct((s, HEADS * QK_ROPE), BF16),
                   jax.ShapeDtypeStruct((hp, s, 2), F32)],
        scratch_shapes=[pltpu.VMEM((2, s, 2 * QK_NOPE), BF16)],
        compiler_params=_params("parallel", "arbitrary"))(qn, qp, kv, kpr, cos4, sin4)


def attn_bwd(qn, qpr, kv, kpr, o, do, lse, cos4, sin4):
    s = qn.shape[0]
    hp = HEADS // 2
    t = _tile(s, ATT_TILE)
    nk = s // t

    def body(qn_ref, qpr_ref, kv_ref, kp_ref, o_ref, do_ref, l_ref, c_ref, s_ref,
             dqn_ref, dqp_ref, dkv_ref, dkp_ref, qcat_ref, dq_ref, delta_ref):
        ki = pl.program_id(1)

        @pl.when(ki == 0)
        def _():
            dq_ref[...] = jnp.zeros_like(dq_ref)
            for hh in range(2):
                qcat_ref[hh] = _q_cat(qn_ref[:, hh * QK_NOPE:(hh + 1) * QK_NOPE], qpr_ref[...], hh)
                cols = slice(hh * V_HEAD, (hh + 1) * V_HEAD)
                delta_ref[hh] = jnp.sum(do_ref[:, cols].astype(F32) * o_ref[:, cols].astype(F32), axis=-1, keepdims=True)

        rows_k = pl.ds(pl.multiple_of(ki * t, t), t)
        kcat = [jnp.concatenate([kv_ref[rows_k, 2 * hh * QK_NOPE:(2 * hh + 1) * QK_NOPE], kp_ref[rows_k, :]], axis=1) for hh in range(2)]
        vs = [kv_ref[rows_k, (2 * hh + 1) * QK_NOPE:(2 * hh + 2) * QK_NOPE] for hh in range(2)]

        def block(qb, carry, diagonal):
            rows = pl.ds(pl.multiple_of(qb * t, t), t)
            out = []
            for hh in range(2):
                dkc, dv = carry[hh]
                q_c = qcat_ref[hh, rows, :]
                do_b = do_ref[rows, hh * V_HEAD:(hh + 1) * V_HEAD].astype(BF16)
                sc = _dot_nt(q_c, kcat[hh]) * _ATT_SCALE
                if diagonal:
                    sc = _causal(sc)
                p = jnp.exp(sc - l_ref[rows, hh:hh + 1])
                dpv = _dot_nt(do_b, vs[hh])
                ds = (p * (dpv - delta_ref[hh, rows, :]) * _ATT_SCALE).astype(BF16)
                dv = dv + _dot_tn(p.astype(BF16), do_b)
                dkc = dkc + _dot_tn(ds, q_c)
                dq_ref[hh, rows, :] += jnp.dot(ds, kcat[hh], preferred_element_type=F32)
                out.append((dkc, dv))
            return tuple(out)

        one = (jnp.zeros((t, 2 * QK_NOPE), F32), jnp.zeros((t, V_HEAD), F32))
        carry = block(ki, (one, one), True)
        carry = lax.fori_loop(ki + 1, nk, lambda qb, cr: block(qb, cr, False), carry)
        dkp = jnp.zeros((t, LANES), F32)
        for hh in range(2):
            dkc, dv = carry[hh]
            dkv_ref[:, 2 * hh * QK_NOPE:(2 * hh + 1) * QK_NOPE] = dkc[:, :QK_NOPE].astype(dkv_ref.dtype)
            dkv_ref[:, (2 * hh + 1) * QK_NOPE:(2 * hh + 2) * QK_NOPE] = dv.astype(dkv_ref.dtype)
            dkp = dkp + dkc[:, QK_NOPE:]
        dkp_ref[...] = dkp

        @pl.when(ki == nk - 1)
        def _():
            lane = lax.broadcasted_iota(jnp.int32, (s, LANES), 1)
            dqp = jnp.where(lane < QK_ROPE, dq_ref[0, :, QK_NOPE:], dq_ref[1, :, QK_NOPE:])
            dqp_ref[...] = _rope(dqp, c_ref[...], -s_ref[...]).astype(dqp_ref.dtype)
            for hh in range(2):
                dqn_ref[:, hh * QK_NOPE:(hh + 1) * QK_NOPE] = dq_ref[hh, :, :QK_NOPE].astype(dqn_ref.dtype)

    qblk = pl.BlockSpec((s, 2 * QK_NOPE), lambda h, i: (0, h))
    pblk = pl.BlockSpec((s, LANES), lambda h, i: (0, h))
    tab = _full((s, LANES))
    return pl.pallas_call(
        body, name="attn_bwd", grid=(hp, nk),
        in_specs=[qblk, pblk, pl.BlockSpec((s, 4 * QK_NOPE), lambda h, i: (0, h)), tab, qblk, qblk,
                  pl.BlockSpec((None, s, 2), lambda h, i: (h, 0, 0)), tab, tab],
        out_specs=[qblk, pblk, pl.BlockSpec((t, 4 * QK_NOPE), lambda h, i: (i, h)), pl.BlockSpec((None, t, LANES), lambda h, i: (h, i, 0))],
        out_shape=[jax.ShapeDtypeStruct((s, HEADS * QK_NOPE), BF16), jax.ShapeDtypeStruct((s, HEADS * QK_ROPE), BF16),
                   jax.ShapeDtypeStruct((s, HEADS * 2 * QK_NOPE), BF16), jax.ShapeDtypeStruct((hp, s, LANES), F32)],
        scratch_shapes=[pltpu.VMEM((2, s, 2 * QK_NOPE), BF16), pltpu.VMEM((2, s, 2 * QK_NOPE), F32), pltpu.VMEM((2, s, 1), F32)],
        compiler_params=_params("parallel", "arbitrary"))(qn, qpr, kv, kpr, o, do, lse, cos4, sin4)


def kpe_bwd(dkp, cos4, sin4, pad_cols):
    hp, s, _ = dkp.shape
    tr = _tile(s, ROW_TILE * 2)

    def body(d_ref, c_ref, s_ref, o_ref):
        tot = d_ref[0]
        for h in range(1, hp):
            tot = tot + d_ref[h]
        tot = tot + pltpu.roll(tot, QK_ROPE, 1)
        lane = lax.broadcasted_iota(jnp.int32, tot.shape, 1)
        dk = jnp.where(lane < QK_ROPE, _rope(tot, c_ref[...], -s_ref[...]), jnp.zeros_like(tot))
        o_ref[...] = jnp.zeros_like(o_ref)
        o_ref[:, 0:LANES] = dk.astype(o_ref.dtype)

    row = pl.BlockSpec((tr, LANES), lambda i: (i, 0))
    return pl.pallas_call(body, name="kpe_bwd", grid=(s // tr,),
                          in_specs=[pl.BlockSpec((hp, tr, LANES), lambda i: (0, i, 0)), row, row],
                          out_specs=pl.BlockSpec((tr, pad_cols), lambda i: (i, 0)),
                          out_shape=jax.ShapeDtypeStruct((s, pad_cols), BF16), compiler_params=_params("parallel"))(dkp, cos4, sin4)


def _shift_down(x, n):
    row = lax.broadcasted_iota(jnp.int32, x.shape, 0)
    return jnp.where(row >= n, pltpu.roll(x, n, 0), jnp.zeros_like(x))


def _shift_up(x, n):
    rows = x.shape[0]
    row = lax.broadcasted_iota(jnp.int32, x.shape, 0)
    return jnp.where(row < rows - n, pltpu.roll(x, rows - n, 0), jnp.zeros_like(x))


def _conv(x, w_ref, b_ref):
    return w_ref[2:3, :] * x + w_ref[1:2, :] * _shift_down(x, 1) + w_ref[0:1, :] * _shift_down(x, 2) + b_ref[...]


def conv_act_fwd(upre, conv_w, conv_b):
    s, f2 = upre.shape
    f = f2 // 2
    tc = _tile(f, COL_TILE)
    nc = f // tc

    def body(ug_ref, uv_ref, wg_ref, wv_ref, bg_ref, bv_ref, o_ref, gv_ref):
        gh = _conv(_f32(ug_ref), wg_ref, bg_ref)
        vh = _conv(_f32(uv_ref), wv_ref, bv_ref)
        o_ref[...] = (gh * _sigmoid(gh) * vh).astype(o_ref.dtype)
        gv_ref[0] = gh.astype(gv_ref.dtype)
        gv_ref[1] = vh.astype(gv_ref.dtype)

    def spec(rows, shift):
        return pl.BlockSpec((rows, tc), lambda j: (0, j + shift))

    return pl.pallas_call(
        body, name="conv_act_fwd", grid=(nc,),
        in_specs=[spec(s, 0), spec(s, nc), spec(3, 0), spec(3, nc), spec(1, 0), spec(1, nc)],
        out_specs=[spec(s, 0), pl.BlockSpec((2, s, tc), lambda j: (0, 0, j))],
        out_shape=[jax.ShapeDtypeStruct((s, f), BF16), jax.ShapeDtypeStruct((2, s, f), ACT)],
        compiler_params=_params("parallel"))(upre, upre, conv_w, conv_w, conv_b, conv_b)


def conv_act_bwd(upre, conv_w, gv, df):
    s, f2 = upre.shape
    f = f2 // 2
    tc = _tile(f, COL_TILE)
    nc = f // tc

    def half(x, d, w_ref, du_ref, which, gw_ref, gb_ref):
        d1, d2 = _shift_up(d, 1), _shift_up(d, 2)
        gb_ref[...] = _colsum(d)
        gw_ref[2:3, :] = _colsum(d * x)
        gw_ref[1:2, :] = _colsum(d1 * x)
        gw_ref[0:1, :] = _colsum(d2 * x)
        du_ref[which] = (w_ref[2:3, :] * d + w_ref[1:2, :] * d1 + w_ref[0:1, :] * d2).astype(du_ref.dtype)

    def body(ug_ref, uv_ref, wg_ref, wv_ref, gv_ref, df_ref, du_ref, gwg_ref, gwv_ref, gbg_ref, gbv_ref):
        xg, xv = _f32(ug_ref), _f32(uv_ref)
        gh, vh = gv_ref[0].astype(F32), gv_ref[1].astype(F32)
        sg = _sigmoid(gh)
        df_v = _f32(df_ref)
        half(xg, df_v * vh * (sg * (1.0 + gh * (1.0 - sg))), wg_ref, du_ref, 0, gwg_ref, gbg_ref)
        half(xv, df_v * (gh * sg), wv_ref, du_ref, 1, gwv_ref, gbv_ref)

    def spec(rows, shift):
        return pl.BlockSpec((rows, tc), lambda j: (0, j + shift))

    gw = jax.ShapeDtypeStruct((3, f), F32)
    gb = jax.ShapeDtypeStruct((1, f), F32)
    return pl.pallas_call(
        body, name="conv_act_bwd", grid=(nc,),
        in_specs=[spec(s, 0), spec(s, nc), spec(3, 0), spec(3, nc), pl.BlockSpec((2, s, tc), lambda j: (0, 0, j)), spec(s, 0)],
        out_specs=[pl.BlockSpec((2, s, tc), lambda j: (0, 0, j)), spec(3, 0), spec(3, 0), spec(1, 0), spec(1, 0)],
        out_shape=[jax.ShapeDtypeStruct((2, s, f), BF16), gw, gw, gb, gb],
        compiler_params=_params("parallel"))(upre, upre, conv_w, conv_w, gv, df)


def _elementwise_tile(r, c, limit):
    if r % 8:
        return r, c
    best = (8, c if c % LANES else LANES)
    for k in (1, 2, 4, 8, 16):
        if k > 1 and c % (LANES * k):
            continue
        tc = c // k
        tr = max(8, min(r, limit // tc) // 8 * 8)
        while r % tr:
            tr -= 8
        if tr * tc <= max(limit, 8 * tc) and tr * tc > best[0] * best[1]:
            best = (tr, tc)
    return best


def adamw(name, w, m, v, parts):
    npart, r, c = parts.shape
    tr, tc = _elementwise_tile(r, c, ADAMW_TILE_ELEMS)
    bc1 = 1.0 - ADAM_B1 ** ADAM_STEP
    bc2 = 1.0 - ADAM_B2 ** ADAM_STEP

    def body(w_ref, m_ref, v_ref, p_ref, g_ref, d_ref, nm_ref, nv_ref):
        g = p_ref[0].astype(F32)
        for k in range(1, npart):
            g = g + p_ref[k].astype(F32)
        m_new = ADAM_B1 * m_ref[...] + (1.0 - ADAM_B1) * g
        v_new = ADAM_B2 * v_ref[...] + (1.0 - ADAM_B2) * (g * g)
        g_ref[...] = g
        nm_ref[...] = m_new
        nv_ref[...] = v_new
        d_ref[...] = -ADAM_LR * ((m_new / bc1) / (jnp.sqrt(v_new / bc2) + ADAM_EPS) + ADAM_WD * w_ref[...])

    deps = _TOKENS.take()
    blk = pl.BlockSpec((tr, tc), lambda i, j: (i, j))
    out = jax.ShapeDtypeStruct((r, c), F32)
    return pl.pallas_call(
        lambda *refs: body(*refs[:4], *refs[4 + len(deps):]), name=name, grid=(r // tr, c // tc),
        in_specs=[blk, blk, blk, pl.BlockSpec((npart, tr, tc), lambda i, j: (0, i, j))] + [pl.BlockSpec(memory_space=pl.ANY)] * len(deps),
        out_specs=[blk, blk, blk, blk], out_shape=[out, out, out, out],
        compiler_params=_params("parallel", "parallel"))(w, m, v, parts, *deps)


def _position():
    return lax.axis_index("x"), lax.axis_index("y"), lax.axis_index("c")


def _index(p):
    return 4 * p[0] + 2 * p[1] + p[2]


def _peer(me, r):
    return (me[0] ^ ((r >> 2) & 1), me[1] ^ ((r >> 1) & 1), me[2] ^ (r & 1))


_ANY = pl.BlockSpec(memory_space=pl.ANY)


_HBM = pl.BlockSpec(memory_space=pltpu.HBM)
_SEM = pl.BlockSpec(memory_space=pltpu.SEMAPHORE)
_EFFECT = pltpu.SideEffectType.DATAFLOW_SIDE_EFFECTING
_TOKEN = jax.ShapeDtypeStruct((8, LANES), F32)
_VM = pl.BlockSpec(memory_space=pltpu.VMEM)
_SIDE = pltpu.CompilerParams(has_side_effects=_EFFECT)


def _hbm(a):
    return pltpu.with_memory_space_constraint(a, pltpu.HBM)


def _like(a):
    return pltpu.HBM(a.shape, a.dtype)


def _dma_sems(n):
    return pltpu.SemaphoreType.DMA((n,))


def _other_chips(x, y):
    return [(1 - x, y), (x, 1 - y), (1 - x, 1 - y)]


COPY_STREAMS = 8


def _row_chunks(src, dst):
    rows = src.shape[0]
    n = COPY_STREAMS
    while n > 1 and rows % (16 * n):
        n //= 2
    r = rows // n
    return [(src.at[pl.ds(i * r, r)], dst.at[pl.ds(i * r, r)]) for i in range(n)]


class _rcopy:
    def __init__(self, src, dst, send_sem, recv_sem, to):
        self.parts = [pltpu.make_async_remote_copy(src_ref=s, dst_ref=d, send_sem=send_sem, recv_sem=recv_sem, device_id=to, device_id_type=MESH)
                      for s, d in _row_chunks(src, dst)]

    def start(self):
        for cp in self.parts:
            cp.start()

    def wait_send(self):
        for cp in self.parts:
            cp.wait_send()

    def wait_recv(self):
        for cp in self.parts:
            cp.wait_recv()


def _afters(after):
    return list(after) if isinstance(after, (list, tuple)) else [after]


def ag_start(name, shards, after):
    n = len(shards)
    lands = [lax.empty((N_DEV,) + a.shape, a.dtype) for a in shards]
    afters = _afters(after)
    na = len(afters)

    def body(*refs):
        ins, lnd, send_sems, recv_sems, token = refs[:n], refs[n:2 * n], refs[2 * n + na], refs[2 * n + na + 1], refs[4 * n + na + 2]
        x, y, c = _position()
        for w in range(n):
            slot = lnd[w].at[_index((x, y, c))]
            for k, to in enumerate([(x, y, 1 - c)] + [(*chip, c) for chip in _other_chips(x, y)]):
                _rcopy(ins[w], slot, send_sems.at[4 * w + k], recv_sems.at[4 * w + k], to).start()
        token[...] = jnp.zeros_like(token)

    out = pl.pallas_call(
        body, name=name, out_shape=(_dma_sems(4 * n), _dma_sems(4 * n)) + tuple(_like(a) for a in shards + lands) + (_TOKEN,),
        in_specs=[_HBM] * (2 * n) + [_ANY] * na, out_specs=(_SEM, _SEM) + (_HBM,) * (2 * n) + (_VM,),
        input_output_aliases={i: 2 + i for i in range(2 * n)}, compiler_params=_SIDE)(*[_hbm(a) for a in shards + lands], *afters)
    _TOKENS.push(out[-1])
    return out[0], out[1], list(out[2:2 + n]), list(out[2 + n:2 + 2 * n])


def ag_forward(name, started, after):
    send, recv, shards, lands = started
    n = len(shards)
    afters = list(after) if isinstance(after, (list, tuple)) else [after]
    na = len(afters)

    def body(*refs):
        ins, lnd, send_sems, recv_sems = refs[:n], refs[n:2 * n], refs[2 * n], refs[2 * n + 1]
        fsend, frecv, token = refs[2 * n + 2 + na], refs[2 * n + 3 + na], refs[4 * n + 4 + na]
        x, y, c = _position()
        for w in range(n):
            for j, chip in enumerate(_other_chips(x, y)):
                slot = lnd[w].at[_index((*chip, c))]
                _rcopy(ins[w], slot, send_sems.at[4 * w + 1 + j], recv_sems.at[4 * w + 1 + j], (*chip, c)).wait_recv()
                _rcopy(slot, slot, fsend.at[3 * w + j], frecv.at[3 * w + j], (x, y, 1 - c)).start()
        token[...] = jnp.zeros_like(token)

    out = pl.pallas_call(
        body, name=name, out_shape=(_dma_sems(3 * n), _dma_sems(3 * n)) + tuple(_like(a) for a in shards + lands) + (_TOKEN,),
        in_specs=[_HBM] * (2 * n) + [_SEM, _SEM] + [_ANY] * na, out_specs=(_SEM, _SEM) + (_HBM,) * (2 * n) + (_VM,),
        input_output_aliases={i: 2 + i for i in range(2 * n)}, compiler_params=_SIDE)(*shards, *lands, send, recv, *afters)
    _TOKENS.push(out[-1])
    return send, recv, out[0], out[1], list(out[2:2 + n]), list(out[2 + n:2 + 2 * n])


def ag_wait(name, forwarded, after):
    send, recv, fsend, frecv, shards, lands = forwarded
    n = len(shards)

    def body(*refs):
        ins, lnd, send_sems, recv_sems, fsend_r, frecv_r = refs[:n], refs[n:2 * n], refs[2 * n], refs[2 * n + 1], refs[2 * n + 2], refs[2 * n + 3]
        x, y, c = _position()
        sibling = (x, y, 1 - c)
        for w in range(n):
            own = lnd[w].at[_index((x, y, c))]
            _rcopy(ins[w], lnd[w].at[_index(sibling)], send_sems.at[4 * w], recv_sems.at[4 * w], sibling).wait_recv()
            for j, chip in enumerate(_other_chips(x, y)):
                _rcopy(ins[w], lnd[w].at[_index((*chip, 1 - c))], fsend_r.at[3 * w + j], frecv_r.at[3 * w + j], sibling).wait_recv()
            for k in range(4):
                _rcopy(ins[w], own, send_sems.at[4 * w + k], recv_sems.at[4 * w + k], sibling).wait_send()
            for j in range(3):
                _rcopy(ins[w], own, fsend_r.at[3 * w + j], frecv_r.at[3 * w + j], sibling).wait_send()

    out = pl.pallas_call(
        body, name=name, out_shape=tuple(_like(a) for a in shards + lands),
        in_specs=[_HBM] * (2 * n) + [_SEM] * 4 + [_ANY] * len(_afters(after)),
        out_specs=(_HBM,) * (2 * n), input_output_aliases={i: i for i in range(2 * n)},
        compiler_params=_SIDE)(*shards, *lands, send, recv, fsend, frecv, *_afters(after))
    return [lax.dynamic_update_index_in_dim(land, shard, _index(_position()), 0) for shard, land in zip(out[:n], out[n:])]


def rs_d2d_start(name, grads):
    n = len(grads)
    lands = [lax.empty((4,) + g.shape[1:], g.dtype) for g in grads]

    def body(*refs):
        ins, lnd, send_sems, recv_sems, token = refs[:n], refs[n:2 * n], refs[2 * n], refs[2 * n + 1], refs[4 * n + 2]
        x, y, c = _position()
        for w in range(n):
            for i in range(4):
                _rcopy(ins[w].at[2 * i + 1 - c], lnd[w].at[i], send_sems.at[4 * w + i], recv_sems.at[4 * w + i], (x, y, 1 - c)).start()
        token[...] = jnp.zeros_like(token)

    out = pl.pallas_call(
        body, name=name, out_shape=(_dma_sems(4 * n), _dma_sems(4 * n)) + tuple(_like(a) for a in grads + lands) + (_TOKEN,),
        in_specs=[_HBM] * (2 * n), out_specs=(_SEM, _SEM) + (_HBM,) * (2 * n) + (_VM,),
        input_output_aliases={i: 2 + i for i in range(2 * n)}, compiler_params=_SIDE)(*[_hbm(a) for a in grads + lands])
    _TOKENS.push(out[-1])
    return out[0], out[1], list(out[2:2 + n]), list(out[2 + n:2 + 2 * n])


def rs_d2d_wait(name, started, after):
    send, recv, grads, lands = started
    n = len(grads)

    def body(*refs):
        ins, lnd, send_sems, recv_sems = refs[:n], refs[n:2 * n], refs[2 * n], refs[2 * n + 1]
        x, y, c = _position()
        for w in range(n):
            for i in range(4):
                cp = _rcopy(ins[w].at[2 * i + 1 - c], lnd[w].at[i], send_sems.at[4 * w + i], recv_sems.at[4 * w + i], (x, y, 1 - c))
                cp.wait_send()
                cp.wait_recv()

    out = pl.pallas_call(
        body, name=name, out_shape=tuple(_like(a) for a in grads + lands),
        in_specs=[_HBM] * (2 * n) + [_SEM, _SEM] + [_ANY] * len(_afters(after)),
        out_specs=(_HBM,) * (2 * n), input_output_aliases={i: i for i in range(2 * n)},
        compiler_params=_SIDE)(*grads, *lands, send, recv, *_afters(after))
    return list(out[:n]), list(out[n:])


def pair_sum(name, grad, land, core):
    _, r, c = grad.shape
    tr = r
    if r % 8 == 0:
        tr = max(8, min(r, 4 * ADAMW_TILE_ELEMS // c) // 8 * 8)
        while r % tr:
            tr -= 8

    def body(core_ref, a_ref, b_ref, o_ref):
        o_ref[...] = (a_ref[...].astype(F32) + b_ref[...].astype(F32)).astype(o_ref.dtype)

    return pl.pallas_call(
        body, name=name, out_shape=jax.ShapeDtypeStruct((4, r, c), grad.dtype),
        grid_spec=pltpu.PrefetchScalarGridSpec(
            num_scalar_prefetch=1, grid=(4, r // tr),
            in_specs=[pl.BlockSpec((None, None, tr, c), lambda i, j, core_ref: (i, core_ref[0], j, 0)),
                      pl.BlockSpec((None, tr, c), lambda i, j, core_ref: (i, j, 0))],
            out_specs=pl.BlockSpec((None, tr, c), lambda i, j, core_ref: (i, j, 0))),
        compiler_params=_params("parallel", "parallel"))(core, grad.reshape(4, 2, r, c), land)


def rs_ici_start(name, sums):
    n = len(sums)
    lands = [lax.empty(a.shape, a.dtype) for a in sums]

    def body(*refs):
        ins, lnd, send_sems, recv_sems, token = refs[:n], refs[n:2 * n], refs[2 * n], refs[2 * n + 1], refs[4 * n + 2]
        x, y, c = _position()
        chip = 2 * x + y
        for w in range(n):
            for j, other in enumerate(_other_chips(x, y)):
                _rcopy(ins[w].at[2 * other[0] + other[1]], lnd[w].at[chip], send_sems.at[3 * w + j], recv_sems.at[3 * w + j], (*other, c)).start()
        token[...] = jnp.zeros_like(token)

    out = pl.pallas_call(
        body, name=name, out_shape=(_dma_sems(3 * n), _dma_sems(3 * n)) + tuple(_like(a) for a in sums + lands) + (_TOKEN,),
        in_specs=[_HBM] * (2 * n), out_specs=(_SEM, _SEM) + (_HBM,) * (2 * n) + (_VM,),
        input_output_aliases={i: 2 + i for i in range(2 * n)}, compiler_params=_SIDE)(*[_hbm(a) for a in sums + lands])
    _TOKENS.push(out[-1])
    return out[0], out[1], list(out[2:2 + n]), list(out[2 + n:2 + 2 * n])


def rs_ici_wait(name, started, after):
    send, recv, sums, lands = started
    n = len(sums)

    def body(*refs):
        ins, lnd, send_sems, recv_sems = refs[:n], refs[n:2 * n], refs[2 * n], refs[2 * n + 1]
        x, y, c = _position()
        for w in range(n):
            for j, other in enumerate(_other_chips(x, y)):
                cp = _rcopy(ins[w].at[2 * other[0] + other[1]], lnd[w].at[2 * other[0] + other[1]], send_sems.at[3 * w + j], recv_sems.at[3 * w + j], (*other, c))
                cp.wait_send()
                cp.wait_recv()

    out = pl.pallas_call(
        body, name=name, out_shape=tuple(_like(a) for a in sums + lands), in_specs=[_HBM] * (2 * n) + [_SEM, _SEM, _ANY],
        out_specs=(_HBM,) * (2 * n), input_output_aliases={i: i for i in range(2 * n)}, compiler_params=_SIDE)(*sums, *lands, send, recv, after)
    chip = 2 * lax.axis_index("x") + lax.axis_index("y")
    return [lax.dynamic_update_index_in_dim(land, lax.dynamic_index_in_dim(s, chip, 0, keepdims=False), chip, 0)
            for s, land in zip(out[:n], out[n:])]


def ada_fwd(c, w_ada, b_ada3, conv_w):
    d, cs = w_ada.shape

    def body(c_ref, w_ref, b_ref, cw_ref, mod_ref, sc_ref, cwa_ref, part_ref, send_sems, recv_sems):
        me = _position()
        my = _index(me)
        cv = c_ref[...]
        sc_ref[my] = cv * _sigmoid(cv)
        cwa_ref[my] = cw_ref[...]
        gather = []
        for r in range(1, N_DEV):
            for k, ref in enumerate((sc_ref, cwa_ref)):
                cp = pltpu.make_async_remote_copy(src_ref=ref.at[my], dst_ref=ref.at[my], send_sem=send_sems.at[14 * k + r - 1],
                                                  recv_sem=recv_sems.at[14 * k + r - 1], device_id=_peer(me, r), device_id_type=MESH)
                cp.start()
                gather.append(cp)
        for cp in gather:
            cp.wait()
        sc_all = jnp.concatenate([sc_ref[k] for k in range(N_DEV)], axis=0).astype(BF16)
        part = jnp.dot(sc_all, w_ref[...].astype(BF16), preferred_element_type=F32)
        for k in range(N_DEV):
            part_ref[k] = part[k:k + 1, :]
        scatter = []
        for r in range(1, N_DEV):
            peer = _peer(me, r)
            cp = pltpu.make_async_remote_copy(src_ref=part_ref.at[_index(peer)], dst_ref=mod_ref.at[my], send_sem=send_sems.at[6 + r],
                                              recv_sem=recv_sems.at[6 + r], device_id=peer, device_id_type=MESH)
            cp.start()
            scatter.append(cp)
        mod_ref[my] = part_ref[my]
        for cp in scatter:
            cp.wait()
        mod_ref[...] = mod_ref[...] + b_ref[...]

    vm = pl.BlockSpec(memory_space=pltpu.VMEM)
    return pl.pallas_call(
        body, name="ada_fwd",
        out_shape=[jax.ShapeDtypeStruct((N_DEV, 1, cs), F32), jax.ShapeDtypeStruct((N_DEV, 1, d), F32),
                   jax.ShapeDtypeStruct((N_DEV,) + conv_w.shape, F32)],
        in_specs=[vm, vm, vm, vm], out_specs=[vm, vm, vm],
        scratch_shapes=[pltpu.VMEM((N_DEV, 1, cs), F32), pltpu.SemaphoreType.DMA((21,)), pltpu.SemaphoreType.DMA((21,))],
        compiler_params=pltpu.CompilerParams(vmem_limit_bytes=VMEM_LIMIT_BYTES))(c, w_ada, b_ada3, conv_w)


def ada_bwd_w(sc_all, dmod_cols):
    _, d = sc_all.shape
    cs = dmod_cols.shape[1]
    tr = _tile(d, ROW_TILE)

    def body(sc_ref, dm_ref, o_ref):
        dm = dm_ref[...].astype(BF16)
        o_ref[...] = lax.dot_general(sc_ref[...].astype(BF16), dm, (((0,), (0,)), ((), ())), preferred_element_type=F32)

    return pl.pallas_call(body, name="ada_bwd_w", grid=(d // tr,),
                          in_specs=[pl.BlockSpec((N_DEV, tr), lambda i: (0, i)), _full((N_DEV, cs))],
                          out_specs=pl.BlockSpec((None, tr, cs), lambda i: (0, i, 0)),
                          out_shape=jax.ShapeDtypeStruct((1, d, cs), F32), compiler_params=_params("parallel"))(sc_all, dmod_cols)


def _round_up(n, m):
    return (n + m - 1) // m * m


def kernel(x, c, positions, w_ada, b_ada, pre_norm1_g, w_in, gm_ln_g, gm_ln_b, gm_w_s, gm_b_s, w_branch_a, q_norm_g, w_uq, kv_norm_g, w_ukv, w_branch_b, w_out, post_norm1_g, pre_norm2_g, w_up, conv_w, conv_b, w_down, post_norm2_g, loss_target, m_w_ada, m_b_ada, m_pre_norm1_g, m_w_in, m_gm_ln_g, m_gm_ln_b, m_gm_w_s, m_gm_b_s, m_w_branch_a, m_q_norm_g, m_w_uq, m_kv_norm_g, m_w_ukv, m_w_branch_b, m_w_out, m_post_norm1_g, m_pre_norm2_g, m_w_up, m_conv_w, m_conv_b, m_w_down, m_post_norm2_g, v_w_ada, v_b_ada, v_pre_norm1_g, v_w_in, v_gm_ln_g, v_gm_ln_b, v_gm_w_s, v_gm_b_s, v_w_branch_a, v_q_norm_g, v_w_uq, v_kv_norm_g, v_w_ukv, v_w_branch_b, v_w_out, v_post_norm1_g, v_pre_norm2_g, v_w_up, v_conv_w, v_conv_b, v_w_down, v_post_norm2_g):
    weights = dict(w_ada=w_ada, b_ada=b_ada, pre_norm1_g=pre_norm1_g, w_in=w_in, gm_ln_g=gm_ln_g, gm_ln_b=gm_ln_b, gm_w_s=gm_w_s,
                   gm_b_s=gm_b_s, w_branch_a=w_branch_a, q_norm_g=q_norm_g, w_uq=w_uq, kv_norm_g=kv_norm_g, w_ukv=w_ukv,
                   w_branch_b=w_branch_b, w_out=w_out, post_norm1_g=post_norm1_g, pre_norm2_g=pre_norm2_g, w_up=w_up, conv_w=conv_w,
                   conv_b=conv_b, w_down=w_down, post_norm2_g=post_norm2_g)
    mom1 = dict(w_ada=m_w_ada, b_ada=m_b_ada, pre_norm1_g=m_pre_norm1_g, w_in=m_w_in, gm_ln_g=m_gm_ln_g, gm_ln_b=m_gm_ln_b,
                gm_w_s=m_gm_w_s, gm_b_s=m_gm_b_s, w_branch_a=m_w_branch_a, q_norm_g=m_q_norm_g, w_uq=m_w_uq, kv_norm_g=m_kv_norm_g,
                w_ukv=m_w_ukv, w_branch_b=m_w_branch_b, w_out=m_w_out, post_norm1_g=m_post_norm1_g, pre_norm2_g=m_pre_norm2_g,
                w_up=m_w_up, conv_w=m_conv_w, conv_b=m_conv_b, w_down=m_w_down, post_norm2_g=m_post_norm2_g)
    mom2 = dict(w_ada=v_w_ada, b_ada=v_b_ada, pre_norm1_g=v_pre_norm1_g, w_in=v_w_in, gm_ln_g=v_gm_ln_g, gm_ln_b=v_gm_ln_b,
                gm_w_s=v_gm_w_s, gm_b_s=v_gm_b_s, w_branch_a=v_w_branch_a, q_norm_g=v_q_norm_g, w_uq=v_w_uq, kv_norm_g=v_kv_norm_g,
                w_ukv=v_w_ukv, w_branch_b=v_w_branch_b, w_out=v_w_out, post_norm1_g=v_post_norm1_g, pre_norm2_g=v_pre_norm2_g,
                w_up=v_w_up, conv_w=v_conv_w, conv_b=v_conv_b, w_down=v_w_down, post_norm2_g=v_post_norm2_g)
    order = list(weights)
    _TOKENS.clear()

    s, d = x.shape[1], x.shape[2]
    gmw = gm_ln_g.shape[0]
    groups = gmw // CHUNK
    ql, kvl = q_norm_g.shape[0], kv_norm_g.shape[0]
    f2 = conv_b.shape[0]
    in_cols = w_in.shape[1] * N_DEV
    o_q, o_kv, o_ga, o_gb, o_kpe = 2 * gmw, 2 * gmw + ql, 2 * gmw + ql + kvl, 2 * gmw + ql + kvl + d, 2 * gmw + ql + kvl + 2 * d
    zp = _round_up(o_kpe + LANES, Z_PAD)
    src_kpe = 2 * gmw + ql + kvl
    assert src_kpe + QK_ROPE + 2 * d == in_cols
    my = 4 * lax.axis_index("x") + 2 * lax.axis_index("y") + lax.axis_index("c")

    x2, tgt = x[0], loss_target[0]
    row = lambda a: a.reshape(1, -1)

    big = ["w_in", "w_branch_a", "w_uq", "w_ukv", "w_branch_b", "w_out", "w_up", "w_down"]
    sh = {k: weights[k].astype(BF16) for k in big[1:]}
    mix = ["w_branch_a", "w_uq", "w_ukv", "w_branch_b", "w_out"]
    ag_in = ag_start("ag_start_in", [w_in.T.astype(BF16)], c)

    mod8, sc_all3, g_cw = ada_fwd(c, w_ada, b_ada.reshape(N_DEV, 1, -1), conv_w)
    mod = mod8.reshape(N_MOD, d)
    shift1, scale1, gate1, shift2, scale2, gate2 = (mod[i:i + 1] for i in range(N_MOD))
    sc_all = sc_all3.reshape(N_DEV, d)
    h1 = norm_mod_fwd("pre1_fwd", x2, row(pre_norm1_g), scale1, shift1)

    inv = ROPE_THETA ** (-jnp.arange(0, QK_ROPE, 2, dtype=F32) / QK_ROPE)
    ang = positions[0].astype(F32)[:, None] * inv
    cos4 = jnp.tile(jnp.cos(ang), (1, 4))
    sin4 = jnp.tile(jnp.concatenate([-jnp.sin(ang), jnp.sin(ang)], axis=1), (1, 2))

    wm = (gm_w_s * jnp.tril(jnp.ones((CHUNK, CHUNK), F32))).astype(BF16)
    bs3 = gm_b_s.reshape(groups, CHUNK, 1)
    ln_g, ln_b = row(gm_ln_g), row(gm_ln_b)

    small_names = ["pre_norm1_g", "gm_ln_g", "gm_ln_b", "gm_b_s", "q_norm_g", "kv_norm_g", "post_norm1_g", "pre_norm2_g", "conv_b",
                   "post_norm2_g", "gm_w_s", "b_ada"]
    n_small_early = sum(weights[k].size for k in small_names)
    n_pack_early = _round_up(n_small_early + 3 * f2, PACK_ALIGN)

    def pack(src):
        return jnp.concatenate([src[k].reshape(-1) for k in small_names] + [jnp.zeros((n_pack_early - n_small_early,), F32)]).reshape(-1, LANES)

    packed_state = [pack(weights), pack(mom1), pack(mom2)]

    early = [h1, cos4, sin4, wm] + [sh[k] for k in big[1:]] + packed_state
    ag_in = ag_forward("ag_forward_in", ag_in, early)
    ag_mix = ag_start("ag_start_mix", [sh[k] for k in mix], _TOKENS.pending[-1])
    (g_in,) = ag_wait("ag_wait_in", ag_in, [h1, _TOKENS.pending[-1]])
    cs_in = w_in.shape[1]

    def w_in_rows(lo, hi):
        return [g_in[k, max(lo - k * cs_in, 0):min(hi - k * cs_in, cs_in)] for k in range(N_DEV) if lo < (k + 1) * cs_in and hi > k * cs_in]

    w_in_p = jnp.concatenate(w_in_rows(0, src_kpe) + w_in_rows(src_kpe + QK_ROPE, in_cols) + w_in_rows(src_kpe, src_kpe + QK_ROPE)
                             + [jnp.zeros((zp - in_cols, d), BF16)], axis=0)

    z = mm_nt("z_proj", h1, w_in_p, ACT)
    ag_mix = ag_forward("ag_forward_mix", ag_mix, z)
    ag_up = ag_start("ag_start_up", [sh["w_up"]], _TOKENS.pending[-1])
    a = gmlp_fwd(z, gmw, ln_g, ln_b, wm, bs3)
    g_a, g_uq, g_ukv, g_b, g_out = ag_wait("ag_wait_mix", ag_mix, [a, _TOKENS.pending[-1]])
    w_a_f, w_b_f, w_out_f = g_a.reshape(-1, d), g_b.reshape(-1, d), g_out.reshape(-1, d)
    w_uq_f = g_uq.transpose(1, 0, 2).reshape(ql, HEADS, QK_NOPE + QK_ROPE)
    w_uq_n = w_uq_f[:, :, :QK_NOPE].reshape(ql, HEADS * QK_NOPE)
    w_uq_r = w_uq_f[:, :, QK_NOPE:].reshape(ql, HEADS * QK_ROPE)
    y_a = mm_nn("branch_a", a, w_a_f, ACT)
    qln = rms_fwd_cols("q_norm", z, o_q, ql, row(q_norm_g))
    kvn = rms_fwd_cols("kv_norm", z, o_kv, kvl, row(kv_norm_g))
    qn = mm_nn("q_nope", qln, w_uq_n, BF16)
    qp = mm_nn("q_rope", qln, w_uq_r, F32)
    kv = mm_nn_b3("kv_up", kvn, g_ukv, BF16)
    kpr = rope_k(z, o_kpe, cos4, sin4)
    o, qpr, lse = attn_fwd(qn, qp, kv, kpr, cos4, sin4)
    ag_up = ag_forward("ag_forward_up", ag_up, o)
    ag_down = ag_start("ag_start_down", [sh["w_down"]], _TOKENS.pending[-1])
    y_b = mm_nn("branch_b", o, w_b_f, ACT)
    merged = merge_fwd(z, o_ga, o_gb, y_a, y_b)
    y1 = mm_nn("out_proj", merged, w_out_f, ACT)
    x1 = post_res_fwd("post1_fwd", x2, y1, gate1, row(post_norm1_g))
    h2 = norm_mod_fwd("pre2_fwd", x1, row(pre_norm2_g), scale2, shift2)
    (g_up,) = ag_wait("ag_wait_up", ag_up, h2)
    upre = mm_nn_b3("up_proj", h2, g_up, ACT)
    ag_down = ag_forward("ag_forward_down", ag_down, upre)
    cw = g_cw.transpose(1, 0, 2).reshape(3, f2)
    cb = row(conv_b)
    f, gv = conv_act_fwd(upre, cw, cb)
    w_down_f = ag_wait("ag_wait_down", ag_down, f)[0].reshape(-1, d)
    ffn = mm_nn("down_proj", f, w_down_f, ACT)
    loss_acc, dout, dffn, acc2 = post2_loss_bwd(x1, ffn, tgt, gate2, row(post_norm2_g))
    loss = lax.psum(loss_acc[0, 0], ("x", "y", "c"))
    _TOKENS.push(jnp.broadcast_to(loss, (8, LANES)))

    blocks = lambda g: g.reshape(N_DEV, g.shape[0] // N_DEV, g.shape[1])
    core = lax.axis_index("c").astype(jnp.int32).reshape(1)
    rs = {}

    def rs_begin(key, grads):
        rs[key] = rs_d2d_start("rs_d2d_start_" + key, grads)

    def rs_middle(key, after):
        grads, lands = rs_d2d_wait("rs_d2d_wait_" + key, rs[key], after)
        sums = [pair_sum("pair_sum_%s_%d" % (key, i), g, l, core) for i, (g, l) in enumerate(zip(grads, lands))]
        rs[key] = rs_ici_start("rs_ici_start_" + key, sums)

    gw_down = mm_tn("g_w_down", f, dffn, BF16)
    rs_begin("down", [blocks(gw_down)])
    df = mm_nt("d_f", dffn, w_down_f, ACT)
    rs_middle("down", df)
    dupre, gcw_g, gcw_v, gcb_g, gcb_v = conv_act_bwd(upre, cw, gv, df)
    gw_up3 = mm_tn_h3("g_w_up", h2, dupre, N_DEV, BF16)
    rs_begin("up", [gw_up3])
    dh2 = mm_nt_h3("d_h2", dupre, g_up, ACT)
    rs_middle("up", dh2)
    dx1, dy1, acc_mid = mid_bwd(dh2, dout, x1, y1, row(pre_norm2_g), scale2, gate1, row(post_norm1_g))
    gw_out = mm_tn("g_w_out", merged, dy1, BF16)
    dmerged = mm_nt("d_merged", dy1, w_out_f, ACT)
    dya, dyb, dga, dgb = merge_bwd(z, o_ga, o_gb, y_a, y_b, dmerged)
    gw_a = mm_tn("g_w_a", a, dya, BF16)
    gw_b = mm_tn("g_w_b", o, dyb, BF16)
    rs_begin("mid", [blocks(gw_out), blocks(gw_a), blocks(gw_b)])
    da = mm_nt("d_a", dya, w_a_f, ACT)
    do = mm_nt("d_o", dyb, w_b_f, ACT)
    rs_middle("mid", do)
    duv, g_ws, g_bs3, acc_gm = gmlp_bwd(z, gmw, da, ln_g, ln_b, wm, bs3)
    dqn, dqp, dkv, dkp = attn_bwd(qn, qpr, kv, kpr, o, do, lse, cos4, sin4)
    dkpe = kpe_bwd(dkp, cos4, sin4, zp - o_kpe)
    dq_cat = jnp.concatenate([dqn, dqp], axis=1)
    w_uq_cat = jnp.concatenate([w_uq_n, w_uq_r], axis=1)
    gw_uq_cat = mm_tn("g_w_uq", qln, dq_cat, BF16)
    gw_uq_f = jnp.concatenate([gw_uq_cat[:, :HEADS * QK_NOPE].reshape(ql, HEADS, QK_NOPE),
                               gw_uq_cat[:, HEADS * QK_NOPE:].reshape(ql, HEADS, QK_ROPE)], axis=2)
    gw_uq3 = gw_uq_f.reshape(ql, N_DEV, -1).transpose(1, 0, 2)
    gw_ukv3 = mm_tn_o3("g_w_ukv", kvn, dkv, N_DEV, BF16)
    rs_begin("mla", [gw_uq3, gw_ukv3])
    dqln = mm_nt("d_qln", dq_cat, w_uq_cat, ACT)
    dq_lat, g_qnorm = rms_bwd_cols("q_norm_bwd", dqln, z, o_q, ql, row(q_norm_g))
    dkvn = mm_nt_b3("d_kvn", dkv, g_ukv, ACT)
    rs_middle("mla", dkvn)
    dkv_lat, g_kvnorm = rms_bwd_cols("kv_norm_bwd", dkvn, z, o_kv, kvl, row(kv_norm_g))
    dz = jnp.concatenate([duv, dq_lat, dkv_lat, dga, dgb, dkpe], axis=1)
    gw_in_p = mm_tn("g_w_in", dz, h1, BF16)

    def gw_in_rows(lo, hi):
        pieces = []
        for a, b, shift in ((0, src_kpe, 0), (src_kpe, src_kpe + QK_ROPE, o_kpe - src_kpe), (src_kpe + QK_ROPE, in_cols, -QK_ROPE)):
            if lo < b and hi > a:
                pieces.append(gw_in_p[max(lo, a) + shift:min(hi, b) + shift])
        return pieces[0] if len(pieces) == 1 else jnp.concatenate(pieces, axis=0)

    rs_begin("in", [jnp.stack([gw_in_rows(k * cs_in, (k + 1) * cs_in) for k in range(N_DEV)])])
    dh1 = mm_nn("d_h1", dz, w_in_p, ACT)
    grad_x, acc1 = pre1_bwd(dh1, dx1, x2, row(pre_norm1_g), scale1)

    dmod = jnp.concatenate([acc1[0], acc1[1], acc_mid[3], acc_mid[0], acc_mid[1], acc2[0]])
    small = [("pre_norm1_g", acc1[2]), ("gm_ln_g", acc_gm[0]), ("gm_ln_b", acc_gm[1]), ("gm_b_s", g_bs3.reshape(-1)),
             ("q_norm_g", g_qnorm[0]), ("kv_norm_g", g_kvnorm[0]), ("post_norm1_g", acc_mid[4]), ("pre_norm2_g", acc_mid[2]),
             ("conv_b", jnp.concatenate([gcb_g[0], gcb_v[0]])), ("post_norm2_g", acc2[1]), ("gm_w_s", g_ws.reshape(-1)),
             ("b_ada", dmod)]
    n_small = sum(v.shape[0] for _, v in small)
    n_cw = 3 * f2
    n_pack = _round_up(n_small + n_cw, PACK_ALIGN)
    tail = jnp.zeros((n_pack - n_small - n_cw,), F32)
    packed = jnp.concatenate([v for _, v in small] + [jnp.concatenate([gcw_g, gcw_v], axis=1).reshape(-1), tail])
    ag_small = ag_start("ag_start_small", [packed.reshape(-1, LANES)], packed)
    rs_middle("in", [packed, _TOKENS.pending[-1]])

    res = {}
    last = packed
    for key, names in (("down", ["w_down"]), ("up", ["w_up"]), ("mid", ["w_out", "w_branch_a", "w_branch_b"]), ("mla", ["w_uq", "w_ukv"])):
        parts = rs_ici_wait("rs_ici_wait_" + key, rs[key], last)
        for k, p in zip(names, parts):
            res[k] = adamw("adamw_" + k, weights[k], mom1[k], mom2[k], p)
            last = res[k][0]

    assert [k for k, _ in small] == small_names and n_small == n_small_early
    (gathered,) = ag_wait("ag_wait_small", ag_forward("ag_forward_small", ag_small, last), last)
    sm = [t.reshape(-1) for t in adamw("adamw_small", *packed_state, gathered)]
    off = 0
    for k, v in small:
        res[k] = tuple(t[off:off + v.shape[0]].reshape(weights[k].shape) for t in sm)
        off += v.shape[0]

    cs_cw = conv_w.shape[1]
    g_cw_full = sm[0][n_small:n_small + n_cw].reshape(3, f2)
    g_cw_mine = lax.dynamic_slice(g_cw_full, (0, my * cs_cw), (3, cs_cw))
    res["conv_w"] = adamw("adamw_conv_w", conv_w, mom1["conv_w"], mom2["conv_w"], g_cw_mine[None])

    cs_ada = w_ada.shape[1]
    off_b = n_small - N_MOD * d
    dmod_all = gathered.reshape(N_DEV, -1)[:, off_b:off_b + N_MOD * d]
    dmod_cols = lax.dynamic_slice(dmod_all, (0, my * cs_ada), (N_DEV, cs_ada))
    res["w_ada"] = adamw("adamw_w_ada", w_ada, mom1["w_ada"], mom2["w_ada"], ada_bwd_w(sc_all, dmod_cols))

    (p_in,) = rs_ici_wait("rs_ici_wait_in", rs["in"], res["w_ada"][0])
    res["w_in"] = tuple(t.T for t in adamw("adamw_w_in", w_in.T, mom1["w_in"].T, mom2["w_in"].T, p_in))

    _TOKENS.clear()
    outs = [loss, grad_x[None]]
    for i in range(4):
        outs += [res[k][i] for k in order]
    return tuple(outs)
```

```python
import jax
import jax.numpy as jnp
from jax import lax
from jax.experimental import pallas as pl
from jax.experimental.pallas import tpu as pltpu

F32 = jnp.float32
BF16 = jnp.bfloat16
ACT = BF16

N_DEV = 8
HEADS = 16
QK_NOPE = 128
QK_ROPE = 64
V_HEAD = 128
CHUNK = 128
ROPE_THETA = 10000.0
EPS = 1e-6
N_MOD = 6
ADAM_LR, ADAM_B1, ADAM_B2, ADAM_EPS, ADAM_WD, ADAM_STEP = 0.001, 0.9, 0.999, 1e-08, 0.01, 10

LANES = 128
VMEM_LIMIT_BYTES = 48 * 2 ** 20
ROW_TILE = 256
COL_TILE = 256
ATT_TILE = 512
Z_PAD = 512
ADAMW_TILE_ELEMS = 1 << 18
PACK_ALIGN = 8 * LANES
MESH = pl.DeviceIdType.MESH


def _params(*sem):
    return pltpu.CompilerParams(dimension_semantics=sem if sem else None, vmem_limit_bytes=VMEM_LIMIT_BYTES)


def _tile(dim, target):
    t = (min(dim, target) // LANES) * LANES
    while t >= LANES:
        if dim % t == 0:
            return t
        t -= LANES
    return dim


def _full(shape):
    nd = len(shape)
    return pl.BlockSpec(shape, lambda *_: (0,) * nd)


class _Tokens:
    KEEP = 2

    def __init__(self):
        self.pending = []

    def push(self, token):
        self.pending = (self.pending + [token])[-self.KEEP:]

    def take(self):
        return list(self.pending)

    def clear(self):
        self.pending = []


_TOKENS = _Tokens()


def _matmul(name, a, b, *, grid, a_spec, b_spec, o_spec, out_shape, contract, acc_shape, split=1):
    nk = grid[2]
    deps = _TOKENS.take()

    def product(a_ref, b_ref):
        if len(b_ref.shape) == 2:
            return lax.dot_general(a_ref[...].astype(BF16), b_ref[...].astype(BF16), (contract, ((), ())), preferred_element_type=F32)
        cs = b_ref.shape[2]
        return sum(lax.dot_general(a_ref[:, s * cs:(s + 1) * cs].astype(BF16), b_ref[s].astype(BF16), (contract, ((), ())),
                                   preferred_element_type=F32) for s in range(split))

    def body_one_step(a_ref, b_ref, *rest):
        o_ref = rest[len(deps)]
        o_ref[...] = product(a_ref, b_ref).astype(o_ref.dtype)

    def body(a_ref, b_ref, *rest):
        o_ref, acc_ref = rest[len(deps):]
        k = pl.program_id(2)

        @pl.when(k == 0)
        def _():
            acc_ref[...] = jnp.zeros_like(acc_ref)

        acc_ref[...] += product(a_ref, b_ref)

        @pl.when(k == nk - 1)
        def _():
            o_ref[...] = acc_ref[...].astype(o_ref.dtype)

    return pl.pallas_call(
        body_one_step if nk == 1 else body, name=name, grid=grid,
        in_specs=[a_spec, b_spec] + [pl.BlockSpec(memory_space=pl.ANY)] * len(deps),
        out_specs=o_spec, out_shape=out_shape, scratch_shapes=[] if nk == 1 else [pltpu.VMEM(acc_shape, F32)],
        compiler_params=_params("parallel", "parallel", "arbitrary"))(a, b, *deps)


T_OUT, T_OUT_WIDE, TK = 1024, 1408, 2816


def _out_tile(dim):
    return T_OUT_WIDE if dim % T_OUT_WIDE == 0 else _tile(dim, T_OUT)


def _tk(a, b):
    return TK if a.dtype == BF16 and b.dtype == BF16 else TK // 2


def mm_nn(name, a, b, dtype):
    (m, k), n = a.shape, b.shape[1]
    tm, tn, tk = _out_tile(m), _out_tile(n), _tile(k, _tk(a, b))
    return _matmul(name, a, b, grid=(m // tm, n // tn, k // tk),
                   a_spec=pl.BlockSpec((tm, tk), lambda i, j, kk: (i, kk)),
                   b_spec=pl.BlockSpec((tk, tn), lambda i, j, kk: (kk, j)),
                   o_spec=pl.BlockSpec((tm, tn), lambda i, j, kk: (i, j)),
                   out_shape=jax.ShapeDtypeStruct((m, n), dtype), contract=((1,), (0,)), acc_shape=(tm, tn))


def mm_nn_b3(name, a, b3, dtype):
    (m, k), (nj, _, cs) = a.shape, b3.shape
    tm, tk = _out_tile(m), _tile(k, _tk(a, b3))
    return _matmul(name, a, b3, grid=(m // tm, nj, k // tk),
                   a_spec=pl.BlockSpec((tm, tk), lambda i, j, kk: (i, kk)),
                   b_spec=pl.BlockSpec((None, tk, cs), lambda i, j, kk: (j, kk, 0)),
                   o_spec=pl.BlockSpec((tm, cs), lambda i, j, kk: (i, j)),
                   out_shape=jax.ShapeDtypeStruct((m, nj * cs), dtype), contract=((1,), (0,)), acc_shape=(tm, cs))


def mm_nt(name, a, b, dtype):
    (m, k), n = a.shape, b.shape[0]
    tm, tn, tk = _out_tile(m), _out_tile(n), _tile(k, _tk(a, b))
    return _matmul(name, a, b, grid=(m // tm, n // tn, k // tk),
                   a_spec=pl.BlockSpec((tm, tk), lambda i, j, kk: (i, kk)),
                   b_spec=pl.BlockSpec((tn, tk), lambda i, j, kk: (j, kk)),
                   o_spec=pl.BlockSpec((tm, tn), lambda i, j, kk: (i, j)),
                   out_shape=jax.ShapeDtypeStruct((m, n), dtype), contract=((1,), (1,)), acc_shape=(tm, tn))


def mm_nt_b3(name, a, b3, dtype):
    m, (nj, n, cs) = a.shape[0], b3.shape
    tm, tn = _out_tile(m), _out_tile(n)
    return _matmul(name, a, b3, grid=(m // tm, n // tn, nj),
                   a_spec=pl.BlockSpec((tm, cs), lambda i, j, kk: (i, kk)),
                   b_spec=pl.BlockSpec((None, tn, cs), lambda i, j, kk: (kk, j, 0)),
                   o_spec=pl.BlockSpec((tm, tn), lambda i, j, kk: (i, j)),
                   out_shape=jax.ShapeDtypeStruct((m, n), dtype), contract=((1,), (1,)), acc_shape=(tm, tn))


def mm_nt_h3(name, a3, b3, dtype):
    (_, m, _), (nj, n, cs) = a3.shape, b3.shape
    tm, tn, hj = _out_tile(m), _out_tile(n), nj // 2
    pair = 2 if hj % 2 == 0 else 1
    return _matmul(name, a3, b3.reshape(nj // pair, pair, n, cs), grid=(m // tm, n // tn, nj // pair),
                   a_spec=pl.BlockSpec((None, tm, pair * cs), lambda i, j, kk: (kk // (hj // pair), i, kk % (hj // pair))),
                   b_spec=pl.BlockSpec((None, pair, tn, cs), lambda i, j, kk: (kk, 0, j, 0)),
                   o_spec=pl.BlockSpec((tm, tn), lambda i, j, kk: (i, j)),
                   out_shape=jax.ShapeDtypeStruct((m, n), dtype), contract=((1,), (1,)), acc_shape=(tm, tn), split=pair)


def mm_tn_h3(name, a, b3, nj, dtype):
    (k, m), half = a.shape, b3.shape[2]
    hj = nj // 2
    cs = half // hj
    tm, tk = _out_tile(m), _tile(k, _tk(a, b3))
    return _matmul(name, a, b3, grid=(m // tm, nj, k // tk),
                   a_spec=pl.BlockSpec((tk, tm), lambda i, j, kk: (kk, i)),
                   b_spec=pl.BlockSpec((None, tk, cs), lambda i, j, kk: (j // hj, kk, j % hj)),
                   o_spec=pl.BlockSpec((None, tm, cs), lambda i, j, kk: (j, i, 0)),
                   out_shape=jax.ShapeDtypeStruct((nj, m, cs), dtype), contract=((0,), (0,)), acc_shape=(tm, cs))


def mm_tn(name, a, b, dtype):
    (k, m), n = a.shape, b.shape[1]
    tm, tn, tk = _out_tile(m), _out_tile(n), _tile(k, _tk(a, b))
    return _matmul(name, a, b, grid=(m // tm, n // tn, k // tk),
                   a_spec=pl.BlockSpec((tk, tm), lambda i, j, kk: (kk, i)),
                   b_spec=pl.BlockSpec((tk, tn), lambda i, j, kk: (kk, j)),
                   o_spec=pl.BlockSpec((tm, tn), lambda i, j, kk: (i, j)),
                   out_shape=jax.ShapeDtypeStruct((m, n), dtype), contract=((0,), (0,)), acc_shape=(tm, tn))


def mm_tn_o3(name, a, b, nj, dtype):
    (k, m), n = a.shape, b.shape[1]
    cs = n // nj
    tm, tk = _out_tile(m), _tile(k, _tk(a, b))
    return _matmul(name, a, b, grid=(m // tm, nj, k // tk),
                   a_spec=pl.BlockSpec((tk, tm), lambda i, j, kk: (kk, i)),
                   b_spec=pl.BlockSpec((tk, cs), lambda i, j, kk: (kk, j)),
                   o_spec=pl.BlockSpec((None, tm, cs), lambda i, j, kk: (j, i, 0)),
                   out_shape=jax.ShapeDtypeStruct((nj, m, cs), dtype), contract=((0,), (0,)), acc_shape=(tm, cs))


_GELU_C = 0.7978845608028654
_GELU_A = 0.044715


def _f32(ref):
    return ref[...].astype(F32)


def _gelu(x):
    x = x.astype(F32)
    return 0.5 * x * (1.0 + jnp.tanh(_GELU_C * (x + _GELU_A * x * x * x)))


def _gelu_and_grad(x):
    x = x.astype(F32)
    t = jnp.tanh(_GELU_C * (x + _GELU_A * x * x * x))
    y = 0.5 * x * (1.0 + t)
    dy = 0.5 * (1.0 + t) + 0.5 * x * (1.0 - t * t) * (_GELU_C * (1.0 + 3.0 * _GELU_A * x * x))
    return y, dy


def _sigmoid(x):
    return 0.5 * jnp.tanh(0.5 * x.astype(F32)) + 0.5


def _rms_stats(x):
    x = x.astype(F32)
    inv = lax.rsqrt(jnp.mean(x * x, axis=-1, keepdims=True) + EPS)
    return inv, x * inv


def _rms_bwd(dyhat, yhat, inv):
    return inv * (dyhat - yhat * jnp.mean(dyhat * yhat, axis=-1, keepdims=True))


def _colsum(x):
    return jnp.sum(x, axis=0, keepdims=True)


def _rope(x, cos4, sin4):
    lane = lax.broadcasted_iota(jnp.int32, x.shape, x.ndim - 1)
    first_half = (lane % QK_ROPE) < (QK_ROPE // 2)
    partner = jnp.where(first_half, pltpu.roll(x, LANES - QK_ROPE // 2, x.ndim - 1), pltpu.roll(x, QK_ROPE // 2, x.ndim - 1))
    return x * cos4 + partner * sin4


def norm_mod_fwd(name, x, g, scale, shift):
    s, d = x.shape
    tr = _tile(s, ROW_TILE)

    def body(x_ref, g_ref, sc_ref, sh_ref, o_ref):
        _, xh = _rms_stats(x_ref[...])
        o_ref[...] = (xh * g_ref[...] * (1.0 + sc_ref[...]) + sh_ref[...]).astype(o_ref.dtype)

    row = pl.BlockSpec((tr, d), lambda i: (i, 0))
    vec = pl.BlockSpec((1, d), lambda i: (0, 0))
    return pl.pallas_call(body, name=name, grid=(s // tr,), in_specs=[row, vec, vec, vec], out_specs=row,
                          out_shape=jax.ShapeDtypeStruct((s, d), BF16), compiler_params=_params("parallel"))(x, g, scale, shift)


def rms_fwd_cols(name, z, off, width, g):
    s = z.shape[0]
    tr = _tile(s, ROW_TILE)
    assert off % width == 0

    def body(x_ref, g_ref, o_ref):
        _, xh = _rms_stats(x_ref[...])
        o_ref[...] = (xh * g_ref[...]).astype(o_ref.dtype)

    return pl.pallas_call(body, name=name, grid=(s // tr,),
                          in_specs=[pl.BlockSpec((tr, width), lambda i: (i, off // width)), pl.BlockSpec((1, width), lambda i: (0, 0))],
                          out_specs=pl.BlockSpec((tr, width), lambda i: (i, 0)),
                          out_shape=jax.ShapeDtypeStruct((s, width), BF16), compiler_params=_params("parallel"))(z, g)


def rms_bwd_cols(name, dy, z, off, width, g):
    s = z.shape[0]
    tr = _tile(s, ROW_TILE)

    def body(dy_ref, x_ref, g_ref, dx_ref, gg_ref):
        @pl.when(pl.program_id(0) == 0)
        def _():
            gg_ref[...] = jnp.zeros_like(gg_ref)

        inv, xh = _rms_stats(x_ref[...])
        dy_v = _f32(dy_ref)
        gg_ref[...] += _colsum(dy_v * xh)
        dx_ref[...] = _rms_bwd(dy_v * g_ref[...], xh, inv).astype(dx_ref.dtype)

    return pl.pallas_call(body, name=name, grid=(s // tr,),
                          in_specs=[pl.BlockSpec((tr, width), lambda i: (i, 0)), pl.BlockSpec((tr, width), lambda i: (i, off // width)),
                                    pl.BlockSpec((1, width), lambda i: (0, 0))],
                          out_specs=[pl.BlockSpec((tr, width), lambda i: (i, 0)), pl.BlockSpec((1, width), lambda i: (0, 0))],
                          out_shape=[jax.ShapeDtypeStruct((s, width), BF16), jax.ShapeDtypeStruct((1, width), F32)],
                          compiler_params=_params("arbitrary"))(dy, z, g)


def post_res_fwd(name, x, y, gate, g):
    s, d = x.shape
    tr = _tile(s, ROW_TILE)

    def body(x_ref, y_ref, gate_ref, g_ref, o_ref):
        _, yh = _rms_stats(y_ref[...])
        o_ref[...] = x_ref[...] + gate_ref[...] * (yh * g_ref[...])

    row = pl.BlockSpec((tr, d), lambda i: (i, 0))
    vec = pl.BlockSpec((1, d), lambda i: (0, 0))
    return pl.pallas_call(body, name=name, grid=(s // tr,), in_specs=[row, row, vec, vec], out_specs=row,
                          out_shape=jax.ShapeDtypeStruct((s, d), F32), compiler_params=_params("parallel"))(x, y, gate, g)


def post2_loss_bwd(x1, ffn, target, gate2, g):
    s, d = x1.shape
    tr = _tile(s, ROW_TILE)

    def body(x_ref, y_ref, t_ref, gate_ref, g_ref, loss_ref, dout_ref, dy_ref, acc_ref):
        @pl.when(pl.program_id(0) == 0)
        def _():
            loss_ref[...] = jnp.zeros_like(loss_ref)
            acc_ref[...] = jnp.zeros_like(acc_ref)

        inv, yh = _rms_stats(y_ref[...])
        r = yh * g_ref[...]
        err = x_ref[...] + gate_ref[...] * r - t_ref[...]
        loss_ref[...] += 0.5 * jnp.sum(jnp.mean(err * err, axis=-1, keepdims=True))
        dout = err / d
        dout_ref[...] = dout
        dr = dout * gate_ref[...]
        acc_ref[0:1, :] += _colsum(dout * r)
        acc_ref[1:2, :] += _colsum(dr * yh)
        dy_ref[...] = _rms_bwd(dr * g_ref[...], yh, inv).astype(dy_ref.dtype)

    row = pl.BlockSpec((tr, d), lambda i: (i, 0))
    vec = pl.BlockSpec((1, d), lambda i: (0, 0))
    return pl.pallas_call(
        body, name="post2_loss_bwd", grid=(s // tr,), in_specs=[row, row, row, vec, vec],
        out_specs=[_full((8, LANES)), row, row, _full((8, d))],
        out_shape=[jax.ShapeDtypeStruct((8, LANES), F32), jax.ShapeDtypeStruct((s, d), F32),
                   jax.ShapeDtypeStruct((s, d), BF16), jax.ShapeDtypeStruct((8, d), F32)],
        compiler_params=_params("arbitrary"))(x1, ffn, target, gate2, g)


def mid_bwd(dh2, dout, x1, y1, pre2_g, scale2, gate1, post1_g):
    s, d = x1.shape
    tr = _tile(s, ROW_TILE)

    def body(dh_ref, dout_ref, x_ref, y_ref, g2_ref, sc_ref, gate_ref, g1_ref, dx_ref, dy_ref, acc_ref):
        @pl.when(pl.program_id(0) == 0)
        def _():
            acc_ref[...] = jnp.zeros_like(acc_ref)

        dh = _f32(dh_ref)
        inv2, xh = _rms_stats(x_ref[...])
        acc_ref[0:1, :] += _colsum(dh)
        acc_ref[1:2, :] += _colsum(dh * (xh * g2_ref[...]))
        t = dh * (1.0 + sc_ref[...])
        acc_ref[2:3, :] += _colsum(t * xh)
        dx1 = dout_ref[...] + _rms_bwd(t * g2_ref[...], xh, inv2)
        dx_ref[...] = dx1
        inv1, yh = _rms_stats(y_ref[...])
        acc_ref[3:4, :] += _colsum(dx1 * (yh * g1_ref[...]))
        dr = dx1 * gate_ref[...]
        acc_ref[4:5, :] += _colsum(dr * yh)
        dy_ref[...] = _rms_bwd(dr * g1_ref[...], yh, inv1).astype(dy_ref.dtype)

    row = pl.BlockSpec((tr, d), lambda i: (i, 0))
    vec = pl.BlockSpec((1, d), lambda i: (0, 0))
    return pl.pallas_call(
        body, name="mid_bwd", grid=(s // tr,), in_specs=[row, row, row, row, vec, vec, vec, vec],
        out_specs=[row, row, _full((8, d))],
        out_shape=[jax.ShapeDtypeStruct((s, d), F32), jax.ShapeDtypeStruct((s, d), BF16), jax.ShapeDtypeStruct((8, d), F32)],
        compiler_params=_params("arbitrary"))(dh2, dout, x1, y1, pre2_g, scale2, gate1, post1_g)


def pre1_bwd(dh1, dx1, x, pre1_g, scale1):
    s, d = x.shape
    tr = _tile(s, ROW_TILE)

    def body(dh_ref, dx1_ref, x_ref, g_ref, sc_ref, dx_ref, acc_ref):
        @pl.when(pl.program_id(0) == 0)
        def _():
            acc_ref[...] = jnp.zeros_like(acc_ref)

        dh = _f32(dh_ref)
        inv, xh = _rms_stats(x_ref[...])
        acc_ref[0:1, :] += _colsum(dh)
        acc_ref[1:2, :] += _colsum(dh * (xh * g_ref[...]))
        t = dh * (1.0 + sc_ref[...])
        acc_ref[2:3, :] += _colsum(t * xh)
        dx_ref[...] = dx1_ref[...] + _rms_bwd(t * g_ref[...], xh, inv)

    row = pl.BlockSpec((tr, d), lambda i: (i, 0))
    vec = pl.BlockSpec((1, d), lambda i: (0, 0))
    return pl.pallas_call(
        body, name="pre1_bwd", grid=(s // tr,), in_specs=[row, row, row, vec, vec], out_specs=[row, _full((8, d))],
        out_shape=[jax.ShapeDtypeStruct((s, d), F32), jax.ShapeDtypeStruct((8, d), F32)],
        compiler_params=_params("arbitrary"))(dh1, dx1, x, pre1_g, scale1)


def _ln_stats(v):
    mu = jnp.mean(v, axis=-1, keepdims=True)
    vc = v - mu
    rstd = lax.rsqrt(jnp.mean(vc * vc, axis=-1, keepdims=True) + EPS)
    return rstd, vc * rstd


def gmlp_fwd(z, width, ln_g, ln_b, wm, bs3):
    s = z.shape[0]
    groups = width // CHUNK

    def body(u_ref, v_ref, g_ref, b_ref, wm_ref, bs_ref, a_ref):
        ug = _gelu(u_ref[...])
        _, vh = _ln_stats(_gelu(v_ref[...]))
        vn = (vh * g_ref[...] + b_ref[...]).astype(BF16)
        for g in range(groups):
            cols = slice(g * CHUNK, (g + 1) * CHUNK)
            mixed = jnp.dot(wm_ref[g], vn[:, cols], preferred_element_type=F32) + bs_ref[g]
            a_ref[:, cols] = (ug[:, cols] * mixed).astype(a_ref.dtype)

    vec = pl.BlockSpec((1, width), lambda n: (0, 0))
    return pl.pallas_call(
        body, name="gmlp_fwd", grid=(s // CHUNK,),
        in_specs=[pl.BlockSpec((CHUNK, width), lambda n: (n, 0)), pl.BlockSpec((CHUNK, width), lambda n: (n, 1)), vec, vec,
                  _full(wm.shape), _full(bs3.shape)],
        out_specs=pl.BlockSpec((CHUNK, width), lambda n: (n, 0)),
        out_shape=jax.ShapeDtypeStruct((s, width), BF16), compiler_params=_params("parallel"))(z, z, ln_g, ln_b, wm, bs3)


def gmlp_bwd(z, width, da, ln_g, ln_b, wm, bs3):
    s = z.shape[0]
    groups = width // CHUNK

    def body(u_ref, v_ref, da_ref, g_ref, b_ref, wm_ref, bs_ref, duv_ref, gw_ref, gb_ref, acc_ref, dvn_ref):
        @pl.when(pl.program_id(0) == 0)
        def _():
            gw_ref[...] = jnp.zeros_like(gw_ref)
            gb_ref[...] = jnp.zeros_like(gb_ref)
            acc_ref[...] = jnp.zeros_like(acc_ref)

        ug, dug = _gelu_and_grad(u_ref[...])
        vg, dvg = _gelu_and_grad(v_ref[...])
        rstd, vh = _ln_stats(vg)
        vn = (vh * g_ref[...] + b_ref[...]).astype(BF16)
        da_v = _f32(da_ref)
        for g in range(groups):
            cols = slice(g * CHUNK, (g + 1) * CHUNK)
            mixed = jnp.dot(wm_ref[g], vn[:, cols], preferred_element_type=F32) + bs_ref[g]
            duv_ref[:, cols] = (da_v[:, cols] * mixed * dug[:, cols]).astype(duv_ref.dtype)
            dm = da_v[:, cols] * ug[:, cols]
            gb_ref[g] += jnp.sum(dm, axis=-1, keepdims=True)
            dmb = dm.astype(BF16)
            gw_ref[g] += lax.dot_general(dmb, vn[:, cols], (((1,), (1,)), ((), ())), preferred_element_type=F32)
            dvn_ref[:, cols] = lax.dot_general(wm_ref[g], dmb, (((0,), (0,)), ((), ())), preferred_element_type=F32)
        dvn = dvn_ref[...]
        acc_ref[0:1, :] += _colsum(dvn * vh)
        acc_ref[1:2, :] += _colsum(dvn)
        dvh = dvn * g_ref[...]
        dv = rstd * (dvh - jnp.mean(dvh, axis=-1, keepdims=True) - vh * jnp.mean(dvh * vh, axis=-1, keepdims=True))
        duv_ref[:, width:] = (dv * dvg).astype(duv_ref.dtype)

        @pl.when(pl.program_id(0) == pl.num_programs(0) - 1)
        def _():
            q = lax.broadcasted_iota(jnp.int32, gw_ref.shape, 1)
            p = lax.broadcasted_iota(jnp.int32, gw_ref.shape, 2)
            gw_ref[...] = jnp.where(p <= q, gw_ref[...], 0.0)

    vec = pl.BlockSpec((1, width), lambda n: (0, 0))
    blk = pl.BlockSpec((CHUNK, width), lambda n: (n, 0))
    return pl.pallas_call(
        body, name="gmlp_bwd", grid=(s // CHUNK,),
        in_specs=[blk, pl.BlockSpec((CHUNK, width), lambda n: (n, 1)), blk, vec, vec, _full(wm.shape), _full(bs3.shape)],
        out_specs=[pl.BlockSpec((CHUNK, 2 * width), lambda n: (n, 0)), _full(wm.shape), _full(bs3.shape), _full((8, width))],
        out_shape=[jax.ShapeDtypeStruct((s, 2 * width), BF16), jax.ShapeDtypeStruct(wm.shape, F32),
                   jax.ShapeDtypeStruct(bs3.shape, F32), jax.ShapeDtypeStruct((8, width), F32)],
        scratch_shapes=[pltpu.VMEM((CHUNK, width), F32)],
        compiler_params=_params("arbitrary"))(z, z, da, ln_g, ln_b, wm, bs3)


def merge_fwd(z, off_a, off_b, ya, yb):
    s, d = ya.shape
    tr, tc = _tile(s, ROW_TILE * 2), _tile(d, COL_TILE)
    assert off_a % tc == 0 and off_b % tc == 0

    def body(ga_ref, gb_ref, ya_ref, yb_ref, o_ref):
        o_ref[...] = (_sigmoid(ga_ref[...]) * _f32(ya_ref) + _sigmoid(gb_ref[...]) * _f32(yb_ref)).astype(o_ref.dtype)

    blk = pl.BlockSpec((tr, tc), lambda i, j: (i, j))
    return pl.pallas_call(
        body, name="merge_fwd", grid=(s // tr, d // tc),
        in_specs=[pl.BlockSpec((tr, tc), lambda i, j: (i, off_a // tc + j)), pl.BlockSpec((tr, tc), lambda i, j: (i, off_b // tc + j)), blk, blk],
        out_specs=blk, out_shape=jax.ShapeDtypeStruct((s, d), BF16), compiler_params=_params("parallel", "parallel"))(z, z, ya, yb)


def merge_bwd(z, off_a, off_b, ya, yb, dm):
    s, d = ya.shape
    tr, tc = _tile(s, ROW_TILE * 2), _tile(d, COL_TILE)
    nc = d // tc

    def body(ga_ref, gb_ref, ya_ref, yb_ref, dm_ref, dya_ref, dyb_ref, dga_ref, dgb_ref):
        dm_v = _f32(dm_ref)
        sa, sb = _sigmoid(ga_ref[...]), _sigmoid(gb_ref[...])
        dya_ref[...] = (dm_v * sa).astype(dya_ref.dtype)
        dyb_ref[...] = (dm_v * sb).astype(dyb_ref.dtype)
        dga_ref[...] = (dm_v * _f32(ya_ref) * sa * (1.0 - sa)).astype(dga_ref.dtype)
        dgb_ref[...] = (dm_v * _f32(yb_ref) * sb * (1.0 - sb)).astype(dgb_ref.dtype)

    blk = pl.BlockSpec((tr, tc), lambda i, j: (i, j))
    out = jax.ShapeDtypeStruct((s, d), BF16)
    return pl.pallas_call(
        body, name="merge_bwd", grid=(s // tr, nc),
        in_specs=[pl.BlockSpec((tr, tc), lambda i, j: (i, off_a // tc + j)), pl.BlockSpec((tr, tc), lambda i, j: (i, off_b // tc + j)), blk, blk, blk],
        out_specs=[blk, blk, blk, blk], out_shape=[out, out, out, out],
        compiler_params=_params("parallel", "parallel"))(z, z, ya, yb, dm)


_ATT_SCALE = (QK_NOPE + QK_ROPE) ** -0.5
_NEG = -1e30


def rope_k(z, off, cos4, sin4):
    s = z.shape[0]
    tr = _tile(s, ROW_TILE * 2)
    assert off % LANES == 0

    def body(k_ref, c_ref, s_ref, o_ref):
        k = _f32(k_ref)
        k = k + pltpu.roll(k, QK_ROPE, 1)
        o_ref[...] = _rope(k, c_ref[...], s_ref[...]).astype(o_ref.dtype)

    row = pl.BlockSpec((tr, LANES), lambda i: (i, 0))
    return pl.pallas_call(body, name="rope_k", grid=(s // tr,),
                          in_specs=[pl.BlockSpec((tr, LANES), lambda i: (i, off // LANES)), row, row], out_specs=row,
                          out_shape=jax.ShapeDtypeStruct((s, LANES), BF16), compiler_params=_params("parallel"))(z, cos4, sin4)


def _dot_nt(a, b):
    return lax.dot_general(a, b, (((1,), (1,)), ((), ())), preferred_element_type=F32)


def _dot_tn(a, b):
    return lax.dot_general(a, b, (((0,), (0,)), ((), ())), preferred_element_type=F32)


def _q_cat(q_n, qpr, hh):
    lane = lax.broadcasted_iota(jnp.int32, qpr.shape, 1)
    sel = (lane < QK_ROPE) if hh == 0 else (lane >= QK_ROPE)
    return jnp.concatenate([q_n, jnp.where(sel, qpr, jnp.zeros_like(qpr))], axis=1)


def _causal(sc):
    row = lax.broadcasted_iota(jnp.int32, sc.shape, 0)
    col = lax.broadcasted_iota(jnp.int32, sc.shape, 1)
    return jnp.where(col <= row, sc, _NEG)


def attn_fwd(qn, qp, kv, kpr, cos4, sin4):
    s = qn.shape[0]
    hp = HEADS // 2
    t = _tile(s, ATT_TILE)
    nq = s // t

    def body(qn_ref, qp_ref, kv_ref, kp_ref, c_ref, s_ref, o_ref, qpr_ref, l_ref, kcat_ref):
        qi = pl.program_id(1)

        @pl.when(qi == 0)
        def _():
            for hh in range(2):
                kcat_ref[hh, :, 0:QK_NOPE] = kv_ref[:, 2 * hh * QK_NOPE:(2 * hh + 1) * QK_NOPE]
                kcat_ref[hh, :, QK_NOPE:] = kp_ref[...]

        qpr = _rope(qp_ref[...], c_ref[...], s_ref[...]).astype(BF16)
        qpr_ref[...] = qpr
        qcat = [_q_cat(qn_ref[:, hh * QK_NOPE:(hh + 1) * QK_NOPE], qpr, hh) for hh in range(2)]

        def block(kb, carry, diagonal):
            rows = pl.ds(pl.multiple_of(kb * t, t), t)
            out = []
            for hh in range(2):
                m, l, acc = carry[hh]
                sc = _dot_nt(qcat[hh], kcat_ref[hh, rows, :]) * _ATT_SCALE
                if diagonal:
                    sc = _causal(sc)
                m_new = jnp.maximum(m, jnp.max(sc, axis=-1, keepdims=True))
                alpha = jnp.exp(m - m_new)
                p = jnp.exp(sc - m_new)
                l = alpha * l + jnp.sum(p, axis=-1, keepdims=True)
                v = kv_ref[rows, (2 * hh + 1) * QK_NOPE:(2 * hh + 2) * QK_NOPE]
                acc = alpha * acc + jnp.dot(p.astype(BF16), v, preferred_element_type=F32)
                out.append((m_new, l, acc))
            return tuple(out)

        one = (jnp.full((t, 1), _NEG, F32), jnp.zeros((t, 1), F32), jnp.zeros((t, V_HEAD), F32))
        carry = lax.fori_loop(0, qi, lambda kb, cr: block(kb, cr, False), (one, one))
        carry = block(qi, carry, True)
        for hh in range(2):
            m, l, acc = carry[hh]
            o_ref[:, hh * V_HEAD:(hh + 1) * V_HEAD] = (acc / l).astype(o_ref.dtype)
            l_ref[:, hh:hh + 1] = m + jnp.log(l)

    return pl.pallas_call(
        body, name="attn_fwd", grid=(hp, nq),
        in_specs=[pl.BlockSpec((t, 2 * QK_NOPE), lambda h, i: (i, h)), pl.BlockSpec((t, LANES), lambda h, i: (i, h)),
                  pl.BlockSpec((s, 4 * QK_NOPE), lambda h, i: (0, h)), _full((s, LANES)),
                  pl.BlockSpec((t, LANES), lambda h, i: (i, 0)), pl.BlockSpec((t, LANES), lambda h, i: (i, 0))],
        out_specs=[pl.BlockSpec((t, 2 * V_HEAD), lambda h, i: (i, h)), pl.BlockSpec((t, LANES), lambda h, i: (i, h)),
                   pl.BlockSpec((None, t, 2), lambda h, i: (h, i, 0))],
        out_shape=[jax.ShapeDtypeStruct((s, HEADS * V_HEAD), ACT), jax.ShapeDtypeStruct((s, HEADS * QK_ROPE), BF16),
                   jax.ShapeDtypeStruct((hp, s, 2), F32)],
        scratch_shapes=[pltpu.VMEM((2, s, 2 * QK_NOPE), BF16)],
        compiler_params=_params("parallel", "arbitrary"))(qn, qp, kv, kpr, cos4, sin4)


def attn_bwd(qn, qpr, kv, kpr, o, do, lse, cos4, sin4):
    s = qn.shape[0]
    hp = HEADS // 2
    t = _tile(s, ATT_TILE)
    nk = s // t

    def body(qn_ref, qpr_ref, kv_ref, kp_ref, o_ref, do_ref, l_ref, c_ref, s_ref,
             dqn_ref, dqp_ref, dkv_ref, dkp_ref, qcat_ref, dq_ref, delta_ref):
        ki = pl.program_id(1)

        @pl.when(ki == 0)
        def _():
            dq_ref[...] = jnp.zeros_like(dq_ref)
            for hh in range(2):
                qcat_ref[hh] = _q_cat(qn_ref[:, hh * QK_NOPE:(hh + 1) * QK_NOPE], qpr_ref[...], hh)
                cols = slice(hh * V_HEAD, (hh + 1) * V_HEAD)
                delta_ref[hh] = jnp.sum(do_ref[:, cols].astype(F32) * o_ref[:, cols].astype(F32), axis=-1, keepdims=True)

        rows_k = pl.ds(pl.multiple_of(ki * t, t), t)
        kcat = [jnp.concatenate([kv_ref[rows_k, 2 * hh * QK_NOPE:(2 * hh + 1) * QK_NOPE], kp_ref[rows_k, :]], axis=1) for hh in range(2)]
        vs = [kv_ref[rows_k, (2 * hh + 1) * QK_NOPE:(2 * hh + 2) * QK_NOPE] for hh in range(2)]

        def block(qb, carry, diagonal):
            rows = pl.ds(pl.multiple_of(qb * t, t), t)
            out = []
            for hh in range(2):
                dkc, dv = carry[hh]
                q_c = qcat_ref[hh, rows, :]
                do_b = do_ref[rows, hh * V_HEAD:(hh + 1) * V_HEAD].astype(BF16)
                sc = _dot_nt(q_c, kcat[hh]) * _ATT_SCALE
                if diagonal:
                    sc = _causal(sc)
                p = jnp.exp(sc - l_ref[rows, hh:hh + 1])
                dpv = _dot_nt(do_b, vs[hh])
                ds = (p * (dpv - delta_ref[hh, rows, :]) * _ATT_SCALE).astype(BF16)
                dv = dv + _dot_tn(p.astype(BF16), do_b)
                dkc = dkc + _dot_tn(ds, q_c)
                dq_ref[hh, rows, :] += jnp.dot(ds, kcat[hh], preferred_element_type=F32)
                out.append((dkc, dv))
            return tuple(out)

        one = (jnp.zeros((t, 2 * QK_NOPE), F32), jnp.zeros((t, V_HEAD), F32))
        carry = block(ki, (one, one), True)
        carry = lax.fori_loop(ki + 1, nk, lambda qb, cr: block(qb, cr, False), carry)
        dkp = jnp.zeros((t, LANES), F32)
        for hh in range(2):
            dkc, dv = carry[hh]
            dkv_ref[:, 2 * hh * QK_NOPE:(2 * hh + 1) * QK_NOPE] = dkc[:, :QK_NOPE].astype(dkv_ref.dtype)
            dkv_ref[:, (2 * hh + 1) * QK_NOPE:(2 * hh + 2) * QK_NOPE] = dv.astype(dkv_ref.dtype)
            dkp = dkp + dkc[:, QK_NOPE:]
        dkp_ref[...] = dkp

        @pl.when(ki == nk - 1)
        def _():
            lane = lax.broadcasted_iota(jnp.int32, (s, LANES), 1)
            dqp = jnp.where(lane < QK_ROPE, dq_ref[0, :, QK_NOPE:], dq_ref[1, :, QK_NOPE:])
            dqp_ref[...] = _rope(dqp, c_ref[...], -s_ref[...]).astype(dqp_ref.dtype)
            for hh in range(2):
                dqn_ref[:, hh * QK_NOPE:(hh + 1) * QK_NOPE] = dq_ref[hh, :, :QK_NOPE].astype(dqn_ref.dtype)

    qblk = pl.BlockSpec((s, 2 * QK_NOPE), lambda h, i: (0, h))
    pblk = pl.BlockSpec((s, LANES), lambda h, i: (0, h))
    tab = _full((s, LANES))
    return pl.pallas_call(
        body, name="attn_bwd", grid=(hp, nk),
        in_specs=[qblk, pblk, pl.BlockSpec((s, 4 * QK_NOPE), lambda h, i: (0, h)), tab, qblk, qblk,
                  pl.BlockSpec((None, s, 2), lambda h, i: (h, 0, 0)), tab, tab],
        out_specs=[qblk, pblk, pl.BlockSpec((t, 4 * QK_NOPE), lambda h, i: (i, h)), pl.BlockSpec((None, t, LANES), lambda h, i: (h, i, 0))],
        out_shape=[jax.ShapeDtypeStruct((s, HEADS * QK_NOPE), BF16), jax.ShapeDtypeStruct((s, HEADS * QK_ROPE), BF16),
                   jax.ShapeDtypeStruct((s, HEADS * 2 * QK_NOPE), BF16), jax.ShapeDtypeStruct((hp, s, LANES), F32)],
        scratch_shapes=[pltpu.VMEM((2, s, 2 * QK_NOPE), BF16), pltpu.VMEM((2, s, 2 * QK_NOPE), F32), pltpu.VMEM((2, s, 1), F32)],
        compiler_params=_params("parallel", "arbitrary"))(qn, qpr, kv, kpr, o, do, lse, cos4, sin4)


def kpe_bwd(dkp, cos4, sin4, pad_cols):
    hp, s, _ = dkp.shape
    tr = _tile(s, ROW_TILE * 2)

    def body(d_ref, c_ref, s_ref, o_ref):
        tot = d_ref[0]
        for h in range(1, hp):
            tot = tot + d_ref[h]
        tot = tot + pltpu.roll(tot, QK_ROPE, 1)
        lane = lax.broadcasted_iota(jnp.int32, tot.shape, 1)
        dk = jnp.where(lane < QK_ROPE, _rope(tot, c_ref[...], -s_ref[...]), jnp.zeros_like(tot))
        o_ref[...] = jnp.zeros_like(o_ref)
        o_ref[:, 0:LANES] = dk.astype(o_ref.dtype)

    row = pl.BlockSpec((tr, LANES), lambda i: (i, 0))
    return pl.pallas_call(body, name="kpe_bwd", grid=(s // tr,),
                          in_specs=[pl.BlockSpec((hp, tr, LANES), lambda i: (0, i, 0)), row, row],
                          out_specs=pl.BlockSpec((tr, pad_cols), lambda i: (i, 0)),
                          out_shape=jax.ShapeDtypeStruct((s, pad_cols), BF16), compiler_params=_params("parallel"))(dkp, cos4, sin4)


def _shift_down(x, n):
    row = lax.broadcasted_iota(jnp.int32, x.shape, 0)
    return jnp.where(row >= n, pltpu.roll(x, n, 0), jnp.zeros_like(x))


def _shift_up(x, n):
    rows = x.shape[0]
    row = lax.broadcasted_iota(jnp.int32, x.shape, 0)
    return jnp.where(row < rows - n, pltpu.roll(x, rows - n, 0), jnp.zeros_like(x))


def _conv(x, w_ref, b_ref):
    return w_ref[2:3, :] * x + w_ref[1:2, :] * _shift_down(x, 1) + w_ref[0:1, :] * _shift_down(x, 2) + b_ref[...]


def conv_act_fwd(upre, conv_w, conv_b):
    s, f2 = upre.shape
    f = f2 // 2
    tc = _tile(f, COL_TILE)
    nc = f // tc

    def body(ug_ref, uv_ref, wg_ref, wv_ref, bg_ref, bv_ref, o_ref, gv_ref):
        gh = _conv(_f32(ug_ref), wg_ref, bg_ref)
        vh = _conv(_f32(uv_ref), wv_ref, bv_ref)
        o_ref[...] = (gh * _sigmoid(gh) * vh).astype(o_ref.dtype)
        gv_ref[0] = gh.astype(gv_ref.dtype)
        gv_ref[1] = vh.astype(gv_ref.dtype)

    def spec(rows, shift):
        return pl.BlockSpec((rows, tc), lambda j: (0, j + shift))

    return pl.pallas_call(
        body, name="conv_act_fwd", grid=(nc,),
        in_specs=[spec(s, 0), spec(s, nc), spec(3, 0), spec(3, nc), spec(1, 0), spec(1, nc)],
        out_specs=[spec(s, 0), pl.BlockSpec((2, s, tc), lambda j: (0, 0, j))],
        out_shape=[jax.ShapeDtypeStruct((s, f), BF16), jax.ShapeDtypeStruct((2, s, f), ACT)],
        compiler_params=_params("parallel"))(upre, upre, conv_w, conv_w, conv_b, conv_b)


def conv_act_bwd(upre, conv_w, gv, df):
    s, f2 = upre.shape
    f = f2 // 2
    tc = _tile(f, COL_TILE)
    nc = f // tc

    def half(x, d, w_ref, du_ref, which, gw_ref, gb_ref):
        d1, d2 = _shift_up(d, 1), _shift_up(d, 2)
        gb_ref[...] = _colsum(d)
        gw_ref[2:3, :] = _colsum(d * x)
        gw_ref[1:2, :] = _colsum(d1 * x)
        gw_ref[0:1, :] = _colsum(d2 * x)
        du_ref[which] = (w_ref[2:3, :] * d + w_ref[1:2, :] * d1 + w_ref[0:1, :] * d2).astype(du_ref.dtype)

    def body(ug_ref, uv_ref, wg_ref, wv_ref, gv_ref, df_ref, du_ref, gwg_ref, gwv_ref, gbg_ref, gbv_ref):
        xg, xv = _f32(ug_ref), _f32(uv_ref)
        gh, vh = gv_ref[0].astype(F32), gv_ref[1].astype(F32)
        sg = _sigmoid(gh)
        df_v = _f32(df_ref)
        half(xg, df_v * vh * (sg * (1.0 + gh * (1.0 - sg))), wg_ref, du_ref, 0, gwg_ref, gbg_ref)
        half(xv, df_v * (gh * sg), wv_ref, du_ref, 1, gwv_ref, gbv_ref)

    def spec(rows, shift):
        return pl.BlockSpec((rows, tc), lambda j: (0, j + shift))

    gw = jax.ShapeDtypeStruct((3, f), F32)
    gb = jax.ShapeDtypeStruct((1, f), F32)
    return pl.pallas_call(
        body, name="conv_act_bwd", grid=(nc,),
        in_specs=[spec(s, 0), spec(s, nc), spec(3, 0), spec(3, nc), pl.BlockSpec((2, s, tc), lambda j: (0, 0, j)), spec(s, 0)],
        out_specs=[pl.BlockSpec((2, s, tc), lambda j: (0, 0, j)), spec(3, 0), spec(3, 0), spec(1, 0), spec(1, 0)],
        out_shape=[jax.ShapeDtypeStruct((2, s, f), BF16), gw, gw, gb, gb],
        compiler_params=_params("parallel"))(upre, upre, conv_w, conv_w, gv, df)


def _elementwise_tile(r, c, limit):
    if r % 8:
        return r, c
    best = (8, c if c % LANES else LANES)
    for k in (1, 2, 4, 8, 16):
        if k > 1 and c % (LANES * k):
            continue
        tc = c // k
        tr = max(8, min(r, limit // tc) // 8 * 8)
        while r % tr:
            tr -= 8
        if tr * tc <= max(limit, 8 * tc) and tr * tc > best[0] * best[1]:
            best = (tr, tc)
    return best


def adamw(name, w, m, v, parts):
    npart, r, c = parts.shape
    tr, tc = _elementwise_tile(r, c, ADAMW_TILE_ELEMS)
    bc1 = 1.0 - ADAM_B1 ** ADAM_STEP
    bc2 = 1.0 - ADAM_B2 ** ADAM_STEP

    def body(w_ref, m_ref, v_ref, p_ref, g_ref, d_ref, nm_ref, nv_ref):
        g = p_ref[0].astype(F32)
        for k in range(1, npart):
            g = g + p_ref[k].astype(F32)
        m_new = ADAM_B1 * m_ref[...] + (1.0 - ADAM_B1) * g
        v_new = ADAM_B2 * v_ref[...] + (1.0 - ADAM_B2) * (g * g)
        g_ref[...] = g
        nm_ref[...] = m_new
        nv_ref[...] = v_new
        d_ref[...] = -ADAM_LR * ((m_new / bc1) / (jnp.sqrt(v_new / bc2) + ADAM_EPS) + ADAM_WD * w_ref[...])

    deps = _TOKENS.take()
    blk = pl.BlockSpec((tr, tc), lambda i, j: (i, j))
    out = jax.ShapeDtypeStruct((r, c), F32)
    return pl.pallas_call(
        lambda *refs: body(*refs[:4], *refs[4 + len(deps):]), name=name, grid=(r // tr, c // tc),
        in_specs=[blk, blk, blk, pl.BlockSpec((npart, tr, tc), lambda i, j: (0, i, j))] + [pl.BlockSpec(memory_space=pl.ANY)] * len(deps),
        out_specs=[blk, blk, blk, blk], out_shape=[out, out, out, out],
        compiler_params=_params("parallel", "parallel"))(w, m, v, parts, *deps)


def _position():
    return lax.axis_index("x"), lax.axis_index("y"), lax.axis_index("c")


def _index(p):
    return 4 * p[0] + 2 * p[1] + p[2]


def _peer(me, r):
    return (me[0] ^ ((r >> 2) & 1), me[1] ^ ((r >> 1) & 1), me[2] ^ (r & 1))


_ANY = pl.BlockSpec(memory_space=pl.ANY)


_HBM = pl.BlockSpec(memory_space=pltpu.HBM)
_SEM = pl.BlockSpec(memory_space=pltpu.SEMAPHORE)
_EFFECT = pltpu.SideEffectType.DATAFLOW_SIDE_EFFECTING
_TOKEN = jax.ShapeDtypeStruct((8, LANES), F32)
_VM = pl.BlockSpec(memory_space=pltpu.VMEM)
_SIDE = pltpu.CompilerParams(has_side_effects=_EFFECT)


def _hbm(a):
    return pltpu.with_memory_space_constraint(a, pltpu.HBM)


def _like(a):
    return pltpu.HBM(a.shape, a.dtype)


def _dma_sems(n):
    return pltpu.SemaphoreType.DMA((n,))


def _other_chips(x, y):
    return [(1 - x, y), (x, 1 - y), (1 - x, 1 - y)]


COPY_STREAMS = 8


def _row_chunks(src, dst):
    rows = src.shape[0]
    n = COPY_STREAMS
    while n > 1 and rows % (16 * n):
        n //= 2
    r = rows // n
    return [(src.at[pl.ds(i * r, r)], dst.at[pl.ds(i * r, r)]) for i in range(n)]


class _rcopy:
    def __init__(self, src, dst, send_sem, recv_sem, to):
        self.parts = [pltpu.make_async_remote_copy(src_ref=s, dst_ref=d, send_sem=send_sem, recv_sem=recv_sem, device_id=to, device_id_type=MESH)
                      for s, d in _row_chunks(src, dst)]

    def start(self):
        for cp in self.parts:
            cp.start()

    def wait_send(self):
        for cp in self.parts:
            cp.wait_send()

    def wait_recv(self):
        for cp in self.parts:
            cp.wait_recv()


def _afters(after):
    return list(after) if isinstance(after, (list, tuple)) else [after]


def ag_start(name, shards, after):
    n = len(shards)
    lands = [lax.empty((N_DEV,) + a.shape, a.dtype) for a in shards]
    afters = _afters(after)
    na = len(afters)

    def body(*refs):
        ins, lnd, send_sems, recv_sems, token = refs[:n], refs[n:2 * n], refs[2 * n + na], refs[2 * n + na + 1], refs[4 * n + na + 2]
        x, y, c = _position()
        for w in range(n):
            slot = lnd[w].at[_index((x, y, c))]
            for k, to in enumerate([(x, y, 1 - c)] + [(*chip, c) for chip in _other_chips(x, y)]):
                _rcopy(ins[w], slot, send_sems.at[4 * w + k], recv_sems.at[4 * w + k], to).start()
        token[...] = jnp.zeros_like(token)

    out = pl.pallas_call(
        body, name=name, out_shape=(_dma_sems(4 * n), _dma_sems(4 * n)) + tuple(_like(a) for a in shards + lands) + (_TOKEN,),
        in_specs=[_HBM] * (2 * n) + [_ANY] * na, out_specs=(_SEM, _SEM) + (_HBM,) * (2 * n) + (_VM,),
        input_output_aliases={i: 2 + i for i in range(2 * n)}, compiler_params=_SIDE)(*[_hbm(a) for a in shards + lands], *afters)
    _TOKENS.push(out[-1])
    return out[0], out[1], list(out[2:2 + n]), list(out[2 + n:2 + 2 * n])


def ag_forward(name, started, after):
    send, recv, shards, lands = started
    n = len(shards)
    afters = list(after) if isinstance(after, (list, tuple)) else [after]
    na = len(afters)

    def body(*refs):
        ins, lnd, send_sems, recv_sems = refs[:n], refs[n:2 * n], refs[2 * n], refs[2 * n + 1]
        fsend, frecv, token = refs[2 * n + 2 + na], refs[2 * n + 3 + na], refs[4 * n + 4 + na]
        x, y, c = _position()
        for w in range(n):
            for j, chip in enumerate(_other_chips(x, y)):
                slot = lnd[w].at[_index((*chip, c))]
                _rcopy(ins[w], slot, send_sems.at[4 * w + 1 + j], recv_sems.at[4 * w + 1 + j], (*chip, c)).wait_recv()
                _rcopy(slot, slot, fsend.at[3 * w + j], frecv.at[3 * w + j], (x, y, 1 - c)).start()
        token[...] = jnp.zeros_like(token)

    out = pl.pallas_call(
        body, name=name, out_shape=(_dma_sems(3 * n), _dma_sems(3 * n)) + tuple(_like(a) for a in shards + lands) + (_TOKEN,),
        in_specs=[_HBM] * (2 * n) + [_SEM, _SEM] + [_ANY] * na, out_specs=(_SEM, _SEM) + (_HBM,) * (2 * n) + (_VM,),
        input_output_aliases={i: 2 + i for i in range(2 * n)}, compiler_params=_SIDE)(*shards, *lands, send, recv, *afters)
    _TOKENS.push(out[-1])
    return send, recv, out[0], out[1], list(out[2:2 + n]), list(out[2 + n:2 + 2 * n])


def ag_wait(name, forwarded, after):
    send, recv, fsend, frecv, shards, lands = forwarded
    n = len(shards)

    def body(*refs):
        ins, lnd, send_sems, recv_sems, fsend_r, frecv_r = refs[:n], refs[n:2 * n], refs[2 * n], refs[2 * n + 1], refs[2 * n + 2], refs[2 * n + 3]
        x, y, c = _position()
        sibling = (x, y, 1 - c)
        for w in range(n):
            own = lnd[w].at[_index((x, y, c))]
            _rcopy(ins[w], lnd[w].at[_index(sibling)], send_sems.at[4 * w], recv_sems.at[4 * w], sibling).wait_recv()
            for j, chip in enumerate(_other_chips(x, y)):
                _rcopy(ins[w], lnd[w].at[_index((*chip, 1 - c))], fsend_r.at[3 * w + j], frecv_r.at[3 * w + j], sibling).wait_recv()
            for k in range(4):
                _rcopy(ins[w], own, send_sems.at[4 * w + k], recv_sems.at[4 * w + k], sibling).wait_send()
            for j in range(3):
                _rcopy(ins[w], own, fsend_r.at[3 * w + j], frecv_r.at[3 * w + j], sibling).wait_send()

    out = pl.pallas_call(
        body, name=name, out_shape=tuple(_like(a) for a in shards + lands),
        in_specs=[_HBM] * (2 * n) + [_SEM] * 4 + [_ANY] * len(_afters(after)),
        out_specs=(_HBM,) * (2 * n), input_output_aliases={i: i for i in range(2 * n)},
        compiler_params=_SIDE)(*shards, *lands, send, recv, fsend, frecv, *_afters(after))
    return [lax.dynamic_update_index_in_dim(land, shard, _index(_position()), 0) for shard, land in zip(out[:n], out[n:])]


def rs_d2d_start(name, grads):
    n = len(grads)
    lands = [lax.empty((4,) + g.shape[1:], g.dtype) for g in grads]

    def body(*refs):
        ins, lnd, send_sems, recv_sems, token = refs[:n], refs[n:2 * n], refs[2 * n], refs[2 * n + 1], refs[4 * n + 2]
        x, y, c = _position()
        for w in range(n):
            for i in range(4):
                _rcopy(ins[w].at[2 * i + 1 - c], lnd[w].at[i], send_sems.at[4 * w + i], recv_sems.at[4 * w + i], (x, y, 1 - c)).start()
        token[...] = jnp.zeros_like(token)

    out = pl.pallas_call(
        body, name=name, out_shape=(_dma_sems(4 * n), _dma_sems(4 * n)) + tuple(_like(a) for a in grads + lands) + (_TOKEN,),
        in_specs=[_HBM] * (2 * n), out_specs=(_SEM, _SEM) + (_HBM,) * (2 * n) + (_VM,),
        input_output_aliases={i: 2 + i for i in range(2 * n)}, compiler_params=_SIDE)(*[_hbm(a) for a in grads + lands])
    _TOKENS.push(out[-1])
    return out[0], out[1], list(out[2:2 + n]), list(out[2 + n:2 + 2 * n])


def rs_d2d_wait(name, started, after):
    send, recv, grads, lands = started
    n = len(grads)

    def body(*refs):
        ins, lnd, send_sems, recv_sems = refs[:n], refs[n:2 * n], refs[2 * n], refs[2 * n + 1]
        x, y, c = _position()
        for w in range(n):
            for i in range(4):
                cp = _rcopy(ins[w].at[2 * i + 1 - c], lnd[w].at[i], send_sems.at[4 * w + i], recv_sems.at[4 * w + i], (x, y, 1 - c))
                cp.wait_send()
                cp.wait_recv()

    out = pl.pallas_call(
        body, name=name, out_shape=tuple(_like(a) for a in grads + lands),
        in_specs=[_HBM] * (2 * n) + [_SEM, _SEM] + [_ANY] * len(_afters(after)),
        out_specs=(_HBM,) * (2 * n), input_output_aliases={i: i for i in range(2 * n)},
        compiler_params=_SIDE)(*grads, *lands, send, recv, *_afters(after))
    return list(out[:n]), list(out[n:])


def pair_sum(name, grad, land, core):
    _, r, c = grad.shape
    tr = r
    if r % 8 == 0:
        tr = max(8, min(r, 4 * ADAMW_TILE_ELEMS // c) // 8 * 8)
        while r % tr:
            tr -= 8

    def body(core_ref, a_ref, b_ref, o_ref):
        o_ref[...] = (a_ref[...].astype(F32) + b_ref[...].astype(F32)).astype(o_ref.dtype)

    return pl.pallas_call(
        body, name=name, out_shape=jax.ShapeDtypeStruct((4, r, c), grad.dtype),
        grid_spec=pltpu.PrefetchScalarGridSpec(
            num_scalar_prefetch=1, grid=(4, r // tr),
            in_specs=[pl.BlockSpec((None, None, tr, c), lambda i, j, core_ref: (i, core_ref[0], j, 0)),
                      pl.BlockSpec((None, tr, c), lambda i, j, core_ref: (i, j, 0))],
            out_specs=pl.BlockSpec((None, tr, c), lambda i, j, core_ref: (i, j, 0))),
        compiler_params=_params("parallel", "parallel"))(core, grad.reshape(4, 2, r, c), land)


def rs_ici_start(name, sums):
    n = len(sums)
    lands = [lax.empty(a.shape, a.dtype) for a in sums]

    def body(*refs):
        ins, lnd, send_sems, recv_sems, token = refs[:n], refs[n:2 * n], refs[2 * n], refs[2 * n + 1], refs[4 * n + 2]
        x, y, c = _position()
        chip = 2 * x + y
        for w in range(n):
            for j, other in enumerate(_other_chips(x, y)):
                _rcopy(ins[w].at[2 * other[0] + other[1]], lnd[w].at[chip], send_sems.at[3 * w + j], recv_sems.at[3 * w + j], (*other, c)).start()
        token[...] = jnp.zeros_like(token)

    out = pl.pallas_call(
        body, name=name, out_shape=(_dma_sems(3 * n), _dma_sems(3 * n)) + tuple(_like(a) for a in sums + lands) + (_TOKEN,),
        in_specs=[_HBM] * (2 * n), out_specs=(_SEM, _SEM) + (_HBM,) * (2 * n) + (_VM,),
        input_output_aliases={i: 2 + i for i in range(2 * n)}, compiler_params=_SIDE)(*[_hbm(a) for a in sums + lands])
    _TOKENS.push(out[-1])
    return out[0], out[1], list(out[2:2 + n]), list(out[2 + n:2 + 2 * n])


def rs_ici_wait(name, started, after):
    send, recv, sums, lands = started
    n = len(sums)

    def body(*refs):
        ins, lnd, send_sems, recv_sems = refs[:n], refs[n:2 * n], refs[2 * n], refs[2 * n + 1]
        x, y, c = _position()
        for w in range(n):
            for j, other in enumerate(_other_chips(x, y)):
                cp = _rcopy(ins[w].at[2 * other[0] + other[1]], lnd[w].at[2 * other[0] + other[1]], send_sems.at[3 * w + j], recv_sems.at[3 * w + j], (*other, c))
                cp.wait_send()
                cp.wait_recv()

    out = pl.pallas_call(
        body, name=name, out_shape=tuple(_like(a) for a in sums + lands), in_specs=[_HBM] * (2 * n) + [_SEM, _SEM, _ANY],
        out_specs=(_HBM,) * (2 * n), input_output_aliases={i: i for i in range(2 * n)}, compiler_params=_SIDE)(*sums, *lands, send, recv, after)
    chip = 2 * lax.axis_index("x") + lax.axis_index("y")
    return [lax.dynamic_update_index_in_dim(land, lax.dynamic_index_in_dim(s, chip, 0, keepdims=False), chip, 0)
            for s, land in zip(out[:n], out[n:])]


def ada_fwd(c, w_ada, b_ada3, conv_w, after):
    d, cs = w_ada.shape

    def body(c_ref, w_ref, b_ref, cw_ref, after_ref, mod_ref, sc_ref, cwa_ref, part_ref, send_sems, recv_sems):
        me = _position()
        my = _index(me)
        cv = c_ref[...]
        sc_ref[my] = cv * _sigmoid(cv)
        cwa_ref[my] = cw_ref[...]
        gather = []
        for r in range(1, N_DEV):
            for k, ref in enumerate((sc_ref, cwa_ref)):
                cp = pltpu.make_async_remote_copy(src_ref=ref.at[my], dst_ref=ref.at[my], send_sem=send_sems.at[14 * k + r - 1],
                                                  recv_sem=recv_sems.at[14 * k + r - 1], device_id=_peer(me, r), device_id_type=MESH)
                cp.start()
                gather.append(cp)
        for cp in gather:
            cp.wait()
        sc_all = jnp.concatenate([sc_ref[k] for k in range(N_DEV)], axis=0).astype(BF16)
        part = jnp.dot(sc_all, w_ref[...].astype(BF16), preferred_element_type=F32)
        for k in range(N_DEV):
            part_ref[k] = part[k:k + 1, :]
        scatter = []
        for r in range(1, N_DEV):
            peer = _peer(me, r)
            cp = pltpu.make_async_remote_copy(src_ref=part_ref.at[_index(peer)], dst_ref=mod_ref.at[my], send_sem=send_sems.at[6 + r],
                                              recv_sem=recv_sems.at[6 + r], device_id=peer, device_id_type=MESH)
            cp.start()
            scatter.append(cp)
        mod_ref[my] = part_ref[my]
        for cp in scatter:
            cp.wait()
        mod_ref[...] = mod_ref[...] + b_ref[...]

    vm = pl.BlockSpec(memory_space=pltpu.VMEM)
    return pl.pallas_call(
        body, name="ada_fwd",
        out_shape=[jax.ShapeDtypeStruct((N_DEV, 1, cs), F32), jax.ShapeDtypeStruct((N_DEV, 1, d), F32),
                   jax.ShapeDtypeStruct((N_DEV,) + conv_w.shape, F32)],
        in_specs=[vm, vm, vm, vm, _ANY], out_specs=[vm, vm, vm],
        scratch_shapes=[pltpu.VMEM((N_DEV, 1, cs), F32), pltpu.SemaphoreType.DMA((21,)), pltpu.SemaphoreType.DMA((21,))],
        compiler_params=pltpu.CompilerParams(vmem_limit_bytes=VMEM_LIMIT_BYTES))(c, w_ada, b_ada3, conv_w, after)


def ada_bwd_w(sc_all, dmod_cols):
    _, d = sc_all.shape
    cs = dmod_cols.shape[1]
    tr = _tile(d, ROW_TILE)

    def body(sc_ref, dm_ref, o_ref):
        dm = dm_ref[...].astype(BF16)
        o_ref[...] = lax.dot_general(sc_ref[...].astype(BF16), dm, (((0,), (0,)), ((), ())), preferred_element_type=F32)

    return pl.pallas_call(body, name="ada_bwd_w", grid=(d // tr,),
                          in_specs=[pl.BlockSpec((N_DEV, tr), lambda i: (0, i)), _full((N_DEV, cs))],
                          out_specs=pl.BlockSpec((None, tr, cs), lambda i: (0, i, 0)),
                          out_shape=jax.ShapeDtypeStruct((1, d, cs), F32), compiler_params=_params("parallel"))(sc_all, dmod_cols)


def _round_up(n, m):
    return (n + m - 1) // m * m


def kernel(x, c, positions, w_ada, b_ada, pre_norm1_g, w_in, gm_ln_g, gm_ln_b, gm_w_s, gm_b_s, w_branch_a, q_norm_g, w_uq, kv_norm_g, w_ukv, w_branch_b, w_out, post_norm1_g, pre_norm2_g, w_up, conv_w, conv_b, w_down, post_norm2_g, loss_target, m_w_ada, m_b_ada, m_pre_norm1_g, m_w_in, m_gm_ln_g, m_gm_ln_b, m_gm_w_s, m_gm_b_s, m_w_branch_a, m_q_norm_g, m_w_uq, m_kv_norm_g, m_w_ukv, m_w_branch_b, m_w_out, m_post_norm1_g, m_pre_norm2_g, m_w_up, m_conv_w, m_conv_b, m_w_down, m_post_norm2_g, v_w_ada, v_b_ada, v_pre_norm1_g, v_w_in, v_gm_ln_g, v_gm_ln_b, v_gm_w_s, v_gm_b_s, v_w_branch_a, v_q_norm_g, v_w_uq, v_kv_norm_g, v_w_ukv, v_w_branch_b, v_w_out, v_post_norm1_g, v_pre_norm2_g, v_w_up, v_conv_w, v_conv_b, v_w_down, v_post_norm2_g):
    weights = dict(w_ada=w_ada, b_ada=b_ada, pre_norm1_g=pre_norm1_g, w_in=w_in, gm_ln_g=gm_ln_g, gm_ln_b=gm_ln_b, gm_w_s=gm_w_s,
                   gm_b_s=gm_b_s, w_branch_a=w_branch_a, q_norm_g=q_norm_g, w_uq=w_uq, kv_norm_g=kv_norm_g, w_ukv=w_ukv,
                   w_branch_b=w_branch_b, w_out=w_out, post_norm1_g=post_norm1_g, pre_norm2_g=pre_norm2_g, w_up=w_up, conv_w=conv_w,
                   conv_b=conv_b, w_down=w_down, post_norm2_g=post_norm2_g)
    mom1 = dict(w_ada=m_w_ada, b_ada=m_b_ada, pre_norm1_g=m_pre_norm1_g, w_in=m_w_in, gm_ln_g=m_gm_ln_g, gm_ln_b=m_gm_ln_b,
                gm_w_s=m_gm_w_s, gm_b_s=m_gm_b_s, w_branch_a=m_w_branch_a, q_norm_g=m_q_norm_g, w_uq=m_w_uq, kv_norm_g=m_kv_norm_g,
                w_ukv=m_w_ukv, w_branch_b=m_w_branch_b, w_out=m_w_out, post_norm1_g=m_post_norm1_g, pre_norm2_g=m_pre_norm2_g,
                w_up=m_w_up, conv_w=m_conv_w, conv_b=m_conv_b, w_down=m_w_down, post_norm2_g=m_post_norm2_g)
    mom2 = dict(w_ada=v_w_ada, b_ada=v_b_ada, pre_norm1_g=v_pre_norm1_g, w_in=v_w_in, gm_ln_g=v_gm_ln_g, gm_ln_b=v_gm_ln_b,
                gm_w_s=v_gm_w_s, gm_b_s=v_gm_b_s, w_branch_a=v_w_branch_a, q_norm_g=v_q_norm_g, w_uq=v_w_uq, kv_norm_g=v_kv_norm_g,
                w_ukv=v_w_ukv, w_branch_b=v_w_branch_b, w_out=v_w_out, post_norm1_g=v_post_norm1_g, pre_norm2_g=v_pre_norm2_g,
                w_up=v_w_up, conv_w=v_conv_w, conv_b=v_conv_b, w_down=v_w_down, post_norm2_g=v_post_norm2_g)
    order = list(weights)
    _TOKENS.clear()

    s, d = x.shape[1], x.shape[2]
    gmw = gm_ln_g.shape[0]
    groups = gmw // CHUNK
    ql, kvl = q_norm_g.shape[0], kv_norm_g.shape[0]
    f2 = conv_b.shape[0]
    in_cols = w_in.shape[1] * N_DEV
    o_q, o_kv, o_ga, o_gb, o_kpe = 2 * gmw, 2 * gmw + ql, 2 * gmw + ql + kvl, 2 * gmw + ql + kvl + d, 2 * gmw + ql + kvl + 2 * d
    zp = _round_up(o_kpe + LANES, Z_PAD)
    src_kpe = 2 * gmw + ql + kvl
    assert src_kpe + QK_ROPE + 2 * d == in_cols
    my = 4 * lax.axis_index("x") + 2 * lax.axis_index("y") + lax.axis_index("c")

    x2, tgt = x[0], loss_target[0]
    row = lambda a: a.reshape(1, -1)

    big = ["w_in", "w_branch_a", "w_uq", "w_ukv", "w_branch_b", "w_out", "w_up", "w_down"]
    sh = {k: weights[k].astype(BF16) for k in big[1:]}
    mix = ["w_branch_a", "w_uq", "w_ukv", "w_branch_b", "w_out"]
    w_in_t = w_in.T.astype(BF16)

    mod8, sc_all3, g_cw = ada_fwd(c, w_ada, b_ada.reshape(N_DEV, 1, -1), conv_w, w_in_t)
    ag_in = ag_start("ag_start_in", [w_in_t], mod8)
    mod = mod8.reshape(N_MOD, d)
    shift1, scale1, gate1, shift2, scale2, gate2 = (mod[i:i + 1] for i in range(N_MOD))
    sc_all = sc_all3.reshape(N_DEV, d)
    h1 = norm_mod_fwd("pre1_fwd", x2, row(pre_norm1_g), scale1, shift1)

    inv = ROPE_THETA ** (-jnp.arange(0, QK_ROPE, 2, dtype=F32) / QK_ROPE)
    ang = positions[0].astype(F32)[:, None] * inv
    cos4 = jnp.tile(jnp.cos(ang), (1, 4))
    sin4 = jnp.tile(jnp.concatenate([-jnp.sin(ang), jnp.sin(ang)], axis=1), (1, 2))

    wm = (gm_w_s * jnp.tril(jnp.ones((CHUNK, CHUNK), F32))).astype(BF16)
    bs3 = gm_b_s.reshape(groups, CHUNK, 1)
    ln_g, ln_b = row(gm_ln_g), row(gm_ln_b)

    small_names = ["pre_norm1_g", "gm_ln_g", "gm_ln_b", "gm_b_s", "q_norm_g", "kv_norm_g", "post_norm1_g", "pre_norm2_g", "conv_b",
                   "post_norm2_g", "gm_w_s", "b_ada"]
    n_small_early = sum(weights[k].size for k in small_names)
    n_pack_early = _round_up(n_small_early + 3 * f2, PACK_ALIGN)

    def pack(src):
        return jnp.concatenate([src[k].reshape(-1) for k in small_names] + [jnp.zeros((n_pack_early - n_small_early,), F32)]).reshape(-1, LANES)

    packed_state = [pack(weights), pack(mom1), pack(mom2)]

    early = [h1, cos4, sin4, wm] + [sh[k] for k in big[1:]] + packed_state
    ag_in = ag_forward("ag_forward_in", ag_in, early)
    ag_mix = ag_start("ag_start_mix", [sh[k] for k in mix], _TOKENS.pending[-1])
    (g_in,) = ag_wait("ag_wait_in", ag_in, [h1, _TOKENS.pending[-1]])
    cs_in = w_in.shape[1]

    def w_in_rows(lo, hi):
        return [g_in[k, max(lo - k * cs_in, 0):min(hi - k * cs_in, cs_in)] for k in range(N_DEV) if lo < (k + 1) * cs_in and hi > k * cs_in]

    w_in_p = jnp.concatenate(w_in_rows(0, src_kpe) + w_in_rows(src_kpe + QK_ROPE, in_cols) + w_in_rows(src_kpe, src_kpe + QK_ROPE)
                             + [jnp.zeros((zp - in_cols, d), BF16)], axis=0)

    z = mm_nt("z_proj", h1, w_in_p, ACT)
    ag_mix = ag_forward("ag_forward_mix", ag_mix, z)
    ag_up = ag_start("ag_start_up", [sh["w_up"]], _TOKENS.pending[-1])
    a = gmlp_fwd(z, gmw, ln_g, ln_b, wm, bs3)
    g_a, g_uq, g_ukv, g_b, g_out = ag_wait("ag_wait_mix", ag_mix, [a, _TOKENS.pending[-1]])
    w_a_f, w_b_f, w_out_f = g_a.reshape(-1, d), g_b.reshape(-1, d), g_out.reshape(-1, d)
    w_uq_f = g_uq.transpose(1, 0, 2).reshape(ql, HEADS, QK_NOPE + QK_ROPE)
    w_uq_n = w_uq_f[:, :, :QK_NOPE].reshape(ql, HEADS * QK_NOPE)
    w_uq_r = w_uq_f[:, :, QK_NOPE:].reshape(ql, HEADS * QK_ROPE)
    y_a = mm_nn("branch_a", a, w_a_f, ACT)
    qln = rms_fwd_cols("q_norm", z, o_q, ql, row(q_norm_g))
    kvn = rms_fwd_cols("kv_norm", z, o_kv, kvl, row(kv_norm_g))
    qn = mm_nn("q_nope", qln, w_uq_n, BF16)
    qp = mm_nn("q_rope", qln, w_uq_r, F32)
    kv = mm_nn_b3("kv_up", kvn, g_ukv, BF16)
    kpr = rope_k(z, o_kpe, cos4, sin4)
    o, qpr, lse = attn_fwd(qn, qp, kv, kpr, cos4, sin4)
    ag_up = ag_forward("ag_forward_up", ag_up, o)
    ag_down = ag_start("ag_start_down", [sh["w_down"]], _TOKENS.pending[-1])
    y_b = mm_nn("branch_b", o, w_b_f, ACT)
    merged = merge_fwd(z, o_ga, o_gb, y_a, y_b)
    y1 = mm_nn("out_proj", merged, w_out_f, ACT)
    x1 = post_res_fwd("post1_fwd", x2, y1, gate1, row(post_norm1_g))
    h2 = norm_mod_fwd("pre2_fwd", x1, row(pre_norm2_g), scale2, shift2)
    (g_up,) = ag_wait("ag_wait_up", ag_up, h2)
    upre = mm_nn_b3("up_proj", h2, g_up, ACT)
    ag_down = ag_forward("ag_forward_down", ag_down, upre)
    cw = g_cw.transpose(1, 0, 2).reshape(3, f2)
    cb = row(conv_b)
    f, gv = conv_act_fwd(upre, cw, cb)
    w_down_f = ag_wait("ag_wait_down", ag_down, f)[0].reshape(-1, d)
    ffn = mm_nn("down_proj", f, w_down_f, ACT)
    loss_acc, dout, dffn, acc2 = post2_loss_bwd(x1, ffn, tgt, gate2, row(post_norm2_g))
    loss = lax.psum(loss_acc[0, 0], ("x", "y", "c"))
    _TOKENS.push(jnp.broadcast_to(loss, (8, LANES)))

    blocks = lambda g: g.reshape(N_DEV, g.shape[0] // N_DEV, g.shape[1])
    core = lax.axis_index("c").astype(jnp.int32).reshape(1)
    rs = {}

    def rs_begin(key, grads):
        rs[key] = rs_d2d_start("rs_d2d_start_" + key, grads)

    def rs_middle(key, after):
        grads, lands = rs_d2d_wait("rs_d2d_wait_" + key, rs[key], after)
        sums = [pair_sum("pair_sum_%s_%d" % (key, i), g, l, core) for i, (g, l) in enumerate(zip(grads, lands))]
        rs[key] = rs_ici_start("rs_ici_start_" + key, sums)

    gw_down = mm_tn("g_w_down", f, dffn, BF16)
    rs_begin("down", [blocks(gw_down)])
    df = mm_nt("d_f", dffn, w_down_f, ACT)
    rs_middle("down", df)
    dupre, gcw_g, gcw_v, gcb_g, gcb_v = conv_act_bwd(upre, cw, gv, df)
    gw_up3 = mm_tn_h3("g_w_up", h2, dupre, N_DEV, BF16)
    rs_begin("up", [gw_up3])
    dh2 = mm_nt_h3("d_h2", dupre, g_up, ACT)
    rs_middle("up", dh2)
    dx1, dy1, acc_mid = mid_bwd(dh2, dout, x1, y1, row(pre_norm2_g), scale2, gate1, row(post_norm1_g))
    gw_out = mm_tn("g_w_out", merged, dy1, BF16)
    dmerged = mm_nt("d_merged", dy1, w_out_f, ACT)
    dya, dyb, dga, dgb = merge_bwd(z, o_ga, o_gb, y_a, y_b, dmerged)
    gw_a = mm_tn("g_w_a", a, dya, BF16)
    gw_b = mm_tn("g_w_b", o, dyb, BF16)
    rs_begin("mid", [blocks(gw_out), blocks(gw_a), blocks(gw_b)])
    da = mm_nt("d_a", dya, w_a_f, ACT)
    do = mm_nt("d_o", dyb, w_b_f, ACT)
    rs_middle("mid", do)
    duv, g_ws, g_bs3, acc_gm = gmlp_bwd(z, gmw, da, ln_g, ln_b, wm, bs3)
    dqn, dqp, dkv, dkp = attn_bwd(qn, qpr, kv, kpr, o, do, lse, cos4, sin4)
    dkpe = kpe_bwd(dkp, cos4, sin4, zp - o_kpe)
    dq_cat = jnp.concatenate([dqn, dqp], axis=1)
    w_uq_cat = jnp.concatenate([w_uq_n, w_uq_r], axis=1)
    gw_uq_cat = mm_tn("g_w_uq", qln, dq_cat, BF16)
    gw_uq_f = jnp.concatenate([gw_uq_cat[:, :HEADS * QK_NOPE].reshape(ql, HEADS, QK_NOPE),
                               gw_uq_cat[:, HEADS * QK_NOPE:].reshape(ql, HEADS, QK_ROPE)], axis=2)
    gw_uq3 = gw_uq_f.reshape(ql, N_DEV, -1).transpose(1, 0, 2)
    gw_ukv3 = mm_tn_o3("g_w_ukv", kvn, dkv, N_DEV, BF16)
    rs_begin("mla", [gw_uq3, gw_ukv3])
    dqln = mm_nt("d_qln", dq_cat, w_uq_cat, ACT)
    dq_lat, g_qnorm = rms_bwd_cols("q_norm_bwd", dqln, z, o_q, ql, row(q_norm_g))
    dkvn = mm_nt_b3("d_kvn", dkv, g_ukv, ACT)
    rs_middle("mla", dkvn)
    dkv_lat, g_kvnorm = rms_bwd_cols("kv_norm_bwd", dkvn, z, o_kv, kvl, row(kv_norm_g))
    dz = jnp.concatenate([duv, dq_lat, dkv_lat, dga, dgb, dkpe], axis=1)
    gw_in_p = mm_tn("g_w_in", dz, h1, BF16)

    def gw_in_rows(lo, hi):
        pieces = []
        for a, b, shift in ((0, src_kpe, 0), (src_kpe, src_kpe + QK_ROPE, o_kpe - src_kpe), (src_kpe + QK_ROPE, in_cols, -QK_ROPE)):
            if lo < b and hi > a:
                pieces.append(gw_in_p[max(lo, a) + shift:min(hi, b) + shift])
        return pieces[0] if len(pieces) == 1 else jnp.concatenate(pieces, axis=0)

    rs_begin("in", [jnp.stack([gw_in_rows(k * cs_in, (k + 1) * cs_in) for k in range(N_DEV)])])
    dh1 = mm_nn("d_h1", dz, w_in_p, ACT)
    grad_x, acc1 = pre1_bwd(dh1, dx1, x2, row(pre_norm1_g), scale1)

    dmod = jnp.concatenate([acc1[0], acc1[1], acc_mid[3], acc_mid[0], acc_mid[1], acc2[0]])
    small = [("pre_norm1_g", acc1[2]), ("gm_ln_g", acc_gm[0]), ("gm_ln_b", acc_gm[1]), ("gm_b_s", g_bs3.reshape(-1)),
             ("q_norm_g", g_qnorm[0]), ("kv_norm_g", g_kvnorm[0]), ("post_norm1_g", acc_mid[4]), ("pre_norm2_g", acc_mid[2]),
             ("conv_b", jnp.concatenate([gcb_g[0], gcb_v[0]])), ("post_norm2_g", acc2[1]), ("gm_w_s", g_ws.reshape(-1)),
             ("b_ada", dmod)]
    n_small = sum(v.shape[0] for _, v in small)
    n_cw = 3 * f2
    n_pack = _round_up(n_small + n_cw, PACK_ALIGN)
    tail = jnp.zeros((n_pack - n_small - n_cw,), F32)
    packed = jnp.concatenate([v for _, v in small] + [jnp.concatenate([gcw_g, gcw_v], axis=1).reshape(-1), tail])
    ag_small = ag_start("ag_start_small", [packed.reshape(-1, LANES)], packed)
    rs_middle("in", [packed, _TOKENS.pending[-1]])

    res = {}
    last = packed
    for key, names in (("down", ["w_down"]), ("up", ["w_up"]), ("mid", ["w_out", "w_branch_a", "w_branch_b"]), ("mla", ["w_uq", "w_ukv"])):
        parts = rs_ici_wait("rs_ici_wait_" + key, rs[key], last)
        for k, p in zip(names, parts):
            res[k] = adamw("adamw_" + k, weights[k], mom1[k], mom2[k], p)
            last = res[k][0]

    assert [k for k, _ in small] == small_names and n_small == n_small_early
    (gathered,) = ag_wait("ag_wait_small", ag_forward("ag_forward_small", ag_small, last), last)
    sm = [t.reshape(-1) for t in adamw("adamw_small", *packed_state, gathered)]
    off = 0
    for k, v in small:
        res[k] = tuple(t[off:off + v.shape[0]].reshape(weights[k].shape) for t in sm)
        off += v.shape[0]

    cs_cw = conv_w.shape[1]
    g_cw_full = sm[0][n_small:n_small + n_cw].reshape(3, f2)
    g_cw_mine = lax.dynamic_slice(g_cw_full, (0, my * cs_cw), (3, cs_cw))
    res["conv_w"] = adamw("adamw_conv_w", conv_w, mom1["conv_w"], mom2["conv_w"], g_cw_mine[None])

    cs_ada = w_ada.shape[1]
    off_b = n_small - N_MOD * d
    dmod_all = gathered.reshape(N_DEV, -1)[:, off_b:off_b + N_MOD * d]
    dmod_cols = lax.dynamic_slice(dmod_all, (0, my * cs_ada), (N_DEV, cs_ada))
    res["w_ada"] = adamw("adamw_w_ada", w_ada, mom1["w_ada"], mom2["w_ada"], ada_bwd_w(sc_all, dmod_cols))

    (p_in,) = rs_ici_wait("rs_ici_wait_in", rs["in"], res["w_ada"][0])
    res["w_in"] = tuple(t.T for t in adamw("adamw_w_in", w_in.T, mom1["w_in"].T, mom2["w_in"].T, p_in))

    _TOKENS.clear()
    outs = [loss, grad_x[None]]
    for i in range(4):
        outs += [res[k][i] for k in order]
    return tuple(outs)
```

```python
import jax
import jax.numpy as jnp
from jax import lax
from jax.experimental import pallas as pl
from jax.experimental.pallas import tpu as pltpu

F32 = jnp.float32
BF16 = jnp.bfloat16
ACT = BF16

N_DEV = 8
HEADS = 16
QK_NOPE = 128
QK_ROPE = 64
V_HEAD = 128
CHUNK = 128
ROPE_THETA = 10000.0
EPS = 1e-6
N_MOD = 6
ADAM_LR, ADAM_B1, ADAM_B2, ADAM_EPS, ADAM_WD, ADAM_STEP = 0.001, 0.9, 0.999, 1e-08, 0.01, 10

LANES = 128
VMEM_LIMIT_BYTES = 48 * 2 ** 20
ROW_TILE = 256
COL_TILE = 256
ATT_TILE = 512
Z_PAD = 512
ADAMW_TILE_ELEMS = 1 << 18
PACK_ALIGN = 8 * LANES
MESH = pl.DeviceIdType.MESH


def _params(*sem):
    return pltpu.CompilerParams(dimension_semantics=sem if sem else None, vmem_limit_bytes=VMEM_LIMIT_BYTES)


def _tile(dim, target):
    t = (min(dim, target) // LANES) * LANES
    while t >= LANES:
        if dim % t == 0:
            return t
        t -= LANES
    return dim


def _full(shape):
    nd = len(shape)
    return pl.BlockSpec(shape, lambda *_: (0,) * nd)


class _Tokens:
    KEEP = 2

    def __init__(self):
        self.pending = []

    def push(self, token):
        self.pending = (self.pending + [token])[-self.KEEP:]

    def take(self):
        return list(self.pending)

    def clear(self):
        self.pending = []


_TOKENS = _Tokens()


def _matmul(name, a, b, *, grid, a_spec, b_spec, o_spec, out_shape, contract, acc_shape, split=1):
    nk = grid[2]
    deps = _TOKENS.take()

    def product(a_ref, b_ref):
        if len(b_ref.shape) == 2:
            return lax.dot_general(a_ref[...].astype(BF16), b_ref[...].astype(BF16), (contract, ((), ())), preferred_element_type=F32)
        cs = b_ref.shape[2]
        return sum(lax.dot_general(a_ref[:, s * cs:(s + 1) * cs].astype(BF16), b_ref[s].astype(BF16), (contract, ((), ())),
                                   preferred_element_type=F32) for s in range(split))

    def body_one_step(a_ref, b_ref, *rest):
        o_ref = rest[len(deps)]
        o_ref[...] = product(a_ref, b_ref).astype(o_ref.dtype)

    def body(a_ref, b_ref, *rest):
        o_ref, acc_ref = rest[len(deps):]
        k = pl.program_id(2)

        @pl.when(k == 0)
        def _():
            acc_ref[...] = jnp.zeros_like(acc_ref)

        acc_ref[...] += product(a_ref, b_ref)

        @pl.when(k == nk - 1)
        def _():
            o_ref[...] = acc_ref[...].astype(o_ref.dtype)

    return pl.pallas_call(
        body_one_step if nk == 1 else body, name=name, grid=grid,
        in_specs=[a_spec, b_spec] + [pl.BlockSpec(memory_space=pl.ANY)] * len(deps),
        out_specs=o_spec, out_shape=out_shape, scratch_shapes=[] if nk == 1 else [pltpu.VMEM(acc_shape, F32)],
        compiler_params=_params("parallel", "parallel", "arbitrary"))(a, b, *deps)


T_OUT, T_OUT_WIDE, TK = 1024, 1408, 2816


def _out_tile(dim):
    return T_OUT_WIDE if dim % T_OUT_WIDE == 0 else _tile(dim, T_OUT)


def _tk(a, b):
    return TK if a.dtype == BF16 and b.dtype == BF16 else TK // 2


def mm_nn(name, a, b, dtype):
    (m, k), n = a.shape, b.shape[1]
    tm, tn, tk = _out_tile(m), _out_tile(n), _tile(k, _tk(a, b))
    return _matmul(name, a, b, grid=(m // tm, n // tn, k // tk),
                   a_spec=pl.BlockSpec((tm, tk), lambda i, j, kk: (i, kk)),
                   b_spec=pl.BlockSpec((tk, tn), lambda i, j, kk: (kk, j)),
                   o_spec=pl.BlockSpec((tm, tn), lambda i, j, kk: (i, j)),
                   out_shape=jax.ShapeDtypeStruct((m, n), dtype), contract=((1,), (0,)), acc_shape=(tm, tn))


def mm_nn_b3(name, a, b3, dtype):
    (m, k), (nj, _, cs) = a.shape, b3.shape
    tm, tk = _out_tile(m), _tile(k, _tk(a, b3))
    return _matmul(name, a, b3, grid=(m // tm, nj, k // tk),
                   a_spec=pl.BlockSpec((tm, tk), lambda i, j, kk: (i, kk)),
                   b_spec=pl.BlockSpec((None, tk, cs), lambda i, j, kk: (j, kk, 0)),
                   o_spec=pl.BlockSpec((tm, cs), lambda i, j, kk: (i, j)),
                   out_shape=jax.ShapeDtypeStruct((m, nj * cs), dtype), contract=((1,), (0,)), acc_shape=(tm, cs))


def mm_nt(name, a, b, dtype):
    (m, k), n = a.shape, b.shape[0]
    tm, tn, tk = _out_tile(m), _out_tile(n), _tile(k, _tk(a, b))
    return _matmul(name, a, b, grid=(m // tm, n // tn, k // tk),
                   a_spec=pl.BlockSpec((tm, tk), lambda i, j, kk: (i, kk)),
                   b_spec=pl.BlockSpec((tn, tk), lambda i, j, kk: (j, kk)),
                   o_spec=pl.BlockSpec((tm, tn), lambda i, j, kk: (i, j)),
                   out_shape=jax.ShapeDtypeStruct((m, n), dtype), contract=((1,), (1,)), acc_shape=(tm, tn))


def mm_nt_b3(name, a, b3, dtype):
    m, (nj, n, cs) = a.shape[0], b3.shape
    tm, tn = _out_tile(m), _out_tile(n)
    return _matmul(name, a, b3, grid=(m // tm, n // tn, nj),
                   a_spec=pl.BlockSpec((tm, cs), lambda i, j, kk: (i, kk)),
                   b_spec=pl.BlockSpec((None, tn, cs), lambda i, j, kk: (kk, j, 0)),
                   o_spec=pl.BlockSpec((tm, tn), lambda i, j, kk: (i, j)),
                   out_shape=jax.ShapeDtypeStruct((m, n), dtype), contract=((1,), (1,)), acc_shape=(tm, tn))


def mm_nt_h3(name, a3, b3, dtype):
    (_, m, _), (nj, n, cs) = a3.shape, b3.shape
    tm, tn, hj = _out_tile(m), _out_tile(n), nj // 2
    pair = 2 if hj % 2 == 0 else 1
    return _matmul(name, a3, b3.reshape(nj // pair, pair, n, cs), grid=(m // tm, n // tn, nj // pair),
                   a_spec=pl.BlockSpec((None, tm, pair * cs), lambda i, j, kk: (kk // (hj // pair), i, kk % (hj // pair))),
                   b_spec=pl.BlockSpec((None, pair, tn, cs), lambda i, j, kk: (kk, 0, j, 0)),
                   o_spec=pl.BlockSpec((tm, tn), lambda i, j, kk: (i, j)),
                   out_shape=jax.ShapeDtypeStruct((m, n), dtype), contract=((1,), (1,)), acc_shape=(tm, tn), split=pair)


def mm_tn_h3(name, a, b3, nj, dtype):
    (k, m), half = a.shape, b3.shape[2]
    hj = nj // 2
    cs = half // hj
    tm, tk = _out_tile(m), _tile(k, _tk(a, b3))
    return _matmul(name, a, b3, grid=(m // tm, nj, k // tk),
                   a_spec=pl.BlockSpec((tk, tm), lambda i, j, kk: (kk, i)),
                   b_spec=pl.BlockSpec((None, tk, cs), lambda i, j, kk: (j // hj, kk, j % hj)),
                   o_spec=pl.BlockSpec((None, tm, cs), lambda i, j, kk: (j, i, 0)),
                   out_shape=jax.ShapeDtypeStruct((nj, m, cs), dtype), contract=((0,), (0,)), acc_shape=(tm, cs))


def mm_tn(name, a, b, dtype):
    (k, m), n = a.shape, b.shape[1]
    tm, tn, tk = _out_tile(m), _out_tile(n), _tile(k, _tk(a, b))
    return _matmul(name, a, b, grid=(m // tm, n // tn, k // tk),
                   a_spec=pl.BlockSpec((tk, tm), lambda i, j, kk: (kk, i)),
                   b_spec=pl.BlockSpec((tk, tn), lambda i, j, kk: (kk, j)),
                   o_spec=pl.BlockSpec((tm, tn), lambda i, j, kk: (i, j)),
                   out_shape=jax.ShapeDtypeStruct((m, n), dtype), contract=((0,), (0,)), acc_shape=(tm, tn))


def mm_tn_o3(name, a, b, nj, dtype):
    (k, m), n = a.shape, b.shape[1]
    cs = n // nj
    tm, tk = _out_tile(m), _tile(k, _tk(a, b))
    return _matmul(name, a, b, grid=(m // tm, nj, k // tk),
                   a_spec=pl.BlockSpec((tk, tm), lambda i, j, kk: (kk, i)),
                   b_spec=pl.BlockSpec((tk, cs), lambda i, j, kk: (kk, j)),
                   o_spec=pl.BlockSpec((None, tm, cs), lambda i, j, kk: (j, i, 0)),
                   out_shape=jax.ShapeDtypeStruct((nj, m, cs), dtype), contract=((0,), (0,)), acc_shape=(tm, cs))


_GELU_C = 0.7978845608028654
_GELU_A = 0.044715


def _f32(ref):
    return ref[...].astype(F32)


def _gelu(x):
    x = x.astype(F32)
    return 0.5 * x * (1.0 + jnp.tanh(_GELU_C * (x + _GELU_A * x * x * x)))


def _gelu_and_grad(x):
    x = x.astype(F32)
    t = jnp.tanh(_GELU_C * (x + _GELU_A * x * x * x))
    y = 0.5 * x * (1.0 + t)
    dy = 0.5 * (1.0 + t) + 0.5 * x * (1.0 - t * t) * (_GELU_C * (1.0 + 3.0 * _GELU_A * x * x))
    return y, dy


def _sigmoid(x):
    return 0.5 * jnp.tanh(0.5 * x.astype(F32)) + 0.5


def _rms_stats(x):
    x = x.astype(F32)
    inv = lax.rsqrt(jnp.mean(x * x, axis=-1, keepdims=True) + EPS)
    return inv, x * inv


def _rms_bwd(dyhat, yhat, inv):
    return inv * (dyhat - yhat * jnp.mean(dyhat * yhat, axis=-1, keepdims=True))


def _colsum(x):
    return jnp.sum(x, axis=0, keepdims=True)


def _rope(x, cos4, sin4):
    lane = lax.broadcasted_iota(jnp.int32, x.shape, x.ndim - 1)
    first_half = (lane % QK_ROPE) < (QK_ROPE // 2)
    partner = jnp.where(first_half, pltpu.roll(x, LANES - QK_ROPE // 2, x.ndim - 1), pltpu.roll(x, QK_ROPE // 2, x.ndim - 1))
    return x * cos4 + partner * sin4


def norm_mod_fwd(name, x, g, scale, shift):
    s, d = x.shape
    tr = _tile(s, ROW_TILE)

    def body(x_ref, g_ref, sc_ref, sh_ref, o_ref):
        _, xh = _rms_stats(x_ref[...])
        o_ref[...] = (xh * g_ref[...] * (1.0 + sc_ref[...]) + sh_ref[...]).astype(o_ref.dtype)

    row = pl.BlockSpec((tr, d), lambda i: (i, 0))
    vec = pl.BlockSpec((1, d), lambda i: (0, 0))
    return pl.pallas_call(body, name=name, grid=(s // tr,), in_specs=[row, vec, vec, vec], out_specs=row,
                          out_shape=jax.ShapeDtypeStruct((s, d), BF16), compiler_params=_params("parallel"))(x, g, scale, shift)


def rms_fwd_cols(name, z, off, width, g):
    s = z.shape[0]
    tr = _tile(s, ROW_TILE)
    assert off % width == 0

    def body(x_ref, g_ref, o_ref):
        _, xh = _rms_stats(x_ref[...])
        o_ref[...] = (xh * g_ref[...]).astype(o_ref.dtype)

    return pl.pallas_call(body, name=name, grid=(s // tr,),
                          in_specs=[pl.BlockSpec((tr, width), lambda i: (i, off // width)), pl.BlockSpec((1, width), lambda i: (0, 0))],
                          out_specs=pl.BlockSpec((tr, width), lambda i: (i, 0)),
                          out_shape=jax.ShapeDtypeStruct((s, width), BF16), compiler_params=_params("parallel"))(z, g)


def rms_bwd_cols(name, dy, z, off, width, g):
    s = z.shape[0]
    tr = _tile(s, ROW_TILE)

    def body(dy_ref, x_ref, g_ref, dx_ref, gg_ref):
        @pl.when(pl.program_id(0) == 0)
        def _():
            gg_ref[...] = jnp.zeros_like(gg_ref)

        inv, xh = _rms_stats(x_ref[...])
        dy_v = _f32(dy_ref)
        gg_ref[...] += _colsum(dy_v * xh)
        dx_ref[...] = _rms_bwd(dy_v * g_ref[...], xh, inv).astype(dx_ref.dtype)

    return pl.pallas_call(body, name=name, grid=(s // tr,),
                          in_specs=[pl.BlockSpec((tr, width), lambda i: (i, 0)), pl.BlockSpec((tr, width), lambda i: (i, off // width)),
                                    pl.BlockSpec((1, width), lambda i: (0, 0))],
                          out_specs=[pl.BlockSpec((tr, width), lambda i: (i, 0)), pl.BlockSpec((1, width), lambda i: (0, 0))],
                          out_shape=[jax.ShapeDtypeStruct((s, width), BF16), jax.ShapeDtypeStruct((1, width), F32)],
                          compiler_params=_params("arbitrary"))(dy, z, g)


def post_res_fwd(name, x, y, gate, g):
    s, d = x.shape
    tr = _tile(s, ROW_TILE)

    def body(x_ref, y_ref, gate_ref, g_ref, o_ref):
        _, yh = _rms_stats(y_ref[...])
        o_ref[...] = x_ref[...] + gate_ref[...] * (yh * g_ref[...])

    row = pl.BlockSpec((tr, d), lambda i: (i, 0))
    vec = pl.BlockSpec((1, d), lambda i: (0, 0))
    return pl.pallas_call(body, name=name, grid=(s // tr,), in_specs=[row, row, vec, vec], out_specs=row,
                          out_shape=jax.ShapeDtypeStruct((s, d), F32), compiler_params=_params("parallel"))(x, y, gate, g)


def post2_loss_bwd(x1, ffn, target, gate2, g):
    s, d = x1.shape
    tr = _tile(s, ROW_TILE)

    def body(x_ref, y_ref, t_ref, gate_ref, g_ref, loss_ref, dout_ref, dy_ref, acc_ref):
        @pl.when(pl.program_id(0) == 0)
        def _():
            loss_ref[...] = jnp.zeros_like(loss_ref)
            acc_ref[...] = jnp.zeros_like(acc_ref)

        inv, yh = _rms_stats(y_ref[...])
        r = yh * g_ref[...]
        err = x_ref[...] + gate_ref[...] * r - t_ref[...]
        loss_ref[...] += 0.5 * jnp.sum(jnp.mean(err * err, axis=-1, keepdims=True))
        dout = err / d
        dout_ref[...] = dout
        dr = dout * gate_ref[...]
        acc_ref[0:1, :] += _colsum(dout * r)
        acc_ref[1:2, :] += _colsum(dr * yh)
        dy_ref[...] = _rms_bwd(dr * g_ref[...], yh, inv).astype(dy_ref.dtype)

    row = pl.BlockSpec((tr, d), lambda i: (i, 0))
    vec = pl.BlockSpec((1, d), lambda i: (0, 0))
    return pl.pallas_call(
        body, name="post2_loss_bwd", grid=(s // tr,), in_specs=[row, row, row, vec, vec],
        out_specs=[_full((8, LANES)), row, row, _full((8, d))],
        out_shape=[jax.ShapeDtypeStruct((8, LANES), F32), jax.ShapeDtypeStruct((s, d), F32),
                   jax.ShapeDtypeStruct((s, d), BF16), jax.ShapeDtypeStruct((8, d), F32)],
        compiler_params=_params("arbitrary"))(x1, ffn, target, gate2, g)


def mid_bwd(dh2, dout, x1, y1, pre2_g, scale2, gate1, post1_g):
    s, d = x1.shape
    tr = _tile(s, ROW_TILE)

    def body(dh_ref, dout_ref, x_ref, y_ref, g2_ref, sc_ref, gate_ref, g1_ref, dx_ref, dy_ref, acc_ref):
        @pl.when(pl.program_id(0) == 0)
        def _():
            acc_ref[...] = jnp.zeros_like(acc_ref)

        dh = _f32(dh_ref)
        inv2, xh = _rms_stats(x_ref[...])
        acc_ref[0:1, :] += _colsum(dh)
        acc_ref[1:2, :] += _colsum(dh * (xh * g2_ref[...]))
        t = dh * (1.0 + sc_ref[...])
        acc_ref[2:3, :] += _colsum(t * xh)
        dx1 = dout_ref[...] + _rms_bwd(t * g2_ref[...], xh, inv2)
        dx_ref[...] = dx1
        inv1, yh = _rms_stats(y_ref[...])
        acc_ref[3:4, :] += _colsum(dx1 * (yh * g1_ref[...]))
        dr = dx1 * gate_ref[...]
        acc_ref[4:5, :] += _colsum(dr * yh)
        dy_ref[...] = _rms_bwd(dr * g1_ref[...], yh, inv1).astype(dy_ref.dtype)

    row = pl.BlockSpec((tr, d), lambda i: (i, 0))
    vec = pl.BlockSpec((1, d), lambda i: (0, 0))
    return pl.pallas_call(
        body, name="mid_bwd", grid=(s // tr,), in_specs=[row, row, row, row, vec, vec, vec, vec],
        out_specs=[row, row, _full((8, d))],
        out_shape=[jax.ShapeDtypeStruct((s, d), F32), jax.ShapeDtypeStruct((s, d), BF16), jax.ShapeDtypeStruct((8, d), F32)],
        compiler_params=_params("arbitrary"))(dh2, dout, x1, y1, pre2_g, scale2, gate1, post1_g)


def pre1_bwd(dh1, dx1, x, pre1_g, scale1):
    s, d = x.shape
    tr = _tile(s, ROW_TILE)

    def body(dh_ref, dx1_ref, x_ref, g_ref, sc_ref, dx_ref, acc_ref):
        @pl.when(pl.program_id(0) == 0)
        def _():
            acc_ref[...] = jnp.zeros_like(acc_ref)

        dh = _f32(dh_ref)
        inv, xh = _rms_stats(x_ref[...])
        acc_ref[0:1, :] += _colsum(dh)
        acc_ref[1:2, :] += _colsum(dh * (xh * g_ref[...]))
        t = dh * (1.0 + sc_ref[...])
        acc_ref[2:3, :] += _colsum(t * xh)
        dx_ref[...] = dx1_ref[...] + _rms_bwd(t * g_ref[...], xh, inv)

    row = pl.BlockSpec((tr, d), lambda i: (i, 0))
    vec = pl.BlockSpec((1, d), lambda i: (0, 0))
    return pl.pallas_call(
        body, name="pre1_bwd", grid=(s // tr,), in_specs=[row, row, row, vec, vec], out_specs=[row, _full((8, d))],
        out_shape=[jax.ShapeDtypeStruct((s, d), F32), jax.ShapeDtypeStruct((8, d), F32)],
        compiler_params=_params("arbitrary"))(dh1, dx1, x, pre1_g, scale1)


def _ln_stats(v):
    mu = jnp.mean(v, axis=-1, keepdims=True)
    vc = v - mu
    rstd = lax.rsqrt(jnp.mean(vc * vc, axis=-1, keepdims=True) + EPS)
    return rstd, vc * rstd


def gmlp_fwd(z, width, ln_g, ln_b, wm, bs3):
    s = z.shape[0]
    groups = width // CHUNK

    def body(u_ref, v_ref, g_ref, b_ref, wm_ref, bs_ref, a_ref):
        ug = _gelu(u_ref[...])
        _, vh = _ln_stats(_gelu(v_ref[...]))
        vn = (vh * g_ref[...] + b_ref[...]).astype(BF16)
        for g in range(groups):
            cols = slice(g * CHUNK, (g + 1) * CHUNK)
            mixed = jnp.dot(wm_ref[g], vn[:, cols], preferred_element_type=F32) + bs_ref[g]
            a_ref[:, cols] = (ug[:, cols] * mixed).astype(a_ref.dtype)

    vec = pl.BlockSpec((1, width), lambda n: (0, 0))
    return pl.pallas_call(
        body, name="gmlp_fwd", grid=(s // CHUNK,),
        in_specs=[pl.BlockSpec((CHUNK, width), lambda n: (n, 0)), pl.BlockSpec((CHUNK, width), lambda n: (n, 1)), vec, vec,
                  _full(wm.shape), _full(bs3.shape)],
        out_specs=pl.BlockSpec((CHUNK, width), lambda n: (n, 0)),
        out_shape=jax.ShapeDtypeStruct((s, width), BF16), compiler_params=_params("parallel"))(z, z, ln_g, ln_b, wm, bs3)


def gmlp_bwd(z, width, da, ln_g, ln_b, wm, bs3):
    s = z.shape[0]
    groups = width // CHUNK

    def body(u_ref, v_ref, da_ref, g_ref, b_ref, wm_ref, bs_ref, duv_ref, gw_ref, gb_ref, acc_ref, dvn_ref):
        @pl.when(pl.program_id(0) == 0)
        def _():
            gw_ref[...] = jnp.zeros_like(gw_ref)
            gb_ref[...] = jnp.zeros_like(gb_ref)
            acc_ref[...] = jnp.zeros_like(acc_ref)

        ug, dug = _gelu_and_grad(u_ref[...])
        vg, dvg = _gelu_and_grad(v_ref[...])
        rstd, vh = _ln_stats(vg)
        vn = (vh * g_ref[...] + b_ref[...]).astype(BF16)
        da_v = _f32(da_ref)
        for g in range(groups):
            cols = slice(g * CHUNK, (g + 1) * CHUNK)
            mixed = jnp.dot(wm_ref[g], vn[:, cols], preferred_element_type=F32) + bs_ref[g]
            duv_ref[:, cols] = (da_v[:, cols] * mixed * dug[:, cols]).astype(duv_ref.dtype)
            dm = da_v[:, cols] * ug[:, cols]
            gb_ref[g] += jnp.sum(dm, axis=-1, keepdims=True)
            dmb = dm.astype(BF16)
            gw_ref[g] += lax.dot_general(dmb, vn[:, cols], (((1,), (1,)), ((), ())), preferred_element_type=F32)
            dvn_ref[:, cols] = lax.dot_general(wm_ref[g], dmb, (((0,), (0,)), ((), ())), preferred_element_type=F32)
        dvn = dvn_ref[...]
        acc_ref[0:1, :] += _colsum(dvn * vh)
        acc_ref[1:2, :] += _colsum(dvn)
        dvh = dvn * g_ref[...]
        dv = rstd * (dvh - jnp.mean(dvh, axis=-1, keepdims=True) - vh * jnp.mean(dvh * vh, axis=-1, keepdims=True))
        duv_ref[:, width:] = (dv * dvg).astype(duv_ref.dtype)

        @pl.when(pl.program_id(0) == pl.num_programs(0) - 1)
        def _():
            q = lax.broadcasted_iota(jnp.int32, gw_ref.shape, 1)
            p = lax.broadcasted_iota(jnp.int32, gw_ref.shape, 2)
            gw_ref[...] = jnp.where(p <= q, gw_ref[...], 0.0)

    vec = pl.BlockSpec((1, width), lambda n: (0, 0))
    blk = pl.BlockSpec((CHUNK, width), lambda n: (n, 0))
    return pl.pallas_call(
        body, name="gmlp_bwd", grid=(s // CHUNK,),
        in_specs=[blk, pl.BlockSpec((CHUNK, width), lambda n: (n, 1)), blk, vec, vec, _full(wm.shape), _full(bs3.shape)],
        out_specs=[pl.BlockSpec((CHUNK, 2 * width), lambda n: (n, 0)), _full(wm.shape), _full(bs3.shape), _full((8, width))],
        out_shape=[jax.ShapeDtypeStruct((s, 2 * width), BF16), jax.ShapeDtypeStruct(wm.shape, F32),
                   jax.ShapeDtypeStruct(bs3.shape, F32), jax.ShapeDtypeStruct((8, width), F32)],
        scratch_shapes=[pltpu.VMEM((CHUNK, width), F32)],
        compiler_params=_params("arbitrary"))(z, z, da, ln_g, ln_b, wm, bs3)


def merge_fwd(z, off_a, off_b, ya, yb):
    s, d = ya.shape
    tr, tc = _tile(s, ROW_TILE * 2), _tile(d, COL_TILE)
    assert off_a % tc == 0 and off_b % tc == 0

    def body(ga_ref, gb_ref, ya_ref, yb_ref, o_ref):
        o_ref[...] = (_sigmoid(ga_ref[...]) * _f32(ya_ref) + _sigmoid(gb_ref[...]) * _f32(yb_ref)).astype(o_ref.dtype)

    blk = pl.BlockSpec((tr, tc), lambda i, j: (i, j))
    return pl.pallas_call(
        body, name="merge_fwd", grid=(s // tr, d // tc),
        in_specs=[pl.BlockSpec((tr, tc), lambda i, j: (i, off_a // tc + j)), pl.BlockSpec((tr, tc), lambda i, j: (i, off_b // tc + j)), blk, blk],
        out_specs=blk, out_shape=jax.ShapeDtypeStruct((s, d), BF16), compiler_params=_params("parallel", "parallel"))(z, z, ya, yb)


def merge_bwd(z, off_a, off_b, ya, yb, dm):
    s, d = ya.shape
    tr, tc = _tile(s, ROW_TILE * 2), _tile(d, COL_TILE)
    nc = d // tc

    def body(ga_ref, gb_ref, ya_ref, yb_ref, dm_ref, dya_ref, dyb_ref, dga_ref, dgb_ref):
        dm_v = _f32(dm_ref)
        sa, sb = _sigmoid(ga_ref[...]), _sigmoid(gb_ref[...])
        dya_ref[...] = (dm_v * sa).astype(dya_ref.dtype)
        dyb_ref[...] = (dm_v * sb).astype(dyb_ref.dtype)
        dga_ref[...] = (dm_v * _f32(ya_ref) * sa * (1.0 - sa)).astype(dga_ref.dtype)
        dgb_ref[...] = (dm_v * _f32(yb_ref) * sb * (1.0 - sb)).astype(dgb_ref.dtype)

    blk = pl.BlockSpec((tr, tc), lambda i, j: (i, j))
    out = jax.ShapeDtypeStruct((s, d), BF16)
    return pl.pallas_call(
        body, name="merge_bwd", grid=(s // tr, nc),
        in_specs=[pl.BlockSpec((tr, tc), lambda i, j: (i, off_a // tc + j)), pl.BlockSpec((tr, tc), lambda i, j: (i, off_b // tc + j)), blk, blk, blk],
        out_specs=[blk, blk, blk, blk], out_shape=[out, out, out, out],
        compiler_params=_params("parallel", "parallel"))(z, z, ya, yb, dm)


_ATT_SCALE = (QK_NOPE + QK_ROPE) ** -0.5
_NEG = -1e30


def rope_k(z, off, cos4, sin4):
    s = z.shape[0]
    tr = _tile(s, ROW_TILE * 2)
    assert off % LANES == 0

    def body(k_ref, c_ref, s_ref, o_ref):
        k = _f32(k_ref)
        k = k + pltpu.roll(k, QK_ROPE, 1)
        o_ref[...] = _rope(k, c_ref[...], s_ref[...]).astype(o_ref.dtype)

    row = pl.BlockSpec((tr, LANES), lambda i: (i, 0))
    return pl.pallas_call(body, name="rope_k", grid=(s // tr,),
                          in_specs=[pl.BlockSpec((tr, LANES), lambda i: (i, off // LANES)), row, row], out_specs=row,
                          out_shape=jax.ShapeDtypeStruct((s, LANES), BF16), compiler_params=_params("parallel"))(z, cos4, sin4)


def _dot_nt(a, b):
    return lax.dot_general(a, b, (((1,), (1,)), ((), ())), preferred_element_type=F32)


def _dot_tn(a, b):
    return lax.dot_general(a, b, (((0,), (0,)), ((), ())), preferred_element_type=F32)


def _q_cat(q_n, qpr, hh):
    lane = lax.broadcasted_iota(jnp.int32, qpr.shape, 1)
    sel = (lane < QK_ROPE) if hh == 0 else (lane >= QK_ROPE)
    return jnp.concatenate([q_n, jnp.where(sel, qpr, jnp.zeros_like(qpr))], axis=1)


def _causal(sc):
    row = lax.broadcasted_iota(jnp.int32, sc.shape, 0)
    col = lax.broadcasted_iota(jnp.int32, sc.shape, 1)
    return jnp.where(col <= row, sc, _NEG)


def attn_fwd(qn, qp, kv, kpr, cos4, sin4):
    s = qn.shape[0]
    hp = HEADS // 2
    t = _tile(s, ATT_TILE)
    nq = s // t

    def body(qn_ref, qp_ref, kv_ref, kp_ref, c_ref, s_ref, o_ref, qpr_ref, l_ref, kcat_ref):
        qi = pl.program_id(1)

        @pl.when(qi == 0)
        def _():
            for hh in range(2):
                kcat_ref[hh, :, 0:QK_NOPE] = kv_ref[:, 2 * hh * QK_NOPE:(2 * hh + 1) * QK_NOPE]
                kcat_ref[hh, :, QK_NOPE:] = kp_ref[...]

        qpr = _rope(qp_ref[...], c_ref[...], s_ref[...]).astype(BF16)
        qpr_ref[...] = qpr
        qcat = [_q_cat(qn_ref[:, hh * QK_NOPE:(hh + 1) * QK_NOPE], qpr, hh) for hh in range(2)]

        def block(kb, carry, diagonal):
            rows = pl.ds(pl.multiple_of(kb * t, t), t)
            out = []
            for hh in range(2):
                m, l, acc = carry[hh]
                sc = _dot_nt(qcat[hh], kcat_ref[hh, rows, :]) * _ATT_SCALE
                if diagonal:
                    sc = _causal(sc)
                m_new = jnp.maximum(m, jnp.max(sc, axis=-1, keepdims=True))
                alpha = jnp.exp(m - m_new)
                p = jnp.exp(sc - m_new)
                l = alpha * l + jnp.sum(p, axis=-1, keepdims=True)
                v = kv_ref[rows, (2 * hh + 1) * QK_NOPE:(2 * hh + 2) * QK_NOPE]
                acc = alpha * acc + jnp.dot(p.astype(BF16), v, preferred_element_type=F32)
                out.append((m_new, l, acc))
            return tuple(out)

        one = (jnp.full((t, 1), _NEG, F32), jnp.zeros((t, 1), F32), jnp.zeros((t, V_HEAD), F32))
        carry = lax.fori_loop(0, qi, lambda kb, cr: block(kb, cr, False), (one, one))
        carry = block(qi, carry, True)
        for hh in range(2):
            m, l, acc = carry[hh]
            o_ref[:, hh * V_HEAD:(hh + 1) * V_HEAD] = (acc / l).astype(o_ref.dtype)
            l_ref[:, hh:hh + 1] = m + jnp.log(l)

    return pl.pallas_call(
        body, name="attn_fwd", grid=(hp, nq),
        in_specs=[pl.BlockSpec((t, 2 * QK_NOPE), lambda h, i: (i, h)), pl.BlockSpec((t, LANES), lambda h, i: (i, h)),
                  pl.BlockSpec((s, 4 * QK_NOPE), lambda h, i: (0, h)), _full((s, LANES)),
                  pl.BlockSpec((t, LANES), lambda h, i: (i, 0)), pl.BlockSpec((t, LANES), lambda h, i: (i, 0))],
        out_specs=[pl.BlockSpec((t, 2 * V_HEAD), lambda h, i: (i, h)), pl.BlockSpec((t, LANES), lambda h, i: (i, h)),
                   pl.BlockSpec((None, t, 2), lambda h, i: (h, i, 0))],
        out_shape=[jax.ShapeDtypeStruct((s, HEADS * V_HEAD), ACT), jax.ShapeDtypeStruct((s, HEADS * QK_ROPE), BF16),
                   jax.ShapeDtypeStruct((hp, s, 2), F32)],
        scratch_shapes=[pltpu.VMEM((2, s, 2 * QK_NOPE), BF16)],
        compiler_params=_params("parallel", "arbitrary"))(qn, qp, kv, kpr, cos4, sin4)


def attn_bwd(qn, qpr, kv, kpr, o, do, lse, cos4, sin4):
    s = qn.shape[0]
    hp = HEADS // 2
    t = _tile(s, ATT_TILE)
    nk = s // t

    def body(qn_ref, qpr_ref, kv_ref, kp_ref, o_ref, do_ref, l_ref, c_ref, s_ref,
             dqn_ref, dqp_ref, dkv_ref, dkp_ref, qcat_ref, dq_ref, delta_ref):
        ki = pl.program_id(1)

        @pl.when(ki == 0)
        def _():
            dq_ref[...] = jnp.zeros_like(dq_ref)
            for hh in range(2):
                qcat_ref[hh] = _q_cat(qn_ref[:, hh * QK_NOPE:(hh + 1) * QK_NOPE], qpr_ref[...], hh)
                cols = slice(hh * V_HEAD, (hh + 1) * V_HEAD)
                delta_ref[hh] = jnp.sum(do_ref[:, cols].astype(F32) * o_ref[:, cols].astype(F32), axis=-1, keepdims=True)

        rows_k = pl.ds(pl.multiple_of(ki * t, t), t)
        kcat = [jnp.concatenate([kv_ref[rows_k, 2 * hh * QK_NOPE:(2 * hh + 1) * QK_NOPE], kp_ref[rows_k, :]], axis=1) for hh in range(2)]
        vs = [kv_ref[rows_k, (2 * hh + 1) * QK_NOPE:(2 * hh + 2) * QK_NOPE] for hh in range(2)]

        def block(qb, carry, diagonal):
            rows = pl.ds(pl.multiple_of(qb * t, t), t)
            out = []
            for hh in range(2):
                dkc, dv = carry[hh]
                q_c = qcat_ref[hh, rows, :]
                do_b = do_ref[rows, hh * V_HEAD:(hh + 1) * V_HEAD].astype(BF16)
                sc = _dot_nt(q_c, kcat[hh]) * _ATT_SCALE
                if diagonal:
                    sc = _causal(sc)
                p = jnp.exp(sc - l_ref[rows, hh:hh + 1])
                dpv = _dot_nt(do_b, vs[hh])
                ds = (p * (dpv - delta_ref[hh, rows, :]) * _ATT_SCALE).astype(BF16)
                dv = dv + _dot_tn(p.astype(BF16), do_b)
                dkc = dkc + _dot_tn(ds, q_c)
                dq_ref[hh, rows, :] += jnp.dot(ds, kcat[hh], preferred_element_type=F32)
                out.append((dkc, dv))
            return tuple(out)

        one = (jnp.zeros((t, 2 * QK_NOPE), F32), jnp.zeros((t, V_HEAD), F32))
        carry = block(ki, (one, one), True)
        carry = lax.fori_loop(ki + 1, nk, lambda qb, cr: block(qb, cr, False), carry)
        dkp = jnp.zeros((t, LANES), F32)
        for hh in range(2):
            dkc, dv = carry[hh]
            dkv_ref[:, 2 * hh * QK_NOPE:(2 * hh + 1) * QK_NOPE] = dkc[:, :QK_NOPE].astype(dkv_ref.dtype)
            dkv_ref[:, (2 * hh + 1) * QK_NOPE:(2 * hh + 2) * QK_NOPE] = dv.astype(dkv_ref.dtype)
            dkp = dkp + dkc[:, QK_NOPE:]
        dkp_ref[...] = dkp

        @pl.when(ki == nk - 1)
        def _():
            lane = lax.broadcasted_iota(jnp.int32, (s, LANES), 1)
            dqp = jnp.where(lane < QK_ROPE, dq_ref[0, :, QK_NOPE:], dq_ref[1, :, QK_NOPE:])
            dqp_ref[...] = _rope(dqp, c_ref[...], -s_ref[...]).astype(dqp_ref.dtype)
            for hh in range(2):
                dqn_ref[:, hh * QK_NOPE:(hh + 1) * QK_NOPE] = dq_ref[hh, :, :QK_NOPE].astype(dqn_ref.dtype)

    qblk = pl.BlockSpec((s, 2 * QK_NOPE), lambda h, i: (0, h))
    pblk = pl.BlockSpec((s, LANES), lambda h, i: (0, h))
    tab = _full((s, LANES))
    return pl.pallas_call(
        body, name="attn_bwd", grid=(hp, nk),
        in_specs=[qblk, pblk, pl.BlockSpec((s, 4 * QK_NOPE), lambda h, i: (0, h)), tab, qblk, qblk,
                  pl.BlockSpec((None, s, 2), lambda h, i: (h, 0, 0)), tab, tab],
        out_specs=[qblk, pblk, pl.BlockSpec((t, 4 * QK_NOPE), lambda h, i: (i, h)), pl.BlockSpec((None, t, LANES), lambda h, i: (h, i, 0))],
        out_shape=[jax.ShapeDtypeStruct((s, HEADS * QK_NOPE), BF16), jax.ShapeDtypeStruct((s, HEADS * QK_ROPE), BF16),
                   jax.ShapeDtypeStruct((s, HEADS * 2 * QK_NOPE), BF16), jax.ShapeDtypeStruct((hp, s, LANES), F32)],
        scratch_shapes=[pltpu.VMEM((2, s, 2 * QK_NOPE), BF16), pltpu.VMEM((2, s, 2 * QK_NOPE), F32), pltpu.VMEM((2, s, 1), F32)],
        compiler_params=_params("parallel", "arbitrary"))(qn, qpr, kv, kpr, o, do, lse, cos4, sin4)


def kpe_bwd(dkp, cos4, sin4, pad_cols):
    hp, s, _ = dkp.shape
    tr = _tile(s, ROW_TILE * 2)

    def body(d_ref, c_ref, s_ref, o_ref):
        tot = d_ref[0]
        for h in range(1, hp):
            tot = tot + d_ref[h]
        tot = tot + pltpu.roll(tot, QK_ROPE, 1)
        lane = lax.broadcasted_iota(jnp.int32, tot.shape, 1)
        dk = jnp.where(lane < QK_ROPE, _rope(tot, c_ref[...], -s_ref[...]), jnp.zeros_like(tot))
        o_ref[...] = jnp.zeros_like(o_ref)
        o_ref[:, 0:LANES] = dk.astype(o_ref.dtype)

    row = pl.BlockSpec((tr, LANES), lambda i: (i, 0))
    return pl.pallas_call(body, name="kpe_bwd", grid=(s // tr,),
                          in_specs=[pl.BlockSpec((hp, tr, LANES), lambda i: (0, i, 0)), row, row],
                          out_specs=pl.BlockSpec((tr, pad_cols), lambda i: (i, 0)),
                          out_shape=jax.ShapeDtypeStruct((s, pad_cols), BF16), compiler_params=_params("parallel"))(dkp, cos4, sin4)


def _shift_down(x, n):
    row = lax.broadcasted_iota(jnp.int32, x.shape, 0)
    return jnp.where(row >= n, pltpu.roll(x, n, 0), jnp.zeros_like(x))


def _shift_up(x, n):
    rows = x.shape[0]
    row = lax.broadcasted_iota(jnp.int32, x.shape, 0)
    return jnp.where(row < rows - n, pltpu.roll(x, rows - n, 0), jnp.zeros_like(x))


def _conv(x, w_ref, b_ref):
    return w_ref[2:3, :] * x + w_ref[1:2, :] * _shift_down(x, 1) + w_ref[0:1, :] * _shift_down(x, 2) + b_ref[...]


def conv_act_fwd(upre, conv_w, conv_b):
    s, f2 = upre.shape
    f = f2 // 2
    tc = _tile(f, COL_TILE)
    nc = f // tc

    def body(ug_ref, uv_ref, wg_ref, wv_ref, bg_ref, bv_ref, o_ref, gv_ref):
        gh = _conv(_f32(ug_ref), wg_ref, bg_ref)
        vh = _conv(_f32(uv_ref), wv_ref, bv_ref)
        o_ref[...] = (gh * _sigmoid(gh) * vh).astype(o_ref.dtype)
        gv_ref[0] = gh.astype(gv_ref.dtype)
        gv_ref[1] = vh.astype(gv_ref.dtype)

    def spec(rows, shift):
        return pl.BlockSpec((rows, tc), lambda j: (0, j + shift))

    return pl.pallas_call(
        body, name="conv_act_fwd", grid=(nc,),
        in_specs=[spec(s, 0), spec(s, nc), spec(3, 0), spec(3, nc), spec(1, 0), spec(1, nc)],
        out_specs=[spec(s, 0), pl.BlockSpec((2, s, tc), lambda j: (0, 0, j))],
        out_shape=[jax.ShapeDtypeStruct((s, f), BF16), jax.ShapeDtypeStruct((2, s, f), ACT)],
        compiler_params=_params("parallel"))(upre, upre, conv_w, conv_w, conv_b, conv_b)


def conv_act_bwd(upre, conv_w, gv, df):
    s, f2 = upre.shape
    f = f2 // 2
    tc = _tile(f, COL_TILE)
    nc = f // tc

    def half(x, d, w_ref, du_ref, which, gw_ref, gb_ref):
        d1, d2 = _shift_up(d, 1), _shift_up(d, 2)
        gb_ref[...] = _colsum(d)
        gw_ref[2:3, :] = _colsum(d * x)
        gw_ref[1:2, :] = _colsum(d1 * x)
        gw_ref[0:1, :] = _colsum(d2 * x)
        du_ref[which] = (w_ref[2:3, :] * d + w_ref[1:2, :] * d1 + w_ref[0:1, :] * d2).astype(du_ref.dtype)

    def body(ug_ref, uv_ref, wg_ref, wv_ref, gv_ref, df_ref, du_ref, gwg_ref, gwv_ref, gbg_ref, gbv_ref):
        xg, xv = _f32(ug_ref), _f32(uv_ref)
        gh, vh = gv_ref[0].astype(F32), gv_ref[1].astype(F32)
        sg = _sigmoid(gh)
        df_v = _f32(df_ref)
        half(xg, df_v * vh * (sg * (1.0 + gh * (1.0 - sg))), wg_ref, du_ref, 0, gwg_ref, gbg_ref)
        half(xv, df_v * (gh * sg), wv_ref, du_ref, 1, gwv_ref, gbv_ref)

    def spec(rows, shift):
        return pl.BlockSpec((rows, tc), lambda j: (0, j + shift))

    gw = jax.ShapeDtypeStruct((3, f), F32)
    gb = jax.ShapeDtypeStruct((1, f), F32)
    return pl.pallas_call(
        body, name="conv_act_bwd", grid=(nc,),
        in_specs=[spec(s, 0), spec(s, nc), spec(3, 0), spec(3, nc), pl.BlockSpec((2, s, tc), lambda j: (0, 0, j)), spec(s, 0)],
        out_specs=[pl.BlockSpec((2, s, tc), lambda j: (0, 0, j)), spec(3, 0), spec(3, 0), spec(1, 0), spec(1, 0)],
        out_shape=[jax.ShapeDtypeStruct((2, s, f), BF16), gw, gw, gb, gb],
        compiler_params=_params("parallel"))(upre, upre, conv_w, conv_w, gv, df)


def _elementwise_tile(r, c, limit):
    if r % 8:
        return r, c
    best = (8, c if c % LANES else LANES)
    for k in (1, 2, 4, 8, 16):
        if k > 1 and c % (LANES * k):
            continue
        tc = c // k
        tr = max(8, min(r, limit // tc) // 8 * 8)
        while r % tr:
            tr -= 8
        if tr * tc <= max(limit, 8 * tc) and tr * tc > best[0] * best[1]:
            best = (tr, tc)
    return best


def adamw(name, w, m, v, parts):
    npart, r, c = parts.shape
    tr, tc = _elementwise_tile(r, c, ADAMW_TILE_ELEMS)
    bc1 = 1.0 - ADAM_B1 ** ADAM_STEP
    bc2 = 1.0 - ADAM_B2 ** ADAM_STEP

    def body(w_ref, m_ref, v_ref, p_ref, g_ref, d_ref, nm_ref, nv_ref):
        g = p_ref[0].astype(F32)
        for k in range(1, npart):
            g = g + p_ref[k].astype(F32)
        m_new = ADAM_B1 * m_ref[...] + (1.0 - ADAM_B1) * g
        v_new = ADAM_B2 * v_ref[...] + (1.0 - ADAM_B2) * (g * g)
        g_ref[...] = g
        nm_ref[...] = m_new
        nv_ref[...] = v_new
        d_ref[...] = -ADAM_LR * ((m_new / bc1) / (jnp.sqrt(v_new / bc2) + ADAM_EPS) + ADAM_WD * w_ref[...])

    deps = _TOKENS.take()
    blk = pl.BlockSpec((tr, tc), lambda i, j: (i, j))
    out = jax.ShapeDtypeStruct((r, c), F32)
    return pl.pallas_call(
        lambda *refs: body(*refs[:4], *refs[4 + len(deps):]), name=name, grid=(r // tr, c // tc),
        in_specs=[blk, blk, blk, pl.BlockSpec((npart, tr, tc), lambda i, j: (0, i, j))] + [pl.BlockSpec(memory_space=pl.ANY)] * len(deps),
        out_specs=[blk, blk, blk, blk], out_shape=[out, out, out, out],
        compiler_params=_params("parallel", "parallel"))(w, m, v, parts, *deps)


def _position():
    return lax.axis_index("x"), lax.axis_index("y"), lax.axis_index("c")


def _index(p):
    return 4 * p[0] + 2 * p[1] + p[2]


def _peer(me, r):
    return (me[0] ^ ((r >> 2) & 1), me[1] ^ ((r >> 1) & 1), me[2] ^ (r & 1))


_ANY = pl.BlockSpec(memory_space=pl.ANY)


_HBM = pl.BlockSpec(memory_space=pltpu.HBM)
_SEM = pl.BlockSpec(memory_space=pltpu.SEMAPHORE)
_EFFECT = pltpu.SideEffectType.DATAFLOW_SIDE_EFFECTING
_TOKEN = jax.ShapeDtypeStruct((8, LANES), F32)
_VM = pl.BlockSpec(memory_space=pltpu.VMEM)
_SIDE = pltpu.CompilerParams(has_side_effects=_EFFECT)


def _hbm(a):
    return pltpu.with_memory_space_constraint(a, pltpu.HBM)


def _like(a):
    return pltpu.HBM(a.shape, a.dtype)


def _dma_sems(n):
    return pltpu.SemaphoreType.DMA((n,))


def _other_chips(x, y):
    return [(1 - x, y), (x, 1 - y), (1 - x, 1 - y)]


COPY_STREAMS = 8


def _row_chunks(src, dst):
    rows = src.shape[0]
    n = COPY_STREAMS
    while n > 1 and rows % (16 * n):
        n //= 2
    r = rows // n
    return [(src.at[pl.ds(i * r, r)], dst.at[pl.ds(i * r, r)]) for i in range(n)]


class _rcopy:
    def __init__(self, src, dst, send_sem, recv_sem, to):
        self.parts = [pltpu.make_async_remote_copy(src_ref=s, dst_ref=d, send_sem=send_sem, recv_sem=recv_sem, device_id=to, device_id_type=MESH)
                      for s, d in _row_chunks(src, dst)]

    def start(self):
        for cp in self.parts:
            cp.start()

    def wait_send(self):
        for cp in self.parts:
            cp.wait_send()

    def wait_recv(self):
        for cp in self.parts:
            cp.wait_recv()


def _afters(after):
    return list(after) if isinstance(after, (list, tuple)) else [after]


def ag_start(name, shards, after):
    n = len(shards)
    lands = [lax.empty((N_DEV,) + a.shape, a.dtype) for a in shards]
    afters = _afters(after)
    na = len(afters)

    def body(*refs):
        ins, lnd, send_sems, recv_sems, token = refs[:n], refs[n:2 * n], refs[2 * n + na], refs[2 * n + na + 1], refs[4 * n + na + 2]
        x, y, c = _position()
        for w in range(n):
            slot = lnd[w].at[_index((x, y, c))]
            for k, to in enumerate([(x, y, 1 - c)] + [(*chip, c) for chip in _other_chips(x, y)]):
                _rcopy(ins[w], slot, send_sems.at[4 * w + k], recv_sems.at[4 * w + k], to).start()
        token[...] = jnp.zeros_like(token)

    out = pl.pallas_call(
        body, name=name, out_shape=(_dma_sems(4 * n), _dma_sems(4 * n)) + tuple(_like(a) for a in shards + lands) + (_TOKEN,),
        in_specs=[_HBM] * (2 * n) + [_ANY] * na, out_specs=(_SEM, _SEM) + (_HBM,) * (2 * n) + (_VM,),
        input_output_aliases={i: 2 + i for i in range(2 * n)}, compiler_params=_SIDE)(*[_hbm(a) for a in shards + lands], *afters)
    _TOKENS.push(out[-1])
    return out[0], out[1], list(out[2:2 + n]), list(out[2 + n:2 + 2 * n])


def _split_rows(ref):
    rows = ref.shape[0]
    h = rows // 32 * 16
    return ref.at[pl.ds(0, h)], ref.at[pl.ds(h, rows - h)]


def relay_start(name, shards, after):
    n = len(shards)
    lands = [lax.empty((N_DEV,) + a.shape, a.dtype) for a in shards]
    afters = _afters(after)
    na = len(afters)

    def body(*refs):
        ins, lnd, send_sems, recv_sems, token = refs[:n], refs[n:2 * n], refs[2 * n + na], refs[2 * n + na + 1], refs[4 * n + na + 2]
        x, y, c = _position()
        for w in range(n):
            slot = lnd[w].at[_index((x, y, c))]
            for k, to in enumerate([(x, y, 1 - c), (1 - x, y, c), (x, 1 - y, c)]):
                _rcopy(ins[w], slot, send_sems.at[3 * w + k], recv_sems.at[3 * w + k], to).start()
        token[...] = jnp.zeros_like(token)

    out = pl.pallas_call(
        body, name=name, out_shape=(_dma_sems(3 * n), _dma_sems(3 * n)) + tuple(_like(a) for a in shards + lands) + (_TOKEN,),
        in_specs=[_HBM] * (2 * n) + [_ANY] * na, out_specs=(_SEM, _SEM) + (_HBM,) * (2 * n) + (_VM,),
        input_output_aliases={i: 2 + i for i in range(2 * n)}, compiler_params=_SIDE)(*[_hbm(a) for a in shards + lands], *afters)
    _TOKENS.push(out[-1])
    return out[0], out[1], list(out[2:2 + n]), list(out[2 + n:2 + 2 * n])


def relay_pass(name, started, after):
    send, recv, shards, lands = started
    n = len(shards)
    afters = _afters(after)
    na = len(afters)

    def body(*refs):
        ins, lnd, send_sems, recv_sems = refs[:n], refs[n:2 * n], refs[2 * n], refs[2 * n + 1]
        fsend, frecv, psend, precv = refs[2 * n + 2 + na:2 * n + 6 + na]
        token = refs[4 * n + 6 + na]
        x, y, c = _position()
        nbrs = [(1 - x, y, c), (x, 1 - y, c)]
        for w in range(n):
            for j, nbr in enumerate(nbrs):
                slot = lnd[w].at[_index(nbr)]
                _rcopy(ins[w], slot, send_sems.at[3 * w + 1 + j], recv_sems.at[3 * w + 1 + j], nbr).wait_recv()
                _rcopy(slot, slot, fsend.at[2 * w + j], frecv.at[2 * w + j], (x, y, 1 - c)).start()
                part = _split_rows(slot)[j]
                _rcopy(part, part, psend.at[2 * w + j], precv.at[2 * w + j], nbrs[1 - j]).start()
        token[...] = jnp.zeros_like(token)

    out = pl.pallas_call(
        body, name=name, out_shape=(_dma_sems(2 * n),) * 4 + tuple(_like(a) for a in shards + lands) + (_TOKEN,),
        in_specs=[_HBM] * (2 * n) + [_SEM, _SEM] + [_ANY] * na, out_specs=(_SEM,) * 4 + (_HBM,) * (2 * n) + (_VM,),
        input_output_aliases={i: 4 + i for i in range(2 * n)}, compiler_params=_SIDE)(*shards, *lands, send, recv, *afters)
    _TOKENS.push(out[-1])
    return (send, recv) + tuple(out[:4]) + (list(out[4:4 + n]), list(out[4 + n:4 + 2 * n]))


def relay_forward(name, passed, after):
    send, recv, fsend, frecv, psend, precv, shards, lands = passed
    n = len(shards)
    afters = _afters(after)
    na = len(afters)

    def body(*refs):
        ins, lnd, precv_r = refs[:n], refs[n:2 * n], refs[2 * n]
        gsend, grecv, token = refs[2 * n + 1 + na], refs[2 * n + 2 + na], refs[4 * n + 3 + na]
        x, y, c = _position()
        for w in range(n):
            slot = lnd[w].at[_index((1 - x, 1 - y, c))]
            for j, part in enumerate(_split_rows(slot)):
                _rcopy(part, part, precv_r.at[2 * w + j], precv_r.at[2 * w + j], (x, y, 1 - c)).wait_recv()
            _rcopy(slot, slot, gsend.at[w], grecv.at[w], (x, y, 1 - c)).start()
        token[...] = jnp.zeros_like(token)

    out = pl.pallas_call(
        body, name=name, out_shape=(_dma_sems(n), _dma_sems(n)) + tuple(_like(a) for a in shards + lands) + (_TOKEN,),
        in_specs=[_HBM] * (2 * n) + [_SEM] + [_ANY] * na, out_specs=(_SEM, _SEM) + (_HBM,) * (2 * n) + (_VM,),
        input_output_aliases={i: 2 + i for i in range(2 * n)}, compiler_params=_SIDE)(*shards, *lands, precv, *afters)
    _TOKENS.push(out[-1])
    return send, recv, fsend, frecv, psend, out[0], out[1], list(out[2:2 + n]), list(out[2 + n:2 + 2 * n])


def relay_wait(name, forwarded, after):
    send, recv, fsend, frecv, psend, gsend, grecv, shards, lands = forwarded
    n = len(shards)

    def body(*refs):
        ins, lnd = refs[:n], refs[n:2 * n]
        send_sems, recv_sems, fsend_r, frecv_r, psend_r, gsend_r, grecv_r = refs[2 * n:2 * n + 7]
        x, y, c = _position()
        sibling = (x, y, 1 - c)
        for w in range(n):
            own = lnd[w].at[_index((x, y, c))]
            _rcopy(ins[w], lnd[w].at[_index(sibling)], send_sems.at[3 * w], recv_sems.at[3 * w], sibling).wait_recv()
            for j, nbr in enumerate([(1 - x, y, 1 - c), (x, 1 - y, 1 - c)]):
                _rcopy(ins[w], lnd[w].at[_index(nbr)], fsend_r.at[2 * w + j], frecv_r.at[2 * w + j], sibling).wait_recv()
            _rcopy(ins[w], lnd[w].at[_index((1 - x, 1 - y, 1 - c))], gsend_r.at[w], grecv_r.at[w], sibling).wait_recv()
            for k in range(3):
                _rcopy(ins[w], own, send_sems.at[3 * w + k], recv_sems.at[3 * w + k], sibling).wait_send()
            for j in range(2):
                _rcopy(ins[w], own, fsend_r.at[2 * w + j], frecv_r.at[2 * w + j], sibling).wait_send()
                part = _split_rows(own)[j]
                _rcopy(part, part, psend_r.at[2 * w + j], psend_r.at[2 * w + j], sibling).wait_send()
            _rcopy(ins[w], own, gsend_r.at[w], grecv_r.at[w], sibling).wait_send()

    out = pl.pallas_call(
        body, name=name, out_shape=tuple(_like(a) for a in shards + lands),
        in_specs=[_HBM] * (2 * n) + [_SEM] * 7 + [_ANY] * len(_afters(after)),
        out_specs=(_HBM,) * (2 * n), input_output_aliases={i: i for i in range(2 * n)},
        compiler_params=_SIDE)(*shards, *lands, send, recv, fsend, frecv, psend, gsend, grecv, *_afters(after))
    return [lax.dynamic_update_index_in_dim(land, shard, _index(_position()), 0) for shard, land in zip(out[:n], out[n:])]


def ag_forward(name, started, after):
    send, recv, shards, lands = started
    n = len(shards)
    afters = list(after) if isinstance(after, (list, tuple)) else [after]
    na = len(afters)

    def body(*refs):
        ins, lnd, send_sems, recv_sems = refs[:n], refs[n:2 * n], refs[2 * n], refs[2 * n + 1]
        fsend, frecv, token = refs[2 * n + 2 + na], refs[2 * n + 3 + na], refs[4 * n + 4 + na]
        x, y, c = _position()
        for w in range(n):
            for j, chip in enumerate(_other_chips(x, y)):
                slot = lnd[w].at[_index((*chip, c))]
                _rcopy(ins[w], slot, send_sems.at[4 * w + 1 + j], recv_sems.at[4 * w + 1 + j], (*chip, c)).wait_recv()
                _rcopy(slot, slot, fsend.at[3 * w + j], frecv.at[3 * w + j], (x, y, 1 - c)).start()
        token[...] = jnp.zeros_like(token)

    out = pl.pallas_call(
        body, name=name, out_shape=(_dma_sems(3 * n), _dma_sems(3 * n)) + tuple(_like(a) for a in shards + lands) + (_TOKEN,),
        in_specs=[_HBM] * (2 * n) + [_SEM, _SEM] + [_ANY] * na, out_specs=(_SEM, _SEM) + (_HBM,) * (2 * n) + (_VM,),
        input_output_aliases={i: 2 + i for i in range(2 * n)}, compiler_params=_SIDE)(*shards, *lands, send, recv, *afters)
    _TOKENS.push(out[-1])
    return send, recv, out[0], out[1], list(out[2:2 + n]), list(out[2 + n:2 + 2 * n])


def ag_wait(name, forwarded, after):
    send, recv, fsend, frecv, shards, lands = forwarded
    n = len(shards)

    def body(*refs):
        ins, lnd, send_sems, recv_sems, fsend_r, frecv_r = refs[:n], refs[n:2 * n], refs[2 * n], refs[2 * n + 1], refs[2 * n + 2], refs[2 * n + 3]
        x, y, c = _position()
        sibling = (x, y, 1 - c)
        for w in range(n):
            own = lnd[w].at[_index((x, y, c))]
            _rcopy(ins[w], lnd[w].at[_index(sibling)], send_sems.at[4 * w], recv_sems.at[4 * w], sibling).wait_recv()
            for j, chip in enumerate(_other_chips(x, y)):
                _rcopy(ins[w], lnd[w].at[_index((*chip, 1 - c))], fsend_r.at[3 * w + j], frecv_r.at[3 * w + j], sibling).wait_recv()
            for k in range(4):
                _rcopy(ins[w], own, send_sems.at[4 * w + k], recv_sems.at[4 * w + k], sibling).wait_send()
            for j in range(3):
                _rcopy(ins[w], own, fsend_r.at[3 * w + j], frecv_r.at[3 * w + j], sibling).wait_send()

    out = pl.pallas_call(
        body, name=name, out_shape=tuple(_like(a) for a in shards + lands),
        in_specs=[_HBM] * (2 * n) + [_SEM] * 4 + [_ANY] * len(_afters(after)),
        out_specs=(_HBM,) * (2 * n), input_output_aliases={i: i for i in range(2 * n)},
        compiler_params=_SIDE)(*shards, *lands, send, recv, fsend, frecv, *_afters(after))
    return [lax.dynamic_update_index_in_dim(land, shard, _index(_position()), 0) for shard, land in zip(out[:n], out[n:])]


def rs_d2d_start(name, grads):
    n = len(grads)
    lands = [lax.empty((4,) + g.shape[1:], g.dtype) for g in grads]

    def body(*refs):
        ins, lnd, send_sems, recv_sems, token = refs[:n], refs[n:2 * n], refs[2 * n], refs[2 * n + 1], refs[4 * n + 2]
        x, y, c = _position()
        for w in range(n):
            for i in range(4):
                _rcopy(ins[w].at[2 * i + 1 - c], lnd[w].at[i], send_sems.at[4 * w + i], recv_sems.at[4 * w + i], (x, y, 1 - c)).start()
        token[...] = jnp.zeros_like(token)

    out = pl.pallas_call(
        body, name=name, out_shape=(_dma_sems(4 * n), _dma_sems(4 * n)) + tuple(_like(a) for a in grads + lands) + (_TOKEN,),
        in_specs=[_HBM] * (2 * n), out_specs=(_SEM, _SEM) + (_HBM,) * (2 * n) + (_VM,),
        input_output_aliases={i: 2 + i for i in range(2 * n)}, compiler_params=_SIDE)(*[_hbm(a) for a in grads + lands])
    _TOKENS.push(out[-1])
    return out[0], out[1], list(out[2:2 + n]), list(out[2 + n:2 + 2 * n])


def rs_d2d_wait(name, started, after):
    send, recv, grads, lands = started
    n = len(grads)

    def body(*refs):
        ins, lnd, send_sems, recv_sems = refs[:n], refs[n:2 * n], refs[2 * n], refs[2 * n + 1]
        x, y, c = _position()
        for w in range(n):
            for i in range(4):
                cp = _rcopy(ins[w].at[2 * i + 1 - c], lnd[w].at[i], send_sems.at[4 * w + i], recv_sems.at[4 * w + i], (x, y, 1 - c))
                cp.wait_send()
                cp.wait_recv()

    out = pl.pallas_call(
        body, name=name, out_shape=tuple(_like(a) for a in grads + lands),
        in_specs=[_HBM] * (2 * n) + [_SEM, _SEM] + [_ANY] * len(_afters(after)),
        out_specs=(_HBM,) * (2 * n), input_output_aliases={i: i for i in range(2 * n)},
        compiler_params=_SIDE)(*grads, *lands, send, recv, *_afters(after))
    return list(out[:n]), list(out[n:])


def pair_sum(name, grad, land, core):
    _, r, c = grad.shape
    tr = r
    if r % 8 == 0:
        tr = max(8, min(r, 4 * ADAMW_TILE_ELEMS // c) // 8 * 8)
        while r % tr:
            tr -= 8

    def body(core_ref, a_ref, b_ref, o_ref):
        o_ref[...] = (a_ref[...].astype(F32) + b_ref[...].astype(F32)).astype(o_ref.dtype)

    return pl.pallas_call(
        body, name=name, out_shape=jax.ShapeDtypeStruct((4, r, c), grad.dtype),
        grid_spec=pltpu.PrefetchScalarGridSpec(
            num_scalar_prefetch=1, grid=(4, r // tr),
            in_specs=[pl.BlockSpec((None, None, tr, c), lambda i, j, core_ref: (i, core_ref[0], j, 0)),
                      pl.BlockSpec((None, tr, c), lambda i, j, core_ref: (i, j, 0))],
            out_specs=pl.BlockSpec((None, tr, c), lambda i, j, core_ref: (i, j, 0))),
        compiler_params=_params("parallel", "parallel"))(core, grad.reshape(4, 2, r, c), land)


def rs_ici_start(name, sums):
    n = len(sums)
    lands = [lax.empty(a.shape, a.dtype) for a in sums]

    def body(*refs):
        ins, lnd, send_sems, recv_sems, token = refs[:n], refs[n:2 * n], refs[2 * n], refs[2 * n + 1], refs[4 * n + 2]
        x, y, c = _position()
        chip = 2 * x + y
        for w in range(n):
            for j, other in enumerate(_other_chips(x, y)):
                _rcopy(ins[w].at[2 * other[0] + other[1]], lnd[w].at[chip], send_sems.at[3 * w + j], recv_sems.at[3 * w + j], (*other, c)).start()
        token[...] = jnp.zeros_like(token)

    out = pl.pallas_call(
        body, name=name, out_shape=(_dma_sems(3 * n), _dma_sems(3 * n)) + tuple(_like(a) for a in sums + lands) + (_TOKEN,),
        in_specs=[_HBM] * (2 * n), out_specs=(_SEM, _SEM) + (_HBM,) * (2 * n) + (_VM,),
        input_output_aliases={i: 2 + i for i in range(2 * n)}, compiler_params=_SIDE)(*[_hbm(a) for a in sums + lands])
    _TOKENS.push(out[-1])
    return out[0], out[1], list(out[2:2 + n]), list(out[2 + n:2 + 2 * n])


def rs_ici_wait(name, started, after):
    send, recv, sums, lands = started
    n = len(sums)

    def body(*refs):
        ins, lnd, send_sems, recv_sems = refs[:n], refs[n:2 * n], refs[2 * n], refs[2 * n + 1]
        x, y, c = _position()
        for w in range(n):
            for j, other in enumerate(_other_chips(x, y)):
                cp = _rcopy(ins[w].at[2 * other[0] + other[1]], lnd[w].at[2 * other[0] + other[1]], send_sems.at[3 * w + j], recv_sems.at[3 * w + j], (*other, c))
                cp.wait_send()
                cp.wait_recv()

    out = pl.pallas_call(
        body, name=name, out_shape=tuple(_like(a) for a in sums + lands), in_specs=[_HBM] * (2 * n) + [_SEM, _SEM, _ANY],
        out_specs=(_HBM,) * (2 * n), input_output_aliases={i: i for i in range(2 * n)}, compiler_params=_SIDE)(*sums, *lands, send, recv, after)
    chip = 2 * lax.axis_index("x") + lax.axis_index("y")
    return [lax.dynamic_update_index_in_dim(land, lax.dynamic_index_in_dim(s, chip, 0, keepdims=False), chip, 0)
            for s, land in zip(out[:n], out[n:])]


def ada_fwd(c, w_ada, b_ada3, conv_w, after):
    d, cs = w_ada.shape

    def body(c_ref, w_ref, b_ref, cw_ref, after_ref, mod_ref, sc_ref, cwa_ref, part_ref, send_sems, recv_sems):
        me = _position()
        my = _index(me)
        cv = c_ref[...]
        sc_ref[my] = cv * _sigmoid(cv)
        cwa_ref[my] = cw_ref[...]
        gather = []
        for r in range(1, N_DEV):
            for k, ref in enumerate((sc_ref, cwa_ref)):
                cp = pltpu.make_async_remote_copy(src_ref=ref.at[my], dst_ref=ref.at[my], send_sem=send_sems.at[14 * k + r - 1],
                                                  recv_sem=recv_sems.at[14 * k + r - 1], device_id=_peer(me, r), device_id_type=MESH)
                cp.start()
                gather.append(cp)
        for cp in gather:
            cp.wait()
        sc_all = jnp.concatenate([sc_ref[k] for k in range(N_DEV)], axis=0).astype(BF16)
        part = jnp.dot(sc_all, w_ref[...].astype(BF16), preferred_element_type=F32)
        for k in range(N_DEV):
            part_ref[k] = part[k:k + 1, :]
        scatter = []
        for r in range(1, N_DEV):
            peer = _peer(me, r)
            cp = pltpu.make_async_remote_copy(src_ref=part_ref.at[_index(peer)], dst_ref=mod_ref.at[my], send_sem=send_sems.at[6 + r],
                                              recv_sem=recv_sems.at[6 + r], device_id=peer, device_id_type=MESH)
            cp.start()
            scatter.append(cp)
        mod_ref[my] = part_ref[my]
        for cp in scatter:
            cp.wait()
        mod_ref[...] = mod_ref[...] + b_ref[...]

    vm = pl.BlockSpec(memory_space=pltpu.VMEM)
    return pl.pallas_call(
        body, name="ada_fwd",
        out_shape=[jax.ShapeDtypeStruct((N_DEV, 1, cs), F32), jax.ShapeDtypeStruct((N_DEV, 1, d), F32),
                   jax.ShapeDtypeStruct((N_DEV,) + conv_w.shape, F32)],
        in_specs=[vm, vm, vm, vm, _ANY], out_specs=[vm, vm, vm],
        scratch_shapes=[pltpu.VMEM((N_DEV, 1, cs), F32), pltpu.SemaphoreType.DMA((21,)), pltpu.SemaphoreType.DMA((21,))],
        compiler_params=pltpu.CompilerParams(vmem_limit_bytes=VMEM_LIMIT_BYTES))(c, w_ada, b_ada3, conv_w, after)


def ada_bwd_w(sc_all, dmod_cols):
    _, d = sc_all.shape
    cs = dmod_cols.shape[1]
    tr = _tile(d, ROW_TILE)

    def body(sc_ref, dm_ref, o_ref):
        dm = dm_ref[...].astype(BF16)
        o_ref[...] = lax.dot_general(sc_ref[...].astype(BF16), dm, (((0,), (0,)), ((), ())), preferred_element_type=F32)

    return pl.pallas_call(body, name="ada_bwd_w", grid=(d // tr,),
                          in_specs=[pl.BlockSpec((N_DEV, tr), lambda i: (0, i)), _full((N_DEV, cs))],
                          out_specs=pl.BlockSpec((None, tr, cs), lambda i: (0, i, 0)),
                          out_shape=jax.ShapeDtypeStruct((1, d, cs), F32), compiler_params=_params("parallel"))(sc_all, dmod_cols)


def _round_up(n, m):
    return (n + m - 1) // m * m


def kernel(x, c, positions, w_ada, b_ada, pre_norm1_g, w_in, gm_ln_g, gm_ln_b, gm_w_s, gm_b_s, w_branch_a, q_norm_g, w_uq, kv_norm_g, w_ukv, w_branch_b, w_out, post_norm1_g, pre_norm2_g, w_up, conv_w, conv_b, w_down, post_norm2_g, loss_target, m_w_ada, m_b_ada, m_pre_norm1_g, m_w_in, m_gm_ln_g, m_gm_ln_b, m_gm_w_s, m_gm_b_s, m_w_branch_a, m_q_norm_g, m_w_uq, m_kv_norm_g, m_w_ukv, m_w_branch_b, m_w_out, m_post_norm1_g, m_pre_norm2_g, m_w_up, m_conv_w, m_conv_b, m_w_down, m_post_norm2_g, v_w_ada, v_b_ada, v_pre_norm1_g, v_w_in, v_gm_ln_g, v_gm_ln_b, v_gm_w_s, v_gm_b_s, v_w_branch_a, v_q_norm_g, v_w_uq, v_kv_norm_g, v_w_ukv, v_w_branch_b, v_w_out, v_post_norm1_g, v_pre_norm2_g, v_w_up, v_conv_w, v_conv_b, v_w_down, v_post_norm2_g):
    weights = dict(w_ada=w_ada, b_ada=b_ada, pre_norm1_g=pre_norm1_g, w_in=w_in, gm_ln_g=gm_ln_g, gm_ln_b=gm_ln_b, gm_w_s=gm_w_s,
                   gm_b_s=gm_b_s, w_branch_a=w_branch_a, q_norm_g=q_norm_g, w_uq=w_uq, kv_norm_g=kv_norm_g, w_ukv=w_ukv,
                   w_branch_b=w_branch_b, w_out=w_out, post_norm1_g=post_norm1_g, pre_norm2_g=pre_norm2_g, w_up=w_up, conv_w=conv_w,
                   conv_b=conv_b, w_down=w_down, post_norm2_g=post_norm2_g)
    mom1 = dict(w_ada=m_w_ada, b_ada=m_b_ada, pre_norm1_g=m_pre_norm1_g, w_in=m_w_in, gm_ln_g=m_gm_ln_g, gm_ln_b=m_gm_ln_b,
                gm_w_s=m_gm_w_s, gm_b_s=m_gm_b_s, w_branch_a=m_w_branch_a, q_norm_g=m_q_norm_g, w_uq=m_w_uq, kv_norm_g=m_kv_norm_g,
                w_ukv=m_w_ukv, w_branch_b=m_w_branch_b, w_out=m_w_out, post_norm1_g=m_post_norm1_g, pre_norm2_g=m_pre_norm2_g,
                w_up=m_w_up, conv_w=m_conv_w, conv_b=m_conv_b, w_down=m_w_down, post_norm2_g=m_post_norm2_g)
    mom2 = dict(w_ada=v_w_ada, b_ada=v_b_ada, pre_norm1_g=v_pre_norm1_g, w_in=v_w_in, gm_ln_g=v_gm_ln_g, gm_ln_b=v_gm_ln_b,
                gm_w_s=v_gm_w_s, gm_b_s=v_gm_b_s, w_branch_a=v_w_branch_a, q_norm_g=v_q_norm_g, w_uq=v_w_uq, kv_norm_g=v_kv_norm_g,
                w_ukv=v_w_ukv, w_branch_b=v_w_branch_b, w_out=v_w_out, post_norm1_g=v_post_norm1_g, pre_norm2_g=v_pre_norm2_g,
                w_up=v_w_up, conv_w=v_conv_w, conv_b=v_conv_b, w_down=v_w_down, post_norm2_g=v_post_norm2_g)
    order = list(weights)
    _TOKENS.clear()

    s, d = x.shape[1], x.shape[2]
    gmw = gm_ln_g.shape[0]
    groups = gmw // CHUNK
    ql, kvl = q_norm_g.shape[0], kv_norm_g.shape[0]
    f2 = conv_b.shape[0]
    in_cols = w_in.shape[1] * N_DEV
    o_q, o_kv, o_ga, o_gb, o_kpe = 2 * gmw, 2 * gmw + ql, 2 * gmw + ql + kvl, 2 * gmw + ql + kvl + d, 2 * gmw + ql + kvl + 2 * d
    zp = _round_up(o_kpe + LANES, Z_PAD)
    src_kpe = 2 * gmw + ql + kvl
    assert src_kpe + QK_ROPE + 2 * d == in_cols
    my = 4 * lax.axis_index("x") + 2 * lax.axis_index("y") + lax.axis_index("c")

    x2, tgt = x[0], loss_target[0]
    row = lambda a: a.reshape(1, -1)

    big = ["w_in", "w_branch_a", "w_uq", "w_ukv", "w_branch_b", "w_out", "w_up", "w_down"]
    sh = {k: weights[k].astype(BF16) for k in big[1:]}
    mix = ["w_branch_a", "w_uq", "w_ukv", "w_branch_b", "w_out"]
    w_in_t = w_in.T.astype(BF16)

    mod8, sc_all3, g_cw = ada_fwd(c, w_ada, b_ada.reshape(N_DEV, 1, -1), conv_w, w_in_t)
    ag_in = relay_start("relay_start_in", [w_in_t], mod8)
    mod = mod8.reshape(N_MOD, d)
    shift1, scale1, gate1, shift2, scale2, gate2 = (mod[i:i + 1] for i in range(N_MOD))
    sc_all = sc_all3.reshape(N_DEV, d)
    h1 = norm_mod_fwd("pre1_fwd", x2, row(pre_norm1_g), scale1, shift1)

    inv = ROPE_THETA ** (-jnp.arange(0, QK_ROPE, 2, dtype=F32) / QK_ROPE)
    ang = positions[0].astype(F32)[:, None] * inv
    cos4 = jnp.tile(jnp.cos(ang), (1, 4))
    sin4 = jnp.tile(jnp.concatenate([-jnp.sin(ang), jnp.sin(ang)], axis=1), (1, 2))

    wm = (gm_w_s * jnp.tril(jnp.ones((CHUNK, CHUNK), F32))).astype(BF16)
    bs3 = gm_b_s.reshape(groups, CHUNK, 1)
    ln_g, ln_b = row(gm_ln_g), row(gm_ln_b)

    small_names = ["pre_norm1_g", "gm_ln_g", "gm_ln_b", "gm_b_s", "q_norm_g", "kv_norm_g", "post_norm1_g", "pre_norm2_g", "conv_b",
                   "post_norm2_g", "gm_w_s", "b_ada"]
    n_small_early = sum(weights[k].size for k in small_names)
    n_pack_early = _round_up(n_small_early + 3 * f2, PACK_ALIGN)

    def pack(src):
        return jnp.concatenate([src[k].reshape(-1) for k in small_names] + [jnp.zeros((n_pack_early - n_small_early,), F32)]).reshape(-1, LANES)

    packed_state = [pack(weights), pack(mom1), pack(mom2)]

    early = [h1, cos4, sin4, wm] + [sh[k] for k in big[1:]] + packed_state
    ag_in = relay_pass("relay_pass_in", ag_in, early)
    ag_in = relay_forward("relay_forward_in", ag_in, _TOKENS.pending[-1])
    ag_mix = ag_start("ag_start_mix", [sh[k] for k in mix], _TOKENS.pending[-1])
    (g_in,) = relay_wait("relay_wait_in", ag_in, [h1, _TOKENS.pending[-1]])
    cs_in = w_in.shape[1]

    def w_in_rows(lo, hi):
        return [g_in[k, max(lo - k * cs_in, 0):min(hi - k * cs_in, cs_in)] for k in range(N_DEV) if lo < (k + 1) * cs_in and hi > k * cs_in]

    w_in_p = jnp.concatenate(w_in_rows(0, src_kpe) + w_in_rows(src_kpe + QK_ROPE, in_cols) + w_in_rows(src_kpe, src_kpe + QK_ROPE)
                             + [jnp.zeros((zp - in_cols, d), BF16)], axis=0)

    z = mm_nt("z_proj", h1, w_in_p, ACT)
    ag_mix = ag_forward("ag_forward_mix", ag_mix, z)
    ag_up = ag_start("ag_start_up", [sh["w_up"]], _TOKENS.pending[-1])
    a = gmlp_fwd(z, gmw, ln_g, ln_b, wm, bs3)
    g_a, g_uq, g_ukv, g_b, g_out = ag_wait("ag_wait_mix", ag_mix, [a, _TOKENS.pending[-1]])
    w_a_f, w_b_f, w_out_f = g_a.reshape(-1, d), g_b.reshape(-1, d), g_out.reshape(-1, d)
    w_uq_f = g_uq.transpose(1, 0, 2).reshape(ql, HEADS, QK_NOPE + QK_ROPE)
    w_uq_n = w_uq_f[:, :, :QK_NOPE].reshape(ql, HEADS * QK_NOPE)
    w_uq_r = w_uq_f[:, :, QK_NOPE:].reshape(ql, HEADS * QK_ROPE)
    y_a = mm_nn("branch_a", a, w_a_f, ACT)
    qln = rms_fwd_cols("q_norm", z, o_q, ql, row(q_norm_g))
    kvn = rms_fwd_cols("kv_norm", z, o_kv, kvl, row(kv_norm_g))
    qn = mm_nn("q_nope", qln, w_uq_n, BF16)
    qp = mm_nn("q_rope", qln, w_uq_r, F32)
    kv = mm_nn_b3("kv_up", kvn, g_ukv, BF16)
    kpr = rope_k(z, o_kpe, cos4, sin4)
    o, qpr, lse = attn_fwd(qn, qp, kv, kpr, cos4, sin4)
    ag_up = ag_forward("ag_forward_up", ag_up, o)
    ag_down = ag_start("ag_start_down", [sh["w_down"]], _TOKENS.pending[-1])
    y_b = mm_nn("branch_b", o, w_b_f, ACT)
    merged = merge_fwd(z, o_ga, o_gb, y_a, y_b)
    y1 = mm_nn("out_proj", merged, w_out_f, ACT)
    x1 = post_res_fwd("post1_fwd", x2, y1, gate1, row(post_norm1_g))
    h2 = norm_mod_fwd("pre2_fwd", x1, row(pre_norm2_g), scale2, shift2)
    (g_up,) = ag_wait("ag_wait_up", ag_up, h2)
    upre = mm_nn_b3("up_proj", h2, g_up, ACT)
    ag_down = ag_forward("ag_forward_down", ag_down, upre)
    cw = g_cw.transpose(1, 0, 2).reshape(3, f2)
    cb = row(conv_b)
    f, gv = conv_act_fwd(upre, cw, cb)
    w_down_f = ag_wait("ag_wait_down", ag_down, f)[0].reshape(-1, d)
    ffn = mm_nn("down_proj", f, w_down_f, ACT)
    loss_acc, dout, dffn, acc2 = post2_loss_bwd(x1, ffn, tgt, gate2, row(post_norm2_g))
    loss = lax.psum(loss_acc[0, 0], ("x", "y", "c"))
    _TOKENS.push(jnp.broadcast_to(loss, (8, LANES)))

    blocks = lambda g: g.reshape(N_DEV, g.shape[0] // N_DEV, g.shape[1])
    core = lax.axis_index("c").astype(jnp.int32).reshape(1)
    rs = {}

    def rs_begin(key, grads):
        rs[key] = rs_d2d_start("rs_d2d_start_" + key, grads)

    def rs_middle(key, after):
        grads, lands = rs_d2d_wait("rs_d2d_wait_" + key, rs[key], after)
        sums = [pair_sum("pair_sum_%s_%d" % (key, i), g, l, core) for i, (g, l) in enumerate(zip(grads, lands))]
        rs[key] = rs_ici_start("rs_ici_start_" + key, sums)

    gw_down = mm_tn("g_w_down", f, dffn, BF16)
    rs_begin("down", [blocks(gw_down)])
    df = mm_nt("d_f", dffn, w_down_f, ACT)
    rs_middle("down", df)
    dupre, gcw_g, gcw_v, gcb_g, gcb_v = conv_act_bwd(upre, cw, gv, df)
    gw_up3 = mm_tn_h3("g_w_up", h2, dupre, N_DEV, BF16)
    rs_begin("up", [gw_up3])
    dh2 = mm_nt_h3("d_h2", dupre, g_up, ACT)
    rs_middle("up", dh2)
    dx1, dy1, acc_mid = mid_bwd(dh2, dout, x1, y1, row(pre_norm2_g), scale2, gate1, row(post_norm1_g))
    gw_out = mm_tn("g_w_out", merged, dy1, BF16)
    dmerged = mm_nt("d_merged", dy1, w_out_f, ACT)
    dya, dyb, dga, dgb = merge_bwd(z, o_ga, o_gb, y_a, y_b, dmerged)
    gw_a = mm_tn("g_w_a", a, dya, BF16)
    gw_b = mm_tn("g_w_b", o, dyb, BF16)
    rs_begin("mid", [blocks(gw_out), blocks(gw_a), blocks(gw_b)])
    da = mm_nt("d_a", dya, w_a_f, ACT)
    do = mm_nt("d_o", dyb, w_b_f, ACT)
    rs_middle("mid", do)
    duv, g_ws, g_bs3, acc_gm = gmlp_bwd(z, gmw, da, ln_g, ln_b, wm, bs3)
    dqn, dqp, dkv, dkp = attn_bwd(qn, qpr, kv, kpr, o, do, lse, cos4, sin4)
    dkpe = kpe_bwd(dkp, cos4, sin4, zp - o_kpe)
    dq_cat = jnp.concatenate([dqn, dqp], axis=1)
    w_uq_cat = jnp.concatenate([w_uq_n, w_uq_r], axis=1)
    gw_uq_cat = mm_tn("g_w_uq", qln, dq_cat, BF16)
    gw_uq_f = jnp.concatenate([gw_uq_cat[:, :HEADS * QK_NOPE].reshape(ql, HEADS, QK_NOPE),
                               gw_uq_cat[:, HEADS * QK_NOPE:].reshape(ql, HEADS, QK_ROPE)], axis=2)
    gw_uq3 = gw_uq_f.reshape(ql, N_DEV, -1).transpose(1, 0, 2)
    gw_ukv3 = mm_tn_o3("g_w_ukv", kvn, dkv, N_DEV, BF16)
    rs_begin("mla", [gw_uq3, gw_ukv3])
    dqln = mm_nt("d_qln", dq_cat, w_uq_cat, ACT)
    dq_lat, g_qnorm = rms_bwd_cols("q_norm_bwd", dqln, z, o_q, ql, row(q_norm_g))
    dkvn = mm_nt_b3("d_kvn", dkv, g_ukv, ACT)
    rs_middle("mla", dkvn)
    dkv_lat, g_kvnorm = rms_bwd_cols("kv_norm_bwd", dkvn, z, o_kv, kvl, row(kv_norm_g))
    dz = jnp.concatenate([duv, dq_lat, dkv_lat, dga, dgb, dkpe], axis=1)
    gw_in_p = mm_tn("g_w_in", dz, h1, BF16)

    def gw_in_rows(lo, hi):
        pieces = []
        for a, b, shift in ((0, src_kpe, 0), (src_kpe, src_kpe + QK_ROPE, o_kpe - src_kpe), (src_kpe + QK_ROPE, in_cols, -QK_ROPE)):
            if lo < b and hi > a:
                pieces.append(gw_in_p[max(lo, a) + shift:min(hi, b) + shift])
        return pieces[0] if len(pieces) == 1 else jnp.concatenate(pieces, axis=0)

    rs_begin("in", [jnp.stack([gw_in_rows(k * cs_in, (k + 1) * cs_in) for k in range(N_DEV)])])
    dh1 = mm_nn("d_h1", dz, w_in_p, ACT)
    grad_x, acc1 = pre1_bwd(dh1, dx1, x2, row(pre_norm1_g), scale1)

    dmod = jnp.concatenate([acc1[0], acc1[1], acc_mid[3], acc_mid[0], acc_mid[1], acc2[0]])
    small = [("pre_norm1_g", acc1[2]), ("gm_ln_g", acc_gm[0]), ("gm_ln_b", acc_gm[1]), ("gm_b_s", g_bs3.reshape(-1)),
             ("q_norm_g", g_qnorm[0]), ("kv_norm_g", g_kvnorm[0]), ("post_norm1_g", acc_mid[4]), ("pre_norm2_g", acc_mid[2]),
             ("conv_b", jnp.concatenate([gcb_g[0], gcb_v[0]])), ("post_norm2_g", acc2[1]), ("gm_w_s", g_ws.reshape(-1)),
             ("b_ada", dmod)]
    n_small = sum(v.shape[0] for _, v in small)
    n_cw = 3 * f2
    n_pack = _round_up(n_small + n_cw, PACK_ALIGN)
    tail = jnp.zeros((n_pack - n_small - n_cw,), F32)
    packed = jnp.concatenate([v for _, v in small] + [jnp.concatenate([gcw_g, gcw_v], axis=1).reshape(-1), tail])
    ag_small = ag_start("ag_start_small", [packed.reshape(-1, LANES)], packed)
    rs_middle("in", [packed, _TOKENS.pending[-1]])

    res = {}
    last = packed
    for key, names in (("down", ["w_down"]), ("up", ["w_up"]), ("mid", ["w_out", "w_branch_a", "w_branch_b"]), ("mla", ["w_uq", "w_ukv"])):
        parts = rs_ici_wait("rs_ici_wait_" + key, rs[key], last)
        for k, p in zip(names, parts):
            res[k] = adamw("adamw_" + k, weights[k], mom1[k], mom2[k], p)
            last = res[k][0]

    assert [k for k, _ in small] == small_names and n_small == n_small_early
    (gathered,) = ag_wait("ag_wait_small", ag_forward("ag_forward_small", ag_small, last), last)
    sm = [t.reshape(-1) for t in adamw("adamw_small", *packed_state, gathered)]
    off = 0
    for k, v in small:
        res[k] = tuple(t[off:off + v.shape[0]].reshape(weights[k].shape) for t in sm)
        off += v.shape[0]

    cs_cw = conv_w.shape[1]
    g_cw_full = sm[0][n_small:n_small + n_cw].reshape(3, f2)
    g_cw_mine = lax.dynamic_slice(g_cw_full, (0, my * cs_cw), (3, cs_cw))
    res["conv_w"] = adamw("adamw_conv_w", conv_w, mom1["conv_w"], mom2["conv_w"], g_cw_mine[None])

    cs_ada = w_ada.shape[1]
    off_b = n_small - N_MOD * d
    dmod_all = gathered.reshape(N_DEV, -1)[:, off_b:off_b + N_MOD * d]
    dmod_cols = lax.dynamic_slice(dmod_all, (0, my * cs_ada), (N_DEV, cs_ada))
    res["w_ada"] = adamw("adamw_w_ada", w_ada, mom1["w_ada"], mom2["w_ada"], ada_bwd_w(sc_all, dmod_cols))

    (p_in,) = rs_ici_wait("rs_ici_wait_in", rs["in"], res["w_ada"][0])
    res["w_in"] = tuple(t.T for t in adamw("adamw_w_in", w_in.T, mom1["w_in"].T, mom2["w_in"].T, p_in))

    _TOKENS.clear()
    outs = [loss, grad_x[None]]
    for i in range(4):
        outs += [res[k][i] for k in order]
    return tuple(outs)
```

```python
import jax
import jax.numpy as jnp
from jax import lax
from jax.experimental import pallas as pl
from jax.experimental.pallas import tpu as pltpu

F32 = jnp.float32
BF16 = jnp.bfloat16
ACT = BF16

N_DEV = 8
HEADS = 16
QK_NOPE = 128
QK_ROPE = 64
V_HEAD = 128
CHUNK = 128
ROPE_THETA = 10000.0
EPS = 1e-6
N_MOD = 6
ADAM_LR, ADAM_B1, ADAM_B2, ADAM_EPS, ADAM_WD, ADAM_STEP = 0.001, 0.9, 0.999, 1e-08, 0.01, 10

LANES = 128
VMEM_LIMIT_BYTES = 48 * 2 ** 20
ROW_TILE = 256
COL_TILE = 256
ATT_TILE = 512
Z_PAD = 512
ADAMW_TILE_ELEMS = 1 << 18
PACK_ALIGN = 8 * LANES
MESH = pl.DeviceIdType.MESH


def _params(*sem):
    return pltpu.CompilerParams(dimension_semantics=sem if sem else None, vmem_limit_bytes=VMEM_LIMIT_BYTES)


def _tile(dim, target):
    t = (min(dim, target) // LANES) * LANES
    while t >= LANES:
        if dim % t == 0:
            return t
        t -= LANES
    return dim


def _full(shape):
    nd = len(shape)
    return pl.BlockSpec(shape, lambda *_: (0,) * nd)


class _Tokens:
    KEEP = 2

    def __init__(self):
        self.pending = []

    def push(self, token):
        self.pending = (self.pending + [token])[-self.KEEP:]

    def take(self):
        return list(self.pending)

    def clear(self):
        self.pending = []


_TOKENS = _Tokens()


def _matmul(name, a, b, *, grid, a_spec, b_spec, o_spec, out_shape, contract, acc_shape, split=1):
    nk = grid[2]
    deps = _TOKENS.take()

    def product(a_ref, b_ref):
        if len(b_ref.shape) == 2:
            return lax.dot_general(a_ref[...].astype(BF16), b_ref[...].astype(BF16), (contract, ((), ())), preferred_element_type=F32)
        cs = b_ref.shape[2]
        return sum(lax.dot_general(a_ref[:, s * cs:(s + 1) * cs].astype(BF16), b_ref[s].astype(BF16), (contract, ((), ())),
                                   preferred_element_type=F32) for s in range(split))

    def body_one_step(a_ref, b_ref, *rest):
        o_ref = rest[len(deps)]
        o_ref[...] = product(a_ref, b_ref).astype(o_ref.dtype)

    def body(a_ref, b_ref, *rest):
        o_ref, acc_ref = rest[len(deps):]
        k = pl.program_id(2)

        @pl.when(k == 0)
        def _():
            acc_ref[...] = jnp.zeros_like(acc_ref)

        acc_ref[...] += product(a_ref, b_ref)

        @pl.when(k == nk - 1)
        def _():
            o_ref[...] = acc_ref[...].astype(o_ref.dtype)

    return pl.pallas_call(
        body_one_step if nk == 1 else body, name=name, grid=grid,
        in_specs=[a_spec, b_spec] + [pl.BlockSpec(memory_space=pl.ANY)] * len(deps),
        out_specs=o_spec, out_shape=out_shape, scratch_shapes=[] if nk == 1 else [pltpu.VMEM(acc_shape, F32)],
        compiler_params=_params("parallel", "parallel", "arbitrary"))(a, b, *deps)


T_OUT, T_OUT_WIDE, TK = 1024, 1408, 2816


def _out_tile(dim):
    return T_OUT_WIDE if dim % T_OUT_WIDE == 0 else _tile(dim, T_OUT)


def _tk(a, b):
    return TK if a.dtype == BF16 and b.dtype == BF16 else TK // 2


def mm_nn(name, a, b, dtype):
    (m, k), n = a.shape, b.shape[1]
    tm, tn, tk = _out_tile(m), _out_tile(n), _tile(k, _tk(a, b))
    return _matmul(name, a, b, grid=(m // tm, n // tn, k // tk),
                   a_spec=pl.BlockSpec((tm, tk), lambda i, j, kk: (i, kk)),
                   b_spec=pl.BlockSpec((tk, tn), lambda i, j, kk: (kk, j)),
                   o_spec=pl.BlockSpec((tm, tn), lambda i, j, kk: (i, j)),
                   out_shape=jax.ShapeDtypeStruct((m, n), dtype), contract=((1,), (0,)), acc_shape=(tm, tn))


def mm_nn_b3(name, a, b3, dtype):
    (m, k), (nj, _, cs) = a.shape, b3.shape
    tm, tk = _out_tile(m), _tile(k, _tk(a, b3))
    return _matmul(name, a, b3, grid=(m // tm, nj, k // tk),
                   a_spec=pl.BlockSpec((tm, tk), lambda i, j, kk: (i, kk)),
                   b_spec=pl.BlockSpec((None, tk, cs), lambda i, j, kk: (j, kk, 0)),
                   o_spec=pl.BlockSpec((tm, cs), lambda i, j, kk: (i, j)),
                   out_shape=jax.ShapeDtypeStruct((m, nj * cs), dtype), contract=((1,), (0,)), acc_shape=(tm, cs))


def mm_nt(name, a, b, dtype):
    (m, k), n = a.shape, b.shape[0]
    tm, tn, tk = _out_tile(m), _out_tile(n), _tile(k, _tk(a, b))
    return _matmul(name, a, b, grid=(m // tm, n // tn, k // tk),
                   a_spec=pl.BlockSpec((tm, tk), lambda i, j, kk: (i, kk)),
                   b_spec=pl.BlockSpec((tn, tk), lambda i, j, kk: (j, kk)),
                   o_spec=pl.BlockSpec((tm, tn), lambda i, j, kk: (i, j)),
                   out_shape=jax.ShapeDtypeStruct((m, n), dtype), contract=((1,), (1,)), acc_shape=(tm, tn))


def mm_nt_b3(name, a, b3, dtype):
    m, (nj, n, cs) = a.shape[0], b3.shape
    tm, tn = _out_tile(m), _out_tile(n)
    return _matmul(name, a, b3, grid=(m // tm, n // tn, nj),
                   a_spec=pl.BlockSpec((tm, cs), lambda i, j, kk: (i, kk)),
                   b_spec=pl.BlockSpec((None, tn, cs), lambda i, j, kk: (kk, j, 0)),
                   o_spec=pl.BlockSpec((tm, tn), lambda i, j, kk: (i, j)),
                   out_shape=jax.ShapeDtypeStruct((m, n), dtype), contract=((1,), (1,)), acc_shape=(tm, tn))


def mm_nt_h3(name, a3, b3, dtype):
    (_, m, _), (nj, n, cs) = a3.shape, b3.shape
    tm, tn, hj = _out_tile(m), _out_tile(n), nj // 2
    pair = 2 if hj % 2 == 0 else 1
    return _matmul(name, a3, b3.reshape(nj // pair, pair, n, cs), grid=(m // tm, n // tn, nj // pair),
                   a_spec=pl.BlockSpec((None, tm, pair * cs), lambda i, j, kk: (kk // (hj // pair), i, kk % (hj // pair))),
                   b_spec=pl.BlockSpec((None, pair, tn, cs), lambda i, j, kk: (kk, 0, j, 0)),
                   o_spec=pl.BlockSpec((tm, tn), lambda i, j, kk: (i, j)),
                   out_shape=jax.ShapeDtypeStruct((m, n), dtype), contract=((1,), (1,)), acc_shape=(tm, tn), split=pair)


def mm_tn_h3(name, a, b3, nj, dtype):
    (k, m), half = a.shape, b3.shape[2]
    hj = nj // 2
    cs = half // hj
    tm, tk = _out_tile(m), _tile(k, _tk(a, b3))
    return _matmul(name, a, b3, grid=(m // tm, nj, k // tk),
                   a_spec=pl.BlockSpec((tk, tm), lambda i, j, kk: (kk, i)),
                   b_spec=pl.BlockSpec((None, tk, cs), lambda i, j, kk: (j // hj, kk, j % hj)),
                   o_spec=pl.BlockSpec((None, tm, cs), lambda i, j, kk: (j, i, 0)),
                   out_shape=jax.ShapeDtypeStruct((nj, m, cs), dtype), contract=((0,), (0,)), acc_shape=(tm, cs))


def mm_tn(name, a, b, dtype):
    (k, m), n = a.shape, b.shape[1]
    tm, tn, tk = _out_tile(m), _out_tile(n), _tile(k, _tk(a, b))
    return _matmul(name, a, b, grid=(m // tm, n // tn, k // tk),
                   a_spec=pl.BlockSpec((tk, tm), lambda i, j, kk: (kk, i)),
                   b_spec=pl.BlockSpec((tk, tn), lambda i, j, kk: (kk, j)),
                   o_spec=pl.BlockSpec((tm, tn), lambda i, j, kk: (i, j)),
                   out_shape=jax.ShapeDtypeStruct((m, n), dtype), contract=((0,), (0,)), acc_shape=(tm, tn))


def mm_tn_o3(name, a, b, nj, dtype):
    (k, m), n = a.shape, b.shape[1]
    cs = n // nj
    tm, tk = _out_tile(m), _tile(k, _tk(a, b))
    return _matmul(name, a, b, grid=(m // tm, nj, k // tk),
                   a_spec=pl.BlockSpec((tk, tm), lambda i, j, kk: (kk, i)),
                   b_spec=pl.BlockSpec((tk, cs), lambda i, j, kk: (kk, j)),
                   o_spec=pl.BlockSpec((None, tm, cs), lambda i, j, kk: (j, i, 0)),
                   out_shape=jax.ShapeDtypeStruct((nj, m, cs), dtype), contract=((0,), (0,)), acc_shape=(tm, cs))


_GELU_C = 0.7978845608028654
_GELU_A = 0.044715


def _f32(ref):
    return ref[...].astype(F32)


def _gelu(x):
    x = x.astype(F32)
    return 0.5 * x * (1.0 + jnp.tanh(_GELU_C * (x + _GELU_A * x * x * x)))


def _gelu_and_grad(x):
    x = x.astype(F32)
    t = jnp.tanh(_GELU_C * (x + _GELU_A * x * x * x))
    y = 0.5 * x * (1.0 + t)
    dy = 0.5 * (1.0 + t) + 0.5 * x * (1.0 - t * t) * (_GELU_C * (1.0 + 3.0 * _GELU_A * x * x))
    return y, dy


def _sigmoid(x):
    return 0.5 * jnp.tanh(0.5 * x.astype(F32)) + 0.5


def _rms_stats(x):
    x = x.astype(F32)
    inv = lax.rsqrt(jnp.mean(x * x, axis=-1, keepdims=True) + EPS)
    return inv, x * inv


def _rms_bwd(dyhat, yhat, inv):
    return inv * (dyhat - yhat * jnp.mean(dyhat * yhat, axis=-1, keepdims=True))


def _colsum(x):
    return jnp.sum(x, axis=0, keepdims=True)


def _rope(x, cos4, sin4):
    lane = lax.broadcasted_iota(jnp.int32, x.shape, x.ndim - 1)
    first_half = (lane % QK_ROPE) < (QK_ROPE // 2)
    partner = jnp.where(first_half, pltpu.roll(x, LANES - QK_ROPE // 2, x.ndim - 1), pltpu.roll(x, QK_ROPE // 2, x.ndim - 1))
    return x * cos4 + partner * sin4


def norm_mod_fwd(name, x, g, scale, shift):
    s, d = x.shape
    tr = _tile(s, ROW_TILE)

    def body(x_ref, g_ref, sc_ref, sh_ref, o_ref):
        _, xh = _rms_stats(x_ref[...])
        o_ref[...] = (xh * g_ref[...] * (1.0 + sc_ref[...]) + sh_ref[...]).astype(o_ref.dtype)

    row = pl.BlockSpec((tr, d), lambda i: (i, 0))
    vec = pl.BlockSpec((1, d), lambda i: (0, 0))
    return pl.pallas_call(body, name=name, grid=(s // tr,), in_specs=[row, vec, vec, vec], out_specs=row,
                          out_shape=jax.ShapeDtypeStruct((s, d), BF16), compiler_params=_params("parallel"))(x, g, scale, shift)


def rms_fwd_cols(name, z, off, width, g):
    s = z.shape[0]
    tr = _tile(s, ROW_TILE)
    assert off % width == 0

    def body(x_ref, g_ref, o_ref):
        _, xh = _rms_stats(x_ref[...])
        o_ref[...] = (xh * g_ref[...]).astype(o_ref.dtype)

    return pl.pallas_call(body, name=name, grid=(s // tr,),
                          in_specs=[pl.BlockSpec((tr, width), lambda i: (i, off // width)), pl.BlockSpec((1, width), lambda i: (0, 0))],
                          out_specs=pl.BlockSpec((tr, width), lambda i: (i, 0)),
                          out_shape=jax.ShapeDtypeStruct((s, width), BF16), compiler_params=_params("parallel"))(z, g)


def rms_bwd_cols(name, dy, z, off, width, g):
    s = z.shape[0]
    tr = _tile(s, ROW_TILE)

    def body(dy_ref, x_ref, g_ref, dx_ref, gg_ref):
        @pl.when(pl.program_id(0) == 0)
        def _():
            gg_ref[...] = jnp.zeros_like(gg_ref)

        inv, xh = _rms_stats(x_ref[...])
        dy_v = _f32(dy_ref)
        gg_ref[...] += _colsum(dy_v * xh)
        dx_ref[...] = _rms_bwd(dy_v * g_ref[...], xh, inv).astype(dx_ref.dtype)

    return pl.pallas_call(body, name=name, grid=(s // tr,),
                          in_specs=[pl.BlockSpec((tr, width), lambda i: (i, 0)), pl.BlockSpec((tr, width), lambda i: (i, off // width)),
                                    pl.BlockSpec((1, width), lambda i: (0, 0))],
                          out_specs=[pl.BlockSpec((tr, width), lambda i: (i, 0)), pl.BlockSpec((1, width), lambda i: (0, 0))],
                          out_shape=[jax.ShapeDtypeStruct((s, width), BF16), jax.ShapeDtypeStruct((1, width), F32)],
                          compiler_params=_params("arbitrary"))(dy, z, g)


def post_res_fwd(name, x, y, gate, g):
    s, d = x.shape
    tr = _tile(s, ROW_TILE)

    def body(x_ref, y_ref, gate_ref, g_ref, o_ref):
        _, yh = _rms_stats(y_ref[...])
        o_ref[...] = x_ref[...] + gate_ref[...] * (yh * g_ref[...])

    row = pl.BlockSpec((tr, d), lambda i: (i, 0))
    vec = pl.BlockSpec((1, d), lambda i: (0, 0))
    return pl.pallas_call(body, name=name, grid=(s // tr,), in_specs=[row, row, vec, vec], out_specs=row,
                          out_shape=jax.ShapeDtypeStruct((s, d), F32), compiler_params=_params("parallel"))(x, y, gate, g)


def post2_loss_bwd(x1, ffn, target, gate2, g):
    s, d = x1.shape
    tr = _tile(s, ROW_TILE)

    def body(x_ref, y_ref, t_ref, gate_ref, g_ref, loss_ref, dout_ref, dy_ref, acc_ref):
        @pl.when(pl.program_id(0) == 0)
        def _():
            loss_ref[...] = jnp.zeros_like(loss_ref)
            acc_ref[...] = jnp.zeros_like(acc_ref)

        inv, yh = _rms_stats(y_ref[...])
        r = yh * g_ref[...]
        err = x_ref[...] + gate_ref[...] * r - t_ref[...]
        loss_ref[...] += 0.5 * jnp.sum(jnp.mean(err * err, axis=-1, keepdims=True))
        dout = err / d
        dout_ref[...] = dout
        dr = dout * gate_ref[...]
        acc_ref[0:1, :] += _colsum(dout * r)
        acc_ref[1:2, :] += _colsum(dr * yh)
        dy_ref[...] = _rms_bwd(dr * g_ref[...], yh, inv).astype(dy_ref.dtype)

    row = pl.BlockSpec((tr, d), lambda i: (i, 0))
    vec = pl.BlockSpec((1, d), lambda i: (0, 0))
    return pl.pallas_call(
        body, name="post2_loss_bwd", grid=(s // tr,), in_specs=[row, row, row, vec, vec],
        out_specs=[_full((8, LANES)), row, row, _full((8, d))],
        out_shape=[jax.ShapeDtypeStruct((8, LANES), F32), jax.ShapeDtypeStruct((s, d), F32),
                   jax.ShapeDtypeStruct((s, d), BF16), jax.ShapeDtypeStruct((8, d), F32)],
        compiler_params=_params("arbitrary"))(x1, ffn, target, gate2, g)


def mid_bwd(dh2, dout, x1, y1, pre2_g, scale2, gate1, post1_g):
    s, d = x1.shape
    tr = _tile(s, ROW_TILE)

    def body(dh_ref, dout_ref, x_ref, y_ref, g2_ref, sc_ref, gate_ref, g1_ref, dx_ref, dy_ref, acc_ref):
        @pl.when(pl.program_id(0) == 0)
        def _():
            acc_ref[...] = jnp.zeros_like(acc_ref)

        dh = _f32(dh_ref)
        inv2, xh = _rms_stats(x_ref[...])
        acc_ref[0:1, :] += _colsum(dh)
        acc_ref[1:2, :] += _colsum(dh * (xh * g2_ref[...]))
        t = dh * (1.0 + sc_ref[...])
        acc_ref[2:3, :] += _colsum(t * xh)
        dx1 = dout_ref[...] + _rms_bwd(t * g2_ref[...], xh, inv2)
        dx_ref[...] = dx1
        inv1, yh = _rms_stats(y_ref[...])
        acc_ref[3:4, :] += _colsum(dx1 * (yh * g1_ref[...]))
        dr = dx1 * gate_ref[...]
        acc_ref[4:5, :] += _colsum(dr * yh)
        dy_ref[...] = _rms_bwd(dr * g1_ref[...], yh, inv1).astype(dy_ref.dtype)

    row = pl.BlockSpec((tr, d), lambda i: (i, 0))
    vec = pl.BlockSpec((1, d), lambda i: (0, 0))
    return pl.pallas_call(
        body, name="mid_bwd", grid=(s // tr,), in_specs=[row, row, row, row, vec, vec, vec, vec],
        out_specs=[row, row, _full((8, d))],
        out_shape=[jax.ShapeDtypeStruct((s, d), F32), jax.ShapeDtypeStruct((s, d), BF16), jax.ShapeDtypeStruct((8, d), F32)],
        compiler_params=_params("arbitrary"))(dh2, dout, x1, y1, pre2_g, scale2, gate1, post1_g)


def pre1_bwd(dh1, dx1, x, pre1_g, scale1):
    s, d = x.shape
    tr = _tile(s, ROW_TILE)

    def body(dh_ref, dx1_ref, x_ref, g_ref, sc_ref, dx_ref, acc_ref):
        @pl.when(pl.program_id(0) == 0)
        def _():
            acc_ref[...] = jnp.zeros_like(acc_ref)

        dh = _f32(dh_ref)
        inv, xh = _rms_stats(x_ref[...])
        acc_ref[0:1, :] += _colsum(dh)
        acc_ref[1:2, :] += _colsum(dh * (xh * g_ref[...]))
        t = dh * (1.0 + sc_ref[...])
        acc_ref[2:3, :] += _colsum(t * xh)
        dx_ref[...] = dx1_ref[...] + _rms_bwd(t * g_ref[...], xh, inv)

    row = pl.BlockSpec((tr, d), lambda i: (i, 0))
    vec = pl.BlockSpec((1, d), lambda i: (0, 0))
    return pl.pallas_call(
        body, name="pre1_bwd", grid=(s // tr,), in_specs=[row, row, row, vec, vec], out_specs=[row, _full((8, d))],
        out_shape=[jax.ShapeDtypeStruct((s, d), F32), jax.ShapeDtypeStruct((8, d), F32)],
        compiler_params=_params("arbitrary"))(dh1, dx1, x, pre1_g, scale1)


def _ln_stats(v):
    mu = jnp.mean(v, axis=-1, keepdims=True)
    vc = v - mu
    rstd = lax.rsqrt(jnp.mean(vc * vc, axis=-1, keepdims=True) + EPS)
    return rstd, vc * rstd


def gmlp_fwd(z, width, ln_g, ln_b, wm, bs3):
    s = z.shape[0]
    groups = width // CHUNK

    def body(u_ref, v_ref, g_ref, b_ref, wm_ref, bs_ref, a_ref):
        ug = _gelu(u_ref[...])
        _, vh = _ln_stats(_gelu(v_ref[...]))
        vn = (vh * g_ref[...] + b_ref[...]).astype(BF16)
        for g in range(groups):
            cols = slice(g * CHUNK, (g + 1) * CHUNK)
            mixed = jnp.dot(wm_ref[g], vn[:, cols], preferred_element_type=F32) + bs_ref[g]
            a_ref[:, cols] = (ug[:, cols] * mixed).astype(a_ref.dtype)

    vec = pl.BlockSpec((1, width), lambda n: (0, 0))
    return pl.pallas_call(
        body, name="gmlp_fwd", grid=(s // CHUNK,),
        in_specs=[pl.BlockSpec((CHUNK, width), lambda n: (n, 0)), pl.BlockSpec((CHUNK, width), lambda n: (n, 1)), vec, vec,
                  _full(wm.shape), _full(bs3.shape)],
        out_specs=pl.BlockSpec((CHUNK, width), lambda n: (n, 0)),
        out_shape=jax.ShapeDtypeStruct((s, width), BF16), compiler_params=_params("parallel"))(z, z, ln_g, ln_b, wm, bs3)


def gmlp_bwd(z, width, da, ln_g, ln_b, wm, bs3):
    s = z.shape[0]
    groups = width // CHUNK

    def body(u_ref, v_ref, da_ref, g_ref, b_ref, wm_ref, bs_ref, duv_ref, gw_ref, gb_ref, acc_ref, dvn_ref):
        @pl.when(pl.program_id(0) == 0)
        def _():
            gw_ref[...] = jnp.zeros_like(gw_ref)
            gb_ref[...] = jnp.zeros_like(gb_ref)
            acc_ref[...] = jnp.zeros_like(acc_ref)

        ug, dug = _gelu_and_grad(u_ref[...])
        vg, dvg = _gelu_and_grad(v_ref[...])
        rstd, vh = _ln_stats(vg)
        vn = (vh * g_ref[...] + b_ref[...]).astype(BF16)
        da_v = _f32(da_ref)
        for g in range(groups):
            cols = slice(g * CHUNK, (g + 1) * CHUNK)
            mixed = jnp.dot(wm_ref[g], vn[:, cols], preferred_element_type=F32) + bs_ref[g]
            duv_ref[:, cols] = (da_v[:, cols] * mixed * dug[:, cols]).astype(duv_ref.dtype)
            dm = da_v[:, cols] * ug[:, cols]
            gb_ref[g] += jnp.sum(dm, axis=-1, keepdims=True)
            dmb = dm.astype(BF16)
            gw_ref[g] += lax.dot_general(dmb, vn[:, cols], (((1,), (1,)), ((), ())), preferred_element_type=F32)
            dvn_ref[:, cols] = lax.dot_general(wm_ref[g], dmb, (((0,), (0,)), ((), ())), preferred_element_type=F32)
        dvn = dvn_ref[...]
        acc_ref[0:1, :] += _colsum(dvn * vh)
        acc_ref[1:2, :] += _colsum(dvn)
        dvh = dvn * g_ref[...]
        dv = rstd * (dvh - jnp.mean(dvh, axis=-1, keepdims=True) - vh * jnp.mean(dvh * vh, axis=-1, keepdims=True))
        duv_ref[:, width:] = (dv * dvg).astype(duv_ref.dtype)

        @pl.when(pl.program_id(0) == pl.num_programs(0) - 1)
        def _():
            q = lax.broadcasted_iota(jnp.int32, gw_ref.shape, 1)
            p = lax.broadcasted_iota(jnp.int32, gw_ref.shape, 2)
            gw_ref[...] = jnp.where(p <= q, gw_ref[...], 0.0)

    vec = pl.BlockSpec((1, width), lambda n: (0, 0))
    blk = pl.BlockSpec((CHUNK, width), lambda n: (n, 0))
    return pl.pallas_call(
        body, name="gmlp_bwd", grid=(s // CHUNK,),
        in_specs=[blk, pl.BlockSpec((CHUNK, width), lambda n: (n, 1)), blk, vec, vec, _full(wm.shape), _full(bs3.shape)],
        out_specs=[pl.BlockSpec((CHUNK, 2 * width), lambda n: (n, 0)), _full(wm.shape), _full(bs3.shape), _full((8, width))],
        out_shape=[jax.ShapeDtypeStruct((s, 2 * width), BF16), jax.ShapeDtypeStruct(wm.shape, F32),
                   jax.ShapeDtypeStruct(bs3.shape, F32), jax.ShapeDtypeStruct((8, width), F32)],
        scratch_shapes=[pltpu.VMEM((CHUNK, width), F32)],
        compiler_params=_params("arbitrary"))(z, z, da, ln_g, ln_b, wm, bs3)


def merge_fwd(z, off_a, off_b, ya, yb):
    s, d = ya.shape
    tr, tc = _tile(s, ROW_TILE * 2), _tile(d, COL_TILE)
    assert off_a % tc == 0 and off_b % tc == 0

    def body(ga_ref, gb_ref, ya_ref, yb_ref, o_ref):
        o_ref[...] = (_sigmoid(ga_ref[...]) * _f32(ya_ref) + _sigmoid(gb_ref[...]) * _f32(yb_ref)).astype(o_ref.dtype)

    blk = pl.BlockSpec((tr, tc), lambda i, j: (i, j))
    return pl.pallas_call(
        body, name="merge_fwd", grid=(s // tr, d // tc),
        in_specs=[pl.BlockSpec((tr, tc), lambda i, j: (i, off_a // tc + j)), pl.BlockSpec((tr, tc), lambda i, j: (i, off_b // tc + j)), blk, blk],
        out_specs=blk, out_shape=jax.ShapeDtypeStruct((s, d), BF16), compiler_params=_params("parallel", "parallel"))(z, z, ya, yb)


def merge_bwd(z, off_a, off_b, ya, yb, dm):
    s, d = ya.shape
    tr, tc = _tile(s, ROW_TILE * 2), _tile(d, COL_TILE)
    nc = d // tc

    def body(ga_ref, gb_ref, ya_ref, yb_ref, dm_ref, dya_ref, dyb_ref, dga_ref, dgb_ref):
        dm_v = _f32(dm_ref)
        sa, sb = _sigmoid(ga_ref[...]), _sigmoid(gb_ref[...])
        dya_ref[...] = (dm_v * sa).astype(dya_ref.dtype)
        dyb_ref[...] = (dm_v * sb).astype(dyb_ref.dtype)
        dga_ref[...] = (dm_v * _f32(ya_ref) * sa * (1.0 - sa)).astype(dga_ref.dtype)
        dgb_ref[...] = (dm_v * _f32(yb_ref) * sb * (1.0 - sb)).astype(dgb_ref.dtype)

    blk = pl.BlockSpec((tr, tc), lambda i, j: (i, j))
    out = jax.ShapeDtypeStruct((s, d), BF16)
    return pl.pallas_call(
        body, name="merge_bwd", grid=(s // tr, nc),
        in_specs=[pl.BlockSpec((tr, tc), lambda i, j: (i, off_a // tc + j)), pl.BlockSpec((tr, tc), lambda i, j: (i, off_b // tc + j)), blk, blk, blk],
        out_specs=[blk, blk, blk, blk], out_shape=[out, out, out, out],
        compiler_params=_params("parallel", "parallel"))(z, z, ya, yb, dm)


_ATT_SCALE = (QK_NOPE + QK_ROPE) ** -0.5
_NEG = -1e30


def rope_k(z, off, cos4, sin4):
    s = z.shape[0]
    tr = _tile(s, ROW_TILE * 2)
    assert off % LANES == 0

    def body(k_ref, c_ref, s_ref, o_ref):
        k = _f32(k_ref)
        k = k + pltpu.roll(k, QK_ROPE, 1)
        o_ref[...] = _rope(k, c_ref[...], s_ref[...]).astype(o_ref.dtype)

    row = pl.BlockSpec((tr, LANES), lambda i: (i, 0))
    return pl.pallas_call(body, name="rope_k", grid=(s // tr,),
                          in_specs=[pl.BlockSpec((tr, LANES), lambda i: (i, off // LANES)), row, row], out_specs=row,
                          out_shape=jax.ShapeDtypeStruct((s, LANES), BF16), compiler_params=_params("parallel"))(z, cos4, sin4)


def _dot_nt(a, b):
    return lax.dot_general(a, b, (((1,), (1,)), ((), ())), preferred_element_type=F32)


def _dot_tn(a, b):
    return lax.dot_general(a, b, (((0,), (0,)), ((), ())), preferred_element_type=F32)


def _q_cat(q_n, qpr, hh):
    lane = lax.broadcasted_iota(jnp.int32, qpr.shape, 1)
    sel = (lane < QK_ROPE) if hh == 0 else (lane >= QK_ROPE)
    return jnp.concatenate([q_n, jnp.where(sel, qpr, jnp.zeros_like(qpr))], axis=1)


def _causal(sc):
    row = lax.broadcasted_iota(jnp.int32, sc.shape, 0)
    col = lax.broadcasted_iota(jnp.int32, sc.shape, 1)
    return jnp.where(col <= row, sc, _NEG)


def attn_fwd(qn, qp, kv, kpr, cos4, sin4):
    s = qn.shape[0]
    hp = HEADS // 2
    t = _tile(s, ATT_TILE)
    nq = s // t

    def body(qn_ref, qp_ref, kv_ref, kp_ref, c_ref, s_ref, o_ref, qpr_ref, l_ref, kcat_ref):
        qi = pl.program_id(1)

        @pl.when(qi == 0)
        def _():
            for hh in range(2):
                kcat_ref[hh, :, 0:QK_NOPE] = kv_ref[:, 2 * hh * QK_NOPE:(2 * hh + 1) * QK_NOPE]
                kcat_ref[hh, :, QK_NOPE:] = kp_ref[...]

        qpr = _rope(qp_ref[...], c_ref[...], s_ref[...]).astype(BF16)
        qpr_ref[...] = qpr
        qcat = [_q_cat(qn_ref[:, hh * QK_NOPE:(hh + 1) * QK_NOPE], qpr, hh) for hh in range(2)]

        def block(kb, carry, diagonal):
            rows = pl.ds(pl.multiple_of(kb * t, t), t)
            out = []
            for hh in range(2):
                m, l, acc = carry[hh]
                sc = _dot_nt(qcat[hh], kcat_ref[hh, rows, :]) * _ATT_SCALE
                if diagonal:
                    sc = _causal(sc)
                m_new = jnp.maximum(m, jnp.max(sc, axis=-1, keepdims=True))
                alpha = jnp.exp(m - m_new)
                p = jnp.exp(sc - m_new)
                l = alpha * l + jnp.sum(p, axis=-1, keepdims=True)
                v = kv_ref[rows, (2 * hh + 1) * QK_NOPE:(2 * hh + 2) * QK_NOPE]
                acc = alpha * acc + jnp.dot(p.astype(BF16), v, preferred_element_type=F32)
                out.append((m_new, l, acc))
            return tuple(out)

        one = (jnp.full((t, 1), _NEG, F32), jnp.zeros((t, 1), F32), jnp.zeros((t, V_HEAD), F32))
        carry = lax.fori_loop(0, qi, lambda kb, cr: block(kb, cr, False), (one, one))
        carry = block(qi, carry, True)
        for hh in range(2):
            m, l, acc = carry[hh]
            o_ref[:, hh * V_HEAD:(hh + 1) * V_HEAD] = (acc / l).astype(o_ref.dtype)
            l_ref[:, hh:hh + 1] = m + jnp.log(l)

    return pl.pallas_call(
        body, name="attn_fwd", grid=(hp, nq),
        in_specs=[pl.BlockSpec((t, 2 * QK_NOPE), lambda h, i: (i, h)), pl.BlockSpec((t, LANES), lambda h, i: (i, h)),
                  pl.BlockSpec((s, 4 * QK_NOPE), lambda h, i: (0, h)), _full((s, LANES)),
                  pl.BlockSpec((t, LANES), lambda h, i: (i, 0)), pl.BlockSpec((t, LANES), lambda h, i: (i, 0))],
        out_specs=[pl.BlockSpec((t, 2 * V_HEAD), lambda h, i: (i, h)), pl.BlockSpec((t, LANES), lambda h, i: (i, h)),
                   pl.BlockSpec((None, t, 2), lambda h, i: (h, i, 0))],
        out_shape=[jax.ShapeDtypeStruct((s, HEADS * V_HEAD), ACT), jax.ShapeDtypeStruct((s, HEADS * QK_ROPE), BF16),
                   jax.ShapeDtypeStruct((hp, s, 2), F32)],
        scratch_shapes=[pltpu.VMEM((2, s, 2 * QK_NOPE), BF16)],
        compiler_params=_params("parallel", "arbitrary"))(qn, qp, kv, kpr, cos4, sin4)


def attn_bwd(qn, qpr, kv, kpr, o, do, lse, cos4, sin4):
    s = qn.shape[0]
    hp = HEADS // 2
    t = _tile(s, ATT_TILE)
    nk = s // t

    def body(qn_ref, qpr_ref, kv_ref, kp_ref, o_ref, do_ref, l_ref, c_ref, s_ref,
             dqn_ref, dqp_ref, dkv_ref, dkp_ref, qcat_ref, dq_ref, delta_ref):
        ki = pl.program_id(1)

        @pl.when(ki == 0)
        def _():
            dq_ref[...] = jnp.zeros_like(dq_ref)
            for hh in range(2):
                qcat_ref[hh] = _q_cat(qn_ref[:, hh * QK_NOPE:(hh + 1) * QK_NOPE], qpr_ref[...], hh)
                cols = slice(hh * V_HEAD, (hh + 1) * V_HEAD)
                delta_ref[hh] = jnp.sum(do_ref[:, cols].astype(F32) * o_ref[:, cols].astype(F32), axis=-1, keepdims=True)

        rows_k = pl.ds(pl.multiple_of(ki * t, t), t)
        kcat = [jnp.concatenate([kv_ref[rows_k, 2 * hh * QK_NOPE:(2 * hh + 1) * QK_NOPE], kp_ref[rows_k, :]], axis=1) for hh in range(2)]
        vs = [kv_ref[rows_k, (2 * hh + 1) * QK_NOPE:(2 * hh + 2) * QK_NOPE] for hh in range(2)]

        def block(qb, carry, diagonal):
            rows = pl.ds(pl.multiple_of(qb * t, t), t)
            out = []
            for hh in range(2):
                dkc, dv = carry[hh]
                q_c = qcat_ref[hh, rows, :]
                do_b = do_ref[rows, hh * V_HEAD:(hh + 1) * V_HEAD].astype(BF16)
                sc = _dot_nt(q_c, kcat[hh]) * _ATT_SCALE
                if diagonal:
                    sc = _causal(sc)
                p = jnp.exp(sc - l_ref[rows, hh:hh + 1])
                dpv = _dot_nt(do_b, vs[hh])
                ds = (p * (dpv - delta_ref[hh, rows, :]) * _ATT_SCALE).astype(BF16)
                dv = dv + _dot_tn(p.astype(BF16), do_b)
                dkc = dkc + _dot_tn(ds, q_c)
                dq_ref[hh, rows, :] += jnp.dot(ds, kcat[hh], preferred_element_type=F32)
                out.append((dkc, dv))
            return tuple(out)

        one = (jnp.zeros((t, 2 * QK_NOPE), F32), jnp.zeros((t, V_HEAD), F32))
        carry = block(ki, (one, one), True)
        carry = lax.fori_loop(ki + 1, nk, lambda qb, cr: block(qb, cr, False), carry)
        dkp = jnp.zeros((t, LANES), F32)
        for hh in range(2):
            dkc, dv = carry[hh]
            dkv_ref[:, 2 * hh * QK_NOPE:(2 * hh + 1) * QK_NOPE] = dkc[:, :QK_NOPE].astype(dkv_ref.dtype)
            dkv_ref[:, (2 * hh + 1) * QK_NOPE:(2 * hh + 2) * QK_NOPE] = dv.astype(dkv_ref.dtype)
            dkp = dkp + dkc[:, QK_NOPE:]
        dkp_ref[...] = dkp

        @pl.when(ki == nk - 1)
        def _():
            lane = lax.broadcasted_iota(jnp.int32, (s, LANES), 1)
            dqp = jnp.where(lane < QK_ROPE, dq_ref[0, :, QK_NOPE:], dq_ref[1, :, QK_NOPE:])
            dqp_ref[...] = _rope(dqp, c_ref[...], -s_ref[...]).astype(dqp_ref.dtype)
            for hh in range(2):
                dqn_ref[:, hh * QK_NOPE:(hh + 1) * QK_NOPE] = dq_ref[hh, :, :QK_NOPE].astype(dqn_ref.dtype)

    qblk = pl.BlockSpec((s, 2 * QK_NOPE), lambda h, i: (0, h))
    pblk = pl.BlockSpec((s, LANES), lambda h, i: (0, h))
    tab = _full((s, LANES))
    return pl.pallas_call(
        body, name="attn_bwd", grid=(hp, nk),
        in_specs=[qblk, pblk, pl.BlockSpec((s, 4 * QK_NOPE), lambda h, i: (0, h)), tab, qblk, qblk,
                  pl.BlockSpec((None, s, 2), lambda h, i: (h, 0, 0)), tab, tab],
        out_specs=[qblk, pblk, pl.BlockSpec((t, 4 * QK_NOPE), lambda h, i: (i, h)), pl.BlockSpec((None, t, LANES), lambda h, i: (h, i, 0))],
        out_shape=[jax.ShapeDtypeStruct((s, HEADS * QK_NOPE), BF16), jax.ShapeDtypeStruct((s, HEADS * QK_ROPE), BF16),
                   jax.ShapeDtypeStruct((s, HEADS * 2 * QK_NOPE), BF16), jax.ShapeDtypeStruct((hp, s, LANES), F32)],
        scratch_shapes=[pltpu.VMEM((2, s, 2 * QK_NOPE), BF16), pltpu.VMEM((2, s, 2 * QK_NOPE), F32), pltpu.VMEM((2, s, 1), F32)],
        compiler_params=_params("parallel", "arbitrary"))(qn, qpr, kv, kpr, o, do, lse, cos4, sin4)


def kpe_bwd(dkp, cos4, sin4, pad_cols):
    hp, s, _ = dkp.shape
    tr = _tile(s, ROW_TILE * 2)

    def body(d_ref, c_ref, s_ref, o_ref):
        tot = d_ref[0]
        for h in range(1, hp):
            tot = tot + d_ref[h]
        tot = tot + pltpu.roll(tot, QK_ROPE, 1)
        lane = lax.broadcasted_iota(jnp.int32, tot.shape, 1)
        dk = jnp.where(lane < QK_ROPE, _rope(tot, c_ref[...], -s_ref[...]), jnp.zeros_like(tot))
        o_ref[...] = jnp.zeros_like(o_ref)
        o_ref[:, 0:LANES] = dk.astype(o_ref.dtype)

    row = pl.BlockSpec((tr, LANES), lambda i: (i, 0))
    return pl.pallas_call(body, name="kpe_bwd", grid=(s // tr,),
                          in_specs=[pl.BlockSpec((hp, tr, LANES), lambda i: (0, i, 0)), row, row],
                          out_specs=pl.BlockSpec((tr, pad_cols), lambda i: (i, 0)),
                          out_shape=jax.ShapeDtypeStruct((s, pad_cols), BF16), compiler_params=_params("parallel"))(dkp, cos4, sin4)


def _shift_down(x, n):
    row = lax.broadcasted_iota(jnp.int32, x.shape, 0)
    return jnp.where(row >= n, pltpu.roll(x, n, 0), jnp.zeros_like(x))


def _shift_up(x, n):
    rows = x.shape[0]
    row = lax.broadcasted_iota(jnp.int32, x.shape, 0)
    return jnp.where(row < rows - n, pltpu.roll(x, rows - n, 0), jnp.zeros_like(x))


def _conv(x, w_ref, b_ref):
    return w_ref[2:3, :] * x + w_ref[1:2, :] * _shift_down(x, 1) + w_ref[0:1, :] * _shift_down(x, 2) + b_ref[...]


def conv_act_fwd(upre, conv_w, conv_b):
    s, f2 = upre.shape
    f = f2 // 2
    tc = _tile(f, COL_TILE)
    nc = f // tc

    def body(ug_ref, uv_ref, wg_ref, wv_ref, bg_ref, bv_ref, o_ref, gv_ref):
        gh = _conv(_f32(ug_ref), wg_ref, bg_ref)
        vh = _conv(_f32(uv_ref), wv_ref, bv_ref)
        o_ref[...] = (gh * _sigmoid(gh) * vh).astype(o_ref.dtype)
        gv_ref[0] = gh.astype(gv_ref.dtype)
        gv_ref[1] = vh.astype(gv_ref.dtype)

    def spec(rows, shift):
        return pl.BlockSpec((rows, tc), lambda j: (0, j + shift))

    return pl.pallas_call(
        body, name="conv_act_fwd", grid=(nc,),
        in_specs=[spec(s, 0), spec(s, nc), spec(3, 0), spec(3, nc), spec(1, 0), spec(1, nc)],
        out_specs=[spec(s, 0), pl.BlockSpec((2, s, tc), lambda j: (0, 0, j))],
        out_shape=[jax.ShapeDtypeStruct((s, f), BF16), jax.ShapeDtypeStruct((2, s, f), ACT)],
        compiler_params=_params("parallel"))(upre, upre, conv_w, conv_w, conv_b, conv_b)


def conv_act_bwd(upre, conv_w, gv, df):
    s, f2 = upre.shape
    f = f2 // 2
    tc = _tile(f, COL_TILE)
    nc = f // tc

    def half(x, d, w_ref, du_ref, which, gw_ref, gb_ref):
        d1, d2 = _shift_up(d, 1), _shift_up(d, 2)
        gb_ref[...] = _colsum(d)
        gw_ref[2:3, :] = _colsum(d * x)
        gw_ref[1:2, :] = _colsum(d1 * x)
        gw_ref[0:1, :] = _colsum(d2 * x)
        du_ref[which] = (w_ref[2:3, :] * d + w_ref[1:2, :] * d1 + w_ref[0:1, :] * d2).astype(du_ref.dtype)

    def body(ug_ref, uv_ref, wg_ref, wv_ref, gv_ref, df_ref, du_ref, gwg_ref, gwv_ref, gbg_ref, gbv_ref):
        xg, xv = _f32(ug_ref), _f32(uv_ref)
        gh, vh = gv_ref[0].astype(F32), gv_ref[1].astype(F32)
        sg = _sigmoid(gh)
        df_v = _f32(df_ref)
        half(xg, df_v * vh * (sg * (1.0 + gh * (1.0 - sg))), wg_ref, du_ref, 0, gwg_ref, gbg_ref)
        half(xv, df_v * (gh * sg), wv_ref, du_ref, 1, gwv_ref, gbv_ref)

    def spec(rows, shift):
        return pl.BlockSpec((rows, tc), lambda j: (0, j + shift))

    gw = jax.ShapeDtypeStruct((3, f), F32)
    gb = jax.ShapeDtypeStruct((1, f), F32)
    return pl.pallas_call(
        body, name="conv_act_bwd", grid=(nc,),
        in_specs=[spec(s, 0), spec(s, nc), spec(3, 0), spec(3, nc), pl.BlockSpec((2, s, tc), lambda j: (0, 0, j)), spec(s, 0)],
        out_specs=[pl.BlockSpec((2, s, tc), lambda j: (0, 0, j)), spec(3, 0), spec(3, 0), spec(1, 0), spec(1, 0)],
        out_shape=[jax.ShapeDtypeStruct((2, s, f), BF16), gw, gw, gb, gb],
        compiler_params=_params("parallel"))(upre, upre, conv_w, conv_w, gv, df)


def _elementwise_tile(r, c, limit):
    if r % 8:
        return r, c
    best = (8, c if c % LANES else LANES)
    for k in (1, 2, 4, 8, 16):
        if k > 1 and c % (LANES * k):
            continue
        tc = c // k
        tr = max(8, min(r, limit // tc) // 8 * 8)
        while r % tr:
            tr -= 8
        if tr * tc <= max(limit, 8 * tc) and tr * tc > best[0] * best[1]:
            best = (tr, tc)
    return best


def adamw(name, w, m, v, parts):
    npart, r, c = parts.shape
    tr, tc = _elementwise_tile(r, c, ADAMW_TILE_ELEMS)
    bc1 = 1.0 - ADAM_B1 ** ADAM_STEP
    bc2 = 1.0 - ADAM_B2 ** ADAM_STEP

    def body(w_ref, m_ref, v_ref, p_ref, g_ref, d_ref, nm_ref, nv_ref):
        g = p_ref[0].astype(F32)
        for k in range(1, npart):
            g = g + p_ref[k].astype(F32)
        m_new = ADAM_B1 * m_ref[...] + (1.0 - ADAM_B1) * g
        v_new = ADAM_B2 * v_ref[...] + (1.0 - ADAM_B2) * (g * g)
        g_ref[...] = g
        nm_ref[...] = m_new
        nv_ref[...] = v_new
        d_ref[...] = -ADAM_LR * ((m_new / bc1) / (jnp.sqrt(v_new / bc2) + ADAM_EPS) + ADAM_WD * w_ref[...])

    deps = _TOKENS.take()
    blk = pl.BlockSpec((tr, tc), lambda i, j: (i, j))
    out = jax.ShapeDtypeStruct((r, c), F32)
    return pl.pallas_call(
        lambda *refs: body(*refs[:4], *refs[4 + len(deps):]), name=name, grid=(r // tr, c // tc),
        in_specs=[blk, blk, blk, pl.BlockSpec((npart, tr, tc), lambda i, j: (0, i, j))] + [pl.BlockSpec(memory_space=pl.ANY)] * len(deps),
        out_specs=[blk, blk, blk, blk], out_shape=[out, out, out, out],
        compiler_params=_params("parallel", "parallel"))(w, m, v, parts, *deps)


def _position():
    return lax.axis_index("x"), lax.axis_index("y"), lax.axis_index("c")


def _index(p):
    return 4 * p[0] + 2 * p[1] + p[2]


def _peer(me, r):
    return (me[0] ^ ((r >> 2) & 1), me[1] ^ ((r >> 1) & 1), me[2] ^ (r & 1))


_ANY = pl.BlockSpec(memory_space=pl.ANY)


_HBM = pl.BlockSpec(memory_space=pltpu.HBM)
_SEM = pl.BlockSpec(memory_space=pltpu.SEMAPHORE)
_EFFECT = pltpu.SideEffectType.DATAFLOW_SIDE_EFFECTING
_TOKEN = jax.ShapeDtypeStruct((8, LANES), F32)
_VM = pl.BlockSpec(memory_space=pltpu.VMEM)
_SIDE = pltpu.CompilerParams(has_side_effects=_EFFECT)


def _hbm(a):
    return pltpu.with_memory_space_constraint(a, pltpu.HBM)


def _like(a):
    return pltpu.HBM(a.shape, a.dtype)


def _dma_sems(n):
    return pltpu.SemaphoreType.DMA((n,))


def _other_chips(x, y):
    return [(1 - x, y), (x, 1 - y), (1 - x, 1 - y)]


COPY_STREAMS = 8


def _row_chunks(src, dst):
    rows = src.shape[0]
    n = COPY_STREAMS
    while n > 1 and rows % (16 * n):
        n //= 2
    r = rows // n
    return [(src.at[pl.ds(i * r, r)], dst.at[pl.ds(i * r, r)]) for i in range(n)]


class _rcopy:
    def __init__(self, src, dst, send_sem, recv_sem, to):
        self.parts = [pltpu.make_async_remote_copy(src_ref=s, dst_ref=d, send_sem=send_sem, recv_sem=recv_sem, device_id=to, device_id_type=MESH)
                      for s, d in _row_chunks(src, dst)]

    def start(self):
        for cp in self.parts:
            cp.start()

    def wait_send(self):
        for cp in self.parts:
            cp.wait_send()

    def wait_recv(self):
        for cp in self.parts:
            cp.wait_recv()


def _afters(after):
    return list(after) if isinstance(after, (list, tuple)) else [after]


def ag_start(name, shards, after):
    n = len(shards)
    lands = [lax.empty((N_DEV,) + a.shape, a.dtype) for a in shards]
    afters = _afters(after)
    na = len(afters)

    def body(*refs):
        ins, lnd, send_sems, recv_sems, token = refs[:n], refs[n:2 * n], refs[2 * n + na], refs[2 * n + na + 1], refs[4 * n + na + 2]
        x, y, c = _position()
        for w in range(n):
            slot = lnd[w].at[_index((x, y, c))]
            for k, to in enumerate([(x, y, 1 - c)] + [(*chip, c) for chip in _other_chips(x, y)]):
                _rcopy(ins[w], slot, send_sems.at[4 * w + k], recv_sems.at[4 * w + k], to).start()
        token[...] = jnp.zeros_like(token)

    out = pl.pallas_call(
        body, name=name, out_shape=(_dma_sems(4 * n), _dma_sems(4 * n)) + tuple(_like(a) for a in shards + lands) + (_TOKEN,),
        in_specs=[_HBM] * (2 * n) + [_ANY] * na, out_specs=(_SEM, _SEM) + (_HBM,) * (2 * n) + (_VM,),
        input_output_aliases={i: 2 + i for i in range(2 * n)}, compiler_params=_SIDE)(*[_hbm(a) for a in shards + lands], *afters)
    _TOKENS.push(out[-1])
    return out[0], out[1], list(out[2:2 + n]), list(out[2 + n:2 + 2 * n])


def _split_rows(ref):
    rows = ref.shape[0]
    h = rows // 32 * 16
    return ref.at[pl.ds(0, h)], ref.at[pl.ds(h, rows - h)]


def relay_start(name, shards, after):
    n = len(shards)
    lands = [lax.empty((N_DEV,) + a.shape, a.dtype) for a in shards]
    afters = _afters(after)
    na = len(afters)

    def body(*refs):
        ins, lnd, send_sems, recv_sems, token = refs[:n], refs[n:2 * n], refs[2 * n + na], refs[2 * n + na + 1], refs[4 * n + na + 2]
        x, y, c = _position()
        for w in range(n):
            slot = lnd[w].at[_index((x, y, c))]
            for k, to in enumerate([(x, y, 1 - c), (1 - x, y, c), (x, 1 - y, c)]):
                _rcopy(ins[w], slot, send_sems.at[3 * w + k], recv_sems.at[3 * w + k], to).start()
        token[...] = jnp.zeros_like(token)

    out = pl.pallas_call(
        body, name=name, out_shape=(_dma_sems(3 * n), _dma_sems(3 * n)) + tuple(_like(a) for a in shards + lands) + (_TOKEN,),
        in_specs=[_HBM] * (2 * n) + [_ANY] * na, out_specs=(_SEM, _SEM) + (_HBM,) * (2 * n) + (_VM,),
        input_output_aliases={i: 2 + i for i in range(2 * n)}, compiler_params=_SIDE)(*[_hbm(a) for a in shards + lands], *afters)
    _TOKENS.push(out[-1])
    return out[0], out[1], list(out[2:2 + n]), list(out[2 + n:2 + 2 * n])


def relay_pass(name, started, after):
    send, recv, shards, lands = started
    n = len(shards)
    afters = _afters(after)
    na = len(afters)

    def body(*refs):
        ins, lnd, send_sems, recv_sems = refs[:n], refs[n:2 * n], refs[2 * n], refs[2 * n + 1]
        fsend, frecv, psend, precv = refs[2 * n + 2 + na:2 * n + 6 + na]
        token = refs[4 * n + 6 + na]
        x, y, c = _position()
        nbrs = [(1 - x, y, c), (x, 1 - y, c)]
        for w in range(n):
            for j, nbr in enumerate(nbrs):
                slot = lnd[w].at[_index(nbr)]
                _rcopy(ins[w], slot, send_sems.at[3 * w + 1 + j], recv_sems.at[3 * w + 1 + j], nbr).wait_recv()
                _rcopy(slot, slot, fsend.at[2 * w + j], frecv.at[2 * w + j], (x, y, 1 - c)).start()
                part = _split_rows(slot)[j]
                _rcopy(part, part, psend.at[2 * w + j], precv.at[2 * w + j], nbrs[1 - j]).start()
        token[...] = jnp.zeros_like(token)

    out = pl.pallas_call(
        body, name=name, out_shape=(_dma_sems(2 * n),) * 4 + tuple(_like(a) for a in shards + lands) + (_TOKEN,),
        in_specs=[_HBM] * (2 * n) + [_SEM, _SEM] + [_ANY] * na, out_specs=(_SEM,) * 4 + (_HBM,) * (2 * n) + (_VM,),
        input_output_aliases={i: 4 + i for i in range(2 * n)}, compiler_params=_SIDE)(*shards, *lands, send, recv, *afters)
    _TOKENS.push(out[-1])
    return (send, recv) + tuple(out[:4]) + (list(out[4:4 + n]), list(out[4 + n:4 + 2 * n]))


def relay_forward(name, passed, after):
    send, recv, fsend, frecv, psend, precv, shards, lands = passed
    n = len(shards)
    afters = _afters(after)
    na = len(afters)

    def body(*refs):
        ins, lnd, precv_r = refs[:n], refs[n:2 * n], refs[2 * n]
        gsend, grecv, token = refs[2 * n + 1 + na], refs[2 * n + 2 + na], refs[4 * n + 3 + na]
        x, y, c = _position()
        for w in range(n):
            slot = lnd[w].at[_index((1 - x, 1 - y, c))]
            for j, part in enumerate(_split_rows(slot)):
                _rcopy(part, part, precv_r.at[2 * w + j], precv_r.at[2 * w + j], (x, y, 1 - c)).wait_recv()
            _rcopy(slot, slot, gsend.at[w], grecv.at[w], (x, y, 1 - c)).start()
        token[...] = jnp.zeros_like(token)

    out = pl.pallas_call(
        body, name=name, out_shape=(_dma_sems(n), _dma_sems(n)) + tuple(_like(a) for a in shards + lands) + (_TOKEN,),
        in_specs=[_HBM] * (2 * n) + [_SEM] + [_ANY] * na, out_specs=(_SEM, _SEM) + (_HBM,) * (2 * n) + (_VM,),
        input_output_aliases={i: 2 + i for i in range(2 * n)}, compiler_params=_SIDE)(*shards, *lands, precv, *afters)
    _TOKENS.push(out[-1])
    return send, recv, fsend, frecv, psend, out[0], out[1], list(out[2:2 + n]), list(out[2 + n:2 + 2 * n])


def relay_wait(name, forwarded, after):
    send, recv, fsend, frecv, psend, gsend, grecv, shards, lands = forwarded
    n = len(shards)

    def body(*refs):
        ins, lnd = refs[:n], refs[n:2 * n]
        send_sems, recv_sems, fsend_r, frecv_r, psend_r, gsend_r, grecv_r = refs[2 * n:2 * n + 7]
        x, y, c = _position()
        sibling = (x, y, 1 - c)
        for w in range(n):
            own = lnd[w].at[_index((x, y, c))]
            _rcopy(ins[w], lnd[w].at[_index(sibling)], send_sems.at[3 * w], recv_sems.at[3 * w], sibling).wait_recv()
            for j, nbr in enumerate([(1 - x, y, 1 - c), (x, 1 - y, 1 - c)]):
                _rcopy(ins[w], lnd[w].at[_index(nbr)], fsend_r.at[2 * w + j], frecv_r.at[2 * w + j], sibling).wait_recv()
            _rcopy(ins[w], lnd[w].at[_index((1 - x, 1 - y, 1 - c))], gsend_r.at[w], grecv_r.at[w], sibling).wait_recv()
            for k in range(3):
                _rcopy(ins[w], own, send_sems.at[3 * w + k], recv_sems.at[3 * w + k], sibling).wait_send()
            for j in range(2):
                _rcopy(ins[w], own, fsend_r.at[2 * w + j], frecv_r.at[2 * w + j], sibling).wait_send()
                part = _split_rows(own)[j]
                _rcopy(part, part, psend_r.at[2 * w + j], psend_r.at[2 * w + j], sibling).wait_send()
            _rcopy(ins[w], own, gsend_r.at[w], grecv_r.at[w], sibling).wait_send()

    out = pl.pallas_call(
        body, name=name, out_shape=tuple(_like(a) for a in shards + lands),
        in_specs=[_HBM] * (2 * n) + [_SEM] * 7 + [_ANY] * len(_afters(after)),
        out_specs=(_HBM,) * (2 * n), input_output_aliases={i: i for i in range(2 * n)},
        compiler_params=_SIDE)(*shards, *lands, send, recv, fsend, frecv, psend, gsend, grecv, *_afters(after))
    return [lax.dynamic_update_index_in_dim(land, shard, _index(_position()), 0) for shard, land in zip(out[:n], out[n:])]


def ag_forward(name, started, after):
    send, recv, shards, lands = started
    n = len(shards)
    afters = list(after) if isinstance(after, (list, tuple)) else [after]
    na = len(afters)

    def body(*refs):
        ins, lnd, send_sems, recv_sems = refs[:n], refs[n:2 * n], refs[2 * n], refs[2 * n + 1]
        fsend, frecv, token = refs[2 * n + 2 + na], refs[2 * n + 3 + na], refs[4 * n + 4 + na]
        x, y, c = _position()
        for w in range(n):
            for j, chip in enumerate(_other_chips(x, y)):
                slot = lnd[w].at[_index((*chip, c))]
                _rcopy(ins[w], slot, send_sems.at[4 * w + 1 + j], recv_sems.at[4 * w + 1 + j], (*chip, c)).wait_recv()
                _rcopy(slot, slot, fsend.at[3 * w + j], frecv.at[3 * w + j], (x, y, 1 - c)).start()
        token[...] = jnp.zeros_like(token)

    out = pl.pallas_call(
        body, name=name, out_shape=(_dma_sems(3 * n), _dma_sems(3 * n)) + tuple(_like(a) for a in shards + lands) + (_TOKEN,),
        in_specs=[_HBM] * (2 * n) + [_SEM, _SEM] + [_ANY] * na, out_specs=(_SEM, _SEM) + (_HBM,) * (2 * n) + (_VM,),
        input_output_aliases={i: 2 + i for i in range(2 * n)}, compiler_params=_SIDE)(*shards, *lands, send, recv, *afters)
    _TOKENS.push(out[-1])
    return send, recv, out[0], out[1], list(out[2:2 + n]), list(out[2 + n:2 + 2 * n])


def ag_wait(name, forwarded, after):
    send, recv, fsend, frecv, shards, lands = forwarded
    n = len(shards)

    def body(*refs):
        ins, lnd, send_sems, recv_sems, fsend_r, frecv_r = refs[:n], refs[n:2 * n], refs[2 * n], refs[2 * n + 1], refs[2 * n + 2], refs[2 * n + 3]
        x, y, c = _position()
        sibling = (x, y, 1 - c)
        for w in range(n):
            own = lnd[w].at[_index((x, y, c))]
            _rcopy(ins[w], lnd[w].at[_index(sibling)], send_sems.at[4 * w], recv_sems.at[4 * w], sibling).wait_recv()
            for j, chip in enumerate(_other_chips(x, y)):
                _rcopy(ins[w], lnd[w].at[_index((*chip, 1 - c))], fsend_r.at[3 * w + j], frecv_r.at[3 * w + j], sibling).wait_recv()
            for k in range(4):
                _rcopy(ins[w], own, send_sems.at[4 * w + k], recv_sems.at[4 * w + k], sibling).wait_send()
            for j in range(3):
                _rcopy(ins[w], own, fsend_r.at[3 * w + j], frecv_r.at[3 * w + j], sibling).wait_send()

    out = pl.pallas_call(
        body, name=name, out_shape=tuple(_like(a) for a in shards + lands),
        in_specs=[_HBM] * (2 * n) + [_SEM] * 4 + [_ANY] * len(_afters(after)),
        out_specs=(_HBM,) * (2 * n), input_output_aliases={i: i for i in range(2 * n)},
        compiler_params=_SIDE)(*shards, *lands, send, recv, fsend, frecv, *_afters(after))
    return [lax.dynamic_update_index_in_dim(land, shard, _index(_position()), 0) for shard, land in zip(out[:n], out[n:])]


def rs_d2d_start(name, grads):
    n = len(grads)
    lands = [lax.empty((4,) + g.shape[1:], g.dtype) for g in grads]

    def body(*refs):
        ins, lnd, send_sems, recv_sems, token = refs[:n], refs[n:2 * n], refs[2 * n], refs[2 * n + 1], refs[4 * n + 2]
        x, y, c = _position()
        for w in range(n):
            for i in range(4):
                _rcopy(ins[w].at[2 * i + 1 - c], lnd[w].at[i], send_sems.at[4 * w + i], recv_sems.at[4 * w + i], (x, y, 1 - c)).start()
        token[...] = jnp.zeros_like(token)

    out = pl.pallas_call(
        body, name=name, out_shape=(_dma_sems(4 * n), _dma_sems(4 * n)) + tuple(_like(a) for a in grads + lands) + (_TOKEN,),
        in_specs=[_HBM] * (2 * n), out_specs=(_SEM, _SEM) + (_HBM,) * (2 * n) + (_VM,),
        input_output_aliases={i: 2 + i for i in range(2 * n)}, compiler_params=_SIDE)(*[_hbm(a) for a in grads + lands])
    _TOKENS.push(out[-1])
    return out[0], out[1], list(out[2:2 + n]), list(out[2 + n:2 + 2 * n])


def rs_d2d_wait(name, started, after):
    send, recv, grads, lands = started
    n = len(grads)

    def body(*refs):
        ins, lnd, send_sems, recv_sems = refs[:n], refs[n:2 * n], refs[2 * n], refs[2 * n + 1]
        x, y, c = _position()
        for w in range(n):
            for i in range(4):
                cp = _rcopy(ins[w].at[2 * i + 1 - c], lnd[w].at[i], send_sems.at[4 * w + i], recv_sems.at[4 * w + i], (x, y, 1 - c))
                cp.wait_send()
                cp.wait_recv()

    out = pl.pallas_call(
        body, name=name, out_shape=tuple(_like(a) for a in grads + lands),
        in_specs=[_HBM] * (2 * n) + [_SEM, _SEM] + [_ANY] * len(_afters(after)),
        out_specs=(_HBM,) * (2 * n), input_output_aliases={i: i for i in range(2 * n)},
        compiler_params=_SIDE)(*grads, *lands, send, recv, *_afters(after))
    return list(out[:n]), list(out[n:])


def pair_sum(name, grad, land, core):
    _, r, c = grad.shape
    tr = r
    if r % 8 == 0:
        tr = max(8, min(r, 4 * ADAMW_TILE_ELEMS // c) // 8 * 8)
        while r % tr:
            tr -= 8

    def body(core_ref, a_ref, b_ref, o_ref):
        o_ref[...] = (a_ref[...].astype(F32) + b_ref[...].astype(F32)).astype(o_ref.dtype)

    return pl.pallas_call(
        body, name=name, out_shape=jax.ShapeDtypeStruct((4, r, c), grad.dtype),
        grid_spec=pltpu.PrefetchScalarGridSpec(
            num_scalar_prefetch=1, grid=(4, r // tr),
            in_specs=[pl.BlockSpec((None, None, tr, c), lambda i, j, core_ref: (i, core_ref[0], j, 0)),
                      pl.BlockSpec((None, tr, c), lambda i, j, core_ref: (i, j, 0))],
            out_specs=pl.BlockSpec((None, tr, c), lambda i, j, core_ref: (i, j, 0))),
        compiler_params=_params("parallel", "parallel"))(core, grad.reshape(4, 2, r, c), land)


def rs_ici_start(name, sums):
    n = len(sums)
    lands = [lax.empty(a.shape, a.dtype) for a in sums]

    def body(*refs):
        ins, lnd, send_sems, recv_sems, token = refs[:n], refs[n:2 * n], refs[2 * n], refs[2 * n + 1], refs[4 * n + 2]
        x, y, c = _position()
        chip = 2 * x + y
        for w in range(n):
            for j, other in enumerate(_other_chips(x, y)):
                _rcopy(ins[w].at[2 * other[0] + other[1]], lnd[w].at[chip], send_sems.at[3 * w + j], recv_sems.at[3 * w + j], (*other, c)).start()
        token[...] = jnp.zeros_like(token)

    out = pl.pallas_call(
        body, name=name, out_shape=(_dma_sems(3 * n), _dma_sems(3 * n)) + tuple(_like(a) for a in sums + lands) + (_TOKEN,),
        in_specs=[_HBM] * (2 * n), out_specs=(_SEM, _SEM) + (_HBM,) * (2 * n) + (_VM,),
        input_output_aliases={i: 2 + i for i in range(2 * n)}, compiler_params=_SIDE)(*[_hbm(a) for a in sums + lands])
    _TOKENS.push(out[-1])
    return out[0], out[1], list(out[2:2 + n]), list(out[2 + n:2 + 2 * n])


def rs_ici_wait(name, started, after):
    send, recv, sums, lands = started
    n = len(sums)

    def body(*refs):
        ins, lnd, send_sems, recv_sems = refs[:n], refs[n:2 * n], refs[2 * n], refs[2 * n + 1]
        x, y, c = _position()
        for w in range(n):
            for j, other in enumerate(_other_chips(x, y)):
                cp = _rcopy(ins[w].at[2 * other[0] + other[1]], lnd[w].at[2 * other[0] + other[1]], send_sems.at[3 * w + j], recv_sems.at[3 * w + j], (*other, c))
                cp.wait_send()
                cp.wait_recv()

    out = pl.pallas_call(
        body, name=name, out_shape=tuple(_like(a) for a in sums + lands), in_specs=[_HBM] * (2 * n) + [_SEM, _SEM, _ANY],
        out_specs=(_HBM,) * (2 * n), input_output_aliases={i: i for i in range(2 * n)}, compiler_params=_SIDE)(*sums, *lands, send, recv, after)
    chip = 2 * lax.axis_index("x") + lax.axis_index("y")
    return [lax.dynamic_update_index_in_dim(land, lax.dynamic_index_in_dim(s, chip, 0, keepdims=False), chip, 0)
            for s, land in zip(out[:n], out[n:])]


def ada_fwd(c, w_ada, b_ada3, conv_w, after):
    d, cs = w_ada.shape

    def body(c_ref, w_ref, b_ref, cw_ref, after_ref, mod_ref, sc_ref, cwa_ref, part_ref, send_sems, recv_sems):
        me = _position()
        my = _index(me)
        cv = c_ref[...]
        sc_ref[my] = cv * _sigmoid(cv)
        cwa_ref[my] = cw_ref[...]
        gather = []
        for r in range(1, N_DEV):
            for k, ref in enumerate((sc_ref, cwa_ref)):
                cp = pltpu.make_async_remote_copy(src_ref=ref.at[my], dst_ref=ref.at[my], send_sem=send_sems.at[14 * k + r - 1],
                                                  recv_sem=recv_sems.at[14 * k + r - 1], device_id=_peer(me, r), device_id_type=MESH)
                cp.start()
                gather.append(cp)
        for cp in gather:
            cp.wait()
        sc_all = jnp.concatenate([sc_ref[k] for k in range(N_DEV)], axis=0).astype(BF16)
        part = jnp.dot(sc_all, w_ref[...].astype(BF16), preferred_element_type=F32)
        for k in range(N_DEV):
            part_ref[k] = part[k:k + 1, :]
        scatter = []
        for r in range(1, N_DEV):
            peer = _peer(me, r)
            cp = pltpu.make_async_remote_copy(src_ref=part_ref.at[_index(peer)], dst_ref=mod_ref.at[my], send_sem=send_sems.at[6 + r],
                                              recv_sem=recv_sems.at[6 + r], device_id=peer, device_id_type=MESH)
            cp.start()
            scatter.append(cp)
        mod_ref[my] = part_ref[my]
        for cp in scatter:
            cp.wait()
        mod_ref[...] = mod_ref[...] + b_ref[...]

    vm = pl.BlockSpec(memory_space=pltpu.VMEM)
    return pl.pallas_call(
        body, name="ada_fwd",
        out_shape=[jax.ShapeDtypeStruct((N_DEV, 1, cs), F32), jax.ShapeDtypeStruct((N_DEV, 1, d), F32),
                   jax.ShapeDtypeStruct((N_DEV,) + conv_w.shape, F32)],
        in_specs=[vm, vm, vm, vm, _ANY], out_specs=[vm, vm, vm],
        scratch_shapes=[pltpu.VMEM((N_DEV, 1, cs), F32), pltpu.SemaphoreType.DMA((21,)), pltpu.SemaphoreType.DMA((21,))],
        compiler_params=pltpu.CompilerParams(vmem_limit_bytes=VMEM_LIMIT_BYTES))(c, w_ada, b_ada3, conv_w, after)


def ada_bwd_w(sc_all, dmod_cols):
    _, d = sc_all.shape
    cs = dmod_cols.shape[1]
    tr = _tile(d, ROW_TILE)

    def body(sc_ref, dm_ref, o_ref):
        dm = dm_ref[...].astype(BF16)
        o_ref[...] = lax.dot_general(sc_ref[...].astype(BF16), dm, (((0,), (0,)), ((), ())), preferred_element_type=F32)

    return pl.pallas_call(body, name="ada_bwd_w", grid=(d // tr,),
                          in_specs=[pl.BlockSpec((N_DEV, tr), lambda i: (0, i)), _full((N_DEV, cs))],
                          out_specs=pl.BlockSpec((None, tr, cs), lambda i: (0, i, 0)),
                          out_shape=jax.ShapeDtypeStruct((1, d, cs), F32), compiler_params=_params("parallel"))(sc_all, dmod_cols)


def _round_up(n, m):
    return (n + m - 1) // m * m


def kernel(x, c, positions, w_ada, b_ada, pre_norm1_g, w_in, gm_ln_g, gm_ln_b, gm_w_s, gm_b_s, w_branch_a, q_norm_g, w_uq, kv_norm_g, w_ukv, w_branch_b, w_out, post_norm1_g, pre_norm2_g, w_up, conv_w, conv_b, w_down, post_norm2_g, loss_target, m_w_ada, m_b_ada, m_pre_norm1_g, m_w_in, m_gm_ln_g, m_gm_ln_b, m_gm_w_s, m_gm_b_s, m_w_branch_a, m_q_norm_g, m_w_uq, m_kv_norm_g, m_w_ukv, m_w_branch_b, m_w_out, m_post_norm1_g, m_pre_norm2_g, m_w_up, m_conv_w, m_conv_b, m_w_down, m_post_norm2_g, v_w_ada, v_b_ada, v_pre_norm1_g, v_w_in, v_gm_ln_g, v_gm_ln_b, v_gm_w_s, v_gm_b_s, v_w_branch_a, v_q_norm_g, v_w_uq, v_kv_norm_g, v_w_ukv, v_w_branch_b, v_w_out, v_post_norm1_g, v_pre_norm2_g, v_w_up, v_conv_w, v_conv_b, v_w_down, v_post_norm2_g):
    weights = dict(w_ada=w_ada, b_ada=b_ada, pre_norm1_g=pre_norm1_g, w_in=w_in, gm_ln_g=gm_ln_g, gm_ln_b=gm_ln_b, gm_w_s=gm_w_s,
                   gm_b_s=gm_b_s, w_branch_a=w_branch_a, q_norm_g=q_norm_g, w_uq=w_uq, kv_norm_g=kv_norm_g, w_ukv=w_ukv,
                   w_branch_b=w_branch_b, w_out=w_out, post_norm1_g=post_norm1_g, pre_norm2_g=pre_norm2_g, w_up=w_up, conv_w=conv_w,
                   conv_b=conv_b, w_down=w_down, post_norm2_g=post_norm2_g)
    mom1 = dict(w_ada=m_w_ada, b_ada=m_b_ada, pre_norm1_g=m_pre_norm1_g, w_in=m_w_in, gm_ln_g=m_gm_ln_g, gm_ln_b=m_gm_ln_b,
                gm_w_s=m_gm_w_s, gm_b_s=m_gm_b_s, w_branch_a=m_w_branch_a, q_norm_g=m_q_norm_g, w_uq=m_w_uq, kv_norm_g=m_kv_norm_g,
                w_ukv=m_w_ukv, w_branch_b=m_w_branch_b, w_out=m_w_out, post_norm1_g=m_post_norm1_g, pre_norm2_g=m_pre_norm2_g,
                w_up=m_w_up, conv_w=m_conv_w, conv_b=m_conv_b, w_down=m_w_down, post_norm2_g=m_post_norm2_g)
    mom2 = dict(w_ada=v_w_ada, b_ada=v_b_ada, pre_norm1_g=v_pre_norm1_g, w_in=v_w_in, gm_ln_g=v_gm_ln_g, gm_ln_b=v_gm_ln_b,
                gm_w_s=v_gm_w_s, gm_b_s=v_gm_b_s, w_branch_a=v_w_branch_a, q_norm_g=v_q_norm_g, w_uq=v_w_uq, kv_norm_g=v_kv_norm_g,
                w_ukv=v_w_ukv, w_branch_b=v_w_branch_b, w_out=v_w_out, post_norm1_g=v_post_norm1_g, pre_norm2_g=v_pre_norm2_g,
                w_up=v_w_up, conv_w=v_conv_w, conv_b=v_conv_b, w_down=v_w_down, post_norm2_g=v_post_norm2_g)
    order = list(weights)
    _TOKENS.clear()

    s, d = x.shape[1], x.shape[2]
    gmw = gm_ln_g.shape[0]
    groups = gmw // CHUNK
    ql, kvl = q_norm_g.shape[0], kv_norm_g.shape[0]
    f2 = conv_b.shape[0]
    in_cols = w_in.shape[1] * N_DEV
    o_q, o_kv, o_ga, o_gb, o_kpe = 2 * gmw, 2 * gmw + ql, 2 * gmw + ql + kvl, 2 * gmw + ql + kvl + d, 2 * gmw + ql + kvl + 2 * d
    zp = _round_up(o_kpe + LANES, Z_PAD)
    src_kpe = 2 * gmw + ql + kvl
    assert src_kpe + QK_ROPE + 2 * d == in_cols
    my = 4 * lax.axis_index("x") + 2 * lax.axis_index("y") + lax.axis_index("c")

    x2, tgt = x[0], loss_target[0]
    row = lambda a: a.reshape(1, -1)

    big = ["w_in", "w_branch_a", "w_uq", "w_ukv", "w_branch_b", "w_out", "w_up", "w_down"]
    sh = {k: weights[k].astype(BF16) for k in big[1:]}
    mix = ["w_branch_a", "w_uq", "w_ukv", "w_branch_b", "w_out"]
    w_in_t = w_in.T.astype(BF16)

    mod8, sc_all3, g_cw = ada_fwd(c, w_ada, b_ada.reshape(N_DEV, 1, -1), conv_w, w_in_t)
    ag_in = relay_start("relay_start_in", [w_in_t], mod8)
    mod = mod8.reshape(N_MOD, d)
    shift1, scale1, gate1, shift2, scale2, gate2 = (mod[i:i + 1] for i in range(N_MOD))
    sc_all = sc_all3.reshape(N_DEV, d)
    h1 = norm_mod_fwd("pre1_fwd", x2, row(pre_norm1_g), scale1, shift1)

    inv = ROPE_THETA ** (-jnp.arange(0, QK_ROPE, 2, dtype=F32) / QK_ROPE)
    ang = positions[0].astype(F32)[:, None] * inv
    cos4 = jnp.tile(jnp.cos(ang), (1, 4))
    sin4 = jnp.tile(jnp.concatenate([-jnp.sin(ang), jnp.sin(ang)], axis=1), (1, 2))

    wm = (gm_w_s * jnp.tril(jnp.ones((CHUNK, CHUNK), F32))).astype(BF16)
    bs3 = gm_b_s.reshape(groups, CHUNK, 1)
    ln_g, ln_b = row(gm_ln_g), row(gm_ln_b)

    small_names = ["pre_norm1_g", "gm_ln_g", "gm_ln_b", "gm_b_s", "q_norm_g", "kv_norm_g", "post_norm1_g", "pre_norm2_g", "conv_b",
                   "post_norm2_g", "gm_w_s", "b_ada"]
    n_small_early = sum(weights[k].size for k in small_names)
    n_pack_early = _round_up(n_small_early + 3 * f2, PACK_ALIGN)

    def pack(src):
        return jnp.concatenate([src[k].reshape(-1) for k in small_names] + [jnp.zeros((n_pack_early - n_small_early,), F32)]).reshape(-1, LANES)

    packed_state = [pack(weights), pack(mom1), pack(mom2)]

    early = [h1, cos4, sin4, wm] + [sh[k] for k in big[1:]] + packed_state
    ag_in = relay_pass("relay_pass_in", ag_in, early)
    ag_in = relay_forward("relay_forward_in", ag_in, _TOKENS.pending[-1])
    ag_mix = ag_start("ag_start_mix", [sh[k] for k in mix], _TOKENS.pending[-1])
    (g_in,) = relay_wait("relay_wait_in", ag_in, [h1, _TOKENS.pending[-1]])
    cs_in = w_in.shape[1]

    def w_in_rows(lo, hi):
        return [g_in[k, max(lo - k * cs_in, 0):min(hi - k * cs_in, cs_in)] for k in range(N_DEV) if lo < (k + 1) * cs_in and hi > k * cs_in]

    w_in_p = jnp.concatenate(w_in_rows(0, src_kpe) + w_in_rows(src_kpe + QK_ROPE, in_cols) + w_in_rows(src_kpe, src_kpe + QK_ROPE)
                             + [jnp.zeros((zp - in_cols, d), BF16)], axis=0)

    z = mm_nt("z_proj", h1, w_in_p, ACT)
    ag_mix = ag_forward("ag_forward_mix", ag_mix, z)
    ag_up = ag_start("ag_start_up", [sh["w_up"]], _TOKENS.pending[-1])
    a = gmlp_fwd(z, gmw, ln_g, ln_b, wm, bs3)
    g_a, g_uq, g_ukv, g_b, g_out = ag_wait("ag_wait_mix", ag_mix, [a, _TOKENS.pending[-1]])
    w_a_f, w_b_f, w_out_f = g_a.reshape(-1, d), g_b.reshape(-1, d), g_out.reshape(-1, d)
    w_uq_f = g_uq.transpose(1, 0, 2).reshape(ql, HEADS, QK_NOPE + QK_ROPE)
    w_uq_n = w_uq_f[:, :, :QK_NOPE].reshape(ql, HEADS * QK_NOPE)
    w_uq_r = w_uq_f[:, :, QK_NOPE:].reshape(ql, HEADS * QK_ROPE)
    y_a = mm_nn("branch_a", a, w_a_f, ACT)
    qln = rms_fwd_cols("q_norm", z, o_q, ql, row(q_norm_g))
    kvn = rms_fwd_cols("kv_norm", z, o_kv, kvl, row(kv_norm_g))
    qn = mm_nn("q_nope", qln, w_uq_n, BF16)
    qp = mm_nn("q_rope", qln, w_uq_r, F32)
    kv = mm_nn_b3("kv_up", kvn, g_ukv, BF16)
    kpr = rope_k(z, o_kpe, cos4, sin4)
    o, qpr, lse = attn_fwd(qn, qp, kv, kpr, cos4, sin4)
    ag_up = ag_forward("ag_forward_up", ag_up, o)
    ag_down = ag_start("ag_start_down", [sh["w_down"]], _TOKENS.pending[-1])
    y_b = mm_nn("branch_b", o, w_b_f, ACT)
    merged = merge_fwd(z, o_ga, o_gb, y_a, y_b)
    y1 = mm_nn("out_proj", merged, w_out_f, ACT)
    x1 = post_res_fwd("post1_fwd", x2, y1, gate1, row(post_norm1_g))
    h2 = norm_mod_fwd("pre2_fwd", x1, row(pre_norm2_g), scale2, shift2)
    (g_up,) = ag_wait("ag_wait_up", ag_up, h2)
    upre = mm_nn_b3("up_proj", h2, g_up, ACT)
    ag_down = ag_forward("ag_forward_down", ag_down, upre)
    cw = g_cw.transpose(1, 0, 2).reshape(3, f2)
    cb = row(conv_b)
    f, gv = conv_act_fwd(upre, cw, cb)
    w_down_f = ag_wait("ag_wait_down", ag_down, f)[0].reshape(-1, d)
    ffn = mm_nn("down_proj", f, w_down_f, ACT)
    loss_acc, dout, dffn, acc2 = post2_loss_bwd(x1, ffn, tgt, gate2, row(post_norm2_g))
    loss = lax.psum(loss_acc[0, 0], ("x", "y", "c"))
    _TOKENS.push(jnp.broadcast_to(loss, (8, LANES)))

    blocks = lambda g: g.reshape(N_DEV, g.shape[0] // N_DEV, g.shape[1])
    core = lax.axis_index("c").astype(jnp.int32).reshape(1)
    rs = {}

    def rs_begin(key, grads):
        rs[key] = rs_d2d_start("rs_d2d_start_" + key, grads)

    def rs_middle(key, after):
        grads, lands = rs_d2d_wait("rs_d2d_wait_" + key, rs[key], after)
        sums = [pair_sum("pair_sum_%s_%d" % (key, i), g, l, core) for i, (g, l) in enumerate(zip(grads, lands))]
        rs[key] = rs_ici_start("rs_ici_start_" + key, sums)

    gw_down = mm_tn("g_w_down", f, dffn, BF16)
    rs_begin("down", [blocks(gw_down)])
    df = mm_nt("d_f", dffn, w_down_f, ACT)
    rs_middle("down", df)
    dupre, gcw_g, gcw_v, gcb_g, gcb_v = conv_act_bwd(upre, cw, gv, df)
    gw_up3 = mm_tn_h3("g_w_up", h2, dupre, N_DEV, BF16)
    rs_begin("up", [gw_up3])
    dh2 = mm_nt_h3("d_h2", dupre, g_up, ACT)
    rs_middle("up", dh2)
    dx1, dy1, acc_mid = mid_bwd(dh2, dout, x1, y1, row(pre_norm2_g), scale2, gate1, row(post_norm1_g))
    gw_out = mm_tn("g_w_out", merged, dy1, BF16)
    dmerged = mm_nt("d_merged", dy1, w_out_f, ACT)
    dya, dyb, dga, dgb = merge_bwd(z, o_ga, o_gb, y_a, y_b, dmerged)
    gw_a = mm_tn("g_w_a", a, dya, BF16)
    gw_b = mm_tn("g_w_b", o, dyb, BF16)
    rs_begin("mid", [blocks(gw_out), blocks(gw_a), blocks(gw_b)])
    da = mm_nt("d_a", dya, w_a_f, ACT)
    do = mm_nt("d_o", dyb, w_b_f, ACT)
    rs_middle("mid", do)
    duv, g_ws, g_bs3, acc_gm = gmlp_bwd(z, gmw, da, ln_g, ln_b, wm, bs3)
    dqn, dqp, dkv, dkp = attn_bwd(qn, qpr, kv, kpr, o, do, lse, cos4, sin4)
    dkpe = kpe_bwd(dkp, cos4, sin4, zp - o_kpe)
    dq_cat = jnp.concatenate([dqn, dqp], axis=1)
    w_uq_cat = jnp.concatenate([w_uq_n, w_uq_r], axis=1)
    dqln = mm_nt("d_qln", dq_cat, w_uq_cat, ACT)
    dq_lat, g_qnorm = rms_bwd_cols("q_norm_bwd", dqln, z, o_q, ql, row(q_norm_g))
    dkvn = mm_nt_b3("d_kvn", dkv, g_ukv, ACT)
    dkv_lat, g_kvnorm = rms_bwd_cols("kv_norm_bwd", dkvn, z, o_kv, kvl, row(kv_norm_g))
    dz = jnp.concatenate([duv, dq_lat, dkv_lat, dga, dgb, dkpe], axis=1)
    gw_in_p = mm_tn("g_w_in", dz, h1, BF16)

    def gw_in_rows(lo, hi):
        pieces = []
        for a, b, shift in ((0, src_kpe, 0), (src_kpe, src_kpe + QK_ROPE, o_kpe - src_kpe), (src_kpe + QK_ROPE, in_cols, -QK_ROPE)):
            if lo < b and hi > a:
                pieces.append(gw_in_p[max(lo, a) + shift:min(hi, b) + shift])
        return pieces[0] if len(pieces) == 1 else jnp.concatenate(pieces, axis=0)

    rs_begin("in", [jnp.stack([gw_in_rows(k * cs_in, (k + 1) * cs_in) for k in range(N_DEV)])])
    dh1 = mm_nn("d_h1", dz, w_in_p, ACT)
    grad_x, acc1 = pre1_bwd(dh1, dx1, x2, row(pre_norm1_g), scale1)

    dmod = jnp.concatenate([acc1[0], acc1[1], acc_mid[3], acc_mid[0], acc_mid[1], acc2[0]])
    small = [("pre_norm1_g", acc1[2]), ("gm_ln_g", acc_gm[0]), ("gm_ln_b", acc_gm[1]), ("gm_b_s", g_bs3.reshape(-1)),
             ("q_norm_g", g_qnorm[0]), ("kv_norm_g", g_kvnorm[0]), ("post_norm1_g", acc_mid[4]), ("pre_norm2_g", acc_mid[2]),
             ("conv_b", jnp.concatenate([gcb_g[0], gcb_v[0]])), ("post_norm2_g", acc2[1]), ("gm_w_s", g_ws.reshape(-1)),
             ("b_ada", dmod)]
    n_small = sum(v.shape[0] for _, v in small)
    n_cw = 3 * f2
    n_pack = _round_up(n_small + n_cw, PACK_ALIGN)
    tail = jnp.zeros((n_pack - n_small - n_cw,), F32)
    packed = jnp.concatenate([v for _, v in small] + [jnp.concatenate([gcw_g, gcw_v], axis=1).reshape(-1), tail])
    ag_small = ag_start("ag_start_small", [packed.reshape(-1, LANES)], packed)
    rs_middle("in", [packed, _TOKENS.pending[-1]])

    gw_uq_cat = mm_tn("g_w_uq", qln, dq_cat, BF16)
    gw_uq_f = jnp.concatenate([gw_uq_cat[:, :HEADS * QK_NOPE].reshape(ql, HEADS, QK_NOPE),
                               gw_uq_cat[:, HEADS * QK_NOPE:].reshape(ql, HEADS, QK_ROPE)], axis=2)
    gw_uq3 = gw_uq_f.reshape(ql, N_DEV, -1).transpose(1, 0, 2)
    gw_ukv3 = mm_tn_o3("g_w_ukv", kvn, dkv, N_DEV, BF16)
    rs_begin("mla", [gw_uq3, gw_ukv3])

    res = {}
    last = packed
    for key, names in (("down", ["w_down"]), ("up", ["w_up"]), ("mid", ["w_out", "w_branch_a", "w_branch_b"])):
        parts = rs_ici_wait("rs_ici_wait_" + key, rs[key], last)
        for k, p in zip(names, parts):
            res[k] = adamw("adamw_" + k, weights[k], mom1[k], mom2[k], p)
            last = res[k][0]
        if key == "down":
            rs_middle("mla", last)

    assert [k for k, _ in small] == small_names and n_small == n_small_early
    (gathered,) = ag_wait("ag_wait_small", ag_forward("ag_forward_small", ag_small, last), last)
    sm = [t.reshape(-1) for t in adamw("adamw_small", *packed_state, gathered)]
    off = 0
    for k, v in small:
        res[k] = tuple(t[off:off + v.shape[0]].reshape(weights[k].shape) for t in sm)
        off += v.shape[0]

    cs_cw = conv_w.shape[1]
    g_cw_full = sm[0][n_small:n_small + n_cw].reshape(3, f2)
    g_cw_mine = lax.dynamic_slice(g_cw_full, (0, my * cs_cw), (3, cs_cw))
    res["conv_w"] = adamw("adamw_conv_w", conv_w, mom1["conv_w"], mom2["conv_w"], g_cw_mine[None])

    cs_ada = w_ada.shape[1]
    off_b = n_small - N_MOD * d
    dmod_all = gathered.reshape(N_DEV, -1)[:, off_b:off_b + N_MOD * d]
    dmod_cols = lax.dynamic_slice(dmod_all, (0, my * cs_ada), (N_DEV, cs_ada))
    res["w_ada"] = adamw("adamw_w_ada", w_ada, mom1["w_ada"], mom2["w_ada"], ada_bwd_w(sc_all, dmod_cols))

    (p_in,) = rs_ici_wait("rs_ici_wait_in", rs["in"], res["w_ada"][0])
    w_in_res = adamw("adamw_w_in", w_in.T, mom1["w_in"].T, mom2["w_in"].T, p_in)
    res["w_in"] = tuple(t.T for t in w_in_res)
    for k, p in zip(["w_uq", "w_ukv"], rs_ici_wait("rs_ici_wait_mla", rs["mla"], w_in_res[0])):
        res[k] = adamw("adamw_" + k, weights[k], mom1[k], mom2[k], p)

    _TOKENS.clear()
    outs = [loss, grad_x[None]]
    for i in range(4):
        outs += [res[k][i] for k in order]
    return tuple(outs)
```

```python
import jax
import jax.numpy as jnp
from jax import lax
from jax.experimental import pallas as pl
from jax.experimental.pallas import tpu as pltpu

F32 = jnp.float32
BF16 = jnp.bfloat16
ACT = BF16

N_DEV = 8
HEADS = 16
QK_NOPE = 128
QK_ROPE = 64
V_HEAD = 128
CHUNK = 128
ROPE_THETA = 10000.0
EPS = 1e-6
N_MOD = 6
ADAM_LR, ADAM_B1, ADAM_B2, ADAM_EPS, ADAM_WD, ADAM_STEP = 0.001, 0.9, 0.999, 1e-08, 0.01, 10

LANES = 128
VMEM_LIMIT_BYTES = 48 * 2 ** 20
ROW_TILE = 256
COL_TILE = 256
ATT_TILE = 512
Z_PAD = 512
ADAMW_TILE_ELEMS = 3 << 17
PACK_ALIGN = 8 * LANES
MESH = pl.DeviceIdType.MESH


def _params(*sem):
    return pltpu.CompilerParams(dimension_semantics=sem if sem else None, vmem_limit_bytes=VMEM_LIMIT_BYTES)


def _tile(dim, target):
    t = (min(dim, target) // LANES) * LANES
    while t >= LANES:
        if dim % t == 0:
            return t
        t -= LANES
    return dim


def _full(shape):
    nd = len(shape)
    return pl.BlockSpec(shape, lambda *_: (0,) * nd)


class _Tokens:
    KEEP = 2

    def __init__(self):
        self.pending = []

    def push(self, token):
        self.pending = (self.pending + [token])[-self.KEEP:]

    def take(self):
        return list(self.pending)

    def clear(self):
        self.pending = []


_TOKENS = _Tokens()


def _matmul(name, a, b, *, grid, a_spec, b_spec, o_spec, out_shape, contract, acc_shape, split=1):
    nk = grid[2]
    deps = _TOKENS.take()

    def product(a_ref, b_ref):
        if len(b_ref.shape) == 2:
            return lax.dot_general(a_ref[...].astype(BF16), b_ref[...].astype(BF16), (contract, ((), ())), preferred_element_type=F32)
        cs = b_ref.shape[2]
        return sum(lax.dot_general(a_ref[:, s * cs:(s + 1) * cs].astype(BF16), b_ref[s].astype(BF16), (contract, ((), ())),
                                   preferred_element_type=F32) for s in range(split))

    def body_one_step(a_ref, b_ref, *rest):
        o_ref = rest[len(deps)]
        o_ref[...] = product(a_ref, b_ref).astype(o_ref.dtype)

    def body(a_ref, b_ref, *rest):
        o_ref, acc_ref = rest[len(deps):]
        k = pl.program_id(2)

        @pl.when(k == 0)
        def _():
            acc_ref[...] = jnp.zeros_like(acc_ref)

        acc_ref[...] += product(a_ref, b_ref)

        @pl.when(k == nk - 1)
        def _():
            o_ref[...] = acc_ref[...].astype(o_ref.dtype)

    return pl.pallas_call(
        body_one_step if nk == 1 else body, name=name, grid=grid,
        in_specs=[a_spec, b_spec] + [pl.BlockSpec(memory_space=pl.ANY)] * len(deps),
        out_specs=o_spec, out_shape=out_shape, scratch_shapes=[] if nk == 1 else [pltpu.VMEM(acc_shape, F32)],
        compiler_params=_params("parallel", "parallel", "arbitrary"))(a, b, *deps)


T_OUT, T_OUT_WIDE, TK = 1024, 1408, 2816


def _out_tile(dim):
    return T_OUT_WIDE if dim % T_OUT_WIDE == 0 else _tile(dim, T_OUT)


def _tk(a, b):
    return TK if a.dtype == BF16 and b.dtype == BF16 else TK // 2


def mm_nn(name, a, b, dtype):
    (m, k), n = a.shape, b.shape[1]
    tm, tn, tk = _out_tile(m), _out_tile(n), _tile(k, _tk(a, b))
    return _matmul(name, a, b, grid=(m // tm, n // tn, k // tk),
                   a_spec=pl.BlockSpec((tm, tk), lambda i, j, kk: (i, kk)),
                   b_spec=pl.BlockSpec((tk, tn), lambda i, j, kk: (kk, j)),
                   o_spec=pl.BlockSpec((tm, tn), lambda i, j, kk: (i, j)),
                   out_shape=jax.ShapeDtypeStruct((m, n), dtype), contract=((1,), (0,)), acc_shape=(tm, tn))


def mm_nn_b3(name, a, b3, dtype):
    (m, k), (nj, _, cs) = a.shape, b3.shape
    tm, tk = _out_tile(m), _tile(k, _tk(a, b3))
    return _matmul(name, a, b3, grid=(m // tm, nj, k // tk),
                   a_spec=pl.BlockSpec((tm, tk), lambda i, j, kk: (i, kk)),
                   b_spec=pl.BlockSpec((None, tk, cs), lambda i, j, kk: (j, kk, 0)),
                   o_spec=pl.BlockSpec((tm, cs), lambda i, j, kk: (i, j)),
                   out_shape=jax.ShapeDtypeStruct((m, nj * cs), dtype), contract=((1,), (0,)), acc_shape=(tm, cs))


def mm_nt(name, a, b, dtype):
    (m, k), n = a.shape, b.shape[0]
    tm, tn, tk = _out_tile(m), _out_tile(n), _tile(k, _tk(a, b))
    return _matmul(name, a, b, grid=(m // tm, n // tn, k // tk),
                   a_spec=pl.BlockSpec((tm, tk), lambda i, j, kk: (i, kk)),
                   b_spec=pl.BlockSpec((tn, tk), lambda i, j, kk: (j, kk)),
                   o_spec=pl.BlockSpec((tm, tn), lambda i, j, kk: (i, j)),
                   out_shape=jax.ShapeDtypeStruct((m, n), dtype), contract=((1,), (1,)), acc_shape=(tm, tn))


def mm_nt_b3(name, a, b3, dtype):
    m, (nj, n, cs) = a.shape[0], b3.shape
    tm, tn = _out_tile(m), _out_tile(n)
    return _matmul(name, a, b3, grid=(m // tm, n // tn, nj),
                   a_spec=pl.BlockSpec((tm, cs), lambda i, j, kk: (i, kk)),
                   b_spec=pl.BlockSpec((None, tn, cs), lambda i, j, kk: (kk, j, 0)),
                   o_spec=pl.BlockSpec((tm, tn), lambda i, j, kk: (i, j)),
                   out_shape=jax.ShapeDtypeStruct((m, n), dtype), contract=((1,), (1,)), acc_shape=(tm, tn))


def mm_nt_h3(name, a3, b3, dtype):
    (_, m, _), (nj, n, cs) = a3.shape, b3.shape
    tm, tn, hj = _out_tile(m), _out_tile(n), nj // 2
    pair = 2 if hj % 2 == 0 else 1
    return _matmul(name, a3, b3.reshape(nj // pair, pair, n, cs), grid=(m // tm, n // tn, nj // pair),
                   a_spec=pl.BlockSpec((None, tm, pair * cs), lambda i, j, kk: (kk // (hj // pair), i, kk % (hj // pair))),
                   b_spec=pl.BlockSpec((None, pair, tn, cs), lambda i, j, kk: (kk, 0, j, 0)),
                   o_spec=pl.BlockSpec((tm, tn), lambda i, j, kk: (i, j)),
                   out_shape=jax.ShapeDtypeStruct((m, n), dtype), contract=((1,), (1,)), acc_shape=(tm, tn), split=pair)


def mm_tn_h3(name, a, b3, nj, dtype):
    (k, m), half = a.shape, b3.shape[2]
    hj = nj // 2
    cs = half // hj
    tm, tk = _out_tile(m), _tile(k, _tk(a, b3))
    return _matmul(name, a, b3, grid=(m // tm, nj, k // tk),
                   a_spec=pl.BlockSpec((tk, tm), lambda i, j, kk: (kk, i)),
                   b_spec=pl.BlockSpec((None, tk, cs), lambda i, j, kk: (j // hj, kk, j % hj)),
                   o_spec=pl.BlockSpec((None, tm, cs), lambda i, j, kk: (j, i, 0)),
                   out_shape=jax.ShapeDtypeStruct((nj, m, cs), dtype), contract=((0,), (0,)), acc_shape=(tm, cs))


def mm_tn(name, a, b, dtype):
    (k, m), n = a.shape, b.shape[1]
    tm, tn, tk = _out_tile(m), _out_tile(n), _tile(k, _tk(a, b))
    return _matmul(name, a, b, grid=(m // tm, n // tn, k // tk),
                   a_spec=pl.BlockSpec((tk, tm), lambda i, j, kk: (kk, i)),
                   b_spec=pl.BlockSpec((tk, tn), lambda i, j, kk: (kk, j)),
                   o_spec=pl.BlockSpec((tm, tn), lambda i, j, kk: (i, j)),
                   out_shape=jax.ShapeDtypeStruct((m, n), dtype), contract=((0,), (0,)), acc_shape=(tm, tn))


def mm_tn_o3(name, a, b, nj, dtype):
    (k, m), n = a.shape, b.shape[1]
    cs = n // nj
    tm, tk = _out_tile(m), _tile(k, _tk(a, b))
    return _matmul(name, a, b, grid=(m // tm, nj, k // tk),
                   a_spec=pl.BlockSpec((tk, tm), lambda i, j, kk: (kk, i)),
                   b_spec=pl.BlockSpec((tk, cs), lambda i, j, kk: (kk, j)),
                   o_spec=pl.BlockSpec((None, tm, cs), lambda i, j, kk: (j, i, 0)),
                   out_shape=jax.ShapeDtypeStruct((nj, m, cs), dtype), contract=((0,), (0,)), acc_shape=(tm, cs))


_GELU_C = 0.7978845608028654
_GELU_A = 0.044715


def _f32(ref):
    return ref[...].astype(F32)


def _gelu(x):
    x = x.astype(F32)
    return 0.5 * x * (1.0 + jnp.tanh(_GELU_C * (x + _GELU_A * x * x * x)))


def _gelu_and_grad(x):
    x = x.astype(F32)
    t = jnp.tanh(_GELU_C * (x + _GELU_A * x * x * x))
    y = 0.5 * x * (1.0 + t)
    dy = 0.5 * (1.0 + t) + 0.5 * x * (1.0 - t * t) * (_GELU_C * (1.0 + 3.0 * _GELU_A * x * x))
    return y, dy


def _sigmoid(x):
    return 0.5 * jnp.tanh(0.5 * x.astype(F32)) + 0.5


def _rms_stats(x):
    x = x.astype(F32)
    inv = lax.rsqrt(jnp.mean(x * x, axis=-1, keepdims=True) + EPS)
    return inv, x * inv


def _rms_bwd(dyhat, yhat, inv):
    return inv * (dyhat - yhat * jnp.mean(dyhat * yhat, axis=-1, keepdims=True))


def _colsum(x):
    return jnp.sum(x, axis=0, keepdims=True)


def _rope(x, cos4, sin4):
    lane = lax.broadcasted_iota(jnp.int32, x.shape, x.ndim - 1)
    first_half = (lane % QK_ROPE) < (QK_ROPE // 2)
    partner = jnp.where(first_half, pltpu.roll(x, LANES - QK_ROPE // 2, x.ndim - 1), pltpu.roll(x, QK_ROPE // 2, x.ndim - 1))
    return x * cos4 + partner * sin4


def norm_mod_fwd(name, x, g, scale, shift):
    s, d = x.shape
    tr = _tile(s, ROW_TILE)

    def body(x_ref, g_ref, sc_ref, sh_ref, o_ref):
        _, xh = _rms_stats(x_ref[...])
        o_ref[...] = (xh * g_ref[...] * (1.0 + sc_ref[...]) + sh_ref[...]).astype(o_ref.dtype)

    row = pl.BlockSpec((tr, d), lambda i: (i, 0))
    vec = pl.BlockSpec((1, d), lambda i: (0, 0))
    return pl.pallas_call(body, name=name, grid=(s // tr,), in_specs=[row, vec, vec, vec], out_specs=row,
                          out_shape=jax.ShapeDtypeStruct((s, d), BF16), compiler_params=_params("parallel"))(x, g, scale, shift)


def rms_fwd_cols(name, z, off, width, g):
    s = z.shape[0]
    tr = _tile(s, ROW_TILE)
    assert off % width == 0

    def body(x_ref, g_ref, o_ref):
        _, xh = _rms_stats(x_ref[...])
        o_ref[...] = (xh * g_ref[...]).astype(o_ref.dtype)

    return pl.pallas_call(body, name=name, grid=(s // tr,),
                          in_specs=[pl.BlockSpec((tr, width), lambda i: (i, off // width)), pl.BlockSpec((1, width), lambda i: (0, 0))],
                          out_specs=pl.BlockSpec((tr, width), lambda i: (i, 0)),
                          out_shape=jax.ShapeDtypeStruct((s, width), BF16), compiler_params=_params("parallel"))(z, g)


def rms_bwd_cols(name, dy, z, off, width, g):
    s = z.shape[0]
    tr = _tile(s, ROW_TILE)

    def body(dy_ref, x_ref, g_ref, dx_ref, gg_ref):
        @pl.when(pl.program_id(0) == 0)
        def _():
            gg_ref[...] = jnp.zeros_like(gg_ref)

        inv, xh = _rms_stats(x_ref[...])
        dy_v = _f32(dy_ref)
        gg_ref[...] += _colsum(dy_v * xh)
        dx_ref[...] = _rms_bwd(dy_v * g_ref[...], xh, inv).astype(dx_ref.dtype)

    return pl.pallas_call(body, name=name, grid=(s // tr,),
                          in_specs=[pl.BlockSpec((tr, width), lambda i: (i, 0)), pl.BlockSpec((tr, width), lambda i: (i, off // width)),
                                    pl.BlockSpec((1, width), lambda i: (0, 0))],
                          out_specs=[pl.BlockSpec((tr, width), lambda i: (i, 0)), pl.BlockSpec((1, width), lambda i: (0, 0))],
                          out_shape=[jax.ShapeDtypeStruct((s, width), BF16), jax.ShapeDtypeStruct((1, width), F32)],
                          compiler_params=_params("arbitrary"))(dy, z, g)


def post_res_fwd(name, x, y, gate, g):
    s, d = x.shape
    tr = _tile(s, ROW_TILE)

    def body(x_ref, y_ref, gate_ref, g_ref, o_ref):
        _, yh = _rms_stats(y_ref[...])
        o_ref[...] = x_ref[...] + gate_ref[...] * (yh * g_ref[...])

    row = pl.BlockSpec((tr, d), lambda i: (i, 0))
    vec = pl.BlockSpec((1, d), lambda i: (0, 0))
    return pl.pallas_call(body, name=name, grid=(s // tr,), in_specs=[row, row, vec, vec], out_specs=row,
                          out_shape=jax.ShapeDtypeStruct((s, d), F32), compiler_params=_params("parallel"))(x, y, gate, g)


def post2_loss_bwd(x1, ffn, target, gate2, g):
    s, d = x1.shape
    tr = _tile(s, ROW_TILE)

    def body(x_ref, y_ref, t_ref, gate_ref, g_ref, loss_ref, dout_ref, dy_ref, acc_ref):
        @pl.when(pl.program_id(0) == 0)
        def _():
            loss_ref[...] = jnp.zeros_like(loss_ref)
            acc_ref[...] = jnp.zeros_like(acc_ref)

        inv, yh = _rms_stats(y_ref[...])
        r = yh * g_ref[...]
        err = x_ref[...] + gate_ref[...] * r - t_ref[...]
        loss_ref[...] += 0.5 * jnp.sum(jnp.mean(err * err, axis=-1, keepdims=True))
        dout = err / d
        dout_ref[...] = dout
        dr = dout * gate_ref[...]
        acc_ref[0:1, :] += _colsum(dout * r)
        acc_ref[1:2, :] += _colsum(dr * yh)
        dy_ref[...] = _rms_bwd(dr * g_ref[...], yh, inv).astype(dy_ref.dtype)

    row = pl.BlockSpec((tr, d), lambda i: (i, 0))
    vec = pl.BlockSpec((1, d), lambda i: (0, 0))
    return pl.pallas_call(
        body, name="post2_loss_bwd", grid=(s // tr,), in_specs=[row, row, row, vec, vec],
        out_specs=[_full((8, LANES)), row, row, _full((8, d))],
        out_shape=[jax.ShapeDtypeStruct((8, LANES), F32), jax.ShapeDtypeStruct((s, d), F32),
                   jax.ShapeDtypeStruct((s, d), BF16), jax.ShapeDtypeStruct((8, d), F32)],
        compiler_params=_params("arbitrary"))(x1, ffn, target, gate2, g)


def mid_bwd(dh2, dout, x1, y1, pre2_g, scale2, gate1, post1_g):
    s, d = x1.shape
    tr = _tile(s, ROW_TILE)

    def body(dh_ref, dout_ref, x_ref, y_ref, g2_ref, sc_ref, gate_ref, g1_ref, dx_ref, dy_ref, acc_ref):
        @pl.when(pl.program_id(0) == 0)
        def _():
            acc_ref[...] = jnp.zeros_like(acc_ref)

        dh = _f32(dh_ref)
        inv2, xh = _rms_stats(x_ref[...])
        acc_ref[0:1, :] += _colsum(dh)
        acc_ref[1:2, :] += _colsum(dh * (xh * g2_ref[...]))
        t = dh * (1.0 + sc_ref[...])
        acc_ref[2:3, :] += _colsum(t * xh)
        dx1 = dout_ref[...] + _rms_bwd(t * g2_ref[...], xh, inv2)
        dx_ref[...] = dx1
        inv1, yh = _rms_stats(y_ref[...])
        acc_ref[3:4, :] += _colsum(dx1 * (yh * g1_ref[...]))
        dr = dx1 * gate_ref[...]
        acc_ref[4:5, :] += _colsum(dr * yh)
        dy_ref[...] = _rms_bwd(dr * g1_ref[...], yh, inv1).astype(dy_ref.dtype)

    row = pl.BlockSpec((tr, d), lambda i: (i, 0))
    vec = pl.BlockSpec((1, d), lambda i: (0, 0))
    return pl.pallas_call(
        body, name="mid_bwd", grid=(s // tr,), in_specs=[row, row, row, row, vec, vec, vec, vec],
        out_specs=[row, row, _full((8, d))],
        out_shape=[jax.ShapeDtypeStruct((s, d), F32), jax.ShapeDtypeStruct((s, d), BF16), jax.ShapeDtypeStruct((8, d), F32)],
        compiler_params=_params("arbitrary"))(dh2, dout, x1, y1, pre2_g, scale2, gate1, post1_g)


def pre1_bwd(dh1, dx1, x, pre1_g, scale1):
    s, d = x.shape
    tr = _tile(s, ROW_TILE)

    def body(dh_ref, dx1_ref, x_ref, g_ref, sc_ref, dx_ref, acc_ref):
        @pl.when(pl.program_id(0) == 0)
        def _():
            acc_ref[...] = jnp.zeros_like(acc_ref)

        dh = _f32(dh_ref)
        inv, xh = _rms_stats(x_ref[...])
        acc_ref[0:1, :] += _colsum(dh)
        acc_ref[1:2, :] += _colsum(dh * (xh * g_ref[...]))
        t = dh * (1.0 + sc_ref[...])
        acc_ref[2:3, :] += _colsum(t * xh)
        dx_ref[...] = dx1_ref[...] + _rms_bwd(t * g_ref[...], xh, inv)

    row = pl.BlockSpec((tr, d), lambda i: (i, 0))
    vec = pl.BlockSpec((1, d), lambda i: (0, 0))
    return pl.pallas_call(
        body, name="pre1_bwd", grid=(s // tr,), in_specs=[row, row, row, vec, vec], out_specs=[row, _full((8, d))],
        out_shape=[jax.ShapeDtypeStruct((s, d), F32), jax.ShapeDtypeStruct((8, d), F32)],
        compiler_params=_params("arbitrary"))(dh1, dx1, x, pre1_g, scale1)


def _ln_stats(v):
    mu = jnp.mean(v, axis=-1, keepdims=True)
    vc = v - mu
    rstd = lax.rsqrt(jnp.mean(vc * vc, axis=-1, keepdims=True) + EPS)
    return rstd, vc * rstd


def gmlp_fwd(z, width, ln_g, ln_b, wm, bs3):
    s = z.shape[0]
    groups = width // CHUNK

    def body(u_ref, v_ref, g_ref, b_ref, wm_ref, bs_ref, a_ref):
        ug = _gelu(u_ref[...])
        _, vh = _ln_stats(_gelu(v_ref[...]))
        vn = (vh * g_ref[...] + b_ref[...]).astype(BF16)
        for g in range(groups):
            cols = slice(g * CHUNK, (g + 1) * CHUNK)
            mixed = jnp.dot(wm_ref[g], vn[:, cols], preferred_element_type=F32) + bs_ref[g]
            a_ref[:, cols] = (ug[:, cols] * mixed).astype(a_ref.dtype)

    vec = pl.BlockSpec((1, width), lambda n: (0, 0))
    return pl.pallas_call(
        body, name="gmlp_fwd", grid=(s // CHUNK,),
        in_specs=[pl.BlockSpec((CHUNK, width), lambda n: (n, 0)), pl.BlockSpec((CHUNK, width), lambda n: (n, 1)), vec, vec,
                  _full(wm.shape), _full(bs3.shape)],
        out_specs=pl.BlockSpec((CHUNK, width), lambda n: (n, 0)),
        out_shape=jax.ShapeDtypeStruct((s, width), BF16), compiler_params=_params("parallel"))(z, z, ln_g, ln_b, wm, bs3)


def gmlp_bwd(z, width, da, ln_g, ln_b, wm, bs3):
    s = z.shape[0]
    groups = width // CHUNK

    def body(u_ref, v_ref, da_ref, g_ref, b_ref, wm_ref, bs_ref, duv_ref, gw_ref, gb_ref, acc_ref, dvn_ref):
        @pl.when(pl.program_id(0) == 0)
        def _():
            gw_ref[...] = jnp.zeros_like(gw_ref)
            gb_ref[...] = jnp.zeros_like(gb_ref)
            acc_ref[...] = jnp.zeros_like(acc_ref)

        ug, dug = _gelu_and_grad(u_ref[...])
        vg, dvg = _gelu_and_grad(v_ref[...])
        rstd, vh = _ln_stats(vg)
        vn = (vh * g_ref[...] + b_ref[...]).astype(BF16)
        da_v = _f32(da_ref)
        for g in range(groups):
            cols = slice(g * CHUNK, (g + 1) * CHUNK)
            mixed = jnp.dot(wm_ref[g], vn[:, cols], preferred_element_type=F32) + bs_ref[g]
            duv_ref[:, cols] = (da_v[:, cols] * mixed * dug[:, cols]).astype(duv_ref.dtype)
            dm = da_v[:, cols] * ug[:, cols]
            gb_ref[g] += jnp.sum(dm, axis=-1, keepdims=True)
            dmb = dm.astype(BF16)
            gw_ref[g] += lax.dot_general(dmb, vn[:, cols], (((1,), (1,)), ((), ())), preferred_element_type=F32)
            dvn_ref[:, cols] = lax.dot_general(wm_ref[g], dmb, (((0,), (0,)), ((), ())), preferred_element_type=F32)
        dvn = dvn_ref[...]
        acc_ref[0:1, :] += _colsum(dvn * vh)
        acc_ref[1:2, :] += _colsum(dvn)
        dvh = dvn * g_ref[...]
        dv = rstd * (dvh - jnp.mean(dvh, axis=-1, keepdims=True) - vh * jnp.mean(dvh * vh, axis=-1, keepdims=True))
        duv_ref[:, width:] = (dv * dvg).astype(duv_ref.dtype)

        @pl.when(pl.program_id(0) == pl.num_programs(0) - 1)
        def _():
            q = lax.broadcasted_iota(jnp.int32, gw_ref.shape, 1)
            p = lax.broadcasted_iota(jnp.int32, gw_ref.shape, 2)
            gw_ref[...] = jnp.where(p <= q, gw_ref[...], 0.0)

    vec = pl.BlockSpec((1, width), lambda n: (0, 0))
    blk = pl.BlockSpec((CHUNK, width), lambda n: (n, 0))
    return pl.pallas_call(
        body, name="gmlp_bwd", grid=(s // CHUNK,),
        in_specs=[blk, pl.BlockSpec((CHUNK, width), lambda n: (n, 1)), blk, vec, vec, _full(wm.shape), _full(bs3.shape)],
        out_specs=[pl.BlockSpec((CHUNK, 2 * width), lambda n: (n, 0)), _full(wm.shape), _full(bs3.shape), _full((8, width))],
        out_shape=[jax.ShapeDtypeStruct((s, 2 * width), BF16), jax.ShapeDtypeStruct(wm.shape, F32),
                   jax.ShapeDtypeStruct(bs3.shape, F32), jax.ShapeDtypeStruct((8, width), F32)],
        scratch_shapes=[pltpu.VMEM((CHUNK, width), F32)],
        compiler_params=_params("arbitrary"))(z, z, da, ln_g, ln_b, wm, bs3)


def merge_fwd(z, off_a, off_b, ya, yb):
    s, d = ya.shape
    tr, tc = _tile(s, ROW_TILE * 2), _tile(d, COL_TILE)
    assert off_a % tc == 0 and off_b % tc == 0

    def body(ga_ref, gb_ref, ya_ref, yb_ref, o_ref):
        o_ref[...] = (_sigmoid(ga_ref[...]) * _f32(ya_ref) + _sigmoid(gb_ref[...]) * _f32(yb_ref)).astype(o_ref.dtype)

    blk = pl.BlockSpec((tr, tc), lambda i, j: (i, j))
    return pl.pallas_call(
        body, name="merge_fwd", grid=(s // tr, d // tc),
        in_specs=[pl.BlockSpec((tr, tc), lambda i, j: (i, off_a // tc + j)), pl.BlockSpec((tr, tc), lambda i, j: (i, off_b // tc + j)), blk, blk],
        out_specs=blk, out_shape=jax.ShapeDtypeStruct((s, d), BF16), compiler_params=_params("parallel", "parallel"))(z, z, ya, yb)


def merge_bwd(z, off_a, off_b, ya, yb, dm):
    s, d = ya.shape
    tr, tc = _tile(s, ROW_TILE * 2), _tile(d, COL_TILE)
    nc = d // tc

    def body(ga_ref, gb_ref, ya_ref, yb_ref, dm_ref, dya_ref, dyb_ref, dga_ref, dgb_ref):
        dm_v = _f32(dm_ref)
        sa, sb = _sigmoid(ga_ref[...]), _sigmoid(gb_ref[...])
        dya_ref[...] = (dm_v * sa).astype(dya_ref.dtype)
        dyb_ref[...] = (dm_v * sb).astype(dyb_ref.dtype)
        dga_ref[...] = (dm_v * _f32(ya_ref) * sa * (1.0 - sa)).astype(dga_ref.dtype)
        dgb_ref[...] = (dm_v * _f32(yb_ref) * sb * (1.0 - sb)).astype(dgb_ref.dtype)

    blk = pl.BlockSpec((tr, tc), lambda i, j: (i, j))
    out = jax.ShapeDtypeStruct((s, d), BF16)
    return pl.pallas_call(
        body, name="merge_bwd", grid=(s // tr, nc),
        in_specs=[pl.BlockSpec((tr, tc), lambda i, j: (i, off_a // tc + j)), pl.BlockSpec((tr, tc), lambda i, j: (i, off_b // tc + j)), blk, blk, blk],
        out_specs=[blk, blk, blk, blk], out_shape=[out, out, out, out],
        compiler_params=_params("parallel", "parallel"))(z, z, ya, yb, dm)


_ATT_SCALE = (QK_NOPE + QK_ROPE) ** -0.5
_NEG = -1e30


def rope_k(z, off, cos4, sin4):
    s = z.shape[0]
    tr = _tile(s, ROW_TILE * 2)
    assert off % LANES == 0

    def body(k_ref, c_ref, s_ref, o_ref):
        k = _f32(k_ref)
        k = k + pltpu.roll(k, QK_ROPE, 1)
        o_ref[...] = _rope(k, c_ref[...], s_ref[...]).astype(o_ref.dtype)

    row = pl.BlockSpec((tr, LANES), lambda i: (i, 0))
    return pl.pallas_call(body, name="rope_k", grid=(s // tr,),
                          in_specs=[pl.BlockSpec((tr, LANES), lambda i: (i, off // LANES)), row, row], out_specs=row,
                          out_shape=jax.ShapeDtypeStruct((s, LANES), BF16), compiler_params=_params("parallel"))(z, cos4, sin4)


def _dot_nt(a, b):
    return lax.dot_general(a, b, (((1,), (1,)), ((), ())), preferred_element_type=F32)


def _dot_tn(a, b):
    return lax.dot_general(a, b, (((0,), (0,)), ((), ())), preferred_element_type=F32)


def _q_cat(q_n, qpr, hh):
    lane = lax.broadcasted_iota(jnp.int32, qpr.shape, 1)
    sel = (lane < QK_ROPE) if hh == 0 else (lane >= QK_ROPE)
    return jnp.concatenate([q_n, jnp.where(sel, qpr, jnp.zeros_like(qpr))], axis=1)


def _causal(sc):
    row = lax.broadcasted_iota(jnp.int32, sc.shape, 0)
    col = lax.broadcasted_iota(jnp.int32, sc.shape, 1)
    return jnp.where(col <= row, sc, _NEG)


def attn_fwd(qn, qp, kv, kpr, cos4, sin4):
    s = qn.shape[0]
    hp = HEADS // 2
    t = _tile(s, ATT_TILE)
    nq = s // t

    def body(qn_ref, qp_ref, kv_ref, kp_ref, c_ref, s_ref, o_ref, qpr_ref, l_ref, kcat_ref):
        qi = pl.program_id(1)

        @pl.when(qi == 0)
        def _():
            for hh in range(2):
                kcat_ref[hh, :, 0:QK_NOPE] = kv_ref[:, 2 * hh * QK_NOPE:(2 * hh + 1) * QK_NOPE]
                kcat_ref[hh, :, QK_NOPE:] = kp_ref[...]

        qpr = _rope(qp_ref[...], c_ref[...], s_ref[...]).astype(BF16)
        qpr_ref[...] = qpr
        qcat = [_q_cat(qn_ref[:, hh * QK_NOPE:(hh + 1) * QK_NOPE], qpr, hh) for hh in range(2)]

        def block(kb, carry, diagonal):
            rows = pl.ds(pl.multiple_of(kb * t, t), t)
            out = []
            for hh in range(2):
                m, l, acc = carry[hh]
                sc = _dot_nt(qcat[hh], kcat_ref[hh, rows, :]) * _ATT_SCALE
                if diagonal:
                    sc = _causal(sc)
                m_new = jnp.maximum(m, jnp.max(sc, axis=-1, keepdims=True))
                alpha = jnp.exp(m - m_new)
                p = jnp.exp(sc - m_new)
                l = alpha * l + jnp.sum(p, axis=-1, keepdims=True)
                v = kv_ref[rows, (2 * hh + 1) * QK_NOPE:(2 * hh + 2) * QK_NOPE]
                acc = alpha * acc + jnp.dot(p.astype(BF16), v, preferred_element_type=F32)
                out.append((m_new, l, acc))
            return tuple(out)

        one = (jnp.full((t, 1), _NEG, F32), jnp.zeros((t, 1), F32), jnp.zeros((t, V_HEAD), F32))
        carry = lax.fori_loop(0, qi, lambda kb, cr: block(kb, cr, False), (one, one))
        carry = block(qi, carry, True)
        for hh in range(2):
            m, l, acc = carry[hh]
            o_ref[:, hh * V_HEAD:(hh + 1) * V_HEAD] = (acc / l).astype(o_ref.dtype)
            l_ref[:, hh:hh + 1] = m + jnp.log(l)

    return pl.pallas_call(
        body, name="attn_fwd", grid=(hp, nq),
        in_specs=[pl.BlockSpec((t, 2 * QK_NOPE), lambda h, i: (i, h)), pl.BlockSpec((t, LANES), lambda h, i: (i, h)),
                  pl.BlockSpec((s, 4 * QK_NOPE), lambda h, i: (0, h)), _full((s, LANES)),
                  pl.BlockSpec((t, LANES), lambda h, i: (i, 0)), pl.BlockSpec((t, LANES), lambda h, i: (i, 0))],
        out_specs=[pl.BlockSpec((t, 2 * V_HEAD), lambda h, i: (i, h)), pl.BlockSpec((t, LANES), lambda h, i: (i, h)),
                   pl.BlockSpec((None, t, 2), lambda h, i: (h, i, 0))],
        out_shape=[jax.ShapeDtypeStruct((s, HEADS * V_HEAD), ACT), jax.ShapeDtypeStruct((s, HEADS * QK_ROPE), BF16),
                   jax.ShapeDtypeStruct((hp, s, 2), F32)],
        scratch_shapes=[pltpu.VMEM((2, s, 2 * QK_NOPE), BF16)],
        compiler_params=_params("parallel", "arbitrary"))(qn, qp, kv, kpr, cos4, sin4)


def attn_bwd(qn, qpr, kv, kpr, o, do, lse, cos4, sin4):
    s = qn.shape[0]
    hp = HEADS // 2
    t = _tile(s, ATT_TILE)
    nk = s // t

    def body(qn_ref, qpr_ref, kv_ref, kp_ref, o_ref, do_ref, l_ref, c_ref, s_ref,
             dqn_ref, dqp_ref, dkv_ref, dkp_ref, qcat_ref, dq_ref, delta_ref):
        ki = pl.program_id(1)

        @pl.when(ki == 0)
        def _():
            dq_ref[...] = jnp.zeros_like(dq_ref)
            for hh in range(2):
                qcat_ref[hh] = _q_cat(qn_ref[:, hh * QK_NOPE:(hh + 1) * QK_NOPE], qpr_ref[...], hh)
                cols = slice(hh * V_HEAD, (hh + 1) * V_HEAD)
                delta_ref[hh] = jnp.sum(do_ref[:, cols].astype(F32) * o_ref[:, cols].astype(F32), axis=-1, keepdims=True)

        rows_k = pl.ds(pl.multiple_of(ki * t, t), t)
        kcat = [jnp.concatenate([kv_ref[rows_k, 2 * hh * QK_NOPE:(2 * hh + 1) * QK_NOPE], kp_ref[rows_k, :]], axis=1) for hh in range(2)]
        vs = [kv_ref[rows_k, (2 * hh + 1) * QK_NOPE:(2 * hh + 2) * QK_NOPE] for hh in range(2)]

        def block(qb, carry, diagonal):
            rows = pl.ds(pl.multiple_of(qb * t, t), t)
            out = []
            for hh in range(2):
                dkc, dv = carry[hh]
                q_c = qcat_ref[hh, rows, :]
                do_b = do_ref[rows, hh * V_HEAD:(hh + 1) * V_HEAD].astype(BF16)
                sc = _dot_nt(q_c, kcat[hh]) * _ATT_SCALE
                if diagonal:
                    sc = _causal(sc)
                p = jnp.exp(sc - l_ref[rows, hh:hh + 1])
                dpv = _dot_nt(do_b, vs[hh])
                ds = (p * (dpv - delta_ref[hh, rows, :]) * _ATT_SCALE).astype(BF16)
                dv = dv + _dot_tn(p.astype(BF16), do_b)
                dkc = dkc + _dot_tn(ds, q_c)
                dq_ref[hh, rows, :] += jnp.dot(ds, kcat[hh], preferred_element_type=F32)
                out.append((dkc, dv))
            return tuple(out)

        one = (jnp.zeros((t, 2 * QK_NOPE), F32), jnp.zeros((t, V_HEAD), F32))
        carry = block(ki, (one, one), True)
        carry = lax.fori_loop(ki + 1, nk, lambda qb, cr: block(qb, cr, False), carry)
        dkp = jnp.zeros((t, LANES), F32)
        for hh in range(2):
            dkc, dv = carry[hh]
            dkv_ref[:, 2 * hh * QK_NOPE:(2 * hh + 1) * QK_NOPE] = dkc[:, :QK_NOPE].astype(dkv_ref.dtype)
            dkv_ref[:, (2 * hh + 1) * QK_NOPE:(2 * hh + 2) * QK_NOPE] = dv.astype(dkv_ref.dtype)
            dkp = dkp + dkc[:, QK_NOPE:]
        dkp_ref[...] = dkp

        @pl.when(ki == nk - 1)
        def _():
            lane = lax.broadcasted_iota(jnp.int32, (s, LANES), 1)
            dqp = jnp.where(lane < QK_ROPE, dq_ref[0, :, QK_NOPE:], dq_ref[1, :, QK_NOPE:])
            dqp_ref[...] = _rope(dqp, c_ref[...], -s_ref[...]).astype(dqp_ref.dtype)
            for hh in range(2):
                dqn_ref[:, hh * QK_NOPE:(hh + 1) * QK_NOPE] = dq_ref[hh, :, :QK_NOPE].astype(dqn_ref.dtype)

    qblk = pl.BlockSpec((s, 2 * QK_NOPE), lambda h, i: (0, h))
    pblk = pl.BlockSpec((s, LANES), lambda h, i: (0, h))
    tab = _full((s, LANES))
    return pl.pallas_call(
        body, name="attn_bwd", grid=(hp, nk),
        in_specs=[qblk, pblk, pl.BlockSpec((s, 4 * QK_NOPE), lambda h, i: (0, h)), tab, qblk, qblk,
                  pl.BlockSpec((None, s, 2), lambda h, i: (h, 0, 0)), tab, tab],
        out_specs=[qblk, pblk, pl.BlockSpec((t, 4 * QK_NOPE), lambda h, i: (i, h)), pl.BlockSpec((None, t, LANES), lambda h, i: (h, i, 0))],
        out_shape=[jax.ShapeDtypeStruct((s, HEADS * QK_NOPE), BF16), jax.ShapeDtypeStruct((s, HEADS * QK_ROPE), BF16),
                   jax.ShapeDtypeStruct((s, HEADS * 2 * QK_NOPE), BF16), jax.ShapeDtypeStruct((hp, s, LANES), F32)],
        scratch_shapes=[pltpu.VMEM((2, s, 2 * QK_NOPE), BF16), pltpu.VMEM((2, s, 2 * QK_NOPE), F32), pltpu.VMEM((2, s, 1), F32)],
        compiler_params=_params("parallel", "arbitrary"))(qn, qpr, kv, kpr, o, do, lse, cos4, sin4)


def kpe_bwd(dkp, cos4, sin4, pad_cols):
    hp, s, _ = dkp.shape
    tr = _tile(s, ROW_TILE * 2)

    def body(d_ref, c_ref, s_ref, o_ref):
        tot = d_ref[0]
        for h in range(1, hp):
            tot = tot + d_ref[h]
        tot = tot + pltpu.roll(tot, QK_ROPE, 1)
        lane = lax.broadcasted_iota(jnp.int32, tot.shape, 1)
        dk = jnp.where(lane < QK_ROPE, _rope(tot, c_ref[...], -s_ref[...]), jnp.zeros_like(tot))
        o_ref[...] = jnp.zeros_like(o_ref)
        o_ref[:, 0:LANES] = dk.astype(o_ref.dtype)

    row = pl.BlockSpec((tr, LANES), lambda i: (i, 0))
    return pl.pallas_call(body, name="kpe_bwd", grid=(s // tr,),
                          in_specs=[pl.BlockSpec((hp, tr, LANES), lambda i: (0, i, 0)), row, row],
                          out_specs=pl.BlockSpec((tr, pad_cols), lambda i: (i, 0)),
                          out_shape=jax.ShapeDtypeStruct((s, pad_cols), BF16), compiler_params=_params("parallel"))(dkp, cos4, sin4)


def _shift_down(x, n):
    row = lax.broadcasted_iota(jnp.int32, x.shape, 0)
    return jnp.where(row >= n, pltpu.roll(x, n, 0), jnp.zeros_like(x))


def _shift_up(x, n):
    rows = x.shape[0]
    row = lax.broadcasted_iota(jnp.int32, x.shape, 0)
    return jnp.where(row < rows - n, pltpu.roll(x, rows - n, 0), jnp.zeros_like(x))


def _conv(x, w_ref, b_ref):
    return w_ref[2:3, :] * x + w_ref[1:2, :] * _shift_down(x, 1) + w_ref[0:1, :] * _shift_down(x, 2) + b_ref[...]


def conv_act_fwd(upre, conv_w, conv_b):
    s, f2 = upre.shape
    f = f2 // 2
    tc = _tile(f, COL_TILE)
    nc = f // tc

    def body(ug_ref, uv_ref, wg_ref, wv_ref, bg_ref, bv_ref, o_ref, gv_ref):
        gh = _conv(_f32(ug_ref), wg_ref, bg_ref)
        vh = _conv(_f32(uv_ref), wv_ref, bv_ref)
        o_ref[...] = (gh * _sigmoid(gh) * vh).astype(o_ref.dtype)
        gv_ref[0] = gh.astype(gv_ref.dtype)
        gv_ref[1] = vh.astype(gv_ref.dtype)

    def spec(rows, shift):
        return pl.BlockSpec((rows, tc), lambda j: (0, j + shift))

    return pl.pallas_call(
        body, name="conv_act_fwd", grid=(nc,),
        in_specs=[spec(s, 0), spec(s, nc), spec(3, 0), spec(3, nc), spec(1, 0), spec(1, nc)],
        out_specs=[spec(s, 0), pl.BlockSpec((2, s, tc), lambda j: (0, 0, j))],
        out_shape=[jax.ShapeDtypeStruct((s, f), BF16), jax.ShapeDtypeStruct((2, s, f), ACT)],
        compiler_params=_params("parallel"))(upre, upre, conv_w, conv_w, conv_b, conv_b)


def conv_act_bwd(upre, conv_w, gv, df):
    s, f2 = upre.shape
    f = f2 // 2
    tc = _tile(f, COL_TILE)
    nc = f // tc

    def half(x, d, w_ref, du_ref, which, gw_ref, gb_ref):
        d1, d2 = _shift_up(d, 1), _shift_up(d, 2)
        gb_ref[...] = _colsum(d)
        gw_ref[2:3, :] = _colsum(d * x)
        gw_ref[1:2, :] = _colsum(d1 * x)
        gw_ref[0:1, :] = _colsum(d2 * x)
        du_ref[which] = (w_ref[2:3, :] * d + w_ref[1:2, :] * d1 + w_ref[0:1, :] * d2).astype(du_ref.dtype)

    def body(ug_ref, uv_ref, wg_ref, wv_ref, gv_ref, df_ref, du_ref, gwg_ref, gwv_ref, gbg_ref, gbv_ref):
        xg, xv = _f32(ug_ref), _f32(uv_ref)
        gh, vh = gv_ref[0].astype(F32), gv_ref[1].astype(F32)
        sg = _sigmoid(gh)
        df_v = _f32(df_ref)
        half(xg, df_v * vh * (sg * (1.0 + gh * (1.0 - sg))), wg_ref, du_ref, 0, gwg_ref, gbg_ref)
        half(xv, df_v * (gh * sg), wv_ref, du_ref, 1, gwv_ref, gbv_ref)

    def spec(rows, shift):
        return pl.BlockSpec((rows, tc), lambda j: (0, j + shift))

    gw = jax.ShapeDtypeStruct((3, f), F32)
    gb = jax.ShapeDtypeStruct((1, f), F32)
    return pl.pallas_call(
        body, name="conv_act_bwd", grid=(nc,),
        in_specs=[spec(s, 0), spec(s, nc), spec(3, 0), spec(3, nc), pl.BlockSpec((2, s, tc), lambda j: (0, 0, j)), spec(s, 0)],
        out_specs=[pl.BlockSpec((2, s, tc), lambda j: (0, 0, j)), spec(3, 0), spec(3, 0), spec(1, 0), spec(1, 0)],
        out_shape=[jax.ShapeDtypeStruct((2, s, f), BF16), gw, gw, gb, gb],
        compiler_params=_params("parallel"))(upre, upre, conv_w, conv_w, gv, df)


def _elementwise_tile(r, c, limit):
    if r % 8:
        return r, c
    best = (8, c if c % LANES else LANES)
    for k in (1, 2, 4, 8, 16):
        if k > 1 and c % (LANES * k):
            continue
        tc = c // k
        tr = max(8, min(r, limit // tc) // 8 * 8)
        while r % tr:
            tr -= 8
        if tr * tc <= max(limit, 8 * tc) and tr * tc > best[0] * best[1]:
            best = (tr, tc)
    return best


def adamw(name, w, m, v, parts):
    npart, r, c = parts.shape
    tr, tc = _elementwise_tile(r, c, ADAMW_TILE_ELEMS)
    bc1 = 1.0 - ADAM_B1 ** ADAM_STEP
    bc2 = 1.0 - ADAM_B2 ** ADAM_STEP

    def body(w_ref, m_ref, v_ref, p_ref, g_ref, d_ref, nm_ref, nv_ref):
        g = p_ref[0].astype(F32)
        for k in range(1, npart):
            g = g + p_ref[k].astype(F32)
        m_new = ADAM_B1 * m_ref[...] + (1.0 - ADAM_B1) * g
        v_new = ADAM_B2 * v_ref[...] + (1.0 - ADAM_B2) * (g * g)
        g_ref[...] = g
        nm_ref[...] = m_new
        nv_ref[...] = v_new
        d_ref[...] = -ADAM_LR * ((m_new / bc1) / (jnp.sqrt(v_new / bc2) + ADAM_EPS) + ADAM_WD * w_ref[...])

    deps = _TOKENS.take()
    blk = pl.BlockSpec((tr, tc), lambda i, j: (i, j))
    out = jax.ShapeDtypeStruct((r, c), F32)
    return pl.pallas_call(
        lambda *refs: body(*refs[:4], *refs[4 + len(deps):]), name=name, grid=(r // tr, c // tc),
        in_specs=[blk, blk, blk, pl.BlockSpec((npart, tr, tc), lambda i, j: (0, i, j))] + [pl.BlockSpec(memory_space=pl.ANY)] * len(deps),
        out_specs=[blk, blk, blk, blk], out_shape=[out, out, out, out],
        compiler_params=_params("parallel", "parallel"))(w, m, v, parts, *deps)


def _position():
    return lax.axis_index("x"), lax.axis_index("y"), lax.axis_index("c")


def _index(p):
    return 4 * p[0] + 2 * p[1] + p[2]


def _peer(me, r):
    return (me[0] ^ ((r >> 2) & 1), me[1] ^ ((r >> 1) & 1), me[2] ^ (r & 1))


_ANY = pl.BlockSpec(memory_space=pl.ANY)


_HBM = pl.BlockSpec(memory_space=pltpu.HBM)
_SEM = pl.BlockSpec(memory_space=pltpu.SEMAPHORE)
_EFFECT = pltpu.SideEffectType.DATAFLOW_SIDE_EFFECTING
_TOKEN = jax.ShapeDtypeStruct((8, LANES), F32)
_VM = pl.BlockSpec(memory_space=pltpu.VMEM)
_SIDE = pltpu.CompilerParams(has_side_effects=_EFFECT)


def _hbm(a):
    return pltpu.with_memory_space_constraint(a, pltpu.HBM)


def _like(a):
    return pltpu.HBM(a.shape, a.dtype)


def _dma_sems(n):
    return pltpu.SemaphoreType.DMA((n,))


def _other_chips(x, y):
    return [(1 - x, y), (x, 1 - y), (1 - x, 1 - y)]


COPY_STREAMS = 8


def _row_chunks(src, dst):
    rows = src.shape[0]
    n = COPY_STREAMS
    while n > 1 and rows % (16 * n):
        n //= 2
    r = rows // n
    return [(src.at[pl.ds(i * r, r)], dst.at[pl.ds(i * r, r)]) for i in range(n)]


class _rcopy:
    def __init__(self, src, dst, send_sem, recv_sem, to):
        self.parts = [pltpu.make_async_remote_copy(src_ref=s, dst_ref=d, send_sem=send_sem, recv_sem=recv_sem, device_id=to, device_id_type=MESH)
                      for s, d in _row_chunks(src, dst)]

    def start(self):
        for cp in self.parts:
            cp.start()

    def wait_send(self):
        for cp in self.parts:
            cp.wait_send()

    def wait_recv(self):
        for cp in self.parts:
            cp.wait_recv()


def _afters(after):
    return list(after) if isinstance(after, (list, tuple)) else [after]


def ag_start(name, shards, after):
    n = len(shards)
    lands = [lax.empty((N_DEV,) + a.shape, a.dtype) for a in shards]
    afters = _afters(after)
    na = len(afters)

    def body(*refs):
        ins, lnd, send_sems, recv_sems, token = refs[:n], refs[n:2 * n], refs[2 * n + na], refs[2 * n + na + 1], refs[4 * n + na + 2]
        x, y, c = _position()
        for w in range(n):
            slot = lnd[w].at[_index((x, y, c))]
            for k, to in enumerate([(x, y, 1 - c)] + [(*chip, c) for chip in _other_chips(x, y)]):
                _rcopy(ins[w], slot, send_sems.at[4 * w + k], recv_sems.at[4 * w + k], to).start()
        token[...] = jnp.zeros_like(token)

    out = pl.pallas_call(
        body, name=name, out_shape=(_dma_sems(4 * n), _dma_sems(4 * n)) + tuple(_like(a) for a in shards + lands) + (_TOKEN,),
        in_specs=[_HBM] * (2 * n) + [_ANY] * na, out_specs=(_SEM, _SEM) + (_HBM,) * (2 * n) + (_VM,),
        input_output_aliases={i: 2 + i for i in range(2 * n)}, compiler_params=_SIDE)(*[_hbm(a) for a in shards + lands], *afters)
    _TOKENS.push(out[-1])
    return out[0], out[1], list(out[2:2 + n]), list(out[2 + n:2 + 2 * n])


def _split_rows(ref):
    rows = ref.shape[0]
    h = rows // 32 * 16
    return ref.at[pl.ds(0, h)], ref.at[pl.ds(h, rows - h)]


def relay_start(name, shards, after):
    n = len(shards)
    lands = [lax.empty((N_DEV,) + a.shape, a.dtype) for a in shards]
    afters = _afters(after)
    na = len(afters)

    def body(*refs):
        ins, lnd, send_sems, recv_sems, token = refs[:n], refs[n:2 * n], refs[2 * n + na], refs[2 * n + na + 1], refs[4 * n + na + 2]
        x, y, c = _position()
        for w in range(n):
            slot = lnd[w].at[_index((x, y, c))]
            for k, to in enumerate([(x, y, 1 - c), (1 - x, y, c), (x, 1 - y, c)]):
                _rcopy(ins[w], slot, send_sems.at[3 * w + k], recv_sems.at[3 * w + k], to).start()
        token[...] = jnp.zeros_like(token)

    out = pl.pallas_call(
        body, name=name, out_shape=(_dma_sems(3 * n), _dma_sems(3 * n)) + tuple(_like(a) for a in shards + lands) + (_TOKEN,),
        in_specs=[_HBM] * (2 * n) + [_ANY] * na, out_specs=(_SEM, _SEM) + (_HBM,) * (2 * n) + (_VM,),
        input_output_aliases={i: 2 + i for i in range(2 * n)}, compiler_params=_SIDE)(*[_hbm(a) for a in shards + lands], *afters)
    _TOKENS.push(out[-1])
    return out[0], out[1], list(out[2:2 + n]), list(out[2 + n:2 + 2 * n])


def relay_pass(name, started, after):
    send, recv, shards, lands = started
    n = len(shards)
    afters = _afters(after)
    na = len(afters)

    def body(*refs):
        ins, lnd, send_sems, recv_sems = refs[:n], refs[n:2 * n], refs[2 * n], refs[2 * n + 1]
        fsend, frecv, psend, precv = refs[2 * n + 2 + na:2 * n + 6 + na]
        token = refs[4 * n + 6 + na]
        x, y, c = _position()
        nbrs = [(1 - x, y, c), (x, 1 - y, c)]
        for w in range(n):
            for j, nbr in enumerate(nbrs):
                slot = lnd[w].at[_index(nbr)]
                _rcopy(ins[w], slot, send_sems.at[3 * w + 1 + j], recv_sems.at[3 * w + 1 + j], nbr).wait_recv()
                _rcopy(slot, slot, fsend.at[2 * w + j], frecv.at[2 * w + j], (x, y, 1 - c)).start()
                part = _split_rows(slot)[j]
                _rcopy(part, part, psend.at[2 * w + j], precv.at[2 * w + j], nbrs[1 - j]).start()
        token[...] = jnp.zeros_like(token)

    out = pl.pallas_call(
        body, name=name, out_shape=(_dma_sems(2 * n),) * 4 + tuple(_like(a) for a in shards + lands) + (_TOKEN,),
        in_specs=[_HBM] * (2 * n) + [_SEM, _SEM] + [_ANY] * na, out_specs=(_SEM,) * 4 + (_HBM,) * (2 * n) + (_VM,),
        input_output_aliases={i: 4 + i for i in range(2 * n)}, compiler_params=_SIDE)(*shards, *lands, send, recv, *afters)
    _TOKENS.push(out[-1])
    return (send, recv) + tuple(out[:4]) + (list(out[4:4 + n]), list(out[4 + n:4 + 2 * n]))


def relay_forward(name, passed, after):
    send, recv, fsend, frecv, psend, precv, shards, lands = passed
    n = len(shards)
    afters = _afters(after)
    na = len(afters)

    def body(*refs):
        ins, lnd, precv_r = refs[:n], refs[n:2 * n], refs[2 * n]
        gsend, grecv, token = refs[2 * n + 1 + na], refs[2 * n + 2 + na], refs[4 * n + 3 + na]
        x, y, c = _position()
        for w in range(n):
            slot = lnd[w].at[_index((1 - x, 1 - y, c))]
            for j, part in enumerate(_split_rows(slot)):
                _rcopy(part, part, precv_r.at[2 * w + j], precv_r.at[2 * w + j], (x, y, 1 - c)).wait_recv()
            _rcopy(slot, slot, gsend.at[w], grecv.at[w], (x, y, 1 - c)).start()
        token[...] = jnp.zeros_like(token)

    out = pl.pallas_call(
        body, name=name, out_shape=(_dma_sems(n), _dma_sems(n)) + tuple(_like(a) for a in shards + lands) + (_TOKEN,),
        in_specs=[_HBM] * (2 * n) + [_SEM] + [_ANY] * na, out_specs=(_SEM, _SEM) + (_HBM,) * (2 * n) + (_VM,),
        input_output_aliases={i: 2 + i for i in range(2 * n)}, compiler_params=_SIDE)(*shards, *lands, precv, *afters)
    _TOKENS.push(out[-1])
    return send, recv, fsend, frecv, psend, out[0], out[1], list(out[2:2 + n]), list(out[2 + n:2 + 2 * n])


def relay_wait(name, forwarded, after):
    send, recv, fsend, frecv, psend, gsend, grecv, shards, lands = forwarded
    n = len(shards)

    def body(*refs):
        ins, lnd = refs[:n], refs[n:2 * n]
        send_sems, recv_sems, fsend_r, frecv_r, psend_r, gsend_r, grecv_r = refs[2 * n:2 * n + 7]
        x, y, c = _position()
        sibling = (x, y, 1 - c)
        for w in range(n):
            own = lnd[w].at[_index((x, y, c))]
            _rcopy(ins[w], lnd[w].at[_index(sibling)], send_sems.at[3 * w], recv_sems.at[3 * w], sibling).wait_recv()
            for j, nbr in enumerate([(1 - x, y, 1 - c), (x, 1 - y, 1 - c)]):
                _rcopy(ins[w], lnd[w].at[_index(nbr)], fsend_r.at[2 * w + j], frecv_r.at[2 * w + j], sibling).wait_recv()
            _rcopy(ins[w], lnd[w].at[_index((1 - x, 1 - y, 1 - c))], gsend_r.at[w], grecv_r.at[w], sibling).wait_recv()
            for k in range(3):
                _rcopy(ins[w], own, send_sems.at[3 * w + k], recv_sems.at[3 * w + k], sibling).wait_send()
            for j in range(2):
                _rcopy(ins[w], own, fsend_r.at[2 * w + j], frecv_r.at[2 * w + j], sibling).wait_send()
                part = _split_rows(own)[j]
                _rcopy(part, part, psend_r.at[2 * w + j], psend_r.at[2 * w + j], sibling).wait_send()
            _rcopy(ins[w], own, gsend_r.at[w], grecv_r.at[w], sibling).wait_send()

    out = pl.pallas_call(
        body, name=name, out_shape=tuple(_like(a) for a in shards + lands),
        in_specs=[_HBM] * (2 * n) + [_SEM] * 7 + [_ANY] * len(_afters(after)),
        out_specs=(_HBM,) * (2 * n), input_output_aliases={i: i for i in range(2 * n)},
        compiler_params=_SIDE)(*shards, *lands, send, recv, fsend, frecv, psend, gsend, grecv, *_afters(after))
    return [lax.dynamic_update_index_in_dim(land, shard, _index(_position()), 0) for shard, land in zip(out[:n], out[n:])]


def ag_forward(name, started, after):
    send, recv, shards, lands = started
    n = len(shards)
    afters = list(after) if isinstance(after, (list, tuple)) else [after]
    na = len(afters)

    def body(*refs):
        ins, lnd, send_sems, recv_sems = refs[:n], refs[n:2 * n], refs[2 * n], refs[2 * n + 1]
        fsend, frecv, token = refs[2 * n + 2 + na], refs[2 * n + 3 + na], refs[4 * n + 4 + na]
        x, y, c = _position()
        for w in range(n):
            for j, chip in enumerate(_other_chips(x, y)):
                slot = lnd[w].at[_index((*chip, c))]
                _rcopy(ins[w], slot, send_sems.at[4 * w + 1 + j], recv_sems.at[4 * w + 1 + j], (*chip, c)).wait_recv()
                _rcopy(slot, slot, fsend.at[3 * w + j], frecv.at[3 * w + j], (x, y, 1 - c)).start()
        token[...] = jnp.zeros_like(token)

    out = pl.pallas_call(
        body, name=name, out_shape=(_dma_sems(3 * n), _dma_sems(3 * n)) + tuple(_like(a) for a in shards + lands) + (_TOKEN,),
        in_specs=[_HBM] * (2 * n) + [_SEM, _SEM] + [_ANY] * na, out_specs=(_SEM, _SEM) + (_HBM,) * (2 * n) + (_VM,),
        input_output_aliases={i: 2 + i for i in range(2 * n)}, compiler_params=_SIDE)(*shards, *lands, send, recv, *afters)
    _TOKENS.push(out[-1])
    return send, recv, out[0], out[1], list(out[2:2 + n]), list(out[2 + n:2 + 2 * n])


def ag_wait(name, forwarded, after):
    send, recv, fsend, frecv, shards, lands = forwarded
    n = len(shards)

    def body(*refs):
        ins, lnd, send_sems, recv_sems, fsend_r, frecv_r = refs[:n], refs[n:2 * n], refs[2 * n], refs[2 * n + 1], refs[2 * n + 2], refs[2 * n + 3]
        x, y, c = _position()
        sibling = (x, y, 1 - c)
        for w in range(n):
            own = lnd[w].at[_index((x, y, c))]
            _rcopy(ins[w], lnd[w].at[_index(sibling)], send_sems.at[4 * w], recv_sems.at[4 * w], sibling).wait_recv()
            for j, chip in enumerate(_other_chips(x, y)):
                _rcopy(ins[w], lnd[w].at[_index((*chip, 1 - c))], fsend_r.at[3 * w + j], frecv_r.at[3 * w + j], sibling).wait_recv()
            for k in range(4):
                _rcopy(ins[w], own, send_sems.at[4 * w + k], recv_sems.at[4 * w + k], sibling).wait_send()
            for j in range(3):
                _rcopy(ins[w], own, fsend_r.at[3 * w + j], frecv_r.at[3 * w + j], sibling).wait_send()

    out = pl.pallas_call(
        body, name=name, out_shape=tuple(_like(a) for a in shards + lands),
        in_specs=[_HBM] * (2 * n) + [_SEM] * 4 + [_ANY] * len(_afters(after)),
        out_specs=(_HBM,) * (2 * n), input_output_aliases={i: i for i in range(2 * n)},
        compiler_params=_SIDE)(*shards, *lands, send, recv, fsend, frecv, *_afters(after))
    return [lax.dynamic_update_index_in_dim(land, shard, _index(_position()), 0) for shard, land in zip(out[:n], out[n:])]


def rs_d2d_start(name, grads):
    n = len(grads)
    lands = [lax.empty((4,) + g.shape[1:], g.dtype) for g in grads]

    def body(*refs):
        ins, lnd, send_sems, recv_sems, token = refs[:n], refs[n:2 * n], refs[2 * n], refs[2 * n + 1], refs[4 * n + 2]
        x, y, c = _position()
        for w in range(n):
            for i in range(4):
                _rcopy(ins[w].at[2 * i + 1 - c], lnd[w].at[i], send_sems.at[4 * w + i], recv_sems.at[4 * w + i], (x, y, 1 - c)).start()
        token[...] = jnp.zeros_like(token)

    out = pl.pallas_call(
        body, name=name, out_shape=(_dma_sems(4 * n), _dma_sems(4 * n)) + tuple(_like(a) for a in grads + lands) + (_TOKEN,),
        in_specs=[_HBM] * (2 * n), out_specs=(_SEM, _SEM) + (_HBM,) * (2 * n) + (_VM,),
        input_output_aliases={i: 2 + i for i in range(2 * n)}, compiler_params=_SIDE)(*[_hbm(a) for a in grads + lands])
    _TOKENS.push(out[-1])
    return out[0], out[1], list(out[2:2 + n]), list(out[2 + n:2 + 2 * n])


def rs_d2d_wait(name, started, after):
    send, recv, grads, lands = started
    n = len(grads)

    def body(*refs):
        ins, lnd, send_sems, recv_sems = refs[:n], refs[n:2 * n], refs[2 * n], refs[2 * n + 1]
        x, y, c = _position()
        for w in range(n):
            for i in range(4):
                cp = _rcopy(ins[w].at[2 * i + 1 - c], lnd[w].at[i], send_sems.at[4 * w + i], recv_sems.at[4 * w + i], (x, y, 1 - c))
                cp.wait_send()
                cp.wait_recv()

    out = pl.pallas_call(
        body, name=name, out_shape=tuple(_like(a) for a in grads + lands),
        in_specs=[_HBM] * (2 * n) + [_SEM, _SEM] + [_ANY] * len(_afters(after)),
        out_specs=(_HBM,) * (2 * n), input_output_aliases={i: i for i in range(2 * n)},
        compiler_params=_SIDE)(*grads, *lands, send, recv, *_afters(after))
    return list(out[:n]), list(out[n:])


def pair_sum(name, grad, land, core):
    _, r, c = grad.shape
    tr = r
    if r % 8 == 0:
        tr = max(8, min(r, 4 * ADAMW_TILE_ELEMS // c) // 8 * 8)
        while r % tr:
            tr -= 8

    def body(core_ref, a_ref, b_ref, o_ref):
        o_ref[...] = (a_ref[...].astype(F32) + b_ref[...].astype(F32)).astype(o_ref.dtype)

    return pl.pallas_call(
        body, name=name, out_shape=jax.ShapeDtypeStruct((4, r, c), grad.dtype),
        grid_spec=pltpu.PrefetchScalarGridSpec(
            num_scalar_prefetch=1, grid=(4, r // tr),
            in_specs=[pl.BlockSpec((None, None, tr, c), lambda i, j, core_ref: (i, core_ref[0], j, 0)),
                      pl.BlockSpec((None, tr, c), lambda i, j, core_ref: (i, j, 0))],
            out_specs=pl.BlockSpec((None, tr, c), lambda i, j, core_ref: (i, j, 0))),
        compiler_params=_params("parallel", "parallel"))(core, grad.reshape(4, 2, r, c), land)


def rs_ici_start(name, sums):
    n = len(sums)
    lands = [lax.empty(a.shape, a.dtype) for a in sums]

    def body(*refs):
        ins, lnd, send_sems, recv_sems, token = refs[:n], refs[n:2 * n], refs[2 * n], refs[2 * n + 1], refs[4 * n + 2]
        x, y, c = _position()
        chip = 2 * x + y
        for w in range(n):
            for j, other in enumerate(_other_chips(x, y)):
                _rcopy(ins[w].at[2 * other[0] + other[1]], lnd[w].at[chip], send_sems.at[3 * w + j], recv_sems.at[3 * w + j], (*other, c)).start()
        token[...] = jnp.zeros_like(token)

    out = pl.pallas_call(
        body, name=name, out_shape=(_dma_sems(3 * n), _dma_sems(3 * n)) + tuple(_like(a) for a in sums + lands) + (_TOKEN,),
        in_specs=[_HBM] * (2 * n), out_specs=(_SEM, _SEM) + (_HBM,) * (2 * n) + (_VM,),
        input_output_aliases={i: 2 + i for i in range(2 * n)}, compiler_params=_SIDE)(*[_hbm(a) for a in sums + lands])
    _TOKENS.push(out[-1])
    return out[0], out[1], list(out[2:2 + n]), list(out[2 + n:2 + 2 * n])


def rs_ici_wait(name, started, after):
    send, recv, sums, lands = started
    n = len(sums)

    def body(*refs):
        ins, lnd, send_sems, recv_sems = refs[:n], refs[n:2 * n], refs[2 * n], refs[2 * n + 1]
        x, y, c = _position()
        for w in range(n):
            for j, other in enumerate(_other_chips(x, y)):
                cp = _rcopy(ins[w].at[2 * other[0] + other[1]], lnd[w].at[2 * other[0] + other[1]], send_sems.at[3 * w + j], recv_sems.at[3 * w + j], (*other, c))
                cp.wait_send()
                cp.wait_recv()

    out = pl.pallas_call(
        body, name=name, out_shape=tuple(_like(a) for a in sums + lands), in_specs=[_HBM] * (2 * n) + [_SEM, _SEM, _ANY],
        out_specs=(_HBM,) * (2 * n), input_output_aliases={i: i for i in range(2 * n)}, compiler_params=_SIDE)(*sums, *lands, send, recv, after)
    chip = 2 * lax.axis_index("x") + lax.axis_index("y")
    return [lax.dynamic_update_index_in_dim(land, lax.dynamic_index_in_dim(s, chip, 0, keepdims=False), chip, 0)
            for s, land in zip(out[:n], out[n:])]


def ada_fwd(c, w_ada, b_ada3, conv_w, after):
    d, cs = w_ada.shape

    def body(c_ref, w_ref, b_ref, cw_ref, after_ref, mod_ref, sc_ref, cwa_ref, part_ref, send_sems, recv_sems):
        me = _position()
        my = _index(me)
        cv = c_ref[...]
        sc_ref[my] = cv * _sigmoid(cv)
        cwa_ref[my] = cw_ref[...]
        gather = []
        for r in range(1, N_DEV):
            for k, ref in enumerate((sc_ref, cwa_ref)):
                cp = pltpu.make_async_remote_copy(src_ref=ref.at[my], dst_ref=ref.at[my], send_sem=send_sems.at[14 * k + r - 1],
                                                  recv_sem=recv_sems.at[14 * k + r - 1], device_id=_peer(me, r), device_id_type=MESH)
                cp.start()
                gather.append(cp)
        for cp in gather:
            cp.wait()
        sc_all = jnp.concatenate([sc_ref[k] for k in range(N_DEV)], axis=0).astype(BF16)
        part = jnp.dot(sc_all, w_ref[...].astype(BF16), preferred_element_type=F32)
        for k in range(N_DEV):
            part_ref[k] = part[k:k + 1, :]
        scatter = []
        for r in range(1, N_DEV):
            peer = _peer(me, r)
            cp = pltpu.make_async_remote_copy(src_ref=part_ref.at[_index(peer)], dst_ref=mod_ref.at[my], send_sem=send_sems.at[6 + r],
                                              recv_sem=recv_sems.at[6 + r], device_id=peer, device_id_type=MESH)
            cp.start()
            scatter.append(cp)
        mod_ref[my] = part_ref[my]
        for cp in scatter:
            cp.wait()
        mod_ref[...] = mod_ref[...] + b_ref[...]

    vm = pl.BlockSpec(memory_space=pltpu.VMEM)
    return pl.pallas_call(
        body, name="ada_fwd",
        out_shape=[jax.ShapeDtypeStruct((N_DEV, 1, cs), F32), jax.ShapeDtypeStruct((N_DEV, 1, d), F32),
                   jax.ShapeDtypeStruct((N_DEV,) + conv_w.shape, F32)],
        in_specs=[vm, vm, vm, vm, _ANY], out_specs=[vm, vm, vm],
        scratch_shapes=[pltpu.VMEM((N_DEV, 1, cs), F32), pltpu.SemaphoreType.DMA((21,)), pltpu.SemaphoreType.DMA((21,))],
        compiler_params=pltpu.CompilerParams(vmem_limit_bytes=VMEM_LIMIT_BYTES))(c, w_ada, b_ada3, conv_w, after)


def ada_bwd_w(sc_all, dmod_cols):
    _, d = sc_all.shape
    cs = dmod_cols.shape[1]
    tr = _tile(d, ROW_TILE)

    def body(sc_ref, dm_ref, o_ref):
        dm = dm_ref[...].astype(BF16)
        o_ref[...] = lax.dot_general(sc_ref[...].astype(BF16), dm, (((0,), (0,)), ((), ())), preferred_element_type=F32)

    return pl.pallas_call(body, name="ada_bwd_w", grid=(d // tr,),
                          in_specs=[pl.BlockSpec((N_DEV, tr), lambda i: (0, i)), _full((N_DEV, cs))],
                          out_specs=pl.BlockSpec((None, tr, cs), lambda i: (0, i, 0)),
                          out_shape=jax.ShapeDtypeStruct((1, d, cs), F32), compiler_params=_params("parallel"))(sc_all, dmod_cols)


def _round_up(n, m):
    return (n + m - 1) // m * m


def kernel(x, c, positions, w_ada, b_ada, pre_norm1_g, w_in, gm_ln_g, gm_ln_b, gm_w_s, gm_b_s, w_branch_a, q_norm_g, w_uq, kv_norm_g, w_ukv, w_branch_b, w_out, post_norm1_g, pre_norm2_g, w_up, conv_w, conv_b, w_down, post_norm2_g, loss_target, m_w_ada, m_b_ada, m_pre_norm1_g, m_w_in, m_gm_ln_g, m_gm_ln_b, m_gm_w_s, m_gm_b_s, m_w_branch_a, m_q_norm_g, m_w_uq, m_kv_norm_g, m_w_ukv, m_w_branch_b, m_w_out, m_post_norm1_g, m_pre_norm2_g, m_w_up, m_conv_w, m_conv_b, m_w_down, m_post_norm2_g, v_w_ada, v_b_ada, v_pre_norm1_g, v_w_in, v_gm_ln_g, v_gm_ln_b, v_gm_w_s, v_gm_b_s, v_w_branch_a, v_q_norm_g, v_w_uq, v_kv_norm_g, v_w_ukv, v_w_branch_b, v_w_out, v_post_norm1_g, v_pre_norm2_g, v_w_up, v_conv_w, v_conv_b, v_w_down, v_post_norm2_g):
    weights = dict(w_ada=w_ada, b_ada=b_ada, pre_norm1_g=pre_norm1_g, w_in=w_in, gm_ln_g=gm_ln_g, gm_ln_b=gm_ln_b, gm_w_s=gm_w_s,
                   gm_b_s=gm_b_s, w_branch_a=w_branch_a, q_norm_g=q_norm_g, w_uq=w_uq, kv_norm_g=kv_norm_g, w_ukv=w_ukv,
                   w_branch_b=w_branch_b, w_out=w_out, post_norm1_g=post_norm1_g, pre_norm2_g=pre_norm2_g, w_up=w_up, conv_w=conv_w,
                   conv_b=conv_b, w_down=w_down, post_norm2_g=post_norm2_g)
    mom1 = dict(w_ada=m_w_ada, b_ada=m_b_ada, pre_norm1_g=m_pre_norm1_g, w_in=m_w_in, gm_ln_g=m_gm_ln_g, gm_ln_b=m_gm_ln_b,
                gm_w_s=m_gm_w_s, gm_b_s=m_gm_b_s, w_branch_a=m_w_branch_a, q_norm_g=m_q_norm_g, w_uq=m_w_uq, kv_norm_g=m_kv_norm_g,
                w_ukv=m_w_ukv, w_branch_b=m_w_branch_b, w_out=m_w_out, post_norm1_g=m_post_norm1_g, pre_norm2_g=m_pre_norm2_g,
                w_up=m_w_up, conv_w=m_conv_w, conv_b=m_conv_b, w_down=m_w_down, post_norm2_g=m_post_norm2_g)
    mom2 = dict(w_ada=v_w_ada, b_ada=v_b_ada, pre_norm1_g=v_pre_norm1_g, w_in=v_w_in, gm_ln_g=v_gm_ln_g, gm_ln_b=v_gm_ln_b,
                gm_w_s=v_gm_w_s, gm_b_s=v_gm_b_s, w_branch_a=v_w_branch_a, q_norm_g=v_q_norm_g, w_uq=v_w_uq, kv_norm_g=v_kv_norm_g,
                w_ukv=v_w_ukv, w_branch_b=v_w_branch_b, w_out=v_w_out, post_norm1_g=v_post_norm1_g, pre_norm2_g=v_pre_norm2_g,
                w_up=v_w_up, conv_w=v_conv_w, conv_b=v_conv_b, w_down=v_w_down, post_norm2_g=v_post_norm2_g)
    order = list(weights)
    _TOKENS.clear()

    s, d = x.shape[1], x.shape[2]
    gmw = gm_ln_g.shape[0]
    groups = gmw // CHUNK
    ql, kvl = q_norm_g.shape[0], kv_norm_g.shape[0]
    f2 = conv_b.shape[0]
    in_cols = w_in.shape[1] * N_DEV
    o_q, o_kv, o_ga, o_gb, o_kpe = 2 * gmw, 2 * gmw + ql, 2 * gmw + ql + kvl, 2 * gmw + ql + kvl + d, 2 * gmw + ql + kvl + 2 * d
    zp = _round_up(o_kpe + LANES, Z_PAD)
    src_kpe = 2 * gmw + ql + kvl
    assert src_kpe + QK_ROPE + 2 * d == in_cols
    my = 4 * lax.axis_index("x") + 2 * lax.axis_index("y") + lax.axis_index("c")

    x2, tgt = x[0], loss_target[0]
    row = lambda a: a.reshape(1, -1)

    big = ["w_in", "w_branch_a", "w_uq", "w_ukv", "w_branch_b", "w_out", "w_up", "w_down"]
    sh = {k: weights[k].astype(BF16) for k in big[1:]}
    mix = ["w_branch_a", "w_uq", "w_ukv", "w_branch_b", "w_out"]
    w_in_t = w_in.T.astype(BF16)

    mod8, sc_all3, g_cw = ada_fwd(c, w_ada, b_ada.reshape(N_DEV, 1, -1), conv_w, w_in_t)
    ag_in = relay_start("relay_start_in", [w_in_t], mod8)
    mod = mod8.reshape(N_MOD, d)
    shift1, scale1, gate1, shift2, scale2, gate2 = (mod[i:i + 1] for i in range(N_MOD))
    sc_all = sc_all3.reshape(N_DEV, d)
    h1 = norm_mod_fwd("pre1_fwd", x2, row(pre_norm1_g), scale1, shift1)

    inv = ROPE_THETA ** (-jnp.arange(0, QK_ROPE, 2, dtype=F32) / QK_ROPE)
    ang = positions[0].astype(F32)[:, None] * inv
    cos4 = jnp.tile(jnp.cos(ang), (1, 4))
    sin4 = jnp.tile(jnp.concatenate([-jnp.sin(ang), jnp.sin(ang)], axis=1), (1, 2))

    wm = (gm_w_s * jnp.tril(jnp.ones((CHUNK, CHUNK), F32))).astype(BF16)
    bs3 = gm_b_s.reshape(groups, CHUNK, 1)
    ln_g, ln_b = row(gm_ln_g), row(gm_ln_b)

    small_names = ["pre_norm1_g", "gm_ln_g", "gm_ln_b", "gm_b_s", "q_norm_g", "kv_norm_g", "post_norm1_g", "pre_norm2_g", "conv_b",
                   "post_norm2_g", "gm_w_s", "b_ada"]
    n_small_early = sum(weights[k].size for k in small_names)
    n_pack_early = _round_up(n_small_early + 3 * f2, PACK_ALIGN)

    def pack(src):
        return jnp.concatenate([src[k].reshape(-1) for k in small_names] + [jnp.zeros((n_pack_early - n_small_early,), F32)]).reshape(-1, LANES)

    packed_state = [pack(weights), pack(mom1), pack(mom2)]

    early = [h1, cos4, sin4, wm] + [sh[k] for k in big[1:]] + packed_state
    ag_in = relay_pass("relay_pass_in", ag_in, early)
    ag_in = relay_forward("relay_forward_in", ag_in, _TOKENS.pending[-1])
    ag_mix = ag_start("ag_start_mix", [sh[k] for k in mix], _TOKENS.pending[-1])
    (g_in,) = relay_wait("relay_wait_in", ag_in, [h1, _TOKENS.pending[-1]])
    cs_in = w_in.shape[1]

    def w_in_rows(lo, hi):
        return [g_in[k, max(lo - k * cs_in, 0):min(hi - k * cs_in, cs_in)] for k in range(N_DEV) if lo < (k + 1) * cs_in and hi > k * cs_in]

    w_in_p = jnp.concatenate(w_in_rows(0, src_kpe) + w_in_rows(src_kpe + QK_ROPE, in_cols) + w_in_rows(src_kpe, src_kpe + QK_ROPE)
                             + [jnp.zeros((zp - in_cols, d), BF16)], axis=0)

    z = mm_nt("z_proj", h1, w_in_p, ACT)
    ag_mix = ag_forward("ag_forward_mix", ag_mix, z)
    ag_up = ag_start("ag_start_up", [sh["w_up"]], _TOKENS.pending[-1])
    a = gmlp_fwd(z, gmw, ln_g, ln_b, wm, bs3)
    g_a, g_uq, g_ukv, g_b, g_out = ag_wait("ag_wait_mix", ag_mix, [a, _TOKENS.pending[-1]])
    w_a_f, w_b_f, w_out_f = g_a.reshape(-1, d), g_b.reshape(-1, d), g_out.reshape(-1, d)
    w_uq_f = g_uq.transpose(1, 0, 2).reshape(ql, HEADS, QK_NOPE + QK_ROPE)
    w_uq_n = w_uq_f[:, :, :QK_NOPE].reshape(ql, HEADS * QK_NOPE)
    w_uq_r = w_uq_f[:, :, QK_NOPE:].reshape(ql, HEADS * QK_ROPE)
    y_a = mm_nn("branch_a", a, w_a_f, ACT)
    qln = rms_fwd_cols("q_norm", z, o_q, ql, row(q_norm_g))
    kvn = rms_fwd_cols("kv_norm", z, o_kv, kvl, row(kv_norm_g))
    qn = mm_nn("q_nope", qln, w_uq_n, BF16)
    qp = mm_nn("q_rope", qln, w_uq_r, F32)
    kv = mm_nn_b3("kv_up", kvn, g_ukv, BF16)
    kpr = rope_k(z, o_kpe, cos4, sin4)
    o, qpr, lse = attn_fwd(qn, qp, kv, kpr, cos4, sin4)
    ag_up = ag_forward("ag_forward_up", ag_up, o)
    ag_down = ag_start("ag_start_down", [sh["w_down"]], _TOKENS.pending[-1])
    y_b = mm_nn("branch_b", o, w_b_f, ACT)
    merged = merge_fwd(z, o_ga, o_gb, y_a, y_b)
    y1 = mm_nn("out_proj", merged, w_out_f, ACT)
    x1 = post_res_fwd("post1_fwd", x2, y1, gate1, row(post_norm1_g))
    h2 = norm_mod_fwd("pre2_fwd", x1, row(pre_norm2_g), scale2, shift2)
    (g_up,) = ag_wait("ag_wait_up", ag_up, h2)
    upre = mm_nn_b3("up_proj", h2, g_up, ACT)
    ag_down = ag_forward("ag_forward_down", ag_down, upre)
    cw = g_cw.transpose(1, 0, 2).reshape(3, f2)
    cb = row(conv_b)
    f, gv = conv_act_fwd(upre, cw, cb)
    w_down_f = ag_wait("ag_wait_down", ag_down, f)[0].reshape(-1, d)
    ffn = mm_nn("down_proj", f, w_down_f, ACT)
    loss_acc, dout, dffn, acc2 = post2_loss_bwd(x1, ffn, tgt, gate2, row(post_norm2_g))
    loss = lax.psum(loss_acc[0, 0], ("x", "y", "c"))
    _TOKENS.push(jnp.broadcast_to(loss, (8, LANES)))

    blocks = lambda g: g.reshape(N_DEV, g.shape[0] // N_DEV, g.shape[1])
    core = lax.axis_index("c").astype(jnp.int32).reshape(1)
    rs = {}

    def rs_begin(key, grads):
        rs[key] = rs_d2d_start("rs_d2d_start_" + key, grads)

    def rs_middle(key, after):
        grads, lands = rs_d2d_wait("rs_d2d_wait_" + key, rs[key], after)
        sums = [pair_sum("pair_sum_%s_%d" % (key, i), g, l, core) for i, (g, l) in enumerate(zip(grads, lands))]
        rs[key] = rs_ici_start("rs_ici_start_" + key, sums)

    gw_down = mm_tn("g_w_down", f, dffn, BF16)
    rs_begin("down", [blocks(gw_down)])
    df = mm_nt("d_f", dffn, w_down_f, ACT)
    rs_middle("down", df)
    dupre, gcw_g, gcw_v, gcb_g, gcb_v = conv_act_bwd(upre, cw, gv, df)
    gw_up3 = mm_tn_h3("g_w_up", h2, dupre, N_DEV, BF16)
    rs_begin("up", [gw_up3])
    dh2 = mm_nt_h3("d_h2", dupre, g_up, ACT)
    rs_middle("up", dh2)
    dx1, dy1, acc_mid = mid_bwd(dh2, dout, x1, y1, row(pre_norm2_g), scale2, gate1, row(post_norm1_g))
    gw_out = mm_tn("g_w_out", merged, dy1, BF16)
    dmerged = mm_nt("d_merged", dy1, w_out_f, ACT)
    dya, dyb, dga, dgb = merge_bwd(z, o_ga, o_gb, y_a, y_b, dmerged)
    gw_a = mm_tn("g_w_a", a, dya, BF16)
    gw_b = mm_tn("g_w_b", o, dyb, BF16)
    rs_begin("mid", [blocks(gw_out), blocks(gw_a), blocks(gw_b)])
    da = mm_nt("d_a", dya, w_a_f, ACT)
    do = mm_nt("d_o", dyb, w_b_f, ACT)
    rs_middle("mid", do)
    duv, g_ws, g_bs3, acc_gm = gmlp_bwd(z, gmw, da, ln_g, ln_b, wm, bs3)
    dqn, dqp, dkv, dkp = attn_bwd(qn, qpr, kv, kpr, o, do, lse, cos4, sin4)
    dkpe = kpe_bwd(dkp, cos4, sin4, zp - o_kpe)
    dq_cat = jnp.concatenate([dqn, dqp], axis=1)
    w_uq_cat = jnp.concatenate([w_uq_n, w_uq_r], axis=1)
    dqln = mm_nt("d_qln", dq_cat, w_uq_cat, ACT)
    dq_lat, g_qnorm = rms_bwd_cols("q_norm_bwd", dqln, z, o_q, ql, row(q_norm_g))
    dkvn = mm_nt_b3("d_kvn", dkv, g_ukv, ACT)
    dkv_lat, g_kvnorm = rms_bwd_cols("kv_norm_bwd", dkvn, z, o_kv, kvl, row(kv_norm_g))
    dz = jnp.concatenate([duv, dq_lat, dkv_lat, dga, dgb, dkpe], axis=1)
    gw_in_p = mm_tn("g_w_in", dz, h1, BF16)

    def gw_in_rows(lo, hi):
        pieces = []
        for a, b, shift in ((0, src_kpe, 0), (src_kpe, src_kpe + QK_ROPE, o_kpe - src_kpe), (src_kpe + QK_ROPE, in_cols, -QK_ROPE)):
            if lo < b and hi > a:
                pieces.append(gw_in_p[max(lo, a) + shift:min(hi, b) + shift])
        return pieces[0] if len(pieces) == 1 else jnp.concatenate(pieces, axis=0)

    rs_begin("in", [jnp.stack([gw_in_rows(k * cs_in, (k + 1) * cs_in) for k in range(N_DEV)])])
    dh1 = mm_nn("d_h1", dz, w_in_p, ACT)
    grad_x, acc1 = pre1_bwd(dh1, dx1, x2, row(pre_norm1_g), scale1)

    dmod = jnp.concatenate([acc1[0], acc1[1], acc_mid[3], acc_mid[0], acc_mid[1], acc2[0]])
    small = [("pre_norm1_g", acc1[2]), ("gm_ln_g", acc_gm[0]), ("gm_ln_b", acc_gm[1]), ("gm_b_s", g_bs3.reshape(-1)),
             ("q_norm_g", g_qnorm[0]), ("kv_norm_g", g_kvnorm[0]), ("post_norm1_g", acc_mid[4]), ("pre_norm2_g", acc_mid[2]),
             ("conv_b", jnp.concatenate([gcb_g[0], gcb_v[0]])), ("post_norm2_g", acc2[1]), ("gm_w_s", g_ws.reshape(-1)),
             ("b_ada", dmod)]
    n_small = sum(v.shape[0] for _, v in small)
    n_cw = 3 * f2
    n_pack = _round_up(n_small + n_cw, PACK_ALIGN)
    tail = jnp.zeros((n_pack - n_small - n_cw,), F32)
    packed = jnp.concatenate([v for _, v in small] + [jnp.concatenate([gcw_g, gcw_v], axis=1).reshape(-1), tail])
    ag_small = ag_start("ag_start_small", [packed.reshape(-1, LANES)], packed)
    rs_middle("in", [packed, _TOKENS.pending[-1]])

    gw_uq_cat = mm_tn("g_w_uq", qln, dq_cat, BF16)
    gw_uq_f = jnp.concatenate([gw_uq_cat[:, :HEADS * QK_NOPE].reshape(ql, HEADS, QK_NOPE),
                               gw_uq_cat[:, HEADS * QK_NOPE:].reshape(ql, HEADS, QK_ROPE)], axis=2)
    gw_uq3 = gw_uq_f.reshape(ql, N_DEV, -1).transpose(1, 0, 2)
    gw_ukv3 = mm_tn_o3("g_w_ukv", kvn, dkv, N_DEV, BF16)
    rs_begin("mla", [gw_uq3, gw_ukv3])

    res = {}
    last = packed
    for key, names in (("down", ["w_down"]), ("up", ["w_up"]), ("mid", ["w_out", "w_branch_a", "w_branch_b"])):
        parts = rs_ici_wait("rs_ici_wait_" + key, rs[key], last)
        for k, p in zip(names, parts):
            res[k] = adamw("adamw_" + k, weights[k], mom1[k], mom2[k], p)
            last = res[k][0]
        if key == "down":
            rs_middle("mla", last)

    assert [k for k, _ in small] == small_names and n_small == n_small_early
    (gathered,) = ag_wait("ag_wait_small", ag_forward("ag_forward_small", ag_small, last), last)
    sm = [t.reshape(-1) for t in adamw("adamw_small", *packed_state, gathered)]
    off = 0
    for k, v in small:
        res[k] = tuple(t[off:off + v.shape[0]].reshape(weights[k].shape) for t in sm)
        off += v.shape[0]

    cs_cw = conv_w.shape[1]
    g_cw_full = sm[0][n_small:n_small + n_cw].reshape(3, f2)
    g_cw_mine = lax.dynamic_slice(g_cw_full, (0, my * cs_cw), (3, cs_cw))
    res["conv_w"] = adamw("adamw_conv_w", conv_w, mom1["conv_w"], mom2["conv_w"], g_cw_mine[None])

    cs_ada = w_ada.shape[1]
    off_b = n_small - N_MOD * d
    dmod_all = gathered.reshape(N_DEV, -1)[:, off_b:off_b + N_MOD * d]
    dmod_cols = lax.dynamic_slice(dmod_all, (0, my * cs_ada), (N_DEV, cs_ada))
    res["w_ada"] = adamw("adamw_w_ada", w_ada, mom1["w_ada"], mom2["w_ada"], ada_bwd_w(sc_all, dmod_cols))

    (p_in,) = rs_ici_wait("rs_ici_wait_in", rs["in"], res["w_ada"][0])
    w_in_res = adamw("adamw_w_in", w_in.T, mom1["w_in"].T, mom2["w_in"].T, p_in)
    res["w_in"] = tuple(t.T for t in w_in_res)
    for k, p in zip(["w_uq", "w_ukv"], rs_ici_wait("rs_ici_wait_mla", rs["mla"], w_in_res[0])):
        res[k] = adamw("adamw_" + k, weights[k], mom1[k], mom2[k], p)

    _TOKENS.clear()
    outs = [loss, grad_x[None]]
    for i in range(4):
        outs += [res[k][i] for k in order]
    return tuple(outs)
```

```python
import jax
import jax.numpy as jnp
from jax import lax
from jax.experimental import pallas as pl
from jax.experimental.pallas import tpu as pltpu

F32 = jnp.float32
BF16 = jnp.bfloat16
ACT = BF16

N_DEV = 8
HEADS = 16
QK_NOPE = 128
QK_ROPE = 64
V_HEAD = 128
CHUNK = 128
ROPE_THETA = 10000.0
EPS = 1e-6
N_MOD = 6
ADAM_LR, ADAM_B1, ADAM_B2, ADAM_EPS, ADAM_WD, ADAM_STEP = 0.001, 0.9, 0.999, 1e-08, 0.01, 10

LANES = 128
VMEM_LIMIT_BYTES = 48 * 2 ** 20
ROW_TILE = 256
COL_TILE = 256
ATT_TILE = 512
Z_PAD = 512
ADAMW_TILE_ELEMS = 1 << 18
PACK_ALIGN = 8 * LANES
MESH = pl.DeviceIdType.MESH


def _params(*sem):
    return pltpu.CompilerParams(dimension_semantics=sem if sem else None, vmem_limit_bytes=VMEM_LIMIT_BYTES)


def _tile(dim, target):
    t = (min(dim, target) // LANES) * LANES
    while t >= LANES:
        if dim % t == 0:
            return t
        t -= LANES
    return dim


def _full(shape):
    nd = len(shape)
    return pl.BlockSpec(shape, lambda *_: (0,) * nd)


class _Tokens:
    KEEP = 2

    def __init__(self):
        self.pending = []

    def push(self, token):
        self.pending = (self.pending + [token])[-self.KEEP:]

    def take(self):
        return list(self.pending)

    def clear(self):
        self.pending = []


_TOKENS = _Tokens()


def _matmul(name, a, b, *, grid, a_spec, b_spec, o_spec, out_shape, contract, acc_shape, split=1):
    nk = grid[2]
    deps = _TOKENS.take()

    def product(a_ref, b_ref):
        if len(b_ref.shape) == 2:
            return lax.dot_general(a_ref[...].astype(BF16), b_ref[...].astype(BF16), (contract, ((), ())), preferred_element_type=F32)
        cs = b_ref.shape[2]
        return sum(lax.dot_general(a_ref[:, s * cs:(s + 1) * cs].astype(BF16), b_ref[s].astype(BF16), (contract, ((), ())),
                                   preferred_element_type=F32) for s in range(split))

    def body_one_step(a_ref, b_ref, *rest):
        o_ref = rest[len(deps)]
        o_ref[...] = product(a_ref, b_ref).astype(o_ref.dtype)

    def body(a_ref, b_ref, *rest):
        o_ref, acc_ref = rest[len(deps):]
        k = pl.program_id(2)

        @pl.when(k == 0)
        def _():
            acc_ref[...] = jnp.zeros_like(acc_ref)

        acc_ref[...] += product(a_ref, b_ref)

        @pl.when(k == nk - 1)
        def _():
            o_ref[...] = acc_ref[...].astype(o_ref.dtype)

    return pl.pallas_call(
        body_one_step if nk == 1 else body, name=name, grid=grid,
        in_specs=[a_spec, b_spec] + [pl.BlockSpec(memory_space=pl.ANY)] * len(deps),
        out_specs=o_spec, out_shape=out_shape, scratch_shapes=[] if nk == 1 else [pltpu.VMEM(acc_shape, F32)],
        compiler_params=_params("parallel", "parallel", "arbitrary"))(a, b, *deps)


T_OUT, T_OUT_WIDE, TK = 1024, 1408, 2816


def _out_tile(dim):
    return T_OUT_WIDE if dim % T_OUT_WIDE == 0 else _tile(dim, T_OUT)


def _tk(a, b):
    return TK if a.dtype == BF16 and b.dtype == BF16 else TK // 2


def mm_nn(name, a, b, dtype):
    (m, k), n = a.shape, b.shape[1]
    tm, tn, tk = _out_tile(m), _out_tile(n), _tile(k, _tk(a, b))
    return _matmul(name, a, b, grid=(m // tm, n // tn, k // tk),
                   a_spec=pl.BlockSpec((tm, tk), lambda i, j, kk: (i, kk)),
                   b_spec=pl.BlockSpec((tk, tn), lambda i, j, kk: (kk, j)),
                   o_spec=pl.BlockSpec((tm, tn), lambda i, j, kk: (i, j)),
                   out_shape=jax.ShapeDtypeStruct((m, n), dtype), contract=((1,), (0,)), acc_shape=(tm, tn))


def mm_nn_b3(name, a, b3, dtype):
    (m, k), (nj, _, cs) = a.shape, b3.shape
    tm, tk = _out_tile(m), _tile(k, _tk(a, b3))
    return _matmul(name, a, b3, grid=(m // tm, nj, k // tk),
                   a_spec=pl.BlockSpec((tm, tk), lambda i, j, kk: (i, kk)),
                   b_spec=pl.BlockSpec((None, tk, cs), lambda i, j, kk: (j, kk, 0)),
                   o_spec=pl.BlockSpec((tm, cs), lambda i, j, kk: (i, j)),
                   out_shape=jax.ShapeDtypeStruct((m, nj * cs), dtype), contract=((1,), (0,)), acc_shape=(tm, cs))


def mm_nt(name, a, b, dtype):
    (m, k), n = a.shape, b.shape[0]
    tm, tn, tk = _out_tile(m), _out_tile(n), _tile(k, _tk(a, b))
    return _matmul(name, a, b, grid=(m // tm, n // tn, k // tk),
                   a_spec=pl.BlockSpec((tm, tk), lambda i, j, kk: (i, kk)),
                   b_spec=pl.BlockSpec((tn, tk), lambda i, j, kk: (j, kk)),
                   o_spec=pl.BlockSpec((tm, tn), lambda i, j, kk: (i, j)),
                   out_shape=jax.ShapeDtypeStruct((m, n), dtype), contract=((1,), (1,)), acc_shape=(tm, tn))


def mm_nt_b3(name, a, b3, dtype):
    m, (nj, n, cs) = a.shape[0], b3.shape
    tm, tn = _out_tile(m), _out_tile(n)
    return _matmul(name, a, b3, grid=(m // tm, n // tn, nj),
                   a_spec=pl.BlockSpec((tm, cs), lambda i, j, kk: (i, kk)),
                   b_spec=pl.BlockSpec((None, tn, cs), lambda i, j, kk: (kk, j, 0)),
                   o_spec=pl.BlockSpec((tm, tn), lambda i, j, kk: (i, j)),
                   out_shape=jax.ShapeDtypeStruct((m, n), dtype), contract=((1,), (1,)), acc_shape=(tm, tn))


def mm_nt_h3(name, a3, b3, dtype):
    (_, m, _), (nj, n, cs) = a3.shape, b3.shape
    tm, tn, hj = _out_tile(m), _out_tile(n), nj // 2
    pair = 2 if hj % 2 == 0 else 1
    return _matmul(name, a3, b3.reshape(nj // pair, pair, n, cs), grid=(m // tm, n // tn, nj // pair),
                   a_spec=pl.BlockSpec((None, tm, pair * cs), lambda i, j, kk: (kk // (hj // pair), i, kk % (hj // pair))),
                   b_spec=pl.BlockSpec((None, pair, tn, cs), lambda i, j, kk: (kk, 0, j, 0)),
                   o_spec=pl.BlockSpec((tm, tn), lambda i, j, kk: (i, j)),
                   out_shape=jax.ShapeDtypeStruct((m, n), dtype), contract=((1,), (1,)), acc_shape=(tm, tn), split=pair)


def mm_tn_h3(name, a, b3, nj, dtype):
    (k, m), half = a.shape, b3.shape[2]
    hj = nj // 2
    cs = half // hj
    tm, tk = _out_tile(m), _tile(k, _tk(a, b3))
    return _matmul(name, a, b3, grid=(m // tm, nj, k // tk),
                   a_spec=pl.BlockSpec((tk, tm), lambda i, j, kk: (kk, i)),
                   b_spec=pl.BlockSpec((None, tk, cs), lambda i, j, kk: (j // hj, kk, j % hj)),
                   o_spec=pl.BlockSpec((None, tm, cs), lambda i, j, kk: (j, i, 0)),
                   out_shape=jax.ShapeDtypeStruct((nj, m, cs), dtype), contract=((0,), (0,)), acc_shape=(tm, cs))


def mm_tn(name, a, b, dtype):
    (k, m), n = a.shape, b.shape[1]
    tm, tn, tk = _out_tile(m), _out_tile(n), _tile(k, _tk(a, b))
    return _matmul(name, a, b, grid=(m // tm, n // tn, k // tk),
                   a_spec=pl.BlockSpec((tk, tm), lambda i, j, kk: (kk, i)),
                   b_spec=pl.BlockSpec((tk, tn), lambda i, j, kk: (kk, j)),
                   o_spec=pl.BlockSpec((tm, tn), lambda i, j, kk: (i, j)),
                   out_shape=jax.ShapeDtypeStruct((m, n), dtype), contract=((0,), (0,)), acc_shape=(tm, tn))


def mm_tn_o3(name, a, b, nj, dtype):
    (k, m), n = a.shape, b.shape[1]
    cs = n // nj
    tm, tk = _out_tile(m), _tile(k, _tk(a, b))
    return _matmul(name, a, b, grid=(m // tm, nj, k // tk),
                   a_spec=pl.BlockSpec((tk, tm), lambda i, j, kk: (kk, i)),
                   b_spec=pl.BlockSpec((tk, cs), lambda i, j, kk: (kk, j)),
                   o_spec=pl.BlockSpec((None, tm, cs), lambda i, j, kk: (j, i, 0)),
                   out_shape=jax.ShapeDtypeStruct((nj, m, cs), dtype), contract=((0,), (0,)), acc_shape=(tm, cs))


_GELU_C = 0.7978845608028654
_GELU_A = 0.044715


def _f32(ref):
    return ref[...].astype(F32)


def _gelu(x):
    x = x.astype(F32)
    return 0.5 * x * (1.0 + jnp.tanh(_GELU_C * (x + _GELU_A * x * x * x)))


def _gelu_and_grad(x):
    x = x.astype(F32)
    t = jnp.tanh(_GELU_C * (x + _GELU_A * x * x * x))
    y = 0.5 * x * (1.0 + t)
    dy = 0.5 * (1.0 + t) + 0.5 * x * (1.0 - t * t) * (_GELU_C * (1.0 + 3.0 * _GELU_A * x * x))
    return y, dy


def _sigmoid(x):
    return 0.5 * jnp.tanh(0.5 * x.astype(F32)) + 0.5


def _rms_stats(x):
    x = x.astype(F32)
    inv = lax.rsqrt(jnp.mean(x * x, axis=-1, keepdims=True) + EPS)
    return inv, x * inv


def _rms_bwd(dyhat, yhat, inv):
    return inv * (dyhat - yhat * jnp.mean(dyhat * yhat, axis=-1, keepdims=True))


def _colsum(x):
    return jnp.sum(x, axis=0, keepdims=True)


def _rope(x, cos4, sin4):
    lane = lax.broadcasted_iota(jnp.int32, x.shape, x.ndim - 1)
    first_half = (lane % QK_ROPE) < (QK_ROPE // 2)
    partner = jnp.where(first_half, pltpu.roll(x, LANES - QK_ROPE // 2, x.ndim - 1), pltpu.roll(x, QK_ROPE // 2, x.ndim - 1))
    return x * cos4 + partner * sin4


def norm_mod_fwd(name, x, g, scale, shift):
    s, d = x.shape
    tr = _tile(s, ROW_TILE)

    def body(x_ref, g_ref, sc_ref, sh_ref, o_ref):
        _, xh = _rms_stats(x_ref[...])
        o_ref[...] = (xh * g_ref[...] * (1.0 + sc_ref[...]) + sh_ref[...]).astype(o_ref.dtype)

    row = pl.BlockSpec((tr, d), lambda i: (i, 0))
    vec = pl.BlockSpec((1, d), lambda i: (0, 0))
    return pl.pallas_call(body, name=name, grid=(s // tr,), in_specs=[row, vec, vec, vec], out_specs=row,
                          out_shape=jax.ShapeDtypeStruct((s, d), BF16), compiler_params=_params("parallel"))(x, g, scale, shift)


def rms_fwd_cols(name, z, off, width, g):
    s = z.shape[0]
    tr = _tile(s, ROW_TILE)
    assert off % width == 0

    def body(x_ref, g_ref, o_ref):
        _, xh = _rms_stats(x_ref[...])
        o_ref[...] = (xh * g_ref[...]).astype(o_ref.dtype)

    return pl.pallas_call(body, name=name, grid=(s // tr,),
                          in_specs=[pl.BlockSpec((tr, width), lambda i: (i, off // width)), pl.BlockSpec((1, width), lambda i: (0, 0))],
                          out_specs=pl.BlockSpec((tr, width), lambda i: (i, 0)),
                          out_shape=jax.ShapeDtypeStruct((s, width), BF16), compiler_params=_params("parallel"))(z, g)


def rms_bwd_cols(name, dy, z, off, width, g):
    s = z.shape[0]
    tr = _tile(s, ROW_TILE)

    def body(dy_ref, x_ref, g_ref, dx_ref, gg_ref):
        @pl.when(pl.program_id(0) == 0)
        def _():
            gg_ref[...] = jnp.zeros_like(gg_ref)

        inv, xh = _rms_stats(x_ref[...])
        dy_v = _f32(dy_ref)
        gg_ref[...] += _colsum(dy_v * xh)
        dx_ref[...] = _rms_bwd(dy_v * g_ref[...], xh, inv).astype(dx_ref.dtype)

    return pl.pallas_call(body, name=name, grid=(s // tr,),
                          in_specs=[pl.BlockSpec((tr, width), lambda i: (i, 0)), pl.BlockSpec((tr, width), lambda i: (i, off // width)),
                                    pl.BlockSpec((1, width), lambda i: (0, 0))],
                          out_specs=[pl.BlockSpec((tr, width), lambda i: (i, 0)), pl.BlockSpec((1, width), lambda i: (0, 0))],
                          out_shape=[jax.ShapeDtypeStruct((s, width), BF16), jax.ShapeDtypeStruct((1, width), F32)],
                          compiler_params=_params("arbitrary"))(dy, z, g)


def post1_pre2_fwd(x, y, gate, g_post, g_pre, scale, shift):
    s, d = x.shape
    tr = _tile(s, ROW_TILE)

    def body(x_ref, y_ref, gate_ref, gp_ref, g_ref, sc_ref, sh_ref, x1_ref, h_ref):
        _, yh = _rms_stats(y_ref[...])
        x1 = x_ref[...] + gate_ref[...] * (yh * gp_ref[...])
        x1_ref[...] = x1
        _, xh = _rms_stats(x1)
        h_ref[...] = (xh * g_ref[...] * (1.0 + sc_ref[...]) + sh_ref[...]).astype(h_ref.dtype)

    row = pl.BlockSpec((tr, d), lambda i: (i, 0))
    vec = pl.BlockSpec((1, d), lambda i: (0, 0))
    return pl.pallas_call(body, name="post1_pre2_fwd", grid=(s // tr,), in_specs=[row, row, vec, vec, vec, vec, vec], out_specs=[row, row],
                          out_shape=[jax.ShapeDtypeStruct((s, d), F32), jax.ShapeDtypeStruct((s, d), BF16)],
                          compiler_params=_params("parallel"))(x, y, gate, g_post, g_pre, scale, shift)


def post2_loss_bwd(x1, ffn, target, gate2, g):
    s, d = x1.shape
    tr = _tile(s, ROW_TILE)

    def body(x_ref, y_ref, t_ref, gate_ref, g_ref, loss_ref, dout_ref, dy_ref, acc_ref):
        @pl.when(pl.program_id(0) == 0)
        def _():
            loss_ref[...] = jnp.zeros_like(loss_ref)
            acc_ref[...] = jnp.zeros_like(acc_ref)

        inv, yh = _rms_stats(y_ref[...])
        r = yh * g_ref[...]
        err = x_ref[...] + gate_ref[...] * r - t_ref[...]
        loss_ref[...] += 0.5 * jnp.sum(jnp.mean(err * err, axis=-1, keepdims=True))
        dout = err / d
        dout_ref[...] = dout
        dr = dout * gate_ref[...]
        acc_ref[0:1, :] += _colsum(dout * r)
        acc_ref[1:2, :] += _colsum(dr * yh)
        dy_ref[...] = _rms_bwd(dr * g_ref[...], yh, inv).astype(dy_ref.dtype)

    row = pl.BlockSpec((tr, d), lambda i: (i, 0))
    vec = pl.BlockSpec((1, d), lambda i: (0, 0))
    return pl.pallas_call(
        body, name="post2_loss_bwd", grid=(s // tr,), in_specs=[row, row, row, vec, vec],
        out_specs=[_full((8, LANES)), row, row, _full((8, d))],
        out_shape=[jax.ShapeDtypeStruct((8, LANES), F32), jax.ShapeDtypeStruct((s, d), F32),
                   jax.ShapeDtypeStruct((s, d), BF16), jax.ShapeDtypeStruct((8, d), F32)],
        compiler_params=_params("arbitrary"))(x1, ffn, target, gate2, g)


def mid_bwd(dh2, dout, x1, y1, pre2_g, scale2, gate1, post1_g):
    s, d = x1.shape
    tr = _tile(s, ROW_TILE)

    def body(dh_ref, dout_ref, x_ref, y_ref, g2_ref, sc_ref, gate_ref, g1_ref, dx_ref, dy_ref, acc_ref):
        @pl.when(pl.program_id(0) == 0)
        def _():
            acc_ref[...] = jnp.zeros_like(acc_ref)

        dh = _f32(dh_ref)
        inv2, xh = _rms_stats(x_ref[...])
        acc_ref[0:1, :] += _colsum(dh)
        acc_ref[1:2, :] += _colsum(dh * (xh * g2_ref[...]))
        t = dh * (1.0 + sc_ref[...])
        acc_ref[2:3, :] += _colsum(t * xh)
        dx1 = dout_ref[...] + _rms_bwd(t * g2_ref[...], xh, inv2)
        dx_ref[...] = dx1
        inv1, yh = _rms_stats(y_ref[...])
        acc_ref[3:4, :] += _colsum(dx1 * (yh * g1_ref[...]))
        dr = dx1 * gate_ref[...]
        acc_ref[4:5, :] += _colsum(dr * yh)
        dy_ref[...] = _rms_bwd(dr * g1_ref[...], yh, inv1).astype(dy_ref.dtype)

    row = pl.BlockSpec((tr, d), lambda i: (i, 0))
    vec = pl.BlockSpec((1, d), lambda i: (0, 0))
    return pl.pallas_call(
        body, name="mid_bwd", grid=(s // tr,), in_specs=[row, row, row, row, vec, vec, vec, vec],
        out_specs=[row, row, _full((8, d))],
        out_shape=[jax.ShapeDtypeStruct((s, d), F32), jax.ShapeDtypeStruct((s, d), BF16), jax.ShapeDtypeStruct((8, d), F32)],
        compiler_params=_params("arbitrary"))(dh2, dout, x1, y1, pre2_g, scale2, gate1, post1_g)


def pre1_bwd(dh1, dx1, x, pre1_g, scale1):
    s, d = x.shape
    tr = _tile(s, ROW_TILE)

    def body(dh_ref, dx1_ref, x_ref, g_ref, sc_ref, dx_ref, acc_ref):
        @pl.when(pl.program_id(0) == 0)
        def _():
            acc_ref[...] = jnp.zeros_like(acc_ref)

        dh = _f32(dh_ref)
        inv, xh = _rms_stats(x_ref[...])
        acc_ref[0:1, :] += _colsum(dh)
        acc_ref[1:2, :] += _colsum(dh * (xh * g_ref[...]))
        t = dh * (1.0 + sc_ref[...])
        acc_ref[2:3, :] += _colsum(t * xh)
        dx_ref[...] = dx1_ref[...] + _rms_bwd(t * g_ref[...], xh, inv)

    row = pl.BlockSpec((tr, d), lambda i: (i, 0))
    vec = pl.BlockSpec((1, d), lambda i: (0, 0))
    return pl.pallas_call(
        body, name="pre1_bwd", grid=(s // tr,), in_specs=[row, row, row, vec, vec], out_specs=[row, _full((8, d))],
        out_shape=[jax.ShapeDtypeStruct((s, d), F32), jax.ShapeDtypeStruct((8, d), F32)],
        compiler_params=_params("arbitrary"))(dh1, dx1, x, pre1_g, scale1)


def _ln_stats(v):
    mu = jnp.mean(v, axis=-1, keepdims=True)
    vc = v - mu
    rstd = lax.rsqrt(jnp.mean(vc * vc, axis=-1, keepdims=True) + EPS)
    return rstd, vc * rstd


def gmlp_fwd(z, width, ln_g, ln_b, wm, bs3):
    s = z.shape[0]
    groups = width // CHUNK

    def body(u_ref, v_ref, g_ref, b_ref, wm_ref, bs_ref, a_ref):
        ug = _gelu(u_ref[...])
        _, vh = _ln_stats(_gelu(v_ref[...]))
        vn = (vh * g_ref[...] + b_ref[...]).astype(BF16)
        for g in range(groups):
            cols = slice(g * CHUNK, (g + 1) * CHUNK)
            mixed = jnp.dot(wm_ref[g], vn[:, cols], preferred_element_type=F32) + bs_ref[g]
            a_ref[:, cols] = (ug[:, cols] * mixed).astype(a_ref.dtype)

    vec = pl.BlockSpec((1, width), lambda n: (0, 0))
    return pl.pallas_call(
        body, name="gmlp_fwd", grid=(s // CHUNK,),
        in_specs=[pl.BlockSpec((CHUNK, width), lambda n: (n, 0)), pl.BlockSpec((CHUNK, width), lambda n: (n, 1)), vec, vec,
                  _full(wm.shape), _full(bs3.shape)],
        out_specs=pl.BlockSpec((CHUNK, width), lambda n: (n, 0)),
        out_shape=jax.ShapeDtypeStruct((s, width), BF16), compiler_params=_params("parallel"))(z, z, ln_g, ln_b, wm, bs3)


def gmlp_bwd(z, width, da, ln_g, ln_b, wm, bs3):
    s = z.shape[0]
    groups = width // CHUNK

    def body(u_ref, v_ref, da_ref, g_ref, b_ref, wm_ref, bs_ref, duv_ref, gw_ref, gb_ref, acc_ref, dvn_ref):
        @pl.when(pl.program_id(0) == 0)
        def _():
            gw_ref[...] = jnp.zeros_like(gw_ref)
            gb_ref[...] = jnp.zeros_like(gb_ref)
            acc_ref[...] = jnp.zeros_like(acc_ref)

        ug, dug = _gelu_and_grad(u_ref[...])
        vg, dvg = _gelu_and_grad(v_ref[...])
        rstd, vh = _ln_stats(vg)
        vn = (vh * g_ref[...] + b_ref[...]).astype(BF16)
        da_v = _f32(da_ref)
        for g in range(groups):
            cols = slice(g * CHUNK, (g + 1) * CHUNK)
            mixed = jnp.dot(wm_ref[g], vn[:, cols], preferred_element_type=F32) + bs_ref[g]
            duv_ref[:, cols] = (da_v[:, cols] * mixed * dug[:, cols]).astype(duv_ref.dtype)
            dm = da_v[:, cols] * ug[:, cols]
            gb_ref[g] += jnp.sum(dm, axis=-1, keepdims=True)
            dmb = dm.astype(BF16)
            gw_ref[g] += lax.dot_general(dmb, vn[:, cols], (((1,), (1,)), ((), ())), preferred_element_type=F32)
            dvn_ref[:, cols] = lax.dot_general(wm_ref[g], dmb, (((0,), (0,)), ((), ())), preferred_element_type=F32)
        dvn = dvn_ref[...]
        acc_ref[0:1, :] += _colsum(dvn * vh)
        acc_ref[1:2, :] += _colsum(dvn)
        dvh = dvn * g_ref[...]
        dv = rstd * (dvh - jnp.mean(dvh, axis=-1, keepdims=True) - vh * jnp.mean(dvh * vh, axis=-1, keepdims=True))
        duv_ref[:, width:] = (dv * dvg).astype(duv_ref.dtype)

        @pl.when(pl.program_id(0) == pl.num_programs(0) - 1)
        def _():
            q = lax.broadcasted_iota(jnp.int32, gw_ref.shape, 1)
            p = lax.broadcasted_iota(jnp.int32, gw_ref.shape, 2)
            gw_ref[...] = jnp.where(p <= q, gw_ref[...], 0.0)

    vec = pl.BlockSpec((1, width), lambda n: (0, 0))
    blk = pl.BlockSpec((CHUNK, width), lambda n: (n, 0))
    return pl.pallas_call(
        body, name="gmlp_bwd", grid=(s // CHUNK,),
        in_specs=[blk, pl.BlockSpec((CHUNK, width), lambda n: (n, 1)), blk, vec, vec, _full(wm.shape), _full(bs3.shape)],
        out_specs=[pl.BlockSpec((CHUNK, 2 * width), lambda n: (n, 0)), _full(wm.shape), _full(bs3.shape), _full((8, width))],
        out_shape=[jax.ShapeDtypeStruct((s, 2 * width), BF16), jax.ShapeDtypeStruct(wm.shape, F32),
                   jax.ShapeDtypeStruct(bs3.shape, F32), jax.ShapeDtypeStruct((8, width), F32)],
        scratch_shapes=[pltpu.VMEM((CHUNK, width), F32)],
        compiler_params=_params("arbitrary"))(z, z, da, ln_g, ln_b, wm, bs3)


def merge_fwd(z, off_a, off_b, ya, yb):
    s, d = ya.shape
    tr, tc = _tile(s, ROW_TILE * 2), _tile(d, COL_TILE)
    assert off_a % tc == 0 and off_b % tc == 0

    def body(ga_ref, gb_ref, ya_ref, yb_ref, o_ref):
        o_ref[...] = (_sigmoid(ga_ref[...]) * _f32(ya_ref) + _sigmoid(gb_ref[...]) * _f32(yb_ref)).astype(o_ref.dtype)

    blk = pl.BlockSpec((tr, tc), lambda i, j: (i, j))
    return pl.pallas_call(
        body, name="merge_fwd", grid=(s // tr, d // tc),
        in_specs=[pl.BlockSpec((tr, tc), lambda i, j: (i, off_a // tc + j)), pl.BlockSpec((tr, tc), lambda i, j: (i, off_b // tc + j)), blk, blk],
        out_specs=blk, out_shape=jax.ShapeDtypeStruct((s, d), BF16), compiler_params=_params("parallel", "parallel"))(z, z, ya, yb)


def merge_bwd(z, off_a, off_b, ya, yb, dm):
    s, d = ya.shape
    tr, tc = _tile(s, ROW_TILE * 2), _tile(d, COL_TILE)
    nc = d // tc

    def body(ga_ref, gb_ref, ya_ref, yb_ref, dm_ref, dya_ref, dyb_ref, dga_ref, dgb_ref):
        dm_v = _f32(dm_ref)
        sa, sb = _sigmoid(ga_ref[...]), _sigmoid(gb_ref[...])
        dya_ref[...] = (dm_v * sa).astype(dya_ref.dtype)
        dyb_ref[...] = (dm_v * sb).astype(dyb_ref.dtype)
        dga_ref[...] = (dm_v * _f32(ya_ref) * sa * (1.0 - sa)).astype(dga_ref.dtype)
        dgb_ref[...] = (dm_v * _f32(yb_ref) * sb * (1.0 - sb)).astype(dgb_ref.dtype)

    blk = pl.BlockSpec((tr, tc), lambda i, j: (i, j))
    out = jax.ShapeDtypeStruct((s, d), BF16)
    return pl.pallas_call(
        body, name="merge_bwd", grid=(s // tr, nc),
        in_specs=[pl.BlockSpec((tr, tc), lambda i, j: (i, off_a // tc + j)), pl.BlockSpec((tr, tc), lambda i, j: (i, off_b // tc + j)), blk, blk, blk],
        out_specs=[blk, blk, blk, blk], out_shape=[out, out, out, out],
        compiler_params=_params("parallel", "parallel"))(z, z, ya, yb, dm)


_ATT_SCALE = (QK_NOPE + QK_ROPE) ** -0.5
_NEG = -1e30


def rope_k(z, off, cos4, sin4):
    s = z.shape[0]
    tr = _tile(s, ROW_TILE * 2)
    assert off % LANES == 0

    def body(k_ref, c_ref, s_ref, o_ref):
        k = _f32(k_ref)
        k = k + pltpu.roll(k, QK_ROPE, 1)
        o_ref[...] = _rope(k, c_ref[...], s_ref[...]).astype(o_ref.dtype)

    row = pl.BlockSpec((tr, LANES), lambda i: (i, 0))
    return pl.pallas_call(body, name="rope_k", grid=(s // tr,),
                          in_specs=[pl.BlockSpec((tr, LANES), lambda i: (i, off // LANES)), row, row], out_specs=row,
                          out_shape=jax.ShapeDtypeStruct((s, LANES), BF16), compiler_params=_params("parallel"))(z, cos4, sin4)


def _dot_nt(a, b):
    return lax.dot_general(a, b, (((1,), (1,)), ((), ())), preferred_element_type=F32)


def _dot_tn(a, b):
    return lax.dot_general(a, b, (((0,), (0,)), ((), ())), preferred_element_type=F32)


def _q_cat(q_n, qpr, hh):
    lane = lax.broadcasted_iota(jnp.int32, qpr.shape, 1)
    sel = (lane < QK_ROPE) if hh == 0 else (lane >= QK_ROPE)
    return jnp.concatenate([q_n, jnp.where(sel, qpr, jnp.zeros_like(qpr))], axis=1)


def _causal(sc):
    row = lax.broadcasted_iota(jnp.int32, sc.shape, 0)
    col = lax.broadcasted_iota(jnp.int32, sc.shape, 1)
    return jnp.where(col <= row, sc, _NEG)


def attn_fwd(qn, qp, kv, kpr, cos4, sin4):
    s = qn.shape[0]
    hp = HEADS // 2
    t = _tile(s, ATT_TILE)
    nq = s // t

    def body(qn_ref, qp_ref, kv_ref, kp_ref, c_ref, s_ref, o_ref, qpr_ref, l_ref, kcat_ref):
        qi = pl.program_id(1)

        @pl.when(qi == 0)
        def _():
            for hh in range(2):
                kcat_ref[hh, :, 0:QK_NOPE] = kv_ref[:, 2 * hh * QK_NOPE:(2 * hh + 1) * QK_NOPE]
                kcat_ref[hh, :, QK_NOPE:] = kp_ref[...]

        qpr = _rope(qp_ref[...], c_ref[...], s_ref[...]).astype(BF16)
        qpr_ref[...] = qpr
        qcat = [_q_cat(qn_ref[:, hh * QK_NOPE:(hh + 1) * QK_NOPE], qpr, hh) for hh in range(2)]

        def block(kb, carry, diagonal):
            rows = pl.ds(pl.multiple_of(kb * t, t), t)
            out = []
            for hh in range(2):
                m, l, acc = carry[hh]
                sc = _dot_nt(qcat[hh], kcat_ref[hh, rows, :]) * _ATT_SCALE
                if diagonal:
                    sc = _causal(sc)
                m_new = jnp.maximum(m, jnp.max(sc, axis=-1, keepdims=True))
                alpha = jnp.exp(m - m_new)
                p = jnp.exp(sc - m_new)
                l = alpha * l + jnp.sum(p, axis=-1, keepdims=True)
                v = kv_ref[rows, (2 * hh + 1) * QK_NOPE:(2 * hh + 2) * QK_NOPE]
                acc = alpha * acc + jnp.dot(p.astype(BF16), v, preferred_element_type=F32)
                out.append((m_new, l, acc))
            return tuple(out)

        one = (jnp.full((t, 1), _NEG, F32), jnp.zeros((t, 1), F32), jnp.zeros((t, V_HEAD), F32))
        carry = lax.fori_loop(0, qi, lambda kb, cr: block(kb, cr, False), (one, one))
        carry = block(qi, carry, True)
        for hh in range(2):
            m, l, acc = carry[hh]
            o_ref[:, hh * V_HEAD:(hh + 1) * V_HEAD] = (acc / l).astype(o_ref.dtype)
            l_ref[:, hh:hh + 1] = m + jnp.log(l)

    return pl.pallas_call(
        body, name="attn_fwd", grid=(hp, nq),
        in_specs=[pl.BlockSpec((t, 2 * QK_NOPE), lambda h, i: (i, h)), pl.BlockSpec((t, LANES), lambda h, i: (i, h)),
                  pl.BlockSpec((s, 4 * QK_NOPE), lambda h, i: (0, h)), _full((s, LANES)),
                  pl.BlockSpec((t, LANES), lambda h, i: (i, 0)), pl.BlockSpec((t, LANES), lambda h, i: (i, 0))],
        out_specs=[pl.BlockSpec((t, 2 * V_HEAD), lambda h, i: (i, h)), pl.BlockSpec((t, LANES), lambda h, i: (i, h)),
                   pl.BlockSpec((None, t, 2), lambda h, i: (h, i, 0))],
        out_shape=[jax.ShapeDtypeStruct((s, HEADS * V_HEAD), ACT), jax.ShapeDtypeStruct((s, HEADS * QK_ROPE), BF16),
                   jax.ShapeDtypeStruct((hp, s, 2), F32)],
        scratch_shapes=[pltpu.VMEM((2, s, 2 * QK_NOPE), BF16)],
        compiler_params=_params("parallel", "arbitrary"))(qn, qp, kv, kpr, cos4, sin4)


def attn_bwd(qn, qpr, kv, kpr, o, do, lse, cos4, sin4):
    s = qn.shape[0]
    hp = HEADS // 2
    t = _tile(s, ATT_TILE)
    nk = s // t

    def body(qn_ref, qpr_ref, kv_ref, kp_ref, o_ref, do_ref, l_ref, c_ref, s_ref,
             dqn_ref, dqp_ref, dkv_ref, dkp_ref, qcat_ref, dq_ref, delta_ref):
        ki = pl.program_id(1)

        @pl.when(ki == 0)
        def _():
            dq_ref[...] = jnp.zeros_like(dq_ref)
            for hh in range(2):
                qcat_ref[hh] = _q_cat(qn_ref[:, hh * QK_NOPE:(hh + 1) * QK_NOPE], qpr_ref[...], hh)
                cols = slice(hh * V_HEAD, (hh + 1) * V_HEAD)
                delta_ref[hh] = jnp.sum(do_ref[:, cols].astype(F32) * o_ref[:, cols].astype(F32), axis=-1, keepdims=True)

        rows_k = pl.ds(pl.multiple_of(ki * t, t), t)
        kcat = [jnp.concatenate([kv_ref[rows_k, 2 * hh * QK_NOPE:(2 * hh + 1) * QK_NOPE], kp_ref[rows_k, :]], axis=1) for hh in range(2)]
        vs = [kv_ref[rows_k, (2 * hh + 1) * QK_NOPE:(2 * hh + 2) * QK_NOPE] for hh in range(2)]

        def block(qb, carry, diagonal):
            rows = pl.ds(pl.multiple_of(qb * t, t), t)
            out = []
            for hh in range(2):
                dkc, dv = carry[hh]
                q_c = qcat_ref[hh, rows, :]
                do_b = do_ref[rows, hh * V_HEAD:(hh + 1) * V_HEAD].astype(BF16)
                sc = _dot_nt(q_c, kcat[hh]) * _ATT_SCALE
                if diagonal:
                    sc = _causal(sc)
                p = jnp.exp(sc - l_ref[rows, hh:hh + 1])
                dpv = _dot_nt(do_b, vs[hh])
                ds = (p * (dpv - delta_ref[hh, rows, :]) * _ATT_SCALE).astype(BF16)
                dv = dv + _dot_tn(p.astype(BF16), do_b)
                dkc = dkc + _dot_tn(ds, q_c)
                dq_ref[hh, rows, :] += jnp.dot(ds, kcat[hh], preferred_element_type=F32)
                out.append((dkc, dv))
            return tuple(out)

        one = (jnp.zeros((t, 2 * QK_NOPE), F32), jnp.zeros((t, V_HEAD), F32))
        carry = block(ki, (one, one), True)
        carry = lax.fori_loop(ki + 1, nk, lambda qb, cr: block(qb, cr, False), carry)
        dkp = jnp.zeros((t, LANES), F32)
        for hh in range(2):
            dkc, dv = carry[hh]
            dkv_ref[:, 2 * hh * QK_NOPE:(2 * hh + 1) * QK_NOPE] = dkc[:, :QK_NOPE].astype(dkv_ref.dtype)
            dkv_ref[:, (2 * hh + 1) * QK_NOPE:(2 * hh + 2) * QK_NOPE] = dv.astype(dkv_ref.dtype)
            dkp = dkp + dkc[:, QK_NOPE:]
        dkp_ref[...] = dkp

        @pl.when(ki == nk - 1)
        def _():
            lane = lax.broadcasted_iota(jnp.int32, (s, LANES), 1)
            dqp = jnp.where(lane < QK_ROPE, dq_ref[0, :, QK_NOPE:], dq_ref[1, :, QK_NOPE:])
            dqp_ref[...] = _rope(dqp, c_ref[...], -s_ref[...]).astype(dqp_ref.dtype)
            for hh in range(2):
                dqn_ref[:, hh * QK_NOPE:(hh + 1) * QK_NOPE] = dq_ref[hh, :, :QK_NOPE].astype(dqn_ref.dtype)

    qblk = pl.BlockSpec((s, 2 * QK_NOPE), lambda h, i: (0, h))
    pblk = pl.BlockSpec((s, LANES), lambda h, i: (0, h))
    tab = _full((s, LANES))
    return pl.pallas_call(
        body, name="attn_bwd", grid=(hp, nk),
        in_specs=[qblk, pblk, pl.BlockSpec((s, 4 * QK_NOPE), lambda h, i: (0, h)), tab, qblk, qblk,
                  pl.BlockSpec((None, s, 2), lambda h, i: (h, 0, 0)), tab, tab],
        out_specs=[qblk, pblk, pl.BlockSpec((t, 4 * QK_NOPE), lambda h, i: (i, h)), pl.BlockSpec((None, t, LANES), lambda h, i: (h, i, 0))],
        out_shape=[jax.ShapeDtypeStruct((s, HEADS * QK_NOPE), BF16), jax.ShapeDtypeStruct((s, HEADS * QK_ROPE), BF16),
                   jax.ShapeDtypeStruct((s, HEADS * 2 * QK_NOPE), BF16), jax.ShapeDtypeStruct((hp, s, LANES), F32)],
        scratch_shapes=[pltpu.VMEM((2, s, 2 * QK_NOPE), BF16), pltpu.VMEM((2, s, 2 * QK_NOPE), F32), pltpu.VMEM((2, s, 1), F32)],
        compiler_params=_params("parallel", "arbitrary"))(qn, qpr, kv, kpr, o, do, lse, cos4, sin4)


def kpe_bwd(dkp, cos4, sin4, pad_cols):
    hp, s, _ = dkp.shape
    tr = _tile(s, ROW_TILE * 2)

    def body(d_ref, c_ref, s_ref, o_ref):
        tot = d_ref[0]
        for h in range(1, hp):
            tot = tot + d_ref[h]
        tot = tot + pltpu.roll(tot, QK_ROPE, 1)
        lane = lax.broadcasted_iota(jnp.int32, tot.shape, 1)
        dk = jnp.where(lane < QK_ROPE, _rope(tot, c_ref[...], -s_ref[...]), jnp.zeros_like(tot))
        o_ref[...] = jnp.zeros_like(o_ref)
        o_ref[:, 0:LANES] = dk.astype(o_ref.dtype)

    row = pl.BlockSpec((tr, LANES), lambda i: (i, 0))
    return pl.pallas_call(body, name="kpe_bwd", grid=(s // tr,),
                          in_specs=[pl.BlockSpec((hp, tr, LANES), lambda i: (0, i, 0)), row, row],
                          out_specs=pl.BlockSpec((tr, pad_cols), lambda i: (i, 0)),
                          out_shape=jax.ShapeDtypeStruct((s, pad_cols), BF16), compiler_params=_params("parallel"))(dkp, cos4, sin4)


def _shift_down(x, n):
    row = lax.broadcasted_iota(jnp.int32, x.shape, 0)
    return jnp.where(row >= n, pltpu.roll(x, n, 0), jnp.zeros_like(x))


def _shift_up(x, n):
    rows = x.shape[0]
    row = lax.broadcasted_iota(jnp.int32, x.shape, 0)
    return jnp.where(row < rows - n, pltpu.roll(x, rows - n, 0), jnp.zeros_like(x))


def _conv(x, w_ref, b_ref):
    return w_ref[2:3, :] * x + w_ref[1:2, :] * _shift_down(x, 1) + w_ref[0:1, :] * _shift_down(x, 2) + b_ref[...]


def conv_act_fwd(upre, conv_w, conv_b):
    s, f2 = upre.shape
    f = f2 // 2
    tc = _tile(f, COL_TILE)
    nc = f // tc

    def body(ug_ref, uv_ref, wg_ref, wv_ref, bg_ref, bv_ref, o_ref, gv_ref):
        gh = _conv(_f32(ug_ref), wg_ref, bg_ref)
        vh = _conv(_f32(uv_ref), wv_ref, bv_ref)
        o_ref[...] = (gh * _sigmoid(gh) * vh).astype(o_ref.dtype)
        gv_ref[0] = gh.astype(gv_ref.dtype)
        gv_ref[1] = vh.astype(gv_ref.dtype)

    def spec(rows, shift):
        return pl.BlockSpec((rows, tc), lambda j: (0, j + shift))

    return pl.pallas_call(
        body, name="conv_act_fwd", grid=(nc,),
        in_specs=[spec(s, 0), spec(s, nc), spec(3, 0), spec(3, nc), spec(1, 0), spec(1, nc)],
        out_specs=[spec(s, 0), pl.BlockSpec((2, s, tc), lambda j: (0, 0, j))],
        out_shape=[jax.ShapeDtypeStruct((s, f), BF16), jax.ShapeDtypeStruct((2, s, f), ACT)],
        compiler_params=_params("parallel"))(upre, upre, conv_w, conv_w, conv_b, conv_b)


def conv_act_bwd(upre, conv_w, gv, df):
    s, f2 = upre.shape
    f = f2 // 2
    tc = _tile(f, COL_TILE)
    nc = f // tc

    def half(x, d, w_ref, du_ref, which, gw_ref, gb_ref):
        d1, d2 = _shift_up(d, 1), _shift_up(d, 2)
        gb_ref[...] = _colsum(d)
        gw_ref[2:3, :] = _colsum(d * x)
        gw_ref[1:2, :] = _colsum(d1 * x)
        gw_ref[0:1, :] = _colsum(d2 * x)
        du_ref[which] = (w_ref[2:3, :] * d + w_ref[1:2, :] * d1 + w_ref[0:1, :] * d2).astype(du_ref.dtype)

    def body(ug_ref, uv_ref, wg_ref, wv_ref, gv_ref, df_ref, du_ref, gwg_ref, gwv_ref, gbg_ref, gbv_ref):
        xg, xv = _f32(ug_ref), _f32(uv_ref)
        gh, vh = gv_ref[0].astype(F32), gv_ref[1].astype(F32)
        sg = _sigmoid(gh)
        df_v = _f32(df_ref)
        half(xg, df_v * vh * (sg * (1.0 + gh * (1.0 - sg))), wg_ref, du_ref, 0, gwg_ref, gbg_ref)
        half(xv, df_v * (gh * sg), wv_ref, du_ref, 1, gwv_ref, gbv_ref)

    def spec(rows, shift):
        return pl.BlockSpec((rows, tc), lambda j: (0, j + shift))

    gw = jax.ShapeDtypeStruct((3, f), F32)
    gb = jax.ShapeDtypeStruct((1, f), F32)
    return pl.pallas_call(
        body, name="conv_act_bwd", grid=(nc,),
        in_specs=[spec(s, 0), spec(s, nc), spec(3, 0), spec(3, nc), pl.BlockSpec((2, s, tc), lambda j: (0, 0, j)), spec(s, 0)],
        out_specs=[pl.BlockSpec((2, s, tc), lambda j: (0, 0, j)), spec(3, 0), spec(3, 0), spec(1, 0), spec(1, 0)],
        out_shape=[jax.ShapeDtypeStruct((2, s, f), BF16), gw, gw, gb, gb],
        compiler_params=_params("parallel"))(upre, upre, conv_w, conv_w, gv, df)


def _elementwise_tile(r, c, limit):
    if r % 8:
        return r, c
    best = (8, c if c % LANES else LANES)
    for k in (1, 2, 4, 8, 16):
        if k > 1 and c % (LANES * k):
            continue
        tc = c // k
        tr = max(8, min(r, limit // tc) // 8 * 8)
        while r % tr:
            tr -= 8
        if tr * tc <= max(limit, 8 * tc) and tr * tc > best[0] * best[1]:
            best = (tr, tc)
    return best


def adamw(name, w, m, v, parts):
    npart, r, c = parts.shape
    tr, tc = _elementwise_tile(r, c, ADAMW_TILE_ELEMS)
    bc1 = 1.0 - ADAM_B1 ** ADAM_STEP
    bc2 = 1.0 - ADAM_B2 ** ADAM_STEP

    def body(w_ref, m_ref, v_ref, p_ref, g_ref, d_ref, nm_ref, nv_ref):
        g = p_ref[0].astype(F32)
        for k in range(1, npart):
            g = g + p_ref[k].astype(F32)
        m_new = ADAM_B1 * m_ref[...] + (1.0 - ADAM_B1) * g
        v_new = ADAM_B2 * v_ref[...] + (1.0 - ADAM_B2) * (g * g)
        g_ref[...] = g
        nm_ref[...] = m_new
        nv_ref[...] = v_new
        d_ref[...] = -ADAM_LR * ((m_new / bc1) / (jnp.sqrt(v_new / bc2) + ADAM_EPS) + ADAM_WD * w_ref[...])

    deps = _TOKENS.take()
    blk = pl.BlockSpec((tr, tc), lambda i, j: (i, j))
    out = jax.ShapeDtypeStruct((r, c), F32)
    return pl.pallas_call(
        lambda *refs: body(*refs[:4], *refs[4 + len(deps):]), name=name, grid=(r // tr, c // tc),
        in_specs=[blk, blk, blk, pl.BlockSpec((npart, tr, tc), lambda i, j: (0, i, j))] + [pl.BlockSpec(memory_space=pl.ANY)] * len(deps),
        out_specs=[blk, blk, blk, blk], out_shape=[out, out, out, out],
        compiler_params=_params("parallel", "parallel"))(w, m, v, parts, *deps)


def _position():
    return lax.axis_index("x"), lax.axis_index("y"), lax.axis_index("c")


def _index(p):
    return 4 * p[0] + 2 * p[1] + p[2]


def _peer(me, r):
    return (me[0] ^ ((r >> 2) & 1), me[1] ^ ((r >> 1) & 1), me[2] ^ (r & 1))


_ANY = pl.BlockSpec(memory_space=pl.ANY)


_HBM = pl.BlockSpec(memory_space=pltpu.HBM)
_SEM = pl.BlockSpec(memory_space=pltpu.SEMAPHORE)
_EFFECT = pltpu.SideEffectType.DATAFLOW_SIDE_EFFECTING
_TOKEN = jax.ShapeDtypeStruct((8, LANES), F32)
_VM = pl.BlockSpec(memory_space=pltpu.VMEM)
_SIDE = pltpu.CompilerParams(has_side_effects=_EFFECT)


def _hbm(a):
    return pltpu.with_memory_space_constraint(a, pltpu.HBM)


def _like(a):
    return pltpu.HBM(a.shape, a.dtype)


def _dma_sems(n):
    return pltpu.SemaphoreType.DMA((n,))


def _other_chips(x, y):
    return [(1 - x, y), (x, 1 - y), (1 - x, 1 - y)]


COPY_STREAMS = 8


def _row_chunks(src, dst):
    rows = src.shape[0]
    n = COPY_STREAMS
    while n > 1 and rows % (16 * n):
        n //= 2
    r = rows // n
    return [(src.at[pl.ds(i * r, r)], dst.at[pl.ds(i * r, r)]) for i in range(n)]


class _rcopy:
    def __init__(self, src, dst, send_sem, recv_sem, to):
        self.parts = [pltpu.make_async_remote_copy(src_ref=s, dst_ref=d, send_sem=send_sem, recv_sem=recv_sem, device_id=to, device_id_type=MESH)
                      for s, d in _row_chunks(src, dst)]

    def start(self):
        for cp in self.parts:
            cp.start()

    def wait_send(self):
        for cp in self.parts:
            cp.wait_send()

    def wait_recv(self):
        for cp in self.parts:
            cp.wait_recv()


def _afters(after):
    return list(after) if isinstance(after, (list, tuple)) else [after]


def ag_start(name, shards, after):
    n = len(shards)
    lands = [lax.empty((N_DEV,) + a.shape, a.dtype) for a in shards]
    afters = _afters(after)
    na = len(afters)

    def body(*refs):
        ins, lnd, send_sems, recv_sems, token = refs[:n], refs[n:2 * n], refs[2 * n + na], refs[2 * n + na + 1], refs[4 * n + na + 2]
        x, y, c = _position()
        for w in range(n):
            slot = lnd[w].at[_index((x, y, c))]
            for k, to in enumerate([(x, y, 1 - c)] + [(*chip, c) for chip in _other_chips(x, y)]):
                _rcopy(ins[w], slot, send_sems.at[4 * w + k], recv_sems.at[4 * w + k], to).start()
        token[...] = jnp.zeros_like(token)

    out = pl.pallas_call(
        body, name=name, out_shape=(_dma_sems(4 * n), _dma_sems(4 * n)) + tuple(_like(a) for a in shards + lands) + (_TOKEN,),
        in_specs=[_HBM] * (2 * n) + [_ANY] * na, out_specs=(_SEM, _SEM) + (_HBM,) * (2 * n) + (_VM,),
        input_output_aliases={i: 2 + i for i in range(2 * n)}, compiler_params=_SIDE)(*[_hbm(a) for a in shards + lands], *afters)
    _TOKENS.push(out[-1])
    return out[0], out[1], list(out[2:2 + n]), list(out[2 + n:2 + 2 * n])


def _split_rows(ref):
    rows = ref.shape[0]
    h = rows // 32 * 16
    return ref.at[pl.ds(0, h)], ref.at[pl.ds(h, rows - h)]


def relay_start(name, shards, after):
    n = len(shards)
    lands = [lax.empty((N_DEV,) + a.shape, a.dtype) for a in shards]
    afters = _afters(after)
    na = len(afters)

    def body(*refs):
        ins, lnd, send_sems, recv_sems, token = refs[:n], refs[n:2 * n], refs[2 * n + na], refs[2 * n + na + 1], refs[4 * n + na + 2]
        x, y, c = _position()
        for w in range(n):
            slot = lnd[w].at[_index((x, y, c))]
            for k, to in enumerate([(x, y, 1 - c), (1 - x, y, c), (x, 1 - y, c)]):
                _rcopy(ins[w], slot, send_sems.at[3 * w + k], recv_sems.at[3 * w + k], to).start()
        token[...] = jnp.zeros_like(token)

    out = pl.pallas_call(
        body, name=name, out_shape=(_dma_sems(3 * n), _dma_sems(3 * n)) + tuple(_like(a) for a in shards + lands) + (_TOKEN,),
        in_specs=[_HBM] * (2 * n) + [_ANY] * na, out_specs=(_SEM, _SEM) + (_HBM,) * (2 * n) + (_VM,),
        input_output_aliases={i: 2 + i for i in range(2 * n)}, compiler_params=_SIDE)(*[_hbm(a) for a in shards + lands], *afters)
    _TOKENS.push(out[-1])
    return out[0], out[1], list(out[2:2 + n]), list(out[2 + n:2 + 2 * n])


def relay_pass(name, started, after):
    send, recv, shards, lands = started
    n = len(shards)
    afters = _afters(after)
    na = len(afters)

    def body(*refs):
        ins, lnd, send_sems, recv_sems = refs[:n], refs[n:2 * n], refs[2 * n], refs[2 * n + 1]
        fsend, frecv, psend, precv = refs[2 * n + 2 + na:2 * n + 6 + na]
        token = refs[4 * n + 6 + na]
        x, y, c = _position()
        nbrs = [(1 - x, y, c), (x, 1 - y, c)]
        for w in range(n):
            for j, nbr in enumerate(nbrs):
                slot = lnd[w].at[_index(nbr)]
                _rcopy(ins[w], slot, send_sems.at[3 * w + 1 + j], recv_sems.at[3 * w + 1 + j], nbr).wait_recv()
                _rcopy(slot, slot, fsend.at[2 * w + j], frecv.at[2 * w + j], (x, y, 1 - c)).start()
                part = _split_rows(slot)[j]
                _rcopy(part, part, psend.at[2 * w + j], precv.at[2 * w + j], nbrs[1 - j]).start()
        token[...] = jnp.zeros_like(token)

    out = pl.pallas_call(
        body, name=name, out_shape=(_dma_sems(2 * n),) * 4 + tuple(_like(a) for a in shards + lands) + (_TOKEN,),
        in_specs=[_HBM] * (2 * n) + [_SEM, _SEM] + [_ANY] * na, out_specs=(_SEM,) * 4 + (_HBM,) * (2 * n) + (_VM,),
        input_output_aliases={i: 4 + i for i in range(2 * n)}, compiler_params=_SIDE)(*shards, *lands, send, recv, *afters)
    _TOKENS.push(out[-1])
    return (send, recv) + tuple(out[:4]) + (list(out[4:4 + n]), list(out[4 + n:4 + 2 * n]))


def relay_forward(name, passed, after):
    send, recv, fsend, frecv, psend, precv, shards, lands = passed
    n = len(shards)
    afters = _afters(after)
    na = len(afters)

    def body(*refs):
        ins, lnd, precv_r = refs[:n], refs[n:2 * n], refs[2 * n]
        gsend, grecv, token = refs[2 * n + 1 + na], refs[2 * n + 2 + na], refs[4 * n + 3 + na]
        x, y, c = _position()
        for w in range(n):
            slot = lnd[w].at[_index((1 - x, 1 - y, c))]
            for j, part in enumerate(_split_rows(slot)):
                _rcopy(part, part, precv_r.at[2 * w + j], precv_r.at[2 * w + j], (x, y, 1 - c)).wait_recv()
            _rcopy(slot, slot, gsend.at[w], grecv.at[w], (x, y, 1 - c)).start()
        token[...] = jnp.zeros_like(token)

    out = pl.pallas_call(
        body, name=name, out_shape=(_dma_sems(n), _dma_sems(n)) + tuple(_like(a) for a in shards + lands) + (_TOKEN,),
        in_specs=[_HBM] * (2 * n) + [_SEM] + [_ANY] * na, out_specs=(_SEM, _SEM) + (_HBM,) * (2 * n) + (_VM,),
        input_output_aliases={i: 2 + i for i in range(2 * n)}, compiler_params=_SIDE)(*shards, *lands, precv, *afters)
    _TOKENS.push(out[-1])
    return send, recv, fsend, frecv, psend, out[0], out[1], list(out[2:2 + n]), list(out[2 + n:2 + 2 * n])


def relay_wait(name, forwarded, after):
    send, recv, fsend, frecv, psend, gsend, grecv, shards, lands = forwarded
    n = len(shards)

    def body(*refs):
        ins, lnd = refs[:n], refs[n:2 * n]
        send_sems, recv_sems, fsend_r, frecv_r, psend_r, gsend_r, grecv_r = refs[2 * n:2 * n + 7]
        x, y, c = _position()
        sibling = (x, y, 1 - c)
        for w in range(n):
            own = lnd[w].at[_index((x, y, c))]
            _rcopy(ins[w], lnd[w].at[_index(sibling)], send_sems.at[3 * w], recv_sems.at[3 * w], sibling).wait_recv()
            for j, nbr in enumerate([(1 - x, y, 1 - c), (x, 1 - y, 1 - c)]):
                _rcopy(ins[w], lnd[w].at[_index(nbr)], fsend_r.at[2 * w + j], frecv_r.at[2 * w + j], sibling).wait_recv()
            _rcopy(ins[w], lnd[w].at[_index((1 - x, 1 - y, 1 - c))], gsend_r.at[w], grecv_r.at[w], sibling).wait_recv()
            for k in range(3):
                _rcopy(ins[w], own, send_sems.at[3 * w + k], recv_sems.at[3 * w + k], sibling).wait_send()
            for j in range(2):
                _rcopy(ins[w], own, fsend_r.at[2 * w + j], frecv_r.at[2 * w + j], sibling).wait_send()
                part = _split_rows(own)[j]
                _rcopy(part, part, psend_r.at[2 * w + j], psend_r.at[2 * w + j], sibling).wait_send()
            _rcopy(ins[w], own, gsend_r.at[w], grecv_r.at[w], sibling).wait_send()

    out = pl.pallas_call(
        body, name=name, out_shape=tuple(_like(a) for a in shards + lands),
        in_specs=[_HBM] * (2 * n) + [_SEM] * 7 + [_ANY] * len(_afters(after)),
        out_specs=(_HBM,) * (2 * n), input_output_aliases={i: i for i in range(2 * n)},
        compiler_params=_SIDE)(*shards, *lands, send, recv, fsend, frecv, psend, gsend, grecv, *_afters(after))
    return [lax.dynamic_update_index_in_dim(land, shard, _index(_position()), 0) for shard, land in zip(out[:n], out[n:])]


def ag_forward(name, started, after):
    send, recv, shards, lands = started
    n = len(shards)
    afters = list(after) if isinstance(after, (list, tuple)) else [after]
    na = len(afters)

    def body(*refs):
        ins, lnd, send_sems, recv_sems = refs[:n], refs[n:2 * n], refs[2 * n], refs[2 * n + 1]
        fsend, frecv, token = refs[2 * n + 2 + na], refs[2 * n + 3 + na], refs[4 * n + 4 + na]
        x, y, c = _position()
        for w in range(n):
            for j, chip in enumerate(_other_chips(x, y)):
                slot = lnd[w].at[_index((*chip, c))]
                _rcopy(ins[w], slot, send_sems.at[4 * w + 1 + j], recv_sems.at[4 * w + 1 + j], (*chip, c)).wait_recv()
                _rcopy(slot, slot, fsend.at[3 * w + j], frecv.at[3 * w + j], (x, y, 1 - c)).start()
        token[...] = jnp.zeros_like(token)

    out = pl.pallas_call(
        body, name=name, out_shape=(_dma_sems(3 * n), _dma_sems(3 * n)) + tuple(_like(a) for a in shards + lands) + (_TOKEN,),
        in_specs=[_HBM] * (2 * n) + [_SEM, _SEM] + [_ANY] * na, out_specs=(_SEM, _SEM) + (_HBM,) * (2 * n) + (_VM,),
        input_output_aliases={i: 2 + i for i in range(2 * n)}, compiler_params=_SIDE)(*shards, *lands, send, recv, *afters)
    _TOKENS.push(out[-1])
    return send, recv, out[0], out[1], list(out[2:2 + n]), list(out[2 + n:2 + 2 * n])


def ag_wait(name, forwarded, after):
    send, recv, fsend, frecv, shards, lands = forwarded
    n = len(shards)

    def body(*refs):
        ins, lnd, send_sems, recv_sems, fsend_r, frecv_r = refs[:n], refs[n:2 * n], refs[2 * n], refs[2 * n + 1], refs[2 * n + 2], refs[2 * n + 3]
        x, y, c = _position()
        sibling = (x, y, 1 - c)
        for w in range(n):
            own = lnd[w].at[_index((x, y, c))]
            _rcopy(ins[w], lnd[w].at[_index(sibling)], send_sems.at[4 * w], recv_sems.at[4 * w], sibling).wait_recv()
            for j, chip in enumerate(_other_chips(x, y)):
                _rcopy(ins[w], lnd[w].at[_index((*chip, 1 - c))], fsend_r.at[3 * w + j], frecv_r.at[3 * w + j], sibling).wait_recv()
            for k in range(4):
                _rcopy(ins[w], own, send_sems.at[4 * w + k], recv_sems.at[4 * w + k], sibling).wait_send()
            for j in range(3):
                _rcopy(ins[w], own, fsend_r.at[3 * w + j], frecv_r.at[3 * w + j], sibling).wait_send()

    out = pl.pallas_call(
        body, name=name, out_shape=tuple(_like(a) for a in shards + lands),
        in_specs=[_HBM] * (2 * n) + [_SEM] * 4 + [_ANY] * len(_afters(after)),
        out_specs=(_HBM,) * (2 * n), input_output_aliases={i: i for i in range(2 * n)},
        compiler_params=_SIDE)(*shards, *lands, send, recv, fsend, frecv, *_afters(after))
    return [lax.dynamic_update_index_in_dim(land, shard, _index(_position()), 0) for shard, land in zip(out[:n], out[n:])]


def rs_d2d_start(name, grads):
    n = len(grads)
    lands = [lax.empty((4,) + g.shape[1:], g.dtype) for g in grads]

    def body(*refs):
        ins, lnd, send_sems, recv_sems, token = refs[:n], refs[n:2 * n], refs[2 * n], refs[2 * n + 1], refs[4 * n + 2]
        x, y, c = _position()
        for w in range(n):
            for i in range(4):
                _rcopy(ins[w].at[2 * i + 1 - c], lnd[w].at[i], send_sems.at[4 * w + i], recv_sems.at[4 * w + i], (x, y, 1 - c)).start()
        token[...] = jnp.zeros_like(token)

    out = pl.pallas_call(
        body, name=name, out_shape=(_dma_sems(4 * n), _dma_sems(4 * n)) + tuple(_like(a) for a in grads + lands) + (_TOKEN,),
        in_specs=[_HBM] * (2 * n), out_specs=(_SEM, _SEM) + (_HBM,) * (2 * n) + (_VM,),
        input_output_aliases={i: 2 + i for i in range(2 * n)}, compiler_params=_SIDE)(*[_hbm(a) for a in grads + lands])
    _TOKENS.push(out[-1])
    return out[0], out[1], list(out[2:2 + n]), list(out[2 + n:2 + 2 * n])


def rs_d2d_wait(name, started, after):
    send, recv, grads, lands = started
    n = len(grads)

    def body(*refs):
        ins, lnd, send_sems, recv_sems = refs[:n], refs[n:2 * n], refs[2 * n], refs[2 * n + 1]
        x, y, c = _position()
        for w in range(n):
            for i in range(4):
                cp = _rcopy(ins[w].at[2 * i + 1 - c], lnd[w].at[i], send_sems.at[4 * w + i], recv_sems.at[4 * w + i], (x, y, 1 - c))
                cp.wait_send()
                cp.wait_recv()

    out = pl.pallas_call(
        body, name=name, out_shape=tuple(_like(a) for a in grads + lands),
        in_specs=[_HBM] * (2 * n) + [_SEM, _SEM] + [_ANY] * len(_afters(after)),
        out_specs=(_HBM,) * (2 * n), input_output_aliases={i: i for i in range(2 * n)},
        compiler_params=_SIDE)(*grads, *lands, send, recv, *_afters(after))
    return list(out[:n]), list(out[n:])


def pair_sum(name, grad, land, core):
    _, r, c = grad.shape
    tr = r
    if r % 8 == 0:
        tr = max(8, min(r, 4 * ADAMW_TILE_ELEMS // c) // 8 * 8)
        while r % tr:
            tr -= 8

    def body(core_ref, a_ref, b_ref, o_ref):
        o_ref[...] = (a_ref[...].astype(F32) + b_ref[...].astype(F32)).astype(o_ref.dtype)

    return pl.pallas_call(
        body, name=name, out_shape=jax.ShapeDtypeStruct((4, r, c), grad.dtype),
        grid_spec=pltpu.PrefetchScalarGridSpec(
            num_scalar_prefetch=1, grid=(4, r // tr),
            in_specs=[pl.BlockSpec((None, None, tr, c), lambda i, j, core_ref: (i, core_ref[0], j, 0)),
                      pl.BlockSpec((None, tr, c), lambda i, j, core_ref: (i, j, 0))],
            out_specs=pl.BlockSpec((None, tr, c), lambda i, j, core_ref: (i, j, 0))),
        compiler_params=_params("parallel", "parallel"))(core, grad.reshape(4, 2, r, c), land)


def rs_ici_start(name, sums):
    n = len(sums)
    lands = [lax.empty(a.shape, a.dtype) for a in sums]

    def body(*refs):
        ins, lnd, send_sems, recv_sems, token = refs[:n], refs[n:2 * n], refs[2 * n], refs[2 * n + 1], refs[4 * n + 2]
        x, y, c = _position()
        chip = 2 * x + y
        for w in range(n):
            for j, other in enumerate(_other_chips(x, y)):
                _rcopy(ins[w].at[2 * other[0] + other[1]], lnd[w].at[chip], send_sems.at[3 * w + j], recv_sems.at[3 * w + j], (*other, c)).start()
        token[...] = jnp.zeros_like(token)

    out = pl.pallas_call(
        body, name=name, out_shape=(_dma_sems(3 * n), _dma_sems(3 * n)) + tuple(_like(a) for a in sums + lands) + (_TOKEN,),
        in_specs=[_HBM] * (2 * n), out_specs=(_SEM, _SEM) + (_HBM,) * (2 * n) + (_VM,),
        input_output_aliases={i: 2 + i for i in range(2 * n)}, compiler_params=_SIDE)(*[_hbm(a) for a in sums + lands])
    _TOKENS.push(out[-1])
    return out[0], out[1], list(out[2:2 + n]), list(out[2 + n:2 + 2 * n])


def rs_ici_wait(name, started, after):
    send, recv, sums, lands = started
    n = len(sums)

    def body(*refs):
        ins, lnd, send_sems, recv_sems = refs[:n], refs[n:2 * n], refs[2 * n], refs[2 * n + 1]
        x, y, c = _position()
        for w in range(n):
            for j, other in enumerate(_other_chips(x, y)):
                cp = _rcopy(ins[w].at[2 * other[0] + other[1]], lnd[w].at[2 * other[0] + other[1]], send_sems.at[3 * w + j], recv_sems.at[3 * w + j], (*other, c))
                cp.wait_send()
                cp.wait_recv()

    out = pl.pallas_call(
        body, name=name, out_shape=tuple(_like(a) for a in sums + lands), in_specs=[_HBM] * (2 * n) + [_SEM, _SEM, _ANY],
        out_specs=(_HBM,) * (2 * n), input_output_aliases={i: i for i in range(2 * n)}, compiler_params=_SIDE)(*sums, *lands, send, recv, after)
    chip = 2 * lax.axis_index("x") + lax.axis_index("y")
    return [lax.dynamic_update_index_in_dim(land, lax.dynamic_index_in_dim(s, chip, 0, keepdims=False), chip, 0)
            for s, land in zip(out[:n], out[n:])]


def ada_fwd(c, w_ada, b_ada3, conv_w, after):
    d, cs = w_ada.shape

    def body(c_ref, w_ref, b_ref, cw_ref, after_ref, mod_ref, sc_ref, cwa_ref, part_ref, send_sems, recv_sems):
        me = _position()
        my = _index(me)
        cv = c_ref[...]
        sc_ref[my] = cv * _sigmoid(cv)
        cwa_ref[my] = cw_ref[...]
        gather = []
        for r in range(1, N_DEV):
            for k, ref in enumerate((sc_ref, cwa_ref)):
                cp = pltpu.make_async_remote_copy(src_ref=ref.at[my], dst_ref=ref.at[my], send_sem=send_sems.at[14 * k + r - 1],
                                                  recv_sem=recv_sems.at[14 * k + r - 1], device_id=_peer(me, r), device_id_type=MESH)
                cp.start()
                gather.append(cp)
        for cp in gather:
            cp.wait()
        sc_all = jnp.concatenate([sc_ref[k] for k in range(N_DEV)], axis=0).astype(BF16)
        part = jnp.dot(sc_all, w_ref[...].astype(BF16), preferred_element_type=F32)
        for k in range(N_DEV):
            part_ref[k] = part[k:k + 1, :]
        scatter = []
        for r in range(1, N_DEV):
            peer = _peer(me, r)
            cp = pltpu.make_async_remote_copy(src_ref=part_ref.at[_index(peer)], dst_ref=mod_ref.at[my], send_sem=send_sems.at[6 + r],
                                              recv_sem=recv_sems.at[6 + r], device_id=peer, device_id_type=MESH)
            cp.start()
            scatter.append(cp)
        mod_ref[my] = part_ref[my]
        for cp in scatter:
            cp.wait()
        mod_ref[...] = mod_ref[...] + b_ref[...]

    vm = pl.BlockSpec(memory_space=pltpu.VMEM)
    return pl.pallas_call(
        body, name="ada_fwd",
        out_shape=[jax.ShapeDtypeStruct((N_DEV, 1, cs), F32), jax.ShapeDtypeStruct((N_DEV, 1, d), F32),
                   jax.ShapeDtypeStruct((N_DEV,) + conv_w.shape, F32)],
        in_specs=[vm, vm, vm, vm, _ANY], out_specs=[vm, vm, vm],
        scratch_shapes=[pltpu.VMEM((N_DEV, 1, cs), F32), pltpu.SemaphoreType.DMA((21,)), pltpu.SemaphoreType.DMA((21,))],
        compiler_params=pltpu.CompilerParams(vmem_limit_bytes=VMEM_LIMIT_BYTES))(c, w_ada, b_ada3, conv_w, after)


def ada_bwd_w(sc_all, dmod_cols):
    _, d = sc_all.shape
    cs = dmod_cols.shape[1]
    tr = _tile(d, ROW_TILE)

    def body(sc_ref, dm_ref, o_ref):
        dm = dm_ref[...].astype(BF16)
        o_ref[...] = lax.dot_general(sc_ref[...].astype(BF16), dm, (((0,), (0,)), ((), ())), preferred_element_type=F32)

    return pl.pallas_call(body, name="ada_bwd_w", grid=(d // tr,),
                          in_specs=[pl.BlockSpec((N_DEV, tr), lambda i: (0, i)), _full((N_DEV, cs))],
                          out_specs=pl.BlockSpec((None, tr, cs), lambda i: (0, i, 0)),
                          out_shape=jax.ShapeDtypeStruct((1, d, cs), F32), compiler_params=_params("parallel"))(sc_all, dmod_cols)


def _round_up(n, m):
    return (n + m - 1) // m * m


def kernel(x, c, positions, w_ada, b_ada, pre_norm1_g, w_in, gm_ln_g, gm_ln_b, gm_w_s, gm_b_s, w_branch_a, q_norm_g, w_uq, kv_norm_g, w_ukv, w_branch_b, w_out, post_norm1_g, pre_norm2_g, w_up, conv_w, conv_b, w_down, post_norm2_g, loss_target, m_w_ada, m_b_ada, m_pre_norm1_g, m_w_in, m_gm_ln_g, m_gm_ln_b, m_gm_w_s, m_gm_b_s, m_w_branch_a, m_q_norm_g, m_w_uq, m_kv_norm_g, m_w_ukv, m_w_branch_b, m_w_out, m_post_norm1_g, m_pre_norm2_g, m_w_up, m_conv_w, m_conv_b, m_w_down, m_post_norm2_g, v_w_ada, v_b_ada, v_pre_norm1_g, v_w_in, v_gm_ln_g, v_gm_ln_b, v_gm_w_s, v_gm_b_s, v_w_branch_a, v_q_norm_g, v_w_uq, v_kv_norm_g, v_w_ukv, v_w_branch_b, v_w_out, v_post_norm1_g, v_pre_norm2_g, v_w_up, v_conv_w, v_conv_b, v_w_down, v_post_norm2_g):
    weights = dict(w_ada=w_ada, b_ada=b_ada, pre_norm1_g=pre_norm1_g, w_in=w_in, gm_ln_g=gm_ln_g, gm_ln_b=gm_ln_b, gm_w_s=gm_w_s,
                   gm_b_s=gm_b_s, w_branch_a=w_branch_a, q_norm_g=q_norm_g, w_uq=w_uq, kv_norm_g=kv_norm_g, w_ukv=w_ukv,
                   w_branch_b=w_branch_b, w_out=w_out, post_norm1_g=post_norm1_g, pre_norm2_g=pre_norm2_g, w_up=w_up, conv_w=conv_w,
                   conv_b=conv_b, w_down=w_down, post_norm2_g=post_norm2_g)
    mom1 = dict(w_ada=m_w_ada, b_ada=m_b_ada, pre_norm1_g=m_pre_norm1_g, w_in=m_w_in, gm_ln_g=m_gm_ln_g, gm_ln_b=m_gm_ln_b,
                gm_w_s=m_gm_w_s, gm_b_s=m_gm_b_s, w_branch_a=m_w_branch_a, q_norm_g=m_q_norm_g, w_uq=m_w_uq, kv_norm_g=m_kv_norm_g,
                w_ukv=m_w_ukv, w_branch_b=m_w_branch_b, w_out=m_w_out, post_norm1_g=m_post_norm1_g, pre_norm2_g=m_pre_norm2_g,
                w_up=m_w_up, conv_w=m_conv_w, conv_b=m_conv_b, w_down=m_w_down, post_norm2_g=m_post_norm2_g)
    mom2 = dict(w_ada=v_w_ada, b_ada=v_b_ada, pre_norm1_g=v_pre_norm1_g, w_in=v_w_in, gm_ln_g=v_gm_ln_g, gm_ln_b=v_gm_ln_b,
                gm_w_s=v_gm_w_s, gm_b_s=v_gm_b_s, w_branch_a=v_w_branch_a, q_norm_g=v_q_norm_g, w_uq=v_w_uq, kv_norm_g=v_kv_norm_g,
                w_ukv=v_w_ukv, w_branch_b=v_w_branch_b, w_out=v_w_out, post_norm1_g=v_post_norm1_g, pre_norm2_g=v_pre_norm2_g,
                w_up=v_w_up, conv_w=v_conv_w, conv_b=v_conv_b, w_down=v_w_down, post_norm2_g=v_post_norm2_g)
    order = list(weights)
    _TOKENS.clear()

    s, d = x.shape[1], x.shape[2]
    gmw = gm_ln_g.shape[0]
    groups = gmw // CHUNK
    ql, kvl = q_norm_g.shape[0], kv_norm_g.shape[0]
    f2 = conv_b.shape[0]
    in_cols = w_in.shape[1] * N_DEV
    o_q, o_kv, o_ga, o_gb, o_kpe = 2 * gmw, 2 * gmw + ql, 2 * gmw + ql + kvl, 2 * gmw + ql + kvl + d, 2 * gmw + ql + kvl + 2 * d
    zp = _round_up(o_kpe + LANES, Z_PAD)
    src_kpe = 2 * gmw + ql + kvl
    assert src_kpe + QK_ROPE + 2 * d == in_cols
    my = 4 * lax.axis_index("x") + 2 * lax.axis_index("y") + lax.axis_index("c")

    x2, tgt = x[0], loss_target[0]
    row = lambda a: a.reshape(1, -1)

    big = ["w_in", "w_branch_a", "w_uq", "w_ukv", "w_branch_b", "w_out", "w_up", "w_down"]
    sh = {k: weights[k].astype(BF16) for k in big[1:]}
    mix = ["w_branch_a", "w_uq", "w_ukv", "w_branch_b", "w_out"]
    w_in_t = w_in.T.astype(BF16)

    mod8, sc_all3, g_cw = ada_fwd(c, w_ada, b_ada.reshape(N_DEV, 1, -1), conv_w, w_in_t)
    ag_in = relay_start("relay_start_in", [w_in_t], mod8)
    mod = mod8.reshape(N_MOD, d)
    shift1, scale1, gate1, shift2, scale2, gate2 = (mod[i:i + 1] for i in range(N_MOD))
    sc_all = sc_all3.reshape(N_DEV, d)
    h1 = norm_mod_fwd("pre1_fwd", x2, row(pre_norm1_g), scale1, shift1)

    inv = ROPE_THETA ** (-jnp.arange(0, QK_ROPE, 2, dtype=F32) / QK_ROPE)
    ang = positions[0].astype(F32)[:, None] * inv
    cos4 = jnp.tile(jnp.cos(ang), (1, 4))
    sin4 = jnp.tile(jnp.concatenate([-jnp.sin(ang), jnp.sin(ang)], axis=1), (1, 2))

    wm = (gm_w_s * jnp.tril(jnp.ones((CHUNK, CHUNK), F32))).astype(BF16)
    bs3 = gm_b_s.reshape(groups, CHUNK, 1)
    ln_g, ln_b = row(gm_ln_g), row(gm_ln_b)

    small_names = ["pre_norm1_g", "gm_ln_g", "gm_ln_b", "gm_b_s", "q_norm_g", "kv_norm_g", "post_norm1_g", "pre_norm2_g", "conv_b",
                   "post_norm2_g", "gm_w_s", "b_ada"]
    n_small_early = sum(weights[k].size for k in small_names)
    n_pack_early = _round_up(n_small_early + 3 * f2, PACK_ALIGN)

    def pack(src):
        return jnp.concatenate([src[k].reshape(-1) for k in small_names] + [jnp.zeros((n_pack_early - n_small_early,), F32)]).reshape(-1, LANES)

    packed_state = [pack(weights), pack(mom1), pack(mom2)]

    early = [h1, cos4, sin4, wm] + [sh[k] for k in big[1:]] + packed_state
    ag_in = relay_pass("relay_pass_in", ag_in, early)
    ag_in = relay_forward("relay_forward_in", ag_in, _TOKENS.pending[-1])
    ag_mix = ag_start("ag_start_mix", [sh[k] for k in mix], _TOKENS.pending[-1])
    (g_in,) = relay_wait("relay_wait_in", ag_in, [h1, _TOKENS.pending[-1]])
    cs_in = w_in.shape[1]

    def w_in_rows(lo, hi):
        return [g_in[k, max(lo - k * cs_in, 0):min(hi - k * cs_in, cs_in)] for k in range(N_DEV) if lo < (k + 1) * cs_in and hi > k * cs_in]

    w_in_p = jnp.concatenate(w_in_rows(0, src_kpe) + w_in_rows(src_kpe + QK_ROPE, in_cols) + w_in_rows(src_kpe, src_kpe + QK_ROPE)
                             + [jnp.zeros((zp - in_cols, d), BF16)], axis=0)

    z = mm_nt("z_proj", h1, w_in_p, ACT)
    ag_mix = ag_forward("ag_forward_mix", ag_mix, z)
    ag_up = ag_start("ag_start_up", [sh["w_up"]], _TOKENS.pending[-1])
    a = gmlp_fwd(z, gmw, ln_g, ln_b, wm, bs3)
    g_a, g_uq, g_ukv, g_b, g_out = ag_wait("ag_wait_mix", ag_mix, [a, _TOKENS.pending[-1]])
    w_a_f, w_b_f, w_out_f = g_a.reshape(-1, d), g_b.reshape(-1, d), g_out.reshape(-1, d)
    w_uq_f = g_uq.transpose(1, 0, 2).reshape(ql, HEADS, QK_NOPE + QK_ROPE)
    w_uq_n = w_uq_f[:, :, :QK_NOPE].reshape(ql, HEADS * QK_NOPE)
    w_uq_r = w_uq_f[:, :, QK_NOPE:].reshape(ql, HEADS * QK_ROPE)
    y_a = mm_nn("branch_a", a, w_a_f, ACT)
    qln = rms_fwd_cols("q_norm", z, o_q, ql, row(q_norm_g))
    kvn = rms_fwd_cols("kv_norm", z, o_kv, kvl, row(kv_norm_g))
    qn = mm_nn("q_nope", qln, w_uq_n, BF16)
    qp = mm_nn("q_rope", qln, w_uq_r, F32)
    kv = mm_nn_b3("kv_up", kvn, g_ukv, BF16)
    kpr = rope_k(z, o_kpe, cos4, sin4)
    o, qpr, lse = attn_fwd(qn, qp, kv, kpr, cos4, sin4)
    ag_up = ag_forward("ag_forward_up", ag_up, o)
    ag_down = ag_start("ag_start_down", [sh["w_down"]], _TOKENS.pending[-1])
    y_b = mm_nn("branch_b", o, w_b_f, ACT)
    merged = merge_fwd(z, o_ga, o_gb, y_a, y_b)
    y1 = mm_nn("out_proj", merged, w_out_f, ACT)
    x1, h2 = post1_pre2_fwd(x2, y1, gate1, row(post_norm1_g), row(pre_norm2_g), scale2, shift2)
    (g_up,) = ag_wait("ag_wait_up", ag_up, h2)
    upre = mm_nn_b3("up_proj", h2, g_up, ACT)
    ag_down = ag_forward("ag_forward_down", ag_down, upre)
    cw = g_cw.transpose(1, 0, 2).reshape(3, f2)
    cb = row(conv_b)
    f, gv = conv_act_fwd(upre, cw, cb)
    w_down_f = ag_wait("ag_wait_down", ag_down, f)[0].reshape(-1, d)
    ffn = mm_nn("down_proj", f, w_down_f, ACT)
    loss_acc, dout, dffn, acc2 = post2_loss_bwd(x1, ffn, tgt, gate2, row(post_norm2_g))
    loss = lax.psum(loss_acc[0, 0], ("x", "y", "c"))
    _TOKENS.push(jnp.broadcast_to(loss, (8, LANES)))

    blocks = lambda g: g.reshape(N_DEV, g.shape[0] // N_DEV, g.shape[1])
    core = lax.axis_index("c").astype(jnp.int32).reshape(1)
    rs = {}

    def rs_begin(key, grads):
        rs[key] = rs_d2d_start("rs_d2d_start_" + key, grads)

    def rs_middle(key, after):
        grads, lands = rs_d2d_wait("rs_d2d_wait_" + key, rs[key], after)
        sums = [pair_sum("pair_sum_%s_%d" % (key, i), g, l, core) for i, (g, l) in enumerate(zip(grads, lands))]
        rs[key] = rs_ici_start("rs_ici_start_" + key, sums)

    gw_down = mm_tn("g_w_down", f, dffn, BF16)
    rs_begin("down", [blocks(gw_down)])
    df = mm_nt("d_f", dffn, w_down_f, ACT)
    rs_middle("down", df)
    dupre, gcw_g, gcw_v, gcb_g, gcb_v = conv_act_bwd(upre, cw, gv, df)
    gw_up3 = mm_tn_h3("g_w_up", h2, dupre, N_DEV, BF16)
    rs_begin("up", [gw_up3])
    dh2 = mm_nt_h3("d_h2", dupre, g_up, ACT)
    rs_middle("up", dh2)
    dx1, dy1, acc_mid = mid_bwd(dh2, dout, x1, y1, row(pre_norm2_g), scale2, gate1, row(post_norm1_g))
    gw_out = mm_tn("g_w_out", merged, dy1, BF16)
    dmerged = mm_nt("d_merged", dy1, w_out_f, ACT)
    dya, dyb, dga, dgb = merge_bwd(z, o_ga, o_gb, y_a, y_b, dmerged)
    gw_a = mm_tn("g_w_a", a, dya, BF16)
    gw_b = mm_tn("g_w_b", o, dyb, BF16)
    rs_begin("mid", [blocks(gw_out), blocks(gw_a), blocks(gw_b)])
    da = mm_nt("d_a", dya, w_a_f, ACT)
    do = mm_nt("d_o", dyb, w_b_f, ACT)
    rs_middle("mid", do)
    duv, g_ws, g_bs3, acc_gm = gmlp_bwd(z, gmw, da, ln_g, ln_b, wm, bs3)
    dqn, dqp, dkv, dkp = attn_bwd(qn, qpr, kv, kpr, o, do, lse, cos4, sin4)
    dkpe = kpe_bwd(dkp, cos4, sin4, zp - o_kpe)
    dq_cat = jnp.concatenate([dqn, dqp], axis=1)
    w_uq_cat = jnp.concatenate([w_uq_n, w_uq_r], axis=1)
    dqln = mm_nt("d_qln", dq_cat, w_uq_cat, ACT)
    dq_lat, g_qnorm = rms_bwd_cols("q_norm_bwd", dqln, z, o_q, ql, row(q_norm_g))
    dkvn = mm_nt_b3("d_kvn", dkv, g_ukv, ACT)
    dkv_lat, g_kvnorm = rms_bwd_cols("kv_norm_bwd", dkvn, z, o_kv, kvl, row(kv_norm_g))
    dz = jnp.concatenate([duv, dq_lat, dkv_lat, dga, dgb, dkpe], axis=1)
    gw_in_p = mm_tn("g_w_in", dz, h1, BF16)

    def gw_in_rows(lo, hi):
        pieces = []
        for a, b, shift in ((0, src_kpe, 0), (src_kpe, src_kpe + QK_ROPE, o_kpe - src_kpe), (src_kpe + QK_ROPE, in_cols, -QK_ROPE)):
            if lo < b and hi > a:
                pieces.append(gw_in_p[max(lo, a) + shift:min(hi, b) + shift])
        return pieces[0] if len(pieces) == 1 else jnp.concatenate(pieces, axis=0)

    rs_begin("in", [jnp.stack([gw_in_rows(k * cs_in, (k + 1) * cs_in) for k in range(N_DEV)])])
    dh1 = mm_nn("d_h1", dz, w_in_p, ACT)
    grad_x, acc1 = pre1_bwd(dh1, dx1, x2, row(pre_norm1_g), scale1)

    dmod = jnp.concatenate([acc1[0], acc1[1], acc_mid[3], acc_mid[0], acc_mid[1], acc2[0]])
    small = [("pre_norm1_g", acc1[2]), ("gm_ln_g", acc_gm[0]), ("gm_ln_b", acc_gm[1]), ("gm_b_s", g_bs3.reshape(-1)),
             ("q_norm_g", g_qnorm[0]), ("kv_norm_g", g_kvnorm[0]), ("post_norm1_g", acc_mid[4]), ("pre_norm2_g", acc_mid[2]),
             ("conv_b", jnp.concatenate([gcb_g[0], gcb_v[0]])), ("post_norm2_g", acc2[1]), ("gm_w_s", g_ws.reshape(-1)),
             ("b_ada", dmod)]
    n_small = sum(v.shape[0] for _, v in small)
    n_cw = 3 * f2
    n_pack = _round_up(n_small + n_cw, PACK_ALIGN)
    tail = jnp.zeros((n_pack - n_small - n_cw,), F32)
    packed = jnp.concatenate([v for _, v in small] + [jnp.concatenate([gcw_g, gcw_v], axis=1).reshape(-1), tail])
    ag_small = ag_start("ag_start_small", [packed.reshape(-1, LANES)], packed)
    rs_middle("in", [packed, _TOKENS.pending[-1]])

    gw_uq_cat = mm_tn("g_w_uq", qln, dq_cat, BF16)
    gw_uq_f = jnp.concatenate([gw_uq_cat[:, :HEADS * QK_NOPE].reshape(ql, HEADS, QK_NOPE),
                               gw_uq_cat[:, HEADS * QK_NOPE:].reshape(ql, HEADS, QK_ROPE)], axis=2)
    gw_uq3 = gw_uq_f.reshape(ql, N_DEV, -1).transpose(1, 0, 2)
    gw_ukv3 = mm_tn_o3("g_w_ukv", kvn, dkv, N_DEV, BF16)
    rs_begin("mla", [gw_uq3, gw_ukv3])

    res = {}
    last = packed
    for key, names in (("down", ["w_down"]), ("up", ["w_up"]), ("mid", ["w_out", "w_branch_a", "w_branch_b"])):
        parts = rs_ici_wait("rs_ici_wait_" + key, rs[key], last)
        for k, p in zip(names, parts):
            res[k] = adamw("adamw_" + k, weights[k], mom1[k], mom2[k], p)
            last = res[k][0]
        if key == "down":
            rs_middle("mla", last)

    assert [k for k, _ in small] == small_names and n_small == n_small_early
    (gathered,) = ag_wait("ag_wait_small", ag_forward("ag_forward_small", ag_small, last), last)
    sm = [t.reshape(-1) for t in adamw("adamw_small", *packed_state, gathered)]
    off = 0
    for k, v in small:
        res[k] = tuple(t[off:off + v.shape[0]].reshape(weights[k].shape) for t in sm)
        off += v.shape[0]

    cs_cw = conv_w.shape[1]
    g_cw_full = sm[0][n_small:n_small + n_cw].reshape(3, f2)
    g_cw_mine = lax.dynamic_slice(g_cw_full, (0, my * cs_cw), (3, cs_cw))
    res["conv_w"] = adamw("adamw_conv_w", conv_w, mom1["conv_w"], mom2["conv_w"], g_cw_mine[None])

    cs_ada = w_ada.shape[1]
    off_b = n_small - N_MOD * d
    dmod_all = gathered.reshape(N_DEV, -1)[:, off_b:off_b + N_MOD * d]
    dmod_cols = lax.dynamic_slice(dmod_all, (0, my * cs_ada), (N_DEV, cs_ada))
    res["w_ada"] = adamw("adamw_w_ada", w_ada, mom1["w_ada"], mom2["w_ada"], ada_bwd_w(sc_all, dmod_cols))

    (p_in,) = rs_ici_wait("rs_ici_wait_in", rs["in"], res["w_ada"][0])
    w_in_res = adamw("adamw_w_in", w_in.T, mom1["w_in"].T, mom2["w_in"].T, p_in)
    res["w_in"] = tuple(t.T for t in w_in_res)
    for k, p in zip(["w_uq", "w_ukv"], rs_ici_wait("rs_ici_wait_mla", rs["mla"], w_in_res[0])):
        res[k] = adamw("adamw_" + k, weights[k], mom1[k], mom2[k], p)

    _TOKENS.clear()
    outs = [loss, grad_x[None]]
    for i in range(4):
        outs += [res[k][i] for k in order]
    return tuple(outs)
```

```python
import jax
import jax.numpy as jnp
from jax import lax
from jax.experimental import pallas as pl
from jax.experimental.pallas import tpu as pltpu

F32 = jnp.float32
BF16 = jnp.bfloat16
ACT = BF16

N_DEV = 8
HEADS = 16
QK_NOPE = 128
QK_ROPE = 64
V_HEAD = 128
CHUNK = 128
ROPE_THETA = 10000.0
EPS = 1e-6
N_MOD = 6
ADAM_LR, ADAM_B1, ADAM_B2, ADAM_EPS, ADAM_WD, ADAM_STEP = 0.001, 0.9, 0.999, 1e-08, 0.01, 10

LANES = 128
VMEM_LIMIT_BYTES = 48 * 2 ** 20
ROW_TILE = 256
COL_TILE = 256
ATT_TILE = 512
Z_PAD = 512
ADAMW_TILE_ELEMS = 1 << 18
PACK_ALIGN = 8 * LANES
MESH = pl.DeviceIdType.MESH


def _params(*sem):
    return pltpu.CompilerParams(dimension_semantics=sem if sem else None, vmem_limit_bytes=VMEM_LIMIT_BYTES)


def _tile(dim, target):
    t = (min(dim, target) // LANES) * LANES
    while t >= LANES:
        if dim % t == 0:
            return t
        t -= LANES
    return dim


def _full(shape):
    nd = len(shape)
    return pl.BlockSpec(shape, lambda *_: (0,) * nd)


class _Tokens:
    KEEP = 2

    def __init__(self):
        self.pending = []

    def push(self, token):
        self.pending = (self.pending + [token])[-self.KEEP:]

    def take(self):
        return list(self.pending)

    def clear(self):
        self.pending = []


_TOKENS = _Tokens()


def _matmul(name, a, b, *, grid, a_spec, b_spec, o_spec, out_shape, contract, acc_shape, split=1):
    nk = grid[2]
    deps = _TOKENS.take()

    def product(a_ref, b_ref):
        if len(b_ref.shape) == 2:
            return lax.dot_general(a_ref[...].astype(BF16), b_ref[...].astype(BF16), (contract, ((), ())), preferred_element_type=F32)
        cs = b_ref.shape[2]
        return sum(lax.dot_general(a_ref[:, s * cs:(s + 1) * cs].astype(BF16), b_ref[s].astype(BF16), (contract, ((), ())),
                                   preferred_element_type=F32) for s in range(split))

    def body_one_step(a_ref, b_ref, *rest):
        o_ref = rest[len(deps)]
        o_ref[...] = product(a_ref, b_ref).astype(o_ref.dtype)

    def body(a_ref, b_ref, *rest):
        o_ref, acc_ref = rest[len(deps):]
        k = pl.program_id(2)

        @pl.when(k == 0)
        def _():
            acc_ref[...] = jnp.zeros_like(acc_ref)

        acc_ref[...] += product(a_ref, b_ref)

        @pl.when(k == nk - 1)
        def _():
            o_ref[...] = acc_ref[...].astype(o_ref.dtype)

    return pl.pallas_call(
        body_one_step if nk == 1 else body, name=name, grid=grid,
        in_specs=[a_spec, b_spec] + [pl.BlockSpec(memory_space=pl.ANY)] * len(deps),
        out_specs=o_spec, out_shape=out_shape, scratch_shapes=[] if nk == 1 else [pltpu.VMEM(acc_shape, F32)],
        compiler_params=_params("parallel", "parallel", "arbitrary"))(a, b, *deps)


T_OUT, T_OUT_WIDE, TK = 1024, 1408, 2816


def _out_tile(dim):
    return T_OUT_WIDE if dim % T_OUT_WIDE == 0 else _tile(dim, T_OUT)


def _tk(a, b):
    return TK if a.dtype == BF16 and b.dtype == BF16 else TK // 2


def mm_nn(name, a, b, dtype):
    (m, k), n = a.shape, b.shape[1]
    tm, tn, tk = _out_tile(m), _out_tile(n), _tile(k, _tk(a, b))
    return _matmul(name, a, b, grid=(m // tm, n // tn, k // tk),
                   a_spec=pl.BlockSpec((tm, tk), lambda i, j, kk: (i, kk)),
                   b_spec=pl.BlockSpec((tk, tn), lambda i, j, kk: (kk, j)),
                   o_spec=pl.BlockSpec((tm, tn), lambda i, j, kk: (i, j)),
                   out_shape=jax.ShapeDtypeStruct((m, n), dtype), contract=((1,), (0,)), acc_shape=(tm, tn))


def mm_nn_b3(name, a, b3, dtype):
    (m, k), (nj, _, cs) = a.shape, b3.shape
    tm, tk = _out_tile(m), _tile(k, _tk(a, b3))
    return _matmul(name, a, b3, grid=(m // tm, nj, k // tk),
                   a_spec=pl.BlockSpec((tm, tk), lambda i, j, kk: (i, kk)),
                   b_spec=pl.BlockSpec((None, tk, cs), lambda i, j, kk: (j, kk, 0)),
                   o_spec=pl.BlockSpec((tm, cs), lambda i, j, kk: (i, j)),
                   out_shape=jax.ShapeDtypeStruct((m, nj * cs), dtype), contract=((1,), (0,)), acc_shape=(tm, cs))


def mm_nt(name, a, b, dtype):
    (m, k), n = a.shape, b.shape[0]
    tm, tn, tk = _out_tile(m), _out_tile(n), _tile(k, _tk(a, b))
    return _matmul(name, a, b, grid=(m // tm, n // tn, k // tk),
                   a_spec=pl.BlockSpec((tm, tk), lambda i, j, kk: (i, kk)),
                   b_spec=pl.BlockSpec((tn, tk), lambda i, j, kk: (j, kk)),
                   o_spec=pl.BlockSpec((tm, tn), lambda i, j, kk: (i, j)),
                   out_shape=jax.ShapeDtypeStruct((m, n), dtype), contract=((1,), (1,)), acc_shape=(tm, tn))


def mm_nt_b3(name, a, b3, dtype):
    m, (nj, n, cs) = a.shape[0], b3.shape
    tm, tn = _out_tile(m), _out_tile(n)
    return _matmul(name, a, b3, grid=(m // tm, n // tn, nj),
                   a_spec=pl.BlockSpec((tm, cs), lambda i, j, kk: (i, kk)),
                   b_spec=pl.BlockSpec((None, tn, cs), lambda i, j, kk: (kk, j, 0)),
                   o_spec=pl.BlockSpec((tm, tn), lambda i, j, kk: (i, j)),
                   out_shape=jax.ShapeDtypeStruct((m, n), dtype), contract=((1,), (1,)), acc_shape=(tm, tn))


def mm_nt_h3(name, a3, b3, dtype):
    (_, m, _), (nj, n, cs) = a3.shape, b3.shape
    tm, tn, hj = _out_tile(m), _out_tile(n), nj // 2
    pair = 2 if hj % 2 == 0 else 1
    return _matmul(name, a3, b3.reshape(nj // pair, pair, n, cs), grid=(m // tm, n // tn, nj // pair),
                   a_spec=pl.BlockSpec((None, tm, pair * cs), lambda i, j, kk: (kk // (hj // pair), i, kk % (hj // pair))),
                   b_spec=pl.BlockSpec((None, pair, tn, cs), lambda i, j, kk: (kk, 0, j, 0)),
                   o_spec=pl.BlockSpec((tm, tn), lambda i, j, kk: (i, j)),
                   out_shape=jax.ShapeDtypeStruct((m, n), dtype), contract=((1,), (1,)), acc_shape=(tm, tn), split=pair)


def mm_tn_h3(name, a, b3, nj, dtype):
    (k, m), half = a.shape, b3.shape[2]
    hj = nj // 2
    cs = half // hj
    tm, tk = _out_tile(m), _tile(k, _tk(a, b3))
    return _matmul(name, a, b3, grid=(m // tm, nj, k // tk),
                   a_spec=pl.BlockSpec((tk, tm), lambda i, j, kk: (kk, i)),
                   b_spec=pl.BlockSpec((None, tk, cs), lambda i, j, kk: (j // hj, kk, j % hj)),
                   o_spec=pl.BlockSpec((None, tm, cs), lambda i, j, kk: (j, i, 0)),
                   out_shape=jax.ShapeDtypeStruct((nj, m, cs), dtype), contract=((0,), (0,)), acc_shape=(tm, cs))


def mm_tn(name, a, b, dtype):
    (k, m), n = a.shape, b.shape[1]
    tm, tn, tk = _out_tile(m), _out_tile(n), _tile(k, _tk(a, b))
    return _matmul(name, a, b, grid=(m // tm, n // tn, k // tk),
                   a_spec=pl.BlockSpec((tk, tm), lambda i, j, kk: (kk, i)),
                   b_spec=pl.BlockSpec((tk, tn), lambda i, j, kk: (kk, j)),
                   o_spec=pl.BlockSpec((tm, tn), lambda i, j, kk: (i, j)),
                   out_shape=jax.ShapeDtypeStruct((m, n), dtype), contract=((0,), (0,)), acc_shape=(tm, tn))


def mm_tn_o3(name, a, b, nj, dtype):
    (k, m), n = a.shape, b.shape[1]
    cs = n // nj
    tm, tk = _out_tile(m), _tile(k, _tk(a, b))
    return _matmul(name, a, b, grid=(m // tm, nj, k // tk),
                   a_spec=pl.BlockSpec((tk, tm), lambda i, j, kk: (kk, i)),
                   b_spec=pl.BlockSpec((tk, cs), lambda i, j, kk: (kk, j)),
                   o_spec=pl.BlockSpec((None, tm, cs), lambda i, j, kk: (j, i, 0)),
                   out_shape=jax.ShapeDtypeStruct((nj, m, cs), dtype), contract=((0,), (0,)), acc_shape=(tm, cs))


_GELU_C = 0.7978845608028654
_GELU_A = 0.044715


def _f32(ref):
    return ref[...].astype(F32)


def _gelu(x):
    x = x.astype(F32)
    return 0.5 * x * (1.0 + jnp.tanh(_GELU_C * (x + _GELU_A * x * x * x)))


def _gelu_and_grad(x):
    x = x.astype(F32)
    t = jnp.tanh(_GELU_C * (x + _GELU_A * x * x * x))
    y = 0.5 * x * (1.0 + t)
    dy = 0.5 * (1.0 + t) + 0.5 * x * (1.0 - t * t) * (_GELU_C * (1.0 + 3.0 * _GELU_A * x * x))
    return y, dy


def _sigmoid(x):
    return 0.5 * jnp.tanh(0.5 * x.astype(F32)) + 0.5


def _rms_stats(x):
    x = x.astype(F32)
    inv = lax.rsqrt(jnp.mean(x * x, axis=-1, keepdims=True) + EPS)
    return inv, x * inv


def _rms_bwd(dyhat, yhat, inv):
    return inv * (dyhat - yhat * jnp.mean(dyhat * yhat, axis=-1, keepdims=True))


def _colsum(x):
    return jnp.sum(x, axis=0, keepdims=True)


def _rope(x, cos4, sin4):
    lane = lax.broadcasted_iota(jnp.int32, x.shape, x.ndim - 1)
    first_half = (lane % QK_ROPE) < (QK_ROPE // 2)
    partner = jnp.where(first_half, pltpu.roll(x, LANES - QK_ROPE // 2, x.ndim - 1), pltpu.roll(x, QK_ROPE // 2, x.ndim - 1))
    return x * cos4 + partner * sin4


def norm_mod_fwd(name, x, g, scale, shift):
    s, d = x.shape
    tr = _tile(s, ROW_TILE)

    def body(x_ref, g_ref, sc_ref, sh_ref, o_ref):
        _, xh = _rms_stats(x_ref[...])
        o_ref[...] = (xh * g_ref[...] * (1.0 + sc_ref[...]) + sh_ref[...]).astype(o_ref.dtype)

    row = pl.BlockSpec((tr, d), lambda i: (i, 0))
    vec = pl.BlockSpec((1, d), lambda i: (0, 0))
    return pl.pallas_call(body, name=name, grid=(s // tr,), in_specs=[row, vec, vec, vec], out_specs=row,
                          out_shape=jax.ShapeDtypeStruct((s, d), BF16), compiler_params=_params("parallel"))(x, g, scale, shift)


def rms_fwd_cols(name, z, off, width, g):
    s = z.shape[0]
    tr = _tile(s, ROW_TILE)
    assert off % width == 0

    def body(x_ref, g_ref, o_ref):
        _, xh = _rms_stats(x_ref[...])
        o_ref[...] = (xh * g_ref[...]).astype(o_ref.dtype)

    return pl.pallas_call(body, name=name, grid=(s // tr,),
                          in_specs=[pl.BlockSpec((tr, width), lambda i: (i, off // width)), pl.BlockSpec((1, width), lambda i: (0, 0))],
                          out_specs=pl.BlockSpec((tr, width), lambda i: (i, 0)),
                          out_shape=jax.ShapeDtypeStruct((s, width), BF16), compiler_params=_params("parallel"))(z, g)


def rms_bwd_cols(name, dy, z, off, width, g):
    s = z.shape[0]
    tr = _tile(s, ROW_TILE)

    def body(dy_ref, x_ref, g_ref, dx_ref, gg_ref):
        @pl.when(pl.program_id(0) == 0)
        def _():
            gg_ref[...] = jnp.zeros_like(gg_ref)

        inv, xh = _rms_stats(x_ref[...])
        dy_v = _f32(dy_ref)
        gg_ref[...] += _colsum(dy_v * xh)
        dx_ref[...] = _rms_bwd(dy_v * g_ref[...], xh, inv).astype(dx_ref.dtype)

    return pl.pallas_call(body, name=name, grid=(s // tr,),
                          in_specs=[pl.BlockSpec((tr, width), lambda i: (i, 0)), pl.BlockSpec((tr, width), lambda i: (i, off // width)),
                                    pl.BlockSpec((1, width), lambda i: (0, 0))],
                          out_specs=[pl.BlockSpec((tr, width), lambda i: (i, 0)), pl.BlockSpec((1, width), lambda i: (0, 0))],
                          out_shape=[jax.ShapeDtypeStruct((s, width), BF16), jax.ShapeDtypeStruct((1, width), F32)],
                          compiler_params=_params("arbitrary"))(dy, z, g)


def post1_pre2_fwd(x, y, gate, g_post, g_pre, scale, shift):
    s, d = x.shape
    tr = _tile(s, ROW_TILE)

    def body(x_ref, y_ref, gate_ref, gp_ref, g_ref, sc_ref, sh_ref, x1_ref, h_ref):
        _, yh = _rms_stats(y_ref[...])
        x1 = x_ref[...] + gate_ref[...] * (yh * gp_ref[...])
        x1_ref[...] = x1
        _, xh = _rms_stats(x1)
        h_ref[...] = (xh * g_ref[...] * (1.0 + sc_ref[...]) + sh_ref[...]).astype(h_ref.dtype)

    row = pl.BlockSpec((tr, d), lambda i: (i, 0))
    vec = pl.BlockSpec((1, d), lambda i: (0, 0))
    return pl.pallas_call(body, name="post1_pre2_fwd", grid=(s // tr,), in_specs=[row, row, vec, vec, vec, vec, vec], out_specs=[row, row],
                          out_shape=[jax.ShapeDtypeStruct((s, d), F32), jax.ShapeDtypeStruct((s, d), BF16)],
                          compiler_params=_params("parallel"))(x, y, gate, g_post, g_pre, scale, shift)


def post2_loss_bwd(x1, ffn, target, gate2, g):
    s, d = x1.shape
    tr = _tile(s, ROW_TILE)

    def body(x_ref, y_ref, t_ref, gate_ref, g_ref, loss_ref, dout_ref, dy_ref, acc_ref):
        @pl.when(pl.program_id(0) == 0)
        def _():
            loss_ref[...] = jnp.zeros_like(loss_ref)
            acc_ref[...] = jnp.zeros_like(acc_ref)

        inv, yh = _rms_stats(y_ref[...])
        r = yh * g_ref[...]
        err = x_ref[...] + gate_ref[...] * r - t_ref[...]
        loss_ref[...] += 0.5 * jnp.sum(jnp.mean(err * err, axis=-1, keepdims=True))
        dout = err / d
        dout_ref[...] = dout
        dr = dout * gate_ref[...]
        acc_ref[0:1, :] += _colsum(dout * r)
        acc_ref[1:2, :] += _colsum(dr * yh)
        dy_ref[...] = _rms_bwd(dr * g_ref[...], yh, inv).astype(dy_ref.dtype)

    row = pl.BlockSpec((tr, d), lambda i: (i, 0))
    vec = pl.BlockSpec((1, d), lambda i: (0, 0))
    return pl.pallas_call(
        body, name="post2_loss_bwd", grid=(s // tr,), in_specs=[row, row, row, vec, vec],
        out_specs=[_full((8, LANES)), row, row, _full((8, d))],
        out_shape=[jax.ShapeDtypeStruct((8, LANES), F32), jax.ShapeDtypeStruct((s, d), F32),
                   jax.ShapeDtypeStruct((s, d), BF16), jax.ShapeDtypeStruct((8, d), F32)],
        compiler_params=_params("arbitrary"))(x1, ffn, target, gate2, g)


def mid_bwd(dh2, dout, x1, y1, pre2_g, scale2, gate1, post1_g):
    s, d = x1.shape
    tr = _tile(s, ROW_TILE)

    def body(dh_ref, dout_ref, x_ref, y_ref, g2_ref, sc_ref, gate_ref, g1_ref, dx_ref, dy_ref, acc_ref):
        @pl.when(pl.program_id(0) == 0)
        def _():
            acc_ref[...] = jnp.zeros_like(acc_ref)

        dh = _f32(dh_ref)
        inv2, xh = _rms_stats(x_ref[...])
        acc_ref[0:1, :] += _colsum(dh)
        acc_ref[1:2, :] += _colsum(dh * (xh * g2_ref[...]))
        t = dh * (1.0 + sc_ref[...])
        acc_ref[2:3, :] += _colsum(t * xh)
        dx1 = dout_ref[...] + _rms_bwd(t * g2_ref[...], xh, inv2)
        dx_ref[...] = dx1
        inv1, yh = _rms_stats(y_ref[...])
        acc_ref[3:4, :] += _colsum(dx1 * (yh * g1_ref[...]))
        dr = dx1 * gate_ref[...]
        acc_ref[4:5, :] += _colsum(dr * yh)
        dy_ref[...] = _rms_bwd(dr * g1_ref[...], yh, inv1).astype(dy_ref.dtype)

    row = pl.BlockSpec((tr, d), lambda i: (i, 0))
    vec = pl.BlockSpec((1, d), lambda i: (0, 0))
    return pl.pallas_call(
        body, name="mid_bwd", grid=(s // tr,), in_specs=[row, row, row, row, vec, vec, vec, vec],
        out_specs=[row, row, _full((8, d))],
        out_shape=[jax.ShapeDtypeStruct((s, d), F32), jax.ShapeDtypeStruct((s, d), BF16), jax.ShapeDtypeStruct((8, d), F32)],
        compiler_params=_params("arbitrary"))(dh2, dout, x1, y1, pre2_g, scale2, gate1, post1_g)


def pre1_bwd(dh1, dx1, x, pre1_g, scale1):
    s, d = x.shape
    tr = _tile(s, ROW_TILE)

    def body(dh_ref, dx1_ref, x_ref, g_ref, sc_ref, dx_ref, acc_ref):
        @pl.when(pl.program_id(0) == 0)
        def _():
            acc_ref[...] = jnp.zeros_like(acc_ref)

        dh = _f32(dh_ref)
        inv, xh = _rms_stats(x_ref[...])
        acc_ref[0:1, :] += _colsum(dh)
        acc_ref[1:2, :] += _colsum(dh * (xh * g_ref[...]))
        t = dh * (1.0 + sc_ref[...])
        acc_ref[2:3, :] += _colsum(t * xh)
        dx_ref[...] = dx1_ref[...] + _rms_bwd(t * g_ref[...], xh, inv)

    row = pl.BlockSpec((tr, d), lambda i: (i, 0))
    vec = pl.BlockSpec((1, d), lambda i: (0, 0))
    return pl.pallas_call(
        body, name="pre1_bwd", grid=(s // tr,), in_specs=[row, row, row, vec, vec], out_specs=[row, _full((8, d))],
        out_shape=[jax.ShapeDtypeStruct((s, d), F32), jax.ShapeDtypeStruct((8, d), F32)],
        compiler_params=_params("arbitrary"))(dh1, dx1, x, pre1_g, scale1)


def _ln_stats(v):
    mu = jnp.mean(v, axis=-1, keepdims=True)
    vc = v - mu
    rstd = lax.rsqrt(jnp.mean(vc * vc, axis=-1, keepdims=True) + EPS)
    return rstd, vc * rstd


def gmlp_fwd(z, width, ln_g, ln_b, wm, bs3):
    s = z.shape[0]
    groups = width // CHUNK

    def body(u_ref, v_ref, g_ref, b_ref, wm_ref, bs_ref, a_ref):
        ug = _gelu(u_ref[...])
        _, vh = _ln_stats(_gelu(v_ref[...]))
        vn = (vh * g_ref[...] + b_ref[...]).astype(BF16)
        for g in range(groups):
            cols = slice(g * CHUNK, (g + 1) * CHUNK)
            mixed = jnp.dot(wm_ref[g], vn[:, cols], preferred_element_type=F32) + bs_ref[g]
            a_ref[:, cols] = (ug[:, cols] * mixed).astype(a_ref.dtype)

    vec = pl.BlockSpec((1, width), lambda n: (0, 0))
    return pl.pallas_call(
        body, name="gmlp_fwd", grid=(s // CHUNK,),
        in_specs=[pl.BlockSpec((CHUNK, width), lambda n: (n, 0)), pl.BlockSpec((CHUNK, width), lambda n: (n, 1)), vec, vec,
                  _full(wm.shape), _full(bs3.shape)],
        out_specs=pl.BlockSpec((CHUNK, width), lambda n: (n, 0)),
        out_shape=jax.ShapeDtypeStruct((s, width), BF16), compiler_params=_params("parallel"))(z, z, ln_g, ln_b, wm, bs3)


def gmlp_bwd(z, width, da, ln_g, ln_b, wm, bs3, dz):
    s = z.shape[0]
    groups = width // CHUNK

    def body(u_ref, v_ref, da_ref, g_ref, b_ref, wm_ref, bs_ref, dz_ref, duv_ref, gw_ref, gb_ref, acc_ref, dvn_ref):
        @pl.when(pl.program_id(0) == 0)
        def _():
            gw_ref[...] = jnp.zeros_like(gw_ref)
            gb_ref[...] = jnp.zeros_like(gb_ref)
            acc_ref[...] = jnp.zeros_like(acc_ref)

        ug, dug = _gelu_and_grad(u_ref[...])
        vg, dvg = _gelu_and_grad(v_ref[...])
        rstd, vh = _ln_stats(vg)
        vn = (vh * g_ref[...] + b_ref[...]).astype(BF16)
        da_v = _f32(da_ref)
        for g in range(groups):
            cols = slice(g * CHUNK, (g + 1) * CHUNK)
            mixed = jnp.dot(wm_ref[g], vn[:, cols], preferred_element_type=F32) + bs_ref[g]
            duv_ref[:, cols] = (da_v[:, cols] * mixed * dug[:, cols]).astype(duv_ref.dtype)
            dm = da_v[:, cols] * ug[:, cols]
            gb_ref[g] += jnp.sum(dm, axis=-1, keepdims=True)
            dmb = dm.astype(BF16)
            gw_ref[g] += lax.dot_general(dmb, vn[:, cols], (((1,), (1,)), ((), ())), preferred_element_type=F32)
            dvn_ref[:, cols] = lax.dot_general(wm_ref[g], dmb, (((0,), (0,)), ((), ())), preferred_element_type=F32)
        dvn = dvn_ref[...]
        acc_ref[0:1, :] += _colsum(dvn * vh)
        acc_ref[1:2, :] += _colsum(dvn)
        dvh = dvn * g_ref[...]
        dv = rstd * (dvh - jnp.mean(dvh, axis=-1, keepdims=True) - vh * jnp.mean(dvh * vh, axis=-1, keepdims=True))
        duv_ref[:, width:] = (dv * dvg).astype(duv_ref.dtype)

        @pl.when(pl.program_id(0) == pl.num_programs(0) - 1)
        def _():
            q = lax.broadcasted_iota(jnp.int32, gw_ref.shape, 1)
            p = lax.broadcasted_iota(jnp.int32, gw_ref.shape, 2)
            gw_ref[...] = jnp.where(p <= q, gw_ref[...], 0.0)

    vec = pl.BlockSpec((1, width), lambda n: (0, 0))
    blk = pl.BlockSpec((CHUNK, width), lambda n: (n, 0))
    return pl.pallas_call(
        body, name="gmlp_bwd", grid=(s // CHUNK,),
        in_specs=[blk, pl.BlockSpec((CHUNK, width), lambda n: (n, 1)), blk, vec, vec, _full(wm.shape), _full(bs3.shape),
                  pl.BlockSpec(memory_space=pl.ANY)],
        out_specs=[pl.BlockSpec((CHUNK, 2 * width), lambda n: (n, 0)), _full(wm.shape), _full(bs3.shape), _full((8, width))],
        out_shape=[jax.ShapeDtypeStruct(dz.shape, dz.dtype), jax.ShapeDtypeStruct(wm.shape, F32),
                   jax.ShapeDtypeStruct(bs3.shape, F32), jax.ShapeDtypeStruct((8, width), F32)],
        scratch_shapes=[pltpu.VMEM((CHUNK, width), F32)], input_output_aliases={7: 0},
        compiler_params=_params("arbitrary"))(z, z, da, ln_g, ln_b, wm, bs3, dz)


def merge_fwd(z, off_a, off_b, ya, yb):
    s, d = ya.shape
    tr, tc = _tile(s, ROW_TILE * 2), _tile(d, COL_TILE)
    assert off_a % tc == 0 and off_b % tc == 0

    def body(ga_ref, gb_ref, ya_ref, yb_ref, o_ref):
        o_ref[...] = (_sigmoid(ga_ref[...]) * _f32(ya_ref) + _sigmoid(gb_ref[...]) * _f32(yb_ref)).astype(o_ref.dtype)

    blk = pl.BlockSpec((tr, tc), lambda i, j: (i, j))
    return pl.pallas_call(
        body, name="merge_fwd", grid=(s // tr, d // tc),
        in_specs=[pl.BlockSpec((tr, tc), lambda i, j: (i, off_a // tc + j)), pl.BlockSpec((tr, tc), lambda i, j: (i, off_b // tc + j)), blk, blk],
        out_specs=blk, out_shape=jax.ShapeDtypeStruct((s, d), BF16), compiler_params=_params("parallel", "parallel"))(z, z, ya, yb)


def merge_bwd(z, off_a, off_b, ya, yb, dm):
    s, d = ya.shape
    tr, tc = _tile(s, ROW_TILE * 2), _tile(d, COL_TILE)
    nc = d // tc

    def body(ga_ref, gb_ref, ya_ref, yb_ref, dm_ref, dya_ref, dyb_ref, dga_ref, dgb_ref):
        dm_v = _f32(dm_ref)
        sa, sb = _sigmoid(ga_ref[...]), _sigmoid(gb_ref[...])
        dya_ref[...] = (dm_v * sa).astype(dya_ref.dtype)
        dyb_ref[...] = (dm_v * sb).astype(dyb_ref.dtype)
        dga_ref[...] = (dm_v * _f32(ya_ref) * sa * (1.0 - sa)).astype(dga_ref.dtype)
        dgb_ref[...] = (dm_v * _f32(yb_ref) * sb * (1.0 - sb)).astype(dgb_ref.dtype)

    blk = pl.BlockSpec((tr, tc), lambda i, j: (i, j))
    out = jax.ShapeDtypeStruct((s, d), BF16)
    return pl.pallas_call(
        body, name="merge_bwd", grid=(s // tr, nc),
        in_specs=[pl.BlockSpec((tr, tc), lambda i, j: (i, off_a // tc + j)), pl.BlockSpec((tr, tc), lambda i, j: (i, off_b // tc + j)), blk, blk, blk],
        out_specs=[blk, blk, blk, blk], out_shape=[out, out, out, out],
        compiler_params=_params("parallel", "parallel"))(z, z, ya, yb, dm)


_ATT_SCALE = (QK_NOPE + QK_ROPE) ** -0.5
_NEG = -1e30


def rope_k(z, off, cos4, sin4):
    s = z.shape[0]
    tr = _tile(s, ROW_TILE * 2)
    assert off % LANES == 0

    def body(k_ref, c_ref, s_ref, o_ref):
        k = _f32(k_ref)
        k = k + pltpu.roll(k, QK_ROPE, 1)
        o_ref[...] = _rope(k, c_ref[...], s_ref[...]).astype(o_ref.dtype)

    row = pl.BlockSpec((tr, LANES), lambda i: (i, 0))
    return pl.pallas_call(body, name="rope_k", grid=(s // tr,),
                          in_specs=[pl.BlockSpec((tr, LANES), lambda i: (i, off // LANES)), row, row], out_specs=row,
                          out_shape=jax.ShapeDtypeStruct((s, LANES), BF16), compiler_params=_params("parallel"))(z, cos4, sin4)


def _dot_nt(a, b):
    return lax.dot_general(a, b, (((1,), (1,)), ((), ())), preferred_element_type=F32)


def _dot_tn(a, b):
    return lax.dot_general(a, b, (((0,), (0,)), ((), ())), preferred_element_type=F32)


def _q_cat(q_n, qpr, hh):
    lane = lax.broadcasted_iota(jnp.int32, qpr.shape, 1)
    sel = (lane < QK_ROPE) if hh == 0 else (lane >= QK_ROPE)
    return jnp.concatenate([q_n, jnp.where(sel, qpr, jnp.zeros_like(qpr))], axis=1)


def _causal(sc):
    row = lax.broadcasted_iota(jnp.int32, sc.shape, 0)
    col = lax.broadcasted_iota(jnp.int32, sc.shape, 1)
    return jnp.where(col <= row, sc, _NEG)


def attn_fwd(qn, qp, kv, kpr, cos4, sin4):
    s = qn.shape[0]
    hp = HEADS // 2
    t = _tile(s, ATT_TILE)
    nq = s // t

    def body(qn_ref, qp_ref, kv_ref, kp_ref, c_ref, s_ref, o_ref, qpr_ref, l_ref, kcat_ref):
        qi = pl.program_id(1)

        @pl.when(qi == 0)
        def _():
            for hh in range(2):
                kcat_ref[hh, :, 0:QK_NOPE] = kv_ref[:, 2 * hh * QK_NOPE:(2 * hh + 1) * QK_NOPE]
                kcat_ref[hh, :, QK_NOPE:] = kp_ref[...]

        qpr = _rope(qp_ref[...], c_ref[...], s_ref[...]).astype(BF16)
        qpr_ref[...] = qpr
        qcat = [_q_cat(qn_ref[:, hh * QK_NOPE:(hh + 1) * QK_NOPE], qpr, hh) for hh in range(2)]

        def block(kb, carry, diagonal):
            rows = pl.ds(pl.multiple_of(kb * t, t), t)
            out = []
            for hh in range(2):
                m, l, acc = carry[hh]
                sc = _dot_nt(qcat[hh], kcat_ref[hh, rows, :]) * _ATT_SCALE
                if diagonal:
                    sc = _causal(sc)
                m_new = jnp.maximum(m, jnp.max(sc, axis=-1, keepdims=True))
                alpha = jnp.exp(m - m_new)
                p = jnp.exp(sc - m_new)
                l = alpha * l + jnp.sum(p, axis=-1, keepdims=True)
                v = kv_ref[rows, (2 * hh + 1) * QK_NOPE:(2 * hh + 2) * QK_NOPE]
                acc = alpha * acc + jnp.dot(p.astype(BF16), v, preferred_element_type=F32)
                out.append((m_new, l, acc))
            return tuple(out)

        one = (jnp.full((t, 1), _NEG, F32), jnp.zeros((t, 1), F32), jnp.zeros((t, V_HEAD), F32))
        carry = lax.fori_loop(0, qi, lambda kb, cr: block(kb, cr, False), (one, one))
        carry = block(qi, carry, True)
        for hh in range(2):
            m, l, acc = carry[hh]
            o_ref[:, hh * V_HEAD:(hh + 1) * V_HEAD] = (acc / l).astype(o_ref.dtype)
            l_ref[:, hh:hh + 1] = m + jnp.log(l)

    return pl.pallas_call(
        body, name="attn_fwd", grid=(hp, nq),
        in_specs=[pl.BlockSpec((t, 2 * QK_NOPE), lambda h, i: (i, h)), pl.BlockSpec((t, LANES), lambda h, i: (i, h)),
                  pl.BlockSpec((s, 4 * QK_NOPE), lambda h, i: (0, h)), _full((s, LANES)),
                  pl.BlockSpec((t, LANES), lambda h, i: (i, 0)), pl.BlockSpec((t, LANES), lambda h, i: (i, 0))],
        out_specs=[pl.BlockSpec((t, 2 * V_HEAD), lambda h, i: (i, h)), pl.BlockSpec((t, LANES), lambda h, i: (i, h)),
                   pl.BlockSpec((None, t, 2), lambda h, i: (h, i, 0))],
        out_shape=[jax.ShapeDtypeStruct((s, HEADS * V_HEAD), ACT), jax.ShapeDtypeStruct((s, HEADS * QK_ROPE), BF16),
                   jax.ShapeDtypeStruct((hp, s, 2), F32)],
        scratch_shapes=[pltpu.VMEM((2, s, 2 * QK_NOPE), BF16)],
        compiler_params=_params("parallel", "arbitrary"))(qn, qp, kv, kpr, cos4, sin4)


def attn_bwd(qn, qpr, kv, kpr, o, do, lse, cos4, sin4):
    s = qn.shape[0]
    hp = HEADS // 2
    t = _tile(s, ATT_TILE)
    nk = s // t

    def body(qn_ref, qpr_ref, kv_ref, kp_ref, o_ref, do_ref, l_ref, c_ref, s_ref,
             dqn_ref, dqp_ref, dkv_ref, dkp_ref, qcat_ref, dq_ref, delta_ref):
        ki = pl.program_id(1)

        @pl.when(ki == 0)
        def _():
            dq_ref[...] = jnp.zeros_like(dq_ref)
            for hh in range(2):
                qcat_ref[hh] = _q_cat(qn_ref[:, hh * QK_NOPE:(hh + 1) * QK_NOPE], qpr_ref[...], hh)
                cols = slice(hh * V_HEAD, (hh + 1) * V_HEAD)
                delta_ref[hh] = jnp.sum(do_ref[:, cols].astype(F32) * o_ref[:, cols].astype(F32), axis=-1, keepdims=True)

        rows_k = pl.ds(pl.multiple_of(ki * t, t), t)
        kcat = [jnp.concatenate([kv_ref[rows_k, 2 * hh * QK_NOPE:(2 * hh + 1) * QK_NOPE], kp_ref[rows_k, :]], axis=1) for hh in range(2)]
        vs = [kv_ref[rows_k, (2 * hh + 1) * QK_NOPE:(2 * hh + 2) * QK_NOPE] for hh in range(2)]

        def block(qb, carry, diagonal):
            rows = pl.ds(pl.multiple_of(qb * t, t), t)
            out = []
            for hh in range(2):
                dkc, dv = carry[hh]
                q_c = qcat_ref[hh, rows, :]
                do_b = do_ref[rows, hh * V_HEAD:(hh + 1) * V_HEAD].astype(BF16)
                sc = _dot_nt(q_c, kcat[hh]) * _ATT_SCALE
                if diagonal:
                    sc = _causal(sc)
                p = jnp.exp(sc - l_ref[rows, hh:hh + 1])
                dpv = _dot_nt(do_b, vs[hh])
                ds = (p * (dpv - delta_ref[hh, rows, :]) * _ATT_SCALE).astype(BF16)
                dv = dv + _dot_tn(p.astype(BF16), do_b)
                dkc = dkc + _dot_tn(ds, q_c)
                dq_ref[hh, rows, :] += jnp.dot(ds, kcat[hh], preferred_element_type=F32)
                out.append((dkc, dv))
            return tuple(out)

        one = (jnp.zeros((t, 2 * QK_NOPE), F32), jnp.zeros((t, V_HEAD), F32))
        carry = block(ki, (one, one), True)
        carry = lax.fori_loop(ki + 1, nk, lambda qb, cr: block(qb, cr, False), carry)
        dkp = jnp.zeros((t, LANES), F32)
        for hh in range(2):
            dkc, dv = carry[hh]
            dkv_ref[:, 2 * hh * QK_NOPE:(2 * hh + 1) * QK_NOPE] = dkc[:, :QK_NOPE].astype(dkv_ref.dtype)
            dkv_ref[:, (2 * hh + 1) * QK_NOPE:(2 * hh + 2) * QK_NOPE] = dv.astype(dkv_ref.dtype)
            dkp = dkp + dkc[:, QK_NOPE:]
        dkp_ref[...] = dkp

        @pl.when(ki == nk - 1)
        def _():
            lane = lax.broadcasted_iota(jnp.int32, (s, LANES), 1)
            dqp = jnp.where(lane < QK_ROPE, dq_ref[0, :, QK_NOPE:], dq_ref[1, :, QK_NOPE:])
            dqp_ref[...] = _rope(dqp, c_ref[...], -s_ref[...]).astype(dqp_ref.dtype)
            for hh in range(2):
                dqn_ref[:, hh * QK_NOPE:(hh + 1) * QK_NOPE] = dq_ref[hh, :, :QK_NOPE].astype(dqn_ref.dtype)

    qblk = pl.BlockSpec((s, 2 * QK_NOPE), lambda h, i: (0, h))
    pblk = pl.BlockSpec((s, LANES), lambda h, i: (0, h))
    tab = _full((s, LANES))
    return pl.pallas_call(
        body, name="attn_bwd", grid=(hp, nk),
        in_specs=[qblk, pblk, pl.BlockSpec((s, 4 * QK_NOPE), lambda h, i: (0, h)), tab, qblk, qblk,
                  pl.BlockSpec((None, s, 2), lambda h, i: (h, 0, 0)), tab, tab],
        out_specs=[qblk, pblk, pl.BlockSpec((t, 4 * QK_NOPE), lambda h, i: (i, h)), pl.BlockSpec((None, t, LANES), lambda h, i: (h, i, 0))],
        out_shape=[jax.ShapeDtypeStruct((s, HEADS * QK_NOPE), BF16), jax.ShapeDtypeStruct((s, HEADS * QK_ROPE), BF16),
                   jax.ShapeDtypeStruct((s, HEADS * 2 * QK_NOPE), BF16), jax.ShapeDtypeStruct((hp, s, LANES), F32)],
        scratch_shapes=[pltpu.VMEM((2, s, 2 * QK_NOPE), BF16), pltpu.VMEM((2, s, 2 * QK_NOPE), F32), pltpu.VMEM((2, s, 1), F32)],
        compiler_params=_params("parallel", "arbitrary"))(qn, qpr, kv, kpr, o, do, lse, cos4, sin4)


def kpe_bwd(dkp, cos4, sin4, pad_cols):
    hp, s, _ = dkp.shape
    tr = _tile(s, ROW_TILE * 2)

    def body(d_ref, c_ref, s_ref, o_ref):
        tot = d_ref[0]
        for h in range(1, hp):
            tot = tot + d_ref[h]
        tot = tot + pltpu.roll(tot, QK_ROPE, 1)
        lane = lax.broadcasted_iota(jnp.int32, tot.shape, 1)
        dk = jnp.where(lane < QK_ROPE, _rope(tot, c_ref[...], -s_ref[...]), jnp.zeros_like(tot))
        o_ref[...] = jnp.zeros_like(o_ref)
        o_ref[:, 0:LANES] = dk.astype(o_ref.dtype)

    row = pl.BlockSpec((tr, LANES), lambda i: (i, 0))
    return pl.pallas_call(body, name="kpe_bwd", grid=(s // tr,),
                          in_specs=[pl.BlockSpec((hp, tr, LANES), lambda i: (0, i, 0)), row, row],
                          out_specs=pl.BlockSpec((tr, pad_cols), lambda i: (i, 0)),
                          out_shape=jax.ShapeDtypeStruct((s, pad_cols), BF16), compiler_params=_params("parallel"))(dkp, cos4, sin4)


def _shift_down(x, n):
    row = lax.broadcasted_iota(jnp.int32, x.shape, 0)
    return jnp.where(row >= n, pltpu.roll(x, n, 0), jnp.zeros_like(x))


def _shift_up(x, n):
    rows = x.shape[0]
    row = lax.broadcasted_iota(jnp.int32, x.shape, 0)
    return jnp.where(row < rows - n, pltpu.roll(x, rows - n, 0), jnp.zeros_like(x))


def _conv(x, w_ref, b_ref):
    return w_ref[2:3, :] * x + w_ref[1:2, :] * _shift_down(x, 1) + w_ref[0:1, :] * _shift_down(x, 2) + b_ref[...]


def conv_act_fwd(upre, conv_w, conv_b):
    s, f2 = upre.shape
    f = f2 // 2
    tc = _tile(f, COL_TILE)
    nc = f // tc

    def body(ug_ref, uv_ref, wg_ref, wv_ref, bg_ref, bv_ref, o_ref, gv_ref):
        gh = _conv(_f32(ug_ref), wg_ref, bg_ref)
        vh = _conv(_f32(uv_ref), wv_ref, bv_ref)
        o_ref[...] = (gh * _sigmoid(gh) * vh).astype(o_ref.dtype)
        gv_ref[0] = gh.astype(gv_ref.dtype)
        gv_ref[1] = vh.astype(gv_ref.dtype)

    def spec(rows, shift):
        return pl.BlockSpec((rows, tc), lambda j: (0, j + shift))

    return pl.pallas_call(
        body, name="conv_act_fwd", grid=(nc,),
        in_specs=[spec(s, 0), spec(s, nc), spec(3, 0), spec(3, nc), spec(1, 0), spec(1, nc)],
        out_specs=[spec(s, 0), pl.BlockSpec((2, s, tc), lambda j: (0, 0, j))],
        out_shape=[jax.ShapeDtypeStruct((s, f), BF16), jax.ShapeDtypeStruct((2, s, f), ACT)],
        compiler_params=_params("parallel"))(upre, upre, conv_w, conv_w, conv_b, conv_b)


def conv_act_bwd(upre, conv_w, gv, df):
    s, f2 = upre.shape
    f = f2 // 2
    tc = _tile(f, COL_TILE)
    nc = f // tc

    def half(x, d, w_ref, du_ref, which, gw_ref, gb_ref):
        d1, d2 = _shift_up(d, 1), _shift_up(d, 2)
        gb_ref[...] = _colsum(d)
        gw_ref[2:3, :] = _colsum(d * x)
        gw_ref[1:2, :] = _colsum(d1 * x)
        gw_ref[0:1, :] = _colsum(d2 * x)
        du_ref[which] = (w_ref[2:3, :] * d + w_ref[1:2, :] * d1 + w_ref[0:1, :] * d2).astype(du_ref.dtype)

    def body(ug_ref, uv_ref, wg_ref, wv_ref, gv_ref, df_ref, du_ref, gwg_ref, gwv_ref, gbg_ref, gbv_ref):
        xg, xv = _f32(ug_ref), _f32(uv_ref)
        gh, vh = gv_ref[0].astype(F32), gv_ref[1].astype(F32)
        sg = _sigmoid(gh)
        df_v = _f32(df_ref)
        half(xg, df_v * vh * (sg * (1.0 + gh * (1.0 - sg))), wg_ref, du_ref, 0, gwg_ref, gbg_ref)
        half(xv, df_v * (gh * sg), wv_ref, du_ref, 1, gwv_ref, gbv_ref)

    def spec(rows, shift):
        return pl.BlockSpec((rows, tc), lambda j: (0, j + shift))

    gw = jax.ShapeDtypeStruct((3, f), F32)
    gb = jax.ShapeDtypeStruct((1, f), F32)
    return pl.pallas_call(
        body, name="conv_act_bwd", grid=(nc,),
        in_specs=[spec(s, 0), spec(s, nc), spec(3, 0), spec(3, nc), pl.BlockSpec((2, s, tc), lambda j: (0, 0, j)), spec(s, 0)],
        out_specs=[pl.BlockSpec((2, s, tc), lambda j: (0, 0, j)), spec(3, 0), spec(3, 0), spec(1, 0), spec(1, 0)],
        out_shape=[jax.ShapeDtypeStruct((2, s, f), BF16), gw, gw, gb, gb],
        compiler_params=_params("parallel"))(upre, upre, conv_w, conv_w, gv, df)


def _elementwise_tile(r, c, limit):
    if r % 8:
        return r, c
    best = (8, c if c % LANES else LANES)
    for k in (1, 2, 4, 8, 16):
        if k > 1 and c % (LANES * k):
            continue
        tc = c // k
        tr = max(8, min(r, limit // tc) // 8 * 8)
        while r % tr:
            tr -= 8
        if tr * tc <= max(limit, 8 * tc) and tr * tc > best[0] * best[1]:
            best = (tr, tc)
    return best


def adamw(name, w, m, v, parts):
    npart, r, c = parts.shape
    tr, tc = _elementwise_tile(r, c, ADAMW_TILE_ELEMS)
    bc1 = 1.0 - ADAM_B1 ** ADAM_STEP
    bc2 = 1.0 - ADAM_B2 ** ADAM_STEP

    def body(w_ref, m_ref, v_ref, p_ref, g_ref, d_ref, nm_ref, nv_ref):
        g = p_ref[0].astype(F32)
        for k in range(1, npart):
            g = g + p_ref[k].astype(F32)
        m_new = ADAM_B1 * m_ref[...] + (1.0 - ADAM_B1) * g
        v_new = ADAM_B2 * v_ref[...] + (1.0 - ADAM_B2) * (g * g)
        g_ref[...] = g
        nm_ref[...] = m_new
        nv_ref[...] = v_new
        d_ref[...] = -ADAM_LR * ((m_new / bc1) / (jnp.sqrt(v_new / bc2) + ADAM_EPS) + ADAM_WD * w_ref[...])

    deps = _TOKENS.take()
    blk = pl.BlockSpec((tr, tc), lambda i, j: (i, j))
    out = jax.ShapeDtypeStruct((r, c), F32)
    return pl.pallas_call(
        lambda *refs: body(*refs[:4], *refs[4 + len(deps):]), name=name, grid=(r // tr, c // tc),
        in_specs=[blk, blk, blk, pl.BlockSpec((npart, tr, tc), lambda i, j: (0, i, j))] + [pl.BlockSpec(memory_space=pl.ANY)] * len(deps),
        out_specs=[blk, blk, blk, blk], out_shape=[out, out, out, out],
        compiler_params=_params("parallel", "parallel"))(w, m, v, parts, *deps)


def _position():
    return lax.axis_index("x"), lax.axis_index("y"), lax.axis_index("c")


def _index(p):
    return 4 * p[0] + 2 * p[1] + p[2]


def _peer(me, r):
    return (me[0] ^ ((r >> 2) & 1), me[1] ^ ((r >> 1) & 1), me[2] ^ (r & 1))


_ANY = pl.BlockSpec(memory_space=pl.ANY)


_HBM = pl.BlockSpec(memory_space=pltpu.HBM)
_SEM = pl.BlockSpec(memory_space=pltpu.SEMAPHORE)
_EFFECT = pltpu.SideEffectType.DATAFLOW_SIDE_EFFECTING
_TOKEN = jax.ShapeDtypeStruct((8, LANES), F32)
_VM = pl.BlockSpec(memory_space=pltpu.VMEM)
_SIDE = pltpu.CompilerParams(has_side_effects=_EFFECT)


def _hbm(a):
    return pltpu.with_memory_space_constraint(a, pltpu.HBM)


def _like(a):
    return pltpu.HBM(a.shape, a.dtype)


def _dma_sems(n):
    return pltpu.SemaphoreType.DMA((n,))


def _other_chips(x, y):
    return [(1 - x, y), (x, 1 - y), (1 - x, 1 - y)]


COPY_STREAMS = 8


def _row_chunks(src, dst):
    rows = src.shape[0]
    n = COPY_STREAMS
    while n > 1 and rows % (16 * n):
        n //= 2
    r = rows // n
    return [(src.at[pl.ds(i * r, r)], dst.at[pl.ds(i * r, r)]) for i in range(n)]


class _rcopy:
    def __init__(self, src, dst, send_sem, recv_sem, to):
        self.parts = [pltpu.make_async_remote_copy(src_ref=s, dst_ref=d, send_sem=send_sem, recv_sem=recv_sem, device_id=to, device_id_type=MESH)
                      for s, d in _row_chunks(src, dst)]

    def start(self):
        for cp in self.parts:
            cp.start()

    def wait_send(self):
        for cp in self.parts:
            cp.wait_send()

    def wait_recv(self):
        for cp in self.parts:
            cp.wait_recv()


def _afters(after):
    return list(after) if isinstance(after, (list, tuple)) else [after]


def ag_start(name, shards, after):
    n = len(shards)
    lands = [lax.empty((N_DEV,) + a.shape, a.dtype) for a in shards]
    afters = _afters(after)
    na = len(afters)

    def body(*refs):
        ins, lnd, send_sems, recv_sems, token = refs[:n], refs[n:2 * n], refs[2 * n + na], refs[2 * n + na + 1], refs[4 * n + na + 2]
        x, y, c = _position()
        for w in range(n):
            slot = lnd[w].at[_index((x, y, c))]
            for k, to in enumerate([(x, y, 1 - c)] + [(*chip, c) for chip in _other_chips(x, y)]):
                _rcopy(ins[w], slot, send_sems.at[4 * w + k], recv_sems.at[4 * w + k], to).start()
        token[...] = jnp.zeros_like(token)

    out = pl.pallas_call(
        body, name=name, out_shape=(_dma_sems(4 * n), _dma_sems(4 * n)) + tuple(_like(a) for a in shards + lands) + (_TOKEN,),
        in_specs=[_HBM] * (2 * n) + [_ANY] * na, out_specs=(_SEM, _SEM) + (_HBM,) * (2 * n) + (_VM,),
        input_output_aliases={i: 2 + i for i in range(2 * n)}, compiler_params=_SIDE)(*[_hbm(a) for a in shards + lands], *afters)
    _TOKENS.push(out[-1])
    return out[0], out[1], list(out[2:2 + n]), list(out[2 + n:2 + 2 * n])


def _split_rows(ref):
    rows = ref.shape[0]
    h = rows // 32 * 16
    return ref.at[pl.ds(0, h)], ref.at[pl.ds(h, rows - h)]


def relay_start(name, shards, after):
    n = len(shards)
    lands = [lax.empty((N_DEV,) + a.shape, a.dtype) for a in shards]
    afters = _afters(after)
    na = len(afters)

    def body(*refs):
        ins, lnd, send_sems, recv_sems, token = refs[:n], refs[n:2 * n], refs[2 * n + na], refs[2 * n + na + 1], refs[4 * n + na + 2]
        x, y, c = _position()
        for w in range(n):
            slot = lnd[w].at[_index((x, y, c))]
            for k, to in enumerate([(x, y, 1 - c), (1 - x, y, c), (x, 1 - y, c)]):
                _rcopy(ins[w], slot, send_sems.at[3 * w + k], recv_sems.at[3 * w + k], to).start()
        token[...] = jnp.zeros_like(token)

    out = pl.pallas_call(
        body, name=name, out_shape=(_dma_sems(3 * n), _dma_sems(3 * n)) + tuple(_like(a) for a in shards + lands) + (_TOKEN,),
        in_specs=[_HBM] * (2 * n) + [_ANY] * na, out_specs=(_SEM, _SEM) + (_HBM,) * (2 * n) + (_VM,),
        input_output_aliases={i: 2 + i for i in range(2 * n)}, compiler_params=_SIDE)(*[_hbm(a) for a in shards + lands], *afters)
    _TOKENS.push(out[-1])
    return out[0], out[1], list(out[2:2 + n]), list(out[2 + n:2 + 2 * n])


def relay_pass(name, started, after):
    send, recv, shards, lands = started
    n = len(shards)
    afters = _afters(after)
    na = len(afters)

    def body(*refs):
        ins, lnd, send_sems, recv_sems = refs[:n], refs[n:2 * n], refs[2 * n], refs[2 * n + 1]
        fsend, frecv, psend, precv = refs[2 * n + 2 + na:2 * n + 6 + na]
        token = refs[4 * n + 6 + na]
        x, y, c = _position()
        nbrs = [(1 - x, y, c), (x, 1 - y, c)]
        for w in range(n):
            for j, nbr in enumerate(nbrs):
                slot = lnd[w].at[_index(nbr)]
                _rcopy(ins[w], slot, send_sems.at[3 * w + 1 + j], recv_sems.at[3 * w + 1 + j], nbr).wait_recv()
                _rcopy(slot, slot, fsend.at[2 * w + j], frecv.at[2 * w + j], (x, y, 1 - c)).start()
                part = _split_rows(slot)[j]
                _rcopy(part, part, psend.at[2 * w + j], precv.at[2 * w + j], nbrs[1 - j]).start()
        token[...] = jnp.zeros_like(token)

    out = pl.pallas_call(
        body, name=name, out_shape=(_dma_sems(2 * n),) * 4 + tuple(_like(a) for a in shards + lands) + (_TOKEN,),
        in_specs=[_HBM] * (2 * n) + [_SEM, _SEM] + [_ANY] * na, out_specs=(_SEM,) * 4 + (_HBM,) * (2 * n) + (_VM,),
        input_output_aliases={i: 4 + i for i in range(2 * n)}, compiler_params=_SIDE)(*shards, *lands, send, recv, *afters)
    _TOKENS.push(out[-1])
    return (send, recv) + tuple(out[:4]) + (list(out[4:4 + n]), list(out[4 + n:4 + 2 * n]))


def relay_forward(name, passed, after):
    send, recv, fsend, frecv, psend, precv, shards, lands = passed
    n = len(shards)
    afters = _afters(after)
    na = len(afters)

    def body(*refs):
        ins, lnd, precv_r = refs[:n], refs[n:2 * n], refs[2 * n]
        gsend, grecv, token = refs[2 * n + 1 + na], refs[2 * n + 2 + na], refs[4 * n + 3 + na]
        x, y, c = _position()
        for w in range(n):
            slot = lnd[w].at[_index((1 - x, 1 - y, c))]
            for j, part in enumerate(_split_rows(slot)):
                _rcopy(part, part, precv_r.at[2 * w + j], precv_r.at[2 * w + j], (x, y, 1 - c)).wait_recv()
            _rcopy(slot, slot, gsend.at[w], grecv.at[w], (x, y, 1 - c)).start()
        token[...] = jnp.zeros_like(token)

    out = pl.pallas_call(
        body, name=name, out_shape=(_dma_sems(n), _dma_sems(n)) + tuple(_like(a) for a in shards + lands) + (_TOKEN,),
        in_specs=[_HBM] * (2 * n) + [_SEM] + [_ANY] * na, out_specs=(_SEM, _SEM) + (_HBM,) * (2 * n) + (_VM,),
        input_output_aliases={i: 2 + i for i in range(2 * n)}, compiler_params=_SIDE)(*shards, *lands, precv, *afters)
    _TOKENS.push(out[-1])
    return send, recv, fsend, frecv, psend, out[0], out[1], list(out[2:2 + n]), list(out[2 + n:2 + 2 * n])


def relay_wait(name, forwarded, after):
    send, recv, fsend, frecv, psend, gsend, grecv, shards, lands = forwarded
    n = len(shards)

    def body(*refs):
        ins, lnd = refs[:n], refs[n:2 * n]
        send_sems, recv_sems, fsend_r, frecv_r, psend_r, gsend_r, grecv_r = refs[2 * n:2 * n + 7]
        x, y, c = _position()
        sibling = (x, y, 1 - c)
        for w in range(n):
            own = lnd[w].at[_index((x, y, c))]
            _rcopy(ins[w], lnd[w].at[_index(sibling)], send_sems.at[3 * w], recv_sems.at[3 * w], sibling).wait_recv()
            for j, nbr in enumerate([(1 - x, y, 1 - c), (x, 1 - y, 1 - c)]):
                _rcopy(ins[w], lnd[w].at[_index(nbr)], fsend_r.at[2 * w + j], frecv_r.at[2 * w + j], sibling).wait_recv()
            _rcopy(ins[w], lnd[w].at[_index((1 - x, 1 - y, 1 - c))], gsend_r.at[w], grecv_r.at[w], sibling).wait_recv()
            for k in range(3):
                _rcopy(ins[w], own, send_sems.at[3 * w + k], recv_sems.at[3 * w + k], sibling).wait_send()
            for j in range(2):
                _rcopy(ins[w], own, fsend_r.at[2 * w + j], frecv_r.at[2 * w + j], sibling).wait_send()
                part = _split_rows(own)[j]
                _rcopy(part, part, psend_r.at[2 * w + j], psend_r.at[2 * w + j], sibling).wait_send()
            _rcopy(ins[w], own, gsend_r.at[w], grecv_r.at[w], sibling).wait_send()

    out = pl.pallas_call(
        body, name=name, out_shape=tuple(_like(a) for a in shards + lands),
        in_specs=[_HBM] * (2 * n) + [_SEM] * 7 + [_ANY] * len(_afters(after)),
        out_specs=(_HBM,) * (2 * n), input_output_aliases={i: i for i in range(2 * n)},
        compiler_params=_SIDE)(*shards, *lands, send, recv, fsend, frecv, psend, gsend, grecv, *_afters(after))
    return [lax.dynamic_update_index_in_dim(land, shard, _index(_position()), 0) for shard, land in zip(out[:n], out[n:])]


def ag_forward(name, started, after):
    send, recv, shards, lands = started
    n = len(shards)
    afters = list(after) if isinstance(after, (list, tuple)) else [after]
    na = len(afters)

    def body(*refs):
        ins, lnd, send_sems, recv_sems = refs[:n], refs[n:2 * n], refs[2 * n], refs[2 * n + 1]
        fsend, frecv, token = refs[2 * n + 2 + na], refs[2 * n + 3 + na], refs[4 * n + 4 + na]
        x, y, c = _position()
        for w in range(n):
            for j, chip in enumerate(_other_chips(x, y)):
                slot = lnd[w].at[_index((*chip, c))]
                _rcopy(ins[w], slot, send_sems.at[4 * w + 1 + j], recv_sems.at[4 * w + 1 + j], (*chip, c)).wait_recv()
                _rcopy(slot, slot, fsend.at[3 * w + j], frecv.at[3 * w + j], (x, y, 1 - c)).start()
        token[...] = jnp.zeros_like(token)

    out = pl.pallas_call(
        body, name=name, out_shape=(_dma_sems(3 * n), _dma_sems(3 * n)) + tuple(_like(a) for a in shards + lands) + (_TOKEN,),
        in_specs=[_HBM] * (2 * n) + [_SEM, _SEM] + [_ANY] * na, out_specs=(_SEM, _SEM) + (_HBM,) * (2 * n) + (_VM,),
        input_output_aliases={i: 2 + i for i in range(2 * n)}, compiler_params=_SIDE)(*shards, *lands, send, recv, *afters)
    _TOKENS.push(out[-1])
    return send, recv, out[0], out[1], list(out[2:2 + n]), list(out[2 + n:2 + 2 * n])


def ag_wait(name, forwarded, after):
    send, recv, fsend, frecv, shards, lands = forwarded
    n = len(shards)

    def body(*refs):
        ins, lnd, send_sems, recv_sems, fsend_r, frecv_r = refs[:n], refs[n:2 * n], refs[2 * n], refs[2 * n + 1], refs[2 * n + 2], refs[2 * n + 3]
        x, y, c = _position()
        sibling = (x, y, 1 - c)
        for w in range(n):
            own = lnd[w].at[_index((x, y, c))]
            _rcopy(ins[w], lnd[w].at[_index(sibling)], send_sems.at[4 * w], recv_sems.at[4 * w], sibling).wait_recv()
            for j, chip in enumerate(_other_chips(x, y)):
                _rcopy(ins[w], lnd[w].at[_index((*chip, 1 - c))], fsend_r.at[3 * w + j], frecv_r.at[3 * w + j], sibling).wait_recv()
            for k in range(4):
                _rcopy(ins[w], own, send_sems.at[4 * w + k], recv_sems.at[4 * w + k], sibling).wait_send()
            for j in range(3):
                _rcopy(ins[w], own, fsend_r.at[3 * w + j], frecv_r.at[3 * w + j], sibling).wait_send()

    out = pl.pallas_call(
        body, name=name, out_shape=tuple(_like(a) for a in shards + lands),
        in_specs=[_HBM] * (2 * n) + [_SEM] * 4 + [_ANY] * len(_afters(after)),
        out_specs=(_HBM,) * (2 * n), input_output_aliases={i: i for i in range(2 * n)},
        compiler_params=_SIDE)(*shards, *lands, send, recv, fsend, frecv, *_afters(after))
    return [lax.dynamic_update_index_in_dim(land, shard, _index(_position()), 0) for shard, land in zip(out[:n], out[n:])]


def rs_d2d_start(name, grads):
    n = len(grads)
    lands = [lax.empty((4,) + g.shape[1:], g.dtype) for g in grads]

    def body(*refs):
        ins, lnd, send_sems, recv_sems, token = refs[:n], refs[n:2 * n], refs[2 * n], refs[2 * n + 1], refs[4 * n + 2]
        x, y, c = _position()
        for w in range(n):
            for i in range(4):
                _rcopy(ins[w].at[2 * i + 1 - c], lnd[w].at[i], send_sems.at[4 * w + i], recv_sems.at[4 * w + i], (x, y, 1 - c)).start()
        token[...] = jnp.zeros_like(token)

    out = pl.pallas_call(
        body, name=name, out_shape=(_dma_sems(4 * n), _dma_sems(4 * n)) + tuple(_like(a) for a in grads + lands) + (_TOKEN,),
        in_specs=[_HBM] * (2 * n), out_specs=(_SEM, _SEM) + (_HBM,) * (2 * n) + (_VM,),
        input_output_aliases={i: 2 + i for i in range(2 * n)}, compiler_params=_SIDE)(*[_hbm(a) for a in grads + lands])
    _TOKENS.push(out[-1])
    return out[0], out[1], list(out[2:2 + n]), list(out[2 + n:2 + 2 * n])


def rs_d2d_wait(name, started, after):
    send, recv, grads, lands = started
    n = len(grads)

    def body(*refs):
        ins, lnd, send_sems, recv_sems = refs[:n], refs[n:2 * n], refs[2 * n], refs[2 * n + 1]
        x, y, c = _position()
        for w in range(n):
            for i in range(4):
                cp = _rcopy(ins[w].at[2 * i + 1 - c], lnd[w].at[i], send_sems.at[4 * w + i], recv_sems.at[4 * w + i], (x, y, 1 - c))
                cp.wait_send()
                cp.wait_recv()

    out = pl.pallas_call(
        body, name=name, out_shape=tuple(_like(a) for a in grads + lands),
        in_specs=[_HBM] * (2 * n) + [_SEM, _SEM] + [_ANY] * len(_afters(after)),
        out_specs=(_HBM,) * (2 * n), input_output_aliases={i: i for i in range(2 * n)},
        compiler_params=_SIDE)(*grads, *lands, send, recv, *_afters(after))
    return list(out[:n]), list(out[n:])


def pair_sum(name, grad, land, core):
    _, r, c = grad.shape
    tr = r
    if r % 8 == 0:
        tr = max(8, min(r, 4 * ADAMW_TILE_ELEMS // c) // 8 * 8)
        while r % tr:
            tr -= 8

    def body(core_ref, a_ref, b_ref, o_ref):
        o_ref[...] = (a_ref[...].astype(F32) + b_ref[...].astype(F32)).astype(o_ref.dtype)

    return pl.pallas_call(
        body, name=name, out_shape=jax.ShapeDtypeStruct((4, r, c), grad.dtype),
        grid_spec=pltpu.PrefetchScalarGridSpec(
            num_scalar_prefetch=1, grid=(4, r // tr),
            in_specs=[pl.BlockSpec((None, None, tr, c), lambda i, j, core_ref: (i, core_ref[0], j, 0)),
                      pl.BlockSpec((None, tr, c), lambda i, j, core_ref: (i, j, 0))],
            out_specs=pl.BlockSpec((None, tr, c), lambda i, j, core_ref: (i, j, 0))),
        compiler_params=_params("parallel", "parallel"))(core, grad.reshape(4, 2, r, c), land)


def rs_ici_start(name, sums):
    n = len(sums)
    lands = [lax.empty(a.shape, a.dtype) for a in sums]

    def body(*refs):
        ins, lnd, send_sems, recv_sems, token = refs[:n], refs[n:2 * n], refs[2 * n], refs[2 * n + 1], refs[4 * n + 2]
        x, y, c = _position()
        chip = 2 * x + y
        for w in range(n):
            for j, other in enumerate(_other_chips(x, y)):
                _rcopy(ins[w].at[2 * other[0] + other[1]], lnd[w].at[chip], send_sems.at[3 * w + j], recv_sems.at[3 * w + j], (*other, c)).start()
        token[...] = jnp.zeros_like(token)

    out = pl.pallas_call(
        body, name=name, out_shape=(_dma_sems(3 * n), _dma_sems(3 * n)) + tuple(_like(a) for a in sums + lands) + (_TOKEN,),
        in_specs=[_HBM] * (2 * n), out_specs=(_SEM, _SEM) + (_HBM,) * (2 * n) + (_VM,),
        input_output_aliases={i: 2 + i for i in range(2 * n)}, compiler_params=_SIDE)(*[_hbm(a) for a in sums + lands])
    _TOKENS.push(out[-1])
    return out[0], out[1], list(out[2:2 + n]), list(out[2 + n:2 + 2 * n])


def rs_ici_wait(name, started, after):
    send, recv, sums, lands = started
    n = len(sums)

    def body(*refs):
        ins, lnd, send_sems, recv_sems = refs[:n], refs[n:2 * n], refs[2 * n], refs[2 * n + 1]
        x, y, c = _position()
        for w in range(n):
            for j, other in enumerate(_other_chips(x, y)):
                cp = _rcopy(ins[w].at[2 * other[0] + other[1]], lnd[w].at[2 * other[0] + other[1]], send_sems.at[3 * w + j], recv_sems.at[3 * w + j], (*other, c))
                cp.wait_send()
                cp.wait_recv()

    out = pl.pallas_call(
        body, name=name, out_shape=tuple(_like(a) for a in sums + lands), in_specs=[_HBM] * (2 * n) + [_SEM, _SEM, _ANY],
        out_specs=(_HBM,) * (2 * n), input_output_aliases={i: i for i in range(2 * n)}, compiler_params=_SIDE)(*sums, *lands, send, recv, after)
    chip = 2 * lax.axis_index("x") + lax.axis_index("y")
    return [lax.dynamic_update_index_in_dim(land, lax.dynamic_index_in_dim(s, chip, 0, keepdims=False), chip, 0)
            for s, land in zip(out[:n], out[n:])]


def ada_fwd(c, w_ada, b_ada3, conv_w, after):
    d, cs = w_ada.shape

    def body(c_ref, w_ref, b_ref, cw_ref, after_ref, mod_ref, sc_ref, cwa_ref, part_ref, send_sems, recv_sems):
        me = _position()
        my = _index(me)
        cv = c_ref[...]
        sc_ref[my] = cv * _sigmoid(cv)
        cwa_ref[my] = cw_ref[...]
        gather = []
        for r in range(1, N_DEV):
            for k, ref in enumerate((sc_ref, cwa_ref)):
                cp = pltpu.make_async_remote_copy(src_ref=ref.at[my], dst_ref=ref.at[my], send_sem=send_sems.at[14 * k + r - 1],
                                                  recv_sem=recv_sems.at[14 * k + r - 1], device_id=_peer(me, r), device_id_type=MESH)
                cp.start()
                gather.append(cp)
        for cp in gather:
            cp.wait()
        sc_all = jnp.concatenate([sc_ref[k] for k in range(N_DEV)], axis=0).astype(BF16)
        part = jnp.dot(sc_all, w_ref[...].astype(BF16), preferred_element_type=F32)
        for k in range(N_DEV):
            part_ref[k] = part[k:k + 1, :]
        scatter = []
        for r in range(1, N_DEV):
            peer = _peer(me, r)
            cp = pltpu.make_async_remote_copy(src_ref=part_ref.at[_index(peer)], dst_ref=mod_ref.at[my], send_sem=send_sems.at[6 + r],
                                              recv_sem=recv_sems.at[6 + r], device_id=peer, device_id_type=MESH)
            cp.start()
            scatter.append(cp)
        mod_ref[my] = part_ref[my]
        for cp in scatter:
            cp.wait()
        mod_ref[...] = mod_ref[...] + b_ref[...]

    vm = pl.BlockSpec(memory_space=pltpu.VMEM)
    return pl.pallas_call(
        body, name="ada_fwd",
        out_shape=[jax.ShapeDtypeStruct((N_DEV, 1, cs), F32), jax.ShapeDtypeStruct((N_DEV, 1, d), F32),
                   jax.ShapeDtypeStruct((N_DEV,) + conv_w.shape, F32)],
        in_specs=[vm, vm, vm, vm, _ANY], out_specs=[vm, vm, vm],
        scratch_shapes=[pltpu.VMEM((N_DEV, 1, cs), F32), pltpu.SemaphoreType.DMA((21,)), pltpu.SemaphoreType.DMA((21,))],
        compiler_params=pltpu.CompilerParams(vmem_limit_bytes=VMEM_LIMIT_BYTES))(c, w_ada, b_ada3, conv_w, after)


def ada_bwd_w(sc_all, dmod_cols):
    _, d = sc_all.shape
    cs = dmod_cols.shape[1]
    tr = _tile(d, ROW_TILE)

    def body(sc_ref, dm_ref, o_ref):
        dm = dm_ref[...].astype(BF16)
        o_ref[...] = lax.dot_general(sc_ref[...].astype(BF16), dm, (((0,), (0,)), ((), ())), preferred_element_type=F32)

    return pl.pallas_call(body, name="ada_bwd_w", grid=(d // tr,),
                          in_specs=[pl.BlockSpec((N_DEV, tr), lambda i: (0, i)), _full((N_DEV, cs))],
                          out_specs=pl.BlockSpec((None, tr, cs), lambda i: (0, i, 0)),
                          out_shape=jax.ShapeDtypeStruct((1, d, cs), F32), compiler_params=_params("parallel"))(sc_all, dmod_cols)


def _round_up(n, m):
    return (n + m - 1) // m * m


def kernel(x, c, positions, w_ada, b_ada, pre_norm1_g, w_in, gm_ln_g, gm_ln_b, gm_w_s, gm_b_s, w_branch_a, q_norm_g, w_uq, kv_norm_g, w_ukv, w_branch_b, w_out, post_norm1_g, pre_norm2_g, w_up, conv_w, conv_b, w_down, post_norm2_g, loss_target, m_w_ada, m_b_ada, m_pre_norm1_g, m_w_in, m_gm_ln_g, m_gm_ln_b, m_gm_w_s, m_gm_b_s, m_w_branch_a, m_q_norm_g, m_w_uq, m_kv_norm_g, m_w_ukv, m_w_branch_b, m_w_out, m_post_norm1_g, m_pre_norm2_g, m_w_up, m_conv_w, m_conv_b, m_w_down, m_post_norm2_g, v_w_ada, v_b_ada, v_pre_norm1_g, v_w_in, v_gm_ln_g, v_gm_ln_b, v_gm_w_s, v_gm_b_s, v_w_branch_a, v_q_norm_g, v_w_uq, v_kv_norm_g, v_w_ukv, v_w_branch_b, v_w_out, v_post_norm1_g, v_pre_norm2_g, v_w_up, v_conv_w, v_conv_b, v_w_down, v_post_norm2_g):
    weights = dict(w_ada=w_ada, b_ada=b_ada, pre_norm1_g=pre_norm1_g, w_in=w_in, gm_ln_g=gm_ln_g, gm_ln_b=gm_ln_b, gm_w_s=gm_w_s,
                   gm_b_s=gm_b_s, w_branch_a=w_branch_a, q_norm_g=q_norm_g, w_uq=w_uq, kv_norm_g=kv_norm_g, w_ukv=w_ukv,
                   w_branch_b=w_branch_b, w_out=w_out, post_norm1_g=post_norm1_g, pre_norm2_g=pre_norm2_g, w_up=w_up, conv_w=conv_w,
                   conv_b=conv_b, w_down=w_down, post_norm2_g=post_norm2_g)
    mom1 = dict(w_ada=m_w_ada, b_ada=m_b_ada, pre_norm1_g=m_pre_norm1_g, w_in=m_w_in, gm_ln_g=m_gm_ln_g, gm_ln_b=m_gm_ln_b,
                gm_w_s=m_gm_w_s, gm_b_s=m_gm_b_s, w_branch_a=m_w_branch_a, q_norm_g=m_q_norm_g, w_uq=m_w_uq, kv_norm_g=m_kv_norm_g,
                w_ukv=m_w_ukv, w_branch_b=m_w_branch_b, w_out=m_w_out, post_norm1_g=m_post_norm1_g, pre_norm2_g=m_pre_norm2_g,
                w_up=m_w_up, conv_w=m_conv_w, conv_b=m_conv_b, w_down=m_w_down, post_norm2_g=m_post_norm2_g)
    mom2 = dict(w_ada=v_w_ada, b_ada=v_b_ada, pre_norm1_g=v_pre_norm1_g, w_in=v_w_in, gm_ln_g=v_gm_ln_g, gm_ln_b=v_gm_ln_b,
                gm_w_s=v_gm_w_s, gm_b_s=v_gm_b_s, w_branch_a=v_w_branch_a, q_norm_g=v_q_norm_g, w_uq=v_w_uq, kv_norm_g=v_kv_norm_g,
                w_ukv=v_w_ukv, w_branch_b=v_w_branch_b, w_out=v_w_out, post_norm1_g=v_post_norm1_g, pre_norm2_g=v_pre_norm2_g,
                w_up=v_w_up, conv_w=v_conv_w, conv_b=v_conv_b, w_down=v_w_down, post_norm2_g=v_post_norm2_g)
    order = list(weights)
    _TOKENS.clear()

    s, d = x.shape[1], x.shape[2]
    gmw = gm_ln_g.shape[0]
    groups = gmw // CHUNK
    ql, kvl = q_norm_g.shape[0], kv_norm_g.shape[0]
    f2 = conv_b.shape[0]
    in_cols = w_in.shape[1] * N_DEV
    o_q, o_kv, o_ga, o_gb, o_kpe = 2 * gmw, 2 * gmw + ql, 2 * gmw + ql + kvl, 2 * gmw + ql + kvl + d, 2 * gmw + ql + kvl + 2 * d
    zp = _round_up(o_kpe + LANES, Z_PAD)
    src_kpe = 2 * gmw + ql + kvl
    assert src_kpe + QK_ROPE + 2 * d == in_cols
    my = 4 * lax.axis_index("x") + 2 * lax.axis_index("y") + lax.axis_index("c")

    x2, tgt = x[0], loss_target[0]
    row = lambda a: a.reshape(1, -1)

    big = ["w_in", "w_branch_a", "w_uq", "w_ukv", "w_branch_b", "w_out", "w_up", "w_down"]
    sh = {k: weights[k].astype(BF16) for k in big[1:]}
    mix = ["w_branch_a", "w_uq", "w_ukv", "w_branch_b", "w_out"]
    w_in_t = w_in.T.astype(BF16)

    mod8, sc_all3, g_cw = ada_fwd(c, w_ada, b_ada.reshape(N_DEV, 1, -1), conv_w, w_in_t)
    ag_in = relay_start("relay_start_in", [w_in_t], mod8)
    mod = mod8.reshape(N_MOD, d)
    shift1, scale1, gate1, shift2, scale2, gate2 = (mod[i:i + 1] for i in range(N_MOD))
    sc_all = sc_all3.reshape(N_DEV, d)
    h1 = norm_mod_fwd("pre1_fwd", x2, row(pre_norm1_g), scale1, shift1)

    inv = ROPE_THETA ** (-jnp.arange(0, QK_ROPE, 2, dtype=F32) / QK_ROPE)
    ang = positions[0].astype(F32)[:, None] * inv
    cos4 = jnp.tile(jnp.cos(ang), (1, 4))
    sin4 = jnp.tile(jnp.concatenate([-jnp.sin(ang), jnp.sin(ang)], axis=1), (1, 2))

    wm = (gm_w_s * jnp.tril(jnp.ones((CHUNK, CHUNK), F32))).astype(BF16)
    bs3 = gm_b_s.reshape(groups, CHUNK, 1)
    ln_g, ln_b = row(gm_ln_g), row(gm_ln_b)

    small_names = ["pre_norm1_g", "gm_ln_g", "gm_ln_b", "gm_b_s", "q_norm_g", "kv_norm_g", "post_norm1_g", "pre_norm2_g", "conv_b",
                   "post_norm2_g", "gm_w_s", "b_ada"]
    n_small_early = sum(weights[k].size for k in small_names)
    n_pack_early = _round_up(n_small_early + 3 * f2, PACK_ALIGN)

    def pack(src):
        return jnp.concatenate([src[k].reshape(-1) for k in small_names] + [jnp.zeros((n_pack_early - n_small_early,), F32)]).reshape(-1, LANES)

    packed_state = [pack(weights), pack(mom1), pack(mom2)]

    early = [h1, cos4, sin4, wm] + [sh[k] for k in big[1:]] + packed_state
    ag_in = relay_pass("relay_pass_in", ag_in, early)
    ag_in = relay_forward("relay_forward_in", ag_in, _TOKENS.pending[-1])
    ag_mix = ag_start("ag_start_mix", [sh[k] for k in mix], _TOKENS.pending[-1])
    (g_in,) = relay_wait("relay_wait_in", ag_in, [h1, _TOKENS.pending[-1]])
    cs_in = w_in.shape[1]

    def w_in_rows(lo, hi):
        return [g_in[k, max(lo - k * cs_in, 0):min(hi - k * cs_in, cs_in)] for k in range(N_DEV) if lo < (k + 1) * cs_in and hi > k * cs_in]

    w_in_p = jnp.concatenate(w_in_rows(0, src_kpe) + w_in_rows(src_kpe + QK_ROPE, in_cols) + w_in_rows(src_kpe, src_kpe + QK_ROPE)
                             + [jnp.zeros((zp - in_cols, d), BF16)], axis=0)

    z = mm_nt("z_proj", h1, w_in_p, ACT)
    ag_mix = ag_forward("ag_forward_mix", ag_mix, z)
    ag_up = ag_start("ag_start_up", [sh["w_up"]], _TOKENS.pending[-1])
    a = gmlp_fwd(z, gmw, ln_g, ln_b, wm, bs3)
    g_a, g_uq, g_ukv, g_b, g_out = ag_wait("ag_wait_mix", ag_mix, [a, _TOKENS.pending[-1]])
    w_a_f, w_b_f, w_out_f = g_a.reshape(-1, d), g_b.reshape(-1, d), g_out.reshape(-1, d)
    w_uq_f = g_uq.transpose(1, 0, 2).reshape(ql, HEADS, QK_NOPE + QK_ROPE)
    w_uq_n = w_uq_f[:, :, :QK_NOPE].reshape(ql, HEADS * QK_NOPE)
    w_uq_r = w_uq_f[:, :, QK_NOPE:].reshape(ql, HEADS * QK_ROPE)
    y_a = mm_nn("branch_a", a, w_a_f, ACT)
    qln = rms_fwd_cols("q_norm", z, o_q, ql, row(q_norm_g))
    kvn = rms_fwd_cols("kv_norm", z, o_kv, kvl, row(kv_norm_g))
    qn = mm_nn("q_nope", qln, w_uq_n, BF16)
    qp = mm_nn("q_rope", qln, w_uq_r, F32)
    kv = mm_nn_b3("kv_up", kvn, g_ukv, BF16)
    kpr = rope_k(z, o_kpe, cos4, sin4)
    o, qpr, lse = attn_fwd(qn, qp, kv, kpr, cos4, sin4)
    ag_up = ag_forward("ag_forward_up", ag_up, o)
    ag_down = ag_start("ag_start_down", [sh["w_down"]], _TOKENS.pending[-1])
    y_b = mm_nn("branch_b", o, w_b_f, ACT)
    merged = merge_fwd(z, o_ga, o_gb, y_a, y_b)
    y1 = mm_nn("out_proj", merged, w_out_f, ACT)
    x1, h2 = post1_pre2_fwd(x2, y1, gate1, row(post_norm1_g), row(pre_norm2_g), scale2, shift2)
    (g_up,) = ag_wait("ag_wait_up", ag_up, h2)
    upre = mm_nn_b3("up_proj", h2, g_up, ACT)
    ag_down = ag_forward("ag_forward_down", ag_down, upre)
    cw = g_cw.transpose(1, 0, 2).reshape(3, f2)
    cb = row(conv_b)
    f, gv = conv_act_fwd(upre, cw, cb)
    w_down_f = ag_wait("ag_wait_down", ag_down, f)[0].reshape(-1, d)
    ffn = mm_nn("down_proj", f, w_down_f, ACT)
    loss_acc, dout, dffn, acc2 = post2_loss_bwd(x1, ffn, tgt, gate2, row(post_norm2_g))
    loss = lax.psum(loss_acc[0, 0], ("x", "y", "c"))
    _TOKENS.push(jnp.broadcast_to(loss, (8, LANES)))

    blocks = lambda g: g.reshape(N_DEV, g.shape[0] // N_DEV, g.shape[1])
    core = lax.axis_index("c").astype(jnp.int32).reshape(1)
    rs = {}

    def rs_begin(key, grads):
        rs[key] = rs_d2d_start("rs_d2d_start_" + key, grads)

    def rs_middle(key, after):
        grads, lands = rs_d2d_wait("rs_d2d_wait_" + key, rs[key], after)
        sums = [pair_sum("pair_sum_%s_%d" % (key, i), g, l, core) for i, (g, l) in enumerate(zip(grads, lands))]
        rs[key] = rs_ici_start("rs_ici_start_" + key, sums)

    gw_down = mm_tn("g_w_down", f, dffn, BF16)
    rs_begin("down", [blocks(gw_down)])
    df = mm_nt("d_f", dffn, w_down_f, ACT)
    rs_middle("down", df)
    dupre, gcw_g, gcw_v, gcb_g, gcb_v = conv_act_bwd(upre, cw, gv, df)
    gw_up3 = mm_tn_h3("g_w_up", h2, dupre, N_DEV, BF16)
    rs_begin("up", [gw_up3])
    dh2 = mm_nt_h3("d_h2", dupre, g_up, ACT)
    rs_middle("up", dh2)
    dx1, dy1, acc_mid = mid_bwd(dh2, dout, x1, y1, row(pre_norm2_g), scale2, gate1, row(post_norm1_g))
    gw_out = mm_tn("g_w_out", merged, dy1, BF16)
    dmerged = mm_nt("d_merged", dy1, w_out_f, ACT)
    dya, dyb, dga, dgb = merge_bwd(z, o_ga, o_gb, y_a, y_b, dmerged)
    gw_a = mm_tn("g_w_a", a, dya, BF16)
    gw_b = mm_tn("g_w_b", o, dyb, BF16)
    rs_begin("mid", [blocks(gw_out), blocks(gw_a), blocks(gw_b)])
    da = mm_nt("d_a", dya, w_a_f, ACT)
    do = mm_nt("d_o", dyb, w_b_f, ACT)
    rs_middle("mid", do)
    dz, g_ws, g_bs3, acc_gm = gmlp_bwd(z, gmw, da, ln_g, ln_b, wm, bs3, lax.empty((s, zp), BF16))
    dqn, dqp, dkv, dkp = attn_bwd(qn, qpr, kv, kpr, o, do, lse, cos4, sin4)
    dkpe = kpe_bwd(dkp, cos4, sin4, zp - o_kpe)
    dq_cat = jnp.concatenate([dqn, dqp], axis=1)
    w_uq_cat = jnp.concatenate([w_uq_n, w_uq_r], axis=1)
    dqln = mm_nt("d_qln", dq_cat, w_uq_cat, ACT)
    dq_lat, g_qnorm = rms_bwd_cols("q_norm_bwd", dqln, z, o_q, ql, row(q_norm_g))
    dkvn = mm_nt_b3("d_kvn", dkv, g_ukv, ACT)
    dkv_lat, g_kvnorm = rms_bwd_cols("kv_norm_bwd", dkvn, z, o_kv, kvl, row(kv_norm_g))
    for piece, off in ((dq_lat, o_q), (dkv_lat, o_kv), (dga, o_ga), (dgb, o_gb), (dkpe, o_kpe)):
        dz = lax.dynamic_update_slice(dz, piece, (0, off))
    gw_in_p = mm_tn("g_w_in", dz, h1, BF16)

    def gw_in_rows(lo, hi):
        pieces = []
        for a, b, shift in ((0, src_kpe, 0), (src_kpe, src_kpe + QK_ROPE, o_kpe - src_kpe), (src_kpe + QK_ROPE, in_cols, -QK_ROPE)):
            if lo < b and hi > a:
                pieces.append(gw_in_p[max(lo, a) + shift:min(hi, b) + shift])
        return pieces[0] if len(pieces) == 1 else jnp.concatenate(pieces, axis=0)

    rs_begin("in", [jnp.stack([gw_in_rows(k * cs_in, (k + 1) * cs_in) for k in range(N_DEV)])])
    dh1 = mm_nn("d_h1", dz, w_in_p, ACT)
    grad_x, acc1 = pre1_bwd(dh1, dx1, x2, row(pre_norm1_g), scale1)

    dmod = jnp.concatenate([acc1[0], acc1[1], acc_mid[3], acc_mid[0], acc_mid[1], acc2[0]])
    small = [("pre_norm1_g", acc1[2]), ("gm_ln_g", acc_gm[0]), ("gm_ln_b", acc_gm[1]), ("gm_b_s", g_bs3.reshape(-1)),
             ("q_norm_g", g_qnorm[0]), ("kv_norm_g", g_kvnorm[0]), ("post_norm1_g", acc_mid[4]), ("pre_norm2_g", acc_mid[2]),
             ("conv_b", jnp.concatenate([gcb_g[0], gcb_v[0]])), ("post_norm2_g", acc2[1]), ("gm_w_s", g_ws.reshape(-1)),
             ("b_ada", dmod)]
    n_small = sum(v.shape[0] for _, v in small)
    n_cw = 3 * f2
    n_pack = _round_up(n_small + n_cw, PACK_ALIGN)
    tail = jnp.zeros((n_pack - n_small - n_cw,), F32)
    packed = jnp.concatenate([v for _, v in small] + [jnp.concatenate([gcw_g, gcw_v], axis=1).reshape(-1), tail])
    ag_small = ag_start("ag_start_small", [packed.reshape(-1, LANES)], packed)
    rs_middle("in", [packed, _TOKENS.pending[-1]])

    gw_uq_cat = mm_tn("g_w_uq", qln, dq_cat, BF16)
    gw_uq_f = jnp.concatenate([gw_uq_cat[:, :HEADS * QK_NOPE].reshape(ql, HEADS, QK_NOPE),
                               gw_uq_cat[:, HEADS * QK_NOPE:].reshape(ql, HEADS, QK_ROPE)], axis=2)
    gw_uq3 = gw_uq_f.reshape(ql, N_DEV, -1).transpose(1, 0, 2)
    gw_ukv3 = mm_tn_o3("g_w_ukv", kvn, dkv, N_DEV, BF16)
    rs_begin("mla", [gw_uq3, gw_ukv3])

    res = {}
    last = packed
    for key, names in (("down", ["w_down"]), ("up", ["w_up"]), ("mid", ["w_out", "w_branch_a", "w_branch_b"])):
        parts = rs_ici_wait("rs_ici_wait_" + key, rs[key], last)
        for k, p in zip(names, parts):
            res[k] = adamw("adamw_" + k, weights[k], mom1[k], mom2[k], p)
            last = res[k][0]
        if key == "down":
            rs_middle("mla", last)

    assert [k for k, _ in small] == small_names and n_small == n_small_early
    (gathered,) = ag_wait("ag_wait_small", ag_forward("ag_forward_small", ag_small, last), last)
    sm = [t.reshape(-1) for t in adamw("adamw_small", *packed_state, gathered)]
    off = 0
    for k, v in small:
        res[k] = tuple(t[off:off + v.shape[0]].reshape(weights[k].shape) for t in sm)
        off += v.shape[0]

    cs_cw = conv_w.shape[1]
    g_cw_full = sm[0][n_small:n_small + n_cw].reshape(3, f2)
    g_cw_mine = lax.dynamic_slice(g_cw_full, (0, my * cs_cw), (3, cs_cw))
    res["conv_w"] = adamw("adamw_conv_w", conv_w, mom1["conv_w"], mom2["conv_w"], g_cw_mine[None])

    cs_ada = w_ada.shape[1]
    off_b = n_small - N_MOD * d
    dmod_all = gathered.reshape(N_DEV, -1)[:, off_b:off_b + N_MOD * d]
    dmod_cols = lax.dynamic_slice(dmod_all, (0, my * cs_ada), (N_DEV, cs_ada))
    res["w_ada"] = adamw("adamw_w_ada", w_ada, mom1["w_ada"], mom2["w_ada"], ada_bwd_w(sc_all, dmod_cols))

    (p_in,) = rs_ici_wait("rs_ici_wait_in", rs["in"], res["w_ada"][0])
    w_in_res = adamw("adamw_w_in", w_in.T, mom1["w_in"].T, mom2["w_in"].T, p_in)
    res["w_in"] = tuple(t.T for t in w_in_res)
    for k, p in zip(["w_uq", "w_ukv"], rs_ici_wait("rs_ici_wait_mla", rs["mla"], w_in_res[0])):
        res[k] = adamw("adamw_" + k, weights[k], mom1[k], mom2[k], p)

    _TOKENS.clear()
    outs = [loss, grad_x[None]]
    for i in range(4):
        outs += [res[k][i] for k in order]
    return tuple(outs)
```

```python
import jax
import jax.numpy as jnp
from jax import lax
from jax.experimental import pallas as pl
from jax.experimental.pallas import tpu as pltpu

F32 = jnp.float32
BF16 = jnp.bfloat16
ACT = BF16

N_DEV = 8
HEADS = 16
QK_NOPE = 128
QK_ROPE = 64
V_HEAD = 128
CHUNK = 128
ROPE_THETA = 10000.0
EPS = 1e-6
N_MOD = 6
ADAM_LR, ADAM_B1, ADAM_B2, ADAM_EPS, ADAM_WD, ADAM_STEP = 0.001, 0.9, 0.999, 1e-08, 0.01, 10

LANES = 128
VMEM_LIMIT_BYTES = 48 * 2 ** 20
ROW_TILE = 256
COL_TILE = 256
ATT_TILE = 512
Z_PAD = 512
ADAMW_TILE_ELEMS = 1 << 18
PACK_ALIGN = 8 * LANES
MESH = pl.DeviceIdType.MESH


def _params(*sem):
    return pltpu.CompilerParams(dimension_semantics=sem if sem else None, vmem_limit_bytes=VMEM_LIMIT_BYTES)


def _tile(dim, target):
    t = (min(dim, target) // LANES) * LANES
    while t >= LANES:
        if dim % t == 0:
            return t
        t -= LANES
    return dim


def _full(shape):
    nd = len(shape)
    return pl.BlockSpec(shape, lambda *_: (0,) * nd)


class _Tokens:
    KEEP = 2

    def __init__(self):
        self.pending = []

    def push(self, token):
        self.pending = (self.pending + [token])[-self.KEEP:]

    def take(self):
        return list(self.pending)

    def clear(self):
        self.pending = []


_TOKENS = _Tokens()


def _matmul(name, a, b, *, grid, a_spec, b_spec, o_spec, out_shape, contract, acc_shape, split=1):
    nk = grid[2]
    deps = _TOKENS.take()

    def product(a_ref, b_ref):
        if len(b_ref.shape) == 2:
            return lax.dot_general(a_ref[...].astype(BF16), b_ref[...].astype(BF16), (contract, ((), ())), preferred_element_type=F32)
        cs = b_ref.shape[2]
        return sum(lax.dot_general(a_ref[:, s * cs:(s + 1) * cs].astype(BF16), b_ref[s].astype(BF16), (contract, ((), ())),
                                   preferred_element_type=F32) for s in range(split))

    def body_one_step(a_ref, b_ref, *rest):
        o_ref = rest[len(deps)]
        o_ref[...] = product(a_ref, b_ref).astype(o_ref.dtype)

    def body(a_ref, b_ref, *rest):
        o_ref, acc_ref = rest[len(deps):]
        k = pl.program_id(2)

        @pl.when(k == 0)
        def _():
            acc_ref[...] = jnp.zeros_like(acc_ref)

        acc_ref[...] += product(a_ref, b_ref)

        @pl.when(k == nk - 1)
        def _():
            o_ref[...] = acc_ref[...].astype(o_ref.dtype)

    return pl.pallas_call(
        body_one_step if nk == 1 else body, name=name, grid=grid,
        in_specs=[a_spec, b_spec] + [pl.BlockSpec(memory_space=pl.ANY)] * len(deps),
        out_specs=o_spec, out_shape=out_shape, scratch_shapes=[] if nk == 1 else [pltpu.VMEM(acc_shape, F32)],
        compiler_params=_params("parallel", "parallel", "arbitrary"))(a, b, *deps)


T_OUT, T_OUT_WIDE, TK = 1024, 1408, 2816


def _out_tile(dim):
    return T_OUT_WIDE if dim % T_OUT_WIDE == 0 else _tile(dim, T_OUT)


def _tk(a, b):
    return TK if a.dtype == BF16 and b.dtype == BF16 else TK // 2


def mm_nn(name, a, b, dtype):
    (m, k), n = a.shape, b.shape[1]
    tm, tn, tk = _out_tile(m), _out_tile(n), _tile(k, _tk(a, b))
    return _matmul(name, a, b, grid=(m // tm, n // tn, k // tk),
                   a_spec=pl.BlockSpec((tm, tk), lambda i, j, kk: (i, kk)),
                   b_spec=pl.BlockSpec((tk, tn), lambda i, j, kk: (kk, j)),
                   o_spec=pl.BlockSpec((tm, tn), lambda i, j, kk: (i, j)),
                   out_shape=jax.ShapeDtypeStruct((m, n), dtype), contract=((1,), (0,)), acc_shape=(tm, tn))


def mm_nn_b3(name, a, b3, dtype):
    (m, k), (nj, _, cs) = a.shape, b3.shape
    tm, tk = _out_tile(m), _tile(k, _tk(a, b3))
    return _matmul(name, a, b3, grid=(m // tm, nj, k // tk),
                   a_spec=pl.BlockSpec((tm, tk), lambda i, j, kk: (i, kk)),
                   b_spec=pl.BlockSpec((None, tk, cs), lambda i, j, kk: (j, kk, 0)),
                   o_spec=pl.BlockSpec((tm, cs), lambda i, j, kk: (i, j)),
                   out_shape=jax.ShapeDtypeStruct((m, nj * cs), dtype), contract=((1,), (0,)), acc_shape=(tm, cs))


def mm_nt(name, a, b, dtype):
    (m, k), n = a.shape, b.shape[0]
    tm, tn, tk = _out_tile(m), _out_tile(n), _tile(k, _tk(a, b))
    return _matmul(name, a, b, grid=(m // tm, n // tn, k // tk),
                   a_spec=pl.BlockSpec((tm, tk), lambda i, j, kk: (i, kk)),
                   b_spec=pl.BlockSpec((tn, tk), lambda i, j, kk: (j, kk)),
                   o_spec=pl.BlockSpec((tm, tn), lambda i, j, kk: (i, j)),
                   out_shape=jax.ShapeDtypeStruct((m, n), dtype), contract=((1,), (1,)), acc_shape=(tm, tn))


def mm_nt_b3(name, a, b3, dtype):
    m, (nj, n, cs) = a.shape[0], b3.shape
    tm, tn = _out_tile(m), _out_tile(n)
    return _matmul(name, a, b3, grid=(m // tm, n // tn, nj),
                   a_spec=pl.BlockSpec((tm, cs), lambda i, j, kk: (i, kk)),
                   b_spec=pl.BlockSpec((None, tn, cs), lambda i, j, kk: (kk, j, 0)),
                   o_spec=pl.BlockSpec((tm, tn), lambda i, j, kk: (i, j)),
                   out_shape=jax.ShapeDtypeStruct((m, n), dtype), contract=((1,), (1,)), acc_shape=(tm, tn))


def mm_nt_h3(name, a3, b3, dtype):
    (_, m, _), (nj, n, cs) = a3.shape, b3.shape
    tm, tn, hj = _out_tile(m), _out_tile(n), nj // 2
    pair = 2 if hj % 2 == 0 else 1
    return _matmul(name, a3, b3.reshape(nj // pair, pair, n, cs), grid=(m // tm, n // tn, nj // pair),
                   a_spec=pl.BlockSpec((None, tm, pair * cs), lambda i, j, kk: (kk // (hj // pair), i, kk % (hj // pair))),
                   b_spec=pl.BlockSpec((None, pair, tn, cs), lambda i, j, kk: (kk, 0, j, 0)),
                   o_spec=pl.BlockSpec((tm, tn), lambda i, j, kk: (i, j)),
                   out_shape=jax.ShapeDtypeStruct((m, n), dtype), contract=((1,), (1,)), acc_shape=(tm, tn), split=pair)


def mm_tn_h3(name, a, b3, nj, dtype):
    (k, m), half = a.shape, b3.shape[2]
    hj = nj // 2
    cs = half // hj
    tm, tk = _out_tile(m), _tile(k, _tk(a, b3))
    return _matmul(name, a, b3, grid=(m // tm, nj, k // tk),
                   a_spec=pl.BlockSpec((tk, tm), lambda i, j, kk: (kk, i)),
                   b_spec=pl.BlockSpec((None, tk, cs), lambda i, j, kk: (j // hj, kk, j % hj)),
                   o_spec=pl.BlockSpec((None, tm, cs), lambda i, j, kk: (j, i, 0)),
                   out_shape=jax.ShapeDtypeStruct((nj, m, cs), dtype), contract=((0,), (0,)), acc_shape=(tm, cs))


def mm_tn(name, a, b, dtype):
    (k, m), n = a.shape, b.shape[1]
    tm, tn, tk = _out_tile(m), _out_tile(n), _tile(k, _tk(a, b))
    return _matmul(name, a, b, grid=(m // tm, n // tn, k // tk),
                   a_spec=pl.BlockSpec((tk, tm), lambda i, j, kk: (kk, i)),
                   b_spec=pl.BlockSpec((tk, tn), lambda i, j, kk: (kk, j)),
                   o_spec=pl.BlockSpec((tm, tn), lambda i, j, kk: (i, j)),
                   out_shape=jax.ShapeDtypeStruct((m, n), dtype), contract=((0,), (0,)), acc_shape=(tm, tn))


def mm_tn_o3(name, a, b, nj, dtype):
    (k, m), n = a.shape, b.shape[1]
    cs = n // nj
    tm, tk = _out_tile(m), _tile(k, _tk(a, b))
    return _matmul(name, a, b, grid=(m // tm, nj, k // tk),
                   a_spec=pl.BlockSpec((tk, tm), lambda i, j, kk: (kk, i)),
                   b_spec=pl.BlockSpec((tk, cs), lambda i, j, kk: (kk, j)),
                   o_spec=pl.BlockSpec((None, tm, cs), lambda i, j, kk: (j, i, 0)),
                   out_shape=jax.ShapeDtypeStruct((nj, m, cs), dtype), contract=((0,), (0,)), acc_shape=(tm, cs))


_GELU_C = 0.7978845608028654
_GELU_A = 0.044715


def _f32(ref):
    return ref[...].astype(F32)


def _gelu(x):
    x = x.astype(F32)
    return 0.5 * x * (1.0 + jnp.tanh(_GELU_C * (x + _GELU_A * x * x * x)))


def _gelu_and_grad(x):
    x = x.astype(F32)
    t = jnp.tanh(_GELU_C * (x + _GELU_A * x * x * x))
    y = 0.5 * x * (1.0 + t)
    dy = 0.5 * (1.0 + t) + 0.5 * x * (1.0 - t * t) * (_GELU_C * (1.0 + 3.0 * _GELU_A * x * x))
    return y, dy


def _sigmoid(x):
    return 0.5 * jnp.tanh(0.5 * x.astype(F32)) + 0.5


def _rms_stats(x):
    x = x.astype(F32)
    inv = lax.rsqrt(jnp.mean(x * x, axis=-1, keepdims=True) + EPS)
    return inv, x * inv


def _rms_bwd(dyhat, yhat, inv):
    return inv * (dyhat - yhat * jnp.mean(dyhat * yhat, axis=-1, keepdims=True))


def _colsum(x):
    return jnp.sum(x, axis=0, keepdims=True)


def _rope(x, cos4, sin4):
    lane = lax.broadcasted_iota(jnp.int32, x.shape, x.ndim - 1)
    first_half = (lane % QK_ROPE) < (QK_ROPE // 2)
    partner = jnp.where(first_half, pltpu.roll(x, LANES - QK_ROPE // 2, x.ndim - 1), pltpu.roll(x, QK_ROPE // 2, x.ndim - 1))
    return x * cos4 + partner * sin4


def norm_mod_fwd(name, x, g, scale, shift):
    s, d = x.shape
    tr = _tile(s, ROW_TILE)

    def body(x_ref, g_ref, sc_ref, sh_ref, o_ref):
        _, xh = _rms_stats(x_ref[...])
        o_ref[...] = (xh * g_ref[...] * (1.0 + sc_ref[...]) + sh_ref[...]).astype(o_ref.dtype)

    row = pl.BlockSpec((tr, d), lambda i: (i, 0))
    vec = pl.BlockSpec((1, d), lambda i: (0, 0))
    return pl.pallas_call(body, name=name, grid=(s // tr,), in_specs=[row, vec, vec, vec], out_specs=row,
                          out_shape=jax.ShapeDtypeStruct((s, d), BF16), compiler_params=_params("parallel"))(x, g, scale, shift)


def rms_fwd_cols(name, z, off, width, g):
    s = z.shape[0]
    tr = _tile(s, ROW_TILE)
    assert off % width == 0

    def body(x_ref, g_ref, o_ref):
        _, xh = _rms_stats(x_ref[...])
        o_ref[...] = (xh * g_ref[...]).astype(o_ref.dtype)

    return pl.pallas_call(body, name=name, grid=(s // tr,),
                          in_specs=[pl.BlockSpec((tr, width), lambda i: (i, off // width)), pl.BlockSpec((1, width), lambda i: (0, 0))],
                          out_specs=pl.BlockSpec((tr, width), lambda i: (i, 0)),
                          out_shape=jax.ShapeDtypeStruct((s, width), BF16), compiler_params=_params("parallel"))(z, g)


def rms_bwd_cols(name, dy, z, off, width, g):
    s = z.shape[0]
    tr = _tile(s, ROW_TILE)

    def body(dy_ref, x_ref, g_ref, dx_ref, gg_ref):
        @pl.when(pl.program_id(0) == 0)
        def _():
            gg_ref[...] = jnp.zeros_like(gg_ref)

        inv, xh = _rms_stats(x_ref[...])
        dy_v = _f32(dy_ref)
        gg_ref[...] += _colsum(dy_v * xh)
        dx_ref[...] = _rms_bwd(dy_v * g_ref[...], xh, inv).astype(dx_ref.dtype)

    return pl.pallas_call(body, name=name, grid=(s // tr,),
                          in_specs=[pl.BlockSpec((tr, width), lambda i: (i, 0)), pl.BlockSpec((tr, width), lambda i: (i, off // width)),
                                    pl.BlockSpec((1, width), lambda i: (0, 0))],
                          out_specs=[pl.BlockSpec((tr, width), lambda i: (i, 0)), pl.BlockSpec((1, width), lambda i: (0, 0))],
                          out_shape=[jax.ShapeDtypeStruct((s, width), BF16), jax.ShapeDtypeStruct((1, width), F32)],
                          compiler_params=_params("arbitrary"))(dy, z, g)


def post1_pre2_fwd(x, y, gate, g_post, g_pre, scale, shift):
    s, d = x.shape
    tr = _tile(s, ROW_TILE)

    def body(x_ref, y_ref, gate_ref, gp_ref, g_ref, sc_ref, sh_ref, x1_ref, h_ref):
        _, yh = _rms_stats(y_ref[...])
        x1 = x_ref[...] + gate_ref[...] * (yh * gp_ref[...])
        x1_ref[...] = x1
        _, xh = _rms_stats(x1)
        h_ref[...] = (xh * g_ref[...] * (1.0 + sc_ref[...]) + sh_ref[...]).astype(h_ref.dtype)

    row = pl.BlockSpec((tr, d), lambda i: (i, 0))
    vec = pl.BlockSpec((1, d), lambda i: (0, 0))
    return pl.pallas_call(body, name="post1_pre2_fwd", grid=(s // tr,), in_specs=[row, row, vec, vec, vec, vec, vec], out_specs=[row, row],
                          out_shape=[jax.ShapeDtypeStruct((s, d), F32), jax.ShapeDtypeStruct((s, d), BF16)],
                          compiler_params=_params("parallel"))(x, y, gate, g_post, g_pre, scale, shift)


def post2_loss_bwd(x1, ffn, target, gate2, g):
    s, d = x1.shape
    tr = _tile(s, ROW_TILE)

    def body(x_ref, y_ref, t_ref, gate_ref, g_ref, loss_ref, dout_ref, dy_ref, acc_ref):
        @pl.when(pl.program_id(0) == 0)
        def _():
            loss_ref[...] = jnp.zeros_like(loss_ref)
            acc_ref[...] = jnp.zeros_like(acc_ref)

        inv, yh = _rms_stats(y_ref[...])
        r = yh * g_ref[...]
        err = x_ref[...] + gate_ref[...] * r - t_ref[...]
        loss_ref[...] += 0.5 * jnp.sum(jnp.mean(err * err, axis=-1, keepdims=True))
        dout = err / d
        dout_ref[...] = dout
        dr = dout * gate_ref[...]
        acc_ref[0:1, :] += _colsum(dout * r)
        acc_ref[1:2, :] += _colsum(dr * yh)
        dy_ref[...] = _rms_bwd(dr * g_ref[...], yh, inv).astype(dy_ref.dtype)

    row = pl.BlockSpec((tr, d), lambda i: (i, 0))
    vec = pl.BlockSpec((1, d), lambda i: (0, 0))
    return pl.pallas_call(
        body, name="post2_loss_bwd", grid=(s // tr,), in_specs=[row, row, row, vec, vec],
        out_specs=[_full((8, LANES)), row, row, _full((8, d))],
        out_shape=[jax.ShapeDtypeStruct((8, LANES), F32), jax.ShapeDtypeStruct((s, d), F32),
                   jax.ShapeDtypeStruct((s, d), BF16), jax.ShapeDtypeStruct((8, d), F32)],
        compiler_params=_params("arbitrary"))(x1, ffn, target, gate2, g)


def mid_bwd(dh2, dout, x1, y1, pre2_g, scale2, gate1, post1_g):
    s, d = x1.shape
    tr = _tile(s, ROW_TILE)

    def body(dh_ref, dout_ref, x_ref, y_ref, g2_ref, sc_ref, gate_ref, g1_ref, dx_ref, dy_ref, acc_ref):
        @pl.when(pl.program_id(0) == 0)
        def _():
            acc_ref[...] = jnp.zeros_like(acc_ref)

        dh = _f32(dh_ref)
        inv2, xh = _rms_stats(x_ref[...])
        acc_ref[0:1, :] += _colsum(dh)
        acc_ref[1:2, :] += _colsum(dh * (xh * g2_ref[...]))
        t = dh * (1.0 + sc_ref[...])
        acc_ref[2:3, :] += _colsum(t * xh)
        dx1 = dout_ref[...] + _rms_bwd(t * g2_ref[...], xh, inv2)
        dx_ref[...] = dx1
        inv1, yh = _rms_stats(y_ref[...])
        acc_ref[3:4, :] += _colsum(dx1 * (yh * g1_ref[...]))
        dr = dx1 * gate_ref[...]
        acc_ref[4:5, :] += _colsum(dr * yh)
        dy_ref[...] = _rms_bwd(dr * g1_ref[...], yh, inv1).astype(dy_ref.dtype)

    row = pl.BlockSpec((tr, d), lambda i: (i, 0))
    vec = pl.BlockSpec((1, d), lambda i: (0, 0))
    return pl.pallas_call(
        body, name="mid_bwd", grid=(s // tr,), in_specs=[row, row, row, row, vec, vec, vec, vec],
        out_specs=[row, row, _full((8, d))],
        out_shape=[jax.ShapeDtypeStruct((s, d), F32), jax.ShapeDtypeStruct((s, d), BF16), jax.ShapeDtypeStruct((8, d), F32)],
        compiler_params=_params("arbitrary"))(dh2, dout, x1, y1, pre2_g, scale2, gate1, post1_g)


def pre1_bwd(dh1, dx1, x, pre1_g, scale1):
    s, d = x.shape
    tr = _tile(s, ROW_TILE)

    def body(dh_ref, dx1_ref, x_ref, g_ref, sc_ref, dx_ref, acc_ref):
        @pl.when(pl.program_id(0) == 0)
        def _():
            acc_ref[...] = jnp.zeros_like(acc_ref)

        dh = _f32(dh_ref)
        inv, xh = _rms_stats(x_ref[...])
        acc_ref[0:1, :] += _colsum(dh)
        acc_ref[1:2, :] += _colsum(dh * (xh * g_ref[...]))
        t = dh * (1.0 + sc_ref[...])
        acc_ref[2:3, :] += _colsum(t * xh)
        dx_ref[...] = dx1_ref[...] + _rms_bwd(t * g_ref[...], xh, inv)

    row = pl.BlockSpec((tr, d), lambda i: (i, 0))
    vec = pl.BlockSpec((1, d), lambda i: (0, 0))
    return pl.pallas_call(
        body, name="pre1_bwd", grid=(s // tr,), in_specs=[row, row, row, vec, vec], out_specs=[row, _full((8, d))],
        out_shape=[jax.ShapeDtypeStruct((s, d), F32), jax.ShapeDtypeStruct((8, d), F32)],
        compiler_params=_params("arbitrary"))(dh1, dx1, x, pre1_g, scale1)


def _ln_stats(v):
    mu = jnp.mean(v, axis=-1, keepdims=True)
    vc = v - mu
    rstd = lax.rsqrt(jnp.mean(vc * vc, axis=-1, keepdims=True) + EPS)
    return rstd, vc * rstd


def gmlp_fwd(z, width, ln_g, ln_b, wm, bs3):
    s = z.shape[0]
    groups = width // CHUNK

    def body(u_ref, v_ref, g_ref, b_ref, wm_ref, bs_ref, a_ref):
        ug = _gelu(u_ref[...])
        _, vh = _ln_stats(_gelu(v_ref[...]))
        vn = (vh * g_ref[...] + b_ref[...]).astype(BF16)
        for g in range(groups):
            cols = slice(g * CHUNK, (g + 1) * CHUNK)
            mixed = jnp.dot(wm_ref[g], vn[:, cols], preferred_element_type=F32) + bs_ref[g]
            a_ref[:, cols] = (ug[:, cols] * mixed).astype(a_ref.dtype)

    vec = pl.BlockSpec((1, width), lambda n: (0, 0))
    return pl.pallas_call(
        body, name="gmlp_fwd", grid=(s // CHUNK,),
        in_specs=[pl.BlockSpec((CHUNK, width), lambda n: (n, 0)), pl.BlockSpec((CHUNK, width), lambda n: (n, 1)), vec, vec,
                  _full(wm.shape), _full(bs3.shape)],
        out_specs=pl.BlockSpec((CHUNK, width), lambda n: (n, 0)),
        out_shape=jax.ShapeDtypeStruct((s, width), BF16), compiler_params=_params("parallel"))(z, z, ln_g, ln_b, wm, bs3)


def gmlp_bwd(z, width, da, ln_g, ln_b, wm, bs3, dz):
    s = z.shape[0]
    groups = width // CHUNK

    def body(u_ref, v_ref, da_ref, g_ref, b_ref, wm_ref, bs_ref, dz_ref, duv_ref, gw_ref, gb_ref, acc_ref, dvn_ref):
        @pl.when(pl.program_id(0) == 0)
        def _():
            gw_ref[...] = jnp.zeros_like(gw_ref)
            gb_ref[...] = jnp.zeros_like(gb_ref)
            acc_ref[...] = jnp.zeros_like(acc_ref)

        ug, dug = _gelu_and_grad(u_ref[...])
        vg, dvg = _gelu_and_grad(v_ref[...])
        rstd, vh = _ln_stats(vg)
        vn = (vh * g_ref[...] + b_ref[...]).astype(BF16)
        da_v = _f32(da_ref)
        for g in range(groups):
            cols = slice(g * CHUNK, (g + 1) * CHUNK)
            mixed = jnp.dot(wm_ref[g], vn[:, cols], preferred_element_type=F32) + bs_ref[g]
            duv_ref[:, cols] = (da_v[:, cols] * mixed * dug[:, cols]).astype(duv_ref.dtype)
            dm = da_v[:, cols] * ug[:, cols]
            gb_ref[g] += jnp.sum(dm, axis=-1, keepdims=True)
            dmb = dm.astype(BF16)
            gw_ref[g] += lax.dot_general(dmb, vn[:, cols], (((1,), (1,)), ((), ())), preferred_element_type=F32)
            dvn_ref[:, cols] = lax.dot_general(wm_ref[g], dmb, (((0,), (0,)), ((), ())), preferred_element_type=F32)
        dvn = dvn_ref[...]
        acc_ref[0:1, :] += _colsum(dvn * vh)
        acc_ref[1:2, :] += _colsum(dvn)
        dvh = dvn * g_ref[...]
        dv = rstd * (dvh - jnp.mean(dvh, axis=-1, keepdims=True) - vh * jnp.mean(dvh * vh, axis=-1, keepdims=True))
        duv_ref[:, width:] = (dv * dvg).astype(duv_ref.dtype)

        @pl.when(pl.program_id(0) == pl.num_programs(0) - 1)
        def _():
            q = lax.broadcasted_iota(jnp.int32, gw_ref.shape, 1)
            p = lax.broadcasted_iota(jnp.int32, gw_ref.shape, 2)
            gw_ref[...] = jnp.where(p <= q, gw_ref[...], 0.0)

    vec = pl.BlockSpec((1, width), lambda n: (0, 0))
    blk = pl.BlockSpec((CHUNK, width), lambda n: (n, 0))
    return pl.pallas_call(
        body, name="gmlp_bwd", grid=(s // CHUNK,),
        in_specs=[blk, pl.BlockSpec((CHUNK, width), lambda n: (n, 1)), blk, vec, vec, _full(wm.shape), _full(bs3.shape),
                  pl.BlockSpec(memory_space=pl.ANY)],
        out_specs=[pl.BlockSpec((CHUNK, 2 * width), lambda n: (n, 0)), _full(wm.shape), _full(bs3.shape), _full((8, width))],
        out_shape=[jax.ShapeDtypeStruct(dz.shape, dz.dtype), jax.ShapeDtypeStruct(wm.shape, F32),
                   jax.ShapeDtypeStruct(bs3.shape, F32), jax.ShapeDtypeStruct((8, width), F32)],
        scratch_shapes=[pltpu.VMEM((CHUNK, width), F32)], input_output_aliases={7: 0},
        compiler_params=_params("arbitrary"))(z, z, da, ln_g, ln_b, wm, bs3, dz)


def merge_fwd(z, off_a, off_b, ya, yb):
    s, d = ya.shape
    tr, tc = _tile(s, ROW_TILE * 2), _tile(d, COL_TILE)
    assert off_a % tc == 0 and off_b % tc == 0

    def body(ga_ref, gb_ref, ya_ref, yb_ref, o_ref):
        o_ref[...] = (_sigmoid(ga_ref[...]) * _f32(ya_ref) + _sigmoid(gb_ref[...]) * _f32(yb_ref)).astype(o_ref.dtype)

    blk = pl.BlockSpec((tr, tc), lambda i, j: (i, j))
    return pl.pallas_call(
        body, name="merge_fwd", grid=(s // tr, d // tc),
        in_specs=[pl.BlockSpec((tr, tc), lambda i, j: (i, off_a // tc + j)), pl.BlockSpec((tr, tc), lambda i, j: (i, off_b // tc + j)), blk, blk],
        out_specs=blk, out_shape=jax.ShapeDtypeStruct((s, d), BF16), compiler_params=_params("parallel", "parallel"))(z, z, ya, yb)


def merge_bwd(z, off_a, off_b, ya, yb, dm, dz):
    s, d = ya.shape
    tr, tc = _tile(s, ROW_TILE * 2), _tile(d, COL_TILE)
    nc = d // tc

    def body(ga_ref, gb_ref, ya_ref, yb_ref, dm_ref, dz_ref, dya_ref, dyb_ref, dga_ref, dgb_ref):
        dm_v = _f32(dm_ref)
        sa, sb = _sigmoid(ga_ref[...]), _sigmoid(gb_ref[...])
        dya_ref[...] = (dm_v * sa).astype(dya_ref.dtype)
        dyb_ref[...] = (dm_v * sb).astype(dyb_ref.dtype)
        dga_ref[...] = (dm_v * _f32(ya_ref) * sa * (1.0 - sa)).astype(dga_ref.dtype)
        dgb_ref[...] = (dm_v * _f32(yb_ref) * sb * (1.0 - sb)).astype(dgb_ref.dtype)

    blk = pl.BlockSpec((tr, tc), lambda i, j: (i, j))
    out = jax.ShapeDtypeStruct((s, d), BF16)
    return pl.pallas_call(
        body, name="merge_bwd", grid=(s // tr, nc),
        in_specs=[pl.BlockSpec((tr, tc), lambda i, j: (i, off_a // tc + j)), pl.BlockSpec((tr, tc), lambda i, j: (i, off_b // tc + j)), blk, blk, blk,
                  pl.BlockSpec(memory_space=pl.ANY)],
        out_specs=[blk, blk, pl.BlockSpec((tr, tc), lambda i, j: (i, off_a // tc + j)), blk],
        out_shape=[out, out, jax.ShapeDtypeStruct(dz.shape, dz.dtype), out], input_output_aliases={5: 2},
        compiler_params=_params("parallel", "parallel"))(z, z, ya, yb, dm, dz)


_ATT_SCALE = (QK_NOPE + QK_ROPE) ** -0.5
_NEG = -1e30


def rope_k(z, off, cos4, sin4):
    s = z.shape[0]
    tr = _tile(s, ROW_TILE * 2)
    assert off % LANES == 0

    def body(k_ref, c_ref, s_ref, o_ref):
        k = _f32(k_ref)
        k = k + pltpu.roll(k, QK_ROPE, 1)
        o_ref[...] = _rope(k, c_ref[...], s_ref[...]).astype(o_ref.dtype)

    row = pl.BlockSpec((tr, LANES), lambda i: (i, 0))
    return pl.pallas_call(body, name="rope_k", grid=(s // tr,),
                          in_specs=[pl.BlockSpec((tr, LANES), lambda i: (i, off // LANES)), row, row], out_specs=row,
                          out_shape=jax.ShapeDtypeStruct((s, LANES), BF16), compiler_params=_params("parallel"))(z, cos4, sin4)


def _dot_nt(a, b):
    return lax.dot_general(a, b, (((1,), (1,)), ((), ())), preferred_element_type=F32)


def _dot_tn(a, b):
    return lax.dot_general(a, b, (((0,), (0,)), ((), ())), preferred_element_type=F32)


def _q_cat(q_n, qpr, hh):
    lane = lax.broadcasted_iota(jnp.int32, qpr.shape, 1)
    sel = (lane < QK_ROPE) if hh == 0 else (lane >= QK_ROPE)
    return jnp.concatenate([q_n, jnp.where(sel, qpr, jnp.zeros_like(qpr))], axis=1)


def _causal(sc):
    row = lax.broadcasted_iota(jnp.int32, sc.shape, 0)
    col = lax.broadcasted_iota(jnp.int32, sc.shape, 1)
    return jnp.where(col <= row, sc, _NEG)


def attn_fwd(qn, qp, kv, kpr, cos4, sin4):
    s = qn.shape[0]
    hp = HEADS // 2
    t = _tile(s, ATT_TILE)
    nq = s // t

    def body(qn_ref, qp_ref, kv_ref, kp_ref, c_ref, s_ref, o_ref, qpr_ref, l_ref, kcat_ref):
        qi = pl.program_id(1)

        @pl.when(qi == 0)
        def _():
            for hh in range(2):
                kcat_ref[hh, :, 0:QK_NOPE] = kv_ref[:, 2 * hh * QK_NOPE:(2 * hh + 1) * QK_NOPE]
                kcat_ref[hh, :, QK_NOPE:] = kp_ref[...]

        qpr = _rope(qp_ref[...], c_ref[...], s_ref[...]).astype(BF16)
        qpr_ref[...] = qpr
        qcat = [_q_cat(qn_ref[:, hh * QK_NOPE:(hh + 1) * QK_NOPE], qpr, hh) for hh in range(2)]

        def block(kb, carry, diagonal):
            rows = pl.ds(pl.multiple_of(kb * t, t), t)
            out = []
            for hh in range(2):
                m, l, acc = carry[hh]
                sc = _dot_nt(qcat[hh], kcat_ref[hh, rows, :]) * _ATT_SCALE
                if diagonal:
                    sc = _causal(sc)
                m_new = jnp.maximum(m, jnp.max(sc, axis=-1, keepdims=True))
                alpha = jnp.exp(m - m_new)
                p = jnp.exp(sc - m_new)
                l = alpha * l + jnp.sum(p, axis=-1, keepdims=True)
                v = kv_ref[rows, (2 * hh + 1) * QK_NOPE:(2 * hh + 2) * QK_NOPE]
                acc = alpha * acc + jnp.dot(p.astype(BF16), v, preferred_element_type=F32)
                out.append((m_new, l, acc))
            return tuple(out)

        one = (jnp.full((t, 1), _NEG, F32), jnp.zeros((t, 1), F32), jnp.zeros((t, V_HEAD), F32))
        carry = lax.fori_loop(0, qi, lambda kb, cr: block(kb, cr, False), (one, one))
        carry = block(qi, carry, True)
        for hh in range(2):
            m, l, acc = carry[hh]
            o_ref[:, hh * V_HEAD:(hh + 1) * V_HEAD] = (acc / l).astype(o_ref.dtype)
            l_ref[:, hh:hh + 1] = m + jnp.log(l)

    return pl.pallas_call(
        body, name="attn_fwd", grid=(hp, nq),
        in_specs=[pl.BlockSpec((t, 2 * QK_NOPE), lambda h, i: (i, h)), pl.BlockSpec((t, LANES), lambda h, i: (i, h)),
                  pl.BlockSpec((s, 4 * QK_NOPE), lambda h, i: (0, h)), _full((s, LANES)),
                  pl.BlockSpec((t, LANES), lambda h, i: (i, 0)), pl.BlockSpec((t, LANES), lambda h, i: (i, 0))],
        out_specs=[pl.BlockSpec((t, 2 * V_HEAD), lambda h, i: (i, h)), pl.BlockSpec((t, LANES), lambda h, i: (i, h)),
                   pl.BlockSpec((None, t, 2), lambda h, i: (h, i, 0))],
        out_shape=[jax.ShapeDtypeStruct((s, HEADS * V_HEAD), ACT), jax.ShapeDtypeStruct((s, HEADS * QK_ROPE), BF16),
                   jax.ShapeDtypeStruct((hp, s, 2), F32)],
        scratch_shapes=[pltpu.VMEM((2, s, 2 * QK_NOPE), BF16)],
        compiler_params=_params("parallel", "arbitrary"))(qn, qp, kv, kpr, cos4, sin4)


def attn_bwd(qn, qpr, kv, kpr, o, do, lse, cos4, sin4):
    s = qn.shape[0]
    hp = HEADS // 2
    t = _tile(s, ATT_TILE)
    nk = s // t

    def body(qn_ref, qpr_ref, kv_ref, kp_ref, o_ref, do_ref, l_ref, c_ref, s_ref,
             dqn_ref, dqp_ref, dkv_ref, dkp_ref, qcat_ref, dq_ref, delta_ref):
        ki = pl.program_id(1)

        @pl.when(ki == 0)
        def _():
            dq_ref[...] = jnp.zeros_like(dq_ref)
            for hh in range(2):
                qcat_ref[hh] = _q_cat(qn_ref[:, hh * QK_NOPE:(hh + 1) * QK_NOPE], qpr_ref[...], hh)
                cols = slice(hh * V_HEAD, (hh + 1) * V_HEAD)
                delta_ref[hh] = jnp.sum(do_ref[:, cols].astype(F32) * o_ref[:, cols].astype(F32), axis=-1, keepdims=True)

        rows_k = pl.ds(pl.multiple_of(ki * t, t), t)
        kcat = [jnp.concatenate([kv_ref[rows_k, 2 * hh * QK_NOPE:(2 * hh + 1) * QK_NOPE], kp_ref[rows_k, :]], axis=1) for hh in range(2)]
        vs = [kv_ref[rows_k, (2 * hh + 1) * QK_NOPE:(2 * hh + 2) * QK_NOPE] for hh in range(2)]

        def block(qb, carry, diagonal):
            rows = pl.ds(pl.multiple_of(qb * t, t), t)
            out = []
            for hh in range(2):
                dkc, dv = carry[hh]
                q_c = qcat_ref[hh, rows, :]
                do_b = do_ref[rows, hh * V_HEAD:(hh + 1) * V_HEAD].astype(BF16)
                sc = _dot_nt(q_c, kcat[hh]) * _ATT_SCALE
                if diagonal:
                    sc = _causal(sc)
                p = jnp.exp(sc - l_ref[rows, hh:hh + 1])
                dpv = _dot_nt(do_b, vs[hh])
                ds = (p * (dpv - delta_ref[hh, rows, :]) * _ATT_SCALE).astype(BF16)
                dv = dv + _dot_tn(p.astype(BF16), do_b)
                dkc = dkc + _dot_tn(ds, q_c)
                dq_ref[hh, rows, :] += jnp.dot(ds, kcat[hh], preferred_element_type=F32)
                out.append((dkc, dv))
            return tuple(out)

        one = (jnp.zeros((t, 2 * QK_NOPE), F32), jnp.zeros((t, V_HEAD), F32))
        carry = block(ki, (one, one), True)
        carry = lax.fori_loop(ki + 1, nk, lambda qb, cr: block(qb, cr, False), carry)
        dkp = jnp.zeros((t, LANES), F32)
        for hh in range(2):
            dkc, dv = carry[hh]
            dkv_ref[:, 2 * hh * QK_NOPE:(2 * hh + 1) * QK_NOPE] = dkc[:, :QK_NOPE].astype(dkv_ref.dtype)
            dkv_ref[:, (2 * hh + 1) * QK_NOPE:(2 * hh + 2) * QK_NOPE] = dv.astype(dkv_ref.dtype)
            dkp = dkp + dkc[:, QK_NOPE:]
        dkp_ref[...] = dkp

        @pl.when(ki == nk - 1)
        def _():
            lane = lax.broadcasted_iota(jnp.int32, (s, LANES), 1)
            dqp = jnp.where(lane < QK_ROPE, dq_ref[0, :, QK_NOPE:], dq_ref[1, :, QK_NOPE:])
            dqp_ref[...] = _rope(dqp, c_ref[...], -s_ref[...]).astype(dqp_ref.dtype)
            for hh in range(2):
                dqn_ref[:, hh * QK_NOPE:(hh + 1) * QK_NOPE] = dq_ref[hh, :, :QK_NOPE].astype(dqn_ref.dtype)

    qblk = pl.BlockSpec((s, 2 * QK_NOPE), lambda h, i: (0, h))
    pblk = pl.BlockSpec((s, LANES), lambda h, i: (0, h))
    tab = _full((s, LANES))
    return pl.pallas_call(
        body, name="attn_bwd", grid=(hp, nk),
        in_specs=[qblk, pblk, pl.BlockSpec((s, 4 * QK_NOPE), lambda h, i: (0, h)), tab, qblk, qblk,
                  pl.BlockSpec((None, s, 2), lambda h, i: (h, 0, 0)), tab, tab],
        out_specs=[qblk, pblk, pl.BlockSpec((t, 4 * QK_NOPE), lambda h, i: (i, h)), pl.BlockSpec((None, t, LANES), lambda h, i: (h, i, 0))],
        out_shape=[jax.ShapeDtypeStruct((s, HEADS * QK_NOPE), BF16), jax.ShapeDtypeStruct((s, HEADS * QK_ROPE), BF16),
                   jax.ShapeDtypeStruct((s, HEADS * 2 * QK_NOPE), BF16), jax.ShapeDtypeStruct((hp, s, LANES), F32)],
        scratch_shapes=[pltpu.VMEM((2, s, 2 * QK_NOPE), BF16), pltpu.VMEM((2, s, 2 * QK_NOPE), F32), pltpu.VMEM((2, s, 1), F32)],
        compiler_params=_params("parallel", "arbitrary"))(qn, qpr, kv, kpr, o, do, lse, cos4, sin4)


def kpe_bwd(dkp, cos4, sin4, pad_cols):
    hp, s, _ = dkp.shape
    tr = _tile(s, ROW_TILE * 2)

    def body(d_ref, c_ref, s_ref, o_ref):
        tot = d_ref[0]
        for h in range(1, hp):
            tot = tot + d_ref[h]
        tot = tot + pltpu.roll(tot, QK_ROPE, 1)
        lane = lax.broadcasted_iota(jnp.int32, tot.shape, 1)
        dk = jnp.where(lane < QK_ROPE, _rope(tot, c_ref[...], -s_ref[...]), jnp.zeros_like(tot))
        o_ref[...] = jnp.zeros_like(o_ref)
        o_ref[:, 0:LANES] = dk.astype(o_ref.dtype)

    row = pl.BlockSpec((tr, LANES), lambda i: (i, 0))
    return pl.pallas_call(body, name="kpe_bwd", grid=(s // tr,),
                          in_specs=[pl.BlockSpec((hp, tr, LANES), lambda i: (0, i, 0)), row, row],
                          out_specs=pl.BlockSpec((tr, pad_cols), lambda i: (i, 0)),
                          out_shape=jax.ShapeDtypeStruct((s, pad_cols), BF16), compiler_params=_params("parallel"))(dkp, cos4, sin4)


def _shift_down(x, n):
    row = lax.broadcasted_iota(jnp.int32, x.shape, 0)
    return jnp.where(row >= n, pltpu.roll(x, n, 0), jnp.zeros_like(x))


def _shift_up(x, n):
    rows = x.shape[0]
    row = lax.broadcasted_iota(jnp.int32, x.shape, 0)
    return jnp.where(row < rows - n, pltpu.roll(x, rows - n, 0), jnp.zeros_like(x))


def _conv(x, w_ref, b_ref):
    return w_ref[2:3, :] * x + w_ref[1:2, :] * _shift_down(x, 1) + w_ref[0:1, :] * _shift_down(x, 2) + b_ref[...]


def conv_act_fwd(upre, conv_w, conv_b):
    s, f2 = upre.shape
    f = f2 // 2
    tc = _tile(f, COL_TILE)
    nc = f // tc

    def body(ug_ref, uv_ref, wg_ref, wv_ref, bg_ref, bv_ref, o_ref, gv_ref):
        gh = _conv(_f32(ug_ref), wg_ref, bg_ref)
        vh = _conv(_f32(uv_ref), wv_ref, bv_ref)
        o_ref[...] = (gh * _sigmoid(gh) * vh).astype(o_ref.dtype)
        gv_ref[0] = gh.astype(gv_ref.dtype)
        gv_ref[1] = vh.astype(gv_ref.dtype)

    def spec(rows, shift):
        return pl.BlockSpec((rows, tc), lambda j: (0, j + shift))

    return pl.pallas_call(
        body, name="conv_act_fwd", grid=(nc,),
        in_specs=[spec(s, 0), spec(s, nc), spec(3, 0), spec(3, nc), spec(1, 0), spec(1, nc)],
        out_specs=[spec(s, 0), pl.BlockSpec((2, s, tc), lambda j: (0, 0, j))],
        out_shape=[jax.ShapeDtypeStruct((s, f), BF16), jax.ShapeDtypeStruct((2, s, f), ACT)],
        compiler_params=_params("parallel"))(upre, upre, conv_w, conv_w, conv_b, conv_b)


def conv_act_bwd(upre, conv_w, gv, df):
    s, f2 = upre.shape
    f = f2 // 2
    tc = _tile(f, COL_TILE)
    nc = f // tc

    def half(x, d, w_ref, du_ref, which, gw_ref, gb_ref):
        d1, d2 = _shift_up(d, 1), _shift_up(d, 2)
        gb_ref[...] = _colsum(d)
        gw_ref[2:3, :] = _colsum(d * x)
        gw_ref[1:2, :] = _colsum(d1 * x)
        gw_ref[0:1, :] = _colsum(d2 * x)
        du_ref[which] = (w_ref[2:3, :] * d + w_ref[1:2, :] * d1 + w_ref[0:1, :] * d2).astype(du_ref.dtype)

    def body(ug_ref, uv_ref, wg_ref, wv_ref, gv_ref, df_ref, du_ref, gwg_ref, gwv_ref, gbg_ref, gbv_ref):
        xg, xv = _f32(ug_ref), _f32(uv_ref)
        gh, vh = gv_ref[0].astype(F32), gv_ref[1].astype(F32)
        sg = _sigmoid(gh)
        df_v = _f32(df_ref)
        half(xg, df_v * vh * (sg * (1.0 + gh * (1.0 - sg))), wg_ref, du_ref, 0, gwg_ref, gbg_ref)
        half(xv, df_v * (gh * sg), wv_ref, du_ref, 1, gwv_ref, gbv_ref)

    def spec(rows, shift):
        return pl.BlockSpec((rows, tc), lambda j: (0, j + shift))

    gw = jax.ShapeDtypeStruct((3, f), F32)
    gb = jax.ShapeDtypeStruct((1, f), F32)
    return pl.pallas_call(
        body, name="conv_act_bwd", grid=(nc,),
        in_specs=[spec(s, 0), spec(s, nc), spec(3, 0), spec(3, nc), pl.BlockSpec((2, s, tc), lambda j: (0, 0, j)), spec(s, 0)],
        out_specs=[pl.BlockSpec((2, s, tc), lambda j: (0, 0, j)), spec(3, 0), spec(3, 0), spec(1, 0), spec(1, 0)],
        out_shape=[jax.ShapeDtypeStruct((2, s, f), BF16), gw, gw, gb, gb],
        compiler_params=_params("parallel"))(upre, upre, conv_w, conv_w, gv, df)


def _elementwise_tile(r, c, limit):
    if r % 8:
        return r, c
    best = (8, c if c % LANES else LANES)
    for k in (1, 2, 4, 8, 16):
        if k > 1 and c % (LANES * k):
            continue
        tc = c // k
        tr = max(8, min(r, limit // tc) // 8 * 8)
        while r % tr:
            tr -= 8
        if tr * tc <= max(limit, 8 * tc) and tr * tc > best[0] * best[1]:
            best = (tr, tc)
    return best


def adamw(name, w, m, v, parts):
    npart, r, c = parts.shape
    tr, tc = _elementwise_tile(r, c, ADAMW_TILE_ELEMS)
    bc1 = 1.0 - ADAM_B1 ** ADAM_STEP
    bc2 = 1.0 - ADAM_B2 ** ADAM_STEP

    def body(w_ref, m_ref, v_ref, p_ref, g_ref, d_ref, nm_ref, nv_ref):
        g = p_ref[0].astype(F32)
        for k in range(1, npart):
            g = g + p_ref[k].astype(F32)
        m_new = ADAM_B1 * m_ref[...] + (1.0 - ADAM_B1) * g
        v_new = ADAM_B2 * v_ref[...] + (1.0 - ADAM_B2) * (g * g)
        g_ref[...] = g
        nm_ref[...] = m_new
        nv_ref[...] = v_new
        d_ref[...] = -ADAM_LR * ((m_new / bc1) / (jnp.sqrt(v_new / bc2) + ADAM_EPS) + ADAM_WD * w_ref[...])

    deps = _TOKENS.take()
    blk = pl.BlockSpec((tr, tc), lambda i, j: (i, j))
    out = jax.ShapeDtypeStruct((r, c), F32)
    return pl.pallas_call(
        lambda *refs: body(*refs[:4], *refs[4 + len(deps):]), name=name, grid=(r // tr, c // tc),
        in_specs=[blk, blk, blk, pl.BlockSpec((npart, tr, tc), lambda i, j: (0, i, j))] + [pl.BlockSpec(memory_space=pl.ANY)] * len(deps),
        out_specs=[blk, blk, blk, blk], out_shape=[out, out, out, out],
        compiler_params=_params("parallel", "parallel"))(w, m, v, parts, *deps)


def _position():
    return lax.axis_index("x"), lax.axis_index("y"), lax.axis_index("c")


def _index(p):
    return 4 * p[0] + 2 * p[1] + p[2]


def _peer(me, r):
    return (me[0] ^ ((r >> 2) & 1), me[1] ^ ((r >> 1) & 1), me[2] ^ (r & 1))


_ANY = pl.BlockSpec(memory_space=pl.ANY)


_HBM = pl.BlockSpec(memory_space=pltpu.HBM)
_SEM = pl.BlockSpec(memory_space=pltpu.SEMAPHORE)
_EFFECT = pltpu.SideEffectType.DATAFLOW_SIDE_EFFECTING
_TOKEN = jax.ShapeDtypeStruct((8, LANES), F32)
_VM = pl.BlockSpec(memory_space=pltpu.VMEM)
_SIDE = pltpu.CompilerParams(has_side_effects=_EFFECT)


def _hbm(a):
    return pltpu.with_memory_space_constraint(a, pltpu.HBM)


def _like(a):
    return pltpu.HBM(a.shape, a.dtype)


def _dma_sems(n):
    return pltpu.SemaphoreType.DMA((n,))


def _other_chips(x, y):
    return [(1 - x, y), (x, 1 - y), (1 - x, 1 - y)]


COPY_STREAMS = 8


def _row_chunks(src, dst):
    rows = src.shape[0]
    n = COPY_STREAMS
    while n > 1 and rows % (16 * n):
        n //= 2
    r = rows // n
    return [(src.at[pl.ds(i * r, r)], dst.at[pl.ds(i * r, r)]) for i in range(n)]


class _rcopy:
    def __init__(self, src, dst, send_sem, recv_sem, to):
        self.parts = [pltpu.make_async_remote_copy(src_ref=s, dst_ref=d, send_sem=send_sem, recv_sem=recv_sem, device_id=to, device_id_type=MESH)
                      for s, d in _row_chunks(src, dst)]

    def start(self):
        for cp in self.parts:
            cp.start()

    def wait_send(self):
        for cp in self.parts:
            cp.wait_send()

    def wait_recv(self):
        for cp in self.parts:
            cp.wait_recv()


def _afters(after):
    return list(after) if isinstance(after, (list, tuple)) else [after]


def ag_start(name, shards, after):
    n = len(shards)
    lands = [lax.empty((N_DEV,) + a.shape, a.dtype) for a in shards]
    afters = _afters(after)
    na = len(afters)

    def body(*refs):
        ins, lnd, send_sems, recv_sems, token = refs[:n], refs[n:2 * n], refs[2 * n + na], refs[2 * n + na + 1], refs[4 * n + na + 2]
        x, y, c = _position()
        for w in range(n):
            slot = lnd[w].at[_index((x, y, c))]
            for k, to in enumerate([(x, y, 1 - c)] + [(*chip, c) for chip in _other_chips(x, y)]):
                _rcopy(ins[w], slot, send_sems.at[4 * w + k], recv_sems.at[4 * w + k], to).start()
        token[...] = jnp.zeros_like(token)

    out = pl.pallas_call(
        body, name=name, out_shape=(_dma_sems(4 * n), _dma_sems(4 * n)) + tuple(_like(a) for a in shards + lands) + (_TOKEN,),
        in_specs=[_HBM] * (2 * n) + [_ANY] * na, out_specs=(_SEM, _SEM) + (_HBM,) * (2 * n) + (_VM,),
        input_output_aliases={i: 2 + i for i in range(2 * n)}, compiler_params=_SIDE)(*[_hbm(a) for a in shards + lands], *afters)
    _TOKENS.push(out[-1])
    return out[0], out[1], list(out[2:2 + n]), list(out[2 + n:2 + 2 * n])


def _split_rows(ref):
    rows = ref.shape[0]
    h = rows // 32 * 16
    return ref.at[pl.ds(0, h)], ref.at[pl.ds(h, rows - h)]


def relay_start(name, shards, after):
    n = len(shards)
    lands = [lax.empty((N_DEV,) + a.shape, a.dtype) for a in shards]
    afters = _afters(after)
    na = len(afters)

    def body(*refs):
        ins, lnd, send_sems, recv_sems, token = refs[:n], refs[n:2 * n], refs[2 * n + na], refs[2 * n + na + 1], refs[4 * n + na + 2]
        x, y, c = _position()
        for w in range(n):
            slot = lnd[w].at[_index((x, y, c))]
            for k, to in enumerate([(x, y, 1 - c), (1 - x, y, c), (x, 1 - y, c)]):
                _rcopy(ins[w], slot, send_sems.at[3 * w + k], recv_sems.at[3 * w + k], to).start()
        token[...] = jnp.zeros_like(token)

    out = pl.pallas_call(
        body, name=name, out_shape=(_dma_sems(3 * n), _dma_sems(3 * n)) + tuple(_like(a) for a in shards + lands) + (_TOKEN,),
        in_specs=[_HBM] * (2 * n) + [_ANY] * na, out_specs=(_SEM, _SEM) + (_HBM,) * (2 * n) + (_VM,),
        input_output_aliases={i: 2 + i for i in range(2 * n)}, compiler_params=_SIDE)(*[_hbm(a) for a in shards + lands], *afters)
    _TOKENS.push(out[-1])
    return out[0], out[1], list(out[2:2 + n]), list(out[2 + n:2 + 2 * n])


def relay_pass(name, started, after):
    send, recv, shards, lands = started
    n = len(shards)
    afters = _afters(after)
    na = len(afters)

    def body(*refs):
        ins, lnd, send_sems, recv_sems = refs[:n], refs[n:2 * n], refs[2 * n], refs[2 * n + 1]
        fsend, frecv, psend, precv = refs[2 * n + 2 + na:2 * n + 6 + na]
        token = refs[4 * n + 6 + na]
        x, y, c = _position()
        nbrs = [(1 - x, y, c), (x, 1 - y, c)]
        for w in range(n):
            for j, nbr in enumerate(nbrs):
                slot = lnd[w].at[_index(nbr)]
                _rcopy(ins[w], slot, send_sems.at[3 * w + 1 + j], recv_sems.at[3 * w + 1 + j], nbr).wait_recv()
                _rcopy(slot, slot, fsend.at[2 * w + j], frecv.at[2 * w + j], (x, y, 1 - c)).start()
                part = _split_rows(slot)[j]
                _rcopy(part, part, psend.at[2 * w + j], precv.at[2 * w + j], nbrs[1 - j]).start()
        token[...] = jnp.zeros_like(token)

    out = pl.pallas_call(
        body, name=name, out_shape=(_dma_sems(2 * n),) * 4 + tuple(_like(a) for a in shards + lands) + (_TOKEN,),
        in_specs=[_HBM] * (2 * n) + [_SEM, _SEM] + [_ANY] * na, out_specs=(_SEM,) * 4 + (_HBM,) * (2 * n) + (_VM,),
        input_output_aliases={i: 4 + i for i in range(2 * n)}, compiler_params=_SIDE)(*shards, *lands, send, recv, *afters)
    _TOKENS.push(out[-1])
    return (send, recv) + tuple(out[:4]) + (list(out[4:4 + n]), list(out[4 + n:4 + 2 * n]))


def relay_forward(name, passed, after):
    send, recv, fsend, frecv, psend, precv, shards, lands = passed
    n = len(shards)
    afters = _afters(after)
    na = len(afters)

    def body(*refs):
        ins, lnd, precv_r = refs[:n], refs[n:2 * n], refs[2 * n]
        gsend, grecv, token = refs[2 * n + 1 + na], refs[2 * n + 2 + na], refs[4 * n + 3 + na]
        x, y, c = _position()
        for w in range(n):
            slot = lnd[w].at[_index((1 - x, 1 - y, c))]
            for j, part in enumerate(_split_rows(slot)):
                _rcopy(part, part, precv_r.at[2 * w + j], precv_r.at[2 * w + j], (x, y, 1 - c)).wait_recv()
            _rcopy(slot, slot, gsend.at[w], grecv.at[w], (x, y, 1 - c)).start()
        token[...] = jnp.zeros_like(token)

    out = pl.pallas_call(
        body, name=name, out_shape=(_dma_sems(n), _dma_sems(n)) + tuple(_like(a) for a in shards + lands) + (_TOKEN,),
        in_specs=[_HBM] * (2 * n) + [_SEM] + [_ANY] * na, out_specs=(_SEM, _SEM) + (_HBM,) * (2 * n) + (_VM,),
        input_output_aliases={i: 2 + i for i in range(2 * n)}, compiler_params=_SIDE)(*shards, *lands, precv, *afters)
    _TOKENS.push(out[-1])
    return send, recv, fsend, frecv, psend, out[0], out[1], list(out[2:2 + n]), list(out[2 + n:2 + 2 * n])


def relay_wait(name, forwarded, after):
    send, recv, fsend, frecv, psend, gsend, grecv, shards, lands = forwarded
    n = len(shards)

    def body(*refs):
        ins, lnd = refs[:n], refs[n:2 * n]
        send_sems, recv_sems, fsend_r, frecv_r, psend_r, gsend_r, grecv_r = refs[2 * n:2 * n + 7]
        x, y, c = _position()
        sibling = (x, y, 1 - c)
        for w in range(n):
            own = lnd[w].at[_index((x, y, c))]
            _rcopy(ins[w], lnd[w].at[_index(sibling)], send_sems.at[3 * w], recv_sems.at[3 * w], sibling).wait_recv()
            for j, nbr in enumerate([(1 - x, y, 1 - c), (x, 1 - y, 1 - c)]):
                _rcopy(ins[w], lnd[w].at[_index(nbr)], fsend_r.at[2 * w + j], frecv_r.at[2 * w + j], sibling).wait_recv()
            _rcopy(ins[w], lnd[w].at[_index((1 - x, 1 - y, 1 - c))], gsend_r.at[w], grecv_r.at[w], sibling).wait_recv()
            for k in range(3):
                _rcopy(ins[w], own, send_sems.at[3 * w + k], recv_sems.at[3 * w + k], sibling).wait_send()
            for j in range(2):
                _rcopy(ins[w], own, fsend_r.at[2 * w + j], frecv_r.at[2 * w + j], sibling).wait_send()
                part = _split_rows(own)[j]
                _rcopy(part, part, psend_r.at[2 * w + j], psend_r.at[2 * w + j], sibling).wait_send()
            _rcopy(ins[w], own, gsend_r.at[w], grecv_r.at[w], sibling).wait_send()

    out = pl.pallas_call(
        body, name=name, out_shape=tuple(_like(a) for a in shards + lands),
        in_specs=[_HBM] * (2 * n) + [_SEM] * 7 + [_ANY] * len(_afters(after)),
        out_specs=(_HBM,) * (2 * n), input_output_aliases={i: i for i in range(2 * n)},
        compiler_params=_SIDE)(*shards, *lands, send, recv, fsend, frecv, psend, gsend, grecv, *_afters(after))
    return [lax.dynamic_update_index_in_dim(land, shard, _index(_position()), 0) for shard, land in zip(out[:n], out[n:])]


def ag_forward(name, started, after):
    send, recv, shards, lands = started
    n = len(shards)
    afters = list(after) if isinstance(after, (list, tuple)) else [after]
    na = len(afters)

    def body(*refs):
        ins, lnd, send_sems, recv_sems = refs[:n], refs[n:2 * n], refs[2 * n], refs[2 * n + 1]
        fsend, frecv, token = refs[2 * n + 2 + na], refs[2 * n + 3 + na], refs[4 * n + 4 + na]
        x, y, c = _position()
        for w in range(n):
            for j, chip in enumerate(_other_chips(x, y)):
                slot = lnd[w].at[_index((*chip, c))]
                _rcopy(ins[w], slot, send_sems.at[4 * w + 1 + j], recv_sems.at[4 * w + 1 + j], (*chip, c)).wait_recv()
                _rcopy(slot, slot, fsend.at[3 * w + j], frecv.at[3 * w + j], (x, y, 1 - c)).start()
        token[...] = jnp.zeros_like(token)

    out = pl.pallas_call(
        body, name=name, out_shape=(_dma_sems(3 * n), _dma_sems(3 * n)) + tuple(_like(a) for a in shards + lands) + (_TOKEN,),
        in_specs=[_HBM] * (2 * n) + [_SEM, _SEM] + [_ANY] * na, out_specs=(_SEM, _SEM) + (_HBM,) * (2 * n) + (_VM,),
        input_output_aliases={i: 2 + i for i in range(2 * n)}, compiler_params=_SIDE)(*shards, *lands, send, recv, *afters)
    _TOKENS.push(out[-1])
    return send, recv, out[0], out[1], list(out[2:2 + n]), list(out[2 + n:2 + 2 * n])


def ag_wait(name, forwarded, after):
    send, recv, fsend, frecv, shards, lands = forwarded
    n = len(shards)

    def body(*refs):
        ins, lnd, send_sems, recv_sems, fsend_r, frecv_r = refs[:n], refs[n:2 * n], refs[2 * n], refs[2 * n + 1], refs[2 * n + 2], refs[2 * n + 3]
        x, y, c = _position()
        sibling = (x, y, 1 - c)
        for w in range(n):
            own = lnd[w].at[_index((x, y, c))]
            _rcopy(ins[w], lnd[w].at[_index(sibling)], send_sems.at[4 * w], recv_sems.at[4 * w], sibling).wait_recv()
            for j, chip in enumerate(_other_chips(x, y)):
                _rcopy(ins[w], lnd[w].at[_index((*chip, 1 - c))], fsend_r.at[3 * w + j], frecv_r.at[3 * w + j], sibling).wait_recv()
            for k in range(4):
                _rcopy(ins[w], own, send_sems.at[4 * w + k], recv_sems.at[4 * w + k], sibling).wait_send()
            for j in range(3):
                _rcopy(ins[w], own, fsend_r.at[3 * w + j], frecv_r.at[3 * w + j], sibling).wait_send()

    out = pl.pallas_call(
        body, name=name, out_shape=tuple(_like(a) for a in shards + lands),
        in_specs=[_HBM] * (2 * n) + [_SEM] * 4 + [_ANY] * len(_afters(after)),
        out_specs=(_HBM,) * (2 * n), input_output_aliases={i: i for i in range(2 * n)},
        compiler_params=_SIDE)(*shards, *lands, send, recv, fsend, frecv, *_afters(after))
    return [lax.dynamic_update_index_in_dim(land, shard, _index(_position()), 0) for shard, land in zip(out[:n], out[n:])]


def rs_d2d_start(name, grads):
    n = len(grads)
    lands = [lax.empty((4,) + g.shape[1:], g.dtype) for g in grads]

    def body(*refs):
        ins, lnd, send_sems, recv_sems, token = refs[:n], refs[n:2 * n], refs[2 * n], refs[2 * n + 1], refs[4 * n + 2]
        x, y, c = _position()
        for w in range(n):
            for i in range(4):
                _rcopy(ins[w].at[2 * i + 1 - c], lnd[w].at[i], send_sems.at[4 * w + i], recv_sems.at[4 * w + i], (x, y, 1 - c)).start()
        token[...] = jnp.zeros_like(token)

    out = pl.pallas_call(
        body, name=name, out_shape=(_dma_sems(4 * n), _dma_sems(4 * n)) + tuple(_like(a) for a in grads + lands) + (_TOKEN,),
        in_specs=[_HBM] * (2 * n), out_specs=(_SEM, _SEM) + (_HBM,) * (2 * n) + (_VM,),
        input_output_aliases={i: 2 + i for i in range(2 * n)}, compiler_params=_SIDE)(*[_hbm(a) for a in grads + lands])
    _TOKENS.push(out[-1])
    return out[0], out[1], list(out[2:2 + n]), list(out[2 + n:2 + 2 * n])


def rs_d2d_wait(name, started, after):
    send, recv, grads, lands = started
    n = len(grads)

    def body(*refs):
        ins, lnd, send_sems, recv_sems = refs[:n], refs[n:2 * n], refs[2 * n], refs[2 * n + 1]
        x, y, c = _position()
        for w in range(n):
            for i in range(4):
                cp = _rcopy(ins[w].at[2 * i + 1 - c], lnd[w].at[i], send_sems.at[4 * w + i], recv_sems.at[4 * w + i], (x, y, 1 - c))
                cp.wait_send()
                cp.wait_recv()

    out = pl.pallas_call(
        body, name=name, out_shape=tuple(_like(a) for a in grads + lands),
        in_specs=[_HBM] * (2 * n) + [_SEM, _SEM] + [_ANY] * len(_afters(after)),
        out_specs=(_HBM,) * (2 * n), input_output_aliases={i: i for i in range(2 * n)},
        compiler_params=_SIDE)(*grads, *lands, send, recv, *_afters(after))
    return list(out[:n]), list(out[n:])


def pair_sum(name, grad, land, core):
    _, r, c = grad.shape
    tr = r
    if r % 8 == 0:
        tr = max(8, min(r, 4 * ADAMW_TILE_ELEMS // c) // 8 * 8)
        while r % tr:
            tr -= 8

    def body(core_ref, a_ref, b_ref, o_ref):
        o_ref[...] = (a_ref[...].astype(F32) + b_ref[...].astype(F32)).astype(o_ref.dtype)

    return pl.pallas_call(
        body, name=name, out_shape=jax.ShapeDtypeStruct((4, r, c), grad.dtype),
        grid_spec=pltpu.PrefetchScalarGridSpec(
            num_scalar_prefetch=1, grid=(4, r // tr),
            in_specs=[pl.BlockSpec((None, None, tr, c), lambda i, j, core_ref: (i, core_ref[0], j, 0)),
                      pl.BlockSpec((None, tr, c), lambda i, j, core_ref: (i, j, 0))],
            out_specs=pl.BlockSpec((None, tr, c), lambda i, j, core_ref: (i, j, 0))),
        compiler_params=_params("parallel", "parallel"))(core, grad.reshape(4, 2, r, c), land)


def rs_ici_start(name, sums):
    n = len(sums)
    lands = [lax.empty(a.shape, a.dtype) for a in sums]

    def body(*refs):
        ins, lnd, send_sems, recv_sems, token = refs[:n], refs[n:2 * n], refs[2 * n], refs[2 * n + 1], refs[4 * n + 2]
        x, y, c = _position()
        chip = 2 * x + y
        for w in range(n):
            for j, other in enumerate(_other_chips(x, y)):
                _rcopy(ins[w].at[2 * other[0] + other[1]], lnd[w].at[chip], send_sems.at[3 * w + j], recv_sems.at[3 * w + j], (*other, c)).start()
        token[...] = jnp.zeros_like(token)

    out = pl.pallas_call(
        body, name=name, out_shape=(_dma_sems(3 * n), _dma_sems(3 * n)) + tuple(_like(a) for a in sums + lands) + (_TOKEN,),
        in_specs=[_HBM] * (2 * n), out_specs=(_SEM, _SEM) + (_HBM,) * (2 * n) + (_VM,),
        input_output_aliases={i: 2 + i for i in range(2 * n)}, compiler_params=_SIDE)(*[_hbm(a) for a in sums + lands])
    _TOKENS.push(out[-1])
    return out[0], out[1], list(out[2:2 + n]), list(out[2 + n:2 + 2 * n])


def rs_ici_wait(name, started, after):
    send, recv, sums, lands = started
    n = len(sums)

    def body(*refs):
        ins, lnd, send_sems, recv_sems = refs[:n], refs[n:2 * n], refs[2 * n], refs[2 * n + 1]
        x, y, c = _position()
        for w in range(n):
            for j, other in enumerate(_other_chips(x, y)):
                cp = _rcopy(ins[w].at[2 * other[0] + other[1]], lnd[w].at[2 * other[0] + other[1]], send_sems.at[3 * w + j], recv_sems.at[3 * w + j], (*other, c))
                cp.wait_send()
                cp.wait_recv()

    out = pl.pallas_call(
        body, name=name, out_shape=tuple(_like(a) for a in sums + lands), in_specs=[_HBM] * (2 * n) + [_SEM, _SEM, _ANY],
        out_specs=(_HBM,) * (2 * n), input_output_aliases={i: i for i in range(2 * n)}, compiler_params=_SIDE)(*sums, *lands, send, recv, after)
    chip = 2 * lax.axis_index("x") + lax.axis_index("y")
    return [lax.dynamic_update_index_in_dim(land, lax.dynamic_index_in_dim(s, chip, 0, keepdims=False), chip, 0)
            for s, land in zip(out[:n], out[n:])]


def ada_fwd(c, w_ada, b_ada3, conv_w, after):
    d, cs = w_ada.shape

    def body(c_ref, w_ref, b_ref, cw_ref, after_ref, mod_ref, sc_ref, cwa_ref, part_ref, send_sems, recv_sems):
        me = _position()
        my = _index(me)
        cv = c_ref[...]
        sc_ref[my] = cv * _sigmoid(cv)
        cwa_ref[my] = cw_ref[...]
        gather = []
        for r in range(1, N_DEV):
            for k, ref in enumerate((sc_ref, cwa_ref)):
                cp = pltpu.make_async_remote_copy(src_ref=ref.at[my], dst_ref=ref.at[my], send_sem=send_sems.at[14 * k + r - 1],
                                                  recv_sem=recv_sems.at[14 * k + r - 1], device_id=_peer(me, r), device_id_type=MESH)
                cp.start()
                gather.append(cp)
        for cp in gather:
            cp.wait()
        sc_all = jnp.concatenate([sc_ref[k] for k in range(N_DEV)], axis=0).astype(BF16)
        part = jnp.dot(sc_all, w_ref[...].astype(BF16), preferred_element_type=F32)
        for k in range(N_DEV):
            part_ref[k] = part[k:k + 1, :]
        scatter = []
        for r in range(1, N_DEV):
            peer = _peer(me, r)
            cp = pltpu.make_async_remote_copy(src_ref=part_ref.at[_index(peer)], dst_ref=mod_ref.at[my], send_sem=send_sems.at[6 + r],
                                              recv_sem=recv_sems.at[6 + r], device_id=peer, device_id_type=MESH)
            cp.start()
            scatter.append(cp)
        mod_ref[my] = part_ref[my]
        for cp in scatter:
            cp.wait()
        mod_ref[...] = mod_ref[...] + b_ref[...]

    vm = pl.BlockSpec(memory_space=pltpu.VMEM)
    return pl.pallas_call(
        body, name="ada_fwd",
        out_shape=[jax.ShapeDtypeStruct((N_DEV, 1, cs), F32), jax.ShapeDtypeStruct((N_DEV, 1, d), F32),
                   jax.ShapeDtypeStruct((N_DEV,) + conv_w.shape, F32)],
        in_specs=[vm, vm, vm, vm, _ANY], out_specs=[vm, vm, vm],
        scratch_shapes=[pltpu.VMEM((N_DEV, 1, cs), F32), pltpu.SemaphoreType.DMA((21,)), pltpu.SemaphoreType.DMA((21,))],
        compiler_params=pltpu.CompilerParams(vmem_limit_bytes=VMEM_LIMIT_BYTES))(c, w_ada, b_ada3, conv_w, after)


def ada_bwd_w(sc_all, dmod_cols):
    _, d = sc_all.shape
    cs = dmod_cols.shape[1]
    tr = _tile(d, ROW_TILE)

    def body(sc_ref, dm_ref, o_ref):
        dm = dm_ref[...].astype(BF16)
        o_ref[...] = lax.dot_general(sc_ref[...].astype(BF16), dm, (((0,), (0,)), ((), ())), preferred_element_type=F32)

    return pl.pallas_call(body, name="ada_bwd_w", grid=(d // tr,),
                          in_specs=[pl.BlockSpec((N_DEV, tr), lambda i: (0, i)), _full((N_DEV, cs))],
                          out_specs=pl.BlockSpec((None, tr, cs), lambda i: (0, i, 0)),
                          out_shape=jax.ShapeDtypeStruct((1, d, cs), F32), compiler_params=_params("parallel"))(sc_all, dmod_cols)


def _round_up(n, m):
    return (n + m - 1) // m * m


def kernel(x, c, positions, w_ada, b_ada, pre_norm1_g, w_in, gm_ln_g, gm_ln_b, gm_w_s, gm_b_s, w_branch_a, q_norm_g, w_uq, kv_norm_g, w_ukv, w_branch_b, w_out, post_norm1_g, pre_norm2_g, w_up, conv_w, conv_b, w_down, post_norm2_g, loss_target, m_w_ada, m_b_ada, m_pre_norm1_g, m_w_in, m_gm_ln_g, m_gm_ln_b, m_gm_w_s, m_gm_b_s, m_w_branch_a, m_q_norm_g, m_w_uq, m_kv_norm_g, m_w_ukv, m_w_branch_b, m_w_out, m_post_norm1_g, m_pre_norm2_g, m_w_up, m_conv_w, m_conv_b, m_w_down, m_post_norm2_g, v_w_ada, v_b_ada, v_pre_norm1_g, v_w_in, v_gm_ln_g, v_gm_ln_b, v_gm_w_s, v_gm_b_s, v_w_branch_a, v_q_norm_g, v_w_uq, v_kv_norm_g, v_w_ukv, v_w_branch_b, v_w_out, v_post_norm1_g, v_pre_norm2_g, v_w_up, v_conv_w, v_conv_b, v_w_down, v_post_norm2_g):
    weights = dict(w_ada=w_ada, b_ada=b_ada, pre_norm1_g=pre_norm1_g, w_in=w_in, gm_ln_g=gm_ln_g, gm_ln_b=gm_ln_b, gm_w_s=gm_w_s,
                   gm_b_s=gm_b_s, w_branch_a=w_branch_a, q_norm_g=q_norm_g, w_uq=w_uq, kv_norm_g=kv_norm_g, w_ukv=w_ukv,
                   w_branch_b=w_branch_b, w_out=w_out, post_norm1_g=post_norm1_g, pre_norm2_g=pre_norm2_g, w_up=w_up, conv_w=conv_w,
                   conv_b=conv_b, w_down=w_down, post_norm2_g=post_norm2_g)
    mom1 = dict(w_ada=m_w_ada, b_ada=m_b_ada, pre_norm1_g=m_pre_norm1_g, w_in=m_w_in, gm_ln_g=m_gm_ln_g, gm_ln_b=m_gm_ln_b,
                gm_w_s=m_gm_w_s, gm_b_s=m_gm_b_s, w_branch_a=m_w_branch_a, q_norm_g=m_q_norm_g, w_uq=m_w_uq, kv_norm_g=m_kv_norm_g,
                w_ukv=m_w_ukv, w_branch_b=m_w_branch_b, w_out=m_w_out, post_norm1_g=m_post_norm1_g, pre_norm2_g=m_pre_norm2_g,
                w_up=m_w_up, conv_w=m_conv_w, conv_b=m_conv_b, w_down=m_w_down, post_norm2_g=m_post_norm2_g)
    mom2 = dict(w_ada=v_w_ada, b_ada=v_b_ada, pre_norm1_g=v_pre_norm1_g, w_in=v_w_in, gm_ln_g=v_gm_ln_g, gm_ln_b=v_gm_ln_b,
                gm_w_s=v_gm_w_s, gm_b_s=v_gm_b_s, w_branch_a=v_w_branch_a, q_norm_g=v_q_norm_g, w_uq=v_w_uq, kv_norm_g=v_kv_norm_g,
                w_ukv=v_w_ukv, w_branch_b=v_w_branch_b, w_out=v_w_out, post_norm1_g=v_post_norm1_g, pre_norm2_g=v_pre_norm2_g,
                w_up=v_w_up, conv_w=v_conv_w, conv_b=v_conv_b, w_down=v_w_down, post_norm2_g=v_post_norm2_g)
    order = list(weights)
    _TOKENS.clear()

    s, d = x.shape[1], x.shape[2]
    gmw = gm_ln_g.shape[0]
    groups = gmw // CHUNK
    ql, kvl = q_norm_g.shape[0], kv_norm_g.shape[0]
    f2 = conv_b.shape[0]
    in_cols = w_in.shape[1] * N_DEV
    o_q, o_kv, o_ga, o_gb, o_kpe = 2 * gmw, 2 * gmw + ql, 2 * gmw + ql + kvl, 2 * gmw + ql + kvl + d, 2 * gmw + ql + kvl + 2 * d
    zp = _round_up(o_kpe + LANES, Z_PAD)
    src_kpe = 2 * gmw + ql + kvl
    assert src_kpe + QK_ROPE + 2 * d == in_cols
    my = 4 * lax.axis_index("x") + 2 * lax.axis_index("y") + lax.axis_index("c")

    x2, tgt = x[0], loss_target[0]
    row = lambda a: a.reshape(1, -1)

    big = ["w_in", "w_branch_a", "w_uq", "w_ukv", "w_branch_b", "w_out", "w_up", "w_down"]
    sh = {k: weights[k].astype(BF16) for k in big[1:]}
    mix = ["w_branch_a", "w_uq", "w_ukv", "w_branch_b", "w_out"]
    w_in_t = w_in.T.astype(BF16)

    mod8, sc_all3, g_cw = ada_fwd(c, w_ada, b_ada.reshape(N_DEV, 1, -1), conv_w, w_in_t)
    ag_in = relay_start("relay_start_in", [w_in_t], mod8)
    mod = mod8.reshape(N_MOD, d)
    shift1, scale1, gate1, shift2, scale2, gate2 = (mod[i:i + 1] for i in range(N_MOD))
    sc_all = sc_all3.reshape(N_DEV, d)
    h1 = norm_mod_fwd("pre1_fwd", x2, row(pre_norm1_g), scale1, shift1)

    inv = ROPE_THETA ** (-jnp.arange(0, QK_ROPE, 2, dtype=F32) / QK_ROPE)
    ang = positions[0].astype(F32)[:, None] * inv
    cos4 = jnp.tile(jnp.cos(ang), (1, 4))
    sin4 = jnp.tile(jnp.concatenate([-jnp.sin(ang), jnp.sin(ang)], axis=1), (1, 2))

    wm = (gm_w_s * jnp.tril(jnp.ones((CHUNK, CHUNK), F32))).astype(BF16)
    bs3 = gm_b_s.reshape(groups, CHUNK, 1)
    ln_g, ln_b = row(gm_ln_g), row(gm_ln_b)

    small_names = ["pre_norm1_g", "gm_ln_g", "gm_ln_b", "gm_b_s", "q_norm_g", "kv_norm_g", "post_norm1_g", "pre_norm2_g", "conv_b",
                   "post_norm2_g", "gm_w_s", "b_ada"]
    n_small_early = sum(weights[k].size for k in small_names)
    n_pack_early = _round_up(n_small_early + 3 * f2, PACK_ALIGN)

    def pack(src):
        return jnp.concatenate([src[k].reshape(-1) for k in small_names] + [jnp.zeros((n_pack_early - n_small_early,), F32)]).reshape(-1, LANES)

    packed_state = [pack(weights), pack(mom1), pack(mom2)]

    early = [h1, cos4, sin4, wm] + [sh[k] for k in big[1:]] + packed_state
    ag_in = relay_pass("relay_pass_in", ag_in, early)
    ag_in = relay_forward("relay_forward_in", ag_in, _TOKENS.pending[-1])
    ag_mix = ag_start("ag_start_mix", [sh[k] for k in mix], _TOKENS.pending[-1])
    (g_in,) = relay_wait("relay_wait_in", ag_in, [h1, _TOKENS.pending[-1]])
    cs_in = w_in.shape[1]

    def w_in_rows(lo, hi):
        return [g_in[k, max(lo - k * cs_in, 0):min(hi - k * cs_in, cs_in)] for k in range(N_DEV) if lo < (k + 1) * cs_in and hi > k * cs_in]

    w_in_p = jnp.concatenate(w_in_rows(0, src_kpe) + w_in_rows(src_kpe + QK_ROPE, in_cols) + w_in_rows(src_kpe, src_kpe + QK_ROPE)
                             + [jnp.zeros((zp - in_cols, d), BF16)], axis=0)

    z = mm_nt("z_proj", h1, w_in_p, ACT)
    ag_mix = ag_forward("ag_forward_mix", ag_mix, z)
    ag_up = ag_start("ag_start_up", [sh["w_up"]], _TOKENS.pending[-1])
    a = gmlp_fwd(z, gmw, ln_g, ln_b, wm, bs3)
    g_a, g_uq, g_ukv, g_b, g_out = ag_wait("ag_wait_mix", ag_mix, [a, _TOKENS.pending[-1]])
    w_a_f, w_b_f, w_out_f = g_a.reshape(-1, d), g_b.reshape(-1, d), g_out.reshape(-1, d)
    w_uq_f = g_uq.transpose(1, 0, 2).reshape(ql, HEADS, QK_NOPE + QK_ROPE)
    w_uq_n = w_uq_f[:, :, :QK_NOPE].reshape(ql, HEADS * QK_NOPE)
    w_uq_r = w_uq_f[:, :, QK_NOPE:].reshape(ql, HEADS * QK_ROPE)
    y_a = mm_nn("branch_a", a, w_a_f, ACT)
    qln = rms_fwd_cols("q_norm", z, o_q, ql, row(q_norm_g))
    kvn = rms_fwd_cols("kv_norm", z, o_kv, kvl, row(kv_norm_g))
    qn = mm_nn("q_nope", qln, w_uq_n, BF16)
    qp = mm_nn("q_rope", qln, w_uq_r, F32)
    kv = mm_nn_b3("kv_up", kvn, g_ukv, BF16)
    kpr = rope_k(z, o_kpe, cos4, sin4)
    o, qpr, lse = attn_fwd(qn, qp, kv, kpr, cos4, sin4)
    ag_up = ag_forward("ag_forward_up", ag_up, o)
    ag_down = ag_start("ag_start_down", [sh["w_down"]], _TOKENS.pending[-1])
    y_b = mm_nn("branch_b", o, w_b_f, ACT)
    merged = merge_fwd(z, o_ga, o_gb, y_a, y_b)
    y1 = mm_nn("out_proj", merged, w_out_f, ACT)
    x1, h2 = post1_pre2_fwd(x2, y1, gate1, row(post_norm1_g), row(pre_norm2_g), scale2, shift2)
    (g_up,) = ag_wait("ag_wait_up", ag_up, h2)
    upre = mm_nn_b3("up_proj", h2, g_up, ACT)
    ag_down = ag_forward("ag_forward_down", ag_down, upre)
    cw = g_cw.transpose(1, 0, 2).reshape(3, f2)
    cb = row(conv_b)
    f, gv = conv_act_fwd(upre, cw, cb)
    w_down_f = ag_wait("ag_wait_down", ag_down, f)[0].reshape(-1, d)
    ffn = mm_nn("down_proj", f, w_down_f, ACT)
    loss_acc, dout, dffn, acc2 = post2_loss_bwd(x1, ffn, tgt, gate2, row(post_norm2_g))
    loss = lax.psum(loss_acc[0, 0], ("x", "y", "c"))
    _TOKENS.push(jnp.broadcast_to(loss, (8, LANES)))

    blocks = lambda g: g.reshape(N_DEV, g.shape[0] // N_DEV, g.shape[1])
    core = lax.axis_index("c").astype(jnp.int32).reshape(1)
    rs = {}

    def rs_begin(key, grads):
        rs[key] = rs_d2d_start("rs_d2d_start_" + key, grads)

    def rs_middle(key, after):
        grads, lands = rs_d2d_wait("rs_d2d_wait_" + key, rs[key], after)
        sums = [pair_sum("pair_sum_%s_%d" % (key, i), g, l, core) for i, (g, l) in enumerate(zip(grads, lands))]
        rs[key] = rs_ici_start("rs_ici_start_" + key, sums)

    gw_down = mm_tn("g_w_down", f, dffn, BF16)
    rs_begin("down", [blocks(gw_down)])
    df = mm_nt("d_f", dffn, w_down_f, ACT)
    rs_middle("down", df)
    dupre, gcw_g, gcw_v, gcb_g, gcb_v = conv_act_bwd(upre, cw, gv, df)
    gw_up3 = mm_tn_h3("g_w_up", h2, dupre, N_DEV, BF16)
    rs_begin("up", [gw_up3])
    dh2 = mm_nt_h3("d_h2", dupre, g_up, ACT)
    rs_middle("up", dh2)
    dx1, dy1, acc_mid = mid_bwd(dh2, dout, x1, y1, row(pre_norm2_g), scale2, gate1, row(post_norm1_g))
    gw_out = mm_tn("g_w_out", merged, dy1, BF16)
    dmerged = mm_nt("d_merged", dy1, w_out_f, ACT)
    dya, dyb, dz, dgb = merge_bwd(z, o_ga, o_gb, y_a, y_b, dmerged, lax.empty((s, zp), BF16))
    gw_a = mm_tn("g_w_a", a, dya, BF16)
    gw_b = mm_tn("g_w_b", o, dyb, BF16)
    rs_begin("mid", [blocks(gw_out), blocks(gw_a), blocks(gw_b)])
    da = mm_nt("d_a", dya, w_a_f, ACT)
    do = mm_nt("d_o", dyb, w_b_f, ACT)
    rs_middle("mid", do)
    dz, g_ws, g_bs3, acc_gm = gmlp_bwd(z, gmw, da, ln_g, ln_b, wm, bs3, dz)
    dqn, dqp, dkv, dkp = attn_bwd(qn, qpr, kv, kpr, o, do, lse, cos4, sin4)
    dkpe = kpe_bwd(dkp, cos4, sin4, zp - o_kpe)
    dq_cat = jnp.concatenate([dqn, dqp], axis=1)
    w_uq_cat = jnp.concatenate([w_uq_n, w_uq_r], axis=1)
    dqln = mm_nt("d_qln", dq_cat, w_uq_cat, ACT)
    dq_lat, g_qnorm = rms_bwd_cols("q_norm_bwd", dqln, z, o_q, ql, row(q_norm_g))
    dkvn = mm_nt_b3("d_kvn", dkv, g_ukv, ACT)
    dkv_lat, g_kvnorm = rms_bwd_cols("kv_norm_bwd", dkvn, z, o_kv, kvl, row(kv_norm_g))
    for piece, off in ((dq_lat, o_q), (dkv_lat, o_kv), (dgb, o_gb), (dkpe, o_kpe)):
        dz = lax.dynamic_update_slice(dz, piece, (0, off))
    gw_in_p = mm_tn("g_w_in", dz, h1, BF16)

    def gw_in_rows(lo, hi):
        pieces = []
        for a, b, shift in ((0, src_kpe, 0), (src_kpe, src_kpe + QK_ROPE, o_kpe - src_kpe), (src_kpe + QK_ROPE, in_cols, -QK_ROPE)):
            if lo < b and hi > a:
                pieces.append(gw_in_p[max(lo, a) + shift:min(hi, b) + shift])
        return pieces[0] if len(pieces) == 1 else jnp.concatenate(pieces, axis=0)

    rs_begin("in", [jnp.stack([gw_in_rows(k * cs_in, (k + 1) * cs_in) for k in range(N_DEV)])])
    dh1 = mm_nn("d_h1", dz, w_in_p, ACT)
    grad_x, acc1 = pre1_bwd(dh1, dx1, x2, row(pre_norm1_g), scale1)

    dmod = jnp.concatenate([acc1[0], acc1[1], acc_mid[3], acc_mid[0], acc_mid[1], acc2[0]])
    small = [("pre_norm1_g", acc1[2]), ("gm_ln_g", acc_gm[0]), ("gm_ln_b", acc_gm[1]), ("gm_b_s", g_bs3.reshape(-1)),
             ("q_norm_g", g_qnorm[0]), ("kv_norm_g", g_kvnorm[0]), ("post_norm1_g", acc_mid[4]), ("pre_norm2_g", acc_mid[2]),
             ("conv_b", jnp.concatenate([gcb_g[0], gcb_v[0]])), ("post_norm2_g", acc2[1]), ("gm_w_s", g_ws.reshape(-1)),
             ("b_ada", dmod)]
    n_small = sum(v.shape[0] for _, v in small)
    n_cw = 3 * f2
    n_pack = _round_up(n_small + n_cw, PACK_ALIGN)
    tail = jnp.zeros((n_pack - n_small - n_cw,), F32)
    packed = jnp.concatenate([v for _, v in small] + [jnp.concatenate([gcw_g, gcw_v], axis=1).reshape(-1), tail])
    ag_small = ag_start("ag_start_small", [packed.reshape(-1, LANES)], packed)
    rs_middle("in", [packed, _TOKENS.pending[-1]])

    gw_uq_cat = mm_tn("g_w_uq", qln, dq_cat, BF16)
    gw_uq_f = jnp.concatenate([gw_uq_cat[:, :HEADS * QK_NOPE].reshape(ql, HEADS, QK_NOPE),
                               gw_uq_cat[:, HEADS * QK_NOPE:].reshape(ql, HEADS, QK_ROPE)], axis=2)
    gw_uq3 = gw_uq_f.reshape(ql, N_DEV, -1).transpose(1, 0, 2)
    gw_ukv3 = mm_tn_o3("g_w_ukv", kvn, dkv, N_DEV, BF16)
    rs_begin("mla", [gw_uq3, gw_ukv3])

    res = {}
    last = packed
    for key, names in (("down", ["w_down"]), ("up", ["w_up"]), ("mid", ["w_out", "w_branch_a", "w_branch_b"])):
        parts = rs_ici_wait("rs_ici_wait_" + key, rs[key], last)
        for k, p in zip(names, parts):
            res[k] = adamw("adamw_" + k, weights[k], mom1[k], mom2[k], p)
            last = res[k][0]
        if key == "down":
            rs_middle("mla", last)

    assert [k for k, _ in small] == small_names and n_small == n_small_early
    (gathered,) = ag_wait("ag_wait_small", ag_forward("ag_forward_small", ag_small, last), last)
    sm = [t.reshape(-1) for t in adamw("adamw_small", *packed_state, gathered)]
    off = 0
    for k, v in small:
        res[k] = tuple(t[off:off + v.shape[0]].reshape(weights[k].shape) for t in sm)
        off += v.shape[0]

    cs_cw = conv_w.shape[1]
    g_cw_full = sm[0][n_small:n_small + n_cw].reshape(3, f2)
    g_cw_mine = lax.dynamic_slice(g_cw_full, (0, my * cs_cw), (3, cs_cw))
    res["conv_w"] = adamw("adamw_conv_w", conv_w, mom1["conv_w"], mom2["conv_w"], g_cw_mine[None])

    cs_ada = w_ada.shape[1]
    off_b = n_small - N_MOD * d
    dmod_all = gathered.reshape(N_DEV, -1)[:, off_b:off_b + N_MOD * d]
    dmod_cols = lax.dynamic_slice(dmod_all, (0, my * cs_ada), (N_DEV, cs_ada))
    res["w_ada"] = adamw("adamw_w_ada", w_ada, mom1["w_ada"], mom2["w_ada"], ada_bwd_w(sc_all, dmod_cols))

    (p_in,) = rs_ici_wait("rs_ici_wait_in", rs["in"], res["w_ada"][0])
    w_in_res = adamw("adamw_w_in", w_in.T, mom1["w_in"].T, mom2["w_in"].T, p_in)
    res["w_in"] = tuple(t.T for t in w_in_res)
    for k, p in zip(["w_uq", "w_ukv"], rs_ici_wait("rs_ici_wait_mla", rs["mla"], w_in_res[0])):
        res[k] = adamw("adamw_" + k, weights[k], mom1[k], mom2[k], p)

    _TOKENS.clear()
    outs = [loss, grad_x[None]]
    for i in range(4):
        outs += [res[k][i] for k in order]
    return tuple(outs)
```

```python
import jax
import jax.numpy as jnp
from jax import lax
from jax.experimental import pallas as pl
from jax.experimental.pallas import tpu as pltpu

F32 = jnp.float32
BF16 = jnp.bfloat16
ACT = BF16

N_DEV = 8
HEADS = 16
QK_NOPE = 128
QK_ROPE = 64
V_HEAD = 128
CHUNK = 128
ROPE_THETA = 10000.0
EPS = 1e-6
N_MOD = 6
ADAM_LR, ADAM_B1, ADAM_B2, ADAM_EPS, ADAM_WD, ADAM_STEP = 0.001, 0.9, 0.999, 1e-08, 0.01, 10

LANES = 128
VMEM_LIMIT_BYTES = 48 * 2 ** 20
ROW_TILE = 256
COL_TILE = 256
ATT_TILE = 512
Z_PAD = 512
ADAMW_TILE_ELEMS = 1 << 18
PACK_ALIGN = 8 * LANES
MESH = pl.DeviceIdType.MESH


def _params(*sem):
    return pltpu.CompilerParams(dimension_semantics=sem if sem else None, vmem_limit_bytes=VMEM_LIMIT_BYTES)


def _tile(dim, target):
    t = (min(dim, target) // LANES) * LANES
    while t >= LANES:
        if dim % t == 0:
            return t
        t -= LANES
    return dim


def _full(shape):
    nd = len(shape)
    return pl.BlockSpec(shape, lambda *_: (0,) * nd)


class _Tokens:
    KEEP = 2

    def __init__(self):
        self.pending = []

    def push(self, token):
        self.pending = (self.pending + [token])[-self.KEEP:]

    def take(self):
        return list(self.pending)

    def clear(self):
        self.pending = []


_TOKENS = _Tokens()


def _matmul(name, a, b, *, grid, a_spec, b_spec, o_spec, out_shape, contract, acc_shape, split=1):
    nk = grid[2]
    deps = _TOKENS.take()

    def product(a_ref, b_ref):
        if len(b_ref.shape) == 2:
            return lax.dot_general(a_ref[...].astype(BF16), b_ref[...].astype(BF16), (contract, ((), ())), preferred_element_type=F32)
        cs = b_ref.shape[2]
        return sum(lax.dot_general(a_ref[:, s * cs:(s + 1) * cs].astype(BF16), b_ref[s].astype(BF16), (contract, ((), ())),
                                   preferred_element_type=F32) for s in range(split))

    def body_one_step(a_ref, b_ref, *rest):
        o_ref = rest[len(deps)]
        o_ref[...] = product(a_ref, b_ref).astype(o_ref.dtype)

    def body(a_ref, b_ref, *rest):
        o_ref, acc_ref = rest[len(deps):]
        k = pl.program_id(2)

        @pl.when(k == 0)
        def _():
            acc_ref[...] = jnp.zeros_like(acc_ref)

        acc_ref[...] += product(a_ref, b_ref)

        @pl.when(k == nk - 1)
        def _():
            o_ref[...] = acc_ref[...].astype(o_ref.dtype)

    return pl.pallas_call(
        body_one_step if nk == 1 else body, name=name, grid=grid,
        in_specs=[a_spec, b_spec] + [pl.BlockSpec(memory_space=pl.ANY)] * len(deps),
        out_specs=o_spec, out_shape=out_shape, scratch_shapes=[] if nk == 1 else [pltpu.VMEM(acc_shape, F32)],
        compiler_params=_params("parallel", "parallel", "arbitrary"))(a, b, *deps)


T_OUT, T_OUT_WIDE, TK = 1024, 1408, 2816


def _out_tile(dim):
    return T_OUT_WIDE if dim % T_OUT_WIDE == 0 else _tile(dim, T_OUT)


def _tk(a, b):
    return TK if a.dtype == BF16 and b.dtype == BF16 else TK // 2


def mm_nn(name, a, b, dtype):
    (m, k), n = a.shape, b.shape[1]
    tm, tn, tk = _out_tile(m), _out_tile(n), _tile(k, _tk(a, b))
    return _matmul(name, a, b, grid=(m // tm, n // tn, k // tk),
                   a_spec=pl.BlockSpec((tm, tk), lambda i, j, kk: (i, kk)),
                   b_spec=pl.BlockSpec((tk, tn), lambda i, j, kk: (kk, j)),
                   o_spec=pl.BlockSpec((tm, tn), lambda i, j, kk: (i, j)),
                   out_shape=jax.ShapeDtypeStruct((m, n), dtype), contract=((1,), (0,)), acc_shape=(tm, tn))


def mm_nn_b3(name, a, b3, dtype):
    (m, k), (nj, _, cs) = a.shape, b3.shape
    tm, tk = _out_tile(m), _tile(k, _tk(a, b3))
    return _matmul(name, a, b3, grid=(m // tm, nj, k // tk),
                   a_spec=pl.BlockSpec((tm, tk), lambda i, j, kk: (i, kk)),
                   b_spec=pl.BlockSpec((None, tk, cs), lambda i, j, kk: (j, kk, 0)),
                   o_spec=pl.BlockSpec((tm, cs), lambda i, j, kk: (i, j)),
                   out_shape=jax.ShapeDtypeStruct((m, nj * cs), dtype), contract=((1,), (0,)), acc_shape=(tm, cs))


def mm_nt(name, a, b, dtype):
    (m, k), n = a.shape, b.shape[0]
    tm, tn, tk = _out_tile(m), _out_tile(n), _tile(k, _tk(a, b))
    return _matmul(name, a, b, grid=(m // tm, n // tn, k // tk),
                   a_spec=pl.BlockSpec((tm, tk), lambda i, j, kk: (i, kk)),
                   b_spec=pl.BlockSpec((tn, tk), lambda i, j, kk: (j, kk)),
                   o_spec=pl.BlockSpec((tm, tn), lambda i, j, kk: (i, j)),
                   out_shape=jax.ShapeDtypeStruct((m, n), dtype), contract=((1,), (1,)), acc_shape=(tm, tn))


def mm_nt_b3(name, a, b3, dtype):
    m, (nj, n, cs) = a.shape[0], b3.shape
    tm, tn = _out_tile(m), _out_tile(n)
    return _matmul(name, a, b3, grid=(m // tm, n // tn, nj),
                   a_spec=pl.BlockSpec((tm, cs), lambda i, j, kk: (i, kk)),
                   b_spec=pl.BlockSpec((None, tn, cs), lambda i, j, kk: (kk, j, 0)),
                   o_spec=pl.BlockSpec((tm, tn), lambda i, j, kk: (i, j)),
                   out_shape=jax.ShapeDtypeStruct((m, n), dtype), contract=((1,), (1,)), acc_shape=(tm, tn))


def mm_nt_h3(name, a3, b3, dtype):
    (_, m, _), (nj, n, cs) = a3.shape, b3.shape
    tm, tn, hj = _out_tile(m), _out_tile(n), nj // 2
    pair = 2 if hj % 2 == 0 else 1
    return _matmul(name, a3, b3.reshape(nj // pair, pair, n, cs), grid=(m // tm, n // tn, nj // pair),
                   a_spec=pl.BlockSpec((None, tm, pair * cs), lambda i, j, kk: (kk // (hj // pair), i, kk % (hj // pair))),
                   b_spec=pl.BlockSpec((None, pair, tn, cs), lambda i, j, kk: (kk, 0, j, 0)),
                   o_spec=pl.BlockSpec((tm, tn), lambda i, j, kk: (i, j)),
                   out_shape=jax.ShapeDtypeStruct((m, n), dtype), contract=((1,), (1,)), acc_shape=(tm, tn), split=pair)


def mm_tn_h3(name, a, b3, nj, dtype):
    (k, m), half = a.shape, b3.shape[2]
    hj = nj // 2
    cs = half // hj
    tm, tk = _out_tile(m), _tile(k, _tk(a, b3))
    return _matmul(name, a, b3, grid=(m // tm, nj, k // tk),
                   a_spec=pl.BlockSpec((tk, tm), lambda i, j, kk: (kk, i)),
                   b_spec=pl.BlockSpec((None, tk, cs), lambda i, j, kk: (j // hj, kk, j % hj)),
                   o_spec=pl.BlockSpec((None, tm, cs), lambda i, j, kk: (j, i, 0)),
                   out_shape=jax.ShapeDtypeStruct((nj, m, cs), dtype), contract=((0,), (0,)), acc_shape=(tm, cs))


def mm_tn(name, a, b, dtype):
    (k, m), n = a.shape, b.shape[1]
    tm, tn, tk = _out_tile(m), _out_tile(n), _tile(k, _tk(a, b))
    return _matmul(name, a, b, grid=(m // tm, n // tn, k // tk),
                   a_spec=pl.BlockSpec((tk, tm), lambda i, j, kk: (kk, i)),
                   b_spec=pl.BlockSpec((tk, tn), lambda i, j, kk: (kk, j)),
                   o_spec=pl.BlockSpec((tm, tn), lambda i, j, kk: (i, j)),
                   out_shape=jax.ShapeDtypeStruct((m, n), dtype), contract=((0,), (0,)), acc_shape=(tm, tn))


def mm_tn_o3(name, a, b, nj, dtype):
    (k, m), n = a.shape, b.shape[1]
    cs = n // nj
    tm, tk = _out_tile(m), _tile(k, _tk(a, b))
    return _matmul(name, a, b, grid=(m // tm, nj, k // tk),
                   a_spec=pl.BlockSpec((tk, tm), lambda i, j, kk: (kk, i)),
                   b_spec=pl.BlockSpec((tk, cs), lambda i, j, kk: (kk, j)),
                   o_spec=pl.BlockSpec((None, tm, cs), lambda i, j, kk: (j, i, 0)),
                   out_shape=jax.ShapeDtypeStruct((nj, m, cs), dtype), contract=((0,), (0,)), acc_shape=(tm, cs))


_GELU_C = 0.7978845608028654
_GELU_A = 0.044715


def _f32(ref):
    return ref[...].astype(F32)


def _gelu(x):
    x = x.astype(F32)
    hx = 0.5 * x
    return hx + hx * jnp.tanh(x * (_GELU_C + (_GELU_C * _GELU_A) * (x * x)))


def _gelu_and_grad(x):
    x = x.astype(F32)
    x2 = x * x
    t = jnp.tanh(x * (_GELU_C + (_GELU_C * _GELU_A) * x2))
    hx = 0.5 * x
    ht = 0.5 * t
    y = hx + hx * t
    dy = (0.5 + ht) + (hx - hx * (t * t)) * (_GELU_C + (3.0 * _GELU_C * _GELU_A) * x2)
    return y, dy


def _sigmoid(x):
    return 0.5 * jnp.tanh(0.5 * x.astype(F32)) + 0.5


def _rms_stats(x):
    x = x.astype(F32)
    inv = lax.rsqrt(jnp.mean(x * x, axis=-1, keepdims=True) + EPS)
    return inv, x * inv


def _rms_bwd(dyhat, yhat, inv):
    return inv * (dyhat - yhat * jnp.mean(dyhat * yhat, axis=-1, keepdims=True))


def _colsum(x):
    return jnp.sum(x, axis=0, keepdims=True)


def _rope(x, cos4, sin4):
    lane = lax.broadcasted_iota(jnp.int32, x.shape, x.ndim - 1)
    first_half = (lane % QK_ROPE) < (QK_ROPE // 2)
    partner = jnp.where(first_half, pltpu.roll(x, LANES - QK_ROPE // 2, x.ndim - 1), pltpu.roll(x, QK_ROPE // 2, x.ndim - 1))
    return x * cos4 + partner * sin4


def norm_mod_fwd(name, x, g, scale, shift):
    s, d = x.shape
    tr = _tile(s, ROW_TILE)

    def body(x_ref, g_ref, sc_ref, sh_ref, o_ref):
        _, xh = _rms_stats(x_ref[...])
        o_ref[...] = (xh * g_ref[...] * (1.0 + sc_ref[...]) + sh_ref[...]).astype(o_ref.dtype)

    row = pl.BlockSpec((tr, d), lambda i: (i, 0))
    vec = pl.BlockSpec((1, d), lambda i: (0, 0))
    return pl.pallas_call(body, name=name, grid=(s // tr,), in_specs=[row, vec, vec, vec], out_specs=row,
                          out_shape=jax.ShapeDtypeStruct((s, d), BF16), compiler_params=_params("parallel"))(x, g, scale, shift)


def rms_fwd_cols(name, z, off, width, g):
    s = z.shape[0]
    tr = _tile(s, ROW_TILE)
    assert off % width == 0

    def body(x_ref, g_ref, o_ref):
        _, xh = _rms_stats(x_ref[...])
        o_ref[...] = (xh * g_ref[...]).astype(o_ref.dtype)

    return pl.pallas_call(body, name=name, grid=(s // tr,),
                          in_specs=[pl.BlockSpec((tr, width), lambda i: (i, off // width)), pl.BlockSpec((1, width), lambda i: (0, 0))],
                          out_specs=pl.BlockSpec((tr, width), lambda i: (i, 0)),
                          out_shape=jax.ShapeDtypeStruct((s, width), BF16), compiler_params=_params("parallel"))(z, g)


def rms_bwd_cols(name, dy, z, off, width, g):
    s = z.shape[0]
    tr = _tile(s, ROW_TILE)

    def body(dy_ref, x_ref, g_ref, dx_ref, gg_ref):
        @pl.when(pl.program_id(0) == 0)
        def _():
            gg_ref[...] = jnp.zeros_like(gg_ref)

        inv, xh = _rms_stats(x_ref[...])
        dy_v = _f32(dy_ref)
        gg_ref[...] += _colsum(dy_v * xh)
        dx_ref[...] = _rms_bwd(dy_v * g_ref[...], xh, inv).astype(dx_ref.dtype)

    return pl.pallas_call(body, name=name, grid=(s // tr,),
                          in_specs=[pl.BlockSpec((tr, width), lambda i: (i, 0)), pl.BlockSpec((tr, width), lambda i: (i, off // width)),
                                    pl.BlockSpec((1, width), lambda i: (0, 0))],
                          out_specs=[pl.BlockSpec((tr, width), lambda i: (i, 0)), pl.BlockSpec((1, width), lambda i: (0, 0))],
                          out_shape=[jax.ShapeDtypeStruct((s, width), BF16), jax.ShapeDtypeStruct((1, width), F32)],
                          compiler_params=_params("arbitrary"))(dy, z, g)


def post1_pre2_fwd(x, y, gate, g_post, g_pre, scale, shift):
    s, d = x.shape
    tr = _tile(s, ROW_TILE)

    def body(x_ref, y_ref, gate_ref, gp_ref, g_ref, sc_ref, sh_ref, x1_ref, h_ref):
        _, yh = _rms_stats(y_ref[...])
        x1 = x_ref[...] + gate_ref[...] * (yh * gp_ref[...])
        x1_ref[...] = x1
        _, xh = _rms_stats(x1)
        h_ref[...] = (xh * g_ref[...] * (1.0 + sc_ref[...]) + sh_ref[...]).astype(h_ref.dtype)

    row = pl.BlockSpec((tr, d), lambda i: (i, 0))
    vec = pl.BlockSpec((1, d), lambda i: (0, 0))
    return pl.pallas_call(body, name="post1_pre2_fwd", grid=(s // tr,), in_specs=[row, row, vec, vec, vec, vec, vec], out_specs=[row, row],
                          out_shape=[jax.ShapeDtypeStruct((s, d), F32), jax.ShapeDtypeStruct((s, d), BF16)],
                          compiler_params=_params("parallel"))(x, y, gate, g_post, g_pre, scale, shift)


def post2_loss_bwd(x1, ffn, target, gate2, g):
    s, d = x1.shape
    tr = _tile(s, ROW_TILE)

    def body(x_ref, y_ref, t_ref, gate_ref, g_ref, loss_ref, dout_ref, dy_ref, acc_ref):
        @pl.when(pl.program_id(0) == 0)
        def _():
            loss_ref[...] = jnp.zeros_like(loss_ref)
            acc_ref[...] = jnp.zeros_like(acc_ref)

        inv, yh = _rms_stats(y_ref[...])
        r = yh * g_ref[...]
        err = x_ref[...] + gate_ref[...] * r - t_ref[...]
        loss_ref[...] += 0.5 * jnp.sum(jnp.mean(err * err, axis=-1, keepdims=True))
        dout = err / d
        dout_ref[...] = dout
        dr = dout * gate_ref[...]
        acc_ref[0:1, :] += _colsum(dout * r)
        acc_ref[1:2, :] += _colsum(dr * yh)
        dy_ref[...] = _rms_bwd(dr * g_ref[...], yh, inv).astype(dy_ref.dtype)

    row = pl.BlockSpec((tr, d), lambda i: (i, 0))
    vec = pl.BlockSpec((1, d), lambda i: (0, 0))
    return pl.pallas_call(
        body, name="post2_loss_bwd", grid=(s // tr,), in_specs=[row, row, row, vec, vec],
        out_specs=[_full((8, LANES)), row, row, _full((8, d))],
        out_shape=[jax.ShapeDtypeStruct((8, LANES), F32), jax.ShapeDtypeStruct((s, d), F32),
                   jax.ShapeDtypeStruct((s, d), BF16), jax.ShapeDtypeStruct((8, d), F32)],
        compiler_params=_params("arbitrary"))(x1, ffn, target, gate2, g)


def mid_bwd(dh2, dout, x1, y1, pre2_g, scale2, gate1, post1_g):
    s, d = x1.shape
    tr = _tile(s, ROW_TILE)

    def body(dh_ref, dout_ref, x_ref, y_ref, g2_ref, sc_ref, gate_ref, g1_ref, dx_ref, dy_ref, acc_ref):
        @pl.when(pl.program_id(0) == 0)
        def _():
            acc_ref[...] = jnp.zeros_like(acc_ref)

        dh = _f32(dh_ref)
        inv2, xh = _rms_stats(x_ref[...])
        acc_ref[0:1, :] += _colsum(dh)
        acc_ref[1:2, :] += _colsum(dh * (xh * g2_ref[...]))
        t = dh * (1.0 + sc_ref[...])
        acc_ref[2:3, :] += _colsum(t * xh)
        dx1 = dout_ref[...] + _rms_bwd(t * g2_ref[...], xh, inv2)
        dx_ref[...] = dx1
        inv1, yh = _rms_stats(y_ref[...])
        acc_ref[3:4, :] += _colsum(dx1 * (yh * g1_ref[...]))
        dr = dx1 * gate_ref[...]
        acc_ref[4:5, :] += _colsum(dr * yh)
        dy_ref[...] = _rms_bwd(dr * g1_ref[...], yh, inv1).astype(dy_ref.dtype)

    row = pl.BlockSpec((tr, d), lambda i: (i, 0))
    vec = pl.BlockSpec((1, d), lambda i: (0, 0))
    return pl.pallas_call(
        body, name="mid_bwd", grid=(s // tr,), in_specs=[row, row, row, row, vec, vec, vec, vec],
        out_specs=[row, row, _full((8, d))],
        out_shape=[jax.ShapeDtypeStruct((s, d), F32), jax.ShapeDtypeStruct((s, d), BF16), jax.ShapeDtypeStruct((8, d), F32)],
        compiler_params=_params("arbitrary"))(dh2, dout, x1, y1, pre2_g, scale2, gate1, post1_g)


def pre1_bwd(dh1, dx1, x, pre1_g, scale1):
    s, d = x.shape
    tr = _tile(s, ROW_TILE)

    def body(dh_ref, dx1_ref, x_ref, g_ref, sc_ref, dx_ref, acc_ref):
        @pl.when(pl.program_id(0) == 0)
        def _():
            acc_ref[...] = jnp.zeros_like(acc_ref)

        dh = _f32(dh_ref)
        inv, xh = _rms_stats(x_ref[...])
        acc_ref[0:1, :] += _colsum(dh)
        acc_ref[1:2, :] += _colsum(dh * (xh * g_ref[...]))
        t = dh * (1.0 + sc_ref[...])
        acc_ref[2:3, :] += _colsum(t * xh)
        dx_ref[...] = dx1_ref[...] + _rms_bwd(t * g_ref[...], xh, inv)

    row = pl.BlockSpec((tr, d), lambda i: (i, 0))
    vec = pl.BlockSpec((1, d), lambda i: (0, 0))
    return pl.pallas_call(
        body, name="pre1_bwd", grid=(s // tr,), in_specs=[row, row, row, vec, vec], out_specs=[row, _full((8, d))],
        out_shape=[jax.ShapeDtypeStruct((s, d), F32), jax.ShapeDtypeStruct((8, d), F32)],
        compiler_params=_params("arbitrary"))(dh1, dx1, x, pre1_g, scale1)


def _ln_stats(v):
    mu = jnp.mean(v, axis=-1, keepdims=True)
    vc = v - mu
    rstd = lax.rsqrt(jnp.mean(vc * vc, axis=-1, keepdims=True) + EPS)
    return rstd, vc * rstd


def gmlp_fwd(z, width, ln_g, ln_b, wm, bs3):
    s = z.shape[0]
    groups = width // CHUNK

    def body(u_ref, v_ref, g_ref, b_ref, wm_ref, bs_ref, a_ref):
        ug = _gelu(u_ref[...])
        _, vh = _ln_stats(_gelu(v_ref[...]))
        vn = (vh * g_ref[...] + b_ref[...]).astype(BF16)
        for g in range(groups):
            cols = slice(g * CHUNK, (g + 1) * CHUNK)
            mixed = jnp.dot(wm_ref[g], vn[:, cols], preferred_element_type=F32) + bs_ref[g]
            a_ref[:, cols] = (ug[:, cols] * mixed).astype(a_ref.dtype)

    vec = pl.BlockSpec((1, width), lambda n: (0, 0))
    return pl.pallas_call(
        body, name="gmlp_fwd", grid=(s // CHUNK,),
        in_specs=[pl.BlockSpec((CHUNK, width), lambda n: (n, 0)), pl.BlockSpec((CHUNK, width), lambda n: (n, 1)), vec, vec,
                  _full(wm.shape), _full(bs3.shape)],
        out_specs=pl.BlockSpec((CHUNK, width), lambda n: (n, 0)),
        out_shape=jax.ShapeDtypeStruct((s, width), BF16), compiler_params=_params("parallel"))(z, z, ln_g, ln_b, wm, bs3)


def gmlp_bwd(z, width, da, ln_g, ln_b, wm, bs3, dz):
    s = z.shape[0]
    groups = width // CHUNK

    def body(u_ref, v_ref, da_ref, g_ref, b_ref, wm_ref, bs_ref, dz_ref, duv_ref, gw_ref, gb_ref, acc_ref, dvn_ref):
        @pl.when(pl.program_id(0) == 0)
        def _():
            gw_ref[...] = jnp.zeros_like(gw_ref)
            gb_ref[...] = jnp.zeros_like(gb_ref)
            acc_ref[...] = jnp.zeros_like(acc_ref)

        ug, dug = _gelu_and_grad(u_ref[...])
        vg, dvg = _gelu_and_grad(v_ref[...])
        rstd, vh = _ln_stats(vg)
        vn = (vh * g_ref[...] + b_ref[...]).astype(BF16)
        da_v = _f32(da_ref)
        for g in range(groups):
            cols = slice(g * CHUNK, (g + 1) * CHUNK)
            mixed = jnp.dot(wm_ref[g], vn[:, cols], preferred_element_type=F32) + bs_ref[g]
            duv_ref[:, cols] = (da_v[:, cols] * mixed * dug[:, cols]).astype(duv_ref.dtype)
            dm = da_v[:, cols] * ug[:, cols]
            gb_ref[g] += jnp.sum(dm, axis=-1, keepdims=True)
            dmb = dm.astype(BF16)
            gw_ref[g] += lax.dot_general(dmb, vn[:, cols], (((1,), (1,)), ((), ())), preferred_element_type=F32)
            dvn_ref[:, cols] = lax.dot_general(wm_ref[g], dmb, (((0,), (0,)), ((), ())), preferred_element_type=F32)
        dvn = dvn_ref[...]
        acc_ref[0:1, :] += _colsum(dvn * vh)
        acc_ref[1:2, :] += _colsum(dvn)
        dvh = dvn * g_ref[...]
        dv = rstd * (dvh - jnp.mean(dvh, axis=-1, keepdims=True) - vh * jnp.mean(dvh * vh, axis=-1, keepdims=True))
        duv_ref[:, width:] = (dv * dvg).astype(duv_ref.dtype)

        @pl.when(pl.program_id(0) == pl.num_programs(0) - 1)
        def _():
            q = lax.broadcasted_iota(jnp.int32, gw_ref.shape, 1)
            p = lax.broadcasted_iota(jnp.int32, gw_ref.shape, 2)
            gw_ref[...] = jnp.where(p <= q, gw_ref[...], 0.0)

    vec = pl.BlockSpec((1, width), lambda n: (0, 0))
    blk = pl.BlockSpec((CHUNK, width), lambda n: (n, 0))
    return pl.pallas_call(
        body, name="gmlp_bwd", grid=(s // CHUNK,),
        in_specs=[blk, pl.BlockSpec((CHUNK, width), lambda n: (n, 1)), blk, vec, vec, _full(wm.shape), _full(bs3.shape),
                  pl.BlockSpec(memory_space=pl.ANY)],
        out_specs=[pl.BlockSpec((CHUNK, 2 * width), lambda n: (n, 0)), _full(wm.shape), _full(bs3.shape), _full((8, width))],
        out_shape=[jax.ShapeDtypeStruct(dz.shape, dz.dtype), jax.ShapeDtypeStruct(wm.shape, F32),
                   jax.ShapeDtypeStruct(bs3.shape, F32), jax.ShapeDtypeStruct((8, width), F32)],
        scratch_shapes=[pltpu.VMEM((CHUNK, width), F32)], input_output_aliases={7: 0},
        compiler_params=_params("arbitrary"))(z, z, da, ln_g, ln_b, wm, bs3, dz)


def merge_fwd(z, off_a, off_b, ya, yb):
    s, d = ya.shape
    tr, tc = _tile(s, ROW_TILE * 2), _tile(d, COL_TILE)
    assert off_a % tc == 0 and off_b % tc == 0

    def body(ga_ref, gb_ref, ya_ref, yb_ref, o_ref):
        o_ref[...] = (_sigmoid(ga_ref[...]) * _f32(ya_ref) + _sigmoid(gb_ref[...]) * _f32(yb_ref)).astype(o_ref.dtype)

    blk = pl.BlockSpec((tr, tc), lambda i, j: (i, j))
    return pl.pallas_call(
        body, name="merge_fwd", grid=(s // tr, d // tc),
        in_specs=[pl.BlockSpec((tr, tc), lambda i, j: (i, off_a // tc + j)), pl.BlockSpec((tr, tc), lambda i, j: (i, off_b // tc + j)), blk, blk],
        out_specs=blk, out_shape=jax.ShapeDtypeStruct((s, d), BF16), compiler_params=_params("parallel", "parallel"))(z, z, ya, yb)


def merge_bwd(z, off_a, off_b, ya, yb, dm, dz):
    s, d = ya.shape
    tr, tc = _tile(s, ROW_TILE * 2), _tile(d, COL_TILE)
    nc = d // tc

    def body(ga_ref, gb_ref, ya_ref, yb_ref, dm_ref, dz_ref, dya_ref, dyb_ref, dga_ref, dgb_ref):
        dm_v = _f32(dm_ref)
        sa, sb = _sigmoid(ga_ref[...]), _sigmoid(gb_ref[...])
        dya_ref[...] = (dm_v * sa).astype(dya_ref.dtype)
        dyb_ref[...] = (dm_v * sb).astype(dyb_ref.dtype)
        dga_ref[...] = (dm_v * _f32(ya_ref) * sa * (1.0 - sa)).astype(dga_ref.dtype)
        dgb_ref[...] = (dm_v * _f32(yb_ref) * sb * (1.0 - sb)).astype(dgb_ref.dtype)

    blk = pl.BlockSpec((tr, tc), lambda i, j: (i, j))
    out = jax.ShapeDtypeStruct((s, d), BF16)
    return pl.pallas_call(
        body, name="merge_bwd", grid=(s // tr, nc),
        in_specs=[pl.BlockSpec((tr, tc), lambda i, j: (i, off_a // tc + j)), pl.BlockSpec((tr, tc), lambda i, j: (i, off_b // tc + j)), blk, blk, blk,
                  pl.BlockSpec(memory_space=pl.ANY)],
        out_specs=[blk, blk, pl.BlockSpec((tr, tc), lambda i, j: (i, off_a // tc + j)), blk],
        out_shape=[out, out, jax.ShapeDtypeStruct(dz.shape, dz.dtype), out], input_output_aliases={5: 2},
        compiler_params=_params("parallel", "parallel"))(z, z, ya, yb, dm, dz)


_ATT_SCALE = (QK_NOPE + QK_ROPE) ** -0.5
_NEG = -1e30


def rope_k(z, off, cos4, sin4):
    s = z.shape[0]
    tr = _tile(s, ROW_TILE * 2)
    assert off % LANES == 0

    def body(k_ref, c_ref, s_ref, o_ref):
        k = _f32(k_ref)
        k = k + pltpu.roll(k, QK_ROPE, 1)
        o_ref[...] = _rope(k, c_ref[...], s_ref[...]).astype(o_ref.dtype)

    row = pl.BlockSpec((tr, LANES), lambda i: (i, 0))
    return pl.pallas_call(body, name="rope_k", grid=(s // tr,),
                          in_specs=[pl.BlockSpec((tr, LANES), lambda i: (i, off // LANES)), row, row], out_specs=row,
                          out_shape=jax.ShapeDtypeStruct((s, LANES), BF16), compiler_params=_params("parallel"))(z, cos4, sin4)


def _dot_nt(a, b):
    return lax.dot_general(a, b, (((1,), (1,)), ((), ())), preferred_element_type=F32)


def _dot_tn(a, b):
    return lax.dot_general(a, b, (((0,), (0,)), ((), ())), preferred_element_type=F32)


def _q_cat(q_n, qpr, hh):
    lane = lax.broadcasted_iota(jnp.int32, qpr.shape, 1)
    sel = (lane < QK_ROPE) if hh == 0 else (lane >= QK_ROPE)
    return jnp.concatenate([q_n, jnp.where(sel, qpr, jnp.zeros_like(qpr))], axis=1)


def _causal(sc):
    row = lax.broadcasted_iota(jnp.int32, sc.shape, 0)
    col = lax.broadcasted_iota(jnp.int32, sc.shape, 1)
    return jnp.where(col <= row, sc, _NEG)


def attn_fwd(qn, qp, kv, kpr, cos4, sin4):
    s = qn.shape[0]
    hp = HEADS // 2
    t = _tile(s, ATT_TILE)
    nq = s // t

    def body(qn_ref, qp_ref, kv_ref, kp_ref, c_ref, s_ref, o_ref, qpr_ref, l_ref, kcat_ref):
        qi = pl.program_id(1)

        @pl.when(qi == 0)
        def _():
            for hh in range(2):
                kcat_ref[hh, :, 0:QK_NOPE] = kv_ref[:, 2 * hh * QK_NOPE:(2 * hh + 1) * QK_NOPE]
                kcat_ref[hh, :, QK_NOPE:] = kp_ref[...]

        qpr = _rope(qp_ref[...], c_ref[...], s_ref[...]).astype(BF16)
        qpr_ref[...] = qpr
        qcat = [_q_cat(qn_ref[:, hh * QK_NOPE:(hh + 1) * QK_NOPE], qpr, hh) for hh in range(2)]

        def block(kb, carry, diagonal):
            rows = pl.ds(pl.multiple_of(kb * t, t), t)
            out = []
            for hh in range(2):
                m, l, acc = carry[hh]
                sc = _dot_nt(qcat[hh], kcat_ref[hh, rows, :]) * _ATT_SCALE
                if diagonal:
                    sc = _causal(sc)
                m_new = jnp.maximum(m, jnp.max(sc, axis=-1, keepdims=True))
                alpha = jnp.exp(m - m_new)
                p = jnp.exp(sc - m_new)
                l = alpha * l + jnp.sum(p, axis=-1, keepdims=True)
                v = kv_ref[rows, (2 * hh + 1) * QK_NOPE:(2 * hh + 2) * QK_NOPE]
                acc = alpha * acc + jnp.dot(p.astype(BF16), v, preferred_element_type=F32)
                out.append((m_new, l, acc))
            return tuple(out)

        one = (jnp.full((t, 1), _NEG, F32), jnp.zeros((t, 1), F32), jnp.zeros((t, V_HEAD), F32))
        carry = lax.fori_loop(0, qi, lambda kb, cr: block(kb, cr, False), (one, one))
        carry = block(qi, carry, True)
        for hh in range(2):
            m, l, acc = carry[hh]
            o_ref[:, hh * V_HEAD:(hh + 1) * V_HEAD] = (acc / l).astype(o_ref.dtype)
            l_ref[:, hh:hh + 1] = m + jnp.log(l)

    return pl.pallas_call(
        body, name="attn_fwd", grid=(hp, nq),
        in_specs=[pl.BlockSpec((t, 2 * QK_NOPE), lambda h, i: (i, h)), pl.BlockSpec((t, LANES), lambda h, i: (i, h)),
                  pl.BlockSpec((s, 4 * QK_NOPE), lambda h, i: (0, h)), _full((s, LANES)),
                  pl.BlockSpec((t, LANES), lambda h, i: (i, 0)), pl.BlockSpec((t, LANES), lambda h, i: (i, 0))],
        out_specs=[pl.BlockSpec((t, 2 * V_HEAD), lambda h, i: (i, h)), pl.BlockSpec((t, LANES), lambda h, i: (i, h)),
                   pl.BlockSpec((None, t, 2), lambda h, i: (h, i, 0))],
        out_shape=[jax.ShapeDtypeStruct((s, HEADS * V_HEAD), ACT), jax.ShapeDtypeStruct((s, HEADS * QK_ROPE), BF16),
                   jax.ShapeDtypeStruct((hp, s, 2), F32)],
        scratch_shapes=[pltpu.VMEM((2, s, 2 * QK_NOPE), BF16)],
        compiler_params=_params("parallel", "arbitrary"))(qn, qp, kv, kpr, cos4, sin4)


def attn_bwd(qn, qpr, kv, kpr, o, do, lse, cos4, sin4):
    s = qn.shape[0]
    hp = HEADS // 2
    t = _tile(s, ATT_TILE)
    nk = s // t

    def body(qn_ref, qpr_ref, kv_ref, kp_ref, o_ref, do_ref, l_ref, c_ref, s_ref,
             dqn_ref, dqp_ref, dkv_ref, dkp_ref, qcat_ref, dq_ref, delta_ref):
        ki = pl.program_id(1)

        @pl.when(ki == 0)
        def _():
            dq_ref[...] = jnp.zeros_like(dq_ref)
            for hh in range(2):
                qcat_ref[hh] = _q_cat(qn_ref[:, hh * QK_NOPE:(hh + 1) * QK_NOPE], qpr_ref[...], hh)
                cols = slice(hh * V_HEAD, (hh + 1) * V_HEAD)
                delta_ref[hh] = jnp.sum(do_ref[:, cols].astype(F32) * o_ref[:, cols].astype(F32), axis=-1, keepdims=True)

        rows_k = pl.ds(pl.multiple_of(ki * t, t), t)
        kcat = [jnp.concatenate([kv_ref[rows_k, 2 * hh * QK_NOPE:(2 * hh + 1) * QK_NOPE], kp_ref[rows_k, :]], axis=1) for hh in range(2)]
        vs = [kv_ref[rows_k, (2 * hh + 1) * QK_NOPE:(2 * hh + 2) * QK_NOPE] for hh in range(2)]

        def block(qb, carry, diagonal):
            rows = pl.ds(pl.multiple_of(qb * t, t), t)
            out = []
            for hh in range(2):
                dkc, dv = carry[hh]
                q_c = qcat_ref[hh, rows, :]
                do_b = do_ref[rows, hh * V_HEAD:(hh + 1) * V_HEAD].astype(BF16)
                sc = _dot_nt(q_c, kcat[hh]) * _ATT_SCALE
                if diagonal:
                    sc = _causal(sc)
                p = jnp.exp(sc - l_ref[rows, hh:hh + 1])
                dpv = _dot_nt(do_b, vs[hh])
                ds = (p * (dpv - delta_ref[hh, rows, :]) * _ATT_SCALE).astype(BF16)
                dv = dv + _dot_tn(p.astype(BF16), do_b)
                dkc = dkc + _dot_tn(ds, q_c)
                dq_ref[hh, rows, :] += jnp.dot(ds, kcat[hh], preferred_element_type=F32)
                out.append((dkc, dv))
            return tuple(out)

        one = (jnp.zeros((t, 2 * QK_NOPE), F32), jnp.zeros((t, V_HEAD), F32))
        carry = block(ki, (one, one), True)
        carry = lax.fori_loop(ki + 1, nk, lambda qb, cr: block(qb, cr, False), carry)
        dkp = jnp.zeros((t, LANES), F32)
        for hh in range(2):
            dkc, dv = carry[hh]
            dkv_ref[:, 2 * hh * QK_NOPE:(2 * hh + 1) * QK_NOPE] = dkc[:, :QK_NOPE].astype(dkv_ref.dtype)
            dkv_ref[:, (2 * hh + 1) * QK_NOPE:(2 * hh + 2) * QK_NOPE] = dv.astype(dkv_ref.dtype)
            dkp = dkp + dkc[:, QK_NOPE:]
        dkp_ref[...] = dkp

        @pl.when(ki == nk - 1)
        def _():
            lane = lax.broadcasted_iota(jnp.int32, (s, LANES), 1)
            dqp = jnp.where(lane < QK_ROPE, dq_ref[0, :, QK_NOPE:], dq_ref[1, :, QK_NOPE:])
            dqp_ref[...] = _rope(dqp, c_ref[...], -s_ref[...]).astype(dqp_ref.dtype)
            for hh in range(2):
                dqn_ref[:, hh * QK_NOPE:(hh + 1) * QK_NOPE] = dq_ref[hh, :, :QK_NOPE].astype(dqn_ref.dtype)

    qblk = pl.BlockSpec((s, 2 * QK_NOPE), lambda h, i: (0, h))
    pblk = pl.BlockSpec((s, LANES), lambda h, i: (0, h))
    tab = _full((s, LANES))
    return pl.pallas_call(
        body, name="attn_bwd", grid=(hp, nk),
        in_specs=[qblk, pblk, pl.BlockSpec((s, 4 * QK_NOPE), lambda h, i: (0, h)), tab, qblk, qblk,
                  pl.BlockSpec((None, s, 2), lambda h, i: (h, 0, 0)), tab, tab],
        out_specs=[qblk, pblk, pl.BlockSpec((t, 4 * QK_NOPE), lambda h, i: (i, h)), pl.BlockSpec((None, t, LANES), lambda h, i: (h, i, 0))],
        out_shape=[jax.ShapeDtypeStruct((s, HEADS * QK_NOPE), BF16), jax.ShapeDtypeStruct((s, HEADS * QK_ROPE), BF16),
                   jax.ShapeDtypeStruct((s, HEADS * 2 * QK_NOPE), BF16), jax.ShapeDtypeStruct((hp, s, LANES), F32)],
        scratch_shapes=[pltpu.VMEM((2, s, 2 * QK_NOPE), BF16), pltpu.VMEM((2, s, 2 * QK_NOPE), F32), pltpu.VMEM((2, s, 1), F32)],
        compiler_params=_params("parallel", "arbitrary"))(qn, qpr, kv, kpr, o, do, lse, cos4, sin4)


def kpe_bwd(dkp, cos4, sin4, pad_cols):
    hp, s, _ = dkp.shape
    tr = _tile(s, ROW_TILE * 2)

    def body(d_ref, c_ref, s_ref, o_ref):
        tot = d_ref[0]
        for h in range(1, hp):
            tot = tot + d_ref[h]
        tot = tot + pltpu.roll(tot, QK_ROPE, 1)
        lane = lax.broadcasted_iota(jnp.int32, tot.shape, 1)
        dk = jnp.where(lane < QK_ROPE, _rope(tot, c_ref[...], -s_ref[...]), jnp.zeros_like(tot))
        o_ref[...] = jnp.zeros_like(o_ref)
        o_ref[:, 0:LANES] = dk.astype(o_ref.dtype)

    row = pl.BlockSpec((tr, LANES), lambda i: (i, 0))
    return pl.pallas_call(body, name="kpe_bwd", grid=(s // tr,),
                          in_specs=[pl.BlockSpec((hp, tr, LANES), lambda i: (0, i, 0)), row, row],
                          out_specs=pl.BlockSpec((tr, pad_cols), lambda i: (i, 0)),
                          out_shape=jax.ShapeDtypeStruct((s, pad_cols), BF16), compiler_params=_params("parallel"))(dkp, cos4, sin4)


def _shift_down(x, n):
    row = lax.broadcasted_iota(jnp.int32, x.shape, 0)
    return jnp.where(row >= n, pltpu.roll(x, n, 0), jnp.zeros_like(x))


def _shift_up(x, n):
    rows = x.shape[0]
    row = lax.broadcasted_iota(jnp.int32, x.shape, 0)
    return jnp.where(row < rows - n, pltpu.roll(x, rows - n, 0), jnp.zeros_like(x))


def _conv(x, w_ref, b_ref):
    return w_ref[2:3, :] * x + w_ref[1:2, :] * _shift_down(x, 1) + w_ref[0:1, :] * _shift_down(x, 2) + b_ref[...]


def conv_act_fwd(upre, conv_w, conv_b):
    s, f2 = upre.shape
    f = f2 // 2
    tc = _tile(f, COL_TILE)
    nc = f // tc

    def body(ug_ref, uv_ref, wg_ref, wv_ref, bg_ref, bv_ref, o_ref, gv_ref):
        gh = _conv(_f32(ug_ref), wg_ref, bg_ref)
        vh = _conv(_f32(uv_ref), wv_ref, bv_ref)
        o_ref[...] = (gh * _sigmoid(gh) * vh).astype(o_ref.dtype)
        gv_ref[0] = gh.astype(gv_ref.dtype)
        gv_ref[1] = vh.astype(gv_ref.dtype)

    def spec(rows, shift):
        return pl.BlockSpec((rows, tc), lambda j: (0, j + shift))

    return pl.pallas_call(
        body, name="conv_act_fwd", grid=(nc,),
        in_specs=[spec(s, 0), spec(s, nc), spec(3, 0), spec(3, nc), spec(1, 0), spec(1, nc)],
        out_specs=[spec(s, 0), pl.BlockSpec((2, s, tc), lambda j: (0, 0, j))],
        out_shape=[jax.ShapeDtypeStruct((s, f), BF16), jax.ShapeDtypeStruct((2, s, f), ACT)],
        compiler_params=_params("parallel"))(upre, upre, conv_w, conv_w, conv_b, conv_b)


def conv_act_bwd(upre, conv_w, gv, df):
    s, f2 = upre.shape
    f = f2 // 2
    tc = _tile(f, COL_TILE)
    nc = f // tc

    def half(x, d, w_ref, du_ref, which, gw_ref, gb_ref):
        d1, d2 = _shift_up(d, 1), _shift_up(d, 2)
        gb_ref[...] = _colsum(d)
        gw_ref[2:3, :] = _colsum(d * x)
        gw_ref[1:2, :] = _colsum(d1 * x)
        gw_ref[0:1, :] = _colsum(d2 * x)
        du_ref[which] = (w_ref[2:3, :] * d + w_ref[1:2, :] * d1 + w_ref[0:1, :] * d2).astype(du_ref.dtype)

    def body(ug_ref, uv_ref, wg_ref, wv_ref, gv_ref, df_ref, du_ref, gwg_ref, gwv_ref, gbg_ref, gbv_ref):
        xg, xv = _f32(ug_ref), _f32(uv_ref)
        gh, vh = gv_ref[0].astype(F32), gv_ref[1].astype(F32)
        sg = _sigmoid(gh)
        df_v = _f32(df_ref)
        half(xg, df_v * vh * (sg * (1.0 + gh * (1.0 - sg))), wg_ref, du_ref, 0, gwg_ref, gbg_ref)
        half(xv, df_v * (gh * sg), wv_ref, du_ref, 1, gwv_ref, gbv_ref)

    def spec(rows, shift):
        return pl.BlockSpec((rows, tc), lambda j: (0, j + shift))

    gw = jax.ShapeDtypeStruct((3, f), F32)
    gb = jax.ShapeDtypeStruct((1, f), F32)
    return pl.pallas_call(
        body, name="conv_act_bwd", grid=(nc,),
        in_specs=[spec(s, 0), spec(s, nc), spec(3, 0), spec(3, nc), pl.BlockSpec((2, s, tc), lambda j: (0, 0, j)), spec(s, 0)],
        out_specs=[pl.BlockSpec((2, s, tc), lambda j: (0, 0, j)), spec(3, 0), spec(3, 0), spec(1, 0), spec(1, 0)],
        out_shape=[jax.ShapeDtypeStruct((2, s, f), BF16), gw, gw, gb, gb],
        compiler_params=_params("parallel"))(upre, upre, conv_w, conv_w, gv, df)


def _elementwise_tile(r, c, limit):
    if r % 8:
        return r, c
    best = (8, c if c % LANES else LANES)
    for k in (1, 2, 4, 8, 16):
        if k > 1 and c % (LANES * k):
            continue
        tc = c // k
        tr = max(8, min(r, limit // tc) // 8 * 8)
        while r % tr:
            tr -= 8
        if tr * tc <= max(limit, 8 * tc) and tr * tc > best[0] * best[1]:
            best = (tr, tc)
    return best


def adamw(name, w, m, v, parts):
    npart, r, c = parts.shape
    tr, tc = _elementwise_tile(r, c, ADAMW_TILE_ELEMS)
    bc1 = 1.0 - ADAM_B1 ** ADAM_STEP
    bc2 = 1.0 - ADAM_B2 ** ADAM_STEP

    def body(w_ref, m_ref, v_ref, p_ref, g_ref, d_ref, nm_ref, nv_ref):
        g = p_ref[0].astype(F32)
        for k in range(1, npart):
            g = g + p_ref[k].astype(F32)
        m_new = ADAM_B1 * m_ref[...] + (1.0 - ADAM_B1) * g
        v_new = ADAM_B2 * v_ref[...] + (1.0 - ADAM_B2) * (g * g)
        g_ref[...] = g
        nm_ref[...] = m_new
        nv_ref[...] = v_new
        d_ref[...] = -ADAM_LR * ((m_new / bc1) / (jnp.sqrt(v_new / bc2) + ADAM_EPS) + ADAM_WD * w_ref[...])

    deps = _TOKENS.take()
    blk = pl.BlockSpec((tr, tc), lambda i, j: (i, j))
    out = jax.ShapeDtypeStruct((r, c), F32)
    return pl.pallas_call(
        lambda *refs: body(*refs[:4], *refs[4 + len(deps):]), name=name, grid=(r // tr, c // tc),
        in_specs=[blk, blk, blk, pl.BlockSpec((npart, tr, tc), lambda i, j: (0, i, j))] + [pl.BlockSpec(memory_space=pl.ANY)] * len(deps),
        out_specs=[blk, blk, blk, blk], out_shape=[out, out, out, out],
        compiler_params=_params("parallel", "parallel"))(w, m, v, parts, *deps)


def _position():
    return lax.axis_index("x"), lax.axis_index("y"), lax.axis_index("c")


def _index(p):
    return 4 * p[0] + 2 * p[1] + p[2]


def _peer(me, r):
    return (me[0] ^ ((r >> 2) & 1), me[1] ^ ((r >> 1) & 1), me[2] ^ (r & 1))


_ANY = pl.BlockSpec(memory_space=pl.ANY)


_HBM = pl.BlockSpec(memory_space=pltpu.HBM)
_SEM = pl.BlockSpec(memory_space=pltpu.SEMAPHORE)
_EFFECT = pltpu.SideEffectType.DATAFLOW_SIDE_EFFECTING
_TOKEN = jax.ShapeDtypeStruct((8, LANES), F32)
_VM = pl.BlockSpec(memory_space=pltpu.VMEM)
_SIDE = pltpu.CompilerParams(has_side_effects=_EFFECT)


def _hbm(a):
    return pltpu.with_memory_space_constraint(a, pltpu.HBM)


def _like(a):
    return pltpu.HBM(a.shape, a.dtype)


def _dma_sems(n):
    return pltpu.SemaphoreType.DMA((n,))


def _other_chips(x, y):
    return [(1 - x, y), (x, 1 - y), (1 - x, 1 - y)]


COPY_STREAMS = 8


def _row_chunks(src, dst):
    rows = src.shape[0]
    n = COPY_STREAMS
    while n > 1 and rows % (16 * n):
        n //= 2
    r = rows // n
    return [(src.at[pl.ds(i * r, r)], dst.at[pl.ds(i * r, r)]) for i in range(n)]


class _rcopy:
    def __init__(self, src, dst, send_sem, recv_sem, to):
        self.parts = [pltpu.make_async_remote_copy(src_ref=s, dst_ref=d, send_sem=send_sem, recv_sem=recv_sem, device_id=to, device_id_type=MESH)
                      for s, d in _row_chunks(src, dst)]

    def start(self):
        for cp in self.parts:
            cp.start()

    def wait_send(self):
        for cp in self.parts:
            cp.wait_send()

    def wait_recv(self):
        for cp in self.parts:
            cp.wait_recv()


def _afters(after):
    return list(after) if isinstance(after, (list, tuple)) else [after]


def ag_start(name, shards, after):
    n = len(shards)
    lands = [lax.empty((N_DEV,) + a.shape, a.dtype) for a in shards]
    afters = _afters(after)
    na = len(afters)

    def body(*refs):
        ins, lnd, send_sems, recv_sems, token = refs[:n], refs[n:2 * n], refs[2 * n + na], refs[2 * n + na + 1], refs[4 * n + na + 2]
        x, y, c = _position()
        for w in range(n):
            slot = lnd[w].at[_index((x, y, c))]
            for k, to in enumerate([(x, y, 1 - c)] + [(*chip, c) for chip in _other_chips(x, y)]):
                _rcopy(ins[w], slot, send_sems.at[4 * w + k], recv_sems.at[4 * w + k], to).start()
        token[...] = jnp.zeros_like(token)

    out = pl.pallas_call(
        body, name=name, out_shape=(_dma_sems(4 * n), _dma_sems(4 * n)) + tuple(_like(a) for a in shards + lands) + (_TOKEN,),
        in_specs=[_HBM] * (2 * n) + [_ANY] * na, out_specs=(_SEM, _SEM) + (_HBM,) * (2 * n) + (_VM,),
        input_output_aliases={i: 2 + i for i in range(2 * n)}, compiler_params=_SIDE)(*[_hbm(a) for a in shards + lands], *afters)
    _TOKENS.push(out[-1])
    return out[0], out[1], list(out[2:2 + n]), list(out[2 + n:2 + 2 * n])


def _split_rows(ref):
    rows = ref.shape[0]
    h = rows // 32 * 16
    return ref.at[pl.ds(0, h)], ref.at[pl.ds(h, rows - h)]


def relay_start(name, shards, after):
    n = len(shards)
    lands = [lax.empty((N_DEV,) + a.shape, a.dtype) for a in shards]
    afters = _afters(after)
    na = len(afters)

    def body(*refs):
        ins, lnd, send_sems, recv_sems, token = refs[:n], refs[n:2 * n], refs[2 * n + na], refs[2 * n + na + 1], refs[4 * n + na + 2]
        x, y, c = _position()
        for w in range(n):
            slot = lnd[w].at[_index((x, y, c))]
            for k, to in enumerate([(x, y, 1 - c), (1 - x, y, c), (x, 1 - y, c)]):
                _rcopy(ins[w], slot, send_sems.at[3 * w + k], recv_sems.at[3 * w + k], to).start()
        token[...] = jnp.zeros_like(token)

    out = pl.pallas_call(
        body, name=name, out_shape=(_dma_sems(3 * n), _dma_sems(3 * n)) + tuple(_like(a) for a in shards + lands) + (_TOKEN,),
        in_specs=[_HBM] * (2 * n) + [_ANY] * na, out_specs=(_SEM, _SEM) + (_HBM,) * (2 * n) + (_VM,),
        input_output_aliases={i: 2 + i for i in range(2 * n)}, compiler_params=_SIDE)(*[_hbm(a) for a in shards + lands], *afters)
    _TOKENS.push(out[-1])
    return out[0], out[1], list(out[2:2 + n]), list(out[2 + n:2 + 2 * n])


def relay_pass(name, started, after):
    send, recv, shards, lands = started
    n = len(shards)
    afters = _afters(after)
    na = len(afters)

    def body(*refs):
        ins, lnd, send_sems, recv_sems = refs[:n], refs[n:2 * n], refs[2 * n], refs[2 * n + 1]
        fsend, frecv, psend, precv = refs[2 * n + 2 + na:2 * n + 6 + na]
        token = refs[4 * n + 6 + na]
        x, y, c = _position()
        nbrs = [(1 - x, y, c), (x, 1 - y, c)]
        for w in range(n):
            for j, nbr in enumerate(nbrs):
                slot = lnd[w].at[_index(nbr)]
                _rcopy(ins[w], slot, send_sems.at[3 * w + 1 + j], recv_sems.at[3 * w + 1 + j], nbr).wait_recv()
                _rcopy(slot, slot, fsend.at[2 * w + j], frecv.at[2 * w + j], (x, y, 1 - c)).start()
                part = _split_rows(slot)[j]
                _rcopy(part, part, psend.at[2 * w + j], precv.at[2 * w + j], nbrs[1 - j]).start()
        token[...] = jnp.zeros_like(token)

    out = pl.pallas_call(
        body, name=name, out_shape=(_dma_sems(2 * n),) * 4 + tuple(_like(a) for a in shards + lands) + (_TOKEN,),
        in_specs=[_HBM] * (2 * n) + [_SEM, _SEM] + [_ANY] * na, out_specs=(_SEM,) * 4 + (_HBM,) * (2 * n) + (_VM,),
        input_output_aliases={i: 4 + i for i in range(2 * n)}, compiler_params=_SIDE)(*shards, *lands, send, recv, *afters)
    _TOKENS.push(out[-1])
    return (send, recv) + tuple(out[:4]) + (list(out[4:4 + n]), list(out[4 + n:4 + 2 * n]))


def relay_forward(name, passed, after):
    send, recv, fsend, frecv, psend, precv, shards, lands = passed
    n = len(shards)
    afters = _afters(after)
    na = len(afters)

    def body(*refs):
        ins, lnd, precv_r = refs[:n], refs[n:2 * n], refs[2 * n]
        gsend, grecv, token = refs[2 * n + 1 + na], refs[2 * n + 2 + na], refs[4 * n + 3 + na]
        x, y, c = _position()
        for w in range(n):
            slot = lnd[w].at[_index((1 - x, 1 - y, c))]
            for j, part in enumerate(_split_rows(slot)):
                _rcopy(part, part, precv_r.at[2 * w + j], precv_r.at[2 * w + j], (x, y, 1 - c)).wait_recv()
            _rcopy(slot, slot, gsend.at[w], grecv.at[w], (x, y, 1 - c)).start()
        token[...] = jnp.zeros_like(token)

    out = pl.pallas_call(
        body, name=name, out_shape=(_dma_sems(n), _dma_sems(n)) + tuple(_like(a) for a in shards + lands) + (_TOKEN,),
        in_specs=[_HBM] * (2 * n) + [_SEM] + [_ANY] * na, out_specs=(_SEM, _SEM) + (_HBM,) * (2 * n) + (_VM,),
        input_output_aliases={i: 2 + i for i in range(2 * n)}, compiler_params=_SIDE)(*shards, *lands, precv, *afters)
    _TOKENS.push(out[-1])
    return send, recv, fsend, frecv, psend, out[0], out[1], list(out[2:2 + n]), list(out[2 + n:2 + 2 * n])


def relay_wait(name, forwarded, after):
    send, recv, fsend, frecv, psend, gsend, grecv, shards, lands = forwarded
    n = len(shards)

    def body(*refs):
        ins, lnd = refs[:n], refs[n:2 * n]
        send_sems, recv_sems, fsend_r, frecv_r, psend_r, gsend_r, grecv_r = refs[2 * n:2 * n + 7]
        x, y, c = _position()
        sibling = (x, y, 1 - c)
        for w in range(n):
            own = lnd[w].at[_index((x, y, c))]
            _rcopy(ins[w], lnd[w].at[_index(sibling)], send_sems.at[3 * w], recv_sems.at[3 * w], sibling).wait_recv()
            for j, nbr in enumerate([(1 - x, y, 1 - c), (x, 1 - y, 1 - c)]):
                _rcopy(ins[w], lnd[w].at[_index(nbr)], fsend_r.at[2 * w + j], frecv_r.at[2 * w + j], sibling).wait_recv()
            _rcopy(ins[w], lnd[w].at[_index((1 - x, 1 - y, 1 - c))], gsend_r.at[w], grecv_r.at[w], sibling).wait_recv()
            for k in range(3):
                _rcopy(ins[w], own, send_sems.at[3 * w + k], recv_sems.at[3 * w + k], sibling).wait_send()
            for j in range(2):
                _rcopy(ins[w], own, fsend_r.at[2 * w + j], frecv_r.at[2 * w + j], sibling).wait_send()
                part = _split_rows(own)[j]
                _rcopy(part, part, psend_r.at[2 * w + j], psend_r.at[2 * w + j], sibling).wait_send()
            _rcopy(ins[w], own, gsend_r.at[w], grecv_r.at[w], sibling).wait_send()

    out = pl.pallas_call(
        body, name=name, out_shape=tuple(_like(a) for a in shards + lands),
        in_specs=[_HBM] * (2 * n) + [_SEM] * 7 + [_ANY] * len(_afters(after)),
        out_specs=(_HBM,) * (2 * n), input_output_aliases={i: i for i in range(2 * n)},
        compiler_params=_SIDE)(*shards, *lands, send, recv, fsend, frecv, psend, gsend, grecv, *_afters(after))
    return [lax.dynamic_update_index_in_dim(land, shard, _index(_position()), 0) for shard, land in zip(out[:n], out[n:])]


def ag_forward(name, started, after):
    send, recv, shards, lands = started
    n = len(shards)
    afters = list(after) if isinstance(after, (list, tuple)) else [after]
    na = len(afters)

    def body(*refs):
        ins, lnd, send_sems, recv_sems = refs[:n], refs[n:2 * n], refs[2 * n], refs[2 * n + 1]
        fsend, frecv, token = refs[2 * n + 2 + na], refs[2 * n + 3 + na], refs[4 * n + 4 + na]
        x, y, c = _position()
        for w in range(n):
            for j, chip in enumerate(_other_chips(x, y)):
                slot = lnd[w].at[_index((*chip, c))]
                _rcopy(ins[w], slot, send_sems.at[4 * w + 1 + j], recv_sems.at[4 * w + 1 + j], (*chip, c)).wait_recv()
                _rcopy(slot, slot, fsend.at[3 * w + j], frecv.at[3 * w + j], (x, y, 1 - c)).start()
        token[...] = jnp.zeros_like(token)

    out = pl.pallas_call(
        body, name=name, out_shape=(_dma_sems(3 * n), _dma_sems(3 * n)) + tuple(_like(a) for a in shards + lands) + (_TOKEN,),
        in_specs=[_HBM] * (2 * n) + [_SEM, _SEM] + [_ANY] * na, out_specs=(_SEM, _SEM) + (_HBM,) * (2 * n) + (_VM,),
        input_output_aliases={i: 2 + i for i in range(2 * n)}, compiler_params=_SIDE)(*shards, *lands, send, recv, *afters)
    _TOKENS.push(out[-1])
    return send, recv, out[0], out[1], list(out[2:2 + n]), list(out[2 + n:2 + 2 * n])


def ag_wait(name, forwarded, after):
    send, recv, fsend, frecv, shards, lands = forwarded
    n = len(shards)

    def body(*refs):
        ins, lnd, send_sems, recv_sems, fsend_r, frecv_r = refs[:n], refs[n:2 * n], refs[2 * n], refs[2 * n + 1], refs[2 * n + 2], refs[2 * n + 3]
        x, y, c = _position()
        sibling = (x, y, 1 - c)
        for w in range(n):
            own = lnd[w].at[_index((x, y, c))]
            _rcopy(ins[w], lnd[w].at[_index(sibling)], send_sems.at[4 * w], recv_sems.at[4 * w], sibling).wait_recv()
            for j, chip in enumerate(_other_chips(x, y)):
                _rcopy(ins[w], lnd[w].at[_index((*chip, 1 - c))], fsend_r.at[3 * w + j], frecv_r.at[3 * w + j], sibling).wait_recv()
            for k in range(4):
                _rcopy(ins[w], own, send_sems.at[4 * w + k], recv_sems.at[4 * w + k], sibling).wait_send()
            for j in range(3):
                _rcopy(ins[w], own, fsend_r.at[3 * w + j], frecv_r.at[3 * w + j], sibling).wait_send()

    out = pl.pallas_call(
        body, name=name, out_shape=tuple(_like(a) for a in shards + lands),
        in_specs=[_HBM] * (2 * n) + [_SEM] * 4 + [_ANY] * len(_afters(after)),
        out_specs=(_HBM,) * (2 * n), input_output_aliases={i: i for i in range(2 * n)},
        compiler_params=_SIDE)(*shards, *lands, send, recv, fsend, frecv, *_afters(after))
    return [lax.dynamic_update_index_in_dim(land, shard, _index(_position()), 0) for shard, land in zip(out[:n], out[n:])]


def rs_d2d_start(name, grads):
    n = len(grads)
    lands = [lax.empty((4,) + g.shape[1:], g.dtype) for g in grads]

    def body(*refs):
        ins, lnd, send_sems, recv_sems, token = refs[:n], refs[n:2 * n], refs[2 * n], refs[2 * n + 1], refs[4 * n + 2]
        x, y, c = _position()
        for w in range(n):
            for i in range(4):
                _rcopy(ins[w].at[2 * i + 1 - c], lnd[w].at[i], send_sems.at[4 * w + i], recv_sems.at[4 * w + i], (x, y, 1 - c)).start()
        token[...] = jnp.zeros_like(token)

    out = pl.pallas_call(
        body, name=name, out_shape=(_dma_sems(4 * n), _dma_sems(4 * n)) + tuple(_like(a) for a in grads + lands) + (_TOKEN,),
        in_specs=[_HBM] * (2 * n), out_specs=(_SEM, _SEM) + (_HBM,) * (2 * n) + (_VM,),
        input_output_aliases={i: 2 + i for i in range(2 * n)}, compiler_params=_SIDE)(*[_hbm(a) for a in grads + lands])
    _TOKENS.push(out[-1])
    return out[0], out[1], list(out[2:2 + n]), list(out[2 + n:2 + 2 * n])


def rs_d2d_wait(name, started, after):
    send, recv, grads, lands = started
    n = len(grads)

    def body(*refs):
        ins, lnd, send_sems, recv_sems = refs[:n], refs[n:2 * n], refs[2 * n], refs[2 * n + 1]
        x, y, c = _position()
        for w in range(n):
            for i in range(4):
                cp = _rcopy(ins[w].at[2 * i + 1 - c], lnd[w].at[i], send_sems.at[4 * w + i], recv_sems.at[4 * w + i], (x, y, 1 - c))
                cp.wait_send()
                cp.wait_recv()

    out = pl.pallas_call(
        body, name=name, out_shape=tuple(_like(a) for a in grads + lands),
        in_specs=[_HBM] * (2 * n) + [_SEM, _SEM] + [_ANY] * len(_afters(after)),
        out_specs=(_HBM,) * (2 * n), input_output_aliases={i: i for i in range(2 * n)},
        compiler_params=_SIDE)(*grads, *lands, send, recv, *_afters(after))
    return list(out[:n]), list(out[n:])


def pair_sum(name, grad, land, core):
    _, r, c = grad.shape
    tr = r
    if r % 8 == 0:
        tr = max(8, min(r, 4 * ADAMW_TILE_ELEMS // c) // 8 * 8)
        while r % tr:
            tr -= 8

    def body(core_ref, a_ref, b_ref, o_ref):
        o_ref[...] = (a_ref[...].astype(F32) + b_ref[...].astype(F32)).astype(o_ref.dtype)

    return pl.pallas_call(
        body, name=name, out_shape=jax.ShapeDtypeStruct((4, r, c), grad.dtype),
        grid_spec=pltpu.PrefetchScalarGridSpec(
            num_scalar_prefetch=1, grid=(4, r // tr),
            in_specs=[pl.BlockSpec((None, None, tr, c), lambda i, j, core_ref: (i, core_ref[0], j, 0)),
                      pl.BlockSpec((None, tr, c), lambda i, j, core_ref: (i, j, 0))],
            out_specs=pl.BlockSpec((None, tr, c), lambda i, j, core_ref: (i, j, 0))),
        compiler_params=_params("parallel", "parallel"))(core, grad.reshape(4, 2, r, c), land)


def rs_ici_start(name, sums):
    n = len(sums)
    lands = [lax.empty(a.shape, a.dtype) for a in sums]

    def body(*refs):
        ins, lnd, send_sems, recv_sems, token = refs[:n], refs[n:2 * n], refs[2 * n], refs[2 * n + 1], refs[4 * n + 2]
        x, y, c = _position()
        chip = 2 * x + y
        for w in range(n):
            for j, other in enumerate(_other_chips(x, y)):
                _rcopy(ins[w].at[2 * other[0] + other[1]], lnd[w].at[chip], send_sems.at[3 * w + j], recv_sems.at[3 * w + j], (*other, c)).start()
        token[...] = jnp.zeros_like(token)

    out = pl.pallas_call(
        body, name=name, out_shape=(_dma_sems(3 * n), _dma_sems(3 * n)) + tuple(_like(a) for a in sums + lands) + (_TOKEN,),
        in_specs=[_HBM] * (2 * n), out_specs=(_SEM, _SEM) + (_HBM,) * (2 * n) + (_VM,),
        input_output_aliases={i: 2 + i for i in range(2 * n)}, compiler_params=_SIDE)(*[_hbm(a) for a in sums + lands])
    _TOKENS.push(out[-1])
    return out[0], out[1], list(out[2:2 + n]), list(out[2 + n:2 + 2 * n])


def rs_ici_wait(name, started, after):
    send, recv, sums, lands = started
    n = len(sums)

    def body(*refs):
        ins, lnd, send_sems, recv_sems = refs[:n], refs[n:2 * n], refs[2 * n], refs[2 * n + 1]
        x, y, c = _position()
        for w in range(n):
            for j, other in enumerate(_other_chips(x, y)):
                cp = _rcopy(ins[w].at[2 * other[0] + other[1]], lnd[w].at[2 * other[0] + other[1]], send_sems.at[3 * w + j], recv_sems.at[3 * w + j], (*other, c))
                cp.wait_send()
                cp.wait_recv()

    out = pl.pallas_call(
        body, name=name, out_shape=tuple(_like(a) for a in sums + lands), in_specs=[_HBM] * (2 * n) + [_SEM, _SEM, _ANY],
        out_specs=(_HBM,) * (2 * n), input_output_aliases={i: i for i in range(2 * n)}, compiler_params=_SIDE)(*sums, *lands, send, recv, after)
    chip = 2 * lax.axis_index("x") + lax.axis_index("y")
    return [lax.dynamic_update_index_in_dim(land, lax.dynamic_index_in_dim(s, chip, 0, keepdims=False), chip, 0)
            for s, land in zip(out[:n], out[n:])]


def ada_fwd(c, w_ada, b_ada3, conv_w, after):
    d, cs = w_ada.shape

    def body(c_ref, w_ref, b_ref, cw_ref, after_ref, mod_ref, sc_ref, cwa_ref, part_ref, send_sems, recv_sems):
        me = _position()
        my = _index(me)
        cv = c_ref[...]
        sc_ref[my] = cv * _sigmoid(cv)
        cwa_ref[my] = cw_ref[...]
        gather = []
        for r in range(1, N_DEV):
            for k, ref in enumerate((sc_ref, cwa_ref)):
                cp = pltpu.make_async_remote_copy(src_ref=ref.at[my], dst_ref=ref.at[my], send_sem=send_sems.at[14 * k + r - 1],
                                                  recv_sem=recv_sems.at[14 * k + r - 1], device_id=_peer(me, r), device_id_type=MESH)
                cp.start()
                gather.append(cp)
        for cp in gather:
            cp.wait()
        sc_all = jnp.concatenate([sc_ref[k] for k in range(N_DEV)], axis=0).astype(BF16)
        part = jnp.dot(sc_all, w_ref[...].astype(BF16), preferred_element_type=F32)
        for k in range(N_DEV):
            part_ref[k] = part[k:k + 1, :]
        scatter = []
        for r in range(1, N_DEV):
            peer = _peer(me, r)
            cp = pltpu.make_async_remote_copy(src_ref=part_ref.at[_index(peer)], dst_ref=mod_ref.at[my], send_sem=send_sems.at[6 + r],
                                              recv_sem=recv_sems.at[6 + r], device_id=peer, device_id_type=MESH)
            cp.start()
            scatter.append(cp)
        mod_ref[my] = part_ref[my]
        for cp in scatter:
            cp.wait()
        mod_ref[...] = mod_ref[...] + b_ref[...]

    vm = pl.BlockSpec(memory_space=pltpu.VMEM)
    return pl.pallas_call(
        body, name="ada_fwd",
        out_shape=[jax.ShapeDtypeStruct((N_DEV, 1, cs), F32), jax.ShapeDtypeStruct((N_DEV, 1, d), F32),
                   jax.ShapeDtypeStruct((N_DEV,) + conv_w.shape, F32)],
        in_specs=[vm, vm, vm, vm, _ANY], out_specs=[vm, vm, vm],
        scratch_shapes=[pltpu.VMEM((N_DEV, 1, cs), F32), pltpu.SemaphoreType.DMA((21,)), pltpu.SemaphoreType.DMA((21,))],
        compiler_params=pltpu.CompilerParams(vmem_limit_bytes=VMEM_LIMIT_BYTES))(c, w_ada, b_ada3, conv_w, after)


def ada_bwd_w(sc_all, dmod_cols):
    _, d = sc_all.shape
    cs = dmod_cols.shape[1]
    tr = _tile(d, ROW_TILE)

    def body(sc_ref, dm_ref, o_ref):
        dm = dm_ref[...].astype(BF16)
        o_ref[...] = lax.dot_general(sc_ref[...].astype(BF16), dm, (((0,), (0,)), ((), ())), preferred_element_type=F32)

    return pl.pallas_call(body, name="ada_bwd_w", grid=(d // tr,),
                          in_specs=[pl.BlockSpec((N_DEV, tr), lambda i: (0, i)), _full((N_DEV, cs))],
                          out_specs=pl.BlockSpec((None, tr, cs), lambda i: (0, i, 0)),
                          out_shape=jax.ShapeDtypeStruct((1, d, cs), F32), compiler_params=_params("parallel"))(sc_all, dmod_cols)


def _round_up(n, m):
    return (n + m - 1) // m * m


def kernel(x, c, positions, w_ada, b_ada, pre_norm1_g, w_in, gm_ln_g, gm_ln_b, gm_w_s, gm_b_s, w_branch_a, q_norm_g, w_uq, kv_norm_g, w_ukv, w_branch_b, w_out, post_norm1_g, pre_norm2_g, w_up, conv_w, conv_b, w_down, post_norm2_g, loss_target, m_w_ada, m_b_ada, m_pre_norm1_g, m_w_in, m_gm_ln_g, m_gm_ln_b, m_gm_w_s, m_gm_b_s, m_w_branch_a, m_q_norm_g, m_w_uq, m_kv_norm_g, m_w_ukv, m_w_branch_b, m_w_out, m_post_norm1_g, m_pre_norm2_g, m_w_up, m_conv_w, m_conv_b, m_w_down, m_post_norm2_g, v_w_ada, v_b_ada, v_pre_norm1_g, v_w_in, v_gm_ln_g, v_gm_ln_b, v_gm_w_s, v_gm_b_s, v_w_branch_a, v_q_norm_g, v_w_uq, v_kv_norm_g, v_w_ukv, v_w_branch_b, v_w_out, v_post_norm1_g, v_pre_norm2_g, v_w_up, v_conv_w, v_conv_b, v_w_down, v_post_norm2_g):
    weights = dict(w_ada=w_ada, b_ada=b_ada, pre_norm1_g=pre_norm1_g, w_in=w_in, gm_ln_g=gm_ln_g, gm_ln_b=gm_ln_b, gm_w_s=gm_w_s,
                   gm_b_s=gm_b_s, w_branch_a=w_branch_a, q_norm_g=q_norm_g, w_uq=w_uq, kv_norm_g=kv_norm_g, w_ukv=w_ukv,
                   w_branch_b=w_branch_b, w_out=w_out, post_norm1_g=post_norm1_g, pre_norm2_g=pre_norm2_g, w_up=w_up, conv_w=conv_w,
                   conv_b=conv_b, w_down=w_down, post_norm2_g=post_norm2_g)
    mom1 = dict(w_ada=m_w_ada, b_ada=m_b_ada, pre_norm1_g=m_pre_norm1_g, w_in=m_w_in, gm_ln_g=m_gm_ln_g, gm_ln_b=m_gm_ln_b,
                gm_w_s=m_gm_w_s, gm_b_s=m_gm_b_s, w_branch_a=m_w_branch_a, q_norm_g=m_q_norm_g, w_uq=m_w_uq, kv_norm_g=m_kv_norm_g,
                w_ukv=m_w_ukv, w_branch_b=m_w_branch_b, w_out=m_w_out, post_norm1_g=m_post_norm1_g, pre_norm2_g=m_pre_norm2_g,
                w_up=m_w_up, conv_w=m_conv_w, conv_b=m_conv_b, w_down=m_w_down, post_norm2_g=m_post_norm2_g)
    mom2 = dict(w_ada=v_w_ada, b_ada=v_b_ada, pre_norm1_g=v_pre_norm1_g, w_in=v_w_in, gm_ln_g=v_gm_ln_g, gm_ln_b=v_gm_ln_b,
                gm_w_s=v_gm_w_s, gm_b_s=v_gm_b_s, w_branch_a=v_w_branch_a, q_norm_g=v_q_norm_g, w_uq=v_w_uq, kv_norm_g=v_kv_norm_g,
                w_ukv=v_w_ukv, w_branch_b=v_w_branch_b, w_out=v_w_out, post_norm1_g=v_post_norm1_g, pre_norm2_g=v_pre_norm2_g,
                w_up=v_w_up, conv_w=v_conv_w, conv_b=v_conv_b, w_down=v_w_down, post_norm2_g=v_post_norm2_g)
    order = list(weights)
    _TOKENS.clear()

    s, d = x.shape[1], x.shape[2]
    gmw = gm_ln_g.shape[0]
    groups = gmw // CHUNK
    ql, kvl = q_norm_g.shape[0], kv_norm_g.shape[0]
    f2 = conv_b.shape[0]
    in_cols = w_in.shape[1] * N_DEV
    o_q, o_kv, o_ga, o_gb, o_kpe = 2 * gmw, 2 * gmw + ql, 2 * gmw + ql + kvl, 2 * gmw + ql + kvl + d, 2 * gmw + ql + kvl + 2 * d
    zp = _round_up(o_kpe + LANES, Z_PAD)
    src_kpe = 2 * gmw + ql + kvl
    assert src_kpe + QK_ROPE + 2 * d == in_cols
    my = 4 * lax.axis_index("x") + 2 * lax.axis_index("y") + lax.axis_index("c")

    x2, tgt = x[0], loss_target[0]
    row = lambda a: a.reshape(1, -1)

    big = ["w_in", "w_branch_a", "w_uq", "w_ukv", "w_branch_b", "w_out", "w_up", "w_down"]
    sh = {k: weights[k].astype(BF16) for k in big[1:]}
    mix = ["w_branch_a", "w_uq", "w_ukv", "w_branch_b", "w_out"]
    w_in_t = w_in.T.astype(BF16)

    mod8, sc_all3, g_cw = ada_fwd(c, w_ada, b_ada.reshape(N_DEV, 1, -1), conv_w, w_in_t)
    ag_in = relay_start("relay_start_in", [w_in_t], mod8)
    mod = mod8.reshape(N_MOD, d)
    shift1, scale1, gate1, shift2, scale2, gate2 = (mod[i:i + 1] for i in range(N_MOD))
    sc_all = sc_all3.reshape(N_DEV, d)
    h1 = norm_mod_fwd("pre1_fwd", x2, row(pre_norm1_g), scale1, shift1)

    inv = ROPE_THETA ** (-jnp.arange(0, QK_ROPE, 2, dtype=F32) / QK_ROPE)
    ang = positions[0].astype(F32)[:, None] * inv
    cos4 = jnp.tile(jnp.cos(ang), (1, 4))
    sin4 = jnp.tile(jnp.concatenate([-jnp.sin(ang), jnp.sin(ang)], axis=1), (1, 2))

    wm = (gm_w_s * jnp.tril(jnp.ones((CHUNK, CHUNK), F32))).astype(BF16)
    bs3 = gm_b_s.reshape(groups, CHUNK, 1)
    ln_g, ln_b = row(gm_ln_g), row(gm_ln_b)

    small_names = ["pre_norm1_g", "gm_ln_g", "gm_ln_b", "gm_b_s", "q_norm_g", "kv_norm_g", "post_norm1_g", "pre_norm2_g", "conv_b",
                   "post_norm2_g", "gm_w_s", "b_ada"]
    n_small_early = sum(weights[k].size for k in small_names)
    n_pack_early = _round_up(n_small_early + 3 * f2, PACK_ALIGN)

    def pack(src):
        return jnp.concatenate([src[k].reshape(-1) for k in small_names] + [jnp.zeros((n_pack_early - n_small_early,), F32)]).reshape(-1, LANES)

    packed_state = [pack(weights), pack(mom1), pack(mom2)]

    early = [h1, cos4, sin4, wm] + [sh[k] for k in big[1:]] + packed_state
    ag_in = relay_pass("relay_pass_in", ag_in, early)
    ag_in = relay_forward("relay_forward_in", ag_in, _TOKENS.pending[-1])
    ag_mix = ag_start("ag_start_mix", [sh[k] for k in mix], _TOKENS.pending[-1])
    (g_in,) = relay_wait("relay_wait_in", ag_in, [h1, _TOKENS.pending[-1]])
    cs_in = w_in.shape[1]

    def w_in_rows(lo, hi):
        return [g_in[k, max(lo - k * cs_in, 0):min(hi - k * cs_in, cs_in)] for k in range(N_DEV) if lo < (k + 1) * cs_in and hi > k * cs_in]

    w_in_p = jnp.concatenate(w_in_rows(0, src_kpe) + w_in_rows(src_kpe + QK_ROPE, in_cols) + w_in_rows(src_kpe, src_kpe + QK_ROPE)
                             + [jnp.zeros((zp - in_cols, d), BF16)], axis=0)

    z = mm_nt("z_proj", h1, w_in_p, ACT)
    ag_mix = ag_forward("ag_forward_mix", ag_mix, z)
    ag_up = ag_start("ag_start_up", [sh["w_up"]], _TOKENS.pending[-1])
    a = gmlp_fwd(z, gmw, ln_g, ln_b, wm, bs3)
    g_a, g_uq, g_ukv, g_b, g_out = ag_wait("ag_wait_mix", ag_mix, [a, _TOKENS.pending[-1]])
    w_a_f, w_b_f, w_out_f = g_a.reshape(-1, d), g_b.reshape(-1, d), g_out.reshape(-1, d)
    w_uq_f = g_uq.transpose(1, 0, 2).reshape(ql, HEADS, QK_NOPE + QK_ROPE)
    w_uq_n = w_uq_f[:, :, :QK_NOPE].reshape(ql, HEADS * QK_NOPE)
    w_uq_r = w_uq_f[:, :, QK_NOPE:].reshape(ql, HEADS * QK_ROPE)
    y_a = mm_nn("branch_a", a, w_a_f, ACT)
    qln = rms_fwd_cols("q_norm", z, o_q, ql, row(q_norm_g))
    kvn = rms_fwd_cols("kv_norm", z, o_kv, kvl, row(kv_norm_g))
    qn = mm_nn("q_nope", qln, w_uq_n, BF16)
    qp = mm_nn("q_rope", qln, w_uq_r, F32)
    kv = mm_nn_b3("kv_up", kvn, g_ukv, BF16)
    kpr = rope_k(z, o_kpe, cos4, sin4)
    o, qpr, lse = attn_fwd(qn, qp, kv, kpr, cos4, sin4)
    ag_up = ag_forward("ag_forward_up", ag_up, o)
    ag_down = ag_start("ag_start_down", [sh["w_down"]], _TOKENS.pending[-1])
    y_b = mm_nn("branch_b", o, w_b_f, ACT)
    merged = merge_fwd(z, o_ga, o_gb, y_a, y_b)
    y1 = mm_nn("out_proj", merged, w_out_f, ACT)
    x1, h2 = post1_pre2_fwd(x2, y1, gate1, row(post_norm1_g), row(pre_norm2_g), scale2, shift2)
    (g_up,) = ag_wait("ag_wait_up", ag_up, h2)
    upre = mm_nn_b3("up_proj", h2, g_up, ACT)
    ag_down = ag_forward("ag_forward_down", ag_down, upre)
    cw = g_cw.transpose(1, 0, 2).reshape(3, f2)
    cb = row(conv_b)
    f, gv = conv_act_fwd(upre, cw, cb)
    w_down_f = ag_wait("ag_wait_down", ag_down, f)[0].reshape(-1, d)
    ffn = mm_nn("down_proj", f, w_down_f, ACT)
    loss_acc, dout, dffn, acc2 = post2_loss_bwd(x1, ffn, tgt, gate2, row(post_norm2_g))
    loss = lax.psum(loss_acc[0, 0], ("x", "y", "c"))
    _TOKENS.push(jnp.broadcast_to(loss, (8, LANES)))

    blocks = lambda g: g.reshape(N_DEV, g.shape[0] // N_DEV, g.shape[1])
    core = lax.axis_index("c").astype(jnp.int32).reshape(1)
    rs = {}

    def rs_begin(key, grads):
        rs[key] = rs_d2d_start("rs_d2d_start_" + key, grads)

    def rs_middle(key, after):
        grads, lands = rs_d2d_wait("rs_d2d_wait_" + key, rs[key], after)
        sums = [pair_sum("pair_sum_%s_%d" % (key, i), g, l, core) for i, (g, l) in enumerate(zip(grads, lands))]
        rs[key] = rs_ici_start("rs_ici_start_" + key, sums)

    gw_down = mm_tn("g_w_down", f, dffn, BF16)
    rs_begin("down", [blocks(gw_down)])
    df = mm_nt("d_f", dffn, w_down_f, ACT)
    rs_middle("down", df)
    dupre, gcw_g, gcw_v, gcb_g, gcb_v = conv_act_bwd(upre, cw, gv, df)
    gw_up3 = mm_tn_h3("g_w_up", h2, dupre, N_DEV, BF16)
    rs_begin("up", [gw_up3])
    dh2 = mm_nt_h3("d_h2", dupre, g_up, ACT)
    rs_middle("up", dh2)
    dx1, dy1, acc_mid = mid_bwd(dh2, dout, x1, y1, row(pre_norm2_g), scale2, gate1, row(post_norm1_g))
    gw_out = mm_tn("g_w_out", merged, dy1, BF16)
    dmerged = mm_nt("d_merged", dy1, w_out_f, ACT)
    dya, dyb, dz, dgb = merge_bwd(z, o_ga, o_gb, y_a, y_b, dmerged, lax.empty((s, zp), BF16))
    gw_a = mm_tn("g_w_a", a, dya, BF16)
    gw_b = mm_tn("g_w_b", o, dyb, BF16)
    rs_begin("mid", [blocks(gw_out), blocks(gw_a), blocks(gw_b)])
    da = mm_nt("d_a", dya, w_a_f, ACT)
    do = mm_nt("d_o", dyb, w_b_f, ACT)
    rs_middle("mid", do)
    dz, g_ws, g_bs3, acc_gm = gmlp_bwd(z, gmw, da, ln_g, ln_b, wm, bs3, dz)
    dqn, dqp, dkv, dkp = attn_bwd(qn, qpr, kv, kpr, o, do, lse, cos4, sin4)
    dkpe = kpe_bwd(dkp, cos4, sin4, zp - o_kpe)
    dq_cat = jnp.concatenate([dqn, dqp], axis=1)
    w_uq_cat = jnp.concatenate([w_uq_n, w_uq_r], axis=1)
    dqln = mm_nt("d_qln", dq_cat, w_uq_cat, ACT)
    dq_lat, g_qnorm = rms_bwd_cols("q_norm_bwd", dqln, z, o_q, ql, row(q_norm_g))
    dkvn = mm_nt_b3("d_kvn", dkv, g_ukv, ACT)
    dkv_lat, g_kvnorm = rms_bwd_cols("kv_norm_bwd", dkvn, z, o_kv, kvl, row(kv_norm_g))
    for piece, off in ((dq_lat, o_q), (dkv_lat, o_kv), (dgb, o_gb), (dkpe, o_kpe)):
        dz = lax.dynamic_update_slice(dz, piece, (0, off))
    gw_in_p = mm_tn("g_w_in", dz, h1, BF16)

    def gw_in_rows(lo, hi):
        pieces = []
        for a, b, shift in ((0, src_kpe, 0), (src_kpe, src_kpe + QK_ROPE, o_kpe - src_kpe), (src_kpe + QK_ROPE, in_cols, -QK_ROPE)):
            if lo < b and hi > a:
                pieces.append(gw_in_p[max(lo, a) + shift:min(hi, b) + shift])
        return pieces[0] if len(pieces) == 1 else jnp.concatenate(pieces, axis=0)

    rs_begin("in", [jnp.stack([gw_in_rows(k * cs_in, (k + 1) * cs_in) for k in range(N_DEV)])])
    dh1 = mm_nn("d_h1", dz, w_in_p, ACT)
    grad_x, acc1 = pre1_bwd(dh1, dx1, x2, row(pre_norm1_g), scale1)

    dmod = jnp.concatenate([acc1[0], acc1[1], acc_mid[3], acc_mid[0], acc_mid[1], acc2[0]])
    small = [("pre_norm1_g", acc1[2]), ("gm_ln_g", acc_gm[0]), ("gm_ln_b", acc_gm[1]), ("gm_b_s", g_bs3.reshape(-1)),
             ("q_norm_g", g_qnorm[0]), ("kv_norm_g", g_kvnorm[0]), ("post_norm1_g", acc_mid[4]), ("pre_norm2_g", acc_mid[2]),
             ("conv_b", jnp.concatenate([gcb_g[0], gcb_v[0]])), ("post_norm2_g", acc2[1]), ("gm_w_s", g_ws.reshape(-1)),
             ("b_ada", dmod)]
    n_small = sum(v.shape[0] for _, v in small)
    n_cw = 3 * f2
    n_pack = _round_up(n_small + n_cw, PACK_ALIGN)
    tail = jnp.zeros((n_pack - n_small - n_cw,), F32)
    packed = jnp.concatenate([v for _, v in small] + [jnp.concatenate([gcw_g, gcw_v], axis=1).reshape(-1), tail])
    ag_small = ag_start("ag_start_small", [packed.reshape(-1, LANES)], packed)
    rs_middle("in", [packed, _TOKENS.pending[-1]])

    gw_uq_cat = mm_tn("g_w_uq", qln, dq_cat, BF16)
    gw_uq_f = jnp.concatenate([gw_uq_cat[:, :HEADS * QK_NOPE].reshape(ql, HEADS, QK_NOPE),
                               gw_uq_cat[:, HEADS * QK_NOPE:].reshape(ql, HEADS, QK_ROPE)], axis=2)
    gw_uq3 = gw_uq_f.reshape(ql, N_DEV, -1).transpose(1, 0, 2)
    gw_ukv3 = mm_tn_o3("g_w_ukv", kvn, dkv, N_DEV, BF16)
    rs_begin("mla", [gw_uq3, gw_ukv3])

    res = {}
    last = packed
    for key, names in (("down", ["w_down"]), ("up", ["w_up"]), ("mid", ["w_out", "w_branch_a", "w_branch_b"])):
        parts = rs_ici_wait("rs_ici_wait_" + key, rs[key], last)
        for k, p in zip(names, parts):
            res[k] = adamw("adamw_" + k, weights[k], mom1[k], mom2[k], p)
            last = res[k][0]
        if key == "down":
            rs_middle("mla", last)

    assert [k for k, _ in small] == small_names and n_small == n_small_early
    (gathered,) = ag_wait("ag_wait_small", ag_forward("ag_forward_small", ag_small, last), last)
    sm = [t.reshape(-1) for t in adamw("adamw_small", *packed_state, gathered)]
    off = 0
    for k, v in small:
        res[k] = tuple(t[off:off + v.shape[0]].reshape(weights[k].shape) for t in sm)
        off += v.shape[0]

    cs_cw = conv_w.shape[1]
    g_cw_full = sm[0][n_small:n_small + n_cw].reshape(3, f2)
    g_cw_mine = lax.dynamic_slice(g_cw_full, (0, my * cs_cw), (3, cs_cw))
    res["conv_w"] = adamw("adamw_conv_w", conv_w, mom1["conv_w"], mom2["conv_w"], g_cw_mine[None])

    cs_ada = w_ada.shape[1]
    off_b = n_small - N_MOD * d
    dmod_all = gathered.reshape(N_DEV, -1)[:, off_b:off_b + N_MOD * d]
    dmod_cols = lax.dynamic_slice(dmod_all, (0, my * cs_ada), (N_DEV, cs_ada))
    res["w_ada"] = adamw("adamw_w_ada", w_ada, mom1["w_ada"], mom2["w_ada"], ada_bwd_w(sc_all, dmod_cols))

    (p_in,) = rs_ici_wait("rs_ici_wait_in", rs["in"], res["w_ada"][0])
    w_in_res = adamw("adamw_w_in", w_in.T, mom1["w_in"].T, mom2["w_in"].T, p_in)
    res["w_in"] = tuple(t.T for t in w_in_res)
    for k, p in zip(["w_uq", "w_ukv"], rs_ici_wait("rs_ici_wait_mla", rs["mla"], w_in_res[0])):
        res[k] = adamw("adamw_" + k, weights[k], mom1[k], mom2[k], p)

    _TOKENS.clear()
    outs = [loss, grad_x[None]]
    for i in range(4):
        outs += [res[k][i] for k in order]
    return tuple(outs)
```
